```python
import math
import jax
import jax.numpy as jnp
from jax import lax
import numpy as np

D_MODEL = 1024
BATCH = 8
SEQ = 2048
DEPTH = 1

CHUNK = 64
Q_BLOCK = 128
GDN_HEADS = D_MODEL // 256
GDN_DK = 128
GDN_DV = 128
GDN_WIDTH = GDN_HEADS * GDN_DV
FOX_HEADS = D_MODEL // 128
FOX_DH = 64
FOX_WIDTH = FOX_HEADS * FOX_DH
CONV_W = 4
D_FF = 4 * D_MODEL
D_PLE = 256
LN_EPS = 1e-5
NORM_EPS = 1e-6
ALPHA = (2.0 * DEPTH) ** 0.25
BETA_INIT = (8.0 * DEPTH) ** -0.25

GDN_QK = GDN_HEADS * GDN_DK
GDN_QKV = 2 * GDN_QK + GDN_WIDTH
OFF_Z = GDN_QKV
OFF_BETA = OFF_Z + GDN_WIDTH
OFF_A = OFF_BETA + GDN_HEADS
OFF_FOX = OFF_A + GDN_HEADS
OFF_F = OFF_FOX + 3 * FOX_WIDTH
D_IN = OFF_F + FOX_HEADS

kernel_name = 'hybrid_gdn_fox_deepnorm_block'


def _layer_norm(x, g, b):
    xf = x.astype(jnp.float32)
    mu = jnp.mean(xf, -1, keepdims=True)
    var = jnp.mean(jnp.square(xf - mu), -1, keepdims=True)
    return ((xf - mu) * lax.rsqrt(var + LN_EPS) * g.astype(jnp.float32) + b.astype(jnp.float32)).astype(x.dtype)


def _rms_norm(x, g):
    xf = x.astype(jnp.float32)
    return (xf * lax.rsqrt(jnp.mean(xf * xf, -1, keepdims=True) + NORM_EPS) * g.astype(jnp.float32)).astype(x.dtype)


def _l2norm(x):
    xf = x.astype(jnp.float32)
    return xf * lax.rsqrt(jnp.sum(xf * xf, -1, keepdims=True) + NORM_EPS)


def _causal_conv(x, w):
    c = x.shape[-1]
    return lax.conv_general_dilated(x, w[:, None, :], window_strides=(1,), padding=[(CONV_W - 1, 0)],
                                    dimension_numbers=('NWC', 'WIO', 'NWC'), feature_group_count=c)


def _gated_delta_rule(q, k, v, beta, log_g):
    B, T, H, dk = q.shape
    dv = v.shape[-1]
    n = T // CHUNK
    f32 = jnp.float32

    def to_chunks(a):
        a = a.reshape((B, n, CHUNK) + a.shape[2:])
        return jnp.moveaxis(a, 3, 1)

    q = to_chunks(q.astype(f32)) * (dk ** -0.5)
    k = to_chunks(k.astype(f32))
    v = to_chunks(v.astype(f32))
    beta = to_chunks(beta)
    gam = jnp.cumsum(to_chunks(log_g), axis=-1)
    idx = jnp.arange(CHUNK)
    causal = idx[:, None] >= idx[None, :]
    strict = idx[:, None] > idx[None, :]
    decay = jnp.exp(jnp.where(causal, gam[..., :, None] - gam[..., None, :], -jnp.inf))

    kk = jnp.einsum('bhnid,bhnjd->bhnij', k, k)
    a_mat = jnp.where(strict, kk * beta[..., :, None] * decay, 0.0) + jnp.eye(CHUNK, dtype=f32)
    rhs = jnp.concatenate([v * beta[..., None], k * (beta * jnp.exp(gam))[..., None]], axis=-1)
    sol = lax.linalg.triangular_solve(a_mat, rhs, left_side=True, lower=True, unit_diagonal=True)
    u, w = sol[..., :dv], sol[..., dv:]

    qk_intra = jnp.where(causal, jnp.einsum('bhnid,bhnjd->bhnij', q, k) * decay, 0.0)
    q_dec = q * jnp.exp(gam)[..., None]
    k_dec = k * jnp.exp(gam[..., -1:] - gam)[..., None]
    g_last = jnp.exp(gam[..., -1])

    xs = tuple(jnp.moveaxis(a, 2, 0) for a in (q_dec, k_dec, u, w, qk_intra, g_last))

    def step(S, inp):
        qd, kd, u_c, w_c, a_c, gl = inp
        v_new = u_c - jnp.einsum('bhck,bhkv->bhcv', w_c, S)
        o = jnp.einsum('bhck,bhkv->bhcv', qd, S) + jnp.einsum('bhij,bhjv->bhiv', a_c, v_new)
        S = S * gl[..., None, None] + jnp.einsum('bhck,bhcv->bhkv', kd, v_new)
        return S, o

    s0 = jnp.zeros((B, H, dk, dv), f32)
    _, o = lax.scan(step, s0, xs)
    o = jnp.moveaxis(o, 0, 2)
    return jnp.moveaxis(o, 1, 3).reshape(B, T, H, dv)


def _forgetting_attention(q, k, v, log_f):
    B, T, H, d = q.shape
    nb = T // Q_BLOCK
    scale = d ** -0.5
    c_all = jnp.transpose(jnp.cumsum(log_f, axis=1), (0, 2, 1))
    qb = jnp.moveaxis(q.reshape(B, nb, Q_BLOCK, H, d), 1, 0)
    cb = jnp.moveaxis(c_all.reshape(B, H, nb, Q_BLOCK), 2, 0)
    k_pos = jnp.arange(T)

    def block(args):
        i, q_i, c_i = args
        s = jnp.einsum('bqhd,bkhd->bhqk', q_i, k).astype(jnp.float32) * scale
        s = s + c_i[..., :, None] - c_all[..., None, :]
        q_pos = i * Q_BLOCK + jnp.arange(Q_BLOCK)
        s = jnp.where(k_pos[None, :] <= q_pos[:, None], s, -jnp.inf)
        attn = jax.nn.softmax(s, axis=-1)
        return jnp.einsum('bhqk,bkhd->bqhd', attn.astype(v.dtype), v)

    out = lax.map(block, (jnp.arange(nb), qb, cb))
    return jnp.moveaxis(out, 0, 1).reshape(B, T, H, d)


def _fwd_setup_inputs(seed: int = 0) -> dict:
    key = jax.random.key(seed)
    ks = jax.random.split(key, 24)
    f32 = jnp.float32

    def nrm(k, shape, s):
        return jax.random.normal(k, shape, f32) * s

    x = nrm(ks[0], (BATCH, SEQ, D_MODEL), 1.0)
    p = nrm(ks[1], (DEPTH, BATCH, SEQ, D_PLE), 1.0)
    ln_in_g = 1.0 + nrm(ks[2], (D_MODEL,), 0.02)
    ln_in_b = nrm(ks[3], (D_MODEL,), 0.02)
    w_in = nrm(ks[4], (DEPTH, D_MODEL, D_IN), D_MODEL ** -0.5)
    conv_w = nrm(ks[5], (DEPTH, CONV_W, GDN_QKV), CONV_W ** -0.5)
    a_log = jnp.log(jax.random.uniform(ks[6], (DEPTH, GDN_HEADS), f32, 1.0, 16.0))
    dt = jnp.exp(jax.random.uniform(ks[7], (DEPTH, GDN_HEADS), f32, math.log(1e-3), math.log(1e-1)))
    dt_bias = dt + jnp.log(-jnp.expm1(-dt))
    gdn_norm_g = 1.0 + nrm(ks[8], (DEPTH, GDN_DV), 0.02)
    b_f = jnp.linspace(1.0, 5.0, FOX_HEADS, dtype=f32)[None, :] + nrm(ks[9], (DEPTH, FOX_HEADS), 0.1)
    fox_norm_g = 1.0 + nrm(ks[10], (DEPTH, FOX_DH), 0.02)
    w_out = nrm(ks[11], (DEPTH, D_MODEL, D_MODEL), BETA_INIT * D_MODEL ** -0.5)
    ln1_g = 1.0 + nrm(ks[12], (DEPTH, D_MODEL), 0.02)
    ln1_b = nrm(ks[13], (DEPTH, D_MODEL), 0.02)
    w_up = nrm(ks[14], (DEPTH, D_MODEL, D_FF), D_MODEL ** -0.5)
    w_down = nrm(ks[15], (DEPTH, D_FF, D_MODEL), BETA_INIT * D_FF ** -0.5)
    w_ple = nrm(ks[16], (DEPTH, D_PLE, D_MODEL), BETA_INIT * D_PLE ** -0.5)
    w_ple_gate = nrm(ks[17], (DEPTH, D_MODEL, D_MODEL), D_MODEL ** -0.5)
    b_ple_gate = nrm(ks[18], (DEPTH, D_MODEL), 0.02)
    ln2_g = 1.0 + nrm(ks[19], (DEPTH, D_MODEL), 0.02)
    ln2_b = nrm(ks[20], (DEPTH, D_MODEL), 0.02)
    return {'x': x, 'p': p, 'ln_in_g': ln_in_g, 'ln_in_b': ln_in_b, 'w_in': w_in, 'conv_w': conv_w,
            'a_log': a_log, 'dt_bias': dt_bias, 'gdn_norm_g': gdn_norm_g, 'b_f': b_f,
            'fox_norm_g': fox_norm_g, 'w_out': w_out, 'ln1_g': ln1_g, 'ln1_b': ln1_b, 'w_up': w_up,
            'w_down': w_down, 'w_ple': w_ple, 'w_ple_gate': w_ple_gate, 'b_ple_gate': b_ple_gate,
            'ln2_g': ln2_g, 'ln2_b': ln2_b}


def _fwd_reference(x, p, ln_in_g, ln_in_b, w_in, conv_w, a_log, dt_bias, gdn_norm_g, b_f, fox_norm_g,
              w_out, ln1_g, ln1_b, w_up, w_down, w_ple, w_ple_gate, b_ple_gate, ln2_g, ln2_b):
    B, T, _ = x.shape
    f32 = jnp.float32
    h = _layer_norm(x, ln_in_g, ln_in_b)
    for i in range(DEPTH):
        proj = h @ w_in[i]

        qkv = jax.nn.silu(_causal_conv(proj[..., :GDN_QKV], conv_w[i]))
        gq = _l2norm(qkv[..., :GDN_QK].reshape(B, T, GDN_HEADS, GDN_DK))
        gk = _l2norm(qkv[..., GDN_QK:2 * GDN_QK].reshape(B, T, GDN_HEADS, GDN_DK))
        gv = qkv[..., 2 * GDN_QK:].reshape(B, T, GDN_HEADS, GDN_DV)
        z = proj[..., OFF_Z:OFF_BETA].reshape(B, T, GDN_HEADS, GDN_DV)
        beta = jax.nn.sigmoid(proj[..., OFF_BETA:OFF_A].astype(f32))
        log_g = -jnp.exp(a_log[i].astype(f32)) * jax.nn.softplus(proj[..., OFF_A:OFF_FOX].astype(f32) + dt_bias[i].astype(f32))
        o_gdn = _gated_delta_rule(gq, gk, gv, beta, log_g).astype(x.dtype)
        o_gdn = (_rms_norm(o_gdn, gdn_norm_g[i]) * jax.nn.silu(z)).reshape(B, T, GDN_WIDTH)

        fqkv = proj[..., OFF_FOX:OFF_F].reshape(B, T, 3, FOX_HEADS, FOX_DH)
        log_f = jax.nn.log_sigmoid(proj[..., OFF_F:].astype(f32) + b_f[i].astype(f32))
        o_fox = _forgetting_attention(fqkv[:, :, 0], fqkv[:, :, 1], fqkv[:, :, 2], log_f)
        o_fox = _rms_norm(o_fox, fox_norm_g[i]).reshape(B, T, FOX_WIDTH)

        mix = jnp.concatenate([o_gdn, o_fox], axis=-1) @ w_out[i]
        h = _layer_norm(ALPHA * h + mix, ln1_g[i], ln1_b[i])

        ff = jnp.square(jax.nn.relu(h @ w_up[i])) @ w_down[i]
        ple = (p[i] @ w_ple[i]) * jax.nn.sigmoid(h @ w_ple_gate[i] + b_ple_gate[i])
        h = _layer_norm(ALPHA * h + ff + ple, ln2_g[i], ln2_b[i])
    return h


import jax as _jax
import jax.numpy as _jnp

TWIN_FORMAT = 'train_step'
FWD_PARAMS = ['x', 'p', 'ln_in_g', 'ln_in_b', 'w_in', 'conv_w', 'a_log', 'dt_bias', 'gdn_norm_g', 'b_f', 'fox_norm_g', 'w_out', 'ln1_g', 'ln1_b', 'w_up', 'w_down', 'w_ple', 'w_ple_gate', 'b_ple_gate', 'ln2_g', 'ln2_b']
TWIN_WEIGHTS = ['ln_in_g', 'ln_in_b', 'w_in', 'conv_w', 'a_log', 'dt_bias', 'gdn_norm_g', 'b_f', 'fox_norm_g', 'w_out', 'ln1_g', 'ln1_b', 'w_up', 'w_down', 'w_ple', 'w_ple_gate', 'b_ple_gate', 'ln2_g', 'ln2_b']
TWIN_DIFF_INPUT = 'x'
TWIN_INPUTS = ['x', 'p', 'ln_in_g', 'ln_in_b', 'w_in', 'conv_w', 'a_log', 'dt_bias', 'gdn_norm_g', 'b_f', 'fox_norm_g', 'w_out', 'ln1_g', 'ln1_b', 'w_up', 'w_down', 'w_ple', 'w_ple_gate', 'b_ple_gate', 'ln2_g', 'ln2_b', 'loss_target', 'm_ln_in_g', 'm_ln_in_b', 'm_w_in', 'm_conv_w', 'm_a_log', 'm_dt_bias', 'm_gdn_norm_g', 'm_b_f', 'm_fox_norm_g', 'm_w_out', 'm_ln1_g', 'm_ln1_b', 'm_w_up', 'm_w_down', 'm_w_ple', 'm_w_ple_gate', 'm_b_ple_gate', 'm_ln2_g', 'm_ln2_b', 'v_ln_in_g', 'v_ln_in_b', 'v_w_in', 'v_conv_w', 'v_a_log', 'v_dt_bias', 'v_gdn_norm_g', 'v_b_f', 'v_fox_norm_g', 'v_w_out', 'v_ln1_g', 'v_ln1_b', 'v_w_up', 'v_w_down', 'v_w_ple', 'v_w_ple_gate', 'v_b_ple_gate', 'v_ln2_g', 'v_ln2_b']
TWIN_OUTPUTS = ['loss', 'grad_x', 'grad_ln_in_g', 'grad_ln_in_b', 'grad_w_in', 'grad_conv_w', 'grad_a_log', 'grad_dt_bias', 'grad_gdn_norm_g', 'grad_b_f', 'grad_fox_norm_g', 'grad_w_out', 'grad_ln1_g', 'grad_ln1_b', 'grad_w_up', 'grad_w_down', 'grad_w_ple', 'grad_w_ple_gate', 'grad_b_ple_gate', 'grad_ln2_g', 'grad_ln2_b', 'delta_ln_in_g', 'delta_ln_in_b', 'delta_w_in', 'delta_conv_w', 'delta_a_log', 'delta_dt_bias', 'delta_gdn_norm_g', 'delta_b_f', 'delta_fox_norm_g', 'delta_w_out', 'delta_ln1_g', 'delta_ln1_b', 'delta_w_up', 'delta_w_down', 'delta_w_ple', 'delta_w_ple_gate', 'delta_b_ple_gate', 'delta_ln2_g', 'delta_ln2_b', 'new_m_ln_in_g', 'new_m_ln_in_b', 'new_m_w_in', 'new_m_conv_w', 'new_m_a_log', 'new_m_dt_bias', 'new_m_gdn_norm_g', 'new_m_b_f', 'new_m_fox_norm_g', 'new_m_w_out', 'new_m_ln1_g', 'new_m_ln1_b', 'new_m_w_up', 'new_m_w_down', 'new_m_w_ple', 'new_m_w_ple_gate', 'new_m_b_ple_gate', 'new_m_ln2_g', 'new_m_ln2_b', 'new_v_ln_in_g', 'new_v_ln_in_b', 'new_v_w_in', 'new_v_conv_w', 'new_v_a_log', 'new_v_dt_bias', 'new_v_gdn_norm_g', 'new_v_b_f', 'new_v_fox_norm_g', 'new_v_w_out', 'new_v_ln1_g', 'new_v_ln1_b', 'new_v_w_up', 'new_v_w_down', 'new_v_w_ple', 'new_v_w_ple_gate', 'new_v_b_ple_gate', 'new_v_ln2_g', 'new_v_ln2_b']
TWIN_LEAF_KINDS = {'loss': 'loss', 'grad_x': 'grad_x', 'grad_ln_in_g': 'grad_w', 'grad_ln_in_b': 'grad_w', 'grad_w_in': 'grad_w', 'grad_conv_w': 'grad_w', 'grad_a_log': 'grad_w', 'grad_dt_bias': 'grad_w', 'grad_gdn_norm_g': 'grad_w', 'grad_b_f': 'grad_w', 'grad_fox_norm_g': 'grad_w', 'grad_w_out': 'grad_w', 'grad_ln1_g': 'grad_w', 'grad_ln1_b': 'grad_w', 'grad_w_up': 'grad_w', 'grad_w_down': 'grad_w', 'grad_w_ple': 'grad_w', 'grad_w_ple_gate': 'grad_w', 'grad_b_ple_gate': 'grad_w', 'grad_ln2_g': 'grad_w', 'grad_ln2_b': 'grad_w', 'delta_ln_in_g': 'delta_w', 'delta_ln_in_b': 'delta_w', 'delta_w_in': 'delta_w', 'delta_conv_w': 'delta_w', 'delta_a_log': 'delta_w', 'delta_dt_bias': 'delta_w', 'delta_gdn_norm_g': 'delta_w', 'delta_b_f': 'delta_w', 'delta_fox_norm_g': 'delta_w', 'delta_w_out': 'delta_w', 'delta_ln1_g': 'delta_w', 'delta_ln1_b': 'delta_w', 'delta_w_up': 'delta_w', 'delta_w_down': 'delta_w', 'delta_w_ple': 'delta_w', 'delta_w_ple_gate': 'delta_w', 'delta_b_ple_gate': 'delta_w', 'delta_ln2_g': 'delta_w', 'delta_ln2_b': 'delta_w', 'new_m_ln_in_g': 'new_m', 'new_m_ln_in_b': 'new_m', 'new_m_w_in': 'new_m', 'new_m_conv_w': 'new_m', 'new_m_a_log': 'new_m', 'new_m_dt_bias': 'new_m', 'new_m_gdn_norm_g': 'new_m', 'new_m_b_f': 'new_m', 'new_m_fox_norm_g': 'new_m', 'new_m_w_out': 'new_m', 'new_m_ln1_g': 'new_m', 'new_m_ln1_b': 'new_m', 'new_m_w_up': 'new_m', 'new_m_w_down': 'new_m', 'new_m_w_ple': 'new_m', 'new_m_w_ple_gate': 'new_m', 'new_m_b_ple_gate': 'new_m', 'new_m_ln2_g': 'new_m', 'new_m_ln2_b': 'new_m', 'new_v_ln_in_g': 'new_v', 'new_v_ln_in_b': 'new_v', 'new_v_w_in': 'new_v', 'new_v_conv_w': 'new_v', 'new_v_a_log': 'new_v', 'new_v_dt_bias': 'new_v', 'new_v_gdn_norm_g': 'new_v', 'new_v_b_f': 'new_v', 'new_v_fox_norm_g': 'new_v', 'new_v_w_out': 'new_v', 'new_v_ln1_g': 'new_v', 'new_v_ln1_b': 'new_v', 'new_v_w_up': 'new_v', 'new_v_w_down': 'new_v', 'new_v_w_ple': 'new_v', 'new_v_w_ple_gate': 'new_v', 'new_v_b_ple_gate': 'new_v', 'new_v_ln2_g': 'new_v', 'new_v_ln2_b': 'new_v'}


def _forward(args):
    return _fwd_reference(*[args[k] for k in FWD_PARAMS])


def _output_shape():
    out = _jax.eval_shape(lambda: _forward(_fwd_setup_inputs(0)))
    return out.shape, out.dtype

N_MICROBATCH = 1
ADAM_LR = 0.001
ADAM_B1 = 0.9
ADAM_B2 = 0.999
ADAM_EPS = 1e-08
ADAM_WD = 0.01
ADAM_STEP = 10
PER_EXAMPLE_BATCH_AXIS = {'x': 0, 'p': 1, 'loss_target': 0}
SHARED_INPUTS = []
_WEIGHT_DTYPES = {'ln_in_g': _jnp.float32, 'ln_in_b': _jnp.float32, 'w_in': _jnp.float32, 'conv_w': _jnp.float32, 'a_log': _jnp.float32, 'dt_bias': _jnp.float32, 'gdn_norm_g': _jnp.float32, 'b_f': _jnp.float32, 'fox_norm_g': _jnp.float32, 'w_out': _jnp.float32, 'ln1_g': _jnp.float32, 'ln1_b': _jnp.float32, 'w_up': _jnp.float32, 'w_down': _jnp.float32, 'w_ple': _jnp.float32, 'w_ple_gate': _jnp.float32, 'b_ple_gate': _jnp.float32, 'ln2_g': _jnp.float32, 'ln2_b': _jnp.float32}
MOMENT_SCALE = {'ln_in_g': 3.210778e-01, 'ln_in_b': 5.166676e-01, 'w_in': 4.144777e-02, 'conv_w': 2.733121e-02, 'a_log': 2.042284e-01, 'dt_bias': 2.003229e-01, 'gdn_norm_g': 8.191508e-02, 'b_f': 2.584075e-01, 'fox_norm_g': 2.230707e-01, 'w_out': 8.400337e-02, 'ln1_g': 3.736135e-01, 'ln1_b': 2.704695e-01, 'w_up': 3.698761e-02, 'w_down': 1.368706e-01, 'w_ple': 4.747358e-02, 'w_ple_gate': 1.101964e-02, 'b_ple_gate': 1.269595e-02, 'ln2_g': 1.603943e+01, 'ln2_b': 3.472539e+00}


def _to_microbatches(a, axis):
    t = _jnp.moveaxis(a, axis, 0)
    t = t.reshape((N_MICROBATCH, t.shape[0] // N_MICROBATCH) + t.shape[1:])
    return _jnp.moveaxis(t, 1, axis + 1)


def setup_inputs(seed: int = 0) -> dict:
    inp = _fwd_setup_inputs(seed)
    key = _jax.random.fold_in(_jax.random.key(seed), 7919)
    shape, _ = _output_shape()
    out = dict(inp)
    out["loss_target"] = _jax.random.normal(_jax.random.fold_in(key, 0), shape, _jnp.float32)
    for i, name in enumerate(TWIN_WEIGHTS):
        w = inp[name].astype(_jnp.float32)
        if MOMENT_SCALE is None:
            s = _jnp.sqrt(_jnp.mean(_jnp.square(w)) + 1e-30)
        else:
            s = MOMENT_SCALE[name]
        km, kv = _jax.random.split(_jax.random.fold_in(key, i + 1))
        out[name] = w
        out["m_" + name] = s * _jax.random.normal(km, w.shape, _jnp.float32)
        out["v_" + name] = (s * s) * _jax.random.uniform(kv, w.shape, _jnp.float32, 0.5, 1.5)
    if N_MICROBATCH > 1:
        for name, axis in PER_EXAMPLE_BATCH_AXIS.items():
            out[name] = _to_microbatches(out[name], axis)
    return {'x': out['x'], 'p': out['p'], 'ln_in_g': out['ln_in_g'], 'ln_in_b': out['ln_in_b'], 'w_in': out['w_in'], 'conv_w': out['conv_w'], 'a_log': out['a_log'], 'dt_bias': out['dt_bias'], 'gdn_norm_g': out['gdn_norm_g'], 'b_f': out['b_f'], 'fox_norm_g': out['fox_norm_g'], 'w_out': out['w_out'], 'ln1_g': out['ln1_g'], 'ln1_b': out['ln1_b'], 'w_up': out['w_up'], 'w_down': out['w_down'], 'w_ple': out['w_ple'], 'w_ple_gate': out['w_ple_gate'], 'b_ple_gate': out['b_ple_gate'], 'ln2_g': out['ln2_g'], 'ln2_b': out['ln2_b'], 'loss_target': out['loss_target'], 'm_ln_in_g': out['m_ln_in_g'], 'm_ln_in_b': out['m_ln_in_b'], 'm_w_in': out['m_w_in'], 'm_conv_w': out['m_conv_w'], 'm_a_log': out['m_a_log'], 'm_dt_bias': out['m_dt_bias'], 'm_gdn_norm_g': out['m_gdn_norm_g'], 'm_b_f': out['m_b_f'], 'm_fox_norm_g': out['m_fox_norm_g'], 'm_w_out': out['m_w_out'], 'm_ln1_g': out['m_ln1_g'], 'm_ln1_b': out['m_ln1_b'], 'm_w_up': out['m_w_up'], 'm_w_down': out['m_w_down'], 'm_w_ple': out['m_w_ple'], 'm_w_ple_gate': out['m_w_ple_gate'], 'm_b_ple_gate': out['m_b_ple_gate'], 'm_ln2_g': out['m_ln2_g'], 'm_ln2_b': out['m_ln2_b'], 'v_ln_in_g': out['v_ln_in_g'], 'v_ln_in_b': out['v_ln_in_b'], 'v_w_in': out['v_w_in'], 'v_conv_w': out['v_conv_w'], 'v_a_log': out['v_a_log'], 'v_dt_bias': out['v_dt_bias'], 'v_gdn_norm_g': out['v_gdn_norm_g'], 'v_b_f': out['v_b_f'], 'v_fox_norm_g': out['v_fox_norm_g'], 'v_w_out': out['v_w_out'], 'v_ln1_g': out['v_ln1_g'], 'v_ln1_b': out['v_ln1_b'], 'v_w_up': out['v_w_up'], 'v_w_down': out['v_w_down'], 'v_w_ple': out['v_w_ple'], 'v_w_ple_gate': out['v_w_ple_gate'], 'v_b_ple_gate': out['v_b_ple_gate'], 'v_ln2_g': out['v_ln2_g'], 'v_ln2_b': out['v_ln2_b']}


def _loss(weights, diff, rest, loss_target):
    with _jax.named_scope("forward"):
        args = {**rest, TWIN_DIFF_INPUT: diff, **{k: w.astype(_WEIGHT_DTYPES[k]) for k, w in weights.items()}}
        y = _forward(args)
    with _jax.named_scope("loss_head"):
        err = _jnp.square(y.astype(_jnp.float32) - loss_target)
        return 0.5 * _jnp.sum(_jnp.mean(err, axis=-1)) if err.ndim else 0.5 * err


def _adamw(w, g, m, v):
    m = ADAM_B1 * m + (1.0 - ADAM_B1) * g
    v = ADAM_B2 * v + (1.0 - ADAM_B2) * _jnp.square(g)
    m_hat = m / (1.0 - ADAM_B1 ** ADAM_STEP)
    v_hat = v / (1.0 - ADAM_B2 ** ADAM_STEP)
    delta = -ADAM_LR * (m_hat / (_jnp.sqrt(v_hat) + ADAM_EPS) + ADAM_WD * w)
    return delta, m, v


def reference(x, p, ln_in_g, ln_in_b, w_in, conv_w, a_log, dt_bias, gdn_norm_g, b_f, fox_norm_g, w_out, ln1_g, ln1_b, w_up, w_down, w_ple, w_ple_gate, b_ple_gate, ln2_g, ln2_b, loss_target, m_ln_in_g, m_ln_in_b, m_w_in, m_conv_w, m_a_log, m_dt_bias, m_gdn_norm_g, m_b_f, m_fox_norm_g, m_w_out, m_ln1_g, m_ln1_b, m_w_up, m_w_down, m_w_ple, m_w_ple_gate, m_b_ple_gate, m_ln2_g, m_ln2_b, v_ln_in_g, v_ln_in_b, v_w_in, v_conv_w, v_a_log, v_dt_bias, v_gdn_norm_g, v_b_f, v_fox_norm_g, v_w_out, v_ln1_g, v_ln1_b, v_w_up, v_w_down, v_w_ple, v_w_ple_gate, v_b_ple_gate, v_ln2_g, v_ln2_b):
    given = dict(x=x, p=p, ln_in_g=ln_in_g, ln_in_b=ln_in_b, w_in=w_in, conv_w=conv_w, a_log=a_log, dt_bias=dt_bias, gdn_norm_g=gdn_norm_g, b_f=b_f, fox_norm_g=fox_norm_g, w_out=w_out, ln1_g=ln1_g, ln1_b=ln1_b, w_up=w_up, w_down=w_down, w_ple=w_ple, w_ple_gate=w_ple_gate, b_ple_gate=b_ple_gate, ln2_g=ln2_g, ln2_b=ln2_b, loss_target=loss_target, m_ln_in_g=m_ln_in_g, m_ln_in_b=m_ln_in_b, m_w_in=m_w_in, m_conv_w=m_conv_w, m_a_log=m_a_log, m_dt_bias=m_dt_bias, m_gdn_norm_g=m_gdn_norm_g, m_b_f=m_b_f, m_fox_norm_g=m_fox_norm_g, m_w_out=m_w_out, m_ln1_g=m_ln1_g, m_ln1_b=m_ln1_b, m_w_up=m_w_up, m_w_down=m_w_down, m_w_ple=m_w_ple, m_w_ple_gate=m_w_ple_gate, m_b_ple_gate=m_b_ple_gate, m_ln2_g=m_ln2_g, m_ln2_b=m_ln2_b, v_ln_in_g=v_ln_in_g, v_ln_in_b=v_ln_in_b, v_w_in=v_w_in, v_conv_w=v_conv_w, v_a_log=v_a_log, v_dt_bias=v_dt_bias, v_gdn_norm_g=v_gdn_norm_g, v_b_f=v_b_f, v_fox_norm_g=v_fox_norm_g, v_w_out=v_w_out, v_ln1_g=v_ln1_g, v_ln1_b=v_ln1_b, v_w_up=v_w_up, v_w_down=v_w_down, v_w_ple=v_w_ple, v_w_ple_gate=v_w_ple_gate, v_b_ple_gate=v_b_ple_gate, v_ln2_g=v_ln2_g, v_ln2_b=v_ln2_b)
    weights = {n: given[n] for n in TWIN_WEIGHTS}
    shared = {n: given[n] for n in SHARED_INPUTS}
    per_example = {n: given[n] for n in ['x', 'p']}
    grad_fn = _jax.value_and_grad(_loss, argnums=(0, 1))

    def one_microbatch(ex, loss_target):
        ex = dict(ex)
        diff = ex.pop(TWIN_DIFF_INPUT)
        return grad_fn(weights, diff, {**shared, **ex}, loss_target)

    if N_MICROBATCH == 1:
        loss, (grad_w, grad_x) = one_microbatch(per_example, given["loss_target"])
    else:
        def body(carry, xs):
            loss_sum, grad_sum = carry
            l_k, (gw_k, gx_k) = one_microbatch(xs[0], xs[1])
            with _jax.named_scope("update"):
                return (loss_sum + l_k, _jax.tree.map(_jnp.add, grad_sum, gw_k)), gx_k

        init = (_jnp.zeros((), _jnp.float32), _jax.tree.map(_jnp.zeros_like, weights))
        (loss, grad_w), grad_x = _jax.lax.scan(body, init, (per_example, given["loss_target"]))
    with _jax.named_scope("update"):
        delta_w, new_m, new_v = {}, {}, {}
        for n in TWIN_WEIGHTS:
            delta_w[n], new_m[n], new_v[n] = _adamw(weights[n], grad_w[n], given["m_" + n], given["v_" + n])
    return (loss, grad_x, *[grad_w[n] for n in TWIN_WEIGHTS], *[delta_w[n] for n in TWIN_WEIGHTS],
            *[new_m[n] for n in TWIN_WEIGHTS], *[new_v[n] for n in TWIN_WEIGHTS])
```

```python
import functools

import jax
import jax.numpy as jnp
from jax import lax
from jax.experimental import pallas as pl
from jax.experimental.pallas import tpu as pltpu

f32 = jnp.float32
bf16 = jnp.bfloat16
HI = lax.Precision.HIGHEST
MESH = pl.DeviceIdType.MESH

D_MODEL = 1024
CHUNK = 64
GDN_HEADS = 4
GDN_DK = 128
FOX_HEADS = 8
FOX_DH = 64
CONV_W = 4
D_FF = 4096
D_PLE = 256
LN_EPS = 1e-5
NORM_EPS = 1e-6
ALPHA = 2.0 ** 0.25
GDN_QKV = 1536
OFF_Z = 1536
OFF_BETA = 2048
OFF_FOX = 2056
OFF_F = 3592
D_IN = 3600
ADAM_LR = 0.001
ADAM_B1 = 0.9
ADAM_B2 = 0.999
ADAM_EPS = 1e-08
ADAM_WD = 0.01
ADAM_STEP = 10

SEG_FOX = 2048
SEG_SMALL = 3584
D_CAT = 3840
LANES = 128
TOK_BLK = 256
FOX_BQ = 256
VMEM_LIMIT = 56 * 1024 * 1024
NEG = -1e30

N_CHIPS = 4
FLAT_COLS = 1024
FLAT_ROWS = 3584
AG_CHUNKS = 4


def _params(sem=None, **kw):
    return pltpu.CompilerParams(dimension_semantics=sem, vmem_limit_bytes=VMEM_LIMIT, **kw)


def _sigmoid(x):
    return 1.0 / (1.0 + jnp.exp(-x))


def _softplus(x):
    return jnp.maximum(x, 0.0) + jnp.log(1.0 + jnp.exp(-jnp.abs(x)))


def _ln_fwd(x, g, b):
    mu = jnp.mean(x, -1, keepdims=True)
    xc = x - mu
    var = jnp.mean(xc * xc, -1, keepdims=True)
    rstd = lax.rsqrt(var + LN_EPS)
    xhat = xc * rstd
    return xhat * g + b, xhat, rstd


def _ln_bwd(dy, xhat, rstd, g):
    dxh = dy * g
    m1 = jnp.mean(dxh, -1, keepdims=True)
    m2 = jnp.mean(dxh * xhat, -1, keepdims=True)
    return rstd * (dxh - m1 - xhat * m2)


def _dot(a, b, prec=HI):
    return jnp.dot(a, b, precision=prec, preferred_element_type=f32)


def _dot_nt(a, b, prec=HI):
    return lax.dot_general(a, b, (((1,), (1,)), ((), ())), precision=prec, preferred_element_type=f32)


def _dot_tn(a, b, prec=HI):
    return lax.dot_general(a, b, (((0,), (0,)), ((), ())), precision=prec, preferred_element_type=f32)


def _bdot(a, b):
    return _dot(a.astype(bf16), b.astype(bf16), None)


def _bdot_nt(a, b):
    return _dot_nt(a.astype(bf16), b.astype(bf16), None)


def _bdot_tn(a, b):
    return _dot_tn(a.astype(bf16), b.astype(bf16), None)


def _lane(shape):
    return lax.broadcasted_iota(jnp.int32, shape, len(shape) - 1)


def _mm(a, b, mode, tm, tn, name, out_dtype=f32, epi=None, extra=None):
    if mode == "nn":
        (m, k), (_, n) = a.shape, b.shape
    elif mode == "nt":
        (m, k), (n, _) = a.shape, b.shape
    else:
        (k, m), (_, n) = a.shape, b.shape
    assert m % tm == 0 and n % tn == 0, (name, m, n, tm, tn)
    nc = 512 if tn % 512 == 0 else (256 if tn % 256 == 0 else 128)

    def body(a_ref, b_ref, *rest):
        for n0 in range(0, tn, nc):
            if mode == "nn":
                acc = jnp.dot(a_ref[...], b_ref[:, n0:n0 + nc], preferred_element_type=f32)
            elif mode == "nt":
                acc = lax.dot_general(a_ref[...], b_ref[n0:n0 + nc, :], (((1,), (1,)), ((), ())), preferred_element_type=f32)
            else:
                acc = lax.dot_general(a_ref[...], b_ref[:, n0:n0 + nc], (((0,), (0,)), ((), ())), preferred_element_type=f32)
            if epi == "relu2":
                up_ref, act_ref = rest
                up_ref[:, n0:n0 + nc] = acc
                r = jnp.maximum(acc, 0.0)
                act_ref[:, n0:n0 + nc] = (r * r).astype(bf16)
            elif epi == "relu2_bwd":
                up_ref, o_ref = rest
                o_ref[:, n0:n0 + nc] = (acc * (2.0 * jnp.maximum(up_ref[:, n0:n0 + nc], 0.0))).astype(bf16)
            else:
                (o_ref,) = rest
                o_ref[:, n0:n0 + nc] = acc.astype(out_dtype)

    if mode == "tn":
        a_spec = pl.BlockSpec((k, tm), lambda j, i: (0, i))
    else:
        a_spec = pl.BlockSpec((tm, k), lambda j, i: (i, 0))
    if mode == "nt":
        b_spec = pl.BlockSpec((tn, k), lambda j, i: (j, 0))
    else:
        b_spec = pl.BlockSpec((k, tn), lambda j, i: (0, j))
    o_spec = pl.BlockSpec((tm, tn), lambda j, i: (i, j))
    in_specs = [a_spec, b_spec]
    args = [a, b]
    if epi == "relu2":
        out_shape = (jax.ShapeDtypeStruct((m, n), f32), jax.ShapeDtypeStruct((m, n), bf16))
        out_specs = (o_spec, o_spec)
    elif epi == "relu2_bwd":
        in_specs.append(o_spec)
        args.append(extra)
        out_shape = jax.ShapeDtypeStruct((m, n), bf16)
        out_specs = o_spec
    else:
        out_shape = jax.ShapeDtypeStruct((m, n), out_dtype)
        out_specs = o_spec
    return pl.pallas_call(
        body, name=name, grid=(n // tn, m // tm), in_specs=in_specs, out_specs=out_specs, out_shape=out_shape,
        compiler_params=_params(("parallel", "parallel")),
    )(*args)


def _row_spec(width, col=0):
    return pl.BlockSpec((TOK_BLK, width), lambda i: (i, col))


def _vec_spec(rows, width):
    return pl.BlockSpec((rows, width), lambda i: (0, 0))


def _ln_in(x, g, b):
    t, d = x.shape

    def body(x_ref, g_ref, b_ref, h_ref, hb_ref):
        h, _, _ = _ln_fwd(x_ref[...], g_ref[...], b_ref[...])
        h_ref[...] = h
        hb_ref[...] = h.astype(bf16)

    return pl.pallas_call(
        body, name="ln_in", grid=(t // TOK_BLK,),
        in_specs=[_row_spec(d), _vec_spec(1, d), _vec_spec(1, d)],
        out_specs=(_row_spec(d), _row_spec(d)),
        out_shape=(jax.ShapeDtypeStruct((t, d), f32), jax.ShapeDtypeStruct((t, d), bf16)),
        compiler_params=_params(("parallel",)),
    )(x, g, b)


def _attn_post(o_gdn, proj, o_fox, g_gdn, g_fox2):
    t = o_gdn.shape[0]

    def body(og_ref, z_ref, of_ref, gg_ref, gf_ref, out_ref):
        for h in range(GDN_HEADS):
            sl = slice(h * LANES, (h + 1) * LANES)
            og = og_ref[:, sl]
            z = z_ref[:, sl]
            r = lax.rsqrt(jnp.mean(og * og, -1, keepdims=True) + NORM_EPS)
            out_ref[:, sl] = (og * r * gg_ref[...] * (z * _sigmoid(z))).astype(bf16)
        lo = _lane((TOK_BLK, LANES)) < FOX_DH
        for pr in range(FOX_HEADS // 2):
            sl = slice(pr * LANES, (pr + 1) * LANES)
            of = of_ref[:, sl]
            sq = of * of
            s0 = jnp.sum(jnp.where(lo, sq, 0.0), -1, keepdims=True)
            s1 = jnp.sum(jnp.where(lo, 0.0, sq), -1, keepdims=True)
            r = lax.rsqrt(jnp.where(lo, s0, s1) * (1.0 / FOX_DH) + NORM_EPS)
            out_ref[:, 512 + pr * LANES:512 + (pr + 1) * LANES] = (of * r * gf_ref[...]).astype(bf16)

    return pl.pallas_call(
        body, name="attn_post", grid=(t // TOK_BLK,),
        in_specs=[_row_spec(512), _row_spec(512, OFF_Z // 512), _row_spec(512), _vec_spec(1, LANES), _vec_spec(1, LANES)],
        out_specs=_row_spec(D_MODEL),
        out_shape=jax.ShapeDtypeStruct((t, D_MODEL), bf16),
        compiler_params=_params(("parallel",)),
    )(o_gdn, proj, o_fox, g_gdn, g_fox2)


def _attn_post_bwd(dattn, o_gdn, proj, o_fox, g_gdn, g_fox2):
    t = o_gdn.shape[0]

    def body(da_ref, og_ref, z_ref, of_ref, gg_ref, gf_ref, dog_ref, dz_ref, dof_ref, pg_ref):
        i = pl.program_id(0)

        @pl.when(i == 0)
        def _():
            pg_ref[...] = jnp.zeros_like(pg_ref)

        dgg = jnp.zeros((1, LANES), f32)
        for h in range(GDN_HEADS):
            sl = slice(h * LANES, (h + 1) * LANES)
            og = og_ref[:, sl]
            z = z_ref[:, sl]
            dout = da_ref[:, sl]
            g = gg_ref[...]
            r = lax.rsqrt(jnp.mean(og * og, -1, keepdims=True) + NORM_EPS)
            sg = _sigmoid(z)
            silu = z * sg
            ng = og * r * g
            dng = dout * silu
            dz_ref[:, sl] = (dout * ng * (sg * (1.0 + z * (1.0 - sg)))).astype(bf16)
            dgg = dgg + jnp.sum(dng * og * r, 0, keepdims=True)
            gd = dng * g
            dog_ref[:, sl] = r * gd - og * (r * r * r) * jnp.mean(og * gd, -1, keepdims=True)
        pg_ref[0:1, :] += dgg
        lo = _lane((TOK_BLK, LANES)) < FOX_DH
        dgf = jnp.zeros((1, LANES), f32)
        for pr in range(FOX_HEADS // 2):
            sl = slice(pr * LANES, (pr + 1) * LANES)
            of = of_ref[:, sl]
            dout = da_ref[:, 512 + pr * LANES:512 + (pr + 1) * LANES]
            g = gf_ref[...]
            sq = of * of
            s0 = jnp.sum(jnp.where(lo, sq, 0.0), -1, keepdims=True)
            s1 = jnp.sum(jnp.where(lo, 0.0, sq), -1, keepdims=True)
            r = lax.rsqrt(jnp.where(lo, s0, s1) * (1.0 / FOX_DH) + NORM_EPS)
            dgf = dgf + jnp.sum(dout * of * r, 0, keepdims=True)
            gd = dout * g
            xg = of * gd
            m0 = jnp.sum(jnp.where(lo, xg, 0.0), -1, keepdims=True)
            m1 = jnp.sum(jnp.where(lo, 0.0, xg), -1, keepdims=True)
            dof_ref[:, sl] = r * gd - of * (r * r * r) * (jnp.where(lo, m0, m1) * (1.0 / FOX_DH))
        pg_ref[1:2, :] += dgf

    return pl.pallas_call(
        body, name="attn_post_bwd", grid=(t // TOK_BLK,),
        in_specs=[_row_spec(D_MODEL), _row_spec(512), _row_spec(512, OFF_Z // 512), _row_spec(512), _vec_spec(1, LANES), _vec_spec(1, LANES)],
        out_specs=(_row_spec(512), _row_spec(512), _row_spec(512), _vec_spec(8, LANES)),
        out_shape=(jax.ShapeDtypeStruct((t, 512), f32), jax.ShapeDtypeStruct((t, 512), bf16),
                   jax.ShapeDtypeStruct((t, 512), f32), jax.ShapeDtypeStruct((8, LANES), f32)),
        compiler_params=_params(("arbitrary",)),
    )(dattn, o_gdn, proj, o_fox, g_gdn, g_fox2)


def _ln1(h0, mix, g, b):
    t, d = h0.shape

    def body(h0_ref, mix_ref, g_ref, b_ref, h_ref, hb_ref, xh_ref, rs_ref):
        h, xhat, rstd = _ln_fwd(ALPHA * h0_ref[...] + mix_ref[...], g_ref[...], b_ref[...])
        h_ref[...] = h
        hb_ref[...] = h.astype(bf16)
        xh_ref[...] = xhat
        rs_ref[...] = jnp.broadcast_to(rstd, rs_ref.shape)

    return pl.pallas_call(
        body, name="ln1", grid=(t // TOK_BLK,),
        in_specs=[_row_spec(d), _row_spec(d), _vec_spec(1, d), _vec_spec(1, d)],
        out_specs=(_row_spec(d), _row_spec(d), _row_spec(d), _row_spec(LANES)),
        out_shape=(jax.ShapeDtypeStruct((t, d), f32), jax.ShapeDtypeStruct((t, d), bf16),
                   jax.ShapeDtypeStruct((t, d), f32), jax.ShapeDtypeStruct((t, LANES), f32)),
        compiler_params=_params(("parallel",)),
    )(h0, mix, g, b)


def _ln2_loss(h1, ff, pe, gp, b_gate, g, b, target):
    t, d = h1.shape

    def body(h1_ref, ff_ref, pe_ref, gp_ref, bg_ref, g_ref, b_ref, t_ref, dr_ref, drb_ref, dpe_ref, dgp_ref, pg_ref):
        i = pl.program_id(0)

        @pl.when(i == 0)
        def _():
            pg_ref[...] = jnp.zeros_like(pg_ref)

        sig = _sigmoid(gp_ref[...] + bg_ref[...])
        pe = pe_ref[...]
        r2 = ALPHA * h1_ref[...] + ff_ref[...] + pe * sig
        y, xhat, rstd = _ln_fwd(r2, g_ref[...], b_ref[...])
        err = y - t_ref[...]
        dy = err * (1.0 / d)
        dr = _ln_bwd(dy, xhat, rstd, g_ref[...])
        dr_ref[...] = dr
        drb_ref[...] = dr.astype(bf16)
        dpe_ref[...] = (dr * sig).astype(bf16)
        dgp = dr * pe * sig * (1.0 - sig)
        dgp_ref[...] = dgp.astype(bf16)
        pg_ref[0:1, :] += jnp.sum(dy * xhat, 0, keepdims=True)
        pg_ref[1:2, :] += jnp.sum(dy, 0, keepdims=True)
        pg_ref[2:3, :] += jnp.sum(dgp, 0, keepdims=True)
        pg_ref[3:4, :] += 0.5 * jnp.sum(jnp.mean(err * err, -1, keepdims=True), 0, keepdims=True)

    return pl.pallas_call(
        body, name="ln2_loss", grid=(t // TOK_BLK,),
        in_specs=[_row_spec(d)] * 4 + [_vec_spec(1, d)] * 3 + [_row_spec(d)],
        out_specs=(_row_spec(d), _row_spec(d), _row_spec(d), _row_spec(d), _vec_spec(8, d)),
        out_shape=(jax.ShapeDtypeStruct((t, d), f32), jax.ShapeDtypeStruct((t, d), bf16), jax.ShapeDtypeStruct((t, d), bf16),
                   jax.ShapeDtypeStruct((t, d), bf16), jax.ShapeDtypeStruct((8, d), f32)),
        compiler_params=_params(("arbitrary",)),
    )(h1, ff, pe, gp, b_gate, g, b, target)


def _ln1_bwd(dr2, da, db, xhat, rstd, g):
    t, d = dr2.shape

    def body(dr2_ref, da_ref, db_ref, xh_ref, rs_ref, g_ref, dr_ref, drb_ref, pg_ref):
        i = pl.program_id(0)

        @pl.when(i == 0)
        def _():
            pg_ref[...] = jnp.zeros_like(pg_ref)

        dh = ALPHA * dr2_ref[...] + da_ref[...] + db_ref[...]
        xhat = xh_ref[...]
        dr = _ln_bwd(dh, xhat, rs_ref[:, 0:1], g_ref[...])
        dr_ref[...] = dr
        drb_ref[...] = dr.astype(bf16)
        pg_ref[0:1, :] += jnp.sum(dh * xhat, 0, keepdims=True)
        pg_ref[1:2, :] += jnp.sum(dh, 0, keepdims=True)

    return pl.pallas_call(
        body, name="ln1_bwd", grid=(t // TOK_BLK,),
        in_specs=[_row_spec(d)] * 4 + [_row_spec(LANES), _vec_spec(1, d)],
        out_specs=(_row_spec(d), _row_spec(d), _vec_spec(8, d)),
        out_shape=(jax.ShapeDtypeStruct((t, d), f32), jax.ShapeDtypeStruct((t, d), bf16), jax.ShapeDtypeStruct((8, d), f32)),
        compiler_params=_params(("arbitrary",)),
    )(dr2, da, db, xhat, rstd, g)


def _ln_in_bwd(x, dr1, dmm, g):
    t, d = x.shape

    def body(x_ref, dr1_ref, dmm_ref, g_ref, dx_ref, pg_ref):
        i = pl.program_id(0)

        @pl.when(i == 0)
        def _():
            pg_ref[...] = jnp.zeros_like(pg_ref)

        dh = ALPHA * dr1_ref[...] + dmm_ref[...]
        _, xhat, rstd = _ln_fwd(x_ref[...], g_ref[...], 0.0)
        dx_ref[...] = _ln_bwd(dh, xhat, rstd, g_ref[...])
        pg_ref[0:1, :] += jnp.sum(dh * xhat, 0, keepdims=True)
        pg_ref[1:2, :] += jnp.sum(dh, 0, keepdims=True)

    return pl.pallas_call(
        body, name="ln_in_bwd", grid=(t // TOK_BLK,),
        in_specs=[_row_spec(d)] * 3 + [_vec_spec(1, d)],
        out_specs=(_row_spec(d), _vec_spec(8, d)),
        out_shape=(jax.ShapeDtypeStruct((t, d), f32), jax.ShapeDtypeStruct((8, d), f32)),
        compiler_params=_params(("arbitrary",)),
    )(x, dr1, dmm, g)


def _tri(n, upper=False, strict=False):
    r = lax.broadcasted_iota(jnp.int32, (n, n), 0)
    c = lax.broadcasted_iota(jnp.int32, (n, n), 1)
    if upper:
        m = (c > r) if strict else (c >= r)
    else:
        m = (c < r) if strict else (c <= r)
    return jnp.where(m, 1.0, 0.0).astype(f32)


def _gate_values(x, bias, alog, lane):
    z = x + bias
    return jnp.where(lane < 4, _sigmoid(z), jnp.where(lane < 8, -jnp.exp(alog) * _softplus(z), jnp.where(lane < 16, -_softplus(-z), 0.0)))


def _gates(proj, bias_row, alog_row):
    t = proj.shape[0]
    nch = t // CHUNK

    def body(x_ref, bias_ref, alog_ref, gates_ref, gcum_ref, gcumt_ref):
        lane = _lane((t, LANES))
        gates = _gate_values(x_ref[...], bias_ref[...], alog_ref[...], lane)
        gates_ref[...] = gates
        g3 = gates.reshape(nch, CHUNK, LANES)
        tri = jnp.broadcast_to(_tri(CHUNK)[None], (nch, CHUNK, CHUNK))
        loc = jnp.einsum("bij,bjk->bik", tri, g3, precision=HI, preferred_element_type=f32)
        tot = jnp.sum(g3, axis=1)
        offs = _dot(_tri(nch, strict=True), tot)
        glob = loc + offs[:, None, :]
        lane3 = _lane((nch, CHUNK, LANES))
        gcum = jnp.where(lane3 < 4, g3, jnp.where(lane3 < 8, loc, glob)).reshape(t, LANES)
        gcum_ref[...] = gcum
        gcumt_ref[...] = gcum.T

    return pl.pallas_call(
        body, name="gates", grid=(1,),
        in_specs=[pl.BlockSpec((t, LANES), lambda i: (0, SEG_SMALL // LANES)), _vec_spec(1, LANES), _vec_spec(1, LANES)],
        out_specs=(pl.BlockSpec((t, LANES), lambda i: (0, 0)), pl.BlockSpec((t, LANES), lambda i: (0, 0)),
                   pl.BlockSpec((LANES, t), lambda i: (0, 0))),
        out_shape=(jax.ShapeDtypeStruct((t, LANES), f32), jax.ShapeDtypeStruct((t, LANES), f32), jax.ShapeDtypeStruct((LANES, t), f32)),
        compiler_params=_params(("arbitrary",)),
    )(proj, bias_row, alog_row)


def _gates_bwd(proj, bias_row, alog_row, gates, dgates, dccol, dct):
    t = proj.shape[0]
    nch = t // CHUNK

    def body(x_ref, bias_ref, alog_ref, gates_ref, dg_ref, dcc_ref, dct_ref, dx_ref, pg_ref):
        lane = _lane((t, LANES))
        d = dg_ref[...] + dcc_ref[...] + dct_ref[...].T
        d3 = d.reshape(nch, CHUNK, LANES)
        tri = jnp.broadcast_to(_tri(CHUNK, upper=True)[None], (nch, CHUNK, CHUNK))
        loc = jnp.einsum("bij,bjk->bik", tri, d3, precision=HI, preferred_element_type=f32)
        tot = jnp.sum(d3, axis=1)
        offs = _dot(_tri(nch, upper=True, strict=True), tot)
        glob = loc + offs[:, None, :]
        lane3 = _lane((nch, CHUNK, LANES))
        dpre = jnp.where(lane3 < 4, d3, jnp.where(lane3 < 8, loc, glob)).reshape(t, LANES)
        z = x_ref[...] + bias_ref[...]
        sg = _sigmoid(z)
        dx = jnp.where(lane < 4, dpre * sg * (1.0 - sg),
                       jnp.where(lane < 8, dpre * (-jnp.exp(alog_ref[...])) * sg, jnp.where(lane < 16, dpre * (1.0 - sg), 0.0)))
        dx_ref[...] = dx.astype(bf16)
        pg_ref[...] = jnp.zeros_like(pg_ref)
        pg_ref[0:1, :] = jnp.sum(dx, 0, keepdims=True)
        pg_ref[1:2, :] = jnp.sum(jnp.where((lane >= 4) & (lane < 8), dpre * gates_ref[...], 0.0), 0, keepdims=True)

    full = pl.BlockSpec((t, LANES), lambda i: (0, 0))
    return pl.pallas_call(
        body, name="gates_bwd", grid=(1,),
        in_specs=[pl.BlockSpec((t, LANES), lambda i: (0, SEG_SMALL // LANES)), _vec_spec(1, LANES), _vec_spec(1, LANES),
                  full, full, full, pl.BlockSpec((LANES, t), lambda i: (0, 0))],
        out_specs=(full, _vec_spec(8, LANES)),
        out_shape=(jax.ShapeDtypeStruct((t, LANES), bf16), jax.ShapeDtypeStruct((8, LANES), f32)),
        compiler_params=_params(("arbitrary",)),
    )(proj, bias_row, alog_row, gates, dgates, dccol, dct)


def _conv_act(u, cw, row, t):
    c = cw[3:4, :] * u
    for jj in range(CONV_W - 1):
        sh = CONV_W - 1 - jj
        c = c + cw[jj:jj + 1, :] * jnp.where(row >= sh, pltpu.roll(u, sh, axis=0), 0.0)
    return c


def _gdn_conv(proj, conv_w):
    t = proj.shape[0]
    nblk = GDN_QKV // LANES

    def body(u_ref, cw_ref, c_ref, y_ref):
        j = pl.program_id(0)
        row = lax.broadcasted_iota(jnp.int32, (t, LANES), 0)
        c = _conv_act(u_ref[...], cw_ref[...], row, t)
        c_ref[...] = c
        s = c * _sigmoid(c)
        r = lax.rsqrt(jnp.sum(s * s, -1, keepdims=True) + NORM_EPS)
        scale = jnp.where(j < GDN_HEADS, GDN_DK ** -0.5, 1.0)
        y_ref[...] = jnp.where(j < 2 * GDN_HEADS, s * (r * scale), s)

    blk = pl.BlockSpec((t, LANES), lambda j: (0, j))
    return pl.pallas_call(
        body, name="gdn_conv", grid=(nblk,),
        in_specs=[blk, pl.BlockSpec((CONV_W, LANES), lambda j: (0, j))],
        out_specs=(blk, blk),
        out_shape=(jax.ShapeDtypeStruct((t, GDN_QKV), f32), jax.ShapeDtypeStruct((t, GDN_QKV), f32)),
        compiler_params=_params(("parallel",)),
    )(proj, conv_w)


def _gdn_conv_bwd(proj, conv_w, c, dy):
    t = proj.shape[0]
    nblk = GDN_QKV // LANES

    def body(u_ref, cw_ref, c_ref, dy_ref, du_ref, dcw_ref):
        j = pl.program_id(0)
        row = lax.broadcasted_iota(jnp.int32, (t, LANES), 0)
        u = u_ref[...]
        cw = cw_ref[...]
        c = c_ref[...]
        dy = dy_ref[...]
        sg = _sigmoid(c)
        s = c * sg
        r = lax.rsqrt(jnp.sum(s * s, -1, keepdims=True) + NORM_EPS)
        n = s * r
        scale = jnp.where(j < GDN_HEADS, GDN_DK ** -0.5, 1.0)
        dn = dy * scale
        ds = jnp.where(j < 2 * GDN_HEADS, r * (dn - n * jnp.sum(dn * n, -1, keepdims=True)), dy)
        dc = ds * (sg * (1.0 + c * (1.0 - sg)))
        du = cw[3:4, :] * dc
        dcw_ref[...] = jnp.zeros_like(dcw_ref)
        dcw_ref[3:4, :] = jnp.sum(dc * u, 0, keepdims=True)
        for jj in range(CONV_W - 1):
            sh = CONV_W - 1 - jj
            du = du + cw[jj:jj + 1, :] * jnp.where(row < t - sh, pltpu.roll(dc, t - sh, axis=0), 0.0)
            dcw_ref[jj:jj + 1, :] = jnp.sum(dc * jnp.where(row >= sh, pltpu.roll(u, sh, axis=0), 0.0), 0, keepdims=True)
        du_ref[...] = du.astype(bf16)

    blk = pl.BlockSpec((t, LANES), lambda j: (0, j))
    return pl.pallas_call(
        body, name="gdn_conv_bwd", grid=(nblk,),
        in_specs=[blk, pl.BlockSpec((CONV_W, LANES), lambda j: (0, j)), blk, blk],
        out_specs=(blk, pl.BlockSpec((8, LANES), lambda j: (0, j))),
        out_shape=(jax.ShapeDtypeStruct((t, GDN_QKV), bf16), jax.ShapeDtypeStruct((8, GDN_QKV), f32)),
        compiler_params=_params(("parallel",)),
    )(proj, conv_w, c, dy)


def _chunk_masks():
    r = lax.broadcasted_iota(jnp.int32, (CHUNK, CHUNK), 0)
    c = lax.broadcasted_iota(jnp.int32, (CHUNK, CHUNK), 1)
    return r >= c, r > c, r == c


def _col_to_row(col, eye):
    return jnp.sum(jnp.where(eye, col, 0.0), axis=0, keepdims=True)


def _row_to_col(row, eye):
    return jnp.sum(jnp.where(eye, row, 0.0), axis=1, keepdims=True)


def _gdn_chunk_local(qn, kn, v, beta, gam, masks):
    causal, strict, eye = masks
    gam_r = _col_to_row(gam, eye)
    dec = jnp.exp(jnp.where(causal, gam - gam_r, NEG))
    kkd = _dot_nt(kn, kn) * dec
    nmat = jnp.where(strict, kkd * beta, 0.0)
    x = -nmat
    tinv = jnp.where(eye, 1.0, 0.0) + x
    for _ in range(5):
        x = _dot(x, x)
        tinv = tinv + _dot(tinv, x)
    e = jnp.exp(gam)
    u = _dot(tinv, beta * v)
    w = _dot(tinv, (beta * e) * kn)
    qk = _dot_nt(qn, kn) * dec
    gam_last = gam[CHUNK - 1:CHUNK, :]
    f = jnp.exp(gam_last - gam)
    return dict(dec=dec, kkd=kkd, nmat=nmat, tinv=tinv, e=e, u=u, w=w, qk=qk, f=f, gl=jnp.exp(gam_last),
                qd=qn * e, kd=kn * f)


def _gdn_fwd(y, gcum):
    t = y.shape[0]
    nch = t // CHUNK

    def body(y_ref, g_ref, o_ref, sall_ref, s_ref):
        @pl.when(pl.program_id(0) == 0)
        def _():
            s_ref[...] = jnp.zeros_like(s_ref)

        masks = _chunk_masks()
        for h in range(GDN_HEADS):
            qn = y_ref[:, h * LANES:(h + 1) * LANES]
            kn = y_ref[:, 512 + h * LANES:512 + (h + 1) * LANES]
            v = y_ref[:, 1024 + h * LANES:1024 + (h + 1) * LANES]
            loc = _gdn_chunk_local(qn, kn, v, g_ref[:, h:h + 1], g_ref[:, 4 + h:5 + h], masks)
            s = s_ref[h]
            sall_ref[0, h] = s
            vn = loc["u"] - _dot(loc["w"], s)
            o_ref[:, h * LANES:(h + 1) * LANES] = _dot(loc["qd"], s) + _dot(loc["qk"], vn)
            s_ref[h] = loc["gl"] * s + _dot_tn(loc["kd"], vn)

    return pl.pallas_call(
        body, name="gdn_fwd", grid=(nch,),
        in_specs=[pl.BlockSpec((CHUNK, GDN_QKV), lambda n: (n, 0)), pl.BlockSpec((CHUNK, LANES), lambda n: (n, 0))],
        out_specs=(pl.BlockSpec((CHUNK, 512), lambda n: (n, 0)), pl.BlockSpec((1, GDN_HEADS, LANES, LANES), lambda n: (n, 0, 0, 0))),
        out_shape=(jax.ShapeDtypeStruct((t, 512), f32), jax.ShapeDtypeStruct((nch, GDN_HEADS, LANES, LANES), f32)),
        scratch_shapes=[pltpu.VMEM((GDN_HEADS, LANES, LANES), f32)],
        compiler_params=_params(("arbitrary",)),
    )(y, gcum)


def _gdn_bwd(y, gcum, sall, do):
    t = y.shape[0]
    nch = t // CHUNK

    def body(y_ref, g_ref, sall_ref, do_ref, dy_ref, dg_ref, ds_ref):
        @pl.when(pl.program_id(0) == 0)
        def _():
            ds_ref[...] = jnp.zeros_like(ds_ref)

        masks = _chunk_masks()
        causal, strict, eye = masks
        lane = _lane((CHUNK, LANES))
        row = lax.broadcasted_iota(jnp.int32, (CHUNK, 1), 0)
        dgates = jnp.zeros((CHUNK, LANES), f32)
        for h in range(GDN_HEADS):
            qn = y_ref[:, h * LANES:(h + 1) * LANES]
            kn = y_ref[:, 512 + h * LANES:512 + (h + 1) * LANES]
            v = y_ref[:, 1024 + h * LANES:1024 + (h + 1) * LANES]
            beta = g_ref[:, h:h + 1]
            L = _gdn_chunk_local(qn, kn, v, beta, g_ref[:, 4 + h:5 + h], masks)
            s = sall_ref[0, h]
            dsn = ds_ref[h]
            d_o = do_ref[:, h * LANES:(h + 1) * LANES]
            u, w, qk, qd, kd, tinv, e, f, gl, dec = (L[k] for k in ("u", "w", "qk", "qd", "kd", "tinv", "e", "f", "gl", "dec"))
            vn = u - _dot(w, s)
            dvn = _dot_tn(qk, d_o) + _dot(kd, dsn)
            dqk = jnp.where(causal, _dot_nt(d_o, vn), 0.0)
            dqd = _dot_nt(d_o, s)
            dkd = _dot_nt(vn, dsn)
            dgl = jnp.sum(jnp.sum(dsn * s, axis=1, keepdims=True), axis=0, keepdims=True)
            dw = -_dot_nt(dvn, s)
            ds_ref[h] = _dot_tn(qd, d_o) - _dot_tn(w, dvn) + gl * dsn
            dru = _dot_tn(tinv, dvn)
            drw = _dot_tn(tinv, dw)
            dn = jnp.where(strict, -(_dot_nt(dru, u) + _dot_nt(drw, w)), 0.0)
            dv = beta * dru
            drw_k = jnp.sum(drw * kn, axis=1, keepdims=True)
            dbeta = jnp.sum(dru * v, axis=1, keepdims=True) + e * drw_k + jnp.sum(dn * L["kkd"], axis=1, keepdims=True)
            de = beta * drw_k
            dk = (beta * e) * drw
            dkk = dn * beta * dec
            dk = dk + _dot(dkk, kn) + _dot_tn(dkk, kn)
            dqkr = dqk * dec
            dq = _dot(dqkr, kn) + dqd * e
            dk = dk + _dot_tn(dqkr, qn) + dkd * f
            m = dn * L["nmat"] + dqk * qk
            dgam = jnp.sum(m, axis=1, keepdims=True) - _row_to_col(jnp.sum(m, axis=0, keepdims=True), eye)
            de = de + jnp.sum(dqd * qn, axis=1, keepdims=True)
            df = jnp.sum(dkd * kn, axis=1, keepdims=True)
            dgam = dgam + de * e - df * f
            dgam_last = jnp.sum(df * f, axis=0, keepdims=True) + dgl * gl
            dgam = dgam + jnp.where(row == CHUNK - 1, dgam_last, 0.0)
            dy_ref[:, h * LANES:(h + 1) * LANES] = dq
            dy_ref[:, 512 + h * LANES:512 + (h + 1) * LANES] = dk
            dy_ref[:, 1024 + h * LANES:1024 + (h + 1) * LANES] = dv
            dgates = dgates + jnp.where(lane == h, dbeta, 0.0) + jnp.where(lane == 4 + h, dgam, 0.0)
        dg_ref[...] = dgates

    rev = lambda n: (nch - 1 - n, 0)
    return pl.pallas_call(
        body, name="gdn_bwd", grid=(nch,),
        in_specs=[pl.BlockSpec((CHUNK, GDN_QKV), rev), pl.BlockSpec((CHUNK, LANES), rev),
                  pl.BlockSpec((1, GDN_HEADS, LANES, LANES), lambda n: (nch - 1 - n, 0, 0, 0)), pl.BlockSpec((CHUNK, 512), rev)],
        out_specs=(pl.BlockSpec((CHUNK, GDN_QKV), rev), pl.BlockSpec((CHUNK, LANES), rev)),
        out_shape=(jax.ShapeDtypeStruct((t, GDN_QKV), f32), jax.ShapeDtypeStruct((t, LANES), f32)),
        scratch_shapes=[pltpu.VMEM((GDN_HEADS, LANES, LANES), f32)],
        compiler_params=_params(("arbitrary",)),
    )(y, gcum, sall, do)


def _fox_scores(q_ref, k_ref, gcum_ref, gcumt_ref, h, i, t):
    pr = h // 2
    lo = (h % 2) * FOX_DH
    lane = _lane((FOX_BQ, LANES))
    mask = (lane >= lo) & (lane < lo + FOX_DH)
    qm = jnp.where(mask, q_ref[:, pr * LANES:(pr + 1) * LANES], 0.0).astype(bf16)
    kp = k_ref[:, pr * LANES:(pr + 1) * LANES].astype(bf16)
    s = _dot_nt(qm, kp, None) * (FOX_DH ** -0.5)
    s = s + gcum_ref[:, 8 + h:9 + h] - gcumt_ref[8 + h:9 + h, :]
    rows = i * FOX_BQ + lax.broadcasted_iota(jnp.int32, (FOX_BQ, t), 0)
    cols = lax.broadcasted_iota(jnp.int32, (FOX_BQ, t), 1)
    return jnp.where(cols <= rows, s, NEG), mask, qm, kp


def _fox_fwd(proj, gcum, gcumt):
    t = proj.shape[0]

    def body(q_ref, k_ref, v_ref, gcum_ref, gcumt_ref, o_ref, lse_ref):
        i = pl.program_id(0)
        lane = _lane((FOX_BQ, LANES))
        lse_all = jnp.zeros((FOX_BQ, LANES), f32)
        for pr in range(FOX_HEADS // 2):
            vp = v_ref[:, pr * LANES:(pr + 1) * LANES].astype(bf16)
            o_pair = jnp.zeros((FOX_BQ, LANES), f32)
            for h in (2 * pr, 2 * pr + 1):
                s, mask, _, _ = _fox_scores(q_ref, k_ref, gcum_ref, gcumt_ref, h, i, t)
                m = jnp.max(s, axis=1, keepdims=True)
                p = jnp.exp(s - m)
                l = jnp.sum(p, axis=1, keepdims=True)
                o_h = _dot((p * (1.0 / l)).astype(bf16), vp, None)
                o_pair = jnp.where(mask, o_h, o_pair)
                lse_all = jnp.where(lane == h, m + jnp.log(l), lse_all)
            o_ref[:, pr * LANES:(pr + 1) * LANES] = o_pair
        lse_ref[...] = lse_all

    qblk = lambda col: pl.BlockSpec((FOX_BQ, 512), lambda i: (i, col))
    full = lambda col: pl.BlockSpec((t, 512), lambda i: (0, col))
    c0 = SEG_FOX // 512
    return pl.pallas_call(
        body, name="fox_fwd", grid=(t // FOX_BQ,),
        in_specs=[qblk(c0), full(c0 + 1), full(c0 + 2), pl.BlockSpec((FOX_BQ, LANES), lambda i: (i, 0)),
                  pl.BlockSpec((LANES, t), lambda i: (0, 0))],
        out_specs=(pl.BlockSpec((FOX_BQ, 512), lambda i: (i, 0)), pl.BlockSpec((FOX_BQ, LANES), lambda i: (i, 0))),
        out_shape=(jax.ShapeDtypeStruct((t, 512), f32), jax.ShapeDtypeStruct((t, LANES), f32)),
        compiler_params=_params(("parallel",)),
    )(proj, proj, proj, gcum, gcumt)


def _fox_bwd(proj, gcum, gcumt, o, lse, do):
    t = proj.shape[0]

    def body(q_ref, k_ref, v_ref, gcum_ref, gcumt_ref, o_ref, lse_ref, do_ref, dq_ref, dk_ref, dv_ref, dcc_ref, dct_ref):
        i = pl.program_id(0)

        @pl.when(i == 0)
        def _():
            dk_ref[...] = jnp.zeros_like(dk_ref)
            dv_ref[...] = jnp.zeros_like(dv_ref)
            dct_ref[...] = jnp.zeros_like(dct_ref)

        lane = _lane((FOX_BQ, LANES))
        dcc = jnp.zeros((FOX_BQ, LANES), f32)
        scale = FOX_DH ** -0.5
        for pr in range(FOX_HEADS // 2):
            sl = slice(pr * LANES, (pr + 1) * LANES)
            vp = v_ref[:, sl].astype(bf16)
            dq_pair = jnp.zeros((FOX_BQ, LANES), f32)
            for h in (2 * pr, 2 * pr + 1):
                s, mask, qm, kp = _fox_scores(q_ref, k_ref, gcum_ref, gcumt_ref, h, i, t)
                p = jnp.exp(s - lse_ref[:, h:h + 1])
                dom = jnp.where(mask, do_ref[:, sl], 0.0)
                delta = jnp.sum(dom * o_ref[:, sl], axis=1, keepdims=True)
                domb = dom.astype(bf16)
                dp = _dot_nt(domb, vp, None)
                ds = p * (dp - delta)
                dsb = ds.astype(bf16)
                dv_ref[:, sl] += _dot_tn(p.astype(bf16), domb, None)
                dk_ref[:, sl] += _dot_tn(dsb, qm, None) * scale
                dq_pair = jnp.where(mask, _dot(dsb, kp, None) * scale, dq_pair)
                dcc = jnp.where(lane == 8 + h, jnp.sum(ds, axis=1, keepdims=True), dcc)
                dct_ref[8 + h:9 + h, :] += -jnp.sum(ds, axis=0, keepdims=True)
            dq_ref[:, sl] = dq_pair.astype(bf16)
        dcc_ref[...] = dcc

    qblk = lambda col: pl.BlockSpec((FOX_BQ, 512), lambda i: (i, col))
    full = lambda col: pl.BlockSpec((t, 512), lambda i: (0, col))
    rblk = pl.BlockSpec((FOX_BQ, LANES), lambda i: (i, 0))
    c0 = SEG_FOX // 512
    return pl.pallas_call(
        body, name="fox_bwd", grid=(t // FOX_BQ,),
        in_specs=[qblk(c0), full(c0 + 1), full(c0 + 2), rblk, pl.BlockSpec((LANES, t), lambda i: (0, 0)),
                  qblk(0), rblk, qblk(0)],
        out_specs=(qblk(0), full(0), full(0), rblk, pl.BlockSpec((LANES, t), lambda i: (0, 0))),
        out_shape=(jax.ShapeDtypeStruct((t, 512), bf16), jax.ShapeDtypeStruct((t, 512), f32), jax.ShapeDtypeStruct((t, 512), f32),
                   jax.ShapeDtypeStruct((t, LANES), f32), jax.ShapeDtypeStruct((LANES, t), f32)),
        compiler_params=_params(("arbitrary",)),
    )(proj, proj, proj, gcum, gcumt, o, lse, do)


def _row(v, width=None):
    v = v.reshape(1, -1).astype(f32)
    if width is not None and v.shape[1] < width:
        v = jnp.pad(v, ((0, 0), (0, width - v.shape[1])))
    return v


def _device_grads(x, p, target, small, w_cat, conv_w, w_out, w_up, w_down, w_ple, w_gate):
    z4 = jnp.zeros((4,), f32)
    bias_row = _row(jnp.concatenate([z4, small["dt_bias"].reshape(-1), small["b_f"].reshape(-1)]), LANES)
    alog_row = _row(jnp.concatenate([z4, small["a_log"].reshape(-1)]), LANES)
    g_gdn = _row(small["gdn_norm_g"])
    g_fox2 = _row(jnp.tile(small["fox_norm_g"].reshape(-1), 2))
    pb = p.astype(bf16)

    h0, h0b = _ln_in(x, _row(small["ln_in_g"]), _row(small["ln_in_b"]))
    proj = _mm(h0b, w_cat, "nn", 256, D_CAT, "mm_proj")
    gates, gcum, gcumt = _gates(proj, bias_row, alog_row)
    conv_c, qkv_n = _gdn_conv(proj, conv_w)
    o_gdn, sall = _gdn_fwd(qkv_n, gcum)
    o_fox, lse = _fox_fwd(proj, gcum, gcumt)
    attn = _attn_post(o_gdn, proj, o_fox, g_gdn, g_fox2)
    mix = _mm(attn, w_out, "nn", 512, D_MODEL, "mm_mix")
    h1, h1b, xhat1, rstd1 = _ln1(h0, mix, _row(small["ln1_g"]), _row(small["ln1_b"]))
    up, act = _mm(h1b, w_up, "nn", 256, 2048, "mm_up", epi="relu2")
    ff = _mm(act, w_down, "nn", 256, D_MODEL, "mm_down")
    gp = _mm(h1b, w_gate, "nn", 512, D_MODEL, "mm_gate")
    pe = _mm(pb, w_ple, "nn", 512, D_MODEL, "mm_ple")
    dr2, dr2b, dpe, dgp, pg2 = _ln2_loss(h1, ff, pe, gp, _row(small["b_ple_gate"]), _row(small["ln2_g"]), _row(small["ln2_b"]), target)

    dup = _mm(dr2b, w_down, "nt", 256, 2048, "mm_dact", epi="relu2_bwd", extra=up)
    g_down = _mm(act, dr2b, "tn", 1024, D_MODEL, "mm_gdown")
    dh1_a = _mm(dup, w_up, "nt", 256, D_MODEL, "mm_dh1a")
    g_up = _mm(h1b, dup, "tn", 1024, 1024, "mm_gup")
    dh1_b = _mm(dgp, w_gate, "nt", 512, D_MODEL, "mm_dh1b")
    g_gate = _mm(h1b, dgp, "tn", 1024, D_MODEL, "mm_ggate")
    g_ple = _mm(pb, dpe, "tn", D_PLE, D_MODEL, "mm_gple")
    dr1, dr1b, pg1 = _ln1_bwd(dr2, dh1_a, dh1_b, xhat1, rstd1, _row(small["ln1_g"]))
    dattn = _mm(dr1b, w_out, "nt", 512, D_MODEL, "mm_dattn")
    g_out = _mm(attn, dr1b, "tn", 1024, D_MODEL, "mm_gout")
    do_gdn, dz, do_fox, pga = _attn_post_bwd(dattn, o_gdn, proj, o_fox, g_gdn, g_fox2)
    dfq, dfk, dfv, dccol, dct = _fox_bwd(proj, gcum, gcumt, o_fox, lse, do_fox)
    dqkv_n, dgates = _gdn_bwd(qkv_n, gcum, sall, do_gdn)
    dsmall, pgg = _gates_bwd(proj, bias_row, alog_row, gates, dgates, dccol, dct)
    du, g_conv8 = _gdn_conv_bwd(proj, conv_w, conv_c, dqkv_n)
    t = x.shape[0]
    dproj = jnp.concatenate([du, dz, dfq, dfk.astype(bf16), dfv.astype(bf16), dsmall, jnp.zeros((t, D_CAT - SEG_SMALL - LANES), bf16)], axis=1)
    dh0_mm = _mm(dproj, w_cat, "nt", 256, D_MODEL, "mm_dh0")
    g_cat = _mm(h0b, dproj, "tn", 1024, 1280, "mm_gcat")
    grad_x, pg0 = _ln_in_bwd(x, dr1, dh0_mm, _row(small["ln_in_g"]))

    g_fox = pga[1, :FOX_DH] + pga[1, FOX_DH:]
    small_grads = dict(
        ln_in_g=pg0[0], ln_in_b=pg0[1], ln1_g=pg1[0], ln1_b=pg1[1], b_ple_gate=pg2[2], ln2_g=pg2[0], ln2_b=pg2[1],
        gdn_norm_g=pga[0], fox_norm_g=g_fox, a_log=pgg[1, 4:8], dt_bias=pgg[0, 4:8], b_f=pgg[0, 8:16], loss=pg2[3, 0:1])
    big_grads = dict(w_cat=g_cat, conv_w=g_conv8[:CONV_W], w_out=g_out, w_up=g_up, w_down=g_down, w_ple=g_ple, w_gate=g_gate)
    return grad_x, big_grads, small_grads


ANY = pl.BlockSpec(memory_space=pl.ANY)
HALF_ROWS = FLAT_ROWS // 2
CONV_PKT_ROWS = 16


def _mesh_pos():
    return lax.axis_index("x"), lax.axis_index("y"), lax.axis_index("c")


def _other_chips(x, y):
    return [(1 - x, y), (x, 1 - y), (1 - x, 1 - y)]


def _rcopy(src, dst, send_sem, recv_sem, dev):
    return pltpu.make_async_remote_copy(src_ref=src, dst_ref=dst, send_sem=send_sem, recv_sem=recv_sem,
                                        device_id=dev, device_id_type=MESH)


def _gather_weights(flat, conv_pkt):
    cr = HALF_ROWS // AG_CHUNKS
    n_ici = 3 * AG_CHUNKS

    def body(flat_ref, conv_ref, out_ref, conv_out_ref, send_sems, recv_sems, csend, crecv, local_sems):
        x, y, c = _mesh_pos()
        q = 2 * x + y
        chips = _other_chips(x, y)
        sib = (x, y, 1 - c)

        def rows(hf, ch):
            return pl.ds(hf * HALF_ROWS + ch * cr, cr)

        own = pltpu.make_async_copy(flat_ref, out_ref.at[q], local_sems.at[0])
        own.start()
        own_conv = pltpu.make_async_copy(conv_ref, conv_out_ref.at[q], local_sems.at[1])
        own_conv.start()
        started = []
        for ch in range(AG_CHUNKS):
            for k, chip in enumerate(chips):
                cp = _rcopy(flat_ref.at[rows(c, ch)], out_ref.at[q, rows(c, ch)], send_sems.at[k * AG_CHUNKS + ch],
                            recv_sems.at[k * AG_CHUNKS + ch], (*chip, c))
                cp.start()
                started.append(cp)
        for k, chip in enumerate(chips):
            cp = _rcopy(conv_ref, conv_out_ref.at[q], csend.at[k], crecv.at[k], (*chip, c))
            cp.start()
            started.append(cp)
        for ch in range(AG_CHUNKS):
            for k, chip in enumerate(chips):
                qk = 2 * chip[0] + chip[1]
                landed = out_ref.at[qk, rows(c, ch)]
                _rcopy(flat_ref.at[rows(c, ch)], landed, send_sems.at[k * AG_CHUNKS + ch], recv_sems.at[k * AG_CHUNKS + ch],
                       (*chip, c)).wait_recv()
                cp = _rcopy(landed, landed, send_sems.at[n_ici + k * AG_CHUNKS + ch], recv_sems.at[n_ici + k * AG_CHUNKS + ch], sib)
                cp.start()
                started.append(cp)
        for ch in range(AG_CHUNKS):
            for k, chip in enumerate(chips):
                qk = 2 * chip[0] + chip[1]
                passed = out_ref.at[qk, rows(1 - c, ch)]
                _rcopy(passed, passed, send_sems.at[n_ici + k * AG_CHUNKS + ch], recv_sems.at[n_ici + k * AG_CHUNKS + ch], sib).wait_recv()
        for k, chip in enumerate(chips):
            qk = 2 * chip[0] + chip[1]
            _rcopy(conv_ref, conv_out_ref.at[qk], csend.at[k], crecv.at[k], (*chip, c)).wait_recv()
        for cp in started:
            cp.wait_send()
        own.wait()
        own_conv.wait()

    return pl.pallas_call(
        body, name="gather_weights",
        out_shape=(jax.ShapeDtypeStruct((N_CHIPS,) + flat.shape, flat.dtype), jax.ShapeDtypeStruct((N_CHIPS,) + conv_pkt.shape, conv_pkt.dtype)),
        in_specs=[ANY, ANY], out_specs=(ANY, ANY),
        scratch_shapes=[pltpu.SemaphoreType.DMA((2 * n_ici,)), pltpu.SemaphoreType.DMA((2 * n_ici,)), pltpu.SemaphoreType.DMA((3,)),
                        pltpu.SemaphoreType.DMA((3,)), pltpu.SemaphoreType.DMA((2,))],
    )(flat, conv_pkt)


def _exchange_pairs(g4, small):
    def body(g_ref, small_ref, b1_ref, all_ref, ssem, rsem, s2, r2, local_sem):
        x, y, c = _mesh_pos()
        me = 4 * x + 2 * y + c
        sib = (x, y, 1 - c)
        own = pltpu.make_async_copy(small_ref, all_ref.at[me], local_sem)
        own.start()
        started = []
        for d in range(N_CHIPS):
            cp = _rcopy(g_ref.at[d, pl.ds((1 - c) * HALF_ROWS, HALF_ROWS)], b1_ref.at[d], ssem.at[d], rsem.at[d], sib)
            cp.start()
            started.append(cp)
        peers = []
        for r in range(1, 8):
            fx, fy, fc = (r >> 2) & 1, (r >> 1) & 1, r & 1
            peers.append((1 - x if fx else x, 1 - y if fy else y, 1 - c if fc else c))
        for r, peer in enumerate(peers):
            cp = _rcopy(small_ref, all_ref.at[me], s2.at[r], r2.at[r], peer)
            cp.start()
            started.append(cp)
        for d in range(N_CHIPS):
            _rcopy(g_ref.at[d, pl.ds((1 - c) * HALF_ROWS, HALF_ROWS)], b1_ref.at[d], ssem.at[d], rsem.at[d], sib).wait_recv()
        for r, peer in enumerate(peers):
            _rcopy(small_ref, all_ref.at[4 * peer[0] + 2 * peer[1] + peer[2]], s2.at[r], r2.at[r], peer).wait_recv()
        for cp in started:
            cp.wait_send()
        own.wait()

    return pl.pallas_call(
        body, name="exchange_pairs",
        out_shape=(jax.ShapeDtypeStruct((N_CHIPS, HALF_ROWS, g4.shape[2]), g4.dtype), jax.ShapeDtypeStruct((8,) + small.shape, small.dtype)),
        in_specs=[ANY, ANY], out_specs=(ANY, ANY),
        scratch_shapes=[pltpu.SemaphoreType.DMA((N_CHIPS,)), pltpu.SemaphoreType.DMA((N_CHIPS,)), pltpu.SemaphoreType.DMA((7,)),
                        pltpu.SemaphoreType.DMA((7,)), pltpu.SemaphoreType.DMA],
    )(g4, small)


def _exchange_chips(a4):
    def body(a_ref, b2_ref, ssem, rsem):
        x, y, c = _mesh_pos()
        started = []
        for k, chip in enumerate(_other_chips(x, y)):
            cp = _rcopy(a_ref.at[2 * chip[0] + chip[1]], b2_ref.at[k], ssem.at[k], rsem.at[k], (*chip, c))
            cp.start()
            started.append(cp)
        for k, chip in enumerate(_other_chips(x, y)):
            _rcopy(a_ref.at[2 * chip[0] + chip[1]], b2_ref.at[k], ssem.at[k], rsem.at[k], (*chip, c)).wait_recv()
        for cp in started:
            cp.wait_send()

    return pl.pallas_call(
        body, name="exchange_chips", out_shape=jax.ShapeDtypeStruct((3,) + a4.shape[1:], a4.dtype),
        in_specs=[ANY], out_specs=ANY,
        scratch_shapes=[pltpu.SemaphoreType.DMA((3,)), pltpu.SemaphoreType.DMA((3,))],
    )(a4)


def _share_halves(rh):
    def body(rh_ref, out_ref, ssem, rsem, local_sem):
        x, y, c = _mesh_pos()
        sib = (x, y, 1 - c)
        mine = out_ref.at[pl.ds(c * HALF_ROWS, HALF_ROWS)]
        own = pltpu.make_async_copy(rh_ref, mine, local_sem)
        own.start()
        cp = _rcopy(rh_ref, mine, ssem, rsem, sib)
        cp.start()
        _rcopy(rh_ref, out_ref.at[pl.ds((1 - c) * HALF_ROWS, HALF_ROWS)], ssem, rsem, sib).wait_recv()
        cp.wait_send()
        own.wait()

    return pl.pallas_call(
        body, name="share_halves", out_shape=jax.ShapeDtypeStruct((FLAT_ROWS, rh.shape[1]), rh.dtype),
        in_specs=[ANY], out_specs=ANY,
        scratch_shapes=[pltpu.SemaphoreType.DMA, pltpu.SemaphoreType.DMA, pltpu.SemaphoreType.DMA],
    )(rh)


ADD_ROWS = 256


def _add_pair(g4, b1, c_idx):
    nb = HALF_ROWS // ADD_ROWS
    cols = g4.shape[2]

    def body(c_ref, g_ref, b_ref, o_ref):
        o_ref[...] = g_ref[...] + b_ref[...]

    blk = (1, ADD_ROWS, cols)
    return pl.pallas_call(
        body, name="add_pair",
        grid_spec=pltpu.PrefetchScalarGridSpec(
            num_scalar_prefetch=1, grid=(N_CHIPS, nb),
            in_specs=[pl.BlockSpec(blk, lambda d, i, c: (d, c[0] * nb + i, 0)), pl.BlockSpec(blk, lambda d, i, c: (d, i, 0))],
            out_specs=pl.BlockSpec(blk, lambda d, i, c: (d, i, 0))),
        out_shape=jax.ShapeDtypeStruct((N_CHIPS, HALF_ROWS, cols), f32),
        compiler_params=_params(("parallel", "parallel")),
    )(c_idx, g4, b1)


def _add_chips(a4, b2, q_idx):
    nb = HALF_ROWS // ADD_ROWS
    cols = a4.shape[2]

    def body(q_ref, a_ref, b_ref, o_ref):
        o_ref[...] = ((a_ref[0] + b_ref[0]) + b_ref[1]) + b_ref[2]

    return pl.pallas_call(
        body, name="add_chips",
        grid_spec=pltpu.PrefetchScalarGridSpec(
            num_scalar_prefetch=1, grid=(nb,),
            in_specs=[pl.BlockSpec((1, ADD_ROWS, cols), lambda i, q: (q[0], i, 0)), pl.BlockSpec((3, ADD_ROWS, cols), lambda i, q: (0, i, 0))],
            out_specs=pl.BlockSpec((ADD_ROWS, cols), lambda i, q: (i, 0))),
        out_shape=jax.ShapeDtypeStruct((HALF_ROWS, cols), f32),
        compiler_params=_params(("parallel",)),
    )(q_idx, a4, b2)


def _adamw_math(w, g, m, v):
    m = ADAM_B1 * m + (1.0 - ADAM_B1) * g
    v = ADAM_B2 * v + (1.0 - ADAM_B2) * (g * g)
    m_hat = m / (1.0 - ADAM_B1 ** ADAM_STEP)
    v_hat = v / (1.0 - ADAM_B2 ** ADAM_STEP)
    return -ADAM_LR * (m_hat / (jnp.sqrt(v_hat) + ADAM_EPS) + ADAM_WD * w), m, v


def _adamw(w, g, m, v, name):
    rows, cols = w.shape
    rb = ADD_ROWS if rows % ADD_ROWS == 0 else rows

    def body(w_ref, g_ref, m_ref, v_ref, d_ref, mo_ref, vo_ref):
        d_ref[...], mo_ref[...], vo_ref[...] = _adamw_math(w_ref[...], g_ref[...], m_ref[...], v_ref[...])

    blk = pl.BlockSpec((rb, cols), lambda i: (i, 0))
    return pl.pallas_call(
        body, name=name, grid=(rows // rb,), in_specs=[blk] * 4, out_specs=(blk,) * 3,
        out_shape=(jax.ShapeDtypeStruct(w.shape, f32),) * 3, compiler_params=_params(("parallel",)),
    )(w, g, m, v)


def _small_sum_adamw(all_pkts, w, m, v):
    def body(a_ref, w_ref, m_ref, v_ref, g_ref, d_ref, mo_ref, vo_ref):
        g = a_ref[0]
        for r in range(1, 8):
            g = g + a_ref[r]
        g_ref[...] = g
        d_ref[...], mo_ref[...], vo_ref[...] = _adamw_math(w_ref[...], g, m_ref[...], v_ref[...])

    return pl.pallas_call(body, name="small_sum_adamw", out_shape=(jax.ShapeDtypeStruct(w.shape, f32),) * 4)(all_pkts, w, m, v)


FLAT_LAYOUT = (("w_in", (D_MODEL, D_IN // N_CHIPS)), ("w_out", (D_MODEL // N_CHIPS, D_MODEL)), ("w_up", (D_MODEL, D_FF // N_CHIPS)),
               ("w_down", (D_FF // N_CHIPS, D_MODEL)), ("w_ple", (D_PLE, D_MODEL // N_CHIPS)), ("w_ple_gate", (D_MODEL // N_CHIPS, D_MODEL)))
SHARD_AXIS = dict(w_in=1, w_out=0, w_up=1, w_down=0, w_ple=1, w_ple_gate=0)
SMALL_LAYOUT = (("ln_in_g", 0, 1024), ("ln_in_b", 8, 1024), ("ln1_g", 16, 1024), ("ln1_b", 24, 1024), ("b_ple_gate", 32, 1024),
                ("ln2_g", 40, 1024), ("ln2_b", 48, 1024), ("gdn_norm_g", 56, 128), ("fox_norm_g", 57, 64), ("a_log", 58, 4),
                ("dt_bias", 59, 4), ("b_f", 60, 8), ("loss", 61, 1))
SMALL_CONV_ROW = 64
SMALL_ROWS = 128


def _pack_flat(parts, dtype):
    flat = jnp.concatenate([parts[n].astype(dtype).reshape(-1) for n, _ in FLAT_LAYOUT])
    return jnp.pad(flat, (0, FLAT_ROWS * FLAT_COLS - flat.shape[0])).reshape(FLAT_ROWS, FLAT_COLS)


def _unpack_flat(flat):
    flat = flat.reshape(-1)
    out, off = {}, 0
    for n, shp in FLAT_LAYOUT:
        size = shp[0] * shp[1]
        out[n] = flat[off:off + size].reshape(shp)
        off += size
    return out


def _pack_small(vals, conv=None):
    rows = []
    nxt = 0
    for n, r0, size in SMALL_LAYOUT:
        assert r0 == nxt
        v = vals[n].reshape(-1).astype(f32) if n in vals else jnp.zeros((size,), f32)
        nrows = -(-size // LANES)
        rows.append(jnp.pad(v, (0, nrows * LANES - size)).reshape(nrows, LANES))
        nxt = r0 + nrows
    rows.append(jnp.zeros((SMALL_CONV_ROW - nxt, LANES), f32))
    conv_rows = CONV_W * GDN_QKV // LANES
    rows.append(jnp.zeros((conv_rows, LANES), f32) if conv is None else conv.reshape(conv_rows, LANES))
    rows.append(jnp.zeros((SMALL_ROWS - SMALL_CONV_ROW - conv_rows, LANES), f32))
    return jnp.concatenate(rows, axis=0)


def _unpack_small(pkt, shapes):
    out = {}
    for n, r0, size in SMALL_LAYOUT:
        if n in shapes:
            nrows = -(-size // LANES)
            out[n] = pkt[r0:r0 + nrows].reshape(-1)[:size].reshape(shapes[n])
    return out


WEIGHTS = ("ln_in_g", "ln_in_b", "w_in", "conv_w", "a_log", "dt_bias", "gdn_norm_g", "b_f", "fox_norm_g", "w_out", "ln1_g", "ln1_b",
           "w_up", "w_down", "w_ple", "w_ple_gate", "b_ple_gate", "ln2_g", "ln2_b")
SMALL_NAMES = tuple(n for n, _, _ in SMALL_LAYOUT if n != "loss")


def kernel(x, p, ln_in_g, ln_in_b, w_in, conv_w, a_log, dt_bias, gdn_norm_g, b_f, fox_norm_g, w_out, ln1_g, ln1_b, w_up, w_down, w_ple, w_ple_gate, b_ple_gate, ln2_g, ln2_b, loss_target, m_ln_in_g, m_ln_in_b, m_w_in, m_conv_w, m_a_log, m_dt_bias, m_gdn_norm_g, m_b_f, m_fox_norm_g, m_w_out, m_ln1_g, m_ln1_b, m_w_up, m_w_down, m_w_ple, m_w_ple_gate, m_b_ple_gate, m_ln2_g, m_ln2_b, v_ln_in_g, v_ln_in_b, v_w_in, v_conv_w, v_a_log, v_dt_bias, v_gdn_norm_g, v_b_f, v_fox_norm_g, v_w_out, v_ln1_g, v_ln1_b, v_w_up, v_w_down, v_w_ple, v_w_ple_gate, v_b_ple_gate, v_ln2_g, v_ln2_b):
    given = dict(locals())
    w = {n: given[n] for n in WEIGHTS}
    m = {n: given["m_" + n] for n in WEIGHTS}
    v = {n: given["v_" + n] for n in WEIGHTS}
    xi, yi, ci = _mesh_pos()
    q = 2 * xi + yi

    shard2d = {n: w[n][0] for n, _ in FLAT_LAYOUT}
    conv_pkt = jnp.pad(w["conv_w"][0].reshape(-1, LANES), ((0, CONV_PKT_ROWS - CONV_W * GDN_QKV // N_CHIPS // LANES), (0, 0)))
    full_flat, conv_all = _gather_weights(_pack_flat(shard2d, bf16), conv_pkt)
    per_chip = [_unpack_flat(full_flat[d]) for d in range(N_CHIPS)]
    full = {n: jnp.concatenate([per_chip[d][n] for d in range(N_CHIPS)], axis=SHARD_AXIS[n]) for n, _ in FLAT_LAYOUT}
    conv_rows = CONV_W * GDN_QKV // N_CHIPS // LANES
    conv_full = jnp.concatenate([conv_all[d, :conv_rows].reshape(CONV_W, GDN_QKV // N_CHIPS) for d in range(N_CHIPS)], axis=1)
    wi = full["w_in"]
    w_cat = jnp.concatenate([wi[:, :OFF_BETA], wi[:, OFF_FOX:OFF_F], wi[:, OFF_BETA:OFF_FOX], wi[:, OFF_F:],
                             jnp.zeros((D_MODEL, D_CAT - D_IN), bf16)], axis=1)

    small = {n: w[n] for n in SMALL_NAMES}
    grad_x, big, small_g = _device_grads(x[0], p[0, 0], loss_target[0], small, w_cat, conv_full, full["w_out"], full["w_up"],
                                         full["w_down"], full["w_ple"], full["w_ple_gate"])

    gc = big["w_cat"]
    g_full = dict(w_in=jnp.concatenate([gc[:, :OFF_BETA], gc[:, SEG_SMALL:SEG_SMALL + 8], gc[:, SEG_FOX:SEG_SMALL],
                                        gc[:, SEG_SMALL + 8:SEG_SMALL + 16]], axis=1),
                  w_out=big["w_out"], w_up=big["w_up"], w_down=big["w_down"], w_ple=big["w_ple"], w_ple_gate=big["w_gate"])
    g4 = []
    for d in range(N_CHIPS):
        parts = {}
        for n, shp in FLAT_LAYOUT:
            ax = SHARD_AXIS[n]
            parts[n] = lax.slice_in_dim(g_full[n], d * shp[ax], (d + 1) * shp[ax], axis=ax)
        g4.append(_pack_flat(parts, f32))
    g4 = jnp.stack(g4)
    b1, small_all = _exchange_pairs(g4, _pack_small(small_g, big["conv_w"]))
    a4 = _add_pair(g4, b1, ci.reshape(1).astype(jnp.int32))
    b2 = _exchange_chips(a4)
    rh = _add_chips(a4, b2, q.reshape(1).astype(jnp.int32))
    g_shard = _unpack_flat(_share_halves(rh))

    grads, delta, new_m, new_v = {}, {}, {}, {}
    for n, shp in FLAT_LAYOUT:
        grads[n] = g_shard[n].reshape(w[n].shape)
        d_, m_, v_ = _adamw(w[n][0], g_shard[n], m[n][0], v[n][0], "adamw_" + n)
        delta[n], new_m[n], new_v[n] = (a.reshape(w[n].shape) for a in (d_, m_, v_))
    shapes = {n: w[n].shape for n in SMALL_NAMES}
    g_pkt, d_pkt, m_pkt, v_pkt = _small_sum_adamw(small_all, _pack_small(w), _pack_small(m), _pack_small(v))
    for dst, pkt in ((grads, g_pkt), (delta, d_pkt), (new_m, m_pkt), (new_v, v_pkt)):
        dst.update(_unpack_small(pkt, shapes))
    conv_rows_all = CONV_W * GDN_QKV // LANES
    conv_g_full = g_pkt[SMALL_CONV_ROW:SMALL_CONV_ROW + conv_rows_all].reshape(CONV_W, GDN_QKV)
    conv_g = lax.dynamic_slice_in_dim(conv_g_full, q * (GDN_QKV // N_CHIPS), GDN_QKV // N_CHIPS, axis=1)
    d_, m_, v_ = _adamw(w["conv_w"][0], conv_g, m["conv_w"][0], v["conv_w"][0], "adamw_conv_w")
    grads["conv_w"] = conv_g.reshape(w["conv_w"].shape)
    delta["conv_w"], new_m["conv_w"], new_v["conv_w"] = (a.reshape(w["conv_w"].shape) for a in (d_, m_, v_))
    loss = g_pkt[61, 0]
    return (loss, grad_x[None], *[grads[n] for n in WEIGHTS], *[delta[n] for n in WEIGHTS],
            *[new_m[n] for n in WEIGHTS], *[new_v[n] for n in WEIGHTS])
```

```python
import functools

import jax
import jax.numpy as jnp
from jax import lax
from jax.experimental import pallas as pl
from jax.experimental.pallas import tpu as pltpu

f32 = jnp.float32
bf16 = jnp.bfloat16
HI = lax.Precision.HIGHEST
MESH = pl.DeviceIdType.MESH

D_MODEL = 1024
CHUNK = 64
GDN_HEADS = 4
GDN_DK = 128
FOX_HEADS = 8
FOX_DH = 64
CONV_W = 4
D_FF = 4096
D_PLE = 256
LN_EPS = 1e-5
NORM_EPS = 1e-6
ALPHA = 2.0 ** 0.25
GDN_QKV = 1536
OFF_Z = 1536
OFF_BETA = 2048
OFF_FOX = 2056
OFF_F = 3592
D_IN = 3600
ADAM_LR = 0.001
ADAM_B1 = 0.9
ADAM_B2 = 0.999
ADAM_EPS = 1e-08
ADAM_WD = 0.01
ADAM_STEP = 10

SEG_FOX = 2048
SEG_SMALL = 3584
D_CAT = 3840
LANES = 128
TOK_BLK = 256
FOX_BQ = 256
VMEM_LIMIT = 56 * 1024 * 1024
NEG = -1e30

N_CHIPS = 4
FLAT_COLS = 1024
FLAT_ROWS = 3584
AG_CHUNKS = 4


def _params(sem=None, **kw):
    return pltpu.CompilerParams(dimension_semantics=sem, vmem_limit_bytes=VMEM_LIMIT, **kw)


def _sigmoid(x):
    return 1.0 / (1.0 + jnp.exp(-x))


def _softplus(x):
    return jnp.maximum(x, 0.0) + jnp.log(1.0 + jnp.exp(-jnp.abs(x)))


def _ln_fwd(x, g, b):
    mu = jnp.mean(x, -1, keepdims=True)
    xc = x - mu
    var = jnp.mean(xc * xc, -1, keepdims=True)
    rstd = lax.rsqrt(var + LN_EPS)
    xhat = xc * rstd
    return xhat * g + b, xhat, rstd


def _ln_bwd(dy, xhat, rstd, g):
    dxh = dy * g
    m1 = jnp.mean(dxh, -1, keepdims=True)
    m2 = jnp.mean(dxh * xhat, -1, keepdims=True)
    return rstd * (dxh - m1 - xhat * m2)


def _dot(a, b, prec=HI):
    return jnp.dot(a, b, precision=prec, preferred_element_type=f32)


def _dot_nt(a, b, prec=HI):
    return lax.dot_general(a, b, (((1,), (1,)), ((), ())), precision=prec, preferred_element_type=f32)


def _dot_tn(a, b, prec=HI):
    return lax.dot_general(a, b, (((0,), (0,)), ((), ())), precision=prec, preferred_element_type=f32)


def _bdot(a, b):
    return _dot(a.astype(bf16), b.astype(bf16), None)


def _bdot_nt(a, b):
    return _dot_nt(a.astype(bf16), b.astype(bf16), None)


def _bdot_tn(a, b):
    return _dot_tn(a.astype(bf16), b.astype(bf16), None)


def _lane(shape):
    return lax.broadcasted_iota(jnp.int32, shape, len(shape) - 1)


def _mm(a, b, mode, tm, tn, name, out_dtype=f32, epi=None, extra=None):
    if mode == "nn":
        (m, k), (_, n) = a.shape, b.shape
    elif mode == "nt":
        (m, k), (n, _) = a.shape, b.shape
    else:
        (k, m), (_, n) = a.shape, b.shape
    assert m % tm == 0 and n % tn == 0, (name, m, n, tm, tn)
    nc = 512 if tn % 512 == 0 else (256 if tn % 256 == 0 else 128)

    def body(a_ref, b_ref, *rest):
        for n0 in range(0, tn, nc):
            if mode == "nn":
                acc = jnp.dot(a_ref[...], b_ref[:, n0:n0 + nc], preferred_element_type=f32)
            elif mode == "nt":
                acc = lax.dot_general(a_ref[...], b_ref[n0:n0 + nc, :], (((1,), (1,)), ((), ())), preferred_element_type=f32)
            else:
                acc = lax.dot_general(a_ref[...], b_ref[:, n0:n0 + nc], (((0,), (0,)), ((), ())), preferred_element_type=f32)
            if epi == "relu2":
                up_ref, act_ref = rest
                up_ref[:, n0:n0 + nc] = acc
                r = jnp.maximum(acc, 0.0)
                act_ref[:, n0:n0 + nc] = (r * r).astype(bf16)
            elif epi == "relu2_bwd":
                up_ref, o_ref = rest
                o_ref[:, n0:n0 + nc] = (acc * (2.0 * jnp.maximum(up_ref[:, n0:n0 + nc], 0.0))).astype(bf16)
            else:
                (o_ref,) = rest
                o_ref[:, n0:n0 + nc] = acc.astype(out_dtype)

    if mode == "tn":
        a_spec = pl.BlockSpec((k, tm), lambda j, i: (0, i))
    else:
        a_spec = pl.BlockSpec((tm, k), lambda j, i: (i, 0))
    if mode == "nt":
        b_spec = pl.BlockSpec((tn, k), lambda j, i: (j, 0))
    else:
        b_spec = pl.BlockSpec((k, tn), lambda j, i: (0, j))
    o_spec = pl.BlockSpec((tm, tn), lambda j, i: (i, j))
    in_specs = [a_spec, b_spec]
    args = [a, b]
    if epi == "relu2":
        out_shape = (jax.ShapeDtypeStruct((m, n), f32), jax.ShapeDtypeStruct((m, n), bf16))
        out_specs = (o_spec, o_spec)
    elif epi == "relu2_bwd":
        in_specs.append(o_spec)
        args.append(extra)
        out_shape = jax.ShapeDtypeStruct((m, n), bf16)
        out_specs = o_spec
    else:
        out_shape = jax.ShapeDtypeStruct((m, n), out_dtype)
        out_specs = o_spec
    return pl.pallas_call(
        body, name=name, grid=(n // tn, m // tm), in_specs=in_specs, out_specs=out_specs, out_shape=out_shape,
        compiler_params=_params(("parallel", "parallel")),
    )(*args)


def _row_spec(width, col=0):
    return pl.BlockSpec((TOK_BLK, width), lambda i: (i, col))


def _vec_spec(rows, width):
    return pl.BlockSpec((rows, width), lambda i: (0, 0))


def _ln_in(x, g, b):
    t, d = x.shape

    def body(x_ref, g_ref, b_ref, h_ref, hb_ref):
        h, _, _ = _ln_fwd(x_ref[...], g_ref[...], b_ref[...])
        h_ref[...] = h
        hb_ref[...] = h.astype(bf16)

    return pl.pallas_call(
        body, name="ln_in", grid=(t // TOK_BLK,),
        in_specs=[_row_spec(d), _vec_spec(1, d), _vec_spec(1, d)],
        out_specs=(_row_spec(d), _row_spec(d)),
        out_shape=(jax.ShapeDtypeStruct((t, d), f32), jax.ShapeDtypeStruct((t, d), bf16)),
        compiler_params=_params(("parallel",)),
    )(x, g, b)


def _attn_post(o_gdn, proj, o_fox, g_gdn, g_fox2):
    t = o_gdn.shape[0]

    def body(og_ref, z_ref, of_ref, gg_ref, gf_ref, out_ref):
        for h in range(GDN_HEADS):
            sl = slice(h * LANES, (h + 1) * LANES)
            og = og_ref[:, sl]
            z = z_ref[:, sl]
            r = lax.rsqrt(jnp.mean(og * og, -1, keepdims=True) + NORM_EPS)
            out_ref[:, sl] = (og * r * gg_ref[...] * (z * _sigmoid(z))).astype(bf16)
        lo = _lane((TOK_BLK, LANES)) < FOX_DH
        for pr in range(FOX_HEADS // 2):
            sl = slice(pr * LANES, (pr + 1) * LANES)
            of = of_ref[:, sl]
            sq = of * of
            s0 = jnp.sum(jnp.where(lo, sq, 0.0), -1, keepdims=True)
            s1 = jnp.sum(jnp.where(lo, 0.0, sq), -1, keepdims=True)
            r = lax.rsqrt(jnp.where(lo, s0, s1) * (1.0 / FOX_DH) + NORM_EPS)
            out_ref[:, 512 + pr * LANES:512 + (pr + 1) * LANES] = (of * r * gf_ref[...]).astype(bf16)

    return pl.pallas_call(
        body, name="attn_post", grid=(t // TOK_BLK,),
        in_specs=[_row_spec(512), _row_spec(512, OFF_Z // 512), _row_spec(512), _vec_spec(1, LANES), _vec_spec(1, LANES)],
        out_specs=_row_spec(D_MODEL),
        out_shape=jax.ShapeDtypeStruct((t, D_MODEL), bf16),
        compiler_params=_params(("parallel",)),
    )(o_gdn, proj, o_fox, g_gdn, g_fox2)


def _attn_post_bwd(dattn, o_gdn, proj, o_fox, g_gdn, g_fox2):
    t = o_gdn.shape[0]

    def body(da_ref, og_ref, z_ref, of_ref, gg_ref, gf_ref, dog_ref, dz_ref, dof_ref, pg_ref):
        i = pl.program_id(0)

        @pl.when(i == 0)
        def _():
            pg_ref[...] = jnp.zeros_like(pg_ref)

        dgg = jnp.zeros((1, LANES), f32)
        for h in range(GDN_HEADS):
            sl = slice(h * LANES, (h + 1) * LANES)
            og = og_ref[:, sl]
            z = z_ref[:, sl]
            dout = da_ref[:, sl]
            g = gg_ref[...]
            r = lax.rsqrt(jnp.mean(og * og, -1, keepdims=True) + NORM_EPS)
            sg = _sigmoid(z)
            silu = z * sg
            ng = og * r * g
            dng = dout * silu
            dz_ref[:, sl] = (dout * ng * (sg * (1.0 + z * (1.0 - sg)))).astype(bf16)
            dgg = dgg + jnp.sum(dng * og * r, 0, keepdims=True)
            gd = dng * g
            dog_ref[:, sl] = r * gd - og * (r * r * r) * jnp.mean(og * gd, -1, keepdims=True)
        pg_ref[0:1, :] += dgg
        lo = _lane((TOK_BLK, LANES)) < FOX_DH
        dgf = jnp.zeros((1, LANES), f32)
        for pr in range(FOX_HEADS // 2):
            sl = slice(pr * LANES, (pr + 1) * LANES)
            of = of_ref[:, sl]
            dout = da_ref[:, 512 + pr * LANES:512 + (pr + 1) * LANES]
            g = gf_ref[...]
            sq = of * of
            s0 = jnp.sum(jnp.where(lo, sq, 0.0), -1, keepdims=True)
            s1 = jnp.sum(jnp.where(lo, 0.0, sq), -1, keepdims=True)
            r = lax.rsqrt(jnp.where(lo, s0, s1) * (1.0 / FOX_DH) + NORM_EPS)
            dgf = dgf + jnp.sum(dout * of * r, 0, keepdims=True)
            gd = dout * g
            xg = of * gd
            m0 = jnp.sum(jnp.where(lo, xg, 0.0), -1, keepdims=True)
            m1 = jnp.sum(jnp.where(lo, 0.0, xg), -1, keepdims=True)
            dof_ref[:, sl] = r * gd - of * (r * r * r) * (jnp.where(lo, m0, m1) * (1.0 / FOX_DH))
        pg_ref[1:2, :] += dgf

    return pl.pallas_call(
        body, name="attn_post_bwd", grid=(t // TOK_BLK,),
        in_specs=[_row_spec(D_MODEL), _row_spec(512), _row_spec(512, OFF_Z // 512), _row_spec(512), _vec_spec(1, LANES), _vec_spec(1, LANES)],
        out_specs=(_row_spec(512), _row_spec(512), _row_spec(512), _vec_spec(8, LANES)),
        out_shape=(jax.ShapeDtypeStruct((t, 512), f32), jax.ShapeDtypeStruct((t, 512), bf16),
                   jax.ShapeDtypeStruct((t, 512), f32), jax.ShapeDtypeStruct((8, LANES), f32)),
        compiler_params=_params(("arbitrary",)),
    )(dattn, o_gdn, proj, o_fox, g_gdn, g_fox2)


def _ln1(h0, mix, g, b):
    t, d = h0.shape

    def body(h0_ref, mix_ref, g_ref, b_ref, h_ref, hb_ref, xh_ref, rs_ref):
        h, xhat, rstd = _ln_fwd(ALPHA * h0_ref[...] + mix_ref[...], g_ref[...], b_ref[...])
        h_ref[...] = h
        hb_ref[...] = h.astype(bf16)
        xh_ref[...] = xhat
        rs_ref[...] = jnp.broadcast_to(rstd, rs_ref.shape)

    return pl.pallas_call(
        body, name="ln1", grid=(t // TOK_BLK,),
        in_specs=[_row_spec(d), _row_spec(d), _vec_spec(1, d), _vec_spec(1, d)],
        out_specs=(_row_spec(d), _row_spec(d), _row_spec(d), _row_spec(LANES)),
        out_shape=(jax.ShapeDtypeStruct((t, d), f32), jax.ShapeDtypeStruct((t, d), bf16),
                   jax.ShapeDtypeStruct((t, d), f32), jax.ShapeDtypeStruct((t, LANES), f32)),
        compiler_params=_params(("parallel",)),
    )(h0, mix, g, b)


def _ln2_loss(h1, ff, pe, gp, b_gate, g, b, target):
    t, d = h1.shape

    def body(h1_ref, ff_ref, pe_ref, gp_ref, bg_ref, g_ref, b_ref, t_ref, dr_ref, drb_ref, dpe_ref, dgp_ref, pg_ref):
        i = pl.program_id(0)

        @pl.when(i == 0)
        def _():
            pg_ref[...] = jnp.zeros_like(pg_ref)

        sig = _sigmoid(gp_ref[...] + bg_ref[...])
        pe = pe_ref[...]
        r2 = ALPHA * h1_ref[...] + ff_ref[...] + pe * sig
        y, xhat, rstd = _ln_fwd(r2, g_ref[...], b_ref[...])
        err = y - t_ref[...]
        dy = err * (1.0 / d)
        dr = _ln_bwd(dy, xhat, rstd, g_ref[...])
        dr_ref[...] = dr
        drb_ref[...] = dr.astype(bf16)
        dpe_ref[...] = (dr * sig).astype(bf16)
        dgp = dr * pe * sig * (1.0 - sig)
        dgp_ref[...] = dgp.astype(bf16)
        pg_ref[0:1, :] += jnp.sum(dy * xhat, 0, keepdims=True)
        pg_ref[1:2, :] += jnp.sum(dy, 0, keepdims=True)
        pg_ref[2:3, :] += jnp.sum(dgp, 0, keepdims=True)
        pg_ref[3:4, :] += 0.5 * jnp.sum(jnp.mean(err * err, -1, keepdims=True), 0, keepdims=True)

    return pl.pallas_call(
        body, name="ln2_loss", grid=(t // TOK_BLK,),
        in_specs=[_row_spec(d)] * 4 + [_vec_spec(1, d)] * 3 + [_row_spec(d)],
        out_specs=(_row_spec(d), _row_spec(d), _row_spec(d), _row_spec(d), _vec_spec(8, d)),
        out_shape=(jax.ShapeDtypeStruct((t, d), f32), jax.ShapeDtypeStruct((t, d), bf16), jax.ShapeDtypeStruct((t, d), bf16),
                   jax.ShapeDtypeStruct((t, d), bf16), jax.ShapeDtypeStruct((8, d), f32)),
        compiler_params=_params(("arbitrary",)),
    )(h1, ff, pe, gp, b_gate, g, b, target)


def _ln1_bwd(dr2, da, db, xhat, rstd, g):
    t, d = dr2.shape

    def body(dr2_ref, da_ref, db_ref, xh_ref, rs_ref, g_ref, dr_ref, drb_ref, pg_ref):
        i = pl.program_id(0)

        @pl.when(i == 0)
        def _():
            pg_ref[...] = jnp.zeros_like(pg_ref)

        dh = ALPHA * dr2_ref[...] + da_ref[...] + db_ref[...]
        xhat = xh_ref[...]
        dr = _ln_bwd(dh, xhat, rs_ref[:, 0:1], g_ref[...])
        dr_ref[...] = dr
        drb_ref[...] = dr.astype(bf16)
        pg_ref[0:1, :] += jnp.sum(dh * xhat, 0, keepdims=True)
        pg_ref[1:2, :] += jnp.sum(dh, 0, keepdims=True)

    return pl.pallas_call(
        body, name="ln1_bwd", grid=(t // TOK_BLK,),
        in_specs=[_row_spec(d)] * 4 + [_row_spec(LANES), _vec_spec(1, d)],
        out_specs=(_row_spec(d), _row_spec(d), _vec_spec(8, d)),
        out_shape=(jax.ShapeDtypeStruct((t, d), f32), jax.ShapeDtypeStruct((t, d), bf16), jax.ShapeDtypeStruct((8, d), f32)),
        compiler_params=_params(("arbitrary",)),
    )(dr2, da, db, xhat, rstd, g)


def _ln_in_bwd(x, dr1, dmm, g):
    t, d = x.shape

    def body(x_ref, dr1_ref, dmm_ref, g_ref, dx_ref, pg_ref):
        i = pl.program_id(0)

        @pl.when(i == 0)
        def _():
            pg_ref[...] = jnp.zeros_like(pg_ref)

        dh = ALPHA * dr1_ref[...] + dmm_ref[...]
        _, xhat, rstd = _ln_fwd(x_ref[...], g_ref[...], 0.0)
        dx_ref[...] = _ln_bwd(dh, xhat, rstd, g_ref[...])
        pg_ref[0:1, :] += jnp.sum(dh * xhat, 0, keepdims=True)
        pg_ref[1:2, :] += jnp.sum(dh, 0, keepdims=True)

    return pl.pallas_call(
        body, name="ln_in_bwd", grid=(t // TOK_BLK,),
        in_specs=[_row_spec(d)] * 3 + [_vec_spec(1, d)],
        out_specs=(_row_spec(d), _vec_spec(8, d)),
        out_shape=(jax.ShapeDtypeStruct((t, d), f32), jax.ShapeDtypeStruct((8, d), f32)),
        compiler_params=_params(("arbitrary",)),
    )(x, dr1, dmm, g)


def _tri(n, upper=False, strict=False):
    r = lax.broadcasted_iota(jnp.int32, (n, n), 0)
    c = lax.broadcasted_iota(jnp.int32, (n, n), 1)
    if upper:
        m = (c > r) if strict else (c >= r)
    else:
        m = (c < r) if strict else (c <= r)
    return jnp.where(m, 1.0, 0.0).astype(f32)


def _gate_values(x, bias, alog, lane):
    z = x + bias
    return jnp.where(lane < 4, _sigmoid(z), jnp.where(lane < 8, -jnp.exp(alog) * _softplus(z), jnp.where(lane < 16, -_softplus(-z), 0.0)))


def _gates(proj, bias_row, alog_row):
    t = proj.shape[0]
    nch = t // CHUNK

    def body(x_ref, bias_ref, alog_ref, gates_ref, gcum_ref, gcumt_ref):
        lane = _lane((t, LANES))
        gates = _gate_values(x_ref[...], bias_ref[...], alog_ref[...], lane)
        gates_ref[...] = gates
        g3 = gates.reshape(nch, CHUNK, LANES)
        tri = jnp.broadcast_to(_tri(CHUNK)[None], (nch, CHUNK, CHUNK))
        loc = jnp.einsum("bij,bjk->bik", tri, g3, precision=HI, preferred_element_type=f32)
        tot = jnp.sum(g3, axis=1)
        offs = _dot(_tri(nch, strict=True), tot)
        glob = loc + offs[:, None, :]
        lane3 = _lane((nch, CHUNK, LANES))
        gcum = jnp.where(lane3 < 4, g3, jnp.where(lane3 < 8, loc, glob)).reshape(t, LANES)
        gcum_ref[...] = gcum
        gcumt_ref[...] = gcum.T

    return pl.pallas_call(
        body, name="gates", grid=(1,),
        in_specs=[pl.BlockSpec((t, LANES), lambda i: (0, SEG_SMALL // LANES)), _vec_spec(1, LANES), _vec_spec(1, LANES)],
        out_specs=(pl.BlockSpec((t, LANES), lambda i: (0, 0)), pl.BlockSpec((t, LANES), lambda i: (0, 0)),
                   pl.BlockSpec((LANES, t), lambda i: (0, 0))),
        out_shape=(jax.ShapeDtypeStruct((t, LANES), f32), jax.ShapeDtypeStruct((t, LANES), f32), jax.ShapeDtypeStruct((LANES, t), f32)),
        compiler_params=_params(("arbitrary",)),
    )(proj, bias_row, alog_row)


def _gates_bwd(proj, bias_row, alog_row, gates, dgates, dccol, dct):
    t = proj.shape[0]
    nch = t // CHUNK

    def body(x_ref, bias_ref, alog_ref, gates_ref, dg_ref, dcc_ref, dct_ref, dx_ref, pg_ref):
        lane = _lane((t, LANES))
        d = dg_ref[...] + dcc_ref[...] + dct_ref[...].T
        d3 = d.reshape(nch, CHUNK, LANES)
        tri = jnp.broadcast_to(_tri(CHUNK, upper=True)[None], (nch, CHUNK, CHUNK))
        loc = jnp.einsum("bij,bjk->bik", tri, d3, precision=HI, preferred_element_type=f32)
        tot = jnp.sum(d3, axis=1)
        offs = _dot(_tri(nch, upper=True, strict=True), tot)
        glob = loc + offs[:, None, :]
        lane3 = _lane((nch, CHUNK, LANES))
        dpre = jnp.where(lane3 < 4, d3, jnp.where(lane3 < 8, loc, glob)).reshape(t, LANES)
        z = x_ref[...] + bias_ref[...]
        sg = _sigmoid(z)
        dx = jnp.where(lane < 4, dpre * sg * (1.0 - sg),
                       jnp.where(lane < 8, dpre * (-jnp.exp(alog_ref[...])) * sg, jnp.where(lane < 16, dpre * (1.0 - sg), 0.0)))
        dx_ref[...] = dx.astype(bf16)
        pg_ref[...] = jnp.zeros_like(pg_ref)
        pg_ref[0:1, :] = jnp.sum(dx, 0, keepdims=True)
        pg_ref[1:2, :] = jnp.sum(jnp.where((lane >= 4) & (lane < 8), dpre * gates_ref[...], 0.0), 0, keepdims=True)

    full = pl.BlockSpec((t, LANES), lambda i: (0, 0))
    return pl.pallas_call(
        body, name="gates_bwd", grid=(1,),
        in_specs=[pl.BlockSpec((t, LANES), lambda i: (0, SEG_SMALL // LANES)), _vec_spec(1, LANES), _vec_spec(1, LANES),
                  full, full, full, pl.BlockSpec((LANES, t), lambda i: (0, 0))],
        out_specs=(full, _vec_spec(8, LANES)),
        out_shape=(jax.ShapeDtypeStruct((t, LANES), bf16), jax.ShapeDtypeStruct((8, LANES), f32)),
        compiler_params=_params(("arbitrary",)),
    )(proj, bias_row, alog_row, gates, dgates, dccol, dct)


def _conv_act(u, cw, row, t):
    c = cw[3:4, :] * u
    for jj in range(CONV_W - 1):
        sh = CONV_W - 1 - jj
        c = c + cw[jj:jj + 1, :] * jnp.where(row >= sh, pltpu.roll(u, sh, axis=0), 0.0)
    return c


def _gdn_conv(proj, conv_w):
    t = proj.shape[0]
    nblk = GDN_QKV // LANES

    def body(u_ref, cw_ref, c_ref, y_ref):
        j = pl.program_id(0)
        row = lax.broadcasted_iota(jnp.int32, (t, LANES), 0)
        c = _conv_act(u_ref[...], cw_ref[...], row, t)
        c_ref[...] = c
        s = c * _sigmoid(c)
        r = lax.rsqrt(jnp.sum(s * s, -1, keepdims=True) + NORM_EPS)
        scale = jnp.where(j < GDN_HEADS, GDN_DK ** -0.5, 1.0)
        y_ref[...] = jnp.where(j < 2 * GDN_HEADS, s * (r * scale), s)

    blk = pl.BlockSpec((t, LANES), lambda j: (0, j))
    return pl.pallas_call(
        body, name="gdn_conv", grid=(nblk,),
        in_specs=[blk, pl.BlockSpec((CONV_W, LANES), lambda j: (0, j))],
        out_specs=(blk, blk),
        out_shape=(jax.ShapeDtypeStruct((t, GDN_QKV), f32), jax.ShapeDtypeStruct((t, GDN_QKV), f32)),
        compiler_params=_params(("parallel",)),
    )(proj, conv_w)


def _gdn_conv_bwd(proj, conv_w, c, dy):
    t = proj.shape[0]
    nblk = GDN_QKV // LANES

    def body(u_ref, cw_ref, c_ref, dy_ref, du_ref, dcw_ref):
        j = pl.program_id(0)
        row = lax.broadcasted_iota(jnp.int32, (t, LANES), 0)
        u = u_ref[...]
        cw = cw_ref[...]
        c = c_ref[...]
        dy = dy_ref[...]
        sg = _sigmoid(c)
        s = c * sg
        r = lax.rsqrt(jnp.sum(s * s, -1, keepdims=True) + NORM_EPS)
        n = s * r
        scale = jnp.where(j < GDN_HEADS, GDN_DK ** -0.5, 1.0)
        dn = dy * scale
        ds = jnp.where(j < 2 * GDN_HEADS, r * (dn - n * jnp.sum(dn * n, -1, keepdims=True)), dy)
        dc = ds * (sg * (1.0 + c * (1.0 - sg)))
        du = cw[3:4, :] * dc
        dcw_ref[...] = jnp.zeros_like(dcw_ref)
        dcw_ref[3:4, :] = jnp.sum(dc * u, 0, keepdims=True)
        for jj in range(CONV_W - 1):
            sh = CONV_W - 1 - jj
            du = du + cw[jj:jj + 1, :] * jnp.where(row < t - sh, pltpu.roll(dc, t - sh, axis=0), 0.0)
            dcw_ref[jj:jj + 1, :] = jnp.sum(dc * jnp.where(row >= sh, pltpu.roll(u, sh, axis=0), 0.0), 0, keepdims=True)
        du_ref[...] = du.astype(bf16)

    blk = pl.BlockSpec((t, LANES), lambda j: (0, j))
    return pl.pallas_call(
        body, name="gdn_conv_bwd", grid=(nblk,),
        in_specs=[blk, pl.BlockSpec((CONV_W, LANES), lambda j: (0, j)), blk, blk],
        out_specs=(blk, pl.BlockSpec((8, LANES), lambda j: (0, j))),
        out_shape=(jax.ShapeDtypeStruct((t, GDN_QKV), bf16), jax.ShapeDtypeStruct((8, GDN_QKV), f32)),
        compiler_params=_params(("parallel",)),
    )(proj, conv_w, c, dy)


def _chunk_masks():
    r = lax.broadcasted_iota(jnp.int32, (CHUNK, CHUNK), 0)
    c = lax.broadcasted_iota(jnp.int32, (CHUNK, CHUNK), 1)
    return r >= c, r > c, r == c


def _col_to_row(col, eye):
    return jnp.sum(jnp.where(eye, col, 0.0), axis=0, keepdims=True)


def _row_to_col(row, eye):
    return jnp.sum(jnp.where(eye, row, 0.0), axis=1, keepdims=True)


NN = (((1,), (0,)), ((), ()))
NT = (((1,), (1,)), ((), ()))
TN = (((0,), (0,)), ((), ()))
GDN_GROUP = 4


def _mx(a, b, dims=NN, passes=1):
    d = lambda p, q: lax.dot_general(p, q, dims, preferred_element_type=f32)
    ah, bh = a.astype(bf16), b.astype(bf16)
    if passes == 1:
        return d(ah, bh)
    al = (a - ah.astype(f32)).astype(bf16)
    bl = (b - bh.astype(f32)).astype(bf16)
    return d(ah, bh) + (d(ah, bl) + d(al, bh))


def _gdn_decay(gam, masks):
    causal, _, eye = masks
    return jnp.exp(jnp.where(causal, gam - _col_to_row(gam, eye), NEG))


def _gdn_local(y, gcum):
    t = y.shape[0]
    nch = t // CHUNK
    rows_blk = GDN_GROUP * CHUNK

    def body(y_ref, g_ref, u_ref, w_ref, qk_ref, tinv_ref):
        masks = _chunk_masks()
        _, strict, eye = masks
        for j in range(GDN_GROUP):
            rs = slice(j * CHUNK, (j + 1) * CHUNK)
            for h in range(GDN_HEADS):
                qn = y_ref[rs, h * LANES:(h + 1) * LANES]
                kn = y_ref[rs, 512 + h * LANES:512 + (h + 1) * LANES]
                v = y_ref[rs, 1024 + h * LANES:1024 + (h + 1) * LANES]
                beta = g_ref[rs, h:h + 1]
                gam = g_ref[rs, 4 + h:5 + h]
                dec = _gdn_decay(gam, masks)
                x = -jnp.where(strict, _mx(kn, kn, NT) * dec * beta, 0.0)
                tinv = jnp.where(eye, 1.0, 0.0) + x
                for _ in range(5):
                    x = _mx(x, x, NN, 3)
                    tinv = tinv + _mx(tinv, x, NN, 3)
                e = jnp.exp(gam)
                u_ref[rs, h * LANES:(h + 1) * LANES] = _mx(tinv, beta * v)
                w_ref[rs, h * LANES:(h + 1) * LANES] = _mx(tinv, (beta * e) * kn)
                qk_ref[j, h] = _mx(qn, kn, NT) * dec
                tinv_ref[j, h] = tinv

    mat = pl.BlockSpec((GDN_GROUP, GDN_HEADS, CHUNK, CHUNK), lambda n: (n, 0, 0, 0))
    return pl.pallas_call(
        body, name="gdn_local", grid=(nch // GDN_GROUP,),
        in_specs=[pl.BlockSpec((rows_blk, GDN_QKV), lambda n: (n, 0)), pl.BlockSpec((rows_blk, LANES), lambda n: (n, 0))],
        out_specs=(pl.BlockSpec((rows_blk, 512), lambda n: (n, 0)), pl.BlockSpec((rows_blk, 512), lambda n: (n, 0)), mat, mat),
        out_shape=(jax.ShapeDtypeStruct((t, 512), f32), jax.ShapeDtypeStruct((t, 512), f32),
                   jax.ShapeDtypeStruct((nch, GDN_HEADS, CHUNK, CHUNK), f32), jax.ShapeDtypeStruct((nch, GDN_HEADS, CHUNK, CHUNK), f32)),
        compiler_params=_params(("parallel",)),
    )(y, gcum)


def _gdn_fwd(y, gcum, u, w, qk):
    t = y.shape[0]
    nch = t // CHUNK

    def body(y_ref, g_ref, u_ref, w_ref, qk_ref, o_ref, sall_ref, s_ref):
        @pl.when(pl.program_id(0) == 0)
        def _():
            s_ref[...] = jnp.zeros_like(s_ref)

        for h in range(GDN_HEADS):
            sl = slice(h * LANES, (h + 1) * LANES)
            gam = g_ref[:, 4 + h:5 + h]
            gam_last = gam[CHUNK - 1:CHUNK, :]
            s = s_ref[h]
            sall_ref[0, h] = s
            vn = u_ref[:, sl] - _mx(w_ref[:, sl], s)
            o_ref[:, sl] = _mx(y_ref[:, sl] * jnp.exp(gam), s) + _mx(qk_ref[0, h], vn)
            kd = y_ref[:, 512 + h * LANES:512 + (h + 1) * LANES] * jnp.exp(gam_last - gam)
            s_ref[h] = jnp.exp(gam_last) * s + _mx(kd, vn, TN)

    row = lambda width: pl.BlockSpec((CHUNK, width), lambda n: (n, 0))
    return pl.pallas_call(
        body, name="gdn_fwd", grid=(nch,),
        in_specs=[row(GDN_QKV), row(LANES), row(512), row(512), pl.BlockSpec((1, GDN_HEADS, CHUNK, CHUNK), lambda n: (n, 0, 0, 0))],
        out_specs=(row(512), pl.BlockSpec((1, GDN_HEADS, LANES, LANES), lambda n: (n, 0, 0, 0))),
        out_shape=(jax.ShapeDtypeStruct((t, 512), f32), jax.ShapeDtypeStruct((nch, GDN_HEADS, LANES, LANES), f32)),
        scratch_shapes=[pltpu.VMEM((GDN_HEADS, LANES, LANES), f32)],
        compiler_params=_params(("arbitrary",)),
    )(y, gcum, u, w, qk)


def _gdn_bwd(y, gcum, u_all, w_all, qk_all, tinv_all, sall, do):
    t = y.shape[0]
    nch = t // CHUNK

    def body(y_ref, g_ref, u_ref, w_ref, qk_ref, tinv_ref, sall_ref, do_ref, dy_ref, dg_ref, ds_ref):
        @pl.when(pl.program_id(0) == 0)
        def _():
            ds_ref[...] = jnp.zeros_like(ds_ref)

        masks = _chunk_masks()
        causal, strict, eye = masks
        lane = _lane((CHUNK, LANES))
        row = lax.broadcasted_iota(jnp.int32, (CHUNK, 1), 0)
        dgates = jnp.zeros((CHUNK, LANES), f32)
        for h in range(GDN_HEADS):
            sl = slice(h * LANES, (h + 1) * LANES)
            qn = y_ref[:, sl]
            kn = y_ref[:, 512 + h * LANES:512 + (h + 1) * LANES]
            v = y_ref[:, 1024 + h * LANES:1024 + (h + 1) * LANES]
            beta = g_ref[:, h:h + 1]
            gam = g_ref[:, 4 + h:5 + h]
            gam_last = gam[CHUNK - 1:CHUNK, :]
            dec = _gdn_decay(gam, masks)
            e = jnp.exp(gam)
            f = jnp.exp(gam_last - gam)
            gl = jnp.exp(gam_last)
            kkd = _mx(kn, kn, NT) * dec
            u, w, qk, tinv = u_ref[:, sl], w_ref[:, sl], qk_ref[0, h], tinv_ref[0, h]
            qd, kd = qn * e, kn * f
            s = sall_ref[0, h]
            dsn = ds_ref[h]
            d_o = do_ref[:, sl]
            vn = u - _mx(w, s)
            dvn = _mx(qk, d_o, TN) + _mx(kd, dsn)
            dqk = jnp.where(causal, _mx(d_o, vn, NT), 0.0)
            dqd = _mx(d_o, s, NT)
            dkd = _mx(vn, dsn, NT)
            dgl = jnp.sum(jnp.sum(dsn * s, axis=1, keepdims=True), axis=0, keepdims=True)
            dw = -_mx(dvn, s, NT)
            ds_ref[h] = _mx(qd, d_o, TN) - _mx(w, dvn, TN) + gl * dsn
            dru = _mx(tinv, dvn, TN)
            drw = _mx(tinv, dw, TN)
            dn = jnp.where(strict, -(_mx(dru, u, NT) + _mx(drw, w, NT)), 0.0)
            dv = beta * dru
            drw_k = jnp.sum(drw * kn, axis=1, keepdims=True)
            dbeta = jnp.sum(dru * v, axis=1, keepdims=True) + e * drw_k + jnp.sum(dn * kkd, axis=1, keepdims=True)
            de = beta * drw_k
            dk = (beta * e) * drw
            dkk = dn * beta * dec
            dk = dk + _mx(dkk, kn) + _mx(dkk, kn, TN)
            dqkr = dqk * dec
            dq = _mx(dqkr, kn) + dqd * e
            dk = dk + _mx(dqkr, qn, TN) + dkd * f
            m = dn * (kkd * beta) + dqk * qk
            dgam = jnp.sum(m, axis=1, keepdims=True) - _row_to_col(jnp.sum(m, axis=0, keepdims=True), eye)
            de = de + jnp.sum(dqd * qn, axis=1, keepdims=True)
            df = jnp.sum(dkd * kn, axis=1, keepdims=True)
            dgam = dgam + de * e - df * f
            dgam_last = jnp.sum(df * f, axis=0, keepdims=True) + dgl * gl
            dgam = dgam + jnp.where(row == CHUNK - 1, dgam_last, 0.0)
            dy_ref[:, sl] = dq
            dy_ref[:, 512 + h * LANES:512 + (h + 1) * LANES] = dk
            dy_ref[:, 1024 + h * LANES:1024 + (h + 1) * LANES] = dv
            dgates = dgates + jnp.where(lane == h, dbeta, 0.0) + jnp.where(lane == 4 + h, dgam, 0.0)
        dg_ref[...] = dgates

    rev = lambda width: pl.BlockSpec((CHUNK, width), lambda n: (nch - 1 - n, 0))
    mat = lambda d: pl.BlockSpec((1, GDN_HEADS, d, d), lambda n: (nch - 1 - n, 0, 0, 0))
    return pl.pallas_call(
        body, name="gdn_bwd", grid=(nch,),
        in_specs=[rev(GDN_QKV), rev(LANES), rev(512), rev(512), mat(CHUNK), mat(CHUNK), mat(LANES), rev(512)],
        out_specs=(rev(GDN_QKV), rev(LANES)),
        out_shape=(jax.ShapeDtypeStruct((t, GDN_QKV), f32), jax.ShapeDtypeStruct((t, LANES), f32)),
        scratch_shapes=[pltpu.VMEM((GDN_HEADS, LANES, LANES), f32)],
        compiler_params=_params(("arbitrary",)),
    )(y, gcum, u_all, w_all, qk_all, tinv_all, sall, do)


def _fox_scores(q_ref, k_ref, gcum_ref, gcumt_ref, h, i, t):
    pr = h // 2
    lo = (h % 2) * FOX_DH
    lane = _lane((FOX_BQ, LANES))
    mask = (lane >= lo) & (lane < lo + FOX_DH)
    qm = jnp.where(mask, q_ref[:, pr * LANES:(pr + 1) * LANES], 0.0).astype(bf16)
    kp = k_ref[:, pr * LANES:(pr + 1) * LANES].astype(bf16)
    s = _dot_nt(qm, kp, None) * (FOX_DH ** -0.5)
    s = s + gcum_ref[:, 8 + h:9 + h] - gcumt_ref[8 + h:9 + h, :]
    rows = i * FOX_BQ + lax.broadcasted_iota(jnp.int32, (FOX_BQ, t), 0)
    cols = lax.broadcasted_iota(jnp.int32, (FOX_BQ, t), 1)
    return jnp.where(cols <= rows, s, NEG), mask, qm, kp


def _fox_fwd(proj, gcum, gcumt):
    t = proj.shape[0]

    def body(q_ref, k_ref, v_ref, gcum_ref, gcumt_ref, o_ref, lse_ref):
        i = pl.program_id(0)
        lane = _lane((FOX_BQ, LANES))
        lse_all = jnp.zeros((FOX_BQ, LANES), f32)
        for pr in range(FOX_HEADS // 2):
            vp = v_ref[:, pr * LANES:(pr + 1) * LANES].astype(bf16)
            o_pair = jnp.zeros((FOX_BQ, LANES), f32)
            for h in (2 * pr, 2 * pr + 1):
                s, mask, _, _ = _fox_scores(q_ref, k_ref, gcum_ref, gcumt_ref, h, i, t)
                m = jnp.max(s, axis=1, keepdims=True)
                p = jnp.exp(s - m)
                l = jnp.sum(p, axis=1, keepdims=True)
                o_h = _dot((p * (1.0 / l)).astype(bf16), vp, None)
                o_pair = jnp.where(mask, o_h, o_pair)
                lse_all = jnp.where(lane == h, m + jnp.log(l), lse_all)
            o_ref[:, pr * LANES:(pr + 1) * LANES] = o_pair
        lse_ref[...] = lse_all

    qblk = lambda col: pl.BlockSpec((FOX_BQ, 512), lambda i: (i, col))
    full = lambda col: pl.BlockSpec((t, 512), lambda i: (0, col))
    c0 = SEG_FOX // 512
    return pl.pallas_call(
        body, name="fox_fwd", grid=(t // FOX_BQ,),
        in_specs=[qblk(c0), full(c0 + 1), full(c0 + 2), pl.BlockSpec((FOX_BQ, LANES), lambda i: (i, 0)),
                  pl.BlockSpec((LANES, t), lambda i: (0, 0))],
        out_specs=(pl.BlockSpec((FOX_BQ, 512), lambda i: (i, 0)), pl.BlockSpec((FOX_BQ, LANES), lambda i: (i, 0))),
        out_shape=(jax.ShapeDtypeStruct((t, 512), f32), jax.ShapeDtypeStruct((t, LANES), f32)),
        compiler_params=_params(("parallel",)),
    )(proj, proj, proj, gcum, gcumt)


def _fox_bwd(proj, gcum, gcumt, o, lse, do):
    t = proj.shape[0]

    def body(q_ref, k_ref, v_ref, gcum_ref, gcumt_ref, o_ref, lse_ref, do_ref, dq_ref, dk_ref, dv_ref, dcc_ref, dct_ref):
        i = pl.program_id(0)

        @pl.when(i == 0)
        def _():
            dk_ref[...] = jnp.zeros_like(dk_ref)
            dv_ref[...] = jnp.zeros_like(dv_ref)
            dct_ref[...] = jnp.zeros_like(dct_ref)

        lane = _lane((FOX_BQ, LANES))
        dcc = jnp.zeros((FOX_BQ, LANES), f32)
        scale = FOX_DH ** -0.5
        for pr in range(FOX_HEADS // 2):
            sl = slice(pr * LANES, (pr + 1) * LANES)
            vp = v_ref[:, sl].astype(bf16)
            dq_pair = jnp.zeros((FOX_BQ, LANES), f32)
            for h in (2 * pr, 2 * pr + 1):
                s, mask, qm, kp = _fox_scores(q_ref, k_ref, gcum_ref, gcumt_ref, h, i, t)
                p = jnp.exp(s - lse_ref[:, h:h + 1])
                dom = jnp.where(mask, do_ref[:, sl], 0.0)
                delta = jnp.sum(dom * o_ref[:, sl], axis=1, keepdims=True)
                domb = dom.astype(bf16)
                dp = _dot_nt(domb, vp, None)
                ds = p * (dp - delta)
                dsb = ds.astype(bf16)
                dv_ref[:, sl] += _dot_tn(p.astype(bf16), domb, None)
                dk_ref[:, sl] += _dot_tn(dsb, qm, None) * scale
                dq_pair = jnp.where(mask, _dot(dsb, kp, None) * scale, dq_pair)
                dcc = jnp.where(lane == 8 + h, jnp.sum(ds, axis=1, keepdims=True), dcc)
                dct_ref[8 + h:9 + h, :] += -jnp.sum(ds, axis=0, keepdims=True)
            dq_ref[:, sl] = dq_pair.astype(bf16)
        dcc_ref[...] = dcc

    qblk = lambda col: pl.BlockSpec((FOX_BQ, 512), lambda i: (i, col))
    full = lambda col: pl.BlockSpec((t, 512), lambda i: (0, col))
    rblk = pl.BlockSpec((FOX_BQ, LANES), lambda i: (i, 0))
    c0 = SEG_FOX // 512
    return pl.pallas_call(
        body, name="fox_bwd", grid=(t // FOX_BQ,),
        in_specs=[qblk(c0), full(c0 + 1), full(c0 + 2), rblk, pl.BlockSpec((LANES, t), lambda i: (0, 0)),
                  qblk(0), rblk, qblk(0)],
        out_specs=(qblk(0), full(0), full(0), rblk, pl.BlockSpec((LANES, t), lambda i: (0, 0))),
        out_shape=(jax.ShapeDtypeStruct((t, 512), bf16), jax.ShapeDtypeStruct((t, 512), f32), jax.ShapeDtypeStruct((t, 512), f32),
                   jax.ShapeDtypeStruct((t, LANES), f32), jax.ShapeDtypeStruct((LANES, t), f32)),
        compiler_params=_params(("arbitrary",)),
    )(proj, proj, proj, gcum, gcumt, o, lse, do)


def _row(v, width=None):
    v = v.reshape(1, -1).astype(f32)
    if width is not None and v.shape[1] < width:
        v = jnp.pad(v, ((0, 0), (0, width - v.shape[1])))
    return v


def _device_grads(x, p, target, small, w_cat, conv_w, w_out, w_up, w_down, w_ple, w_gate):
    z4 = jnp.zeros((4,), f32)
    bias_row = _row(jnp.concatenate([z4, small["dt_bias"].reshape(-1), small["b_f"].reshape(-1)]), LANES)
    alog_row = _row(jnp.concatenate([z4, small["a_log"].reshape(-1)]), LANES)
    g_gdn = _row(small["gdn_norm_g"])
    g_fox2 = _row(jnp.tile(small["fox_norm_g"].reshape(-1), 2))
    pb = p.astype(bf16)

    h0, h0b = _ln_in(x, _row(small["ln_in_g"]), _row(small["ln_in_b"]))
    proj = _mm(h0b, w_cat, "nn", 256, D_CAT, "mm_proj")
    gates, gcum, gcumt = _gates(proj, bias_row, alog_row)
    conv_c, qkv_n = _gdn_conv(proj, conv_w)
    gu, gw, gqk, gtinv = _gdn_local(qkv_n, gcum)
    o_gdn, sall = _gdn_fwd(qkv_n, gcum, gu, gw, gqk)
    o_fox, lse = _fox_fwd(proj, gcum, gcumt)
    attn = _attn_post(o_gdn, proj, o_fox, g_gdn, g_fox2)
    mix = _mm(attn, w_out, "nn", 512, D_MODEL, "mm_mix")
    h1, h1b, xhat1, rstd1 = _ln1(h0, mix, _row(small["ln1_g"]), _row(small["ln1_b"]))
    up, act = _mm(h1b, w_up, "nn", 256, 2048, "mm_up", epi="relu2")
    ff = _mm(act, w_down, "nn", 256, D_MODEL, "mm_down")
    gp = _mm(h1b, w_gate, "nn", 512, D_MODEL, "mm_gate")
    pe = _mm(pb, w_ple, "nn", 512, D_MODEL, "mm_ple")
    dr2, dr2b, dpe, dgp, pg2 = _ln2_loss(h1, ff, pe, gp, _row(small["b_ple_gate"]), _row(small["ln2_g"]), _row(small["ln2_b"]), target)

    dup = _mm(dr2b, w_down, "nt", 256, 2048, "mm_dact", epi="relu2_bwd", extra=up)
    g_down = _mm(act, dr2b, "tn", 1024, D_MODEL, "mm_gdown")
    dh1_a = _mm(dup, w_up, "nt", 256, D_MODEL, "mm_dh1a")
    g_up = _mm(h1b, dup, "tn", 1024, 1024, "mm_gup")
    dh1_b = _mm(dgp, w_gate, "nt", 512, D_MODEL, "mm_dh1b")
    g_gate = _mm(h1b, dgp, "tn", 1024, D_MODEL, "mm_ggate")
    g_ple = _mm(pb, dpe, "tn", D_PLE, D_MODEL, "mm_gple")
    dr1, dr1b, pg1 = _ln1_bwd(dr2, dh1_a, dh1_b, xhat1, rstd1, _row(small["ln1_g"]))
    dattn = _mm(dr1b, w_out, "nt", 512, D_MODEL, "mm_dattn")
    g_out = _mm(attn, dr1b, "tn", 1024, D_MODEL, "mm_gout")
    do_gdn, dz, do_fox, pga = _attn_post_bwd(dattn, o_gdn, proj, o_fox, g_gdn, g_fox2)
    dfq, dfk, dfv, dccol, dct = _fox_bwd(proj, gcum, gcumt, o_fox, lse, do_fox)
    dqkv_n, dgates = _gdn_bwd(qkv_n, gcum, gu, gw, gqk, gtinv, sall, do_gdn)
    dsmall, pgg = _gates_bwd(proj, bias_row, alog_row, gates, dgates, dccol, dct)
    du, g_conv8 = _gdn_conv_bwd(proj, conv_w, conv_c, dqkv_n)
    t = x.shape[0]
    dproj = jnp.concatenate([du, dz, dfq, dfk.astype(bf16), dfv.astype(bf16), dsmall, jnp.zeros((t, D_CAT - SEG_SMALL - LANES), bf16)], axis=1)
    dh0_mm = _mm(dproj, w_cat, "nt", 256, D_MODEL, "mm_dh0")
    g_cat = _mm(h0b, dproj, "tn", 1024, 1280, "mm_gcat")
    grad_x, pg0 = _ln_in_bwd(x, dr1, dh0_mm, _row(small["ln_in_g"]))

    g_fox = pga[1, :FOX_DH] + pga[1, FOX_DH:]
    small_grads = dict(
        ln_in_g=pg0[0], ln_in_b=pg0[1], ln1_g=pg1[0], ln1_b=pg1[1], b_ple_gate=pg2[2], ln2_g=pg2[0], ln2_b=pg2[1],
        gdn_norm_g=pga[0], fox_norm_g=g_fox, a_log=pgg[1, 4:8], dt_bias=pgg[0, 4:8], b_f=pgg[0, 8:16], loss=pg2[3, 0:1])
    big_grads = dict(w_cat=g_cat, conv_w=g_conv8[:CONV_W], w_out=g_out, w_up=g_up, w_down=g_down, w_ple=g_ple, w_gate=g_gate)
    return grad_x, big_grads, small_grads


ANY = pl.BlockSpec(memory_space=pl.ANY)
HALF_ROWS = FLAT_ROWS // 2
CONV_PKT_ROWS = 16


def _mesh_pos():
    return lax.axis_index("x"), lax.axis_index("y"), lax.axis_index("c")


def _other_chips(x, y):
    return [(1 - x, y), (x, 1 - y), (1 - x, 1 - y)]


def _rcopy(src, dst, send_sem, recv_sem, dev):
    return pltpu.make_async_remote_copy(src_ref=src, dst_ref=dst, send_sem=send_sem, recv_sem=recv_sem,
                                        device_id=dev, device_id_type=MESH)


def _gather_weights(buf4, conv4):
    cr = HALF_ROWS // AG_CHUNKS
    n_ici = 3 * AG_CHUNKS

    def body(buf_in, conv_in, out_ref, conv_out_ref, send_sems, recv_sems, csend, crecv):
        del buf_in, conv_in
        x, y, c = _mesh_pos()
        q = 2 * x + y
        chips = _other_chips(x, y)
        sib = (x, y, 1 - c)

        def rows(hf, ch):
            return pl.ds(hf * HALF_ROWS + ch * cr, cr)

        started = []
        for ch in range(AG_CHUNKS):
            for k, chip in enumerate(chips):
                mine = out_ref.at[q, rows(c, ch)]
                cp = _rcopy(mine, mine, send_sems.at[k * AG_CHUNKS + ch], recv_sems.at[k * AG_CHUNKS + ch], (*chip, c))
                cp.start()
                started.append(cp)
        for k, chip in enumerate(chips):
            cp = _rcopy(conv_out_ref.at[q], conv_out_ref.at[q], csend.at[k], crecv.at[k], (*chip, c))
            cp.start()
            started.append(cp)
        for ch in range(AG_CHUNKS):
            for k, chip in enumerate(chips):
                landed = out_ref.at[2 * chip[0] + chip[1], rows(c, ch)]
                _rcopy(landed, landed, send_sems.at[k * AG_CHUNKS + ch], recv_sems.at[k * AG_CHUNKS + ch], (*chip, c)).wait_recv()
                cp = _rcopy(landed, landed, send_sems.at[n_ici + k * AG_CHUNKS + ch], recv_sems.at[n_ici + k * AG_CHUNKS + ch], sib)
                cp.start()
                started.append(cp)
        for ch in range(AG_CHUNKS):
            for k, chip in enumerate(chips):
                passed = out_ref.at[2 * chip[0] + chip[1], rows(1 - c, ch)]
                _rcopy(passed, passed, send_sems.at[n_ici + k * AG_CHUNKS + ch], recv_sems.at[n_ici + k * AG_CHUNKS + ch], sib).wait_recv()
        for k, chip in enumerate(chips):
            theirs = conv_out_ref.at[2 * chip[0] + chip[1]]
            _rcopy(theirs, theirs, csend.at[k], crecv.at[k], (*chip, c)).wait_recv()
        for cp in started:
            cp.wait_send()

    return pl.pallas_call(
        body, name="gather_weights",
        out_shape=(jax.ShapeDtypeStruct(buf4.shape, buf4.dtype), jax.ShapeDtypeStruct(conv4.shape, conv4.dtype)),
        in_specs=[ANY, ANY], out_specs=(ANY, ANY), input_output_aliases={0: 0, 1: 1},
        scratch_shapes=[pltpu.SemaphoreType.DMA((2 * n_ici,)), pltpu.SemaphoreType.DMA((2 * n_ici,)), pltpu.SemaphoreType.DMA((3,)),
                        pltpu.SemaphoreType.DMA((3,))],
    )(buf4, conv4)


def _exchange_pairs(g4, small):
    def body(g_ref, small_ref, b1_ref, all_ref, ssem, rsem, s2, r2, local_sem):
        x, y, c = _mesh_pos()
        me = 4 * x + 2 * y + c
        sib = (x, y, 1 - c)
        own = pltpu.make_async_copy(small_ref, all_ref.at[me], local_sem)
        own.start()
        started = []
        for d in range(N_CHIPS):
            cp = _rcopy(g_ref.at[d, pl.ds((1 - c) * HALF_ROWS, HALF_ROWS)], b1_ref.at[d], ssem.at[d], rsem.at[d], sib)
            cp.start()
            started.append(cp)
        peers = []
        for r in range(1, 8):
            fx, fy, fc = (r >> 2) & 1, (r >> 1) & 1, r & 1
            peers.append((1 - x if fx else x, 1 - y if fy else y, 1 - c if fc else c))
        for r, peer in enumerate(peers):
            cp = _rcopy(small_ref, all_ref.at[me], s2.at[r], r2.at[r], peer)
            cp.start()
            started.append(cp)
        for d in range(N_CHIPS):
            _rcopy(g_ref.at[d, pl.ds((1 - c) * HALF_ROWS, HALF_ROWS)], b1_ref.at[d], ssem.at[d], rsem.at[d], sib).wait_recv()
        for r, peer in enumerate(peers):
            _rcopy(small_ref, all_ref.at[4 * peer[0] + 2 * peer[1] + peer[2]], s2.at[r], r2.at[r], peer).wait_recv()
        for cp in started:
            cp.wait_send()
        own.wait()

    return pl.pallas_call(
        body, name="exchange_pairs",
        out_shape=(jax.ShapeDtypeStruct((N_CHIPS, HALF_ROWS, g4.shape[2]), g4.dtype), jax.ShapeDtypeStruct((8,) + small.shape, small.dtype)),
        in_specs=[ANY, ANY], out_specs=(ANY, ANY),
        scratch_shapes=[pltpu.SemaphoreType.DMA((N_CHIPS,)), pltpu.SemaphoreType.DMA((N_CHIPS,)), pltpu.SemaphoreType.DMA((7,)),
                        pltpu.SemaphoreType.DMA((7,)), pltpu.SemaphoreType.DMA],
    )(g4, small)


def _exchange_chips(a4):
    def body(a_ref, b2_ref, ssem, rsem):
        x, y, c = _mesh_pos()
        started = []
        for k, chip in enumerate(_other_chips(x, y)):
            cp = _rcopy(a_ref.at[2 * chip[0] + chip[1]], b2_ref.at[k], ssem.at[k], rsem.at[k], (*chip, c))
            cp.start()
            started.append(cp)
        for k, chip in enumerate(_other_chips(x, y)):
            _rcopy(a_ref.at[2 * chip[0] + chip[1]], b2_ref.at[k], ssem.at[k], rsem.at[k], (*chip, c)).wait_recv()
        for cp in started:
            cp.wait_send()

    return pl.pallas_call(
        body, name="exchange_chips", out_shape=jax.ShapeDtypeStruct((3,) + a4.shape[1:], a4.dtype),
        in_specs=[ANY], out_specs=ANY,
        scratch_shapes=[pltpu.SemaphoreType.DMA((3,)), pltpu.SemaphoreType.DMA((3,))],
    )(a4)


def _share_halves(r):
    def body(r_in, out_ref, ssem, rsem):
        del r_in
        x, y, c = _mesh_pos()
        sib = (x, y, 1 - c)
        mine = out_ref.at[pl.ds(c * HALF_ROWS, HALF_ROWS)]
        theirs = out_ref.at[pl.ds((1 - c) * HALF_ROWS, HALF_ROWS)]
        cp = _rcopy(mine, mine, ssem, rsem, sib)
        cp.start()
        _rcopy(theirs, theirs, ssem, rsem, sib).wait_recv()
        cp.wait_send()

    return pl.pallas_call(
        body, name="share_halves", out_shape=jax.ShapeDtypeStruct(r.shape, r.dtype),
        in_specs=[ANY], out_specs=ANY, input_output_aliases={0: 0},
        scratch_shapes=[pltpu.SemaphoreType.DMA, pltpu.SemaphoreType.DMA],
    )(r)


ADD_ROWS = 256


def _add_pair(g4, b1, qc_idx):
    nb = HALF_ROWS // ADD_ROWS
    cols = g4.shape[2]

    def body(qc_ref, g_ref, b_ref, o_ref, ob_ref):
        a = g_ref[...] + b_ref[...]
        o_ref[...] = a
        ob_ref[...] = a.astype(bf16)

    blk = (1, ADD_ROWS, cols)
    out = pl.BlockSpec(blk, lambda d, i, qc: (d, i, 0))
    return pl.pallas_call(
        body, name="add_pair",
        grid_spec=pltpu.PrefetchScalarGridSpec(
            num_scalar_prefetch=1, grid=(N_CHIPS, nb),
            in_specs=[pl.BlockSpec(blk, lambda d, i, qc: (d, qc[1] * nb + i, 0)), out],
            out_specs=(out, out)),
        out_shape=(jax.ShapeDtypeStruct((N_CHIPS, HALF_ROWS, cols), f32), jax.ShapeDtypeStruct((N_CHIPS, HALF_ROWS, cols), bf16)),
        compiler_params=_params(("parallel", "parallel")),
    )(qc_idx, g4, b1)


def _add_chips(a4, b2, qc_idx):
    nb = HALF_ROWS // ADD_ROWS
    cols = a4.shape[2]

    def body(qc_ref, a_ref, b_ref, o_ref):
        o_ref[...] = ((a_ref[0] + b_ref[0].astype(f32)) + b_ref[1].astype(f32)) + b_ref[2].astype(f32)

    return pl.pallas_call(
        body, name="add_chips",
        grid_spec=pltpu.PrefetchScalarGridSpec(
            num_scalar_prefetch=1, grid=(nb,),
            in_specs=[pl.BlockSpec((1, ADD_ROWS, cols), lambda i, qc: (qc[0], i, 0)), pl.BlockSpec((3, ADD_ROWS, cols), lambda i, qc: (0, i, 0))],
            out_specs=pl.BlockSpec((ADD_ROWS, cols), lambda i, qc: (qc[1] * nb + i, 0))),
        out_shape=jax.ShapeDtypeStruct((FLAT_ROWS, cols), f32),
        compiler_params=_params(("parallel",)),
    )(qc_idx, a4, b2)


def _adamw_math(w, g, m, v):
    m = ADAM_B1 * m + (1.0 - ADAM_B1) * g
    v = ADAM_B2 * v + (1.0 - ADAM_B2) * (g * g)
    m_hat = m / (1.0 - ADAM_B1 ** ADAM_STEP)
    v_hat = v / (1.0 - ADAM_B2 ** ADAM_STEP)
    return -ADAM_LR * (m_hat / (jnp.sqrt(v_hat) + ADAM_EPS) + ADAM_WD * w), m, v


def _adamw(w, g, m, v, name):
    rows, cols = w.shape
    rb = ADD_ROWS if rows % ADD_ROWS == 0 else rows

    def body(w_ref, g_ref, m_ref, v_ref, d_ref, mo_ref, vo_ref):
        d_ref[...], mo_ref[...], vo_ref[...] = _adamw_math(w_ref[...], g_ref[...], m_ref[...], v_ref[...])

    blk = pl.BlockSpec((rb, cols), lambda i: (i, 0))
    return pl.pallas_call(
        body, name=name, grid=(rows // rb,), in_specs=[blk] * 4, out_specs=(blk,) * 3,
        out_shape=(jax.ShapeDtypeStruct(w.shape, f32),) * 3, compiler_params=_params(("parallel",)),
    )(w, g, m, v)


def _small_sum_adamw(all_pkts, w, m, v):
    def body(a_ref, w_ref, m_ref, v_ref, g_ref, d_ref, mo_ref, vo_ref):
        g = a_ref[0]
        for r in range(1, 8):
            g = g + a_ref[r]
        g_ref[...] = g
        d_ref[...], mo_ref[...], vo_ref[...] = _adamw_math(w_ref[...], g, m_ref[...], v_ref[...])

    return pl.pallas_call(body, name="small_sum_adamw", out_shape=(jax.ShapeDtypeStruct(w.shape, f32),) * 4)(all_pkts, w, m, v)


FLAT_LAYOUT = (("w_in", (D_MODEL, D_IN // N_CHIPS)), ("w_out", (D_MODEL // N_CHIPS, D_MODEL)), ("w_up", (D_MODEL, D_FF // N_CHIPS)),
               ("w_down", (D_FF // N_CHIPS, D_MODEL)), ("w_ple", (D_PLE, D_MODEL // N_CHIPS)), ("w_ple_gate", (D_MODEL // N_CHIPS, D_MODEL)))
SHARD_AXIS = dict(w_in=1, w_out=0, w_up=1, w_down=0, w_ple=1, w_ple_gate=0)
SMALL_LAYOUT = (("ln_in_g", 0, 1024), ("ln_in_b", 8, 1024), ("ln1_g", 16, 1024), ("ln1_b", 24, 1024), ("b_ple_gate", 32, 1024),
                ("ln2_g", 40, 1024), ("ln2_b", 48, 1024), ("gdn_norm_g", 56, 128), ("fox_norm_g", 57, 64), ("a_log", 58, 4),
                ("dt_bias", 59, 4), ("b_f", 60, 8), ("loss", 61, 1))
SMALL_CONV_ROW = 64
SMALL_ROWS = 128


def _pack_flat(parts, dtype):
    flat = jnp.concatenate([parts[n].astype(dtype).reshape(-1) for n, _ in FLAT_LAYOUT])
    return jnp.pad(flat, (0, FLAT_ROWS * FLAT_COLS - flat.shape[0])).reshape(FLAT_ROWS, FLAT_COLS)


def _unpack_flat(flat):
    flat = flat.reshape(-1)
    out, off = {}, 0
    for n, shp in FLAT_LAYOUT:
        size = shp[0] * shp[1]
        out[n] = flat[off:off + size].reshape(shp)
        off += size
    return out


def _pack_small(vals, conv=None):
    rows = []
    nxt = 0
    for n, r0, size in SMALL_LAYOUT:
        assert r0 == nxt
        v = vals[n].reshape(-1).astype(f32) if n in vals else jnp.zeros((size,), f32)
        nrows = -(-size // LANES)
        rows.append(jnp.pad(v, (0, nrows * LANES - size)).reshape(nrows, LANES))
        nxt = r0 + nrows
    rows.append(jnp.zeros((SMALL_CONV_ROW - nxt, LANES), f32))
    conv_rows = CONV_W * GDN_QKV // LANES
    rows.append(jnp.zeros((conv_rows, LANES), f32) if conv is None else conv.reshape(conv_rows, LANES))
    rows.append(jnp.zeros((SMALL_ROWS - SMALL_CONV_ROW - conv_rows, LANES), f32))
    return jnp.concatenate(rows, axis=0)


def _unpack_small(pkt, shapes):
    out = {}
    for n, r0, size in SMALL_LAYOUT:
        if n in shapes:
            nrows = -(-size // LANES)
            out[n] = pkt[r0:r0 + nrows].reshape(-1)[:size].reshape(shapes[n])
    return out


WEIGHTS = ("ln_in_g", "ln_in_b", "w_in", "conv_w", "a_log", "dt_bias", "gdn_norm_g", "b_f", "fox_norm_g", "w_out", "ln1_g", "ln1_b",
           "w_up", "w_down", "w_ple", "w_ple_gate", "b_ple_gate", "ln2_g", "ln2_b")
SMALL_NAMES = tuple(n for n, _, _ in SMALL_LAYOUT if n != "loss")


def kernel(x, p, ln_in_g, ln_in_b, w_in, conv_w, a_log, dt_bias, gdn_norm_g, b_f, fox_norm_g, w_out, ln1_g, ln1_b, w_up, w_down, w_ple, w_ple_gate, b_ple_gate, ln2_g, ln2_b, loss_target, m_ln_in_g, m_ln_in_b, m_w_in, m_conv_w, m_a_log, m_dt_bias, m_gdn_norm_g, m_b_f, m_fox_norm_g, m_w_out, m_ln1_g, m_ln1_b, m_w_up, m_w_down, m_w_ple, m_w_ple_gate, m_b_ple_gate, m_ln2_g, m_ln2_b, v_ln_in_g, v_ln_in_b, v_w_in, v_conv_w, v_a_log, v_dt_bias, v_gdn_norm_g, v_b_f, v_fox_norm_g, v_w_out, v_ln1_g, v_ln1_b, v_w_up, v_w_down, v_w_ple, v_w_ple_gate, v_b_ple_gate, v_ln2_g, v_ln2_b):
    given = dict(locals())
    w = {n: given[n] for n in WEIGHTS}
    m = {n: given["m_" + n] for n in WEIGHTS}
    v = {n: given["v_" + n] for n in WEIGHTS}
    xi, yi, ci = _mesh_pos()
    q = 2 * xi + yi

    shard2d = {n: w[n][0] for n, _ in FLAT_LAYOUT}
    conv_pkt = jnp.pad(w["conv_w"][0].reshape(-1, LANES), ((0, CONV_PKT_ROWS - CONV_W * GDN_QKV // N_CHIPS // LANES), (0, 0)))
    buf4 = lax.dynamic_update_slice(lax.empty((N_CHIPS, FLAT_ROWS, FLAT_COLS), bf16), _pack_flat(shard2d, bf16)[None], (q, 0, 0))
    conv4 = lax.dynamic_update_slice(lax.empty((N_CHIPS, CONV_PKT_ROWS, LANES), f32), conv_pkt[None], (q, 0, 0))
    full_flat, conv_all = _gather_weights(buf4, conv4)
    per_chip = [_unpack_flat(full_flat[d]) for d in range(N_CHIPS)]
    full = {n: jnp.concatenate([per_chip[d][n] for d in range(N_CHIPS)], axis=SHARD_AXIS[n]) for n, _ in FLAT_LAYOUT}
    conv_rows = CONV_W * GDN_QKV // N_CHIPS // LANES
    conv_full = jnp.concatenate([conv_all[d, :conv_rows].reshape(CONV_W, GDN_QKV // N_CHIPS) for d in range(N_CHIPS)], axis=1)
    wi = full["w_in"]
    w_cat = jnp.concatenate([wi[:, :OFF_BETA], wi[:, OFF_FOX:OFF_F], wi[:, OFF_BETA:OFF_FOX], wi[:, OFF_F:],
                             jnp.zeros((D_MODEL, D_CAT - D_IN), bf16)], axis=1)

    small = {n: w[n] for n in SMALL_NAMES}
    grad_x, big, small_g = _device_grads(x[0], p[0, 0], loss_target[0], small, w_cat, conv_full, full["w_out"], full["w_up"],
                                         full["w_down"], full["w_ple"], full["w_ple_gate"])

    gc = big["w_cat"]
    g_full = dict(w_in=jnp.concatenate([gc[:, :OFF_BETA], gc[:, SEG_SMALL:SEG_SMALL + 8], gc[:, SEG_FOX:SEG_SMALL],
                                        gc[:, SEG_SMALL + 8:SEG_SMALL + 16]], axis=1),
                  w_out=big["w_out"], w_up=big["w_up"], w_down=big["w_down"], w_ple=big["w_ple"], w_ple_gate=big["w_gate"])
    g4 = []
    for d in range(N_CHIPS):
        parts = {}
        for n, shp in FLAT_LAYOUT:
            ax = SHARD_AXIS[n]
            parts[n] = lax.slice_in_dim(g_full[n], d * shp[ax], (d + 1) * shp[ax], axis=ax)
        g4.append(_pack_flat(parts, f32))
    g4 = jnp.stack(g4)
    b1, small_all = _exchange_pairs(g4, _pack_small(small_g, big["conv_w"]))
    qc = jnp.stack([q, ci]).astype(jnp.int32)
    a4, a4b = _add_pair(g4, b1, qc)
    b2 = _exchange_chips(a4b)
    g_shard = _unpack_flat(_share_halves(_add_chips(a4, b2, qc)))

    grads, delta, new_m, new_v = {}, {}, {}, {}
    for n, shp in FLAT_LAYOUT:
        grads[n] = g_shard[n].reshape(w[n].shape)
        d_, m_, v_ = _adamw(w[n][0], g_shard[n], m[n][0], v[n][0], "adamw_" + n)
        delta[n], new_m[n], new_v[n] = (a.reshape(w[n].shape) for a in (d_, m_, v_))
    shapes = {n: w[n].shape for n in SMALL_NAMES}
    g_pkt, d_pkt, m_pkt, v_pkt = _small_sum_adamw(small_all, _pack_small(w), _pack_small(m), _pack_small(v))
    for dst, pkt in ((grads, g_pkt), (delta, d_pkt), (new_m, m_pkt), (new_v, v_pkt)):
        dst.update(_unpack_small(pkt, shapes))
    conv_rows_all = CONV_W * GDN_QKV // LANES
    conv_g_full = g_pkt[SMALL_CONV_ROW:SMALL_CONV_ROW + conv_rows_all].reshape(CONV_W, GDN_QKV)
    conv_g = lax.dynamic_slice_in_dim(conv_g_full, q * (GDN_QKV // N_CHIPS), GDN_QKV // N_CHIPS, axis=1)
    d_, m_, v_ = _adamw(w["conv_w"][0], conv_g, m["conv_w"][0], v["conv_w"][0], "adamw_conv_w")
    grads["conv_w"] = conv_g.reshape(w["conv_w"].shape)
    delta["conv_w"], new_m["conv_w"], new_v["conv_w"] = (a.reshape(w["conv_w"].shape) for a in (d_, m_, v_))
    loss = g_pkt[61, 0]
    return (loss, grad_x[None], *[grads[n] for n in WEIGHTS], *[delta[n] for n in WEIGHTS],
            *[new_m[n] for n in WEIGHTS], *[new_v[n] for n in WEIGHTS])
```

```python
import functools

import jax
import jax.numpy as jnp
from jax import lax
from jax.experimental import pallas as pl
from jax.experimental.pallas import tpu as pltpu

f32 = jnp.float32
bf16 = jnp.bfloat16
HI = lax.Precision.HIGHEST
MESH = pl.DeviceIdType.MESH

D_MODEL = 1024
CHUNK = 64
GDN_HEADS = 4
GDN_DK = 128
FOX_HEADS = 8
FOX_DH = 64
CONV_W = 4
D_FF = 4096
D_PLE = 256
LN_EPS = 1e-5
NORM_EPS = 1e-6
ALPHA = 2.0 ** 0.25
GDN_QKV = 1536
OFF_Z = 1536
OFF_BETA = 2048
OFF_FOX = 2056
OFF_F = 3592
D_IN = 3600
ADAM_LR = 0.001
ADAM_B1 = 0.9
ADAM_B2 = 0.999
ADAM_EPS = 1e-08
ADAM_WD = 0.01
ADAM_STEP = 10

SEG_FOX = 2048
SEG_SMALL = 3584
D_CAT = 3840
LANES = 128
TOK_BLK = 256
FOX_BQ = 256
VMEM_LIMIT = 56 * 1024 * 1024
NEG = -1e30

N_CHIPS = 4


def _params(sem=None, **kw):
    return pltpu.CompilerParams(dimension_semantics=sem, vmem_limit_bytes=VMEM_LIMIT, **kw)


def _sigmoid(x):
    return 1.0 / (1.0 + jnp.exp(-x))


def _softplus(x):
    return jnp.maximum(x, 0.0) + jnp.log(1.0 + jnp.exp(-jnp.abs(x)))


def _ln_fwd(x, g, b):
    mu = jnp.mean(x, -1, keepdims=True)
    xc = x - mu
    var = jnp.mean(xc * xc, -1, keepdims=True)
    rstd = lax.rsqrt(var + LN_EPS)
    xhat = xc * rstd
    return xhat * g + b, xhat, rstd


def _ln_bwd(dy, xhat, rstd, g):
    dxh = dy * g
    m1 = jnp.mean(dxh, -1, keepdims=True)
    m2 = jnp.mean(dxh * xhat, -1, keepdims=True)
    return rstd * (dxh - m1 - xhat * m2)


def _dot(a, b, prec=HI):
    return jnp.dot(a, b, precision=prec, preferred_element_type=f32)


def _dot_nt(a, b, prec=HI):
    return lax.dot_general(a, b, (((1,), (1,)), ((), ())), precision=prec, preferred_element_type=f32)


def _dot_tn(a, b, prec=HI):
    return lax.dot_general(a, b, (((0,), (0,)), ((), ())), precision=prec, preferred_element_type=f32)


def _bdot(a, b):
    return _dot(a.astype(bf16), b.astype(bf16), None)


def _bdot_nt(a, b):
    return _dot_nt(a.astype(bf16), b.astype(bf16), None)


def _bdot_tn(a, b):
    return _dot_tn(a.astype(bf16), b.astype(bf16), None)


def _lane(shape):
    return lax.broadcasted_iota(jnp.int32, shape, len(shape) - 1)


def _mm(a, b, mode, tm, tn, name, out_dtype=f32, epi=None, extra=None, shards=1):
    if mode == "nn":
        (m, k), n = a.shape, b.shape[-1] * shards
    elif mode == "nt":
        (m, k), n = a.shape, b.shape[-2]
    else:
        (k, m), n = a.shape, b.shape[1]
    assert m % tm == 0 and n % tn == 0, (name, m, n, tm, tn)
    per = (n // shards) // tn
    assert mode == "nt" or per * tn * shards == n, (name, n, tn, shards)
    nc = 512 if tn % 512 == 0 else (256 if tn % 256 == 0 else 128)
    ks = k // shards

    def body(a_ref, b_ref, *rest):
        for n0 in range(0, tn, nc):
            if mode == "nn":
                acc = jnp.dot(a_ref[...], b_ref[:, n0:n0 + nc], preferred_element_type=f32)
            elif mode == "nt" and shards > 1:
                acc = jnp.zeros((tm, nc), f32)
                for d in range(shards):
                    acc = acc + lax.dot_general(a_ref[:, d * ks:(d + 1) * ks], b_ref[d, n0:n0 + nc, :], (((1,), (1,)), ((), ())),
                                                preferred_element_type=f32)
            elif mode == "nt":
                acc = lax.dot_general(a_ref[...], b_ref[n0:n0 + nc, :], (((1,), (1,)), ((), ())), preferred_element_type=f32)
            else:
                acc = lax.dot_general(a_ref[...], b_ref[:, n0:n0 + nc], (((0,), (0,)), ((), ())), preferred_element_type=f32)
            if epi == "relu2":
                up_ref, act_ref = rest
                up_ref[:, n0:n0 + nc] = acc
                r = jnp.maximum(acc, 0.0)
                act_ref[:, n0:n0 + nc] = (r * r).astype(bf16)
            elif epi == "relu2_bwd":
                up_ref, o_ref = rest
                o_ref[:, n0:n0 + nc] = (acc * (2.0 * jnp.maximum(up_ref[:, n0:n0 + nc], 0.0))).astype(bf16)
            else:
                (o_ref,) = rest
                o_ref[:, n0:n0 + nc] = acc.astype(out_dtype)

    if mode == "tn":
        a_spec = pl.BlockSpec((k, tm), lambda j, i: (0, i))
    else:
        a_spec = pl.BlockSpec((tm, k), lambda j, i: (i, 0))
    if mode == "nt" and shards > 1:
        b_spec = pl.BlockSpec((shards, tn, ks), lambda j, i: (0, j, 0))
    elif mode == "nt":
        b_spec = pl.BlockSpec((tn, k), lambda j, i: (j, 0))
    elif mode == "nn" and shards > 1:
        b_spec = pl.BlockSpec((None, k, tn), lambda j, i: (j // per, 0, j % per))
    else:
        b_spec = pl.BlockSpec((k, tn), lambda j, i: (0, j))
    o_spec = pl.BlockSpec((tm, tn), lambda j, i: (i, j))
    in_specs = [a_spec, b_spec]
    args = [a, b]
    if epi == "relu2":
        out_shape = (jax.ShapeDtypeStruct((m, n), f32), jax.ShapeDtypeStruct((m, n), bf16))
        out_specs = (o_spec, o_spec)
    elif epi == "relu2_bwd":
        in_specs.append(o_spec)
        args.append(extra)
        out_shape = jax.ShapeDtypeStruct((m, n), bf16)
        out_specs = o_spec
    elif mode == "tn" and shards > 1:
        out_shape = jax.ShapeDtypeStruct((shards, m, n // shards), out_dtype)
        out_specs = pl.BlockSpec((None, tm, tn), lambda j, i: (j // per, i, j % per))
    else:
        out_shape = jax.ShapeDtypeStruct((m, n), out_dtype)
        out_specs = o_spec
    return pl.pallas_call(
        body, name=name, grid=(n // tn, m // tm), in_specs=in_specs, out_specs=out_specs, out_shape=out_shape,
        compiler_params=_params(("parallel", "parallel")),
    )(*args)


def _row_spec(width, col=0):
    return pl.BlockSpec((TOK_BLK, width), lambda i: (i, col))


def _vec_spec(rows, width):
    return pl.BlockSpec((rows, width), lambda i: (0, 0))


def _ln_in(x, g, b):
    t, d = x.shape

    def body(x_ref, g_ref, b_ref, h_ref, hb_ref):
        h, _, _ = _ln_fwd(x_ref[...], g_ref[...], b_ref[...])
        h_ref[...] = h
        hb_ref[...] = h.astype(bf16)

    return pl.pallas_call(
        body, name="ln_in", grid=(t // TOK_BLK,),
        in_specs=[_row_spec(d), _vec_spec(1, d), _vec_spec(1, d)],
        out_specs=(_row_spec(d), _row_spec(d)),
        out_shape=(jax.ShapeDtypeStruct((t, d), f32), jax.ShapeDtypeStruct((t, d), bf16)),
        compiler_params=_params(("parallel",)),
    )(x, g, b)


def _attn_post(o_gdn, proj, o_fox, g_gdn, g_fox2):
    t = o_gdn.shape[0]

    def body(og_ref, z_ref, of_ref, gg_ref, gf_ref, out_ref):
        for h in range(GDN_HEADS):
            sl = slice(h * LANES, (h + 1) * LANES)
            og = og_ref[:, sl]
            z = z_ref[:, sl]
            r = lax.rsqrt(jnp.mean(og * og, -1, keepdims=True) + NORM_EPS)
            out_ref[:, sl] = (og * r * gg_ref[...] * (z * _sigmoid(z))).astype(bf16)
        lo = _lane((TOK_BLK, LANES)) < FOX_DH
        for pr in range(FOX_HEADS // 2):
            sl = slice(pr * LANES, (pr + 1) * LANES)
            of = of_ref[:, sl]
            sq = of * of
            s0 = jnp.sum(jnp.where(lo, sq, 0.0), -1, keepdims=True)
            s1 = jnp.sum(jnp.where(lo, 0.0, sq), -1, keepdims=True)
            r = lax.rsqrt(jnp.where(lo, s0, s1) * (1.0 / FOX_DH) + NORM_EPS)
            out_ref[:, 512 + pr * LANES:512 + (pr + 1) * LANES] = (of * r * gf_ref[...]).astype(bf16)

    return pl.pallas_call(
        body, name="attn_post", grid=(t // TOK_BLK,),
        in_specs=[_row_spec(512), _row_spec(512, OFF_Z // 512), _row_spec(512), _vec_spec(1, LANES), _vec_spec(1, LANES)],
        out_specs=_row_spec(D_MODEL),
        out_shape=jax.ShapeDtypeStruct((t, D_MODEL), bf16),
        compiler_params=_params(("parallel",)),
    )(o_gdn, proj, o_fox, g_gdn, g_fox2)


def _attn_post_bwd(dattn, o_gdn, proj, o_fox, g_gdn, g_fox2):
    t = o_gdn.shape[0]

    def body(da_ref, og_ref, z_ref, of_ref, gg_ref, gf_ref, dog_ref, dz_ref, dof_ref, pg_ref):
        i = pl.program_id(0)

        @pl.when(i == 0)
        def _():
            pg_ref[...] = jnp.zeros_like(pg_ref)

        dgg = jnp.zeros((1, LANES), f32)
        for h in range(GDN_HEADS):
            sl = slice(h * LANES, (h + 1) * LANES)
            og = og_ref[:, sl]
            z = z_ref[:, sl]
            dout = da_ref[:, sl]
            g = gg_ref[...]
            r = lax.rsqrt(jnp.mean(og * og, -1, keepdims=True) + NORM_EPS)
            sg = _sigmoid(z)
            silu = z * sg
            ng = og * r * g
            dng = dout * silu
            dz_ref[:, sl] = (dout * ng * (sg * (1.0 + z * (1.0 - sg)))).astype(bf16)
            dgg = dgg + jnp.sum(dng * og * r, 0, keepdims=True)
            gd = dng * g
            dog_ref[:, sl] = r * gd - og * (r * r * r) * jnp.mean(og * gd, -1, keepdims=True)
        pg_ref[0:1, :] += dgg
        lo = _lane((TOK_BLK, LANES)) < FOX_DH
        dgf = jnp.zeros((1, LANES), f32)
        for pr in range(FOX_HEADS // 2):
            sl = slice(pr * LANES, (pr + 1) * LANES)
            of = of_ref[:, sl]
            dout = da_ref[:, 512 + pr * LANES:512 + (pr + 1) * LANES]
            g = gf_ref[...]
            sq = of * of
            s0 = jnp.sum(jnp.where(lo, sq, 0.0), -1, keepdims=True)
            s1 = jnp.sum(jnp.where(lo, 0.0, sq), -1, keepdims=True)
            r = lax.rsqrt(jnp.where(lo, s0, s1) * (1.0 / FOX_DH) + NORM_EPS)
            dgf = dgf + jnp.sum(dout * of * r, 0, keepdims=True)
            gd = dout * g
            xg = of * gd
            m0 = jnp.sum(jnp.where(lo, xg, 0.0), -1, keepdims=True)
            m1 = jnp.sum(jnp.where(lo, 0.0, xg), -1, keepdims=True)
            dof_ref[:, sl] = r * gd - of * (r * r * r) * (jnp.where(lo, m0, m1) * (1.0 / FOX_DH))
        pg_ref[1:2, :] += dgf

    return pl.pallas_call(
        body, name="attn_post_bwd", grid=(t // TOK_BLK,),
        in_specs=[_row_spec(D_MODEL), _row_spec(512), _row_spec(512, OFF_Z // 512), _row_spec(512), _vec_spec(1, LANES), _vec_spec(1, LANES)],
        out_specs=(_row_spec(512), _row_spec(512), _row_spec(512), _vec_spec(8, LANES)),
        out_shape=(jax.ShapeDtypeStruct((t, 512), f32), jax.ShapeDtypeStruct((t, 512), bf16),
                   jax.ShapeDtypeStruct((t, 512), f32), jax.ShapeDtypeStruct((8, LANES), f32)),
        compiler_params=_params(("arbitrary",)),
    )(dattn, o_gdn, proj, o_fox, g_gdn, g_fox2)


def _ln1(h0, mix, g, b):
    t, d = h0.shape

    def body(h0_ref, mix_ref, g_ref, b_ref, h_ref, hb_ref, xh_ref, rs_ref):
        h, xhat, rstd = _ln_fwd(ALPHA * h0_ref[...] + mix_ref[...], g_ref[...], b_ref[...])
        h_ref[...] = h
        hb_ref[...] = h.astype(bf16)
        xh_ref[...] = xhat
        rs_ref[...] = jnp.broadcast_to(rstd, rs_ref.shape)

    return pl.pallas_call(
        body, name="ln1", grid=(t // TOK_BLK,),
        in_specs=[_row_spec(d), _row_spec(d), _vec_spec(1, d), _vec_spec(1, d)],
        out_specs=(_row_spec(d), _row_spec(d), _row_spec(d), _row_spec(LANES)),
        out_shape=(jax.ShapeDtypeStruct((t, d), f32), jax.ShapeDtypeStruct((t, d), bf16),
                   jax.ShapeDtypeStruct((t, d), f32), jax.ShapeDtypeStruct((t, LANES), f32)),
        compiler_params=_params(("parallel",)),
    )(h0, mix, g, b)


def _ln2_loss(h1, ff, pe, gp, b_gate, g, b, target):
    t, d = h1.shape

    def body(h1_ref, ff_ref, pe_ref, gp_ref, bg_ref, g_ref, b_ref, t_ref, dr_ref, drb_ref, dpe_ref, dgp_ref, pg_ref):
        i = pl.program_id(0)

        @pl.when(i == 0)
        def _():
            pg_ref[...] = jnp.zeros_like(pg_ref)

        sig = _sigmoid(gp_ref[...] + bg_ref[...])
        pe = pe_ref[...]
        r2 = ALPHA * h1_ref[...] + ff_ref[...] + pe * sig
        y, xhat, rstd = _ln_fwd(r2, g_ref[...], b_ref[...])
        err = y - t_ref[...]
        dy = err * (1.0 / d)
        dr = _ln_bwd(dy, xhat, rstd, g_ref[...])
        dr_ref[...] = dr
        drb_ref[...] = dr.astype(bf16)
        dpe_ref[...] = (dr * sig).astype(bf16)
        dgp = dr * pe * sig * (1.0 - sig)
        dgp_ref[...] = dgp.astype(bf16)
        pg_ref[0:1, :] += jnp.sum(dy * xhat, 0, keepdims=True)
        pg_ref[1:2, :] += jnp.sum(dy, 0, keepdims=True)
        pg_ref[2:3, :] += jnp.sum(dgp, 0, keepdims=True)
        pg_ref[3:4, :] += 0.5 * jnp.sum(jnp.mean(err * err, -1, keepdims=True), 0, keepdims=True)

    return pl.pallas_call(
        body, name="ln2_loss", grid=(t // TOK_BLK,),
        in_specs=[_row_spec(d)] * 4 + [_vec_spec(1, d)] * 3 + [_row_spec(d)],
        out_specs=(_row_spec(d), _row_spec(d), _row_spec(d), _row_spec(d), _vec_spec(8, d)),
        out_shape=(jax.ShapeDtypeStruct((t, d), f32), jax.ShapeDtypeStruct((t, d), bf16), jax.ShapeDtypeStruct((t, d), bf16),
                   jax.ShapeDtypeStruct((t, d), bf16), jax.ShapeDtypeStruct((8, d), f32)),
        compiler_params=_params(("arbitrary",)),
    )(h1, ff, pe, gp, b_gate, g, b, target)


def _ln1_bwd(dr2, da, db, xhat, rstd, g):
    t, d = dr2.shape

    def body(dr2_ref, da_ref, db_ref, xh_ref, rs_ref, g_ref, dr_ref, drb_ref, pg_ref):
        i = pl.program_id(0)

        @pl.when(i == 0)
        def _():
            pg_ref[...] = jnp.zeros_like(pg_ref)

        dh = ALPHA * dr2_ref[...] + da_ref[...] + db_ref[...]
        xhat = xh_ref[...]
        dr = _ln_bwd(dh, xhat, rs_ref[:, 0:1], g_ref[...])
        dr_ref[...] = dr
        drb_ref[...] = dr.astype(bf16)
        pg_ref[0:1, :] += jnp.sum(dh * xhat, 0, keepdims=True)
        pg_ref[1:2, :] += jnp.sum(dh, 0, keepdims=True)

    return pl.pallas_call(
        body, name="ln1_bwd", grid=(t // TOK_BLK,),
        in_specs=[_row_spec(d)] * 4 + [_row_spec(LANES), _vec_spec(1, d)],
        out_specs=(_row_spec(d), _row_spec(d), _vec_spec(8, d)),
        out_shape=(jax.ShapeDtypeStruct((t, d), f32), jax.ShapeDtypeStruct((t, d), bf16), jax.ShapeDtypeStruct((8, d), f32)),
        compiler_params=_params(("arbitrary",)),
    )(dr2, da, db, xhat, rstd, g)


def _ln_in_bwd(x, dr1, dmm, g):
    t, d = x.shape

    def body(x_ref, dr1_ref, dmm_ref, g_ref, dx_ref, pg_ref):
        i = pl.program_id(0)

        @pl.when(i == 0)
        def _():
            pg_ref[...] = jnp.zeros_like(pg_ref)

        dh = ALPHA * dr1_ref[...] + dmm_ref[...]
        _, xhat, rstd = _ln_fwd(x_ref[...], g_ref[...], 0.0)
        dx_ref[...] = _ln_bwd(dh, xhat, rstd, g_ref[...])
        pg_ref[0:1, :] += jnp.sum(dh * xhat, 0, keepdims=True)
        pg_ref[1:2, :] += jnp.sum(dh, 0, keepdims=True)

    return pl.pallas_call(
        body, name="ln_in_bwd", grid=(t // TOK_BLK,),
        in_specs=[_row_spec(d)] * 3 + [_vec_spec(1, d)],
        out_specs=(_row_spec(d), _vec_spec(8, d)),
        out_shape=(jax.ShapeDtypeStruct((t, d), f32), jax.ShapeDtypeStruct((8, d), f32)),
        compiler_params=_params(("arbitrary",)),
    )(x, dr1, dmm, g)


def _tri(n, upper=False, strict=False):
    r = lax.broadcasted_iota(jnp.int32, (n, n), 0)
    c = lax.broadcasted_iota(jnp.int32, (n, n), 1)
    if upper:
        m = (c > r) if strict else (c >= r)
    else:
        m = (c < r) if strict else (c <= r)
    return jnp.where(m, 1.0, 0.0).astype(f32)


def _gate_values(x, bias, alog, lane):
    z = x + bias
    return jnp.where(lane < 4, _sigmoid(z), jnp.where(lane < 8, -jnp.exp(alog) * _softplus(z), jnp.where(lane < 16, -_softplus(-z), 0.0)))


def _gates(proj, bias_row, alog_row):
    t = proj.shape[0]
    nch = t // CHUNK

    def body(x_ref, bias_ref, alog_ref, gates_ref, gcum_ref, gcumt_ref):
        lane = _lane((t, LANES))
        gates = _gate_values(x_ref[...], bias_ref[...], alog_ref[...], lane)
        gates_ref[...] = gates
        g3 = gates.reshape(nch, CHUNK, LANES)
        tri = jnp.broadcast_to(_tri(CHUNK)[None], (nch, CHUNK, CHUNK))
        loc = jnp.einsum("bij,bjk->bik", tri, g3, precision=HI, preferred_element_type=f32)
        tot = jnp.sum(g3, axis=1)
        offs = _dot(_tri(nch, strict=True), tot)
        glob = loc + offs[:, None, :]
        lane3 = _lane((nch, CHUNK, LANES))
        gcum = jnp.where(lane3 < 4, g3, jnp.where(lane3 < 8, loc, glob)).reshape(t, LANES)
        gcum_ref[...] = gcum
        gcumt_ref[...] = gcum.T

    return pl.pallas_call(
        body, name="gates", grid=(1,),
        in_specs=[pl.BlockSpec((t, LANES), lambda i: (0, SEG_SMALL // LANES)), _vec_spec(1, LANES), _vec_spec(1, LANES)],
        out_specs=(pl.BlockSpec((t, LANES), lambda i: (0, 0)), pl.BlockSpec((t, LANES), lambda i: (0, 0)),
                   pl.BlockSpec((LANES, t), lambda i: (0, 0))),
        out_shape=(jax.ShapeDtypeStruct((t, LANES), f32), jax.ShapeDtypeStruct((t, LANES), f32), jax.ShapeDtypeStruct((LANES, t), f32)),
        compiler_params=_params(("arbitrary",)),
    )(proj, bias_row, alog_row)


def _gates_bwd(proj, bias_row, alog_row, gates, dgates, dccol, dct):
    t = proj.shape[0]
    nch = t // CHUNK

    def body(x_ref, bias_ref, alog_ref, gates_ref, dg_ref, dcc_ref, dct_ref, dx_ref, pg_ref):
        lane = _lane((t, LANES))
        d = dg_ref[...] + dcc_ref[...] + dct_ref[...].T
        d3 = d.reshape(nch, CHUNK, LANES)
        tri = jnp.broadcast_to(_tri(CHUNK, upper=True)[None], (nch, CHUNK, CHUNK))
        loc = jnp.einsum("bij,bjk->bik", tri, d3, precision=HI, preferred_element_type=f32)
        tot = jnp.sum(d3, axis=1)
        offs = _dot(_tri(nch, upper=True, strict=True), tot)
        glob = loc + offs[:, None, :]
        lane3 = _lane((nch, CHUNK, LANES))
        dpre = jnp.where(lane3 < 4, d3, jnp.where(lane3 < 8, loc, glob)).reshape(t, LANES)
        z = x_ref[...] + bias_ref[...]
        sg = _sigmoid(z)
        dx = jnp.where(lane < 4, dpre * sg * (1.0 - sg),
                       jnp.where(lane < 8, dpre * (-jnp.exp(alog_ref[...])) * sg, jnp.where(lane < 16, dpre * (1.0 - sg), 0.0)))
        dx_ref[...] = dx.astype(bf16)
        pg_ref[...] = jnp.zeros_like(pg_ref)
        pg_ref[0:1, :] = jnp.sum(dx, 0, keepdims=True)
        pg_ref[1:2, :] = jnp.sum(jnp.where((lane >= 4) & (lane < 8), dpre * gates_ref[...], 0.0), 0, keepdims=True)

    full = pl.BlockSpec((t, LANES), lambda i: (0, 0))
    return pl.pallas_call(
        body, name="gates_bwd", grid=(1,),
        in_specs=[pl.BlockSpec((t, LANES), lambda i: (0, SEG_SMALL // LANES)), _vec_spec(1, LANES), _vec_spec(1, LANES),
                  full, full, full, pl.BlockSpec((LANES, t), lambda i: (0, 0))],
        out_specs=(full, _vec_spec(8, LANES)),
        out_shape=(jax.ShapeDtypeStruct((t, LANES), bf16), jax.ShapeDtypeStruct((8, LANES), f32)),
        compiler_params=_params(("arbitrary",)),
    )(proj, bias_row, alog_row, gates, dgates, dccol, dct)


def _conv_act(u, cw, row, t):
    c = cw[3:4, :] * u
    for jj in range(CONV_W - 1):
        sh = CONV_W - 1 - jj
        c = c + cw[jj:jj + 1, :] * jnp.where(row >= sh, pltpu.roll(u, sh, axis=0), 0.0)
    return c


def _gdn_conv(proj, conv_w):
    t = proj.shape[0]
    nblk = GDN_QKV // LANES

    def body(u_ref, cw_ref, c_ref, y_ref):
        j = pl.program_id(0)
        row = lax.broadcasted_iota(jnp.int32, (t, LANES), 0)
        c = _conv_act(u_ref[...], cw_ref[...], row, t)
        c_ref[...] = c
        s = c * _sigmoid(c)
        r = lax.rsqrt(jnp.sum(s * s, -1, keepdims=True) + NORM_EPS)
        scale = jnp.where(j < GDN_HEADS, GDN_DK ** -0.5, 1.0)
        y_ref[...] = jnp.where(j < 2 * GDN_HEADS, s * (r * scale), s)

    blk = pl.BlockSpec((t, LANES), lambda j: (0, j))
    return pl.pallas_call(
        body, name="gdn_conv", grid=(nblk,),
        in_specs=[blk, pl.BlockSpec((CONV_W, LANES), lambda j: (0, j))],
        out_specs=(blk, blk),
        out_shape=(jax.ShapeDtypeStruct((t, GDN_QKV), f32), jax.ShapeDtypeStruct((t, GDN_QKV), f32)),
        compiler_params=_params(("parallel",)),
    )(proj, conv_w)


def _gdn_conv_bwd(proj, conv_w, c, dy):
    t = proj.shape[0]
    nblk = GDN_QKV // LANES

    def body(u_ref, cw_ref, c_ref, dy_ref, du_ref, dcw_ref):
        j = pl.program_id(0)
        row = lax.broadcasted_iota(jnp.int32, (t, LANES), 0)
        u = u_ref[...]
        cw = cw_ref[...]
        c = c_ref[...]
        dy = dy_ref[...]
        sg = _sigmoid(c)
        s = c * sg
        r = lax.rsqrt(jnp.sum(s * s, -1, keepdims=True) + NORM_EPS)
        n = s * r
        scale = jnp.where(j < GDN_HEADS, GDN_DK ** -0.5, 1.0)
        dn = dy * scale
        ds = jnp.where(j < 2 * GDN_HEADS, r * (dn - n * jnp.sum(dn * n, -1, keepdims=True)), dy)
        dc = ds * (sg * (1.0 + c * (1.0 - sg)))
        du = cw[3:4, :] * dc
        dcw_ref[...] = jnp.zeros_like(dcw_ref)
        dcw_ref[3:4, :] = jnp.sum(dc * u, 0, keepdims=True)
        for jj in range(CONV_W - 1):
            sh = CONV_W - 1 - jj
            du = du + cw[jj:jj + 1, :] * jnp.where(row < t - sh, pltpu.roll(dc, t - sh, axis=0), 0.0)
            dcw_ref[jj:jj + 1, :] = jnp.sum(dc * jnp.where(row >= sh, pltpu.roll(u, sh, axis=0), 0.0), 0, keepdims=True)
        du_ref[...] = du.astype(bf16)

    blk = pl.BlockSpec((t, LANES), lambda j: (0, j))
    return pl.pallas_call(
        body, name="gdn_conv_bwd", grid=(nblk,),
        in_specs=[blk, pl.BlockSpec((CONV_W, LANES), lambda j: (0, j)), blk, blk],
        out_specs=(blk, pl.BlockSpec((8, LANES), lambda j: (0, j))),
        out_shape=(jax.ShapeDtypeStruct((t, GDN_QKV), bf16), jax.ShapeDtypeStruct((8, GDN_QKV), f32)),
        compiler_params=_params(("parallel",)),
    )(proj, conv_w, c, dy)


def _chunk_masks():
    r = lax.broadcasted_iota(jnp.int32, (CHUNK, CHUNK), 0)
    c = lax.broadcasted_iota(jnp.int32, (CHUNK, CHUNK), 1)
    return r >= c, r > c, r == c


def _col_to_row(col, eye):
    return jnp.sum(jnp.where(eye, col, 0.0), axis=0, keepdims=True)


def _row_to_col(row, eye):
    return jnp.sum(jnp.where(eye, row, 0.0), axis=1, keepdims=True)


NN = (((1,), (0,)), ((), ()))
NT = (((1,), (1,)), ((), ()))
TN = (((0,), (0,)), ((), ()))
GDN_GROUP = 4


def _mx(a, b, dims=NN, passes=1):
    d = lambda p, q: lax.dot_general(p, q, dims, preferred_element_type=f32)
    ah, bh = a.astype(bf16), b.astype(bf16)
    if passes == 1:
        return d(ah, bh)
    al = (a - ah.astype(f32)).astype(bf16)
    bl = (b - bh.astype(f32)).astype(bf16)
    return d(ah, bh) + (d(ah, bl) + d(al, bh))


def _gdn_decay(gam, masks):
    causal, _, eye = masks
    return jnp.exp(jnp.where(causal, gam - _col_to_row(gam, eye), NEG))


def _gdn_local(y, gcum):
    t = y.shape[0]
    nch = t // CHUNK
    rows_blk = GDN_GROUP * CHUNK

    def body(y_ref, g_ref, u_ref, w_ref, qk_ref, tinv_ref):
        masks = _chunk_masks()
        _, strict, eye = masks
        for j in range(GDN_GROUP):
            rs = slice(j * CHUNK, (j + 1) * CHUNK)
            for h in range(GDN_HEADS):
                qn = y_ref[rs, h * LANES:(h + 1) * LANES]
                kn = y_ref[rs, 512 + h * LANES:512 + (h + 1) * LANES]
                v = y_ref[rs, 1024 + h * LANES:1024 + (h + 1) * LANES]
                beta = g_ref[rs, h:h + 1]
                gam = g_ref[rs, 4 + h:5 + h]
                dec = _gdn_decay(gam, masks)
                x = -jnp.where(strict, _mx(kn, kn, NT) * dec * beta, 0.0)
                tinv = jnp.where(eye, 1.0, 0.0) + x
                for _ in range(5):
                    x = _mx(x, x, NN, 3)
                    tinv = tinv + _mx(tinv, x, NN, 3)
                e = jnp.exp(gam)
                u_ref[rs, h * LANES:(h + 1) * LANES] = _mx(tinv, beta * v)
                w_ref[rs, h * LANES:(h + 1) * LANES] = _mx(tinv, (beta * e) * kn)
                qk_ref[j, h] = _mx(qn, kn, NT) * dec
                tinv_ref[j, h] = tinv

    mat = pl.BlockSpec((GDN_GROUP, GDN_HEADS, CHUNK, CHUNK), lambda n: (n, 0, 0, 0))
    return pl.pallas_call(
        body, name="gdn_local", grid=(nch // GDN_GROUP,),
        in_specs=[pl.BlockSpec((rows_blk, GDN_QKV), lambda n: (n, 0)), pl.BlockSpec((rows_blk, LANES), lambda n: (n, 0))],
        out_specs=(pl.BlockSpec((rows_blk, 512), lambda n: (n, 0)), pl.BlockSpec((rows_blk, 512), lambda n: (n, 0)), mat, mat),
        out_shape=(jax.ShapeDtypeStruct((t, 512), f32), jax.ShapeDtypeStruct((t, 512), f32),
                   jax.ShapeDtypeStruct((nch, GDN_HEADS, CHUNK, CHUNK), f32), jax.ShapeDtypeStruct((nch, GDN_HEADS, CHUNK, CHUNK), f32)),
        compiler_params=_params(("parallel",)),
    )(y, gcum)


def _gdn_fwd(y, gcum, u, w, qk):
    t = y.shape[0]
    nch = t // CHUNK

    def body(y_ref, g_ref, u_ref, w_ref, qk_ref, o_ref, sall_ref, s_ref):
        @pl.when(pl.program_id(0) == 0)
        def _():
            s_ref[...] = jnp.zeros_like(s_ref)

        for h in range(GDN_HEADS):
            sl = slice(h * LANES, (h + 1) * LANES)
            gam = g_ref[:, 4 + h:5 + h]
            gam_last = gam[CHUNK - 1:CHUNK, :]
            s = s_ref[h]
            sall_ref[0, h] = s
            vn = u_ref[:, sl] - _mx(w_ref[:, sl], s)
            o_ref[:, sl] = _mx(y_ref[:, sl] * jnp.exp(gam), s) + _mx(qk_ref[0, h], vn)
            kd = y_ref[:, 512 + h * LANES:512 + (h + 1) * LANES] * jnp.exp(gam_last - gam)
            s_ref[h] = jnp.exp(gam_last) * s + _mx(kd, vn, TN)

    row = lambda width: pl.BlockSpec((CHUNK, width), lambda n: (n, 0))
    return pl.pallas_call(
        body, name="gdn_fwd", grid=(nch,),
        in_specs=[row(GDN_QKV), row(LANES), row(512), row(512), pl.BlockSpec((1, GDN_HEADS, CHUNK, CHUNK), lambda n: (n, 0, 0, 0))],
        out_specs=(row(512), pl.BlockSpec((1, GDN_HEADS, LANES, LANES), lambda n: (n, 0, 0, 0))),
        out_shape=(jax.ShapeDtypeStruct((t, 512), f32), jax.ShapeDtypeStruct((nch, GDN_HEADS, LANES, LANES), f32)),
        scratch_shapes=[pltpu.VMEM((GDN_HEADS, LANES, LANES), f32)],
        compiler_params=_params(("arbitrary",)),
    )(y, gcum, u, w, qk)


def _gdn_bwd(y, gcum, u_all, w_all, qk_all, tinv_all, sall, do):
    t = y.shape[0]
    nch = t // CHUNK

    def body(y_ref, g_ref, u_ref, w_ref, qk_ref, tinv_ref, sall_ref, do_ref, dy_ref, dg_ref, ds_ref):
        @pl.when(pl.program_id(0) == 0)
        def _():
            ds_ref[...] = jnp.zeros_like(ds_ref)

        masks = _chunk_masks()
        causal, strict, eye = masks
        lane = _lane((CHUNK, LANES))
        row = lax.broadcasted_iota(jnp.int32, (CHUNK, 1), 0)
        dgates = jnp.zeros((CHUNK, LANES), f32)
        for h in range(GDN_HEADS):
            sl = slice(h * LANES, (h + 1) * LANES)
            qn = y_ref[:, sl]
            kn = y_ref[:, 512 + h * LANES:512 + (h + 1) * LANES]
            v = y_ref[:, 1024 + h * LANES:1024 + (h + 1) * LANES]
            beta = g_ref[:, h:h + 1]
            gam = g_ref[:, 4 + h:5 + h]
            gam_last = gam[CHUNK - 1:CHUNK, :]
            dec = _gdn_decay(gam, masks)
            e = jnp.exp(gam)
            f = jnp.exp(gam_last - gam)
            gl = jnp.exp(gam_last)
            kkd = _mx(kn, kn, NT) * dec
            u, w, qk, tinv = u_ref[:, sl], w_ref[:, sl], qk_ref[0, h], tinv_ref[0, h]
            qd, kd = qn * e, kn * f
            s = sall_ref[0, h]
            dsn = ds_ref[h]
            d_o = do_ref[:, sl]
            vn = u - _mx(w, s)
            dvn = _mx(qk, d_o, TN) + _mx(kd, dsn)
            dqk = jnp.where(causal, _mx(d_o, vn, NT), 0.0)
            dqd = _mx(d_o, s, NT)
            dkd = _mx(vn, dsn, NT)
            dgl = jnp.sum(jnp.sum(dsn * s, axis=1, keepdims=True), axis=0, keepdims=True)
            dw = -_mx(dvn, s, NT)
            ds_ref[h] = _mx(qd, d_o, TN) - _mx(w, dvn, TN) + gl * dsn
            dru = _mx(tinv, dvn, TN)
            drw = _mx(tinv, dw, TN)
            dn = jnp.where(strict, -(_mx(dru, u, NT) + _mx(drw, w, NT)), 0.0)
            dv = beta * dru
            drw_k = jnp.sum(drw * kn, axis=1, keepdims=True)
            dbeta = jnp.sum(dru * v, axis=1, keepdims=True) + e * drw_k + jnp.sum(dn * kkd, axis=1, keepdims=True)
            de = beta * drw_k
            dk = (beta * e) * drw
            dkk = dn * beta * dec
            dk = dk + _mx(dkk, kn) + _mx(dkk, kn, TN)
            dqkr = dqk * dec
            dq = _mx(dqkr, kn) + dqd * e
            dk = dk + _mx(dqkr, qn, TN) + dkd * f
            m = dn * (kkd * beta) + dqk * qk
            dgam = jnp.sum(m, axis=1, keepdims=True) - _row_to_col(jnp.sum(m, axis=0, keepdims=True), eye)
            de = de + jnp.sum(dqd * qn, axis=1, keepdims=True)
            df = jnp.sum(dkd * kn, axis=1, keepdims=True)
            dgam = dgam + de * e - df * f
            dgam_last = jnp.sum(df * f, axis=0, keepdims=True) + dgl * gl
            dgam = dgam + jnp.where(row == CHUNK - 1, dgam_last, 0.0)
            dy_ref[:, sl] = dq
            dy_ref[:, 512 + h * LANES:512 + (h + 1) * LANES] = dk
            dy_ref[:, 1024 + h * LANES:1024 + (h + 1) * LANES] = dv
            dgates = dgates + jnp.where(lane == h, dbeta, 0.0) + jnp.where(lane == 4 + h, dgam, 0.0)
        dg_ref[...] = dgates

    rev = lambda width: pl.BlockSpec((CHUNK, width), lambda n: (nch - 1 - n, 0))
    mat = lambda d: pl.BlockSpec((1, GDN_HEADS, d, d), lambda n: (nch - 1 - n, 0, 0, 0))
    return pl.pallas_call(
        body, name="gdn_bwd", grid=(nch,),
        in_specs=[rev(GDN_QKV), rev(LANES), rev(512), rev(512), mat(CHUNK), mat(CHUNK), mat(LANES), rev(512)],
        out_specs=(rev(GDN_QKV), rev(LANES)),
        out_shape=(jax.ShapeDtypeStruct((t, GDN_QKV), f32), jax.ShapeDtypeStruct((t, LANES), f32)),
        scratch_shapes=[pltpu.VMEM((GDN_HEADS, LANES, LANES), f32)],
        compiler_params=_params(("arbitrary",)),
    )(y, gcum, u_all, w_all, qk_all, tinv_all, sall, do)


def _fox_scores(q_ref, k_ref, gcum_ref, gcumt_ref, h, i, t):
    pr = h // 2
    lo = (h % 2) * FOX_DH
    lane = _lane((FOX_BQ, LANES))
    mask = (lane >= lo) & (lane < lo + FOX_DH)
    qm = jnp.where(mask, q_ref[:, pr * LANES:(pr + 1) * LANES], 0.0).astype(bf16)
    kp = k_ref[:, pr * LANES:(pr + 1) * LANES].astype(bf16)
    s = _dot_nt(qm, kp, None) * (FOX_DH ** -0.5)
    s = s + gcum_ref[:, 8 + h:9 + h] - gcumt_ref[8 + h:9 + h, :]
    rows = i * FOX_BQ + lax.broadcasted_iota(jnp.int32, (FOX_BQ, t), 0)
    cols = lax.broadcasted_iota(jnp.int32, (FOX_BQ, t), 1)
    return jnp.where(cols <= rows, s, NEG), mask, qm, kp


def _fox_fwd(proj, gcum, gcumt):
    t = proj.shape[0]

    def body(q_ref, k_ref, v_ref, gcum_ref, gcumt_ref, o_ref, lse_ref):
        i = pl.program_id(0)
        lane = _lane((FOX_BQ, LANES))
        lse_all = jnp.zeros((FOX_BQ, LANES), f32)
        for pr in range(FOX_HEADS // 2):
            vp = v_ref[:, pr * LANES:(pr + 1) * LANES].astype(bf16)
            o_pair = jnp.zeros((FOX_BQ, LANES), f32)
            for h in (2 * pr, 2 * pr + 1):
                s, mask, _, _ = _fox_scores(q_ref, k_ref, gcum_ref, gcumt_ref, h, i, t)
                m = jnp.max(s, axis=1, keepdims=True)
                p = jnp.exp(s - m)
                l = jnp.sum(p, axis=1, keepdims=True)
                o_h = _dot((p * (1.0 / l)).astype(bf16), vp, None)
                o_pair = jnp.where(mask, o_h, o_pair)
                lse_all = jnp.where(lane == h, m + jnp.log(l), lse_all)
            o_ref[:, pr * LANES:(pr + 1) * LANES] = o_pair
        lse_ref[...] = lse_all

    qblk = lambda col: pl.BlockSpec((FOX_BQ, 512), lambda i: (i, col))
    full = lambda col: pl.BlockSpec((t, 512), lambda i: (0, col))
    c0 = SEG_FOX // 512
    return pl.pallas_call(
        body, name="fox_fwd", grid=(t // FOX_BQ,),
        in_specs=[qblk(c0), full(c0 + 1), full(c0 + 2), pl.BlockSpec((FOX_BQ, LANES), lambda i: (i, 0)),
                  pl.BlockSpec((LANES, t), lambda i: (0, 0))],
        out_specs=(pl.BlockSpec((FOX_BQ, 512), lambda i: (i, 0)), pl.BlockSpec((FOX_BQ, LANES), lambda i: (i, 0))),
        out_shape=(jax.ShapeDtypeStruct((t, 512), f32), jax.ShapeDtypeStruct((t, LANES), f32)),
        compiler_params=_params(("parallel",)),
    )(proj, proj, proj, gcum, gcumt)


def _fox_bwd(proj, gcum, gcumt, o, lse, do):
    t = proj.shape[0]

    def body(q_ref, k_ref, v_ref, gcum_ref, gcumt_ref, o_ref, lse_ref, do_ref, dq_ref, dk_ref, dv_ref, dcc_ref, dct_ref):
        i = pl.program_id(0)

        @pl.when(i == 0)
        def _():
            dk_ref[...] = jnp.zeros_like(dk_ref)
            dv_ref[...] = jnp.zeros_like(dv_ref)
            dct_ref[...] = jnp.zeros_like(dct_ref)

        lane = _lane((FOX_BQ, LANES))
        dcc = jnp.zeros((FOX_BQ, LANES), f32)
        scale = FOX_DH ** -0.5
        for pr in range(FOX_HEADS // 2):
            sl = slice(pr * LANES, (pr + 1) * LANES)
            vp = v_ref[:, sl].astype(bf16)
            dq_pair = jnp.zeros((FOX_BQ, LANES), f32)
            for h in (2 * pr, 2 * pr + 1):
                s, mask, qm, kp = _fox_scores(q_ref, k_ref, gcum_ref, gcumt_ref, h, i, t)
                p = jnp.exp(s - lse_ref[:, h:h + 1])
                dom = jnp.where(mask, do_ref[:, sl], 0.0)
                delta = jnp.sum(dom * o_ref[:, sl], axis=1, keepdims=True)
                domb = dom.astype(bf16)
                dp = _dot_nt(domb, vp, None)
                ds = p * (dp - delta)
                dsb = ds.astype(bf16)
                dv_ref[:, sl] += _dot_tn(p.astype(bf16), domb, None)
                dk_ref[:, sl] += _dot_tn(dsb, qm, None) * scale
                dq_pair = jnp.where(mask, _dot(dsb, kp, None) * scale, dq_pair)
                dcc = jnp.where(lane == 8 + h, jnp.sum(ds, axis=1, keepdims=True), dcc)
                dct_ref[8 + h:9 + h, :] += -jnp.sum(ds, axis=0, keepdims=True)
            dq_ref[:, sl] = dq_pair.astype(bf16)
        dcc_ref[...] = dcc

    qblk = lambda col: pl.BlockSpec((FOX_BQ, 512), lambda i: (i, col))
    full = lambda col: pl.BlockSpec((t, 512), lambda i: (0, col))
    rblk = pl.BlockSpec((FOX_BQ, LANES), lambda i: (i, 0))
    c0 = SEG_FOX // 512
    return pl.pallas_call(
        body, name="fox_bwd", grid=(t // FOX_BQ,),
        in_specs=[qblk(c0), full(c0 + 1), full(c0 + 2), rblk, pl.BlockSpec((LANES, t), lambda i: (0, 0)),
                  qblk(0), rblk, qblk(0)],
        out_specs=(qblk(0), full(0), full(0), rblk, pl.BlockSpec((LANES, t), lambda i: (0, 0))),
        out_shape=(jax.ShapeDtypeStruct((t, 512), bf16), jax.ShapeDtypeStruct((t, 512), f32), jax.ShapeDtypeStruct((t, 512), f32),
                   jax.ShapeDtypeStruct((t, LANES), f32), jax.ShapeDtypeStruct((LANES, t), f32)),
        compiler_params=_params(("arbitrary",)),
    )(proj, proj, proj, gcum, gcumt, o, lse, do)


def _row(v, width=None):
    v = v.reshape(1, -1).astype(f32)
    if width is not None and v.shape[1] < width:
        v = jnp.pad(v, ((0, 0), (0, width - v.shape[1])))
    return v


def _device_grads(x, p, target, small, w_cat, conv_w, w_out, w_up, w_down, w_ple, w_gate):
    z4 = jnp.zeros((4,), f32)
    bias_row = _row(jnp.concatenate([z4, small["dt_bias"].reshape(-1), small["b_f"].reshape(-1)]), LANES)
    alog_row = _row(jnp.concatenate([z4, small["a_log"].reshape(-1)]), LANES)
    g_gdn = _row(small["gdn_norm_g"])
    g_fox2 = _row(jnp.tile(small["fox_norm_g"].reshape(-1), 2))
    pb = p.astype(bf16)

    h0, h0b = _ln_in(x, _row(small["ln_in_g"]), _row(small["ln_in_b"]))
    proj = _mm(h0b, w_cat, "nn", 256, D_CAT, "mm_proj")
    gates, gcum, gcumt = _gates(proj, bias_row, alog_row)
    conv_c, qkv_n = _gdn_conv(proj, conv_w)
    gu, gw, gqk, gtinv = _gdn_local(qkv_n, gcum)
    o_gdn, sall = _gdn_fwd(qkv_n, gcum, gu, gw, gqk)
    o_fox, lse = _fox_fwd(proj, gcum, gcumt)
    attn = _attn_post(o_gdn, proj, o_fox, g_gdn, g_fox2)
    mix = _mm(attn, w_out, "nn", 512, D_MODEL, "mm_mix")
    h1, h1b, xhat1, rstd1 = _ln1(h0, mix, _row(small["ln1_g"]), _row(small["ln1_b"]))
    up, act = _mm(h1b, w_up, "nn", 256, 1024, "mm_up", epi="relu2", shards=N_CHIPS)
    ff = _mm(act, w_down, "nn", 256, D_MODEL, "mm_down")
    gp = _mm(h1b, w_gate, "nn", 512, D_MODEL, "mm_gate")
    pe = _mm(pb, w_ple, "nn", 512, D_MODEL // N_CHIPS, "mm_ple", shards=N_CHIPS)
    dr2, dr2b, dpe, dgp, pg2 = _ln2_loss(h1, ff, pe, gp, _row(small["b_ple_gate"]), _row(small["ln2_g"]), _row(small["ln2_b"]), target)

    dup = _mm(dr2b, w_down, "nt", 256, 2048, "mm_dact", epi="relu2_bwd", extra=up)
    g_down = _mm(act, dr2b, "tn", 1024, D_MODEL, "mm_gdown")
    dh1_a = _mm(dup, w_up, "nt", 256, D_MODEL, "mm_dh1a", shards=N_CHIPS)
    g_up = _mm(h1b, dup, "tn", 1024, 1024, "mm_gup", shards=N_CHIPS)
    dh1_b = _mm(dgp, w_gate, "nt", 512, D_MODEL, "mm_dh1b")
    g_gate = _mm(h1b, dgp, "tn", 1024, D_MODEL, "mm_ggate")
    g_ple = _mm(pb, dpe, "tn", D_PLE, D_MODEL // N_CHIPS, "mm_gple", shards=N_CHIPS)
    dr1, dr1b, pg1 = _ln1_bwd(dr2, dh1_a, dh1_b, xhat1, rstd1, _row(small["ln1_g"]))
    dattn = _mm(dr1b, w_out, "nt", 512, D_MODEL, "mm_dattn")
    g_out = _mm(attn, dr1b, "tn", 1024, D_MODEL, "mm_gout")
    do_gdn, dz, do_fox, pga = _attn_post_bwd(dattn, o_gdn, proj, o_fox, g_gdn, g_fox2)
    dfq, dfk, dfv, dccol, dct = _fox_bwd(proj, gcum, gcumt, o_fox, lse, do_fox)
    dqkv_n, dgates = _gdn_bwd(qkv_n, gcum, gu, gw, gqk, gtinv, sall, do_gdn)
    dsmall, pgg = _gates_bwd(proj, bias_row, alog_row, gates, dgates, dccol, dct)
    du, g_conv8 = _gdn_conv_bwd(proj, conv_w, conv_c, dqkv_n)
    t = x.shape[0]
    dproj = jnp.concatenate([du, dz, dfq, dfk.astype(bf16), dfv.astype(bf16), dsmall, jnp.zeros((t, D_CAT - SEG_SMALL - LANES), bf16)], axis=1)
    dh0_mm = _mm(dproj, w_cat, "nt", 256, D_MODEL, "mm_dh0")
    g_cat = _mm(h0b, dproj, "tn", 1024, 1280, "mm_gcat")
    grad_x, pg0 = _ln_in_bwd(x, dr1, dh0_mm, _row(small["ln_in_g"]))

    g_fox = pga[1, :FOX_DH] + pga[1, FOX_DH:]
    small_grads = dict(
        ln_in_g=pg0[0], ln_in_b=pg0[1], ln1_g=pg1[0], ln1_b=pg1[1], b_ple_gate=pg2[2], ln2_g=pg2[0], ln2_b=pg2[1],
        gdn_norm_g=pga[0], fox_norm_g=g_fox, a_log=pgg[1, 4:8], dt_bias=pgg[0, 4:8], b_f=pgg[0, 8:16], loss=pg2[3, 0:1])
    big_grads = dict(w_cat=g_cat, conv_w=g_conv8[:CONV_W], w_out=g_out, w_up=g_up, w_down=g_down, w_ple=g_ple, w_gate=g_gate)
    return grad_x, big_grads, small_grads


ANY = pl.BlockSpec(memory_space=pl.ANY)
CONV_PKT_ROWS = 16


def _mesh_pos():
    return lax.axis_index("x"), lax.axis_index("y"), lax.axis_index("c")


def _other_chips(x, y):
    return [(1 - x, y), (x, 1 - y), (1 - x, 1 - y)]


def _rcopy(src, dst, send_sem, recv_sem, dev):
    return pltpu.make_async_remote_copy(src_ref=src, dst_ref=dst, send_sem=send_sem, recv_sem=recv_sem,
                                        device_id=dev, device_id_type=MESH)


def _gather_weights(bufs, chunks, conv4):
    nw = len(bufs)
    base, n_ici = [], 0
    for nch in chunks:
        base.append(n_ici)
        n_ici += 3 * nch

    def body(*refs):
        outs = refs[nw + 1:2 * nw + 1]
        conv_ref = refs[2 * nw + 1]
        send_sems, recv_sems, csend, crecv = refs[2 * nw + 2:]
        x, y, c = _mesh_pos()
        q = 2 * x + y
        chips = _other_chips(x, y)
        sib = (x, y, 1 - c)

        def piece(i, slot, hf, ch):
            half = bufs[i].shape[1] // 2
            cr = half // chunks[i]
            return outs[i].at[slot, pl.ds(hf * half + ch * cr, cr)]

        started = []
        for i in range(nw):
            for ch in range(chunks[i]):
                for k, chip in enumerate(chips):
                    s = base[i] + k * chunks[i] + ch
                    cp = _rcopy(piece(i, q, c, ch), piece(i, q, c, ch), send_sems.at[s], recv_sems.at[s], (*chip, c))
                    cp.start()
                    started.append(cp)
        for k, chip in enumerate(chips):
            cp = _rcopy(conv_ref.at[q], conv_ref.at[q], csend.at[k], crecv.at[k], (*chip, c))
            cp.start()
            started.append(cp)
        for i in range(nw):
            for ch in range(chunks[i]):
                for k, chip in enumerate(chips):
                    s = base[i] + k * chunks[i] + ch
                    landed = piece(i, 2 * chip[0] + chip[1], c, ch)
                    _rcopy(landed, landed, send_sems.at[s], recv_sems.at[s], (*chip, c)).wait_recv()
                    cp = _rcopy(landed, landed, send_sems.at[n_ici + s], recv_sems.at[n_ici + s], sib)
                    cp.start()
                    started.append(cp)
        for i in range(nw):
            for ch in range(chunks[i]):
                for k, chip in enumerate(chips):
                    s = base[i] + k * chunks[i] + ch
                    passed = piece(i, 2 * chip[0] + chip[1], 1 - c, ch)
                    _rcopy(passed, passed, send_sems.at[n_ici + s], recv_sems.at[n_ici + s], sib).wait_recv()
        for k, chip in enumerate(chips):
            theirs = conv_ref.at[2 * chip[0] + chip[1]]
            _rcopy(theirs, theirs, csend.at[k], crecv.at[k], (*chip, c)).wait_recv()
        for cp in started:
            cp.wait_send()

    args = list(bufs) + [conv4]
    return pl.pallas_call(
        body, name="gather_weights", out_shape=tuple(jax.ShapeDtypeStruct(a.shape, a.dtype) for a in args),
        in_specs=[ANY] * (nw + 1), out_specs=(ANY,) * (nw + 1), input_output_aliases={i: i for i in range(nw + 1)},
        scratch_shapes=[pltpu.SemaphoreType.DMA((2 * n_ici,)), pltpu.SemaphoreType.DMA((2 * n_ici,)), pltpu.SemaphoreType.DMA((3,)),
                        pltpu.SemaphoreType.DMA((3,))],
    )(*args)


def _exchange_pairs(gs, small):
    nw = len(gs)

    def body(*refs):
        g_refs, small_ref = refs[:nw], refs[nw]
        b1_refs, all_ref = refs[nw + 1:2 * nw + 1], refs[2 * nw + 1]
        ssem, rsem, s2, r2, local_sem = refs[2 * nw + 2:]
        x, y, c = _mesh_pos()
        me = 4 * x + 2 * y + c
        sib = (x, y, 1 - c)
        own = pltpu.make_async_copy(small_ref, all_ref.at[me], local_sem)
        own.start()

        def pair_copy(i, d):
            half = gs[i].shape[1] // 2
            return _rcopy(g_refs[i].at[d, pl.ds((1 - c) * half, half)], b1_refs[i].at[d], ssem.at[i * N_CHIPS + d],
                          rsem.at[i * N_CHIPS + d], sib)

        started = []
        for i in range(nw):
            for d in range(N_CHIPS):
                cp = pair_copy(i, d)
                cp.start()
                started.append(cp)
        peers = []
        for r in range(1, 8):
            fx, fy, fc = (r >> 2) & 1, (r >> 1) & 1, r & 1
            peers.append((1 - x if fx else x, 1 - y if fy else y, 1 - c if fc else c))
        for r, peer in enumerate(peers):
            cp = _rcopy(small_ref, all_ref.at[me], s2.at[r], r2.at[r], peer)
            cp.start()
            started.append(cp)
        for i in range(nw):
            for d in range(N_CHIPS):
                pair_copy(i, d).wait_recv()
        for r, peer in enumerate(peers):
            _rcopy(small_ref, all_ref.at[4 * peer[0] + 2 * peer[1] + peer[2]], s2.at[r], r2.at[r], peer).wait_recv()
        for cp in started:
            cp.wait_send()
        own.wait()

    out_shape = tuple(jax.ShapeDtypeStruct((N_CHIPS, g.shape[1] // 2, g.shape[2]), g.dtype) for g in gs)
    return pl.pallas_call(
        body, name="exchange_pairs", out_shape=out_shape + (jax.ShapeDtypeStruct((8,) + small.shape, small.dtype),),
        in_specs=[ANY] * (nw + 1), out_specs=(ANY,) * (nw + 1),
        scratch_shapes=[pltpu.SemaphoreType.DMA((nw * N_CHIPS,)), pltpu.SemaphoreType.DMA((nw * N_CHIPS,)), pltpu.SemaphoreType.DMA((7,)),
                        pltpu.SemaphoreType.DMA((7,)), pltpu.SemaphoreType.DMA],
    )(*gs, small)


def _exchange_chips(a4s):
    nw = len(a4s)

    def body(*refs):
        a_refs, b2_refs = refs[:nw], refs[nw:2 * nw]
        ssem, rsem = refs[2 * nw:]
        x, y, c = _mesh_pos()
        chips = _other_chips(x, y)

        def chip_copy(i, k):
            chip = chips[k]
            return _rcopy(a_refs[i].at[2 * chip[0] + chip[1]], b2_refs[i].at[k], ssem.at[3 * i + k], rsem.at[3 * i + k], (*chip, c))

        started = []
        for i in range(nw):
            for k in range(3):
                cp = chip_copy(i, k)
                cp.start()
                started.append(cp)
        for i in range(nw):
            for k in range(3):
                chip_copy(i, k).wait_recv()
        for cp in started:
            cp.wait_send()

    return pl.pallas_call(
        body, name="exchange_chips", out_shape=tuple(jax.ShapeDtypeStruct((3,) + a.shape[1:], a.dtype) for a in a4s),
        in_specs=[ANY] * nw, out_specs=(ANY,) * nw,
        scratch_shapes=[pltpu.SemaphoreType.DMA((3 * nw,)), pltpu.SemaphoreType.DMA((3 * nw,))],
    )(*a4s)


def _share_halves(rs):
    nw = len(rs)

    def body(*refs):
        outs = refs[nw:2 * nw]
        ssem, rsem = refs[2 * nw:]
        x, y, c = _mesh_pos()
        sib = (x, y, 1 - c)
        started = []
        for i in range(nw):
            half = rs[i].shape[0] // 2
            mine = outs[i].at[pl.ds(c * half, half)]
            cp = _rcopy(mine, mine, ssem.at[i], rsem.at[i], sib)
            cp.start()
            started.append(cp)
        for i in range(nw):
            half = rs[i].shape[0] // 2
            theirs = outs[i].at[pl.ds((1 - c) * half, half)]
            _rcopy(theirs, theirs, ssem.at[i], rsem.at[i], sib).wait_recv()
        for cp in started:
            cp.wait_send()

    return pl.pallas_call(
        body, name="share_halves", out_shape=tuple(jax.ShapeDtypeStruct(r.shape, r.dtype) for r in rs),
        in_specs=[ANY] * nw, out_specs=(ANY,) * nw, input_output_aliases={i: i for i in range(nw)},
        scratch_shapes=[pltpu.SemaphoreType.DMA((nw,)), pltpu.SemaphoreType.DMA((nw,))],
    )(*rs)


ADD_ROWS = 256


def _add_pair(g4, b1, qc_idx, name):
    _, half, cols = b1.shape
    rb = min(ADD_ROWS, half)
    nb = half // rb

    def body(qc_ref, g_ref, b_ref, o_ref, ob_ref):
        a = g_ref[...] + b_ref[...]
        o_ref[...] = a
        ob_ref[...] = a.astype(bf16)

    blk = (1, rb, cols)
    out = pl.BlockSpec(blk, lambda d, i, qc: (d, i, 0))
    return pl.pallas_call(
        body, name=name,
        grid_spec=pltpu.PrefetchScalarGridSpec(
            num_scalar_prefetch=1, grid=(N_CHIPS, nb),
            in_specs=[pl.BlockSpec(blk, lambda d, i, qc: (d, qc[1] * nb + i, 0)), out],
            out_specs=(out, out)),
        out_shape=(jax.ShapeDtypeStruct(b1.shape, f32), jax.ShapeDtypeStruct(b1.shape, bf16)),
        compiler_params=_params(("parallel", "parallel")),
    )(qc_idx, g4, b1)


def _add_chips(a4, b2, qc_idx, name):
    _, half, cols = a4.shape
    rb = min(ADD_ROWS, half)
    nb = half // rb

    def body(qc_ref, a_ref, b_ref, o_ref):
        o_ref[...] = ((a_ref[0] + b_ref[0].astype(f32)) + b_ref[1].astype(f32)) + b_ref[2].astype(f32)

    return pl.pallas_call(
        body, name=name,
        grid_spec=pltpu.PrefetchScalarGridSpec(
            num_scalar_prefetch=1, grid=(nb,),
            in_specs=[pl.BlockSpec((1, rb, cols), lambda i, qc: (qc[0], i, 0)), pl.BlockSpec((3, rb, cols), lambda i, qc: (0, i, 0))],
            out_specs=pl.BlockSpec((rb, cols), lambda i, qc: (qc[1] * nb + i, 0))),
        out_shape=jax.ShapeDtypeStruct((2 * half, cols), f32),
        compiler_params=_params(("parallel",)),
    )(qc_idx, a4, b2)


def _adamw_math(w, g, m, v):
    m = ADAM_B1 * m + (1.0 - ADAM_B1) * g
    v = ADAM_B2 * v + (1.0 - ADAM_B2) * (g * g)
    m_hat = m / (1.0 - ADAM_B1 ** ADAM_STEP)
    v_hat = v / (1.0 - ADAM_B2 ** ADAM_STEP)
    return -ADAM_LR * (m_hat / (jnp.sqrt(v_hat) + ADAM_EPS) + ADAM_WD * w), m, v


def _adamw(w, g, m, v, name):
    rows, cols = w.shape
    rb = ADD_ROWS if rows % ADD_ROWS == 0 else rows

    def body(w_ref, g_ref, m_ref, v_ref, d_ref, mo_ref, vo_ref):
        d_ref[...], mo_ref[...], vo_ref[...] = _adamw_math(w_ref[...], g_ref[...], m_ref[...], v_ref[...])

    blk = pl.BlockSpec((rb, cols), lambda i: (i, 0))
    return pl.pallas_call(
        body, name=name, grid=(rows // rb,), in_specs=[blk] * 4, out_specs=(blk,) * 3,
        out_shape=(jax.ShapeDtypeStruct(w.shape, f32),) * 3, compiler_params=_params(("parallel",)),
    )(w, g, m, v)


def _small_sum_adamw(all_pkts, w, m, v):
    def body(a_ref, w_ref, m_ref, v_ref, g_ref, d_ref, mo_ref, vo_ref):
        g = a_ref[0]
        for r in range(1, 8):
            g = g + a_ref[r]
        g_ref[...] = g
        d_ref[...], mo_ref[...], vo_ref[...] = _adamw_math(w_ref[...], g, m_ref[...], v_ref[...])

    return pl.pallas_call(body, name="small_sum_adamw", out_shape=(jax.ShapeDtypeStruct(w.shape, f32),) * 4)(all_pkts, w, m, v)


SHARDED = (("w_in", (D_MODEL, D_IN // N_CHIPS), 2), ("w_out", (D_MODEL // N_CHIPS, D_MODEL), 1), ("w_up", (D_MODEL, D_FF // N_CHIPS), 2),
           ("w_ple_gate", (D_MODEL // N_CHIPS, D_MODEL), 1), ("w_ple", (D_PLE, D_MODEL // N_CHIPS), 1),
           ("w_down", (D_FF // N_CHIPS, D_MODEL), 2))
SMALL_LAYOUT = (("ln_in_g", 0, 1024), ("ln_in_b", 8, 1024), ("ln1_g", 16, 1024), ("ln1_b", 24, 1024), ("b_ple_gate", 32, 1024),
                ("ln2_g", 40, 1024), ("ln2_b", 48, 1024), ("gdn_norm_g", 56, 128), ("fox_norm_g", 57, 64), ("a_log", 58, 4),
                ("dt_bias", 59, 4), ("b_f", 60, 8), ("loss", 61, 1))
SMALL_CONV_ROW = 64
SMALL_ROWS = 128


def _pack_small(vals, conv=None):
    rows = []
    nxt = 0
    for n, r0, size in SMALL_LAYOUT:
        assert r0 == nxt
        v = vals[n].reshape(-1).astype(f32) if n in vals else jnp.zeros((size,), f32)
        nrows = -(-size // LANES)
        rows.append(jnp.pad(v, (0, nrows * LANES - size)).reshape(nrows, LANES))
        nxt = r0 + nrows
    rows.append(jnp.zeros((SMALL_CONV_ROW - nxt, LANES), f32))
    conv_rows = CONV_W * GDN_QKV // LANES
    rows.append(jnp.zeros((conv_rows, LANES), f32) if conv is None else conv.reshape(conv_rows, LANES))
    rows.append(jnp.zeros((SMALL_ROWS - SMALL_CONV_ROW - conv_rows, LANES), f32))
    return jnp.concatenate(rows, axis=0)


def _unpack_small(pkt, shapes):
    out = {}
    for n, r0, size in SMALL_LAYOUT:
        if n in shapes:
            nrows = -(-size // LANES)
            out[n] = pkt[r0:r0 + nrows].reshape(-1)[:size].reshape(shapes[n])
    return out


WEIGHTS = ("ln_in_g", "ln_in_b", "w_in", "conv_w", "a_log", "dt_bias", "gdn_norm_g", "b_f", "fox_norm_g", "w_out", "ln1_g", "ln1_b",
           "w_up", "w_down", "w_ple", "w_ple_gate", "b_ple_gate", "ln2_g", "ln2_b")
SMALL_NAMES = tuple(n for n, _, _ in SMALL_LAYOUT if n != "loss")


def kernel(x, p, ln_in_g, ln_in_b, w_in, conv_w, a_log, dt_bias, gdn_norm_g, b_f, fox_norm_g, w_out, ln1_g, ln1_b, w_up, w_down, w_ple, w_ple_gate, b_ple_gate, ln2_g, ln2_b, loss_target, m_ln_in_g, m_ln_in_b, m_w_in, m_conv_w, m_a_log, m_dt_bias, m_gdn_norm_g, m_b_f, m_fox_norm_g, m_w_out, m_ln1_g, m_ln1_b, m_w_up, m_w_down, m_w_ple, m_w_ple_gate, m_b_ple_gate, m_ln2_g, m_ln2_b, v_ln_in_g, v_ln_in_b, v_w_in, v_conv_w, v_a_log, v_dt_bias, v_gdn_norm_g, v_b_f, v_fox_norm_g, v_w_out, v_ln1_g, v_ln1_b, v_w_up, v_w_down, v_w_ple, v_w_ple_gate, v_b_ple_gate, v_ln2_g, v_ln2_b):
    given = dict(locals())
    w = {n: given[n] for n in WEIGHTS}
    m = {n: given["m_" + n] for n in WEIGHTS}
    v = {n: given["v_" + n] for n in WEIGHTS}
    xi, yi, ci = _mesh_pos()
    q = 2 * xi + yi

    def slot_buffer(val, dtype):
        return lax.dynamic_update_slice(lax.empty((N_CHIPS,) + val.shape, dtype), val.astype(dtype)[None], (q, 0, 0))

    conv_rows = CONV_W * GDN_QKV // N_CHIPS // LANES
    conv_pkt = jnp.pad(w["conv_w"][0].reshape(-1, LANES), ((0, CONV_PKT_ROWS - conv_rows), (0, 0)))
    gathered = _gather_weights([slot_buffer(w[n][0], bf16) for n, _, _ in SHARDED], [nch for _, _, nch in SHARDED],
                               slot_buffer(conv_pkt, f32))
    full = dict(zip([n for n, _, _ in SHARDED], gathered[:-1]))
    conv_all = gathered[-1]
    conv_full = jnp.concatenate([conv_all[d, :conv_rows].reshape(CONV_W, GDN_QKV // N_CHIPS) for d in range(N_CHIPS)], axis=1)
    wi = jnp.concatenate([full["w_in"][d] for d in range(N_CHIPS)], axis=1)
    w_cat = jnp.concatenate([wi[:, :OFF_BETA], wi[:, OFF_FOX:OFF_F], wi[:, OFF_BETA:OFF_FOX], wi[:, OFF_F:],
                             jnp.zeros((D_MODEL, D_CAT - D_IN), bf16)], axis=1)

    small = {n: w[n] for n in SMALL_NAMES}
    grad_x, big, small_g = _device_grads(
        x[0], p[0, 0], loss_target[0], small, w_cat, conv_full, full["w_out"].reshape(D_MODEL, D_MODEL), full["w_up"],
        full["w_down"].reshape(D_FF, D_MODEL), full["w_ple"], full["w_ple_gate"].reshape(D_MODEL, D_MODEL))

    gc = big["w_cat"]
    g_in = jnp.concatenate([gc[:, :OFF_BETA], gc[:, SEG_SMALL:SEG_SMALL + 8], gc[:, SEG_FOX:SEG_SMALL],
                            gc[:, SEG_SMALL + 8:SEG_SMALL + 16]], axis=1)
    shard_cols = D_IN // N_CHIPS
    by_dest = dict(w_in=jnp.stack([g_in[:, d * shard_cols:(d + 1) * shard_cols] for d in range(N_CHIPS)]), w_up=big["w_up"],
                   w_ple=big["w_ple"], w_out=big["w_out"], w_down=big["w_down"], w_ple_gate=big["w_gate"])
    gs = [by_dest[n].reshape((N_CHIPS,) + shp) for n, shp, _ in SHARDED]
    *b1s, small_all = _exchange_pairs(gs, _pack_small(small_g, big["conv_w"]))
    qc = jnp.stack([q, ci]).astype(jnp.int32)
    sums = [_add_pair(g, b1, qc, "add_pair_" + n) for g, b1, (n, _, _) in zip(gs, b1s, SHARDED)]
    b2s = _exchange_chips([ab for _, ab in sums])
    reduced = _share_halves([_add_chips(a, b2, qc, "add_chips_" + n) for (a, _), b2, (n, _, _) in zip(sums, b2s, SHARDED)])

    grads, delta, new_m, new_v = {}, {}, {}, {}
    for g, (n, _, _) in zip(reduced, SHARDED):
        grads[n] = g.reshape(w[n].shape)
        d_, m_, v_ = _adamw(w[n][0], g, m[n][0], v[n][0], "adamw_" + n)
        delta[n], new_m[n], new_v[n] = (a.reshape(w[n].shape) for a in (d_, m_, v_))
    shapes = {n: w[n].shape for n in SMALL_NAMES}
    g_pkt, d_pkt, m_pkt, v_pkt = _small_sum_adamw(small_all, _pack_small(w), _pack_small(m), _pack_small(v))
    for dst, pkt in ((grads, g_pkt), (delta, d_pkt), (new_m, m_pkt), (new_v, v_pkt)):
        dst.update(_unpack_small(pkt, shapes))
    conv_rows_all = CONV_W * GDN_QKV // LANES
    conv_g_full = g_pkt[SMALL_CONV_ROW:SMALL_CONV_ROW + conv_rows_all].reshape(CONV_W, GDN_QKV)
    conv_g = lax.dynamic_slice_in_dim(conv_g_full, q * (GDN_QKV // N_CHIPS), GDN_QKV // N_CHIPS, axis=1)
    d_, m_, v_ = _adamw(w["conv_w"][0], conv_g, m["conv_w"][0], v["conv_w"][0], "adamw_conv_w")
    grads["conv_w"] = conv_g.reshape(w["conv_w"].shape)
    delta["conv_w"], new_m["conv_w"], new_v["conv_w"] = (a.reshape(w["conv_w"].shape) for a in (d_, m_, v_))
    loss = g_pkt[61, 0]
    return (loss, grad_x[None], *[grads[n] for n in WEIGHTS], *[delta[n] for n in WEIGHTS],
            *[new_m[n] for n in WEIGHTS], *[new_v[n] for n in WEIGHTS])
```

```python
import functools

import jax
import jax.numpy as jnp
from jax import lax
from jax.experimental import pallas as pl
from jax.experimental.pallas import tpu as pltpu

f32 = jnp.float32
bf16 = jnp.bfloat16
HI = lax.Precision.HIGHEST
MESH = pl.DeviceIdType.MESH

D_MODEL = 1024
CHUNK = 64
GDN_HEADS = 4
GDN_DK = 128
FOX_HEADS = 8
FOX_DH = 64
CONV_W = 4
D_FF = 4096
D_PLE = 256
LN_EPS = 1e-5
NORM_EPS = 1e-6
ALPHA = 2.0 ** 0.25
GDN_QKV = 1536
OFF_Z = 1536
OFF_BETA = 2048
OFF_FOX = 2056
OFF_F = 3592
D_IN = 3600
ADAM_LR = 0.001
ADAM_B1 = 0.9
ADAM_B2 = 0.999
ADAM_EPS = 1e-08
ADAM_WD = 0.01
ADAM_STEP = 10

SEG_FOX = 2048
SEG_SMALL = 3584
D_CAT = 3840
LANES = 128
TOK_BLK = 256
FOX_BQ = 256
VMEM_LIMIT = 56 * 1024 * 1024
NEG = -1e30

N_CHIPS = 4


def _params(sem=None, **kw):
    return pltpu.CompilerParams(dimension_semantics=sem, vmem_limit_bytes=VMEM_LIMIT, **kw)


def _sigmoid(x):
    return 1.0 / (1.0 + jnp.exp(-x))


def _softplus(x):
    return jnp.maximum(x, 0.0) + jnp.log(1.0 + jnp.exp(-jnp.abs(x)))


def _ln_fwd(x, g, b):
    mu = jnp.mean(x, -1, keepdims=True)
    xc = x - mu
    var = jnp.mean(xc * xc, -1, keepdims=True)
    rstd = lax.rsqrt(var + LN_EPS)
    xhat = xc * rstd
    return xhat * g + b, xhat, rstd


def _ln_bwd(dy, xhat, rstd, g):
    dxh = dy * g
    m1 = jnp.mean(dxh, -1, keepdims=True)
    m2 = jnp.mean(dxh * xhat, -1, keepdims=True)
    return rstd * (dxh - m1 - xhat * m2)


def _dot(a, b, prec=HI):
    return jnp.dot(a, b, precision=prec, preferred_element_type=f32)


def _dot_nt(a, b, prec=HI):
    return lax.dot_general(a, b, (((1,), (1,)), ((), ())), precision=prec, preferred_element_type=f32)


def _dot_tn(a, b, prec=HI):
    return lax.dot_general(a, b, (((0,), (0,)), ((), ())), precision=prec, preferred_element_type=f32)


def _bdot(a, b):
    return _dot(a.astype(bf16), b.astype(bf16), None)


def _bdot_nt(a, b):
    return _dot_nt(a.astype(bf16), b.astype(bf16), None)


def _bdot_tn(a, b):
    return _dot_tn(a.astype(bf16), b.astype(bf16), None)


def _lane(shape):
    return lax.broadcasted_iota(jnp.int32, shape, len(shape) - 1)


def _mm(a, b, mode, tm, tn, name, out_dtype=f32, epi=None, extra=None, shards=1):
    if mode == "nn":
        (m, k), n = a.shape, b.shape[-1] * shards
    elif mode == "nt":
        (m, k), n = a.shape, b.shape[-2]
    else:
        (k, m), n = a.shape, b.shape[1]
    assert m % tm == 0 and n % tn == 0, (name, m, n, tm, tn)
    per = (n // shards) // tn
    assert mode == "nt" or per * tn * shards == n, (name, n, tn, shards)
    nc = 512 if tn % 512 == 0 else (256 if tn % 256 == 0 else 128)
    ks = k // shards

    def body(a_ref, b_ref, *rest):
        for n0 in range(0, tn, nc):
            if mode == "nn":
                acc = jnp.dot(a_ref[...], b_ref[:, n0:n0 + nc], preferred_element_type=f32)
            elif mode == "nt" and shards > 1:
                acc = jnp.zeros((tm, nc), f32)
                for d in range(shards):
                    acc = acc + lax.dot_general(a_ref[:, d * ks:(d + 1) * ks], b_ref[d, n0:n0 + nc, :], (((1,), (1,)), ((), ())),
                                                preferred_element_type=f32)
            elif mode == "nt":
                acc = lax.dot_general(a_ref[...], b_ref[n0:n0 + nc, :], (((1,), (1,)), ((), ())), preferred_element_type=f32)
            else:
                acc = lax.dot_general(a_ref[...], b_ref[:, n0:n0 + nc], (((0,), (0,)), ((), ())), preferred_element_type=f32)
            if epi == "relu2":
                up_ref, act_ref = rest
                up_ref[:, n0:n0 + nc] = acc
                r = jnp.maximum(acc, 0.0)
                act_ref[:, n0:n0 + nc] = (r * r).astype(bf16)
            elif epi == "relu2_bwd":
                up_ref, o_ref = rest
                o_ref[:, n0:n0 + nc] = (acc * (2.0 * jnp.maximum(up_ref[:, n0:n0 + nc], 0.0))).astype(bf16)
            else:
                (o_ref,) = rest
                o_ref[:, n0:n0 + nc] = acc.astype(out_dtype)

    if mode == "tn":
        a_spec = pl.BlockSpec((k, tm), lambda j, i: (0, i))
    else:
        a_spec = pl.BlockSpec((tm, k), lambda j, i: (i, 0))
    if mode == "nt" and shards > 1:
        b_spec = pl.BlockSpec((shards, tn, ks), lambda j, i: (0, j, 0))
    elif mode == "nt":
        b_spec = pl.BlockSpec((tn, k), lambda j, i: (j, 0))
    elif mode == "nn" and shards > 1:
        b_spec = pl.BlockSpec((None, k, tn), lambda j, i: (j // per, 0, j % per))
    else:
        b_spec = pl.BlockSpec((k, tn), lambda j, i: (0, j))
    o_spec = pl.BlockSpec((tm, tn), lambda j, i: (i, j))
    in_specs = [a_spec, b_spec]
    args = [a, b]
    if epi == "relu2":
        out_shape = (jax.ShapeDtypeStruct((m, n), f32), jax.ShapeDtypeStruct((m, n), bf16))
        out_specs = (o_spec, o_spec)
    elif epi == "relu2_bwd":
        in_specs.append(o_spec)
        args.append(extra)
        out_shape = jax.ShapeDtypeStruct((m, n), bf16)
        out_specs = o_spec
    elif mode == "tn" and shards > 1:
        out_shape = jax.ShapeDtypeStruct((shards, m, n // shards), out_dtype)
        out_specs = pl.BlockSpec((None, tm, tn), lambda j, i: (j // per, i, j % per))
    else:
        out_shape = jax.ShapeDtypeStruct((m, n), out_dtype)
        out_specs = o_spec
    return pl.pallas_call(
        body, name=name, grid=(n // tn, m // tm), in_specs=in_specs, out_specs=out_specs, out_shape=out_shape,
        compiler_params=_params(("parallel", "parallel")),
    )(*args)


def _row_spec(width, col=0):
    return pl.BlockSpec((TOK_BLK, width), lambda i: (i, col))


def _vec_spec(rows, width):
    return pl.BlockSpec((rows, width), lambda i: (0, 0))


def _ln_in(x, g, b):
    t, d = x.shape

    def body(x_ref, g_ref, b_ref, h_ref, hb_ref):
        h, _, _ = _ln_fwd(x_ref[...], g_ref[...], b_ref[...])
        h_ref[...] = h
        hb_ref[...] = h.astype(bf16)

    return pl.pallas_call(
        body, name="ln_in", grid=(t // TOK_BLK,),
        in_specs=[_row_spec(d), _vec_spec(1, d), _vec_spec(1, d)],
        out_specs=(_row_spec(d), _row_spec(d)),
        out_shape=(jax.ShapeDtypeStruct((t, d), f32), jax.ShapeDtypeStruct((t, d), bf16)),
        compiler_params=_params(("parallel",)),
    )(x, g, b)


def _attn_post(o_gdn, proj, o_fox, g_gdn, g_fox2):
    t = o_gdn.shape[0]

    def body(og_ref, z_ref, of_ref, gg_ref, gf_ref, out_ref):
        for h in range(GDN_HEADS):
            sl = slice(h * LANES, (h + 1) * LANES)
            og = og_ref[:, sl]
            z = z_ref[:, sl]
            r = lax.rsqrt(jnp.mean(og * og, -1, keepdims=True) + NORM_EPS)
            out_ref[:, sl] = (og * r * gg_ref[...] * (z * _sigmoid(z))).astype(bf16)
        lo = _lane((TOK_BLK, LANES)) < FOX_DH
        for pr in range(FOX_HEADS // 2):
            sl = slice(pr * LANES, (pr + 1) * LANES)
            of = of_ref[:, sl]
            sq = of * of
            s0 = jnp.sum(jnp.where(lo, sq, 0.0), -1, keepdims=True)
            s1 = jnp.sum(jnp.where(lo, 0.0, sq), -1, keepdims=True)
            r = lax.rsqrt(jnp.where(lo, s0, s1) * (1.0 / FOX_DH) + NORM_EPS)
            out_ref[:, 512 + pr * LANES:512 + (pr + 1) * LANES] = (of * r * gf_ref[...]).astype(bf16)

    return pl.pallas_call(
        body, name="attn_post", grid=(t // TOK_BLK,),
        in_specs=[_row_spec(512), _row_spec(512, OFF_Z // 512), _row_spec(512), _vec_spec(1, LANES), _vec_spec(1, LANES)],
        out_specs=_row_spec(D_MODEL),
        out_shape=jax.ShapeDtypeStruct((t, D_MODEL), bf16),
        compiler_params=_params(("parallel",)),
    )(o_gdn, proj, o_fox, g_gdn, g_fox2)


def _attn_post_bwd(dattn, o_gdn, proj, o_fox, g_gdn, g_fox2):
    t = o_gdn.shape[0]

    def body(da_ref, og_ref, z_ref, of_ref, gg_ref, gf_ref, dog_ref, dz_ref, dof_ref, pg_ref):
        i = pl.program_id(0)

        @pl.when(i == 0)
        def _():
            pg_ref[...] = jnp.zeros_like(pg_ref)

        dgg = jnp.zeros((1, LANES), f32)
        for h in range(GDN_HEADS):
            sl = slice(h * LANES, (h + 1) * LANES)
            og = og_ref[:, sl]
            z = z_ref[:, sl]
            dout = da_ref[:, sl]
            g = gg_ref[...]
            r = lax.rsqrt(jnp.mean(og * og, -1, keepdims=True) + NORM_EPS)
            sg = _sigmoid(z)
            silu = z * sg
            ng = og * r * g
            dng = dout * silu
            dz_ref[:, sl] = (dout * ng * (sg * (1.0 + z * (1.0 - sg)))).astype(bf16)
            dgg = dgg + jnp.sum(dng * og * r, 0, keepdims=True)
            gd = dng * g
            dog_ref[:, sl] = r * gd - og * (r * r * r) * jnp.mean(og * gd, -1, keepdims=True)
        pg_ref[0:1, :] += dgg
        lo = _lane((TOK_BLK, LANES)) < FOX_DH
        dgf = jnp.zeros((1, LANES), f32)
        for pr in range(FOX_HEADS // 2):
            sl = slice(pr * LANES, (pr + 1) * LANES)
            of = of_ref[:, sl]
            dout = da_ref[:, 512 + pr * LANES:512 + (pr + 1) * LANES]
            g = gf_ref[...]
            sq = of * of
            s0 = jnp.sum(jnp.where(lo, sq, 0.0), -1, keepdims=True)
            s1 = jnp.sum(jnp.where(lo, 0.0, sq), -1, keepdims=True)
            r = lax.rsqrt(jnp.where(lo, s0, s1) * (1.0 / FOX_DH) + NORM_EPS)
            dgf = dgf + jnp.sum(dout * of * r, 0, keepdims=True)
            gd = dout * g
            xg = of * gd
            m0 = jnp.sum(jnp.where(lo, xg, 0.0), -1, keepdims=True)
            m1 = jnp.sum(jnp.where(lo, 0.0, xg), -1, keepdims=True)
            dof_ref[:, sl] = r * gd - of * (r * r * r) * (jnp.where(lo, m0, m1) * (1.0 / FOX_DH))
        pg_ref[1:2, :] += dgf

    return pl.pallas_call(
        body, name="attn_post_bwd", grid=(t // TOK_BLK,),
        in_specs=[_row_spec(D_MODEL), _row_spec(512), _row_spec(512, OFF_Z // 512), _row_spec(512), _vec_spec(1, LANES), _vec_spec(1, LANES)],
        out_specs=(_row_spec(512), _row_spec(512), _row_spec(512), _vec_spec(8, LANES)),
        out_shape=(jax.ShapeDtypeStruct((t, 512), f32), jax.ShapeDtypeStruct((t, 512), bf16),
                   jax.ShapeDtypeStruct((t, 512), f32), jax.ShapeDtypeStruct((8, LANES), f32)),
        compiler_params=_params(("arbitrary",)),
    )(dattn, o_gdn, proj, o_fox, g_gdn, g_fox2)


def _ln1(h0, mix, g, b):
    t, d = h0.shape

    def body(h0_ref, mix_ref, g_ref, b_ref, h_ref, hb_ref, xh_ref, rs_ref):
        h, xhat, rstd = _ln_fwd(ALPHA * h0_ref[...] + mix_ref[...], g_ref[...], b_ref[...])
        h_ref[...] = h
        hb_ref[...] = h.astype(bf16)
        xh_ref[...] = xhat
        rs_ref[...] = jnp.broadcast_to(rstd, rs_ref.shape)

    return pl.pallas_call(
        body, name="ln1", grid=(t // TOK_BLK,),
        in_specs=[_row_spec(d), _row_spec(d), _vec_spec(1, d), _vec_spec(1, d)],
        out_specs=(_row_spec(d), _row_spec(d), _row_spec(d), _row_spec(LANES)),
        out_shape=(jax.ShapeDtypeStruct((t, d), f32), jax.ShapeDtypeStruct((t, d), bf16),
                   jax.ShapeDtypeStruct((t, d), f32), jax.ShapeDtypeStruct((t, LANES), f32)),
        compiler_params=_params(("parallel",)),
    )(h0, mix, g, b)


def _ln2_loss(h1, ff, pe, gp, b_gate, g, b, target):
    t, d = h1.shape

    def body(h1_ref, ff_ref, pe_ref, gp_ref, bg_ref, g_ref, b_ref, t_ref, dr_ref, drb_ref, dpe_ref, dgp_ref, pg_ref):
        i = pl.program_id(0)

        @pl.when(i == 0)
        def _():
            pg_ref[...] = jnp.zeros_like(pg_ref)

        sig = _sigmoid(gp_ref[...] + bg_ref[...])
        pe = pe_ref[...]
        r2 = ALPHA * h1_ref[...] + ff_ref[...] + pe * sig
        y, xhat, rstd = _ln_fwd(r2, g_ref[...], b_ref[...])
        err = y - t_ref[...]
        dy = err * (1.0 / d)
        dr = _ln_bwd(dy, xhat, rstd, g_ref[...])
        dr_ref[...] = dr
        drb_ref[...] = dr.astype(bf16)
        dpe_ref[...] = (dr * sig).astype(bf16)
        dgp = dr * pe * sig * (1.0 - sig)
        dgp_ref[...] = dgp.astype(bf16)
        pg_ref[0:1, :] += jnp.sum(dy * xhat, 0, keepdims=True)
        pg_ref[1:2, :] += jnp.sum(dy, 0, keepdims=True)
        pg_ref[2:3, :] += jnp.sum(dgp, 0, keepdims=True)
        pg_ref[3:4, :] += 0.5 * jnp.sum(jnp.mean(err * err, -1, keepdims=True), 0, keepdims=True)

    return pl.pallas_call(
        body, name="ln2_loss", grid=(t // TOK_BLK,),
        in_specs=[_row_spec(d)] * 4 + [_vec_spec(1, d)] * 3 + [_row_spec(d)],
        out_specs=(_row_spec(d), _row_spec(d), _row_spec(d), _row_spec(d), _vec_spec(8, d)),
        out_shape=(jax.ShapeDtypeStruct((t, d), f32), jax.ShapeDtypeStruct((t, d), bf16), jax.ShapeDtypeStruct((t, d), bf16),
                   jax.ShapeDtypeStruct((t, d), bf16), jax.ShapeDtypeStruct((8, d), f32)),
        compiler_params=_params(("arbitrary",)),
    )(h1, ff, pe, gp, b_gate, g, b, target)


def _ln1_bwd(dr2, da, db, xhat, rstd, g):
    t, d = dr2.shape

    def body(dr2_ref, da_ref, db_ref, xh_ref, rs_ref, g_ref, dr_ref, drb_ref, pg_ref):
        i = pl.program_id(0)

        @pl.when(i == 0)
        def _():
            pg_ref[...] = jnp.zeros_like(pg_ref)

        dh = ALPHA * dr2_ref[...] + da_ref[...] + db_ref[...]
        xhat = xh_ref[...]
        dr = _ln_bwd(dh, xhat, rs_ref[:, 0:1], g_ref[...])
        dr_ref[...] = dr
        drb_ref[...] = dr.astype(bf16)
        pg_ref[0:1, :] += jnp.sum(dh * xhat, 0, keepdims=True)
        pg_ref[1:2, :] += jnp.sum(dh, 0, keepdims=True)

    return pl.pallas_call(
        body, name="ln1_bwd", grid=(t // TOK_BLK,),
        in_specs=[_row_spec(d)] * 4 + [_row_spec(LANES), _vec_spec(1, d)],
        out_specs=(_row_spec(d), _row_spec(d), _vec_spec(8, d)),
        out_shape=(jax.ShapeDtypeStruct((t, d), f32), jax.ShapeDtypeStruct((t, d), bf16), jax.ShapeDtypeStruct((8, d), f32)),
        compiler_params=_params(("arbitrary",)),
    )(dr2, da, db, xhat, rstd, g)


def _ln_in_bwd(x, dr1, dmm, g):
    t, d = x.shape

    def body(x_ref, dr1_ref, dmm_ref, g_ref, dx_ref, pg_ref):
        i = pl.program_id(0)

        @pl.when(i == 0)
        def _():
            pg_ref[...] = jnp.zeros_like(pg_ref)

        dh = ALPHA * dr1_ref[...] + dmm_ref[...]
        _, xhat, rstd = _ln_fwd(x_ref[...], g_ref[...], 0.0)
        dx_ref[...] = _ln_bwd(dh, xhat, rstd, g_ref[...])
        pg_ref[0:1, :] += jnp.sum(dh * xhat, 0, keepdims=True)
        pg_ref[1:2, :] += jnp.sum(dh, 0, keepdims=True)

    return pl.pallas_call(
        body, name="ln_in_bwd", grid=(t // TOK_BLK,),
        in_specs=[_row_spec(d)] * 3 + [_vec_spec(1, d)],
        out_specs=(_row_spec(d), _vec_spec(8, d)),
        out_shape=(jax.ShapeDtypeStruct((t, d), f32), jax.ShapeDtypeStruct((8, d), f32)),
        compiler_params=_params(("arbitrary",)),
    )(x, dr1, dmm, g)


def _tri(n, upper=False, strict=False):
    r = lax.broadcasted_iota(jnp.int32, (n, n), 0)
    c = lax.broadcasted_iota(jnp.int32, (n, n), 1)
    if upper:
        m = (c > r) if strict else (c >= r)
    else:
        m = (c < r) if strict else (c <= r)
    return jnp.where(m, 1.0, 0.0).astype(f32)


def _gate_values(x, bias, alog, lane):
    z = x + bias
    return jnp.where(lane < 4, _sigmoid(z), jnp.where(lane < 8, -jnp.exp(alog) * _softplus(z), jnp.where(lane < 16, -_softplus(-z), 0.0)))


def _gates(proj, bias_row, alog_row):
    t = proj.shape[0]
    nch = t // CHUNK

    def body(x_ref, bias_ref, alog_ref, gates_ref, gcum_ref, gcumt_ref):
        lane = _lane((t, LANES))
        gates = _gate_values(x_ref[...], bias_ref[...], alog_ref[...], lane)
        gates_ref[...] = gates
        g3 = gates.reshape(nch, CHUNK, LANES)
        tri = jnp.broadcast_to(_tri(CHUNK)[None], (nch, CHUNK, CHUNK))
        loc = jnp.einsum("bij,bjk->bik", tri, g3, precision=HI, preferred_element_type=f32)
        tot = jnp.sum(g3, axis=1)
        offs = _dot(_tri(nch, strict=True), tot)
        glob = loc + offs[:, None, :]
        lane3 = _lane((nch, CHUNK, LANES))
        gcum = jnp.where(lane3 < 4, g3, jnp.where(lane3 < 8, loc, glob)).reshape(t, LANES)
        gcum_ref[...] = gcum
        gcumt_ref[...] = gcum.T

    return pl.pallas_call(
        body, name="gates", grid=(1,),
        in_specs=[pl.BlockSpec((t, LANES), lambda i: (0, SEG_SMALL // LANES)), _vec_spec(1, LANES), _vec_spec(1, LANES)],
        out_specs=(pl.BlockSpec((t, LANES), lambda i: (0, 0)), pl.BlockSpec((t, LANES), lambda i: (0, 0)),
                   pl.BlockSpec((LANES, t), lambda i: (0, 0))),
        out_shape=(jax.ShapeDtypeStruct((t, LANES), f32), jax.ShapeDtypeStruct((t, LANES), f32), jax.ShapeDtypeStruct((LANES, t), f32)),
        compiler_params=_params(("arbitrary",)),
    )(proj, bias_row, alog_row)


def _gates_bwd(proj, bias_row, alog_row, gates, dgates, dccol, dct):
    t = proj.shape[0]
    nch = t // CHUNK

    def body(x_ref, bias_ref, alog_ref, gates_ref, dg_ref, dcc_ref, dct_ref, dx_ref, pg_ref):
        lane = _lane((t, LANES))
        d = dg_ref[...] + dcc_ref[...] + dct_ref[...].T
        d3 = d.reshape(nch, CHUNK, LANES)
        tri = jnp.broadcast_to(_tri(CHUNK, upper=True)[None], (nch, CHUNK, CHUNK))
        loc = jnp.einsum("bij,bjk->bik", tri, d3, precision=HI, preferred_element_type=f32)
        tot = jnp.sum(d3, axis=1)
        offs = _dot(_tri(nch, upper=True, strict=True), tot)
        glob = loc + offs[:, None, :]
        lane3 = _lane((nch, CHUNK, LANES))
        dpre = jnp.where(lane3 < 4, d3, jnp.where(lane3 < 8, loc, glob)).reshape(t, LANES)
        z = x_ref[...] + bias_ref[...]
        sg = _sigmoid(z)
        dx = jnp.where(lane < 4, dpre * sg * (1.0 - sg),
                       jnp.where(lane < 8, dpre * (-jnp.exp(alog_ref[...])) * sg, jnp.where(lane < 16, dpre * (1.0 - sg), 0.0)))
        dx_ref[...] = dx.astype(bf16)
        pg_ref[...] = jnp.zeros_like(pg_ref)
        pg_ref[0:1, :] = jnp.sum(dx, 0, keepdims=True)
        pg_ref[1:2, :] = jnp.sum(jnp.where((lane >= 4) & (lane < 8), dpre * gates_ref[...], 0.0), 0, keepdims=True)

    full = pl.BlockSpec((t, LANES), lambda i: (0, 0))
    return pl.pallas_call(
        body, name="gates_bwd", grid=(1,),
        in_specs=[pl.BlockSpec((t, LANES), lambda i: (0, SEG_SMALL // LANES)), _vec_spec(1, LANES), _vec_spec(1, LANES),
                  full, full, full, pl.BlockSpec((LANES, t), lambda i: (0, 0))],
        out_specs=(full, _vec_spec(8, LANES)),
        out_shape=(jax.ShapeDtypeStruct((t, LANES), bf16), jax.ShapeDtypeStruct((8, LANES), f32)),
        compiler_params=_params(("arbitrary",)),
    )(proj, bias_row, alog_row, gates, dgates, dccol, dct)


def _conv_act(u, cw, row, t):
    c = cw[3:4, :] * u
    for jj in range(CONV_W - 1):
        sh = CONV_W - 1 - jj
        c = c + cw[jj:jj + 1, :] * jnp.where(row >= sh, pltpu.roll(u, sh, axis=0), 0.0)
    return c


def _gdn_conv(proj, conv_w):
    t = proj.shape[0]
    nblk = GDN_QKV // LANES

    def body(u_ref, cw_ref, c_ref, y_ref):
        j = pl.program_id(0)
        row = lax.broadcasted_iota(jnp.int32, (t, LANES), 0)
        c = _conv_act(u_ref[...], cw_ref[...], row, t)
        c_ref[...] = c
        s = c * _sigmoid(c)
        r = lax.rsqrt(jnp.sum(s * s, -1, keepdims=True) + NORM_EPS)
        scale = jnp.where(j < GDN_HEADS, GDN_DK ** -0.5, 1.0)
        y_ref[...] = jnp.where(j < 2 * GDN_HEADS, s * (r * scale), s)

    blk = pl.BlockSpec((t, LANES), lambda j: (0, j))
    return pl.pallas_call(
        body, name="gdn_conv", grid=(nblk,),
        in_specs=[blk, pl.BlockSpec((CONV_W, LANES), lambda j: (0, j))],
        out_specs=(blk, blk),
        out_shape=(jax.ShapeDtypeStruct((t, GDN_QKV), f32), jax.ShapeDtypeStruct((t, GDN_QKV), f32)),
        compiler_params=_params(("parallel",)),
    )(proj, conv_w)


def _gdn_conv_bwd(proj, conv_w, c, dy):
    t = proj.shape[0]
    nblk = GDN_QKV // LANES

    def body(u_ref, cw_ref, c_ref, dy_ref, du_ref, dcw_ref):
        j = pl.program_id(0)
        row = lax.broadcasted_iota(jnp.int32, (t, LANES), 0)
        u = u_ref[...]
        cw = cw_ref[...]
        c = c_ref[...]
        dy = dy_ref[...]
        sg = _sigmoid(c)
        s = c * sg
        r = lax.rsqrt(jnp.sum(s * s, -1, keepdims=True) + NORM_EPS)
        n = s * r
        scale = jnp.where(j < GDN_HEADS, GDN_DK ** -0.5, 1.0)
        dn = dy * scale
        ds = jnp.where(j < 2 * GDN_HEADS, r * (dn - n * jnp.sum(dn * n, -1, keepdims=True)), dy)
        dc = ds * (sg * (1.0 + c * (1.0 - sg)))
        du = cw[3:4, :] * dc
        dcw_ref[...] = jnp.zeros_like(dcw_ref)
        dcw_ref[3:4, :] = jnp.sum(dc * u, 0, keepdims=True)
        for jj in range(CONV_W - 1):
            sh = CONV_W - 1 - jj
            du = du + cw[jj:jj + 1, :] * jnp.where(row < t - sh, pltpu.roll(dc, t - sh, axis=0), 0.0)
            dcw_ref[jj:jj + 1, :] = jnp.sum(dc * jnp.where(row >= sh, pltpu.roll(u, sh, axis=0), 0.0), 0, keepdims=True)
        du_ref[...] = du.astype(bf16)

    blk = pl.BlockSpec((t, LANES), lambda j: (0, j))
    return pl.pallas_call(
        body, name="gdn_conv_bwd", grid=(nblk,),
        in_specs=[blk, pl.BlockSpec((CONV_W, LANES), lambda j: (0, j)), blk, blk],
        out_specs=(blk, pl.BlockSpec((8, LANES), lambda j: (0, j))),
        out_shape=(jax.ShapeDtypeStruct((t, GDN_QKV), bf16), jax.ShapeDtypeStruct((8, GDN_QKV), f32)),
        compiler_params=_params(("parallel",)),
    )(proj, conv_w, c, dy)


def _chunk_masks():
    r = lax.broadcasted_iota(jnp.int32, (CHUNK, CHUNK), 0)
    c = lax.broadcasted_iota(jnp.int32, (CHUNK, CHUNK), 1)
    return r >= c, r > c, r == c


def _col_to_row(col, eye):
    return jnp.sum(jnp.where(eye, col, 0.0), axis=0, keepdims=True)


def _row_to_col(row, eye):
    return jnp.sum(jnp.where(eye, row, 0.0), axis=1, keepdims=True)


NN = (((1,), (0,)), ((), ()))
NT = (((1,), (1,)), ((), ()))
TN = (((0,), (0,)), ((), ()))
GDN_GROUP = 4


def _mx(a, b, dims=NN, passes=1):
    d = lambda p, q: lax.dot_general(p, q, dims, preferred_element_type=f32)
    ah, bh = a.astype(bf16), b.astype(bf16)
    if passes == 1:
        return d(ah, bh)
    al = (a - ah.astype(f32)).astype(bf16)
    bl = (b - bh.astype(f32)).astype(bf16)
    return d(ah, bh) + (d(ah, bl) + d(al, bh))


def _gdn_decay(gam, masks):
    causal, _, eye = masks
    return jnp.exp(jnp.where(causal, gam - _col_to_row(gam, eye), NEG))


def _gdn_local(y, gcum):
    t = y.shape[0]
    nch = t // CHUNK
    rows_blk = GDN_GROUP * CHUNK

    def body(y_ref, g_ref, u_ref, w_ref, qk_ref, tinv_ref):
        masks = _chunk_masks()
        _, strict, eye = masks
        for j in range(GDN_GROUP):
            rs = slice(j * CHUNK, (j + 1) * CHUNK)
            for h in range(GDN_HEADS):
                qn = y_ref[rs, h * LANES:(h + 1) * LANES]
                kn = y_ref[rs, 512 + h * LANES:512 + (h + 1) * LANES]
                v = y_ref[rs, 1024 + h * LANES:1024 + (h + 1) * LANES]
                beta = g_ref[rs, h:h + 1]
                gam = g_ref[rs, 4 + h:5 + h]
                dec = _gdn_decay(gam, masks)
                x = -jnp.where(strict, _mx(kn, kn, NT) * dec * beta, 0.0)
                tinv = jnp.where(eye, 1.0, 0.0) + x
                for _ in range(5):
                    x = _mx(x, x, NN, 3)
                    tinv = tinv + _mx(tinv, x, NN, 3)
                e = jnp.exp(gam)
                u_ref[rs, h * LANES:(h + 1) * LANES] = _mx(tinv, beta * v)
                w_ref[rs, h * LANES:(h + 1) * LANES] = _mx(tinv, (beta * e) * kn)
                qk_ref[j, h] = _mx(qn, kn, NT) * dec
                tinv_ref[j, h] = tinv

    mat = pl.BlockSpec((GDN_GROUP, GDN_HEADS, CHUNK, CHUNK), lambda n: (n, 0, 0, 0))
    return pl.pallas_call(
        body, name="gdn_local", grid=(nch // GDN_GROUP,),
        in_specs=[pl.BlockSpec((rows_blk, GDN_QKV), lambda n: (n, 0)), pl.BlockSpec((rows_blk, LANES), lambda n: (n, 0))],
        out_specs=(pl.BlockSpec((rows_blk, 512), lambda n: (n, 0)), pl.BlockSpec((rows_blk, 512), lambda n: (n, 0)), mat, mat),
        out_shape=(jax.ShapeDtypeStruct((t, 512), f32), jax.ShapeDtypeStruct((t, 512), f32),
                   jax.ShapeDtypeStruct((nch, GDN_HEADS, CHUNK, CHUNK), f32), jax.ShapeDtypeStruct((nch, GDN_HEADS, CHUNK, CHUNK), f32)),
        compiler_params=_params(("parallel",)),
    )(y, gcum)


def _gdn_fwd(y, gcum, u, w, qk):
    t = y.shape[0]
    nch = t // CHUNK

    def body(y_ref, g_ref, u_ref, w_ref, qk_ref, o_ref, sall_ref, s_ref):
        @pl.when(pl.program_id(0) == 0)
        def _():
            s_ref[...] = jnp.zeros_like(s_ref)

        for h in range(GDN_HEADS):
            sl = slice(h * LANES, (h + 1) * LANES)
            gam = g_ref[:, 4 + h:5 + h]
            gam_last = gam[CHUNK - 1:CHUNK, :]
            s = s_ref[h]
            sall_ref[0, h] = s
            vn = u_ref[:, sl] - _mx(w_ref[:, sl], s)
            o_ref[:, sl] = _mx(y_ref[:, sl] * jnp.exp(gam), s) + _mx(qk_ref[0, h], vn)
            kd = y_ref[:, 512 + h * LANES:512 + (h + 1) * LANES] * jnp.exp(gam_last - gam)
            s_ref[h] = jnp.exp(gam_last) * s + _mx(kd, vn, TN)

    row = lambda width: pl.BlockSpec((CHUNK, width), lambda n: (n, 0))
    return pl.pallas_call(
        body, name="gdn_fwd", grid=(nch,),
        in_specs=[row(GDN_QKV), row(LANES), row(512), row(512), pl.BlockSpec((1, GDN_HEADS, CHUNK, CHUNK), lambda n: (n, 0, 0, 0))],
        out_specs=(row(512), pl.BlockSpec((1, GDN_HEADS, LANES, LANES), lambda n: (n, 0, 0, 0))),
        out_shape=(jax.ShapeDtypeStruct((t, 512), f32), jax.ShapeDtypeStruct((nch, GDN_HEADS, LANES, LANES), f32)),
        scratch_shapes=[pltpu.VMEM((GDN_HEADS, LANES, LANES), f32)],
        compiler_params=_params(("arbitrary",)),
    )(y, gcum, u, w, qk)


def _gdn_bwd(y, gcum, u_all, w_all, qk_all, tinv_all, sall, do):
    t = y.shape[0]
    nch = t // CHUNK

    def body(y_ref, g_ref, u_ref, w_ref, qk_ref, tinv_ref, sall_ref, do_ref, dy_ref, dg_ref, ds_ref):
        @pl.when(pl.program_id(0) == 0)
        def _():
            ds_ref[...] = jnp.zeros_like(ds_ref)

        masks = _chunk_masks()
        causal, strict, eye = masks
        lane = _lane((CHUNK, LANES))
        row = lax.broadcasted_iota(jnp.int32, (CHUNK, 1), 0)
        dgates = jnp.zeros((CHUNK, LANES), f32)
        for h in range(GDN_HEADS):
            sl = slice(h * LANES, (h + 1) * LANES)
            qn = y_ref[:, sl]
            kn = y_ref[:, 512 + h * LANES:512 + (h + 1) * LANES]
            v = y_ref[:, 1024 + h * LANES:1024 + (h + 1) * LANES]
            beta = g_ref[:, h:h + 1]
            gam = g_ref[:, 4 + h:5 + h]
            gam_last = gam[CHUNK - 1:CHUNK, :]
            dec = _gdn_decay(gam, masks)
            e = jnp.exp(gam)
            f = jnp.exp(gam_last - gam)
            gl = jnp.exp(gam_last)
            kkd = _mx(kn, kn, NT) * dec
            u, w, qk, tinv = u_ref[:, sl], w_ref[:, sl], qk_ref[0, h], tinv_ref[0, h]
            qd, kd = qn * e, kn * f
            s = sall_ref[0, h]
            dsn = ds_ref[h]
            d_o = do_ref[:, sl]
            vn = u - _mx(w, s)
            dvn = _mx(qk, d_o, TN) + _mx(kd, dsn)
            dqk = jnp.where(causal, _mx(d_o, vn, NT), 0.0)
            dqd = _mx(d_o, s, NT)
            dkd = _mx(vn, dsn, NT)
            dgl = jnp.sum(jnp.sum(dsn * s, axis=1, keepdims=True), axis=0, keepdims=True)
            dw = -_mx(dvn, s, NT)
            ds_ref[h] = _mx(qd, d_o, TN) - _mx(w, dvn, TN) + gl * dsn
            dru = _mx(tinv, dvn, TN)
            drw = _mx(tinv, dw, TN)
            dn = jnp.where(strict, -(_mx(dru, u, NT) + _mx(drw, w, NT)), 0.0)
            dv = beta * dru
            drw_k = jnp.sum(drw * kn, axis=1, keepdims=True)
            dbeta = jnp.sum(dru * v, axis=1, keepdims=True) + e * drw_k + jnp.sum(dn * kkd, axis=1, keepdims=True)
            de = beta * drw_k
            dk = (beta * e) * drw
            dkk = dn * beta * dec
            dk = dk + _mx(dkk, kn) + _mx(dkk, kn, TN)
            dqkr = dqk * dec
            dq = _mx(dqkr, kn) + dqd * e
            dk = dk + _mx(dqkr, qn, TN) + dkd * f
            m = dn * (kkd * beta) + dqk * qk
            dgam = jnp.sum(m, axis=1, keepdims=True) - _row_to_col(jnp.sum(m, axis=0, keepdims=True), eye)
            de = de + jnp.sum(dqd * qn, axis=1, keepdims=True)
            df = jnp.sum(dkd * kn, axis=1, keepdims=True)
            dgam = dgam + de * e - df * f
            dgam_last = jnp.sum(df * f, axis=0, keepdims=True) + dgl * gl
            dgam = dgam + jnp.where(row == CHUNK - 1, dgam_last, 0.0)
            dy_ref[:, sl] = dq
            dy_ref[:, 512 + h * LANES:512 + (h + 1) * LANES] = dk
            dy_ref[:, 1024 + h * LANES:1024 + (h + 1) * LANES] = dv
            dgates = dgates + jnp.where(lane == h, dbeta, 0.0) + jnp.where(lane == 4 + h, dgam, 0.0)
        dg_ref[...] = dgates

    rev = lambda width: pl.BlockSpec((CHUNK, width), lambda n: (nch - 1 - n, 0))
    mat = lambda d: pl.BlockSpec((1, GDN_HEADS, d, d), lambda n: (nch - 1 - n, 0, 0, 0))
    return pl.pallas_call(
        body, name="gdn_bwd", grid=(nch,),
        in_specs=[rev(GDN_QKV), rev(LANES), rev(512), rev(512), mat(CHUNK), mat(CHUNK), mat(LANES), rev(512)],
        out_specs=(rev(GDN_QKV), rev(LANES)),
        out_shape=(jax.ShapeDtypeStruct((t, GDN_QKV), f32), jax.ShapeDtypeStruct((t, LANES), f32)),
        scratch_shapes=[pltpu.VMEM((GDN_HEADS, LANES, LANES), f32)],
        compiler_params=_params(("arbitrary",)),
    )(y, gcum, u_all, w_all, qk_all, tinv_all, sall, do)


def _fox_head(q_ref, gcum_ref, h):
    pr = h // 2
    lo = (h % 2) * FOX_DH
    lane = _lane((FOX_BQ, LANES))
    mask = (lane >= lo) & (lane < lo + FOX_DH)
    qm = jnp.where(mask, q_ref[:, pr * LANES:(pr + 1) * LANES], 0.0).astype(bf16)
    return mask, qm, gcum_ref[:, 8 + h:9 + h]


def _fox_scores(qm, c_col, k_ref, gcumt_ref, h, i, j):
    pr = h // 2
    ks = pl.multiple_of(j * FOX_BQ, FOX_BQ)
    kp = k_ref[pl.ds(ks, FOX_BQ), pr * LANES:(pr + 1) * LANES].astype(bf16)
    s = _dot_nt(qm, kp, None) * (FOX_DH ** -0.5) + c_col - gcumt_ref[8 + h:9 + h, pl.ds(ks, FOX_BQ)]
    rows = i * FOX_BQ + lax.broadcasted_iota(jnp.int32, (FOX_BQ, FOX_BQ), 0)
    cols = ks + lax.broadcasted_iota(jnp.int32, (FOX_BQ, FOX_BQ), 1)
    return jnp.where(cols <= rows, s, NEG), kp, ks


def _fox_fwd(proj, gcum, gcumt):
    t = proj.shape[0]

    def body(q_ref, k_ref, v_ref, gcum_ref, gcumt_ref, o_ref, lse_ref):
        i = pl.program_id(0)
        lane = _lane((FOX_BQ, LANES))
        lse_all = jnp.zeros((FOX_BQ, LANES), f32)
        for pr in range(FOX_HEADS // 2):
            sl = slice(pr * LANES, (pr + 1) * LANES)
            o_pair = jnp.zeros((FOX_BQ, LANES), f32)
            for h in (2 * pr, 2 * pr + 1):
                mask, qm, c_col = _fox_head(q_ref, gcum_ref, h)

                def kv_step(j, carry, qm=qm, c_col=c_col, h=h, sl=sl):
                    m, l, acc = carry
                    s, _, ks = _fox_scores(qm, c_col, k_ref, gcumt_ref, h, i, j)
                    m_new = jnp.maximum(m, jnp.max(s, axis=1, keepdims=True))
                    alpha = jnp.exp(m - m_new)
                    p = jnp.exp(s - m_new)
                    vp = v_ref[pl.ds(ks, FOX_BQ), sl].astype(bf16)
                    return m_new, alpha * l + jnp.sum(p, axis=1, keepdims=True), alpha * acc + _dot(p.astype(bf16), vp, None)

                init = (jnp.full((FOX_BQ, 1), NEG, f32), jnp.zeros((FOX_BQ, 1), f32), jnp.zeros((FOX_BQ, LANES), f32))
                m, l, acc = lax.fori_loop(0, i + 1, kv_step, init)
                o_pair = jnp.where(mask, acc * (1.0 / l), o_pair)
                lse_all = jnp.where(lane == h, m + jnp.log(l), lse_all)
            o_ref[:, sl] = o_pair
        lse_ref[...] = lse_all

    qblk = lambda col: pl.BlockSpec((FOX_BQ, 512), lambda i: (i, col))
    full = lambda col: pl.BlockSpec((t, 512), lambda i: (0, col))
    c0 = SEG_FOX // 512
    return pl.pallas_call(
        body, name="fox_fwd", grid=(t // FOX_BQ,),
        in_specs=[qblk(c0), full(c0 + 1), full(c0 + 2), pl.BlockSpec((FOX_BQ, LANES), lambda i: (i, 0)),
                  pl.BlockSpec((LANES, t), lambda i: (0, 0))],
        out_specs=(pl.BlockSpec((FOX_BQ, 512), lambda i: (i, 0)), pl.BlockSpec((FOX_BQ, LANES), lambda i: (i, 0))),
        out_shape=(jax.ShapeDtypeStruct((t, 512), f32), jax.ShapeDtypeStruct((t, LANES), f32)),
        compiler_params=_params(("parallel",)),
    )(proj, proj, proj, gcum, gcumt)


def _fox_bwd(proj, gcum, gcumt, o, lse, do):
    t = proj.shape[0]

    def body(q_ref, k_ref, v_ref, gcum_ref, gcumt_ref, o_ref, lse_ref, do_ref, dq_ref, dk_ref, dv_ref, dcc_ref, dct_ref):
        i = pl.program_id(0)

        @pl.when(i == 0)
        def _():
            dk_ref[...] = jnp.zeros_like(dk_ref)
            dv_ref[...] = jnp.zeros_like(dv_ref)
            dct_ref[...] = jnp.zeros_like(dct_ref)

        lane = _lane((FOX_BQ, LANES))
        dcc = jnp.zeros((FOX_BQ, LANES), f32)
        scale = FOX_DH ** -0.5
        for pr in range(FOX_HEADS // 2):
            sl = slice(pr * LANES, (pr + 1) * LANES)
            dq_pair = jnp.zeros((FOX_BQ, LANES), f32)
            for h in (2 * pr, 2 * pr + 1):
                mask, qm, c_col = _fox_head(q_ref, gcum_ref, h)
                lse = lse_ref[:, h:h + 1]
                dom = jnp.where(mask, do_ref[:, sl], 0.0)
                delta = jnp.sum(dom * o_ref[:, sl], axis=1, keepdims=True)
                domb = dom.astype(bf16)

                def kv_step(j, carry, qm=qm, c_col=c_col, h=h, sl=sl, lse=lse, delta=delta, domb=domb):
                    dq_acc, dcc_acc = carry
                    s, kp, ks = _fox_scores(qm, c_col, k_ref, gcumt_ref, h, i, j)
                    p = jnp.exp(s - lse)
                    vp = v_ref[pl.ds(ks, FOX_BQ), sl].astype(bf16)
                    ds = p * (_dot_nt(domb, vp, None) - delta)
                    dsb = ds.astype(bf16)
                    dv_ref[pl.ds(ks, FOX_BQ), sl] += _dot_tn(p.astype(bf16), domb, None)
                    dk_ref[pl.ds(ks, FOX_BQ), sl] += _dot_tn(dsb, qm, None) * scale
                    dct_ref[8 + h:9 + h, pl.ds(ks, FOX_BQ)] += -jnp.sum(ds, axis=0, keepdims=True)
                    return dq_acc + _dot(dsb, kp, None), dcc_acc + jnp.sum(ds, axis=1, keepdims=True)

                dq_h, dcc_h = lax.fori_loop(0, i + 1, kv_step, (jnp.zeros((FOX_BQ, LANES), f32), jnp.zeros((FOX_BQ, 1), f32)))
                dq_pair = jnp.where(mask, dq_h * scale, dq_pair)
                dcc = jnp.where(lane == 8 + h, dcc_h, dcc)
            dq_ref[:, sl] = dq_pair.astype(bf16)
        dcc_ref[...] = dcc

    qblk = lambda col: pl.BlockSpec((FOX_BQ, 512), lambda i: (i, col))
    full = lambda col: pl.BlockSpec((t, 512), lambda i: (0, col))
    rblk = pl.BlockSpec((FOX_BQ, LANES), lambda i: (i, 0))
    c0 = SEG_FOX // 512
    return pl.pallas_call(
        body, name="fox_bwd", grid=(t // FOX_BQ,),
        in_specs=[qblk(c0), full(c0 + 1), full(c0 + 2), rblk, pl.BlockSpec((LANES, t), lambda i: (0, 0)),
                  qblk(0), rblk, qblk(0)],
        out_specs=(qblk(0), full(0), full(0), rblk, pl.BlockSpec((LANES, t), lambda i: (0, 0))),
        out_shape=(jax.ShapeDtypeStruct((t, 512), bf16), jax.ShapeDtypeStruct((t, 512), f32), jax.ShapeDtypeStruct((t, 512), f32),
                   jax.ShapeDtypeStruct((t, LANES), f32), jax.ShapeDtypeStruct((LANES, t), f32)),
        compiler_params=_params(("arbitrary",)),
    )(proj, proj, proj, gcum, gcumt, o, lse, do)


def _row(v, width=None):
    v = v.reshape(1, -1).astype(f32)
    if width is not None and v.shape[1] < width:
        v = jnp.pad(v, ((0, 0), (0, width - v.shape[1])))
    return v


def _device_grads(x, p, target, small, w_cat, conv_w, w_out, w_up, w_down, w_ple, w_gate):
    z4 = jnp.zeros((4,), f32)
    bias_row = _row(jnp.concatenate([z4, small["dt_bias"].reshape(-1), small["b_f"].reshape(-1)]), LANES)
    alog_row = _row(jnp.concatenate([z4, small["a_log"].reshape(-1)]), LANES)
    g_gdn = _row(small["gdn_norm_g"])
    g_fox2 = _row(jnp.tile(small["fox_norm_g"].reshape(-1), 2))
    pb = p.astype(bf16)

    h0, h0b = _ln_in(x, _row(small["ln_in_g"]), _row(small["ln_in_b"]))
    proj = _mm(h0b, w_cat, "nn", 256, D_CAT, "mm_proj")
    gates, gcum, gcumt = _gates(proj, bias_row, alog_row)
    conv_c, qkv_n = _gdn_conv(proj, conv_w)
    gu, gw, gqk, gtinv = _gdn_local(qkv_n, gcum)
    o_gdn, sall = _gdn_fwd(qkv_n, gcum, gu, gw, gqk)
    o_fox, lse = _fox_fwd(proj, gcum, gcumt)
    attn = _attn_post(o_gdn, proj, o_fox, g_gdn, g_fox2)
    mix = _mm(attn, w_out, "nn", 512, D_MODEL, "mm_mix")
    h1, h1b, xhat1, rstd1 = _ln1(h0, mix, _row(small["ln1_g"]), _row(small["ln1_b"]))
    up, act = _mm(h1b, w_up, "nn", 256, 1024, "mm_up", epi="relu2", shards=N_CHIPS)
    ff = _mm(act, w_down, "nn", 256, D_MODEL, "mm_down")
    gp = _mm(h1b, w_gate, "nn", 512, D_MODEL, "mm_gate")
    pe = _mm(pb, w_ple, "nn", 512, D_MODEL // N_CHIPS, "mm_ple", shards=N_CHIPS)
    dr2, dr2b, dpe, dgp, pg2 = _ln2_loss(h1, ff, pe, gp, _row(small["b_ple_gate"]), _row(small["ln2_g"]), _row(small["ln2_b"]), target)

    dup = _mm(dr2b, w_down, "nt", 256, 2048, "mm_dact", epi="relu2_bwd", extra=up)
    g_down = _mm(act, dr2b, "tn", 1024, D_MODEL, "mm_gdown")
    dh1_a = _mm(dup, w_up, "nt", 256, D_MODEL, "mm_dh1a", shards=N_CHIPS)
    g_up = _mm(h1b, dup, "tn", 1024, 1024, "mm_gup", shards=N_CHIPS)
    dh1_b = _mm(dgp, w_gate, "nt", 512, D_MODEL, "mm_dh1b")
    g_gate = _mm(h1b, dgp, "tn", 1024, D_MODEL, "mm_ggate")
    g_ple = _mm(pb, dpe, "tn", D_PLE, D_MODEL // N_CHIPS, "mm_gple", shards=N_CHIPS)
    dr1, dr1b, pg1 = _ln1_bwd(dr2, dh1_a, dh1_b, xhat1, rstd1, _row(small["ln1_g"]))
    dattn = _mm(dr1b, w_out, "nt", 512, D_MODEL, "mm_dattn")
    g_out = _mm(attn, dr1b, "tn", 1024, D_MODEL, "mm_gout")
    do_gdn, dz, do_fox, pga = _attn_post_bwd(dattn, o_gdn, proj, o_fox, g_gdn, g_fox2)
    dfq, dfk, dfv, dccol, dct = _fox_bwd(proj, gcum, gcumt, o_fox, lse, do_fox)
    dqkv_n, dgates = _gdn_bwd(qkv_n, gcum, gu, gw, gqk, gtinv, sall, do_gdn)
    dsmall, pgg = _gates_bwd(proj, bias_row, alog_row, gates, dgates, dccol, dct)
    du, g_conv8 = _gdn_conv_bwd(proj, conv_w, conv_c, dqkv_n)
    t = x.shape[0]
    dproj = jnp.concatenate([du, dz, dfq, dfk.astype(bf16), dfv.astype(bf16), dsmall, jnp.zeros((t, D_CAT - SEG_SMALL - LANES), bf16)], axis=1)
    dh0_mm = _mm(dproj, w_cat, "nt", 256, D_MODEL, "mm_dh0")
    g_cat = _mm(h0b, dproj, "tn", 1024, 1280, "mm_gcat")
    grad_x, pg0 = _ln_in_bwd(x, dr1, dh0_mm, _row(small["ln_in_g"]))

    g_fox = pga[1, :FOX_DH] + pga[1, FOX_DH:]
    small_grads = dict(
        ln_in_g=pg0[0], ln_in_b=pg0[1], ln1_g=pg1[0], ln1_b=pg1[1], b_ple_gate=pg2[2], ln2_g=pg2[0], ln2_b=pg2[1],
        gdn_norm_g=pga[0], fox_norm_g=g_fox, a_log=pgg[1, 4:8], dt_bias=pgg[0, 4:8], b_f=pgg[0, 8:16], loss=pg2[3, 0:1])
    big_grads = dict(w_cat=g_cat, conv_w=g_conv8[:CONV_W], w_out=g_out, w_up=g_up, w_down=g_down, w_ple=g_ple, w_gate=g_gate)
    return grad_x, big_grads, small_grads


ANY = pl.BlockSpec(memory_space=pl.ANY)
CONV_PKT_ROWS = 16


def _mesh_pos():
    return lax.axis_index("x"), lax.axis_index("y"), lax.axis_index("c")


def _other_chips(x, y):
    return [(1 - x, y), (x, 1 - y), (1 - x, 1 - y)]


def _rcopy(src, dst, send_sem, recv_sem, dev):
    return pltpu.make_async_remote_copy(src_ref=src, dst_ref=dst, send_sem=send_sem, recv_sem=recv_sem,
                                        device_id=dev, device_id_type=MESH)


def _gather_weights(bufs, chunks, conv4):
    nw = len(bufs)
    base, n_ici = [], 0
    for nch in chunks:
        base.append(n_ici)
        n_ici += 3 * nch

    def body(*refs):
        outs = refs[nw + 1:2 * nw + 1]
        conv_ref = refs[2 * nw + 1]
        send_sems, recv_sems, csend, crecv = refs[2 * nw + 2:]
        x, y, c = _mesh_pos()
        q = 2 * x + y
        chips = _other_chips(x, y)
        sib = (x, y, 1 - c)

        def piece(i, slot, hf, ch):
            half = bufs[i].shape[1] // 2
            cr = half // chunks[i]
            return outs[i].at[slot, pl.ds(hf * half + ch * cr, cr)]

        started = []
        for i in range(nw):
            for ch in range(chunks[i]):
                for k, chip in enumerate(chips):
                    s = base[i] + k * chunks[i] + ch
                    cp = _rcopy(piece(i, q, c, ch), piece(i, q, c, ch), send_sems.at[s], recv_sems.at[s], (*chip, c))
                    cp.start()
                    started.append(cp)
        for k, chip in enumerate(chips):
            cp = _rcopy(conv_ref.at[q], conv_ref.at[q], csend.at[k], crecv.at[k], (*chip, c))
            cp.start()
            started.append(cp)
        for i in range(nw):
            for ch in range(chunks[i]):
                for k, chip in enumerate(chips):
                    s = base[i] + k * chunks[i] + ch
                    landed = piece(i, 2 * chip[0] + chip[1], c, ch)
                    _rcopy(landed, landed, send_sems.at[s], recv_sems.at[s], (*chip, c)).wait_recv()
                    cp = _rcopy(landed, landed, send_sems.at[n_ici + s], recv_sems.at[n_ici + s], sib)
                    cp.start()
                    started.append(cp)
        for i in range(nw):
            for ch in range(chunks[i]):
                for k, chip in enumerate(chips):
                    s = base[i] + k * chunks[i] + ch
                    passed = piece(i, 2 * chip[0] + chip[1], 1 - c, ch)
                    _rcopy(passed, passed, send_sems.at[n_ici + s], recv_sems.at[n_ici + s], sib).wait_recv()
        for k, chip in enumerate(chips):
            theirs = conv_ref.at[2 * chip[0] + chip[1]]
            _rcopy(theirs, theirs, csend.at[k], crecv.at[k], (*chip, c)).wait_recv()
        for cp in started:
            cp.wait_send()

    args = list(bufs) + [conv4]
    return pl.pallas_call(
        body, name="gather_weights", out_shape=tuple(jax.ShapeDtypeStruct(a.shape, a.dtype) for a in args),
        in_specs=[ANY] * (nw + 1), out_specs=(ANY,) * (nw + 1), input_output_aliases={i: i for i in range(nw + 1)},
        scratch_shapes=[pltpu.SemaphoreType.DMA((2 * n_ici,)), pltpu.SemaphoreType.DMA((2 * n_ici,)), pltpu.SemaphoreType.DMA((3,)),
                        pltpu.SemaphoreType.DMA((3,))],
    )(*args)


def _exchange_pairs(gs, small):
    nw = len(gs)

    def body(*refs):
        g_refs, small_ref = refs[:nw], refs[nw]
        b1_refs, all_ref = refs[nw + 1:2 * nw + 1], refs[2 * nw + 1]
        ssem, rsem, s2, r2, local_sem = refs[2 * nw + 2:]
        x, y, c = _mesh_pos()
        me = 4 * x + 2 * y + c
        sib = (x, y, 1 - c)
        own = pltpu.make_async_copy(small_ref, all_ref.at[me], local_sem)
        own.start()

        def pair_copy(i, d):
            half = gs[i].shape[1] // 2
            return _rcopy(g_refs[i].at[d, pl.ds((1 - c) * half, half)], b1_refs[i].at[d], ssem.at[i * N_CHIPS + d],
                          rsem.at[i * N_CHIPS + d], sib)

        started = []
        for i in range(nw):
            for d in range(N_CHIPS):
                cp = pair_copy(i, d)
                cp.start()
                started.append(cp)
        peers = []
        for r in range(1, 8):
            fx, fy, fc = (r >> 2) & 1, (r >> 1) & 1, r & 1
            peers.append((1 - x if fx else x, 1 - y if fy else y, 1 - c if fc else c))
        for r, peer in enumerate(peers):
            cp = _rcopy(small_ref, all_ref.at[me], s2.at[r], r2.at[r], peer)
            cp.start()
            started.append(cp)
        for i in range(nw):
            for d in range(N_CHIPS):
                pair_copy(i, d).wait_recv()
        for r, peer in enumerate(peers):
            _rcopy(small_ref, all_ref.at[4 * peer[0] + 2 * peer[1] + peer[2]], s2.at[r], r2.at[r], peer).wait_recv()
        for cp in started:
            cp.wait_send()
        own.wait()

    out_shape = tuple(jax.ShapeDtypeStruct((N_CHIPS, g.shape[1] // 2, g.shape[2]), g.dtype) for g in gs)
    return pl.pallas_call(
        body, name="exchange_pairs", out_shape=out_shape + (jax.ShapeDtypeStruct((8,) + small.shape, small.dtype),),
        in_specs=[ANY] * (nw + 1), out_specs=(ANY,) * (nw + 1),
        scratch_shapes=[pltpu.SemaphoreType.DMA((nw * N_CHIPS,)), pltpu.SemaphoreType.DMA((nw * N_CHIPS,)), pltpu.SemaphoreType.DMA((7,)),
                        pltpu.SemaphoreType.DMA((7,)), pltpu.SemaphoreType.DMA],
    )(*gs, small)


def _exchange_chips(a4s):
    nw = len(a4s)

    def body(*refs):
        a_refs, b2_refs = refs[:nw], refs[nw:2 * nw]
        ssem, rsem = refs[2 * nw:]
        x, y, c = _mesh_pos()
        chips = _other_chips(x, y)

        def chip_copy(i, k):
            chip = chips[k]
            return _rcopy(a_refs[i].at[2 * chip[0] + chip[1]], b2_refs[i].at[k], ssem.at[3 * i + k], rsem.at[3 * i + k], (*chip, c))

        started = []
        for i in range(nw):
            for k in range(3):
                cp = chip_copy(i, k)
                cp.start()
                started.append(cp)
        for i in range(nw):
            for k in range(3):
                chip_copy(i, k).wait_recv()
        for cp in started:
            cp.wait_send()

    return pl.pallas_call(
        body, name="exchange_chips", out_shape=tuple(jax.ShapeDtypeStruct((3,) + a.shape[1:], a.dtype) for a in a4s),
        in_specs=[ANY] * nw, out_specs=(ANY,) * nw,
        scratch_shapes=[pltpu.SemaphoreType.DMA((3 * nw,)), pltpu.SemaphoreType.DMA((3 * nw,))],
    )(*a4s)


def _share_halves(rs):
    nw = len(rs)

    def body(*refs):
        outs = refs[nw:2 * nw]
        ssem, rsem = refs[2 * nw:]
        x, y, c = _mesh_pos()
        sib = (x, y, 1 - c)
        started = []
        for i in range(nw):
            half = rs[i].shape[0] // 2
            mine = outs[i].at[pl.ds(c * half, half)]
            cp = _rcopy(mine, mine, ssem.at[i], rsem.at[i], sib)
            cp.start()
            started.append(cp)
        for i in range(nw):
            half = rs[i].shape[0] // 2
            theirs = outs[i].at[pl.ds((1 - c) * half, half)]
            _rcopy(theirs, theirs, ssem.at[i], rsem.at[i], sib).wait_recv()
        for cp in started:
            cp.wait_send()

    return pl.pallas_call(
        body, name="share_halves", out_shape=tuple(jax.ShapeDtypeStruct(r.shape, r.dtype) for r in rs),
        in_specs=[ANY] * nw, out_specs=(ANY,) * nw, input_output_aliases={i: i for i in range(nw)},
        scratch_shapes=[pltpu.SemaphoreType.DMA((nw,)), pltpu.SemaphoreType.DMA((nw,))],
    )(*rs)


ADD_ROWS = 256


def _add_pair(g4, b1, qc_idx, name):
    _, half, cols = b1.shape
    rb = min(ADD_ROWS, half)
    nb = half // rb

    def body(qc_ref, g_ref, b_ref, o_ref, ob_ref):
        a = g_ref[...] + b_ref[...]
        o_ref[...] = a
        ob_ref[...] = a.astype(bf16)

    blk = (1, rb, cols)
    out = pl.BlockSpec(blk, lambda d, i, qc: (d, i, 0))
    return pl.pallas_call(
        body, name=name,
        grid_spec=pltpu.PrefetchScalarGridSpec(
            num_scalar_prefetch=1, grid=(N_CHIPS, nb),
            in_specs=[pl.BlockSpec(blk, lambda d, i, qc: (d, qc[1] * nb + i, 0)), out],
            out_specs=(out, out)),
        out_shape=(jax.ShapeDtypeStruct(b1.shape, f32), jax.ShapeDtypeStruct(b1.shape, bf16)),
        compiler_params=_params(("parallel", "parallel")),
    )(qc_idx, g4, b1)


def _add_chips(a4, b2, qc_idx, name):
    _, half, cols = a4.shape
    rb = min(ADD_ROWS, half)
    nb = half // rb

    def body(qc_ref, a_ref, b_ref, o_ref):
        o_ref[...] = ((a_ref[0] + b_ref[0].astype(f32)) + b_ref[1].astype(f32)) + b_ref[2].astype(f32)

    return pl.pallas_call(
        body, name=name,
        grid_spec=pltpu.PrefetchScalarGridSpec(
            num_scalar_prefetch=1, grid=(nb,),
            in_specs=[pl.BlockSpec((1, rb, cols), lambda i, qc: (qc[0], i, 0)), pl.BlockSpec((3, rb, cols), lambda i, qc: (0, i, 0))],
            out_specs=pl.BlockSpec((rb, cols), lambda i, qc: (qc[1] * nb + i, 0))),
        out_shape=jax.ShapeDtypeStruct((2 * half, cols), f32),
        compiler_params=_params(("parallel",)),
    )(qc_idx, a4, b2)


def _adamw_math(w, g, m, v):
    m = ADAM_B1 * m + (1.0 - ADAM_B1) * g
    v = ADAM_B2 * v + (1.0 - ADAM_B2) * (g * g)
    m_hat = m / (1.0 - ADAM_B1 ** ADAM_STEP)
    v_hat = v / (1.0 - ADAM_B2 ** ADAM_STEP)
    return -ADAM_LR * (m_hat / (jnp.sqrt(v_hat) + ADAM_EPS) + ADAM_WD * w), m, v


def _adamw(w, g, m, v, name):
    rows, cols = w.shape
    rb = ADD_ROWS if rows % ADD_ROWS == 0 else rows

    def body(w_ref, g_ref, m_ref, v_ref, go_ref, d_ref, mo_ref, vo_ref):
        g = g_ref[...]
        go_ref[...] = g
        d_ref[...], mo_ref[...], vo_ref[...] = _adamw_math(w_ref[...], g, m_ref[...], v_ref[...])

    blk = pl.BlockSpec((rb, cols), lambda i: (i, 0))
    return pl.pallas_call(
        body, name=name, grid=(rows // rb,), in_specs=[blk] * 4, out_specs=(blk,) * 4,
        out_shape=(jax.ShapeDtypeStruct(w.shape, f32),) * 4, compiler_params=_params(("parallel",)),
    )(w, g, m, v)


def _small_sum_adamw(all_pkts, w, m, v):
    def body(a_ref, w_ref, m_ref, v_ref, g_ref, d_ref, mo_ref, vo_ref):
        g = a_ref[0]
        for r in range(1, 8):
            g = g + a_ref[r]
        g_ref[...] = g
        d_ref[...], mo_ref[...], vo_ref[...] = _adamw_math(w_ref[...], g, m_ref[...], v_ref[...])

    return pl.pallas_call(body, name="small_sum_adamw", out_shape=(jax.ShapeDtypeStruct(w.shape, f32),) * 4)(all_pkts, w, m, v)


SHARDED = (("w_in", (D_MODEL, D_IN // N_CHIPS), 2), ("w_out", (D_MODEL // N_CHIPS, D_MODEL), 1), ("w_up", (D_MODEL, D_FF // N_CHIPS), 2),
           ("w_ple_gate", (D_MODEL // N_CHIPS, D_MODEL), 1), ("w_ple", (D_PLE, D_MODEL // N_CHIPS), 1),
           ("w_down", (D_FF // N_CHIPS, D_MODEL), 2))
SMALL_LAYOUT = (("ln_in_g", 0, 1024), ("ln_in_b", 8, 1024), ("ln1_g", 16, 1024), ("ln1_b", 24, 1024), ("b_ple_gate", 32, 1024),
                ("ln2_g", 40, 1024), ("ln2_b", 48, 1024), ("gdn_norm_g", 56, 128), ("fox_norm_g", 57, 64), ("a_log", 58, 4),
                ("dt_bias", 59, 4), ("b_f", 60, 8), ("loss", 61, 1))
SMALL_CONV_ROW = 64
SMALL_ROWS = 128


def _pack_small(vals, conv=None):
    rows = []
    nxt = 0
    for n, r0, size in SMALL_LAYOUT:
        assert r0 == nxt
        v = vals[n].reshape(-1).astype(f32) if n in vals else jnp.zeros((size,), f32)
        nrows = -(-size // LANES)
        rows.append(jnp.pad(v, (0, nrows * LANES - size)).reshape(nrows, LANES))
        nxt = r0 + nrows
    rows.append(jnp.zeros((SMALL_CONV_ROW - nxt, LANES), f32))
    conv_rows = CONV_W * GDN_QKV // LANES
    rows.append(jnp.zeros((conv_rows, LANES), f32) if conv is None else conv.reshape(conv_rows, LANES))
    rows.append(jnp.zeros((SMALL_ROWS - SMALL_CONV_ROW - conv_rows, LANES), f32))
    return jnp.concatenate(rows, axis=0)


def _unpack_small(pkt, shapes):
    out = {}
    for n, r0, size in SMALL_LAYOUT:
        if n in shapes:
            nrows = -(-size // LANES)
            out[n] = pkt[r0:r0 + nrows].reshape(-1)[:size].reshape(shapes[n])
    return out


WEIGHTS = ("ln_in_g", "ln_in_b", "w_in", "conv_w", "a_log", "dt_bias", "gdn_norm_g", "b_f", "fox_norm_g", "w_out", "ln1_g", "ln1_b",
           "w_up", "w_down", "w_ple", "w_ple_gate", "b_ple_gate", "ln2_g", "ln2_b")
SMALL_NAMES = tuple(n for n, _, _ in SMALL_LAYOUT if n != "loss")


def kernel(x, p, ln_in_g, ln_in_b, w_in, conv_w, a_log, dt_bias, gdn_norm_g, b_f, fox_norm_g, w_out, ln1_g, ln1_b, w_up, w_down, w_ple, w_ple_gate, b_ple_gate, ln2_g, ln2_b, loss_target, m_ln_in_g, m_ln_in_b, m_w_in, m_conv_w, m_a_log, m_dt_bias, m_gdn_norm_g, m_b_f, m_fox_norm_g, m_w_out, m_ln1_g, m_ln1_b, m_w_up, m_w_down, m_w_ple, m_w_ple_gate, m_b_ple_gate, m_ln2_g, m_ln2_b, v_ln_in_g, v_ln_in_b, v_w_in, v_conv_w, v_a_log, v_dt_bias, v_gdn_norm_g, v_b_f, v_fox_norm_g, v_w_out, v_ln1_g, v_ln1_b, v_w_up, v_w_down, v_w_ple, v_w_ple_gate, v_b_ple_gate, v_ln2_g, v_ln2_b):
    given = dict(locals())
    w = {n: given[n] for n in WEIGHTS}
    m = {n: given["m_" + n] for n in WEIGHTS}
    v = {n: given["v_" + n] for n in WEIGHTS}
    xi, yi, ci = _mesh_pos()
    q = 2 * xi + yi

    def slot_buffer(val, dtype):
        return lax.dynamic_update_slice(lax.empty((N_CHIPS,) + val.shape, dtype), val.astype(dtype)[None], (q, 0, 0))

    conv_rows = CONV_W * GDN_QKV // N_CHIPS // LANES
    conv_pkt = jnp.pad(w["conv_w"][0].reshape(-1, LANES), ((0, CONV_PKT_ROWS - conv_rows), (0, 0)))
    gathered = _gather_weights([slot_buffer(w[n][0], bf16) for n, _, _ in SHARDED], [nch for _, _, nch in SHARDED],
                               slot_buffer(conv_pkt, f32))
    full = dict(zip([n for n, _, _ in SHARDED], gathered[:-1]))
    conv_all = gathered[-1]
    conv_full = jnp.concatenate([conv_all[d, :conv_rows].reshape(CONV_W, GDN_QKV // N_CHIPS) for d in range(N_CHIPS)], axis=1)
    wi = jnp.concatenate([full["w_in"][d] for d in range(N_CHIPS)], axis=1)
    w_cat = jnp.concatenate([wi[:, :OFF_BETA], wi[:, OFF_FOX:OFF_F], wi[:, OFF_BETA:OFF_FOX], wi[:, OFF_F:],
                             jnp.zeros((D_MODEL, D_CAT - D_IN), bf16)], axis=1)

    small = {n: w[n] for n in SMALL_NAMES}
    grad_x, big, small_g = _device_grads(
        x[0], p[0, 0], loss_target[0], small, w_cat, conv_full, full["w_out"].reshape(D_MODEL, D_MODEL), full["w_up"],
        full["w_down"].reshape(D_FF, D_MODEL), full["w_ple"], full["w_ple_gate"].reshape(D_MODEL, D_MODEL))

    gc = big["w_cat"]
    g_in = jnp.concatenate([gc[:, :OFF_BETA], gc[:, SEG_SMALL:SEG_SMALL + 8], gc[:, SEG_FOX:SEG_SMALL],
                            gc[:, SEG_SMALL + 8:SEG_SMALL + 16]], axis=1)
    shard_cols = D_IN // N_CHIPS
    by_dest = dict(w_in=jnp.stack([g_in[:, d * shard_cols:(d + 1) * shard_cols] for d in range(N_CHIPS)]), w_up=big["w_up"],
                   w_ple=big["w_ple"], w_out=big["w_out"], w_down=big["w_down"], w_ple_gate=big["w_gate"])
    gs = [by_dest[n].reshape((N_CHIPS,) + shp) for n, shp, _ in SHARDED]
    *b1s, small_all = _exchange_pairs(gs, _pack_small(small_g, big["conv_w"]))
    qc = jnp.stack([q, ci]).astype(jnp.int32)
    sums = [_add_pair(g, b1, qc, "add_pair_" + n) for g, b1, (n, _, _) in zip(gs, b1s, SHARDED)]
    b2s = _exchange_chips([ab for _, ab in sums])
    reduced = _share_halves([_add_chips(a, b2, qc, "add_chips_" + n) for (a, _), b2, (n, _, _) in zip(sums, b2s, SHARDED)])

    grads, delta, new_m, new_v = {}, {}, {}, {}
    for g, (n, _, _) in zip(reduced, SHARDED):
        outs = _adamw(w[n][0], g, m[n][0], v[n][0], "adamw_" + n)
        grads[n], delta[n], new_m[n], new_v[n] = (a.reshape(w[n].shape) for a in outs)
    shapes = {n: w[n].shape for n in SMALL_NAMES}
    g_pkt, d_pkt, m_pkt, v_pkt = _small_sum_adamw(small_all, _pack_small(w), _pack_small(m), _pack_small(v))
    for dst, pkt in ((grads, g_pkt), (delta, d_pkt), (new_m, m_pkt), (new_v, v_pkt)):
        dst.update(_unpack_small(pkt, shapes))
    conv_rows_all = CONV_W * GDN_QKV // LANES
    conv_g_full = g_pkt[SMALL_CONV_ROW:SMALL_CONV_ROW + conv_rows_all].reshape(CONV_W, GDN_QKV)
    conv_g = lax.dynamic_slice_in_dim(conv_g_full, q * (GDN_QKV // N_CHIPS), GDN_QKV // N_CHIPS, axis=1)
    outs = _adamw(w["conv_w"][0], conv_g, m["conv_w"][0], v["conv_w"][0], "adamw_conv_w")
    grads["conv_w"], delta["conv_w"], new_m["conv_w"], new_v["conv_w"] = (a.reshape(w["conv_w"].shape) for a in outs)
    loss = g_pkt[61, 0]
    return (loss, grad_x[None], *[grads[n] for n in WEIGHTS], *[delta[n] for n in WEIGHTS],
            *[new_m[n] for n in WEIGHTS], *[new_v[n] for n in WEIGHTS])
```

```python
import functools

import jax
import jax.numpy as jnp
from jax import lax
from jax.experimental import pallas as pl
from jax.experimental.pallas import tpu as pltpu

f32 = jnp.float32
bf16 = jnp.bfloat16
HI = lax.Precision.HIGHEST
MESH = pl.DeviceIdType.MESH

D_MODEL = 1024
CHUNK = 64
GDN_HEADS = 4
GDN_DK = 128
FOX_HEADS = 8
FOX_DH = 64
CONV_W = 4
D_FF = 4096
D_PLE = 256
LN_EPS = 1e-5
NORM_EPS = 1e-6
ALPHA = 2.0 ** 0.25
GDN_QKV = 1536
OFF_Z = 1536
OFF_BETA = 2048
OFF_FOX = 2056
OFF_F = 3592
D_IN = 3600
ADAM_LR = 0.001
ADAM_B1 = 0.9
ADAM_B2 = 0.999
ADAM_EPS = 1e-08
ADAM_WD = 0.01
ADAM_STEP = 10

SEG_FOX = 2048
SEG_SMALL = 3584
D_CAT = 3840
LANES = 128
TOK_BLK = 256
FOX_BQ = 256
VMEM_LIMIT = 56 * 1024 * 1024
NEG = -1e30

N_CHIPS = 4


def _params(sem=None, **kw):
    return pltpu.CompilerParams(dimension_semantics=sem, vmem_limit_bytes=VMEM_LIMIT, **kw)


def _sigmoid(x):
    return 1.0 / (1.0 + jnp.exp(-x))


def _softplus(x):
    return jnp.maximum(x, 0.0) + jnp.log(1.0 + jnp.exp(-jnp.abs(x)))


def _ln_fwd(x, g, b):
    mu = jnp.mean(x, -1, keepdims=True)
    xc = x - mu
    var = jnp.mean(xc * xc, -1, keepdims=True)
    rstd = lax.rsqrt(var + LN_EPS)
    xhat = xc * rstd
    return xhat * g + b, xhat, rstd


def _ln_bwd(dy, xhat, rstd, g):
    dxh = dy * g
    m1 = jnp.mean(dxh, -1, keepdims=True)
    m2 = jnp.mean(dxh * xhat, -1, keepdims=True)
    return rstd * (dxh - m1 - xhat * m2)


def _dot(a, b, prec=HI):
    return jnp.dot(a, b, precision=prec, preferred_element_type=f32)


def _dot_nt(a, b, prec=HI):
    return lax.dot_general(a, b, (((1,), (1,)), ((), ())), precision=prec, preferred_element_type=f32)


def _dot_tn(a, b, prec=HI):
    return lax.dot_general(a, b, (((0,), (0,)), ((), ())), precision=prec, preferred_element_type=f32)


def _bdot(a, b):
    return _dot(a.astype(bf16), b.astype(bf16), None)


def _bdot_nt(a, b):
    return _dot_nt(a.astype(bf16), b.astype(bf16), None)


def _bdot_tn(a, b):
    return _dot_tn(a.astype(bf16), b.astype(bf16), None)


def _lane(shape):
    return lax.broadcasted_iota(jnp.int32, shape, len(shape) - 1)


def _mm(a, b, mode, tm, tn, name, out_dtype=f32, epi=None, extra=None, shards=1):
    if mode == "nn":
        (m, k), n = a.shape, b.shape[-1] * shards
    elif mode == "nt":
        (m, k), n = a.shape, b.shape[-2]
    else:
        (k, m), n = a.shape, b.shape[1]
    assert m % tm == 0 and n % tn == 0, (name, m, n, tm, tn)
    per = (n // shards) // tn
    assert mode == "nt" or per * tn * shards == n, (name, n, tn, shards)
    nc = 512 if tn % 512 == 0 else (256 if tn % 256 == 0 else 128)
    ks = k // shards

    def body(a_ref, b_ref, *rest):
        for n0 in range(0, tn, nc):
            if mode == "nn":
                acc = jnp.dot(a_ref[...], b_ref[:, n0:n0 + nc], preferred_element_type=f32)
            elif mode == "nt" and shards > 1:
                acc = jnp.zeros((tm, nc), f32)
                for d in range(shards):
                    acc = acc + lax.dot_general(a_ref[:, d * ks:(d + 1) * ks], b_ref[d, n0:n0 + nc, :], (((1,), (1,)), ((), ())),
                                                preferred_element_type=f32)
            elif mode == "nt":
                acc = lax.dot_general(a_ref[...], b_ref[n0:n0 + nc, :], (((1,), (1,)), ((), ())), preferred_element_type=f32)
            else:
                acc = lax.dot_general(a_ref[...], b_ref[:, n0:n0 + nc], (((0,), (0,)), ((), ())), preferred_element_type=f32)
            if epi == "relu2":
                up_ref, act_ref = rest
                up_ref[:, n0:n0 + nc] = acc
                r = jnp.maximum(acc, 0.0)
                act_ref[:, n0:n0 + nc] = (r * r).astype(bf16)
            elif epi == "relu2_bwd":
                up_ref, o_ref = rest
                o_ref[:, n0:n0 + nc] = (acc * (2.0 * jnp.maximum(up_ref[:, n0:n0 + nc], 0.0))).astype(bf16)
            else:
                (o_ref,) = rest
                o_ref[:, n0:n0 + nc] = acc.astype(out_dtype)

    if mode == "tn":
        a_spec = pl.BlockSpec((k, tm), lambda j, i: (0, i))
    else:
        a_spec = pl.BlockSpec((tm, k), lambda j, i: (i, 0))
    if mode == "nt" and shards > 1:
        b_spec = pl.BlockSpec((shards, tn, ks), lambda j, i: (0, j, 0))
    elif mode == "nt":
        b_spec = pl.BlockSpec((tn, k), lambda j, i: (j, 0))
    elif mode == "nn" and shards > 1:
        b_spec = pl.BlockSpec((None, k, tn), lambda j, i: (j // per, 0, j % per))
    else:
        b_spec = pl.BlockSpec((k, tn), lambda j, i: (0, j))
    o_spec = pl.BlockSpec((tm, tn), lambda j, i: (i, j))
    in_specs = [a_spec, b_spec]
    args = [a, b]
    if epi == "relu2":
        out_shape = (jax.ShapeDtypeStruct((m, n), f32), jax.ShapeDtypeStruct((m, n), bf16))
        out_specs = (o_spec, o_spec)
    elif epi == "relu2_bwd":
        in_specs.append(o_spec)
        args.append(extra)
        out_shape = jax.ShapeDtypeStruct((m, n), bf16)
        out_specs = o_spec
    elif mode == "tn" and shards > 1:
        out_shape = jax.ShapeDtypeStruct((shards, m, n // shards), out_dtype)
        out_specs = pl.BlockSpec((None, tm, tn), lambda j, i: (j // per, i, j % per))
    else:
        out_shape = jax.ShapeDtypeStruct((m, n), out_dtype)
        out_specs = o_spec
    return pl.pallas_call(
        body, name=name, grid=(n // tn, m // tm), in_specs=in_specs, out_specs=out_specs, out_shape=out_shape,
        compiler_params=_params(("parallel", "parallel")),
    )(*args)


def _row_spec(width, col=0):
    return pl.BlockSpec((TOK_BLK, width), lambda i: (i, col))


def _vec_spec(rows, width):
    return pl.BlockSpec((rows, width), lambda i: (0, 0))


def _ln_in(x, g, b):
    t, d = x.shape

    def body(x_ref, g_ref, b_ref, h_ref, hb_ref):
        h, _, _ = _ln_fwd(x_ref[...], g_ref[...], b_ref[...])
        h_ref[...] = h
        hb_ref[...] = h.astype(bf16)

    return pl.pallas_call(
        body, name="ln_in", grid=(t // TOK_BLK,),
        in_specs=[_row_spec(d), _vec_spec(1, d), _vec_spec(1, d)],
        out_specs=(_row_spec(d), _row_spec(d)),
        out_shape=(jax.ShapeDtypeStruct((t, d), f32), jax.ShapeDtypeStruct((t, d), bf16)),
        compiler_params=_params(("parallel",)),
    )(x, g, b)


def _attn_post(o_gdn, proj, o_fox, g_gdn, g_fox2):
    t = o_gdn.shape[0]

    def body(og_ref, z_ref, of_ref, gg_ref, gf_ref, out_ref):
        for h in range(GDN_HEADS):
            sl = slice(h * LANES, (h + 1) * LANES)
            og = og_ref[:, sl]
            z = z_ref[:, sl]
            r = lax.rsqrt(jnp.mean(og * og, -1, keepdims=True) + NORM_EPS)
            out_ref[:, sl] = (og * r * gg_ref[...] * (z * _sigmoid(z))).astype(bf16)
        lo = _lane((TOK_BLK, LANES)) < FOX_DH
        for pr in range(FOX_HEADS // 2):
            sl = slice(pr * LANES, (pr + 1) * LANES)
            of = of_ref[:, sl]
            sq = of * of
            s0 = jnp.sum(jnp.where(lo, sq, 0.0), -1, keepdims=True)
            s1 = jnp.sum(jnp.where(lo, 0.0, sq), -1, keepdims=True)
            r = lax.rsqrt(jnp.where(lo, s0, s1) * (1.0 / FOX_DH) + NORM_EPS)
            out_ref[:, 512 + pr * LANES:512 + (pr + 1) * LANES] = (of * r * gf_ref[...]).astype(bf16)

    return pl.pallas_call(
        body, name="attn_post", grid=(t // TOK_BLK,),
        in_specs=[_row_spec(512), _row_spec(512, OFF_Z // 512), _row_spec(512), _vec_spec(1, LANES), _vec_spec(1, LANES)],
        out_specs=_row_spec(D_MODEL),
        out_shape=jax.ShapeDtypeStruct((t, D_MODEL), bf16),
        compiler_params=_params(("parallel",)),
    )(o_gdn, proj, o_fox, g_gdn, g_fox2)


def _attn_post_bwd(dattn, o_gdn, proj, o_fox, g_gdn, g_fox2):
    t = o_gdn.shape[0]

    def body(da_ref, og_ref, z_ref, of_ref, gg_ref, gf_ref, dog_ref, dz_ref, dof_ref, pg_ref):
        i = pl.program_id(0)

        @pl.when(i == 0)
        def _():
            pg_ref[...] = jnp.zeros_like(pg_ref)

        dgg = jnp.zeros((1, LANES), f32)
        for h in range(GDN_HEADS):
            sl = slice(h * LANES, (h + 1) * LANES)
            og = og_ref[:, sl]
            z = z_ref[:, sl]
            dout = da_ref[:, sl]
            g = gg_ref[...]
            r = lax.rsqrt(jnp.mean(og * og, -1, keepdims=True) + NORM_EPS)
            sg = _sigmoid(z)
            silu = z * sg
            ng = og * r * g
            dng = dout * silu
            dz_ref[:, sl] = (dout * ng * (sg * (1.0 + z * (1.0 - sg)))).astype(bf16)
            dgg = dgg + jnp.sum(dng * og * r, 0, keepdims=True)
            gd = dng * g
            dog_ref[:, sl] = r * gd - og * (r * r * r) * jnp.mean(og * gd, -1, keepdims=True)
        pg_ref[0:1, :] += dgg
        lo = _lane((TOK_BLK, LANES)) < FOX_DH
        dgf = jnp.zeros((1, LANES), f32)
        for pr in range(FOX_HEADS // 2):
            sl = slice(pr * LANES, (pr + 1) * LANES)
            of = of_ref[:, sl]
            dout = da_ref[:, 512 + pr * LANES:512 + (pr + 1) * LANES]
            g = gf_ref[...]
            sq = of * of
            s0 = jnp.sum(jnp.where(lo, sq, 0.0), -1, keepdims=True)
            s1 = jnp.sum(jnp.where(lo, 0.0, sq), -1, keepdims=True)
            r = lax.rsqrt(jnp.where(lo, s0, s1) * (1.0 / FOX_DH) + NORM_EPS)
            dgf = dgf + jnp.sum(dout * of * r, 0, keepdims=True)
            gd = dout * g
            xg = of * gd
            m0 = jnp.sum(jnp.where(lo, xg, 0.0), -1, keepdims=True)
            m1 = jnp.sum(jnp.where(lo, 0.0, xg), -1, keepdims=True)
            dof_ref[:, sl] = r * gd - of * (r * r * r) * (jnp.where(lo, m0, m1) * (1.0 / FOX_DH))
        pg_ref[1:2, :] += dgf

    return pl.pallas_call(
        body, name="attn_post_bwd", grid=(t // TOK_BLK,),
        in_specs=[_row_spec(D_MODEL), _row_spec(512), _row_spec(512, OFF_Z // 512), _row_spec(512), _vec_spec(1, LANES), _vec_spec(1, LANES)],
        out_specs=(_row_spec(512), _row_spec(512), _row_spec(512), _vec_spec(8, LANES)),
        out_shape=(jax.ShapeDtypeStruct((t, 512), f32), jax.ShapeDtypeStruct((t, 512), bf16),
                   jax.ShapeDtypeStruct((t, 512), f32), jax.ShapeDtypeStruct((8, LANES), f32)),
        compiler_params=_params(("arbitrary",)),
    )(dattn, o_gdn, proj, o_fox, g_gdn, g_fox2)


def _ln1(h0, mix, g, b):
    t, d = h0.shape

    def body(h0_ref, mix_ref, g_ref, b_ref, h_ref, hb_ref, xh_ref, rs_ref):
        h, xhat, rstd = _ln_fwd(ALPHA * h0_ref[...] + mix_ref[...], g_ref[...], b_ref[...])
        h_ref[...] = h
        hb_ref[...] = h.astype(bf16)
        xh_ref[...] = xhat
        rs_ref[...] = jnp.broadcast_to(rstd, rs_ref.shape)

    return pl.pallas_call(
        body, name="ln1", grid=(t // TOK_BLK,),
        in_specs=[_row_spec(d), _row_spec(d), _vec_spec(1, d), _vec_spec(1, d)],
        out_specs=(_row_spec(d), _row_spec(d), _row_spec(d), _row_spec(LANES)),
        out_shape=(jax.ShapeDtypeStruct((t, d), f32), jax.ShapeDtypeStruct((t, d), bf16),
                   jax.ShapeDtypeStruct((t, d), f32), jax.ShapeDtypeStruct((t, LANES), f32)),
        compiler_params=_params(("parallel",)),
    )(h0, mix, g, b)


def _ln2_loss(h1, ff, pe, gp, b_gate, g, b, target):
    t, d = h1.shape

    def body(h1_ref, ff_ref, pe_ref, gp_ref, bg_ref, g_ref, b_ref, t_ref, dr_ref, drb_ref, dpe_ref, dgp_ref, pg_ref):
        i = pl.program_id(0)

        @pl.when(i == 0)
        def _():
            pg_ref[...] = jnp.zeros_like(pg_ref)

        sig = _sigmoid(gp_ref[...] + bg_ref[...])
        pe = pe_ref[...]
        r2 = ALPHA * h1_ref[...] + ff_ref[...] + pe * sig
        y, xhat, rstd = _ln_fwd(r2, g_ref[...], b_ref[...])
        err = y - t_ref[...]
        dy = err * (1.0 / d)
        dr = _ln_bwd(dy, xhat, rstd, g_ref[...])
        dr_ref[...] = dr
        drb_ref[...] = dr.astype(bf16)
        dpe_ref[...] = (dr * sig).astype(bf16)
        dgp = dr * pe * sig * (1.0 - sig)
        dgp_ref[...] = dgp.astype(bf16)
        pg_ref[0:1, :] += jnp.sum(dy * xhat, 0, keepdims=True)
        pg_ref[1:2, :] += jnp.sum(dy, 0, keepdims=True)
        pg_ref[2:3, :] += jnp.sum(dgp, 0, keepdims=True)
        pg_ref[3:4, :] += 0.5 * jnp.sum(jnp.mean(err * err, -1, keepdims=True), 0, keepdims=True)

    return pl.pallas_call(
        body, name="ln2_loss", grid=(t // TOK_BLK,),
        in_specs=[_row_spec(d)] * 4 + [_vec_spec(1, d)] * 3 + [_row_spec(d)],
        out_specs=(_row_spec(d), _row_spec(d), _row_spec(d), _row_spec(d), _vec_spec(8, d)),
        out_shape=(jax.ShapeDtypeStruct((t, d), f32), jax.ShapeDtypeStruct((t, d), bf16), jax.ShapeDtypeStruct((t, d), bf16),
                   jax.ShapeDtypeStruct((t, d), bf16), jax.ShapeDtypeStruct((8, d), f32)),
        compiler_params=_params(("arbitrary",)),
    )(h1, ff, pe, gp, b_gate, g, b, target)


def _ln1_bwd(dr2, da, db, xhat, rstd, g):
    t, d = dr2.shape

    def body(dr2_ref, da_ref, db_ref, xh_ref, rs_ref, g_ref, dr_ref, drb_ref, pg_ref):
        i = pl.program_id(0)

        @pl.when(i == 0)
        def _():
            pg_ref[...] = jnp.zeros_like(pg_ref)

        dh = ALPHA * dr2_ref[...] + da_ref[...] + db_ref[...]
        xhat = xh_ref[...]
        dr = _ln_bwd(dh, xhat, rs_ref[:, 0:1], g_ref[...])
        dr_ref[...] = dr
        drb_ref[...] = dr.astype(bf16)
        pg_ref[0:1, :] += jnp.sum(dh * xhat, 0, keepdims=True)
        pg_ref[1:2, :] += jnp.sum(dh, 0, keepdims=True)

    return pl.pallas_call(
        body, name="ln1_bwd", grid=(t // TOK_BLK,),
        in_specs=[_row_spec(d)] * 4 + [_row_spec(LANES), _vec_spec(1, d)],
        out_specs=(_row_spec(d), _row_spec(d), _vec_spec(8, d)),
        out_shape=(jax.ShapeDtypeStruct((t, d), f32), jax.ShapeDtypeStruct((t, d), bf16), jax.ShapeDtypeStruct((8, d), f32)),
        compiler_params=_params(("arbitrary",)),
    )(dr2, da, db, xhat, rstd, g)


def _ln_in_bwd(x, dr1, dmm, g):
    t, d = x.shape

    def body(x_ref, dr1_ref, dmm_ref, g_ref, dx_ref, pg_ref):
        i = pl.program_id(0)

        @pl.when(i == 0)
        def _():
            pg_ref[...] = jnp.zeros_like(pg_ref)

        dh = ALPHA * dr1_ref[...] + dmm_ref[...]
        _, xhat, rstd = _ln_fwd(x_ref[...], g_ref[...], 0.0)
        dx_ref[...] = _ln_bwd(dh, xhat, rstd, g_ref[...])
        pg_ref[0:1, :] += jnp.sum(dh * xhat, 0, keepdims=True)
        pg_ref[1:2, :] += jnp.sum(dh, 0, keepdims=True)

    return pl.pallas_call(
        body, name="ln_in_bwd", grid=(t // TOK_BLK,),
        in_specs=[_row_spec(d)] * 3 + [_vec_spec(1, d)],
        out_specs=(_row_spec(d), _vec_spec(8, d)),
        out_shape=(jax.ShapeDtypeStruct((t, d), f32), jax.ShapeDtypeStruct((8, d), f32)),
        compiler_params=_params(("arbitrary",)),
    )(x, dr1, dmm, g)


def _tri(n, upper=False, strict=False):
    r = lax.broadcasted_iota(jnp.int32, (n, n), 0)
    c = lax.broadcasted_iota(jnp.int32, (n, n), 1)
    if upper:
        m = (c > r) if strict else (c >= r)
    else:
        m = (c < r) if strict else (c <= r)
    return jnp.where(m, 1.0, 0.0).astype(f32)


def _gate_values(x, bias, alog, lane):
    z = x + bias
    return jnp.where(lane < 4, _sigmoid(z), jnp.where(lane < 8, -jnp.exp(alog) * _softplus(z), jnp.where(lane < 16, -_softplus(-z), 0.0)))


def _gates(proj, bias_row, alog_row):
    t = proj.shape[0]
    nch = t // CHUNK

    def body(x_ref, bias_ref, alog_ref, gates_ref, gcum_ref, gcumt_ref):
        lane = _lane((t, LANES))
        gates = _gate_values(x_ref[...], bias_ref[...], alog_ref[...], lane)
        gates_ref[...] = gates
        g3 = gates.reshape(nch, CHUNK, LANES)
        tri = jnp.broadcast_to(_tri(CHUNK)[None], (nch, CHUNK, CHUNK))
        loc = jnp.einsum("bij,bjk->bik", tri, g3, precision=HI, preferred_element_type=f32)
        tot = jnp.sum(g3, axis=1)
        offs = _dot(_tri(nch, strict=True), tot)
        glob = loc + offs[:, None, :]
        lane3 = _lane((nch, CHUNK, LANES))
        gcum = jnp.where(lane3 < 4, g3, jnp.where(lane3 < 8, loc, glob)).reshape(t, LANES)
        gcum_ref[...] = gcum
        gcumt_ref[...] = gcum.T

    return pl.pallas_call(
        body, name="gates", grid=(1,),
        in_specs=[pl.BlockSpec((t, LANES), lambda i: (0, SEG_SMALL // LANES)), _vec_spec(1, LANES), _vec_spec(1, LANES)],
        out_specs=(pl.BlockSpec((t, LANES), lambda i: (0, 0)), pl.BlockSpec((t, LANES), lambda i: (0, 0)),
                   pl.BlockSpec((LANES, t), lambda i: (0, 0))),
        out_shape=(jax.ShapeDtypeStruct((t, LANES), f32), jax.ShapeDtypeStruct((t, LANES), f32), jax.ShapeDtypeStruct((LANES, t), f32)),
        compiler_params=_params(("arbitrary",)),
    )(proj, bias_row, alog_row)


def _gates_bwd(proj, bias_row, alog_row, gates, dgates, dccol, dct):
    t = proj.shape[0]
    nch = t // CHUNK

    def body(x_ref, bias_ref, alog_ref, gates_ref, dg_ref, dcc_ref, dct_ref, dx_ref, pg_ref):
        lane = _lane((t, LANES))
        d = dg_ref[...] + dcc_ref[...] + dct_ref[...].T
        d3 = d.reshape(nch, CHUNK, LANES)
        tri = jnp.broadcast_to(_tri(CHUNK, upper=True)[None], (nch, CHUNK, CHUNK))
        loc = jnp.einsum("bij,bjk->bik", tri, d3, precision=HI, preferred_element_type=f32)
        tot = jnp.sum(d3, axis=1)
        offs = _dot(_tri(nch, upper=True, strict=True), tot)
        glob = loc + offs[:, None, :]
        lane3 = _lane((nch, CHUNK, LANES))
        dpre = jnp.where(lane3 < 4, d3, jnp.where(lane3 < 8, loc, glob)).reshape(t, LANES)
        z = x_ref[...] + bias_ref[...]
        sg = _sigmoid(z)
        dx = jnp.where(lane < 4, dpre * sg * (1.0 - sg),
                       jnp.where(lane < 8, dpre * (-jnp.exp(alog_ref[...])) * sg, jnp.where(lane < 16, dpre * (1.0 - sg), 0.0)))
        dx_ref[...] = dx.astype(bf16)
        pg_ref[...] = jnp.zeros_like(pg_ref)
        pg_ref[0:1, :] = jnp.sum(dx, 0, keepdims=True)
        pg_ref[1:2, :] = jnp.sum(jnp.where((lane >= 4) & (lane < 8), dpre * gates_ref[...], 0.0), 0, keepdims=True)

    full = pl.BlockSpec((t, LANES), lambda i: (0, 0))
    return pl.pallas_call(
        body, name="gates_bwd", grid=(1,),
        in_specs=[pl.BlockSpec((t, LANES), lambda i: (0, SEG_SMALL // LANES)), _vec_spec(1, LANES), _vec_spec(1, LANES),
                  full, full, full, pl.BlockSpec((LANES, t), lambda i: (0, 0))],
        out_specs=(full, _vec_spec(8, LANES)),
        out_shape=(jax.ShapeDtypeStruct((t, LANES), bf16), jax.ShapeDtypeStruct((8, LANES), f32)),
        compiler_params=_params(("arbitrary",)),
    )(proj, bias_row, alog_row, gates, dgates, dccol, dct)


def _conv_act(u, cw, row, t):
    c = cw[3:4, :] * u
    for jj in range(CONV_W - 1):
        sh = CONV_W - 1 - jj
        c = c + cw[jj:jj + 1, :] * jnp.where(row >= sh, pltpu.roll(u, sh, axis=0), 0.0)
    return c


def _gdn_conv(proj, conv_w):
    t = proj.shape[0]
    nblk = GDN_QKV // LANES

    def body(u_ref, cw_ref, c_ref, y_ref):
        j = pl.program_id(0)
        row = lax.broadcasted_iota(jnp.int32, (t, LANES), 0)
        c = _conv_act(u_ref[...], cw_ref[...], row, t)
        c_ref[...] = c
        s = c * _sigmoid(c)
        r = lax.rsqrt(jnp.sum(s * s, -1, keepdims=True) + NORM_EPS)
        scale = jnp.where(j < GDN_HEADS, GDN_DK ** -0.5, 1.0)
        y_ref[...] = jnp.where(j < 2 * GDN_HEADS, s * (r * scale), s)

    blk = pl.BlockSpec((t, LANES), lambda j: (0, j))
    return pl.pallas_call(
        body, name="gdn_conv", grid=(nblk,),
        in_specs=[blk, pl.BlockSpec((CONV_W, LANES), lambda j: (0, j))],
        out_specs=(blk, blk),
        out_shape=(jax.ShapeDtypeStruct((t, GDN_QKV), f32), jax.ShapeDtypeStruct((t, GDN_QKV), f32)),
        compiler_params=_params(("parallel",)),
    )(proj, conv_w)


def _gdn_conv_bwd(proj, conv_w, c, dy):
    t = proj.shape[0]
    nblk = GDN_QKV // LANES

    def body(u_ref, cw_ref, c_ref, dy_ref, du_ref, dcw_ref):
        j = pl.program_id(0)
        row = lax.broadcasted_iota(jnp.int32, (t, LANES), 0)
        u = u_ref[...]
        cw = cw_ref[...]
        c = c_ref[...]
        dy = dy_ref[...]
        sg = _sigmoid(c)
        s = c * sg
        r = lax.rsqrt(jnp.sum(s * s, -1, keepdims=True) + NORM_EPS)
        n = s * r
        scale = jnp.where(j < GDN_HEADS, GDN_DK ** -0.5, 1.0)
        dn = dy * scale
        ds = jnp.where(j < 2 * GDN_HEADS, r * (dn - n * jnp.sum(dn * n, -1, keepdims=True)), dy)
        dc = ds * (sg * (1.0 + c * (1.0 - sg)))
        du = cw[3:4, :] * dc
        dcw_ref[...] = jnp.zeros_like(dcw_ref)
        dcw_ref[3:4, :] = jnp.sum(dc * u, 0, keepdims=True)
        for jj in range(CONV_W - 1):
            sh = CONV_W - 1 - jj
            du = du + cw[jj:jj + 1, :] * jnp.where(row < t - sh, pltpu.roll(dc, t - sh, axis=0), 0.0)
            dcw_ref[jj:jj + 1, :] = jnp.sum(dc * jnp.where(row >= sh, pltpu.roll(u, sh, axis=0), 0.0), 0, keepdims=True)
        du_ref[...] = du.astype(bf16)

    blk = pl.BlockSpec((t, LANES), lambda j: (0, j))
    return pl.pallas_call(
        body, name="gdn_conv_bwd", grid=(nblk,),
        in_specs=[blk, pl.BlockSpec((CONV_W, LANES), lambda j: (0, j)), blk, blk],
        out_specs=(blk, pl.BlockSpec((8, LANES), lambda j: (0, j))),
        out_shape=(jax.ShapeDtypeStruct((t, GDN_QKV), bf16), jax.ShapeDtypeStruct((8, GDN_QKV), f32)),
        compiler_params=_params(("parallel",)),
    )(proj, conv_w, c, dy)


def _chunk_masks():
    r = lax.broadcasted_iota(jnp.int32, (CHUNK, CHUNK), 0)
    c = lax.broadcasted_iota(jnp.int32, (CHUNK, CHUNK), 1)
    return r >= c, r > c, r == c


def _col_to_row(col, eye):
    return jnp.sum(jnp.where(eye, col, 0.0), axis=0, keepdims=True)


def _row_to_col(row, eye):
    return jnp.sum(jnp.where(eye, row, 0.0), axis=1, keepdims=True)


NN = (((1,), (0,)), ((), ()))
NT = (((1,), (1,)), ((), ()))
TN = (((0,), (0,)), ((), ()))
GDN_GROUP = 4


def _mx(a, b, dims=NN, passes=1):
    d = lambda p, q: lax.dot_general(p, q, dims, preferred_element_type=f32)
    ah, bh = a.astype(bf16), b.astype(bf16)
    if passes == 1:
        return d(ah, bh)
    al = (a - ah.astype(f32)).astype(bf16)
    bl = (b - bh.astype(f32)).astype(bf16)
    return d(ah, bh) + (d(ah, bl) + d(al, bh))


def _gdn_decay(gam, masks):
    causal, _, eye = masks
    return jnp.exp(jnp.where(causal, gam - _col_to_row(gam, eye), NEG))


def _gdn_local(y, gcum):
    t = y.shape[0]
    nch = t // CHUNK
    rows_blk = GDN_GROUP * CHUNK

    def body(y_ref, g_ref, u_ref, w_ref, qk_ref, tinv_ref):
        masks = _chunk_masks()
        _, strict, eye = masks
        for j in range(GDN_GROUP):
            rs = slice(j * CHUNK, (j + 1) * CHUNK)
            for h in range(GDN_HEADS):
                qn = y_ref[rs, h * LANES:(h + 1) * LANES]
                kn = y_ref[rs, 512 + h * LANES:512 + (h + 1) * LANES]
                v = y_ref[rs, 1024 + h * LANES:1024 + (h + 1) * LANES]
                beta = g_ref[rs, h:h + 1]
                gam = g_ref[rs, 4 + h:5 + h]
                dec = _gdn_decay(gam, masks)
                x = -jnp.where(strict, _mx(kn, kn, NT) * dec * beta, 0.0)
                tinv = jnp.where(eye, 1.0, 0.0) + x
                for _ in range(5):
                    x = _mx(x, x, NN, 3)
                    tinv = tinv + _mx(tinv, x, NN, 3)
                e = jnp.exp(gam)
                u_ref[rs, h * LANES:(h + 1) * LANES] = _mx(tinv, beta * v)
                w_ref[rs, h * LANES:(h + 1) * LANES] = _mx(tinv, (beta * e) * kn)
                qk_ref[j, h] = _mx(qn, kn, NT) * dec
                tinv_ref[j, h] = tinv

    mat = pl.BlockSpec((GDN_GROUP, GDN_HEADS, CHUNK, CHUNK), lambda n: (n, 0, 0, 0))
    return pl.pallas_call(
        body, name="gdn_local", grid=(nch // GDN_GROUP,),
        in_specs=[pl.BlockSpec((rows_blk, GDN_QKV), lambda n: (n, 0)), pl.BlockSpec((rows_blk, LANES), lambda n: (n, 0))],
        out_specs=(pl.BlockSpec((rows_blk, 512), lambda n: (n, 0)), pl.BlockSpec((rows_blk, 512), lambda n: (n, 0)), mat, mat),
        out_shape=(jax.ShapeDtypeStruct((t, 512), f32), jax.ShapeDtypeStruct((t, 512), f32),
                   jax.ShapeDtypeStruct((nch, GDN_HEADS, CHUNK, CHUNK), f32), jax.ShapeDtypeStruct((nch, GDN_HEADS, CHUNK, CHUNK), f32)),
        compiler_params=_params(("parallel",)),
    )(y, gcum)


def _gdn_fwd(y, gcum, u, w, qk):
    t = y.shape[0]
    nch = t // CHUNK

    def body(y_ref, g_ref, u_ref, w_ref, qk_ref, o_ref, sall_ref, s_ref):
        @pl.when(pl.program_id(0) == 0)
        def _():
            s_ref[...] = jnp.zeros_like(s_ref)

        for h in range(GDN_HEADS):
            sl = slice(h * LANES, (h + 1) * LANES)
            gam = g_ref[:, 4 + h:5 + h]
            gam_last = gam[CHUNK - 1:CHUNK, :]
            s = s_ref[h]
            sall_ref[0, h] = s
            vn = u_ref[:, sl] - _mx(w_ref[:, sl], s)
            o_ref[:, sl] = _mx(y_ref[:, sl] * jnp.exp(gam), s) + _mx(qk_ref[0, h], vn)
            kd = y_ref[:, 512 + h * LANES:512 + (h + 1) * LANES] * jnp.exp(gam_last - gam)
            s_ref[h] = jnp.exp(gam_last) * s + _mx(kd, vn, TN)

    row = lambda width: pl.BlockSpec((CHUNK, width), lambda n: (n, 0))
    return pl.pallas_call(
        body, name="gdn_fwd", grid=(nch,),
        in_specs=[row(GDN_QKV), row(LANES), row(512), row(512), pl.BlockSpec((1, GDN_HEADS, CHUNK, CHUNK), lambda n: (n, 0, 0, 0))],
        out_specs=(row(512), pl.BlockSpec((1, GDN_HEADS, LANES, LANES), lambda n: (n, 0, 0, 0))),
        out_shape=(jax.ShapeDtypeStruct((t, 512), f32), jax.ShapeDtypeStruct((nch, GDN_HEADS, LANES, LANES), f32)),
        scratch_shapes=[pltpu.VMEM((GDN_HEADS, LANES, LANES), f32)],
        compiler_params=_params(("arbitrary",)),
    )(y, gcum, u, w, qk)


def _gdn_bwd(y, gcum, u_all, w_all, qk_all, tinv_all, sall, do):
    t = y.shape[0]
    nch = t // CHUNK

    def body(y_ref, g_ref, u_ref, w_ref, qk_ref, tinv_ref, sall_ref, do_ref, dy_ref, dg_ref, ds_ref):
        @pl.when(pl.program_id(0) == 0)
        def _():
            ds_ref[...] = jnp.zeros_like(ds_ref)

        masks = _chunk_masks()
        causal, strict, eye = masks
        lane = _lane((CHUNK, LANES))
        row = lax.broadcasted_iota(jnp.int32, (CHUNK, 1), 0)
        dgates = jnp.zeros((CHUNK, LANES), f32)
        for h in range(GDN_HEADS):
            sl = slice(h * LANES, (h + 1) * LANES)
            qn = y_ref[:, sl]
            kn = y_ref[:, 512 + h * LANES:512 + (h + 1) * LANES]
            v = y_ref[:, 1024 + h * LANES:1024 + (h + 1) * LANES]
            beta = g_ref[:, h:h + 1]
            gam = g_ref[:, 4 + h:5 + h]
            gam_last = gam[CHUNK - 1:CHUNK, :]
            dec = _gdn_decay(gam, masks)
            e = jnp.exp(gam)
            f = jnp.exp(gam_last - gam)
            gl = jnp.exp(gam_last)
            kkd = _mx(kn, kn, NT) * dec
            u, w, qk, tinv = u_ref[:, sl], w_ref[:, sl], qk_ref[0, h], tinv_ref[0, h]
            qd, kd = qn * e, kn * f
            s = sall_ref[0, h]
            dsn = ds_ref[h]
            d_o = do_ref[:, sl]
            vn = u - _mx(w, s)
            dvn = _mx(qk, d_o, TN) + _mx(kd, dsn)
            dqk = jnp.where(causal, _mx(d_o, vn, NT), 0.0)
            dqd = _mx(d_o, s, NT)
            dkd = _mx(vn, dsn, NT)
            dgl = jnp.sum(jnp.sum(dsn * s, axis=1, keepdims=True), axis=0, keepdims=True)
            dw = -_mx(dvn, s, NT)
            ds_ref[h] = _mx(qd, d_o, TN) - _mx(w, dvn, TN) + gl * dsn
            dru = _mx(tinv, dvn, TN)
            drw = _mx(tinv, dw, TN)
            dn = jnp.where(strict, -(_mx(dru, u, NT) + _mx(drw, w, NT)), 0.0)
            dv = beta * dru
            drw_k = jnp.sum(drw * kn, axis=1, keepdims=True)
            dbeta = jnp.sum(dru * v, axis=1, keepdims=True) + e * drw_k + jnp.sum(dn * kkd, axis=1, keepdims=True)
            de = beta * drw_k
            dk = (beta * e) * drw
            dkk = dn * beta * dec
            dk = dk + _mx(dkk, kn) + _mx(dkk, kn, TN)
            dqkr = dqk * dec
            dq = _mx(dqkr, kn) + dqd * e
            dk = dk + _mx(dqkr, qn, TN) + dkd * f
            m = dn * (kkd * beta) + dqk * qk
            dgam = jnp.sum(m, axis=1, keepdims=True) - _row_to_col(jnp.sum(m, axis=0, keepdims=True), eye)
            de = de + jnp.sum(dqd * qn, axis=1, keepdims=True)
            df = jnp.sum(dkd * kn, axis=1, keepdims=True)
            dgam = dgam + de * e - df * f
            dgam_last = jnp.sum(df * f, axis=0, keepdims=True) + dgl * gl
            dgam = dgam + jnp.where(row == CHUNK - 1, dgam_last, 0.0)
            dy_ref[:, sl] = dq
            dy_ref[:, 512 + h * LANES:512 + (h + 1) * LANES] = dk
            dy_ref[:, 1024 + h * LANES:1024 + (h + 1) * LANES] = dv
            dgates = dgates + jnp.where(lane == h, dbeta, 0.0) + jnp.where(lane == 4 + h, dgam, 0.0)
        dg_ref[...] = dgates

    rev = lambda width: pl.BlockSpec((CHUNK, width), lambda n: (nch - 1 - n, 0))
    mat = lambda d: pl.BlockSpec((1, GDN_HEADS, d, d), lambda n: (nch - 1 - n, 0, 0, 0))
    return pl.pallas_call(
        body, name="gdn_bwd", grid=(nch,),
        in_specs=[rev(GDN_QKV), rev(LANES), rev(512), rev(512), mat(CHUNK), mat(CHUNK), mat(LANES), rev(512)],
        out_specs=(rev(GDN_QKV), rev(LANES)),
        out_shape=(jax.ShapeDtypeStruct((t, GDN_QKV), f32), jax.ShapeDtypeStruct((t, LANES), f32)),
        scratch_shapes=[pltpu.VMEM((GDN_HEADS, LANES, LANES), f32)],
        compiler_params=_params(("arbitrary",)),
    )(y, gcum, u_all, w_all, qk_all, tinv_all, sall, do)


FOX_CLASSES = 4


def _fox_groups(t):
    nq = t // FOX_BQ
    ncls = min(FOX_CLASSES, nq)
    per = nq // ncls
    return [(g * per, per, (g + 1) * per * FOX_BQ) for g in range(ncls)]


def _fox_scores(q_ref, k_ref, gcum_ref, gcumt_ref, h, i, keys):
    pr = h // 2
    lo = (h % 2) * FOX_DH
    lane = _lane((FOX_BQ, LANES))
    mask = (lane >= lo) & (lane < lo + FOX_DH)
    qm = jnp.where(mask, q_ref[:, pr * LANES:(pr + 1) * LANES], 0.0).astype(bf16)
    kp = k_ref[:, pr * LANES:(pr + 1) * LANES].astype(bf16)
    s = _dot_nt(qm, kp, None) * (FOX_DH ** -0.5)
    s = s + gcum_ref[:, 8 + h:9 + h] - gcumt_ref[8 + h:9 + h, :]
    rows = i * FOX_BQ + lax.broadcasted_iota(jnp.int32, (FOX_BQ, keys), 0)
    cols = lax.broadcasted_iota(jnp.int32, (FOX_BQ, keys), 1)
    return jnp.where(cols <= rows, s, NEG), mask, qm, kp


def _fox_fwd(proj, gcum, gcumt):
    c0 = SEG_FOX // 512

    def group_call(q0, nq, keys):
        def body(q_ref, k_ref, v_ref, gcum_ref, gcumt_ref, o_ref, lse_ref):
            i = q0 + pl.program_id(0)
            lane = _lane((FOX_BQ, LANES))
            lse_all = jnp.zeros((FOX_BQ, LANES), f32)
            for pr in range(FOX_HEADS // 2):
                vp = v_ref[:, pr * LANES:(pr + 1) * LANES].astype(bf16)
                o_pair = jnp.zeros((FOX_BQ, LANES), f32)
                for h in (2 * pr, 2 * pr + 1):
                    s, mask, _, _ = _fox_scores(q_ref, k_ref, gcum_ref, gcumt_ref, h, i, keys)
                    m = jnp.max(s, axis=1, keepdims=True)
                    p = jnp.exp(s - m)
                    l = jnp.sum(p, axis=1, keepdims=True)
                    o_h = _dot((p * (1.0 / l)).astype(bf16), vp, None)
                    o_pair = jnp.where(mask, o_h, o_pair)
                    lse_all = jnp.where(lane == h, m + jnp.log(l), lse_all)
                o_ref[:, pr * LANES:(pr + 1) * LANES] = o_pair
            lse_ref[...] = lse_all

        seen = lambda col: pl.BlockSpec((keys, 512), lambda i: (0, col))
        return pl.pallas_call(
            body, name=f"fox_fwd_{keys}", grid=(nq,),
            in_specs=[pl.BlockSpec((FOX_BQ, 512), lambda i: (q0 + i, c0)), seen(c0 + 1), seen(c0 + 2),
                      pl.BlockSpec((FOX_BQ, LANES), lambda i: (q0 + i, 0)), pl.BlockSpec((LANES, keys), lambda i: (0, 0))],
            out_specs=(pl.BlockSpec((FOX_BQ, 512), lambda i: (i, 0)), pl.BlockSpec((FOX_BQ, LANES), lambda i: (i, 0))),
            out_shape=(jax.ShapeDtypeStruct((nq * FOX_BQ, 512), f32), jax.ShapeDtypeStruct((nq * FOX_BQ, LANES), f32)),
            compiler_params=_params(("parallel",)),
        )(proj, proj, proj, gcum, gcumt)

    parts = [group_call(*g) for g in _fox_groups(proj.shape[0])]
    return jnp.concatenate([o for o, _ in parts], axis=0), jnp.concatenate([l for _, l in parts], axis=0)


def _fox_bwd(proj, gcum, gcumt, o, lse, do):
    t = proj.shape[0]
    c0 = SEG_FOX // 512

    def group_call(q0, nq, keys, acc):
        first = acc is None

        def body(q_ref, k_ref, v_ref, gcum_ref, gcumt_ref, o_ref, lse_ref, do_ref, *rest):
            dq_ref, dk_ref, dv_ref, dcc_ref, dct_ref = rest[-5:]
            j = pl.program_id(0)
            i = q0 + j

            @pl.when(j == 0)
            def _():
                if first:
                    dk_ref[...] = jnp.zeros_like(dk_ref)
                    dv_ref[...] = jnp.zeros_like(dv_ref)
                    dct_ref[...] = jnp.zeros_like(dct_ref)
                else:
                    dk_ref[...], dv_ref[...], dct_ref[...] = rest[0][...], rest[1][...], rest[2][...]

            lane = _lane((FOX_BQ, LANES))
            dcc = jnp.zeros((FOX_BQ, LANES), f32)
            scale = FOX_DH ** -0.5
            for pr in range(FOX_HEADS // 2):
                sl = slice(pr * LANES, (pr + 1) * LANES)
                vp = v_ref[:, sl].astype(bf16)
                dq_pair = jnp.zeros((FOX_BQ, LANES), f32)
                for h in (2 * pr, 2 * pr + 1):
                    s, mask, qm, kp = _fox_scores(q_ref, k_ref, gcum_ref, gcumt_ref, h, i, keys)
                    p = jnp.exp(s - lse_ref[:, h:h + 1])
                    dom = jnp.where(mask, do_ref[:, sl], 0.0)
                    delta = jnp.sum(dom * o_ref[:, sl], axis=1, keepdims=True)
                    domb = dom.astype(bf16)
                    ds = p * (_dot_nt(domb, vp, None) - delta)
                    dsb = ds.astype(bf16)
                    dv_ref[:, sl] += _dot_tn(p.astype(bf16), domb, None)
                    dk_ref[:, sl] += _dot_tn(dsb, qm, None) * scale
                    dq_pair = jnp.where(mask, _dot(dsb, kp, None) * scale, dq_pair)
                    dcc = jnp.where(lane == 8 + h, jnp.sum(ds, axis=1, keepdims=True), dcc)
                    dct_ref[8 + h:9 + h, :] += -jnp.sum(ds, axis=0, keepdims=True)
                dq_ref[:, sl] = dq_pair.astype(bf16)
            dcc_ref[...] = dcc

        qblk = lambda col: pl.BlockSpec((FOX_BQ, 512), lambda i: (q0 + i, col))
        oblk = pl.BlockSpec((FOX_BQ, 512), lambda i: (i, 0))
        seen = lambda col: pl.BlockSpec((keys, 512), lambda i: (0, col))
        rblk = pl.BlockSpec((FOX_BQ, LANES), lambda i: (q0 + i, 0))
        seen_t = pl.BlockSpec((LANES, keys), lambda i: (0, 0))
        in_specs = [qblk(c0), seen(c0 + 1), seen(c0 + 2), rblk, seen_t, qblk(0), rblk, qblk(0)]
        args = [proj, proj, proj, gcum, gcumt, o, lse, do]
        aliases = {}
        if not first:
            in_specs += [seen(0), seen(0), seen_t]
            args += list(acc)
            aliases = {8: 1, 9: 2, 10: 4}
        return pl.pallas_call(
            body, name=f"fox_bwd_{keys}", grid=(nq,), in_specs=in_specs,
            out_specs=(oblk, seen(0), seen(0), pl.BlockSpec((FOX_BQ, LANES), lambda i: (i, 0)), seen_t),
            out_shape=(jax.ShapeDtypeStruct((nq * FOX_BQ, 512), bf16), jax.ShapeDtypeStruct((t, 512), f32), jax.ShapeDtypeStruct((t, 512), f32),
                       jax.ShapeDtypeStruct((nq * FOX_BQ, LANES), f32), jax.ShapeDtypeStruct((LANES, t), f32)),
            input_output_aliases=aliases, compiler_params=_params(("arbitrary",)),
        )(*args)

    acc, dqs, dccs = None, [], []
    for g in reversed(_fox_groups(t)):
        dq, dk, dv, dcc, dct = group_call(*g, acc)
        acc = (dk, dv, dct)
        dqs.insert(0, dq)
        dccs.insert(0, dcc)
    return jnp.concatenate(dqs, axis=0), acc[0], acc[1], jnp.concatenate(dccs, axis=0), acc[2]


def _row(v, width=None):
    v = v.reshape(1, -1).astype(f32)
    if width is not None and v.shape[1] < width:
        v = jnp.pad(v, ((0, 0), (0, width - v.shape[1])))
    return v


def _device_grads(x, p, target, small, w_cat, conv_w, w_out, w_up, w_down, w_ple, w_gate):
    z4 = jnp.zeros((4,), f32)
    bias_row = _row(jnp.concatenate([z4, small["dt_bias"].reshape(-1), small["b_f"].reshape(-1)]), LANES)
    alog_row = _row(jnp.concatenate([z4, small["a_log"].reshape(-1)]), LANES)
    g_gdn = _row(small["gdn_norm_g"])
    g_fox2 = _row(jnp.tile(small["fox_norm_g"].reshape(-1), 2))
    pb = p.astype(bf16)

    h0, h0b = _ln_in(x, _row(small["ln_in_g"]), _row(small["ln_in_b"]))
    proj = _mm(h0b, w_cat, "nn", 256, D_CAT, "mm_proj")
    gates, gcum, gcumt = _gates(proj, bias_row, alog_row)
    conv_c, qkv_n = _gdn_conv(proj, conv_w)
    gu, gw, gqk, gtinv = _gdn_local(qkv_n, gcum)
    o_gdn, sall = _gdn_fwd(qkv_n, gcum, gu, gw, gqk)
    o_fox, lse = _fox_fwd(proj, gcum, gcumt)
    attn = _attn_post(o_gdn, proj, o_fox, g_gdn, g_fox2)
    mix = _mm(attn, w_out, "nn", 512, D_MODEL, "mm_mix")
    h1, h1b, xhat1, rstd1 = _ln1(h0, mix, _row(small["ln1_g"]), _row(small["ln1_b"]))
    up, act = _mm(h1b, w_up, "nn", 256, 1024, "mm_up", epi="relu2", shards=N_CHIPS)
    ff = _mm(act, w_down, "nn", 256, D_MODEL, "mm_down")
    gp = _mm(h1b, w_gate, "nn", 512, D_MODEL, "mm_gate")
    pe = _mm(pb, w_ple, "nn", 512, D_MODEL // N_CHIPS, "mm_ple", shards=N_CHIPS)
    dr2, dr2b, dpe, dgp, pg2 = _ln2_loss(h1, ff, pe, gp, _row(small["b_ple_gate"]), _row(small["ln2_g"]), _row(small["ln2_b"]), target)

    dup = _mm(dr2b, w_down, "nt", 256, 2048, "mm_dact", epi="relu2_bwd", extra=up)
    g_down = _mm(act, dr2b, "tn", 1024, D_MODEL, "mm_gdown")
    dh1_a = _mm(dup, w_up, "nt", 256, D_MODEL, "mm_dh1a", shards=N_CHIPS)
    g_up = _mm(h1b, dup, "tn", 1024, 1024, "mm_gup", shards=N_CHIPS)
    dh1_b = _mm(dgp, w_gate, "nt", 512, D_MODEL, "mm_dh1b")
    g_gate = _mm(h1b, dgp, "tn", 1024, D_MODEL, "mm_ggate")
    g_ple = _mm(pb, dpe, "tn", D_PLE, D_MODEL // N_CHIPS, "mm_gple", shards=N_CHIPS)
    dr1, dr1b, pg1 = _ln1_bwd(dr2, dh1_a, dh1_b, xhat1, rstd1, _row(small["ln1_g"]))
    dattn = _mm(dr1b, w_out, "nt", 512, D_MODEL, "mm_dattn")
    g_out = _mm(attn, dr1b, "tn", 1024, D_MODEL, "mm_gout")
    do_gdn, dz, do_fox, pga = _attn_post_bwd(dattn, o_gdn, proj, o_fox, g_gdn, g_fox2)
    dfq, dfk, dfv, dccol, dct = _fox_bwd(proj, gcum, gcumt, o_fox, lse, do_fox)
    dqkv_n, dgates = _gdn_bwd(qkv_n, gcum, gu, gw, gqk, gtinv, sall, do_gdn)
    dsmall, pgg = _gates_bwd(proj, bias_row, alog_row, gates, dgates, dccol, dct)
    du, g_conv8 = _gdn_conv_bwd(proj, conv_w, conv_c, dqkv_n)
    t = x.shape[0]
    dproj = jnp.concatenate([du, dz, dfq, dfk.astype(bf16), dfv.astype(bf16), dsmall, jnp.zeros((t, D_CAT - SEG_SMALL - LANES), bf16)], axis=1)
    dh0_mm = _mm(dproj, w_cat, "nt", 256, D_MODEL, "mm_dh0")
    g_cat = _mm(h0b, dproj, "tn", 1024, 1280, "mm_gcat")
    grad_x, pg0 = _ln_in_bwd(x, dr1, dh0_mm, _row(small["ln_in_g"]))

    g_fox = pga[1, :FOX_DH] + pga[1, FOX_DH:]
    small_grads = dict(
        ln_in_g=pg0[0], ln_in_b=pg0[1], ln1_g=pg1[0], ln1_b=pg1[1], b_ple_gate=pg2[2], ln2_g=pg2[0], ln2_b=pg2[1],
        gdn_norm_g=pga[0], fox_norm_g=g_fox, a_log=pgg[1, 4:8], dt_bias=pgg[0, 4:8], b_f=pgg[0, 8:16], loss=pg2[3, 0:1])
    big_grads = dict(w_cat=g_cat, conv_w=g_conv8[:CONV_W], w_out=g_out, w_up=g_up, w_down=g_down, w_ple=g_ple, w_gate=g_gate)
    return grad_x, big_grads, small_grads


ANY = pl.BlockSpec(memory_space=pl.ANY)
CONV_PKT_ROWS = 16


def _mesh_pos():
    return lax.axis_index("x"), lax.axis_index("y"), lax.axis_index("c")


def _other_chips(x, y):
    return [(1 - x, y), (x, 1 - y), (1 - x, 1 - y)]


def _rcopy(src, dst, send_sem, recv_sem, dev):
    return pltpu.make_async_remote_copy(src_ref=src, dst_ref=dst, send_sem=send_sem, recv_sem=recv_sem,
                                        device_id=dev, device_id_type=MESH)


def _gather_weights(bufs, chunks, conv4):
    nw = len(bufs)
    base, n_ici = [], 0
    for nch in chunks:
        base.append(n_ici)
        n_ici += 3 * nch

    def body(*refs):
        outs = refs[nw + 1:2 * nw + 1]
        conv_ref = refs[2 * nw + 1]
        send_sems, recv_sems, csend, crecv = refs[2 * nw + 2:]
        x, y, c = _mesh_pos()
        q = 2 * x + y
        chips = _other_chips(x, y)
        sib = (x, y, 1 - c)

        def piece(i, slot, hf, ch):
            half = bufs[i].shape[1] // 2
            cr = half // chunks[i]
            return outs[i].at[slot, pl.ds(hf * half + ch * cr, cr)]

        started = []
        for i in range(nw):
            for ch in range(chunks[i]):
                for k, chip in enumerate(chips):
                    s = base[i] + k * chunks[i] + ch
                    cp = _rcopy(piece(i, q, c, ch), piece(i, q, c, ch), send_sems.at[s], recv_sems.at[s], (*chip, c))
                    cp.start()
                    started.append(cp)
        for k, chip in enumerate(chips):
            cp = _rcopy(conv_ref.at[q], conv_ref.at[q], csend.at[k], crecv.at[k], (*chip, c))
            cp.start()
            started.append(cp)
        for i in range(nw):
            for ch in range(chunks[i]):
                for k, chip in enumerate(chips):
                    s = base[i] + k * chunks[i] + ch
                    landed = piece(i, 2 * chip[0] + chip[1], c, ch)
                    _rcopy(landed, landed, send_sems.at[s], recv_sems.at[s], (*chip, c)).wait_recv()
                    cp = _rcopy(landed, landed, send_sems.at[n_ici + s], recv_sems.at[n_ici + s], sib)
                    cp.start()
                    started.append(cp)
        for i in range(nw):
            for ch in range(chunks[i]):
                for k, chip in enumerate(chips):
                    s = base[i] + k * chunks[i] + ch
                    passed = piece(i, 2 * chip[0] + chip[1], 1 - c, ch)
                    _rcopy(passed, passed, send_sems.at[n_ici + s], recv_sems.at[n_ici + s], sib).wait_recv()
        for k, chip in enumerate(chips):
            theirs = conv_ref.at[2 * chip[0] + chip[1]]
            _rcopy(theirs, theirs, csend.at[k], crecv.at[k], (*chip, c)).wait_recv()
        for cp in started:
            cp.wait_send()

    args = list(bufs) + [conv4]
    return pl.pallas_call(
        body, name="gather_weights", out_shape=tuple(jax.ShapeDtypeStruct(a.shape, a.dtype) for a in args),
        in_specs=[ANY] * (nw + 1), out_specs=(ANY,) * (nw + 1), input_output_aliases={i: i for i in range(nw + 1)},
        scratch_shapes=[pltpu.SemaphoreType.DMA((2 * n_ici,)), pltpu.SemaphoreType.DMA((2 * n_ici,)), pltpu.SemaphoreType.DMA((3,)),
                        pltpu.SemaphoreType.DMA((3,))],
    )(*args)


def _exchange_pairs(gs, small):
    nw = len(gs)

    def body(*refs):
        g_refs, small_ref = refs[:nw], refs[nw]
        b1_refs, all_ref = refs[nw + 1:2 * nw + 1], refs[2 * nw + 1]
        ssem, rsem, s2, r2, local_sem = refs[2 * nw + 2:]
        x, y, c = _mesh_pos()
        me = 4 * x + 2 * y + c
        sib = (x, y, 1 - c)
        own = pltpu.make_async_copy(small_ref, all_ref.at[me], local_sem)
        own.start()

        def pair_copy(i, d):
            half = gs[i].shape[1] // 2
            return _rcopy(g_refs[i].at[d, pl.ds((1 - c) * half, half)], b1_refs[i].at[d], ssem.at[i * N_CHIPS + d],
                          rsem.at[i * N_CHIPS + d], sib)

        started = []
        for i in range(nw):
            for d in range(N_CHIPS):
                cp = pair_copy(i, d)
                cp.start()
                started.append(cp)
        peers = []
        for r in range(1, 8):
            fx, fy, fc = (r >> 2) & 1, (r >> 1) & 1, r & 1
            peers.append((1 - x if fx else x, 1 - y if fy else y, 1 - c if fc else c))
        for r, peer in enumerate(peers):
            cp = _rcopy(small_ref, all_ref.at[me], s2.at[r], r2.at[r], peer)
            cp.start()
            started.append(cp)
        for i in range(nw):
            for d in range(N_CHIPS):
                pair_copy(i, d).wait_recv()
        for r, peer in enumerate(peers):
            _rcopy(small_ref, all_ref.at[4 * peer[0] + 2 * peer[1] + peer[2]], s2.at[r], r2.at[r], peer).wait_recv()
        for cp in started:
            cp.wait_send()
        own.wait()

    out_shape = tuple(jax.ShapeDtypeStruct((N_CHIPS, g.shape[1] // 2, g.shape[2]), g.dtype) for g in gs)
    return pl.pallas_call(
        body, name="exchange_pairs", out_shape=out_shape + (jax.ShapeDtypeStruct((8,) + small.shape, small.dtype),),
        in_specs=[ANY] * (nw + 1), out_specs=(ANY,) * (nw + 1),
        scratch_shapes=[pltpu.SemaphoreType.DMA((nw * N_CHIPS,)), pltpu.SemaphoreType.DMA((nw * N_CHIPS,)), pltpu.SemaphoreType.DMA((7,)),
                        pltpu.SemaphoreType.DMA((7,)), pltpu.SemaphoreType.DMA],
    )(*gs, small)


def _exchange_chips(a4s):
    nw = len(a4s)

    def body(*refs):
        a_refs, b2_refs = refs[:nw], refs[nw:2 * nw]
        ssem, rsem = refs[2 * nw:]
        x, y, c = _mesh_pos()
        chips = _other_chips(x, y)

        def chip_copy(i, k):
            chip = chips[k]
            return _rcopy(a_refs[i].at[2 * chip[0] + chip[1]], b2_refs[i].at[k], ssem.at[3 * i + k], rsem.at[3 * i + k], (*chip, c))

        started = []
        for i in range(nw):
            for k in range(3):
                cp = chip_copy(i, k)
                cp.start()
                started.append(cp)
        for i in range(nw):
            for k in range(3):
                chip_copy(i, k).wait_recv()
        for cp in started:
            cp.wait_send()

    return pl.pallas_call(
        body, name="exchange_chips", out_shape=tuple(jax.ShapeDtypeStruct((3,) + a.shape[1:], a.dtype) for a in a4s),
        in_specs=[ANY] * nw, out_specs=(ANY,) * nw,
        scratch_shapes=[pltpu.SemaphoreType.DMA((3 * nw,)), pltpu.SemaphoreType.DMA((3 * nw,))],
    )(*a4s)


def _share_halves(rs):
    nw = len(rs)

    def body(*refs):
        outs = refs[nw:2 * nw]
        ssem, rsem = refs[2 * nw:]
        x, y, c = _mesh_pos()
        sib = (x, y, 1 - c)
        started = []
        for i in range(nw):
            half = rs[i].shape[0] // 2
            mine = outs[i].at[pl.ds(c * half, half)]
            cp = _rcopy(mine, mine, ssem.at[i], rsem.at[i], sib)
            cp.start()
            started.append(cp)
        for i in range(nw):
            half = rs[i].shape[0] // 2
            theirs = outs[i].at[pl.ds((1 - c) * half, half)]
            _rcopy(theirs, theirs, ssem.at[i], rsem.at[i], sib).wait_recv()
        for cp in started:
            cp.wait_send()

    return pl.pallas_call(
        body, name="share_halves", out_shape=tuple(jax.ShapeDtypeStruct(r.shape, r.dtype) for r in rs),
        in_specs=[ANY] * nw, out_specs=(ANY,) * nw, input_output_aliases={i: i for i in range(nw)},
        scratch_shapes=[pltpu.SemaphoreType.DMA((nw,)), pltpu.SemaphoreType.DMA((nw,))],
    )(*rs)


ADD_ROWS = 256


def _add_pair(g4, b1, qc_idx, name):
    _, half, cols = b1.shape
    rb = min(ADD_ROWS, half)
    nb = half // rb

    def body(qc_ref, g_ref, b_ref, o_ref, ob_ref):
        a = g_ref[...] + b_ref[...]
        o_ref[...] = a
        ob_ref[...] = a.astype(bf16)

    blk = (1, rb, cols)
    out = pl.BlockSpec(blk, lambda d, i, qc: (d, i, 0))
    return pl.pallas_call(
        body, name=name,
        grid_spec=pltpu.PrefetchScalarGridSpec(
            num_scalar_prefetch=1, grid=(N_CHIPS, nb),
            in_specs=[pl.BlockSpec(blk, lambda d, i, qc: (d, qc[1] * nb + i, 0)), out],
            out_specs=(out, out)),
        out_shape=(jax.ShapeDtypeStruct(b1.shape, f32), jax.ShapeDtypeStruct(b1.shape, bf16)),
        compiler_params=_params(("parallel", "parallel")),
    )(qc_idx, g4, b1)


def _add_chips(a4, b2, qc_idx, name):
    _, half, cols = a4.shape
    rb = min(ADD_ROWS, half)
    nb = half // rb

    def body(qc_ref, a_ref, b_ref, o_ref):
        o_ref[...] = ((a_ref[0] + b_ref[0].astype(f32)) + b_ref[1].astype(f32)) + b_ref[2].astype(f32)

    return pl.pallas_call(
        body, name=name,
        grid_spec=pltpu.PrefetchScalarGridSpec(
            num_scalar_prefetch=1, grid=(nb,),
            in_specs=[pl.BlockSpec((1, rb, cols), lambda i, qc: (qc[0], i, 0)), pl.BlockSpec((3, rb, cols), lambda i, qc: (0, i, 0))],
            out_specs=pl.BlockSpec((rb, cols), lambda i, qc: (qc[1] * nb + i, 0))),
        out_shape=jax.ShapeDtypeStruct((2 * half, cols), f32),
        compiler_params=_params(("parallel",)),
    )(qc_idx, a4, b2)


def _adamw_math(w, g, m, v):
    m = ADAM_B1 * m + (1.0 - ADAM_B1) * g
    v = ADAM_B2 * v + (1.0 - ADAM_B2) * (g * g)
    m_hat = m / (1.0 - ADAM_B1 ** ADAM_STEP)
    v_hat = v / (1.0 - ADAM_B2 ** ADAM_STEP)
    return -ADAM_LR * (m_hat / (jnp.sqrt(v_hat) + ADAM_EPS) + ADAM_WD * w), m, v


def _adamw(w, g, m, v, name):
    rows, cols = w.shape
    rb = ADD_ROWS if rows % ADD_ROWS == 0 else rows

    def body(w_ref, g_ref, m_ref, v_ref, go_ref, d_ref, mo_ref, vo_ref):
        g = g_ref[...]
        go_ref[...] = g
        d_ref[...], mo_ref[...], vo_ref[...] = _adamw_math(w_ref[...], g, m_ref[...], v_ref[...])

    blk = pl.BlockSpec((rb, cols), lambda i: (i, 0))
    return pl.pallas_call(
        body, name=name, grid=(rows // rb,), in_specs=[blk] * 4, out_specs=(blk,) * 4,
        out_shape=(jax.ShapeDtypeStruct(w.shape, f32),) * 4, compiler_params=_params(("parallel",)),
    )(w, g, m, v)


def _small_sum_adamw(all_pkts, w, m, v):
    def body(a_ref, w_ref, m_ref, v_ref, g_ref, d_ref, mo_ref, vo_ref):
        g = a_ref[0]
        for r in range(1, 8):
            g = g + a_ref[r]
        g_ref[...] = g
        d_ref[...], mo_ref[...], vo_ref[...] = _adamw_math(w_ref[...], g, m_ref[...], v_ref[...])

    return pl.pallas_call(body, name="small_sum_adamw", out_shape=(jax.ShapeDtypeStruct(w.shape, f32),) * 4)(all_pkts, w, m, v)


SHARDED = (("w_in", (D_MODEL, D_IN // N_CHIPS), 2), ("w_out", (D_MODEL // N_CHIPS, D_MODEL), 1), ("w_up", (D_MODEL, D_FF // N_CHIPS), 2),
           ("w_ple_gate", (D_MODEL // N_CHIPS, D_MODEL), 1), ("w_ple", (D_PLE, D_MODEL // N_CHIPS), 1),
           ("w_down", (D_FF // N_CHIPS, D_MODEL), 2))
SMALL_LAYOUT = (("ln_in_g", 0, 1024), ("ln_in_b", 8, 1024), ("ln1_g", 16, 1024), ("ln1_b", 24, 1024), ("b_ple_gate", 32, 1024),
                ("ln2_g", 40, 1024), ("ln2_b", 48, 1024), ("gdn_norm_g", 56, 128), ("fox_norm_g", 57, 64), ("a_log", 58, 4),
                ("dt_bias", 59, 4), ("b_f", 60, 8), ("loss", 61, 1))
SMALL_CONV_ROW = 64
SMALL_ROWS = 128


def _pack_small(vals, conv=None):
    rows = []
    nxt = 0
    for n, r0, size in SMALL_LAYOUT:
        assert r0 == nxt
        v = vals[n].reshape(-1).astype(f32) if n in vals else jnp.zeros((size,), f32)
        nrows = -(-size // LANES)
        rows.append(jnp.pad(v, (0, nrows * LANES - size)).reshape(nrows, LANES))
        nxt = r0 + nrows
    rows.append(jnp.zeros((SMALL_CONV_ROW - nxt, LANES), f32))
    conv_rows = CONV_W * GDN_QKV // LANES
    rows.append(jnp.zeros((conv_rows, LANES), f32) if conv is None else conv.reshape(conv_rows, LANES))
    rows.append(jnp.zeros((SMALL_ROWS - SMALL_CONV_ROW - conv_rows, LANES), f32))
    return jnp.concatenate(rows, axis=0)


def _unpack_small(pkt, shapes):
    out = {}
    for n, r0, size in SMALL_LAYOUT:
        if n in shapes:
            nrows = -(-size // LANES)
            out[n] = pkt[r0:r0 + nrows].reshape(-1)[:size].reshape(shapes[n])
    return out


WEIGHTS = ("ln_in_g", "ln_in_b", "w_in", "conv_w", "a_log", "dt_bias", "gdn_norm_g", "b_f", "fox_norm_g", "w_out", "ln1_g", "ln1_b",
           "w_up", "w_down", "w_ple", "w_ple_gate", "b_ple_gate", "ln2_g", "ln2_b")
SMALL_NAMES = tuple(n for n, _, _ in SMALL_LAYOUT if n != "loss")


def kernel(x, p, ln_in_g, ln_in_b, w_in, conv_w, a_log, dt_bias, gdn_norm_g, b_f, fox_norm_g, w_out, ln1_g, ln1_b, w_up, w_down, w_ple, w_ple_gate, b_ple_gate, ln2_g, ln2_b, loss_target, m_ln_in_g, m_ln_in_b, m_w_in, m_conv_w, m_a_log, m_dt_bias, m_gdn_norm_g, m_b_f, m_fox_norm_g, m_w_out, m_ln1_g, m_ln1_b, m_w_up, m_w_down, m_w_ple, m_w_ple_gate, m_b_ple_gate, m_ln2_g, m_ln2_b, v_ln_in_g, v_ln_in_b, v_w_in, v_conv_w, v_a_log, v_dt_bias, v_gdn_norm_g, v_b_f, v_fox_norm_g, v_w_out, v_ln1_g, v_ln1_b, v_w_up, v_w_down, v_w_ple, v_w_ple_gate, v_b_ple_gate, v_ln2_g, v_ln2_b):
    given = dict(locals())
    w = {n: given[n] for n in WEIGHTS}
    m = {n: given["m_" + n] for n in WEIGHTS}
    v = {n: given["v_" + n] for n in WEIGHTS}
    xi, yi, ci = _mesh_pos()
    q = 2 * xi + yi

    def slot_buffer(val, dtype):
        return lax.dynamic_update_slice(lax.empty((N_CHIPS,) + val.shape, dtype), val.astype(dtype)[None], (q, 0, 0))

    conv_rows = CONV_W * GDN_QKV // N_CHIPS // LANES
    conv_pkt = jnp.pad(w["conv_w"][0].reshape(-1, LANES), ((0, CONV_PKT_ROWS - conv_rows), (0, 0)))
    gathered = _gather_weights([slot_buffer(w[n][0], bf16) for n, _, _ in SHARDED], [nch for _, _, nch in SHARDED],
                               slot_buffer(conv_pkt, f32))
    full = dict(zip([n for n, _, _ in SHARDED], gathered[:-1]))
    conv_all = gathered[-1]
    conv_full = jnp.concatenate([conv_all[d, :conv_rows].reshape(CONV_W, GDN_QKV // N_CHIPS) for d in range(N_CHIPS)], axis=1)
    wi = jnp.concatenate([full["w_in"][d] for d in range(N_CHIPS)], axis=1)
    w_cat = jnp.concatenate([wi[:, :OFF_BETA], wi[:, OFF_FOX:OFF_F], wi[:, OFF_BETA:OFF_FOX], wi[:, OFF_F:],
                             jnp.zeros((D_MODEL, D_CAT - D_IN), bf16)], axis=1)

    small = {n: w[n] for n in SMALL_NAMES}
    grad_x, big, small_g = _device_grads(
        x[0], p[0, 0], loss_target[0], small, w_cat, conv_full, full["w_out"].reshape(D_MODEL, D_MODEL), full["w_up"],
        full["w_down"].reshape(D_FF, D_MODEL), full["w_ple"], full["w_ple_gate"].reshape(D_MODEL, D_MODEL))

    gc = big["w_cat"]
    g_in = jnp.concatenate([gc[:, :OFF_BETA], gc[:, SEG_SMALL:SEG_SMALL + 8], gc[:, SEG_FOX:SEG_SMALL],
                            gc[:, SEG_SMALL + 8:SEG_SMALL + 16]], axis=1)
    shard_cols = D_IN // N_CHIPS
    by_dest = dict(w_in=jnp.stack([g_in[:, d * shard_cols:(d + 1) * shard_cols] for d in range(N_CHIPS)]), w_up=big["w_up"],
                   w_ple=big["w_ple"], w_out=big["w_out"], w_down=big["w_down"], w_ple_gate=big["w_gate"])
    gs = [by_dest[n].reshape((N_CHIPS,) + shp) for n, shp, _ in SHARDED]
    *b1s, small_all = _exchange_pairs(gs, _pack_small(small_g, big["conv_w"]))
    qc = jnp.stack([q, ci]).astype(jnp.int32)
    sums = [_add_pair(g, b1, qc, "add_pair_" + n) for g, b1, (n, _, _) in zip(gs, b1s, SHARDED)]
    b2s = _exchange_chips([ab for _, ab in sums])
    reduced = _share_halves([_add_chips(a, b2, qc, "add_chips_" + n) for (a, _), b2, (n, _, _) in zip(sums, b2s, SHARDED)])

    grads, delta, new_m, new_v = {}, {}, {}, {}
    for g, (n, _, _) in zip(reduced, SHARDED):
        outs = _adamw(w[n][0], g, m[n][0], v[n][0], "adamw_" + n)
        grads[n], delta[n], new_m[n], new_v[n] = (a.reshape(w[n].shape) for a in outs)
    shapes = {n: w[n].shape for n in SMALL_NAMES}
    g_pkt, d_pkt, m_pkt, v_pkt = _small_sum_adamw(small_all, _pack_small(w), _pack_small(m), _pack_small(v))
    for dst, pkt in ((grads, g_pkt), (delta, d_pkt), (new_m, m_pkt), (new_v, v_pkt)):
        dst.update(_unpack_small(pkt, shapes))
    conv_rows_all = CONV_W * GDN_QKV // LANES
    conv_g_full = g_pkt[SMALL_CONV_ROW:SMALL_CONV_ROW + conv_rows_all].reshape(CONV_W, GDN_QKV)
    conv_g = lax.dynamic_slice_in_dim(conv_g_full, q * (GDN_QKV // N_CHIPS), GDN_QKV // N_CHIPS, axis=1)
    outs = _adamw(w["conv_w"][0], conv_g, m["conv_w"][0], v["conv_w"][0], "adamw_conv_w")
    grads["conv_w"], delta["conv_w"], new_m["conv_w"], new_v["conv_w"] = (a.reshape(w["conv_w"].shape) for a in outs)
    loss = g_pkt[61, 0]
    return (loss, grad_x[None], *[grads[n] for n in WEIGHTS], *[delta[n] for n in WEIGHTS],
            *[new_m[n] for n in WEIGHTS], *[new_v[n] for n in WEIGHTS])
```

```python
import functools

import jax
import jax.numpy as jnp
from jax import lax
from jax.experimental import pallas as pl
from jax.experimental.pallas import tpu as pltpu

f32 = jnp.float32
bf16 = jnp.bfloat16
HI = lax.Precision.HIGHEST
MESH = pl.DeviceIdType.MESH

D_MODEL = 1024
CHUNK = 64
GDN_HEADS = 4
GDN_DK = 128
FOX_HEADS = 8
FOX_DH = 64
CONV_W = 4
D_FF = 4096
D_PLE = 256
LN_EPS = 1e-5
NORM_EPS = 1e-6
ALPHA = 2.0 ** 0.25
GDN_QKV = 1536
OFF_Z = 1536
OFF_BETA = 2048
OFF_FOX = 2056
OFF_F = 3592
D_IN = 3600
ADAM_LR = 0.001
ADAM_B1 = 0.9
ADAM_B2 = 0.999
ADAM_EPS = 1e-08
ADAM_WD = 0.01
ADAM_STEP = 10

SEG_FOX = 2048
SEG_SMALL = 3584
D_CAT = 3840
LANES = 128
TOK_BLK = 256
FOX_BQ = 256
VMEM_LIMIT = 56 * 1024 * 1024
NEG = -1e30

N_CHIPS = 4


def _params(sem=None, **kw):
    return pltpu.CompilerParams(dimension_semantics=sem, vmem_limit_bytes=VMEM_LIMIT, **kw)


def _sigmoid(x):
    return 1.0 / (1.0 + jnp.exp(-x))


def _softplus(x):
    return jnp.maximum(x, 0.0) + jnp.log(1.0 + jnp.exp(-jnp.abs(x)))


def _ln_fwd(x, g, b):
    mu = jnp.mean(x, -1, keepdims=True)
    xc = x - mu
    var = jnp.mean(xc * xc, -1, keepdims=True)
    rstd = lax.rsqrt(var + LN_EPS)
    xhat = xc * rstd
    return xhat * g + b, xhat, rstd


def _ln_bwd(dy, xhat, rstd, g):
    dxh = dy * g
    m1 = jnp.mean(dxh, -1, keepdims=True)
    m2 = jnp.mean(dxh * xhat, -1, keepdims=True)
    return rstd * (dxh - m1 - xhat * m2)


def _dot(a, b, prec=HI):
    return jnp.dot(a, b, precision=prec, preferred_element_type=f32)


def _dot_nt(a, b, prec=HI):
    return lax.dot_general(a, b, (((1,), (1,)), ((), ())), precision=prec, preferred_element_type=f32)


def _dot_tn(a, b, prec=HI):
    return lax.dot_general(a, b, (((0,), (0,)), ((), ())), precision=prec, preferred_element_type=f32)


def _bdot(a, b):
    return _dot(a.astype(bf16), b.astype(bf16), None)


def _bdot_nt(a, b):
    return _dot_nt(a.astype(bf16), b.astype(bf16), None)


def _bdot_tn(a, b):
    return _dot_tn(a.astype(bf16), b.astype(bf16), None)


def _lane(shape):
    return lax.broadcasted_iota(jnp.int32, shape, len(shape) - 1)


def _mm(a, b, mode, tm, tn, name, out_dtype=f32, epi=None, extra=None, shards=1):
    if mode == "nn":
        (m, k), n = a.shape, b.shape[-1] * shards
    elif mode == "nt":
        (m, k), n = a.shape, b.shape[-2]
    else:
        (k, m), n = a.shape, b.shape[1]
    assert m % tm == 0 and n % tn == 0, (name, m, n, tm, tn)
    per = (n // shards) // tn
    assert mode == "nt" or per * tn * shards == n, (name, n, tn, shards)
    nc = 512 if tn % 512 == 0 else (256 if tn % 256 == 0 else 128)
    ks = k // shards

    def body(a_ref, b_ref, *rest):
        for n0 in range(0, tn, nc):
            if mode == "nn":
                acc = jnp.dot(a_ref[...], b_ref[:, n0:n0 + nc], preferred_element_type=f32)
            elif mode == "nt" and shards > 1:
                acc = jnp.zeros((tm, nc), f32)
                for d in range(shards):
                    acc = acc + lax.dot_general(a_ref[:, d * ks:(d + 1) * ks], b_ref[d, n0:n0 + nc, :], (((1,), (1,)), ((), ())),
                                                preferred_element_type=f32)
            elif mode == "nt":
                acc = lax.dot_general(a_ref[...], b_ref[n0:n0 + nc, :], (((1,), (1,)), ((), ())), preferred_element_type=f32)
            else:
                acc = lax.dot_general(a_ref[...], b_ref[:, n0:n0 + nc], (((0,), (0,)), ((), ())), preferred_element_type=f32)
            if epi == "relu2":
                up_ref, act_ref = rest
                up_ref[:, n0:n0 + nc] = acc
                r = jnp.maximum(acc, 0.0)
                act_ref[:, n0:n0 + nc] = (r * r).astype(bf16)
            elif epi == "relu2_bwd":
                up_ref, o_ref = rest
                o_ref[:, n0:n0 + nc] = (acc * (2.0 * jnp.maximum(up_ref[:, n0:n0 + nc], 0.0))).astype(bf16)
            else:
                (o_ref,) = rest
                o_ref[:, n0:n0 + nc] = acc.astype(out_dtype)

    if mode == "tn":
        a_spec = pl.BlockSpec((k, tm), lambda j, i: (0, i))
    else:
        a_spec = pl.BlockSpec((tm, k), lambda j, i: (i, 0))
    if mode == "nt" and shards > 1:
        b_spec = pl.BlockSpec((shards, tn, ks), lambda j, i: (0, j, 0))
    elif mode == "nt":
        b_spec = pl.BlockSpec((tn, k), lambda j, i: (j, 0))
    elif mode == "nn" and shards > 1:
        b_spec = pl.BlockSpec((None, k, tn), lambda j, i: (j // per, 0, j % per))
    else:
        b_spec = pl.BlockSpec((k, tn), lambda j, i: (0, j))
    o_spec = pl.BlockSpec((tm, tn), lambda j, i: (i, j))
    in_specs = [a_spec, b_spec]
    args = [a, b]
    if epi == "relu2":
        out_shape = (jax.ShapeDtypeStruct((m, n), f32), jax.ShapeDtypeStruct((m, n), bf16))
        out_specs = (o_spec, o_spec)
    elif epi == "relu2_bwd":
        in_specs.append(o_spec)
        args.append(extra)
        out_shape = jax.ShapeDtypeStruct((m, n), bf16)
        out_specs = o_spec
    elif mode == "tn" and shards > 1:
        out_shape = jax.ShapeDtypeStruct((shards, m, n // shards), out_dtype)
        out_specs = pl.BlockSpec((None, tm, tn), lambda j, i: (j // per, i, j % per))
    else:
        out_shape = jax.ShapeDtypeStruct((m, n), out_dtype)
        out_specs = o_spec
    return pl.pallas_call(
        body, name=name, grid=(n // tn, m // tm), in_specs=in_specs, out_specs=out_specs, out_shape=out_shape,
        compiler_params=_params(("parallel", "parallel")),
    )(*args)


def _row_spec(width, col=0):
    return pl.BlockSpec((TOK_BLK, width), lambda i: (i, col))


def _vec_spec(rows, width):
    return pl.BlockSpec((rows, width), lambda i: (0, 0))


def _ln_in(x, g, b):
    t, d = x.shape

    def body(x_ref, g_ref, b_ref, h_ref, hb_ref):
        h, _, _ = _ln_fwd(x_ref[...], g_ref[...], b_ref[...])
        h_ref[...] = h
        hb_ref[...] = h.astype(bf16)

    return pl.pallas_call(
        body, name="ln_in", grid=(t // TOK_BLK,),
        in_specs=[_row_spec(d), _vec_spec(1, d), _vec_spec(1, d)],
        out_specs=(_row_spec(d), _row_spec(d)),
        out_shape=(jax.ShapeDtypeStruct((t, d), f32), jax.ShapeDtypeStruct((t, d), bf16)),
        compiler_params=_params(("parallel",)),
    )(x, g, b)


def _attn_post(o_gdn, proj, o_fox, g_gdn, g_fox2):
    t = o_gdn.shape[0]

    def body(og_ref, z_ref, of_ref, gg_ref, gf_ref, out_ref):
        for h in range(GDN_HEADS):
            sl = slice(h * LANES, (h + 1) * LANES)
            og = og_ref[:, sl]
            z = z_ref[:, sl]
            r = lax.rsqrt(jnp.mean(og * og, -1, keepdims=True) + NORM_EPS)
            out_ref[:, sl] = (og * r * gg_ref[...] * (z * _sigmoid(z))).astype(bf16)
        lo = _lane((TOK_BLK, LANES)) < FOX_DH
        for pr in range(FOX_HEADS // 2):
            sl = slice(pr * LANES, (pr + 1) * LANES)
            of = of_ref[:, sl]
            sq = of * of
            s0 = jnp.sum(jnp.where(lo, sq, 0.0), -1, keepdims=True)
            s1 = jnp.sum(jnp.where(lo, 0.0, sq), -1, keepdims=True)
            r = lax.rsqrt(jnp.where(lo, s0, s1) * (1.0 / FOX_DH) + NORM_EPS)
            out_ref[:, 512 + pr * LANES:512 + (pr + 1) * LANES] = (of * r * gf_ref[...]).astype(bf16)

    return pl.pallas_call(
        body, name="attn_post", grid=(t // TOK_BLK,),
        in_specs=[_row_spec(512), _row_spec(512, OFF_Z // 512), _row_spec(512), _vec_spec(1, LANES), _vec_spec(1, LANES)],
        out_specs=_row_spec(D_MODEL),
        out_shape=jax.ShapeDtypeStruct((t, D_MODEL), bf16),
        compiler_params=_params(("parallel",)),
    )(o_gdn, proj, o_fox, g_gdn, g_fox2)


def _attn_post_bwd(dattn, o_gdn, proj, o_fox, g_gdn, g_fox2):
    t = o_gdn.shape[0]

    def body(da_ref, og_ref, z_ref, of_ref, gg_ref, gf_ref, dog_ref, dz_ref, dof_ref, pg_ref):
        i = pl.program_id(0)

        @pl.when(i == 0)
        def _():
            pg_ref[...] = jnp.zeros_like(pg_ref)

        dgg = jnp.zeros((1, LANES), f32)
        for h in range(GDN_HEADS):
            sl = slice(h * LANES, (h + 1) * LANES)
            og = og_ref[:, sl]
            z = z_ref[:, sl]
            dout = da_ref[:, sl]
            g = gg_ref[...]
            r = lax.rsqrt(jnp.mean(og * og, -1, keepdims=True) + NORM_EPS)
            sg = _sigmoid(z)
            silu = z * sg
            ng = og * r * g
            dng = dout * silu
            dz_ref[:, sl] = (dout * ng * (sg * (1.0 + z * (1.0 - sg)))).astype(bf16)
            dgg = dgg + jnp.sum(dng * og * r, 0, keepdims=True)
            gd = dng * g
            dog_ref[:, sl] = r * gd - og * (r * r * r) * jnp.mean(og * gd, -1, keepdims=True)
        pg_ref[0:1, :] += dgg
        lo = _lane((TOK_BLK, LANES)) < FOX_DH
        dgf = jnp.zeros((1, LANES), f32)
        for pr in range(FOX_HEADS // 2):
            sl = slice(pr * LANES, (pr + 1) * LANES)
            of = of_ref[:, sl]
            dout = da_ref[:, 512 + pr * LANES:512 + (pr + 1) * LANES]
            g = gf_ref[...]
            sq = of * of
            s0 = jnp.sum(jnp.where(lo, sq, 0.0), -1, keepdims=True)
            s1 = jnp.sum(jnp.where(lo, 0.0, sq), -1, keepdims=True)
            r = lax.rsqrt(jnp.where(lo, s0, s1) * (1.0 / FOX_DH) + NORM_EPS)
            dgf = dgf + jnp.sum(dout * of * r, 0, keepdims=True)
            gd = dout * g
            xg = of * gd
            m0 = jnp.sum(jnp.where(lo, xg, 0.0), -1, keepdims=True)
            m1 = jnp.sum(jnp.where(lo, 0.0, xg), -1, keepdims=True)
            dof_ref[:, sl] = r * gd - of * (r * r * r) * (jnp.where(lo, m0, m1) * (1.0 / FOX_DH))
        pg_ref[1:2, :] += dgf

    return pl.pallas_call(
        body, name="attn_post_bwd", grid=(t // TOK_BLK,),
        in_specs=[_row_spec(D_MODEL), _row_spec(512), _row_spec(512, OFF_Z // 512), _row_spec(512), _vec_spec(1, LANES), _vec_spec(1, LANES)],
        out_specs=(_row_spec(512), _row_spec(512), _row_spec(512), _vec_spec(8, LANES)),
        out_shape=(jax.ShapeDtypeStruct((t, 512), f32), jax.ShapeDtypeStruct((t, 512), bf16),
                   jax.ShapeDtypeStruct((t, 512), f32), jax.ShapeDtypeStruct((8, LANES), f32)),
        compiler_params=_params(("arbitrary",)),
    )(dattn, o_gdn, proj, o_fox, g_gdn, g_fox2)


def _ln1(h0, mix, g, b):
    t, d = h0.shape

    def body(h0_ref, mix_ref, g_ref, b_ref, h_ref, hb_ref, xh_ref, rs_ref):
        h, xhat, rstd = _ln_fwd(ALPHA * h0_ref[...] + mix_ref[...], g_ref[...], b_ref[...])
        h_ref[...] = h
        hb_ref[...] = h.astype(bf16)
        xh_ref[...] = xhat
        rs_ref[...] = jnp.broadcast_to(rstd, rs_ref.shape)

    return pl.pallas_call(
        body, name="ln1", grid=(t // TOK_BLK,),
        in_specs=[_row_spec(d), _row_spec(d), _vec_spec(1, d), _vec_spec(1, d)],
        out_specs=(_row_spec(d), _row_spec(d), _row_spec(d), _row_spec(LANES)),
        out_shape=(jax.ShapeDtypeStruct((t, d), f32), jax.ShapeDtypeStruct((t, d), bf16),
                   jax.ShapeDtypeStruct((t, d), f32), jax.ShapeDtypeStruct((t, LANES), f32)),
        compiler_params=_params(("parallel",)),
    )(h0, mix, g, b)


def _ln2_loss(h1, ff, pe, gp, b_gate, g, b, target):
    t, d = h1.shape

    def body(h1_ref, ff_ref, pe_ref, gp_ref, bg_ref, g_ref, b_ref, t_ref, dr_ref, drb_ref, dpe_ref, dgp_ref, pg_ref):
        i = pl.program_id(0)

        @pl.when(i == 0)
        def _():
            pg_ref[...] = jnp.zeros_like(pg_ref)

        sig = _sigmoid(gp_ref[...] + bg_ref[...])
        pe = pe_ref[...]
        r2 = ALPHA * h1_ref[...] + ff_ref[...] + pe * sig
        y, xhat, rstd = _ln_fwd(r2, g_ref[...], b_ref[...])
        err = y - t_ref[...]
        dy = err * (1.0 / d)
        dr = _ln_bwd(dy, xhat, rstd, g_ref[...])
        dr_ref[...] = dr
        drb_ref[...] = dr.astype(bf16)
        dpe_ref[...] = (dr * sig).astype(bf16)
        dgp = dr * pe * sig * (1.0 - sig)
        dgp_ref[...] = dgp.astype(bf16)
        pg_ref[0:1, :] += jnp.sum(dy * xhat, 0, keepdims=True)
        pg_ref[1:2, :] += jnp.sum(dy, 0, keepdims=True)
        pg_ref[2:3, :] += jnp.sum(dgp, 0, keepdims=True)
        pg_ref[3:4, :] += 0.5 * jnp.sum(jnp.mean(err * err, -1, keepdims=True), 0, keepdims=True)

    return pl.pallas_call(
        body, name="ln2_loss", grid=(t // TOK_BLK,),
        in_specs=[_row_spec(d)] * 4 + [_vec_spec(1, d)] * 3 + [_row_spec(d)],
        out_specs=(_row_spec(d), _row_spec(d), _row_spec(d), _row_spec(d), _vec_spec(8, d)),
        out_shape=(jax.ShapeDtypeStruct((t, d), f32), jax.ShapeDtypeStruct((t, d), bf16), jax.ShapeDtypeStruct((t, d), bf16),
                   jax.ShapeDtypeStruct((t, d), bf16), jax.ShapeDtypeStruct((8, d), f32)),
        compiler_params=_params(("arbitrary",)),
    )(h1, ff, pe, gp, b_gate, g, b, target)


def _ln1_bwd(dr2, da, db, xhat, rstd, g):
    t, d = dr2.shape

    def body(dr2_ref, da_ref, db_ref, xh_ref, rs_ref, g_ref, dr_ref, drb_ref, pg_ref):
        i = pl.program_id(0)

        @pl.when(i == 0)
        def _():
            pg_ref[...] = jnp.zeros_like(pg_ref)

        dh = ALPHA * dr2_ref[...] + da_ref[...] + db_ref[...]
        xhat = xh_ref[...]
        dr = _ln_bwd(dh, xhat, rs_ref[:, 0:1], g_ref[...])
        dr_ref[...] = dr
        drb_ref[...] = dr.astype(bf16)
        pg_ref[0:1, :] += jnp.sum(dh * xhat, 0, keepdims=True)
        pg_ref[1:2, :] += jnp.sum(dh, 0, keepdims=True)

    return pl.pallas_call(
        body, name="ln1_bwd", grid=(t // TOK_BLK,),
        in_specs=[_row_spec(d)] * 4 + [_row_spec(LANES), _vec_spec(1, d)],
        out_specs=(_row_spec(d), _row_spec(d), _vec_spec(8, d)),
        out_shape=(jax.ShapeDtypeStruct((t, d), f32), jax.ShapeDtypeStruct((t, d), bf16), jax.ShapeDtypeStruct((8, d), f32)),
        compiler_params=_params(("arbitrary",)),
    )(dr2, da, db, xhat, rstd, g)


def _ln_in_bwd(x, dr1, dmm, g):
    t, d = x.shape

    def body(x_ref, dr1_ref, dmm_ref, g_ref, dx_ref, pg_ref):
        i = pl.program_id(0)

        @pl.when(i == 0)
        def _():
            pg_ref[...] = jnp.zeros_like(pg_ref)

        dh = ALPHA * dr1_ref[...] + dmm_ref[...]
        _, xhat, rstd = _ln_fwd(x_ref[...], g_ref[...], 0.0)
        dx_ref[...] = _ln_bwd(dh, xhat, rstd, g_ref[...])
        pg_ref[0:1, :] += jnp.sum(dh * xhat, 0, keepdims=True)
        pg_ref[1:2, :] += jnp.sum(dh, 0, keepdims=True)

    return pl.pallas_call(
        body, name="ln_in_bwd", grid=(t // TOK_BLK,),
        in_specs=[_row_spec(d)] * 3 + [_vec_spec(1, d)],
        out_specs=(_row_spec(d), _vec_spec(8, d)),
        out_shape=(jax.ShapeDtypeStruct((t, d), f32), jax.ShapeDtypeStruct((8, d), f32)),
        compiler_params=_params(("arbitrary",)),
    )(x, dr1, dmm, g)


def _tri(n, upper=False, strict=False):
    r = lax.broadcasted_iota(jnp.int32, (n, n), 0)
    c = lax.broadcasted_iota(jnp.int32, (n, n), 1)
    if upper:
        m = (c > r) if strict else (c >= r)
    else:
        m = (c < r) if strict else (c <= r)
    return jnp.where(m, 1.0, 0.0).astype(f32)


def _gate_values(x, bias, alog, lane):
    z = x + bias
    return jnp.where(lane < 4, _sigmoid(z), jnp.where(lane < 8, -jnp.exp(alog) * _softplus(z), jnp.where(lane < 16, -_softplus(-z), 0.0)))


def _gates(proj, bias_row, alog_row):
    t = proj.shape[0]
    nch = t // CHUNK

    def body(x_ref, bias_ref, alog_ref, gates_ref, gcum_ref, gcumt_ref):
        lane = _lane((t, LANES))
        gates = _gate_values(x_ref[...], bias_ref[...], alog_ref[...], lane)
        gates_ref[...] = gates
        g3 = gates.reshape(nch, CHUNK, LANES)
        tri = jnp.broadcast_to(_tri(CHUNK)[None], (nch, CHUNK, CHUNK))
        loc = jnp.einsum("bij,bjk->bik", tri, g3, precision=HI, preferred_element_type=f32)
        tot = jnp.sum(g3, axis=1)
        offs = _dot(_tri(nch, strict=True), tot)
        glob = loc + offs[:, None, :]
        lane3 = _lane((nch, CHUNK, LANES))
        gcum = jnp.where(lane3 < 4, g3, jnp.where(lane3 < 8, loc, glob)).reshape(t, LANES)
        gcum_ref[...] = gcum
        gcumt_ref[...] = gcum.T

    return pl.pallas_call(
        body, name="gates", grid=(1,),
        in_specs=[pl.BlockSpec((t, LANES), lambda i: (0, SEG_SMALL // LANES)), _vec_spec(1, LANES), _vec_spec(1, LANES)],
        out_specs=(pl.BlockSpec((t, LANES), lambda i: (0, 0)), pl.BlockSpec((t, LANES), lambda i: (0, 0)),
                   pl.BlockSpec((LANES, t), lambda i: (0, 0))),
        out_shape=(jax.ShapeDtypeStruct((t, LANES), f32), jax.ShapeDtypeStruct((t, LANES), f32), jax.ShapeDtypeStruct((LANES, t), f32)),
        compiler_params=_params(("arbitrary",)),
    )(proj, bias_row, alog_row)


def _gates_bwd(proj, bias_row, alog_row, gates, dgates, dccol, dct):
    t = proj.shape[0]
    nch = t // CHUNK

    def body(x_ref, bias_ref, alog_ref, gates_ref, dg_ref, dcc_ref, dct_ref, dx_ref, pg_ref):
        lane = _lane((t, LANES))
        d = dg_ref[...] + dcc_ref[...] + dct_ref[...].T
        d3 = d.reshape(nch, CHUNK, LANES)
        tri = jnp.broadcast_to(_tri(CHUNK, upper=True)[None], (nch, CHUNK, CHUNK))
        loc = jnp.einsum("bij,bjk->bik", tri, d3, precision=HI, preferred_element_type=f32)
        tot = jnp.sum(d3, axis=1)
        offs = _dot(_tri(nch, upper=True, strict=True), tot)
        glob = loc + offs[:, None, :]
        lane3 = _lane((nch, CHUNK, LANES))
        dpre = jnp.where(lane3 < 4, d3, jnp.where(lane3 < 8, loc, glob)).reshape(t, LANES)
        z = x_ref[...] + bias_ref[...]
        sg = _sigmoid(z)
        dx = jnp.where(lane < 4, dpre * sg * (1.0 - sg),
                       jnp.where(lane < 8, dpre * (-jnp.exp(alog_ref[...])) * sg, jnp.where(lane < 16, dpre * (1.0 - sg), 0.0)))
        dx_ref[...] = dx.astype(bf16)
        pg_ref[...] = jnp.zeros_like(pg_ref)
        pg_ref[0:1, :] = jnp.sum(dx, 0, keepdims=True)
        pg_ref[1:2, :] = jnp.sum(jnp.where((lane >= 4) & (lane < 8), dpre * gates_ref[...], 0.0), 0, keepdims=True)

    full = pl.BlockSpec((t, LANES), lambda i: (0, 0))
    return pl.pallas_call(
        body, name="gates_bwd", grid=(1,),
        in_specs=[pl.BlockSpec((t, LANES), lambda i: (0, SEG_SMALL // LANES)), _vec_spec(1, LANES), _vec_spec(1, LANES),
                  full, full, full, pl.BlockSpec((LANES, t), lambda i: (0, 0))],
        out_specs=(full, _vec_spec(8, LANES)),
        out_shape=(jax.ShapeDtypeStruct((t, LANES), bf16), jax.ShapeDtypeStruct((8, LANES), f32)),
        compiler_params=_params(("arbitrary",)),
    )(proj, bias_row, alog_row, gates, dgates, dccol, dct)


def _conv_act(u, cw, row, t):
    c = cw[3:4, :] * u
    for jj in range(CONV_W - 1):
        sh = CONV_W - 1 - jj
        c = c + cw[jj:jj + 1, :] * jnp.where(row >= sh, pltpu.roll(u, sh, axis=0), 0.0)
    return c


def _gdn_conv(proj, conv_w):
    t = proj.shape[0]
    nblk = GDN_QKV // LANES

    def body(u_ref, cw_ref, c_ref, y_ref):
        j = pl.program_id(0)
        row = lax.broadcasted_iota(jnp.int32, (t, LANES), 0)
        c = _conv_act(u_ref[...], cw_ref[...], row, t)
        c_ref[...] = c
        s = c * _sigmoid(c)
        r = lax.rsqrt(jnp.sum(s * s, -1, keepdims=True) + NORM_EPS)
        scale = jnp.where(j < GDN_HEADS, GDN_DK ** -0.5, 1.0)
        y_ref[...] = jnp.where(j < 2 * GDN_HEADS, s * (r * scale), s)

    blk = pl.BlockSpec((t, LANES), lambda j: (0, j))
    return pl.pallas_call(
        body, name="gdn_conv", grid=(nblk,),
        in_specs=[blk, pl.BlockSpec((CONV_W, LANES), lambda j: (0, j))],
        out_specs=(blk, blk),
        out_shape=(jax.ShapeDtypeStruct((t, GDN_QKV), f32), jax.ShapeDtypeStruct((t, GDN_QKV), f32)),
        compiler_params=_params(("parallel",)),
    )(proj, conv_w)


def _gdn_conv_bwd(proj, conv_w, c, dy, comm=None):
    t = proj.shape[0]
    nblk = GDN_QKV // LANES

    def body(u_ref, cw_ref, c_ref, dy_ref, du_ref, dcw_ref):
        j = pl.program_id(0)
        row = lax.broadcasted_iota(jnp.int32, (t, LANES), 0)
        u = u_ref[...]
        cw = cw_ref[...]
        c = c_ref[...]
        dy = dy_ref[...]
        sg = _sigmoid(c)
        s = c * sg
        r = lax.rsqrt(jnp.sum(s * s, -1, keepdims=True) + NORM_EPS)
        n = s * r
        scale = jnp.where(j < GDN_HEADS, GDN_DK ** -0.5, 1.0)
        dn = dy * scale
        ds = jnp.where(j < 2 * GDN_HEADS, r * (dn - n * jnp.sum(dn * n, -1, keepdims=True)), dy)
        dc = ds * (sg * (1.0 + c * (1.0 - sg)))
        du = cw[3:4, :] * dc
        dcw_ref[...] = jnp.zeros_like(dcw_ref)
        dcw_ref[3:4, :] = jnp.sum(dc * u, 0, keepdims=True)
        for jj in range(CONV_W - 1):
            sh = CONV_W - 1 - jj
            du = du + cw[jj:jj + 1, :] * jnp.where(row < t - sh, pltpu.roll(dc, t - sh, axis=0), 0.0)
            dcw_ref[jj:jj + 1, :] = jnp.sum(dc * jnp.where(row >= sh, pltpu.roll(u, sh, axis=0), 0.0), 0, keepdims=True)
        du_ref[...] = du.astype(bf16)

    blk = pl.BlockSpec((t, LANES), lambda j: (0, j))
    return _hosted(
        body, comm, name="gdn_conv_bwd", grid=(nblk,),
        in_specs=[blk, pl.BlockSpec((CONV_W, LANES), lambda j: (0, j)), blk, blk],
        out_specs=(blk, pl.BlockSpec((8, LANES), lambda j: (0, j))),
        out_shape=(jax.ShapeDtypeStruct((t, GDN_QKV), bf16), jax.ShapeDtypeStruct((8, GDN_QKV), f32)),
        args=(proj, conv_w, c, dy))


def _chunk_masks():
    r = lax.broadcasted_iota(jnp.int32, (CHUNK, CHUNK), 0)
    c = lax.broadcasted_iota(jnp.int32, (CHUNK, CHUNK), 1)
    return r >= c, r > c, r == c


def _col_to_row(col, eye):
    return jnp.sum(jnp.where(eye, col, 0.0), axis=0, keepdims=True)


def _row_to_col(row, eye):
    return jnp.sum(jnp.where(eye, row, 0.0), axis=1, keepdims=True)


NN = (((1,), (0,)), ((), ()))
NT = (((1,), (1,)), ((), ()))
TN = (((0,), (0,)), ((), ()))
GDN_GROUP = 4


def _mx(a, b, dims=NN, passes=1):
    d = lambda p, q: lax.dot_general(p, q, dims, preferred_element_type=f32)
    ah, bh = a.astype(bf16), b.astype(bf16)
    if passes == 1:
        return d(ah, bh)
    al = (a - ah.astype(f32)).astype(bf16)
    bl = (b - bh.astype(f32)).astype(bf16)
    return d(ah, bh) + (d(ah, bl) + d(al, bh))


def _gdn_decay(gam, masks):
    causal, _, eye = masks
    return jnp.exp(jnp.where(causal, gam - _col_to_row(gam, eye), NEG))


def _gdn_local(y, gcum, comm=None):
    t = y.shape[0]
    nch = t // CHUNK
    rows_blk = GDN_GROUP * CHUNK

    def body(y_ref, g_ref, u_ref, w_ref, qk_ref, tinv_ref):
        masks = _chunk_masks()
        _, strict, eye = masks
        for j in range(GDN_GROUP):
            rs = slice(j * CHUNK, (j + 1) * CHUNK)
            for h in range(GDN_HEADS):
                qn = y_ref[rs, h * LANES:(h + 1) * LANES]
                kn = y_ref[rs, 512 + h * LANES:512 + (h + 1) * LANES]
                v = y_ref[rs, 1024 + h * LANES:1024 + (h + 1) * LANES]
                beta = g_ref[rs, h:h + 1]
                gam = g_ref[rs, 4 + h:5 + h]
                dec = _gdn_decay(gam, masks)
                x = -jnp.where(strict, _mx(kn, kn, NT) * dec * beta, 0.0)
                tinv = jnp.where(eye, 1.0, 0.0) + x
                for _ in range(5):
                    x = _mx(x, x, NN, 3)
                    tinv = tinv + _mx(tinv, x, NN, 3)
                e = jnp.exp(gam)
                u_ref[rs, h * LANES:(h + 1) * LANES] = _mx(tinv, beta * v)
                w_ref[rs, h * LANES:(h + 1) * LANES] = _mx(tinv, (beta * e) * kn)
                qk_ref[j, h] = _mx(qn, kn, NT) * dec
                tinv_ref[j, h] = tinv

    mat = pl.BlockSpec((GDN_GROUP, GDN_HEADS, CHUNK, CHUNK), lambda n: (n, 0, 0, 0))
    return _hosted(
        body, comm, name="gdn_local", grid=(nch // GDN_GROUP,),
        in_specs=[pl.BlockSpec((rows_blk, GDN_QKV), lambda n: (n, 0)), pl.BlockSpec((rows_blk, LANES), lambda n: (n, 0))],
        out_specs=(pl.BlockSpec((rows_blk, 512), lambda n: (n, 0)), pl.BlockSpec((rows_blk, 512), lambda n: (n, 0)), mat, mat),
        out_shape=(jax.ShapeDtypeStruct((t, 512), f32), jax.ShapeDtypeStruct((t, 512), f32),
                   jax.ShapeDtypeStruct((nch, GDN_HEADS, CHUNK, CHUNK), f32), jax.ShapeDtypeStruct((nch, GDN_HEADS, CHUNK, CHUNK), f32)),
        args=(y, gcum))


def _gdn_fwd(y, gcum, u, w, qk, comm=None):
    t = y.shape[0]
    nch = t // CHUNK

    def body(y_ref, g_ref, u_ref, w_ref, qk_ref, o_ref, sall_ref, s_ref):
        @pl.when(pl.program_id(0) == 0)
        def _():
            s_ref[...] = jnp.zeros_like(s_ref)

        for h in range(GDN_HEADS):
            sl = slice(h * LANES, (h + 1) * LANES)
            gam = g_ref[:, 4 + h:5 + h]
            gam_last = gam[CHUNK - 1:CHUNK, :]
            s = s_ref[h]
            sall_ref[0, h] = s
            vn = u_ref[:, sl] - _mx(w_ref[:, sl], s)
            o_ref[:, sl] = _mx(y_ref[:, sl] * jnp.exp(gam), s) + _mx(qk_ref[0, h], vn)
            kd = y_ref[:, 512 + h * LANES:512 + (h + 1) * LANES] * jnp.exp(gam_last - gam)
            s_ref[h] = jnp.exp(gam_last) * s + _mx(kd, vn, TN)

    row = lambda width: pl.BlockSpec((CHUNK, width), lambda n: (n, 0))
    return _hosted(
        body, comm, name="gdn_fwd", grid=(nch,),
        in_specs=[row(GDN_QKV), row(LANES), row(512), row(512), pl.BlockSpec((1, GDN_HEADS, CHUNK, CHUNK), lambda n: (n, 0, 0, 0))],
        out_specs=(row(512), pl.BlockSpec((1, GDN_HEADS, LANES, LANES), lambda n: (n, 0, 0, 0))),
        out_shape=(jax.ShapeDtypeStruct((t, 512), f32), jax.ShapeDtypeStruct((nch, GDN_HEADS, LANES, LANES), f32)),
        scratch_shapes=[pltpu.VMEM((GDN_HEADS, LANES, LANES), f32)],
        args=(y, gcum, u, w, qk))


def _gdn_bwd(y, gcum, u_all, w_all, qk_all, tinv_all, sall, do, comm=None):
    t = y.shape[0]
    nch = t // CHUNK

    def body(y_ref, g_ref, u_ref, w_ref, qk_ref, tinv_ref, sall_ref, do_ref, dy_ref, dg_ref, ds_ref):
        @pl.when(pl.program_id(0) == 0)
        def _():
            ds_ref[...] = jnp.zeros_like(ds_ref)

        masks = _chunk_masks()
        causal, strict, eye = masks
        lane = _lane((CHUNK, LANES))
        row = lax.broadcasted_iota(jnp.int32, (CHUNK, 1), 0)
        dgates = jnp.zeros((CHUNK, LANES), f32)
        for h in range(GDN_HEADS):
            sl = slice(h * LANES, (h + 1) * LANES)
            qn = y_ref[:, sl]
            kn = y_ref[:, 512 + h * LANES:512 + (h + 1) * LANES]
            v = y_ref[:, 1024 + h * LANES:1024 + (h + 1) * LANES]
            beta = g_ref[:, h:h + 1]
            gam = g_ref[:, 4 + h:5 + h]
            gam_last = gam[CHUNK - 1:CHUNK, :]
            dec = _gdn_decay(gam, masks)
            e = jnp.exp(gam)
            f = jnp.exp(gam_last - gam)
            gl = jnp.exp(gam_last)
            kkd = _mx(kn, kn, NT) * dec
            u, w, qk, tinv = u_ref[:, sl], w_ref[:, sl], qk_ref[0, h], tinv_ref[0, h]
            qd, kd = qn * e, kn * f
            s = sall_ref[0, h]
            dsn = ds_ref[h]
            d_o = do_ref[:, sl]
            vn = u - _mx(w, s)
            dvn = _mx(qk, d_o, TN) + _mx(kd, dsn)
            dqk = jnp.where(causal, _mx(d_o, vn, NT), 0.0)
            dqd = _mx(d_o, s, NT)
            dkd = _mx(vn, dsn, NT)
            dgl = jnp.sum(jnp.sum(dsn * s, axis=1, keepdims=True), axis=0, keepdims=True)
            dw = -_mx(dvn, s, NT)
            ds_ref[h] = _mx(qd, d_o, TN) - _mx(w, dvn, TN) + gl * dsn
            dru = _mx(tinv, dvn, TN)
            drw = _mx(tinv, dw, TN)
            dn = jnp.where(strict, -(_mx(dru, u, NT) + _mx(drw, w, NT)), 0.0)
            dv = beta * dru
            drw_k = jnp.sum(drw * kn, axis=1, keepdims=True)
            dbeta = jnp.sum(dru * v, axis=1, keepdims=True) + e * drw_k + jnp.sum(dn * kkd, axis=1, keepdims=True)
            de = beta * drw_k
            dk = (beta * e) * drw
            dkk = dn * beta * dec
            dk = dk + _mx(dkk, kn) + _mx(dkk, kn, TN)
            dqkr = dqk * dec
            dq = _mx(dqkr, kn) + dqd * e
            dk = dk + _mx(dqkr, qn, TN) + dkd * f
            m = dn * (kkd * beta) + dqk * qk
            dgam = jnp.sum(m, axis=1, keepdims=True) - _row_to_col(jnp.sum(m, axis=0, keepdims=True), eye)
            de = de + jnp.sum(dqd * qn, axis=1, keepdims=True)
            df = jnp.sum(dkd * kn, axis=1, keepdims=True)
            dgam = dgam + de * e - df * f
            dgam_last = jnp.sum(df * f, axis=0, keepdims=True) + dgl * gl
            dgam = dgam + jnp.where(row == CHUNK - 1, dgam_last, 0.0)
            dy_ref[:, sl] = dq
            dy_ref[:, 512 + h * LANES:512 + (h + 1) * LANES] = dk
            dy_ref[:, 1024 + h * LANES:1024 + (h + 1) * LANES] = dv
            dgates = dgates + jnp.where(lane == h, dbeta, 0.0) + jnp.where(lane == 4 + h, dgam, 0.0)
        dg_ref[...] = dgates

    rev = lambda width: pl.BlockSpec((CHUNK, width), lambda n: (nch - 1 - n, 0))
    mat = lambda d: pl.BlockSpec((1, GDN_HEADS, d, d), lambda n: (nch - 1 - n, 0, 0, 0))
    return _hosted(
        body, comm, name="gdn_bwd", grid=(nch,),
        in_specs=[rev(GDN_QKV), rev(LANES), rev(512), rev(512), mat(CHUNK), mat(CHUNK), mat(LANES), rev(512)],
        out_specs=(rev(GDN_QKV), rev(LANES)),
        out_shape=(jax.ShapeDtypeStruct((t, GDN_QKV), f32), jax.ShapeDtypeStruct((t, LANES), f32)),
        scratch_shapes=[pltpu.VMEM((GDN_HEADS, LANES, LANES), f32)],
        args=(y, gcum, u_all, w_all, qk_all, tinv_all, sall, do))


FOX_CLASSES = 4


def _fox_groups(t):
    nq = t // FOX_BQ
    ncls = min(FOX_CLASSES, nq)
    per = nq // ncls
    return [(g * per, per, (g + 1) * per * FOX_BQ) for g in range(ncls)]


def _fox_scores(q_ref, k_ref, gcum_ref, gcumt_ref, h, i, keys):
    pr = h // 2
    lo = (h % 2) * FOX_DH
    lane = _lane((FOX_BQ, LANES))
    mask = (lane >= lo) & (lane < lo + FOX_DH)
    qm = jnp.where(mask, q_ref[:, pr * LANES:(pr + 1) * LANES], 0.0).astype(bf16)
    kp = k_ref[:, pr * LANES:(pr + 1) * LANES].astype(bf16)
    s = _dot_nt(qm, kp, None) * (FOX_DH ** -0.5)
    s = s + gcum_ref[:, 8 + h:9 + h] - gcumt_ref[8 + h:9 + h, :]
    rows = i * FOX_BQ + lax.broadcasted_iota(jnp.int32, (FOX_BQ, keys), 0)
    cols = lax.broadcasted_iota(jnp.int32, (FOX_BQ, keys), 1)
    return jnp.where(cols <= rows, s, NEG), mask, qm, kp


def _fox_fwd(proj, gcum, gcumt):
    c0 = SEG_FOX // 512

    def group_call(q0, nq, keys):
        def body(q_ref, k_ref, v_ref, gcum_ref, gcumt_ref, o_ref, lse_ref):
            i = q0 + pl.program_id(0)
            lane = _lane((FOX_BQ, LANES))
            lse_all = jnp.zeros((FOX_BQ, LANES), f32)
            for pr in range(FOX_HEADS // 2):
                vp = v_ref[:, pr * LANES:(pr + 1) * LANES].astype(bf16)
                o_pair = jnp.zeros((FOX_BQ, LANES), f32)
                for h in (2 * pr, 2 * pr + 1):
                    s, mask, _, _ = _fox_scores(q_ref, k_ref, gcum_ref, gcumt_ref, h, i, keys)
                    m = jnp.max(s, axis=1, keepdims=True)
                    p = jnp.exp(s - m)
                    l = jnp.sum(p, axis=1, keepdims=True)
                    o_h = _dot((p * (1.0 / l)).astype(bf16), vp, None)
                    o_pair = jnp.where(mask, o_h, o_pair)
                    lse_all = jnp.where(lane == h, m + jnp.log(l), lse_all)
                o_ref[:, pr * LANES:(pr + 1) * LANES] = o_pair
            lse_ref[...] = lse_all

        seen = lambda col: pl.BlockSpec((keys, 512), lambda i: (0, col))
        return pl.pallas_call(
            body, name=f"fox_fwd_{keys}", grid=(nq,),
            in_specs=[pl.BlockSpec((FOX_BQ, 512), lambda i: (q0 + i, c0)), seen(c0 + 1), seen(c0 + 2),
                      pl.BlockSpec((FOX_BQ, LANES), lambda i: (q0 + i, 0)), pl.BlockSpec((LANES, keys), lambda i: (0, 0))],
            out_specs=(pl.BlockSpec((FOX_BQ, 512), lambda i: (i, 0)), pl.BlockSpec((FOX_BQ, LANES), lambda i: (i, 0))),
            out_shape=(jax.ShapeDtypeStruct((nq * FOX_BQ, 512), f32), jax.ShapeDtypeStruct((nq * FOX_BQ, LANES), f32)),
            compiler_params=_params(("parallel",)),
        )(proj, proj, proj, gcum, gcumt)

    parts = [group_call(*g) for g in _fox_groups(proj.shape[0])]
    return jnp.concatenate([o for o, _ in parts], axis=0), jnp.concatenate([l for _, l in parts], axis=0)


def _fox_bwd(proj, gcum, gcumt, o, lse, do, comm=None):
    t = proj.shape[0]
    c0 = SEG_FOX // 512

    def group_call(q0, nq, keys, acc):
        first = acc is None

        def body(q_ref, k_ref, v_ref, gcum_ref, gcumt_ref, o_ref, lse_ref, do_ref, *rest):
            dq_ref, dk_ref, dv_ref, dcc_ref, dct_ref = rest[-5:]
            j = pl.program_id(0)
            i = q0 + j

            @pl.when(j == 0)
            def _():
                if first:
                    dk_ref[...] = jnp.zeros_like(dk_ref)
                    dv_ref[...] = jnp.zeros_like(dv_ref)
                    dct_ref[...] = jnp.zeros_like(dct_ref)
                else:
                    dk_ref[...], dv_ref[...], dct_ref[...] = rest[0][...], rest[1][...], rest[2][...]

            lane = _lane((FOX_BQ, LANES))
            dcc = jnp.zeros((FOX_BQ, LANES), f32)
            scale = FOX_DH ** -0.5
            for pr in range(FOX_HEADS // 2):
                sl = slice(pr * LANES, (pr + 1) * LANES)
                vp = v_ref[:, sl].astype(bf16)
                dq_pair = jnp.zeros((FOX_BQ, LANES), f32)
                for h in (2 * pr, 2 * pr + 1):
                    s, mask, qm, kp = _fox_scores(q_ref, k_ref, gcum_ref, gcumt_ref, h, i, keys)
                    p = jnp.exp(s - lse_ref[:, h:h + 1])
                    dom = jnp.where(mask, do_ref[:, sl], 0.0)
                    delta = jnp.sum(dom * o_ref[:, sl], axis=1, keepdims=True)
                    domb = dom.astype(bf16)
                    ds = p * (_dot_nt(domb, vp, None) - delta)
                    dsb = ds.astype(bf16)
                    dv_ref[:, sl] += _dot_tn(p.astype(bf16), domb, None)
                    dk_ref[:, sl] += _dot_tn(dsb, qm, None) * scale
                    dq_pair = jnp.where(mask, _dot(dsb, kp, None) * scale, dq_pair)
                    dcc = jnp.where(lane == 8 + h, jnp.sum(ds, axis=1, keepdims=True), dcc)
                    dct_ref[8 + h:9 + h, :] += -jnp.sum(ds, axis=0, keepdims=True)
                dq_ref[:, sl] = dq_pair.astype(bf16)
            dcc_ref[...] = dcc

        qblk = lambda col: pl.BlockSpec((FOX_BQ, 512), lambda i: (q0 + i, col))
        oblk = pl.BlockSpec((FOX_BQ, 512), lambda i: (i, 0))
        seen = lambda col: pl.BlockSpec((keys, 512), lambda i: (0, col))
        rblk = pl.BlockSpec((FOX_BQ, LANES), lambda i: (q0 + i, 0))
        seen_t = pl.BlockSpec((LANES, keys), lambda i: (0, 0))
        in_specs = [qblk(c0), seen(c0 + 1), seen(c0 + 2), rblk, seen_t, qblk(0), rblk, qblk(0)]
        args = [proj, proj, proj, gcum, gcumt, o, lse, do]
        aliases = {}
        if not first:
            in_specs += [seen(0), seen(0), seen_t]
            args += list(acc)
            aliases = {8: 1, 9: 2, 10: 4}
        return _hosted(
            body, comm if first else None, name=f"fox_bwd_{keys}", grid=(nq,), in_specs=in_specs,
            out_specs=(oblk, seen(0), seen(0), pl.BlockSpec((FOX_BQ, LANES), lambda i: (i, 0)), seen_t),
            out_shape=(jax.ShapeDtypeStruct((nq * FOX_BQ, 512), bf16), jax.ShapeDtypeStruct((t, 512), f32), jax.ShapeDtypeStruct((t, 512), f32),
                       jax.ShapeDtypeStruct((nq * FOX_BQ, LANES), f32), jax.ShapeDtypeStruct((LANES, t), f32)),
            aliases=aliases, args=args)

    acc, dqs, dccs, comm_out = None, [], [], ()
    for g in reversed(_fox_groups(t)):
        (dq, dk, dv, dcc, dct), moved = group_call(*g, acc)
        comm_out = comm_out or moved
        acc = (dk, dv, dct)
        dqs.insert(0, dq)
        dccs.insert(0, dcc)
    return (jnp.concatenate(dqs, axis=0), acc[0], acc[1], jnp.concatenate(dccs, axis=0), acc[2]), comm_out


def _row(v, width=None):
    v = v.reshape(1, -1).astype(f32)
    if width is not None and v.shape[1] < width:
        v = jnp.pad(v, ((0, 0), (0, width - v.shape[1])))
    return v


LATE = ("w_out", "w_up", "w_ple_gate", "w_ple", "w_down")


def _device_grads(x, p, target, small, w_cat, conv_w, late, qc=None):
    z4 = jnp.zeros((4,), f32)
    bias_row = _row(jnp.concatenate([z4, small["dt_bias"].reshape(-1), small["b_f"].reshape(-1)]), LANES)
    alog_row = _row(jnp.concatenate([z4, small["a_log"].reshape(-1)]), LANES)
    g_gdn = _row(small["gdn_norm_g"])
    g_fox2 = _row(jnp.tile(small["fox_norm_g"].reshape(-1), 2))
    pb = p.astype(bf16)
    late = list(late)
    comm = qc is not None

    h0, h0b = _ln_in(x, _row(small["ln_in_g"]), _row(small["ln_in_b"]))
    proj = _mm(h0b, w_cat, "nn", 256, D_CAT, "mm_proj")
    gates, gcum, gcumt = _gates(proj, bias_row, alog_row)
    conv_c, qkv_n = _gdn_conv(proj, conv_w)
    (gu, gw, gqk, gtinv), landed = _gdn_local(qkv_n, gcum, _gather_chips(late) if comm else None)
    (o_gdn, sall), passed = _gdn_fwd(qkv_n, gcum, gu, gw, gqk, _gather_pass_on(landed) if comm else None)
    if comm:
        late = list(passed)
    w_out, w_up, w_gate, w_ple, w_down = late
    w_out, w_gate, w_down = w_out.reshape(D_MODEL, D_MODEL), w_gate.reshape(D_MODEL, D_MODEL), w_down.reshape(D_FF, D_MODEL)
    o_fox, lse = _fox_fwd(proj, gcum, gcumt)
    attn = _attn_post(o_gdn, proj, o_fox, g_gdn, g_fox2)
    mix = _mm(attn, w_out, "nn", 512, D_MODEL, "mm_mix")
    h1, h1b, xhat1, rstd1 = _ln1(h0, mix, _row(small["ln1_g"]), _row(small["ln1_b"]))
    up, act = _mm(h1b, w_up, "nn", 256, 1024, "mm_up", epi="relu2", shards=N_CHIPS)
    ff = _mm(act, w_down, "nn", 256, D_MODEL, "mm_down")
    gp = _mm(h1b, w_gate, "nn", 512, D_MODEL, "mm_gate")
    pe = _mm(pb, w_ple, "nn", 512, D_MODEL // N_CHIPS, "mm_ple", shards=N_CHIPS)
    dr2, dr2b, dpe, dgp, pg2 = _ln2_loss(h1, ff, pe, gp, _row(small["b_ple_gate"]), _row(small["ln2_g"]), _row(small["ln2_b"]), target)

    dup = _mm(dr2b, w_down, "nt", 256, 2048, "mm_dact", epi="relu2_bwd", extra=up)
    g_down = _mm(act, dr2b, "tn", 1024, D_MODEL, "mm_gdown")
    dh1_a = _mm(dup, w_up, "nt", 256, D_MODEL, "mm_dh1a", shards=N_CHIPS)
    g_up = _mm(h1b, dup, "tn", 1024, 1024, "mm_gup", shards=N_CHIPS)
    dh1_b = _mm(dgp, w_gate, "nt", 512, D_MODEL, "mm_dh1b")
    g_gate = _mm(h1b, dgp, "tn", 1024, D_MODEL, "mm_ggate")
    g_ple = _mm(pb, dpe, "tn", D_PLE, D_MODEL // N_CHIPS, "mm_gple", shards=N_CHIPS)
    dr1, dr1b, pg1 = _ln1_bwd(dr2, dh1_a, dh1_b, xhat1, rstd1, _row(small["ln1_g"]))
    dattn = _mm(dr1b, w_out, "nt", 512, D_MODEL, "mm_dattn")
    g_out = _mm(attn, dr1b, "tn", 1024, D_MODEL, "mm_gout")
    do_gdn, dz, do_fox, pga = _attn_post_bwd(dattn, o_gdn, proj, o_fox, g_gdn, g_fox2)
    g_late = [g.reshape((N_CHIPS, -1, g.shape[-1])) for g in (g_out, g_up, g_gate, g_ple, g_down)]
    (dfq, dfk, dfv, dccol, dct), from_sibling = _fox_bwd(proj, gcum, gcumt, o_fox, lse, do_fox, _exchange_pairs(g_late) if comm else None)
    if comm:
        sums = [_add_pair(g, b1, qc, "add_pair_" + n) for g, b1, n in zip(g_late, from_sibling, LATE)]
    (dqkv_n, dgates), from_chips = _gdn_bwd(qkv_n, gcum, gu, gw, gqk, gtinv, sall, do_gdn,
                                            _exchange_chips([ab for _, ab in sums]) if comm else None)
    dsmall, pgg = _gates_bwd(proj, bias_row, alog_row, gates, dgates, dccol, dct)
    if comm:
        halves = [_add_chips(a, b2, qc, "add_chips_" + n) for (a, _), b2, n in zip(sums, from_chips, LATE)]
    (du, g_conv8), reduced = _gdn_conv_bwd(proj, conv_w, conv_c, dqkv_n, _share_halves(halves) if comm else None)
    if comm:
        g_late = list(reduced)
    t = x.shape[0]
    dproj = jnp.concatenate([du, dz, dfq, dfk.astype(bf16), dfv.astype(bf16), dsmall, jnp.zeros((t, D_CAT - SEG_SMALL - LANES), bf16)], axis=1)
    dh0_mm = _mm(dproj, w_cat, "nt", 256, D_MODEL, "mm_dh0")
    g_cat = _mm(h0b, dproj, "tn", 1024, 1280, "mm_gcat")
    grad_x, pg0 = _ln_in_bwd(x, dr1, dh0_mm, _row(small["ln_in_g"]))

    g_fox = pga[1, :FOX_DH] + pga[1, FOX_DH:]
    small_grads = dict(
        ln_in_g=pg0[0], ln_in_b=pg0[1], ln1_g=pg1[0], ln1_b=pg1[1], b_ple_gate=pg2[2], ln2_g=pg2[0], ln2_b=pg2[1],
        gdn_norm_g=pga[0], fox_norm_g=g_fox, a_log=pgg[1, 4:8], dt_bias=pgg[0, 4:8], b_f=pgg[0, 8:16], loss=pg2[3, 0:1])
    return grad_x, g_cat, g_conv8[:CONV_W], dict(zip(LATE, g_late)), small_grads


ANY = pl.BlockSpec(memory_space=pl.ANY)
CONV_PKT_ROWS = 16


def _mesh_pos():
    return lax.axis_index("x"), lax.axis_index("y"), lax.axis_index("c")


def _other_chips(x, y):
    return [(1 - x, y), (x, 1 - y), (1 - x, 1 - y)]


def _rcopy(src, dst, send_sem, recv_sem, dev):
    return pltpu.make_async_remote_copy(src_ref=src, dst_ref=dst, send_sem=send_sem, recv_sem=recv_sem,
                                        device_id=dev, device_id_type=MESH)


class _Comm:
    def __init__(self, ins, outs, aliases, n_sems, start, finish):
        self.ins, self.outs, self.aliases, self.n_sems, self.start, self.finish = list(ins), list(outs), dict(aliases), n_sems, start, finish


def _hosted(body, comm, *, name, grid, in_specs, out_specs, out_shape, args, scratch_shapes=(), aliases=None):
    n_in, n_out, n_sc = len(in_specs), len(out_specs), len(scratch_shapes)
    k, ko = (len(comm.ins), len(comm.outs)) if comm else (0, 0)

    def kernel_body(*refs):
        o0 = n_in + k
        s0 = o0 + n_out + ko
        if comm:
            cins, couts, (ssem, rsem) = refs[n_in:o0], refs[o0 + n_out:s0], refs[s0 + n_sc:]

            @pl.when(pl.program_id(0) == 0)
            def _():
                comm.start(cins, couts, ssem, rsem)

        body(*refs[:n_in], *refs[o0:o0 + n_out], *refs[s0:s0 + n_sc])
        if comm:
            @pl.when(pl.program_id(0) == grid[0] - 1)
            def _():
                comm.finish(cins, couts, ssem, rsem)

    io_aliases = dict(aliases or {})
    scratch = list(scratch_shapes)
    if comm:
        io_aliases.update({n_in + i: n_out + j for i, j in comm.aliases.items()})
        scratch += [pltpu.SemaphoreType.DMA((comm.n_sems,)), pltpu.SemaphoreType.DMA((comm.n_sems,))]
    res = pl.pallas_call(
        kernel_body, name=name, grid=grid, in_specs=list(in_specs) + [ANY] * k, out_specs=tuple(out_specs) + (ANY,) * ko,
        out_shape=tuple(out_shape) + tuple(comm.outs if comm else ()), scratch_shapes=scratch, input_output_aliases=io_aliases,
        compiler_params=_params(("arbitrary",) * len(grid)),
    )(*args, *(comm.ins if comm else ()))
    return tuple(res[:n_out]), tuple(res[n_out:])


def _comm_only(phases, name):
    n_in = sum(len(p.ins) for p in phases)

    def body(*refs):
        n_out = sum(len(p.outs) for p in phases)
        sems = refs[n_in + n_out:]
        i0, o0 = 0, n_in
        for j, p in enumerate(phases):
            cins, couts = refs[i0:i0 + len(p.ins)], refs[o0:o0 + len(p.outs)]
            p.start(cins, couts, sems[2 * j], sems[2 * j + 1])
            p.finish(cins, couts, sems[2 * j], sems[2 * j + 1])
            i0 += len(p.ins)
            o0 += len(p.outs)

    aliases, i0, o0 = {}, 0, 0
    for p in phases:
        aliases.update({i0 + i: o0 + j for i, j in p.aliases.items()})
        i0 += len(p.ins)
        o0 += len(p.outs)
    outs = [o for p in phases for o in p.outs]
    res = pl.pallas_call(
        body, name=name, out_shape=tuple(outs), in_specs=[ANY] * n_in, out_specs=(ANY,) * len(outs), input_output_aliases=aliases,
        scratch_shapes=[pltpu.SemaphoreType.DMA((p.n_sems,)) for p in phases for _ in range(2)],
    )(*[a for p in phases for a in p.ins])
    split, o0 = [], 0
    for p in phases:
        split.append(tuple(res[o0:o0 + len(p.outs)]))
        o0 += len(p.outs)
    return split


def _like(arrays):
    return [jax.ShapeDtypeStruct(a.shape, a.dtype) for a in arrays]


def _half(ref, slot, hf):
    rows = ref.shape[1] // 2
    return ref.at[slot, pl.ds(hf * rows, rows)]


def _gather_chips(bufs, whole=False, base=0):
    nw = len(bufs)
    part = (lambda ref, slot, c: ref.at[slot]) if whole else _half

    def copies(couts):
        x, y, c = _mesh_pos()
        q = 2 * x + y
        for i in range(nw):
            for k, chip in enumerate(_other_chips(x, y)):
                mine, theirs = part(couts[i], q, c), part(couts[i], 2 * chip[0] + chip[1], c)
                yield base + i * 3 + k, mine, theirs, (*chip, c)

    def start(cins, couts, ssem, rsem):
        for s, mine, _, dev in copies(couts):
            _rcopy(mine, mine, ssem.at[s], rsem.at[s], dev).start()

    def finish(cins, couts, ssem, rsem):
        for s, _, theirs, dev in copies(couts):
            _rcopy(theirs, theirs, ssem.at[s], rsem.at[s], dev).wait_recv()
        for s, mine, _, dev in copies(couts):
            _rcopy(mine, mine, ssem.at[s], rsem.at[s], dev).wait_send()

    return _Comm(bufs, _like(bufs), {i: i for i in range(nw)}, 3 * nw, start, finish)


def _gather_pass_on(bufs, base=0):
    nw = len(bufs)

    def copies(couts):
        x, y, c = _mesh_pos()
        for i in range(nw):
            for k, chip in enumerate(_other_chips(x, y)):
                slot = 2 * chip[0] + chip[1]
                yield base + i * 3 + k, _half(couts[i], slot, c), _half(couts[i], slot, 1 - c), (x, y, 1 - c)

    def start(cins, couts, ssem, rsem):
        for s, landed, _, sib in copies(couts):
            _rcopy(landed, landed, ssem.at[s], rsem.at[s], sib).start()

    def finish(cins, couts, ssem, rsem):
        for s, _, passed, sib in copies(couts):
            _rcopy(passed, passed, ssem.at[s], rsem.at[s], sib).wait_recv()
        for s, landed, _, sib in copies(couts):
            _rcopy(landed, landed, ssem.at[s], rsem.at[s], sib).wait_send()

    return _Comm(bufs, _like(bufs), {i: i for i in range(nw)}, 3 * nw, start, finish)


def _gather_now(bufs, packets):
    nb = len(bufs)
    over, on, pk = _gather_chips(bufs), _gather_pass_on(bufs, base=3 * nb), _gather_chips(packets, whole=True, base=6 * nb)

    def start(cins, couts, ssem, rsem):
        over.start(cins[:nb], couts[:nb], ssem, rsem)
        pk.start(cins[nb:], couts[nb:], ssem, rsem)

    def finish(cins, couts, ssem, rsem):
        over.finish(cins[:nb], couts[:nb], ssem, rsem)
        on.start(cins[:nb], couts[:nb], ssem, rsem)
        on.finish(cins[:nb], couts[:nb], ssem, rsem)
        pk.finish(cins[nb:], couts[nb:], ssem, rsem)

    every = list(bufs) + list(packets)
    return _Comm(every, _like(every), {i: i for i in range(len(every))}, 6 * nb + 3 * len(packets), start, finish)


def _exchange_pairs(gs):
    nw = len(gs)

    def copies(cins, couts):
        x, y, c = _mesh_pos()
        for i in range(nw):
            for d in range(N_CHIPS):
                yield i * N_CHIPS + d, _half(cins[i], d, 1 - c), couts[i].at[d], (x, y, 1 - c)

    def start(cins, couts, ssem, rsem):
        for s, src, dst, sib in copies(cins, couts):
            _rcopy(src, dst, ssem.at[s], rsem.at[s], sib).start()

    def finish(cins, couts, ssem, rsem):
        for s, src, dst, sib in copies(cins, couts):
            _rcopy(src, dst, ssem.at[s], rsem.at[s], sib).wait_recv()
        for s, src, dst, sib in copies(cins, couts):
            _rcopy(src, dst, ssem.at[s], rsem.at[s], sib).wait_send()

    outs = [jax.ShapeDtypeStruct((N_CHIPS, g.shape[1] // 2, g.shape[2]), g.dtype) for g in gs]
    return _Comm(gs, outs, {}, N_CHIPS * nw, start, finish)


def _gather_packets(small):
    def peers():
        x, y, c = _mesh_pos()
        for r in range(1, 8):
            fx, fy, fc = (r >> 2) & 1, (r >> 1) & 1, r & 1
            yield r - 1, (1 - x if fx else x, 1 - y if fy else y, 1 - c if fc else c)

    def start(cins, couts, ssem, rsem):
        x, y, c = _mesh_pos()
        mine = couts[0].at[4 * x + 2 * y + c]
        for s, peer in peers():
            _rcopy(mine, mine, ssem.at[s], rsem.at[s], peer).start()

    def finish(cins, couts, ssem, rsem):
        x, y, c = _mesh_pos()
        mine = couts[0].at[4 * x + 2 * y + c]
        for s, peer in peers():
            theirs = couts[0].at[4 * peer[0] + 2 * peer[1] + peer[2]]
            _rcopy(theirs, theirs, ssem.at[s], rsem.at[s], peer).wait_recv()
        for s, peer in peers():
            _rcopy(mine, mine, ssem.at[s], rsem.at[s], peer).wait_send()

    return _Comm([small], _like([small]), {0: 0}, 7, start, finish)


def _exchange_chips(a4s):
    nw = len(a4s)

    def copies(cins, couts):
        x, y, c = _mesh_pos()
        for i in range(nw):
            for k, chip in enumerate(_other_chips(x, y)):
                yield i * 3 + k, cins[i].at[2 * chip[0] + chip[1]], couts[i].at[k], (*chip, c)

    def start(cins, couts, ssem, rsem):
        for s, src, dst, dev in copies(cins, couts):
            _rcopy(src, dst, ssem.at[s], rsem.at[s], dev).start()

    def finish(cins, couts, ssem, rsem):
        for s, src, dst, dev in copies(cins, couts):
            _rcopy(src, dst, ssem.at[s], rsem.at[s], dev).wait_recv()
        for s, src, dst, dev in copies(cins, couts):
            _rcopy(src, dst, ssem.at[s], rsem.at[s], dev).wait_send()

    return _Comm(a4s, [jax.ShapeDtypeStruct((3,) + a.shape[1:], a.dtype) for a in a4s], {}, 3 * nw, start, finish)


def _share_halves(rs):
    nw = len(rs)

    def halves(couts, i, hf):
        rows = rs[i].shape[0] // 2
        return couts[i].at[pl.ds(hf * rows, rows)]

    def start(cins, couts, ssem, rsem):
        x, y, c = _mesh_pos()
        for i in range(nw):
            _rcopy(halves(couts, i, c), halves(couts, i, c), ssem.at[i], rsem.at[i], (x, y, 1 - c)).start()

    def finish(cins, couts, ssem, rsem):
        x, y, c = _mesh_pos()
        for i in range(nw):
            _rcopy(halves(couts, i, 1 - c), halves(couts, i, 1 - c), ssem.at[i], rsem.at[i], (x, y, 1 - c)).wait_recv()
        for i in range(nw):
            _rcopy(halves(couts, i, c), halves(couts, i, c), ssem.at[i], rsem.at[i], (x, y, 1 - c)).wait_send()

    return _Comm(rs, _like(rs), {i: i for i in range(nw)}, nw, start, finish)


ADD_ROWS = 256


def _add_pair(g4, b1, qc_idx, name):
    _, half, cols = b1.shape
    rb = min(ADD_ROWS, half)
    nb = half // rb

    def body(qc_ref, g_ref, b_ref, o_ref, ob_ref):
        a = g_ref[...] + b_ref[...]
        o_ref[...] = a
        ob_ref[...] = a.astype(bf16)

    blk = (1, rb, cols)
    out = pl.BlockSpec(blk, lambda d, i, qc: (d, i, 0))
    return pl.pallas_call(
        body, name=name,
        grid_spec=pltpu.PrefetchScalarGridSpec(
            num_scalar_prefetch=1, grid=(N_CHIPS, nb),
            in_specs=[pl.BlockSpec(blk, lambda d, i, qc: (d, qc[1] * nb + i, 0)), out],
            out_specs=(out, out)),
        out_shape=(jax.ShapeDtypeStruct(b1.shape, f32), jax.ShapeDtypeStruct(b1.shape, bf16)),
        compiler_params=_params(("parallel", "parallel")),
    )(qc_idx, g4, b1)


def _add_chips(a4, b2, qc_idx, name):
    _, half, cols = a4.shape
    rb = min(ADD_ROWS, half)
    nb = half // rb

    def body(qc_ref, a_ref, b_ref, o_ref):
        o_ref[...] = ((a_ref[0] + b_ref[0].astype(f32)) + b_ref[1].astype(f32)) + b_ref[2].astype(f32)

    return pl.pallas_call(
        body, name=name,
        grid_spec=pltpu.PrefetchScalarGridSpec(
            num_scalar_prefetch=1, grid=(nb,),
            in_specs=[pl.BlockSpec((1, rb, cols), lambda i, qc: (qc[0], i, 0)), pl.BlockSpec((3, rb, cols), lambda i, qc: (0, i, 0))],
            out_specs=pl.BlockSpec((rb, cols), lambda i, qc: (qc[1] * nb + i, 0))),
        out_shape=jax.ShapeDtypeStruct((2 * half, cols), f32),
        compiler_params=_params(("parallel",)),
    )(qc_idx, a4, b2)


def _adamw_math(w, g, m, v):
    m = ADAM_B1 * m + (1.0 - ADAM_B1) * g
    v = ADAM_B2 * v + (1.0 - ADAM_B2) * (g * g)
    m_hat = m / (1.0 - ADAM_B1 ** ADAM_STEP)
    v_hat = v / (1.0 - ADAM_B2 ** ADAM_STEP)
    return -ADAM_LR * (m_hat / (jnp.sqrt(v_hat) + ADAM_EPS) + ADAM_WD * w), m, v


def _adamw(w, g, m, v, name):
    rows, cols = w.shape
    rb = ADD_ROWS if rows % ADD_ROWS == 0 else rows

    def body(w_ref, g_ref, m_ref, v_ref, go_ref, d_ref, mo_ref, vo_ref):
        g = g_ref[...]
        go_ref[...] = g
        d_ref[...], mo_ref[...], vo_ref[...] = _adamw_math(w_ref[...], g, m_ref[...], v_ref[...])

    blk = pl.BlockSpec((rb, cols), lambda i: (i, 0))
    return pl.pallas_call(
        body, name=name, grid=(rows // rb,), in_specs=[blk] * 4, out_specs=(blk,) * 4,
        out_shape=(jax.ShapeDtypeStruct(w.shape, f32),) * 4, compiler_params=_params(("parallel",)),
    )(w, g, m, v)


def _small_sum_adamw(all_pkts, w, m, v):
    def body(a_ref, w_ref, m_ref, v_ref, g_ref, d_ref, mo_ref, vo_ref):
        g = a_ref[0]
        for r in range(1, 8):
            g = g + a_ref[r]
        g_ref[...] = g
        d_ref[...], mo_ref[...], vo_ref[...] = _adamw_math(w_ref[...], g, m_ref[...], v_ref[...])

    return pl.pallas_call(body, name="small_sum_adamw", out_shape=(jax.ShapeDtypeStruct(w.shape, f32),) * 4)(all_pkts, w, m, v)


SHARDED = (("w_in", (D_MODEL, D_IN // N_CHIPS), 2), ("w_out", (D_MODEL // N_CHIPS, D_MODEL), 1), ("w_up", (D_MODEL, D_FF // N_CHIPS), 2),
           ("w_ple_gate", (D_MODEL // N_CHIPS, D_MODEL), 1), ("w_ple", (D_PLE, D_MODEL // N_CHIPS), 1),
           ("w_down", (D_FF // N_CHIPS, D_MODEL), 2))
SMALL_LAYOUT = (("ln_in_g", 0, 1024), ("ln_in_b", 8, 1024), ("ln1_g", 16, 1024), ("ln1_b", 24, 1024), ("b_ple_gate", 32, 1024),
                ("ln2_g", 40, 1024), ("ln2_b", 48, 1024), ("gdn_norm_g", 56, 128), ("fox_norm_g", 57, 64), ("a_log", 58, 4),
                ("dt_bias", 59, 4), ("b_f", 60, 8), ("loss", 61, 1))
SMALL_CONV_ROW = 64
SMALL_ROWS = 128


def _pack_small(vals, conv=None):
    rows = []
    nxt = 0
    for n, r0, size in SMALL_LAYOUT:
        assert r0 == nxt
        v = vals[n].reshape(-1).astype(f32) if n in vals else jnp.zeros((size,), f32)
        nrows = -(-size // LANES)
        rows.append(jnp.pad(v, (0, nrows * LANES - size)).reshape(nrows, LANES))
        nxt = r0 + nrows
    rows.append(jnp.zeros((SMALL_CONV_ROW - nxt, LANES), f32))
    conv_rows = CONV_W * GDN_QKV // LANES
    rows.append(jnp.zeros((conv_rows, LANES), f32) if conv is None else conv.reshape(conv_rows, LANES))
    rows.append(jnp.zeros((SMALL_ROWS - SMALL_CONV_ROW - conv_rows, LANES), f32))
    return jnp.concatenate(rows, axis=0)


def _unpack_small(pkt, shapes):
    out = {}
    for n, r0, size in SMALL_LAYOUT:
        if n in shapes:
            nrows = -(-size // LANES)
            out[n] = pkt[r0:r0 + nrows].reshape(-1)[:size].reshape(shapes[n])
    return out


WEIGHTS = ("ln_in_g", "ln_in_b", "w_in", "conv_w", "a_log", "dt_bias", "gdn_norm_g", "b_f", "fox_norm_g", "w_out", "ln1_g", "ln1_b",
           "w_up", "w_down", "w_ple", "w_ple_gate", "b_ple_gate", "ln2_g", "ln2_b")
SMALL_NAMES = tuple(n for n, _, _ in SMALL_LAYOUT if n != "loss")


def kernel(x, p, ln_in_g, ln_in_b, w_in, conv_w, a_log, dt_bias, gdn_norm_g, b_f, fox_norm_g, w_out, ln1_g, ln1_b, w_up, w_down, w_ple, w_ple_gate, b_ple_gate, ln2_g, ln2_b, loss_target, m_ln_in_g, m_ln_in_b, m_w_in, m_conv_w, m_a_log, m_dt_bias, m_gdn_norm_g, m_b_f, m_fox_norm_g, m_w_out, m_ln1_g, m_ln1_b, m_w_up, m_w_down, m_w_ple, m_w_ple_gate, m_b_ple_gate, m_ln2_g, m_ln2_b, v_ln_in_g, v_ln_in_b, v_w_in, v_conv_w, v_a_log, v_dt_bias, v_gdn_norm_g, v_b_f, v_fox_norm_g, v_w_out, v_ln1_g, v_ln1_b, v_w_up, v_w_down, v_w_ple, v_w_ple_gate, v_b_ple_gate, v_ln2_g, v_ln2_b):
    given = dict(locals())
    w = {n: given[n] for n in WEIGHTS}
    m = {n: given["m_" + n] for n in WEIGHTS}
    v = {n: given["v_" + n] for n in WEIGHTS}
    xi, yi, ci = _mesh_pos()
    q = 2 * xi + yi

    def slot_buffer(val, dtype, slots=N_CHIPS, slot=q):
        return lax.dynamic_update_slice(lax.empty((slots,) + val.shape, dtype), val.astype(dtype)[None], (slot, 0, 0))

    conv_rows = CONV_W * GDN_QKV // N_CHIPS // LANES
    conv_pkt = jnp.pad(w["conv_w"][0].reshape(-1, LANES), ((0, CONV_PKT_ROWS - conv_rows), (0, 0)))
    (w_in4, conv_all), = _comm_only([_gather_now([slot_buffer(w["w_in"][0], bf16)], [slot_buffer(conv_pkt, f32)])], "gather_w_in")
    conv_full = jnp.concatenate([conv_all[d, :conv_rows].reshape(CONV_W, GDN_QKV // N_CHIPS) for d in range(N_CHIPS)], axis=1)
    wi = jnp.concatenate([w_in4[d] for d in range(N_CHIPS)], axis=1)
    w_cat = jnp.concatenate([wi[:, :OFF_BETA], wi[:, OFF_FOX:OFF_F], wi[:, OFF_BETA:OFF_FOX], wi[:, OFF_F:],
                             jnp.zeros((D_MODEL, D_CAT - D_IN), bf16)], axis=1)

    small = {n: w[n] for n in SMALL_NAMES}
    qc = jnp.stack([q, ci]).astype(jnp.int32)
    grad_x, gc, g_conv, g_late, small_g = _device_grads(x[0], p[0, 0], loss_target[0], small, w_cat, conv_full,
                                                        [slot_buffer(w[n][0], bf16) for n in LATE], qc)

    g_in = jnp.concatenate([gc[:, :OFF_BETA], gc[:, SEG_SMALL:SEG_SMALL + 8], gc[:, SEG_FOX:SEG_SMALL],
                            gc[:, SEG_SMALL + 8:SEG_SMALL + 16]], axis=1)
    shard_cols = D_IN // N_CHIPS
    g_in4 = jnp.stack([g_in[:, d * shard_cols:(d + 1) * shard_cols] for d in range(N_CHIPS)])
    (b1,), (small_all,) = _comm_only(
        [_exchange_pairs([g_in4]), _gather_packets(slot_buffer(_pack_small(small_g, g_conv), f32, 8, 4 * xi + 2 * yi + ci))], "exchange_w_in")
    a4, a4b = _add_pair(g_in4, b1, qc, "add_pair_w_in")
    (b2,), = _comm_only([_exchange_chips([a4b])], "exchange_chips_w_in")
    (g_in_red,), = _comm_only([_share_halves([_add_chips(a4, b2, qc, "add_chips_w_in")])], "share_w_in")
    g_late["w_in"] = g_in_red

    grads, delta, new_m, new_v = {}, {}, {}, {}
    for n, g in g_late.items():
        outs = _adamw(w[n][0], g.reshape(w[n].shape[1:]), m[n][0], v[n][0], "adamw_" + n)
        grads[n], delta[n], new_m[n], new_v[n] = (a.reshape(w[n].shape) for a in outs)
    shapes = {n: w[n].shape for n in SMALL_NAMES}
    g_pkt, d_pkt, m_pkt, v_pkt = _small_sum_adamw(small_all, _pack_small(w), _pack_small(m), _pack_small(v))
    for dst, pkt in ((grads, g_pkt), (delta, d_pkt), (new_m, m_pkt), (new_v, v_pkt)):
        dst.update(_unpack_small(pkt, shapes))
    conv_rows_all = CONV_W * GDN_QKV // LANES
    conv_g_full = g_pkt[SMALL_CONV_ROW:SMALL_CONV_ROW + conv_rows_all].reshape(CONV_W, GDN_QKV)
    conv_g = lax.dynamic_slice_in_dim(conv_g_full, q * (GDN_QKV // N_CHIPS), GDN_QKV // N_CHIPS, axis=1)
    outs = _adamw(w["conv_w"][0], conv_g, m["conv_w"][0], v["conv_w"][0], "adamw_conv_w")
    grads["conv_w"], delta["conv_w"], new_m["conv_w"], new_v["conv_w"] = (a.reshape(w["conv_w"].shape) for a in outs)
    loss = g_pkt[61, 0]
    return (loss, grad_x[None], *[grads[n] for n in WEIGHTS], *[delta[n] for n in WEIGHTS],
            *[new_m[n] for n in WEIGHTS], *[new_v[n] for n in WEIGHTS])
```

```python
import functools

import jax
import jax.numpy as jnp
from jax import lax
from jax.experimental import pallas as pl
from jax.experimental.pallas import tpu as pltpu

f32 = jnp.float32
bf16 = jnp.bfloat16
HI = lax.Precision.HIGHEST
MESH = pl.DeviceIdType.MESH

D_MODEL = 1024
CHUNK = 64
GDN_HEADS = 4
GDN_DK = 128
FOX_HEADS = 8
FOX_DH = 64
CONV_W = 4
D_FF = 4096
D_PLE = 256
LN_EPS = 1e-5
NORM_EPS = 1e-6
ALPHA = 2.0 ** 0.25
GDN_QKV = 1536
OFF_Z = 1536
OFF_BETA = 2048
OFF_FOX = 2056
OFF_F = 3592
D_IN = 3600
ADAM_LR = 0.001
ADAM_B1 = 0.9
ADAM_B2 = 0.999
ADAM_EPS = 1e-08
ADAM_WD = 0.01
ADAM_STEP = 10

SEG_FOX = 2048
SEG_SMALL = 3584
D_CAT = 3840
LANES = 128
TOK_BLK = 256
FOX_BQ = 256
VMEM_LIMIT = 56 * 1024 * 1024
NEG = -1e30

N_CHIPS = 4


def _params(sem=None, **kw):
    return pltpu.CompilerParams(dimension_semantics=sem, vmem_limit_bytes=VMEM_LIMIT, **kw)


def _sigmoid(x):
    return 1.0 / (1.0 + jnp.exp(-x))


def _softplus(x):
    return jnp.maximum(x, 0.0) + jnp.log(1.0 + jnp.exp(-jnp.abs(x)))


def _ln_fwd(x, g, b):
    mu = jnp.mean(x, -1, keepdims=True)
    xc = x - mu
    var = jnp.mean(xc * xc, -1, keepdims=True)
    rstd = lax.rsqrt(var + LN_EPS)
    xhat = xc * rstd
    return xhat * g + b, xhat, rstd


def _ln_bwd(dy, xhat, rstd, g):
    dxh = dy * g
    m1 = jnp.mean(dxh, -1, keepdims=True)
    m2 = jnp.mean(dxh * xhat, -1, keepdims=True)
    return rstd * (dxh - m1 - xhat * m2)


def _dot(a, b, prec=HI):
    return jnp.dot(a, b, precision=prec, preferred_element_type=f32)


def _dot_nt(a, b, prec=HI):
    return lax.dot_general(a, b, (((1,), (1,)), ((), ())), precision=prec, preferred_element_type=f32)


def _dot_tn(a, b, prec=HI):
    return lax.dot_general(a, b, (((0,), (0,)), ((), ())), precision=prec, preferred_element_type=f32)


def _bdot(a, b):
    return _dot(a.astype(bf16), b.astype(bf16), None)


def _bdot_nt(a, b):
    return _dot_nt(a.astype(bf16), b.astype(bf16), None)


def _bdot_tn(a, b):
    return _dot_tn(a.astype(bf16), b.astype(bf16), None)


def _lane(shape):
    return lax.broadcasted_iota(jnp.int32, shape, len(shape) - 1)


def _mm(a, b, mode, tm, tn, name, out_dtype=f32, epi=None, extra=None, shards=1):
    if mode == "nn":
        (m, k), n = a.shape, b.shape[-1] * shards
    elif mode == "nt":
        (m, k), n = a.shape, b.shape[-2]
    else:
        (k, m), n = a.shape, b.shape[1]
    assert m % tm == 0 and n % tn == 0, (name, m, n, tm, tn)
    per = (n // shards) // tn
    assert mode == "nt" or per * tn * shards == n, (name, n, tn, shards)
    nc = 512 if tn % 512 == 0 else (256 if tn % 256 == 0 else 128)
    ks = k // shards

    def body(a_ref, b_ref, *rest):
        for n0 in range(0, tn, nc):
            if mode == "nn":
                acc = jnp.dot(a_ref[...], b_ref[:, n0:n0 + nc], preferred_element_type=f32)
            elif mode == "nt" and shards > 1:
                acc = jnp.zeros((tm, nc), f32)
                for d in range(shards):
                    acc = acc + lax.dot_general(a_ref[:, d * ks:(d + 1) * ks], b_ref[d, n0:n0 + nc, :], (((1,), (1,)), ((), ())),
                                                preferred_element_type=f32)
            elif mode == "nt":
                acc = lax.dot_general(a_ref[...], b_ref[n0:n0 + nc, :], (((1,), (1,)), ((), ())), preferred_element_type=f32)
            else:
                acc = lax.dot_general(a_ref[...], b_ref[:, n0:n0 + nc], (((0,), (0,)), ((), ())), preferred_element_type=f32)
            if epi == "relu2":
                up_ref, act_ref = rest
                up_ref[:, n0:n0 + nc] = acc
                r = jnp.maximum(acc, 0.0)
                act_ref[:, n0:n0 + nc] = (r * r).astype(bf16)
            elif epi == "relu2_bwd":
                up_ref, o_ref = rest
                o_ref[:, n0:n0 + nc] = (acc * (2.0 * jnp.maximum(up_ref[:, n0:n0 + nc], 0.0))).astype(bf16)
            else:
                (o_ref,) = rest
                o_ref[:, n0:n0 + nc] = acc.astype(out_dtype)

    if mode == "tn":
        a_spec = pl.BlockSpec((k, tm), lambda j, i: (0, i))
    else:
        a_spec = pl.BlockSpec((tm, k), lambda j, i: (i, 0))
    if mode == "nt" and shards > 1:
        b_spec = pl.BlockSpec((shards, tn, ks), lambda j, i: (0, j, 0))
    elif mode == "nt":
        b_spec = pl.BlockSpec((tn, k), lambda j, i: (j, 0))
    elif mode == "nn" and shards > 1:
        b_spec = pl.BlockSpec((None, k, tn), lambda j, i: (j // per, 0, j % per))
    else:
        b_spec = pl.BlockSpec((k, tn), lambda j, i: (0, j))
    o_spec = pl.BlockSpec((tm, tn), lambda j, i: (i, j))
    in_specs = [a_spec, b_spec]
    args = [a, b]
    if epi == "relu2":
        out_shape = (jax.ShapeDtypeStruct((m, n), f32), jax.ShapeDtypeStruct((m, n), bf16))
        out_specs = (o_spec, o_spec)
    elif epi == "relu2_bwd":
        in_specs.append(o_spec)
        args.append(extra)
        out_shape = jax.ShapeDtypeStruct((m, n), bf16)
        out_specs = o_spec
    elif mode == "tn" and shards > 1:
        out_shape = jax.ShapeDtypeStruct((shards, m, n // shards), out_dtype)
        out_specs = pl.BlockSpec((None, tm, tn), lambda j, i: (j // per, i, j % per))
    else:
        out_shape = jax.ShapeDtypeStruct((m, n), out_dtype)
        out_specs = o_spec
    return pl.pallas_call(
        body, name=name, grid=(n // tn, m // tm), in_specs=in_specs, out_specs=out_specs, out_shape=out_shape,
        compiler_params=_params(("parallel", "parallel")),
    )(*args)


def _row_spec(width, col=0):
    return pl.BlockSpec((TOK_BLK, width), lambda i: (i, col))


def _vec_spec(rows, width):
    return pl.BlockSpec((rows, width), lambda i: (0, 0))


def _ln_in(x, g, b):
    t, d = x.shape

    def body(x_ref, g_ref, b_ref, h_ref, hb_ref):
        h, _, _ = _ln_fwd(x_ref[...], g_ref[...], b_ref[...])
        h_ref[...] = h
        hb_ref[...] = h.astype(bf16)

    return pl.pallas_call(
        body, name="ln_in", grid=(t // TOK_BLK,),
        in_specs=[_row_spec(d), _vec_spec(1, d), _vec_spec(1, d)],
        out_specs=(_row_spec(d), _row_spec(d)),
        out_shape=(jax.ShapeDtypeStruct((t, d), f32), jax.ShapeDtypeStruct((t, d), bf16)),
        compiler_params=_params(("parallel",)),
    )(x, g, b)


def _attn_post(o_gdn, proj, o_fox, g_gdn, g_fox2):
    t = o_gdn.shape[0]

    def body(og_ref, z_ref, of_ref, gg_ref, gf_ref, out_ref):
        for h in range(GDN_HEADS):
            sl = slice(h * LANES, (h + 1) * LANES)
            og = og_ref[:, sl]
            z = z_ref[:, sl]
            r = lax.rsqrt(jnp.mean(og * og, -1, keepdims=True) + NORM_EPS)
            out_ref[:, sl] = (og * r * gg_ref[...] * (z * _sigmoid(z))).astype(bf16)
        lo = _lane((TOK_BLK, LANES)) < FOX_DH
        for pr in range(FOX_HEADS // 2):
            sl = slice(pr * LANES, (pr + 1) * LANES)
            of = of_ref[:, sl]
            sq = of * of
            s0 = jnp.sum(jnp.where(lo, sq, 0.0), -1, keepdims=True)
            s1 = jnp.sum(jnp.where(lo, 0.0, sq), -1, keepdims=True)
            r = lax.rsqrt(jnp.where(lo, s0, s1) * (1.0 / FOX_DH) + NORM_EPS)
            out_ref[:, 512 + pr * LANES:512 + (pr + 1) * LANES] = (of * r * gf_ref[...]).astype(bf16)

    return pl.pallas_call(
        body, name="attn_post", grid=(t // TOK_BLK,),
        in_specs=[_row_spec(512), _row_spec(512, OFF_Z // 512), _row_spec(512), _vec_spec(1, LANES), _vec_spec(1, LANES)],
        out_specs=_row_spec(D_MODEL),
        out_shape=jax.ShapeDtypeStruct((t, D_MODEL), bf16),
        compiler_params=_params(("parallel",)),
    )(o_gdn, proj, o_fox, g_gdn, g_fox2)


def _attn_post_bwd(dattn, o_gdn, proj, o_fox, g_gdn, g_fox2):
    t = o_gdn.shape[0]

    def body(da_ref, og_ref, z_ref, of_ref, gg_ref, gf_ref, dog_ref, dz_ref, dof_ref, pg_ref):
        i = pl.program_id(0)

        @pl.when(i == 0)
        def _():
            pg_ref[...] = jnp.zeros_like(pg_ref)

        dgg = jnp.zeros((1, LANES), f32)
        for h in range(GDN_HEADS):
            sl = slice(h * LANES, (h + 1) * LANES)
            og = og_ref[:, sl]
            z = z_ref[:, sl]
            dout = da_ref[:, sl]
            g = gg_ref[...]
            r = lax.rsqrt(jnp.mean(og * og, -1, keepdims=True) + NORM_EPS)
            sg = _sigmoid(z)
            silu = z * sg
            ng = og * r * g
            dng = dout * silu
            dz_ref[:, sl] = (dout * ng * (sg * (1.0 + z * (1.0 - sg)))).astype(bf16)
            dgg = dgg + jnp.sum(dng * og * r, 0, keepdims=True)
            gd = dng * g
            dog_ref[:, sl] = r * gd - og * (r * r * r) * jnp.mean(og * gd, -1, keepdims=True)
        pg_ref[0:1, :] += dgg
        lo = _lane((TOK_BLK, LANES)) < FOX_DH
        dgf = jnp.zeros((1, LANES), f32)
        for pr in range(FOX_HEADS // 2):
            sl = slice(pr * LANES, (pr + 1) * LANES)
            of = of_ref[:, sl]
            dout = da_ref[:, 512 + pr * LANES:512 + (pr + 1) * LANES]
            g = gf_ref[...]
            sq = of * of
            s0 = jnp.sum(jnp.where(lo, sq, 0.0), -1, keepdims=True)
            s1 = jnp.sum(jnp.where(lo, 0.0, sq), -1, keepdims=True)
            r = lax.rsqrt(jnp.where(lo, s0, s1) * (1.0 / FOX_DH) + NORM_EPS)
            dgf = dgf + jnp.sum(dout * of * r, 0, keepdims=True)
            gd = dout * g
            xg = of * gd
            m0 = jnp.sum(jnp.where(lo, xg, 0.0), -1, keepdims=True)
            m1 = jnp.sum(jnp.where(lo, 0.0, xg), -1, keepdims=True)
            dof_ref[:, sl] = r * gd - of * (r * r * r) * (jnp.where(lo, m0, m1) * (1.0 / FOX_DH))
        pg_ref[1:2, :] += dgf

    return pl.pallas_call(
        body, name="attn_post_bwd", grid=(t // TOK_BLK,),
        in_specs=[_row_spec(D_MODEL), _row_spec(512), _row_spec(512, OFF_Z // 512), _row_spec(512), _vec_spec(1, LANES), _vec_spec(1, LANES)],
        out_specs=(_row_spec(512), _row_spec(512), _row_spec(512), _vec_spec(8, LANES)),
        out_shape=(jax.ShapeDtypeStruct((t, 512), f32), jax.ShapeDtypeStruct((t, 512), bf16),
                   jax.ShapeDtypeStruct((t, 512), f32), jax.ShapeDtypeStruct((8, LANES), f32)),
        compiler_params=_params(("arbitrary",)),
    )(dattn, o_gdn, proj, o_fox, g_gdn, g_fox2)


def _ln1(h0, mix, g, b):
    t, d = h0.shape

    def body(h0_ref, mix_ref, g_ref, b_ref, h_ref, hb_ref, xh_ref, rs_ref):
        h, xhat, rstd = _ln_fwd(ALPHA * h0_ref[...] + mix_ref[...], g_ref[...], b_ref[...])
        h_ref[...] = h
        hb_ref[...] = h.astype(bf16)
        xh_ref[...] = xhat
        rs_ref[...] = jnp.broadcast_to(rstd, rs_ref.shape)

    return pl.pallas_call(
        body, name="ln1", grid=(t // TOK_BLK,),
        in_specs=[_row_spec(d), _row_spec(d), _vec_spec(1, d), _vec_spec(1, d)],
        out_specs=(_row_spec(d), _row_spec(d), _row_spec(d), _row_spec(LANES)),
        out_shape=(jax.ShapeDtypeStruct((t, d), f32), jax.ShapeDtypeStruct((t, d), bf16),
                   jax.ShapeDtypeStruct((t, d), f32), jax.ShapeDtypeStruct((t, LANES), f32)),
        compiler_params=_params(("parallel",)),
    )(h0, mix, g, b)


def _ln2_loss(h1, ff, pe, gp, b_gate, g, b, target):
    t, d = h1.shape

    def body(h1_ref, ff_ref, pe_ref, gp_ref, bg_ref, g_ref, b_ref, t_ref, dr_ref, drb_ref, dpe_ref, dgp_ref, pg_ref):
        i = pl.program_id(0)

        @pl.when(i == 0)
        def _():
            pg_ref[...] = jnp.zeros_like(pg_ref)

        sig = _sigmoid(gp_ref[...] + bg_ref[...])
        pe = pe_ref[...]
        r2 = ALPHA * h1_ref[...] + ff_ref[...] + pe * sig
        y, xhat, rstd = _ln_fwd(r2, g_ref[...], b_ref[...])
        err = y - t_ref[...]
        dy = err * (1.0 / d)
        dr = _ln_bwd(dy, xhat, rstd, g_ref[...])
        dr_ref[...] = dr
        drb_ref[...] = dr.astype(bf16)
        dpe_ref[...] = (dr * sig).astype(bf16)
        dgp = dr * pe * sig * (1.0 - sig)
        dgp_ref[...] = dgp.astype(bf16)
        pg_ref[0:1, :] += jnp.sum(dy * xhat, 0, keepdims=True)
        pg_ref[1:2, :] += jnp.sum(dy, 0, keepdims=True)
        pg_ref[2:3, :] += jnp.sum(dgp, 0, keepdims=True)
        pg_ref[3:4, :] += 0.5 * jnp.sum(jnp.mean(err * err, -1, keepdims=True), 0, keepdims=True)

    return pl.pallas_call(
        body, name="ln2_loss", grid=(t // TOK_BLK,),
        in_specs=[_row_spec(d)] * 4 + [_vec_spec(1, d)] * 3 + [_row_spec(d)],
        out_specs=(_row_spec(d), _row_spec(d), _row_spec(d), _row_spec(d), _vec_spec(8, d)),
        out_shape=(jax.ShapeDtypeStruct((t, d), f32), jax.ShapeDtypeStruct((t, d), bf16), jax.ShapeDtypeStruct((t, d), bf16),
                   jax.ShapeDtypeStruct((t, d), bf16), jax.ShapeDtypeStruct((8, d), f32)),
        compiler_params=_params(("arbitrary",)),
    )(h1, ff, pe, gp, b_gate, g, b, target)


def _ln1_bwd(dr2, da, db, xhat, rstd, g):
    t, d = dr2.shape

    def body(dr2_ref, da_ref, db_ref, xh_ref, rs_ref, g_ref, dr_ref, drb_ref, pg_ref):
        i = pl.program_id(0)

        @pl.when(i == 0)
        def _():
            pg_ref[...] = jnp.zeros_like(pg_ref)

        dh = ALPHA * dr2_ref[...] + da_ref[...] + db_ref[...]
        xhat = xh_ref[...]
        dr = _ln_bwd(dh, xhat, rs_ref[:, 0:1], g_ref[...])
        dr_ref[...] = dr
        drb_ref[...] = dr.astype(bf16)
        pg_ref[0:1, :] += jnp.sum(dh * xhat, 0, keepdims=True)
        pg_ref[1:2, :] += jnp.sum(dh, 0, keepdims=True)

    return pl.pallas_call(
        body, name="ln1_bwd", grid=(t // TOK_BLK,),
        in_specs=[_row_spec(d)] * 4 + [_row_spec(LANES), _vec_spec(1, d)],
        out_specs=(_row_spec(d), _row_spec(d), _vec_spec(8, d)),
        out_shape=(jax.ShapeDtypeStruct((t, d), f32), jax.ShapeDtypeStruct((t, d), bf16), jax.ShapeDtypeStruct((8, d), f32)),
        compiler_params=_params(("arbitrary",)),
    )(dr2, da, db, xhat, rstd, g)


def _ln_in_bwd(x, dr1, dmm, g):
    t, d = x.shape

    def body(x_ref, dr1_ref, dmm_ref, g_ref, dx_ref, pg_ref):
        i = pl.program_id(0)

        @pl.when(i == 0)
        def _():
            pg_ref[...] = jnp.zeros_like(pg_ref)

        dh = ALPHA * dr1_ref[...] + dmm_ref[...]
        _, xhat, rstd = _ln_fwd(x_ref[...], g_ref[...], 0.0)
        dx_ref[...] = _ln_bwd(dh, xhat, rstd, g_ref[...])
        pg_ref[0:1, :] += jnp.sum(dh * xhat, 0, keepdims=True)
        pg_ref[1:2, :] += jnp.sum(dh, 0, keepdims=True)

    return pl.pallas_call(
        body, name="ln_in_bwd", grid=(t // TOK_BLK,),
        in_specs=[_row_spec(d)] * 3 + [_vec_spec(1, d)],
        out_specs=(_row_spec(d), _vec_spec(8, d)),
        out_shape=(jax.ShapeDtypeStruct((t, d), f32), jax.ShapeDtypeStruct((8, d), f32)),
        compiler_params=_params(("arbitrary",)),
    )(x, dr1, dmm, g)


def _tri(n, upper=False, strict=False):
    r = lax.broadcasted_iota(jnp.int32, (n, n), 0)
    c = lax.broadcasted_iota(jnp.int32, (n, n), 1)
    if upper:
        m = (c > r) if strict else (c >= r)
    else:
        m = (c < r) if strict else (c <= r)
    return jnp.where(m, 1.0, 0.0).astype(f32)


def _gate_values(x, bias, alog, lane):
    z = x + bias
    return jnp.where(lane < 4, _sigmoid(z), jnp.where(lane < 8, -jnp.exp(alog) * _softplus(z), jnp.where(lane < 16, -_softplus(-z), 0.0)))


def _gates(proj, bias_row, alog_row):
    t = proj.shape[0]
    nch = t // CHUNK

    def body(x_ref, bias_ref, alog_ref, gates_ref, gcum_ref, gcumt_ref):
        lane = _lane((t, LANES))
        gates = _gate_values(x_ref[...], bias_ref[...], alog_ref[...], lane)
        gates_ref[...] = gates
        g3 = gates.reshape(nch, CHUNK, LANES)
        tri = jnp.broadcast_to(_tri(CHUNK)[None], (nch, CHUNK, CHUNK))
        loc = jnp.einsum("bij,bjk->bik", tri, g3, precision=HI, preferred_element_type=f32)
        tot = jnp.sum(g3, axis=1)
        offs = _dot(_tri(nch, strict=True), tot)
        glob = loc + offs[:, None, :]
        lane3 = _lane((nch, CHUNK, LANES))
        gcum = jnp.where(lane3 < 4, g3, jnp.where(lane3 < 8, loc, glob)).reshape(t, LANES)
        gcum_ref[...] = gcum
        gcumt_ref[...] = gcum.T

    return pl.pallas_call(
        body, name="gates", grid=(1,),
        in_specs=[pl.BlockSpec((t, LANES), lambda i: (0, SEG_SMALL // LANES)), _vec_spec(1, LANES), _vec_spec(1, LANES)],
        out_specs=(pl.BlockSpec((t, LANES), lambda i: (0, 0)), pl.BlockSpec((t, LANES), lambda i: (0, 0)),
                   pl.BlockSpec((LANES, t), lambda i: (0, 0))),
        out_shape=(jax.ShapeDtypeStruct((t, LANES), f32), jax.ShapeDtypeStruct((t, LANES), f32), jax.ShapeDtypeStruct((LANES, t), f32)),
        compiler_params=_params(("arbitrary",)),
    )(proj, bias_row, alog_row)


def _gates_bwd(proj, bias_row, alog_row, gates, dgates, dccol, dct):
    t = proj.shape[0]
    nch = t // CHUNK

    def body(x_ref, bias_ref, alog_ref, gates_ref, dg_ref, dcc_ref, dct_ref, dx_ref, pg_ref):
        lane = _lane((t, LANES))
        d = dg_ref[...] + dcc_ref[...] + dct_ref[...].T
        d3 = d.reshape(nch, CHUNK, LANES)
        tri = jnp.broadcast_to(_tri(CHUNK, upper=True)[None], (nch, CHUNK, CHUNK))
        loc = jnp.einsum("bij,bjk->bik", tri, d3, precision=HI, preferred_element_type=f32)
        tot = jnp.sum(d3, axis=1)
        offs = _dot(_tri(nch, upper=True, strict=True), tot)
        glob = loc + offs[:, None, :]
        lane3 = _lane((nch, CHUNK, LANES))
        dpre = jnp.where(lane3 < 4, d3, jnp.where(lane3 < 8, loc, glob)).reshape(t, LANES)
        z = x_ref[...] + bias_ref[...]
        sg = _sigmoid(z)
        dx = jnp.where(lane < 4, dpre * sg * (1.0 - sg),
                       jnp.where(lane < 8, dpre * (-jnp.exp(alog_ref[...])) * sg, jnp.where(lane < 16, dpre * (1.0 - sg), 0.0)))
        dx_ref[...] = dx.astype(bf16)
        pg_ref[...] = jnp.zeros_like(pg_ref)
        pg_ref[0:1, :] = jnp.sum(dx, 0, keepdims=True)
        pg_ref[1:2, :] = jnp.sum(jnp.where((lane >= 4) & (lane < 8), dpre * gates_ref[...], 0.0), 0, keepdims=True)

    full = pl.BlockSpec((t, LANES), lambda i: (0, 0))
    return pl.pallas_call(
        body, name="gates_bwd", grid=(1,),
        in_specs=[pl.BlockSpec((t, LANES), lambda i: (0, SEG_SMALL // LANES)), _vec_spec(1, LANES), _vec_spec(1, LANES),
                  full, full, full, pl.BlockSpec((LANES, t), lambda i: (0, 0))],
        out_specs=(full, _vec_spec(8, LANES)),
        out_shape=(jax.ShapeDtypeStruct((t, LANES), bf16), jax.ShapeDtypeStruct((8, LANES), f32)),
        compiler_params=_params(("arbitrary",)),
    )(proj, bias_row, alog_row, gates, dgates, dccol, dct)


def _conv_act(u, cw, row, t):
    c = cw[3:4, :] * u
    for jj in range(CONV_W - 1):
        sh = CONV_W - 1 - jj
        c = c + cw[jj:jj + 1, :] * jnp.where(row >= sh, pltpu.roll(u, sh, axis=0), 0.0)
    return c


def _gdn_conv(proj, conv_w):
    t = proj.shape[0]
    nblk = GDN_QKV // LANES

    def body(u_ref, cw_ref, c_ref, y_ref):
        j = pl.program_id(0)
        row = lax.broadcasted_iota(jnp.int32, (t, LANES), 0)
        c = _conv_act(u_ref[...], cw_ref[...], row, t)
        c_ref[...] = c
        s = c * _sigmoid(c)
        r = lax.rsqrt(jnp.sum(s * s, -1, keepdims=True) + NORM_EPS)
        scale = jnp.where(j < GDN_HEADS, GDN_DK ** -0.5, 1.0)
        y_ref[...] = jnp.where(j < 2 * GDN_HEADS, s * (r * scale), s)

    blk = pl.BlockSpec((t, LANES), lambda j: (0, j))
    return pl.pallas_call(
        body, name="gdn_conv", grid=(nblk,),
        in_specs=[blk, pl.BlockSpec((CONV_W, LANES), lambda j: (0, j))],
        out_specs=(blk, blk),
        out_shape=(jax.ShapeDtypeStruct((t, GDN_QKV), f32), jax.ShapeDtypeStruct((t, GDN_QKV), f32)),
        compiler_params=_params(("parallel",)),
    )(proj, conv_w)


def _gdn_conv_bwd(proj, conv_w, c, dy, comm=None):
    t = proj.shape[0]
    nblk = GDN_QKV // LANES

    def body(u_ref, cw_ref, c_ref, dy_ref, du_ref, dcw_ref):
        j = pl.program_id(0)
        row = lax.broadcasted_iota(jnp.int32, (t, LANES), 0)
        u = u_ref[...]
        cw = cw_ref[...]
        c = c_ref[...]
        dy = dy_ref[...]
        sg = _sigmoid(c)
        s = c * sg
        r = lax.rsqrt(jnp.sum(s * s, -1, keepdims=True) + NORM_EPS)
        n = s * r
        scale = jnp.where(j < GDN_HEADS, GDN_DK ** -0.5, 1.0)
        dn = dy * scale
        ds = jnp.where(j < 2 * GDN_HEADS, r * (dn - n * jnp.sum(dn * n, -1, keepdims=True)), dy)
        dc = ds * (sg * (1.0 + c * (1.0 - sg)))
        du = cw[3:4, :] * dc
        dcw_ref[...] = jnp.zeros_like(dcw_ref)
        dcw_ref[3:4, :] = jnp.sum(dc * u, 0, keepdims=True)
        for jj in range(CONV_W - 1):
            sh = CONV_W - 1 - jj
            du = du + cw[jj:jj + 1, :] * jnp.where(row < t - sh, pltpu.roll(dc, t - sh, axis=0), 0.0)
            dcw_ref[jj:jj + 1, :] = jnp.sum(dc * jnp.where(row >= sh, pltpu.roll(u, sh, axis=0), 0.0), 0, keepdims=True)
        du_ref[...] = du.astype(bf16)

    blk = pl.BlockSpec((t, LANES), lambda j: (0, j))
    return _hosted(
        body, comm, name="gdn_conv_bwd", grid=(nblk,),
        in_specs=[blk, pl.BlockSpec((CONV_W, LANES), lambda j: (0, j)), blk, blk],
        out_specs=(blk, pl.BlockSpec((8, LANES), lambda j: (0, j))),
        out_shape=(jax.ShapeDtypeStruct((t, GDN_QKV), bf16), jax.ShapeDtypeStruct((8, GDN_QKV), f32)),
        args=(proj, conv_w, c, dy))


def _chunk_masks():
    r = lax.broadcasted_iota(jnp.int32, (CHUNK, CHUNK), 0)
    c = lax.broadcasted_iota(jnp.int32, (CHUNK, CHUNK), 1)
    return r >= c, r > c, r == c


def _col_to_row(col, eye):
    return jnp.sum(jnp.where(eye, col, 0.0), axis=0, keepdims=True)


def _row_to_col(row, eye):
    return jnp.sum(jnp.where(eye, row, 0.0), axis=1, keepdims=True)


NN = (((1,), (0,)), ((), ()))
NT = (((1,), (1,)), ((), ()))
TN = (((0,), (0,)), ((), ()))
GDN_GROUP = 4


def _mx(a, b, dims=NN, passes=1):
    d = lambda p, q: lax.dot_general(p, q, dims, preferred_element_type=f32)
    ah, bh = a.astype(bf16), b.astype(bf16)
    if passes == 1:
        return d(ah, bh)
    al = (a - ah.astype(f32)).astype(bf16)
    bl = (b - bh.astype(f32)).astype(bf16)
    return d(ah, bh) + (d(ah, bl) + d(al, bh))


def _gdn_decay(gam, masks):
    causal, _, eye = masks
    return jnp.exp(jnp.where(causal, gam - _col_to_row(gam, eye), NEG))


def _gdn_local(y, gcum, comm=None):
    t = y.shape[0]
    nch = t // CHUNK
    rows_blk = GDN_GROUP * CHUNK

    def body(y_ref, g_ref, u_ref, w_ref, qk_ref, tinv_ref):
        masks = _chunk_masks()
        _, strict, eye = masks
        ids = [(j, h) for j in range(GDN_GROUP) for h in range(GDN_HEADS)]
        rs = lambda j: slice(j * CHUNK, (j + 1) * CHUNK)
        col = lambda base, h: slice(base + h * LANES, base + (h + 1) * LANES)
        kn = [y_ref[rs(j), col(512, h)] for j, h in ids]
        beta = [g_ref[rs(j), h:h + 1] for j, h in ids]
        gam = [g_ref[rs(j), 4 + h:5 + h] for j, h in ids]
        dec = [_gdn_decay(g, masks) for g in gam]
        x = [-jnp.where(strict, _mx(k, k, NT) * d * b, 0.0) for k, d, b in zip(kn, dec, beta)]
        tinv = [jnp.where(eye, 1.0, 0.0) + a for a in x]
        for _ in range(5):
            x = [_mx(a, a, NN, 3) for a in x]
            tinv = [t_ + _mx(t_, a, NN, 3) for t_, a in zip(tinv, x)]
        for (j, h), t_, k, d, b, g in zip(ids, tinv, kn, dec, beta, gam):
            u_ref[rs(j), col(0, h)] = _mx(t_, b * y_ref[rs(j), col(1024, h)])
            w_ref[rs(j), col(0, h)] = _mx(t_, (b * jnp.exp(g)) * k)
            qk_ref[j, h] = _mx(y_ref[rs(j), col(0, h)], k, NT) * d
            tinv_ref[j, h] = t_

    mat = pl.BlockSpec((GDN_GROUP, GDN_HEADS, CHUNK, CHUNK), lambda n: (n, 0, 0, 0))
    return _hosted(
        body, comm, name="gdn_local", grid=(nch // GDN_GROUP,),
        in_specs=[pl.BlockSpec((rows_blk, GDN_QKV), lambda n: (n, 0)), pl.BlockSpec((rows_blk, LANES), lambda n: (n, 0))],
        out_specs=(pl.BlockSpec((rows_blk, 512), lambda n: (n, 0)), pl.BlockSpec((rows_blk, 512), lambda n: (n, 0)), mat, mat),
        out_shape=(jax.ShapeDtypeStruct((t, 512), f32), jax.ShapeDtypeStruct((t, 512), f32),
                   jax.ShapeDtypeStruct((nch, GDN_HEADS, CHUNK, CHUNK), f32), jax.ShapeDtypeStruct((nch, GDN_HEADS, CHUNK, CHUNK), f32)),
        args=(y, gcum))


def _gdn_fwd(y, gcum, u, w, qk, comm=None):
    t = y.shape[0]
    nch = t // CHUNK

    def body(y_ref, g_ref, u_ref, w_ref, qk_ref, o_ref, sall_ref, s_ref):
        @pl.when(pl.program_id(0) == 0)
        def _():
            s_ref[...] = jnp.zeros_like(s_ref)

        heads = range(GDN_HEADS)
        sl = [slice(h * LANES, (h + 1) * LANES) for h in heads]
        gam = [g_ref[:, 4 + h:5 + h] for h in heads]
        gam_last = [g[CHUNK - 1:CHUNK, :] for g in gam]
        s = [s_ref[h] for h in heads]
        for h in heads:
            sall_ref[0, h] = s[h]
        ws = [_mx(w_ref[:, sl[h]], s[h]) for h in heads]
        qs = [_mx(y_ref[:, sl[h]] * jnp.exp(gam[h]), s[h]) for h in heads]
        vn = [u_ref[:, sl[h]] - ws[h] for h in heads]
        av = [_mx(qk_ref[0, h], vn[h]) for h in heads]
        kv = [_mx(y_ref[:, 512 + h * LANES:512 + (h + 1) * LANES] * jnp.exp(gam_last[h] - gam[h]), vn[h], TN) for h in heads]
        for h in heads:
            o_ref[:, sl[h]] = qs[h] + av[h]
            s_ref[h] = jnp.exp(gam_last[h]) * s[h] + kv[h]

    row = lambda width: pl.BlockSpec((CHUNK, width), lambda n: (n, 0))
    return _hosted(
        body, comm, name="gdn_fwd", grid=(nch,),
        in_specs=[row(GDN_QKV), row(LANES), row(512), row(512), pl.BlockSpec((1, GDN_HEADS, CHUNK, CHUNK), lambda n: (n, 0, 0, 0))],
        out_specs=(row(512), pl.BlockSpec((1, GDN_HEADS, LANES, LANES), lambda n: (n, 0, 0, 0))),
        out_shape=(jax.ShapeDtypeStruct((t, 512), f32), jax.ShapeDtypeStruct((nch, GDN_HEADS, LANES, LANES), f32)),
        scratch_shapes=[pltpu.VMEM((GDN_HEADS, LANES, LANES), f32)],
        args=(y, gcum, u, w, qk))


def _gdn_bwd(y, gcum, u_all, w_all, qk_all, tinv_all, sall, do, comm=None):
    t = y.shape[0]
    nch = t // CHUNK

    def body(y_ref, g_ref, u_ref, w_ref, qk_ref, tinv_ref, sall_ref, do_ref, dy_ref, dg_ref, ds_ref):
        @pl.when(pl.program_id(0) == 0)
        def _():
            ds_ref[...] = jnp.zeros_like(ds_ref)

        masks = _chunk_masks()
        causal, strict, eye = masks
        lane = _lane((CHUNK, LANES))
        row = lax.broadcasted_iota(jnp.int32, (CHUNK, 1), 0)
        heads = range(GDN_HEADS)
        each = lambda f, *ls: [f(*a) for a in zip(*ls)]
        rsum = lambda a: jnp.sum(a, axis=1, keepdims=True)
        sl = [slice(h * LANES, (h + 1) * LANES) for h in heads]
        qn = [y_ref[:, sl[h]] for h in heads]
        kn = [y_ref[:, 512 + h * LANES:512 + (h + 1) * LANES] for h in heads]
        v = [y_ref[:, 1024 + h * LANES:1024 + (h + 1) * LANES] for h in heads]
        beta = [g_ref[:, h:h + 1] for h in heads]
        gam = [g_ref[:, 4 + h:5 + h] for h in heads]
        gam_last = [g[CHUNK - 1:CHUNK, :] for g in gam]
        dec = [_gdn_decay(g, masks) for g in gam]
        e = [jnp.exp(g) for g in gam]
        f = each(lambda gl_, g: jnp.exp(gl_ - g), gam_last, gam)
        gl = [jnp.exp(g) for g in gam_last]
        u = [u_ref[:, sl[h]] for h in heads]
        w = [w_ref[:, sl[h]] for h in heads]
        qk = [qk_ref[0, h] for h in heads]
        tinv = [tinv_ref[0, h] for h in heads]
        s = [sall_ref[0, h] for h in heads]
        dsn = [ds_ref[h] for h in heads]
        d_o = [do_ref[:, sl[h]] for h in heads]
        qd = each(lambda a, b: a * b, qn, e)
        kd = each(lambda a, b: a * b, kn, f)
        ws = each(_mx, w, s)
        kds = each(_mx, kd, dsn)
        qkdo = each(lambda a, b: _mx(a, b, TN), qk, d_o)
        dqd = each(lambda a, b: _mx(a, b, NT), d_o, s)
        qddo = each(lambda a, b: _mx(a, b, TN), qd, d_o)
        kkd = each(lambda k, d: _mx(k, k, NT) * d, kn, dec)
        vn = each(lambda a, b: a - b, u, ws)
        dvn = each(lambda a, b: a + b, qkdo, kds)
        dqk = each(lambda a, b: jnp.where(causal, _mx(a, b, NT), 0.0), d_o, vn)
        dkd = each(lambda a, b: _mx(a, b, NT), vn, dsn)
        dw = each(lambda a, b: -_mx(a, b, NT), dvn, s)
        wdvn = each(lambda a, b: _mx(a, b, TN), w, dvn)
        dgl = each(lambda a, b: jnp.sum(rsum(a * b), axis=0, keepdims=True), dsn, s)
        for h in heads:
            ds_ref[h] = qddo[h] - wdvn[h] + gl[h] * dsn[h]
        dru = each(lambda a, b: _mx(a, b, TN), tinv, dvn)
        drw = each(lambda a, b: _mx(a, b, TN), tinv, dw)
        dqkr = each(lambda a, b: a * b, dqk, dec)
        dq1 = each(_mx, dqkr, kn)
        dk1 = each(lambda a, b: _mx(a, b, TN), dqkr, qn)
        dnu = each(lambda a, b: _mx(a, b, NT), dru, u)
        dnw = each(lambda a, b: _mx(a, b, NT), drw, w)
        dn = each(lambda a, b: jnp.where(strict, -(a + b), 0.0), dnu, dnw)
        dkk = each(lambda a, b, d: a * b * d, dn, beta, dec)
        dk2 = each(_mx, dkk, kn)
        dk3 = each(lambda a, b: _mx(a, b, TN), dkk, kn)
        dgates = jnp.zeros((CHUNK, LANES), f32)
        for h in heads:
            drw_k = rsum(drw[h] * kn[h])
            dbeta = rsum(dru[h] * v[h]) + e[h] * drw_k + rsum(dn[h] * kkd[h])
            m = dn[h] * (kkd[h] * beta[h]) + dqk[h] * qk[h]
            de = beta[h] * drw_k + rsum(dqd[h] * qn[h])
            df = rsum(dkd[h] * kn[h])
            dgam = rsum(m) - _row_to_col(jnp.sum(m, axis=0, keepdims=True), eye) + de * e[h] - df * f[h]
            dgam_last = jnp.sum(df * f[h], axis=0, keepdims=True) + dgl[h] * gl[h]
            dgam = dgam + jnp.where(row == CHUNK - 1, dgam_last, 0.0)
            dy_ref[:, sl[h]] = dq1[h] + dqd[h] * e[h]
            dy_ref[:, 512 + h * LANES:512 + (h + 1) * LANES] = (beta[h] * e[h]) * drw[h] + dk2[h] + dk3[h] + dk1[h] + dkd[h] * f[h]
            dy_ref[:, 1024 + h * LANES:1024 + (h + 1) * LANES] = beta[h] * dru[h]
            dgates = dgates + jnp.where(lane == h, dbeta, 0.0) + jnp.where(lane == 4 + h, dgam, 0.0)
        dg_ref[...] = dgates

    rev = lambda width: pl.BlockSpec((CHUNK, width), lambda n: (nch - 1 - n, 0))
    mat = lambda d: pl.BlockSpec((1, GDN_HEADS, d, d), lambda n: (nch - 1 - n, 0, 0, 0))
    return _hosted(
        body, comm, name="gdn_bwd", grid=(nch,),
        in_specs=[rev(GDN_QKV), rev(LANES), rev(512), rev(512), mat(CHUNK), mat(CHUNK), mat(LANES), rev(512)],
        out_specs=(rev(GDN_QKV), rev(LANES)),
        out_shape=(jax.ShapeDtypeStruct((t, GDN_QKV), f32), jax.ShapeDtypeStruct((t, LANES), f32)),
        scratch_shapes=[pltpu.VMEM((GDN_HEADS, LANES, LANES), f32)],
        args=(y, gcum, u_all, w_all, qk_all, tinv_all, sall, do))


FOX_CLASSES = 4


def _fox_groups(t):
    nq = t // FOX_BQ
    ncls = min(FOX_CLASSES, nq)
    per = nq // ncls
    return [(g * per, per, (g + 1) * per * FOX_BQ) for g in range(ncls)]


def _fox_scores(q_ref, k_ref, gcum_ref, gcumt_ref, h, i, keys):
    pr = h // 2
    lo = (h % 2) * FOX_DH
    lane = _lane((FOX_BQ, LANES))
    mask = (lane >= lo) & (lane < lo + FOX_DH)
    qm = jnp.where(mask, q_ref[:, pr * LANES:(pr + 1) * LANES], 0.0).astype(bf16)
    kp = k_ref[:, pr * LANES:(pr + 1) * LANES].astype(bf16)
    s = _dot_nt(qm, kp, None) * (FOX_DH ** -0.5)
    s = s + gcum_ref[:, 8 + h:9 + h] - gcumt_ref[8 + h:9 + h, :]
    rows = i * FOX_BQ + lax.broadcasted_iota(jnp.int32, (FOX_BQ, keys), 0)
    cols = lax.broadcasted_iota(jnp.int32, (FOX_BQ, keys), 1)
    return jnp.where(cols <= rows, s, NEG), mask, qm, kp


def _fox_fwd(proj, gcum, gcumt):
    c0 = SEG_FOX // 512

    def group_call(q0, nq, keys):
        def body(q_ref, k_ref, v_ref, gcum_ref, gcumt_ref, o_ref, lse_ref):
            i = q0 + pl.program_id(0)
            lane = _lane((FOX_BQ, LANES))
            lse_all = jnp.zeros((FOX_BQ, LANES), f32)
            for pr in range(FOX_HEADS // 2):
                vp = v_ref[:, pr * LANES:(pr + 1) * LANES].astype(bf16)
                o_pair = jnp.zeros((FOX_BQ, LANES), f32)
                for h in (2 * pr, 2 * pr + 1):
                    s, mask, _, _ = _fox_scores(q_ref, k_ref, gcum_ref, gcumt_ref, h, i, keys)
                    m = jnp.max(s, axis=1, keepdims=True)
                    p = jnp.exp(s - m)
                    l = jnp.sum(p, axis=1, keepdims=True)
                    o_h = _dot((p * (1.0 / l)).astype(bf16), vp, None)
                    o_pair = jnp.where(mask, o_h, o_pair)
                    lse_all = jnp.where(lane == h, m + jnp.log(l), lse_all)
                o_ref[:, pr * LANES:(pr + 1) * LANES] = o_pair
            lse_ref[...] = lse_all

        seen = lambda col: pl.BlockSpec((keys, 512), lambda i: (0, col))
        return pl.pallas_call(
            body, name=f"fox_fwd_{keys}", grid=(nq,),
            in_specs=[pl.BlockSpec((FOX_BQ, 512), lambda i: (q0 + i, c0)), seen(c0 + 1), seen(c0 + 2),
                      pl.BlockSpec((FOX_BQ, LANES), lambda i: (q0 + i, 0)), pl.BlockSpec((LANES, keys), lambda i: (0, 0))],
            out_specs=(pl.BlockSpec((FOX_BQ, 512), lambda i: (i, 0)), pl.BlockSpec((FOX_BQ, LANES), lambda i: (i, 0))),
            out_shape=(jax.ShapeDtypeStruct((nq * FOX_BQ, 512), f32), jax.ShapeDtypeStruct((nq * FOX_BQ, LANES), f32)),
            compiler_params=_params(("parallel",)),
        )(proj, proj, proj, gcum, gcumt)

    parts = [group_call(*g) for g in _fox_groups(proj.shape[0])]
    return jnp.concatenate([o for o, _ in parts], axis=0), jnp.concatenate([l for _, l in parts], axis=0)


def _fox_bwd(proj, gcum, gcumt, o, lse, do, comm=None):
    t = proj.shape[0]
    c0 = SEG_FOX // 512

    def group_call(q0, nq, keys, acc):
        first = acc is None

        def body(q_ref, k_ref, v_ref, gcum_ref, gcumt_ref, o_ref, lse_ref, do_ref, *rest):
            dq_ref, dk_ref, dv_ref, dcc_ref, dct_ref = rest[-5:]
            j = pl.program_id(0)
            i = q0 + j

            @pl.when(j == 0)
            def _():
                if first:
                    dk_ref[...] = jnp.zeros_like(dk_ref)
                    dv_ref[...] = jnp.zeros_like(dv_ref)
                    dct_ref[...] = jnp.zeros_like(dct_ref)
                else:
                    dk_ref[...], dv_ref[...], dct_ref[...] = rest[0][...], rest[1][...], rest[2][...]

            lane = _lane((FOX_BQ, LANES))
            dcc = jnp.zeros((FOX_BQ, LANES), f32)
            scale = FOX_DH ** -0.5
            for pr in range(FOX_HEADS // 2):
                sl = slice(pr * LANES, (pr + 1) * LANES)
                vp = v_ref[:, sl].astype(bf16)
                dq_pair = jnp.zeros((FOX_BQ, LANES), f32)
                for h in (2 * pr, 2 * pr + 1):
                    s, mask, qm, kp = _fox_scores(q_ref, k_ref, gcum_ref, gcumt_ref, h, i, keys)
                    p = jnp.exp(s - lse_ref[:, h:h + 1])
                    dom = jnp.where(mask, do_ref[:, sl], 0.0)
                    delta = jnp.sum(dom * o_ref[:, sl], axis=1, keepdims=True)
                    domb = dom.astype(bf16)
                    ds = p * (_dot_nt(domb, vp, None) - delta)
                    dsb = ds.astype(bf16)
                    dv_ref[:, sl] += _dot_tn(p.astype(bf16), domb, None)
                    dk_ref[:, sl] += _dot_tn(dsb, qm, None) * scale
                    dq_pair = jnp.where(mask, _dot(dsb, kp, None) * scale, dq_pair)
                    dcc = jnp.where(lane == 8 + h, jnp.sum(ds, axis=1, keepdims=True), dcc)
                    dct_ref[8 + h:9 + h, :] += -jnp.sum(ds, axis=0, keepdims=True)
                dq_ref[:, sl] = dq_pair.astype(bf16)
            dcc_ref[...] = dcc

        qblk = lambda col: pl.BlockSpec((FOX_BQ, 512), lambda i: (q0 + i, col))
        oblk = pl.BlockSpec((FOX_BQ, 512), lambda i: (i, 0))
        seen = lambda col: pl.BlockSpec((keys, 512), lambda i: (0, col))
        rblk = pl.BlockSpec((FOX_BQ, LANES), lambda i: (q0 + i, 0))
        seen_t = pl.BlockSpec((LANES, keys), lambda i: (0, 0))
        in_specs = [qblk(c0), seen(c0 + 1), seen(c0 + 2), rblk, seen_t, qblk(0), rblk, qblk(0)]
        args = [proj, proj, proj, gcum, gcumt, o, lse, do]
        aliases = {}
        if not first:
            in_specs += [seen(0), seen(0), seen_t]
            args += list(acc)
            aliases = {8: 1, 9: 2, 10: 4}
        return _hosted(
            body, comm if first else None, name=f"fox_bwd_{keys}", grid=(nq,), in_specs=in_specs,
            out_specs=(oblk, seen(0), seen(0), pl.BlockSpec((FOX_BQ, LANES), lambda i: (i, 0)), seen_t),
            out_shape=(jax.ShapeDtypeStruct((nq * FOX_BQ, 512), bf16), jax.ShapeDtypeStruct((t, 512), f32), jax.ShapeDtypeStruct((t, 512), f32),
                       jax.ShapeDtypeStruct((nq * FOX_BQ, LANES), f32), jax.ShapeDtypeStruct((LANES, t), f32)),
            aliases=aliases, args=args)

    acc, dqs, dccs, comm_out = None, [], [], ()
    for g in reversed(_fox_groups(t)):
        (dq, dk, dv, dcc, dct), moved = group_call(*g, acc)
        comm_out = comm_out or moved
        acc = (dk, dv, dct)
        dqs.insert(0, dq)
        dccs.insert(0, dcc)
    return (jnp.concatenate(dqs, axis=0), acc[0], acc[1], jnp.concatenate(dccs, axis=0), acc[2]), comm_out


def _row(v, width=None):
    v = v.reshape(1, -1).astype(f32)
    if width is not None and v.shape[1] < width:
        v = jnp.pad(v, ((0, 0), (0, width - v.shape[1])))
    return v


LATE = ("w_out", "w_up", "w_ple_gate", "w_ple", "w_down")


def _device_grads(x, p, target, small, w_cat, conv_w, late, qc=None):
    z4 = jnp.zeros((4,), f32)
    bias_row = _row(jnp.concatenate([z4, small["dt_bias"].reshape(-1), small["b_f"].reshape(-1)]), LANES)
    alog_row = _row(jnp.concatenate([z4, small["a_log"].reshape(-1)]), LANES)
    g_gdn = _row(small["gdn_norm_g"])
    g_fox2 = _row(jnp.tile(small["fox_norm_g"].reshape(-1), 2))
    pb = p.astype(bf16)
    late = list(late)
    comm = qc is not None

    h0, h0b = _ln_in(x, _row(small["ln_in_g"]), _row(small["ln_in_b"]))
    proj = _mm(h0b, w_cat, "nn", 256, D_CAT, "mm_proj")
    gates, gcum, gcumt = _gates(proj, bias_row, alog_row)
    conv_c, qkv_n = _gdn_conv(proj, conv_w)
    (gu, gw, gqk, gtinv), landed = _gdn_local(qkv_n, gcum, _gather_chips(late) if comm else None)
    (o_gdn, sall), passed = _gdn_fwd(qkv_n, gcum, gu, gw, gqk, _gather_pass_on(landed) if comm else None)
    if comm:
        late = list(passed)
    w_out, w_up, w_gate, w_ple, w_down = late
    w_out, w_gate, w_down = w_out.reshape(D_MODEL, D_MODEL), w_gate.reshape(D_MODEL, D_MODEL), w_down.reshape(D_FF, D_MODEL)
    o_fox, lse = _fox_fwd(proj, gcum, gcumt)
    attn = _attn_post(o_gdn, proj, o_fox, g_gdn, g_fox2)
    mix = _mm(attn, w_out, "nn", 512, D_MODEL, "mm_mix")
    h1, h1b, xhat1, rstd1 = _ln1(h0, mix, _row(small["ln1_g"]), _row(small["ln1_b"]))
    up, act = _mm(h1b, w_up, "nn", 256, 1024, "mm_up", epi="relu2", shards=N_CHIPS)
    ff = _mm(act, w_down, "nn", 256, D_MODEL, "mm_down")
    gp = _mm(h1b, w_gate, "nn", 512, D_MODEL, "mm_gate")
    pe = _mm(pb, w_ple, "nn", 512, D_MODEL // N_CHIPS, "mm_ple", shards=N_CHIPS)
    dr2, dr2b, dpe, dgp, pg2 = _ln2_loss(h1, ff, pe, gp, _row(small["b_ple_gate"]), _row(small["ln2_g"]), _row(small["ln2_b"]), target)

    dup = _mm(dr2b, w_down, "nt", 256, 2048, "mm_dact", epi="relu2_bwd", extra=up)
    g_down = _mm(act, dr2b, "tn", 1024, D_MODEL, "mm_gdown")
    dh1_a = _mm(dup, w_up, "nt", 256, D_MODEL, "mm_dh1a", shards=N_CHIPS)
    g_up = _mm(h1b, dup, "tn", 1024, 1024, "mm_gup", shards=N_CHIPS)
    dh1_b = _mm(dgp, w_gate, "nt", 512, D_MODEL, "mm_dh1b")
    g_gate = _mm(h1b, dgp, "tn", 1024, D_MODEL, "mm_ggate")
    g_ple = _mm(pb, dpe, "tn", D_PLE, D_MODEL // N_CHIPS, "mm_gple", shards=N_CHIPS)
    dr1, dr1b, pg1 = _ln1_bwd(dr2, dh1_a, dh1_b, xhat1, rstd1, _row(small["ln1_g"]))
    dattn = _mm(dr1b, w_out, "nt", 512, D_MODEL, "mm_dattn")
    g_out = _mm(attn, dr1b, "tn", 1024, D_MODEL, "mm_gout")
    do_gdn, dz, do_fox, pga = _attn_post_bwd(dattn, o_gdn, proj, o_fox, g_gdn, g_fox2)
    g_late = [g.reshape((N_CHIPS, -1, g.shape[-1])) for g in (g_out, g_up, g_gate, g_ple, g_down)]
    (dfq, dfk, dfv, dccol, dct), from_sibling = _fox_bwd(proj, gcum, gcumt, o_fox, lse, do_fox, _exchange_pairs(g_late) if comm else None)
    if comm:
        sums = [_add_pair(g, b1, qc, "add_pair_" + n) for g, b1, n in zip(g_late, from_sibling, LATE)]
    (dqkv_n, dgates), from_chips = _gdn_bwd(qkv_n, gcum, gu, gw, gqk, gtinv, sall, do_gdn,
                                            _exchange_chips([ab for _, ab in sums]) if comm else None)
    dsmall, pgg = _gates_bwd(proj, bias_row, alog_row, gates, dgates, dccol, dct)
    if comm:
        halves = [_add_chips(a, b2, qc, "add_chips_" + n) for (a, _), b2, n in zip(sums, from_chips, LATE)]
    (du, g_conv8), reduced = _gdn_conv_bwd(proj, conv_w, conv_c, dqkv_n, _share_halves(halves) if comm else None)
    if comm:
        g_late = list(reduced)
    t = x.shape[0]
    dproj = jnp.concatenate([du, dz, dfq, dfk.astype(bf16), dfv.astype(bf16), dsmall, jnp.zeros((t, D_CAT - SEG_SMALL - LANES), bf16)], axis=1)
    dh0_mm = _mm(dproj, w_cat, "nt", 256, D_MODEL, "mm_dh0")
    g_cat = _mm(h0b, dproj, "tn", 1024, 1280, "mm_gcat")
    grad_x, pg0 = _ln_in_bwd(x, dr1, dh0_mm, _row(small["ln_in_g"]))

    g_fox = pga[1, :FOX_DH] + pga[1, FOX_DH:]
    small_grads = dict(
        ln_in_g=pg0[0], ln_in_b=pg0[1], ln1_g=pg1[0], ln1_b=pg1[1], b_ple_gate=pg2[2], ln2_g=pg2[0], ln2_b=pg2[1],
        gdn_norm_g=pga[0], fox_norm_g=g_fox, a_log=pgg[1, 4:8], dt_bias=pgg[0, 4:8], b_f=pgg[0, 8:16], loss=pg2[3, 0:1])
    return grad_x, g_cat, g_conv8[:CONV_W], dict(zip(LATE, g_late)), small_grads


ANY = pl.BlockSpec(memory_space=pl.ANY)
CONV_PKT_ROWS = 16


def _mesh_pos():
    return lax.axis_index("x"), lax.axis_index("y"), lax.axis_index("c")


def _other_chips(x, y):
    return [(1 - x, y), (x, 1 - y), (1 - x, 1 - y)]


def _rcopy(src, dst, send_sem, recv_sem, dev):
    return pltpu.make_async_remote_copy(src_ref=src, dst_ref=dst, send_sem=send_sem, recv_sem=recv_sem,
                                        device_id=dev, device_id_type=MESH)


class _Comm:
    def __init__(self, ins, outs, aliases, n_sems, start, finish):
        self.ins, self.outs, self.aliases, self.n_sems, self.start, self.finish = list(ins), list(outs), dict(aliases), n_sems, start, finish


def _hosted(body, comm, *, name, grid, in_specs, out_specs, out_shape, args, scratch_shapes=(), aliases=None):
    n_in, n_out, n_sc = len(in_specs), len(out_specs), len(scratch_shapes)
    k, ko = (len(comm.ins), len(comm.outs)) if comm else (0, 0)

    def kernel_body(*refs):
        o0 = n_in + k
        s0 = o0 + n_out + ko
        if comm:
            cins, couts, (ssem, rsem) = refs[n_in:o0], refs[o0 + n_out:s0], refs[s0 + n_sc:]

            @pl.when(pl.program_id(0) == 0)
            def _():
                comm.start(cins, couts, ssem, rsem)

        body(*refs[:n_in], *refs[o0:o0 + n_out], *refs[s0:s0 + n_sc])
        if comm:
            @pl.when(pl.program_id(0) == grid[0] - 1)
            def _():
                comm.finish(cins, couts, ssem, rsem)

    io_aliases = dict(aliases or {})
    scratch = list(scratch_shapes)
    if comm:
        io_aliases.update({n_in + i: n_out + j for i, j in comm.aliases.items()})
        scratch += [pltpu.SemaphoreType.DMA((comm.n_sems,)), pltpu.SemaphoreType.DMA((comm.n_sems,))]
    res = pl.pallas_call(
        kernel_body, name=name, grid=grid, in_specs=list(in_specs) + [ANY] * k, out_specs=tuple(out_specs) + (ANY,) * ko,
        out_shape=tuple(out_shape) + tuple(comm.outs if comm else ()), scratch_shapes=scratch, input_output_aliases=io_aliases,
        compiler_params=_params(("arbitrary",) * len(grid)),
    )(*args, *(comm.ins if comm else ()))
    return tuple(res[:n_out]), tuple(res[n_out:])


def _comm_only(phases, name):
    n_in = sum(len(p.ins) for p in phases)

    def body(*refs):
        n_out = sum(len(p.outs) for p in phases)
        sems = refs[n_in + n_out:]
        i0, o0 = 0, n_in
        for j, p in enumerate(phases):
            cins, couts = refs[i0:i0 + len(p.ins)], refs[o0:o0 + len(p.outs)]
            p.start(cins, couts, sems[2 * j], sems[2 * j + 1])
            p.finish(cins, couts, sems[2 * j], sems[2 * j + 1])
            i0 += len(p.ins)
            o0 += len(p.outs)

    aliases, i0, o0 = {}, 0, 0
    for p in phases:
        aliases.update({i0 + i: o0 + j for i, j in p.aliases.items()})
        i0 += len(p.ins)
        o0 += len(p.outs)
    outs = [o for p in phases for o in p.outs]
    res = pl.pallas_call(
        body, name=name, out_shape=tuple(outs), in_specs=[ANY] * n_in, out_specs=(ANY,) * len(outs), input_output_aliases=aliases,
        scratch_shapes=[pltpu.SemaphoreType.DMA((p.n_sems,)) for p in phases for _ in range(2)],
    )(*[a for p in phases for a in p.ins])
    split, o0 = [], 0
    for p in phases:
        split.append(tuple(res[o0:o0 + len(p.outs)]))
        o0 += len(p.outs)
    return split


def _like(arrays):
    return [jax.ShapeDtypeStruct(a.shape, a.dtype) for a in arrays]


def _half(ref, slot, hf):
    rows = ref.shape[1] // 2
    return ref.at[slot, pl.ds(hf * rows, rows)]


def _gather_chips(bufs, whole=False, base=0):
    nw = len(bufs)
    part = (lambda ref, slot, c: ref.at[slot]) if whole else _half

    def copies(couts):
        x, y, c = _mesh_pos()
        q = 2 * x + y
        for i in range(nw):
            for k, chip in enumerate(_other_chips(x, y)):
                mine, theirs = part(couts[i], q, c), part(couts[i], 2 * chip[0] + chip[1], c)
                yield base + i * 3 + k, mine, theirs, (*chip, c)

    def start(cins, couts, ssem, rsem):
        for s, mine, _, dev in copies(couts):
            _rcopy(mine, mine, ssem.at[s], rsem.at[s], dev).start()

    def finish(cins, couts, ssem, rsem):
        for s, _, theirs, dev in copies(couts):
            _rcopy(theirs, theirs, ssem.at[s], rsem.at[s], dev).wait_recv()
        for s, mine, _, dev in copies(couts):
            _rcopy(mine, mine, ssem.at[s], rsem.at[s], dev).wait_send()

    return _Comm(bufs, _like(bufs), {i: i for i in range(nw)}, 3 * nw, start, finish)


def _gather_pass_on(bufs, base=0):
    nw = len(bufs)

    def copies(couts):
        x, y, c = _mesh_pos()
        for i in range(nw):
            for k, chip in enumerate(_other_chips(x, y)):
                slot = 2 * chip[0] + chip[1]
                yield base + i * 3 + k, _half(couts[i], slot, c), _half(couts[i], slot, 1 - c), (x, y, 1 - c)

    def start(cins, couts, ssem, rsem):
        for s, landed, _, sib in copies(couts):
            _rcopy(landed, landed, ssem.at[s], rsem.at[s], sib).start()

    def finish(cins, couts, ssem, rsem):
        for s, _, passed, sib in copies(couts):
            _rcopy(passed, passed, ssem.at[s], rsem.at[s], sib).wait_recv()
        for s, landed, _, sib in copies(couts):
            _rcopy(landed, landed, ssem.at[s], rsem.at[s], sib).wait_send()

    return _Comm(bufs, _like(bufs), {i: i for i in range(nw)}, 3 * nw, start, finish)


def _gather_now(bufs, packets):
    nb = len(bufs)
    over, on, pk = _gather_chips(bufs), _gather_pass_on(bufs, base=3 * nb), _gather_chips(packets, whole=True, base=6 * nb)

    def start(cins, couts, ssem, rsem):
        over.start(cins[:nb], couts[:nb], ssem, rsem)
        pk.start(cins[nb:], couts[nb:], ssem, rsem)

    def finish(cins, couts, ssem, rsem):
        over.finish(cins[:nb], couts[:nb], ssem, rsem)
        on.start(cins[:nb], couts[:nb], ssem, rsem)
        on.finish(cins[:nb], couts[:nb], ssem, rsem)
        pk.finish(cins[nb:], couts[nb:], ssem, rsem)

    every = list(bufs) + list(packets)
    return _Comm(every, _like(every), {i: i for i in range(len(every))}, 6 * nb + 3 * len(packets), start, finish)


def _exchange_pairs(gs):
    nw = len(gs)

    def copies(cins, couts):
        x, y, c = _mesh_pos()
        for i in range(nw):
            for d in range(N_CHIPS):
                yield i * N_CHIPS + d, _half(cins[i], d, 1 - c), couts[i].at[d], (x, y, 1 - c)

    def start(cins, couts, ssem, rsem):
        for s, src, dst, sib in copies(cins, couts):
            _rcopy(src, dst, ssem.at[s], rsem.at[s], sib).start()

    def finish(cins, couts, ssem, rsem):
        for s, src, dst, sib in copies(cins, couts):
            _rcopy(src, dst, ssem.at[s], rsem.at[s], sib).wait_recv()
        for s, src, dst, sib in copies(cins, couts):
            _rcopy(src, dst, ssem.at[s], rsem.at[s], sib).wait_send()

    outs = [jax.ShapeDtypeStruct((N_CHIPS, g.shape[1] // 2, g.shape[2]), g.dtype) for g in gs]
    return _Comm(gs, outs, {}, N_CHIPS * nw, start, finish)


def _gather_packets(small):
    def peers():
        x, y, c = _mesh_pos()
        for r in range(1, 8):
            fx, fy, fc = (r >> 2) & 1, (r >> 1) & 1, r & 1
            yield r - 1, (1 - x if fx else x, 1 - y if fy else y, 1 - c if fc else c)

    def start(cins, couts, ssem, rsem):
        x, y, c = _mesh_pos()
        mine = couts[0].at[4 * x + 2 * y + c]
        for s, peer in peers():
            _rcopy(mine, mine, ssem.at[s], rsem.at[s], peer).start()

    def finish(cins, couts, ssem, rsem):
        x, y, c = _mesh_pos()
        mine = couts[0].at[4 * x + 2 * y + c]
        for s, peer in peers():
            theirs = couts[0].at[4 * peer[0] + 2 * peer[1] + peer[2]]
            _rcopy(theirs, theirs, ssem.at[s], rsem.at[s], peer).wait_recv()
        for s, peer in peers():
            _rcopy(mine, mine, ssem.at[s], rsem.at[s], peer).wait_send()

    return _Comm([small], _like([small]), {0: 0}, 7, start, finish)


def _exchange_chips(a4s):
    nw = len(a4s)

    def copies(cins, couts):
        x, y, c = _mesh_pos()
        for i in range(nw):
            for k, chip in enumerate(_other_chips(x, y)):
                yield i * 3 + k, cins[i].at[2 * chip[0] + chip[1]], couts[i].at[k], (*chip, c)

    def start(cins, couts, ssem, rsem):
        for s, src, dst, dev in copies(cins, couts):
            _rcopy(src, dst, ssem.at[s], rsem.at[s], dev).start()

    def finish(cins, couts, ssem, rsem):
        for s, src, dst, dev in copies(cins, couts):
            _rcopy(src, dst, ssem.at[s], rsem.at[s], dev).wait_recv()
        for s, src, dst, dev in copies(cins, couts):
            _rcopy(src, dst, ssem.at[s], rsem.at[s], dev).wait_send()

    return _Comm(a4s, [jax.ShapeDtypeStruct((3,) + a.shape[1:], a.dtype) for a in a4s], {}, 3 * nw, start, finish)


def _share_halves(rs):
    nw = len(rs)

    def halves(couts, i, hf):
        rows = rs[i].shape[0] // 2
        return couts[i].at[pl.ds(hf * rows, rows)]

    def start(cins, couts, ssem, rsem):
        x, y, c = _mesh_pos()
        for i in range(nw):
            _rcopy(halves(couts, i, c), halves(couts, i, c), ssem.at[i], rsem.at[i], (x, y, 1 - c)).start()

    def finish(cins, couts, ssem, rsem):
        x, y, c = _mesh_pos()
        for i in range(nw):
            _rcopy(halves(couts, i, 1 - c), halves(couts, i, 1 - c), ssem.at[i], rsem.at[i], (x, y, 1 - c)).wait_recv()
        for i in range(nw):
            _rcopy(halves(couts, i, c), halves(couts, i, c), ssem.at[i], rsem.at[i], (x, y, 1 - c)).wait_send()

    return _Comm(rs, _like(rs), {i: i for i in range(nw)}, nw, start, finish)


ADD_ROWS = 256


def _add_pair(g4, b1, qc_idx, name):
    _, half, cols = b1.shape
    rb = min(ADD_ROWS, half)
    nb = half // rb

    def body(qc_ref, g_ref, b_ref, o_ref, ob_ref):
        a = g_ref[...] + b_ref[...]
        o_ref[...] = a
        ob_ref[...] = a.astype(bf16)

    blk = (1, rb, cols)
    out = pl.BlockSpec(blk, lambda d, i, qc: (d, i, 0))
    return pl.pallas_call(
        body, name=name,
        grid_spec=pltpu.PrefetchScalarGridSpec(
            num_scalar_prefetch=1, grid=(N_CHIPS, nb),
            in_specs=[pl.BlockSpec(blk, lambda d, i, qc: (d, qc[1] * nb + i, 0)), out],
            out_specs=(out, out)),
        out_shape=(jax.ShapeDtypeStruct(b1.shape, f32), jax.ShapeDtypeStruct(b1.shape, bf16)),
        compiler_params=_params(("parallel", "parallel")),
    )(qc_idx, g4, b1)


def _add_chips(a4, b2, qc_idx, name):
    _, half, cols = a4.shape
    rb = min(ADD_ROWS, half)
    nb = half // rb

    def body(qc_ref, a_ref, b_ref, o_ref):
        o_ref[...] = ((a_ref[0] + b_ref[0].astype(f32)) + b_ref[1].astype(f32)) + b_ref[2].astype(f32)

    return pl.pallas_call(
        body, name=name,
        grid_spec=pltpu.PrefetchScalarGridSpec(
            num_scalar_prefetch=1, grid=(nb,),
            in_specs=[pl.BlockSpec((1, rb, cols), lambda i, qc: (qc[0], i, 0)), pl.BlockSpec((3, rb, cols), lambda i, qc: (0, i, 0))],
            out_specs=pl.BlockSpec((rb, cols), lambda i, qc: (qc[1] * nb + i, 0))),
        out_shape=jax.ShapeDtypeStruct((2 * half, cols), f32),
        compiler_params=_params(("parallel",)),
    )(qc_idx, a4, b2)


def _adamw_math(w, g, m, v):
    m = ADAM_B1 * m + (1.0 - ADAM_B1) * g
    v = ADAM_B2 * v + (1.0 - ADAM_B2) * (g * g)
    m_hat = m / (1.0 - ADAM_B1 ** ADAM_STEP)
    v_hat = v / (1.0 - ADAM_B2 ** ADAM_STEP)
    return -ADAM_LR * (m_hat / (jnp.sqrt(v_hat) + ADAM_EPS) + ADAM_WD * w), m, v


def _adamw(w, g, m, v, name):
    rows, cols = w.shape
    rb = ADD_ROWS if rows % ADD_ROWS == 0 else rows

    def body(w_ref, g_ref, m_ref, v_ref, go_ref, d_ref, mo_ref, vo_ref):
        g = g_ref[...]
        go_ref[...] = g
        d_ref[...], mo_ref[...], vo_ref[...] = _adamw_math(w_ref[...], g, m_ref[...], v_ref[...])

    blk = pl.BlockSpec((rb, cols), lambda i: (i, 0))
    return pl.pallas_call(
        body, name=name, grid=(rows // rb,), in_specs=[blk] * 4, out_specs=(blk,) * 4,
        out_shape=(jax.ShapeDtypeStruct(w.shape, f32),) * 4, compiler_params=_params(("parallel",)),
    )(w, g, m, v)


def _small_sum_adamw(all_pkts, w, m, v):
    def body(a_ref, w_ref, m_ref, v_ref, g_ref, d_ref, mo_ref, vo_ref):
        g = a_ref[0]
        for r in range(1, 8):
            g = g + a_ref[r]
        g_ref[...] = g
        d_ref[...], mo_ref[...], vo_ref[...] = _adamw_math(w_ref[...], g, m_ref[...], v_ref[...])

    return pl.pallas_call(body, name="small_sum_adamw", out_shape=(jax.ShapeDtypeStruct(w.shape, f32),) * 4)(all_pkts, w, m, v)


SHARDED = (("w_in", (D_MODEL, D_IN // N_CHIPS), 2), ("w_out", (D_MODEL // N_CHIPS, D_MODEL), 1), ("w_up", (D_MODEL, D_FF // N_CHIPS), 2),
           ("w_ple_gate", (D_MODEL // N_CHIPS, D_MODEL), 1), ("w_ple", (D_PLE, D_MODEL // N_CHIPS), 1),
           ("w_down", (D_FF // N_CHIPS, D_MODEL), 2))
SMALL_LAYOUT = (("ln_in_g", 0, 1024), ("ln_in_b", 8, 1024), ("ln1_g", 16, 1024), ("ln1_b", 24, 1024), ("b_ple_gate", 32, 1024),
                ("ln2_g", 40, 1024), ("ln2_b", 48, 1024), ("gdn_norm_g", 56, 128), ("fox_norm_g", 57, 64), ("a_log", 58, 4),
                ("dt_bias", 59, 4), ("b_f", 60, 8), ("loss", 61, 1))
SMALL_CONV_ROW = 64
SMALL_ROWS = 128


def _pack_small(vals, conv=None):
    rows = []
    nxt = 0
    for n, r0, size in SMALL_LAYOUT:
        assert r0 == nxt
        v = vals[n].reshape(-1).astype(f32) if n in vals else jnp.zeros((size,), f32)
        nrows = -(-size // LANES)
        rows.append(jnp.pad(v, (0, nrows * LANES - size)).reshape(nrows, LANES))
        nxt = r0 + nrows
    rows.append(jnp.zeros((SMALL_CONV_ROW - nxt, LANES), f32))
    conv_rows = CONV_W * GDN_QKV // LANES
    rows.append(jnp.zeros((conv_rows, LANES), f32) if conv is None else conv.reshape(conv_rows, LANES))
    rows.append(jnp.zeros((SMALL_ROWS - SMALL_CONV_ROW - conv_rows, LANES), f32))
    return jnp.concatenate(rows, axis=0)


def _unpack_small(pkt, shapes):
    out = {}
    for n, r0, size in SMALL_LAYOUT:
        if n in shapes:
            nrows = -(-size // LANES)
            out[n] = pkt[r0:r0 + nrows].reshape(-1)[:size].reshape(shapes[n])
    return out


WEIGHTS = ("ln_in_g", "ln_in_b", "w_in", "conv_w", "a_log", "dt_bias", "gdn_norm_g", "b_f", "fox_norm_g", "w_out", "ln1_g", "ln1_b",
           "w_up", "w_down", "w_ple", "w_ple_gate", "b_ple_gate", "ln2_g", "ln2_b")
SMALL_NAMES = tuple(n for n, _, _ in SMALL_LAYOUT if n != "loss")


def kernel(x, p, ln_in_g, ln_in_b, w_in, conv_w, a_log, dt_bias, gdn_norm_g, b_f, fox_norm_g, w_out, ln1_g, ln1_b, w_up, w_down, w_ple, w_ple_gate, b_ple_gate, ln2_g, ln2_b, loss_target, m_ln_in_g, m_ln_in_b, m_w_in, m_conv_w, m_a_log, m_dt_bias, m_gdn_norm_g, m_b_f, m_fox_norm_g, m_w_out, m_ln1_g, m_ln1_b, m_w_up, m_w_down, m_w_ple, m_w_ple_gate, m_b_ple_gate, m_ln2_g, m_ln2_b, v_ln_in_g, v_ln_in_b, v_w_in, v_conv_w, v_a_log, v_dt_bias, v_gdn_norm_g, v_b_f, v_fox_norm_g, v_w_out, v_ln1_g, v_ln1_b, v_w_up, v_w_down, v_w_ple, v_w_ple_gate, v_b_ple_gate, v_ln2_g, v_ln2_b):
    given = dict(locals())
    w = {n: given[n] for n in WEIGHTS}
    m = {n: given["m_" + n] for n in WEIGHTS}
    v = {n: given["v_" + n] for n in WEIGHTS}
    xi, yi, ci = _mesh_pos()
    q = 2 * xi + yi

    def slot_buffer(val, dtype, slots=N_CHIPS, slot=q):
        return lax.dynamic_update_slice(lax.empty((slots,) + val.shape, dtype), val.astype(dtype)[None], (slot, 0, 0))

    conv_rows = CONV_W * GDN_QKV // N_CHIPS // LANES
    conv_pkt = jnp.pad(w["conv_w"][0].reshape(-1, LANES), ((0, CONV_PKT_ROWS - conv_rows), (0, 0)))
    (w_in4, conv_all), = _comm_only([_gather_now([slot_buffer(w["w_in"][0], bf16)], [slot_buffer(conv_pkt, f32)])], "gather_w_in")
    conv_full = jnp.concatenate([conv_all[d, :conv_rows].reshape(CONV_W, GDN_QKV // N_CHIPS) for d in range(N_CHIPS)], axis=1)
    wi = jnp.concatenate([w_in4[d] for d in range(N_CHIPS)], axis=1)
    w_cat = jnp.concatenate([wi[:, :OFF_BETA], wi[:, OFF_FOX:OFF_F], wi[:, OFF_BETA:OFF_FOX], wi[:, OFF_F:],
                             jnp.zeros((D_MODEL, D_CAT - D_IN), bf16)], axis=1)

    small = {n: w[n] for n in SMALL_NAMES}
    qc = jnp.stack([q, ci]).astype(jnp.int32)
    grad_x, gc, g_conv, g_late, small_g = _device_grads(x[0], p[0, 0], loss_target[0], small, w_cat, conv_full,
                                                        [slot_buffer(w[n][0], bf16) for n in LATE], qc)

    g_in = jnp.concatenate([gc[:, :OFF_BETA], gc[:, SEG_SMALL:SEG_SMALL + 8], gc[:, SEG_FOX:SEG_SMALL],
                            gc[:, SEG_SMALL + 8:SEG_SMALL + 16]], axis=1)
    shard_cols = D_IN // N_CHIPS
    g_in4 = jnp.stack([g_in[:, d * shard_cols:(d + 1) * shard_cols] for d in range(N_CHIPS)])
    (b1,), (small_all,) = _comm_only(
        [_exchange_pairs([g_in4]), _gather_packets(slot_buffer(_pack_small(small_g, g_conv), f32, 8, 4 * xi + 2 * yi + ci))], "exchange_w_in")
    a4, a4b = _add_pair(g_in4, b1, qc, "add_pair_w_in")
    (b2,), = _comm_only([_exchange_chips([a4b])], "exchange_chips_w_in")
    (g_in_red,), = _comm_only([_share_halves([_add_chips(a4, b2, qc, "add_chips_w_in")])], "share_w_in")
    g_late["w_in"] = g_in_red

    grads, delta, new_m, new_v = {}, {}, {}, {}
    for n, g in g_late.items():
        outs = _adamw(w[n][0], g.reshape(w[n].shape[1:]), m[n][0], v[n][0], "adamw_" + n)
        grads[n], delta[n], new_m[n], new_v[n] = (a.reshape(w[n].shape) for a in outs)
    shapes = {n: w[n].shape for n in SMALL_NAMES}
    g_pkt, d_pkt, m_pkt, v_pkt = _small_sum_adamw(small_all, _pack_small(w), _pack_small(m), _pack_small(v))
    for dst, pkt in ((grads, g_pkt), (delta, d_pkt), (new_m, m_pkt), (new_v, v_pkt)):
        dst.update(_unpack_small(pkt, shapes))
    conv_rows_all = CONV_W * GDN_QKV // LANES
    conv_g_full = g_pkt[SMALL_CONV_ROW:SMALL_CONV_ROW + conv_rows_all].reshape(CONV_W, GDN_QKV)
    conv_g = lax.dynamic_slice_in_dim(conv_g_full, q * (GDN_QKV // N_CHIPS), GDN_QKV // N_CHIPS, axis=1)
    outs = _adamw(w["conv_w"][0], conv_g, m["conv_w"][0], v["conv_w"][0], "adamw_conv_w")
    grads["conv_w"], delta["conv_w"], new_m["conv_w"], new_v["conv_w"] = (a.reshape(w["conv_w"].shape) for a in outs)
    loss = g_pkt[61, 0]
    return (loss, grad_x[None], *[grads[n] for n in WEIGHTS], *[delta[n] for n in WEIGHTS],
            *[new_m[n] for n in WEIGHTS], *[new_v[n] for n in WEIGHTS])
```

```python
import functools

import jax
import jax.numpy as jnp
from jax import lax
from jax.experimental import pallas as pl
from jax.experimental.pallas import tpu as pltpu

f32 = jnp.float32
bf16 = jnp.bfloat16
HI = lax.Precision.HIGHEST
MESH = pl.DeviceIdType.MESH

D_MODEL = 1024
CHUNK = 64
GDN_HEADS = 4
GDN_DK = 128
FOX_HEADS = 8
FOX_DH = 64
CONV_W = 4
D_FF = 4096
D_PLE = 256
LN_EPS = 1e-5
NORM_EPS = 1e-6
ALPHA = 2.0 ** 0.25
GDN_QKV = 1536
OFF_Z = 1536
OFF_BETA = 2048
OFF_FOX = 2056
OFF_F = 3592
D_IN = 3600
ADAM_LR = 0.001
ADAM_B1 = 0.9
ADAM_B2 = 0.999
ADAM_EPS = 1e-08
ADAM_WD = 0.01
ADAM_STEP = 10

SEG_FOX = 2048
SEG_SMALL = 3584
D_CAT = 3840
LANES = 128
TOK_BLK = 256
FOX_BQ = 256
VMEM_LIMIT = 56 * 1024 * 1024
NEG = -1e30

N_CHIPS = 4


def _params(sem=None, **kw):
    return pltpu.CompilerParams(dimension_semantics=sem, vmem_limit_bytes=VMEM_LIMIT, **kw)


def _sigmoid(x):
    return 1.0 / (1.0 + jnp.exp(-x))


def _softplus(x):
    return jnp.maximum(x, 0.0) + jnp.log(1.0 + jnp.exp(-jnp.abs(x)))


def _ln_fwd(x, g, b):
    mu = jnp.mean(x, -1, keepdims=True)
    xc = x - mu
    var = jnp.mean(xc * xc, -1, keepdims=True)
    rstd = lax.rsqrt(var + LN_EPS)
    xhat = xc * rstd
    return xhat * g + b, xhat, rstd


def _ln_bwd(dy, xhat, rstd, g):
    dxh = dy * g
    m1 = jnp.mean(dxh, -1, keepdims=True)
    m2 = jnp.mean(dxh * xhat, -1, keepdims=True)
    return rstd * (dxh - m1 - xhat * m2)


def _dot(a, b, prec=HI):
    return jnp.dot(a, b, precision=prec, preferred_element_type=f32)


def _dot_nt(a, b, prec=HI):
    return lax.dot_general(a, b, (((1,), (1,)), ((), ())), precision=prec, preferred_element_type=f32)


def _dot_tn(a, b, prec=HI):
    return lax.dot_general(a, b, (((0,), (0,)), ((), ())), precision=prec, preferred_element_type=f32)


def _bdot(a, b):
    return _dot(a.astype(bf16), b.astype(bf16), None)


def _bdot_nt(a, b):
    return _dot_nt(a.astype(bf16), b.astype(bf16), None)


def _bdot_tn(a, b):
    return _dot_tn(a.astype(bf16), b.astype(bf16), None)


def _lane(shape):
    return lax.broadcasted_iota(jnp.int32, shape, len(shape) - 1)


def _mm(a, b, mode, tm, tn, name, out_dtype=f32, epi=None, extra=None, shards=1):
    if mode == "nn":
        (m, k), n = a.shape, b.shape[-1] * shards
    elif mode == "nt":
        (m, k), n = a.shape, b.shape[-2]
    else:
        (k, m), n = a.shape, b.shape[1]
    assert m % tm == 0 and n % tn == 0, (name, m, n, tm, tn)
    per = (n // shards) // tn
    assert mode == "nt" or per * tn * shards == n, (name, n, tn, shards)
    nc = 512 if tn % 512 == 0 else (256 if tn % 256 == 0 else 128)
    ks = k // shards

    def body(a_ref, b_ref, *rest):
        for n0 in range(0, tn, nc):
            if mode == "nn":
                acc = jnp.dot(a_ref[...], b_ref[:, n0:n0 + nc], preferred_element_type=f32)
            elif mode == "nt" and shards > 1:
                acc = jnp.zeros((tm, nc), f32)
                for d in range(shards):
                    acc = acc + lax.dot_general(a_ref[:, d * ks:(d + 1) * ks], b_ref[d, n0:n0 + nc, :], (((1,), (1,)), ((), ())),
                                                preferred_element_type=f32)
            elif mode == "nt":
                acc = lax.dot_general(a_ref[...], b_ref[n0:n0 + nc, :], (((1,), (1,)), ((), ())), preferred_element_type=f32)
            else:
                acc = lax.dot_general(a_ref[...], b_ref[:, n0:n0 + nc], (((0,), (0,)), ((), ())), preferred_element_type=f32)
            if epi == "relu2":
                up_ref, act_ref = rest
                up_ref[:, n0:n0 + nc] = acc
                r = jnp.maximum(acc, 0.0)
                act_ref[:, n0:n0 + nc] = (r * r).astype(bf16)
            elif epi == "relu2_bwd":
                up_ref, o_ref = rest
                o_ref[:, n0:n0 + nc] = (acc * (2.0 * jnp.maximum(up_ref[:, n0:n0 + nc], 0.0))).astype(bf16)
            else:
                (o_ref,) = rest
                o_ref[:, n0:n0 + nc] = acc.astype(out_dtype)

    if mode == "tn":
        a_spec = pl.BlockSpec((k, tm), lambda j, i: (0, i))
    else:
        a_spec = pl.BlockSpec((tm, k), lambda j, i: (i, 0))
    if mode == "nt" and shards > 1:
        b_spec = pl.BlockSpec((shards, tn, ks), lambda j, i: (0, j, 0))
    elif mode == "nt":
        b_spec = pl.BlockSpec((tn, k), lambda j, i: (j, 0))
    elif mode == "nn" and shards > 1:
        b_spec = pl.BlockSpec((None, k, tn), lambda j, i: (j // per, 0, j % per))
    else:
        b_spec = pl.BlockSpec((k, tn), lambda j, i: (0, j))
    o_spec = pl.BlockSpec((tm, tn), lambda j, i: (i, j))
    in_specs = [a_spec, b_spec]
    args = [a, b]
    if epi == "relu2":
        out_shape = (jax.ShapeDtypeStruct((m, n), f32), jax.ShapeDtypeStruct((m, n), bf16))
        out_specs = (o_spec, o_spec)
    elif epi == "relu2_bwd":
        in_specs.append(o_spec)
        args.append(extra)
        out_shape = jax.ShapeDtypeStruct((m, n), bf16)
        out_specs = o_spec
    elif mode == "tn" and shards > 1:
        out_shape = jax.ShapeDtypeStruct((shards, m, n // shards), out_dtype)
        out_specs = pl.BlockSpec((None, tm, tn), lambda j, i: (j // per, i, j % per))
    else:
        out_shape = jax.ShapeDtypeStruct((m, n), out_dtype)
        out_specs = o_spec
    return pl.pallas_call(
        body, name=name, grid=(n // tn, m // tm), in_specs=in_specs, out_specs=out_specs, out_shape=out_shape,
        compiler_params=_params(("parallel", "parallel")),
    )(*args)


def _row_spec(width, col=0):
    return pl.BlockSpec((TOK_BLK, width), lambda i: (i, col))


def _vec_spec(rows, width):
    return pl.BlockSpec((rows, width), lambda i: (0, 0))


def _ln_in(x, g, b):
    t, d = x.shape

    def body(x_ref, g_ref, b_ref, h_ref, hb_ref):
        h, _, _ = _ln_fwd(x_ref[...], g_ref[...], b_ref[...])
        h_ref[...] = h
        hb_ref[...] = h.astype(bf16)

    return pl.pallas_call(
        body, name="ln_in", grid=(t // TOK_BLK,),
        in_specs=[_row_spec(d), _vec_spec(1, d), _vec_spec(1, d)],
        out_specs=(_row_spec(d), _row_spec(d)),
        out_shape=(jax.ShapeDtypeStruct((t, d), f32), jax.ShapeDtypeStruct((t, d), bf16)),
        compiler_params=_params(("parallel",)),
    )(x, g, b)


def _attn_post(o_gdn, proj, o_fox, g_gdn, g_fox2, comm=None):
    t = o_gdn.shape[0]

    def body(og_ref, z_ref, of_ref, gg_ref, gf_ref, out_ref):
        for h in range(GDN_HEADS):
            sl = slice(h * LANES, (h + 1) * LANES)
            og = og_ref[:, sl]
            z = z_ref[:, sl]
            r = lax.rsqrt(jnp.mean(og * og, -1, keepdims=True) + NORM_EPS)
            out_ref[:, sl] = (og * r * gg_ref[...] * (z * _sigmoid(z))).astype(bf16)
        lo = _lane((TOK_BLK, LANES)) < FOX_DH
        for pr in range(FOX_HEADS // 2):
            sl = slice(pr * LANES, (pr + 1) * LANES)
            of = of_ref[:, sl]
            sq = of * of
            s0 = jnp.sum(jnp.where(lo, sq, 0.0), -1, keepdims=True)
            s1 = jnp.sum(jnp.where(lo, 0.0, sq), -1, keepdims=True)
            r = lax.rsqrt(jnp.where(lo, s0, s1) * (1.0 / FOX_DH) + NORM_EPS)
            out_ref[:, 512 + pr * LANES:512 + (pr + 1) * LANES] = (of * r * gf_ref[...]).astype(bf16)

    return _hosted(
        body, comm, name="attn_post", grid=(t // TOK_BLK,),
        in_specs=[_row_spec(512), _row_spec(512, OFF_Z // 512), _row_spec(512), _vec_spec(1, LANES), _vec_spec(1, LANES)],
        out_specs=(_row_spec(D_MODEL),),
        out_shape=(jax.ShapeDtypeStruct((t, D_MODEL), bf16),),
        args=(o_gdn, proj, o_fox, g_gdn, g_fox2))


def _attn_post_bwd(dattn, o_gdn, proj, o_fox, g_gdn, g_fox2):
    t = o_gdn.shape[0]

    def body(da_ref, og_ref, z_ref, of_ref, gg_ref, gf_ref, dog_ref, dz_ref, dof_ref, pg_ref):
        i = pl.program_id(0)

        @pl.when(i == 0)
        def _():
            pg_ref[...] = jnp.zeros_like(pg_ref)

        dgg = jnp.zeros((1, LANES), f32)
        for h in range(GDN_HEADS):
            sl = slice(h * LANES, (h + 1) * LANES)
            og = og_ref[:, sl]
            z = z_ref[:, sl]
            dout = da_ref[:, sl]
            g = gg_ref[...]
            r = lax.rsqrt(jnp.mean(og * og, -1, keepdims=True) + NORM_EPS)
            sg = _sigmoid(z)
            silu = z * sg
            ng = og * r * g
            dng = dout * silu
            dz_ref[:, sl] = (dout * ng * (sg * (1.0 + z * (1.0 - sg)))).astype(bf16)
            dgg = dgg + jnp.sum(dng * og * r, 0, keepdims=True)
            gd = dng * g
            dog_ref[:, sl] = r * gd - og * (r * r * r) * jnp.mean(og * gd, -1, keepdims=True)
        pg_ref[0:1, :] += dgg
        lo = _lane((TOK_BLK, LANES)) < FOX_DH
        dgf = jnp.zeros((1, LANES), f32)
        for pr in range(FOX_HEADS // 2):
            sl = slice(pr * LANES, (pr + 1) * LANES)
            of = of_ref[:, sl]
            dout = da_ref[:, 512 + pr * LANES:512 + (pr + 1) * LANES]
            g = gf_ref[...]
            sq = of * of
            s0 = jnp.sum(jnp.where(lo, sq, 0.0), -1, keepdims=True)
            s1 = jnp.sum(jnp.where(lo, 0.0, sq), -1, keepdims=True)
            r = lax.rsqrt(jnp.where(lo, s0, s1) * (1.0 / FOX_DH) + NORM_EPS)
            dgf = dgf + jnp.sum(dout * of * r, 0, keepdims=True)
            gd = dout * g
            xg = of * gd
            m0 = jnp.sum(jnp.where(lo, xg, 0.0), -1, keepdims=True)
            m1 = jnp.sum(jnp.where(lo, 0.0, xg), -1, keepdims=True)
            dof_ref[:, sl] = r * gd - of * (r * r * r) * (jnp.where(lo, m0, m1) * (1.0 / FOX_DH))
        pg_ref[1:2, :] += dgf

    return pl.pallas_call(
        body, name="attn_post_bwd", grid=(t // TOK_BLK,),
        in_specs=[_row_spec(D_MODEL), _row_spec(512), _row_spec(512, OFF_Z // 512), _row_spec(512), _vec_spec(1, LANES), _vec_spec(1, LANES)],
        out_specs=(_row_spec(512), _row_spec(512), _row_spec(512), _vec_spec(8, LANES)),
        out_shape=(jax.ShapeDtypeStruct((t, 512), f32), jax.ShapeDtypeStruct((t, 512), bf16),
                   jax.ShapeDtypeStruct((t, 512), f32), jax.ShapeDtypeStruct((8, LANES), f32)),
        compiler_params=_params(("arbitrary",)),
    )(dattn, o_gdn, proj, o_fox, g_gdn, g_fox2)


def _ln1(h0, mix, g, b):
    t, d = h0.shape

    def body(h0_ref, mix_ref, g_ref, b_ref, h_ref, hb_ref, xh_ref, rs_ref):
        h, xhat, rstd = _ln_fwd(ALPHA * h0_ref[...] + mix_ref[...], g_ref[...], b_ref[...])
        h_ref[...] = h
        hb_ref[...] = h.astype(bf16)
        xh_ref[...] = xhat
        rs_ref[...] = jnp.broadcast_to(rstd, rs_ref.shape)

    return pl.pallas_call(
        body, name="ln1", grid=(t // TOK_BLK,),
        in_specs=[_row_spec(d), _row_spec(d), _vec_spec(1, d), _vec_spec(1, d)],
        out_specs=(_row_spec(d), _row_spec(d), _row_spec(d), _row_spec(LANES)),
        out_shape=(jax.ShapeDtypeStruct((t, d), f32), jax.ShapeDtypeStruct((t, d), bf16),
                   jax.ShapeDtypeStruct((t, d), f32), jax.ShapeDtypeStruct((t, LANES), f32)),
        compiler_params=_params(("parallel",)),
    )(h0, mix, g, b)


def _ln2_loss(h1, ff, pe, gp, b_gate, g, b, target):
    t, d = h1.shape

    def body(h1_ref, ff_ref, pe_ref, gp_ref, bg_ref, g_ref, b_ref, t_ref, dr_ref, drb_ref, dpe_ref, dgp_ref, pg_ref):
        i = pl.program_id(0)

        @pl.when(i == 0)
        def _():
            pg_ref[...] = jnp.zeros_like(pg_ref)

        sig = _sigmoid(gp_ref[...] + bg_ref[...])
        pe = pe_ref[...]
        r2 = ALPHA * h1_ref[...] + ff_ref[...] + pe * sig
        y, xhat, rstd = _ln_fwd(r2, g_ref[...], b_ref[...])
        err = y - t_ref[...]
        dy = err * (1.0 / d)
        dr = _ln_bwd(dy, xhat, rstd, g_ref[...])
        dr_ref[...] = dr
        drb_ref[...] = dr.astype(bf16)
        dpe_ref[...] = (dr * sig).astype(bf16)
        dgp = dr * pe * sig * (1.0 - sig)
        dgp_ref[...] = dgp.astype(bf16)
        pg_ref[0:1, :] += jnp.sum(dy * xhat, 0, keepdims=True)
        pg_ref[1:2, :] += jnp.sum(dy, 0, keepdims=True)
        pg_ref[2:3, :] += jnp.sum(dgp, 0, keepdims=True)
        pg_ref[3:4, :] += 0.5 * jnp.sum(jnp.mean(err * err, -1, keepdims=True), 0, keepdims=True)

    return pl.pallas_call(
        body, name="ln2_loss", grid=(t // TOK_BLK,),
        in_specs=[_row_spec(d)] * 4 + [_vec_spec(1, d)] * 3 + [_row_spec(d)],
        out_specs=(_row_spec(d), _row_spec(d), _row_spec(d), _row_spec(d), _vec_spec(8, d)),
        out_shape=(jax.ShapeDtypeStruct((t, d), f32), jax.ShapeDtypeStruct((t, d), bf16), jax.ShapeDtypeStruct((t, d), bf16),
                   jax.ShapeDtypeStruct((t, d), bf16), jax.ShapeDtypeStruct((8, d), f32)),
        compiler_params=_params(("arbitrary",)),
    )(h1, ff, pe, gp, b_gate, g, b, target)


def _ln1_bwd(dr2, da, db, xhat, rstd, g):
    t, d = dr2.shape

    def body(dr2_ref, da_ref, db_ref, xh_ref, rs_ref, g_ref, dr_ref, drb_ref, pg_ref):
        i = pl.program_id(0)

        @pl.when(i == 0)
        def _():
            pg_ref[...] = jnp.zeros_like(pg_ref)

        dh = ALPHA * dr2_ref[...] + da_ref[...] + db_ref[...]
        xhat = xh_ref[...]
        dr = _ln_bwd(dh, xhat, rs_ref[:, 0:1], g_ref[...])
        dr_ref[...] = dr
        drb_ref[...] = dr.astype(bf16)
        pg_ref[0:1, :] += jnp.sum(dh * xhat, 0, keepdims=True)
        pg_ref[1:2, :] += jnp.sum(dh, 0, keepdims=True)

    return pl.pallas_call(
        body, name="ln1_bwd", grid=(t // TOK_BLK,),
        in_specs=[_row_spec(d)] * 4 + [_row_spec(LANES), _vec_spec(1, d)],
        out_specs=(_row_spec(d), _row_spec(d), _vec_spec(8, d)),
        out_shape=(jax.ShapeDtypeStruct((t, d), f32), jax.ShapeDtypeStruct((t, d), bf16), jax.ShapeDtypeStruct((8, d), f32)),
        compiler_params=_params(("arbitrary",)),
    )(dr2, da, db, xhat, rstd, g)


def _ln_in_bwd(x, dr1, dmm, g):
    t, d = x.shape

    def body(x_ref, dr1_ref, dmm_ref, g_ref, dx_ref, pg_ref):
        i = pl.program_id(0)

        @pl.when(i == 0)
        def _():
            pg_ref[...] = jnp.zeros_like(pg_ref)

        dh = ALPHA * dr1_ref[...] + dmm_ref[...]
        _, xhat, rstd = _ln_fwd(x_ref[...], g_ref[...], 0.0)
        dx_ref[...] = _ln_bwd(dh, xhat, rstd, g_ref[...])
        pg_ref[0:1, :] += jnp.sum(dh * xhat, 0, keepdims=True)
        pg_ref[1:2, :] += jnp.sum(dh, 0, keepdims=True)

    return pl.pallas_call(
        body, name="ln_in_bwd", grid=(t // TOK_BLK,),
        in_specs=[_row_spec(d)] * 3 + [_vec_spec(1, d)],
        out_specs=(_row_spec(d), _vec_spec(8, d)),
        out_shape=(jax.ShapeDtypeStruct((t, d), f32), jax.ShapeDtypeStruct((8, d), f32)),
        compiler_params=_params(("arbitrary",)),
    )(x, dr1, dmm, g)


def _tri(n, upper=False, strict=False):
    r = lax.broadcasted_iota(jnp.int32, (n, n), 0)
    c = lax.broadcasted_iota(jnp.int32, (n, n), 1)
    if upper:
        m = (c > r) if strict else (c >= r)
    else:
        m = (c < r) if strict else (c <= r)
    return jnp.where(m, 1.0, 0.0).astype(f32)


def _gate_values(x, bias, alog, lane):
    z = x + bias
    return jnp.where(lane < 4, _sigmoid(z), jnp.where(lane < 8, -jnp.exp(alog) * _softplus(z), jnp.where(lane < 16, -_softplus(-z), 0.0)))


def _gates(proj, bias_row, alog_row):
    t = proj.shape[0]
    nch = t // CHUNK

    def body(x_ref, bias_ref, alog_ref, gates_ref, gcum_ref, gcumt_ref):
        lane = _lane((t, LANES))
        gates = _gate_values(x_ref[...], bias_ref[...], alog_ref[...], lane)
        gates_ref[...] = gates
        g3 = gates.reshape(nch, CHUNK, LANES)
        tri = jnp.broadcast_to(_tri(CHUNK)[None], (nch, CHUNK, CHUNK))
        loc = jnp.einsum("bij,bjk->bik", tri, g3, precision=HI, preferred_element_type=f32)
        tot = jnp.sum(g3, axis=1)
        offs = _dot(_tri(nch, strict=True), tot)
        glob = loc + offs[:, None, :]
        lane3 = _lane((nch, CHUNK, LANES))
        gcum = jnp.where(lane3 < 4, g3, jnp.where(lane3 < 8, loc, glob)).reshape(t, LANES)
        gcum_ref[...] = gcum
        gcumt_ref[...] = gcum.T

    return pl.pallas_call(
        body, name="gates", grid=(1,),
        in_specs=[pl.BlockSpec((t, LANES), lambda i: (0, SEG_SMALL // LANES)), _vec_spec(1, LANES), _vec_spec(1, LANES)],
        out_specs=(pl.BlockSpec((t, LANES), lambda i: (0, 0)), pl.BlockSpec((t, LANES), lambda i: (0, 0)),
                   pl.BlockSpec((LANES, t), lambda i: (0, 0))),
        out_shape=(jax.ShapeDtypeStruct((t, LANES), f32), jax.ShapeDtypeStruct((t, LANES), f32), jax.ShapeDtypeStruct((LANES, t), f32)),
        compiler_params=_params(("arbitrary",)),
    )(proj, bias_row, alog_row)


def _gates_bwd(proj, bias_row, alog_row, gates, dgates, dccol, dct):
    t = proj.shape[0]
    nch = t // CHUNK

    def body(x_ref, bias_ref, alog_ref, gates_ref, dg_ref, dcc_ref, dct_ref, dx_ref, pg_ref):
        lane = _lane((t, LANES))
        d = dg_ref[...] + dcc_ref[...] + dct_ref[...].T
        d3 = d.reshape(nch, CHUNK, LANES)
        tri = jnp.broadcast_to(_tri(CHUNK, upper=True)[None], (nch, CHUNK, CHUNK))
        loc = jnp.einsum("bij,bjk->bik", tri, d3, precision=HI, preferred_element_type=f32)
        tot = jnp.sum(d3, axis=1)
        offs = _dot(_tri(nch, upper=True, strict=True), tot)
        glob = loc + offs[:, None, :]
        lane3 = _lane((nch, CHUNK, LANES))
        dpre = jnp.where(lane3 < 4, d3, jnp.where(lane3 < 8, loc, glob)).reshape(t, LANES)
        z = x_ref[...] + bias_ref[...]
        sg = _sigmoid(z)
        dx = jnp.where(lane < 4, dpre * sg * (1.0 - sg),
                       jnp.where(lane < 8, dpre * (-jnp.exp(alog_ref[...])) * sg, jnp.where(lane < 16, dpre * (1.0 - sg), 0.0)))
        dx_ref[...] = dx.astype(bf16)
        pg_ref[...] = jnp.zeros_like(pg_ref)
        pg_ref[0:1, :] = jnp.sum(dx, 0, keepdims=True)
        pg_ref[1:2, :] = jnp.sum(jnp.where((lane >= 4) & (lane < 8), dpre * gates_ref[...], 0.0), 0, keepdims=True)

    full = pl.BlockSpec((t, LANES), lambda i: (0, 0))
    return pl.pallas_call(
        body, name="gates_bwd", grid=(1,),
        in_specs=[pl.BlockSpec((t, LANES), lambda i: (0, SEG_SMALL // LANES)), _vec_spec(1, LANES), _vec_spec(1, LANES),
                  full, full, full, pl.BlockSpec((LANES, t), lambda i: (0, 0))],
        out_specs=(full, _vec_spec(8, LANES)),
        out_shape=(jax.ShapeDtypeStruct((t, LANES), bf16), jax.ShapeDtypeStruct((8, LANES), f32)),
        compiler_params=_params(("arbitrary",)),
    )(proj, bias_row, alog_row, gates, dgates, dccol, dct)


def _conv_act(u, cw, row, t):
    c = cw[3:4, :] * u
    for jj in range(CONV_W - 1):
        sh = CONV_W - 1 - jj
        c = c + cw[jj:jj + 1, :] * jnp.where(row >= sh, pltpu.roll(u, sh, axis=0), 0.0)
    return c


def _gdn_conv(proj, conv_w, comm=None):
    t = proj.shape[0]
    nblk = GDN_QKV // LANES

    def body(u_ref, cw_ref, c_ref, y_ref):
        j = pl.program_id(0)
        row = lax.broadcasted_iota(jnp.int32, (t, LANES), 0)
        c = _conv_act(u_ref[...], cw_ref[...], row, t)
        c_ref[...] = c
        s = c * _sigmoid(c)
        r = lax.rsqrt(jnp.sum(s * s, -1, keepdims=True) + NORM_EPS)
        scale = jnp.where(j < GDN_HEADS, GDN_DK ** -0.5, 1.0)
        y_ref[...] = jnp.where(j < 2 * GDN_HEADS, s * (r * scale), s)

    blk = pl.BlockSpec((t, LANES), lambda j: (0, j))
    return _hosted(
        body, comm, name="gdn_conv", grid=(nblk,),
        in_specs=[blk, pl.BlockSpec((CONV_W, LANES), lambda j: (0, j))],
        out_specs=(blk, blk),
        out_shape=(jax.ShapeDtypeStruct((t, GDN_QKV), f32), jax.ShapeDtypeStruct((t, GDN_QKV), f32)),
        args=(proj, conv_w))


def _gdn_conv_bwd(proj, conv_w, c, dy, comm=None):
    t = proj.shape[0]
    nblk = GDN_QKV // LANES

    def body(u_ref, cw_ref, c_ref, dy_ref, du_ref, dcw_ref):
        j = pl.program_id(0)
        row = lax.broadcasted_iota(jnp.int32, (t, LANES), 0)
        u = u_ref[...]
        cw = cw_ref[...]
        c = c_ref[...]
        dy = dy_ref[...]
        sg = _sigmoid(c)
        s = c * sg
        r = lax.rsqrt(jnp.sum(s * s, -1, keepdims=True) + NORM_EPS)
        n = s * r
        scale = jnp.where(j < GDN_HEADS, GDN_DK ** -0.5, 1.0)
        dn = dy * scale
        ds = jnp.where(j < 2 * GDN_HEADS, r * (dn - n * jnp.sum(dn * n, -1, keepdims=True)), dy)
        dc = ds * (sg * (1.0 + c * (1.0 - sg)))
        du = cw[3:4, :] * dc
        dcw_ref[...] = jnp.zeros_like(dcw_ref)
        dcw_ref[3:4, :] = jnp.sum(dc * u, 0, keepdims=True)
        for jj in range(CONV_W - 1):
            sh = CONV_W - 1 - jj
            du = du + cw[jj:jj + 1, :] * jnp.where(row < t - sh, pltpu.roll(dc, t - sh, axis=0), 0.0)
            dcw_ref[jj:jj + 1, :] = jnp.sum(dc * jnp.where(row >= sh, pltpu.roll(u, sh, axis=0), 0.0), 0, keepdims=True)
        du_ref[...] = du.astype(bf16)

    blk = pl.BlockSpec((t, LANES), lambda j: (0, j))
    return _hosted(
        body, comm, name="gdn_conv_bwd", grid=(nblk,),
        in_specs=[blk, pl.BlockSpec((CONV_W, LANES), lambda j: (0, j)), blk, blk],
        out_specs=(blk, pl.BlockSpec((8, LANES), lambda j: (0, j))),
        out_shape=(jax.ShapeDtypeStruct((t, GDN_QKV), bf16), jax.ShapeDtypeStruct((8, GDN_QKV), f32)),
        args=(proj, conv_w, c, dy))


def _chunk_masks():
    r = lax.broadcasted_iota(jnp.int32, (CHUNK, CHUNK), 0)
    c = lax.broadcasted_iota(jnp.int32, (CHUNK, CHUNK), 1)
    return r >= c, r > c, r == c


def _col_to_row(col, eye):
    return jnp.sum(jnp.where(eye, col, 0.0), axis=0, keepdims=True)


def _row_to_col(row, eye):
    return jnp.sum(jnp.where(eye, row, 0.0), axis=1, keepdims=True)


NN = (((1,), (0,)), ((), ()))
NT = (((1,), (1,)), ((), ()))
TN = (((0,), (0,)), ((), ()))
GDN_GROUP = 4


def _mx(a, b, dims=NN, passes=1):
    d = lambda p, q: lax.dot_general(p, q, dims, preferred_element_type=f32)
    ah, bh = a.astype(bf16), b.astype(bf16)
    if passes == 1:
        return d(ah, bh)
    al = (a - ah.astype(f32)).astype(bf16)
    bl = (b - bh.astype(f32)).astype(bf16)
    return d(ah, bh) + (d(ah, bl) + d(al, bh))


def _gdn_decay(gam, masks):
    causal, _, eye = masks
    return jnp.exp(jnp.where(causal, gam - _col_to_row(gam, eye), NEG))


def _gdn_local(y, gcum, comm=None):
    t = y.shape[0]
    nch = t // CHUNK
    rows_blk = GDN_GROUP * CHUNK

    def body(y_ref, g_ref, u_ref, w_ref, qk_ref, tinv_ref):
        masks = _chunk_masks()
        _, strict, eye = masks
        ids = [(j, h) for j in range(GDN_GROUP) for h in range(GDN_HEADS)]
        rs = lambda j: slice(j * CHUNK, (j + 1) * CHUNK)
        col = lambda base, h: slice(base + h * LANES, base + (h + 1) * LANES)
        kn = [y_ref[rs(j), col(512, h)] for j, h in ids]
        beta = [g_ref[rs(j), h:h + 1] for j, h in ids]
        gam = [g_ref[rs(j), 4 + h:5 + h] for j, h in ids]
        dec = [_gdn_decay(g, masks) for g in gam]
        x = [-jnp.where(strict, _mx(k, k, NT) * d * b, 0.0) for k, d, b in zip(kn, dec, beta)]
        tinv = [jnp.where(eye, 1.0, 0.0) + a for a in x]
        for _ in range(5):
            x = [_mx(a, a, NN, 3) for a in x]
            tinv = [t_ + _mx(t_, a, NN, 3) for t_, a in zip(tinv, x)]
        for (j, h), t_, k, d, b, g in zip(ids, tinv, kn, dec, beta, gam):
            u_ref[rs(j), col(0, h)] = _mx(t_, b * y_ref[rs(j), col(1024, h)])
            w_ref[rs(j), col(0, h)] = _mx(t_, (b * jnp.exp(g)) * k)
            qk_ref[j, h] = _mx(y_ref[rs(j), col(0, h)], k, NT) * d
            tinv_ref[j, h] = t_

    mat = pl.BlockSpec((GDN_GROUP, GDN_HEADS, CHUNK, CHUNK), lambda n: (n, 0, 0, 0))
    return _hosted(
        body, comm, name="gdn_local", grid=(nch // GDN_GROUP,),
        in_specs=[pl.BlockSpec((rows_blk, GDN_QKV), lambda n: (n, 0)), pl.BlockSpec((rows_blk, LANES), lambda n: (n, 0))],
        out_specs=(pl.BlockSpec((rows_blk, 512), lambda n: (n, 0)), pl.BlockSpec((rows_blk, 512), lambda n: (n, 0)), mat, mat),
        out_shape=(jax.ShapeDtypeStruct((t, 512), f32), jax.ShapeDtypeStruct((t, 512), f32),
                   jax.ShapeDtypeStruct((nch, GDN_HEADS, CHUNK, CHUNK), f32), jax.ShapeDtypeStruct((nch, GDN_HEADS, CHUNK, CHUNK), f32)),
        args=(y, gcum))


def _gdn_fwd(y, gcum, u, w, qk, comm=None):
    t = y.shape[0]
    nch = t // CHUNK

    def body(y_ref, g_ref, u_ref, w_ref, qk_ref, o_ref, sall_ref, s_ref):
        @pl.when(pl.program_id(0) == 0)
        def _():
            s_ref[...] = jnp.zeros_like(s_ref)

        heads = range(GDN_HEADS)
        sl = [slice(h * LANES, (h + 1) * LANES) for h in heads]
        gam = [g_ref[:, 4 + h:5 + h] for h in heads]
        gam_last = [g[CHUNK - 1:CHUNK, :] for g in gam]
        s = [s_ref[h] for h in heads]
        for h in heads:
            sall_ref[0, h] = s[h]
        ws = [_mx(w_ref[:, sl[h]], s[h]) for h in heads]
        qs = [_mx(y_ref[:, sl[h]] * jnp.exp(gam[h]), s[h]) for h in heads]
        vn = [u_ref[:, sl[h]] - ws[h] for h in heads]
        av = [_mx(qk_ref[0, h], vn[h]) for h in heads]
        kv = [_mx(y_ref[:, 512 + h * LANES:512 + (h + 1) * LANES] * jnp.exp(gam_last[h] - gam[h]), vn[h], TN) for h in heads]
        for h in heads:
            o_ref[:, sl[h]] = qs[h] + av[h]
            s_ref[h] = jnp.exp(gam_last[h]) * s[h] + kv[h]

    row = lambda width: pl.BlockSpec((CHUNK, width), lambda n: (n, 0))
    return _hosted(
        body, comm, name="gdn_fwd", grid=(nch,),
        in_specs=[row(GDN_QKV), row(LANES), row(512), row(512), pl.BlockSpec((1, GDN_HEADS, CHUNK, CHUNK), lambda n: (n, 0, 0, 0))],
        out_specs=(row(512), pl.BlockSpec((1, GDN_HEADS, LANES, LANES), lambda n: (n, 0, 0, 0))),
        out_shape=(jax.ShapeDtypeStruct((t, 512), f32), jax.ShapeDtypeStruct((nch, GDN_HEADS, LANES, LANES), f32)),
        scratch_shapes=[pltpu.VMEM((GDN_HEADS, LANES, LANES), f32)],
        args=(y, gcum, u, w, qk))


def _gdn_bwd(y, gcum, u_all, w_all, qk_all, tinv_all, sall, do, comm=None):
    t = y.shape[0]
    nch = t // CHUNK

    def body(y_ref, g_ref, u_ref, w_ref, qk_ref, tinv_ref, sall_ref, do_ref, dy_ref, dg_ref, ds_ref):
        @pl.when(pl.program_id(0) == 0)
        def _():
            ds_ref[...] = jnp.zeros_like(ds_ref)

        masks = _chunk_masks()
        causal, strict, eye = masks
        lane = _lane((CHUNK, LANES))
        row = lax.broadcasted_iota(jnp.int32, (CHUNK, 1), 0)
        heads = range(GDN_HEADS)
        each = lambda f, *ls: [f(*a) for a in zip(*ls)]
        rsum = lambda a: jnp.sum(a, axis=1, keepdims=True)
        sl = [slice(h * LANES, (h + 1) * LANES) for h in heads]
        qn = [y_ref[:, sl[h]] for h in heads]
        kn = [y_ref[:, 512 + h * LANES:512 + (h + 1) * LANES] for h in heads]
        v = [y_ref[:, 1024 + h * LANES:1024 + (h + 1) * LANES] for h in heads]
        beta = [g_ref[:, h:h + 1] for h in heads]
        gam = [g_ref[:, 4 + h:5 + h] for h in heads]
        gam_last = [g[CHUNK - 1:CHUNK, :] for g in gam]
        dec = [_gdn_decay(g, masks) for g in gam]
        e = [jnp.exp(g) for g in gam]
        f = each(lambda gl_, g: jnp.exp(gl_ - g), gam_last, gam)
        gl = [jnp.exp(g) for g in gam_last]
        u = [u_ref[:, sl[h]] for h in heads]
        w = [w_ref[:, sl[h]] for h in heads]
        qk = [qk_ref[0, h] for h in heads]
        tinv = [tinv_ref[0, h] for h in heads]
        s = [sall_ref[0, h] for h in heads]
        dsn = [ds_ref[h] for h in heads]
        d_o = [do_ref[:, sl[h]] for h in heads]
        qd = each(lambda a, b: a * b, qn, e)
        kd = each(lambda a, b: a * b, kn, f)
        ws = each(_mx, w, s)
        kds = each(_mx, kd, dsn)
        qkdo = each(lambda a, b: _mx(a, b, TN), qk, d_o)
        dqd = each(lambda a, b: _mx(a, b, NT), d_o, s)
        qddo = each(lambda a, b: _mx(a, b, TN), qd, d_o)
        kkd = each(lambda k, d: _mx(k, k, NT) * d, kn, dec)
        vn = each(lambda a, b: a - b, u, ws)
        dvn = each(lambda a, b: a + b, qkdo, kds)
        dqk = each(lambda a, b: jnp.where(causal, _mx(a, b, NT), 0.0), d_o, vn)
        dkd = each(lambda a, b: _mx(a, b, NT), vn, dsn)
        dw = each(lambda a, b: -_mx(a, b, NT), dvn, s)
        wdvn = each(lambda a, b: _mx(a, b, TN), w, dvn)
        dgl = each(lambda a, b: jnp.sum(rsum(a * b), axis=0, keepdims=True), dsn, s)
        for h in heads:
            ds_ref[h] = qddo[h] - wdvn[h] + gl[h] * dsn[h]
        dru = each(lambda a, b: _mx(a, b, TN), tinv, dvn)
        drw = each(lambda a, b: _mx(a, b, TN), tinv, dw)
        dqkr = each(lambda a, b: a * b, dqk, dec)
        dq1 = each(_mx, dqkr, kn)
        dk1 = each(lambda a, b: _mx(a, b, TN), dqkr, qn)
        dnu = each(lambda a, b: _mx(a, b, NT), dru, u)
        dnw = each(lambda a, b: _mx(a, b, NT), drw, w)
        dn = each(lambda a, b: jnp.where(strict, -(a + b), 0.0), dnu, dnw)
        dkk = each(lambda a, b, d: a * b * d, dn, beta, dec)
        dk2 = each(_mx, dkk, kn)
        dk3 = each(lambda a, b: _mx(a, b, TN), dkk, kn)
        dgates = jnp.zeros((CHUNK, LANES), f32)
        for h in heads:
            drw_k = rsum(drw[h] * kn[h])
            dbeta = rsum(dru[h] * v[h]) + e[h] * drw_k + rsum(dn[h] * kkd[h])
            m = dn[h] * (kkd[h] * beta[h]) + dqk[h] * qk[h]
            de = beta[h] * drw_k + rsum(dqd[h] * qn[h])
            df = rsum(dkd[h] * kn[h])
            dgam = rsum(m) - _row_to_col(jnp.sum(m, axis=0, keepdims=True), eye) + de * e[h] - df * f[h]
            dgam_last = jnp.sum(df * f[h], axis=0, keepdims=True) + dgl[h] * gl[h]
            dgam = dgam + jnp.where(row == CHUNK - 1, dgam_last, 0.0)
            dy_ref[:, sl[h]] = dq1[h] + dqd[h] * e[h]
            dy_ref[:, 512 + h * LANES:512 + (h + 1) * LANES] = (beta[h] * e[h]) * drw[h] + dk2[h] + dk3[h] + dk1[h] + dkd[h] * f[h]
            dy_ref[:, 1024 + h * LANES:1024 + (h + 1) * LANES] = beta[h] * dru[h]
            dgates = dgates + jnp.where(lane == h, dbeta, 0.0) + jnp.where(lane == 4 + h, dgam, 0.0)
        dg_ref[...] = dgates

    rev = lambda width: pl.BlockSpec((CHUNK, width), lambda n: (nch - 1 - n, 0))
    mat = lambda d: pl.BlockSpec((1, GDN_HEADS, d, d), lambda n: (nch - 1 - n, 0, 0, 0))
    return _hosted(
        body, comm, name="gdn_bwd", grid=(nch,),
        in_specs=[rev(GDN_QKV), rev(LANES), rev(512), rev(512), mat(CHUNK), mat(CHUNK), mat(LANES), rev(512)],
        out_specs=(rev(GDN_QKV), rev(LANES)),
        out_shape=(jax.ShapeDtypeStruct((t, GDN_QKV), f32), jax.ShapeDtypeStruct((t, LANES), f32)),
        scratch_shapes=[pltpu.VMEM((GDN_HEADS, LANES, LANES), f32)],
        args=(y, gcum, u_all, w_all, qk_all, tinv_all, sall, do))


FOX_CLASSES = 4


def _fox_groups(t):
    nq = t // FOX_BQ
    ncls = min(FOX_CLASSES, nq)
    per = nq // ncls
    return [(g * per, per, (g + 1) * per * FOX_BQ) for g in range(ncls)]


def _fox_scores(q_ref, k_ref, gcum_ref, gcumt_ref, h, i, keys):
    pr = h // 2
    lo = (h % 2) * FOX_DH
    lane = _lane((FOX_BQ, LANES))
    mask = (lane >= lo) & (lane < lo + FOX_DH)
    qm = jnp.where(mask, q_ref[:, pr * LANES:(pr + 1) * LANES], 0.0).astype(bf16)
    kp = k_ref[:, pr * LANES:(pr + 1) * LANES].astype(bf16)
    s = _dot_nt(qm, kp, None) * (FOX_DH ** -0.5)
    s = s + gcum_ref[:, 8 + h:9 + h] - gcumt_ref[8 + h:9 + h, :]
    rows = i * FOX_BQ + lax.broadcasted_iota(jnp.int32, (FOX_BQ, keys), 0)
    cols = lax.broadcasted_iota(jnp.int32, (FOX_BQ, keys), 1)
    return jnp.where(cols <= rows, s, NEG), mask, qm, kp


def _fox_fwd(proj, gcum, gcumt, ride=None):
    c0 = SEG_FOX // 512

    def group_call(q0, nq, keys, comm):
        def body(q_ref, k_ref, v_ref, gcum_ref, gcumt_ref, o_ref, lse_ref):
            i = q0 + pl.program_id(0)
            lane = _lane((FOX_BQ, LANES))
            lse_all = jnp.zeros((FOX_BQ, LANES), f32)
            for pr in range(FOX_HEADS // 2):
                vp = v_ref[:, pr * LANES:(pr + 1) * LANES].astype(bf16)
                o_pair = jnp.zeros((FOX_BQ, LANES), f32)
                for h in (2 * pr, 2 * pr + 1):
                    s, mask, _, _ = _fox_scores(q_ref, k_ref, gcum_ref, gcumt_ref, h, i, keys)
                    m = jnp.max(s, axis=1, keepdims=True)
                    p = jnp.exp(s - m)
                    l = jnp.sum(p, axis=1, keepdims=True)
                    o_h = _dot((p * (1.0 / l)).astype(bf16), vp, None)
                    o_pair = jnp.where(mask, o_h, o_pair)
                    lse_all = jnp.where(lane == h, m + jnp.log(l), lse_all)
                o_ref[:, pr * LANES:(pr + 1) * LANES] = o_pair
            lse_ref[...] = lse_all

        seen = lambda col: pl.BlockSpec((keys, 512), lambda i: (0, col))
        return _hosted(
            body, comm, name=f"fox_fwd_{keys}", grid=(nq,),
            in_specs=[pl.BlockSpec((FOX_BQ, 512), lambda i: (q0 + i, c0)), seen(c0 + 1), seen(c0 + 2),
                      pl.BlockSpec((FOX_BQ, LANES), lambda i: (q0 + i, 0)), pl.BlockSpec((LANES, keys), lambda i: (0, 0))],
            out_specs=(pl.BlockSpec((FOX_BQ, 512), lambda i: (i, 0)), pl.BlockSpec((FOX_BQ, LANES), lambda i: (i, 0))),
            out_shape=(jax.ShapeDtypeStruct((nq * FOX_BQ, 512), f32), jax.ShapeDtypeStruct((nq * FOX_BQ, LANES), f32)),
            args=(proj, proj, proj, gcum, gcumt))

    parts = []
    for n, g in enumerate(_fox_groups(proj.shape[0])):
        hook = ride(n) if ride else None
        part, moved = group_call(*g, hook[0] if hook else None)
        parts.append(part)
        if hook:
            hook[1](moved)
    return jnp.concatenate([o for o, _ in parts], axis=0), jnp.concatenate([l for _, l in parts], axis=0)


def _fox_bwd(proj, gcum, gcumt, o, lse, do, ride=None):
    t = proj.shape[0]
    c0 = SEG_FOX // 512

    def group_call(q0, nq, keys, acc, comm):
        first = acc is None

        def body(q_ref, k_ref, v_ref, gcum_ref, gcumt_ref, o_ref, lse_ref, do_ref, *rest):
            dq_ref, dk_ref, dv_ref, dcc_ref, dct_ref = rest[-5:]
            j = pl.program_id(0)
            i = q0 + j

            @pl.when(j == 0)
            def _():
                if first:
                    dk_ref[...] = jnp.zeros_like(dk_ref)
                    dv_ref[...] = jnp.zeros_like(dv_ref)
                    dct_ref[...] = jnp.zeros_like(dct_ref)
                else:
                    dk_ref[...], dv_ref[...], dct_ref[...] = rest[0][...], rest[1][...], rest[2][...]

            lane = _lane((FOX_BQ, LANES))
            dcc = jnp.zeros((FOX_BQ, LANES), f32)
            scale = FOX_DH ** -0.5
            for pr in range(FOX_HEADS // 2):
                sl = slice(pr * LANES, (pr + 1) * LANES)
                vp = v_ref[:, sl].astype(bf16)
                dq_pair = jnp.zeros((FOX_BQ, LANES), f32)
                for h in (2 * pr, 2 * pr + 1):
                    s, mask, qm, kp = _fox_scores(q_ref, k_ref, gcum_ref, gcumt_ref, h, i, keys)
                    p = jnp.exp(s - lse_ref[:, h:h + 1])
                    dom = jnp.where(mask, do_ref[:, sl], 0.0)
                    delta = jnp.sum(dom * o_ref[:, sl], axis=1, keepdims=True)
                    domb = dom.astype(bf16)
                    ds = p * (_dot_nt(domb, vp, None) - delta)
                    dsb = ds.astype(bf16)
                    dv_ref[:, sl] += _dot_tn(p.astype(bf16), domb, None)
                    dk_ref[:, sl] += _dot_tn(dsb, qm, None) * scale
                    dq_pair = jnp.where(mask, _dot(dsb, kp, None) * scale, dq_pair)
                    dcc = jnp.where(lane == 8 + h, jnp.sum(ds, axis=1, keepdims=True), dcc)
                    dct_ref[8 + h:9 + h, :] += -jnp.sum(ds, axis=0, keepdims=True)
                dq_ref[:, sl] = dq_pair.astype(bf16)
            dcc_ref[...] = dcc

        qblk = lambda col: pl.BlockSpec((FOX_BQ, 512), lambda i: (q0 + i, col))
        oblk = pl.BlockSpec((FOX_BQ, 512), lambda i: (i, 0))
        seen = lambda col: pl.BlockSpec((keys, 512), lambda i: (0, col))
        rblk = pl.BlockSpec((FOX_BQ, LANES), lambda i: (q0 + i, 0))
        seen_t = pl.BlockSpec((LANES, keys), lambda i: (0, 0))
        in_specs = [qblk(c0), seen(c0 + 1), seen(c0 + 2), rblk, seen_t, qblk(0), rblk, qblk(0)]
        args = [proj, proj, proj, gcum, gcumt, o, lse, do]
        aliases = {}
        if not first:
            in_specs += [seen(0), seen(0), seen_t]
            args += list(acc)
            aliases = {8: 1, 9: 2, 10: 4}
        return _hosted(
            body, comm, name=f"fox_bwd_{keys}", grid=(nq,), in_specs=in_specs,
            out_specs=(oblk, seen(0), seen(0), pl.BlockSpec((FOX_BQ, LANES), lambda i: (i, 0)), seen_t),
            out_shape=(jax.ShapeDtypeStruct((nq * FOX_BQ, 512), bf16), jax.ShapeDtypeStruct((t, 512), f32), jax.ShapeDtypeStruct((t, 512), f32),
                       jax.ShapeDtypeStruct((nq * FOX_BQ, LANES), f32), jax.ShapeDtypeStruct((LANES, t), f32)),
            aliases=aliases, args=args)

    acc, dqs, dccs = None, [], []
    for n, g in enumerate(reversed(_fox_groups(t))):
        hook = ride(n) if ride else None
        (dq, dk, dv, dcc, dct), moved = group_call(*g, acc, hook[0] if hook else None)
        if hook:
            hook[1](moved)
        acc = (dk, dv, dct)
        dqs.insert(0, dq)
        dccs.insert(0, dcc)
    return jnp.concatenate(dqs, axis=0), acc[0], acc[1], jnp.concatenate(dccs, axis=0), acc[2]


def _row(v, width=None):
    v = v.reshape(1, -1).astype(f32)
    if width is not None and v.shape[1] < width:
        v = jnp.pad(v, ((0, 0), (0, width - v.shape[1])))
    return v


LATE = ("w_out", "w_up", "w_ple_gate", "w_ple", "w_down")


def _device_grads(x, p, target, small, w_cat, conv_w, late, qc=None):
    z4 = jnp.zeros((4,), f32)
    bias_row = _row(jnp.concatenate([z4, small["dt_bias"].reshape(-1), small["b_f"].reshape(-1)]), LANES)
    alog_row = _row(jnp.concatenate([z4, small["a_log"].reshape(-1)]), LANES)
    g_gdn = _row(small["gdn_norm_g"])
    g_fox2 = _row(jnp.tile(small["fox_norm_g"].reshape(-1), 2))
    pb = p.astype(bf16)
    late = list(late)
    comm = qc is not None

    h0, h0b = _ln_in(x, _row(small["ln_in_g"]), _row(small["ln_in_b"]))
    proj = _mm(h0b, w_cat, "nn", 256, D_CAT, "mm_proj")
    gates, gcum, gcumt = _gates(proj, bias_row, alog_row)
    early = [(0, 0, 1), (3, 0, 1), (1, 0, 2), (1, 1, 2)]
    rest = [(2, 0, 1), (4, 0, 2), (4, 1, 2)]
    state = dict(late=late)

    def gather(phase, pieces):
        if not comm:
            return None, lambda moved: None
        return phase(state["late"], pieces), lambda moved: state.update(late=list(moved))

    cm, took = gather(_gather_chips, early[:2])
    (conv_c, qkv_n), moved = _gdn_conv(proj, conv_w, cm)
    took(moved)
    cm, took = gather(_gather_chips, early[2:3])
    (gu, gw, gqk, gtinv), moved = _gdn_local(qkv_n, gcum, cm)
    took(moved)
    cm, took = gather(_gather_chips, early[3:])
    (o_gdn, sall), moved = _gdn_fwd(qkv_n, gcum, gu, gw, gqk, cm)
    took(moved)
    fox_plan = [(_gather_chips, rest[:1]), (_gather_pass_on, early), (_gather_chips, rest[1:2]), (_gather_chips, rest[2:])]
    assert not comm or len(_fox_groups(x.shape[0])) == len(fox_plan)
    o_fox, lse = _fox_fwd(proj, gcum, gcumt, (lambda n: gather(*fox_plan[n])) if comm else None)
    cm, took = gather(_gather_pass_on, rest)
    (attn,), moved = _attn_post(o_gdn, proj, o_fox, g_gdn, g_fox2, cm)
    took(moved)
    w_out, w_up, w_gate, w_ple, w_down = state["late"]
    w_out, w_gate, w_down = w_out.reshape(D_MODEL, D_MODEL), w_gate.reshape(D_MODEL, D_MODEL), w_down.reshape(D_FF, D_MODEL)
    mix = _mm(attn, w_out, "nn", 512, D_MODEL, "mm_mix")
    h1, h1b, xhat1, rstd1 = _ln1(h0, mix, _row(small["ln1_g"]), _row(small["ln1_b"]))
    up, act = _mm(h1b, w_up, "nn", 256, 1024, "mm_up", epi="relu2", shards=N_CHIPS)
    ff = _mm(act, w_down, "nn", 256, D_MODEL, "mm_down")
    gp = _mm(h1b, w_gate, "nn", 512, D_MODEL, "mm_gate")
    pe = _mm(pb, w_ple, "nn", 512, D_MODEL // N_CHIPS, "mm_ple", shards=N_CHIPS)
    dr2, dr2b, dpe, dgp, pg2 = _ln2_loss(h1, ff, pe, gp, _row(small["b_ple_gate"]), _row(small["ln2_g"]), _row(small["ln2_b"]), target)

    dup = _mm(dr2b, w_down, "nt", 256, 2048, "mm_dact", epi="relu2_bwd", extra=up)
    g_down = _mm(act, dr2b, "tn", 1024, D_MODEL, "mm_gdown")
    dh1_a = _mm(dup, w_up, "nt", 256, D_MODEL, "mm_dh1a", shards=N_CHIPS)
    g_up = _mm(h1b, dup, "tn", 1024, 1024, "mm_gup", shards=N_CHIPS)
    dh1_b = _mm(dgp, w_gate, "nt", 512, D_MODEL, "mm_dh1b")
    g_gate = _mm(h1b, dgp, "tn", 1024, D_MODEL, "mm_ggate")
    g_ple = _mm(pb, dpe, "tn", D_PLE, D_MODEL // N_CHIPS, "mm_gple", shards=N_CHIPS)
    dr1, dr1b, pg1 = _ln1_bwd(dr2, dh1_a, dh1_b, xhat1, rstd1, _row(small["ln1_g"]))
    dattn = _mm(dr1b, w_out, "nt", 512, D_MODEL, "mm_dattn")
    g_out = _mm(attn, dr1b, "tn", 1024, D_MODEL, "mm_gout")
    do_gdn, dz, do_fox, pga = _attn_post_bwd(dattn, o_gdn, proj, o_fox, g_gdn, g_fox2)
    g_late = [g.reshape((N_CHIPS, -1, g.shape[-1])) for g in (g_out, g_up, g_gate, g_ple, g_down)]
    chip_plan = [[(4, 0, 2), (0, 0, 1)], [(4, 1, 2)], [(2, 0, 1), (3, 0, 1)], [(1, 0, 2), (1, 1, 2)]]

    def to_sibling():
        def took(moved):
            sums = [_add_pair(g, b1, qc, "add_pair_" + n) for g, b1, n in zip(g_late, moved, LATE)]
            state.update(own=[a for a, _ in sums], sent=[ab for _, ab in sums], landing=_landing([ab for _, ab in sums]))
        return _exchange_pairs(g_late), took

    def to_chips(pieces):
        if not comm:
            return None, lambda moved: None
        return _exchange_chips(state["sent"], state["landing"], pieces), lambda moved: state.update(landing=list(moved))

    assert not comm or len(_fox_groups(x.shape[0])) == len(chip_plan)
    dfq, dfk, dfv, dccol, dct = _fox_bwd(proj, gcum, gcumt, o_fox, lse, do_fox,
                                         (lambda n: to_sibling() if n == 0 else to_chips(chip_plan[n - 1])) if comm else None)
    cm, took = to_chips(chip_plan[-1])
    (dqkv_n, dgates), moved = _gdn_bwd(qkv_n, gcum, gu, gw, gqk, gtinv, sall, do_gdn, cm)
    took(moved)
    dsmall, pgg = _gates_bwd(proj, bias_row, alog_row, gates, dgates, dccol, dct)
    cm = None
    if comm:
        cm = _share_halves([_add_chips(a, b2, qc, "add_chips_" + n) for a, b2, n in zip(state["own"], state["landing"], LATE)])
    (du, g_conv8), reduced = _gdn_conv_bwd(proj, conv_w, conv_c, dqkv_n, cm)
    if comm:
        g_late = list(reduced)
    t = x.shape[0]
    dproj = jnp.concatenate([du, dz, dfq, dfk.astype(bf16), dfv.astype(bf16), dsmall, jnp.zeros((t, D_CAT - SEG_SMALL - LANES), bf16)], axis=1)
    dh0_mm = _mm(dproj, w_cat, "nt", 256, D_MODEL, "mm_dh0")
    g_cat = _mm(h0b, dproj, "tn", 1024, 1280, "mm_gcat")
    grad_x, pg0 = _ln_in_bwd(x, dr1, dh0_mm, _row(small["ln_in_g"]))

    g_fox = pga[1, :FOX_DH] + pga[1, FOX_DH:]
    small_grads = dict(
        ln_in_g=pg0[0], ln_in_b=pg0[1], ln1_g=pg1[0], ln1_b=pg1[1], b_ple_gate=pg2[2], ln2_g=pg2[0], ln2_b=pg2[1],
        gdn_norm_g=pga[0], fox_norm_g=g_fox, a_log=pgg[1, 4:8], dt_bias=pgg[0, 4:8], b_f=pgg[0, 8:16], loss=pg2[3, 0:1])
    return grad_x, g_cat, g_conv8[:CONV_W], dict(zip(LATE, g_late)), small_grads


ANY = pl.BlockSpec(memory_space=pl.ANY)
CONV_PKT_ROWS = 16


def _mesh_pos():
    return lax.axis_index("x"), lax.axis_index("y"), lax.axis_index("c")


def _other_chips(x, y):
    return [(1 - x, y), (x, 1 - y), (1 - x, 1 - y)]


def _rcopy(src, dst, send_sem, recv_sem, dev):
    return pltpu.make_async_remote_copy(src_ref=src, dst_ref=dst, send_sem=send_sem, recv_sem=recv_sem,
                                        device_id=dev, device_id_type=MESH)


class _Comm:
    def __init__(self, ins, outs, aliases, n_sems, start, finish):
        self.ins, self.outs, self.aliases, self.n_sems, self.start, self.finish = list(ins), list(outs), dict(aliases), n_sems, start, finish


def _hosted(body, comm, *, name, grid, in_specs, out_specs, out_shape, args, scratch_shapes=(), aliases=None):
    n_in, n_out, n_sc = len(in_specs), len(out_specs), len(scratch_shapes)
    k, ko = (len(comm.ins), len(comm.outs)) if comm else (0, 0)

    def kernel_body(*refs):
        o0 = n_in + k
        s0 = o0 + n_out + ko
        if comm:
            cins, couts, (ssem, rsem) = refs[n_in:o0], refs[o0 + n_out:s0], refs[s0 + n_sc:]

            @pl.when(pl.program_id(0) == 0)
            def _():
                comm.start(cins, couts, ssem, rsem)

        body(*refs[:n_in], *refs[o0:o0 + n_out], *refs[s0:s0 + n_sc])
        if comm:
            @pl.when(pl.program_id(0) == grid[0] - 1)
            def _():
                comm.finish(cins, couts, ssem, rsem)

    io_aliases = dict(aliases or {})
    scratch = list(scratch_shapes)
    if comm:
        io_aliases.update({n_in + i: n_out + j for i, j in comm.aliases.items()})
        scratch += [pltpu.SemaphoreType.DMA((comm.n_sems,)), pltpu.SemaphoreType.DMA((comm.n_sems,))]
    res = pl.pallas_call(
        kernel_body, name=name, grid=grid, in_specs=list(in_specs) + [ANY] * k, out_specs=tuple(out_specs) + (ANY,) * ko,
        out_shape=tuple(out_shape) + tuple(comm.outs if comm else ()), scratch_shapes=scratch, input_output_aliases=io_aliases,
        compiler_params=_params(("arbitrary",) * len(grid)),
    )(*args, *(comm.ins if comm else ()))
    return tuple(res[:n_out]), tuple(res[n_out:])


def _comm_only(phases, name):
    n_in = sum(len(p.ins) for p in phases)

    def body(*refs):
        n_out = sum(len(p.outs) for p in phases)
        sems = refs[n_in + n_out:]
        i0, o0 = 0, n_in
        for j, p in enumerate(phases):
            cins, couts = refs[i0:i0 + len(p.ins)], refs[o0:o0 + len(p.outs)]
            p.start(cins, couts, sems[2 * j], sems[2 * j + 1])
            p.finish(cins, couts, sems[2 * j], sems[2 * j + 1])
            i0 += len(p.ins)
            o0 += len(p.outs)

    aliases, i0, o0 = {}, 0, 0
    for p in phases:
        aliases.update({i0 + i: o0 + j for i, j in p.aliases.items()})
        i0 += len(p.ins)
        o0 += len(p.outs)
    outs = [o for p in phases for o in p.outs]
    res = pl.pallas_call(
        body, name=name, out_shape=tuple(outs), in_specs=[ANY] * n_in, out_specs=(ANY,) * len(outs), input_output_aliases=aliases,
        scratch_shapes=[pltpu.SemaphoreType.DMA((p.n_sems,)) for p in phases for _ in range(2)],
    )(*[a for p in phases for a in p.ins])
    split, o0 = [], 0
    for p in phases:
        split.append(tuple(res[o0:o0 + len(p.outs)]))
        o0 += len(p.outs)
    return split


def _like(arrays):
    return [jax.ShapeDtypeStruct(a.shape, a.dtype) for a in arrays]


def _half(ref, slot, hf, piece=(0, 1)):
    k, n = piece
    rows = ref.shape[1] // 2 // n
    return ref.at[slot, pl.ds((hf * n + k) * rows, rows)]


def _whole_halves(arrays):
    return [(i, 0, 1) for i in range(len(arrays))]


def _gather_chips(bufs, pieces=None, whole=False, base=0):
    nw = len(bufs)
    pieces = _whole_halves(bufs) if pieces is None else pieces
    part = (lambda ref, slot, c, piece: ref.at[slot]) if whole else _half

    def copies(couts):
        x, y, c = _mesh_pos()
        q = 2 * x + y
        for j, (i, k, n) in enumerate(pieces):
            for kc, chip in enumerate(_other_chips(x, y)):
                mine, theirs = part(couts[i], q, c, (k, n)), part(couts[i], 2 * chip[0] + chip[1], c, (k, n))
                yield base + j * 3 + kc, mine, theirs, (*chip, c)

    def start(cins, couts, ssem, rsem):
        for s, mine, _, dev in copies(couts):
            _rcopy(mine, mine, ssem.at[s], rsem.at[s], dev).start()

    def finish(cins, couts, ssem, rsem):
        for s, _, theirs, dev in copies(couts):
            _rcopy(theirs, theirs, ssem.at[s], rsem.at[s], dev).wait_recv()
        for s, mine, _, dev in copies(couts):
            _rcopy(mine, mine, ssem.at[s], rsem.at[s], dev).wait_send()

    return _Comm(bufs, _like(bufs), {i: i for i in range(nw)}, 3 * len(pieces), start, finish)


def _gather_pass_on(bufs, pieces=None, base=0):
    nw = len(bufs)
    pieces = _whole_halves(bufs) if pieces is None else pieces

    def copies(couts):
        x, y, c = _mesh_pos()
        for j, (i, k, n) in enumerate(pieces):
            for kc, chip in enumerate(_other_chips(x, y)):
                slot = 2 * chip[0] + chip[1]
                yield base + j * 3 + kc, _half(couts[i], slot, c, (k, n)), _half(couts[i], slot, 1 - c, (k, n)), (x, y, 1 - c)

    def start(cins, couts, ssem, rsem):
        for s, landed, _, sib in copies(couts):
            _rcopy(landed, landed, ssem.at[s], rsem.at[s], sib).start()

    def finish(cins, couts, ssem, rsem):
        for s, _, passed, sib in copies(couts):
            _rcopy(passed, passed, ssem.at[s], rsem.at[s], sib).wait_recv()
        for s, landed, _, sib in copies(couts):
            _rcopy(landed, landed, ssem.at[s], rsem.at[s], sib).wait_send()

    return _Comm(bufs, _like(bufs), {i: i for i in range(nw)}, 3 * len(pieces), start, finish)


def _gather_now(bufs, packets):
    nb = len(bufs)
    over, on, pk = _gather_chips(bufs), _gather_pass_on(bufs, base=3 * nb), _gather_chips(packets, whole=True, base=6 * nb)

    def start(cins, couts, ssem, rsem):
        over.start(cins[:nb], couts[:nb], ssem, rsem)
        pk.start(cins[nb:], couts[nb:], ssem, rsem)

    def finish(cins, couts, ssem, rsem):
        over.finish(cins[:nb], couts[:nb], ssem, rsem)
        on.start(cins[:nb], couts[:nb], ssem, rsem)
        on.finish(cins[:nb], couts[:nb], ssem, rsem)
        pk.finish(cins[nb:], couts[nb:], ssem, rsem)

    every = list(bufs) + list(packets)
    return _Comm(every, _like(every), {i: i for i in range(len(every))}, 6 * nb + 3 * len(packets), start, finish)


def _exchange_pairs(gs):
    nw = len(gs)

    def copies(cins, couts):
        x, y, c = _mesh_pos()
        for i in range(nw):
            for d in range(N_CHIPS):
                yield i * N_CHIPS + d, _half(cins[i], d, 1 - c), couts[i].at[d], (x, y, 1 - c)

    def start(cins, couts, ssem, rsem):
        for s, src, dst, sib in copies(cins, couts):
            _rcopy(src, dst, ssem.at[s], rsem.at[s], sib).start()

    def finish(cins, couts, ssem, rsem):
        for s, src, dst, sib in copies(cins, couts):
            _rcopy(src, dst, ssem.at[s], rsem.at[s], sib).wait_recv()
        for s, src, dst, sib in copies(cins, couts):
            _rcopy(src, dst, ssem.at[s], rsem.at[s], sib).wait_send()

    outs = [jax.ShapeDtypeStruct((N_CHIPS, g.shape[1] // 2, g.shape[2]), g.dtype) for g in gs]
    return _Comm(gs, outs, {}, N_CHIPS * nw, start, finish)


def _gather_packets(small):
    def peers():
        x, y, c = _mesh_pos()
        for r in range(1, 8):
            fx, fy, fc = (r >> 2) & 1, (r >> 1) & 1, r & 1
            yield r - 1, (1 - x if fx else x, 1 - y if fy else y, 1 - c if fc else c)

    def start(cins, couts, ssem, rsem):
        x, y, c = _mesh_pos()
        mine = couts[0].at[4 * x + 2 * y + c]
        for s, peer in peers():
            _rcopy(mine, mine, ssem.at[s], rsem.at[s], peer).start()

    def finish(cins, couts, ssem, rsem):
        x, y, c = _mesh_pos()
        mine = couts[0].at[4 * x + 2 * y + c]
        for s, peer in peers():
            theirs = couts[0].at[4 * peer[0] + 2 * peer[1] + peer[2]]
            _rcopy(theirs, theirs, ssem.at[s], rsem.at[s], peer).wait_recv()
        for s, peer in peers():
            _rcopy(mine, mine, ssem.at[s], rsem.at[s], peer).wait_send()

    return _Comm([small], _like([small]), {0: 0}, 7, start, finish)


def _exchange_chips(a4s, b2s, pieces=None):
    nw = len(a4s)
    pieces = _whole_halves(a4s) if pieces is None else pieces

    def copies(cins, couts):
        x, y, c = _mesh_pos()
        for j, (i, k, n) in enumerate(pieces):
            rows = a4s[i].shape[1] // n
            part = pl.ds(k * rows, rows)
            for kc, chip in enumerate(_other_chips(x, y)):
                yield j * 3 + kc, cins[i].at[2 * chip[0] + chip[1], part], couts[i].at[kc, part], (*chip, c)

    def start(cins, couts, ssem, rsem):
        for s, src, dst, dev in copies(cins, couts):
            _rcopy(src, dst, ssem.at[s], rsem.at[s], dev).start()

    def finish(cins, couts, ssem, rsem):
        for s, src, dst, dev in copies(cins, couts):
            _rcopy(src, dst, ssem.at[s], rsem.at[s], dev).wait_recv()
        for s, src, dst, dev in copies(cins, couts):
            _rcopy(src, dst, ssem.at[s], rsem.at[s], dev).wait_send()

    return _Comm(list(a4s) + list(b2s), _like(b2s), {nw + i: i for i in range(nw)}, 3 * len(pieces), start, finish)


def _landing(a4s):
    return [lax.empty((3,) + a.shape[1:], a.dtype) for a in a4s]


def _share_halves(rs):
    nw = len(rs)

    def halves(couts, i, hf):
        rows = rs[i].shape[0] // 2
        return couts[i].at[pl.ds(hf * rows, rows)]

    def start(cins, couts, ssem, rsem):
        x, y, c = _mesh_pos()
        for i in range(nw):
            _rcopy(halves(couts, i, c), halves(couts, i, c), ssem.at[i], rsem.at[i], (x, y, 1 - c)).start()

    def finish(cins, couts, ssem, rsem):
        x, y, c = _mesh_pos()
        for i in range(nw):
            _rcopy(halves(couts, i, 1 - c), halves(couts, i, 1 - c), ssem.at[i], rsem.at[i], (x, y, 1 - c)).wait_recv()
        for i in range(nw):
            _rcopy(halves(couts, i, c), halves(couts, i, c), ssem.at[i], rsem.at[i], (x, y, 1 - c)).wait_send()

    return _Comm(rs, _like(rs), {i: i for i in range(nw)}, nw, start, finish)


ADD_ROWS = 256


def _add_pair(g4, b1, qc_idx, name):
    _, half, cols = b1.shape
    rb = min(ADD_ROWS, half)
    nb = half // rb

    def body(qc_ref, g_ref, b_ref, o_ref, ob_ref):
        a = g_ref[...] + b_ref[...]
        o_ref[...] = a
        ob_ref[...] = a.astype(bf16)

    blk = (1, rb, cols)
    out = pl.BlockSpec(blk, lambda d, i, qc: (d, i, 0))
    return pl.pallas_call(
        body, name=name,
        grid_spec=pltpu.PrefetchScalarGridSpec(
            num_scalar_prefetch=1, grid=(N_CHIPS, nb),
            in_specs=[pl.BlockSpec(blk, lambda d, i, qc: (d, qc[1] * nb + i, 0)), out],
            out_specs=(out, out)),
        out_shape=(jax.ShapeDtypeStruct(b1.shape, f32), jax.ShapeDtypeStruct(b1.shape, bf16)),
        compiler_params=_params(("parallel", "parallel")),
    )(qc_idx, g4, b1)


def _add_chips(a4, b2, qc_idx, name):
    _, half, cols = a4.shape
    rb = min(ADD_ROWS, half)
    nb = half // rb

    def body(qc_ref, a_ref, b_ref, o_ref):
        o_ref[...] = ((a_ref[0] + b_ref[0].astype(f32)) + b_ref[1].astype(f32)) + b_ref[2].astype(f32)

    return pl.pallas_call(
        body, name=name,
        grid_spec=pltpu.PrefetchScalarGridSpec(
            num_scalar_prefetch=1, grid=(nb,),
            in_specs=[pl.BlockSpec((1, rb, cols), lambda i, qc: (qc[0], i, 0)), pl.BlockSpec((3, rb, cols), lambda i, qc: (0, i, 0))],
            out_specs=pl.BlockSpec((rb, cols), lambda i, qc: (qc[1] * nb + i, 0))),
        out_shape=jax.ShapeDtypeStruct((2 * half, cols), f32),
        compiler_params=_params(("parallel",)),
    )(qc_idx, a4, b2)


def _adamw_math(w, g, m, v):
    m = ADAM_B1 * m + (1.0 - ADAM_B1) * g
    v = ADAM_B2 * v + (1.0 - ADAM_B2) * (g * g)
    m_hat = m / (1.0 - ADAM_B1 ** ADAM_STEP)
    v_hat = v / (1.0 - ADAM_B2 ** ADAM_STEP)
    return -ADAM_LR * (m_hat / (jnp.sqrt(v_hat) + ADAM_EPS) + ADAM_WD * w), m, v


def _adamw(w, g, m, v, name):
    rows, cols = w.shape
    rb = ADD_ROWS if rows % ADD_ROWS == 0 else rows

    def body(w_ref, g_ref, m_ref, v_ref, go_ref, d_ref, mo_ref, vo_ref):
        g = g_ref[...]
        go_ref[...] = g
        d_ref[...], mo_ref[...], vo_ref[...] = _adamw_math(w_ref[...], g, m_ref[...], v_ref[...])

    blk = pl.BlockSpec((rb, cols), lambda i: (i, 0))
    return pl.pallas_call(
        body, name=name, grid=(rows // rb,), in_specs=[blk] * 4, out_specs=(blk,) * 4,
        out_shape=(jax.ShapeDtypeStruct(w.shape, f32),) * 4, compiler_params=_params(("parallel",)),
    )(w, g, m, v)


def _small_sum_adamw(all_pkts, w, m, v):
    def body(a_ref, w_ref, m_ref, v_ref, g_ref, d_ref, mo_ref, vo_ref):
        g = a_ref[0]
        for r in range(1, 8):
            g = g + a_ref[r]
        g_ref[...] = g
        d_ref[...], mo_ref[...], vo_ref[...] = _adamw_math(w_ref[...], g, m_ref[...], v_ref[...])

    return pl.pallas_call(body, name="small_sum_adamw", out_shape=(jax.ShapeDtypeStruct(w.shape, f32),) * 4)(all_pkts, w, m, v)


SHARDED = (("w_in", (D_MODEL, D_IN // N_CHIPS), 2), ("w_out", (D_MODEL // N_CHIPS, D_MODEL), 1), ("w_up", (D_MODEL, D_FF // N_CHIPS), 2),
           ("w_ple_gate", (D_MODEL // N_CHIPS, D_MODEL), 1), ("w_ple", (D_PLE, D_MODEL // N_CHIPS), 1),
           ("w_down", (D_FF // N_CHIPS, D_MODEL), 2))
SMALL_LAYOUT = (("ln_in_g", 0, 1024), ("ln_in_b", 8, 1024), ("ln1_g", 16, 1024), ("ln1_b", 24, 1024), ("b_ple_gate", 32, 1024),
                ("ln2_g", 40, 1024), ("ln2_b", 48, 1024), ("gdn_norm_g", 56, 128), ("fox_norm_g", 57, 64), ("a_log", 58, 4),
                ("dt_bias", 59, 4), ("b_f", 60, 8), ("loss", 61, 1))
SMALL_CONV_ROW = 64
SMALL_ROWS = 128


def _pack_small(vals, conv=None):
    rows = []
    nxt = 0
    for n, r0, size in SMALL_LAYOUT:
        assert r0 == nxt
        v = vals[n].reshape(-1).astype(f32) if n in vals else jnp.zeros((size,), f32)
        nrows = -(-size // LANES)
        rows.append(jnp.pad(v, (0, nrows * LANES - size)).reshape(nrows, LANES))
        nxt = r0 + nrows
    rows.append(jnp.zeros((SMALL_CONV_ROW - nxt, LANES), f32))
    conv_rows = CONV_W * GDN_QKV // LANES
    rows.append(jnp.zeros((conv_rows, LANES), f32) if conv is None else conv.reshape(conv_rows, LANES))
    rows.append(jnp.zeros((SMALL_ROWS - SMALL_CONV_ROW - conv_rows, LANES), f32))
    return jnp.concatenate(rows, axis=0)


def _unpack_small(pkt, shapes):
    out = {}
    for n, r0, size in SMALL_LAYOUT:
        if n in shapes:
            nrows = -(-size // LANES)
            out[n] = pkt[r0:r0 + nrows].reshape(-1)[:size].reshape(shapes[n])
    return out


WEIGHTS = ("ln_in_g", "ln_in_b", "w_in", "conv_w", "a_log", "dt_bias", "gdn_norm_g", "b_f", "fox_norm_g", "w_out", "ln1_g", "ln1_b",
           "w_up", "w_down", "w_ple", "w_ple_gate", "b_ple_gate", "ln2_g", "ln2_b")
SMALL_NAMES = tuple(n for n, _, _ in SMALL_LAYOUT if n != "loss")


def kernel(x, p, ln_in_g, ln_in_b, w_in, conv_w, a_log, dt_bias, gdn_norm_g, b_f, fox_norm_g, w_out, ln1_g, ln1_b, w_up, w_down, w_ple, w_ple_gate, b_ple_gate, ln2_g, ln2_b, loss_target, m_ln_in_g, m_ln_in_b, m_w_in, m_conv_w, m_a_log, m_dt_bias, m_gdn_norm_g, m_b_f, m_fox_norm_g, m_w_out, m_ln1_g, m_ln1_b, m_w_up, m_w_down, m_w_ple, m_w_ple_gate, m_b_ple_gate, m_ln2_g, m_ln2_b, v_ln_in_g, v_ln_in_b, v_w_in, v_conv_w, v_a_log, v_dt_bias, v_gdn_norm_g, v_b_f, v_fox_norm_g, v_w_out, v_ln1_g, v_ln1_b, v_w_up, v_w_down, v_w_ple, v_w_ple_gate, v_b_ple_gate, v_ln2_g, v_ln2_b):
    given = dict(locals())
    w = {n: given[n] for n in WEIGHTS}
    m = {n: given["m_" + n] for n in WEIGHTS}
    v = {n: given["v_" + n] for n in WEIGHTS}
    xi, yi, ci = _mesh_pos()
    q = 2 * xi + yi

    def slot_buffer(val, dtype, slots=N_CHIPS, slot=q):
        return lax.dynamic_update_slice(lax.empty((slots,) + val.shape, dtype), val.astype(dtype)[None], (slot, 0, 0))

    conv_rows = CONV_W * GDN_QKV // N_CHIPS // LANES
    conv_pkt = jnp.pad(w["conv_w"][0].reshape(-1, LANES), ((0, CONV_PKT_ROWS - conv_rows), (0, 0)))
    (w_in4, conv_all), = _comm_only([_gather_now([slot_buffer(w["w_in"][0], bf16)], [slot_buffer(conv_pkt, f32)])], "gather_w_in")
    conv_full = jnp.concatenate([conv_all[d, :conv_rows].reshape(CONV_W, GDN_QKV // N_CHIPS) for d in range(N_CHIPS)], axis=1)
    wi = jnp.concatenate([w_in4[d] for d in range(N_CHIPS)], axis=1)
    w_cat = jnp.concatenate([wi[:, :OFF_BETA], wi[:, OFF_FOX:OFF_F], wi[:, OFF_BETA:OFF_FOX], wi[:, OFF_F:],
                             jnp.zeros((D_MODEL, D_CAT - D_IN), bf16)], axis=1)

    small = {n: w[n] for n in SMALL_NAMES}
    qc = jnp.stack([q, ci]).astype(jnp.int32)
    grad_x, gc, g_conv, g_late, small_g = _device_grads(x[0], p[0, 0], loss_target[0], small, w_cat, conv_full,
                                                        [slot_buffer(w[n][0], bf16) for n in LATE], qc)

    g_in = jnp.concatenate([gc[:, :OFF_BETA], gc[:, SEG_SMALL:SEG_SMALL + 8], gc[:, SEG_FOX:SEG_SMALL],
                            gc[:, SEG_SMALL + 8:SEG_SMALL + 16]], axis=1)
    shard_cols = D_IN // N_CHIPS
    g_in4 = jnp.stack([g_in[:, d * shard_cols:(d + 1) * shard_cols] for d in range(N_CHIPS)])
    (b1,), (small_all,) = _comm_only(
        [_exchange_pairs([g_in4]), _gather_packets(slot_buffer(_pack_small(small_g, g_conv), f32, 8, 4 * xi + 2 * yi + ci))], "exchange_w_in")
    a4, a4b = _add_pair(g_in4, b1, qc, "add_pair_w_in")
    (b2,), = _comm_only([_exchange_chips([a4b], _landing([a4b]))], "exchange_chips_w_in")
    (g_in_red,), = _comm_only([_share_halves([_add_chips(a4, b2, qc, "add_chips_w_in")])], "share_w_in")
    g_late["w_in"] = g_in_red

    grads, delta, new_m, new_v = {}, {}, {}, {}
    for n, g in g_late.items():
        outs = _adamw(w[n][0], g.reshape(w[n].shape[1:]), m[n][0], v[n][0], "adamw_" + n)
        grads[n], delta[n], new_m[n], new_v[n] = (a.reshape(w[n].shape) for a in outs)
    shapes = {n: w[n].shape for n in SMALL_NAMES}
    g_pkt, d_pkt, m_pkt, v_pkt = _small_sum_adamw(small_all, _pack_small(w), _pack_small(m), _pack_small(v))
    for dst, pkt in ((grads, g_pkt), (delta, d_pkt), (new_m, m_pkt), (new_v, v_pkt)):
        dst.update(_unpack_small(pkt, shapes))
    conv_rows_all = CONV_W * GDN_QKV // LANES
    conv_g_full = g_pkt[SMALL_CONV_ROW:SMALL_CONV_ROW + conv_rows_all].reshape(CONV_W, GDN_QKV)
    conv_g = lax.dynamic_slice_in_dim(conv_g_full, q * (GDN_QKV // N_CHIPS), GDN_QKV // N_CHIPS, axis=1)
    outs = _adamw(w["conv_w"][0], conv_g, m["conv_w"][0], v["conv_w"][0], "adamw_conv_w")
    grads["conv_w"], delta["conv_w"], new_m["conv_w"], new_v["conv_w"] = (a.reshape(w["conv_w"].shape) for a in outs)
    loss = g_pkt[61, 0]
    return (loss, grad_x[None], *[grads[n] for n in WEIGHTS], *[delta[n] for n in WEIGHTS],
            *[new_m[n] for n in WEIGHTS], *[new_v[n] for n in WEIGHTS])
```

```python
import functools

import jax
import jax.numpy as jnp
from jax import lax
from jax.experimental import pallas as pl
from jax.experimental.pallas import tpu as pltpu

f32 = jnp.float32
bf16 = jnp.bfloat16
HI = lax.Precision.HIGHEST
MESH = pl.DeviceIdType.MESH

D_MODEL = 1024
CHUNK = 64
GDN_HEADS = 4
GDN_DK = 128
FOX_HEADS = 8
FOX_DH = 64
CONV_W = 4
D_FF = 4096
D_PLE = 256
LN_EPS = 1e-5
NORM_EPS = 1e-6
ALPHA = 2.0 ** 0.25
GDN_QKV = 1536
OFF_Z = 1536
OFF_BETA = 2048
OFF_FOX = 2056
OFF_F = 3592
D_IN = 3600
ADAM_LR = 0.001
ADAM_B1 = 0.9
ADAM_B2 = 0.999
ADAM_EPS = 1e-08
ADAM_WD = 0.01
ADAM_STEP = 10

SEG_FOX = 2048
SEG_SMALL = 3584
D_CAT = 3840
LANES = 128
TOK_BLK = 256
FOX_BQ = 256
VMEM_LIMIT = 56 * 1024 * 1024
NEG = -1e30

N_CHIPS = 4


def _params(sem=None, **kw):
    return pltpu.CompilerParams(dimension_semantics=sem, vmem_limit_bytes=VMEM_LIMIT, **kw)


def _sigmoid(x):
    return 1.0 / (1.0 + jnp.exp(-x))


def _softplus(x):
    return jnp.maximum(x, 0.0) + jnp.log(1.0 + jnp.exp(-jnp.abs(x)))


def _ln_fwd(x, g, b):
    mu = jnp.mean(x, -1, keepdims=True)
    xc = x - mu
    var = jnp.mean(xc * xc, -1, keepdims=True)
    rstd = lax.rsqrt(var + LN_EPS)
    xhat = xc * rstd
    return xhat * g + b, xhat, rstd


def _ln_bwd(dy, xhat, rstd, g):
    dxh = dy * g
    m1 = jnp.mean(dxh, -1, keepdims=True)
    m2 = jnp.mean(dxh * xhat, -1, keepdims=True)
    return rstd * (dxh - m1 - xhat * m2)


def _dot(a, b, prec=HI):
    return jnp.dot(a, b, precision=prec, preferred_element_type=f32)


def _dot_nt(a, b, prec=HI):
    return lax.dot_general(a, b, (((1,), (1,)), ((), ())), precision=prec, preferred_element_type=f32)


def _dot_tn(a, b, prec=HI):
    return lax.dot_general(a, b, (((0,), (0,)), ((), ())), precision=prec, preferred_element_type=f32)


def _bdot(a, b):
    return _dot(a.astype(bf16), b.astype(bf16), None)


def _bdot_nt(a, b):
    return _dot_nt(a.astype(bf16), b.astype(bf16), None)


def _bdot_tn(a, b):
    return _dot_tn(a.astype(bf16), b.astype(bf16), None)


def _lane(shape):
    return lax.broadcasted_iota(jnp.int32, shape, len(shape) - 1)


def _mm(a, b, mode, tm, tn, name, out_dtype=f32, epi=None, extra=None, shards=1, comm=None):
    if mode == "nn":
        (m, k), n = a.shape, b.shape[-1] * shards
    elif mode == "nt":
        (m, k), n = a.shape, b.shape[-2]
    else:
        (k, m), n = a.shape, b.shape[1]
    assert m % tm == 0 and n % tn == 0, (name, m, n, tm, tn)
    per = (n // shards) // tn
    assert mode == "nt" or per * tn * shards == n, (name, n, tn, shards)
    nc = 512 if tn % 512 == 0 else (256 if tn % 256 == 0 else 128)
    ks = k // shards

    def body(a_ref, b_ref, *rest):
        for n0 in range(0, tn, nc):
            if mode == "nn":
                acc = jnp.dot(a_ref[...], b_ref[:, n0:n0 + nc], preferred_element_type=f32)
            elif mode == "nt" and shards > 1:
                acc = jnp.zeros((tm, nc), f32)
                for d in range(shards):
                    acc = acc + lax.dot_general(a_ref[:, d * ks:(d + 1) * ks], b_ref[d, n0:n0 + nc, :], (((1,), (1,)), ((), ())),
                                                preferred_element_type=f32)
            elif mode == "nt":
                acc = lax.dot_general(a_ref[...], b_ref[n0:n0 + nc, :], (((1,), (1,)), ((), ())), preferred_element_type=f32)
            else:
                acc = lax.dot_general(a_ref[...], b_ref[:, n0:n0 + nc], (((0,), (0,)), ((), ())), preferred_element_type=f32)
            if epi == "relu2":
                up_ref, act_ref = rest
                up_ref[:, n0:n0 + nc] = acc
                r = jnp.maximum(acc, 0.0)
                act_ref[:, n0:n0 + nc] = (r * r).astype(bf16)
            elif epi == "relu2_bwd":
                up_ref, o_ref = rest
                o_ref[:, n0:n0 + nc] = (acc * (2.0 * jnp.maximum(up_ref[:, n0:n0 + nc], 0.0))).astype(bf16)
            else:
                (o_ref,) = rest
                o_ref[:, n0:n0 + nc] = acc.astype(out_dtype)

    if mode == "tn":
        a_spec = pl.BlockSpec((k, tm), lambda j, i: (0, i))
    else:
        a_spec = pl.BlockSpec((tm, k), lambda j, i: (i, 0))
    if mode == "nt" and shards > 1:
        b_spec = pl.BlockSpec((shards, tn, ks), lambda j, i: (0, j, 0))
    elif mode == "nt":
        b_spec = pl.BlockSpec((tn, k), lambda j, i: (j, 0))
    elif mode == "nn" and shards > 1:
        b_spec = pl.BlockSpec((None, k, tn), lambda j, i: (j // per, 0, j % per))
    else:
        b_spec = pl.BlockSpec((k, tn), lambda j, i: (0, j))
    o_spec = pl.BlockSpec((tm, tn), lambda j, i: (i, j))
    in_specs = [a_spec, b_spec]
    args = [a, b]
    if epi == "relu2":
        out_shape = (jax.ShapeDtypeStruct((m, n), f32), jax.ShapeDtypeStruct((m, n), bf16))
        out_specs = (o_spec, o_spec)
    elif epi == "relu2_bwd":
        in_specs.append(o_spec)
        args.append(extra)
        out_shape = jax.ShapeDtypeStruct((m, n), bf16)
        out_specs = o_spec
    elif mode == "tn" and shards > 1:
        out_shape = jax.ShapeDtypeStruct((shards, m, n // shards), out_dtype)
        out_specs = pl.BlockSpec((None, tm, tn), lambda j, i: (j // per, i, j % per))
    else:
        out_shape = jax.ShapeDtypeStruct((m, n), out_dtype)
        out_specs = o_spec
    single = not isinstance(out_shape, tuple)
    res, moved = _hosted(body, comm, name=name, grid=(n // tn, m // tm), in_specs=in_specs,
                         out_specs=(out_specs,) if single else out_specs, out_shape=(out_shape,) if single else out_shape, args=args)
    res = res[0] if single else res
    return res if comm is None else (res, moved)


def _row_spec(width, col=0):
    return pl.BlockSpec((TOK_BLK, width), lambda i: (i, col))


def _vec_spec(rows, width):
    return pl.BlockSpec((rows, width), lambda i: (0, 0))


def _ln_in(x, g, b):
    t, d = x.shape

    def body(x_ref, g_ref, b_ref, h_ref, hb_ref):
        h, _, _ = _ln_fwd(x_ref[...], g_ref[...], b_ref[...])
        h_ref[...] = h
        hb_ref[...] = h.astype(bf16)

    return pl.pallas_call(
        body, name="ln_in", grid=(t // TOK_BLK,),
        in_specs=[_row_spec(d), _vec_spec(1, d), _vec_spec(1, d)],
        out_specs=(_row_spec(d), _row_spec(d)),
        out_shape=(jax.ShapeDtypeStruct((t, d), f32), jax.ShapeDtypeStruct((t, d), bf16)),
        compiler_params=_params(("parallel",)),
    )(x, g, b)


def _attn_post(o_gdn, proj, o_fox, g_gdn, g_fox2, comm=None):
    t = o_gdn.shape[0]

    def body(og_ref, z_ref, of_ref, gg_ref, gf_ref, out_ref):
        for h in range(GDN_HEADS):
            sl = slice(h * LANES, (h + 1) * LANES)
            og = og_ref[:, sl]
            z = z_ref[:, sl]
            r = lax.rsqrt(jnp.mean(og * og, -1, keepdims=True) + NORM_EPS)
            out_ref[:, sl] = (og * r * gg_ref[...] * (z * _sigmoid(z))).astype(bf16)
        lo = _lane((TOK_BLK, LANES)) < FOX_DH
        for pr in range(FOX_HEADS // 2):
            sl = slice(pr * LANES, (pr + 1) * LANES)
            of = of_ref[:, sl]
            sq = of * of
            s0 = jnp.sum(jnp.where(lo, sq, 0.0), -1, keepdims=True)
            s1 = jnp.sum(jnp.where(lo, 0.0, sq), -1, keepdims=True)
            r = lax.rsqrt(jnp.where(lo, s0, s1) * (1.0 / FOX_DH) + NORM_EPS)
            out_ref[:, 512 + pr * LANES:512 + (pr + 1) * LANES] = (of * r * gf_ref[...]).astype(bf16)

    return _hosted(
        body, comm, name="attn_post", grid=(t // TOK_BLK,),
        in_specs=[_row_spec(512), _row_spec(512, OFF_Z // 512), _row_spec(512), _vec_spec(1, LANES), _vec_spec(1, LANES)],
        out_specs=(_row_spec(D_MODEL),),
        out_shape=(jax.ShapeDtypeStruct((t, D_MODEL), bf16),),
        args=(o_gdn, proj, o_fox, g_gdn, g_fox2))


def _attn_post_bwd(dattn, o_gdn, proj, o_fox, g_gdn, g_fox2):
    t = o_gdn.shape[0]

    def body(da_ref, og_ref, z_ref, of_ref, gg_ref, gf_ref, dog_ref, dz_ref, dof_ref, pg_ref):
        i = pl.program_id(0)

        @pl.when(i == 0)
        def _():
            pg_ref[...] = jnp.zeros_like(pg_ref)

        dgg = jnp.zeros((1, LANES), f32)
        for h in range(GDN_HEADS):
            sl = slice(h * LANES, (h + 1) * LANES)
            og = og_ref[:, sl]
            z = z_ref[:, sl]
            dout = da_ref[:, sl]
            g = gg_ref[...]
            r = lax.rsqrt(jnp.mean(og * og, -1, keepdims=True) + NORM_EPS)
            sg = _sigmoid(z)
            silu = z * sg
            ng = og * r * g
            dng = dout * silu
            dz_ref[:, sl] = (dout * ng * (sg * (1.0 + z * (1.0 - sg)))).astype(bf16)
            dgg = dgg + jnp.sum(dng * og * r, 0, keepdims=True)
            gd = dng * g
            dog_ref[:, sl] = r * gd - og * (r * r * r) * jnp.mean(og * gd, -1, keepdims=True)
        pg_ref[0:1, :] += dgg
        lo = _lane((TOK_BLK, LANES)) < FOX_DH
        dgf = jnp.zeros((1, LANES), f32)
        for pr in range(FOX_HEADS // 2):
            sl = slice(pr * LANES, (pr + 1) * LANES)
            of = of_ref[:, sl]
            dout = da_ref[:, 512 + pr * LANES:512 + (pr + 1) * LANES]
            g = gf_ref[...]
            sq = of * of
            s0 = jnp.sum(jnp.where(lo, sq, 0.0), -1, keepdims=True)
            s1 = jnp.sum(jnp.where(lo, 0.0, sq), -1, keepdims=True)
            r = lax.rsqrt(jnp.where(lo, s0, s1) * (1.0 / FOX_DH) + NORM_EPS)
            dgf = dgf + jnp.sum(dout * of * r, 0, keepdims=True)
            gd = dout * g
            xg = of * gd
            m0 = jnp.sum(jnp.where(lo, xg, 0.0), -1, keepdims=True)
            m1 = jnp.sum(jnp.where(lo, 0.0, xg), -1, keepdims=True)
            dof_ref[:, sl] = r * gd - of * (r * r * r) * (jnp.where(lo, m0, m1) * (1.0 / FOX_DH))
        pg_ref[1:2, :] += dgf

    return pl.pallas_call(
        body, name="attn_post_bwd", grid=(t // TOK_BLK,),
        in_specs=[_row_spec(D_MODEL), _row_spec(512), _row_spec(512, OFF_Z // 512), _row_spec(512), _vec_spec(1, LANES), _vec_spec(1, LANES)],
        out_specs=(_row_spec(512), _row_spec(512), _row_spec(512), _vec_spec(8, LANES)),
        out_shape=(jax.ShapeDtypeStruct((t, 512), f32), jax.ShapeDtypeStruct((t, 512), bf16),
                   jax.ShapeDtypeStruct((t, 512), f32), jax.ShapeDtypeStruct((8, LANES), f32)),
        compiler_params=_params(("arbitrary",)),
    )(dattn, o_gdn, proj, o_fox, g_gdn, g_fox2)


def _ln1(h0, mix, g, b):
    t, d = h0.shape

    def body(h0_ref, mix_ref, g_ref, b_ref, h_ref, hb_ref, xh_ref, rs_ref):
        h, xhat, rstd = _ln_fwd(ALPHA * h0_ref[...] + mix_ref[...], g_ref[...], b_ref[...])
        h_ref[...] = h
        hb_ref[...] = h.astype(bf16)
        xh_ref[...] = xhat
        rs_ref[...] = jnp.broadcast_to(rstd, rs_ref.shape)

    return pl.pallas_call(
        body, name="ln1", grid=(t // TOK_BLK,),
        in_specs=[_row_spec(d), _row_spec(d), _vec_spec(1, d), _vec_spec(1, d)],
        out_specs=(_row_spec(d), _row_spec(d), _row_spec(d), _row_spec(LANES)),
        out_shape=(jax.ShapeDtypeStruct((t, d), f32), jax.ShapeDtypeStruct((t, d), bf16),
                   jax.ShapeDtypeStruct((t, d), f32), jax.ShapeDtypeStruct((t, LANES), f32)),
        compiler_params=_params(("parallel",)),
    )(h0, mix, g, b)


def _ln2_loss(h1, ff, pe, gp, b_gate, g, b, target):
    t, d = h1.shape

    def body(h1_ref, ff_ref, pe_ref, gp_ref, bg_ref, g_ref, b_ref, t_ref, dr_ref, drb_ref, dpe_ref, dgp_ref, pg_ref):
        i = pl.program_id(0)

        @pl.when(i == 0)
        def _():
            pg_ref[...] = jnp.zeros_like(pg_ref)

        sig = _sigmoid(gp_ref[...] + bg_ref[...])
        pe = pe_ref[...]
        r2 = ALPHA * h1_ref[...] + ff_ref[...] + pe * sig
        y, xhat, rstd = _ln_fwd(r2, g_ref[...], b_ref[...])
        err = y - t_ref[...]
        dy = err * (1.0 / d)
        dr = _ln_bwd(dy, xhat, rstd, g_ref[...])
        dr_ref[...] = dr
        drb_ref[...] = dr.astype(bf16)
        dpe_ref[...] = (dr * sig).astype(bf16)
        dgp = dr * pe * sig * (1.0 - sig)
        dgp_ref[...] = dgp.astype(bf16)
        pg_ref[0:1, :] += jnp.sum(dy * xhat, 0, keepdims=True)
        pg_ref[1:2, :] += jnp.sum(dy, 0, keepdims=True)
        pg_ref[2:3, :] += jnp.sum(dgp, 0, keepdims=True)
        pg_ref[3:4, :] += 0.5 * jnp.sum(jnp.mean(err * err, -1, keepdims=True), 0, keepdims=True)

    return pl.pallas_call(
        body, name="ln2_loss", grid=(t // TOK_BLK,),
        in_specs=[_row_spec(d)] * 4 + [_vec_spec(1, d)] * 3 + [_row_spec(d)],
        out_specs=(_row_spec(d), _row_spec(d), _row_spec(d), _row_spec(d), _vec_spec(8, d)),
        out_shape=(jax.ShapeDtypeStruct((t, d), f32), jax.ShapeDtypeStruct((t, d), bf16), jax.ShapeDtypeStruct((t, d), bf16),
                   jax.ShapeDtypeStruct((t, d), bf16), jax.ShapeDtypeStruct((8, d), f32)),
        compiler_params=_params(("arbitrary",)),
    )(h1, ff, pe, gp, b_gate, g, b, target)


def _ln1_bwd(dr2, da, db, xhat, rstd, g):
    t, d = dr2.shape

    def body(dr2_ref, da_ref, db_ref, xh_ref, rs_ref, g_ref, dr_ref, drb_ref, pg_ref):
        i = pl.program_id(0)

        @pl.when(i == 0)
        def _():
            pg_ref[...] = jnp.zeros_like(pg_ref)

        dh = ALPHA * dr2_ref[...] + da_ref[...] + db_ref[...]
        xhat = xh_ref[...]
        dr = _ln_bwd(dh, xhat, rs_ref[:, 0:1], g_ref[...])
        dr_ref[...] = dr
        drb_ref[...] = dr.astype(bf16)
        pg_ref[0:1, :] += jnp.sum(dh * xhat, 0, keepdims=True)
        pg_ref[1:2, :] += jnp.sum(dh, 0, keepdims=True)

    return pl.pallas_call(
        body, name="ln1_bwd", grid=(t // TOK_BLK,),
        in_specs=[_row_spec(d)] * 4 + [_row_spec(LANES), _vec_spec(1, d)],
        out_specs=(_row_spec(d), _row_spec(d), _vec_spec(8, d)),
        out_shape=(jax.ShapeDtypeStruct((t, d), f32), jax.ShapeDtypeStruct((t, d), bf16), jax.ShapeDtypeStruct((8, d), f32)),
        compiler_params=_params(("arbitrary",)),
    )(dr2, da, db, xhat, rstd, g)


def _ln_in_bwd(x, dr1, dmm, g, comm=None):
    t, d = x.shape

    def body(x_ref, dr1_ref, dmm_ref, g_ref, dx_ref, pg_ref):
        i = pl.program_id(0)

        @pl.when(i == 0)
        def _():
            pg_ref[...] = jnp.zeros_like(pg_ref)

        dh = ALPHA * dr1_ref[...] + dmm_ref[...]
        _, xhat, rstd = _ln_fwd(x_ref[...], g_ref[...], 0.0)
        dx_ref[...] = _ln_bwd(dh, xhat, rstd, g_ref[...])
        pg_ref[0:1, :] += jnp.sum(dh * xhat, 0, keepdims=True)
        pg_ref[1:2, :] += jnp.sum(dh, 0, keepdims=True)

    return _hosted(
        body, comm, name="ln_in_bwd", grid=(t // TOK_BLK,),
        in_specs=[_row_spec(d)] * 3 + [_vec_spec(1, d)],
        out_specs=(_row_spec(d), _vec_spec(8, d)),
        out_shape=(jax.ShapeDtypeStruct((t, d), f32), jax.ShapeDtypeStruct((8, d), f32)),
        args=(x, dr1, dmm, g))


def _tri(n, upper=False, strict=False):
    r = lax.broadcasted_iota(jnp.int32, (n, n), 0)
    c = lax.broadcasted_iota(jnp.int32, (n, n), 1)
    if upper:
        m = (c > r) if strict else (c >= r)
    else:
        m = (c < r) if strict else (c <= r)
    return jnp.where(m, 1.0, 0.0).astype(f32)


def _gate_values(x, bias, alog, lane):
    z = x + bias
    return jnp.where(lane < 4, _sigmoid(z), jnp.where(lane < 8, -jnp.exp(alog) * _softplus(z), jnp.where(lane < 16, -_softplus(-z), 0.0)))


def _gates(proj, bias_row, alog_row):
    t = proj.shape[0]
    nch = t // CHUNK

    def body(x_ref, bias_ref, alog_ref, gates_ref, gcum_ref, gcumt_ref):
        lane = _lane((t, LANES))
        gates = _gate_values(x_ref[...], bias_ref[...], alog_ref[...], lane)
        gates_ref[...] = gates
        g3 = gates.reshape(nch, CHUNK, LANES)
        tri = jnp.broadcast_to(_tri(CHUNK)[None], (nch, CHUNK, CHUNK))
        loc = jnp.einsum("bij,bjk->bik", tri, g3, precision=HI, preferred_element_type=f32)
        tot = jnp.sum(g3, axis=1)
        offs = _dot(_tri(nch, strict=True), tot)
        glob = loc + offs[:, None, :]
        lane3 = _lane((nch, CHUNK, LANES))
        gcum = jnp.where(lane3 < 4, g3, jnp.where(lane3 < 8, loc, glob)).reshape(t, LANES)
        gcum_ref[...] = gcum
        gcumt_ref[...] = gcum.T

    return pl.pallas_call(
        body, name="gates", grid=(1,),
        in_specs=[pl.BlockSpec((t, LANES), lambda i: (0, SEG_SMALL // LANES)), _vec_spec(1, LANES), _vec_spec(1, LANES)],
        out_specs=(pl.BlockSpec((t, LANES), lambda i: (0, 0)), pl.BlockSpec((t, LANES), lambda i: (0, 0)),
                   pl.BlockSpec((LANES, t), lambda i: (0, 0))),
        out_shape=(jax.ShapeDtypeStruct((t, LANES), f32), jax.ShapeDtypeStruct((t, LANES), f32), jax.ShapeDtypeStruct((LANES, t), f32)),
        compiler_params=_params(("arbitrary",)),
    )(proj, bias_row, alog_row)


def _gates_bwd(proj, bias_row, alog_row, gates, dgates, dccol, dct):
    t = proj.shape[0]
    nch = t // CHUNK

    def body(x_ref, bias_ref, alog_ref, gates_ref, dg_ref, dcc_ref, dct_ref, dx_ref, pg_ref):
        lane = _lane((t, LANES))
        d = dg_ref[...] + dcc_ref[...] + dct_ref[...].T
        d3 = d.reshape(nch, CHUNK, LANES)
        tri = jnp.broadcast_to(_tri(CHUNK, upper=True)[None], (nch, CHUNK, CHUNK))
        loc = jnp.einsum("bij,bjk->bik", tri, d3, precision=HI, preferred_element_type=f32)
        tot = jnp.sum(d3, axis=1)
        offs = _dot(_tri(nch, upper=True, strict=True), tot)
        glob = loc + offs[:, None, :]
        lane3 = _lane((nch, CHUNK, LANES))
        dpre = jnp.where(lane3 < 4, d3, jnp.where(lane3 < 8, loc, glob)).reshape(t, LANES)
        z = x_ref[...] + bias_ref[...]
        sg = _sigmoid(z)
        dx = jnp.where(lane < 4, dpre * sg * (1.0 - sg),
                       jnp.where(lane < 8, dpre * (-jnp.exp(alog_ref[...])) * sg, jnp.where(lane < 16, dpre * (1.0 - sg), 0.0)))
        dx_ref[...] = dx.astype(bf16)
        pg_ref[...] = jnp.zeros_like(pg_ref)
        pg_ref[0:1, :] = jnp.sum(dx, 0, keepdims=True)
        pg_ref[1:2, :] = jnp.sum(jnp.where((lane >= 4) & (lane < 8), dpre * gates_ref[...], 0.0), 0, keepdims=True)

    full = pl.BlockSpec((t, LANES), lambda i: (0, 0))
    return pl.pallas_call(
        body, name="gates_bwd", grid=(1,),
        in_specs=[pl.BlockSpec((t, LANES), lambda i: (0, SEG_SMALL // LANES)), _vec_spec(1, LANES), _vec_spec(1, LANES),
                  full, full, full, pl.BlockSpec((LANES, t), lambda i: (0, 0))],
        out_specs=(full, _vec_spec(8, LANES)),
        out_shape=(jax.ShapeDtypeStruct((t, LANES), bf16), jax.ShapeDtypeStruct((8, LANES), f32)),
        compiler_params=_params(("arbitrary",)),
    )(proj, bias_row, alog_row, gates, dgates, dccol, dct)


def _conv_act(u, cw, row, t):
    c = cw[3:4, :] * u
    for jj in range(CONV_W - 1):
        sh = CONV_W - 1 - jj
        c = c + cw[jj:jj + 1, :] * jnp.where(row >= sh, pltpu.roll(u, sh, axis=0), 0.0)
    return c


def _gdn_conv(proj, conv_w, comm=None):
    t = proj.shape[0]
    nblk = GDN_QKV // LANES

    def body(u_ref, cw_ref, c_ref, y_ref):
        j = pl.program_id(0)
        row = lax.broadcasted_iota(jnp.int32, (t, LANES), 0)
        c = _conv_act(u_ref[...], cw_ref[...], row, t)
        c_ref[...] = c
        s = c * _sigmoid(c)
        r = lax.rsqrt(jnp.sum(s * s, -1, keepdims=True) + NORM_EPS)
        scale = jnp.where(j < GDN_HEADS, GDN_DK ** -0.5, 1.0)
        y_ref[...] = jnp.where(j < 2 * GDN_HEADS, s * (r * scale), s)

    blk = pl.BlockSpec((t, LANES), lambda j: (0, j))
    return _hosted(
        body, comm, name="gdn_conv", grid=(nblk,),
        in_specs=[blk, pl.BlockSpec((CONV_W, LANES), lambda j: (0, j))],
        out_specs=(blk, blk),
        out_shape=(jax.ShapeDtypeStruct((t, GDN_QKV), f32), jax.ShapeDtypeStruct((t, GDN_QKV), f32)),
        args=(proj, conv_w))


def _gdn_conv_bwd(proj, conv_w, c, dy, comm=None):
    t = proj.shape[0]
    nblk = GDN_QKV // LANES

    def body(u_ref, cw_ref, c_ref, dy_ref, du_ref, dcw_ref):
        j = pl.program_id(0)
        row = lax.broadcasted_iota(jnp.int32, (t, LANES), 0)
        u = u_ref[...]
        cw = cw_ref[...]
        c = c_ref[...]
        dy = dy_ref[...]
        sg = _sigmoid(c)
        s = c * sg
        r = lax.rsqrt(jnp.sum(s * s, -1, keepdims=True) + NORM_EPS)
        n = s * r
        scale = jnp.where(j < GDN_HEADS, GDN_DK ** -0.5, 1.0)
        dn = dy * scale
        ds = jnp.where(j < 2 * GDN_HEADS, r * (dn - n * jnp.sum(dn * n, -1, keepdims=True)), dy)
        dc = ds * (sg * (1.0 + c * (1.0 - sg)))
        du = cw[3:4, :] * dc
        dcw_ref[...] = jnp.zeros_like(dcw_ref)
        dcw_ref[3:4, :] = jnp.sum(dc * u, 0, keepdims=True)
        for jj in range(CONV_W - 1):
            sh = CONV_W - 1 - jj
            du = du + cw[jj:jj + 1, :] * jnp.where(row < t - sh, pltpu.roll(dc, t - sh, axis=0), 0.0)
            dcw_ref[jj:jj + 1, :] = jnp.sum(dc * jnp.where(row >= sh, pltpu.roll(u, sh, axis=0), 0.0), 0, keepdims=True)
        du_ref[...] = du.astype(bf16)

    blk = pl.BlockSpec((t, LANES), lambda j: (0, j))
    return _hosted(
        body, comm, name="gdn_conv_bwd", grid=(nblk,),
        in_specs=[blk, pl.BlockSpec((CONV_W, LANES), lambda j: (0, j)), blk, blk],
        out_specs=(blk, pl.BlockSpec((8, LANES), lambda j: (0, j))),
        out_shape=(jax.ShapeDtypeStruct((t, GDN_QKV), bf16), jax.ShapeDtypeStruct((8, GDN_QKV), f32)),
        args=(proj, conv_w, c, dy))


def _chunk_masks():
    r = lax.broadcasted_iota(jnp.int32, (CHUNK, CHUNK), 0)
    c = lax.broadcasted_iota(jnp.int32, (CHUNK, CHUNK), 1)
    return r >= c, r > c, r == c


def _col_to_row(col, eye):
    return jnp.sum(jnp.where(eye, col, 0.0), axis=0, keepdims=True)


def _row_to_col(row, eye):
    return jnp.sum(jnp.where(eye, row, 0.0), axis=1, keepdims=True)


NN = (((1,), (0,)), ((), ()))
NT = (((1,), (1,)), ((), ()))
TN = (((0,), (0,)), ((), ()))
GDN_GROUP = 4


def _mx(a, b, dims=NN, passes=1):
    d = lambda p, q: lax.dot_general(p, q, dims, preferred_element_type=f32)
    ah, bh = a.astype(bf16), b.astype(bf16)
    if passes == 1:
        return d(ah, bh)
    al = (a - ah.astype(f32)).astype(bf16)
    bl = (b - bh.astype(f32)).astype(bf16)
    return d(ah, bh) + (d(ah, bl) + d(al, bh))


def _gdn_decay(gam, masks):
    causal, _, eye = masks
    return jnp.exp(jnp.where(causal, gam - _col_to_row(gam, eye), NEG))


def _gdn_local(y, gcum, comm=None):
    t = y.shape[0]
    nch = t // CHUNK
    rows_blk = GDN_GROUP * CHUNK

    def body(y_ref, g_ref, u_ref, w_ref, qk_ref, tinv_ref):
        masks = _chunk_masks()
        _, strict, eye = masks
        ids = [(j, h) for j in range(GDN_GROUP) for h in range(GDN_HEADS)]
        rs = lambda j: slice(j * CHUNK, (j + 1) * CHUNK)
        col = lambda base, h: slice(base + h * LANES, base + (h + 1) * LANES)
        kn = [y_ref[rs(j), col(512, h)] for j, h in ids]
        beta = [g_ref[rs(j), h:h + 1] for j, h in ids]
        gam = [g_ref[rs(j), 4 + h:5 + h] for j, h in ids]
        dec = [_gdn_decay(g, masks) for g in gam]
        x = [-jnp.where(strict, _mx(k, k, NT) * d * b, 0.0) for k, d, b in zip(kn, dec, beta)]
        tinv = [jnp.where(eye, 1.0, 0.0) + a for a in x]
        for _ in range(5):
            x = [_mx(a, a, NN, 3) for a in x]
            tinv = [t_ + _mx(t_, a, NN, 3) for t_, a in zip(tinv, x)]
        for (j, h), t_, k, d, b, g in zip(ids, tinv, kn, dec, beta, gam):
            u_ref[rs(j), col(0, h)] = _mx(t_, b * y_ref[rs(j), col(1024, h)])
            w_ref[rs(j), col(0, h)] = _mx(t_, (b * jnp.exp(g)) * k)
            qk_ref[j, h] = _mx(y_ref[rs(j), col(0, h)], k, NT) * d
            tinv_ref[j, h] = t_

    mat = pl.BlockSpec((GDN_GROUP, GDN_HEADS, CHUNK, CHUNK), lambda n: (n, 0, 0, 0))
    return _hosted(
        body, comm, name="gdn_local", grid=(nch // GDN_GROUP,),
        in_specs=[pl.BlockSpec((rows_blk, GDN_QKV), lambda n: (n, 0)), pl.BlockSpec((rows_blk, LANES), lambda n: (n, 0))],
        out_specs=(pl.BlockSpec((rows_blk, 512), lambda n: (n, 0)), pl.BlockSpec((rows_blk, 512), lambda n: (n, 0)), mat, mat),
        out_shape=(jax.ShapeDtypeStruct((t, 512), f32), jax.ShapeDtypeStruct((t, 512), f32),
                   jax.ShapeDtypeStruct((nch, GDN_HEADS, CHUNK, CHUNK), f32), jax.ShapeDtypeStruct((nch, GDN_HEADS, CHUNK, CHUNK), f32)),
        args=(y, gcum))


def _gdn_fwd(y, gcum, u, w, qk, comm=None):
    t = y.shape[0]
    nch = t // CHUNK

    def body(y_ref, g_ref, u_ref, w_ref, qk_ref, o_ref, sall_ref, s_ref):
        @pl.when(pl.program_id(0) == 0)
        def _():
            s_ref[...] = jnp.zeros_like(s_ref)

        heads = range(GDN_HEADS)
        sl = [slice(h * LANES, (h + 1) * LANES) for h in heads]
        gam = [g_ref[:, 4 + h:5 + h] for h in heads]
        gam_last = [g[CHUNK - 1:CHUNK, :] for g in gam]
        s = [s_ref[h] for h in heads]
        for h in heads:
            sall_ref[0, h] = s[h]
        ws = [_mx(w_ref[:, sl[h]], s[h]) for h in heads]
        qs = [_mx(y_ref[:, sl[h]] * jnp.exp(gam[h]), s[h]) for h in heads]
        vn = [u_ref[:, sl[h]] - ws[h] for h in heads]
        av = [_mx(qk_ref[0, h], vn[h]) for h in heads]
        kv = [_mx(y_ref[:, 512 + h * LANES:512 + (h + 1) * LANES] * jnp.exp(gam_last[h] - gam[h]), vn[h], TN) for h in heads]
        for h in heads:
            o_ref[:, sl[h]] = qs[h] + av[h]
            s_ref[h] = jnp.exp(gam_last[h]) * s[h] + kv[h]

    row = lambda width: pl.BlockSpec((CHUNK, width), lambda n: (n, 0))
    return _hosted(
        body, comm, name="gdn_fwd", grid=(nch,),
        in_specs=[row(GDN_QKV), row(LANES), row(512), row(512), pl.BlockSpec((1, GDN_HEADS, CHUNK, CHUNK), lambda n: (n, 0, 0, 0))],
        out_specs=(row(512), pl.BlockSpec((1, GDN_HEADS, LANES, LANES), lambda n: (n, 0, 0, 0))),
        out_shape=(jax.ShapeDtypeStruct((t, 512), f32), jax.ShapeDtypeStruct((nch, GDN_HEADS, LANES, LANES), f32)),
        scratch_shapes=[pltpu.VMEM((GDN_HEADS, LANES, LANES), f32)],
        args=(y, gcum, u, w, qk))


def _gdn_bwd(y, gcum, u_all, w_all, qk_all, tinv_all, sall, do, comm=None):
    t = y.shape[0]
    nch = t // CHUNK

    def body(y_ref, g_ref, u_ref, w_ref, qk_ref, tinv_ref, sall_ref, do_ref, dy_ref, dg_ref, ds_ref):
        @pl.when(pl.program_id(0) == 0)
        def _():
            ds_ref[...] = jnp.zeros_like(ds_ref)

        masks = _chunk_masks()
        causal, strict, eye = masks
        lane = _lane((CHUNK, LANES))
        row = lax.broadcasted_iota(jnp.int32, (CHUNK, 1), 0)
        heads = range(GDN_HEADS)
        each = lambda f, *ls: [f(*a) for a in zip(*ls)]
        rsum = lambda a: jnp.sum(a, axis=1, keepdims=True)
        sl = [slice(h * LANES, (h + 1) * LANES) for h in heads]
        qn = [y_ref[:, sl[h]] for h in heads]
        kn = [y_ref[:, 512 + h * LANES:512 + (h + 1) * LANES] for h in heads]
        v = [y_ref[:, 1024 + h * LANES:1024 + (h + 1) * LANES] for h in heads]
        beta = [g_ref[:, h:h + 1] for h in heads]
        gam = [g_ref[:, 4 + h:5 + h] for h in heads]
        gam_last = [g[CHUNK - 1:CHUNK, :] for g in gam]
        dec = [_gdn_decay(g, masks) for g in gam]
        e = [jnp.exp(g) for g in gam]
        f = each(lambda gl_, g: jnp.exp(gl_ - g), gam_last, gam)
        gl = [jnp.exp(g) for g in gam_last]
        u = [u_ref[:, sl[h]] for h in heads]
        w = [w_ref[:, sl[h]] for h in heads]
        qk = [qk_ref[0, h] for h in heads]
        tinv = [tinv_ref[0, h] for h in heads]
        s = [sall_ref[0, h] for h in heads]
        dsn = [ds_ref[h] for h in heads]
        d_o = [do_ref[:, sl[h]] for h in heads]
        qd = each(lambda a, b: a * b, qn, e)
        kd = each(lambda a, b: a * b, kn, f)
        ws = each(_mx, w, s)
        kds = each(_mx, kd, dsn)
        qkdo = each(lambda a, b: _mx(a, b, TN), qk, d_o)
        dqd = each(lambda a, b: _mx(a, b, NT), d_o, s)
        qddo = each(lambda a, b: _mx(a, b, TN), qd, d_o)
        kkd = each(lambda k, d: _mx(k, k, NT) * d, kn, dec)
        vn = each(lambda a, b: a - b, u, ws)
        dvn = each(lambda a, b: a + b, qkdo, kds)
        dqk = each(lambda a, b: jnp.where(causal, _mx(a, b, NT), 0.0), d_o, vn)
        dkd = each(lambda a, b: _mx(a, b, NT), vn, dsn)
        dw = each(lambda a, b: -_mx(a, b, NT), dvn, s)
        wdvn = each(lambda a, b: _mx(a, b, TN), w, dvn)
        dgl = each(lambda a, b: jnp.sum(rsum(a * b), axis=0, keepdims=True), dsn, s)
        for h in heads:
            ds_ref[h] = qddo[h] - wdvn[h] + gl[h] * dsn[h]
        dru = each(lambda a, b: _mx(a, b, TN), tinv, dvn)
        drw = each(lambda a, b: _mx(a, b, TN), tinv, dw)
        dqkr = each(lambda a, b: a * b, dqk, dec)
        dq1 = each(_mx, dqkr, kn)
        dk1 = each(lambda a, b: _mx(a, b, TN), dqkr, qn)
        dnu = each(lambda a, b: _mx(a, b, NT), dru, u)
        dnw = each(lambda a, b: _mx(a, b, NT), drw, w)
        dn = each(lambda a, b: jnp.where(strict, -(a + b), 0.0), dnu, dnw)
        dkk = each(lambda a, b, d: a * b * d, dn, beta, dec)
        dk2 = each(_mx, dkk, kn)
        dk3 = each(lambda a, b: _mx(a, b, TN), dkk, kn)
        dgates = jnp.zeros((CHUNK, LANES), f32)
        for h in heads:
            drw_k = rsum(drw[h] * kn[h])
            dbeta = rsum(dru[h] * v[h]) + e[h] * drw_k + rsum(dn[h] * kkd[h])
            m = dn[h] * (kkd[h] * beta[h]) + dqk[h] * qk[h]
            de = beta[h] * drw_k + rsum(dqd[h] * qn[h])
            df = rsum(dkd[h] * kn[h])
            dgam = rsum(m) - _row_to_col(jnp.sum(m, axis=0, keepdims=True), eye) + de * e[h] - df * f[h]
            dgam_last = jnp.sum(df * f[h], axis=0, keepdims=True) + dgl[h] * gl[h]
            dgam = dgam + jnp.where(row == CHUNK - 1, dgam_last, 0.0)
            dy_ref[:, sl[h]] = dq1[h] + dqd[h] * e[h]
            dy_ref[:, 512 + h * LANES:512 + (h + 1) * LANES] = (beta[h] * e[h]) * drw[h] + dk2[h] + dk3[h] + dk1[h] + dkd[h] * f[h]
            dy_ref[:, 1024 + h * LANES:1024 + (h + 1) * LANES] = beta[h] * dru[h]
            dgates = dgates + jnp.where(lane == h, dbeta, 0.0) + jnp.where(lane == 4 + h, dgam, 0.0)
        dg_ref[...] = dgates

    rev = lambda width: pl.BlockSpec((CHUNK, width), lambda n: (nch - 1 - n, 0))
    mat = lambda d: pl.BlockSpec((1, GDN_HEADS, d, d), lambda n: (nch - 1 - n, 0, 0, 0))
    return _hosted(
        body, comm, name="gdn_bwd", grid=(nch,),
        in_specs=[rev(GDN_QKV), rev(LANES), rev(512), rev(512), mat(CHUNK), mat(CHUNK), mat(LANES), rev(512)],
        out_specs=(rev(GDN_QKV), rev(LANES)),
        out_shape=(jax.ShapeDtypeStruct((t, GDN_QKV), f32), jax.ShapeDtypeStruct((t, LANES), f32)),
        scratch_shapes=[pltpu.VMEM((GDN_HEADS, LANES, LANES), f32)],
        args=(y, gcum, u_all, w_all, qk_all, tinv_all, sall, do))


FOX_CLASSES = 4


def _fox_groups(t):
    nq = t // FOX_BQ
    ncls = min(FOX_CLASSES, nq)
    per = nq // ncls
    return [(g * per, per, (g + 1) * per * FOX_BQ) for g in range(ncls)]


def _fox_scores(q_ref, k_ref, gcum_ref, gcumt_ref, h, i, keys):
    pr = h // 2
    lo = (h % 2) * FOX_DH
    lane = _lane((FOX_BQ, LANES))
    mask = (lane >= lo) & (lane < lo + FOX_DH)
    qm = jnp.where(mask, q_ref[:, pr * LANES:(pr + 1) * LANES], 0.0).astype(bf16)
    kp = k_ref[:, pr * LANES:(pr + 1) * LANES].astype(bf16)
    s = _dot_nt(qm, kp, None) * (FOX_DH ** -0.5)
    s = s + gcum_ref[:, 8 + h:9 + h] - gcumt_ref[8 + h:9 + h, :]
    rows = i * FOX_BQ + lax.broadcasted_iota(jnp.int32, (FOX_BQ, keys), 0)
    cols = lax.broadcasted_iota(jnp.int32, (FOX_BQ, keys), 1)
    return jnp.where(cols <= rows, s, NEG), mask, qm, kp


def _fox_fwd(proj, gcum, gcumt, ride=None):
    c0 = SEG_FOX // 512

    def group_call(q0, nq, keys, comm):
        def body(q_ref, k_ref, v_ref, gcum_ref, gcumt_ref, o_ref, lse_ref):
            i = q0 + pl.program_id(0)
            lane = _lane((FOX_BQ, LANES))
            lse_all = jnp.zeros((FOX_BQ, LANES), f32)
            for pr in range(FOX_HEADS // 2):
                vp = v_ref[:, pr * LANES:(pr + 1) * LANES].astype(bf16)
                o_pair = jnp.zeros((FOX_BQ, LANES), f32)
                for h in (2 * pr, 2 * pr + 1):
                    s, mask, _, _ = _fox_scores(q_ref, k_ref, gcum_ref, gcumt_ref, h, i, keys)
                    m = jnp.max(s, axis=1, keepdims=True)
                    p = jnp.exp(s - m)
                    l = jnp.sum(p, axis=1, keepdims=True)
                    o_h = _dot((p * (1.0 / l)).astype(bf16), vp, None)
                    o_pair = jnp.where(mask, o_h, o_pair)
                    lse_all = jnp.where(lane == h, m + jnp.log(l), lse_all)
                o_ref[:, pr * LANES:(pr + 1) * LANES] = o_pair
            lse_ref[...] = lse_all

        seen = lambda col: pl.BlockSpec((keys, 512), lambda i: (0, col))
        return _hosted(
            body, comm, name=f"fox_fwd_{keys}", grid=(nq,),
            in_specs=[pl.BlockSpec((FOX_BQ, 512), lambda i: (q0 + i, c0)), seen(c0 + 1), seen(c0 + 2),
                      pl.BlockSpec((FOX_BQ, LANES), lambda i: (q0 + i, 0)), pl.BlockSpec((LANES, keys), lambda i: (0, 0))],
            out_specs=(pl.BlockSpec((FOX_BQ, 512), lambda i: (i, 0)), pl.BlockSpec((FOX_BQ, LANES), lambda i: (i, 0))),
            out_shape=(jax.ShapeDtypeStruct((nq * FOX_BQ, 512), f32), jax.ShapeDtypeStruct((nq * FOX_BQ, LANES), f32)),
            args=(proj, proj, proj, gcum, gcumt))

    parts = []
    for n, g in enumerate(_fox_groups(proj.shape[0])):
        hook = ride(n) if ride else None
        part, moved = group_call(*g, hook[0] if hook else None)
        parts.append(part)
        if hook:
            hook[1](moved)
    return jnp.concatenate([o for o, _ in parts], axis=0), jnp.concatenate([l for _, l in parts], axis=0)


def _fox_bwd(proj, gcum, gcumt, o, lse, do, ride=None):
    t = proj.shape[0]
    c0 = SEG_FOX // 512

    def group_call(q0, nq, keys, acc, comm):
        first = acc is None

        def body(q_ref, k_ref, v_ref, gcum_ref, gcumt_ref, o_ref, lse_ref, do_ref, *rest):
            dq_ref, dk_ref, dv_ref, dcc_ref, dct_ref = rest[-5:]
            j = pl.program_id(0)
            i = q0 + j

            @pl.when(j == 0)
            def _():
                if first:
                    dk_ref[...] = jnp.zeros_like(dk_ref)
                    dv_ref[...] = jnp.zeros_like(dv_ref)
                    dct_ref[...] = jnp.zeros_like(dct_ref)
                else:
                    dk_ref[...], dv_ref[...], dct_ref[...] = rest[0][...], rest[1][...], rest[2][...]

            lane = _lane((FOX_BQ, LANES))
            dcc = jnp.zeros((FOX_BQ, LANES), f32)
            scale = FOX_DH ** -0.5
            for pr in range(FOX_HEADS // 2):
                sl = slice(pr * LANES, (pr + 1) * LANES)
                vp = v_ref[:, sl].astype(bf16)
                dq_pair = jnp.zeros((FOX_BQ, LANES), f32)
                for h in (2 * pr, 2 * pr + 1):
                    s, mask, qm, kp = _fox_scores(q_ref, k_ref, gcum_ref, gcumt_ref, h, i, keys)
                    p = jnp.exp(s - lse_ref[:, h:h + 1])
                    dom = jnp.where(mask, do_ref[:, sl], 0.0)
                    delta = jnp.sum(dom * o_ref[:, sl], axis=1, keepdims=True)
                    domb = dom.astype(bf16)
                    ds = p * (_dot_nt(domb, vp, None) - delta)
                    dsb = ds.astype(bf16)
                    dv_ref[:, sl] += _dot_tn(p.astype(bf16), domb, None)
                    dk_ref[:, sl] += _dot_tn(dsb, qm, None) * scale
                    dq_pair = jnp.where(mask, _dot(dsb, kp, None) * scale, dq_pair)
                    dcc = jnp.where(lane == 8 + h, jnp.sum(ds, axis=1, keepdims=True), dcc)
                    dct_ref[8 + h:9 + h, :] += -jnp.sum(ds, axis=0, keepdims=True)
                dq_ref[:, sl] = dq_pair.astype(bf16)
            dcc_ref[...] = dcc

        qblk = lambda col: pl.BlockSpec((FOX_BQ, 512), lambda i: (q0 + i, col))
        oblk = pl.BlockSpec((FOX_BQ, 512), lambda i: (i, 0))
        seen = lambda col: pl.BlockSpec((keys, 512), lambda i: (0, col))
        rblk = pl.BlockSpec((FOX_BQ, LANES), lambda i: (q0 + i, 0))
        seen_t = pl.BlockSpec((LANES, keys), lambda i: (0, 0))
        in_specs = [qblk(c0), seen(c0 + 1), seen(c0 + 2), rblk, seen_t, qblk(0), rblk, qblk(0)]
        args = [proj, proj, proj, gcum, gcumt, o, lse, do]
        aliases = {}
        if not first:
            in_specs += [seen(0), seen(0), seen_t]
            args += list(acc)
            aliases = {8: 1, 9: 2, 10: 4}
        return _hosted(
            body, comm, name=f"fox_bwd_{keys}", grid=(nq,), in_specs=in_specs,
            out_specs=(oblk, seen(0), seen(0), pl.BlockSpec((FOX_BQ, LANES), lambda i: (i, 0)), seen_t),
            out_shape=(jax.ShapeDtypeStruct((nq * FOX_BQ, 512), bf16), jax.ShapeDtypeStruct((t, 512), f32), jax.ShapeDtypeStruct((t, 512), f32),
                       jax.ShapeDtypeStruct((nq * FOX_BQ, LANES), f32), jax.ShapeDtypeStruct((LANES, t), f32)),
            aliases=aliases, args=args)

    acc, dqs, dccs = None, [], []
    for n, g in enumerate(reversed(_fox_groups(t))):
        hook = ride(n) if ride else None
        (dq, dk, dv, dcc, dct), moved = group_call(*g, acc, hook[0] if hook else None)
        if hook:
            hook[1](moved)
        acc = (dk, dv, dct)
        dqs.insert(0, dq)
        dccs.insert(0, dcc)
    return jnp.concatenate(dqs, axis=0), acc[0], acc[1], jnp.concatenate(dccs, axis=0), acc[2]


def _row(v, width=None):
    v = v.reshape(1, -1).astype(f32)
    if width is not None and v.shape[1] < width:
        v = jnp.pad(v, ((0, 0), (0, width - v.shape[1])))
    return v


LATE = ("w_out", "w_up", "w_ple_gate", "w_ple", "w_down")


def _device_grads(x, p, target, small, w_cat, conv_w, late, qc=None, tail=None):
    z4 = jnp.zeros((4,), f32)
    bias_row = _row(jnp.concatenate([z4, small["dt_bias"].reshape(-1), small["b_f"].reshape(-1)]), LANES)
    alog_row = _row(jnp.concatenate([z4, small["a_log"].reshape(-1)]), LANES)
    g_gdn = _row(small["gdn_norm_g"])
    g_fox2 = _row(jnp.tile(small["fox_norm_g"].reshape(-1), 2))
    pb = p.astype(bf16)
    late = list(late)
    comm = qc is not None

    h0, h0b = _ln_in(x, _row(small["ln_in_g"]), _row(small["ln_in_b"]))
    proj = _mm(h0b, w_cat, "nn", 256, D_CAT, "mm_proj")
    gates, gcum, gcumt = _gates(proj, bias_row, alog_row)
    early = [(0, 0, 1), (3, 0, 1), (1, 0, 2), (1, 1, 2)]
    rest = [(2, 0, 1), (4, 0, 2), (4, 1, 2)]
    state = dict(late=late)

    def gather(phase, pieces):
        if not comm:
            return None, lambda moved: None
        return phase(state["late"], pieces), lambda moved: state.update(late=list(moved))

    cm, took = gather(_gather_chips, early[:2])
    (conv_c, qkv_n), moved = _gdn_conv(proj, conv_w, cm)
    took(moved)
    cm, took = gather(_gather_chips, early[2:3])
    (gu, gw, gqk, gtinv), moved = _gdn_local(qkv_n, gcum, cm)
    took(moved)
    cm, took = gather(_gather_chips, early[3:])
    (o_gdn, sall), moved = _gdn_fwd(qkv_n, gcum, gu, gw, gqk, cm)
    took(moved)
    fox_plan = [(_gather_chips, rest[:1]), (_gather_pass_on, early), (_gather_chips, rest[1:2]), (_gather_chips, rest[2:])]
    assert not comm or len(_fox_groups(x.shape[0])) == len(fox_plan)
    o_fox, lse = _fox_fwd(proj, gcum, gcumt, (lambda n: gather(*fox_plan[n])) if comm else None)
    cm, took = gather(_gather_pass_on, rest)
    (attn,), moved = _attn_post(o_gdn, proj, o_fox, g_gdn, g_fox2, cm)
    took(moved)
    w_out, w_up, w_gate, w_ple, w_down = state["late"]
    w_out, w_gate, w_down = w_out.reshape(D_MODEL, D_MODEL), w_gate.reshape(D_MODEL, D_MODEL), w_down.reshape(D_FF, D_MODEL)
    mix = _mm(attn, w_out, "nn", 512, D_MODEL, "mm_mix")
    h1, h1b, xhat1, rstd1 = _ln1(h0, mix, _row(small["ln1_g"]), _row(small["ln1_b"]))
    up, act = _mm(h1b, w_up, "nn", 256, 1024, "mm_up", epi="relu2", shards=N_CHIPS)
    ff = _mm(act, w_down, "nn", 256, D_MODEL, "mm_down")
    gp = _mm(h1b, w_gate, "nn", 512, D_MODEL, "mm_gate")
    pe = _mm(pb, w_ple, "nn", 512, D_MODEL // N_CHIPS, "mm_ple", shards=N_CHIPS)
    dr2, dr2b, dpe, dgp, pg2 = _ln2_loss(h1, ff, pe, gp, _row(small["b_ple_gate"]), _row(small["ln2_g"]), _row(small["ln2_b"]), target)

    dup = _mm(dr2b, w_down, "nt", 256, 2048, "mm_dact", epi="relu2_bwd", extra=up)
    g_down = _mm(act, dr2b, "tn", 1024, D_MODEL, "mm_gdown")
    dh1_a = _mm(dup, w_up, "nt", 256, D_MODEL, "mm_dh1a", shards=N_CHIPS)
    g_up = _mm(h1b, dup, "tn", 1024, 1024, "mm_gup", shards=N_CHIPS)
    dh1_b = _mm(dgp, w_gate, "nt", 512, D_MODEL, "mm_dh1b")
    g_gate = _mm(h1b, dgp, "tn", 1024, D_MODEL, "mm_ggate")
    g_ple = _mm(pb, dpe, "tn", D_PLE, D_MODEL // N_CHIPS, "mm_gple", shards=N_CHIPS)
    dr1, dr1b, pg1 = _ln1_bwd(dr2, dh1_a, dh1_b, xhat1, rstd1, _row(small["ln1_g"]))
    dattn = _mm(dr1b, w_out, "nt", 512, D_MODEL, "mm_dattn")
    g_out = _mm(attn, dr1b, "tn", 1024, D_MODEL, "mm_gout")
    do_gdn, dz, do_fox, pga = _attn_post_bwd(dattn, o_gdn, proj, o_fox, g_gdn, g_fox2)
    g_late = [g.reshape((N_CHIPS, -1, g.shape[-1])) for g in (g_out, g_up, g_gate, g_ple, g_down)]
    chip_plan = [[(4, 0, 2), (0, 0, 1)], [(4, 1, 2)], [(2, 0, 1), (3, 0, 1)], [(1, 0, 2), (1, 1, 2)]]

    def to_sibling():
        def took(moved):
            sums = [_add_pair(g, b1, qc, "add_pair_" + n) for g, b1, n in zip(g_late, moved, LATE)]
            state.update(own=[a for a, _ in sums], sent=[ab for _, ab in sums], landing=_landing([ab for _, ab in sums]))
        return _exchange_pairs(g_late), took

    def to_chips(pieces):
        if not comm:
            return None, lambda moved: None
        return _exchange_chips(state["sent"], state["landing"], pieces), lambda moved: state.update(landing=list(moved))

    assert not comm or len(_fox_groups(x.shape[0])) == len(chip_plan)
    dfq, dfk, dfv, dccol, dct = _fox_bwd(proj, gcum, gcumt, o_fox, lse, do_fox,
                                         (lambda n: to_sibling() if n == 0 else to_chips(chip_plan[n - 1])) if comm else None)
    cm, took = to_chips(chip_plan[-1])
    (dqkv_n, dgates), moved = _gdn_bwd(qkv_n, gcum, gu, gw, gqk, gtinv, sall, do_gdn, cm)
    took(moved)
    dsmall, pgg = _gates_bwd(proj, bias_row, alog_row, gates, dgates, dccol, dct)
    cm = None
    if comm:
        cm = _share_halves([_add_chips(a, b2, qc, "add_chips_" + n) for a, b2, n in zip(state["own"], state["landing"], LATE)])
    (du, g_conv8), reduced = _gdn_conv_bwd(proj, conv_w, conv_c, dqkv_n, cm)
    if comm:
        g_late = list(reduced)
    t = x.shape[0]
    dproj = jnp.concatenate([du, dz, dfq, dfk.astype(bf16), dfv.astype(bf16), dsmall, jnp.zeros((t, D_CAT - SEG_SMALL - LANES), bf16)], axis=1)
    g_cat = _mm(h0b, dproj, "tn", 1024, 1280, "mm_gcat")
    cm, took = tail[0](g_cat) if tail else (None, None)
    dh0_mm = _mm(dproj, w_cat, "nt", 256, D_MODEL, "mm_dh0", comm=cm)
    if cm:
        dh0_mm, moved = dh0_mm
        took(moved)
    cm, took = tail[1]() if tail else (None, None)
    (grad_x, pg0), moved = _ln_in_bwd(x, dr1, dh0_mm, _row(small["ln_in_g"]), cm)
    if cm:
        took(moved)

    g_fox = pga[1, :FOX_DH] + pga[1, FOX_DH:]
    small_grads = dict(
        ln_in_g=pg0[0], ln_in_b=pg0[1], ln1_g=pg1[0], ln1_b=pg1[1], b_ple_gate=pg2[2], ln2_g=pg2[0], ln2_b=pg2[1],
        gdn_norm_g=pga[0], fox_norm_g=g_fox, a_log=pgg[1, 4:8], dt_bias=pgg[0, 4:8], b_f=pgg[0, 8:16], loss=pg2[3, 0:1])
    return grad_x, g_cat, g_conv8[:CONV_W], dict(zip(LATE, g_late)), small_grads


ANY = pl.BlockSpec(memory_space=pl.ANY)
CONV_PKT_ROWS = 16


def _mesh_pos():
    return lax.axis_index("x"), lax.axis_index("y"), lax.axis_index("c")


def _other_chips(x, y):
    return [(1 - x, y), (x, 1 - y), (1 - x, 1 - y)]


def _rcopy(src, dst, send_sem, recv_sem, dev):
    return pltpu.make_async_remote_copy(src_ref=src, dst_ref=dst, send_sem=send_sem, recv_sem=recv_sem,
                                        device_id=dev, device_id_type=MESH)


class _Comm:
    def __init__(self, ins, outs, aliases, n_sems, start, finish):
        self.ins, self.outs, self.aliases, self.n_sems, self.start, self.finish = list(ins), list(outs), dict(aliases), n_sems, start, finish


def _hosted(body, comm, *, name, grid, in_specs, out_specs, out_shape, args, scratch_shapes=(), aliases=None):
    n_in, n_out, n_sc = len(in_specs), len(out_specs), len(scratch_shapes)
    k, ko = (len(comm.ins), len(comm.outs)) if comm else (0, 0)

    def kernel_body(*refs):
        o0 = n_in + k
        s0 = o0 + n_out + ko
        if comm:
            cins, couts, (ssem, rsem) = refs[n_in:o0], refs[o0 + n_out:s0], refs[s0 + n_sc:]
            step = pl.program_id(0)
            for d in range(1, len(grid)):
                step = step * grid[d] + pl.program_id(d)

            @pl.when(step == 0)
            def _():
                comm.start(cins, couts, ssem, rsem)

        body(*refs[:n_in], *refs[o0:o0 + n_out], *refs[s0:s0 + n_sc])
        if comm:
            last = 1
            for n in grid:
                last *= n

            @pl.when(step == last - 1)
            def _():
                comm.finish(cins, couts, ssem, rsem)

    io_aliases = dict(aliases or {})
    scratch = list(scratch_shapes)
    if comm:
        io_aliases.update({n_in + i: n_out + j for i, j in comm.aliases.items()})
        scratch += [pltpu.SemaphoreType.DMA((comm.n_sems,)), pltpu.SemaphoreType.DMA((comm.n_sems,))]
    res = pl.pallas_call(
        kernel_body, name=name, grid=grid, in_specs=list(in_specs) + [ANY] * k, out_specs=tuple(out_specs) + (ANY,) * ko,
        out_shape=tuple(out_shape) + tuple(comm.outs if comm else ()), scratch_shapes=scratch, input_output_aliases=io_aliases,
        compiler_params=_params(("arbitrary",) * len(grid)),
    )(*args, *(comm.ins if comm else ()))
    return tuple(res[:n_out]), tuple(res[n_out:])


def _comm_only(phases, name):
    n_in = sum(len(p.ins) for p in phases)

    def body(*refs):
        n_out = sum(len(p.outs) for p in phases)
        sems = refs[n_in + n_out:]
        i0, o0 = 0, n_in
        for j, p in enumerate(phases):
            cins, couts = refs[i0:i0 + len(p.ins)], refs[o0:o0 + len(p.outs)]
            p.start(cins, couts, sems[2 * j], sems[2 * j + 1])
            p.finish(cins, couts, sems[2 * j], sems[2 * j + 1])
            i0 += len(p.ins)
            o0 += len(p.outs)

    aliases, i0, o0 = {}, 0, 0
    for p in phases:
        aliases.update({i0 + i: o0 + j for i, j in p.aliases.items()})
        i0 += len(p.ins)
        o0 += len(p.outs)
    outs = [o for p in phases for o in p.outs]
    res = pl.pallas_call(
        body, name=name, out_shape=tuple(outs), in_specs=[ANY] * n_in, out_specs=(ANY,) * len(outs), input_output_aliases=aliases,
        scratch_shapes=[pltpu.SemaphoreType.DMA((p.n_sems,)) for p in phases for _ in range(2)],
    )(*[a for p in phases for a in p.ins])
    split, o0 = [], 0
    for p in phases:
        split.append(tuple(res[o0:o0 + len(p.outs)]))
        o0 += len(p.outs)
    return split


def _like(arrays):
    return [jax.ShapeDtypeStruct(a.shape, a.dtype) for a in arrays]


def _half(ref, slot, hf, piece=(0, 1)):
    k, n = piece
    rows = ref.shape[1] // 2 // n
    return ref.at[slot, pl.ds((hf * n + k) * rows, rows)]


def _whole_halves(arrays):
    return [(i, 0, 1) for i in range(len(arrays))]


def _gather_chips(bufs, pieces=None, whole=False, base=0):
    nw = len(bufs)
    pieces = _whole_halves(bufs) if pieces is None else pieces
    part = (lambda ref, slot, c, piece: ref.at[slot]) if whole else _half

    def copies(couts):
        x, y, c = _mesh_pos()
        q = 2 * x + y
        for j, (i, k, n) in enumerate(pieces):
            for kc, chip in enumerate(_other_chips(x, y)):
                mine, theirs = part(couts[i], q, c, (k, n)), part(couts[i], 2 * chip[0] + chip[1], c, (k, n))
                yield base + j * 3 + kc, mine, theirs, (*chip, c)

    def start(cins, couts, ssem, rsem):
        for s, mine, _, dev in copies(couts):
            _rcopy(mine, mine, ssem.at[s], rsem.at[s], dev).start()

    def finish(cins, couts, ssem, rsem):
        for s, _, theirs, dev in copies(couts):
            _rcopy(theirs, theirs, ssem.at[s], rsem.at[s], dev).wait_recv()
        for s, mine, _, dev in copies(couts):
            _rcopy(mine, mine, ssem.at[s], rsem.at[s], dev).wait_send()

    return _Comm(bufs, _like(bufs), {i: i for i in range(nw)}, 3 * len(pieces), start, finish)


def _gather_pass_on(bufs, pieces=None, base=0):
    nw = len(bufs)
    pieces = _whole_halves(bufs) if pieces is None else pieces

    def copies(couts):
        x, y, c = _mesh_pos()
        for j, (i, k, n) in enumerate(pieces):
            for kc, chip in enumerate(_other_chips(x, y)):
                slot = 2 * chip[0] + chip[1]
                yield base + j * 3 + kc, _half(couts[i], slot, c, (k, n)), _half(couts[i], slot, 1 - c, (k, n)), (x, y, 1 - c)

    def start(cins, couts, ssem, rsem):
        for s, landed, _, sib in copies(couts):
            _rcopy(landed, landed, ssem.at[s], rsem.at[s], sib).start()

    def finish(cins, couts, ssem, rsem):
        for s, _, passed, sib in copies(couts):
            _rcopy(passed, passed, ssem.at[s], rsem.at[s], sib).wait_recv()
        for s, landed, _, sib in copies(couts):
            _rcopy(landed, landed, ssem.at[s], rsem.at[s], sib).wait_send()

    return _Comm(bufs, _like(bufs), {i: i for i in range(nw)}, 3 * len(pieces), start, finish)


def _gather_now(bufs, packets):
    nb = len(bufs)
    over, on, pk = _gather_chips(bufs), _gather_pass_on(bufs, base=3 * nb), _gather_chips(packets, whole=True, base=6 * nb)

    def start(cins, couts, ssem, rsem):
        over.start(cins[:nb], couts[:nb], ssem, rsem)
        pk.start(cins[nb:], couts[nb:], ssem, rsem)

    def finish(cins, couts, ssem, rsem):
        over.finish(cins[:nb], couts[:nb], ssem, rsem)
        on.start(cins[:nb], couts[:nb], ssem, rsem)
        on.finish(cins[:nb], couts[:nb], ssem, rsem)
        pk.finish(cins[nb:], couts[nb:], ssem, rsem)

    every = list(bufs) + list(packets)
    return _Comm(every, _like(every), {i: i for i in range(len(every))}, 6 * nb + 3 * len(packets), start, finish)


def _exchange_pairs(gs):
    nw = len(gs)

    def copies(cins, couts):
        x, y, c = _mesh_pos()
        for i in range(nw):
            for d in range(N_CHIPS):
                yield i * N_CHIPS + d, _half(cins[i], d, 1 - c), couts[i].at[d], (x, y, 1 - c)

    def start(cins, couts, ssem, rsem):
        for s, src, dst, sib in copies(cins, couts):
            _rcopy(src, dst, ssem.at[s], rsem.at[s], sib).start()

    def finish(cins, couts, ssem, rsem):
        for s, src, dst, sib in copies(cins, couts):
            _rcopy(src, dst, ssem.at[s], rsem.at[s], sib).wait_recv()
        for s, src, dst, sib in copies(cins, couts):
            _rcopy(src, dst, ssem.at[s], rsem.at[s], sib).wait_send()

    outs = [jax.ShapeDtypeStruct((N_CHIPS, g.shape[1] // 2, g.shape[2]), g.dtype) for g in gs]
    return _Comm(gs, outs, {}, N_CHIPS * nw, start, finish)


def _gather_packets(small):
    def peers():
        x, y, c = _mesh_pos()
        for r in range(1, 8):
            fx, fy, fc = (r >> 2) & 1, (r >> 1) & 1, r & 1
            yield r - 1, (1 - x if fx else x, 1 - y if fy else y, 1 - c if fc else c)

    def start(cins, couts, ssem, rsem):
        x, y, c = _mesh_pos()
        mine = couts[0].at[4 * x + 2 * y + c]
        for s, peer in peers():
            _rcopy(mine, mine, ssem.at[s], rsem.at[s], peer).start()

    def finish(cins, couts, ssem, rsem):
        x, y, c = _mesh_pos()
        mine = couts[0].at[4 * x + 2 * y + c]
        for s, peer in peers():
            theirs = couts[0].at[4 * peer[0] + 2 * peer[1] + peer[2]]
            _rcopy(theirs, theirs, ssem.at[s], rsem.at[s], peer).wait_recv()
        for s, peer in peers():
            _rcopy(mine, mine, ssem.at[s], rsem.at[s], peer).wait_send()

    return _Comm([small], _like([small]), {0: 0}, 7, start, finish)


def _exchange_chips(a4s, b2s, pieces=None):
    nw = len(a4s)
    pieces = _whole_halves(a4s) if pieces is None else pieces

    def copies(cins, couts):
        x, y, c = _mesh_pos()
        for j, (i, k, n) in enumerate(pieces):
            rows = a4s[i].shape[1] // n
            part = pl.ds(k * rows, rows)
            for kc, chip in enumerate(_other_chips(x, y)):
                yield j * 3 + kc, cins[i].at[2 * chip[0] + chip[1], part], couts[i].at[kc, part], (*chip, c)

    def start(cins, couts, ssem, rsem):
        for s, src, dst, dev in copies(cins, couts):
            _rcopy(src, dst, ssem.at[s], rsem.at[s], dev).start()

    def finish(cins, couts, ssem, rsem):
        for s, src, dst, dev in copies(cins, couts):
            _rcopy(src, dst, ssem.at[s], rsem.at[s], dev).wait_recv()
        for s, src, dst, dev in copies(cins, couts):
            _rcopy(src, dst, ssem.at[s], rsem.at[s], dev).wait_send()

    return _Comm(list(a4s) + list(b2s), _like(b2s), {nw + i: i for i in range(nw)}, 3 * len(pieces), start, finish)


def _landing(a4s):
    return [lax.empty((3,) + a.shape[1:], a.dtype) for a in a4s]


def _share_halves(rs):
    nw = len(rs)

    def halves(couts, i, hf):
        rows = rs[i].shape[0] // 2
        return couts[i].at[pl.ds(hf * rows, rows)]

    def start(cins, couts, ssem, rsem):
        x, y, c = _mesh_pos()
        for i in range(nw):
            _rcopy(halves(couts, i, c), halves(couts, i, c), ssem.at[i], rsem.at[i], (x, y, 1 - c)).start()

    def finish(cins, couts, ssem, rsem):
        x, y, c = _mesh_pos()
        for i in range(nw):
            _rcopy(halves(couts, i, 1 - c), halves(couts, i, 1 - c), ssem.at[i], rsem.at[i], (x, y, 1 - c)).wait_recv()
        for i in range(nw):
            _rcopy(halves(couts, i, c), halves(couts, i, c), ssem.at[i], rsem.at[i], (x, y, 1 - c)).wait_send()

    return _Comm(rs, _like(rs), {i: i for i in range(nw)}, nw, start, finish)


ADD_ROWS = 256


def _add_pair(g4, b1, qc_idx, name):
    _, half, cols = b1.shape
    rb = min(ADD_ROWS, half)
    nb = half // rb

    def body(qc_ref, g_ref, b_ref, o_ref, ob_ref):
        a = g_ref[...] + b_ref[...]
        o_ref[...] = a
        ob_ref[...] = a.astype(bf16)

    blk = (1, rb, cols)
    out = pl.BlockSpec(blk, lambda d, i, qc: (d, i, 0))
    return pl.pallas_call(
        body, name=name,
        grid_spec=pltpu.PrefetchScalarGridSpec(
            num_scalar_prefetch=1, grid=(N_CHIPS, nb),
            in_specs=[pl.BlockSpec(blk, lambda d, i, qc: (d, qc[1] * nb + i, 0)), out],
            out_specs=(out, out)),
        out_shape=(jax.ShapeDtypeStruct(b1.shape, f32), jax.ShapeDtypeStruct(b1.shape, bf16)),
        compiler_params=_params(("parallel", "parallel")),
    )(qc_idx, g4, b1)


def _add_chips(a4, b2, qc_idx, name):
    _, half, cols = a4.shape
    rb = min(ADD_ROWS, half)
    nb = half // rb

    def body(qc_ref, a_ref, b_ref, o_ref):
        o_ref[...] = ((a_ref[0] + b_ref[0].astype(f32)) + b_ref[1].astype(f32)) + b_ref[2].astype(f32)

    return pl.pallas_call(
        body, name=name,
        grid_spec=pltpu.PrefetchScalarGridSpec(
            num_scalar_prefetch=1, grid=(nb,),
            in_specs=[pl.BlockSpec((1, rb, cols), lambda i, qc: (qc[0], i, 0)), pl.BlockSpec((3, rb, cols), lambda i, qc: (0, i, 0))],
            out_specs=pl.BlockSpec((rb, cols), lambda i, qc: (qc[1] * nb + i, 0))),
        out_shape=jax.ShapeDtypeStruct((2 * half, cols), f32),
        compiler_params=_params(("parallel",)),
    )(qc_idx, a4, b2)


def _adamw_math(w, g, m, v):
    m = ADAM_B1 * m + (1.0 - ADAM_B1) * g
    v = ADAM_B2 * v + (1.0 - ADAM_B2) * (g * g)
    m_hat = m / (1.0 - ADAM_B1 ** ADAM_STEP)
    v_hat = v / (1.0 - ADAM_B2 ** ADAM_STEP)
    return -ADAM_LR * (m_hat / (jnp.sqrt(v_hat) + ADAM_EPS) + ADAM_WD * w), m, v


def _adamw(w, g, m, v, name, comm=None):
    rows, cols = w.shape
    rb = ADD_ROWS if rows % ADD_ROWS == 0 else rows

    def body(w_ref, g_ref, m_ref, v_ref, go_ref, d_ref, mo_ref, vo_ref):
        g = g_ref[...]
        go_ref[...] = g
        d_ref[...], mo_ref[...], vo_ref[...] = _adamw_math(w_ref[...], g, m_ref[...], v_ref[...])

    blk = pl.BlockSpec((rb, cols), lambda i: (i, 0))
    return _hosted(body, comm, name=name, grid=(rows // rb,), in_specs=[blk] * 4, out_specs=(blk,) * 4,
                   out_shape=(jax.ShapeDtypeStruct(w.shape, f32),) * 4, args=(w, g, m, v))


def _small_sum_adamw(all_pkts, w, m, v):
    def body(a_ref, w_ref, m_ref, v_ref, g_ref, d_ref, mo_ref, vo_ref):
        g = a_ref[0]
        for r in range(1, 8):
            g = g + a_ref[r]
        g_ref[...] = g
        d_ref[...], mo_ref[...], vo_ref[...] = _adamw_math(w_ref[...], g, m_ref[...], v_ref[...])

    return pl.pallas_call(body, name="small_sum_adamw", out_shape=(jax.ShapeDtypeStruct(w.shape, f32),) * 4)(all_pkts, w, m, v)


SHARDED = (("w_in", (D_MODEL, D_IN // N_CHIPS), 2), ("w_out", (D_MODEL // N_CHIPS, D_MODEL), 1), ("w_up", (D_MODEL, D_FF // N_CHIPS), 2),
           ("w_ple_gate", (D_MODEL // N_CHIPS, D_MODEL), 1), ("w_ple", (D_PLE, D_MODEL // N_CHIPS), 1),
           ("w_down", (D_FF // N_CHIPS, D_MODEL), 2))
SMALL_LAYOUT = (("ln_in_g", 0, 1024), ("ln_in_b", 8, 1024), ("ln1_g", 16, 1024), ("ln1_b", 24, 1024), ("b_ple_gate", 32, 1024),
                ("ln2_g", 40, 1024), ("ln2_b", 48, 1024), ("gdn_norm_g", 56, 128), ("fox_norm_g", 57, 64), ("a_log", 58, 4),
                ("dt_bias", 59, 4), ("b_f", 60, 8), ("loss", 61, 1))
SMALL_CONV_ROW = 64
SMALL_ROWS = 128


def _pack_small(vals, conv=None):
    rows = []
    nxt = 0
    for n, r0, size in SMALL_LAYOUT:
        assert r0 == nxt
        v = vals[n].reshape(-1).astype(f32) if n in vals else jnp.zeros((size,), f32)
        nrows = -(-size // LANES)
        rows.append(jnp.pad(v, (0, nrows * LANES - size)).reshape(nrows, LANES))
        nxt = r0 + nrows
    rows.append(jnp.zeros((SMALL_CONV_ROW - nxt, LANES), f32))
    conv_rows = CONV_W * GDN_QKV // LANES
    rows.append(jnp.zeros((conv_rows, LANES), f32) if conv is None else conv.reshape(conv_rows, LANES))
    rows.append(jnp.zeros((SMALL_ROWS - SMALL_CONV_ROW - conv_rows, LANES), f32))
    return jnp.concatenate(rows, axis=0)


def _unpack_small(pkt, shapes):
    out = {}
    for n, r0, size in SMALL_LAYOUT:
        if n in shapes:
            nrows = -(-size // LANES)
            out[n] = pkt[r0:r0 + nrows].reshape(-1)[:size].reshape(shapes[n])
    return out


WEIGHTS = ("ln_in_g", "ln_in_b", "w_in", "conv_w", "a_log", "dt_bias", "gdn_norm_g", "b_f", "fox_norm_g", "w_out", "ln1_g", "ln1_b",
           "w_up", "w_down", "w_ple", "w_ple_gate", "b_ple_gate", "ln2_g", "ln2_b")
SMALL_NAMES = tuple(n for n, _, _ in SMALL_LAYOUT if n != "loss")


def kernel(x, p, ln_in_g, ln_in_b, w_in, conv_w, a_log, dt_bias, gdn_norm_g, b_f, fox_norm_g, w_out, ln1_g, ln1_b, w_up, w_down, w_ple, w_ple_gate, b_ple_gate, ln2_g, ln2_b, loss_target, m_ln_in_g, m_ln_in_b, m_w_in, m_conv_w, m_a_log, m_dt_bias, m_gdn_norm_g, m_b_f, m_fox_norm_g, m_w_out, m_ln1_g, m_ln1_b, m_w_up, m_w_down, m_w_ple, m_w_ple_gate, m_b_ple_gate, m_ln2_g, m_ln2_b, v_ln_in_g, v_ln_in_b, v_w_in, v_conv_w, v_a_log, v_dt_bias, v_gdn_norm_g, v_b_f, v_fox_norm_g, v_w_out, v_ln1_g, v_ln1_b, v_w_up, v_w_down, v_w_ple, v_w_ple_gate, v_b_ple_gate, v_ln2_g, v_ln2_b):
    given = dict(locals())
    w = {n: given[n] for n in WEIGHTS}
    m = {n: given["m_" + n] for n in WEIGHTS}
    v = {n: given["v_" + n] for n in WEIGHTS}
    xi, yi, ci = _mesh_pos()
    q = 2 * xi + yi

    def slot_buffer(val, dtype, slots=N_CHIPS, slot=q):
        return lax.dynamic_update_slice(lax.empty((slots,) + val.shape, dtype), val.astype(dtype)[None], (slot, 0, 0))

    conv_rows = CONV_W * GDN_QKV // N_CHIPS // LANES
    conv_pkt = jnp.pad(w["conv_w"][0].reshape(-1, LANES), ((0, CONV_PKT_ROWS - conv_rows), (0, 0)))
    (w_in4, conv_all), = _comm_only([_gather_now([slot_buffer(w["w_in"][0], bf16)], [slot_buffer(conv_pkt, f32)])], "gather_w_in")
    conv_full = jnp.concatenate([conv_all[d, :conv_rows].reshape(CONV_W, GDN_QKV // N_CHIPS) for d in range(N_CHIPS)], axis=1)
    wi = jnp.concatenate([w_in4[d] for d in range(N_CHIPS)], axis=1)
    w_cat = jnp.concatenate([wi[:, :OFF_BETA], wi[:, OFF_FOX:OFF_F], wi[:, OFF_BETA:OFF_FOX], wi[:, OFF_F:],
                             jnp.zeros((D_MODEL, D_CAT - D_IN), bf16)], axis=1)

    small = {n: w[n] for n in SMALL_NAMES}
    qc = jnp.stack([q, ci]).astype(jnp.int32)
    tail_state = {}
    shard_cols = D_IN // N_CHIPS

    def pairs_phase(gc):
        g_in = jnp.concatenate([gc[:, :OFF_BETA], gc[:, SEG_SMALL:SEG_SMALL + 8], gc[:, SEG_FOX:SEG_SMALL],
                                gc[:, SEG_SMALL + 8:SEG_SMALL + 16]], axis=1)
        g_in4 = jnp.stack([g_in[:, d * shard_cols:(d + 1) * shard_cols] for d in range(N_CHIPS)])

        def took(moved):
            own, sent = _add_pair(g_in4, moved[0], qc, "add_pair_w_in")
            tail_state.update(own=own, sent=[sent], landing=_landing([sent]))
        return _exchange_pairs([g_in4]), took

    def chips_phase(piece):
        return (_exchange_chips(tail_state["sent"], tail_state["landing"], [(0, piece, 4)]),
                lambda moved: tail_state.update(landing=list(moved)))

    grad_x, _, g_conv, g_late, small_g = _device_grads(
        x[0], p[0, 0], loss_target[0], small, w_cat, conv_full, [slot_buffer(w[n][0], bf16) for n in LATE], qc,
        tail=(pairs_phase, lambda: chips_phase(0)))

    grads, delta, new_m, new_v = {}, {}, {}, {}
    packets = _gather_packets(slot_buffer(_pack_small(small_g, g_conv), f32, 8, 4 * xi + 2 * yi + ci))
    riders = dict(w_up=lambda: chips_phase(1), w_down=lambda: chips_phase(2), w_out=lambda: chips_phase(3),
                  w_ple_gate=lambda: (packets, lambda moved: tail_state.update(small_all=moved[0])))
    for n in ("w_up", "w_down", "w_out", "w_ple_gate", "w_ple"):
        cm, took = riders[n]() if n in riders else (None, None)
        outs, moved = _adamw(w[n][0], g_late[n], m[n][0], v[n][0], "adamw_" + n, cm)
        if cm:
            took(moved)
        grads[n], delta[n], new_m[n], new_v[n] = (a.reshape(w[n].shape) for a in outs)
    (g_in_red,), = _comm_only([_share_halves([_add_chips(tail_state["own"], tail_state["landing"][0], qc, "add_chips_w_in")])], "share_w_in")
    outs, _ = _adamw(w["w_in"][0], g_in_red, m["w_in"][0], v["w_in"][0], "adamw_w_in")
    grads["w_in"], delta["w_in"], new_m["w_in"], new_v["w_in"] = (a.reshape(w["w_in"].shape) for a in outs)
    shapes = {n: w[n].shape for n in SMALL_NAMES}
    g_pkt, d_pkt, m_pkt, v_pkt = _small_sum_adamw(tail_state["small_all"], _pack_small(w), _pack_small(m), _pack_small(v))
    for dst, pkt in ((grads, g_pkt), (delta, d_pkt), (new_m, m_pkt), (new_v, v_pkt)):
        dst.update(_unpack_small(pkt, shapes))
    conv_rows_all = CONV_W * GDN_QKV // LANES
    conv_g_full = g_pkt[SMALL_CONV_ROW:SMALL_CONV_ROW + conv_rows_all].reshape(CONV_W, GDN_QKV)
    conv_g = lax.dynamic_slice_in_dim(conv_g_full, q * (GDN_QKV // N_CHIPS), GDN_QKV // N_CHIPS, axis=1)
    outs, _ = _adamw(w["conv_w"][0], conv_g, m["conv_w"][0], v["conv_w"][0], "adamw_conv_w")
    grads["conv_w"], delta["conv_w"], new_m["conv_w"], new_v["conv_w"] = (a.reshape(w["conv_w"].shape) for a in outs)
    loss = g_pkt[61, 0]
    return (loss, grad_x[None], *[grads[n] for n in WEIGHTS], *[delta[n] for n in WEIGHTS],
            *[new_m[n] for n in WEIGHTS], *[new_v[n] for n in WEIGHTS])
```

```python
import functools

import jax
import jax.numpy as jnp
from jax import lax
from jax.experimental import pallas as pl
from jax.experimental.pallas import tpu as pltpu

f32 = jnp.float32
bf16 = jnp.bfloat16
HI = lax.Precision.HIGHEST
MESH = pl.DeviceIdType.MESH

D_MODEL = 1024
CHUNK = 64
GDN_HEADS = 4
GDN_DK = 128
FOX_HEADS = 8
FOX_DH = 64
CONV_W = 4
D_FF = 4096
D_PLE = 256
LN_EPS = 1e-5
NORM_EPS = 1e-6
ALPHA = 2.0 ** 0.25
GDN_QKV = 1536
OFF_Z = 1536
OFF_BETA = 2048
OFF_FOX = 2056
OFF_F = 3592
D_IN = 3600
ADAM_LR = 0.001
ADAM_B1 = 0.9
ADAM_B2 = 0.999
ADAM_EPS = 1e-08
ADAM_WD = 0.01
ADAM_STEP = 10

SEG_FOX = 2048
SEG_SMALL = 3584
D_CAT = 3840
LANES = 128
TOK_BLK = 256
FOX_BQ = 256
VMEM_LIMIT = 56 * 1024 * 1024
NEG = -1e30

N_CHIPS = 4
W_IN_ROWS = 928


def _params(sem=None, **kw):
    return pltpu.CompilerParams(dimension_semantics=sem, vmem_limit_bytes=VMEM_LIMIT, **kw)


def _sigmoid(x):
    return 1.0 / (1.0 + jnp.exp(-x))


def _softplus(x):
    return jnp.maximum(x, 0.0) + jnp.log(1.0 + jnp.exp(-jnp.abs(x)))


def _ln_fwd(x, g, b):
    mu = jnp.mean(x, -1, keepdims=True)
    xc = x - mu
    var = jnp.mean(xc * xc, -1, keepdims=True)
    rstd = lax.rsqrt(var + LN_EPS)
    xhat = xc * rstd
    return xhat * g + b, xhat, rstd


def _ln_bwd(dy, xhat, rstd, g):
    dxh = dy * g
    m1 = jnp.mean(dxh, -1, keepdims=True)
    m2 = jnp.mean(dxh * xhat, -1, keepdims=True)
    return rstd * (dxh - m1 - xhat * m2)


def _dot(a, b, prec=HI):
    return jnp.dot(a, b, precision=prec, preferred_element_type=f32)


def _dot_nt(a, b, prec=HI):
    return lax.dot_general(a, b, (((1,), (1,)), ((), ())), precision=prec, preferred_element_type=f32)


def _dot_tn(a, b, prec=HI):
    return lax.dot_general(a, b, (((0,), (0,)), ((), ())), precision=prec, preferred_element_type=f32)


def _bdot(a, b):
    return _dot(a.astype(bf16), b.astype(bf16), None)


def _bdot_nt(a, b):
    return _dot_nt(a.astype(bf16), b.astype(bf16), None)


def _bdot_tn(a, b):
    return _dot_tn(a.astype(bf16), b.astype(bf16), None)


def _lane(shape):
    return lax.broadcasted_iota(jnp.int32, shape, len(shape) - 1)


def _mm(a, b, mode, tm, tn, name, out_dtype=f32, epi=None, extra=None, shards=1, comm=None):
    if mode == "nn":
        (m, k), n = a.shape, b.shape[-1] * shards
    elif mode == "nt":
        (m, k), n = a.shape, b.shape[-2]
    else:
        (k, m), n = a.shape, b.shape[1]
    assert m % tm == 0 and n % tn == 0, (name, m, n, tm, tn)
    per = (n // shards) // tn
    assert mode == "nt" or per * tn * shards == n, (name, n, tn, shards)
    nc = 512 if tn % 512 == 0 else (256 if tn % 256 == 0 else 128)
    ks = k // shards

    def body(a_ref, b_ref, *rest):
        for n0 in range(0, tn, nc):
            if mode == "nn":
                acc = jnp.dot(a_ref[...], b_ref[:, n0:n0 + nc], preferred_element_type=f32)
            elif mode == "nt" and shards > 1:
                acc = jnp.zeros((tm, nc), f32)
                for d in range(shards):
                    acc = acc + lax.dot_general(a_ref[:, d * ks:(d + 1) * ks], b_ref[d, n0:n0 + nc, :], (((1,), (1,)), ((), ())),
                                                preferred_element_type=f32)
            elif mode == "nt":
                acc = lax.dot_general(a_ref[...], b_ref[n0:n0 + nc, :], (((1,), (1,)), ((), ())), preferred_element_type=f32)
            else:
                acc = lax.dot_general(a_ref[...], b_ref[:, n0:n0 + nc], (((0,), (0,)), ((), ())), preferred_element_type=f32)
            if epi == "relu2":
                up_ref, act_ref = rest
                up_ref[:, n0:n0 + nc] = acc
                r = jnp.maximum(acc, 0.0)
                act_ref[:, n0:n0 + nc] = (r * r).astype(bf16)
            elif epi == "relu2_bwd":
                up_ref, o_ref = rest
                o_ref[:, n0:n0 + nc] = (acc * (2.0 * jnp.maximum(up_ref[:, n0:n0 + nc], 0.0))).astype(bf16)
            else:
                (o_ref,) = rest
                o_ref[:, n0:n0 + nc] = acc.astype(out_dtype)

    if mode == "tn":
        a_spec = pl.BlockSpec((k, tm), lambda j, i: (0, i))
    else:
        a_spec = pl.BlockSpec((tm, k), lambda j, i: (i, 0))
    if mode == "nt" and shards > 1:
        b_spec = pl.BlockSpec((shards, tn, ks), lambda j, i: (0, j, 0))
    elif mode == "nt":
        b_spec = pl.BlockSpec((tn, k), lambda j, i: (j, 0))
    elif mode == "nn" and shards > 1:
        b_spec = pl.BlockSpec((None, k, tn), lambda j, i: (j // per, 0, j % per))
    else:
        b_spec = pl.BlockSpec((k, tn), lambda j, i: (0, j))
    o_spec = pl.BlockSpec((tm, tn), lambda j, i: (i, j))
    in_specs = [a_spec, b_spec]
    args = [a, b]
    if epi == "relu2":
        out_shape = (jax.ShapeDtypeStruct((m, n), f32), jax.ShapeDtypeStruct((m, n), bf16))
        out_specs = (o_spec, o_spec)
    elif epi == "relu2_bwd":
        in_specs.append(o_spec)
        args.append(extra)
        out_shape = jax.ShapeDtypeStruct((m, n), bf16)
        out_specs = o_spec
    elif mode == "tn" and shards > 1:
        out_shape = jax.ShapeDtypeStruct((shards, m, n // shards), out_dtype)
        out_specs = pl.BlockSpec((None, tm, tn), lambda j, i: (j // per, i, j % per))
    else:
        out_shape = jax.ShapeDtypeStruct((m, n), out_dtype)
        out_specs = o_spec
    single = not isinstance(out_shape, tuple)
    res, moved = _hosted(body, comm, name=name, grid=(n // tn, m // tm), in_specs=in_specs,
                         out_specs=(out_specs,) if single else out_specs, out_shape=(out_shape,) if single else out_shape, args=args)
    res = res[0] if single else res
    return res if comm is None else (res, moved)


def _row_spec(width, col=0):
    return pl.BlockSpec((TOK_BLK, width), lambda i: (i, col))


def _vec_spec(rows, width):
    return pl.BlockSpec((rows, width), lambda i: (0, 0))


def _ln_in(x, g, b):
    t, d = x.shape

    def body(x_ref, g_ref, b_ref, h_ref, hb_ref):
        h, _, _ = _ln_fwd(x_ref[...], g_ref[...], b_ref[...])
        h_ref[...] = h
        hb_ref[...] = h.astype(bf16)

    return pl.pallas_call(
        body, name="ln_in", grid=(t // TOK_BLK,),
        in_specs=[_row_spec(d), _vec_spec(1, d), _vec_spec(1, d)],
        out_specs=(_row_spec(d), _row_spec(d)),
        out_shape=(jax.ShapeDtypeStruct((t, d), f32), jax.ShapeDtypeStruct((t, d), bf16)),
        compiler_params=_params(("parallel",)),
    )(x, g, b)


def _attn_post(o_gdn, proj, o_fox, g_gdn, g_fox2, comm=None):
    t = o_gdn.shape[0]

    def body(og_ref, z_ref, of_ref, gg_ref, gf_ref, out_ref):
        for h in range(GDN_HEADS):
            sl = slice(h * LANES, (h + 1) * LANES)
            og = og_ref[:, sl]
            z = z_ref[:, sl]
            r = lax.rsqrt(jnp.mean(og * og, -1, keepdims=True) + NORM_EPS)
            out_ref[:, sl] = (og * r * gg_ref[...] * (z * _sigmoid(z))).astype(bf16)
        lo = _lane((TOK_BLK, LANES)) < FOX_DH
        for pr in range(FOX_HEADS // 2):
            sl = slice(pr * LANES, (pr + 1) * LANES)
            of = of_ref[:, sl]
            sq = of * of
            s0 = jnp.sum(jnp.where(lo, sq, 0.0), -1, keepdims=True)
            s1 = jnp.sum(jnp.where(lo, 0.0, sq), -1, keepdims=True)
            r = lax.rsqrt(jnp.where(lo, s0, s1) * (1.0 / FOX_DH) + NORM_EPS)
            out_ref[:, 512 + pr * LANES:512 + (pr + 1) * LANES] = (of * r * gf_ref[...]).astype(bf16)

    return _hosted(
        body, comm, name="attn_post", grid=(t // TOK_BLK,),
        in_specs=[_row_spec(512), _row_spec(512, OFF_Z // 512), _row_spec(512), _vec_spec(1, LANES), _vec_spec(1, LANES)],
        out_specs=(_row_spec(D_MODEL),),
        out_shape=(jax.ShapeDtypeStruct((t, D_MODEL), bf16),),
        args=(o_gdn, proj, o_fox, g_gdn, g_fox2))


def _attn_post_bwd(dattn, o_gdn, proj, o_fox, g_gdn, g_fox2):
    t = o_gdn.shape[0]

    def body(da_ref, og_ref, z_ref, of_ref, gg_ref, gf_ref, dog_ref, dz_ref, dof_ref, pg_ref):
        i = pl.program_id(0)

        @pl.when(i == 0)
        def _():
            pg_ref[...] = jnp.zeros_like(pg_ref)

        dgg = jnp.zeros((1, LANES), f32)
        for h in range(GDN_HEADS):
            sl = slice(h * LANES, (h + 1) * LANES)
            og = og_ref[:, sl]
            z = z_ref[:, sl]
            dout = da_ref[:, sl]
            g = gg_ref[...]
            r = lax.rsqrt(jnp.mean(og * og, -1, keepdims=True) + NORM_EPS)
            sg = _sigmoid(z)
            silu = z * sg
            ng = og * r * g
            dng = dout * silu
            dz_ref[:, sl] = (dout * ng * (sg * (1.0 + z * (1.0 - sg)))).astype(bf16)
            dgg = dgg + jnp.sum(dng * og * r, 0, keepdims=True)
            gd = dng * g
            dog_ref[:, sl] = r * gd - og * (r * r * r) * jnp.mean(og * gd, -1, keepdims=True)
        pg_ref[0:1, :] += dgg
        lo = _lane((TOK_BLK, LANES)) < FOX_DH
        dgf = jnp.zeros((1, LANES), f32)
        for pr in range(FOX_HEADS // 2):
            sl = slice(pr * LANES, (pr + 1) * LANES)
            of = of_ref[:, sl]
            dout = da_ref[:, 512 + pr * LANES:512 + (pr + 1) * LANES]
            g = gf_ref[...]
            sq = of * of
            s0 = jnp.sum(jnp.where(lo, sq, 0.0), -1, keepdims=True)
            s1 = jnp.sum(jnp.where(lo, 0.0, sq), -1, keepdims=True)
            r = lax.rsqrt(jnp.where(lo, s0, s1) * (1.0 / FOX_DH) + NORM_EPS)
            dgf = dgf + jnp.sum(dout * of * r, 0, keepdims=True)
            gd = dout * g
            xg = of * gd
            m0 = jnp.sum(jnp.where(lo, xg, 0.0), -1, keepdims=True)
            m1 = jnp.sum(jnp.where(lo, 0.0, xg), -1, keepdims=True)
            dof_ref[:, sl] = r * gd - of * (r * r * r) * (jnp.where(lo, m0, m1) * (1.0 / FOX_DH))
        pg_ref[1:2, :] += dgf

    return pl.pallas_call(
        body, name="attn_post_bwd", grid=(t // TOK_BLK,),
        in_specs=[_row_spec(D_MODEL), _row_spec(512), _row_spec(512, OFF_Z // 512), _row_spec(512), _vec_spec(1, LANES), _vec_spec(1, LANES)],
        out_specs=(_row_spec(512), _row_spec(512), _row_spec(512), _vec_spec(8, LANES)),
        out_shape=(jax.ShapeDtypeStruct((t, 512), f32), jax.ShapeDtypeStruct((t, 512), bf16),
                   jax.ShapeDtypeStruct((t, 512), f32), jax.ShapeDtypeStruct((8, LANES), f32)),
        compiler_params=_params(("arbitrary",)),
    )(dattn, o_gdn, proj, o_fox, g_gdn, g_fox2)


def _ln1(h0, mix, g, b):
    t, d = h0.shape

    def body(h0_ref, mix_ref, g_ref, b_ref, h_ref, hb_ref, xh_ref, rs_ref):
        h, xhat, rstd = _ln_fwd(ALPHA * h0_ref[...] + mix_ref[...], g_ref[...], b_ref[...])
        h_ref[...] = h
        hb_ref[...] = h.astype(bf16)
        xh_ref[...] = xhat
        rs_ref[...] = jnp.broadcast_to(rstd, rs_ref.shape)

    return pl.pallas_call(
        body, name="ln1", grid=(t // TOK_BLK,),
        in_specs=[_row_spec(d), _row_spec(d), _vec_spec(1, d), _vec_spec(1, d)],
        out_specs=(_row_spec(d), _row_spec(d), _row_spec(d), _row_spec(LANES)),
        out_shape=(jax.ShapeDtypeStruct((t, d), f32), jax.ShapeDtypeStruct((t, d), bf16),
                   jax.ShapeDtypeStruct((t, d), f32), jax.ShapeDtypeStruct((t, LANES), f32)),
        compiler_params=_params(("parallel",)),
    )(h0, mix, g, b)


def _ln2_loss(h1, ff, pe, gp, b_gate, g, b, target):
    t, d = h1.shape

    def body(h1_ref, ff_ref, pe_ref, gp_ref, bg_ref, g_ref, b_ref, t_ref, dr_ref, drb_ref, dpe_ref, dgp_ref, pg_ref):
        i = pl.program_id(0)

        @pl.when(i == 0)
        def _():
            pg_ref[...] = jnp.zeros_like(pg_ref)

        sig = _sigmoid(gp_ref[...] + bg_ref[...])
        pe = pe_ref[...]
        r2 = ALPHA * h1_ref[...] + ff_ref[...] + pe * sig
        y, xhat, rstd = _ln_fwd(r2, g_ref[...], b_ref[...])
        err = y - t_ref[...]
        dy = err * (1.0 / d)
        dr = _ln_bwd(dy, xhat, rstd, g_ref[...])
        dr_ref[...] = dr
        drb_ref[...] = dr.astype(bf16)
        dpe_ref[...] = (dr * sig).astype(bf16)
        dgp = dr * pe * sig * (1.0 - sig)
        dgp_ref[...] = dgp.astype(bf16)
        pg_ref[0:1, :] += jnp.sum(dy * xhat, 0, keepdims=True)
        pg_ref[1:2, :] += jnp.sum(dy, 0, keepdims=True)
        pg_ref[2:3, :] += jnp.sum(dgp, 0, keepdims=True)
        pg_ref[3:4, :] += 0.5 * jnp.sum(jnp.mean(err * err, -1, keepdims=True), 0, keepdims=True)

    return pl.pallas_call(
        body, name="ln2_loss", grid=(t // TOK_BLK,),
        in_specs=[_row_spec(d)] * 4 + [_vec_spec(1, d)] * 3 + [_row_spec(d)],
        out_specs=(_row_spec(d), _row_spec(d), _row_spec(d), _row_spec(d), _vec_spec(8, d)),
        out_shape=(jax.ShapeDtypeStruct((t, d), f32), jax.ShapeDtypeStruct((t, d), bf16), jax.ShapeDtypeStruct((t, d), bf16),
                   jax.ShapeDtypeStruct((t, d), bf16), jax.ShapeDtypeStruct((8, d), f32)),
        compiler_params=_params(("arbitrary",)),
    )(h1, ff, pe, gp, b_gate, g, b, target)


def _ln1_bwd(dr2, da, db, xhat, rstd, g):
    t, d = dr2.shape

    def body(dr2_ref, da_ref, db_ref, xh_ref, rs_ref, g_ref, dr_ref, drb_ref, pg_ref):
        i = pl.program_id(0)

        @pl.when(i == 0)
        def _():
            pg_ref[...] = jnp.zeros_like(pg_ref)

        dh = ALPHA * dr2_ref[...] + da_ref[...] + db_ref[...]
        xhat = xh_ref[...]
        dr = _ln_bwd(dh, xhat, rs_ref[:, 0:1], g_ref[...])
        dr_ref[...] = dr
        drb_ref[...] = dr.astype(bf16)
        pg_ref[0:1, :] += jnp.sum(dh * xhat, 0, keepdims=True)
        pg_ref[1:2, :] += jnp.sum(dh, 0, keepdims=True)

    return pl.pallas_call(
        body, name="ln1_bwd", grid=(t // TOK_BLK,),
        in_specs=[_row_spec(d)] * 4 + [_row_spec(LANES), _vec_spec(1, d)],
        out_specs=(_row_spec(d), _row_spec(d), _vec_spec(8, d)),
        out_shape=(jax.ShapeDtypeStruct((t, d), f32), jax.ShapeDtypeStruct((t, d), bf16), jax.ShapeDtypeStruct((8, d), f32)),
        compiler_params=_params(("arbitrary",)),
    )(dr2, da, db, xhat, rstd, g)


def _ln_in_bwd(x, dr1, dmm, g, comm=None):
    t, d = x.shape

    def body(x_ref, dr1_ref, dmm_ref, g_ref, dx_ref, pg_ref):
        i = pl.program_id(0)

        @pl.when(i == 0)
        def _():
            pg_ref[...] = jnp.zeros_like(pg_ref)

        dh = ALPHA * dr1_ref[...] + dmm_ref[...]
        _, xhat, rstd = _ln_fwd(x_ref[...], g_ref[...], 0.0)
        dx_ref[...] = _ln_bwd(dh, xhat, rstd, g_ref[...])
        pg_ref[0:1, :] += jnp.sum(dh * xhat, 0, keepdims=True)
        pg_ref[1:2, :] += jnp.sum(dh, 0, keepdims=True)

    return _hosted(
        body, comm, name="ln_in_bwd", grid=(t // TOK_BLK,),
        in_specs=[_row_spec(d)] * 3 + [_vec_spec(1, d)],
        out_specs=(_row_spec(d), _vec_spec(8, d)),
        out_shape=(jax.ShapeDtypeStruct((t, d), f32), jax.ShapeDtypeStruct((8, d), f32)),
        args=(x, dr1, dmm, g))


def _tri(n, upper=False, strict=False):
    r = lax.broadcasted_iota(jnp.int32, (n, n), 0)
    c = lax.broadcasted_iota(jnp.int32, (n, n), 1)
    if upper:
        m = (c > r) if strict else (c >= r)
    else:
        m = (c < r) if strict else (c <= r)
    return jnp.where(m, 1.0, 0.0).astype(f32)


def _gate_values(x, bias, alog, lane):
    z = x + bias
    return jnp.where(lane < 4, _sigmoid(z), jnp.where(lane < 8, -jnp.exp(alog) * _softplus(z), jnp.where(lane < 16, -_softplus(-z), 0.0)))


def _gates(proj, bias_row, alog_row):
    t = proj.shape[0]
    nch = t // CHUNK

    def body(x_ref, bias_ref, alog_ref, gates_ref, gcum_ref, gcumt_ref):
        lane = _lane((t, LANES))
        gates = _gate_values(x_ref[...], bias_ref[...], alog_ref[...], lane)
        gates_ref[...] = gates
        g3 = gates.reshape(nch, CHUNK, LANES)
        tri = jnp.broadcast_to(_tri(CHUNK)[None], (nch, CHUNK, CHUNK))
        loc = jnp.einsum("bij,bjk->bik", tri, g3, precision=HI, preferred_element_type=f32)
        tot = jnp.sum(g3, axis=1)
        offs = _dot(_tri(nch, strict=True), tot)
        glob = loc + offs[:, None, :]
        lane3 = _lane((nch, CHUNK, LANES))
        gcum = jnp.where(lane3 < 4, g3, jnp.where(lane3 < 8, loc, glob)).reshape(t, LANES)
        gcum_ref[...] = gcum
        gcumt_ref[...] = gcum.T

    return pl.pallas_call(
        body, name="gates", grid=(1,),
        in_specs=[pl.BlockSpec((t, LANES), lambda i: (0, SEG_SMALL // LANES)), _vec_spec(1, LANES), _vec_spec(1, LANES)],
        out_specs=(pl.BlockSpec((t, LANES), lambda i: (0, 0)), pl.BlockSpec((t, LANES), lambda i: (0, 0)),
                   pl.BlockSpec((LANES, t), lambda i: (0, 0))),
        out_shape=(jax.ShapeDtypeStruct((t, LANES), f32), jax.ShapeDtypeStruct((t, LANES), f32), jax.ShapeDtypeStruct((LANES, t), f32)),
        compiler_params=_params(("arbitrary",)),
    )(proj, bias_row, alog_row)


def _gates_bwd(proj, bias_row, alog_row, gates, dgates, dccol, dct):
    t = proj.shape[0]
    nch = t // CHUNK

    def body(x_ref, bias_ref, alog_ref, gates_ref, dg_ref, dcc_ref, dct_ref, dx_ref, pg_ref):
        lane = _lane((t, LANES))
        d = dg_ref[...] + dcc_ref[...] + dct_ref[...].T
        d3 = d.reshape(nch, CHUNK, LANES)
        tri = jnp.broadcast_to(_tri(CHUNK, upper=True)[None], (nch, CHUNK, CHUNK))
        loc = jnp.einsum("bij,bjk->bik", tri, d3, precision=HI, preferred_element_type=f32)
        tot = jnp.sum(d3, axis=1)
        offs = _dot(_tri(nch, upper=True, strict=True), tot)
        glob = loc + offs[:, None, :]
        lane3 = _lane((nch, CHUNK, LANES))
        dpre = jnp.where(lane3 < 4, d3, jnp.where(lane3 < 8, loc, glob)).reshape(t, LANES)
        z = x_ref[...] + bias_ref[...]
        sg = _sigmoid(z)
        dx = jnp.where(lane < 4, dpre * sg * (1.0 - sg),
                       jnp.where(lane < 8, dpre * (-jnp.exp(alog_ref[...])) * sg, jnp.where(lane < 16, dpre * (1.0 - sg), 0.0)))
        dx_ref[...] = dx.astype(bf16)
        pg_ref[...] = jnp.zeros_like(pg_ref)
        pg_ref[0:1, :] = jnp.sum(dx, 0, keepdims=True)
        pg_ref[1:2, :] = jnp.sum(jnp.where((lane >= 4) & (lane < 8), dpre * gates_ref[...], 0.0), 0, keepdims=True)

    full = pl.BlockSpec((t, LANES), lambda i: (0, 0))
    return pl.pallas_call(
        body, name="gates_bwd", grid=(1,),
        in_specs=[pl.BlockSpec((t, LANES), lambda i: (0, SEG_SMALL // LANES)), _vec_spec(1, LANES), _vec_spec(1, LANES),
                  full, full, full, pl.BlockSpec((LANES, t), lambda i: (0, 0))],
        out_specs=(full, _vec_spec(8, LANES)),
        out_shape=(jax.ShapeDtypeStruct((t, LANES), bf16), jax.ShapeDtypeStruct((8, LANES), f32)),
        compiler_params=_params(("arbitrary",)),
    )(proj, bias_row, alog_row, gates, dgates, dccol, dct)


def _conv_act(u, cw, row, t):
    c = cw[3:4, :] * u
    for jj in range(CONV_W - 1):
        sh = CONV_W - 1 - jj
        c = c + cw[jj:jj + 1, :] * jnp.where(row >= sh, pltpu.roll(u, sh, axis=0), 0.0)
    return c


def _gdn_conv(proj, conv_w, comm=None):
    t = proj.shape[0]
    nblk = GDN_QKV // LANES

    def body(u_ref, cw_ref, c_ref, y_ref):
        j = pl.program_id(0)
        row = lax.broadcasted_iota(jnp.int32, (t, LANES), 0)
        c = _conv_act(u_ref[...], cw_ref[...], row, t)
        c_ref[...] = c
        s = c * _sigmoid(c)
        r = lax.rsqrt(jnp.sum(s * s, -1, keepdims=True) + NORM_EPS)
        scale = jnp.where(j < GDN_HEADS, GDN_DK ** -0.5, 1.0)
        y_ref[...] = jnp.where(j < 2 * GDN_HEADS, s * (r * scale), s)

    blk = pl.BlockSpec((t, LANES), lambda j: (0, j))
    return _hosted(
        body, comm, name="gdn_conv", grid=(nblk,),
        in_specs=[blk, pl.BlockSpec((CONV_W, LANES), lambda j: (0, j))],
        out_specs=(blk, blk),
        out_shape=(jax.ShapeDtypeStruct((t, GDN_QKV), f32), jax.ShapeDtypeStruct((t, GDN_QKV), f32)),
        args=(proj, conv_w))


def _gdn_conv_bwd(proj, conv_w, c, dy, comm=None):
    t = proj.shape[0]
    nblk = GDN_QKV // LANES

    def body(u_ref, cw_ref, c_ref, dy_ref, du_ref, dcw_ref):
        j = pl.program_id(0)
        row = lax.broadcasted_iota(jnp.int32, (t, LANES), 0)
        u = u_ref[...]
        cw = cw_ref[...]
        c = c_ref[...]
        dy = dy_ref[...]
        sg = _sigmoid(c)
        s = c * sg
        r = lax.rsqrt(jnp.sum(s * s, -1, keepdims=True) + NORM_EPS)
        n = s * r
        scale = jnp.where(j < GDN_HEADS, GDN_DK ** -0.5, 1.0)
        dn = dy * scale
        ds = jnp.where(j < 2 * GDN_HEADS, r * (dn - n * jnp.sum(dn * n, -1, keepdims=True)), dy)
        dc = ds * (sg * (1.0 + c * (1.0 - sg)))
        du = cw[3:4, :] * dc
        dcw_ref[...] = jnp.zeros_like(dcw_ref)
        dcw_ref[3:4, :] = jnp.sum(dc * u, 0, keepdims=True)
        for jj in range(CONV_W - 1):
            sh = CONV_W - 1 - jj
            du = du + cw[jj:jj + 1, :] * jnp.where(row < t - sh, pltpu.roll(dc, t - sh, axis=0), 0.0)
            dcw_ref[jj:jj + 1, :] = jnp.sum(dc * jnp.where(row >= sh, pltpu.roll(u, sh, axis=0), 0.0), 0, keepdims=True)
        du_ref[...] = du.astype(bf16)

    blk = pl.BlockSpec((t, LANES), lambda j: (0, j))
    return _hosted(
        body, comm, name="gdn_conv_bwd", grid=(nblk,),
        in_specs=[blk, pl.BlockSpec((CONV_W, LANES), lambda j: (0, j)), blk, blk],
        out_specs=(blk, pl.BlockSpec((8, LANES), lambda j: (0, j))),
        out_shape=(jax.ShapeDtypeStruct((t, GDN_QKV), bf16), jax.ShapeDtypeStruct((8, GDN_QKV), f32)),
        args=(proj, conv_w, c, dy))


def _chunk_masks():
    r = lax.broadcasted_iota(jnp.int32, (CHUNK, CHUNK), 0)
    c = lax.broadcasted_iota(jnp.int32, (CHUNK, CHUNK), 1)
    return r >= c, r > c, r == c


def _col_to_row(col, eye):
    return jnp.sum(jnp.where(eye, col, 0.0), axis=0, keepdims=True)


def _row_to_col(row, eye):
    return jnp.sum(jnp.where(eye, row, 0.0), axis=1, keepdims=True)


NN = (((1,), (0,)), ((), ()))
NT = (((1,), (1,)), ((), ()))
TN = (((0,), (0,)), ((), ()))
GDN_GROUP = 4


def _mx(a, b, dims=NN, passes=1):
    d = lambda p, q: lax.dot_general(p, q, dims, preferred_element_type=f32)
    ah, bh = a.astype(bf16), b.astype(bf16)
    if passes == 1:
        return d(ah, bh)
    al = (a - ah.astype(f32)).astype(bf16)
    bl = (b - bh.astype(f32)).astype(bf16)
    return d(ah, bh) + (d(ah, bl) + d(al, bh))


def _gdn_decay(gam, masks):
    causal, _, eye = masks
    return jnp.exp(jnp.where(causal, gam - _col_to_row(gam, eye), NEG))


def _gdn_local(y, gcum, comm=None):
    t = y.shape[0]
    nch = t // CHUNK
    rows_blk = GDN_GROUP * CHUNK

    def body(y_ref, g_ref, u_ref, w_ref, qk_ref, tinv_ref):
        masks = _chunk_masks()
        _, strict, eye = masks
        ids = [(j, h) for j in range(GDN_GROUP) for h in range(GDN_HEADS)]
        rs = lambda j: slice(j * CHUNK, (j + 1) * CHUNK)
        col = lambda base, h: slice(base + h * LANES, base + (h + 1) * LANES)
        kn = [y_ref[rs(j), col(512, h)] for j, h in ids]
        beta = [g_ref[rs(j), h:h + 1] for j, h in ids]
        gam = [g_ref[rs(j), 4 + h:5 + h] for j, h in ids]
        dec = [_gdn_decay(g, masks) for g in gam]
        x = [-jnp.where(strict, _mx(k, k, NT) * d * b, 0.0) for k, d, b in zip(kn, dec, beta)]
        tinv = [jnp.where(eye, 1.0, 0.0) + a for a in x]
        for _ in range(5):
            x = [_mx(a, a, NN, 3) for a in x]
            tinv = [t_ + _mx(t_, a, NN, 3) for t_, a in zip(tinv, x)]
        for (j, h), t_, k, d, b, g in zip(ids, tinv, kn, dec, beta, gam):
            u_ref[rs(j), col(0, h)] = _mx(t_, b * y_ref[rs(j), col(1024, h)])
            w_ref[rs(j), col(0, h)] = _mx(t_, (b * jnp.exp(g)) * k)
            qk_ref[j, h] = _mx(y_ref[rs(j), col(0, h)], k, NT) * d
            tinv_ref[j, h] = t_

    mat = pl.BlockSpec((GDN_GROUP, GDN_HEADS, CHUNK, CHUNK), lambda n: (n, 0, 0, 0))
    return _hosted(
        body, comm, name="gdn_local", grid=(nch // GDN_GROUP,),
        in_specs=[pl.BlockSpec((rows_blk, GDN_QKV), lambda n: (n, 0)), pl.BlockSpec((rows_blk, LANES), lambda n: (n, 0))],
        out_specs=(pl.BlockSpec((rows_blk, 512), lambda n: (n, 0)), pl.BlockSpec((rows_blk, 512), lambda n: (n, 0)), mat, mat),
        out_shape=(jax.ShapeDtypeStruct((t, 512), f32), jax.ShapeDtypeStruct((t, 512), f32),
                   jax.ShapeDtypeStruct((nch, GDN_HEADS, CHUNK, CHUNK), f32), jax.ShapeDtypeStruct((nch, GDN_HEADS, CHUNK, CHUNK), f32)),
        args=(y, gcum))


def _gdn_fwd(y, gcum, u, w, qk, comm=None):
    t = y.shape[0]
    nch = t // CHUNK

    def body(y_ref, g_ref, u_ref, w_ref, qk_ref, o_ref, sall_ref, s_ref):
        @pl.when(pl.program_id(0) == 0)
        def _():
            s_ref[...] = jnp.zeros_like(s_ref)

        heads = range(GDN_HEADS)
        sl = [slice(h * LANES, (h + 1) * LANES) for h in heads]
        gam = [g_ref[:, 4 + h:5 + h] for h in heads]
        gam_last = [g[CHUNK - 1:CHUNK, :] for g in gam]
        s = [s_ref[h] for h in heads]
        for h in heads:
            sall_ref[0, h] = s[h]
        ws = [_mx(w_ref[:, sl[h]], s[h]) for h in heads]
        qs = [_mx(y_ref[:, sl[h]] * jnp.exp(gam[h]), s[h]) for h in heads]
        vn = [u_ref[:, sl[h]] - ws[h] for h in heads]
        av = [_mx(qk_ref[0, h], vn[h]) for h in heads]
        kv = [_mx(y_ref[:, 512 + h * LANES:512 + (h + 1) * LANES] * jnp.exp(gam_last[h] - gam[h]), vn[h], TN) for h in heads]
        for h in heads:
            o_ref[:, sl[h]] = qs[h] + av[h]
            s_ref[h] = jnp.exp(gam_last[h]) * s[h] + kv[h]

    row = lambda width: pl.BlockSpec((CHUNK, width), lambda n: (n, 0))
    return _hosted(
        body, comm, name="gdn_fwd", grid=(nch,),
        in_specs=[row(GDN_QKV), row(LANES), row(512), row(512), pl.BlockSpec((1, GDN_HEADS, CHUNK, CHUNK), lambda n: (n, 0, 0, 0))],
        out_specs=(row(512), pl.BlockSpec((1, GDN_HEADS, LANES, LANES), lambda n: (n, 0, 0, 0))),
        out_shape=(jax.ShapeDtypeStruct((t, 512), f32), jax.ShapeDtypeStruct((nch, GDN_HEADS, LANES, LANES), f32)),
        scratch_shapes=[pltpu.VMEM((GDN_HEADS, LANES, LANES), f32)],
        args=(y, gcum, u, w, qk))


def _gdn_bwd(y, gcum, u_all, w_all, qk_all, tinv_all, sall, do, comm=None):
    t = y.shape[0]
    nch = t // CHUNK

    def body(y_ref, g_ref, u_ref, w_ref, qk_ref, tinv_ref, sall_ref, do_ref, dy_ref, dg_ref, ds_ref):
        @pl.when(pl.program_id(0) == 0)
        def _():
            ds_ref[...] = jnp.zeros_like(ds_ref)

        masks = _chunk_masks()
        causal, strict, eye = masks
        lane = _lane((CHUNK, LANES))
        row = lax.broadcasted_iota(jnp.int32, (CHUNK, 1), 0)
        heads = range(GDN_HEADS)
        each = lambda f, *ls: [f(*a) for a in zip(*ls)]
        rsum = lambda a: jnp.sum(a, axis=1, keepdims=True)
        sl = [slice(h * LANES, (h + 1) * LANES) for h in heads]
        qn = [y_ref[:, sl[h]] for h in heads]
        kn = [y_ref[:, 512 + h * LANES:512 + (h + 1) * LANES] for h in heads]
        v = [y_ref[:, 1024 + h * LANES:1024 + (h + 1) * LANES] for h in heads]
        beta = [g_ref[:, h:h + 1] for h in heads]
        gam = [g_ref[:, 4 + h:5 + h] for h in heads]
        gam_last = [g[CHUNK - 1:CHUNK, :] for g in gam]
        dec = [_gdn_decay(g, masks) for g in gam]
        e = [jnp.exp(g) for g in gam]
        f = each(lambda gl_, g: jnp.exp(gl_ - g), gam_last, gam)
        gl = [jnp.exp(g) for g in gam_last]
        u = [u_ref[:, sl[h]] for h in heads]
        w = [w_ref[:, sl[h]] for h in heads]
        qk = [qk_ref[0, h] for h in heads]
        tinv = [tinv_ref[0, h] for h in heads]
        s = [sall_ref[0, h] for h in heads]
        dsn = [ds_ref[h] for h in heads]
        d_o = [do_ref[:, sl[h]] for h in heads]
        qd = each(lambda a, b: a * b, qn, e)
        kd = each(lambda a, b: a * b, kn, f)
        ws = each(_mx, w, s)
        kds = each(_mx, kd, dsn)
        qkdo = each(lambda a, b: _mx(a, b, TN), qk, d_o)
        dqd = each(lambda a, b: _mx(a, b, NT), d_o, s)
        qddo = each(lambda a, b: _mx(a, b, TN), qd, d_o)
        kkd = each(lambda k, d: _mx(k, k, NT) * d, kn, dec)
        vn = each(lambda a, b: a - b, u, ws)
        dvn = each(lambda a, b: a + b, qkdo, kds)
        dqk = each(lambda a, b: jnp.where(causal, _mx(a, b, NT), 0.0), d_o, vn)
        dkd = each(lambda a, b: _mx(a, b, NT), vn, dsn)
        dw = each(lambda a, b: -_mx(a, b, NT), dvn, s)
        wdvn = each(lambda a, b: _mx(a, b, TN), w, dvn)
        dgl = each(lambda a, b: jnp.sum(rsum(a * b), axis=0, keepdims=True), dsn, s)
        for h in heads:
            ds_ref[h] = qddo[h] - wdvn[h] + gl[h] * dsn[h]
        dru = each(lambda a, b: _mx(a, b, TN), tinv, dvn)
        drw = each(lambda a, b: _mx(a, b, TN), tinv, dw)
        dqkr = each(lambda a, b: a * b, dqk, dec)
        dq1 = each(_mx, dqkr, kn)
        dk1 = each(lambda a, b: _mx(a, b, TN), dqkr, qn)
        dnu = each(lambda a, b: _mx(a, b, NT), dru, u)
        dnw = each(lambda a, b: _mx(a, b, NT), drw, w)
        dn = each(lambda a, b: jnp.where(strict, -(a + b), 0.0), dnu, dnw)
        dkk = each(lambda a, b, d: a * b * d, dn, beta, dec)
        dk2 = each(_mx, dkk, kn)
        dk3 = each(lambda a, b: _mx(a, b, TN), dkk, kn)
        dgates = jnp.zeros((CHUNK, LANES), f32)
        for h in heads:
            drw_k = rsum(drw[h] * kn[h])
            dbeta = rsum(dru[h] * v[h]) + e[h] * drw_k + rsum(dn[h] * kkd[h])
            m = dn[h] * (kkd[h] * beta[h]) + dqk[h] * qk[h]
            de = beta[h] * drw_k + rsum(dqd[h] * qn[h])
            df = rsum(dkd[h] * kn[h])
            dgam = rsum(m) - _row_to_col(jnp.sum(m, axis=0, keepdims=True), eye) + de * e[h] - df * f[h]
            dgam_last = jnp.sum(df * f[h], axis=0, keepdims=True) + dgl[h] * gl[h]
            dgam = dgam + jnp.where(row == CHUNK - 1, dgam_last, 0.0)
            dy_ref[:, sl[h]] = dq1[h] + dqd[h] * e[h]
            dy_ref[:, 512 + h * LANES:512 + (h + 1) * LANES] = (beta[h] * e[h]) * drw[h] + dk2[h] + dk3[h] + dk1[h] + dkd[h] * f[h]
            dy_ref[:, 1024 + h * LANES:1024 + (h + 1) * LANES] = beta[h] * dru[h]
            dgates = dgates + jnp.where(lane == h, dbeta, 0.0) + jnp.where(lane == 4 + h, dgam, 0.0)
        dg_ref[...] = dgates

    rev = lambda width: pl.BlockSpec((CHUNK, width), lambda n: (nch - 1 - n, 0))
    mat = lambda d: pl.BlockSpec((1, GDN_HEADS, d, d), lambda n: (nch - 1 - n, 0, 0, 0))
    return _hosted(
        body, comm, name="gdn_bwd", grid=(nch,),
        in_specs=[rev(GDN_QKV), rev(LANES), rev(512), rev(512), mat(CHUNK), mat(CHUNK), mat(LANES), rev(512)],
        out_specs=(rev(GDN_QKV), rev(LANES)),
        out_shape=(jax.ShapeDtypeStruct((t, GDN_QKV), f32), jax.ShapeDtypeStruct((t, LANES), f32)),
        scratch_shapes=[pltpu.VMEM((GDN_HEADS, LANES, LANES), f32)],
        args=(y, gcum, u_all, w_all, qk_all, tinv_all, sall, do))


FOX_CLASSES = 4


def _fox_groups(t):
    nq = t // FOX_BQ
    ncls = min(FOX_CLASSES, nq)
    per = nq // ncls
    return [(g * per, per, (g + 1) * per * FOX_BQ) for g in range(ncls)]


def _fox_scores(q_ref, k_ref, gcum_ref, gcumt_ref, h, i, keys):
    pr = h // 2
    lo = (h % 2) * FOX_DH
    lane = _lane((FOX_BQ, LANES))
    mask = (lane >= lo) & (lane < lo + FOX_DH)
    qm = jnp.where(mask, q_ref[:, pr * LANES:(pr + 1) * LANES], 0.0).astype(bf16)
    kp = k_ref[:, pr * LANES:(pr + 1) * LANES].astype(bf16)
    s = _dot_nt(qm, kp, None) * (FOX_DH ** -0.5)
    s = s + gcum_ref[:, 8 + h:9 + h] - gcumt_ref[8 + h:9 + h, :]
    rows = i * FOX_BQ + lax.broadcasted_iota(jnp.int32, (FOX_BQ, keys), 0)
    cols = lax.broadcasted_iota(jnp.int32, (FOX_BQ, keys), 1)
    return jnp.where(cols <= rows, s, NEG), mask, qm, kp


def _fox_fwd(proj, gcum, gcumt, ride=None):
    c0 = SEG_FOX // 512

    def group_call(q0, nq, keys, comm):
        def body(q_ref, k_ref, v_ref, gcum_ref, gcumt_ref, o_ref, lse_ref):
            i = q0 + pl.program_id(0)
            lane = _lane((FOX_BQ, LANES))
            lse_all = jnp.zeros((FOX_BQ, LANES), f32)
            for pr in range(FOX_HEADS // 2):
                vp = v_ref[:, pr * LANES:(pr + 1) * LANES].astype(bf16)
                o_pair = jnp.zeros((FOX_BQ, LANES), f32)
                for h in (2 * pr, 2 * pr + 1):
                    s, mask, _, _ = _fox_scores(q_ref, k_ref, gcum_ref, gcumt_ref, h, i, keys)
                    m = jnp.max(s, axis=1, keepdims=True)
                    p = jnp.exp(s - m)
                    l = jnp.sum(p, axis=1, keepdims=True)
                    o_h = _dot((p * (1.0 / l)).astype(bf16), vp, None)
                    o_pair = jnp.where(mask, o_h, o_pair)
                    lse_all = jnp.where(lane == h, m + jnp.log(l), lse_all)
                o_ref[:, pr * LANES:(pr + 1) * LANES] = o_pair
            lse_ref[...] = lse_all

        seen = lambda col: pl.BlockSpec((keys, 512), lambda i: (0, col))
        return _hosted(
            body, comm, name=f"fox_fwd_{keys}", grid=(nq,),
            in_specs=[pl.BlockSpec((FOX_BQ, 512), lambda i: (q0 + i, c0)), seen(c0 + 1), seen(c0 + 2),
                      pl.BlockSpec((FOX_BQ, LANES), lambda i: (q0 + i, 0)), pl.BlockSpec((LANES, keys), lambda i: (0, 0))],
            out_specs=(pl.BlockSpec((FOX_BQ, 512), lambda i: (i, 0)), pl.BlockSpec((FOX_BQ, LANES), lambda i: (i, 0))),
            out_shape=(jax.ShapeDtypeStruct((nq * FOX_BQ, 512), f32), jax.ShapeDtypeStruct((nq * FOX_BQ, LANES), f32)),
            args=(proj, proj, proj, gcum, gcumt))

    parts = []
    for n, g in enumerate(_fox_groups(proj.shape[0])):
        hook = ride(n) if ride else None
        part, moved = group_call(*g, hook[0] if hook else None)
        parts.append(part)
        if hook:
            hook[1](moved)
    return jnp.concatenate([o for o, _ in parts], axis=0), jnp.concatenate([l for _, l in parts], axis=0)


def _fox_bwd(proj, gcum, gcumt, o, lse, do, ride=None):
    t = proj.shape[0]
    c0 = SEG_FOX // 512

    def group_call(q0, nq, keys, acc, comm):
        first = acc is None

        def body(q_ref, k_ref, v_ref, gcum_ref, gcumt_ref, o_ref, lse_ref, do_ref, *rest):
            dq_ref, dk_ref, dv_ref, dcc_ref, dct_ref = rest[-5:]
            j = pl.program_id(0)
            i = q0 + j

            @pl.when(j == 0)
            def _():
                if first:
                    dk_ref[...] = jnp.zeros_like(dk_ref)
                    dv_ref[...] = jnp.zeros_like(dv_ref)
                    dct_ref[...] = jnp.zeros_like(dct_ref)
                else:
                    dk_ref[...], dv_ref[...], dct_ref[...] = rest[0][...], rest[1][...], rest[2][...]

            lane = _lane((FOX_BQ, LANES))
            dcc = jnp.zeros((FOX_BQ, LANES), f32)
            scale = FOX_DH ** -0.5
            for pr in range(FOX_HEADS // 2):
                sl = slice(pr * LANES, (pr + 1) * LANES)
                vp = v_ref[:, sl].astype(bf16)
                dq_pair = jnp.zeros((FOX_BQ, LANES), f32)
                for h in (2 * pr, 2 * pr + 1):
                    s, mask, qm, kp = _fox_scores(q_ref, k_ref, gcum_ref, gcumt_ref, h, i, keys)
                    p = jnp.exp(s - lse_ref[:, h:h + 1])
                    dom = jnp.where(mask, do_ref[:, sl], 0.0)
                    delta = jnp.sum(dom * o_ref[:, sl], axis=1, keepdims=True)
                    domb = dom.astype(bf16)
                    ds = p * (_dot_nt(domb, vp, None) - delta)
                    dsb = ds.astype(bf16)
                    dv_ref[:, sl] += _dot_tn(p.astype(bf16), domb, None)
                    dk_ref[:, sl] += _dot_tn(dsb, qm, None) * scale
                    dq_pair = jnp.where(mask, _dot(dsb, kp, None) * scale, dq_pair)
                    dcc = jnp.where(lane == 8 + h, jnp.sum(ds, axis=1, keepdims=True), dcc)
                    dct_ref[8 + h:9 + h, :] += -jnp.sum(ds, axis=0, keepdims=True)
                dq_ref[:, sl] = dq_pair.astype(bf16)
            dcc_ref[...] = dcc

        qblk = lambda col: pl.BlockSpec((FOX_BQ, 512), lambda i: (q0 + i, col))
        oblk = pl.BlockSpec((FOX_BQ, 512), lambda i: (i, 0))
        seen = lambda col: pl.BlockSpec((keys, 512), lambda i: (0, col))
        rblk = pl.BlockSpec((FOX_BQ, LANES), lambda i: (q0 + i, 0))
        seen_t = pl.BlockSpec((LANES, keys), lambda i: (0, 0))
        in_specs = [qblk(c0), seen(c0 + 1), seen(c0 + 2), rblk, seen_t, qblk(0), rblk, qblk(0)]
        args = [proj, proj, proj, gcum, gcumt, o, lse, do]
        aliases = {}
        if not first:
            in_specs += [seen(0), seen(0), seen_t]
            args += list(acc)
            aliases = {8: 1, 9: 2, 10: 4}
        return _hosted(
            body, comm, name=f"fox_bwd_{keys}", grid=(nq,), in_specs=in_specs,
            out_specs=(oblk, seen(0), seen(0), pl.BlockSpec((FOX_BQ, LANES), lambda i: (i, 0)), seen_t),
            out_shape=(jax.ShapeDtypeStruct((nq * FOX_BQ, 512), bf16), jax.ShapeDtypeStruct((t, 512), f32), jax.ShapeDtypeStruct((t, 512), f32),
                       jax.ShapeDtypeStruct((nq * FOX_BQ, LANES), f32), jax.ShapeDtypeStruct((LANES, t), f32)),
            aliases=aliases, args=args)

    acc, dqs, dccs = None, [], []
    for n, g in enumerate(reversed(_fox_groups(t))):
        hook = ride(n) if ride else None
        (dq, dk, dv, dcc, dct), moved = group_call(*g, acc, hook[0] if hook else None)
        if hook:
            hook[1](moved)
        acc = (dk, dv, dct)
        dqs.insert(0, dq)
        dccs.insert(0, dcc)
    return jnp.concatenate(dqs, axis=0), acc[0], acc[1], jnp.concatenate(dccs, axis=0), acc[2]


def _row(v, width=None):
    v = v.reshape(1, -1).astype(f32)
    if width is not None and v.shape[1] < width:
        v = jnp.pad(v, ((0, 0), (0, width - v.shape[1])))
    return v


LATE = ("w_out", "w_up", "w_ple_gate", "w_ple", "w_down")


def _device_grads(x, p, target, small, w_cat, conv_w, late, qc=None, tail=None):
    z4 = jnp.zeros((4,), f32)
    bias_row = _row(jnp.concatenate([z4, small["dt_bias"].reshape(-1), small["b_f"].reshape(-1)]), LANES)
    alog_row = _row(jnp.concatenate([z4, small["a_log"].reshape(-1)]), LANES)
    g_gdn = _row(small["gdn_norm_g"])
    g_fox2 = _row(jnp.tile(small["fox_norm_g"].reshape(-1), 2))
    pb = p.astype(bf16)
    late = list(late)
    comm = qc is not None

    h0, h0b = _ln_in(x, _row(small["ln_in_g"]), _row(small["ln_in_b"]))
    proj = _mm(h0b, w_cat, "nt", 256, D_CAT, "mm_proj")
    gates, gcum, gcumt = _gates(proj, bias_row, alog_row)
    early = [(0, 0, 1), (3, 0, 1), (1, 0, 2), (1, 1, 2)]
    rest = [(2, 0, 1), (4, 0, 2), (4, 1, 2)]
    state = dict(late=late)

    def gather(phase, pieces):
        if not comm:
            return None, lambda moved: None
        return phase(state["late"], pieces), lambda moved: state.update(late=list(moved))

    cm, took = gather(_gather_chips, early[:2])
    (conv_c, qkv_n), moved = _gdn_conv(proj, conv_w, cm)
    took(moved)
    cm, took = gather(_gather_chips, early[2:3])
    (gu, gw, gqk, gtinv), moved = _gdn_local(qkv_n, gcum, cm)
    took(moved)
    cm, took = gather(_gather_chips, early[3:])
    (o_gdn, sall), moved = _gdn_fwd(qkv_n, gcum, gu, gw, gqk, cm)
    took(moved)
    fox_plan = [(_gather_chips, rest[:1]), (_gather_pass_on, early), (_gather_chips, rest[1:2]), (_gather_chips, rest[2:])]
    assert not comm or len(_fox_groups(x.shape[0])) == len(fox_plan)
    o_fox, lse = _fox_fwd(proj, gcum, gcumt, (lambda n: gather(*fox_plan[n])) if comm else None)
    cm, took = gather(_gather_pass_on, rest)
    (attn,), moved = _attn_post(o_gdn, proj, o_fox, g_gdn, g_fox2, cm)
    took(moved)
    w_out, w_up, w_gate, w_ple, w_down = state["late"]
    w_out, w_gate, w_down = w_out.reshape(D_MODEL, D_MODEL), w_gate.reshape(D_MODEL, D_MODEL), w_down.reshape(D_FF, D_MODEL)
    mix = _mm(attn, w_out, "nn", 512, D_MODEL, "mm_mix")
    h1, h1b, xhat1, rstd1 = _ln1(h0, mix, _row(small["ln1_g"]), _row(small["ln1_b"]))
    up, act = _mm(h1b, w_up, "nn", 256, 1024, "mm_up", epi="relu2", shards=N_CHIPS)
    ff = _mm(act, w_down, "nn", 256, D_MODEL, "mm_down")
    gp = _mm(h1b, w_gate, "nn", 512, D_MODEL, "mm_gate")
    pe = _mm(pb, w_ple, "nn", 512, D_MODEL // N_CHIPS, "mm_ple", shards=N_CHIPS)
    dr2, dr2b, dpe, dgp, pg2 = _ln2_loss(h1, ff, pe, gp, _row(small["b_ple_gate"]), _row(small["ln2_g"]), _row(small["ln2_b"]), target)

    dup = _mm(dr2b, w_down, "nt", 256, 2048, "mm_dact", epi="relu2_bwd", extra=up)
    g_down = _mm(act, dr2b, "tn", 1024, D_MODEL, "mm_gdown")
    dh1_a = _mm(dup, w_up, "nt", 256, D_MODEL, "mm_dh1a", shards=N_CHIPS)
    g_up = _mm(h1b, dup, "tn", 1024, 1024, "mm_gup", shards=N_CHIPS)
    dh1_b = _mm(dgp, w_gate, "nt", 512, D_MODEL, "mm_dh1b")
    g_gate = _mm(h1b, dgp, "tn", 1024, D_MODEL, "mm_ggate")
    g_ple = _mm(pb, dpe, "tn", D_PLE, D_MODEL // N_CHIPS, "mm_gple", shards=N_CHIPS)
    dr1, dr1b, pg1 = _ln1_bwd(dr2, dh1_a, dh1_b, xhat1, rstd1, _row(small["ln1_g"]))
    dattn = _mm(dr1b, w_out, "nt", 512, D_MODEL, "mm_dattn")
    g_out = _mm(attn, dr1b, "tn", 1024, D_MODEL, "mm_gout")
    do_gdn, dz, do_fox, pga = _attn_post_bwd(dattn, o_gdn, proj, o_fox, g_gdn, g_fox2)
    g_late = [g.reshape((N_CHIPS, -1, g.shape[-1])) for g in (g_out, g_up, g_gate, g_ple, g_down)]
    chip_plan = [[(4, 0, 2), (0, 0, 1)], [(4, 1, 2)], [(2, 0, 1), (3, 0, 1)], [(1, 0, 2), (1, 1, 2)]]

    def to_sibling():
        def took(moved):
            sums = [_add_pair(g, b1, qc, "add_pair_" + n) for g, b1, n in zip(g_late, moved, LATE)]
            state.update(own=[a for a, _ in sums], sent=[ab for _, ab in sums], landing=_landing([ab for _, ab in sums]))
        return _exchange_pairs(g_late), took

    def to_chips(pieces):
        if not comm:
            return None, lambda moved: None
        return _exchange_chips(state["sent"], state["landing"], pieces), lambda moved: state.update(landing=list(moved))

    assert not comm or len(_fox_groups(x.shape[0])) == len(chip_plan)
    dfq, dfk, dfv, dccol, dct = _fox_bwd(proj, gcum, gcumt, o_fox, lse, do_fox,
                                         (lambda n: to_sibling() if n == 0 else to_chips(chip_plan[n - 1])) if comm else None)
    cm, took = to_chips(chip_plan[-1])
    (dqkv_n, dgates), moved = _gdn_bwd(qkv_n, gcum, gu, gw, gqk, gtinv, sall, do_gdn, cm)
    took(moved)
    dsmall, pgg = _gates_bwd(proj, bias_row, alog_row, gates, dgates, dccol, dct)
    cm = None
    if comm:
        cm = _share_halves([_add_chips(a, b2, qc, "add_chips_" + n) for a, b2, n in zip(state["own"], state["landing"], LATE)])
    (du, g_conv8), reduced = _gdn_conv_bwd(proj, conv_w, conv_c, dqkv_n, cm)
    if comm:
        g_late = list(reduced)
    t = x.shape[0]
    dproj = jnp.concatenate([du, dz, dfq, dfk.astype(bf16), dfv.astype(bf16), dsmall, jnp.zeros((t, D_CAT - SEG_SMALL - LANES), bf16)], axis=1)
    g_cat = _mm(dproj, h0b, "tn", 1280, D_MODEL, "mm_gcat")
    cm, took = tail[0](g_cat) if tail else (None, None)
    dh0_mm = _mm(dproj, w_cat, "nn", 256, D_MODEL, "mm_dh0", comm=cm)
    if cm:
        dh0_mm, moved = dh0_mm
        took(moved)
    cm, took = tail[1]() if tail and tail[1] else (None, None)
    (grad_x, pg0), moved = _ln_in_bwd(x, dr1, dh0_mm, _row(small["ln_in_g"]), cm)
    if cm:
        took(moved)

    g_fox = pga[1, :FOX_DH] + pga[1, FOX_DH:]
    small_grads = dict(
        ln_in_g=pg0[0], ln_in_b=pg0[1], ln1_g=pg1[0], ln1_b=pg1[1], b_ple_gate=pg2[2], ln2_g=pg2[0], ln2_b=pg2[1],
        gdn_norm_g=pga[0], fox_norm_g=g_fox, a_log=pgg[1, 4:8], dt_bias=pgg[0, 4:8], b_f=pgg[0, 8:16], loss=pg2[3, 0:1])
    return grad_x, g_cat, g_conv8[:CONV_W], dict(zip(LATE, g_late)), small_grads


ANY = pl.BlockSpec(memory_space=pl.ANY)
CONV_PKT_ROWS = 16


def _mesh_pos():
    return lax.axis_index("x"), lax.axis_index("y"), lax.axis_index("c")


def _other_chips(x, y):
    return [(1 - x, y), (x, 1 - y), (1 - x, 1 - y)]


def _rcopy(src, dst, send_sem, recv_sem, dev):
    return pltpu.make_async_remote_copy(src_ref=src, dst_ref=dst, send_sem=send_sem, recv_sem=recv_sem,
                                        device_id=dev, device_id_type=MESH)


class _Comm:
    def __init__(self, ins, outs, aliases, n_sems, start, finish):
        self.ins, self.outs, self.aliases, self.n_sems, self.start, self.finish = list(ins), list(outs), dict(aliases), n_sems, start, finish


def _hosted(body, comm, *, name, grid, in_specs, out_specs, out_shape, args, scratch_shapes=(), aliases=None):
    n_in, n_out, n_sc = len(in_specs), len(out_specs), len(scratch_shapes)
    k, ko = (len(comm.ins), len(comm.outs)) if comm else (0, 0)

    def kernel_body(*refs):
        o0 = n_in + k
        s0 = o0 + n_out + ko
        if comm:
            cins, couts, (ssem, rsem) = refs[n_in:o0], refs[o0 + n_out:s0], refs[s0 + n_sc:]
            step = pl.program_id(0)
            for d in range(1, len(grid)):
                step = step * grid[d] + pl.program_id(d)

            @pl.when(step == 0)
            def _():
                comm.start(cins, couts, ssem, rsem)

        body(*refs[:n_in], *refs[o0:o0 + n_out], *refs[s0:s0 + n_sc])
        if comm:
            last = 1
            for n in grid:
                last *= n

            @pl.when(step == last - 1)
            def _():
                comm.finish(cins, couts, ssem, rsem)

    io_aliases = dict(aliases or {})
    scratch = list(scratch_shapes)
    if comm:
        io_aliases.update({n_in + i: n_out + j for i, j in comm.aliases.items()})
        scratch += [pltpu.SemaphoreType.DMA((comm.n_sems,)), pltpu.SemaphoreType.DMA((comm.n_sems,))]
    res = pl.pallas_call(
        kernel_body, name=name, grid=grid, in_specs=list(in_specs) + [ANY] * k, out_specs=tuple(out_specs) + (ANY,) * ko,
        out_shape=tuple(out_shape) + tuple(comm.outs if comm else ()), scratch_shapes=scratch, input_output_aliases=io_aliases,
        compiler_params=_params(("arbitrary",) * len(grid)),
    )(*args, *(comm.ins if comm else ()))
    return tuple(res[:n_out]), tuple(res[n_out:])


def _comm_only(phases, name):
    n_in = sum(len(p.ins) for p in phases)

    def body(*refs):
        n_out = sum(len(p.outs) for p in phases)
        sems = refs[n_in + n_out:]
        i0, o0 = 0, n_in
        for j, p in enumerate(phases):
            cins, couts = refs[i0:i0 + len(p.ins)], refs[o0:o0 + len(p.outs)]
            p.start(cins, couts, sems[2 * j], sems[2 * j + 1])
            p.finish(cins, couts, sems[2 * j], sems[2 * j + 1])
            i0 += len(p.ins)
            o0 += len(p.outs)

    aliases, i0, o0 = {}, 0, 0
    for p in phases:
        aliases.update({i0 + i: o0 + j for i, j in p.aliases.items()})
        i0 += len(p.ins)
        o0 += len(p.outs)
    outs = [o for p in phases for o in p.outs]
    res = pl.pallas_call(
        body, name=name, out_shape=tuple(outs), in_specs=[ANY] * n_in, out_specs=(ANY,) * len(outs), input_output_aliases=aliases,
        scratch_shapes=[pltpu.SemaphoreType.DMA((p.n_sems,)) for p in phases for _ in range(2)],
    )(*[a for p in phases for a in p.ins])
    split, o0 = [], 0
    for p in phases:
        split.append(tuple(res[o0:o0 + len(p.outs)]))
        o0 += len(p.outs)
    return split


def _like(arrays):
    return [jax.ShapeDtypeStruct(a.shape, a.dtype) for a in arrays]


def _half(ref, slot, hf, piece=(0, 1)):
    k, n = piece
    rows = ref.shape[1] // 2 // n
    return ref.at[slot, pl.ds((hf * n + k) * rows, rows)]


def _whole_halves(arrays):
    return [(i, 0, 1) for i in range(len(arrays))]


def _gather_chips(bufs, pieces=None, whole=False, base=0):
    nw = len(bufs)
    pieces = _whole_halves(bufs) if pieces is None else pieces
    part = (lambda ref, slot, c, piece: ref.at[slot]) if whole else _half

    def copies(couts):
        x, y, c = _mesh_pos()
        q = 2 * x + y
        for j, (i, k, n) in enumerate(pieces):
            for kc, chip in enumerate(_other_chips(x, y)):
                mine, theirs = part(couts[i], q, c, (k, n)), part(couts[i], 2 * chip[0] + chip[1], c, (k, n))
                yield base + j * 3 + kc, mine, theirs, (*chip, c)

    def start(cins, couts, ssem, rsem):
        for s, mine, _, dev in copies(couts):
            _rcopy(mine, mine, ssem.at[s], rsem.at[s], dev).start()

    def finish(cins, couts, ssem, rsem):
        for s, _, theirs, dev in copies(couts):
            _rcopy(theirs, theirs, ssem.at[s], rsem.at[s], dev).wait_recv()
        for s, mine, _, dev in copies(couts):
            _rcopy(mine, mine, ssem.at[s], rsem.at[s], dev).wait_send()

    return _Comm(bufs, _like(bufs), {i: i for i in range(nw)}, 3 * len(pieces), start, finish)


def _gather_pass_on(bufs, pieces=None, base=0):
    nw = len(bufs)
    pieces = _whole_halves(bufs) if pieces is None else pieces

    def copies(couts):
        x, y, c = _mesh_pos()
        for j, (i, k, n) in enumerate(pieces):
            for kc, chip in enumerate(_other_chips(x, y)):
                slot = 2 * chip[0] + chip[1]
                yield base + j * 3 + kc, _half(couts[i], slot, c, (k, n)), _half(couts[i], slot, 1 - c, (k, n)), (x, y, 1 - c)

    def start(cins, couts, ssem, rsem):
        for s, landed, _, sib in copies(couts):
            _rcopy(landed, landed, ssem.at[s], rsem.at[s], sib).start()

    def finish(cins, couts, ssem, rsem):
        for s, _, passed, sib in copies(couts):
            _rcopy(passed, passed, ssem.at[s], rsem.at[s], sib).wait_recv()
        for s, landed, _, sib in copies(couts):
            _rcopy(landed, landed, ssem.at[s], rsem.at[s], sib).wait_send()

    return _Comm(bufs, _like(bufs), {i: i for i in range(nw)}, 3 * len(pieces), start, finish)


def _gather_now(bufs, packets):
    nb = len(bufs)
    over, on, pk = _gather_chips(bufs), _gather_pass_on(bufs, base=3 * nb), _gather_chips(packets, whole=True, base=6 * nb)

    def start(cins, couts, ssem, rsem):
        over.start(cins[:nb], couts[:nb], ssem, rsem)
        pk.start(cins[nb:], couts[nb:], ssem, rsem)

    def finish(cins, couts, ssem, rsem):
        over.finish(cins[:nb], couts[:nb], ssem, rsem)
        on.start(cins[:nb], couts[:nb], ssem, rsem)
        on.finish(cins[:nb], couts[:nb], ssem, rsem)
        pk.finish(cins[nb:], couts[nb:], ssem, rsem)

    every = list(bufs) + list(packets)
    return _Comm(every, _like(every), {i: i for i in range(len(every))}, 6 * nb + 3 * len(packets), start, finish)


def _exchange_pairs(gs):
    nw = len(gs)

    def copies(cins, couts):
        x, y, c = _mesh_pos()
        for i in range(nw):
            for d in range(N_CHIPS):
                yield i * N_CHIPS + d, _half(cins[i], d, 1 - c), couts[i].at[d], (x, y, 1 - c)

    def start(cins, couts, ssem, rsem):
        for s, src, dst, sib in copies(cins, couts):
            _rcopy(src, dst, ssem.at[s], rsem.at[s], sib).start()

    def finish(cins, couts, ssem, rsem):
        for s, src, dst, sib in copies(cins, couts):
            _rcopy(src, dst, ssem.at[s], rsem.at[s], sib).wait_recv()
        for s, src, dst, sib in copies(cins, couts):
            _rcopy(src, dst, ssem.at[s], rsem.at[s], sib).wait_send()

    outs = [jax.ShapeDtypeStruct((N_CHIPS, g.shape[1] // 2, g.shape[2]), g.dtype) for g in gs]
    return _Comm(gs, outs, {}, N_CHIPS * nw, start, finish)


def _gather_packets(small):
    def peers():
        x, y, c = _mesh_pos()
        for r in range(1, 8):
            fx, fy, fc = (r >> 2) & 1, (r >> 1) & 1, r & 1
            yield r - 1, (1 - x if fx else x, 1 - y if fy else y, 1 - c if fc else c)

    def start(cins, couts, ssem, rsem):
        x, y, c = _mesh_pos()
        mine = couts[0].at[4 * x + 2 * y + c]
        for s, peer in peers():
            _rcopy(mine, mine, ssem.at[s], rsem.at[s], peer).start()

    def finish(cins, couts, ssem, rsem):
        x, y, c = _mesh_pos()
        mine = couts[0].at[4 * x + 2 * y + c]
        for s, peer in peers():
            theirs = couts[0].at[4 * peer[0] + 2 * peer[1] + peer[2]]
            _rcopy(theirs, theirs, ssem.at[s], rsem.at[s], peer).wait_recv()
        for s, peer in peers():
            _rcopy(mine, mine, ssem.at[s], rsem.at[s], peer).wait_send()

    return _Comm([small], _like([small]), {0: 0}, 7, start, finish)


def _exchange_chips(a4s, b2s, pieces=None):
    nw = len(a4s)
    pieces = _whole_halves(a4s) if pieces is None else pieces

    def copies(cins, couts):
        x, y, c = _mesh_pos()
        for j, (i, k, n) in enumerate(pieces):
            rows = a4s[i].shape[1] // n
            part = pl.ds(k * rows, rows)
            for kc, chip in enumerate(_other_chips(x, y)):
                yield j * 3 + kc, cins[i].at[2 * chip[0] + chip[1], part], couts[i].at[kc, part], (*chip, c)

    def start(cins, couts, ssem, rsem):
        for s, src, dst, dev in copies(cins, couts):
            _rcopy(src, dst, ssem.at[s], rsem.at[s], dev).start()

    def finish(cins, couts, ssem, rsem):
        for s, src, dst, dev in copies(cins, couts):
            _rcopy(src, dst, ssem.at[s], rsem.at[s], dev).wait_recv()
        for s, src, dst, dev in copies(cins, couts):
            _rcopy(src, dst, ssem.at[s], rsem.at[s], dev).wait_send()

    return _Comm(list(a4s) + list(b2s), _like(b2s), {nw + i: i for i in range(nw)}, 3 * len(pieces), start, finish)


def _landing(a4s):
    return [lax.empty((3,) + a.shape[1:], a.dtype) for a in a4s]


def _share_halves(rs):
    nw = len(rs)

    def halves(couts, i, hf):
        rows = rs[i].shape[0] // 2
        return couts[i].at[pl.ds(hf * rows, rows)]

    def start(cins, couts, ssem, rsem):
        x, y, c = _mesh_pos()
        for i in range(nw):
            _rcopy(halves(couts, i, c), halves(couts, i, c), ssem.at[i], rsem.at[i], (x, y, 1 - c)).start()

    def finish(cins, couts, ssem, rsem):
        x, y, c = _mesh_pos()
        for i in range(nw):
            _rcopy(halves(couts, i, 1 - c), halves(couts, i, 1 - c), ssem.at[i], rsem.at[i], (x, y, 1 - c)).wait_recv()
        for i in range(nw):
            _rcopy(halves(couts, i, c), halves(couts, i, c), ssem.at[i], rsem.at[i], (x, y, 1 - c)).wait_send()

    return _Comm(rs, _like(rs), {i: i for i in range(nw)}, nw, start, finish)


ADD_ROWS = 256


def _add_pair(g4, b1, qc_idx, name):
    _, half, cols = b1.shape
    rb = ADD_ROWS if half % ADD_ROWS == 0 else half
    nb = half // rb

    def body(qc_ref, g_ref, b_ref, o_ref, ob_ref):
        a = g_ref[...] + b_ref[...]
        o_ref[...] = a
        ob_ref[...] = a.astype(bf16)

    blk = (1, rb, cols)
    out = pl.BlockSpec(blk, lambda d, i, qc: (d, i, 0))
    return pl.pallas_call(
        body, name=name,
        grid_spec=pltpu.PrefetchScalarGridSpec(
            num_scalar_prefetch=1, grid=(N_CHIPS, nb),
            in_specs=[pl.BlockSpec(blk, lambda d, i, qc: (d, qc[1] * nb + i, 0)), out],
            out_specs=(out, out)),
        out_shape=(jax.ShapeDtypeStruct(b1.shape, f32), jax.ShapeDtypeStruct(b1.shape, bf16)),
        compiler_params=_params(("parallel", "parallel")),
    )(qc_idx, g4, b1)


def _add_chips(a4, b2, qc_idx, name):
    _, half, cols = a4.shape
    rb = ADD_ROWS if half % ADD_ROWS == 0 else half
    nb = half // rb

    def body(qc_ref, a_ref, b_ref, o_ref):
        o_ref[...] = ((a_ref[0] + b_ref[0].astype(f32)) + b_ref[1].astype(f32)) + b_ref[2].astype(f32)

    return pl.pallas_call(
        body, name=name,
        grid_spec=pltpu.PrefetchScalarGridSpec(
            num_scalar_prefetch=1, grid=(nb,),
            in_specs=[pl.BlockSpec((1, rb, cols), lambda i, qc: (qc[0], i, 0)), pl.BlockSpec((3, rb, cols), lambda i, qc: (0, i, 0))],
            out_specs=pl.BlockSpec((rb, cols), lambda i, qc: (qc[1] * nb + i, 0))),
        out_shape=jax.ShapeDtypeStruct((2 * half, cols), f32),
        compiler_params=_params(("parallel",)),
    )(qc_idx, a4, b2)


def _adamw_math(w, g, m, v):
    m = ADAM_B1 * m + (1.0 - ADAM_B1) * g
    v = ADAM_B2 * v + (1.0 - ADAM_B2) * (g * g)
    m_hat = m / (1.0 - ADAM_B1 ** ADAM_STEP)
    v_hat = v / (1.0 - ADAM_B2 ** ADAM_STEP)
    return -ADAM_LR * (m_hat / (jnp.sqrt(v_hat) + ADAM_EPS) + ADAM_WD * w), m, v


def _adamw(w, g, m, v, name, comm=None):
    rows = w.shape[0]
    if w.ndim == 3:
        rb = max(r for r in range(1, ADD_ROWS // 4 + 1) if rows % r == 0)
    else:
        rb = ADD_ROWS if rows % ADD_ROWS == 0 else rows

    def body(w_ref, g_ref, m_ref, v_ref, go_ref, d_ref, mo_ref, vo_ref):
        g = g_ref[...]
        go_ref[...] = g
        d_ref[...], mo_ref[...], vo_ref[...] = _adamw_math(w_ref[...], g, m_ref[...], v_ref[...])

    blk = pl.BlockSpec((rb,) + w.shape[1:], lambda i: (i,) + (0,) * (w.ndim - 1))
    return _hosted(body, comm, name=name, grid=(rows // rb,), in_specs=[blk] * 4, out_specs=(blk,) * 4,
                   out_shape=(jax.ShapeDtypeStruct(w.shape, f32),) * 4, args=(w, g, m, v))


def _small_sum_adamw(all_pkts, w, m, v):
    def body(a_ref, w_ref, m_ref, v_ref, g_ref, d_ref, mo_ref, vo_ref):
        g = a_ref[0]
        for r in range(1, 8):
            g = g + a_ref[r]
        g_ref[...] = g
        d_ref[...], mo_ref[...], vo_ref[...] = _adamw_math(w_ref[...], g, m_ref[...], v_ref[...])

    return pl.pallas_call(body, name="small_sum_adamw", out_shape=(jax.ShapeDtypeStruct(w.shape, f32),) * 4)(all_pkts, w, m, v)


SMALL_LAYOUT = (("ln_in_g", 0, 1024), ("ln_in_b", 8, 1024), ("ln1_g", 16, 1024), ("ln1_b", 24, 1024), ("b_ple_gate", 32, 1024),
                ("ln2_g", 40, 1024), ("ln2_b", 48, 1024), ("gdn_norm_g", 56, 128), ("fox_norm_g", 57, 64), ("a_log", 58, 4),
                ("dt_bias", 59, 4), ("b_f", 60, 8), ("loss", 61, 1))
SMALL_CONV_ROW = 64
SMALL_ROWS = 128


def _pack_small(vals, conv=None):
    rows = []
    nxt = 0
    for n, r0, size in SMALL_LAYOUT:
        assert r0 == nxt
        v = vals[n].reshape(-1).astype(f32) if n in vals else jnp.zeros((size,), f32)
        nrows = -(-size // LANES)
        rows.append(jnp.pad(v, (0, nrows * LANES - size)).reshape(nrows, LANES))
        nxt = r0 + nrows
    rows.append(jnp.zeros((SMALL_CONV_ROW - nxt, LANES), f32))
    conv_rows = CONV_W * GDN_QKV // LANES
    rows.append(jnp.zeros((conv_rows, LANES), f32) if conv is None else conv.reshape(conv_rows, LANES))
    rows.append(jnp.zeros((SMALL_ROWS - SMALL_CONV_ROW - conv_rows, LANES), f32))
    return jnp.concatenate(rows, axis=0)


def _unpack_small(pkt, shapes):
    out = {}
    for n, r0, size in SMALL_LAYOUT:
        if n in shapes:
            nrows = -(-size // LANES)
            out[n] = pkt[r0:r0 + nrows].reshape(-1)[:size].reshape(shapes[n])
    return out


WEIGHTS = ("ln_in_g", "ln_in_b", "w_in", "conv_w", "a_log", "dt_bias", "gdn_norm_g", "b_f", "fox_norm_g", "w_out", "ln1_g", "ln1_b",
           "w_up", "w_down", "w_ple", "w_ple_gate", "b_ple_gate", "ln2_g", "ln2_b")
SMALL_NAMES = tuple(n for n, _, _ in SMALL_LAYOUT if n != "loss")


def kernel(x, p, ln_in_g, ln_in_b, w_in, conv_w, a_log, dt_bias, gdn_norm_g, b_f, fox_norm_g, w_out, ln1_g, ln1_b, w_up, w_down, w_ple, w_ple_gate, b_ple_gate, ln2_g, ln2_b, loss_target, m_ln_in_g, m_ln_in_b, m_w_in, m_conv_w, m_a_log, m_dt_bias, m_gdn_norm_g, m_b_f, m_fox_norm_g, m_w_out, m_ln1_g, m_ln1_b, m_w_up, m_w_down, m_w_ple, m_w_ple_gate, m_b_ple_gate, m_ln2_g, m_ln2_b, v_ln_in_g, v_ln_in_b, v_w_in, v_conv_w, v_a_log, v_dt_bias, v_gdn_norm_g, v_b_f, v_fox_norm_g, v_w_out, v_ln1_g, v_ln1_b, v_w_up, v_w_down, v_w_ple, v_w_ple_gate, v_b_ple_gate, v_ln2_g, v_ln2_b):
    given = dict(locals())
    w = {n: given[n] for n in WEIGHTS}
    m = {n: given["m_" + n] for n in WEIGHTS}
    v = {n: given["v_" + n] for n in WEIGHTS}
    xi, yi, ci = _mesh_pos()
    q = 2 * xi + yi

    def slot_buffer(val, dtype, slots=N_CHIPS, slot=q, rows=None):
        rows = val.shape[0] if rows is None else rows
        return lax.dynamic_update_slice(lax.empty((slots, rows) + val.shape[1:], dtype), val.astype(dtype)[None], (slot, 0, 0))

    shard_cols = D_IN // N_CHIPS
    conv_rows = CONV_W * GDN_QKV // N_CHIPS // LANES
    conv_pkt = jnp.pad(w["conv_w"][0].reshape(-1, LANES), ((0, CONV_PKT_ROWS - conv_rows), (0, 0)))
    (w_in4, conv_all), = _comm_only(
        [_gather_now([slot_buffer(w["w_in"][0].T, bf16, rows=W_IN_ROWS)], [slot_buffer(conv_pkt, f32)])], "gather_w_in")
    conv_full = jnp.concatenate([conv_all[d, :conv_rows].reshape(CONV_W, GDN_QKV // N_CHIPS) for d in range(N_CHIPS)], axis=1)
    wi = jnp.concatenate([w_in4[d, :shard_cols] for d in range(N_CHIPS)], axis=0)
    w_cat = jnp.concatenate([wi[:OFF_BETA], wi[OFF_FOX:OFF_F], wi[OFF_BETA:OFF_FOX], wi[OFF_F:],
                             jnp.zeros((D_CAT - D_IN, D_MODEL), bf16)], axis=0)

    small = {n: w[n] for n in SMALL_NAMES}
    qc = jnp.stack([q, ci]).astype(jnp.int32)
    tail_state = {}

    def pairs_phase(gc):
        g_in = jnp.concatenate([gc[:OFF_BETA], gc[SEG_SMALL:SEG_SMALL + 8], gc[SEG_FOX:SEG_SMALL], gc[SEG_SMALL + 8:SEG_SMALL + 16]], axis=0)
        g_in4 = jnp.stack([jnp.pad(g_in[d * shard_cols:(d + 1) * shard_cols], ((0, W_IN_ROWS - shard_cols), (0, 0))) for d in range(N_CHIPS)])

        def took(moved):
            own, sent = _add_pair(g_in4, moved[0], qc, "add_pair_w_in")
            tail_state.update(own=own, sent=[sent], landing=_landing([sent]))
        return _exchange_pairs([g_in4]), took

    grad_x, _, g_conv, g_late, small_g = _device_grads(
        x[0], p[0, 0], loss_target[0], small, w_cat, conv_full, [slot_buffer(w[n][0], bf16) for n in LATE], qc, tail=(pairs_phase, None))
    packets = _gather_packets(slot_buffer(_pack_small(small_g, g_conv), f32, 8, 4 * xi + 2 * yi + ci))
    (b2,), (small_all,) = _comm_only([_exchange_chips(tail_state["sent"], tail_state["landing"]), packets], "exchange_chips_w_in")
    (g_late["w_in"],), = _comm_only([_share_halves([_add_chips(tail_state["own"], b2, qc, "add_chips_w_in")])], "share_w_in")

    grads, delta, new_m, new_v = {}, {}, {}, {}
    for n, g in g_late.items():
        if n == "w_in":
            as_stored = lambda a: jnp.transpose(a, (2, 0, 1))
            outs, _ = _adamw(as_stored(w[n]), g[:shard_cols].reshape(shard_cols, 1, D_MODEL), as_stored(m[n]), as_stored(v[n]), "adamw_" + n)
            grads[n], delta[n], new_m[n], new_v[n] = (jnp.transpose(a, (1, 2, 0)) for a in outs)
        else:
            outs, _ = _adamw(w[n][0], g, m[n][0], v[n][0], "adamw_" + n)
            grads[n], delta[n], new_m[n], new_v[n] = (a.reshape(w[n].shape) for a in outs)
    shapes = {n: w[n].shape for n in SMALL_NAMES}
    g_pkt, d_pkt, m_pkt, v_pkt = _small_sum_adamw(small_all, _pack_small(w), _pack_small(m), _pack_small(v))
    for dst, pkt in ((grads, g_pkt), (delta, d_pkt), (new_m, m_pkt), (new_v, v_pkt)):
        dst.update(_unpack_small(pkt, shapes))
    conv_rows_all = CONV_W * GDN_QKV // LANES
    conv_g_full = g_pkt[SMALL_CONV_ROW:SMALL_CONV_ROW + conv_rows_all].reshape(CONV_W, GDN_QKV)
    conv_g = lax.dynamic_slice_in_dim(conv_g_full, q * (GDN_QKV // N_CHIPS), GDN_QKV // N_CHIPS, axis=1)
    outs, _ = _adamw(w["conv_w"][0], conv_g, m["conv_w"][0], v["conv_w"][0], "adamw_conv_w")
    grads["conv_w"], delta["conv_w"], new_m["conv_w"], new_v["conv_w"] = (a.reshape(w["conv_w"].shape) for a in outs)
    loss = g_pkt[61, 0]
    return (loss, grad_x[None], *[grads[n] for n in WEIGHTS], *[delta[n] for n in WEIGHTS],
            *[new_m[n] for n in WEIGHTS], *[new_v[n] for n in WEIGHTS])
```

```python
import functools

import jax
import jax.numpy as jnp
from jax import lax
from jax.experimental import pallas as pl
from jax.experimental.pallas import tpu as pltpu

f32 = jnp.float32
bf16 = jnp.bfloat16
HI = lax.Precision.HIGHEST
MESH = pl.DeviceIdType.MESH

D_MODEL = 1024
CHUNK = 64
GDN_HEADS = 4
GDN_DK = 128
FOX_HEADS = 8
FOX_DH = 64
CONV_W = 4
D_FF = 4096
D_PLE = 256
LN_EPS = 1e-5
NORM_EPS = 1e-6
ALPHA = 2.0 ** 0.25
GDN_QKV = 1536
OFF_Z = 1536
OFF_BETA = 2048
OFF_FOX = 2056
OFF_F = 3592
D_IN = 3600
ADAM_LR = 0.001
ADAM_B1 = 0.9
ADAM_B2 = 0.999
ADAM_EPS = 1e-08
ADAM_WD = 0.01
ADAM_STEP = 10

SEG_FOX = 2048
SEG_SMALL = 3584
D_CAT = 3840
LANES = 128
TOK_BLK = 256
FOX_BQ = 256
VMEM_LIMIT = 56 * 1024 * 1024
NEG = -1e30

N_CHIPS = 4
W_IN_ROWS = 928


def _params(sem=None, **kw):
    return pltpu.CompilerParams(dimension_semantics=sem, vmem_limit_bytes=VMEM_LIMIT, **kw)


def _sigmoid(x):
    return 1.0 / (1.0 + jnp.exp(-x))


def _softplus(x):
    return jnp.maximum(x, 0.0) + jnp.log(1.0 + jnp.exp(-jnp.abs(x)))


def _ln_fwd(x, g, b):
    mu = jnp.mean(x, -1, keepdims=True)
    xc = x - mu
    var = jnp.mean(xc * xc, -1, keepdims=True)
    rstd = lax.rsqrt(var + LN_EPS)
    xhat = xc * rstd
    return xhat * g + b, xhat, rstd


def _ln_bwd(dy, xhat, rstd, g):
    dxh = dy * g
    m1 = jnp.mean(dxh, -1, keepdims=True)
    m2 = jnp.mean(dxh * xhat, -1, keepdims=True)
    return rstd * (dxh - m1 - xhat * m2)


def _dot(a, b, prec=HI):
    return jnp.dot(a, b, precision=prec, preferred_element_type=f32)


def _dot_nt(a, b, prec=HI):
    return lax.dot_general(a, b, (((1,), (1,)), ((), ())), precision=prec, preferred_element_type=f32)


def _dot_tn(a, b, prec=HI):
    return lax.dot_general(a, b, (((0,), (0,)), ((), ())), precision=prec, preferred_element_type=f32)


def _bdot(a, b):
    return _dot(a.astype(bf16), b.astype(bf16), None)


def _bdot_nt(a, b):
    return _dot_nt(a.astype(bf16), b.astype(bf16), None)


def _bdot_tn(a, b):
    return _dot_tn(a.astype(bf16), b.astype(bf16), None)


def _lane(shape):
    return lax.broadcasted_iota(jnp.int32, shape, len(shape) - 1)


def _mm(a, b, mode, tm, tn, name, out_dtype=f32, epi=None, extra=None, shards=1, comm=None):
    if mode == "nn":
        (m, k), n = a.shape, b.shape[-1] * shards
    elif mode == "nt":
        (m, k), n = a.shape, b.shape[-2]
    else:
        (k, m), n = a.shape, b.shape[1]
    assert m % tm == 0 and n % tn == 0, (name, m, n, tm, tn)
    per = (n // shards) // tn
    assert mode == "nt" or per * tn * shards == n, (name, n, tn, shards)
    nc = 512 if tn % 512 == 0 else (256 if tn % 256 == 0 else 128)
    ks = k // shards

    def body(a_ref, b_ref, *rest):
        for n0 in range(0, tn, nc):
            if mode == "nn":
                acc = jnp.dot(a_ref[...], b_ref[:, n0:n0 + nc], preferred_element_type=f32)
            elif mode == "nt" and shards > 1:
                acc = jnp.zeros((tm, nc), f32)
                for d in range(shards):
                    acc = acc + lax.dot_general(a_ref[:, d * ks:(d + 1) * ks], b_ref[d, n0:n0 + nc, :], (((1,), (1,)), ((), ())),
                                                preferred_element_type=f32)
            elif mode == "nt":
                acc = lax.dot_general(a_ref[...], b_ref[n0:n0 + nc, :], (((1,), (1,)), ((), ())), preferred_element_type=f32)
            else:
                acc = lax.dot_general(a_ref[...], b_ref[:, n0:n0 + nc], (((0,), (0,)), ((), ())), preferred_element_type=f32)
            if epi == "relu2":
                up_ref, act_ref = rest
                up_ref[:, n0:n0 + nc] = acc
                r = jnp.maximum(acc, 0.0)
                act_ref[:, n0:n0 + nc] = (r * r).astype(bf16)
            elif epi == "relu2_bwd":
                up_ref, o_ref = rest
                o_ref[:, n0:n0 + nc] = (acc * (2.0 * jnp.maximum(up_ref[:, n0:n0 + nc], 0.0))).astype(bf16)
            else:
                (o_ref,) = rest
                o_ref[:, n0:n0 + nc] = acc.astype(out_dtype)

    if mode == "tn":
        a_spec = pl.BlockSpec((k, tm), lambda j, i: (0, i))
    else:
        a_spec = pl.BlockSpec((tm, k), lambda j, i: (i, 0))
    if mode == "nt" and shards > 1:
        b_spec = pl.BlockSpec((shards, tn, ks), lambda j, i: (0, j, 0))
    elif mode == "nt":
        b_spec = pl.BlockSpec((tn, k), lambda j, i: (j, 0))
    elif mode == "nn" and shards > 1:
        b_spec = pl.BlockSpec((None, k, tn), lambda j, i: (j // per, 0, j % per))
    else:
        b_spec = pl.BlockSpec((k, tn), lambda j, i: (0, j))
    o_spec = pl.BlockSpec((tm, tn), lambda j, i: (i, j))
    in_specs = [a_spec, b_spec]
    args = [a, b]
    if epi == "relu2":
        out_shape = (jax.ShapeDtypeStruct((m, n), f32), jax.ShapeDtypeStruct((m, n), bf16))
        out_specs = (o_spec, o_spec)
    elif epi == "relu2_bwd":
        in_specs.append(o_spec)
        args.append(extra)
        out_shape = jax.ShapeDtypeStruct((m, n), bf16)
        out_specs = o_spec
    elif mode == "tn" and shards > 1:
        out_shape = jax.ShapeDtypeStruct((shards, m, n // shards), out_dtype)
        out_specs = pl.BlockSpec((None, tm, tn), lambda j, i: (j // per, i, j % per))
    else:
        out_shape = jax.ShapeDtypeStruct((m, n), out_dtype)
        out_specs = o_spec
    single = not isinstance(out_shape, tuple)
    res, moved = _hosted(body, comm, name=name, grid=(n // tn, m // tm), in_specs=in_specs,
                         out_specs=(out_specs,) if single else out_specs, out_shape=(out_shape,) if single else out_shape, args=args)
    res = res[0] if single else res
    return res if comm is None else (res, moved)


def _row_spec(width, col=0):
    return pl.BlockSpec((TOK_BLK, width), lambda i: (i, col))


def _vec_spec(rows, width):
    return pl.BlockSpec((rows, width), lambda i: (0, 0))


def _ln_in(x, g, b, comm=None):
    t, d = x.shape

    def body(x_ref, g_ref, b_ref, h_ref, hb_ref):
        h, _, _ = _ln_fwd(x_ref[...], g_ref[...], b_ref[...])
        h_ref[...] = h
        hb_ref[...] = h.astype(bf16)

    return _hosted(
        body, comm, name="ln_in", grid=(t // TOK_BLK,),
        in_specs=[_row_spec(d), _vec_spec(1, d), _vec_spec(1, d)],
        out_specs=(_row_spec(d), _row_spec(d)),
        out_shape=(jax.ShapeDtypeStruct((t, d), f32), jax.ShapeDtypeStruct((t, d), bf16)),
        args=(x, g, b))


def _attn_post(o_gdn, proj, o_fox, g_gdn, g_fox2, comm=None):
    t = o_gdn.shape[0]

    def body(og_ref, z_ref, of_ref, gg_ref, gf_ref, out_ref):
        for h in range(GDN_HEADS):
            sl = slice(h * LANES, (h + 1) * LANES)
            og = og_ref[:, sl]
            z = z_ref[:, sl]
            r = lax.rsqrt(jnp.mean(og * og, -1, keepdims=True) + NORM_EPS)
            out_ref[:, sl] = (og * r * gg_ref[...] * (z * _sigmoid(z))).astype(bf16)
        lo = _lane((TOK_BLK, LANES)) < FOX_DH
        for pr in range(FOX_HEADS // 2):
            sl = slice(pr * LANES, (pr + 1) * LANES)
            of = of_ref[:, sl]
            sq = of * of
            s0 = jnp.sum(jnp.where(lo, sq, 0.0), -1, keepdims=True)
            s1 = jnp.sum(jnp.where(lo, 0.0, sq), -1, keepdims=True)
            r = lax.rsqrt(jnp.where(lo, s0, s1) * (1.0 / FOX_DH) + NORM_EPS)
            out_ref[:, 512 + pr * LANES:512 + (pr + 1) * LANES] = (of * r * gf_ref[...]).astype(bf16)

    return _hosted(
        body, comm, name="attn_post", grid=(t // TOK_BLK,),
        in_specs=[_row_spec(512), _row_spec(512, OFF_Z // 512), _row_spec(512), _vec_spec(1, LANES), _vec_spec(1, LANES)],
        out_specs=(_row_spec(D_MODEL),),
        out_shape=(jax.ShapeDtypeStruct((t, D_MODEL), bf16),),
        args=(o_gdn, proj, o_fox, g_gdn, g_fox2))


def _attn_post_bwd(dattn, o_gdn, proj, o_fox, g_gdn, g_fox2):
    t = o_gdn.shape[0]

    def body(da_ref, og_ref, z_ref, of_ref, gg_ref, gf_ref, dog_ref, dz_ref, dof_ref, pg_ref):
        i = pl.program_id(0)

        @pl.when(i == 0)
        def _():
            pg_ref[...] = jnp.zeros_like(pg_ref)

        dgg = jnp.zeros((1, LANES), f32)
        for h in range(GDN_HEADS):
            sl = slice(h * LANES, (h + 1) * LANES)
            og = og_ref[:, sl]
            z = z_ref[:, sl]
            dout = da_ref[:, sl]
            g = gg_ref[...]
            r = lax.rsqrt(jnp.mean(og * og, -1, keepdims=True) + NORM_EPS)
            sg = _sigmoid(z)
            silu = z * sg
            ng = og * r * g
            dng = dout * silu
            dz_ref[:, sl] = (dout * ng * (sg * (1.0 + z * (1.0 - sg)))).astype(bf16)
            dgg = dgg + jnp.sum(dng * og * r, 0, keepdims=True)
            gd = dng * g
            dog_ref[:, sl] = r * gd - og * (r * r * r) * jnp.mean(og * gd, -1, keepdims=True)
        pg_ref[0:1, :] += dgg
        lo = _lane((TOK_BLK, LANES)) < FOX_DH
        dgf = jnp.zeros((1, LANES), f32)
        for pr in range(FOX_HEADS // 2):
            sl = slice(pr * LANES, (pr + 1) * LANES)
            of = of_ref[:, sl]
            dout = da_ref[:, 512 + pr * LANES:512 + (pr + 1) * LANES]
            g = gf_ref[...]
            sq = of * of
            s0 = jnp.sum(jnp.where(lo, sq, 0.0), -1, keepdims=True)
            s1 = jnp.sum(jnp.where(lo, 0.0, sq), -1, keepdims=True)
            r = lax.rsqrt(jnp.where(lo, s0, s1) * (1.0 / FOX_DH) + NORM_EPS)
            dgf = dgf + jnp.sum(dout * of * r, 0, keepdims=True)
            gd = dout * g
            xg = of * gd
            m0 = jnp.sum(jnp.where(lo, xg, 0.0), -1, keepdims=True)
            m1 = jnp.sum(jnp.where(lo, 0.0, xg), -1, keepdims=True)
            dof_ref[:, sl] = r * gd - of * (r * r * r) * (jnp.where(lo, m0, m1) * (1.0 / FOX_DH))
        pg_ref[1:2, :] += dgf

    return pl.pallas_call(
        body, name="attn_post_bwd", grid=(t // TOK_BLK,),
        in_specs=[_row_spec(D_MODEL), _row_spec(512), _row_spec(512, OFF_Z // 512), _row_spec(512), _vec_spec(1, LANES), _vec_spec(1, LANES)],
        out_specs=(_row_spec(512), _row_spec(512), _row_spec(512), _vec_spec(8, LANES)),
        out_shape=(jax.ShapeDtypeStruct((t, 512), f32), jax.ShapeDtypeStruct((t, 512), bf16),
                   jax.ShapeDtypeStruct((t, 512), f32), jax.ShapeDtypeStruct((8, LANES), f32)),
        compiler_params=_params(("arbitrary",)),
    )(dattn, o_gdn, proj, o_fox, g_gdn, g_fox2)


def _ln1(h0, mix, g, b, comm=None):
    t, d = h0.shape

    def body(h0_ref, mix_ref, g_ref, b_ref, h_ref, hb_ref, xh_ref, rs_ref):
        h, xhat, rstd = _ln_fwd(ALPHA * h0_ref[...] + mix_ref[...], g_ref[...], b_ref[...])
        h_ref[...] = h
        hb_ref[...] = h.astype(bf16)
        xh_ref[...] = xhat
        rs_ref[...] = jnp.broadcast_to(rstd, rs_ref.shape)

    return _hosted(
        body, comm, name="ln1", grid=(t // TOK_BLK,),
        in_specs=[_row_spec(d), _row_spec(d), _vec_spec(1, d), _vec_spec(1, d)],
        out_specs=(_row_spec(d), _row_spec(d), _row_spec(d), _row_spec(LANES)),
        out_shape=(jax.ShapeDtypeStruct((t, d), f32), jax.ShapeDtypeStruct((t, d), bf16),
                   jax.ShapeDtypeStruct((t, d), f32), jax.ShapeDtypeStruct((t, LANES), f32)),
        args=(h0, mix, g, b))


def _ln2_loss(h1, ff, pe, gp, b_gate, g, b, target):
    t, d = h1.shape

    def body(h1_ref, ff_ref, pe_ref, gp_ref, bg_ref, g_ref, b_ref, t_ref, dr_ref, drb_ref, dpe_ref, dgp_ref, pg_ref):
        i = pl.program_id(0)

        @pl.when(i == 0)
        def _():
            pg_ref[...] = jnp.zeros_like(pg_ref)

        sig = _sigmoid(gp_ref[...] + bg_ref[...])
        pe = pe_ref[...]
        r2 = ALPHA * h1_ref[...] + ff_ref[...] + pe * sig
        y, xhat, rstd = _ln_fwd(r2, g_ref[...], b_ref[...])
        err = y - t_ref[...]
        dy = err * (1.0 / d)
        dr = _ln_bwd(dy, xhat, rstd, g_ref[...])
        dr_ref[...] = dr
        drb_ref[...] = dr.astype(bf16)
        dpe_ref[...] = (dr * sig).astype(bf16)
        dgp = dr * pe * sig * (1.0 - sig)
        dgp_ref[...] = dgp.astype(bf16)
        pg_ref[0:1, :] += jnp.sum(dy * xhat, 0, keepdims=True)
        pg_ref[1:2, :] += jnp.sum(dy, 0, keepdims=True)
        pg_ref[2:3, :] += jnp.sum(dgp, 0, keepdims=True)
        pg_ref[3:4, :] += 0.5 * jnp.sum(jnp.mean(err * err, -1, keepdims=True), 0, keepdims=True)

    return pl.pallas_call(
        body, name="ln2_loss", grid=(t // TOK_BLK,),
        in_specs=[_row_spec(d)] * 4 + [_vec_spec(1, d)] * 3 + [_row_spec(d)],
        out_specs=(_row_spec(d), _row_spec(d), _row_spec(d), _row_spec(d), _vec_spec(8, d)),
        out_shape=(jax.ShapeDtypeStruct((t, d), f32), jax.ShapeDtypeStruct((t, d), bf16), jax.ShapeDtypeStruct((t, d), bf16),
                   jax.ShapeDtypeStruct((t, d), bf16), jax.ShapeDtypeStruct((8, d), f32)),
        compiler_params=_params(("arbitrary",)),
    )(h1, ff, pe, gp, b_gate, g, b, target)


def _ln1_bwd(dr2, da, db, xhat, rstd, g):
    t, d = dr2.shape

    def body(dr2_ref, da_ref, db_ref, xh_ref, rs_ref, g_ref, dr_ref, drb_ref, pg_ref):
        i = pl.program_id(0)

        @pl.when(i == 0)
        def _():
            pg_ref[...] = jnp.zeros_like(pg_ref)

        dh = ALPHA * dr2_ref[...] + da_ref[...] + db_ref[...]
        xhat = xh_ref[...]
        dr = _ln_bwd(dh, xhat, rs_ref[:, 0:1], g_ref[...])
        dr_ref[...] = dr
        drb_ref[...] = dr.astype(bf16)
        pg_ref[0:1, :] += jnp.sum(dh * xhat, 0, keepdims=True)
        pg_ref[1:2, :] += jnp.sum(dh, 0, keepdims=True)

    return pl.pallas_call(
        body, name="ln1_bwd", grid=(t // TOK_BLK,),
        in_specs=[_row_spec(d)] * 4 + [_row_spec(LANES), _vec_spec(1, d)],
        out_specs=(_row_spec(d), _row_spec(d), _vec_spec(8, d)),
        out_shape=(jax.ShapeDtypeStruct((t, d), f32), jax.ShapeDtypeStruct((t, d), bf16), jax.ShapeDtypeStruct((8, d), f32)),
        compiler_params=_params(("arbitrary",)),
    )(dr2, da, db, xhat, rstd, g)


def _ln_in_bwd(x, dr1, dmm, g, comm=None):
    t, d = x.shape

    def body(x_ref, dr1_ref, dmm_ref, g_ref, dx_ref, pg_ref):
        i = pl.program_id(0)

        @pl.when(i == 0)
        def _():
            pg_ref[...] = jnp.zeros_like(pg_ref)

        dh = ALPHA * dr1_ref[...] + dmm_ref[...]
        _, xhat, rstd = _ln_fwd(x_ref[...], g_ref[...], 0.0)
        dx_ref[...] = _ln_bwd(dh, xhat, rstd, g_ref[...])
        pg_ref[0:1, :] += jnp.sum(dh * xhat, 0, keepdims=True)
        pg_ref[1:2, :] += jnp.sum(dh, 0, keepdims=True)

    return _hosted(
        body, comm, name="ln_in_bwd", grid=(t // TOK_BLK,),
        in_specs=[_row_spec(d)] * 3 + [_vec_spec(1, d)],
        out_specs=(_row_spec(d), _vec_spec(8, d)),
        out_shape=(jax.ShapeDtypeStruct((t, d), f32), jax.ShapeDtypeStruct((8, d), f32)),
        args=(x, dr1, dmm, g))


def _tri(n, upper=False, strict=False):
    r = lax.broadcasted_iota(jnp.int32, (n, n), 0)
    c = lax.broadcasted_iota(jnp.int32, (n, n), 1)
    if upper:
        m = (c > r) if strict else (c >= r)
    else:
        m = (c < r) if strict else (c <= r)
    return jnp.where(m, 1.0, 0.0).astype(f32)


def _gate_values(x, bias, alog, lane):
    z = x + bias
    return jnp.where(lane < 4, _sigmoid(z), jnp.where(lane < 8, -jnp.exp(alog) * _softplus(z), jnp.where(lane < 16, -_softplus(-z), 0.0)))


def _gates(proj, bias_row, alog_row):
    t = proj.shape[0]
    nch = t // CHUNK

    def body(x_ref, bias_ref, alog_ref, gates_ref, gcum_ref, gcumt_ref):
        lane = _lane((t, LANES))
        gates = _gate_values(x_ref[...], bias_ref[...], alog_ref[...], lane)
        gates_ref[...] = gates
        g3 = gates.reshape(nch, CHUNK, LANES)
        tri = jnp.broadcast_to(_tri(CHUNK)[None], (nch, CHUNK, CHUNK))
        loc = jnp.einsum("bij,bjk->bik", tri, g3, precision=HI, preferred_element_type=f32)
        tot = jnp.sum(g3, axis=1)
        offs = _dot(_tri(nch, strict=True), tot)
        glob = loc + offs[:, None, :]
        lane3 = _lane((nch, CHUNK, LANES))
        gcum = jnp.where(lane3 < 4, g3, jnp.where(lane3 < 8, loc, glob)).reshape(t, LANES)
        gcum_ref[...] = gcum
        gcumt_ref[...] = gcum.T

    return pl.pallas_call(
        body, name="gates", grid=(1,),
        in_specs=[pl.BlockSpec((t, LANES), lambda i: (0, SEG_SMALL // LANES)), _vec_spec(1, LANES), _vec_spec(1, LANES)],
        out_specs=(pl.BlockSpec((t, LANES), lambda i: (0, 0)), pl.BlockSpec((t, LANES), lambda i: (0, 0)),
                   pl.BlockSpec((LANES, t), lambda i: (0, 0))),
        out_shape=(jax.ShapeDtypeStruct((t, LANES), f32), jax.ShapeDtypeStruct((t, LANES), f32), jax.ShapeDtypeStruct((LANES, t), f32)),
        compiler_params=_params(("arbitrary",)),
    )(proj, bias_row, alog_row)


def _gates_bwd(proj, bias_row, alog_row, gates, dgates, dccol, dct):
    t = proj.shape[0]
    nch = t // CHUNK

    def body(x_ref, bias_ref, alog_ref, gates_ref, dg_ref, dcc_ref, dct_ref, dx_ref, pg_ref):
        lane = _lane((t, LANES))
        d = dg_ref[...] + dcc_ref[...] + dct_ref[...].T
        d3 = d.reshape(nch, CHUNK, LANES)
        tri = jnp.broadcast_to(_tri(CHUNK, upper=True)[None], (nch, CHUNK, CHUNK))
        loc = jnp.einsum("bij,bjk->bik", tri, d3, precision=HI, preferred_element_type=f32)
        tot = jnp.sum(d3, axis=1)
        offs = _dot(_tri(nch, upper=True, strict=True), tot)
        glob = loc + offs[:, None, :]
        lane3 = _lane((nch, CHUNK, LANES))
        dpre = jnp.where(lane3 < 4, d3, jnp.where(lane3 < 8, loc, glob)).reshape(t, LANES)
        z = x_ref[...] + bias_ref[...]
        sg = _sigmoid(z)
        dx = jnp.where(lane < 4, dpre * sg * (1.0 - sg),
                       jnp.where(lane < 8, dpre * (-jnp.exp(alog_ref[...])) * sg, jnp.where(lane < 16, dpre * (1.0 - sg), 0.0)))
        dx_ref[...] = dx.astype(bf16)
        pg_ref[...] = jnp.zeros_like(pg_ref)
        pg_ref[0:1, :] = jnp.sum(dx, 0, keepdims=True)
        pg_ref[1:2, :] = jnp.sum(jnp.where((lane >= 4) & (lane < 8), dpre * gates_ref[...], 0.0), 0, keepdims=True)

    full = pl.BlockSpec((t, LANES), lambda i: (0, 0))
    return pl.pallas_call(
        body, name="gates_bwd", grid=(1,),
        in_specs=[pl.BlockSpec((t, LANES), lambda i: (0, SEG_SMALL // LANES)), _vec_spec(1, LANES), _vec_spec(1, LANES),
                  full, full, full, pl.BlockSpec((LANES, t), lambda i: (0, 0))],
        out_specs=(full, _vec_spec(8, LANES)),
        out_shape=(jax.ShapeDtypeStruct((t, LANES), bf16), jax.ShapeDtypeStruct((8, LANES), f32)),
        compiler_params=_params(("arbitrary",)),
    )(proj, bias_row, alog_row, gates, dgates, dccol, dct)


def _conv_act(u, cw, row, t):
    c = cw[3:4, :] * u
    for jj in range(CONV_W - 1):
        sh = CONV_W - 1 - jj
        c = c + cw[jj:jj + 1, :] * jnp.where(row >= sh, pltpu.roll(u, sh, axis=0), 0.0)
    return c


def _gdn_conv(proj, conv_w, comm=None):
    t = proj.shape[0]
    nblk = GDN_QKV // LANES

    def body(u_ref, cw_ref, c_ref, y_ref):
        j = pl.program_id(0)
        row = lax.broadcasted_iota(jnp.int32, (t, LANES), 0)
        c = _conv_act(u_ref[...], cw_ref[...], row, t)
        c_ref[...] = c
        s = c * _sigmoid(c)
        r = lax.rsqrt(jnp.sum(s * s, -1, keepdims=True) + NORM_EPS)
        scale = jnp.where(j < GDN_HEADS, GDN_DK ** -0.5, 1.0)
        y_ref[...] = jnp.where(j < 2 * GDN_HEADS, s * (r * scale), s)

    blk = pl.BlockSpec((t, LANES), lambda j: (0, j))
    return _hosted(
        body, comm, name="gdn_conv", grid=(nblk,),
        in_specs=[blk, pl.BlockSpec((CONV_W, LANES), lambda j: (0, j))],
        out_specs=(blk, blk),
        out_shape=(jax.ShapeDtypeStruct((t, GDN_QKV), f32), jax.ShapeDtypeStruct((t, GDN_QKV), f32)),
        args=(proj, conv_w))


def _gdn_conv_bwd(proj, conv_w, c, dy, comm=None):
    t = proj.shape[0]
    nblk = GDN_QKV // LANES

    def body(u_ref, cw_ref, c_ref, dy_ref, du_ref, dcw_ref):
        j = pl.program_id(0)
        row = lax.broadcasted_iota(jnp.int32, (t, LANES), 0)
        u = u_ref[...]
        cw = cw_ref[...]
        c = c_ref[...]
        dy = dy_ref[...]
        sg = _sigmoid(c)
        s = c * sg
        r = lax.rsqrt(jnp.sum(s * s, -1, keepdims=True) + NORM_EPS)
        n = s * r
        scale = jnp.where(j < GDN_HEADS, GDN_DK ** -0.5, 1.0)
        dn = dy * scale
        ds = jnp.where(j < 2 * GDN_HEADS, r * (dn - n * jnp.sum(dn * n, -1, keepdims=True)), dy)
        dc = ds * (sg * (1.0 + c * (1.0 - sg)))
        du = cw[3:4, :] * dc
        dcw_ref[...] = jnp.zeros_like(dcw_ref)
        dcw_ref[3:4, :] = jnp.sum(dc * u, 0, keepdims=True)
        for jj in range(CONV_W - 1):
            sh = CONV_W - 1 - jj
            du = du + cw[jj:jj + 1, :] * jnp.where(row < t - sh, pltpu.roll(dc, t - sh, axis=0), 0.0)
            dcw_ref[jj:jj + 1, :] = jnp.sum(dc * jnp.where(row >= sh, pltpu.roll(u, sh, axis=0), 0.0), 0, keepdims=True)
        du_ref[...] = du.astype(bf16)

    blk = pl.BlockSpec((t, LANES), lambda j: (0, j))
    return _hosted(
        body, comm, name="gdn_conv_bwd", grid=(nblk,),
        in_specs=[blk, pl.BlockSpec((CONV_W, LANES), lambda j: (0, j)), blk, blk],
        out_specs=(blk, pl.BlockSpec((8, LANES), lambda j: (0, j))),
        out_shape=(jax.ShapeDtypeStruct((t, GDN_QKV), bf16), jax.ShapeDtypeStruct((8, GDN_QKV), f32)),
        args=(proj, conv_w, c, dy))


def _chunk_masks():
    r = lax.broadcasted_iota(jnp.int32, (CHUNK, CHUNK), 0)
    c = lax.broadcasted_iota(jnp.int32, (CHUNK, CHUNK), 1)
    return r >= c, r > c, r == c


def _col_to_row(col, eye):
    return jnp.sum(jnp.where(eye, col, 0.0), axis=0, keepdims=True)


def _row_to_col(row, eye):
    return jnp.sum(jnp.where(eye, row, 0.0), axis=1, keepdims=True)


NN = (((1,), (0,)), ((), ()))
NT = (((1,), (1,)), ((), ()))
TN = (((0,), (0,)), ((), ()))
GDN_GROUP = 4


def _mx(a, b, dims=NN, passes=1):
    d = lambda p, q: lax.dot_general(p, q, dims, preferred_element_type=f32)
    ah, bh = a.astype(bf16), b.astype(bf16)
    if passes == 1:
        return d(ah, bh)
    al = (a - ah.astype(f32)).astype(bf16)
    bl = (b - bh.astype(f32)).astype(bf16)
    return d(ah, bh) + (d(ah, bl) + d(al, bh))


def _gdn_decay(gam, masks):
    causal, _, eye = masks
    return jnp.exp(jnp.where(causal, gam - _col_to_row(gam, eye), NEG))


def _gdn_local(y, gcum, comm=None):
    t = y.shape[0]
    nch = t // CHUNK
    rows_blk = GDN_GROUP * CHUNK

    def body(y_ref, g_ref, u_ref, w_ref, qk_ref, tinv_ref):
        masks = _chunk_masks()
        _, strict, eye = masks
        ids = [(j, h) for j in range(GDN_GROUP) for h in range(GDN_HEADS)]
        rs = lambda j: slice(j * CHUNK, (j + 1) * CHUNK)
        col = lambda base, h: slice(base + h * LANES, base + (h + 1) * LANES)
        kn = [y_ref[rs(j), col(512, h)] for j, h in ids]
        beta = [g_ref[rs(j), h:h + 1] for j, h in ids]
        gam = [g_ref[rs(j), 4 + h:5 + h] for j, h in ids]
        dec = [_gdn_decay(g, masks) for g in gam]
        x = [-jnp.where(strict, _mx(k, k, NT) * d * b, 0.0) for k, d, b in zip(kn, dec, beta)]
        tinv = [jnp.where(eye, 1.0, 0.0) + a for a in x]
        for _ in range(5):
            x = [_mx(a, a, NN, 3) for a in x]
            tinv = [t_ + _mx(t_, a, NN, 3) for t_, a in zip(tinv, x)]
        for (j, h), t_, k, d, b, g in zip(ids, tinv, kn, dec, beta, gam):
            u_ref[rs(j), col(0, h)] = _mx(t_, b * y_ref[rs(j), col(1024, h)])
            w_ref[rs(j), col(0, h)] = _mx(t_, (b * jnp.exp(g)) * k)
            qk_ref[j, h] = _mx(y_ref[rs(j), col(0, h)], k, NT) * d
            tinv_ref[j, h] = t_

    mat = pl.BlockSpec((GDN_GROUP, GDN_HEADS, CHUNK, CHUNK), lambda n: (n, 0, 0, 0))
    return _hosted(
        body, comm, name="gdn_local", grid=(nch // GDN_GROUP,),
        in_specs=[pl.BlockSpec((rows_blk, GDN_QKV), lambda n: (n, 0)), pl.BlockSpec((rows_blk, LANES), lambda n: (n, 0))],
        out_specs=(pl.BlockSpec((rows_blk, 512), lambda n: (n, 0)), pl.BlockSpec((rows_blk, 512), lambda n: (n, 0)), mat, mat),
        out_shape=(jax.ShapeDtypeStruct((t, 512), f32), jax.ShapeDtypeStruct((t, 512), f32),
                   jax.ShapeDtypeStruct((nch, GDN_HEADS, CHUNK, CHUNK), f32), jax.ShapeDtypeStruct((nch, GDN_HEADS, CHUNK, CHUNK), f32)),
        args=(y, gcum))


def _gdn_fwd(y, gcum, u, w, qk, comm=None):
    t = y.shape[0]
    nch = t // CHUNK

    def body(y_ref, g_ref, u_ref, w_ref, qk_ref, o_ref, sall_ref, s_ref):
        @pl.when(pl.program_id(0) == 0)
        def _():
            s_ref[...] = jnp.zeros_like(s_ref)

        heads = range(GDN_HEADS)
        sl = [slice(h * LANES, (h + 1) * LANES) for h in heads]
        gam = [g_ref[:, 4 + h:5 + h] for h in heads]
        gam_last = [g[CHUNK - 1:CHUNK, :] for g in gam]
        s = [s_ref[h] for h in heads]
        for h in heads:
            sall_ref[0, h] = s[h]
        ws = [_mx(w_ref[:, sl[h]], s[h]) for h in heads]
        qs = [_mx(y_ref[:, sl[h]] * jnp.exp(gam[h]), s[h]) for h in heads]
        vn = [u_ref[:, sl[h]] - ws[h] for h in heads]
        av = [_mx(qk_ref[0, h], vn[h]) for h in heads]
        kv = [_mx(y_ref[:, 512 + h * LANES:512 + (h + 1) * LANES] * jnp.exp(gam_last[h] - gam[h]), vn[h], TN) for h in heads]
        for h in heads:
            o_ref[:, sl[h]] = qs[h] + av[h]
            s_ref[h] = jnp.exp(gam_last[h]) * s[h] + kv[h]

    row = lambda width: pl.BlockSpec((CHUNK, width), lambda n: (n, 0))
    return _hosted(
        body, comm, name="gdn_fwd", grid=(nch,),
        in_specs=[row(GDN_QKV), row(LANES), row(512), row(512), pl.BlockSpec((1, GDN_HEADS, CHUNK, CHUNK), lambda n: (n, 0, 0, 0))],
        out_specs=(row(512), pl.BlockSpec((1, GDN_HEADS, LANES, LANES), lambda n: (n, 0, 0, 0))),
        out_shape=(jax.ShapeDtypeStruct((t, 512), f32), jax.ShapeDtypeStruct((nch, GDN_HEADS, LANES, LANES), f32)),
        scratch_shapes=[pltpu.VMEM((GDN_HEADS, LANES, LANES), f32)],
        args=(y, gcum, u, w, qk))


def _gdn_bwd(y, gcum, u_all, w_all, qk_all, tinv_all, sall, do, comm=None):
    t = y.shape[0]
    nch = t // CHUNK

    def body(y_ref, g_ref, u_ref, w_ref, qk_ref, tinv_ref, sall_ref, do_ref, dy_ref, dg_ref, ds_ref):
        @pl.when(pl.program_id(0) == 0)
        def _():
            ds_ref[...] = jnp.zeros_like(ds_ref)

        masks = _chunk_masks()
        causal, strict, eye = masks
        lane = _lane((CHUNK, LANES))
        row = lax.broadcasted_iota(jnp.int32, (CHUNK, 1), 0)
        heads = range(GDN_HEADS)
        each = lambda f, *ls: [f(*a) for a in zip(*ls)]
        rsum = lambda a: jnp.sum(a, axis=1, keepdims=True)
        sl = [slice(h * LANES, (h + 1) * LANES) for h in heads]
        qn = [y_ref[:, sl[h]] for h in heads]
        kn = [y_ref[:, 512 + h * LANES:512 + (h + 1) * LANES] for h in heads]
        v = [y_ref[:, 1024 + h * LANES:1024 + (h + 1) * LANES] for h in heads]
        beta = [g_ref[:, h:h + 1] for h in heads]
        gam = [g_ref[:, 4 + h:5 + h] for h in heads]
        gam_last = [g[CHUNK - 1:CHUNK, :] for g in gam]
        dec = [_gdn_decay(g, masks) for g in gam]
        e = [jnp.exp(g) for g in gam]
        f = each(lambda gl_, g: jnp.exp(gl_ - g), gam_last, gam)
        gl = [jnp.exp(g) for g in gam_last]
        u = [u_ref[:, sl[h]] for h in heads]
        w = [w_ref[:, sl[h]] for h in heads]
        qk = [qk_ref[0, h] for h in heads]
        tinv = [tinv_ref[0, h] for h in heads]
        s = [sall_ref[0, h] for h in heads]
        dsn = [ds_ref[h] for h in heads]
        d_o = [do_ref[:, sl[h]] for h in heads]
        qd = each(lambda a, b: a * b, qn, e)
        kd = each(lambda a, b: a * b, kn, f)
        ws = each(_mx, w, s)
        kds = each(_mx, kd, dsn)
        qkdo = each(lambda a, b: _mx(a, b, TN), qk, d_o)
        dqd = each(lambda a, b: _mx(a, b, NT), d_o, s)
        qddo = each(lambda a, b: _mx(a, b, TN), qd, d_o)
        kkd = each(lambda k, d: _mx(k, k, NT) * d, kn, dec)
        vn = each(lambda a, b: a - b, u, ws)
        dvn = each(lambda a, b: a + b, qkdo, kds)
        dqk = each(lambda a, b: jnp.where(causal, _mx(a, b, NT), 0.0), d_o, vn)
        dkd = each(lambda a, b: _mx(a, b, NT), vn, dsn)
        dw = each(lambda a, b: -_mx(a, b, NT), dvn, s)
        wdvn = each(lambda a, b: _mx(a, b, TN), w, dvn)
        dgl = each(lambda a, b: jnp.sum(rsum(a * b), axis=0, keepdims=True), dsn, s)
        for h in heads:
            ds_ref[h] = qddo[h] - wdvn[h] + gl[h] * dsn[h]
        dru = each(lambda a, b: _mx(a, b, TN), tinv, dvn)
        drw = each(lambda a, b: _mx(a, b, TN), tinv, dw)
        dqkr = each(lambda a, b: a * b, dqk, dec)
        dq1 = each(_mx, dqkr, kn)
        dk1 = each(lambda a, b: _mx(a, b, TN), dqkr, qn)
        dnu = each(lambda a, b: _mx(a, b, NT), dru, u)
        dnw = each(lambda a, b: _mx(a, b, NT), drw, w)
        dn = each(lambda a, b: jnp.where(strict, -(a + b), 0.0), dnu, dnw)
        dkk = each(lambda a, b, d: a * b * d, dn, beta, dec)
        dk2 = each(_mx, dkk, kn)
        dk3 = each(lambda a, b: _mx(a, b, TN), dkk, kn)
        dgates = jnp.zeros((CHUNK, LANES), f32)
        for h in heads:
            drw_k = rsum(drw[h] * kn[h])
            dbeta = rsum(dru[h] * v[h]) + e[h] * drw_k + rsum(dn[h] * kkd[h])
            m = dn[h] * (kkd[h] * beta[h]) + dqk[h] * qk[h]
            de = beta[h] * drw_k + rsum(dqd[h] * qn[h])
            df = rsum(dkd[h] * kn[h])
            dgam = rsum(m) - _row_to_col(jnp.sum(m, axis=0, keepdims=True), eye) + de * e[h] - df * f[h]
            dgam_last = jnp.sum(df * f[h], axis=0, keepdims=True) + dgl[h] * gl[h]
            dgam = dgam + jnp.where(row == CHUNK - 1, dgam_last, 0.0)
            dy_ref[:, sl[h]] = dq1[h] + dqd[h] * e[h]
            dy_ref[:, 512 + h * LANES:512 + (h + 1) * LANES] = (beta[h] * e[h]) * drw[h] + dk2[h] + dk3[h] + dk1[h] + dkd[h] * f[h]
            dy_ref[:, 1024 + h * LANES:1024 + (h + 1) * LANES] = beta[h] * dru[h]
            dgates = dgates + jnp.where(lane == h, dbeta, 0.0) + jnp.where(lane == 4 + h, dgam, 0.0)
        dg_ref[...] = dgates

    rev = lambda width: pl.BlockSpec((CHUNK, width), lambda n: (nch - 1 - n, 0))
    mat = lambda d: pl.BlockSpec((1, GDN_HEADS, d, d), lambda n: (nch - 1 - n, 0, 0, 0))
    return _hosted(
        body, comm, name="gdn_bwd", grid=(nch,),
        in_specs=[rev(GDN_QKV), rev(LANES), rev(512), rev(512), mat(CHUNK), mat(CHUNK), mat(LANES), rev(512)],
        out_specs=(rev(GDN_QKV), rev(LANES)),
        out_shape=(jax.ShapeDtypeStruct((t, GDN_QKV), f32), jax.ShapeDtypeStruct((t, LANES), f32)),
        scratch_shapes=[pltpu.VMEM((GDN_HEADS, LANES, LANES), f32)],
        args=(y, gcum, u_all, w_all, qk_all, tinv_all, sall, do))


FOX_CLASSES = 4


def _fox_groups(t):
    nq = t // FOX_BQ
    ncls = min(FOX_CLASSES, nq)
    per = nq // ncls
    return [(g * per, per, (g + 1) * per * FOX_BQ) for g in range(ncls)]


def _fox_scores(q_ref, k_ref, gcum_ref, gcumt_ref, h, i, keys):
    pr = h // 2
    lo = (h % 2) * FOX_DH
    lane = _lane((FOX_BQ, LANES))
    mask = (lane >= lo) & (lane < lo + FOX_DH)
    qm = jnp.where(mask, q_ref[:, pr * LANES:(pr + 1) * LANES], 0.0).astype(bf16)
    kp = k_ref[:, pr * LANES:(pr + 1) * LANES].astype(bf16)
    s = _dot_nt(qm, kp, None) * (FOX_DH ** -0.5)
    s = s + gcum_ref[:, 8 + h:9 + h] - gcumt_ref[8 + h:9 + h, :]
    rows = i * FOX_BQ + lax.broadcasted_iota(jnp.int32, (FOX_BQ, keys), 0)
    cols = lax.broadcasted_iota(jnp.int32, (FOX_BQ, keys), 1)
    return jnp.where(cols <= rows, s, NEG), mask, qm, kp


def _fox_fwd(proj, gcum, gcumt, ride=None):
    c0 = SEG_FOX // 512

    def group_call(q0, nq, keys, comm):
        def body(q_ref, k_ref, v_ref, gcum_ref, gcumt_ref, o_ref, lse_ref):
            i = q0 + pl.program_id(0)
            lane = _lane((FOX_BQ, LANES))
            lse_all = jnp.zeros((FOX_BQ, LANES), f32)
            for pr in range(FOX_HEADS // 2):
                vp = v_ref[:, pr * LANES:(pr + 1) * LANES].astype(bf16)
                o_pair = jnp.zeros((FOX_BQ, LANES), f32)
                for h in (2 * pr, 2 * pr + 1):
                    s, mask, _, _ = _fox_scores(q_ref, k_ref, gcum_ref, gcumt_ref, h, i, keys)
                    m = jnp.max(s, axis=1, keepdims=True)
                    p = jnp.exp(s - m)
                    l = jnp.sum(p, axis=1, keepdims=True)
                    o_h = _dot((p * (1.0 / l)).astype(bf16), vp, None)
                    o_pair = jnp.where(mask, o_h, o_pair)
                    lse_all = jnp.where(lane == h, m + jnp.log(l), lse_all)
                o_ref[:, pr * LANES:(pr + 1) * LANES] = o_pair
            lse_ref[...] = lse_all

        seen = lambda col: pl.BlockSpec((keys, 512), lambda i: (0, col))
        return _hosted(
            body, comm, name=f"fox_fwd_{keys}", grid=(nq,),
            in_specs=[pl.BlockSpec((FOX_BQ, 512), lambda i: (q0 + i, c0)), seen(c0 + 1), seen(c0 + 2),
                      pl.BlockSpec((FOX_BQ, LANES), lambda i: (q0 + i, 0)), pl.BlockSpec((LANES, keys), lambda i: (0, 0))],
            out_specs=(pl.BlockSpec((FOX_BQ, 512), lambda i: (i, 0)), pl.BlockSpec((FOX_BQ, LANES), lambda i: (i, 0))),
            out_shape=(jax.ShapeDtypeStruct((nq * FOX_BQ, 512), f32), jax.ShapeDtypeStruct((nq * FOX_BQ, LANES), f32)),
            args=(proj, proj, proj, gcum, gcumt))

    parts = []
    for n, g in enumerate(_fox_groups(proj.shape[0])):
        hook = ride(n) if ride else None
        part, moved = group_call(*g, hook[0] if hook else None)
        parts.append(part)
        if hook:
            hook[1](moved)
    return jnp.concatenate([o for o, _ in parts], axis=0), jnp.concatenate([l for _, l in parts], axis=0)


def _fox_bwd(proj, gcum, gcumt, o, lse, do, ride=None):
    t = proj.shape[0]
    c0 = SEG_FOX // 512

    def group_call(q0, nq, keys, acc, comm):
        first = acc is None

        def body(q_ref, k_ref, v_ref, gcum_ref, gcumt_ref, o_ref, lse_ref, do_ref, *rest):
            dq_ref, dk_ref, dv_ref, dcc_ref, dct_ref = rest[-5:]
            j = pl.program_id(0)
            i = q0 + j

            @pl.when(j == 0)
            def _():
                if first:
                    dk_ref[...] = jnp.zeros_like(dk_ref)
                    dv_ref[...] = jnp.zeros_like(dv_ref)
                    dct_ref[...] = jnp.zeros_like(dct_ref)
                else:
                    dk_ref[...], dv_ref[...], dct_ref[...] = rest[0][...], rest[1][...], rest[2][...]

            lane = _lane((FOX_BQ, LANES))
            dcc = jnp.zeros((FOX_BQ, LANES), f32)
            scale = FOX_DH ** -0.5
            for pr in range(FOX_HEADS // 2):
                sl = slice(pr * LANES, (pr + 1) * LANES)
                vp = v_ref[:, sl].astype(bf16)
                dq_pair = jnp.zeros((FOX_BQ, LANES), f32)
                for h in (2 * pr, 2 * pr + 1):
                    s, mask, qm, kp = _fox_scores(q_ref, k_ref, gcum_ref, gcumt_ref, h, i, keys)
                    p = jnp.exp(s - lse_ref[:, h:h + 1])
                    dom = jnp.where(mask, do_ref[:, sl], 0.0)
                    delta = jnp.sum(dom * o_ref[:, sl], axis=1, keepdims=True)
                    domb = dom.astype(bf16)
                    ds = p * (_dot_nt(domb, vp, None) - delta)
                    dsb = ds.astype(bf16)
                    dv_ref[:, sl] += _dot_tn(p.astype(bf16), domb, None)
                    dk_ref[:, sl] += _dot_tn(dsb, qm, None) * scale
                    dq_pair = jnp.where(mask, _dot(dsb, kp, None) * scale, dq_pair)
                    dcc = jnp.where(lane == 8 + h, jnp.sum(ds, axis=1, keepdims=True), dcc)
                    dct_ref[8 + h:9 + h, :] += -jnp.sum(ds, axis=0, keepdims=True)
                dq_ref[:, sl] = dq_pair.astype(bf16)
            dcc_ref[...] = dcc

        qblk = lambda col: pl.BlockSpec((FOX_BQ, 512), lambda i: (q0 + i, col))
        oblk = pl.BlockSpec((FOX_BQ, 512), lambda i: (i, 0))
        seen = lambda col: pl.BlockSpec((keys, 512), lambda i: (0, col))
        rblk = pl.BlockSpec((FOX_BQ, LANES), lambda i: (q0 + i, 0))
        seen_t = pl.BlockSpec((LANES, keys), lambda i: (0, 0))
        in_specs = [qblk(c0), seen(c0 + 1), seen(c0 + 2), rblk, seen_t, qblk(0), rblk, qblk(0)]
        args = [proj, proj, proj, gcum, gcumt, o, lse, do]
        aliases = {}
        if not first:
            in_specs += [seen(0), seen(0), seen_t]
            args += list(acc)
            aliases = {8: 1, 9: 2, 10: 4}
        return _hosted(
            body, comm, name=f"fox_bwd_{keys}", grid=(nq,), in_specs=in_specs,
            out_specs=(oblk, seen(0), seen(0), pl.BlockSpec((FOX_BQ, LANES), lambda i: (i, 0)), seen_t),
            out_shape=(jax.ShapeDtypeStruct((nq * FOX_BQ, 512), bf16), jax.ShapeDtypeStruct((t, 512), f32), jax.ShapeDtypeStruct((t, 512), f32),
                       jax.ShapeDtypeStruct((nq * FOX_BQ, LANES), f32), jax.ShapeDtypeStruct((LANES, t), f32)),
            aliases=aliases, args=args)

    acc, dqs, dccs = None, [], []
    for n, g in enumerate(reversed(_fox_groups(t))):
        hook = ride(n) if ride else None
        (dq, dk, dv, dcc, dct), moved = group_call(*g, acc, hook[0] if hook else None)
        if hook:
            hook[1](moved)
        acc = (dk, dv, dct)
        dqs.insert(0, dq)
        dccs.insert(0, dcc)
    return jnp.concatenate(dqs, axis=0), acc[0], acc[1], jnp.concatenate(dccs, axis=0), acc[2]


def _row(v, width=None):
    v = v.reshape(1, -1).astype(f32)
    if width is not None and v.shape[1] < width:
        v = jnp.pad(v, ((0, 0), (0, width - v.shape[1])))
    return v


LATE = ("w_out", "w_up", "w_ple_gate", "w_ple", "w_down")


def _device_grads(x, p, target, small, w_cat, conv_w, late, qc=None, tail=None, ln_in_out=None):
    z4 = jnp.zeros((4,), f32)
    bias_row = _row(jnp.concatenate([z4, small["dt_bias"].reshape(-1), small["b_f"].reshape(-1)]), LANES)
    alog_row = _row(jnp.concatenate([z4, small["a_log"].reshape(-1)]), LANES)
    g_gdn = _row(small["gdn_norm_g"])
    g_fox2 = _row(jnp.tile(small["fox_norm_g"].reshape(-1), 2))
    pb = p.astype(bf16)
    late = list(late)
    comm = qc is not None

    h0, h0b = ln_in_out if ln_in_out is not None else _ln_in(x, _row(small["ln_in_g"]), _row(small["ln_in_b"]))[0]
    proj = _mm(h0b, w_cat, "nt", 256, D_CAT, "mm_proj")
    gates, gcum, gcumt = _gates(proj, bias_row, alog_row)
    w_down_pieces = [(4, k, 4) for k in range(4)]

    def gather(phase, pieces):
        if not comm or not pieces:
            return None, lambda moved: None
        touched = sorted({i for i, _, _ in pieces})

        def took(moved):
            for i, buf in zip(touched, moved):
                late[i] = buf
        return phase([late[i] for i in touched], [(touched.index(i), k, n) for i, k, n in pieces]), took

    over, on = _gather_chips, _gather_pass_on
    cm, took = gather(over, [(0, 0, 1), (3, 0, 1)])
    (conv_c, qkv_n), moved = _gdn_conv(proj, conv_w, cm)
    took(moved)
    cm, took = gather(over, [(1, 0, 2)])
    (gu, gw, gqk, gtinv), moved = _gdn_local(qkv_n, gcum, cm)
    took(moved)
    cm, took = gather(over, [(1, 1, 2)])
    (o_gdn, sall), moved = _gdn_fwd(qkv_n, gcum, gu, gw, gqk, cm)
    took(moved)
    fox_plan = [(over, [(2, 0, 1)]), (on, [(0, 0, 1), (3, 0, 1), (1, 0, 2), (1, 1, 2)]), (over, w_down_pieces[:1]), (over, w_down_pieces[1:2])]
    assert not comm or len(_fox_groups(x.shape[0])) == len(fox_plan)
    o_fox, lse = _fox_fwd(proj, gcum, gcumt, (lambda n: gather(*fox_plan[n])) if comm else None)
    cm, took = gather(on, [(2, 0, 1)])
    (attn,), moved = _attn_post(o_gdn, proj, o_fox, g_gdn, g_fox2, cm)
    took(moved)
    mix = _mm(attn, late[0].reshape(D_MODEL, D_MODEL), "nn", 512, D_MODEL, "mm_mix")
    cm, took = gather(over, w_down_pieces[2:3])
    (h1, h1b, xhat1, rstd1), moved = _ln1(h0, mix, _row(small["ln1_g"]), _row(small["ln1_b"]), cm)
    took(moved)
    w_gate = late[2].reshape(D_MODEL, D_MODEL)
    gp = _mm(h1b, w_gate, "nn", 512, D_MODEL, "mm_gate")
    cm, took = gather(over, w_down_pieces[3:])
    pe = _mm(pb, late[3], "nn", 512, D_MODEL // N_CHIPS, "mm_ple", shards=N_CHIPS, comm=cm)
    if cm:
        pe, moved = pe
        took(moved)
    w_up = late[1]
    cm, took = gather(on, w_down_pieces)
    up_act = _mm(h1b, w_up, "nn", 256, 1024, "mm_up", epi="relu2", shards=N_CHIPS, comm=cm)
    if cm:
        up_act, moved = up_act
        took(moved)
    up, act = up_act
    w_out, w_ple, w_down = late[0].reshape(D_MODEL, D_MODEL), late[3], late[4].reshape(D_FF, D_MODEL)
    ff = _mm(act, w_down, "nn", 256, D_MODEL, "mm_down")
    dr2, dr2b, dpe, dgp, pg2 = _ln2_loss(h1, ff, pe, gp, _row(small["b_ple_gate"]), _row(small["ln2_g"]), _row(small["ln2_b"]), target)

    dup = _mm(dr2b, w_down, "nt", 256, 2048, "mm_dact", epi="relu2_bwd", extra=up)
    g_down = _mm(act, dr2b, "tn", 1024, D_MODEL, "mm_gdown")
    dh1_a = _mm(dup, w_up, "nt", 256, D_MODEL, "mm_dh1a", shards=N_CHIPS)
    g_up = _mm(h1b, dup, "tn", 1024, 1024, "mm_gup", shards=N_CHIPS)
    dh1_b = _mm(dgp, w_gate, "nt", 512, D_MODEL, "mm_dh1b")
    g_gate = _mm(h1b, dgp, "tn", 1024, D_MODEL, "mm_ggate")
    g_ple = _mm(pb, dpe, "tn", D_PLE, D_MODEL // N_CHIPS, "mm_gple", shards=N_CHIPS)
    dr1, dr1b, pg1 = _ln1_bwd(dr2, dh1_a, dh1_b, xhat1, rstd1, _row(small["ln1_g"]))
    dattn = _mm(dr1b, w_out, "nt", 512, D_MODEL, "mm_dattn")
    g_out = _mm(attn, dr1b, "tn", 1024, D_MODEL, "mm_gout")
    do_gdn, dz, do_fox, pga = _attn_post_bwd(dattn, o_gdn, proj, o_fox, g_gdn, g_fox2)
    g_late = [g.reshape((N_CHIPS, -1, g.shape[-1])) for g in (g_out, g_up, g_gate, g_ple, g_down)]
    chip_plan = [[(4, 0, 2), (0, 0, 1)], [(4, 1, 2)], [(2, 0, 1), (3, 0, 1)], [(1, 0, 2), (1, 1, 2)]]
    state = {}

    def to_sibling():
        def took(moved):
            sums = [_add_pair(g, b1, qc, "add_pair_" + n) for g, b1, n in zip(g_late, moved, LATE)]
            state.update(own=[a for a, _ in sums], sent=[ab for _, ab in sums], landing=_landing([ab for _, ab in sums]))
        return _exchange_pairs(g_late), took

    def to_chips(pieces):
        if not comm:
            return None, lambda moved: None
        return _exchange_chips(state["sent"], state["landing"], pieces), lambda moved: state.update(landing=list(moved))

    assert not comm or len(_fox_groups(x.shape[0])) == len(chip_plan)
    dfq, dfk, dfv, dccol, dct = _fox_bwd(proj, gcum, gcumt, o_fox, lse, do_fox,
                                         (lambda n: to_sibling() if n == 0 else to_chips(chip_plan[n - 1])) if comm else None)
    cm, took = to_chips(chip_plan[-1])
    (dqkv_n, dgates), moved = _gdn_bwd(qkv_n, gcum, gu, gw, gqk, gtinv, sall, do_gdn, cm)
    took(moved)
    dsmall, pgg = _gates_bwd(proj, bias_row, alog_row, gates, dgates, dccol, dct)
    cm = None
    if comm:
        cm = _share_halves([_add_chips(a, b2, qc, "add_chips_" + n) for a, b2, n in zip(state["own"], state["landing"], LATE)])
    (du, g_conv8), reduced = _gdn_conv_bwd(proj, conv_w, conv_c, dqkv_n, cm)
    if comm:
        g_late = list(reduced)
    t = x.shape[0]
    dproj = jnp.concatenate([du, dz, dfq, dfk.astype(bf16), dfv.astype(bf16), dsmall, jnp.zeros((t, D_CAT - SEG_SMALL - LANES), bf16)], axis=1)
    g_cat = _mm(dproj, h0b, "tn", 1280, D_MODEL, "mm_gcat")
    cm, took = tail[0](g_cat) if tail else (None, None)
    dh0_mm = _mm(dproj, w_cat, "nn", 256, D_MODEL, "mm_dh0", comm=cm)
    if cm:
        dh0_mm, moved = dh0_mm
        took(moved)
    cm, took = tail[1]() if tail and tail[1] else (None, None)
    (grad_x, pg0), moved = _ln_in_bwd(x, dr1, dh0_mm, _row(small["ln_in_g"]), cm)
    if cm:
        took(moved)

    g_fox = pga[1, :FOX_DH] + pga[1, FOX_DH:]
    small_grads = dict(
        ln_in_g=pg0[0], ln_in_b=pg0[1], ln1_g=pg1[0], ln1_b=pg1[1], b_ple_gate=pg2[2], ln2_g=pg2[0], ln2_b=pg2[1],
        gdn_norm_g=pga[0], fox_norm_g=g_fox, a_log=pgg[1, 4:8], dt_bias=pgg[0, 4:8], b_f=pgg[0, 8:16], loss=pg2[3, 0:1])
    return grad_x, g_cat, g_conv8[:CONV_W], dict(zip(LATE, g_late)), small_grads


ANY = pl.BlockSpec(memory_space=pl.ANY)
CONV_PKT_ROWS = 16


def _mesh_pos():
    return lax.axis_index("x"), lax.axis_index("y"), lax.axis_index("c")


def _other_chips(x, y):
    return [(1 - x, y), (x, 1 - y), (1 - x, 1 - y)]


def _rcopy(src, dst, send_sem, recv_sem, dev):
    return pltpu.make_async_remote_copy(src_ref=src, dst_ref=dst, send_sem=send_sem, recv_sem=recv_sem,
                                        device_id=dev, device_id_type=MESH)


class _Comm:
    def __init__(self, ins, outs, aliases, n_sems, start, finish):
        self.ins, self.outs, self.aliases, self.n_sems, self.start, self.finish = list(ins), list(outs), dict(aliases), n_sems, start, finish


def _hosted(body, comm, *, name, grid, in_specs, out_specs, out_shape, args, scratch_shapes=(), aliases=None):
    n_in, n_out, n_sc = len(in_specs), len(out_specs), len(scratch_shapes)
    k, ko = (len(comm.ins), len(comm.outs)) if comm else (0, 0)

    def kernel_body(*refs):
        o0 = n_in + k
        s0 = o0 + n_out + ko
        if comm:
            cins, couts, (ssem, rsem) = refs[n_in:o0], refs[o0 + n_out:s0], refs[s0 + n_sc:]
            step = pl.program_id(0)
            for d in range(1, len(grid)):
                step = step * grid[d] + pl.program_id(d)

            @pl.when(step == 0)
            def _():
                comm.start(cins, couts, ssem, rsem)

        body(*refs[:n_in], *refs[o0:o0 + n_out], *refs[s0:s0 + n_sc])
        if comm:
            last = 1
            for n in grid:
                last *= n

            @pl.when(step == last - 1)
            def _():
                comm.finish(cins, couts, ssem, rsem)

    io_aliases = dict(aliases or {})
    scratch = list(scratch_shapes)
    if comm:
        io_aliases.update({n_in + i: n_out + j for i, j in comm.aliases.items()})
        scratch += [pltpu.SemaphoreType.DMA((comm.n_sems,)), pltpu.SemaphoreType.DMA((comm.n_sems,))]
    res = pl.pallas_call(
        kernel_body, name=name, grid=grid, in_specs=list(in_specs) + [ANY] * k, out_specs=tuple(out_specs) + (ANY,) * ko,
        out_shape=tuple(out_shape) + tuple(comm.outs if comm else ()), scratch_shapes=scratch, input_output_aliases=io_aliases,
        compiler_params=_params(("arbitrary",) * len(grid)),
    )(*args, *(comm.ins if comm else ()))
    return tuple(res[:n_out]), tuple(res[n_out:])


def _comm_only(phases, name):
    n_in = sum(len(p.ins) for p in phases)

    def body(*refs):
        n_out = sum(len(p.outs) for p in phases)
        sems = refs[n_in + n_out:]
        i0, o0 = 0, n_in
        for j, p in enumerate(phases):
            cins, couts = refs[i0:i0 + len(p.ins)], refs[o0:o0 + len(p.outs)]
            p.start(cins, couts, sems[2 * j], sems[2 * j + 1])
            p.finish(cins, couts, sems[2 * j], sems[2 * j + 1])
            i0 += len(p.ins)
            o0 += len(p.outs)

    aliases, i0, o0 = {}, 0, 0
    for p in phases:
        aliases.update({i0 + i: o0 + j for i, j in p.aliases.items()})
        i0 += len(p.ins)
        o0 += len(p.outs)
    outs = [o for p in phases for o in p.outs]
    res = pl.pallas_call(
        body, name=name, out_shape=tuple(outs), in_specs=[ANY] * n_in, out_specs=(ANY,) * len(outs), input_output_aliases=aliases,
        scratch_shapes=[pltpu.SemaphoreType.DMA((p.n_sems,)) for p in phases for _ in range(2)],
    )(*[a for p in phases for a in p.ins])
    split, o0 = [], 0
    for p in phases:
        split.append(tuple(res[o0:o0 + len(p.outs)]))
        o0 += len(p.outs)
    return split


def _like(arrays):
    return [jax.ShapeDtypeStruct(a.shape, a.dtype) for a in arrays]


def _half(ref, slot, hf, piece=(0, 1)):
    k, n = piece
    rows = ref.shape[1] // 2 // n
    return ref.at[slot, pl.ds((hf * n + k) * rows, rows)]


def _whole_halves(arrays):
    return [(i, 0, 1) for i in range(len(arrays))]


def _gather_chips(bufs, pieces=None, whole=False, base=0):
    nw = len(bufs)
    pieces = _whole_halves(bufs) if pieces is None else pieces
    part = (lambda ref, slot, c, piece: ref.at[slot]) if whole else _half

    def copies(couts):
        x, y, c = _mesh_pos()
        q = 2 * x + y
        for j, (i, k, n) in enumerate(pieces):
            for kc, chip in enumerate(_other_chips(x, y)):
                mine, theirs = part(couts[i], q, c, (k, n)), part(couts[i], 2 * chip[0] + chip[1], c, (k, n))
                yield base + j * 3 + kc, mine, theirs, (*chip, c)

    def start(cins, couts, ssem, rsem):
        for s, mine, _, dev in copies(couts):
            _rcopy(mine, mine, ssem.at[s], rsem.at[s], dev).start()

    def finish(cins, couts, ssem, rsem):
        for s, _, theirs, dev in copies(couts):
            _rcopy(theirs, theirs, ssem.at[s], rsem.at[s], dev).wait_recv()
        for s, mine, _, dev in copies(couts):
            _rcopy(mine, mine, ssem.at[s], rsem.at[s], dev).wait_send()

    return _Comm(bufs, _like(bufs), {i: i for i in range(nw)}, 3 * len(pieces), start, finish)


def _gather_pass_on(bufs, pieces=None, base=0):
    nw = len(bufs)
    pieces = _whole_halves(bufs) if pieces is None else pieces

    def copies(couts):
        x, y, c = _mesh_pos()
        for j, (i, k, n) in enumerate(pieces):
            for kc, chip in enumerate(_other_chips(x, y)):
                slot = 2 * chip[0] + chip[1]
                yield base + j * 3 + kc, _half(couts[i], slot, c, (k, n)), _half(couts[i], slot, 1 - c, (k, n)), (x, y, 1 - c)

    def start(cins, couts, ssem, rsem):
        for s, landed, _, sib in copies(couts):
            _rcopy(landed, landed, ssem.at[s], rsem.at[s], sib).start()

    def finish(cins, couts, ssem, rsem):
        for s, _, passed, sib in copies(couts):
            _rcopy(passed, passed, ssem.at[s], rsem.at[s], sib).wait_recv()
        for s, landed, _, sib in copies(couts):
            _rcopy(landed, landed, ssem.at[s], rsem.at[s], sib).wait_send()

    return _Comm(bufs, _like(bufs), {i: i for i in range(nw)}, 3 * len(pieces), start, finish)


def _gather_now(bufs, packets):
    nb = len(bufs)
    over, on, pk = _gather_chips(bufs), _gather_pass_on(bufs, base=3 * nb), _gather_chips(packets, whole=True, base=6 * nb)

    def start(cins, couts, ssem, rsem):
        over.start(cins[:nb], couts[:nb], ssem, rsem)
        pk.start(cins[nb:], couts[nb:], ssem, rsem)

    def finish(cins, couts, ssem, rsem):
        over.finish(cins[:nb], couts[:nb], ssem, rsem)
        on.start(cins[:nb], couts[:nb], ssem, rsem)
        on.finish(cins[:nb], couts[:nb], ssem, rsem)
        pk.finish(cins[nb:], couts[nb:], ssem, rsem)

    every = list(bufs) + list(packets)
    return _Comm(every, _like(every), {i: i for i in range(len(every))}, 6 * nb + 3 * len(packets), start, finish)


def _exchange_pairs(gs):
    nw = len(gs)

    def copies(cins, couts):
        x, y, c = _mesh_pos()
        for i in range(nw):
            for d in range(N_CHIPS):
                yield i * N_CHIPS + d, _half(cins[i], d, 1 - c), couts[i].at[d], (x, y, 1 - c)

    def start(cins, couts, ssem, rsem):
        for s, src, dst, sib in copies(cins, couts):
            _rcopy(src, dst, ssem.at[s], rsem.at[s], sib).start()

    def finish(cins, couts, ssem, rsem):
        for s, src, dst, sib in copies(cins, couts):
            _rcopy(src, dst, ssem.at[s], rsem.at[s], sib).wait_recv()
        for s, src, dst, sib in copies(cins, couts):
            _rcopy(src, dst, ssem.at[s], rsem.at[s], sib).wait_send()

    outs = [jax.ShapeDtypeStruct((N_CHIPS, g.shape[1] // 2, g.shape[2]), g.dtype) for g in gs]
    return _Comm(gs, outs, {}, N_CHIPS * nw, start, finish)


def _gather_packets(small):
    def peers():
        x, y, c = _mesh_pos()
        for r in range(1, 8):
            fx, fy, fc = (r >> 2) & 1, (r >> 1) & 1, r & 1
            yield r - 1, (1 - x if fx else x, 1 - y if fy else y, 1 - c if fc else c)

    def start(cins, couts, ssem, rsem):
        x, y, c = _mesh_pos()
        mine = couts[0].at[4 * x + 2 * y + c]
        for s, peer in peers():
            _rcopy(mine, mine, ssem.at[s], rsem.at[s], peer).start()

    def finish(cins, couts, ssem, rsem):
        x, y, c = _mesh_pos()
        mine = couts[0].at[4 * x + 2 * y + c]
        for s, peer in peers():
            theirs = couts[0].at[4 * peer[0] + 2 * peer[1] + peer[2]]
            _rcopy(theirs, theirs, ssem.at[s], rsem.at[s], peer).wait_recv()
        for s, peer in peers():
            _rcopy(mine, mine, ssem.at[s], rsem.at[s], peer).wait_send()

    return _Comm([small], _like([small]), {0: 0}, 7, start, finish)


def _exchange_chips(a4s, b2s, pieces=None):
    nw = len(a4s)
    pieces = _whole_halves(a4s) if pieces is None else pieces

    def copies(cins, couts):
        x, y, c = _mesh_pos()
        for j, (i, k, n) in enumerate(pieces):
            rows = a4s[i].shape[1] // n
            part = pl.ds(k * rows, rows)
            for kc, chip in enumerate(_other_chips(x, y)):
                yield j * 3 + kc, cins[i].at[2 * chip[0] + chip[1], part], couts[i].at[kc, part], (*chip, c)

    def start(cins, couts, ssem, rsem):
        for s, src, dst, dev in copies(cins, couts):
            _rcopy(src, dst, ssem.at[s], rsem.at[s], dev).start()

    def finish(cins, couts, ssem, rsem):
        for s, src, dst, dev in copies(cins, couts):
            _rcopy(src, dst, ssem.at[s], rsem.at[s], dev).wait_recv()
        for s, src, dst, dev in copies(cins, couts):
            _rcopy(src, dst, ssem.at[s], rsem.at[s], dev).wait_send()

    return _Comm(list(a4s) + list(b2s), _like(b2s), {nw + i: i for i in range(nw)}, 3 * len(pieces), start, finish)


def _landing(a4s):
    return [lax.empty((3,) + a.shape[1:], a.dtype) for a in a4s]


def _share_halves(rs):
    nw = len(rs)

    def halves(couts, i, hf):
        rows = rs[i].shape[0] // 2
        return couts[i].at[pl.ds(hf * rows, rows)]

    def start(cins, couts, ssem, rsem):
        x, y, c = _mesh_pos()
        for i in range(nw):
            _rcopy(halves(couts, i, c), halves(couts, i, c), ssem.at[i], rsem.at[i], (x, y, 1 - c)).start()

    def finish(cins, couts, ssem, rsem):
        x, y, c = _mesh_pos()
        for i in range(nw):
            _rcopy(halves(couts, i, 1 - c), halves(couts, i, 1 - c), ssem.at[i], rsem.at[i], (x, y, 1 - c)).wait_recv()
        for i in range(nw):
            _rcopy(halves(couts, i, c), halves(couts, i, c), ssem.at[i], rsem.at[i], (x, y, 1 - c)).wait_send()

    return _Comm(rs, _like(rs), {i: i for i in range(nw)}, nw, start, finish)


ADD_ROWS = 256


def _add_pair(g4, b1, qc_idx, name):
    _, half, cols = b1.shape
    rb = ADD_ROWS if half % ADD_ROWS == 0 else half
    nb = half // rb

    def body(qc_ref, g_ref, b_ref, o_ref, ob_ref):
        a = g_ref[...] + b_ref[...]
        o_ref[...] = a
        ob_ref[...] = a.astype(bf16)

    blk = (1, rb, cols)
    out = pl.BlockSpec(blk, lambda d, i, qc: (d, i, 0))
    return pl.pallas_call(
        body, name=name,
        grid_spec=pltpu.PrefetchScalarGridSpec(
            num_scalar_prefetch=1, grid=(N_CHIPS, nb),
            in_specs=[pl.BlockSpec(blk, lambda d, i, qc: (d, qc[1] * nb + i, 0)), out],
            out_specs=(out, out)),
        out_shape=(jax.ShapeDtypeStruct(b1.shape, f32), jax.ShapeDtypeStruct(b1.shape, bf16)),
        compiler_params=_params(("parallel", "parallel")),
    )(qc_idx, g4, b1)


def _add_chips(a4, b2, qc_idx, name):
    _, half, cols = a4.shape
    rb = ADD_ROWS if half % ADD_ROWS == 0 else half
    nb = half // rb

    def body(qc_ref, a_ref, b_ref, o_ref):
        o_ref[...] = ((a_ref[0] + b_ref[0].astype(f32)) + b_ref[1].astype(f32)) + b_ref[2].astype(f32)

    return pl.pallas_call(
        body, name=name,
        grid_spec=pltpu.PrefetchScalarGridSpec(
            num_scalar_prefetch=1, grid=(nb,),
            in_specs=[pl.BlockSpec((1, rb, cols), lambda i, qc: (qc[0], i, 0)), pl.BlockSpec((3, rb, cols), lambda i, qc: (0, i, 0))],
            out_specs=pl.BlockSpec((rb, cols), lambda i, qc: (qc[1] * nb + i, 0))),
        out_shape=jax.ShapeDtypeStruct((2 * half, cols), f32),
        compiler_params=_params(("parallel",)),
    )(qc_idx, a4, b2)


def _adamw_math(w, g, m, v):
    m = ADAM_B1 * m + (1.0 - ADAM_B1) * g
    v = ADAM_B2 * v + (1.0 - ADAM_B2) * (g * g)
    m_hat = m / (1.0 - ADAM_B1 ** ADAM_STEP)
    v_hat = v / (1.0 - ADAM_B2 ** ADAM_STEP)
    return -ADAM_LR * (m_hat / (jnp.sqrt(v_hat) + ADAM_EPS) + ADAM_WD * w), m, v


def _adamw(w, g, m, v, name, comm=None):
    rows = w.shape[0]
    if w.ndim == 3:
        rb = max(r for r in range(1, ADD_ROWS // 4 + 1) if rows % r == 0)
    else:
        rb = ADD_ROWS if rows % ADD_ROWS == 0 else rows

    def body(w_ref, g_ref, m_ref, v_ref, go_ref, d_ref, mo_ref, vo_ref):
        g = g_ref[...]
        go_ref[...] = g
        d_ref[...], mo_ref[...], vo_ref[...] = _adamw_math(w_ref[...], g, m_ref[...], v_ref[...])

    blk = pl.BlockSpec((rb,) + w.shape[1:], lambda i: (i,) + (0,) * (w.ndim - 1))
    return _hosted(body, comm, name=name, grid=(rows // rb,), in_specs=[blk] * 4, out_specs=(blk,) * 4,
                   out_shape=(jax.ShapeDtypeStruct(w.shape, f32),) * 4, args=(w, g, m, v))


def _small_sum_adamw(all_pkts, w, m, v):
    def body(a_ref, w_ref, m_ref, v_ref, g_ref, d_ref, mo_ref, vo_ref):
        g = a_ref[0]
        for r in range(1, 8):
            g = g + a_ref[r]
        g_ref[...] = g
        d_ref[...], mo_ref[...], vo_ref[...] = _adamw_math(w_ref[...], g, m_ref[...], v_ref[...])

    return pl.pallas_call(body, name="small_sum_adamw", out_shape=(jax.ShapeDtypeStruct(w.shape, f32),) * 4)(all_pkts, w, m, v)


SMALL_LAYOUT = (("ln_in_g", 0, 1024), ("ln_in_b", 8, 1024), ("ln1_g", 16, 1024), ("ln1_b", 24, 1024), ("b_ple_gate", 32, 1024),
                ("ln2_g", 40, 1024), ("ln2_b", 48, 1024), ("gdn_norm_g", 56, 128), ("fox_norm_g", 57, 64), ("a_log", 58, 4),
                ("dt_bias", 59, 4), ("b_f", 60, 8), ("loss", 61, 1))
SMALL_CONV_ROW = 64
SMALL_ROWS = 128


def _pack_small(vals, conv=None):
    rows = []
    nxt = 0
    for n, r0, size in SMALL_LAYOUT:
        assert r0 == nxt
        v = vals[n].reshape(-1).astype(f32) if n in vals else jnp.zeros((size,), f32)
        nrows = -(-size // LANES)
        rows.append(jnp.pad(v, (0, nrows * LANES - size)).reshape(nrows, LANES))
        nxt = r0 + nrows
    rows.append(jnp.zeros((SMALL_CONV_ROW - nxt, LANES), f32))
    conv_rows = CONV_W * GDN_QKV // LANES
    rows.append(jnp.zeros((conv_rows, LANES), f32) if conv is None else conv.reshape(conv_rows, LANES))
    rows.append(jnp.zeros((SMALL_ROWS - SMALL_CONV_ROW - conv_rows, LANES), f32))
    return jnp.concatenate(rows, axis=0)


def _unpack_small(pkt, shapes):
    out = {}
    for n, r0, size in SMALL_LAYOUT:
        if n in shapes:
            nrows = -(-size // LANES)
            out[n] = pkt[r0:r0 + nrows].reshape(-1)[:size].reshape(shapes[n])
    return out


WEIGHTS = ("ln_in_g", "ln_in_b", "w_in", "conv_w", "a_log", "dt_bias", "gdn_norm_g", "b_f", "fox_norm_g", "w_out", "ln1_g", "ln1_b",
           "w_up", "w_down", "w_ple", "w_ple_gate", "b_ple_gate", "ln2_g", "ln2_b")
SMALL_NAMES = tuple(n for n, _, _ in SMALL_LAYOUT if n != "loss")


def kernel(x, p, ln_in_g, ln_in_b, w_in, conv_w, a_log, dt_bias, gdn_norm_g, b_f, fox_norm_g, w_out, ln1_g, ln1_b, w_up, w_down, w_ple, w_ple_gate, b_ple_gate, ln2_g, ln2_b, loss_target, m_ln_in_g, m_ln_in_b, m_w_in, m_conv_w, m_a_log, m_dt_bias, m_gdn_norm_g, m_b_f, m_fox_norm_g, m_w_out, m_ln1_g, m_ln1_b, m_w_up, m_w_down, m_w_ple, m_w_ple_gate, m_b_ple_gate, m_ln2_g, m_ln2_b, v_ln_in_g, v_ln_in_b, v_w_in, v_conv_w, v_a_log, v_dt_bias, v_gdn_norm_g, v_b_f, v_fox_norm_g, v_w_out, v_ln1_g, v_ln1_b, v_w_up, v_w_down, v_w_ple, v_w_ple_gate, v_b_ple_gate, v_ln2_g, v_ln2_b):
    given = dict(locals())
    w = {n: given[n] for n in WEIGHTS}
    m = {n: given["m_" + n] for n in WEIGHTS}
    v = {n: given["v_" + n] for n in WEIGHTS}
    xi, yi, ci = _mesh_pos()
    q = 2 * xi + yi

    def slot_buffer(val, dtype, slots=N_CHIPS, slot=q, rows=None):
        rows = val.shape[0] if rows is None else rows
        return lax.dynamic_update_slice(lax.empty((slots, rows) + val.shape[1:], dtype), val.astype(dtype)[None], (slot, 0, 0))

    shard_cols = D_IN // N_CHIPS
    conv_rows = CONV_W * GDN_QKV // N_CHIPS // LANES
    conv_pkt = jnp.pad(w["conv_w"][0].reshape(-1, LANES), ((0, CONV_PKT_ROWS - conv_rows), (0, 0)))
    ln_in_out, (w_in4, conv_all) = _ln_in(x[0], _row(w["ln_in_g"]), _row(w["ln_in_b"]),
                                          _gather_now([slot_buffer(w["w_in"][0].T, bf16, rows=W_IN_ROWS)], [slot_buffer(conv_pkt, f32)]))
    conv_full = jnp.concatenate([conv_all[d, :conv_rows].reshape(CONV_W, GDN_QKV // N_CHIPS) for d in range(N_CHIPS)], axis=1)
    wi = jnp.concatenate([w_in4[d, :shard_cols] for d in range(N_CHIPS)], axis=0)
    w_cat = jnp.concatenate([wi[:OFF_BETA], wi[OFF_FOX:OFF_F], wi[OFF_BETA:OFF_FOX], wi[OFF_F:],
                             jnp.zeros((D_CAT - D_IN, D_MODEL), bf16)], axis=0)

    small = {n: w[n] for n in SMALL_NAMES}
    qc = jnp.stack([q, ci]).astype(jnp.int32)
    tail_state = {}

    def pairs_phase(gc):
        g_in = jnp.concatenate([gc[:OFF_BETA], gc[SEG_SMALL:SEG_SMALL + 8], gc[SEG_FOX:SEG_SMALL], gc[SEG_SMALL + 8:SEG_SMALL + 16]], axis=0)
        g_in4 = jnp.stack([jnp.pad(g_in[d * shard_cols:(d + 1) * shard_cols], ((0, W_IN_ROWS - shard_cols), (0, 0))) for d in range(N_CHIPS)])

        def took(moved):
            own, sent = _add_pair(g_in4, moved[0], qc, "add_pair_w_in")
            tail_state.update(own=own, sent=[sent], landing=_landing([sent]))
        return _exchange_pairs([g_in4]), took

    grad_x, _, g_conv, g_late, small_g = _device_grads(
        x[0], p[0, 0], loss_target[0], small, w_cat, conv_full, [slot_buffer(w[n][0], bf16) for n in LATE], qc, tail=(pairs_phase, None),
        ln_in_out=ln_in_out)
    packets = _gather_packets(slot_buffer(_pack_small(small_g, g_conv), f32, 8, 4 * xi + 2 * yi + ci))
    (b2,), (small_all,) = _comm_only([_exchange_chips(tail_state["sent"], tail_state["landing"]), packets], "exchange_chips_w_in")
    (g_late["w_in"],), = _comm_only([_share_halves([_add_chips(tail_state["own"], b2, qc, "add_chips_w_in")])], "share_w_in")

    grads, delta, new_m, new_v = {}, {}, {}, {}
    for n, g in g_late.items():
        if n == "w_in":
            as_stored = lambda a: jnp.transpose(a, (2, 0, 1))
            outs, _ = _adamw(as_stored(w[n]), g[:shard_cols].reshape(shard_cols, 1, D_MODEL), as_stored(m[n]), as_stored(v[n]), "adamw_" + n)
            grads[n], delta[n], new_m[n], new_v[n] = (jnp.transpose(a, (1, 2, 0)) for a in outs)
        else:
            outs, _ = _adamw(w[n][0], g, m[n][0], v[n][0], "adamw_" + n)
            grads[n], delta[n], new_m[n], new_v[n] = (a.reshape(w[n].shape) for a in outs)
    shapes = {n: w[n].shape for n in SMALL_NAMES}
    g_pkt, d_pkt, m_pkt, v_pkt = _small_sum_adamw(small_all, _pack_small(w), _pack_small(m), _pack_small(v))
    for dst, pkt in ((grads, g_pkt), (delta, d_pkt), (new_m, m_pkt), (new_v, v_pkt)):
        dst.update(_unpack_small(pkt, shapes))
    conv_rows_all = CONV_W * GDN_QKV // LANES
    conv_g_full = g_pkt[SMALL_CONV_ROW:SMALL_CONV_ROW + conv_rows_all].reshape(CONV_W, GDN_QKV)
    conv_g = lax.dynamic_slice_in_dim(conv_g_full, q * (GDN_QKV // N_CHIPS), GDN_QKV // N_CHIPS, axis=1)
    outs, _ = _adamw(w["conv_w"][0], conv_g, m["conv_w"][0], v["conv_w"][0], "adamw_conv_w")
    grads["conv_w"], delta["conv_w"], new_m["conv_w"], new_v["conv_w"] = (a.reshape(w["conv_w"].shape) for a in outs)
    loss = g_pkt[61, 0]
    return (loss, grad_x[None], *[grads[n] for n in WEIGHTS], *[delta[n] for n in WEIGHTS],
            *[new_m[n] for n in WEIGHTS], *[new_v[n] for n in WEIGHTS])
```

```python
import functools

import jax
import jax.numpy as jnp
from jax import lax
from jax.experimental import pallas as pl
from jax.experimental.pallas import tpu as pltpu

f32 = jnp.float32
bf16 = jnp.bfloat16
HI = lax.Precision.HIGHEST
MESH = pl.DeviceIdType.MESH

D_MODEL = 1024
CHUNK = 64
GDN_HEADS = 4
GDN_DK = 128
FOX_HEADS = 8
FOX_DH = 64
CONV_W = 4
D_FF = 4096
D_PLE = 256
LN_EPS = 1e-5
NORM_EPS = 1e-6
ALPHA = 2.0 ** 0.25
GDN_QKV = 1536
OFF_Z = 1536
OFF_BETA = 2048
OFF_FOX = 2056
OFF_F = 3592
D_IN = 3600
ADAM_LR = 0.001
ADAM_B1 = 0.9
ADAM_B2 = 0.999
ADAM_EPS = 1e-08
ADAM_WD = 0.01
ADAM_STEP = 10

SEG_FOX = 2048
SEG_SMALL = 3584
D_CAT = 3840
LANES = 128
TOK_BLK = 256
FOX_BQ = 256
VMEM_LIMIT = 56 * 1024 * 1024
NEG = -1e30

N_CHIPS = 4
W_IN_ROWS = 928


def _params(sem=None, **kw):
    return pltpu.CompilerParams(dimension_semantics=sem, vmem_limit_bytes=VMEM_LIMIT, **kw)


def _sigmoid(x):
    return 1.0 / (1.0 + jnp.exp(-x))


def _softplus(x):
    return jnp.maximum(x, 0.0) + jnp.log(1.0 + jnp.exp(-jnp.abs(x)))


def _ln_fwd(x, g, b):
    mu = jnp.mean(x, -1, keepdims=True)
    xc = x - mu
    var = jnp.mean(xc * xc, -1, keepdims=True)
    rstd = lax.rsqrt(var + LN_EPS)
    xhat = xc * rstd
    return xhat * g + b, xhat, rstd


def _ln_bwd(dy, xhat, rstd, g):
    dxh = dy * g
    m1 = jnp.mean(dxh, -1, keepdims=True)
    m2 = jnp.mean(dxh * xhat, -1, keepdims=True)
    return rstd * (dxh - m1 - xhat * m2)


def _dot(a, b, prec=HI):
    return jnp.dot(a, b, precision=prec, preferred_element_type=f32)


def _dot_nt(a, b, prec=HI):
    return lax.dot_general(a, b, (((1,), (1,)), ((), ())), precision=prec, preferred_element_type=f32)


def _dot_tn(a, b, prec=HI):
    return lax.dot_general(a, b, (((0,), (0,)), ((), ())), precision=prec, preferred_element_type=f32)


def _bdot(a, b):
    return _dot(a.astype(bf16), b.astype(bf16), None)


def _bdot_nt(a, b):
    return _dot_nt(a.astype(bf16), b.astype(bf16), None)


def _bdot_tn(a, b):
    return _dot_tn(a.astype(bf16), b.astype(bf16), None)


def _lane(shape):
    return lax.broadcasted_iota(jnp.int32, shape, len(shape) - 1)


def _mm(a, b, mode, tm, tn, name, out_dtype=f32, epi=None, extra=None, shards=1, comm=None):
    if mode == "nn":
        (m, k), n = a.shape, b.shape[-1] * shards
    elif mode == "nt":
        (m, k), n = a.shape, b.shape[-2]
    else:
        (k, m), n = a.shape, b.shape[1]
    assert m % tm == 0 and n % tn == 0, (name, m, n, tm, tn)
    per = (n // shards) // tn
    assert mode == "nt" or per * tn * shards == n, (name, n, tn, shards)
    nc = 512 if tn % 512 == 0 else (256 if tn % 256 == 0 else 128)
    ks = k // shards

    def body(a_ref, b_ref, *rest):
        for n0 in range(0, tn, nc):
            if mode == "nn":
                acc = jnp.dot(a_ref[...], b_ref[:, n0:n0 + nc], preferred_element_type=f32)
            elif mode == "nt" and shards > 1:
                acc = jnp.zeros((tm, nc), f32)
                for d in range(shards):
                    acc = acc + lax.dot_general(a_ref[:, d * ks:(d + 1) * ks], b_ref[d, n0:n0 + nc, :], (((1,), (1,)), ((), ())),
                                                preferred_element_type=f32)
            elif mode == "nt":
                acc = lax.dot_general(a_ref[...], b_ref[n0:n0 + nc, :], (((1,), (1,)), ((), ())), preferred_element_type=f32)
            else:
                acc = lax.dot_general(a_ref[...], b_ref[:, n0:n0 + nc], (((0,), (0,)), ((), ())), preferred_element_type=f32)
            if epi == "relu2":
                relu_ref, act_ref = rest
                r = jnp.maximum(acc, 0.0)
                relu_ref[:, n0:n0 + nc] = r.astype(bf16)
                act_ref[:, n0:n0 + nc] = (r * r).astype(bf16)
            elif epi == "relu2_bwd":
                relu_ref, o_ref = rest
                o_ref[:, n0:n0 + nc] = (acc * (2.0 * relu_ref[:, n0:n0 + nc].astype(f32))).astype(bf16)
            else:
                (o_ref,) = rest
                o_ref[:, n0:n0 + nc] = acc.astype(out_dtype)

    if mode == "tn":
        a_spec = pl.BlockSpec((k, tm), lambda j, i: (0, i))
    else:
        a_spec = pl.BlockSpec((tm, k), lambda j, i: (i, 0))
    if mode == "nt" and shards > 1:
        b_spec = pl.BlockSpec((shards, tn, ks), lambda j, i: (0, j, 0))
    elif mode == "nt":
        b_spec = pl.BlockSpec((tn, k), lambda j, i: (j, 0))
    elif mode == "nn" and shards > 1:
        b_spec = pl.BlockSpec((None, k, tn), lambda j, i: (j // per, 0, j % per))
    else:
        b_spec = pl.BlockSpec((k, tn), lambda j, i: (0, j))
    o_spec = pl.BlockSpec((tm, tn), lambda j, i: (i, j))
    in_specs = [a_spec, b_spec]
    args = [a, b]
    if epi == "relu2":
        out_shape = (jax.ShapeDtypeStruct((m, n), bf16), jax.ShapeDtypeStruct((m, n), bf16))
        out_specs = (o_spec, o_spec)
    elif epi == "relu2_bwd":
        in_specs.append(o_spec)
        args.append(extra)
        out_shape = jax.ShapeDtypeStruct((m, n), bf16)
        out_specs = o_spec
    elif mode == "tn" and shards > 1:
        out_shape = jax.ShapeDtypeStruct((shards, m, n // shards), out_dtype)
        out_specs = pl.BlockSpec((None, tm, tn), lambda j, i: (j // per, i, j % per))
    else:
        out_shape = jax.ShapeDtypeStruct((m, n), out_dtype)
        out_specs = o_spec
    single = not isinstance(out_shape, tuple)
    res, moved = _hosted(body, comm, name=name, grid=(n // tn, m // tm), in_specs=in_specs,
                         out_specs=(out_specs,) if single else out_specs, out_shape=(out_shape,) if single else out_shape, args=args)
    res = res[0] if single else res
    return res if comm is None else (res, moved)


def _row_spec(width, col=0):
    return pl.BlockSpec((TOK_BLK, width), lambda i: (i, col))


def _vec_spec(rows, width):
    return pl.BlockSpec((rows, width), lambda i: (0, 0))


def _ln_in(x, g, b, comm=None):
    t, d = x.shape

    def body(x_ref, g_ref, b_ref, h_ref, hb_ref):
        h, _, _ = _ln_fwd(x_ref[...], g_ref[...], b_ref[...])
        h_ref[...] = h
        hb_ref[...] = h.astype(bf16)

    return _hosted(
        body, comm, name="ln_in", grid=(t // TOK_BLK,),
        in_specs=[_row_spec(d), _vec_spec(1, d), _vec_spec(1, d)],
        out_specs=(_row_spec(d), _row_spec(d)),
        out_shape=(jax.ShapeDtypeStruct((t, d), f32), jax.ShapeDtypeStruct((t, d), bf16)),
        args=(x, g, b))


def _attn_post(o_gdn, proj, o_fox, g_gdn, g_fox2, comm=None):
    t = o_gdn.shape[0]

    def body(og_ref, z_ref, of_ref, gg_ref, gf_ref, out_ref):
        for h in range(GDN_HEADS):
            sl = slice(h * LANES, (h + 1) * LANES)
            og = og_ref[:, sl]
            z = z_ref[:, sl]
            r = lax.rsqrt(jnp.mean(og * og, -1, keepdims=True) + NORM_EPS)
            out_ref[:, sl] = (og * r * gg_ref[...] * (z * _sigmoid(z))).astype(bf16)
        lo = _lane((TOK_BLK, LANES)) < FOX_DH
        for pr in range(FOX_HEADS // 2):
            sl = slice(pr * LANES, (pr + 1) * LANES)
            of = of_ref[:, sl]
            sq = of * of
            s0 = jnp.sum(jnp.where(lo, sq, 0.0), -1, keepdims=True)
            s1 = jnp.sum(jnp.where(lo, 0.0, sq), -1, keepdims=True)
            r = lax.rsqrt(jnp.where(lo, s0, s1) * (1.0 / FOX_DH) + NORM_EPS)
            out_ref[:, 512 + pr * LANES:512 + (pr + 1) * LANES] = (of * r * gf_ref[...]).astype(bf16)

    return _hosted(
        body, comm, name="attn_post", grid=(t // TOK_BLK,),
        in_specs=[_row_spec(512), _row_spec(512, OFF_Z // 512), _row_spec(512), _vec_spec(1, LANES), _vec_spec(1, LANES)],
        out_specs=(_row_spec(D_MODEL),),
        out_shape=(jax.ShapeDtypeStruct((t, D_MODEL), bf16),),
        args=(o_gdn, proj, o_fox, g_gdn, g_fox2))


def _attn_post_bwd(dattn, o_gdn, proj, o_fox, g_gdn, g_fox2):
    t = o_gdn.shape[0]

    def body(da_ref, og_ref, z_ref, of_ref, gg_ref, gf_ref, dog_ref, dz_ref, dof_ref, pg_ref):
        i = pl.program_id(0)

        @pl.when(i == 0)
        def _():
            pg_ref[...] = jnp.zeros_like(pg_ref)

        dgg = jnp.zeros((1, LANES), f32)
        for h in range(GDN_HEADS):
            sl = slice(h * LANES, (h + 1) * LANES)
            og = og_ref[:, sl]
            z = z_ref[:, sl]
            dout = da_ref[:, sl]
            g = gg_ref[...]
            r = lax.rsqrt(jnp.mean(og * og, -1, keepdims=True) + NORM_EPS)
            sg = _sigmoid(z)
            silu = z * sg
            ng = og * r * g
            dng = dout * silu
            dz_ref[:, sl] = (dout * ng * (sg * (1.0 + z * (1.0 - sg)))).astype(bf16)
            dgg = dgg + jnp.sum(dng * og * r, 0, keepdims=True)
            gd = dng * g
            dog_ref[:, sl] = r * gd - og * (r * r * r) * jnp.mean(og * gd, -1, keepdims=True)
        pg_ref[0:1, :] += dgg
        lo = _lane((TOK_BLK, LANES)) < FOX_DH
        dgf = jnp.zeros((1, LANES), f32)
        for pr in range(FOX_HEADS // 2):
            sl = slice(pr * LANES, (pr + 1) * LANES)
            of = of_ref[:, sl]
            dout = da_ref[:, 512 + pr * LANES:512 + (pr + 1) * LANES]
            g = gf_ref[...]
            sq = of * of
            s0 = jnp.sum(jnp.where(lo, sq, 0.0), -1, keepdims=True)
            s1 = jnp.sum(jnp.where(lo, 0.0, sq), -1, keepdims=True)
            r = lax.rsqrt(jnp.where(lo, s0, s1) * (1.0 / FOX_DH) + NORM_EPS)
            dgf = dgf + jnp.sum(dout * of * r, 0, keepdims=True)
            gd = dout * g
            xg = of * gd
            m0 = jnp.sum(jnp.where(lo, xg, 0.0), -1, keepdims=True)
            m1 = jnp.sum(jnp.where(lo, 0.0, xg), -1, keepdims=True)
            dof_ref[:, sl] = r * gd - of * (r * r * r) * (jnp.where(lo, m0, m1) * (1.0 / FOX_DH))
        pg_ref[1:2, :] += dgf

    return pl.pallas_call(
        body, name="attn_post_bwd", grid=(t // TOK_BLK,),
        in_specs=[_row_spec(D_MODEL), _row_spec(512), _row_spec(512, OFF_Z // 512), _row_spec(512), _vec_spec(1, LANES), _vec_spec(1, LANES)],
        out_specs=(_row_spec(512), _row_spec(512), _row_spec(512), _vec_spec(8, LANES)),
        out_shape=(jax.ShapeDtypeStruct((t, 512), f32), jax.ShapeDtypeStruct((t, 512), bf16),
                   jax.ShapeDtypeStruct((t, 512), f32), jax.ShapeDtypeStruct((8, LANES), f32)),
        compiler_params=_params(("arbitrary",)),
    )(dattn, o_gdn, proj, o_fox, g_gdn, g_fox2)


def _ln1(h0, mix, g, b, comm=None):
    t, d = h0.shape

    def body(h0_ref, mix_ref, g_ref, b_ref, h_ref, hb_ref, xh_ref, rs_ref):
        h, xhat, rstd = _ln_fwd(ALPHA * h0_ref[...] + mix_ref[...], g_ref[...], b_ref[...])
        h_ref[...] = h
        hb_ref[...] = h.astype(bf16)
        xh_ref[...] = xhat
        rs_ref[...] = jnp.broadcast_to(rstd, rs_ref.shape)

    return _hosted(
        body, comm, name="ln1", grid=(t // TOK_BLK,),
        in_specs=[_row_spec(d), _row_spec(d), _vec_spec(1, d), _vec_spec(1, d)],
        out_specs=(_row_spec(d), _row_spec(d), _row_spec(d), _row_spec(LANES)),
        out_shape=(jax.ShapeDtypeStruct((t, d), f32), jax.ShapeDtypeStruct((t, d), bf16),
                   jax.ShapeDtypeStruct((t, d), f32), jax.ShapeDtypeStruct((t, LANES), f32)),
        args=(h0, mix, g, b))


def _ln2_loss(h1, ff, pe, gp, b_gate, g, b, target):
    t, d = h1.shape

    def body(h1_ref, ff_ref, pe_ref, gp_ref, bg_ref, g_ref, b_ref, t_ref, dr_ref, drb_ref, dpe_ref, dgp_ref, pg_ref):
        i = pl.program_id(0)

        @pl.when(i == 0)
        def _():
            pg_ref[...] = jnp.zeros_like(pg_ref)

        sig = _sigmoid(gp_ref[...] + bg_ref[...])
        pe = pe_ref[...]
        r2 = ALPHA * h1_ref[...] + ff_ref[...] + pe * sig
        y, xhat, rstd = _ln_fwd(r2, g_ref[...], b_ref[...])
        err = y - t_ref[...]
        dy = err * (1.0 / d)
        dr = _ln_bwd(dy, xhat, rstd, g_ref[...])
        dr_ref[...] = dr
        drb_ref[...] = dr.astype(bf16)
        dpe_ref[...] = (dr * sig).astype(bf16)
        dgp = dr * pe * sig * (1.0 - sig)
        dgp_ref[...] = dgp.astype(bf16)
        pg_ref[0:1, :] += jnp.sum(dy * xhat, 0, keepdims=True)
        pg_ref[1:2, :] += jnp.sum(dy, 0, keepdims=True)
        pg_ref[2:3, :] += jnp.sum(dgp, 0, keepdims=True)
        pg_ref[3:4, :] += 0.5 * jnp.sum(jnp.mean(err * err, -1, keepdims=True), 0, keepdims=True)

    return pl.pallas_call(
        body, name="ln2_loss", grid=(t // TOK_BLK,),
        in_specs=[_row_spec(d)] * 4 + [_vec_spec(1, d)] * 3 + [_row_spec(d)],
        out_specs=(_row_spec(d), _row_spec(d), _row_spec(d), _row_spec(d), _vec_spec(8, d)),
        out_shape=(jax.ShapeDtypeStruct((t, d), f32), jax.ShapeDtypeStruct((t, d), bf16), jax.ShapeDtypeStruct((t, d), bf16),
                   jax.ShapeDtypeStruct((t, d), bf16), jax.ShapeDtypeStruct((8, d), f32)),
        compiler_params=_params(("arbitrary",)),
    )(h1, ff, pe, gp, b_gate, g, b, target)


def _ln1_bwd(dr2, da, db, xhat, rstd, g):
    t, d = dr2.shape

    def body(dr2_ref, da_ref, db_ref, xh_ref, rs_ref, g_ref, dr_ref, drb_ref, pg_ref):
        i = pl.program_id(0)

        @pl.when(i == 0)
        def _():
            pg_ref[...] = jnp.zeros_like(pg_ref)

        dh = ALPHA * dr2_ref[...] + da_ref[...] + db_ref[...]
        xhat = xh_ref[...]
        dr = _ln_bwd(dh, xhat, rs_ref[:, 0:1], g_ref[...])
        dr_ref[...] = dr
        drb_ref[...] = dr.astype(bf16)
        pg_ref[0:1, :] += jnp.sum(dh * xhat, 0, keepdims=True)
        pg_ref[1:2, :] += jnp.sum(dh, 0, keepdims=True)

    return pl.pallas_call(
        body, name="ln1_bwd", grid=(t // TOK_BLK,),
        in_specs=[_row_spec(d)] * 4 + [_row_spec(LANES), _vec_spec(1, d)],
        out_specs=(_row_spec(d), _row_spec(d), _vec_spec(8, d)),
        out_shape=(jax.ShapeDtypeStruct((t, d), f32), jax.ShapeDtypeStruct((t, d), bf16), jax.ShapeDtypeStruct((8, d), f32)),
        compiler_params=_params(("arbitrary",)),
    )(dr2, da, db, xhat, rstd, g)


def _ln_in_bwd(x, dr1, dmm, g, comm=None):
    t, d = x.shape

    def body(x_ref, dr1_ref, dmm_ref, g_ref, dx_ref, pg_ref):
        i = pl.program_id(0)

        @pl.when(i == 0)
        def _():
            pg_ref[...] = jnp.zeros_like(pg_ref)

        dh = ALPHA * dr1_ref[...] + dmm_ref[...]
        _, xhat, rstd = _ln_fwd(x_ref[...], g_ref[...], 0.0)
        dx_ref[...] = _ln_bwd(dh, xhat, rstd, g_ref[...])
        pg_ref[0:1, :] += jnp.sum(dh * xhat, 0, keepdims=True)
        pg_ref[1:2, :] += jnp.sum(dh, 0, keepdims=True)

    return _hosted(
        body, comm, name="ln_in_bwd", grid=(t // TOK_BLK,),
        in_specs=[_row_spec(d)] * 3 + [_vec_spec(1, d)],
        out_specs=(_row_spec(d), _vec_spec(8, d)),
        out_shape=(jax.ShapeDtypeStruct((t, d), f32), jax.ShapeDtypeStruct((8, d), f32)),
        args=(x, dr1, dmm, g))


def _tri(n, upper=False, strict=False):
    r = lax.broadcasted_iota(jnp.int32, (n, n), 0)
    c = lax.broadcasted_iota(jnp.int32, (n, n), 1)
    if upper:
        m = (c > r) if strict else (c >= r)
    else:
        m = (c < r) if strict else (c <= r)
    return jnp.where(m, 1.0, 0.0).astype(f32)


def _gate_values(x, bias, alog, lane):
    z = x + bias
    return jnp.where(lane < 4, _sigmoid(z), jnp.where(lane < 8, -jnp.exp(alog) * _softplus(z), jnp.where(lane < 16, -_softplus(-z), 0.0)))


def _gates(proj, bias_row, alog_row):
    t = proj.shape[0]
    nch = t // CHUNK

    def body(x_ref, bias_ref, alog_ref, gates_ref, gcum_ref, gcumt_ref):
        lane = _lane((t, LANES))
        gates = _gate_values(x_ref[...], bias_ref[...], alog_ref[...], lane)
        gates_ref[...] = gates
        g3 = gates.reshape(nch, CHUNK, LANES)
        tri = jnp.broadcast_to(_tri(CHUNK)[None], (nch, CHUNK, CHUNK))
        loc = jnp.einsum("bij,bjk->bik", tri, g3, precision=HI, preferred_element_type=f32)
        tot = jnp.sum(g3, axis=1)
        offs = _dot(_tri(nch, strict=True), tot)
        glob = loc + offs[:, None, :]
        lane3 = _lane((nch, CHUNK, LANES))
        gcum = jnp.where(lane3 < 4, g3, jnp.where(lane3 < 8, loc, glob)).reshape(t, LANES)
        gcum_ref[...] = gcum
        gcumt_ref[...] = gcum.T

    return pl.pallas_call(
        body, name="gates", grid=(1,),
        in_specs=[pl.BlockSpec((t, LANES), lambda i: (0, SEG_SMALL // LANES)), _vec_spec(1, LANES), _vec_spec(1, LANES)],
        out_specs=(pl.BlockSpec((t, LANES), lambda i: (0, 0)), pl.BlockSpec((t, LANES), lambda i: (0, 0)),
                   pl.BlockSpec((LANES, t), lambda i: (0, 0))),
        out_shape=(jax.ShapeDtypeStruct((t, LANES), f32), jax.ShapeDtypeStruct((t, LANES), f32), jax.ShapeDtypeStruct((LANES, t), f32)),
        compiler_params=_params(("arbitrary",)),
    )(proj, bias_row, alog_row)


def _gates_bwd(proj, bias_row, alog_row, gates, dgates, dccol, dct):
    t = proj.shape[0]
    nch = t // CHUNK

    def body(x_ref, bias_ref, alog_ref, gates_ref, dg_ref, dcc_ref, dct_ref, dx_ref, pg_ref):
        lane = _lane((t, LANES))
        d = dg_ref[...] + dcc_ref[...] + dct_ref[...].T
        d3 = d.reshape(nch, CHUNK, LANES)
        tri = jnp.broadcast_to(_tri(CHUNK, upper=True)[None], (nch, CHUNK, CHUNK))
        loc = jnp.einsum("bij,bjk->bik", tri, d3, precision=HI, preferred_element_type=f32)
        tot = jnp.sum(d3, axis=1)
        offs = _dot(_tri(nch, upper=True, strict=True), tot)
        glob = loc + offs[:, None, :]
        lane3 = _lane((nch, CHUNK, LANES))
        dpre = jnp.where(lane3 < 4, d3, jnp.where(lane3 < 8, loc, glob)).reshape(t, LANES)
        z = x_ref[...] + bias_ref[...]
        sg = _sigmoid(z)
        dx = jnp.where(lane < 4, dpre * sg * (1.0 - sg),
                       jnp.where(lane < 8, dpre * (-jnp.exp(alog_ref[...])) * sg, jnp.where(lane < 16, dpre * (1.0 - sg), 0.0)))
        dx_ref[...] = dx.astype(bf16)
        pg_ref[...] = jnp.zeros_like(pg_ref)
        pg_ref[0:1, :] = jnp.sum(dx, 0, keepdims=True)
        pg_ref[1:2, :] = jnp.sum(jnp.where((lane >= 4) & (lane < 8), dpre * gates_ref[...], 0.0), 0, keepdims=True)

    full = pl.BlockSpec((t, LANES), lambda i: (0, 0))
    return pl.pallas_call(
        body, name="gates_bwd", grid=(1,),
        in_specs=[pl.BlockSpec((t, LANES), lambda i: (0, SEG_SMALL // LANES)), _vec_spec(1, LANES), _vec_spec(1, LANES),
                  full, full, full, pl.BlockSpec((LANES, t), lambda i: (0, 0))],
        out_specs=(full, _vec_spec(8, LANES)),
        out_shape=(jax.ShapeDtypeStruct((t, LANES), bf16), jax.ShapeDtypeStruct((8, LANES), f32)),
        compiler_params=_params(("arbitrary",)),
    )(proj, bias_row, alog_row, gates, dgates, dccol, dct)


def _conv_act(u, cw, row, t):
    c = cw[3:4, :] * u
    for jj in range(CONV_W - 1):
        sh = CONV_W - 1 - jj
        c = c + cw[jj:jj + 1, :] * jnp.where(row >= sh, pltpu.roll(u, sh, axis=0), 0.0)
    return c


def _gdn_conv(proj, conv_w, comm=None):
    t = proj.shape[0]
    nblk = GDN_QKV // LANES

    def body(u_ref, cw_ref, c_ref, y_ref):
        j = pl.program_id(0)
        row = lax.broadcasted_iota(jnp.int32, (t, LANES), 0)
        c = _conv_act(u_ref[...], cw_ref[...], row, t)
        c_ref[...] = c
        s = c * _sigmoid(c)
        r = lax.rsqrt(jnp.sum(s * s, -1, keepdims=True) + NORM_EPS)
        scale = jnp.where(j < GDN_HEADS, GDN_DK ** -0.5, 1.0)
        y_ref[...] = jnp.where(j < 2 * GDN_HEADS, s * (r * scale), s)

    blk = pl.BlockSpec((t, LANES), lambda j: (0, j))
    return _hosted(
        body, comm, name="gdn_conv", grid=(nblk,),
        in_specs=[blk, pl.BlockSpec((CONV_W, LANES), lambda j: (0, j))],
        out_specs=(blk, blk),
        out_shape=(jax.ShapeDtypeStruct((t, GDN_QKV), f32), jax.ShapeDtypeStruct((t, GDN_QKV), f32)),
        args=(proj, conv_w))


def _gdn_conv_bwd(proj, conv_w, c, dy, comm=None):
    t = proj.shape[0]
    nblk = GDN_QKV // LANES

    def body(u_ref, cw_ref, c_ref, dy_ref, du_ref, dcw_ref):
        j = pl.program_id(0)
        row = lax.broadcasted_iota(jnp.int32, (t, LANES), 0)
        u = u_ref[...]
        cw = cw_ref[...]
        c = c_ref[...]
        dy = dy_ref[...]
        sg = _sigmoid(c)
        s = c * sg
        r = lax.rsqrt(jnp.sum(s * s, -1, keepdims=True) + NORM_EPS)
        n = s * r
        scale = jnp.where(j < GDN_HEADS, GDN_DK ** -0.5, 1.0)
        dn = dy * scale
        ds = jnp.where(j < 2 * GDN_HEADS, r * (dn - n * jnp.sum(dn * n, -1, keepdims=True)), dy)
        dc = ds * (sg * (1.0 + c * (1.0 - sg)))
        du = cw[3:4, :] * dc
        dcw_ref[...] = jnp.zeros_like(dcw_ref)
        dcw_ref[3:4, :] = jnp.sum(dc * u, 0, keepdims=True)
        for jj in range(CONV_W - 1):
            sh = CONV_W - 1 - jj
            du = du + cw[jj:jj + 1, :] * jnp.where(row < t - sh, pltpu.roll(dc, t - sh, axis=0), 0.0)
            dcw_ref[jj:jj + 1, :] = jnp.sum(dc * jnp.where(row >= sh, pltpu.roll(u, sh, axis=0), 0.0), 0, keepdims=True)
        du_ref[...] = du.astype(bf16)

    blk = pl.BlockSpec((t, LANES), lambda j: (0, j))
    return _hosted(
        body, comm, name="gdn_conv_bwd", grid=(nblk,),
        in_specs=[blk, pl.BlockSpec((CONV_W, LANES), lambda j: (0, j)), blk, blk],
        out_specs=(blk, pl.BlockSpec((8, LANES), lambda j: (0, j))),
        out_shape=(jax.ShapeDtypeStruct((t, GDN_QKV), bf16), jax.ShapeDtypeStruct((8, GDN_QKV), f32)),
        args=(proj, conv_w, c, dy))


def _chunk_masks():
    r = lax.broadcasted_iota(jnp.int32, (CHUNK, CHUNK), 0)
    c = lax.broadcasted_iota(jnp.int32, (CHUNK, CHUNK), 1)
    return r >= c, r > c, r == c


def _col_to_row(col, eye):
    return jnp.sum(jnp.where(eye, col, 0.0), axis=0, keepdims=True)


def _row_to_col(row, eye):
    return jnp.sum(jnp.where(eye, row, 0.0), axis=1, keepdims=True)


NN = (((1,), (0,)), ((), ()))
NT = (((1,), (1,)), ((), ()))
TN = (((0,), (0,)), ((), ()))
GDN_GROUP = 4


def _mx(a, b, dims=NN, passes=1):
    d = lambda p, q: lax.dot_general(p, q, dims, preferred_element_type=f32)
    ah, bh = a.astype(bf16), b.astype(bf16)
    if passes == 1:
        return d(ah, bh)
    al = (a - ah.astype(f32)).astype(bf16)
    bl = (b - bh.astype(f32)).astype(bf16)
    return d(ah, bh) + (d(ah, bl) + d(al, bh))


def _gdn_decay(gam, masks):
    causal, _, eye = masks
    return jnp.exp(jnp.where(causal, gam - _col_to_row(gam, eye), NEG))


def _gdn_local(y, gcum, comm=None):
    t = y.shape[0]
    nch = t // CHUNK
    rows_blk = GDN_GROUP * CHUNK

    def body(y_ref, g_ref, u_ref, w_ref, qk_ref, tinv_ref):
        masks = _chunk_masks()
        _, strict, eye = masks
        ids = [(j, h) for j in range(GDN_GROUP) for h in range(GDN_HEADS)]
        rs = lambda j: slice(j * CHUNK, (j + 1) * CHUNK)
        col = lambda base, h: slice(base + h * LANES, base + (h + 1) * LANES)
        kn = [y_ref[rs(j), col(512, h)] for j, h in ids]
        beta = [g_ref[rs(j), h:h + 1] for j, h in ids]
        gam = [g_ref[rs(j), 4 + h:5 + h] for j, h in ids]
        dec = [_gdn_decay(g, masks) for g in gam]
        x = [-jnp.where(strict, _mx(k, k, NT) * d * b, 0.0) for k, d, b in zip(kn, dec, beta)]
        tinv = [jnp.where(eye, 1.0, 0.0) + a for a in x]
        for _ in range(5):
            x = [_mx(a, a, NN, 3) for a in x]
            tinv = [t_ + _mx(t_, a, NN, 3) for t_, a in zip(tinv, x)]
        for (j, h), t_, k, d, b, g in zip(ids, tinv, kn, dec, beta, gam):
            u_ref[rs(j), col(0, h)] = _mx(t_, b * y_ref[rs(j), col(1024, h)])
            w_ref[rs(j), col(0, h)] = _mx(t_, (b * jnp.exp(g)) * k)
            qk_ref[j, h] = _mx(y_ref[rs(j), col(0, h)], k, NT) * d
            tinv_ref[j, h] = t_

    mat = pl.BlockSpec((GDN_GROUP, GDN_HEADS, CHUNK, CHUNK), lambda n: (n, 0, 0, 0))
    return _hosted(
        body, comm, name="gdn_local", grid=(nch // GDN_GROUP,),
        in_specs=[pl.BlockSpec((rows_blk, GDN_QKV), lambda n: (n, 0)), pl.BlockSpec((rows_blk, LANES), lambda n: (n, 0))],
        out_specs=(pl.BlockSpec((rows_blk, 512), lambda n: (n, 0)), pl.BlockSpec((rows_blk, 512), lambda n: (n, 0)), mat, mat),
        out_shape=(jax.ShapeDtypeStruct((t, 512), f32), jax.ShapeDtypeStruct((t, 512), f32),
                   jax.ShapeDtypeStruct((nch, GDN_HEADS, CHUNK, CHUNK), f32), jax.ShapeDtypeStruct((nch, GDN_HEADS, CHUNK, CHUNK), f32)),
        args=(y, gcum))


def _gdn_fwd(y, gcum, u, w, qk, comm=None):
    t = y.shape[0]
    nch = t // CHUNK

    def body(y_ref, g_ref, u_ref, w_ref, qk_ref, o_ref, sall_ref, s_ref):
        @pl.when(pl.program_id(0) == 0)
        def _():
            s_ref[...] = jnp.zeros_like(s_ref)

        heads = range(GDN_HEADS)
        sl = [slice(h * LANES, (h + 1) * LANES) for h in heads]
        gam = [g_ref[:, 4 + h:5 + h] for h in heads]
        gam_last = [g[CHUNK - 1:CHUNK, :] for g in gam]
        s = [s_ref[h] for h in heads]
        for h in heads:
            sall_ref[0, h] = s[h]
        ws = [_mx(w_ref[:, sl[h]], s[h]) for h in heads]
        qs = [_mx(y_ref[:, sl[h]] * jnp.exp(gam[h]), s[h]) for h in heads]
        vn = [u_ref[:, sl[h]] - ws[h] for h in heads]
        av = [_mx(qk_ref[0, h], vn[h]) for h in heads]
        kv = [_mx(y_ref[:, 512 + h * LANES:512 + (h + 1) * LANES] * jnp.exp(gam_last[h] - gam[h]), vn[h], TN) for h in heads]
        for h in heads:
            o_ref[:, sl[h]] = qs[h] + av[h]
            s_ref[h] = jnp.exp(gam_last[h]) * s[h] + kv[h]

    row = lambda width: pl.BlockSpec((CHUNK, width), lambda n: (n, 0))
    return _hosted(
        body, comm, name="gdn_fwd", grid=(nch,),
        in_specs=[row(GDN_QKV), row(LANES), row(512), row(512), pl.BlockSpec((1, GDN_HEADS, CHUNK, CHUNK), lambda n: (n, 0, 0, 0))],
        out_specs=(row(512), pl.BlockSpec((1, GDN_HEADS, LANES, LANES), lambda n: (n, 0, 0, 0))),
        out_shape=(jax.ShapeDtypeStruct((t, 512), f32), jax.ShapeDtypeStruct((nch, GDN_HEADS, LANES, LANES), f32)),
        scratch_shapes=[pltpu.VMEM((GDN_HEADS, LANES, LANES), f32)],
        args=(y, gcum, u, w, qk))


def _gdn_bwd(y, gcum, u_all, w_all, qk_all, tinv_all, sall, do, comm=None):
    t = y.shape[0]
    nch = t // CHUNK

    def body(y_ref, g_ref, u_ref, w_ref, qk_ref, tinv_ref, sall_ref, do_ref, dy_ref, dg_ref, ds_ref):
        @pl.when(pl.program_id(0) == 0)
        def _():
            ds_ref[...] = jnp.zeros_like(ds_ref)

        masks = _chunk_masks()
        causal, strict, eye = masks
        lane = _lane((CHUNK, LANES))
        row = lax.broadcasted_iota(jnp.int32, (CHUNK, 1), 0)
        heads = range(GDN_HEADS)
        each = lambda f, *ls: [f(*a) for a in zip(*ls)]
        rsum = lambda a: jnp.sum(a, axis=1, keepdims=True)
        sl = [slice(h * LANES, (h + 1) * LANES) for h in heads]
        qn = [y_ref[:, sl[h]] for h in heads]
        kn = [y_ref[:, 512 + h * LANES:512 + (h + 1) * LANES] for h in heads]
        v = [y_ref[:, 1024 + h * LANES:1024 + (h + 1) * LANES] for h in heads]
        beta = [g_ref[:, h:h + 1] for h in heads]
        gam = [g_ref[:, 4 + h:5 + h] for h in heads]
        gam_last = [g[CHUNK - 1:CHUNK, :] for g in gam]
        dec = [_gdn_decay(g, masks) for g in gam]
        e = [jnp.exp(g) for g in gam]
        f = each(lambda gl_, g: jnp.exp(gl_ - g), gam_last, gam)
        gl = [jnp.exp(g) for g in gam_last]
        u = [u_ref[:, sl[h]] for h in heads]
        w = [w_ref[:, sl[h]] for h in heads]
        qk = [qk_ref[0, h] for h in heads]
        tinv = [tinv_ref[0, h] for h in heads]
        s = [sall_ref[0, h] for h in heads]
        dsn = [ds_ref[h] for h in heads]
        d_o = [do_ref[:, sl[h]] for h in heads]
        qd = each(lambda a, b: a * b, qn, e)
        kd = each(lambda a, b: a * b, kn, f)
        ws = each(_mx, w, s)
        kds = each(_mx, kd, dsn)
        qkdo = each(lambda a, b: _mx(a, b, TN), qk, d_o)
        dqd = each(lambda a, b: _mx(a, b, NT), d_o, s)
        qddo = each(lambda a, b: _mx(a, b, TN), qd, d_o)
        kkd = each(lambda k, d: _mx(k, k, NT) * d, kn, dec)
        vn = each(lambda a, b: a - b, u, ws)
        dvn = each(lambda a, b: a + b, qkdo, kds)
        dqk = each(lambda a, b: jnp.where(causal, _mx(a, b, NT), 0.0), d_o, vn)
        dkd = each(lambda a, b: _mx(a, b, NT), vn, dsn)
        dw = each(lambda a, b: -_mx(a, b, NT), dvn, s)
        wdvn = each(lambda a, b: _mx(a, b, TN), w, dvn)
        dgl = each(lambda a, b: jnp.sum(rsum(a * b), axis=0, keepdims=True), dsn, s)
        for h in heads:
            ds_ref[h] = qddo[h] - wdvn[h] + gl[h] * dsn[h]
        dru = each(lambda a, b: _mx(a, b, TN), tinv, dvn)
        drw = each(lambda a, b: _mx(a, b, TN), tinv, dw)
        dqkr = each(lambda a, b: a * b, dqk, dec)
        dq1 = each(_mx, dqkr, kn)
        dk1 = each(lambda a, b: _mx(a, b, TN), dqkr, qn)
        dnu = each(lambda a, b: _mx(a, b, NT), dru, u)
        dnw = each(lambda a, b: _mx(a, b, NT), drw, w)
        dn = each(lambda a, b: jnp.where(strict, -(a + b), 0.0), dnu, dnw)
        dkk = each(lambda a, b, d: a * b * d, dn, beta, dec)
        dk2 = each(_mx, dkk, kn)
        dk3 = each(lambda a, b: _mx(a, b, TN), dkk, kn)
        dgates = jnp.zeros((CHUNK, LANES), f32)
        for h in heads:
            drw_k = rsum(drw[h] * kn[h])
            dbeta = rsum(dru[h] * v[h]) + e[h] * drw_k + rsum(dn[h] * kkd[h])
            m = dn[h] * (kkd[h] * beta[h]) + dqk[h] * qk[h]
            de = beta[h] * drw_k + rsum(dqd[h] * qn[h])
            df = rsum(dkd[h] * kn[h])
            dgam = rsum(m) - _row_to_col(jnp.sum(m, axis=0, keepdims=True), eye) + de * e[h] - df * f[h]
            dgam_last = jnp.sum(df * f[h], axis=0, keepdims=True) + dgl[h] * gl[h]
            dgam = dgam + jnp.where(row == CHUNK - 1, dgam_last, 0.0)
            dy_ref[:, sl[h]] = dq1[h] + dqd[h] * e[h]
            dy_ref[:, 512 + h * LANES:512 + (h + 1) * LANES] = (beta[h] * e[h]) * drw[h] + dk2[h] + dk3[h] + dk1[h] + dkd[h] * f[h]
            dy_ref[:, 1024 + h * LANES:1024 + (h + 1) * LANES] = beta[h] * dru[h]
            dgates = dgates + jnp.where(lane == h, dbeta, 0.0) + jnp.where(lane == 4 + h, dgam, 0.0)
        dg_ref[...] = dgates

    rev = lambda width: pl.BlockSpec((CHUNK, width), lambda n: (nch - 1 - n, 0))
    mat = lambda d: pl.BlockSpec((1, GDN_HEADS, d, d), lambda n: (nch - 1 - n, 0, 0, 0))
    return _hosted(
        body, comm, name="gdn_bwd", grid=(nch,),
        in_specs=[rev(GDN_QKV), rev(LANES), rev(512), rev(512), mat(CHUNK), mat(CHUNK), mat(LANES), rev(512)],
        out_specs=(rev(GDN_QKV), rev(LANES)),
        out_shape=(jax.ShapeDtypeStruct((t, GDN_QKV), f32), jax.ShapeDtypeStruct((t, LANES), f32)),
        scratch_shapes=[pltpu.VMEM((GDN_HEADS, LANES, LANES), f32)],
        args=(y, gcum, u_all, w_all, qk_all, tinv_all, sall, do))


FOX_CLASSES = 4


def _fox_groups(t):
    nq = t // FOX_BQ
    ncls = min(FOX_CLASSES, nq)
    per = nq // ncls
    return [(g * per, per, (g + 1) * per * FOX_BQ) for g in range(ncls)]


def _fox_scores(q_ref, k_ref, gcum_ref, gcumt_ref, h, i, keys):
    pr = h // 2
    lo = (h % 2) * FOX_DH
    lane = _lane((FOX_BQ, LANES))
    mask = (lane >= lo) & (lane < lo + FOX_DH)
    qm = jnp.where(mask, q_ref[:, pr * LANES:(pr + 1) * LANES], 0.0).astype(bf16)
    kp = k_ref[:, pr * LANES:(pr + 1) * LANES].astype(bf16)
    s = _dot_nt(qm, kp, None) * (FOX_DH ** -0.5)
    s = s + gcum_ref[:, 8 + h:9 + h] - gcumt_ref[8 + h:9 + h, :]
    rows = i * FOX_BQ + lax.broadcasted_iota(jnp.int32, (FOX_BQ, keys), 0)
    cols = lax.broadcasted_iota(jnp.int32, (FOX_BQ, keys), 1)
    return jnp.where(cols <= rows, s, NEG), mask, qm, kp


def _fox_fwd(proj, gcum, gcumt, ride=None):
    c0 = SEG_FOX // 512

    def group_call(q0, nq, keys, comm):
        def body(q_ref, k_ref, v_ref, gcum_ref, gcumt_ref, o_ref, lse_ref):
            i = q0 + pl.program_id(0)
            lane = _lane((FOX_BQ, LANES))
            lse_all = jnp.zeros((FOX_BQ, LANES), f32)
            for pr in range(FOX_HEADS // 2):
                vp = v_ref[:, pr * LANES:(pr + 1) * LANES].astype(bf16)
                o_pair = jnp.zeros((FOX_BQ, LANES), f32)
                for h in (2 * pr, 2 * pr + 1):
                    s, mask, _, _ = _fox_scores(q_ref, k_ref, gcum_ref, gcumt_ref, h, i, keys)
                    m = jnp.max(s, axis=1, keepdims=True)
                    p = jnp.exp(s - m)
                    l = jnp.sum(p, axis=1, keepdims=True)
                    o_h = _dot((p * (1.0 / l)).astype(bf16), vp, None)
                    o_pair = jnp.where(mask, o_h, o_pair)
                    lse_all = jnp.where(lane == h, m + jnp.log(l), lse_all)
                o_ref[:, pr * LANES:(pr + 1) * LANES] = o_pair
            lse_ref[...] = lse_all

        seen = lambda col: pl.BlockSpec((keys, 512), lambda i: (0, col))
        return _hosted(
            body, comm, name=f"fox_fwd_{keys}", grid=(nq,),
            in_specs=[pl.BlockSpec((FOX_BQ, 512), lambda i: (q0 + i, c0)), seen(c0 + 1), seen(c0 + 2),
                      pl.BlockSpec((FOX_BQ, LANES), lambda i: (q0 + i, 0)), pl.BlockSpec((LANES, keys), lambda i: (0, 0))],
            out_specs=(pl.BlockSpec((FOX_BQ, 512), lambda i: (i, 0)), pl.BlockSpec((FOX_BQ, LANES), lambda i: (i, 0))),
            out_shape=(jax.ShapeDtypeStruct((nq * FOX_BQ, 512), f32), jax.ShapeDtypeStruct((nq * FOX_BQ, LANES), f32)),
            args=(proj, proj, proj, gcum, gcumt))

    parts = []
    for n, g in enumerate(_fox_groups(proj.shape[0])):
        hook = ride(n) if ride else None
        part, moved = group_call(*g, hook[0] if hook else None)
        parts.append(part)
        if hook:
            hook[1](moved)
    return jnp.concatenate([o for o, _ in parts], axis=0), jnp.concatenate([l for _, l in parts], axis=0)


def _fox_bwd(proj, gcum, gcumt, o, lse, do, ride=None):
    t = proj.shape[0]
    c0 = SEG_FOX // 512

    def group_call(q0, nq, keys, acc, comm):
        first = acc is None

        def body(q_ref, k_ref, v_ref, gcum_ref, gcumt_ref, o_ref, lse_ref, do_ref, *rest):
            dq_ref, dk_ref, dv_ref, dcc_ref, dct_ref = rest[-5:]
            j = pl.program_id(0)
            i = q0 + j

            @pl.when(j == 0)
            def _():
                if first:
                    dk_ref[...] = jnp.zeros_like(dk_ref)
                    dv_ref[...] = jnp.zeros_like(dv_ref)
                    dct_ref[...] = jnp.zeros_like(dct_ref)
                else:
                    dk_ref[...], dv_ref[...], dct_ref[...] = rest[0][...], rest[1][...], rest[2][...]

            lane = _lane((FOX_BQ, LANES))
            dcc = jnp.zeros((FOX_BQ, LANES), f32)
            scale = FOX_DH ** -0.5
            for pr in range(FOX_HEADS // 2):
                sl = slice(pr * LANES, (pr + 1) * LANES)
                vp = v_ref[:, sl].astype(bf16)
                dq_pair = jnp.zeros((FOX_BQ, LANES), f32)
                for h in (2 * pr, 2 * pr + 1):
                    s, mask, qm, kp = _fox_scores(q_ref, k_ref, gcum_ref, gcumt_ref, h, i, keys)
                    p = jnp.exp(s - lse_ref[:, h:h + 1])
                    dom = jnp.where(mask, do_ref[:, sl], 0.0)
                    delta = jnp.sum(dom * o_ref[:, sl], axis=1, keepdims=True)
                    domb = dom.astype(bf16)
                    ds = p * (_dot_nt(domb, vp, None) - delta)
                    dsb = ds.astype(bf16)
                    dv_ref[:, sl] += _dot_tn(p.astype(bf16), domb, None)
                    dk_ref[:, sl] += _dot_tn(dsb, qm, None) * scale
                    dq_pair = jnp.where(mask, _dot(dsb, kp, None) * scale, dq_pair)
                    dcc = jnp.where(lane == 8 + h, jnp.sum(ds, axis=1, keepdims=True), dcc)
                    dct_ref[8 + h:9 + h, :] += -jnp.sum(ds, axis=0, keepdims=True)
                dq_ref[:, sl] = dq_pair.astype(bf16)
            dcc_ref[...] = dcc

        qblk = lambda col: pl.BlockSpec((FOX_BQ, 512), lambda i: (q0 + i, col))
        oblk = pl.BlockSpec((FOX_BQ, 512), lambda i: (i, 0))
        seen = lambda col: pl.BlockSpec((keys, 512), lambda i: (0, col))
        rblk = pl.BlockSpec((FOX_BQ, LANES), lambda i: (q0 + i, 0))
        seen_t = pl.BlockSpec((LANES, keys), lambda i: (0, 0))
        in_specs = [qblk(c0), seen(c0 + 1), seen(c0 + 2), rblk, seen_t, qblk(0), rblk, qblk(0)]
        args = [proj, proj, proj, gcum, gcumt, o, lse, do]
        aliases = {}
        if not first:
            in_specs += [seen(0), seen(0), seen_t]
            args += list(acc)
            aliases = {8: 1, 9: 2, 10: 4}
        return _hosted(
            body, comm, name=f"fox_bwd_{keys}", grid=(nq,), in_specs=in_specs,
            out_specs=(oblk, seen(0), seen(0), pl.BlockSpec((FOX_BQ, LANES), lambda i: (i, 0)), seen_t),
            out_shape=(jax.ShapeDtypeStruct((nq * FOX_BQ, 512), bf16), jax.ShapeDtypeStruct((t, 512), f32), jax.ShapeDtypeStruct((t, 512), f32),
                       jax.ShapeDtypeStruct((nq * FOX_BQ, LANES), f32), jax.ShapeDtypeStruct((LANES, t), f32)),
            aliases=aliases, args=args)

    acc, dqs, dccs = None, [], []
    for n, g in enumerate(reversed(_fox_groups(t))):
        hook = ride(n) if ride else None
        (dq, dk, dv, dcc, dct), moved = group_call(*g, acc, hook[0] if hook else None)
        if hook:
            hook[1](moved)
        acc = (dk, dv, dct)
        dqs.insert(0, dq)
        dccs.insert(0, dcc)
    return jnp.concatenate(dqs, axis=0), acc[0], acc[1], jnp.concatenate(dccs, axis=0), acc[2]


def _row(v, width=None):
    v = v.reshape(1, -1).astype(f32)
    if width is not None and v.shape[1] < width:
        v = jnp.pad(v, ((0, 0), (0, width - v.shape[1])))
    return v


LATE = ("w_out", "w_up", "w_ple_gate", "w_ple", "w_down")


def _device_grads(x, p, target, small, w_cat, conv_w, late, qc=None, tail=None, ln_in_out=None):
    z4 = jnp.zeros((4,), f32)
    bias_row = _row(jnp.concatenate([z4, small["dt_bias"].reshape(-1), small["b_f"].reshape(-1)]), LANES)
    alog_row = _row(jnp.concatenate([z4, small["a_log"].reshape(-1)]), LANES)
    g_gdn = _row(small["gdn_norm_g"])
    g_fox2 = _row(jnp.tile(small["fox_norm_g"].reshape(-1), 2))
    pb = p.astype(bf16)
    late = list(late)
    comm = qc is not None

    h0, h0b = ln_in_out if ln_in_out is not None else _ln_in(x, _row(small["ln_in_g"]), _row(small["ln_in_b"]))[0]
    proj = _mm(h0b, w_cat, "nt", 256, D_CAT, "mm_proj")
    gates, gcum, gcumt = _gates(proj, bias_row, alog_row)
    w_down_pieces = [(4, 0, 1)]

    def gather(phase, pieces):
        if not comm or not pieces:
            return None, lambda moved: None
        touched = sorted({i for i, _, _ in pieces})

        def took(moved):
            for i, buf in zip(touched, moved):
                late[i] = buf
        return phase([late[i] for i in touched], [(touched.index(i), k, n) for i, k, n in pieces]), took

    over, on = _gather_chips, _gather_pass_on
    cm, took = gather(over, [(0, 0, 1), (3, 0, 1)])
    (conv_c, qkv_n), moved = _gdn_conv(proj, conv_w, cm)
    took(moved)
    cm, took = gather(over, [(1, 0, 2)])
    (gu, gw, gqk, gtinv), moved = _gdn_local(qkv_n, gcum, cm)
    took(moved)
    cm, took = gather(over, [(1, 1, 2)])
    (o_gdn, sall), moved = _gdn_fwd(qkv_n, gcum, gu, gw, gqk, cm)
    took(moved)
    fox_plan = [(over, []), (on, [(0, 0, 1), (3, 0, 1), (1, 0, 2), (1, 1, 2)]), (over, [(2, 0, 1)]), (over, [])]
    assert not comm or len(_fox_groups(x.shape[0])) == len(fox_plan)
    o_fox, lse = _fox_fwd(proj, gcum, gcumt, (lambda n: gather(*fox_plan[n])) if comm else None)
    cm, took = gather(on, [(2, 0, 1)])
    (attn,), moved = _attn_post(o_gdn, proj, o_fox, g_gdn, g_fox2, cm)
    took(moved)
    w_out = late[0].reshape(D_MODEL, D_MODEL)
    mix = _mm(attn, w_out, "nn", 512, D_MODEL, "mm_mix")
    (h1, h1b, xhat1, rstd1), _ = _ln1(h0, mix, _row(small["ln1_g"]), _row(small["ln1_b"]))
    w_up, w_ple = late[1], late[3]
    cm, took = gather(over, w_down_pieces)
    up_act = _mm(h1b, w_up, "nn", 512, 1024, "mm_up", epi="relu2", shards=N_CHIPS, comm=cm)
    if cm:
        up_act, moved = up_act
        took(moved)
    up, act = up_act
    w_gate = late[2].reshape(D_MODEL, D_MODEL)
    cm, took = gather(on, w_down_pieces)
    gp = _mm(h1b, w_gate, "nn", 512, D_MODEL, "mm_gate", comm=cm)
    if cm:
        gp, moved = gp
        took(moved)
    pe = _mm(pb, w_ple, "nn", 512, D_MODEL // N_CHIPS, "mm_ple", shards=N_CHIPS)
    w_down = late[4].reshape(D_FF, D_MODEL)
    ff = _mm(act, w_down, "nn", 256, D_MODEL, "mm_down")
    dr2, dr2b, dpe, dgp, pg2 = _ln2_loss(h1, ff, pe, gp, _row(small["b_ple_gate"]), _row(small["ln2_g"]), _row(small["ln2_b"]), target)

    dup = _mm(dr2b, w_down, "nt", 256, 2048, "mm_dact", epi="relu2_bwd", extra=up)
    g_down = _mm(act, dr2b, "tn", 1024, D_MODEL, "mm_gdown")
    dh1_a = _mm(dup, w_up, "nt", 256, D_MODEL, "mm_dh1a", shards=N_CHIPS)
    g_up = _mm(h1b, dup, "tn", 1024, 1024, "mm_gup", shards=N_CHIPS)
    dh1_b = _mm(dgp, w_gate, "nt", 512, D_MODEL, "mm_dh1b")
    g_gate = _mm(h1b, dgp, "tn", 1024, D_MODEL, "mm_ggate")
    g_ple = _mm(pb, dpe, "tn", D_PLE, D_MODEL // N_CHIPS, "mm_gple", shards=N_CHIPS)
    dr1, dr1b, pg1 = _ln1_bwd(dr2, dh1_a, dh1_b, xhat1, rstd1, _row(small["ln1_g"]))
    dattn = _mm(dr1b, w_out, "nt", 512, D_MODEL, "mm_dattn")
    g_out = _mm(attn, dr1b, "tn", 1024, D_MODEL, "mm_gout")
    do_gdn, dz, do_fox, pga = _attn_post_bwd(dattn, o_gdn, proj, o_fox, g_gdn, g_fox2)
    g_late = [g.reshape((N_CHIPS, -1, g.shape[-1])) for g in (g_out, g_up, g_gate, g_ple, g_down)]
    chip_plan = [[(4, 0, 2), (0, 0, 1)], [(4, 1, 2)], [(2, 0, 1), (3, 0, 1)], [(1, 0, 2), (1, 1, 2)]]
    state = {}

    def to_sibling():
        def took(moved):
            sums = [_add_pair(g, b1, qc, "add_pair_" + n) for g, b1, n in zip(g_late, moved, LATE)]
            state.update(own=[a for a, _ in sums], sent=[ab for _, ab in sums], landing=_landing([ab for _, ab in sums]))
        return _exchange_pairs(g_late), took

    def to_chips(pieces):
        if not comm:
            return None, lambda moved: None
        return _exchange_chips(state["sent"], state["landing"], pieces), lambda moved: state.update(landing=list(moved))

    assert not comm or len(_fox_groups(x.shape[0])) == len(chip_plan)
    dfq, dfk, dfv, dccol, dct = _fox_bwd(proj, gcum, gcumt, o_fox, lse, do_fox,
                                         (lambda n: to_sibling() if n == 0 else to_chips(chip_plan[n - 1])) if comm else None)
    cm, took = to_chips(chip_plan[-1])
    (dqkv_n, dgates), moved = _gdn_bwd(qkv_n, gcum, gu, gw, gqk, gtinv, sall, do_gdn, cm)
    took(moved)
    dsmall, pgg = _gates_bwd(proj, bias_row, alog_row, gates, dgates, dccol, dct)
    cm = None
    if comm:
        cm = _share_halves([_add_chips(a, b2, qc, "add_chips_" + n) for a, b2, n in zip(state["own"], state["landing"], LATE)])
    (du, g_conv8), reduced = _gdn_conv_bwd(proj, conv_w, conv_c, dqkv_n, cm)
    if comm:
        g_late = list(reduced)
    t = x.shape[0]
    dproj = jnp.concatenate([du, dz, dfq, dfk.astype(bf16), dfv.astype(bf16), dsmall, jnp.zeros((t, D_CAT - SEG_SMALL - LANES), bf16)], axis=1)
    g_cat = _mm(dproj, h0b, "tn", 1280, D_MODEL, "mm_gcat")
    cm, took = tail[0](g_cat) if tail else (None, None)
    dh0_mm = _mm(dproj, w_cat, "nn", 256, D_MODEL, "mm_dh0", comm=cm)
    if cm:
        dh0_mm, moved = dh0_mm
        took(moved)
    cm, took = tail[1]() if tail and tail[1] else (None, None)
    (grad_x, pg0), moved = _ln_in_bwd(x, dr1, dh0_mm, _row(small["ln_in_g"]), cm)
    if cm:
        took(moved)

    g_fox = pga[1, :FOX_DH] + pga[1, FOX_DH:]
    small_grads = dict(
        ln_in_g=pg0[0], ln_in_b=pg0[1], ln1_g=pg1[0], ln1_b=pg1[1], b_ple_gate=pg2[2], ln2_g=pg2[0], ln2_b=pg2[1],
        gdn_norm_g=pga[0], fox_norm_g=g_fox, a_log=pgg[1, 4:8], dt_bias=pgg[0, 4:8], b_f=pgg[0, 8:16], loss=pg2[3, 0:1])
    return grad_x, g_cat, g_conv8[:CONV_W], dict(zip(LATE, g_late)), small_grads


ANY = pl.BlockSpec(memory_space=pl.ANY)
CONV_PKT_ROWS = 16


def _mesh_pos():
    return lax.axis_index("x"), lax.axis_index("y"), lax.axis_index("c")


def _other_chips(x, y):
    return [(1 - x, y), (x, 1 - y), (1 - x, 1 - y)]


def _rcopy(src, dst, send_sem, recv_sem, dev):
    return pltpu.make_async_remote_copy(src_ref=src, dst_ref=dst, send_sem=send_sem, recv_sem=recv_sem,
                                        device_id=dev, device_id_type=MESH)


class _Comm:
    def __init__(self, ins, outs, aliases, n_sems, start, finish):
        self.ins, self.outs, self.aliases, self.n_sems, self.start, self.finish = list(ins), list(outs), dict(aliases), n_sems, start, finish


def _hosted(body, comm, *, name, grid, in_specs, out_specs, out_shape, args, scratch_shapes=(), aliases=None):
    n_in, n_out, n_sc = len(in_specs), len(out_specs), len(scratch_shapes)
    k, ko = (len(comm.ins), len(comm.outs)) if comm else (0, 0)

    def kernel_body(*refs):
        o0 = n_in + k
        s0 = o0 + n_out + ko
        if comm:
            cins, couts, (ssem, rsem) = refs[n_in:o0], refs[o0 + n_out:s0], refs[s0 + n_sc:]
            step = pl.program_id(0)
            for d in range(1, len(grid)):
                step = step * grid[d] + pl.program_id(d)

            @pl.when(step == 0)
            def _():
                comm.start(cins, couts, ssem, rsem)

        body(*refs[:n_in], *refs[o0:o0 + n_out], *refs[s0:s0 + n_sc])
        if comm:
            last = 1
            for n in grid:
                last *= n

            @pl.when(step == last - 1)
            def _():
                comm.finish(cins, couts, ssem, rsem)

    io_aliases = dict(aliases or {})
    scratch = list(scratch_shapes)
    if comm:
        io_aliases.update({n_in + i: n_out + j for i, j in comm.aliases.items()})
        scratch += [pltpu.SemaphoreType.DMA((comm.n_sems,)), pltpu.SemaphoreType.DMA((comm.n_sems,))]
    res = pl.pallas_call(
        kernel_body, name=name, grid=grid, in_specs=list(in_specs) + [ANY] * k, out_specs=tuple(out_specs) + (ANY,) * ko,
        out_shape=tuple(out_shape) + tuple(comm.outs if comm else ()), scratch_shapes=scratch, input_output_aliases=io_aliases,
        compiler_params=_params(("arbitrary",) * len(grid)),
    )(*args, *(comm.ins if comm else ()))
    return tuple(res[:n_out]), tuple(res[n_out:])


def _comm_only(phases, name):
    n_in = sum(len(p.ins) for p in phases)

    def body(*refs):
        n_out = sum(len(p.outs) for p in phases)
        sems = refs[n_in + n_out:]
        i0, o0 = 0, n_in
        for j, p in enumerate(phases):
            cins, couts = refs[i0:i0 + len(p.ins)], refs[o0:o0 + len(p.outs)]
            p.start(cins, couts, sems[2 * j], sems[2 * j + 1])
            p.finish(cins, couts, sems[2 * j], sems[2 * j + 1])
            i0 += len(p.ins)
            o0 += len(p.outs)

    aliases, i0, o0 = {}, 0, 0
    for p in phases:
        aliases.update({i0 + i: o0 + j for i, j in p.aliases.items()})
        i0 += len(p.ins)
        o0 += len(p.outs)
    outs = [o for p in phases for o in p.outs]
    res = pl.pallas_call(
        body, name=name, out_shape=tuple(outs), in_specs=[ANY] * n_in, out_specs=(ANY,) * len(outs), input_output_aliases=aliases,
        scratch_shapes=[pltpu.SemaphoreType.DMA((p.n_sems,)) for p in phases for _ in range(2)],
    )(*[a for p in phases for a in p.ins])
    split, o0 = [], 0
    for p in phases:
        split.append(tuple(res[o0:o0 + len(p.outs)]))
        o0 += len(p.outs)
    return split


def _like(arrays):
    return [jax.ShapeDtypeStruct(a.shape, a.dtype) for a in arrays]


def _half(ref, slot, hf, piece=(0, 1)):
    k, n = piece
    rows = ref.shape[1] // 2 // n
    return ref.at[slot, pl.ds((hf * n + k) * rows, rows)]


def _whole_halves(arrays):
    return [(i, 0, 1) for i in range(len(arrays))]


def _gather_chips(bufs, pieces=None, whole=False, base=0):
    nw = len(bufs)
    pieces = _whole_halves(bufs) if pieces is None else pieces
    part = (lambda ref, slot, c, piece: ref.at[slot]) if whole else _half

    def copies(couts):
        x, y, c = _mesh_pos()
        q = 2 * x + y
        for j, (i, k, n) in enumerate(pieces):
            for kc, chip in enumerate(_other_chips(x, y)):
                mine, theirs = part(couts[i], q, c, (k, n)), part(couts[i], 2 * chip[0] + chip[1], c, (k, n))
                yield base + j * 3 + kc, mine, theirs, (*chip, c)

    def start(cins, couts, ssem, rsem):
        for s, mine, _, dev in copies(couts):
            _rcopy(mine, mine, ssem.at[s], rsem.at[s], dev).start()

    def finish(cins, couts, ssem, rsem):
        for s, _, theirs, dev in copies(couts):
            _rcopy(theirs, theirs, ssem.at[s], rsem.at[s], dev).wait_recv()
        for s, mine, _, dev in copies(couts):
            _rcopy(mine, mine, ssem.at[s], rsem.at[s], dev).wait_send()

    return _Comm(bufs, _like(bufs), {i: i for i in range(nw)}, 3 * len(pieces), start, finish)


def _gather_pass_on(bufs, pieces=None, base=0):
    nw = len(bufs)
    pieces = _whole_halves(bufs) if pieces is None else pieces

    def copies(couts):
        x, y, c = _mesh_pos()
        for j, (i, k, n) in enumerate(pieces):
            for kc, chip in enumerate(_other_chips(x, y)):
                slot = 2 * chip[0] + chip[1]
                yield base + j * 3 + kc, _half(couts[i], slot, c, (k, n)), _half(couts[i], slot, 1 - c, (k, n)), (x, y, 1 - c)

    def start(cins, couts, ssem, rsem):
        for s, landed, _, sib in copies(couts):
            _rcopy(landed, landed, ssem.at[s], rsem.at[s], sib).start()

    def finish(cins, couts, ssem, rsem):
        for s, _, passed, sib in copies(couts):
            _rcopy(passed, passed, ssem.at[s], rsem.at[s], sib).wait_recv()
        for s, landed, _, sib in copies(couts):
            _rcopy(landed, landed, ssem.at[s], rsem.at[s], sib).wait_send()

    return _Comm(bufs, _like(bufs), {i: i for i in range(nw)}, 3 * len(pieces), start, finish)


def _gather_now(bufs, packets):
    nb = len(bufs)
    over, on, pk = _gather_chips(bufs), _gather_pass_on(bufs, base=3 * nb), _gather_chips(packets, whole=True, base=6 * nb)

    def start(cins, couts, ssem, rsem):
        over.start(cins[:nb], couts[:nb], ssem, rsem)
        pk.start(cins[nb:], couts[nb:], ssem, rsem)

    def finish(cins, couts, ssem, rsem):
        over.finish(cins[:nb], couts[:nb], ssem, rsem)
        on.start(cins[:nb], couts[:nb], ssem, rsem)
        on.finish(cins[:nb], couts[:nb], ssem, rsem)
        pk.finish(cins[nb:], couts[nb:], ssem, rsem)

    every = list(bufs) + list(packets)
    return _Comm(every, _like(every), {i: i for i in range(len(every))}, 6 * nb + 3 * len(packets), start, finish)


def _exchange_pairs(gs):
    nw = len(gs)

    def copies(cins, couts):
        x, y, c = _mesh_pos()
        for i in range(nw):
            for d in range(N_CHIPS):
                yield i * N_CHIPS + d, _half(cins[i], d, 1 - c), couts[i].at[d], (x, y, 1 - c)

    def start(cins, couts, ssem, rsem):
        for s, src, dst, sib in copies(cins, couts):
            _rcopy(src, dst, ssem.at[s], rsem.at[s], sib).start()

    def finish(cins, couts, ssem, rsem):
        for s, src, dst, sib in copies(cins, couts):
            _rcopy(src, dst, ssem.at[s], rsem.at[s], sib).wait_recv()
        for s, src, dst, sib in copies(cins, couts):
            _rcopy(src, dst, ssem.at[s], rsem.at[s], sib).wait_send()

    outs = [jax.ShapeDtypeStruct((N_CHIPS, g.shape[1] // 2, g.shape[2]), g.dtype) for g in gs]
    return _Comm(gs, outs, {}, N_CHIPS * nw, start, finish)


def _gather_packets(small):
    def peers():
        x, y, c = _mesh_pos()
        for r in range(1, 8):
            fx, fy, fc = (r >> 2) & 1, (r >> 1) & 1, r & 1
            yield r - 1, (1 - x if fx else x, 1 - y if fy else y, 1 - c if fc else c)

    def start(cins, couts, ssem, rsem):
        x, y, c = _mesh_pos()
        mine = couts[0].at[4 * x + 2 * y + c]
        for s, peer in peers():
            _rcopy(mine, mine, ssem.at[s], rsem.at[s], peer).start()

    def finish(cins, couts, ssem, rsem):
        x, y, c = _mesh_pos()
        mine = couts[0].at[4 * x + 2 * y + c]
        for s, peer in peers():
            theirs = couts[0].at[4 * peer[0] + 2 * peer[1] + peer[2]]
            _rcopy(theirs, theirs, ssem.at[s], rsem.at[s], peer).wait_recv()
        for s, peer in peers():
            _rcopy(mine, mine, ssem.at[s], rsem.at[s], peer).wait_send()

    return _Comm([small], _like([small]), {0: 0}, 7, start, finish)


def _exchange_chips(a4s, b2s, pieces=None):
    nw = len(a4s)
    pieces = _whole_halves(a4s) if pieces is None else pieces

    def copies(cins, couts):
        x, y, c = _mesh_pos()
        for j, (i, k, n) in enumerate(pieces):
            rows = a4s[i].shape[1] // n
            part = pl.ds(k * rows, rows)
            for kc, chip in enumerate(_other_chips(x, y)):
                yield j * 3 + kc, cins[i].at[2 * chip[0] + chip[1], part], couts[i].at[kc, part], (*chip, c)

    def start(cins, couts, ssem, rsem):
        for s, src, dst, dev in copies(cins, couts):
            _rcopy(src, dst, ssem.at[s], rsem.at[s], dev).start()

    def finish(cins, couts, ssem, rsem):
        for s, src, dst, dev in copies(cins, couts):
            _rcopy(src, dst, ssem.at[s], rsem.at[s], dev).wait_recv()
        for s, src, dst, dev in copies(cins, couts):
            _rcopy(src, dst, ssem.at[s], rsem.at[s], dev).wait_send()

    return _Comm(list(a4s) + list(b2s), _like(b2s), {nw + i: i for i in range(nw)}, 3 * len(pieces), start, finish)


def _landing(a4s):
    return [lax.empty((3,) + a.shape[1:], a.dtype) for a in a4s]


def _share_halves(rs):
    nw = len(rs)

    def halves(couts, i, hf):
        rows = rs[i].shape[0] // 2
        return couts[i].at[pl.ds(hf * rows, rows)]

    def start(cins, couts, ssem, rsem):
        x, y, c = _mesh_pos()
        for i in range(nw):
            _rcopy(halves(couts, i, c), halves(couts, i, c), ssem.at[i], rsem.at[i], (x, y, 1 - c)).start()

    def finish(cins, couts, ssem, rsem):
        x, y, c = _mesh_pos()
        for i in range(nw):
            _rcopy(halves(couts, i, 1 - c), halves(couts, i, 1 - c), ssem.at[i], rsem.at[i], (x, y, 1 - c)).wait_recv()
        for i in range(nw):
            _rcopy(halves(couts, i, c), halves(couts, i, c), ssem.at[i], rsem.at[i], (x, y, 1 - c)).wait_send()

    return _Comm(rs, _like(rs), {i: i for i in range(nw)}, nw, start, finish)


ADD_ROWS = 256


def _add_pair(g4, b1, qc_idx, name):
    _, half, cols = b1.shape
    rb = ADD_ROWS if half % ADD_ROWS == 0 else half
    nb = half // rb

    def body(qc_ref, g_ref, b_ref, o_ref, ob_ref):
        a = g_ref[...] + b_ref[...]
        o_ref[...] = a
        ob_ref[...] = a.astype(bf16)

    blk = (1, rb, cols)
    out = pl.BlockSpec(blk, lambda d, i, qc: (d, i, 0))
    return pl.pallas_call(
        body, name=name,
        grid_spec=pltpu.PrefetchScalarGridSpec(
            num_scalar_prefetch=1, grid=(N_CHIPS, nb),
            in_specs=[pl.BlockSpec(blk, lambda d, i, qc: (d, qc[1] * nb + i, 0)), out],
            out_specs=(out, out)),
        out_shape=(jax.ShapeDtypeStruct(b1.shape, f32), jax.ShapeDtypeStruct(b1.shape, bf16)),
        compiler_params=_params(("parallel", "parallel")),
    )(qc_idx, g4, b1)


def _add_chips(a4, b2, qc_idx, name):
    _, half, cols = a4.shape
    rb = ADD_ROWS if half % ADD_ROWS == 0 else half
    nb = half // rb

    def body(qc_ref, a_ref, b_ref, o_ref):
        o_ref[...] = ((a_ref[0] + b_ref[0].astype(f32)) + b_ref[1].astype(f32)) + b_ref[2].astype(f32)

    return pl.pallas_call(
        body, name=name,
        grid_spec=pltpu.PrefetchScalarGridSpec(
            num_scalar_prefetch=1, grid=(nb,),
            in_specs=[pl.BlockSpec((1, rb, cols), lambda i, qc: (qc[0], i, 0)), pl.BlockSpec((3, rb, cols), lambda i, qc: (0, i, 0))],
            out_specs=pl.BlockSpec((rb, cols), lambda i, qc: (qc[1] * nb + i, 0))),
        out_shape=jax.ShapeDtypeStruct((2 * half, cols), f32),
        compiler_params=_params(("parallel",)),
    )(qc_idx, a4, b2)


def _adamw_math(w, g, m, v):
    m = ADAM_B1 * m + (1.0 - ADAM_B1) * g
    v = ADAM_B2 * v + (1.0 - ADAM_B2) * (g * g)
    m_hat = m / (1.0 - ADAM_B1 ** ADAM_STEP)
    v_hat = v / (1.0 - ADAM_B2 ** ADAM_STEP)
    return -ADAM_LR * (m_hat / (jnp.sqrt(v_hat) + ADAM_EPS) + ADAM_WD * w), m, v


def _adamw(w, g, m, v, name, comm=None):
    rows = w.shape[0]
    if w.ndim == 3:
        rb = max(r for r in range(1, ADD_ROWS // 4 + 1) if rows % r == 0)
    else:
        rb = ADD_ROWS if rows % ADD_ROWS == 0 else rows

    def body(w_ref, g_ref, m_ref, v_ref, go_ref, d_ref, mo_ref, vo_ref):
        g = g_ref[...]
        go_ref[...] = g
        d_ref[...], mo_ref[...], vo_ref[...] = _adamw_math(w_ref[...], g, m_ref[...], v_ref[...])

    blk = pl.BlockSpec((rb,) + w.shape[1:], lambda i: (i,) + (0,) * (w.ndim - 1))
    return _hosted(body, comm, name=name, grid=(rows // rb,), in_specs=[blk] * 4, out_specs=(blk,) * 4,
                   out_shape=(jax.ShapeDtypeStruct(w.shape, f32),) * 4, args=(w, g, m, v))


def _small_sum_adamw(all_pkts, w, m, v):
    def body(a_ref, w_ref, m_ref, v_ref, g_ref, d_ref, mo_ref, vo_ref):
        g = a_ref[0]
        for r in range(1, 8):
            g = g + a_ref[r]
        g_ref[...] = g
        d_ref[...], mo_ref[...], vo_ref[...] = _adamw_math(w_ref[...], g, m_ref[...], v_ref[...])

    return pl.pallas_call(body, name="small_sum_adamw", out_shape=(jax.ShapeDtypeStruct(w.shape, f32),) * 4)(all_pkts, w, m, v)


SMALL_LAYOUT = (("ln_in_g", 0, 1024), ("ln_in_b", 8, 1024), ("ln1_g", 16, 1024), ("ln1_b", 24, 1024), ("b_ple_gate", 32, 1024),
                ("ln2_g", 40, 1024), ("ln2_b", 48, 1024), ("gdn_norm_g", 56, 128), ("fox_norm_g", 57, 64), ("a_log", 58, 4),
                ("dt_bias", 59, 4), ("b_f", 60, 8), ("loss", 61, 1))
SMALL_CONV_ROW = 64
SMALL_ROWS = 128


def _pack_small(vals, conv=None):
    rows = []
    nxt = 0
    for n, r0, size in SMALL_LAYOUT:
        assert r0 == nxt
        v = vals[n].reshape(-1).astype(f32) if n in vals else jnp.zeros((size,), f32)
        nrows = -(-size // LANES)
        rows.append(jnp.pad(v, (0, nrows * LANES - size)).reshape(nrows, LANES))
        nxt = r0 + nrows
    rows.append(jnp.zeros((SMALL_CONV_ROW - nxt, LANES), f32))
    conv_rows = CONV_W * GDN_QKV // LANES
    rows.append(jnp.zeros((conv_rows, LANES), f32) if conv is None else conv.reshape(conv_rows, LANES))
    rows.append(jnp.zeros((SMALL_ROWS - SMALL_CONV_ROW - conv_rows, LANES), f32))
    return jnp.concatenate(rows, axis=0)


def _unpack_small(pkt, shapes):
    out = {}
    for n, r0, size in SMALL_LAYOUT:
        if n in shapes:
            nrows = -(-size // LANES)
            out[n] = pkt[r0:r0 + nrows].reshape(-1)[:size].reshape(shapes[n])
    return out


WEIGHTS = ("ln_in_g", "ln_in_b", "w_in", "conv_w", "a_log", "dt_bias", "gdn_norm_g", "b_f", "fox_norm_g", "w_out", "ln1_g", "ln1_b",
           "w_up", "w_down", "w_ple", "w_ple_gate", "b_ple_gate", "ln2_g", "ln2_b")
SMALL_NAMES = tuple(n for n, _, _ in SMALL_LAYOUT if n != "loss")


def kernel(x, p, ln_in_g, ln_in_b, w_in, conv_w, a_log, dt_bias, gdn_norm_g, b_f, fox_norm_g, w_out, ln1_g, ln1_b, w_up, w_down, w_ple, w_ple_gate, b_ple_gate, ln2_g, ln2_b, loss_target, m_ln_in_g, m_ln_in_b, m_w_in, m_conv_w, m_a_log, m_dt_bias, m_gdn_norm_g, m_b_f, m_fox_norm_g, m_w_out, m_ln1_g, m_ln1_b, m_w_up, m_w_down, m_w_ple, m_w_ple_gate, m_b_ple_gate, m_ln2_g, m_ln2_b, v_ln_in_g, v_ln_in_b, v_w_in, v_conv_w, v_a_log, v_dt_bias, v_gdn_norm_g, v_b_f, v_fox_norm_g, v_w_out, v_ln1_g, v_ln1_b, v_w_up, v_w_down, v_w_ple, v_w_ple_gate, v_b_ple_gate, v_ln2_g, v_ln2_b):
    given = dict(locals())
    w = {n: given[n] for n in WEIGHTS}
    m = {n: given["m_" + n] for n in WEIGHTS}
    v = {n: given["v_" + n] for n in WEIGHTS}
    xi, yi, ci = _mesh_pos()
    q = 2 * xi + yi

    def slot_buffer(val, dtype, slots=N_CHIPS, slot=q, rows=None):
        rows = val.shape[0] if rows is None else rows
        return lax.dynamic_update_slice(lax.empty((slots, rows) + val.shape[1:], dtype), val.astype(dtype)[None], (slot, 0, 0))

    shard_cols = D_IN // N_CHIPS
    conv_rows = CONV_W * GDN_QKV // N_CHIPS // LANES
    conv_pkt = jnp.pad(w["conv_w"][0].reshape(-1, LANES), ((0, CONV_PKT_ROWS - conv_rows), (0, 0)))
    ln_in_out, (w_in4, conv_all) = _ln_in(x[0], _row(w["ln_in_g"]), _row(w["ln_in_b"]),
                                          _gather_now([slot_buffer(w["w_in"][0].T, bf16, rows=W_IN_ROWS)], [slot_buffer(conv_pkt, f32)]))
    conv_full = jnp.concatenate([conv_all[d, :conv_rows].reshape(CONV_W, GDN_QKV // N_CHIPS) for d in range(N_CHIPS)], axis=1)
    wi = jnp.concatenate([w_in4[d, :shard_cols] for d in range(N_CHIPS)], axis=0)
    w_cat = jnp.concatenate([wi[:OFF_BETA], wi[OFF_FOX:OFF_F], wi[OFF_BETA:OFF_FOX], wi[OFF_F:],
                             jnp.zeros((D_CAT - D_IN, D_MODEL), bf16)], axis=0)

    small = {n: w[n] for n in SMALL_NAMES}
    qc = jnp.stack([q, ci]).astype(jnp.int32)
    tail_state = {}

    def pairs_phase(gc):
        g_in = jnp.concatenate([gc[:OFF_BETA], gc[SEG_SMALL:SEG_SMALL + 8], gc[SEG_FOX:SEG_SMALL], gc[SEG_SMALL + 8:SEG_SMALL + 16]], axis=0)
        g_in4 = jnp.stack([jnp.pad(g_in[d * shard_cols:(d + 1) * shard_cols], ((0, W_IN_ROWS - shard_cols), (0, 0))) for d in range(N_CHIPS)])

        def took(moved):
            own, sent = _add_pair(g_in4, moved[0], qc, "add_pair_w_in")
            tail_state.update(own=own, sent=[sent], landing=_landing([sent]))
        return _exchange_pairs([g_in4]), took

    grad_x, _, g_conv, g_late, small_g = _device_grads(
        x[0], p[0, 0], loss_target[0], small, w_cat, conv_full, [slot_buffer(w[n][0], bf16) for n in LATE], qc, tail=(pairs_phase, None),
        ln_in_out=ln_in_out)
    packets = _gather_packets(slot_buffer(_pack_small(small_g, g_conv), f32, 8, 4 * xi + 2 * yi + ci))
    (b2,), (small_all,) = _comm_only([_exchange_chips(tail_state["sent"], tail_state["landing"]), packets], "exchange_chips_w_in")
    (g_late["w_in"],), = _comm_only([_share_halves([_add_chips(tail_state["own"], b2, qc, "add_chips_w_in")])], "share_w_in")

    grads, delta, new_m, new_v = {}, {}, {}, {}
    for n, g in g_late.items():
        if n == "w_in":
            as_stored = lambda a: jnp.transpose(a, (2, 0, 1))
            outs, _ = _adamw(as_stored(w[n]), g[:shard_cols].reshape(shard_cols, 1, D_MODEL), as_stored(m[n]), as_stored(v[n]), "adamw_" + n)
            grads[n], delta[n], new_m[n], new_v[n] = (jnp.transpose(a, (1, 2, 0)) for a in outs)
        else:
            outs, _ = _adamw(w[n][0], g, m[n][0], v[n][0], "adamw_" + n)
            grads[n], delta[n], new_m[n], new_v[n] = (a.reshape(w[n].shape) for a in outs)
    shapes = {n: w[n].shape for n in SMALL_NAMES}
    g_pkt, d_pkt, m_pkt, v_pkt = _small_sum_adamw(small_all, _pack_small(w), _pack_small(m), _pack_small(v))
    for dst, pkt in ((grads, g_pkt), (delta, d_pkt), (new_m, m_pkt), (new_v, v_pkt)):
        dst.update(_unpack_small(pkt, shapes))
    conv_rows_all = CONV_W * GDN_QKV // LANES
    conv_g_full = g_pkt[SMALL_CONV_ROW:SMALL_CONV_ROW + conv_rows_all].reshape(CONV_W, GDN_QKV)
    conv_g = lax.dynamic_slice_in_dim(conv_g_full, q * (GDN_QKV // N_CHIPS), GDN_QKV // N_CHIPS, axis=1)
    outs, _ = _adamw(w["conv_w"][0], conv_g, m["conv_w"][0], v["conv_w"][0], "adamw_conv_w")
    grads["conv_w"], delta["conv_w"], new_m["conv_w"], new_v["conv_w"] = (a.reshape(w["conv_w"].shape) for a in outs)
    loss = g_pkt[61, 0]
    return (loss, grad_x[None], *[grads[n] for n in WEIGHTS], *[delta[n] for n in WEIGHTS],
            *[new_m[n] for n in WEIGHTS], *[new_v[n] for n in WEIGHTS])
```

```python
import functools

import jax
import jax.numpy as jnp
from jax import lax
from jax.experimental import pallas as pl
from jax.experimental.pallas import tpu as pltpu

f32 = jnp.float32
bf16 = jnp.bfloat16
HI = lax.Precision.HIGHEST
MESH = pl.DeviceIdType.MESH

D_MODEL = 1024
CHUNK = 64
GDN_HEADS = 4
GDN_DK = 128
FOX_HEADS = 8
FOX_DH = 64
CONV_W = 4
D_FF = 4096
D_PLE = 256
LN_EPS = 1e-5
NORM_EPS = 1e-6
ALPHA = 2.0 ** 0.25
GDN_QKV = 1536
OFF_Z = 1536
OFF_BETA = 2048
OFF_FOX = 2056
OFF_F = 3592
D_IN = 3600
ADAM_LR = 0.001
ADAM_B1 = 0.9
ADAM_B2 = 0.999
ADAM_EPS = 1e-08
ADAM_WD = 0.01
ADAM_STEP = 10

SEG_FOX = 2048
SEG_SMALL = 3584
D_CAT = 3840
LANES = 128
TOK_BLK = 256
FOX_BQ = 256
VMEM_LIMIT = 56 * 1024 * 1024
NEG = -1e30

N_CHIPS = 4
W_IN_ROWS = 928


def _params(sem=None, **kw):
    return pltpu.CompilerParams(dimension_semantics=sem, vmem_limit_bytes=VMEM_LIMIT, **kw)


def _sigmoid(x):
    return 1.0 / (1.0 + jnp.exp(-x))


def _softplus(x):
    return jnp.maximum(x, 0.0) + jnp.log(1.0 + jnp.exp(-jnp.abs(x)))


def _ln_fwd(x, g, b):
    mu = jnp.mean(x, -1, keepdims=True)
    xc = x - mu
    var = jnp.mean(xc * xc, -1, keepdims=True)
    rstd = lax.rsqrt(var + LN_EPS)
    xhat = xc * rstd
    return xhat * g + b, xhat, rstd


def _ln_bwd(dy, xhat, rstd, g):
    dxh = dy * g
    m1 = jnp.mean(dxh, -1, keepdims=True)
    m2 = jnp.mean(dxh * xhat, -1, keepdims=True)
    return rstd * (dxh - m1 - xhat * m2)


def _dot(a, b, prec=HI):
    return jnp.dot(a, b, precision=prec, preferred_element_type=f32)


def _dot_nt(a, b, prec=HI):
    return lax.dot_general(a, b, (((1,), (1,)), ((), ())), precision=prec, preferred_element_type=f32)


def _dot_tn(a, b, prec=HI):
    return lax.dot_general(a, b, (((0,), (0,)), ((), ())), precision=prec, preferred_element_type=f32)


def _bdot(a, b):
    return _dot(a.astype(bf16), b.astype(bf16), None)


def _bdot_nt(a, b):
    return _dot_nt(a.astype(bf16), b.astype(bf16), None)


def _bdot_tn(a, b):
    return _dot_tn(a.astype(bf16), b.astype(bf16), None)


def _lane(shape):
    return lax.broadcasted_iota(jnp.int32, shape, len(shape) - 1)


def _mm(a, b, mode, tm, tn, name, out_dtype=f32, epi=None, extra=None, shards=1, comm=None):
    if mode == "nn":
        (m, k), n = a.shape, b.shape[-1] * shards
    elif mode == "nt":
        (m, k), n = a.shape, b.shape[-2]
    else:
        (k, m), n = a.shape, b.shape[1]
    assert m % tm == 0 and n % tn == 0, (name, m, n, tm, tn)
    per = (n // shards) // tn
    assert mode == "nt" or per * tn * shards == n, (name, n, tn, shards)
    nc = 512 if tn % 512 == 0 else (256 if tn % 256 == 0 else 128)
    ks = k // shards

    def body(a_ref, b_ref, *rest):
        for n0 in range(0, tn, nc):
            if mode == "nn":
                acc = jnp.dot(a_ref[...], b_ref[:, n0:n0 + nc], preferred_element_type=f32)
            elif mode == "nt" and shards > 1:
                acc = jnp.zeros((tm, nc), f32)
                for d in range(shards):
                    acc = acc + lax.dot_general(a_ref[:, d * ks:(d + 1) * ks], b_ref[d, n0:n0 + nc, :], (((1,), (1,)), ((), ())),
                                                preferred_element_type=f32)
            elif mode == "nt":
                acc = lax.dot_general(a_ref[...], b_ref[n0:n0 + nc, :], (((1,), (1,)), ((), ())), preferred_element_type=f32)
            else:
                acc = lax.dot_general(a_ref[...], b_ref[:, n0:n0 + nc], (((0,), (0,)), ((), ())), preferred_element_type=f32)
            if epi == "relu2":
                relu_ref, act_ref = rest
                r = jnp.maximum(acc, 0.0)
                relu_ref[:, n0:n0 + nc] = r.astype(bf16)
                act_ref[:, n0:n0 + nc] = (r * r).astype(bf16)
            elif epi == "relu2_bwd":
                relu_ref, o_ref = rest
                o_ref[:, n0:n0 + nc] = (acc * (2.0 * relu_ref[:, n0:n0 + nc].astype(f32))).astype(bf16)
            else:
                (o_ref,) = rest
                o_ref[:, n0:n0 + nc] = acc.astype(out_dtype)

    if mode == "tn":
        a_spec = pl.BlockSpec((k, tm), lambda j, i: (0, i))
    else:
        a_spec = pl.BlockSpec((tm, k), lambda j, i: (i, 0))
    if mode == "nt" and shards > 1:
        b_spec = pl.BlockSpec((shards, tn, ks), lambda j, i: (0, j, 0))
    elif mode == "nt":
        b_spec = pl.BlockSpec((tn, k), lambda j, i: (j, 0))
    elif mode == "nn" and shards > 1:
        b_spec = pl.BlockSpec((None, k, tn), lambda j, i: (j // per, 0, j % per))
    else:
        b_spec = pl.BlockSpec((k, tn), lambda j, i: (0, j))
    o_spec = pl.BlockSpec((tm, tn), lambda j, i: (i, j))
    in_specs = [a_spec, b_spec]
    args = [a, b]
    if epi == "relu2":
        out_shape = (jax.ShapeDtypeStruct((m, n), bf16), jax.ShapeDtypeStruct((m, n), bf16))
        out_specs = (o_spec, o_spec)
    elif epi == "relu2_bwd":
        in_specs.append(o_spec)
        args.append(extra)
        out_shape = jax.ShapeDtypeStruct((m, n), bf16)
        out_specs = o_spec
    elif mode == "tn" and shards > 1:
        out_shape = jax.ShapeDtypeStruct((shards, m, n // shards), out_dtype)
        out_specs = pl.BlockSpec((None, tm, tn), lambda j, i: (j // per, i, j % per))
    else:
        out_shape = jax.ShapeDtypeStruct((m, n), out_dtype)
        out_specs = o_spec
    single = not isinstance(out_shape, tuple)
    res, moved = _hosted(body, comm, name=name, grid=(n // tn, m // tm), in_specs=in_specs,
                         out_specs=(out_specs,) if single else out_specs, out_shape=(out_shape,) if single else out_shape, args=args)
    res = res[0] if single else res
    return res if comm is None else (res, moved)


def _row_spec(width, col=0):
    return pl.BlockSpec((TOK_BLK, width), lambda i: (i, col))


def _vec_spec(rows, width):
    return pl.BlockSpec((rows, width), lambda i: (0, 0))


def _ln_in(x, g, b, comm=None):
    t, d = x.shape

    def body(x_ref, g_ref, b_ref, h_ref, hb_ref):
        h, _, _ = _ln_fwd(x_ref[...], g_ref[...], b_ref[...])
        h_ref[...] = h
        hb_ref[...] = h.astype(bf16)

    return _hosted(
        body, comm, name="ln_in", grid=(t // TOK_BLK,),
        in_specs=[_row_spec(d), _vec_spec(1, d), _vec_spec(1, d)],
        out_specs=(_row_spec(d), _row_spec(d)),
        out_shape=(jax.ShapeDtypeStruct((t, d), f32), jax.ShapeDtypeStruct((t, d), bf16)),
        args=(x, g, b))


def _attn_post(o_gdn, proj, o_fox, g_gdn, g_fox2, comm=None):
    t = o_gdn.shape[0]

    def body(og_ref, z_ref, of_ref, gg_ref, gf_ref, out_ref):
        for h in range(GDN_HEADS):
            sl = slice(h * LANES, (h + 1) * LANES)
            og = og_ref[:, sl]
            z = z_ref[:, sl]
            r = lax.rsqrt(jnp.mean(og * og, -1, keepdims=True) + NORM_EPS)
            out_ref[:, sl] = (og * r * gg_ref[...] * (z * _sigmoid(z))).astype(bf16)
        lo = _lane((TOK_BLK, LANES)) < FOX_DH
        for pr in range(FOX_HEADS // 2):
            sl = slice(pr * LANES, (pr + 1) * LANES)
            of = of_ref[:, sl]
            sq = of * of
            s0 = jnp.sum(jnp.where(lo, sq, 0.0), -1, keepdims=True)
            s1 = jnp.sum(jnp.where(lo, 0.0, sq), -1, keepdims=True)
            r = lax.rsqrt(jnp.where(lo, s0, s1) * (1.0 / FOX_DH) + NORM_EPS)
            out_ref[:, 512 + pr * LANES:512 + (pr + 1) * LANES] = (of * r * gf_ref[...]).astype(bf16)

    return _hosted(
        body, comm, name="attn_post", grid=(t // TOK_BLK,),
        in_specs=[_row_spec(512), _row_spec(512, OFF_Z // 512), _row_spec(512), _vec_spec(1, LANES), _vec_spec(1, LANES)],
        out_specs=(_row_spec(D_MODEL),),
        out_shape=(jax.ShapeDtypeStruct((t, D_MODEL), bf16),),
        args=(o_gdn, proj, o_fox, g_gdn, g_fox2))


def _attn_post_bwd(dattn, o_gdn, proj, o_fox, g_gdn, g_fox2):
    t = o_gdn.shape[0]

    def body(da_ref, og_ref, z_ref, of_ref, gg_ref, gf_ref, dog_ref, dz_ref, dof_ref, pg_ref):
        i = pl.program_id(0)

        @pl.when(i == 0)
        def _():
            pg_ref[...] = jnp.zeros_like(pg_ref)

        dgg = jnp.zeros((1, LANES), f32)
        for h in range(GDN_HEADS):
            sl = slice(h * LANES, (h + 1) * LANES)
            og = og_ref[:, sl]
            z = z_ref[:, sl]
            dout = da_ref[:, sl]
            g = gg_ref[...]
            r = lax.rsqrt(jnp.mean(og * og, -1, keepdims=True) + NORM_EPS)
            sg = _sigmoid(z)
            silu = z * sg
            ng = og * r * g
            dng = dout * silu
            dz_ref[:, sl] = (dout * ng * (sg * (1.0 + z * (1.0 - sg)))).astype(bf16)
            dgg = dgg + jnp.sum(dng * og * r, 0, keepdims=True)
            gd = dng * g
            dog_ref[:, sl] = r * gd - og * (r * r * r) * jnp.mean(og * gd, -1, keepdims=True)
        pg_ref[0:1, :] += dgg
        lo = _lane((TOK_BLK, LANES)) < FOX_DH
        dgf = jnp.zeros((1, LANES), f32)
        for pr in range(FOX_HEADS // 2):
            sl = slice(pr * LANES, (pr + 1) * LANES)
            of = of_ref[:, sl]
            dout = da_ref[:, 512 + pr * LANES:512 + (pr + 1) * LANES]
            g = gf_ref[...]
            sq = of * of
            s0 = jnp.sum(jnp.where(lo, sq, 0.0), -1, keepdims=True)
            s1 = jnp.sum(jnp.where(lo, 0.0, sq), -1, keepdims=True)
            r = lax.rsqrt(jnp.where(lo, s0, s1) * (1.0 / FOX_DH) + NORM_EPS)
            dgf = dgf + jnp.sum(dout * of * r, 0, keepdims=True)
            gd = dout * g
            xg = of * gd
            m0 = jnp.sum(jnp.where(lo, xg, 0.0), -1, keepdims=True)
            m1 = jnp.sum(jnp.where(lo, 0.0, xg), -1, keepdims=True)
            dof_ref[:, sl] = r * gd - of * (r * r * r) * (jnp.where(lo, m0, m1) * (1.0 / FOX_DH))
        pg_ref[1:2, :] += dgf

    return pl.pallas_call(
        body, name="attn_post_bwd", grid=(t // TOK_BLK,),
        in_specs=[_row_spec(D_MODEL), _row_spec(512), _row_spec(512, OFF_Z // 512), _row_spec(512), _vec_spec(1, LANES), _vec_spec(1, LANES)],
        out_specs=(_row_spec(512), _row_spec(512), _row_spec(512), _vec_spec(8, LANES)),
        out_shape=(jax.ShapeDtypeStruct((t, 512), f32), jax.ShapeDtypeStruct((t, 512), bf16),
                   jax.ShapeDtypeStruct((t, 512), f32), jax.ShapeDtypeStruct((8, LANES), f32)),
        compiler_params=_params(("arbitrary",)),
    )(dattn, o_gdn, proj, o_fox, g_gdn, g_fox2)


def _ln1(h0, mix, g, b, comm=None):
    t, d = h0.shape

    def body(h0_ref, mix_ref, g_ref, b_ref, h_ref, hb_ref, xh_ref, rs_ref):
        h, xhat, rstd = _ln_fwd(ALPHA * h0_ref[...] + mix_ref[...], g_ref[...], b_ref[...])
        h_ref[...] = h
        hb_ref[...] = h.astype(bf16)
        xh_ref[...] = xhat
        rs_ref[...] = jnp.broadcast_to(rstd, rs_ref.shape)

    return _hosted(
        body, comm, name="ln1", grid=(t // TOK_BLK,),
        in_specs=[_row_spec(d), _row_spec(d), _vec_spec(1, d), _vec_spec(1, d)],
        out_specs=(_row_spec(d), _row_spec(d), _row_spec(d), _row_spec(LANES)),
        out_shape=(jax.ShapeDtypeStruct((t, d), f32), jax.ShapeDtypeStruct((t, d), bf16),
                   jax.ShapeDtypeStruct((t, d), f32), jax.ShapeDtypeStruct((t, LANES), f32)),
        args=(h0, mix, g, b))


def _ln2_loss(h1, ff, pe, gp, b_gate, g, b, target):
    t, d = h1.shape

    def body(h1_ref, ff_ref, pe_ref, gp_ref, bg_ref, g_ref, b_ref, t_ref, dr_ref, drb_ref, dpe_ref, dgp_ref, pg_ref):
        i = pl.program_id(0)

        @pl.when(i == 0)
        def _():
            pg_ref[...] = jnp.zeros_like(pg_ref)

        sig = _sigmoid(gp_ref[...] + bg_ref[...])
        pe = pe_ref[...]
        r2 = ALPHA * h1_ref[...] + ff_ref[...] + pe * sig
        y, xhat, rstd = _ln_fwd(r2, g_ref[...], b_ref[...])
        err = y - t_ref[...]
        dy = err * (1.0 / d)
        dr = _ln_bwd(dy, xhat, rstd, g_ref[...])
        dr_ref[...] = dr
        drb_ref[...] = dr.astype(bf16)
        dpe_ref[...] = (dr * sig).astype(bf16)
        dgp = dr * pe * sig * (1.0 - sig)
        dgp_ref[...] = dgp.astype(bf16)
        pg_ref[0:1, :] += jnp.sum(dy * xhat, 0, keepdims=True)
        pg_ref[1:2, :] += jnp.sum(dy, 0, keepdims=True)
        pg_ref[2:3, :] += jnp.sum(dgp, 0, keepdims=True)
        pg_ref[3:4, :] += 0.5 * jnp.sum(jnp.mean(err * err, -1, keepdims=True), 0, keepdims=True)

    return pl.pallas_call(
        body, name="ln2_loss", grid=(t // TOK_BLK,),
        in_specs=[_row_spec(d)] * 4 + [_vec_spec(1, d)] * 3 + [_row_spec(d)],
        out_specs=(_row_spec(d), _row_spec(d), _row_spec(d), _row_spec(d), _vec_spec(8, d)),
        out_shape=(jax.ShapeDtypeStruct((t, d), f32), jax.ShapeDtypeStruct((t, d), bf16), jax.ShapeDtypeStruct((t, d), bf16),
                   jax.ShapeDtypeStruct((t, d), bf16), jax.ShapeDtypeStruct((8, d), f32)),
        compiler_params=_params(("arbitrary",)),
    )(h1, ff, pe, gp, b_gate, g, b, target)


def _ln1_bwd(dr2, da, db, xhat, rstd, g):
    t, d = dr2.shape

    def body(dr2_ref, da_ref, db_ref, xh_ref, rs_ref, g_ref, dr_ref, drb_ref, pg_ref):
        i = pl.program_id(0)

        @pl.when(i == 0)
        def _():
            pg_ref[...] = jnp.zeros_like(pg_ref)

        dh = ALPHA * dr2_ref[...] + da_ref[...] + db_ref[...]
        xhat = xh_ref[...]
        dr = _ln_bwd(dh, xhat, rs_ref[:, 0:1], g_ref[...])
        dr_ref[...] = dr
        drb_ref[...] = dr.astype(bf16)
        pg_ref[0:1, :] += jnp.sum(dh * xhat, 0, keepdims=True)
        pg_ref[1:2, :] += jnp.sum(dh, 0, keepdims=True)

    return pl.pallas_call(
        body, name="ln1_bwd", grid=(t // TOK_BLK,),
        in_specs=[_row_spec(d)] * 4 + [_row_spec(LANES), _vec_spec(1, d)],
        out_specs=(_row_spec(d), _row_spec(d), _vec_spec(8, d)),
        out_shape=(jax.ShapeDtypeStruct((t, d), f32), jax.ShapeDtypeStruct((t, d), bf16), jax.ShapeDtypeStruct((8, d), f32)),
        compiler_params=_params(("arbitrary",)),
    )(dr2, da, db, xhat, rstd, g)


def _ln_in_bwd(x, dr1, dmm, g, comm=None):
    t, d = x.shape

    def body(x_ref, dr1_ref, dmm_ref, g_ref, dx_ref, pg_ref):
        i = pl.program_id(0)

        @pl.when(i == 0)
        def _():
            pg_ref[...] = jnp.zeros_like(pg_ref)

        dh = ALPHA * dr1_ref[...] + dmm_ref[...]
        _, xhat, rstd = _ln_fwd(x_ref[...], g_ref[...], 0.0)
        dx_ref[...] = _ln_bwd(dh, xhat, rstd, g_ref[...])
        pg_ref[0:1, :] += jnp.sum(dh * xhat, 0, keepdims=True)
        pg_ref[1:2, :] += jnp.sum(dh, 0, keepdims=True)

    return _hosted(
        body, comm, name="ln_in_bwd", grid=(t // TOK_BLK,),
        in_specs=[_row_spec(d)] * 3 + [_vec_spec(1, d)],
        out_specs=(_row_spec(d), _vec_spec(8, d)),
        out_shape=(jax.ShapeDtypeStruct((t, d), f32), jax.ShapeDtypeStruct((8, d), f32)),
        args=(x, dr1, dmm, g))


def _tri(n, upper=False, strict=False):
    r = lax.broadcasted_iota(jnp.int32, (n, n), 0)
    c = lax.broadcasted_iota(jnp.int32, (n, n), 1)
    if upper:
        m = (c > r) if strict else (c >= r)
    else:
        m = (c < r) if strict else (c <= r)
    return jnp.where(m, 1.0, 0.0).astype(f32)


def _gate_values(x, bias, alog, lane):
    z = x + bias
    return jnp.where(lane < 4, _sigmoid(z), jnp.where(lane < 8, -jnp.exp(alog) * _softplus(z), jnp.where(lane < 16, -_softplus(-z), 0.0)))


def _gates(proj, bias_row, alog_row):
    t = proj.shape[0]
    nch = t // CHUNK

    def body(x_ref, bias_ref, alog_ref, gates_ref, gcum_ref, gcumt_ref):
        lane = _lane((t, LANES))
        gates = _gate_values(x_ref[...], bias_ref[...], alog_ref[...], lane)
        gates_ref[...] = gates
        g3 = gates.reshape(nch, CHUNK, LANES)
        tri = jnp.broadcast_to(_tri(CHUNK)[None], (nch, CHUNK, CHUNK))
        loc = jnp.einsum("bij,bjk->bik", tri, g3, precision=HI, preferred_element_type=f32)
        tot = jnp.sum(g3, axis=1)
        offs = _dot(_tri(nch, strict=True), tot)
        glob = loc + offs[:, None, :]
        lane3 = _lane((nch, CHUNK, LANES))
        gcum = jnp.where(lane3 < 4, g3, jnp.where(lane3 < 8, loc, glob)).reshape(t, LANES)
        gcum_ref[...] = gcum
        gcumt_ref[...] = gcum.T

    return pl.pallas_call(
        body, name="gates", grid=(1,),
        in_specs=[pl.BlockSpec((t, LANES), lambda i: (0, SEG_SMALL // LANES)), _vec_spec(1, LANES), _vec_spec(1, LANES)],
        out_specs=(pl.BlockSpec((t, LANES), lambda i: (0, 0)), pl.BlockSpec((t, LANES), lambda i: (0, 0)),
                   pl.BlockSpec((LANES, t), lambda i: (0, 0))),
        out_shape=(jax.ShapeDtypeStruct((t, LANES), f32), jax.ShapeDtypeStruct((t, LANES), f32), jax.ShapeDtypeStruct((LANES, t), f32)),
        compiler_params=_params(("arbitrary",)),
    )(proj, bias_row, alog_row)


def _gates_bwd(proj, bias_row, alog_row, gates, dgates, dccol, dct):
    t = proj.shape[0]
    nch = t // CHUNK

    def body(x_ref, bias_ref, alog_ref, gates_ref, dg_ref, dcc_ref, dct_ref, dx_ref, pg_ref):
        lane = _lane((t, LANES))
        d = dg_ref[...] + dcc_ref[...] + dct_ref[...].T
        d3 = d.reshape(nch, CHUNK, LANES)
        tri = jnp.broadcast_to(_tri(CHUNK, upper=True)[None], (nch, CHUNK, CHUNK))
        loc = jnp.einsum("bij,bjk->bik", tri, d3, precision=HI, preferred_element_type=f32)
        tot = jnp.sum(d3, axis=1)
        offs = _dot(_tri(nch, upper=True, strict=True), tot)
        glob = loc + offs[:, None, :]
        lane3 = _lane((nch, CHUNK, LANES))
        dpre = jnp.where(lane3 < 4, d3, jnp.where(lane3 < 8, loc, glob)).reshape(t, LANES)
        z = x_ref[...] + bias_ref[...]
        sg = _sigmoid(z)
        dx = jnp.where(lane < 4, dpre * sg * (1.0 - sg),
                       jnp.where(lane < 8, dpre * (-jnp.exp(alog_ref[...])) * sg, jnp.where(lane < 16, dpre * (1.0 - sg), 0.0)))
        dx_ref[...] = dx.astype(bf16)
        pg_ref[...] = jnp.zeros_like(pg_ref)
        pg_ref[0:1, :] = jnp.sum(dx, 0, keepdims=True)
        pg_ref[1:2, :] = jnp.sum(jnp.where((lane >= 4) & (lane < 8), dpre * gates_ref[...], 0.0), 0, keepdims=True)

    full = pl.BlockSpec((t, LANES), lambda i: (0, 0))
    return pl.pallas_call(
        body, name="gates_bwd", grid=(1,),
        in_specs=[pl.BlockSpec((t, LANES), lambda i: (0, SEG_SMALL // LANES)), _vec_spec(1, LANES), _vec_spec(1, LANES),
                  full, full, full, pl.BlockSpec((LANES, t), lambda i: (0, 0))],
        out_specs=(full, _vec_spec(8, LANES)),
        out_shape=(jax.ShapeDtypeStruct((t, LANES), bf16), jax.ShapeDtypeStruct((8, LANES), f32)),
        compiler_params=_params(("arbitrary",)),
    )(proj, bias_row, alog_row, gates, dgates, dccol, dct)


def _conv_act(u, cw, row, t):
    c = cw[3:4, :] * u
    for jj in range(CONV_W - 1):
        sh = CONV_W - 1 - jj
        c = c + cw[jj:jj + 1, :] * jnp.where(row >= sh, pltpu.roll(u, sh, axis=0), 0.0)
    return c


def _gdn_conv(proj, conv_w, comm=None):
    t = proj.shape[0]
    nblk = GDN_QKV // LANES

    def body(u_ref, cw_ref, c_ref, y_ref):
        j = pl.program_id(0)
        row = lax.broadcasted_iota(jnp.int32, (t, LANES), 0)
        c = _conv_act(u_ref[...], cw_ref[...], row, t)
        c_ref[...] = c
        s = c * _sigmoid(c)
        r = lax.rsqrt(jnp.sum(s * s, -1, keepdims=True) + NORM_EPS)
        scale = jnp.where(j < GDN_HEADS, GDN_DK ** -0.5, 1.0)
        y_ref[...] = jnp.where(j < 2 * GDN_HEADS, s * (r * scale), s)

    blk = pl.BlockSpec((t, LANES), lambda j: (0, j))
    return _hosted(
        body, comm, name="gdn_conv", grid=(nblk,),
        in_specs=[blk, pl.BlockSpec((CONV_W, LANES), lambda j: (0, j))],
        out_specs=(blk, blk),
        out_shape=(jax.ShapeDtypeStruct((t, GDN_QKV), f32), jax.ShapeDtypeStruct((t, GDN_QKV), f32)),
        args=(proj, conv_w))


def _gdn_conv_bwd(proj, conv_w, c, dy, comm=None):
    t = proj.shape[0]
    nblk = GDN_QKV // LANES

    def body(u_ref, cw_ref, c_ref, dy_ref, du_ref, dcw_ref):
        j = pl.program_id(0)
        row = lax.broadcasted_iota(jnp.int32, (t, LANES), 0)
        u = u_ref[...]
        cw = cw_ref[...]
        c = c_ref[...]
        dy = dy_ref[...]
        sg = _sigmoid(c)
        s = c * sg
        r = lax.rsqrt(jnp.sum(s * s, -1, keepdims=True) + NORM_EPS)
        n = s * r
        scale = jnp.where(j < GDN_HEADS, GDN_DK ** -0.5, 1.0)
        dn = dy * scale
        ds = jnp.where(j < 2 * GDN_HEADS, r * (dn - n * jnp.sum(dn * n, -1, keepdims=True)), dy)
        dc = ds * (sg * (1.0 + c * (1.0 - sg)))
        du = cw[3:4, :] * dc
        dcw_ref[...] = jnp.zeros_like(dcw_ref)
        dcw_ref[3:4, :] = jnp.sum(dc * u, 0, keepdims=True)
        for jj in range(CONV_W - 1):
            sh = CONV_W - 1 - jj
            du = du + cw[jj:jj + 1, :] * jnp.where(row < t - sh, pltpu.roll(dc, t - sh, axis=0), 0.0)
            dcw_ref[jj:jj + 1, :] = jnp.sum(dc * jnp.where(row >= sh, pltpu.roll(u, sh, axis=0), 0.0), 0, keepdims=True)
        du_ref[...] = du.astype(bf16)

    blk = pl.BlockSpec((t, LANES), lambda j: (0, j))
    return _hosted(
        body, comm, name="gdn_conv_bwd", grid=(nblk,),
        in_specs=[blk, pl.BlockSpec((CONV_W, LANES), lambda j: (0, j)), blk, blk],
        out_specs=(blk, pl.BlockSpec((8, LANES), lambda j: (0, j))),
        out_shape=(jax.ShapeDtypeStruct((t, GDN_QKV), bf16), jax.ShapeDtypeStruct((8, GDN_QKV), f32)),
        args=(proj, conv_w, c, dy))


def _chunk_masks():
    r = lax.broadcasted_iota(jnp.int32, (CHUNK, CHUNK), 0)
    c = lax.broadcasted_iota(jnp.int32, (CHUNK, CHUNK), 1)
    return r >= c, r > c, r == c


def _col_to_row(col, eye):
    return jnp.sum(jnp.where(eye, col, 0.0), axis=0, keepdims=True)


def _row_to_col(row, eye):
    return jnp.sum(jnp.where(eye, row, 0.0), axis=1, keepdims=True)


NN = (((1,), (0,)), ((), ()))
NT = (((1,), (1,)), ((), ()))
TN = (((0,), (0,)), ((), ()))
GDN_GROUP = 4


def _mx(a, b, dims=NN, passes=1):
    d = lambda p, q: lax.dot_general(p, q, dims, preferred_element_type=f32)
    ah, bh = a.astype(bf16), b.astype(bf16)
    if passes == 1:
        return d(ah, bh)
    al = (a - ah.astype(f32)).astype(bf16)
    bl = (b - bh.astype(f32)).astype(bf16)
    return d(ah, bh) + (d(ah, bl) + d(al, bh))


def _gdn_decay(gam, masks):
    causal, _, eye = masks
    return jnp.exp(jnp.where(causal, gam - _col_to_row(gam, eye), NEG))


def _gdn_local(y, gcum, comm=None):
    t = y.shape[0]
    nch = t // CHUNK
    rows_blk = GDN_GROUP * CHUNK

    def body(y_ref, g_ref, u_ref, w_ref, qk_ref, tinv_ref):
        masks = _chunk_masks()
        _, strict, eye = masks
        ids = [(j, h) for j in range(GDN_GROUP) for h in range(GDN_HEADS)]
        rs = lambda j: slice(j * CHUNK, (j + 1) * CHUNK)
        col = lambda base, h: slice(base + h * LANES, base + (h + 1) * LANES)
        kn = [y_ref[rs(j), col(512, h)] for j, h in ids]
        beta = [g_ref[rs(j), h:h + 1] for j, h in ids]
        gam = [g_ref[rs(j), 4 + h:5 + h] for j, h in ids]
        dec = [_gdn_decay(g, masks) for g in gam]
        x = [-jnp.where(strict, _mx(k, k, NT) * d * b, 0.0) for k, d, b in zip(kn, dec, beta)]
        tinv = [jnp.where(eye, 1.0, 0.0) + a for a in x]
        for _ in range(5):
            x = [_mx(a, a, NN, 3) for a in x]
            tinv = [t_ + _mx(t_, a, NN, 3) for t_, a in zip(tinv, x)]
        for (j, h), t_, k, d, b, g in zip(ids, tinv, kn, dec, beta, gam):
            u_ref[rs(j), col(0, h)] = _mx(t_, b * y_ref[rs(j), col(1024, h)])
            w_ref[rs(j), col(0, h)] = _mx(t_, (b * jnp.exp(g)) * k)
            qk_ref[j, h] = _mx(y_ref[rs(j), col(0, h)], k, NT) * d
            tinv_ref[j, h] = t_

    mat = pl.BlockSpec((GDN_GROUP, GDN_HEADS, CHUNK, CHUNK), lambda n: (n, 0, 0, 0))
    return _hosted(
        body, comm, name="gdn_local", grid=(nch // GDN_GROUP,),
        in_specs=[pl.BlockSpec((rows_blk, GDN_QKV), lambda n: (n, 0)), pl.BlockSpec((rows_blk, LANES), lambda n: (n, 0))],
        out_specs=(pl.BlockSpec((rows_blk, 512), lambda n: (n, 0)), pl.BlockSpec((rows_blk, 512), lambda n: (n, 0)), mat, mat),
        out_shape=(jax.ShapeDtypeStruct((t, 512), f32), jax.ShapeDtypeStruct((t, 512), f32),
                   jax.ShapeDtypeStruct((nch, GDN_HEADS, CHUNK, CHUNK), f32), jax.ShapeDtypeStruct((nch, GDN_HEADS, CHUNK, CHUNK), f32)),
        args=(y, gcum))


def _gdn_fwd(y, gcum, u, w, qk, comm=None):
    t = y.shape[0]
    nch = t // CHUNK

    def body(y_ref, g_ref, u_ref, w_ref, qk_ref, o_ref, sall_ref, s_ref):
        @pl.when(pl.program_id(0) == 0)
        def _():
            s_ref[...] = jnp.zeros_like(s_ref)

        heads = range(GDN_HEADS)
        sl = [slice(h * LANES, (h + 1) * LANES) for h in heads]
        gam = [g_ref[:, 4 + h:5 + h] for h in heads]
        gam_last = [g[CHUNK - 1:CHUNK, :] for g in gam]
        s = [s_ref[h] for h in heads]
        for h in heads:
            sall_ref[0, h] = s[h]
        ws = [_mx(w_ref[:, sl[h]], s[h]) for h in heads]
        qs = [_mx(y_ref[:, sl[h]] * jnp.exp(gam[h]), s[h]) for h in heads]
        vn = [u_ref[:, sl[h]] - ws[h] for h in heads]
        av = [_mx(qk_ref[0, h], vn[h]) for h in heads]
        kv = [_mx(y_ref[:, 512 + h * LANES:512 + (h + 1) * LANES] * jnp.exp(gam_last[h] - gam[h]), vn[h], TN) for h in heads]
        for h in heads:
            o_ref[:, sl[h]] = qs[h] + av[h]
            s_ref[h] = jnp.exp(gam_last[h]) * s[h] + kv[h]

    row = lambda width: pl.BlockSpec((CHUNK, width), lambda n: (n, 0))
    return _hosted(
        body, comm, name="gdn_fwd", grid=(nch,),
        in_specs=[row(GDN_QKV), row(LANES), row(512), row(512), pl.BlockSpec((1, GDN_HEADS, CHUNK, CHUNK), lambda n: (n, 0, 0, 0))],
        out_specs=(row(512), pl.BlockSpec((1, GDN_HEADS, LANES, LANES), lambda n: (n, 0, 0, 0))),
        out_shape=(jax.ShapeDtypeStruct((t, 512), f32), jax.ShapeDtypeStruct((nch, GDN_HEADS, LANES, LANES), f32)),
        scratch_shapes=[pltpu.VMEM((GDN_HEADS, LANES, LANES), f32)],
        args=(y, gcum, u, w, qk))


def _gdn_bwd(y, gcum, u_all, w_all, qk_all, tinv_all, sall, do, comm=None):
    t = y.shape[0]
    nch = t // CHUNK

    def body(y_ref, g_ref, u_ref, w_ref, qk_ref, tinv_ref, sall_ref, do_ref, dy_ref, dg_ref, ds_ref):
        @pl.when(pl.program_id(0) == 0)
        def _():
            ds_ref[...] = jnp.zeros_like(ds_ref)

        masks = _chunk_masks()
        causal, strict, eye = masks
        lane = _lane((CHUNK, LANES))
        row = lax.broadcasted_iota(jnp.int32, (CHUNK, 1), 0)
        heads = range(GDN_HEADS)
        each = lambda f, *ls: [f(*a) for a in zip(*ls)]
        rsum = lambda a: jnp.sum(a, axis=1, keepdims=True)
        sl = [slice(h * LANES, (h + 1) * LANES) for h in heads]
        qn = [y_ref[:, sl[h]] for h in heads]
        kn = [y_ref[:, 512 + h * LANES:512 + (h + 1) * LANES] for h in heads]
        v = [y_ref[:, 1024 + h * LANES:1024 + (h + 1) * LANES] for h in heads]
        beta = [g_ref[:, h:h + 1] for h in heads]
        gam = [g_ref[:, 4 + h:5 + h] for h in heads]
        gam_last = [g[CHUNK - 1:CHUNK, :] for g in gam]
        dec = [_gdn_decay(g, masks) for g in gam]
        e = [jnp.exp(g) for g in gam]
        f = each(lambda gl_, g: jnp.exp(gl_ - g), gam_last, gam)
        gl = [jnp.exp(g) for g in gam_last]
        u = [u_ref[:, sl[h]] for h in heads]
        w = [w_ref[:, sl[h]] for h in heads]
        qk = [qk_ref[0, h] for h in heads]
        tinv = [tinv_ref[0, h] for h in heads]
        s = [sall_ref[0, h] for h in heads]
        dsn = [ds_ref[h] for h in heads]
        d_o = [do_ref[:, sl[h]] for h in heads]
        qd = each(lambda a, b: a * b, qn, e)
        kd = each(lambda a, b: a * b, kn, f)
        ws = each(_mx, w, s)
        kds = each(_mx, kd, dsn)
        qkdo = each(lambda a, b: _mx(a, b, TN), qk, d_o)
        dqd = each(lambda a, b: _mx(a, b, NT), d_o, s)
        qddo = each(lambda a, b: _mx(a, b, TN), qd, d_o)
        kkd = each(lambda k, d: _mx(k, k, NT) * d, kn, dec)
        vn = each(lambda a, b: a - b, u, ws)
        dvn = each(lambda a, b: a + b, qkdo, kds)
        dqk = each(lambda a, b: jnp.where(causal, _mx(a, b, NT), 0.0), d_o, vn)
        dkd = each(lambda a, b: _mx(a, b, NT), vn, dsn)
        dw = each(lambda a, b: -_mx(a, b, NT), dvn, s)
        wdvn = each(lambda a, b: _mx(a, b, TN), w, dvn)
        dgl = each(lambda a, b: jnp.sum(rsum(a * b), axis=0, keepdims=True), dsn, s)
        for h in heads:
            ds_ref[h] = qddo[h] - wdvn[h] + gl[h] * dsn[h]
        dru = each(lambda a, b: _mx(a, b, TN), tinv, dvn)
        drw = each(lambda a, b: _mx(a, b, TN), tinv, dw)
        dqkr = each(lambda a, b: a * b, dqk, dec)
        dq1 = each(_mx, dqkr, kn)
        dk1 = each(lambda a, b: _mx(a, b, TN), dqkr, qn)
        dnu = each(lambda a, b: _mx(a, b, NT), dru, u)
        dnw = each(lambda a, b: _mx(a, b, NT), drw, w)
        dn = each(lambda a, b: jnp.where(strict, -(a + b), 0.0), dnu, dnw)
        dkk = each(lambda a, b, d: a * b * d, dn, beta, dec)
        dk2 = each(_mx, dkk, kn)
        dk3 = each(lambda a, b: _mx(a, b, TN), dkk, kn)
        dgates = jnp.zeros((CHUNK, LANES), f32)
        for h in heads:
            drw_k = rsum(drw[h] * kn[h])
            dbeta = rsum(dru[h] * v[h]) + e[h] * drw_k + rsum(dn[h] * kkd[h])
            m = dn[h] * (kkd[h] * beta[h]) + dqk[h] * qk[h]
            de = beta[h] * drw_k + rsum(dqd[h] * qn[h])
            df = rsum(dkd[h] * kn[h])
            dgam = rsum(m) - _row_to_col(jnp.sum(m, axis=0, keepdims=True), eye) + de * e[h] - df * f[h]
            dgam_last = jnp.sum(df * f[h], axis=0, keepdims=True) + dgl[h] * gl[h]
            dgam = dgam + jnp.where(row == CHUNK - 1, dgam_last, 0.0)
            dy_ref[:, sl[h]] = dq1[h] + dqd[h] * e[h]
            dy_ref[:, 512 + h * LANES:512 + (h + 1) * LANES] = (beta[h] * e[h]) * drw[h] + dk2[h] + dk3[h] + dk1[h] + dkd[h] * f[h]
            dy_ref[:, 1024 + h * LANES:1024 + (h + 1) * LANES] = beta[h] * dru[h]
            dgates = dgates + jnp.where(lane == h, dbeta, 0.0) + jnp.where(lane == 4 + h, dgam, 0.0)
        dg_ref[...] = dgates

    rev = lambda width: pl.BlockSpec((CHUNK, width), lambda n: (nch - 1 - n, 0))
    mat = lambda d: pl.BlockSpec((1, GDN_HEADS, d, d), lambda n: (nch - 1 - n, 0, 0, 0))
    return _hosted(
        body, comm, name="gdn_bwd", grid=(nch,),
        in_specs=[rev(GDN_QKV), rev(LANES), rev(512), rev(512), mat(CHUNK), mat(CHUNK), mat(LANES), rev(512)],
        out_specs=(rev(GDN_QKV), rev(LANES)),
        out_shape=(jax.ShapeDtypeStruct((t, GDN_QKV), f32), jax.ShapeDtypeStruct((t, LANES), f32)),
        scratch_shapes=[pltpu.VMEM((GDN_HEADS, LANES, LANES), f32)],
        args=(y, gcum, u_all, w_all, qk_all, tinv_all, sall, do))


FOX_CLASSES = 4


def _fox_groups(t):
    nq = t // FOX_BQ
    ncls = min(FOX_CLASSES, nq)
    per = nq // ncls
    return [(g * per, per, (g + 1) * per * FOX_BQ) for g in range(ncls)]


def _fox_scores(q_ref, k_ref, gcum_ref, gcumt_ref, h, i, keys):
    pr = h // 2
    lo = (h % 2) * FOX_DH
    lane = _lane((FOX_BQ, LANES))
    mask = (lane >= lo) & (lane < lo + FOX_DH)
    qm = jnp.where(mask, q_ref[:, pr * LANES:(pr + 1) * LANES], 0.0).astype(bf16)
    kp = k_ref[:, pr * LANES:(pr + 1) * LANES].astype(bf16)
    s = _dot_nt(qm, kp, None) * (FOX_DH ** -0.5)
    s = s + gcum_ref[:, 8 + h:9 + h] - gcumt_ref[8 + h:9 + h, :]
    rows = i * FOX_BQ + lax.broadcasted_iota(jnp.int32, (FOX_BQ, keys), 0)
    cols = lax.broadcasted_iota(jnp.int32, (FOX_BQ, keys), 1)
    return jnp.where(cols <= rows, s, NEG), mask, qm, kp


def _fox_fwd(proj, gcum, gcumt, ride=None):
    c0 = SEG_FOX // 512

    def group_call(q0, nq, keys, comm):
        def body(q_ref, k_ref, v_ref, gcum_ref, gcumt_ref, o_ref, lse_ref):
            i = q0 + pl.program_id(0)
            lane = _lane((FOX_BQ, LANES))
            lse_all = jnp.zeros((FOX_BQ, LANES), f32)
            for pr in range(FOX_HEADS // 2):
                vp = v_ref[:, pr * LANES:(pr + 1) * LANES].astype(bf16)
                o_pair = jnp.zeros((FOX_BQ, LANES), f32)
                for h in (2 * pr, 2 * pr + 1):
                    s, mask, _, _ = _fox_scores(q_ref, k_ref, gcum_ref, gcumt_ref, h, i, keys)
                    m = jnp.max(s, axis=1, keepdims=True)
                    p = jnp.exp(s - m)
                    l = jnp.sum(p, axis=1, keepdims=True)
                    o_h = _dot((p * (1.0 / l)).astype(bf16), vp, None)
                    o_pair = jnp.where(mask, o_h, o_pair)
                    lse_all = jnp.where(lane == h, m + jnp.log(l), lse_all)
                o_ref[:, pr * LANES:(pr + 1) * LANES] = o_pair
            lse_ref[...] = lse_all

        seen = lambda col: pl.BlockSpec((keys, 512), lambda i: (0, col))
        return _hosted(
            body, comm, name=f"fox_fwd_{keys}", grid=(nq,),
            in_specs=[pl.BlockSpec((FOX_BQ, 512), lambda i: (q0 + i, c0)), seen(c0 + 1), seen(c0 + 2),
                      pl.BlockSpec((FOX_BQ, LANES), lambda i: (q0 + i, 0)), pl.BlockSpec((LANES, keys), lambda i: (0, 0))],
            out_specs=(pl.BlockSpec((FOX_BQ, 512), lambda i: (i, 0)), pl.BlockSpec((FOX_BQ, LANES), lambda i: (i, 0))),
            out_shape=(jax.ShapeDtypeStruct((nq * FOX_BQ, 512), f32), jax.ShapeDtypeStruct((nq * FOX_BQ, LANES), f32)),
            args=(proj, proj, proj, gcum, gcumt))

    parts = []
    for n, g in enumerate(_fox_groups(proj.shape[0])):
        hook = ride(n) if ride else None
        part, moved = group_call(*g, hook[0] if hook else None)
        parts.append(part)
        if hook:
            hook[1](moved)
    return jnp.concatenate([o for o, _ in parts], axis=0), jnp.concatenate([l for _, l in parts], axis=0)


def _fox_bwd(proj, gcum, gcumt, o, lse, do, ride=None):
    t = proj.shape[0]
    c0 = SEG_FOX // 512

    def group_call(q0, nq, keys, acc, comm):
        first = acc is None

        def body(q_ref, k_ref, v_ref, gcum_ref, gcumt_ref, o_ref, lse_ref, do_ref, *rest):
            dq_ref, dk_ref, dv_ref, dcc_ref, dct_ref = rest[-5:]
            j = pl.program_id(0)
            i = q0 + j

            @pl.when(j == 0)
            def _():
                if first:
                    dk_ref[...] = jnp.zeros_like(dk_ref)
                    dv_ref[...] = jnp.zeros_like(dv_ref)
                    dct_ref[...] = jnp.zeros_like(dct_ref)
                else:
                    dk_ref[...], dv_ref[...], dct_ref[...] = rest[0][...], rest[1][...], rest[2][...]

            lane = _lane((FOX_BQ, LANES))
            dcc = jnp.zeros((FOX_BQ, LANES), f32)
            scale = FOX_DH ** -0.5
            for pr in range(FOX_HEADS // 2):
                sl = slice(pr * LANES, (pr + 1) * LANES)
                vp = v_ref[:, sl].astype(bf16)
                dq_pair = jnp.zeros((FOX_BQ, LANES), f32)
                for h in (2 * pr, 2 * pr + 1):
                    s, mask, qm, kp = _fox_scores(q_ref, k_ref, gcum_ref, gcumt_ref, h, i, keys)
                    p = jnp.exp(s - lse_ref[:, h:h + 1])
                    dom = jnp.where(mask, do_ref[:, sl], 0.0)
                    delta = jnp.sum(dom * o_ref[:, sl], axis=1, keepdims=True)
                    domb = dom.astype(bf16)
                    ds = p * (_dot_nt(domb, vp, None) - delta)
                    dsb = ds.astype(bf16)
                    dv_ref[:, sl] += _dot_tn(p.astype(bf16), domb, None)
                    dk_ref[:, sl] += _dot_tn(dsb, qm, None) * scale
                    dq_pair = jnp.where(mask, _dot(dsb, kp, None) * scale, dq_pair)
                    dcc = jnp.where(lane == 8 + h, jnp.sum(ds, axis=1, keepdims=True), dcc)
                    dct_ref[8 + h:9 + h, :] += -jnp.sum(ds, axis=0, keepdims=True)
                dq_ref[:, sl] = dq_pair.astype(bf16)
            dcc_ref[...] = dcc

        qblk = lambda col: pl.BlockSpec((FOX_BQ, 512), lambda i: (q0 + i, col))
        oblk = pl.BlockSpec((FOX_BQ, 512), lambda i: (i, 0))
        seen = lambda col: pl.BlockSpec((keys, 512), lambda i: (0, col))
        rblk = pl.BlockSpec((FOX_BQ, LANES), lambda i: (q0 + i, 0))
        seen_t = pl.BlockSpec((LANES, keys), lambda i: (0, 0))
        in_specs = [qblk(c0), seen(c0 + 1), seen(c0 + 2), rblk, seen_t, qblk(0), rblk, qblk(0)]
        args = [proj, proj, proj, gcum, gcumt, o, lse, do]
        aliases = {}
        if not first:
            in_specs += [seen(0), seen(0), seen_t]
            args += list(acc)
            aliases = {8: 1, 9: 2, 10: 4}
        return _hosted(
            body, comm, name=f"fox_bwd_{keys}", grid=(nq,), in_specs=in_specs,
            out_specs=(oblk, seen(0), seen(0), pl.BlockSpec((FOX_BQ, LANES), lambda i: (i, 0)), seen_t),
            out_shape=(jax.ShapeDtypeStruct((nq * FOX_BQ, 512), bf16), jax.ShapeDtypeStruct((t, 512), f32), jax.ShapeDtypeStruct((t, 512), f32),
                       jax.ShapeDtypeStruct((nq * FOX_BQ, LANES), f32), jax.ShapeDtypeStruct((LANES, t), f32)),
            aliases=aliases, args=args)

    acc, dqs, dccs = None, [], []
    for n, g in enumerate(reversed(_fox_groups(t))):
        hook = ride(n) if ride else None
        (dq, dk, dv, dcc, dct), moved = group_call(*g, acc, hook[0] if hook else None)
        if hook:
            hook[1](moved)
        acc = (dk, dv, dct)
        dqs.insert(0, dq)
        dccs.insert(0, dcc)
    return jnp.concatenate(dqs, axis=0), acc[0], acc[1], jnp.concatenate(dccs, axis=0), acc[2]


def _row(v, width=None):
    v = v.reshape(1, -1).astype(f32)
    if width is not None and v.shape[1] < width:
        v = jnp.pad(v, ((0, 0), (0, width - v.shape[1])))
    return v


LATE = ("w_out", "w_up", "w_ple_gate", "w_ple", "w_down")


def _device_grads(x, p, target, small, w_cat, conv_w, late, qc=None, tail=None, ln_in_out=None):
    z4 = jnp.zeros((4,), f32)
    bias_row = _row(jnp.concatenate([z4, small["dt_bias"].reshape(-1), small["b_f"].reshape(-1)]), LANES)
    alog_row = _row(jnp.concatenate([z4, small["a_log"].reshape(-1)]), LANES)
    g_gdn = _row(small["gdn_norm_g"])
    g_fox2 = _row(jnp.tile(small["fox_norm_g"].reshape(-1), 2))
    pb = p.astype(bf16)
    late = list(late)
    comm = qc is not None

    h0, h0b = ln_in_out if ln_in_out is not None else _ln_in(x, _row(small["ln_in_g"]), _row(small["ln_in_b"]))[0]
    proj = _mm(h0b, w_cat, "nt", 256, D_CAT, "mm_proj")
    gates, gcum, gcumt = _gates(proj, bias_row, alog_row)
    w_down_pieces = [(4, 0, 1)]

    def gather(phase, pieces):
        if not comm or not pieces:
            return None, lambda moved: None
        touched = sorted({i for i, _, _ in pieces})

        def took(moved):
            for i, buf in zip(touched, moved):
                late[i] = buf
        return phase([late[i] for i in touched], [(touched.index(i), k, n) for i, k, n in pieces]), took

    over, on = _gather_chips, _gather_pass_on
    cm, took = gather(over, [(0, 0, 1), (3, 0, 1)])
    (conv_c, qkv_n), moved = _gdn_conv(proj, conv_w, cm)
    took(moved)
    cm, took = gather(over, [(1, 0, 2)])
    (gu, gw, gqk, gtinv), moved = _gdn_local(qkv_n, gcum, cm)
    took(moved)
    cm, took = gather(over, [(1, 1, 2)])
    (o_gdn, sall), moved = _gdn_fwd(qkv_n, gcum, gu, gw, gqk, cm)
    took(moved)
    fox_plan = [(over, []), (on, [(0, 0, 1), (3, 0, 1), (1, 0, 2), (1, 1, 2)]), (over, [(2, 0, 1)]), (over, [(4, 0, 2)])]
    assert not comm or len(_fox_groups(x.shape[0])) == len(fox_plan)
    o_fox, lse = _fox_fwd(proj, gcum, gcumt, (lambda n: gather(*fox_plan[n])) if comm else None)
    cm, took = gather(on, [(2, 0, 1)])
    (attn,), moved = _attn_post(o_gdn, proj, o_fox, g_gdn, g_fox2, cm)
    took(moved)
    w_out = late[0].reshape(D_MODEL, D_MODEL)
    mix = _mm(attn, w_out, "nn", 512, D_MODEL, "mm_mix")
    (h1, h1b, xhat1, rstd1), _ = _ln1(h0, mix, _row(small["ln1_g"]), _row(small["ln1_b"]))
    w_up, w_ple = late[1], late[3]
    cm, took = gather(over, [(4, 1, 2)])
    up_act = _mm(h1b, w_up, "nn", 512, 1024, "mm_up", epi="relu2", shards=N_CHIPS, comm=cm)
    if cm:
        up_act, moved = up_act
        took(moved)
    up, act = up_act
    w_gate = late[2].reshape(D_MODEL, D_MODEL)
    cm, took = gather(on, w_down_pieces)
    gp = _mm(h1b, w_gate, "nn", 512, D_MODEL, "mm_gate", comm=cm)
    if cm:
        gp, moved = gp
        took(moved)
    pe = _mm(pb, w_ple, "nn", 512, D_MODEL // N_CHIPS, "mm_ple", shards=N_CHIPS)
    w_down = late[4].reshape(D_FF, D_MODEL)
    ff = _mm(act, w_down, "nn", 256, D_MODEL, "mm_down")
    dr2, dr2b, dpe, dgp, pg2 = _ln2_loss(h1, ff, pe, gp, _row(small["b_ple_gate"]), _row(small["ln2_g"]), _row(small["ln2_b"]), target)

    dup = _mm(dr2b, w_down, "nt", 256, 2048, "mm_dact", epi="relu2_bwd", extra=up)
    g_down = _mm(act, dr2b, "tn", 1024, D_MODEL, "mm_gdown")
    dh1_a = _mm(dup, w_up, "nt", 256, D_MODEL, "mm_dh1a", shards=N_CHIPS)
    g_up = _mm(h1b, dup, "tn", 1024, 1024, "mm_gup", shards=N_CHIPS)
    dh1_b = _mm(dgp, w_gate, "nt", 512, D_MODEL, "mm_dh1b")
    g_gate = _mm(h1b, dgp, "tn", 1024, D_MODEL, "mm_ggate")
    g_ple = _mm(pb, dpe, "tn", D_PLE, D_MODEL // N_CHIPS, "mm_gple", shards=N_CHIPS)
    dr1, dr1b, pg1 = _ln1_bwd(dr2, dh1_a, dh1_b, xhat1, rstd1, _row(small["ln1_g"]))
    dattn = _mm(dr1b, w_out, "nt", 512, D_MODEL, "mm_dattn")
    g_out = _mm(attn, dr1b, "tn", 1024, D_MODEL, "mm_gout")
    do_gdn, dz, do_fox, pga = _attn_post_bwd(dattn, o_gdn, proj, o_fox, g_gdn, g_fox2)
    g_late = [g.reshape((N_CHIPS, -1, g.shape[-1])) for g in (g_out, g_up, g_gate, g_ple, g_down)]
    chip_plan = [[(4, 0, 2), (0, 0, 1)], [(4, 1, 2)], [(2, 0, 1), (3, 0, 1)], [(1, 0, 2), (1, 1, 2)]]
    state = {}

    def to_sibling():
        def took(moved):
            sums = [_add_pair(g, b1, qc, "add_pair_" + n) for g, b1, n in zip(g_late, moved, LATE)]
            state.update(own=[a for a, _ in sums], sent=[ab for _, ab in sums], landing=_landing([ab for _, ab in sums]))
        return _exchange_pairs(g_late), took

    def to_chips(pieces):
        if not comm:
            return None, lambda moved: None
        return _exchange_chips(state["sent"], state["landing"], pieces), lambda moved: state.update(landing=list(moved))

    assert not comm or len(_fox_groups(x.shape[0])) == len(chip_plan)
    dfq, dfk, dfv, dccol, dct = _fox_bwd(proj, gcum, gcumt, o_fox, lse, do_fox,
                                         (lambda n: to_sibling() if n == 0 else to_chips(chip_plan[n - 1])) if comm else None)
    cm, took = to_chips(chip_plan[-1])
    (dqkv_n, dgates), moved = _gdn_bwd(qkv_n, gcum, gu, gw, gqk, gtinv, sall, do_gdn, cm)
    took(moved)
    dsmall, pgg = _gates_bwd(proj, bias_row, alog_row, gates, dgates, dccol, dct)
    cm = None
    if comm:
        cm = _share_halves([_add_chips(a, b2, qc, "add_chips_" + n) for a, b2, n in zip(state["own"], state["landing"], LATE)])
    (du, g_conv8), reduced = _gdn_conv_bwd(proj, conv_w, conv_c, dqkv_n, cm)
    if comm:
        g_late = list(reduced)
    t = x.shape[0]
    dproj = jnp.concatenate([du, dz, dfq, dfk.astype(bf16), dfv.astype(bf16), dsmall, jnp.zeros((t, D_CAT - SEG_SMALL - LANES), bf16)], axis=1)
    g_cat = _mm(dproj, h0b, "tn", 1280, D_MODEL, "mm_gcat")
    cm, took = tail[0](g_cat) if tail else (None, None)
    dh0_mm = _mm(dproj, w_cat, "nn", 256, D_MODEL, "mm_dh0", comm=cm)
    if cm:
        dh0_mm, moved = dh0_mm
        took(moved)
    cm, took = tail[1]() if tail and tail[1] else (None, None)
    (grad_x, pg0), moved = _ln_in_bwd(x, dr1, dh0_mm, _row(small["ln_in_g"]), cm)
    if cm:
        took(moved)

    g_fox = pga[1, :FOX_DH] + pga[1, FOX_DH:]
    small_grads = dict(
        ln_in_g=pg0[0], ln_in_b=pg0[1], ln1_g=pg1[0], ln1_b=pg1[1], b_ple_gate=pg2[2], ln2_g=pg2[0], ln2_b=pg2[1],
        gdn_norm_g=pga[0], fox_norm_g=g_fox, a_log=pgg[1, 4:8], dt_bias=pgg[0, 4:8], b_f=pgg[0, 8:16], loss=pg2[3, 0:1])
    return grad_x, g_cat, g_conv8[:CONV_W], dict(zip(LATE, g_late)), small_grads


ANY = pl.BlockSpec(memory_space=pl.ANY)
CONV_PKT_ROWS = 16


def _mesh_pos():
    return lax.axis_index("x"), lax.axis_index("y"), lax.axis_index("c")


def _other_chips(x, y):
    return [(1 - x, y), (x, 1 - y), (1 - x, 1 - y)]


def _rcopy(src, dst, send_sem, recv_sem, dev):
    return pltpu.make_async_remote_copy(src_ref=src, dst_ref=dst, send_sem=send_sem, recv_sem=recv_sem,
                                        device_id=dev, device_id_type=MESH)


class _Comm:
    def __init__(self, ins, outs, aliases, n_sems, start, finish):
        self.ins, self.outs, self.aliases, self.n_sems, self.start, self.finish = list(ins), list(outs), dict(aliases), n_sems, start, finish


def _hosted(body, comm, *, name, grid, in_specs, out_specs, out_shape, args, scratch_shapes=(), aliases=None):
    n_in, n_out, n_sc = len(in_specs), len(out_specs), len(scratch_shapes)
    k, ko = (len(comm.ins), len(comm.outs)) if comm else (0, 0)

    def kernel_body(*refs):
        o0 = n_in + k
        s0 = o0 + n_out + ko
        if comm:
            cins, couts, (ssem, rsem) = refs[n_in:o0], refs[o0 + n_out:s0], refs[s0 + n_sc:]
            step = pl.program_id(0)
            for d in range(1, len(grid)):
                step = step * grid[d] + pl.program_id(d)

            @pl.when(step == 0)
            def _():
                comm.start(cins, couts, ssem, rsem)

        body(*refs[:n_in], *refs[o0:o0 + n_out], *refs[s0:s0 + n_sc])
        if comm:
            last = 1
            for n in grid:
                last *= n

            @pl.when(step == last - 1)
            def _():
                comm.finish(cins, couts, ssem, rsem)

    io_aliases = dict(aliases or {})
    scratch = list(scratch_shapes)
    if comm:
        io_aliases.update({n_in + i: n_out + j for i, j in comm.aliases.items()})
        scratch += [pltpu.SemaphoreType.DMA((comm.n_sems,)), pltpu.SemaphoreType.DMA((comm.n_sems,))]
    res = pl.pallas_call(
        kernel_body, name=name, grid=grid, in_specs=list(in_specs) + [ANY] * k, out_specs=tuple(out_specs) + (ANY,) * ko,
        out_shape=tuple(out_shape) + tuple(comm.outs if comm else ()), scratch_shapes=scratch, input_output_aliases=io_aliases,
        compiler_params=_params(("arbitrary",) * len(grid)),
    )(*args, *(comm.ins if comm else ()))
    return tuple(res[:n_out]), tuple(res[n_out:])


def _comm_only(phases, name):
    n_in = sum(len(p.ins) for p in phases)

    def body(*refs):
        n_out = sum(len(p.outs) for p in phases)
        sems = refs[n_in + n_out:]
        i0, o0 = 0, n_in
        for j, p in enumerate(phases):
            cins, couts = refs[i0:i0 + len(p.ins)], refs[o0:o0 + len(p.outs)]
            p.start(cins, couts, sems[2 * j], sems[2 * j + 1])
            p.finish(cins, couts, sems[2 * j], sems[2 * j + 1])
            i0 += len(p.ins)
            o0 += len(p.outs)

    aliases, i0, o0 = {}, 0, 0
    for p in phases:
        aliases.update({i0 + i: o0 + j for i, j in p.aliases.items()})
        i0 += len(p.ins)
        o0 += len(p.outs)
    outs = [o for p in phases for o in p.outs]
    res = pl.pallas_call(
        body, name=name, out_shape=tuple(outs), in_specs=[ANY] * n_in, out_specs=(ANY,) * len(outs), input_output_aliases=aliases,
        scratch_shapes=[pltpu.SemaphoreType.DMA((p.n_sems,)) for p in phases for _ in range(2)],
    )(*[a for p in phases for a in p.ins])
    split, o0 = [], 0
    for p in phases:
        split.append(tuple(res[o0:o0 + len(p.outs)]))
        o0 += len(p.outs)
    return split


def _like(arrays):
    return [jax.ShapeDtypeStruct(a.shape, a.dtype) for a in arrays]


def _half(ref, slot, hf, piece=(0, 1)):
    k, n = piece
    rows = ref.shape[1] // 2 // n
    return ref.at[slot, pl.ds((hf * n + k) * rows, rows)]


def _whole_halves(arrays):
    return [(i, 0, 1) for i in range(len(arrays))]


def _gather_chips(bufs, pieces=None, whole=False, base=0):
    nw = len(bufs)
    pieces = _whole_halves(bufs) if pieces is None else pieces
    part = (lambda ref, slot, c, piece: ref.at[slot]) if whole else _half

    def copies(couts):
        x, y, c = _mesh_pos()
        q = 2 * x + y
        for j, (i, k, n) in enumerate(pieces):
            for kc, chip in enumerate(_other_chips(x, y)):
                mine, theirs = part(couts[i], q, c, (k, n)), part(couts[i], 2 * chip[0] + chip[1], c, (k, n))
                yield base + j * 3 + kc, mine, theirs, (*chip, c)

    def start(cins, couts, ssem, rsem):
        for s, mine, _, dev in copies(couts):
            _rcopy(mine, mine, ssem.at[s], rsem.at[s], dev).start()

    def finish(cins, couts, ssem, rsem):
        for s, _, theirs, dev in copies(couts):
            _rcopy(theirs, theirs, ssem.at[s], rsem.at[s], dev).wait_recv()
        for s, mine, _, dev in copies(couts):
            _rcopy(mine, mine, ssem.at[s], rsem.at[s], dev).wait_send()

    return _Comm(bufs, _like(bufs), {i: i for i in range(nw)}, 3 * len(pieces), start, finish)


def _gather_pass_on(bufs, pieces=None, base=0):
    nw = len(bufs)
    pieces = _whole_halves(bufs) if pieces is None else pieces

    def copies(couts):
        x, y, c = _mesh_pos()
        for j, (i, k, n) in enumerate(pieces):
            for kc, chip in enumerate(_other_chips(x, y)):
                slot = 2 * chip[0] + chip[1]
                yield base + j * 3 + kc, _half(couts[i], slot, c, (k, n)), _half(couts[i], slot, 1 - c, (k, n)), (x, y, 1 - c)

    def start(cins, couts, ssem, rsem):
        for s, landed, _, sib in copies(couts):
            _rcopy(landed, landed, ssem.at[s], rsem.at[s], sib).start()

    def finish(cins, couts, ssem, rsem):
        for s, _, passed, sib in copies(couts):
            _rcopy(passed, passed, ssem.at[s], rsem.at[s], sib).wait_recv()
        for s, landed, _, sib in copies(couts):
            _rcopy(landed, landed, ssem.at[s], rsem.at[s], sib).wait_send()

    return _Comm(bufs, _like(bufs), {i: i for i in range(nw)}, 3 * len(pieces), start, finish)


def _gather_now(bufs, packets):
    nb = len(bufs)
    over, on, pk = _gather_chips(bufs), _gather_pass_on(bufs, base=3 * nb), _gather_chips(packets, whole=True, base=6 * nb)

    def start(cins, couts, ssem, rsem):
        over.start(cins[:nb], couts[:nb], ssem, rsem)
        pk.start(cins[nb:], couts[nb:], ssem, rsem)

    def finish(cins, couts, ssem, rsem):
        over.finish(cins[:nb], couts[:nb], ssem, rsem)
        on.start(cins[:nb], couts[:nb], ssem, rsem)
        on.finish(cins[:nb], couts[:nb], ssem, rsem)
        pk.finish(cins[nb:], couts[nb:], ssem, rsem)

    every = list(bufs) + list(packets)
    return _Comm(every, _like(every), {i: i for i in range(len(every))}, 6 * nb + 3 * len(packets), start, finish)


def _exchange_pairs(gs):
    nw = len(gs)

    def copies(cins, couts):
        x, y, c = _mesh_pos()
        for i in range(nw):
            for d in range(N_CHIPS):
                yield i * N_CHIPS + d, _half(cins[i], d, 1 - c), couts[i].at[d], (x, y, 1 - c)

    def start(cins, couts, ssem, rsem):
        for s, src, dst, sib in copies(cins, couts):
            _rcopy(src, dst, ssem.at[s], rsem.at[s], sib).start()

    def finish(cins, couts, ssem, rsem):
        for s, src, dst, sib in copies(cins, couts):
            _rcopy(src, dst, ssem.at[s], rsem.at[s], sib).wait_recv()
        for s, src, dst, sib in copies(cins, couts):
            _rcopy(src, dst, ssem.at[s], rsem.at[s], sib).wait_send()

    outs = [jax.ShapeDtypeStruct((N_CHIPS, g.shape[1] // 2, g.shape[2]), g.dtype) for g in gs]
    return _Comm(gs, outs, {}, N_CHIPS * nw, start, finish)


def _gather_packets(small):
    def peers():
        x, y, c = _mesh_pos()
        for r in range(1, 8):
            fx, fy, fc = (r >> 2) & 1, (r >> 1) & 1, r & 1
            yield r - 1, (1 - x if fx else x, 1 - y if fy else y, 1 - c if fc else c)

    def start(cins, couts, ssem, rsem):
        x, y, c = _mesh_pos()
        mine = couts[0].at[4 * x + 2 * y + c]
        for s, peer in peers():
            _rcopy(mine, mine, ssem.at[s], rsem.at[s], peer).start()

    def finish(cins, couts, ssem, rsem):
        x, y, c = _mesh_pos()
        mine = couts[0].at[4 * x + 2 * y + c]
        for s, peer in peers():
            theirs = couts[0].at[4 * peer[0] + 2 * peer[1] + peer[2]]
            _rcopy(theirs, theirs, ssem.at[s], rsem.at[s], peer).wait_recv()
        for s, peer in peers():
            _rcopy(mine, mine, ssem.at[s], rsem.at[s], peer).wait_send()

    return _Comm([small], _like([small]), {0: 0}, 7, start, finish)


def _exchange_chips(a4s, b2s, pieces=None):
    nw = len(a4s)
    pieces = _whole_halves(a4s) if pieces is None else pieces

    def copies(cins, couts):
        x, y, c = _mesh_pos()
        for j, (i, k, n) in enumerate(pieces):
            rows = a4s[i].shape[1] // n
            part = pl.ds(k * rows, rows)
            for kc, chip in enumerate(_other_chips(x, y)):
                yield j * 3 + kc, cins[i].at[2 * chip[0] + chip[1], part], couts[i].at[kc, part], (*chip, c)

    def start(cins, couts, ssem, rsem):
        for s, src, dst, dev in copies(cins, couts):
            _rcopy(src, dst, ssem.at[s], rsem.at[s], dev).start()

    def finish(cins, couts, ssem, rsem):
        for s, src, dst, dev in copies(cins, couts):
            _rcopy(src, dst, ssem.at[s], rsem.at[s], dev).wait_recv()
        for s, src, dst, dev in copies(cins, couts):
            _rcopy(src, dst, ssem.at[s], rsem.at[s], dev).wait_send()

    return _Comm(list(a4s) + list(b2s), _like(b2s), {nw + i: i for i in range(nw)}, 3 * len(pieces), start, finish)


def _landing(a4s):
    return [lax.empty((3,) + a.shape[1:], a.dtype) for a in a4s]


def _share_halves(rs):
    nw = len(rs)

    def halves(couts, i, hf):
        rows = rs[i].shape[0] // 2
        return couts[i].at[pl.ds(hf * rows, rows)]

    def start(cins, couts, ssem, rsem):
        x, y, c = _mesh_pos()
        for i in range(nw):
            _rcopy(halves(couts, i, c), halves(couts, i, c), ssem.at[i], rsem.at[i], (x, y, 1 - c)).start()

    def finish(cins, couts, ssem, rsem):
        x, y, c = _mesh_pos()
        for i in range(nw):
            _rcopy(halves(couts, i, 1 - c), halves(couts, i, 1 - c), ssem.at[i], rsem.at[i], (x, y, 1 - c)).wait_recv()
        for i in range(nw):
            _rcopy(halves(couts, i, c), halves(couts, i, c), ssem.at[i], rsem.at[i], (x, y, 1 - c)).wait_send()

    return _Comm(rs, _like(rs), {i: i for i in range(nw)}, nw, start, finish)


ADD_ROWS = 256


def _add_pair(g4, b1, qc_idx, name):
    _, half, cols = b1.shape
    rb = ADD_ROWS if half % ADD_ROWS == 0 else half
    nb = half // rb

    def body(qc_ref, g_ref, b_ref, o_ref, ob_ref):
        a = g_ref[...] + b_ref[...]
        o_ref[...] = a
        ob_ref[...] = a.astype(bf16)

    blk = (1, rb, cols)
    out = pl.BlockSpec(blk, lambda d, i, qc: (d, i, 0))
    return pl.pallas_call(
        body, name=name,
        grid_spec=pltpu.PrefetchScalarGridSpec(
            num_scalar_prefetch=1, grid=(N_CHIPS, nb),
            in_specs=[pl.BlockSpec(blk, lambda d, i, qc: (d, qc[1] * nb + i, 0)), out],
            out_specs=(out, out)),
        out_shape=(jax.ShapeDtypeStruct(b1.shape, f32), jax.ShapeDtypeStruct(b1.shape, bf16)),
        compiler_params=_params(("parallel", "parallel")),
    )(qc_idx, g4, b1)


def _add_chips(a4, b2, qc_idx, name):
    _, half, cols = a4.shape
    rb = ADD_ROWS if half % ADD_ROWS == 0 else half
    nb = half // rb

    def body(qc_ref, a_ref, b_ref, o_ref):
        o_ref[...] = ((a_ref[0] + b_ref[0].astype(f32)) + b_ref[1].astype(f32)) + b_ref[2].astype(f32)

    return pl.pallas_call(
        body, name=name,
        grid_spec=pltpu.PrefetchScalarGridSpec(
            num_scalar_prefetch=1, grid=(nb,),
            in_specs=[pl.BlockSpec((1, rb, cols), lambda i, qc: (qc[0], i, 0)), pl.BlockSpec((3, rb, cols), lambda i, qc: (0, i, 0))],
            out_specs=pl.BlockSpec((rb, cols), lambda i, qc: (qc[1] * nb + i, 0))),
        out_shape=jax.ShapeDtypeStruct((2 * half, cols), f32),
        compiler_params=_params(("parallel",)),
    )(qc_idx, a4, b2)


def _adamw_math(w, g, m, v):
    m = ADAM_B1 * m + (1.0 - ADAM_B1) * g
    v = ADAM_B2 * v + (1.0 - ADAM_B2) * (g * g)
    m_hat = m / (1.0 - ADAM_B1 ** ADAM_STEP)
    v_hat = v / (1.0 - ADAM_B2 ** ADAM_STEP)
    return -ADAM_LR * (m_hat / (jnp.sqrt(v_hat) + ADAM_EPS) + ADAM_WD * w), m, v


def _adamw(w, g, m, v, name, comm=None):
    rows = w.shape[0]
    if w.ndim == 3:
        rb = max(r for r in range(1, ADD_ROWS // 4 + 1) if rows % r == 0)
    else:
        rb = ADD_ROWS if rows % ADD_ROWS == 0 else rows

    def body(w_ref, g_ref, m_ref, v_ref, go_ref, d_ref, mo_ref, vo_ref):
        g = g_ref[...]
        go_ref[...] = g
        d_ref[...], mo_ref[...], vo_ref[...] = _adamw_math(w_ref[...], g, m_ref[...], v_ref[...])

    blk = pl.BlockSpec((rb,) + w.shape[1:], lambda i: (i,) + (0,) * (w.ndim - 1))
    return _hosted(body, comm, name=name, grid=(rows // rb,), in_specs=[blk] * 4, out_specs=(blk,) * 4,
                   out_shape=(jax.ShapeDtypeStruct(w.shape, f32),) * 4, args=(w, g, m, v))


def _small_sum_adamw(all_pkts, w, m, v):
    names = [n for n, _, _ in SMALL_LAYOUT if n in w]
    place = {n: (r0, size) for n, r0, size in SMALL_LAYOUT}
    rows_of = lambda size: -(-size // LANES)
    flat = lambda a: a.reshape(1, -1)
    k = len(names)

    def body(*refs):
        a_ref, ins = refs[0], refs[1:1 + 3 * k]
        g_ref, outs = refs[1 + 3 * k], refs[2 + 3 * k:2 + 7 * k]
        packs = refs[2 + 7 * k:]
        g = a_ref[0]
        for r in range(1, 8):
            g = g + a_ref[r]
        g_ref[...] = g
        for kind in range(3):
            packs[kind][...] = jnp.zeros_like(packs[kind])
            for j, n in enumerate(names):
                r0, size = place[n]
                for r in range(rows_of(size)):
                    width = min(LANES, size - r * LANES)
                    packs[kind][r0 + r:r0 + r + 1, 0:width] = ins[kind * k + j][:, r * LANES:r * LANES + width]
        results = (g,) + _adamw_math(packs[0][...], g, packs[1][...], packs[2][...])
        for kind, val in enumerate(results):
            for j, n in enumerate(names):
                r0, size = place[n]
                for r in range(rows_of(size)):
                    width = min(LANES, size - r * LANES)
                    outs[kind * k + j][:, r * LANES:r * LANES + width] = val[r0 + r:r0 + r + 1, 0:width]

    args = [all_pkts] + [flat(d[n]) for d in (w, m, v) for n in names]
    out_shape = [jax.ShapeDtypeStruct(all_pkts.shape[1:], f32)] + [jax.ShapeDtypeStruct((1, place[n][1]), f32) for _ in range(4) for n in names]
    res = pl.pallas_call(body, name="small_sum_adamw", out_shape=tuple(out_shape),
                         scratch_shapes=[pltpu.VMEM(all_pkts.shape[1:], f32)] * 3)(*args)
    by_kind = [{n: res[1 + kind * k + j].reshape(w[n].shape) for j, n in enumerate(names)} for kind in range(4)]
    return res[0], by_kind


SMALL_LAYOUT = (("ln_in_g", 0, 1024), ("ln_in_b", 8, 1024), ("ln1_g", 16, 1024), ("ln1_b", 24, 1024), ("b_ple_gate", 32, 1024),
                ("ln2_g", 40, 1024), ("ln2_b", 48, 1024), ("gdn_norm_g", 56, 128), ("fox_norm_g", 57, 64), ("a_log", 58, 4),
                ("dt_bias", 59, 4), ("b_f", 60, 8), ("loss", 61, 1))
SMALL_CONV_ROW = 64
SMALL_ROWS = 128


def _pack_small(vals, conv=None):
    rows = []
    nxt = 0
    for n, r0, size in SMALL_LAYOUT:
        assert r0 == nxt
        v = vals[n].reshape(-1).astype(f32) if n in vals else jnp.zeros((size,), f32)
        nrows = -(-size // LANES)
        rows.append(jnp.pad(v, (0, nrows * LANES - size)).reshape(nrows, LANES))
        nxt = r0 + nrows
    rows.append(jnp.zeros((SMALL_CONV_ROW - nxt, LANES), f32))
    conv_rows = CONV_W * GDN_QKV // LANES
    rows.append(jnp.zeros((conv_rows, LANES), f32) if conv is None else conv.reshape(conv_rows, LANES))
    rows.append(jnp.zeros((SMALL_ROWS - SMALL_CONV_ROW - conv_rows, LANES), f32))
    return jnp.concatenate(rows, axis=0)


def _unpack_small(pkt, shapes):
    out = {}
    for n, r0, size in SMALL_LAYOUT:
        if n in shapes:
            nrows = -(-size // LANES)
            out[n] = pkt[r0:r0 + nrows].reshape(-1)[:size].reshape(shapes[n])
    return out


WEIGHTS = ("ln_in_g", "ln_in_b", "w_in", "conv_w", "a_log", "dt_bias", "gdn_norm_g", "b_f", "fox_norm_g", "w_out", "ln1_g", "ln1_b",
           "w_up", "w_down", "w_ple", "w_ple_gate", "b_ple_gate", "ln2_g", "ln2_b")
SMALL_NAMES = tuple(n for n, _, _ in SMALL_LAYOUT if n != "loss")


def kernel(x, p, ln_in_g, ln_in_b, w_in, conv_w, a_log, dt_bias, gdn_norm_g, b_f, fox_norm_g, w_out, ln1_g, ln1_b, w_up, w_down, w_ple, w_ple_gate, b_ple_gate, ln2_g, ln2_b, loss_target, m_ln_in_g, m_ln_in_b, m_w_in, m_conv_w, m_a_log, m_dt_bias, m_gdn_norm_g, m_b_f, m_fox_norm_g, m_w_out, m_ln1_g, m_ln1_b, m_w_up, m_w_down, m_w_ple, m_w_ple_gate, m_b_ple_gate, m_ln2_g, m_ln2_b, v_ln_in_g, v_ln_in_b, v_w_in, v_conv_w, v_a_log, v_dt_bias, v_gdn_norm_g, v_b_f, v_fox_norm_g, v_w_out, v_ln1_g, v_ln1_b, v_w_up, v_w_down, v_w_ple, v_w_ple_gate, v_b_ple_gate, v_ln2_g, v_ln2_b):
    given = dict(locals())
    w = {n: given[n] for n in WEIGHTS}
    m = {n: given["m_" + n] for n in WEIGHTS}
    v = {n: given["v_" + n] for n in WEIGHTS}
    xi, yi, ci = _mesh_pos()
    q = 2 * xi + yi

    def slot_buffer(val, dtype, slots=N_CHIPS, slot=q, rows=None):
        rows = val.shape[0] if rows is None else rows
        return lax.dynamic_update_slice(lax.empty((slots, rows) + val.shape[1:], dtype), val.astype(dtype)[None], (slot, 0, 0))

    shard_cols = D_IN // N_CHIPS
    conv_rows = CONV_W * GDN_QKV // N_CHIPS // LANES
    conv_pkt = jnp.pad(w["conv_w"][0].reshape(-1, LANES), ((0, CONV_PKT_ROWS - conv_rows), (0, 0)))
    ln_in_out, (w_in4, conv_all) = _ln_in(x[0], _row(w["ln_in_g"]), _row(w["ln_in_b"]),
                                          _gather_now([slot_buffer(w["w_in"][0].T, bf16, rows=W_IN_ROWS)], [slot_buffer(conv_pkt, f32)]))
    conv_full = jnp.concatenate([conv_all[d, :conv_rows].reshape(CONV_W, GDN_QKV // N_CHIPS) for d in range(N_CHIPS)], axis=1)
    wi = jnp.concatenate([w_in4[d, :shard_cols] for d in range(N_CHIPS)], axis=0)
    w_cat = jnp.concatenate([wi[:OFF_BETA], wi[OFF_FOX:OFF_F], wi[OFF_BETA:OFF_FOX], wi[OFF_F:],
                             jnp.zeros((D_CAT - D_IN, D_MODEL), bf16)], axis=0)

    small = {n: w[n] for n in SMALL_NAMES}
    qc = jnp.stack([q, ci]).astype(jnp.int32)
    tail_state = {}

    def pairs_phase(gc):
        g_in = jnp.concatenate([gc[:OFF_BETA], gc[SEG_SMALL:SEG_SMALL + 8], gc[SEG_FOX:SEG_SMALL], gc[SEG_SMALL + 8:SEG_SMALL + 16]], axis=0)
        g_in4 = jnp.stack([jnp.pad(g_in[d * shard_cols:(d + 1) * shard_cols], ((0, W_IN_ROWS - shard_cols), (0, 0))) for d in range(N_CHIPS)])

        def took(moved):
            own, sent = _add_pair(g_in4, moved[0], qc, "add_pair_w_in")
            tail_state.update(own=own, sent=[sent], landing=_landing([sent]))
        return _exchange_pairs([g_in4]), took

    grad_x, _, g_conv, g_late, small_g = _device_grads(
        x[0], p[0, 0], loss_target[0], small, w_cat, conv_full, [slot_buffer(w[n][0], bf16) for n in LATE], qc, tail=(pairs_phase, None),
        ln_in_out=ln_in_out)
    packets = _gather_packets(slot_buffer(_pack_small(small_g, g_conv), f32, 8, 4 * xi + 2 * yi + ci))
    (b2,), (small_all,) = _comm_only([_exchange_chips(tail_state["sent"], tail_state["landing"]), packets], "exchange_chips_w_in")
    (g_late["w_in"],), = _comm_only([_share_halves([_add_chips(tail_state["own"], b2, qc, "add_chips_w_in")])], "share_w_in")

    grads, delta, new_m, new_v = {}, {}, {}, {}
    for n, g in g_late.items():
        if n == "w_in":
            as_stored = lambda a: jnp.transpose(a, (2, 0, 1))
            outs, _ = _adamw(as_stored(w[n]), g[:shard_cols].reshape(shard_cols, 1, D_MODEL), as_stored(m[n]), as_stored(v[n]), "adamw_" + n)
            grads[n], delta[n], new_m[n], new_v[n] = (jnp.transpose(a, (1, 2, 0)) for a in outs)
        else:
            outs, _ = _adamw(w[n][0], g, m[n][0], v[n][0], "adamw_" + n)
            grads[n], delta[n], new_m[n], new_v[n] = (a.reshape(w[n].shape) for a in outs)
    pick = lambda d: {n: d[n] for n in SMALL_NAMES}
    g_pkt, by_kind = _small_sum_adamw(small_all, pick(w), pick(m), pick(v))
    for dst, vals in zip((grads, delta, new_m, new_v), by_kind):
        dst.update(vals)
    conv_rows_all = CONV_W * GDN_QKV // LANES
    conv_g_full = g_pkt[SMALL_CONV_ROW:SMALL_CONV_ROW + conv_rows_all].reshape(CONV_W, GDN_QKV)
    conv_g = lax.dynamic_slice_in_dim(conv_g_full, q * (GDN_QKV // N_CHIPS), GDN_QKV // N_CHIPS, axis=1)
    outs, _ = _adamw(w["conv_w"][0], conv_g, m["conv_w"][0], v["conv_w"][0], "adamw_conv_w")
    grads["conv_w"], delta["conv_w"], new_m["conv_w"], new_v["conv_w"] = (a.reshape(w["conv_w"].shape) for a in outs)
    loss = g_pkt[61, 0]
    return (loss, grad_x[None], *[grads[n] for n in WEIGHTS], *[delta[n] for n in WEIGHTS],
            *[new_m[n] for n in WEIGHTS], *[new_v[n] for n in WEIGHTS])
```

```python
import functools

import jax
import jax.numpy as jnp
from jax import lax
from jax.experimental import pallas as pl
from jax.experimental.pallas import tpu as pltpu

f32 = jnp.float32
bf16 = jnp.bfloat16
HI = lax.Precision.HIGHEST
MESH = pl.DeviceIdType.MESH

D_MODEL = 1024
CHUNK = 64
GDN_HEADS = 4
GDN_DK = 128
FOX_HEADS = 8
FOX_DH = 64
CONV_W = 4
D_FF = 4096
D_PLE = 256
LN_EPS = 1e-5
NORM_EPS = 1e-6
ALPHA = 2.0 ** 0.25
GDN_QKV = 1536
OFF_Z = 1536
OFF_BETA = 2048
OFF_FOX = 2056
OFF_F = 3592
D_IN = 3600
ADAM_LR = 0.001
ADAM_B1 = 0.9
ADAM_B2 = 0.999
ADAM_EPS = 1e-08
ADAM_WD = 0.01
ADAM_STEP = 10

SEG_FOX = 2048
SEG_SMALL = 3584
D_CAT = 3840
LANES = 128
TOK_BLK = 256
FOX_BQ = 256
VMEM_LIMIT = 56 * 1024 * 1024
NEG = -1e30

N_CHIPS = 4
W_IN_ROWS = 928


def _params(sem=None, **kw):
    return pltpu.CompilerParams(dimension_semantics=sem, vmem_limit_bytes=VMEM_LIMIT, **kw)


def _sigmoid(x):
    return 1.0 / (1.0 + jnp.exp(-x))


def _softplus(x):
    return jnp.maximum(x, 0.0) + jnp.log(1.0 + jnp.exp(-jnp.abs(x)))


def _ln_fwd(x, g, b):
    mu = jnp.mean(x, -1, keepdims=True)
    xc = x - mu
    var = jnp.mean(xc * xc, -1, keepdims=True)
    rstd = lax.rsqrt(var + LN_EPS)
    xhat = xc * rstd
    return xhat * g + b, xhat, rstd


def _ln_bwd(dy, xhat, rstd, g):
    dxh = dy * g
    m1 = jnp.mean(dxh, -1, keepdims=True)
    m2 = jnp.mean(dxh * xhat, -1, keepdims=True)
    return rstd * (dxh - m1 - xhat * m2)


def _dot(a, b, prec=HI):
    return jnp.dot(a, b, precision=prec, preferred_element_type=f32)


def _dot_nt(a, b, prec=HI):
    return lax.dot_general(a, b, (((1,), (1,)), ((), ())), precision=prec, preferred_element_type=f32)


def _dot_tn(a, b, prec=HI):
    return lax.dot_general(a, b, (((0,), (0,)), ((), ())), precision=prec, preferred_element_type=f32)


def _bdot(a, b):
    return _dot(a.astype(bf16), b.astype(bf16), None)


def _bdot_nt(a, b):
    return _dot_nt(a.astype(bf16), b.astype(bf16), None)


def _bdot_tn(a, b):
    return _dot_tn(a.astype(bf16), b.astype(bf16), None)


def _lane(shape):
    return lax.broadcasted_iota(jnp.int32, shape, len(shape) - 1)


def _mm(a, b, mode, tm, tn, name, out_dtype=f32, epi=None, extra=None, shards=1, comm=None):
    if mode == "nn":
        (m, k), n = a.shape, b.shape[-1] * shards
    elif mode == "nt":
        (m, k), n = a.shape, b.shape[-2]
    else:
        (k, m), n = a.shape, b.shape[1]
    assert m % tm == 0 and n % tn == 0, (name, m, n, tm, tn)
    per = (n // shards) // tn
    assert mode == "nt" or per * tn * shards == n, (name, n, tn, shards)
    nc = 512 if tn % 512 == 0 else (256 if tn % 256 == 0 else 128)
    ks = k // shards

    def body(a_ref, b_ref, *rest):
        for n0 in range(0, tn, nc):
            if mode == "nn":
                acc = jnp.dot(a_ref[...], b_ref[:, n0:n0 + nc], preferred_element_type=f32)
            elif mode == "nt" and shards > 1:
                acc = jnp.zeros((tm, nc), f32)
                for d in range(shards):
                    acc = acc + lax.dot_general(a_ref[:, d * ks:(d + 1) * ks], b_ref[d, n0:n0 + nc, :], (((1,), (1,)), ((), ())),
                                                preferred_element_type=f32)
            elif mode == "nt":
                acc = lax.dot_general(a_ref[...], b_ref[n0:n0 + nc, :], (((1,), (1,)), ((), ())), preferred_element_type=f32)
            else:
                acc = lax.dot_general(a_ref[...], b_ref[:, n0:n0 + nc], (((0,), (0,)), ((), ())), preferred_element_type=f32)
            if epi == "relu2":
                relu_ref, act_ref = rest
                r = jnp.maximum(acc, 0.0)
                relu_ref[:, n0:n0 + nc] = r.astype(bf16)
                act_ref[:, n0:n0 + nc] = (r * r).astype(bf16)
            elif epi == "relu2_bwd":
                relu_ref, o_ref = rest
                o_ref[:, n0:n0 + nc] = (acc * (2.0 * relu_ref[:, n0:n0 + nc].astype(f32))).astype(bf16)
            else:
                (o_ref,) = rest
                o_ref[:, n0:n0 + nc] = acc.astype(out_dtype)

    if mode == "tn":
        a_spec = pl.BlockSpec((k, tm), lambda j, i: (0, i))
    else:
        a_spec = pl.BlockSpec((tm, k), lambda j, i: (i, 0))
    if mode == "nt" and shards > 1:
        b_spec = pl.BlockSpec((shards, tn, ks), lambda j, i: (0, j, 0))
    elif mode == "nt":
        b_spec = pl.BlockSpec((tn, k), lambda j, i: (j, 0))
    elif mode == "nn" and shards > 1:
        b_spec = pl.BlockSpec((None, k, tn), lambda j, i: (j // per, 0, j % per))
    else:
        b_spec = pl.BlockSpec((k, tn), lambda j, i: (0, j))
    o_spec = pl.BlockSpec((tm, tn), lambda j, i: (i, j))
    in_specs = [a_spec, b_spec]
    args = [a, b]
    if epi == "relu2":
        out_shape = (jax.ShapeDtypeStruct((m, n), bf16), jax.ShapeDtypeStruct((m, n), bf16))
        out_specs = (o_spec, o_spec)
    elif epi == "relu2_bwd":
        in_specs.append(o_spec)
        args.append(extra)
        out_shape = jax.ShapeDtypeStruct((m, n), bf16)
        out_specs = o_spec
    elif mode == "tn" and shards > 1:
        out_shape = jax.ShapeDtypeStruct((shards, m, n // shards), out_dtype)
        out_specs = pl.BlockSpec((None, tm, tn), lambda j, i: (j // per, i, j % per))
    else:
        out_shape = jax.ShapeDtypeStruct((m, n), out_dtype)
        out_specs = o_spec
    single = not isinstance(out_shape, tuple)
    res, moved = _hosted(body, comm, name=name, grid=(n // tn, m // tm), in_specs=in_specs,
                         out_specs=(out_specs,) if single else out_specs, out_shape=(out_shape,) if single else out_shape, args=args)
    res = res[0] if single else res
    return res if comm is None else (res, moved)


def _row_spec(width, col=0):
    return pl.BlockSpec((TOK_BLK, width), lambda i: (i, col))


def _vec_spec(rows, width):
    return pl.BlockSpec((rows, width), lambda i: (0, 0))


def _ln_in(x, g, b, comm=None):
    t, d = x.shape

    def body(x_ref, g_ref, b_ref, h_ref, hb_ref):
        h, _, _ = _ln_fwd(x_ref[...], g_ref[...], b_ref[...])
        h_ref[...] = h
        hb_ref[...] = h.astype(bf16)

    return _hosted(
        body, comm, name="ln_in", grid=(t // TOK_BLK,),
        in_specs=[_row_spec(d), _vec_spec(1, d), _vec_spec(1, d)],
        out_specs=(_row_spec(d), _row_spec(d)),
        out_shape=(jax.ShapeDtypeStruct((t, d), f32), jax.ShapeDtypeStruct((t, d), bf16)),
        args=(x, g, b))


def _attn_post(o_gdn, proj, o_fox, g_gdn, g_fox2, comm=None):
    t = o_gdn.shape[0]

    def body(og_ref, z_ref, of_ref, gg_ref, gf_ref, out_ref):
        for h in range(GDN_HEADS):
            sl = slice(h * LANES, (h + 1) * LANES)
            og = og_ref[:, sl]
            z = z_ref[:, sl]
            r = lax.rsqrt(jnp.mean(og * og, -1, keepdims=True) + NORM_EPS)
            out_ref[:, sl] = (og * r * gg_ref[...] * (z * _sigmoid(z))).astype(bf16)
        lo = _lane((TOK_BLK, LANES)) < FOX_DH
        for pr in range(FOX_HEADS // 2):
            sl = slice(pr * LANES, (pr + 1) * LANES)
            of = of_ref[:, sl]
            sq = of * of
            s0 = jnp.sum(jnp.where(lo, sq, 0.0), -1, keepdims=True)
            s1 = jnp.sum(jnp.where(lo, 0.0, sq), -1, keepdims=True)
            r = lax.rsqrt(jnp.where(lo, s0, s1) * (1.0 / FOX_DH) + NORM_EPS)
            out_ref[:, 512 + pr * LANES:512 + (pr + 1) * LANES] = (of * r * gf_ref[...]).astype(bf16)

    return _hosted(
        body, comm, name="attn_post", grid=(t // TOK_BLK,),
        in_specs=[_row_spec(512), _row_spec(512, OFF_Z // 512), _row_spec(512), _vec_spec(1, LANES), _vec_spec(1, LANES)],
        out_specs=(_row_spec(D_MODEL),),
        out_shape=(jax.ShapeDtypeStruct((t, D_MODEL), bf16),),
        args=(o_gdn, proj, o_fox, g_gdn, g_fox2))


def _attn_post_bwd(dattn, o_gdn, proj, o_fox, g_gdn, g_fox2):
    t = o_gdn.shape[0]

    def body(da_ref, og_ref, z_ref, of_ref, gg_ref, gf_ref, dog_ref, dz_ref, dof_ref, pg_ref):
        i = pl.program_id(0)

        @pl.when(i == 0)
        def _():
            pg_ref[...] = jnp.zeros_like(pg_ref)

        dgg = jnp.zeros((1, LANES), f32)
        for h in range(GDN_HEADS):
            sl = slice(h * LANES, (h + 1) * LANES)
            og = og_ref[:, sl]
            z = z_ref[:, sl]
            dout = da_ref[:, sl]
            g = gg_ref[...]
            r = lax.rsqrt(jnp.mean(og * og, -1, keepdims=True) + NORM_EPS)
            sg = _sigmoid(z)
            silu = z * sg
            ng = og * r * g
            dng = dout * silu
            dz_ref[:, sl] = (dout * ng * (sg * (1.0 + z * (1.0 - sg)))).astype(bf16)
            dgg = dgg + jnp.sum(dng * og * r, 0, keepdims=True)
            gd = dng * g
            dog_ref[:, sl] = r * gd - og * (r * r * r) * jnp.mean(og * gd, -1, keepdims=True)
        pg_ref[0:1, :] += dgg
        lo = _lane((TOK_BLK, LANES)) < FOX_DH
        dgf = jnp.zeros((1, LANES), f32)
        for pr in range(FOX_HEADS // 2):
            sl = slice(pr * LANES, (pr + 1) * LANES)
            of = of_ref[:, sl]
            dout = da_ref[:, 512 + pr * LANES:512 + (pr + 1) * LANES]
            g = gf_ref[...]
            sq = of * of
            s0 = jnp.sum(jnp.where(lo, sq, 0.0), -1, keepdims=True)
            s1 = jnp.sum(jnp.where(lo, 0.0, sq), -1, keepdims=True)
            r = lax.rsqrt(jnp.where(lo, s0, s1) * (1.0 / FOX_DH) + NORM_EPS)
            dgf = dgf + jnp.sum(dout * of * r, 0, keepdims=True)
            gd = dout * g
            xg = of * gd
            m0 = jnp.sum(jnp.where(lo, xg, 0.0), -1, keepdims=True)
            m1 = jnp.sum(jnp.where(lo, 0.0, xg), -1, keepdims=True)
            dof_ref[:, sl] = r * gd - of * (r * r * r) * (jnp.where(lo, m0, m1) * (1.0 / FOX_DH))
        pg_ref[1:2, :] += dgf

    return pl.pallas_call(
        body, name="attn_post_bwd", grid=(t // TOK_BLK,),
        in_specs=[_row_spec(D_MODEL), _row_spec(512), _row_spec(512, OFF_Z // 512), _row_spec(512), _vec_spec(1, LANES), _vec_spec(1, LANES)],
        out_specs=(_row_spec(512), _row_spec(512), _row_spec(512), _vec_spec(8, LANES)),
        out_shape=(jax.ShapeDtypeStruct((t, 512), f32), jax.ShapeDtypeStruct((t, 512), bf16),
                   jax.ShapeDtypeStruct((t, 512), f32), jax.ShapeDtypeStruct((8, LANES), f32)),
        compiler_params=_params(("arbitrary",)),
    )(dattn, o_gdn, proj, o_fox, g_gdn, g_fox2)


def _ln1(h0, mix, g, b, comm=None):
    t, d = h0.shape

    def body(h0_ref, mix_ref, g_ref, b_ref, h_ref, hb_ref, xh_ref, rs_ref):
        h, xhat, rstd = _ln_fwd(ALPHA * h0_ref[...] + mix_ref[...], g_ref[...], b_ref[...])
        h_ref[...] = h
        hb_ref[...] = h.astype(bf16)
        xh_ref[...] = xhat
        rs_ref[...] = jnp.broadcast_to(rstd, rs_ref.shape)

    return _hosted(
        body, comm, name="ln1", grid=(t // TOK_BLK,),
        in_specs=[_row_spec(d), _row_spec(d), _vec_spec(1, d), _vec_spec(1, d)],
        out_specs=(_row_spec(d), _row_spec(d), _row_spec(d), _row_spec(LANES)),
        out_shape=(jax.ShapeDtypeStruct((t, d), f32), jax.ShapeDtypeStruct((t, d), bf16),
                   jax.ShapeDtypeStruct((t, d), f32), jax.ShapeDtypeStruct((t, LANES), f32)),
        args=(h0, mix, g, b))


def _ln2_loss(h1, ff, pe, gp, b_gate, g, b, target):
    t, d = h1.shape

    def body(h1_ref, ff_ref, pe_ref, gp_ref, bg_ref, g_ref, b_ref, t_ref, dr_ref, drb_ref, dpe_ref, dgp_ref, pg_ref):
        i = pl.program_id(0)

        @pl.when(i == 0)
        def _():
            pg_ref[...] = jnp.zeros_like(pg_ref)

        sig = _sigmoid(gp_ref[...] + bg_ref[...])
        pe = pe_ref[...]
        r2 = ALPHA * h1_ref[...] + ff_ref[...] + pe * sig
        y, xhat, rstd = _ln_fwd(r2, g_ref[...], b_ref[...])
        err = y - t_ref[...]
        dy = err * (1.0 / d)
        dr = _ln_bwd(dy, xhat, rstd, g_ref[...])
        dr_ref[...] = dr
        drb_ref[...] = dr.astype(bf16)
        dpe_ref[...] = (dr * sig).astype(bf16)
        dgp = dr * pe * sig * (1.0 - sig)
        dgp_ref[...] = dgp.astype(bf16)
        pg_ref[0:1, :] += jnp.sum(dy * xhat, 0, keepdims=True)
        pg_ref[1:2, :] += jnp.sum(dy, 0, keepdims=True)
        pg_ref[2:3, :] += jnp.sum(dgp, 0, keepdims=True)
        pg_ref[3:4, :] += 0.5 * jnp.sum(jnp.mean(err * err, -1, keepdims=True), 0, keepdims=True)

    return pl.pallas_call(
        body, name="ln2_loss", grid=(t // TOK_BLK,),
        in_specs=[_row_spec(d)] * 4 + [_vec_spec(1, d)] * 3 + [_row_spec(d)],
        out_specs=(_row_spec(d), _row_spec(d), _row_spec(d), _row_spec(d), _vec_spec(8, d)),
        out_shape=(jax.ShapeDtypeStruct((t, d), f32), jax.ShapeDtypeStruct((t, d), bf16), jax.ShapeDtypeStruct((t, d), bf16),
                   jax.ShapeDtypeStruct((t, d), bf16), jax.ShapeDtypeStruct((8, d), f32)),
        compiler_params=_params(("arbitrary",)),
    )(h1, ff, pe, gp, b_gate, g, b, target)


def _ln1_bwd(dr2, da, db, xhat, rstd, g):
    t, d = dr2.shape

    def body(dr2_ref, da_ref, db_ref, xh_ref, rs_ref, g_ref, dr_ref, drb_ref, pg_ref):
        i = pl.program_id(0)

        @pl.when(i == 0)
        def _():
            pg_ref[...] = jnp.zeros_like(pg_ref)

        dh = ALPHA * dr2_ref[...] + da_ref[...] + db_ref[...]
        xhat = xh_ref[...]
        dr = _ln_bwd(dh, xhat, rs_ref[:, 0:1], g_ref[...])
        dr_ref[...] = dr
        drb_ref[...] = dr.astype(bf16)
        pg_ref[0:1, :] += jnp.sum(dh * xhat, 0, keepdims=True)
        pg_ref[1:2, :] += jnp.sum(dh, 0, keepdims=True)

    return pl.pallas_call(
        body, name="ln1_bwd", grid=(t // TOK_BLK,),
        in_specs=[_row_spec(d)] * 4 + [_row_spec(LANES), _vec_spec(1, d)],
        out_specs=(_row_spec(d), _row_spec(d), _vec_spec(8, d)),
        out_shape=(jax.ShapeDtypeStruct((t, d), f32), jax.ShapeDtypeStruct((t, d), bf16), jax.ShapeDtypeStruct((8, d), f32)),
        compiler_params=_params(("arbitrary",)),
    )(dr2, da, db, xhat, rstd, g)


def _ln_in_bwd(x, dr1, dmm, g, comm=None):
    t, d = x.shape

    def body(x_ref, dr1_ref, dmm_ref, g_ref, dx_ref, pg_ref):
        i = pl.program_id(0)

        @pl.when(i == 0)
        def _():
            pg_ref[...] = jnp.zeros_like(pg_ref)

        dh = ALPHA * dr1_ref[...] + dmm_ref[...]
        _, xhat, rstd = _ln_fwd(x_ref[...], g_ref[...], 0.0)
        dx_ref[...] = _ln_bwd(dh, xhat, rstd, g_ref[...])
        pg_ref[0:1, :] += jnp.sum(dh * xhat, 0, keepdims=True)
        pg_ref[1:2, :] += jnp.sum(dh, 0, keepdims=True)

    return _hosted(
        body, comm, name="ln_in_bwd", grid=(t // TOK_BLK,),
        in_specs=[_row_spec(d)] * 3 + [_vec_spec(1, d)],
        out_specs=(_row_spec(d), _vec_spec(8, d)),
        out_shape=(jax.ShapeDtypeStruct((t, d), f32), jax.ShapeDtypeStruct((8, d), f32)),
        args=(x, dr1, dmm, g))


def _tri(n, upper=False, strict=False):
    r = lax.broadcasted_iota(jnp.int32, (n, n), 0)
    c = lax.broadcasted_iota(jnp.int32, (n, n), 1)
    if upper:
        m = (c > r) if strict else (c >= r)
    else:
        m = (c < r) if strict else (c <= r)
    return jnp.where(m, 1.0, 0.0).astype(f32)


def _gate_values(x, bias, alog, lane):
    z = x + bias
    return jnp.where(lane < 4, _sigmoid(z), jnp.where(lane < 8, -jnp.exp(alog) * _softplus(z), jnp.where(lane < 16, -_softplus(-z), 0.0)))


def _gates(proj, bias_row, alog_row):
    t = proj.shape[0]
    nch = t // CHUNK

    def body(x_ref, bias_ref, alog_ref, gates_ref, gcum_ref, gcumt_ref):
        lane = _lane((t, LANES))
        gates = _gate_values(x_ref[...], bias_ref[...], alog_ref[...], lane)
        gates_ref[...] = gates
        g3 = gates.reshape(nch, CHUNK, LANES)
        tri = jnp.broadcast_to(_tri(CHUNK)[None], (nch, CHUNK, CHUNK))
        loc = jnp.einsum("bij,bjk->bik", tri, g3, precision=HI, preferred_element_type=f32)
        tot = jnp.sum(g3, axis=1)
        offs = _dot(_tri(nch, strict=True), tot)
        glob = loc + offs[:, None, :]
        lane3 = _lane((nch, CHUNK, LANES))
        gcum = jnp.where(lane3 < 4, g3, jnp.where(lane3 < 8, loc, glob)).reshape(t, LANES)
        gcum_ref[...] = gcum
        gcumt_ref[...] = gcum.T

    return pl.pallas_call(
        body, name="gates", grid=(1,),
        in_specs=[pl.BlockSpec((t, LANES), lambda i: (0, SEG_SMALL // LANES)), _vec_spec(1, LANES), _vec_spec(1, LANES)],
        out_specs=(pl.BlockSpec((t, LANES), lambda i: (0, 0)), pl.BlockSpec((t, LANES), lambda i: (0, 0)),
                   pl.BlockSpec((LANES, t), lambda i: (0, 0))),
        out_shape=(jax.ShapeDtypeStruct((t, LANES), f32), jax.ShapeDtypeStruct((t, LANES), f32), jax.ShapeDtypeStruct((LANES, t), f32)),
        compiler_params=_params(("arbitrary",)),
    )(proj, bias_row, alog_row)


def _gates_bwd(proj, bias_row, alog_row, gates, dgates, dccol, dct):
    t = proj.shape[0]
    nch = t // CHUNK

    def body(x_ref, bias_ref, alog_ref, gates_ref, dg_ref, dcc_ref, dct_ref, dx_ref, pg_ref):
        lane = _lane((t, LANES))
        d = dg_ref[...] + dcc_ref[...] + dct_ref[...].T
        d3 = d.reshape(nch, CHUNK, LANES)
        tri = jnp.broadcast_to(_tri(CHUNK, upper=True)[None], (nch, CHUNK, CHUNK))
        loc = jnp.einsum("bij,bjk->bik", tri, d3, precision=HI, preferred_element_type=f32)
        tot = jnp.sum(d3, axis=1)
        offs = _dot(_tri(nch, upper=True, strict=True), tot)
        glob = loc + offs[:, None, :]
        lane3 = _lane((nch, CHUNK, LANES))
        dpre = jnp.where(lane3 < 4, d3, jnp.where(lane3 < 8, loc, glob)).reshape(t, LANES)
        z = x_ref[...] + bias_ref[...]
        sg = _sigmoid(z)
        dx = jnp.where(lane < 4, dpre * sg * (1.0 - sg),
                       jnp.where(lane < 8, dpre * (-jnp.exp(alog_ref[...])) * sg, jnp.where(lane < 16, dpre * (1.0 - sg), 0.0)))
        dx_ref[...] = dx.astype(bf16)
        pg_ref[...] = jnp.zeros_like(pg_ref)
        pg_ref[0:1, :] = jnp.sum(dx, 0, keepdims=True)
        pg_ref[1:2, :] = jnp.sum(jnp.where((lane >= 4) & (lane < 8), dpre * gates_ref[...], 0.0), 0, keepdims=True)

    full = pl.BlockSpec((t, LANES), lambda i: (0, 0))
    return pl.pallas_call(
        body, name="gates_bwd", grid=(1,),
        in_specs=[pl.BlockSpec((t, LANES), lambda i: (0, SEG_SMALL // LANES)), _vec_spec(1, LANES), _vec_spec(1, LANES),
                  full, full, full, pl.BlockSpec((LANES, t), lambda i: (0, 0))],
        out_specs=(full, _vec_spec(8, LANES)),
        out_shape=(jax.ShapeDtypeStruct((t, LANES), bf16), jax.ShapeDtypeStruct((8, LANES), f32)),
        compiler_params=_params(("arbitrary",)),
    )(proj, bias_row, alog_row, gates, dgates, dccol, dct)


def _conv_act(u, cw, row, t):
    c = cw[3:4, :] * u
    for jj in range(CONV_W - 1):
        sh = CONV_W - 1 - jj
        c = c + cw[jj:jj + 1, :] * jnp.where(row >= sh, pltpu.roll(u, sh, axis=0), 0.0)
    return c


def _gdn_conv(proj, conv_w, comm=None):
    t = proj.shape[0]
    nblk = GDN_QKV // LANES

    def body(u_ref, cw_ref, c_ref, y_ref):
        j = pl.program_id(0)
        row = lax.broadcasted_iota(jnp.int32, (t, LANES), 0)
        c = _conv_act(u_ref[...], cw_ref[...], row, t)
        c_ref[...] = c
        s = c * _sigmoid(c)
        r = lax.rsqrt(jnp.sum(s * s, -1, keepdims=True) + NORM_EPS)
        scale = jnp.where(j < GDN_HEADS, GDN_DK ** -0.5, 1.0)
        y_ref[...] = jnp.where(j < 2 * GDN_HEADS, s * (r * scale), s)

    blk = pl.BlockSpec((t, LANES), lambda j: (0, j))
    return _hosted(
        body, comm, name="gdn_conv", grid=(nblk,),
        in_specs=[blk, pl.BlockSpec((CONV_W, LANES), lambda j: (0, j))],
        out_specs=(blk, blk),
        out_shape=(jax.ShapeDtypeStruct((t, GDN_QKV), f32), jax.ShapeDtypeStruct((t, GDN_QKV), f32)),
        args=(proj, conv_w))


def _gdn_conv_bwd(proj, conv_w, c, dy, comm=None):
    t = proj.shape[0]
    nblk = GDN_QKV // LANES

    def body(u_ref, cw_ref, c_ref, dy_ref, du_ref, dcw_ref):
        j = pl.program_id(0)
        row = lax.broadcasted_iota(jnp.int32, (t, LANES), 0)
        u = u_ref[...]
        cw = cw_ref[...]
        c = c_ref[...]
        dy = dy_ref[...]
        sg = _sigmoid(c)
        s = c * sg
        r = lax.rsqrt(jnp.sum(s * s, -1, keepdims=True) + NORM_EPS)
        n = s * r
        scale = jnp.where(j < GDN_HEADS, GDN_DK ** -0.5, 1.0)
        dn = dy * scale
        ds = jnp.where(j < 2 * GDN_HEADS, r * (dn - n * jnp.sum(dn * n, -1, keepdims=True)), dy)
        dc = ds * (sg * (1.0 + c * (1.0 - sg)))
        du = cw[3:4, :] * dc
        dcw_ref[...] = jnp.zeros_like(dcw_ref)
        dcw_ref[3:4, :] = jnp.sum(dc * u, 0, keepdims=True)
        for jj in range(CONV_W - 1):
            sh = CONV_W - 1 - jj
            du = du + cw[jj:jj + 1, :] * jnp.where(row < t - sh, pltpu.roll(dc, t - sh, axis=0), 0.0)
            dcw_ref[jj:jj + 1, :] = jnp.sum(dc * jnp.where(row >= sh, pltpu.roll(u, sh, axis=0), 0.0), 0, keepdims=True)
        du_ref[...] = du.astype(bf16)

    blk = pl.BlockSpec((t, LANES), lambda j: (0, j))
    return _hosted(
        body, comm, name="gdn_conv_bwd", grid=(nblk,),
        in_specs=[blk, pl.BlockSpec((CONV_W, LANES), lambda j: (0, j)), blk, blk],
        out_specs=(blk, pl.BlockSpec((8, LANES), lambda j: (0, j))),
        out_shape=(jax.ShapeDtypeStruct((t, GDN_QKV), bf16), jax.ShapeDtypeStruct((8, GDN_QKV), f32)),
        args=(proj, conv_w, c, dy))


def _chunk_masks():
    r = lax.broadcasted_iota(jnp.int32, (CHUNK, CHUNK), 0)
    c = lax.broadcasted_iota(jnp.int32, (CHUNK, CHUNK), 1)
    return r >= c, r > c, r == c


def _col_to_row(col, eye):
    return jnp.sum(jnp.where(eye, col, 0.0), axis=0, keepdims=True)


def _row_to_col(row, eye):
    return jnp.sum(jnp.where(eye, row, 0.0), axis=1, keepdims=True)


NN = (((1,), (0,)), ((), ()))
NT = (((1,), (1,)), ((), ()))
TN = (((0,), (0,)), ((), ()))
GDN_GROUP = 4


def _mx(a, b, dims=NN, passes=1):
    d = lambda p, q: lax.dot_general(p, q, dims, preferred_element_type=f32)
    ah, bh = a.astype(bf16), b.astype(bf16)
    if passes == 1:
        return d(ah, bh)
    al = (a - ah.astype(f32)).astype(bf16)
    bl = (b - bh.astype(f32)).astype(bf16)
    return d(ah, bh) + (d(ah, bl) + d(al, bh))


def _gdn_decay(gam, masks):
    causal, _, eye = masks
    return jnp.exp(jnp.where(causal, gam - _col_to_row(gam, eye), NEG))


def _gdn_local(y, gcum, comm=None):
    t = y.shape[0]
    nch = t // CHUNK
    rows_blk = GDN_GROUP * CHUNK

    def body(y_ref, g_ref, u_ref, w_ref, qk_ref, tinv_ref):
        masks = _chunk_masks()
        _, strict, eye = masks
        ids = [(j, h) for j in range(GDN_GROUP) for h in range(GDN_HEADS)]
        rs = lambda j: slice(j * CHUNK, (j + 1) * CHUNK)
        col = lambda base, h: slice(base + h * LANES, base + (h + 1) * LANES)
        kn = [y_ref[rs(j), col(512, h)] for j, h in ids]
        beta = [g_ref[rs(j), h:h + 1] for j, h in ids]
        gam = [g_ref[rs(j), 4 + h:5 + h] for j, h in ids]
        dec = [_gdn_decay(g, masks) for g in gam]
        x = [-jnp.where(strict, _mx(k, k, NT) * d * b, 0.0) for k, d, b in zip(kn, dec, beta)]
        tinv = [jnp.where(eye, 1.0, 0.0) + a for a in x]
        for _ in range(5):
            x = [_mx(a, a, NN, 3) for a in x]
            tinv = [t_ + _mx(t_, a, NN, 3) for t_, a in zip(tinv, x)]
        for (j, h), t_, k, d, b, g in zip(ids, tinv, kn, dec, beta, gam):
            u_ref[rs(j), col(0, h)] = _mx(t_, b * y_ref[rs(j), col(1024, h)])
            w_ref[rs(j), col(0, h)] = _mx(t_, (b * jnp.exp(g)) * k)
            qk_ref[j, h] = _mx(y_ref[rs(j), col(0, h)], k, NT) * d
            tinv_ref[j, h] = t_

    mat = pl.BlockSpec((GDN_GROUP, GDN_HEADS, CHUNK, CHUNK), lambda n: (n, 0, 0, 0))
    return _hosted(
        body, comm, name="gdn_local", grid=(nch // GDN_GROUP,),
        in_specs=[pl.BlockSpec((rows_blk, GDN_QKV), lambda n: (n, 0)), pl.BlockSpec((rows_blk, LANES), lambda n: (n, 0))],
        out_specs=(pl.BlockSpec((rows_blk, 512), lambda n: (n, 0)), pl.BlockSpec((rows_blk, 512), lambda n: (n, 0)), mat, mat),
        out_shape=(jax.ShapeDtypeStruct((t, 512), f32), jax.ShapeDtypeStruct((t, 512), f32),
                   jax.ShapeDtypeStruct((nch, GDN_HEADS, CHUNK, CHUNK), f32), jax.ShapeDtypeStruct((nch, GDN_HEADS, CHUNK, CHUNK), f32)),
        args=(y, gcum))


def _gdn_fwd(y, gcum, u, w, qk, comm=None):
    t = y.shape[0]
    nch = t // CHUNK

    def body(y_ref, g_ref, u_ref, w_ref, qk_ref, o_ref, sall_ref, s_ref):
        @pl.when(pl.program_id(0) == 0)
        def _():
            s_ref[...] = jnp.zeros_like(s_ref)

        heads = range(GDN_HEADS)
        sl = [slice(h * LANES, (h + 1) * LANES) for h in heads]
        gam = [g_ref[:, 4 + h:5 + h] for h in heads]
        gam_last = [g[CHUNK - 1:CHUNK, :] for g in gam]
        s = [s_ref[h] for h in heads]
        for h in heads:
            sall_ref[0, h] = s[h]
        ws = [_mx(w_ref[:, sl[h]], s[h]) for h in heads]
        qs = [_mx(y_ref[:, sl[h]] * jnp.exp(gam[h]), s[h]) for h in heads]
        vn = [u_ref[:, sl[h]] - ws[h] for h in heads]
        av = [_mx(qk_ref[0, h], vn[h]) for h in heads]
        kv = [_mx(y_ref[:, 512 + h * LANES:512 + (h + 1) * LANES] * jnp.exp(gam_last[h] - gam[h]), vn[h], TN) for h in heads]
        for h in heads:
            o_ref[:, sl[h]] = qs[h] + av[h]
            s_ref[h] = jnp.exp(gam_last[h]) * s[h] + kv[h]

    row = lambda width: pl.BlockSpec((CHUNK, width), lambda n: (n, 0))
    return _hosted(
        body, comm, name="gdn_fwd", grid=(nch,),
        in_specs=[row(GDN_QKV), row(LANES), row(512), row(512), pl.BlockSpec((1, GDN_HEADS, CHUNK, CHUNK), lambda n: (n, 0, 0, 0))],
        out_specs=(row(512), pl.BlockSpec((1, GDN_HEADS, LANES, LANES), lambda n: (n, 0, 0, 0))),
        out_shape=(jax.ShapeDtypeStruct((t, 512), f32), jax.ShapeDtypeStruct((nch, GDN_HEADS, LANES, LANES), f32)),
        scratch_shapes=[pltpu.VMEM((GDN_HEADS, LANES, LANES), f32)],
        args=(y, gcum, u, w, qk))


def _gdn_bwd(y, gcum, u_all, w_all, qk_all, tinv_all, sall, do, comm=None):
    t = y.shape[0]
    nch = t // CHUNK

    def body(y_ref, g_ref, u_ref, w_ref, qk_ref, tinv_ref, sall_ref, do_ref, dy_ref, dg_ref, ds_ref):
        @pl.when(pl.program_id(0) == 0)
        def _():
            ds_ref[...] = jnp.zeros_like(ds_ref)

        masks = _chunk_masks()
        causal, strict, eye = masks
        lane = _lane((CHUNK, LANES))
        row = lax.broadcasted_iota(jnp.int32, (CHUNK, 1), 0)
        heads = range(GDN_HEADS)
        each = lambda f, *ls: [f(*a) for a in zip(*ls)]
        rsum = lambda a: jnp.sum(a, axis=1, keepdims=True)
        sl = [slice(h * LANES, (h + 1) * LANES) for h in heads]
        qn = [y_ref[:, sl[h]] for h in heads]
        kn = [y_ref[:, 512 + h * LANES:512 + (h + 1) * LANES] for h in heads]
        v = [y_ref[:, 1024 + h * LANES:1024 + (h + 1) * LANES] for h in heads]
        beta = [g_ref[:, h:h + 1] for h in heads]
        gam = [g_ref[:, 4 + h:5 + h] for h in heads]
        gam_last = [g[CHUNK - 1:CHUNK, :] for g in gam]
        dec = [_gdn_decay(g, masks) for g in gam]
        e = [jnp.exp(g) for g in gam]
        f = each(lambda gl_, g: jnp.exp(gl_ - g), gam_last, gam)
        gl = [jnp.exp(g) for g in gam_last]
        u = [u_ref[:, sl[h]] for h in heads]
        w = [w_ref[:, sl[h]] for h in heads]
        qk = [qk_ref[0, h] for h in heads]
        tinv = [tinv_ref[0, h] for h in heads]
        s = [sall_ref[0, h] for h in heads]
        dsn = [ds_ref[h] for h in heads]
        d_o = [do_ref[:, sl[h]] for h in heads]
        qd = each(lambda a, b: a * b, qn, e)
        kd = each(lambda a, b: a * b, kn, f)
        ws = each(_mx, w, s)
        kds = each(_mx, kd, dsn)
        qkdo = each(lambda a, b: _mx(a, b, TN), qk, d_o)
        dqd = each(lambda a, b: _mx(a, b, NT), d_o, s)
        qddo = each(lambda a, b: _mx(a, b, TN), qd, d_o)
        kkd = each(lambda k, d: _mx(k, k, NT) * d, kn, dec)
        vn = each(lambda a, b: a - b, u, ws)
        dvn = each(lambda a, b: a + b, qkdo, kds)
        dqk = each(lambda a, b: jnp.where(causal, _mx(a, b, NT), 0.0), d_o, vn)
        dkd = each(lambda a, b: _mx(a, b, NT), vn, dsn)
        dw = each(lambda a, b: -_mx(a, b, NT), dvn, s)
        wdvn = each(lambda a, b: _mx(a, b, TN), w, dvn)
        dgl = each(lambda a, b: jnp.sum(rsum(a * b), axis=0, keepdims=True), dsn, s)
        for h in heads:
            ds_ref[h] = qddo[h] - wdvn[h] + gl[h] * dsn[h]
        dru = each(lambda a, b: _mx(a, b, TN), tinv, dvn)
        drw = each(lambda a, b: _mx(a, b, TN), tinv, dw)
        dqkr = each(lambda a, b: a * b, dqk, dec)
        dq1 = each(_mx, dqkr, kn)
        dk1 = each(lambda a, b: _mx(a, b, TN), dqkr, qn)
        dnu = each(lambda a, b: _mx(a, b, NT), dru, u)
        dnw = each(lambda a, b: _mx(a, b, NT), drw, w)
        dn = each(lambda a, b: jnp.where(strict, -(a + b), 0.0), dnu, dnw)
        dkk = each(lambda a, b, d: a * b * d, dn, beta, dec)
        dk2 = each(_mx, dkk, kn)
        dk3 = each(lambda a, b: _mx(a, b, TN), dkk, kn)
        dgates = jnp.zeros((CHUNK, LANES), f32)
        for h in heads:
            drw_k = rsum(drw[h] * kn[h])
            dbeta = rsum(dru[h] * v[h]) + e[h] * drw_k + rsum(dn[h] * kkd[h])
            m = dn[h] * (kkd[h] * beta[h]) + dqk[h] * qk[h]
            de = beta[h] * drw_k + rsum(dqd[h] * qn[h])
            df = rsum(dkd[h] * kn[h])
            dgam = rsum(m) - _row_to_col(jnp.sum(m, axis=0, keepdims=True), eye) + de * e[h] - df * f[h]
            dgam_last = jnp.sum(df * f[h], axis=0, keepdims=True) + dgl[h] * gl[h]
            dgam = dgam + jnp.where(row == CHUNK - 1, dgam_last, 0.0)
            dy_ref[:, sl[h]] = dq1[h] + dqd[h] * e[h]
            dy_ref[:, 512 + h * LANES:512 + (h + 1) * LANES] = (beta[h] * e[h]) * drw[h] + dk2[h] + dk3[h] + dk1[h] + dkd[h] * f[h]
            dy_ref[:, 1024 + h * LANES:1024 + (h + 1) * LANES] = beta[h] * dru[h]
            dgates = dgates + jnp.where(lane == h, dbeta, 0.0) + jnp.where(lane == 4 + h, dgam, 0.0)
        dg_ref[...] = dgates

    rev = lambda width: pl.BlockSpec((CHUNK, width), lambda n: (nch - 1 - n, 0))
    mat = lambda d: pl.BlockSpec((1, GDN_HEADS, d, d), lambda n: (nch - 1 - n, 0, 0, 0))
    return _hosted(
        body, comm, name="gdn_bwd", grid=(nch,),
        in_specs=[rev(GDN_QKV), rev(LANES), rev(512), rev(512), mat(CHUNK), mat(CHUNK), mat(LANES), rev(512)],
        out_specs=(rev(GDN_QKV), rev(LANES)),
        out_shape=(jax.ShapeDtypeStruct((t, GDN_QKV), f32), jax.ShapeDtypeStruct((t, LANES), f32)),
        scratch_shapes=[pltpu.VMEM((GDN_HEADS, LANES, LANES), f32)],
        args=(y, gcum, u_all, w_all, qk_all, tinv_all, sall, do))


FOX_CLASSES = 4


def _fox_groups(t):
    nq = t // FOX_BQ
    ncls = min(FOX_CLASSES, nq)
    per = nq // ncls
    return [(g * per, per, (g + 1) * per * FOX_BQ) for g in range(ncls)]


def _fox_scores(q_ref, k_ref, gcum_ref, gcumt_ref, h, i, keys):
    pr = h // 2
    lo = (h % 2) * FOX_DH
    lane = _lane((FOX_BQ, LANES))
    mask = (lane >= lo) & (lane < lo + FOX_DH)
    qm = jnp.where(mask, q_ref[:, pr * LANES:(pr + 1) * LANES], 0.0).astype(bf16)
    kp = k_ref[:, pr * LANES:(pr + 1) * LANES].astype(bf16)
    s = _dot_nt(qm, kp, None) * (FOX_DH ** -0.5)
    s = s + gcum_ref[:, 8 + h:9 + h] - gcumt_ref[8 + h:9 + h, :]
    rows = i * FOX_BQ + lax.broadcasted_iota(jnp.int32, (FOX_BQ, keys), 0)
    cols = lax.broadcasted_iota(jnp.int32, (FOX_BQ, keys), 1)
    return jnp.where(cols <= rows, s, NEG), mask, qm, kp


def _fox_fwd(proj, gcum, gcumt, ride=None):
    c0 = SEG_FOX // 512

    def group_call(q0, nq, keys, comm):
        def body(q_ref, k_ref, v_ref, gcum_ref, gcumt_ref, o_ref, lse_ref):
            i = q0 + pl.program_id(0)
            lane = _lane((FOX_BQ, LANES))
            lse_all = jnp.zeros((FOX_BQ, LANES), f32)
            for pr in range(FOX_HEADS // 2):
                vp = v_ref[:, pr * LANES:(pr + 1) * LANES].astype(bf16)
                o_pair = jnp.zeros((FOX_BQ, LANES), f32)
                for h in (2 * pr, 2 * pr + 1):
                    s, mask, _, _ = _fox_scores(q_ref, k_ref, gcum_ref, gcumt_ref, h, i, keys)
                    m = jnp.max(s, axis=1, keepdims=True)
                    p = jnp.exp(s - m)
                    l = jnp.sum(p, axis=1, keepdims=True)
                    o_h = _dot((p * (1.0 / l)).astype(bf16), vp, None)
                    o_pair = jnp.where(mask, o_h, o_pair)
                    lse_all = jnp.where(lane == h, m + jnp.log(l), lse_all)
                o_ref[:, pr * LANES:(pr + 1) * LANES] = o_pair
            lse_ref[...] = lse_all

        seen = lambda col: pl.BlockSpec((keys, 512), lambda i: (0, col))
        return _hosted(
            body, comm, name=f"fox_fwd_{keys}", grid=(nq,),
            in_specs=[pl.BlockSpec((FOX_BQ, 512), lambda i: (q0 + i, c0)), seen(c0 + 1), seen(c0 + 2),
                      pl.BlockSpec((FOX_BQ, LANES), lambda i: (q0 + i, 0)), pl.BlockSpec((LANES, keys), lambda i: (0, 0))],
            out_specs=(pl.BlockSpec((FOX_BQ, 512), lambda i: (i, 0)), pl.BlockSpec((FOX_BQ, LANES), lambda i: (i, 0))),
            out_shape=(jax.ShapeDtypeStruct((nq * FOX_BQ, 512), f32), jax.ShapeDtypeStruct((nq * FOX_BQ, LANES), f32)),
            args=(proj, proj, proj, gcum, gcumt))

    parts = []
    for n, g in enumerate(_fox_groups(proj.shape[0])):
        hook = ride(n) if ride else None
        part, moved = group_call(*g, hook[0] if hook else None)
        parts.append(part)
        if hook:
            hook[1](moved)
    return jnp.concatenate([o for o, _ in parts], axis=0), jnp.concatenate([l for _, l in parts], axis=0)


def _fox_bwd(proj, gcum, gcumt, o, lse, do, ride=None):
    t = proj.shape[0]
    c0 = SEG_FOX // 512

    def group_call(q0, nq, keys, acc, comm):
        first = acc is None

        def body(q_ref, k_ref, v_ref, gcum_ref, gcumt_ref, o_ref, lse_ref, do_ref, *rest):
            dq_ref, dk_ref, dv_ref, dcc_ref, dct_ref = rest[-5:]
            j = pl.program_id(0)
            i = q0 + j

            @pl.when(j == 0)
            def _():
                if first:
                    dk_ref[...] = jnp.zeros_like(dk_ref)
                    dv_ref[...] = jnp.zeros_like(dv_ref)
                    dct_ref[...] = jnp.zeros_like(dct_ref)
                else:
                    dk_ref[...], dv_ref[...], dct_ref[...] = rest[0][...], rest[1][...], rest[2][...]

            lane = _lane((FOX_BQ, LANES))
            dcc = jnp.zeros((FOX_BQ, LANES), f32)
            scale = FOX_DH ** -0.5
            for pr in range(FOX_HEADS // 2):
                sl = slice(pr * LANES, (pr + 1) * LANES)
                vp = v_ref[:, sl].astype(bf16)
                dq_pair = jnp.zeros((FOX_BQ, LANES), f32)
                for h in (2 * pr, 2 * pr + 1):
                    s, mask, qm, kp = _fox_scores(q_ref, k_ref, gcum_ref, gcumt_ref, h, i, keys)
                    p = jnp.exp(s - lse_ref[:, h:h + 1])
                    dom = jnp.where(mask, do_ref[:, sl], 0.0)
                    delta = jnp.sum(dom * o_ref[:, sl], axis=1, keepdims=True)
                    domb = dom.astype(bf16)
                    ds = p * (_dot_nt(domb, vp, None) - delta)
                    dsb = ds.astype(bf16)
                    dv_ref[:, sl] += _dot_tn(p.astype(bf16), domb, None)
                    dk_ref[:, sl] += _dot_tn(dsb, qm, None) * scale
                    dq_pair = jnp.where(mask, _dot(dsb, kp, None) * scale, dq_pair)
                    dcc = jnp.where(lane == 8 + h, jnp.sum(ds, axis=1, keepdims=True), dcc)
                    dct_ref[8 + h:9 + h, :] += -jnp.sum(ds, axis=0, keepdims=True)
                dq_ref[:, sl] = dq_pair.astype(bf16)
            dcc_ref[...] = dcc

        qblk = lambda col: pl.BlockSpec((FOX_BQ, 512), lambda i: (q0 + i, col))
        oblk = pl.BlockSpec((FOX_BQ, 512), lambda i: (i, 0))
        seen = lambda col: pl.BlockSpec((keys, 512), lambda i: (0, col))
        rblk = pl.BlockSpec((FOX_BQ, LANES), lambda i: (q0 + i, 0))
        seen_t = pl.BlockSpec((LANES, keys), lambda i: (0, 0))
        in_specs = [qblk(c0), seen(c0 + 1), seen(c0 + 2), rblk, seen_t, qblk(0), rblk, qblk(0)]
        args = [proj, proj, proj, gcum, gcumt, o, lse, do]
        aliases = {}
        if not first:
            in_specs += [seen(0), seen(0), seen_t]
            args += list(acc)
            aliases = {8: 1, 9: 2, 10: 4}
        return _hosted(
            body, comm, name=f"fox_bwd_{keys}", grid=(nq,), in_specs=in_specs,
            out_specs=(oblk, seen(0), seen(0), pl.BlockSpec((FOX_BQ, LANES), lambda i: (i, 0)), seen_t),
            out_shape=(jax.ShapeDtypeStruct((nq * FOX_BQ, 512), bf16), jax.ShapeDtypeStruct((t, 512), f32), jax.ShapeDtypeStruct((t, 512), f32),
                       jax.ShapeDtypeStruct((nq * FOX_BQ, LANES), f32), jax.ShapeDtypeStruct((LANES, t), f32)),
            aliases=aliases, args=args)

    acc, dqs, dccs = None, [], []
    for n, g in enumerate(reversed(_fox_groups(t))):
        hook = ride(n) if ride else None
        (dq, dk, dv, dcc, dct), moved = group_call(*g, acc, hook[0] if hook else None)
        if hook:
            hook[1](moved)
        acc = (dk, dv, dct)
        dqs.insert(0, dq)
        dccs.insert(0, dcc)
    return jnp.concatenate(dqs, axis=0), acc[0], acc[1], jnp.concatenate(dccs, axis=0), acc[2]


def _row(v, width=None):
    v = v.reshape(1, -1).astype(f32)
    if width is not None and v.shape[1] < width:
        v = jnp.pad(v, ((0, 0), (0, width - v.shape[1])))
    return v


LATE = ("w_out", "w_up", "w_ple_gate", "w_ple", "w_down")


def _device_grads(x, p, target, small, w_cat, conv_w, late, qc=None, tail=None, ln_in_out=None):
    z4 = jnp.zeros((4,), f32)
    bias_row = _row(jnp.concatenate([z4, small["dt_bias"].reshape(-1), small["b_f"].reshape(-1)]), LANES)
    alog_row = _row(jnp.concatenate([z4, small["a_log"].reshape(-1)]), LANES)
    g_gdn = _row(small["gdn_norm_g"])
    g_fox2 = _row(jnp.tile(small["fox_norm_g"].reshape(-1), 2))
    pb = p.astype(bf16)
    late = list(late)
    comm = qc is not None

    h0, h0b = ln_in_out if ln_in_out is not None else _ln_in(x, _row(small["ln_in_g"]), _row(small["ln_in_b"]))[0]
    proj = _mm(h0b, w_cat, "nt", 256, D_CAT, "mm_proj")
    gates, gcum, gcumt = _gates(proj, bias_row, alog_row)
    w_down_pieces = [(4, 0, 1)]

    def gather(phase, pieces):
        if not comm or not pieces:
            return None, lambda moved: None
        touched = sorted({i for i, _, _ in pieces})

        def took(moved):
            for i, buf in zip(touched, moved):
                late[i] = buf
        return phase([late[i] for i in touched], [(touched.index(i), k, n) for i, k, n in pieces]), took

    over, on = _gather_chips, _gather_pass_on
    cm, took = gather(over, [(0, 0, 1), (3, 0, 1)])
    (conv_c, qkv_n), moved = _gdn_conv(proj, conv_w, cm)
    took(moved)
    cm, took = gather(over, [(1, 0, 2)])
    (gu, gw, gqk, gtinv), moved = _gdn_local(qkv_n, gcum, cm)
    took(moved)
    cm, took = gather(over, [(1, 1, 2)])
    (o_gdn, sall), moved = _gdn_fwd(qkv_n, gcum, gu, gw, gqk, cm)
    took(moved)
    fox_plan = [(over, []), (on, [(0, 0, 1), (3, 0, 1), (1, 0, 2), (1, 1, 2)]), (over, [(2, 0, 1)]), (over, [(4, 0, 2)])]
    assert not comm or len(_fox_groups(x.shape[0])) == len(fox_plan)
    o_fox, lse = _fox_fwd(proj, gcum, gcumt, (lambda n: gather(*fox_plan[n])) if comm else None)
    cm, took = gather(on, [(2, 0, 1)])
    (attn,), moved = _attn_post(o_gdn, proj, o_fox, g_gdn, g_fox2, cm)
    took(moved)
    w_out = late[0].reshape(D_MODEL, D_MODEL)
    mix = _mm(attn, w_out, "nn", 512, D_MODEL, "mm_mix")
    (h1, h1b, xhat1, rstd1), _ = _ln1(h0, mix, _row(small["ln1_g"]), _row(small["ln1_b"]))
    w_up, w_ple = late[1], late[3]
    cm, took = gather(over, [(4, 1, 2)])
    up_act = _mm(h1b, w_up, "nn", 512, 1024, "mm_up", epi="relu2", shards=N_CHIPS, comm=cm)
    if cm:
        up_act, moved = up_act
        took(moved)
    up, act = up_act
    w_gate = late[2].reshape(D_MODEL, D_MODEL)
    cm, took = gather(on, w_down_pieces)
    gp = _mm(h1b, w_gate, "nn", 512, D_MODEL, "mm_gate", comm=cm)
    if cm:
        gp, moved = gp
        took(moved)
    pe = _mm(pb, w_ple, "nn", 512, D_MODEL // N_CHIPS, "mm_ple", shards=N_CHIPS)
    w_down = late[4].reshape(D_FF, D_MODEL)
    ff = _mm(act, w_down, "nn", 256, D_MODEL, "mm_down")
    dr2, dr2b, dpe, dgp, pg2 = _ln2_loss(h1, ff, pe, gp, _row(small["b_ple_gate"]), _row(small["ln2_g"]), _row(small["ln2_b"]), target)

    dup = _mm(dr2b, w_down, "nt", 256, 2048, "mm_dact", epi="relu2_bwd", extra=up)
    g_down = _mm(act, dr2b, "tn", 1024, D_MODEL, "mm_gdown")
    dh1_a = _mm(dup, w_up, "nt", 256, D_MODEL, "mm_dh1a", shards=N_CHIPS)
    g_up = _mm(h1b, dup, "tn", 1024, 1024, "mm_gup", shards=N_CHIPS)
    dh1_b = _mm(dgp, w_gate, "nt", 512, D_MODEL, "mm_dh1b")
    g_gate = _mm(h1b, dgp, "tn", 1024, D_MODEL, "mm_ggate")
    g_ple = _mm(pb, dpe, "tn", D_PLE, D_MODEL // N_CHIPS, "mm_gple", shards=N_CHIPS)
    dr1, dr1b, pg1 = _ln1_bwd(dr2, dh1_a, dh1_b, xhat1, rstd1, _row(small["ln1_g"]))
    dattn = _mm(dr1b, w_out, "nt", 512, D_MODEL, "mm_dattn")
    g_out = _mm(attn, dr1b, "tn", 1024, D_MODEL, "mm_gout")
    do_gdn, dz, do_fox, pga = _attn_post_bwd(dattn, o_gdn, proj, o_fox, g_gdn, g_fox2)
    g_late = [g.reshape((N_CHIPS, -1, g.shape[-1])) for g in (g_out, g_up, g_gate, g_ple, g_down)]
    chip_plan = [[(4, 0, 2), (4, 1, 2), (0, 0, 1)], [(1, 0, 2)], [(1, 1, 2)], [(2, 0, 1), (3, 0, 1)]]
    state = {}

    def to_sibling():
        def took(moved):
            sums = [_add_pair(g, b1, qc, "add_pair_" + n) for g, b1, n in zip(g_late, moved, LATE)]
            state.update(own=[a for a, _ in sums], sent=[ab for _, ab in sums], landing=_landing([ab for _, ab in sums]))
        return _exchange_pairs(g_late), took

    def to_chips(pieces):
        if not comm:
            return None, lambda moved: None
        return _exchange_chips(state["sent"], state["landing"], pieces), lambda moved: state.update(landing=list(moved))

    def gdn_backward():
        cm, took = to_chips(chip_plan[0])
        state["gdn"], moved = _gdn_bwd(qkv_n, gcum, gu, gw, gqk, gtinv, sall, do_gdn, cm)
        took(moved)

    def fox_ride(n):
        if n == 0:
            return to_sibling()
        if n == 1:
            gdn_backward()
        return to_chips(chip_plan[n])

    assert not comm or len(_fox_groups(x.shape[0])) == len(chip_plan)
    dfq, dfk, dfv, dccol, dct = _fox_bwd(proj, gcum, gcumt, o_fox, lse, do_fox, fox_ride if comm else None)
    if not comm:
        gdn_backward()
    dqkv_n, dgates = state["gdn"]
    dsmall, pgg = _gates_bwd(proj, bias_row, alog_row, gates, dgates, dccol, dct)
    cm = None
    if comm:
        cm = _share_halves([_add_chips(a, b2, qc, "add_chips_" + n) for a, b2, n in zip(state["own"], state["landing"], LATE)])
    (du, g_conv8), reduced = _gdn_conv_bwd(proj, conv_w, conv_c, dqkv_n, cm)
    if comm:
        g_late = list(reduced)
    t = x.shape[0]
    dproj = jnp.concatenate([du, dz, dfq, dfk.astype(bf16), dfv.astype(bf16), dsmall, jnp.zeros((t, D_CAT - SEG_SMALL - LANES), bf16)], axis=1)
    g_cat = _mm(dproj, h0b, "tn", 1280, D_MODEL, "mm_gcat")
    cm, took = tail[0](g_cat) if tail else (None, None)
    dh0_mm = _mm(dproj, w_cat, "nn", 256, D_MODEL, "mm_dh0", comm=cm)
    if cm:
        dh0_mm, moved = dh0_mm
        took(moved)
    cm, took = tail[1]() if tail and tail[1] else (None, None)
    (grad_x, pg0), moved = _ln_in_bwd(x, dr1, dh0_mm, _row(small["ln_in_g"]), cm)
    if cm:
        took(moved)

    g_fox = pga[1, :FOX_DH] + pga[1, FOX_DH:]
    small_grads = dict(
        ln_in_g=pg0[0], ln_in_b=pg0[1], ln1_g=pg1[0], ln1_b=pg1[1], b_ple_gate=pg2[2], ln2_g=pg2[0], ln2_b=pg2[1],
        gdn_norm_g=pga[0], fox_norm_g=g_fox, a_log=pgg[1, 4:8], dt_bias=pgg[0, 4:8], b_f=pgg[0, 8:16], loss=pg2[3, 0:1])
    return grad_x, g_cat, g_conv8[:CONV_W], dict(zip(LATE, g_late)), small_grads


ANY = pl.BlockSpec(memory_space=pl.ANY)
CONV_PKT_ROWS = 16


def _mesh_pos():
    return lax.axis_index("x"), lax.axis_index("y"), lax.axis_index("c")


def _other_chips(x, y):
    return [(1 - x, y), (x, 1 - y), (1 - x, 1 - y)]


def _rcopy(src, dst, send_sem, recv_sem, dev):
    return pltpu.make_async_remote_copy(src_ref=src, dst_ref=dst, send_sem=send_sem, recv_sem=recv_sem,
                                        device_id=dev, device_id_type=MESH)


class _Comm:
    def __init__(self, ins, outs, aliases, n_sems, start, finish):
        self.ins, self.outs, self.aliases, self.n_sems, self.start, self.finish = list(ins), list(outs), dict(aliases), n_sems, start, finish


def _hosted(body, comm, *, name, grid, in_specs, out_specs, out_shape, args, scratch_shapes=(), aliases=None):
    n_in, n_out, n_sc = len(in_specs), len(out_specs), len(scratch_shapes)
    k, ko = (len(comm.ins), len(comm.outs)) if comm else (0, 0)

    def kernel_body(*refs):
        o0 = n_in + k
        s0 = o0 + n_out + ko
        if comm:
            cins, couts, (ssem, rsem) = refs[n_in:o0], refs[o0 + n_out:s0], refs[s0 + n_sc:]
            step = pl.program_id(0)
            for d in range(1, len(grid)):
                step = step * grid[d] + pl.program_id(d)

            @pl.when(step == 0)
            def _():
                comm.start(cins, couts, ssem, rsem)

        body(*refs[:n_in], *refs[o0:o0 + n_out], *refs[s0:s0 + n_sc])
        if comm:
            last = 1
            for n in grid:
                last *= n

            @pl.when(step == last - 1)
            def _():
                comm.finish(cins, couts, ssem, rsem)

    io_aliases = dict(aliases or {})
    scratch = list(scratch_shapes)
    if comm:
        io_aliases.update({n_in + i: n_out + j for i, j in comm.aliases.items()})
        scratch += [pltpu.SemaphoreType.DMA((comm.n_sems,)), pltpu.SemaphoreType.DMA((comm.n_sems,))]
    res = pl.pallas_call(
        kernel_body, name=name, grid=grid, in_specs=list(in_specs) + [ANY] * k, out_specs=tuple(out_specs) + (ANY,) * ko,
        out_shape=tuple(out_shape) + tuple(comm.outs if comm else ()), scratch_shapes=scratch, input_output_aliases=io_aliases,
        compiler_params=_params(("arbitrary",) * len(grid)),
    )(*args, *(comm.ins if comm else ()))
    return tuple(res[:n_out]), tuple(res[n_out:])


def _comm_only(phases, name):
    n_in = sum(len(p.ins) for p in phases)

    def body(*refs):
        n_out = sum(len(p.outs) for p in phases)
        sems = refs[n_in + n_out:]
        i0, o0 = 0, n_in
        for j, p in enumerate(phases):
            cins, couts = refs[i0:i0 + len(p.ins)], refs[o0:o0 + len(p.outs)]
            p.start(cins, couts, sems[2 * j], sems[2 * j + 1])
            p.finish(cins, couts, sems[2 * j], sems[2 * j + 1])
            i0 += len(p.ins)
            o0 += len(p.outs)

    aliases, i0, o0 = {}, 0, 0
    for p in phases:
        aliases.update({i0 + i: o0 + j for i, j in p.aliases.items()})
        i0 += len(p.ins)
        o0 += len(p.outs)
    outs = [o for p in phases for o in p.outs]
    res = pl.pallas_call(
        body, name=name, out_shape=tuple(outs), in_specs=[ANY] * n_in, out_specs=(ANY,) * len(outs), input_output_aliases=aliases,
        scratch_shapes=[pltpu.SemaphoreType.DMA((p.n_sems,)) for p in phases for _ in range(2)],
    )(*[a for p in phases for a in p.ins])
    split, o0 = [], 0
    for p in phases:
        split.append(tuple(res[o0:o0 + len(p.outs)]))
        o0 += len(p.outs)
    return split


def _like(arrays):
    return [jax.ShapeDtypeStruct(a.shape, a.dtype) for a in arrays]


def _half(ref, slot, hf, piece=(0, 1)):
    k, n = piece
    rows = ref.shape[1] // 2 // n
    return ref.at[slot, pl.ds((hf * n + k) * rows, rows)]


def _whole_halves(arrays):
    return [(i, 0, 1) for i in range(len(arrays))]


def _gather_chips(bufs, pieces=None, whole=False, base=0):
    nw = len(bufs)
    pieces = _whole_halves(bufs) if pieces is None else pieces
    part = (lambda ref, slot, c, piece: ref.at[slot]) if whole else _half

    def copies(couts):
        x, y, c = _mesh_pos()
        q = 2 * x + y
        for j, (i, k, n) in enumerate(pieces):
            for kc, chip in enumerate(_other_chips(x, y)):
                mine, theirs = part(couts[i], q, c, (k, n)), part(couts[i], 2 * chip[0] + chip[1], c, (k, n))
                yield base + j * 3 + kc, mine, theirs, (*chip, c)

    def start(cins, couts, ssem, rsem):
        for s, mine, _, dev in copies(couts):
            _rcopy(mine, mine, ssem.at[s], rsem.at[s], dev).start()

    def finish(cins, couts, ssem, rsem):
        for s, _, theirs, dev in copies(couts):
            _rcopy(theirs, theirs, ssem.at[s], rsem.at[s], dev).wait_recv()
        for s, mine, _, dev in copies(couts):
            _rcopy(mine, mine, ssem.at[s], rsem.at[s], dev).wait_send()

    return _Comm(bufs, _like(bufs), {i: i for i in range(nw)}, 3 * len(pieces), start, finish)


def _gather_pass_on(bufs, pieces=None, base=0):
    nw = len(bufs)
    pieces = _whole_halves(bufs) if pieces is None else pieces

    def copies(couts):
        x, y, c = _mesh_pos()
        for j, (i, k, n) in enumerate(pieces):
            for kc, chip in enumerate(_other_chips(x, y)):
                slot = 2 * chip[0] + chip[1]
                yield base + j * 3 + kc, _half(couts[i], slot, c, (k, n)), _half(couts[i], slot, 1 - c, (k, n)), (x, y, 1 - c)

    def start(cins, couts, ssem, rsem):
        for s, landed, _, sib in copies(couts):
            _rcopy(landed, landed, ssem.at[s], rsem.at[s], sib).start()

    def finish(cins, couts, ssem, rsem):
        for s, _, passed, sib in copies(couts):
            _rcopy(passed, passed, ssem.at[s], rsem.at[s], sib).wait_recv()
        for s, landed, _, sib in copies(couts):
            _rcopy(landed, landed, ssem.at[s], rsem.at[s], sib).wait_send()

    return _Comm(bufs, _like(bufs), {i: i for i in range(nw)}, 3 * len(pieces), start, finish)


def _gather_now(bufs, packets):
    nb = len(bufs)
    over, on, pk = _gather_chips(bufs), _gather_pass_on(bufs, base=3 * nb), _gather_chips(packets, whole=True, base=6 * nb)

    def start(cins, couts, ssem, rsem):
        over.start(cins[:nb], couts[:nb], ssem, rsem)
        pk.start(cins[nb:], couts[nb:], ssem, rsem)

    def finish(cins, couts, ssem, rsem):
        over.finish(cins[:nb], couts[:nb], ssem, rsem)
        on.start(cins[:nb], couts[:nb], ssem, rsem)
        on.finish(cins[:nb], couts[:nb], ssem, rsem)
        pk.finish(cins[nb:], couts[nb:], ssem, rsem)

    every = list(bufs) + list(packets)
    return _Comm(every, _like(every), {i: i for i in range(len(every))}, 6 * nb + 3 * len(packets), start, finish)


def _exchange_pairs(gs):
    nw = len(gs)

    def copies(cins, couts):
        x, y, c = _mesh_pos()
        for i in range(nw):
            for d in range(N_CHIPS):
                yield i * N_CHIPS + d, _half(cins[i], d, 1 - c), couts[i].at[d], (x, y, 1 - c)

    def start(cins, couts, ssem, rsem):
        for s, src, dst, sib in copies(cins, couts):
            _rcopy(src, dst, ssem.at[s], rsem.at[s], sib).start()

    def finish(cins, couts, ssem, rsem):
        for s, src, dst, sib in copies(cins, couts):
            _rcopy(src, dst, ssem.at[s], rsem.at[s], sib).wait_recv()
        for s, src, dst, sib in copies(cins, couts):
            _rcopy(src, dst, ssem.at[s], rsem.at[s], sib).wait_send()

    outs = [jax.ShapeDtypeStruct((N_CHIPS, g.shape[1] // 2, g.shape[2]), g.dtype) for g in gs]
    return _Comm(gs, outs, {}, N_CHIPS * nw, start, finish)


def _gather_packets(small):
    def peers():
        x, y, c = _mesh_pos()
        for r in range(1, 8):
            fx, fy, fc = (r >> 2) & 1, (r >> 1) & 1, r & 1
            yield r - 1, (1 - x if fx else x, 1 - y if fy else y, 1 - c if fc else c)

    def start(cins, couts, ssem, rsem):
        x, y, c = _mesh_pos()
        mine = couts[0].at[4 * x + 2 * y + c]
        for s, peer in peers():
            _rcopy(mine, mine, ssem.at[s], rsem.at[s], peer).start()

    def finish(cins, couts, ssem, rsem):
        x, y, c = _mesh_pos()
        mine = couts[0].at[4 * x + 2 * y + c]
        for s, peer in peers():
            theirs = couts[0].at[4 * peer[0] + 2 * peer[1] + peer[2]]
            _rcopy(theirs, theirs, ssem.at[s], rsem.at[s], peer).wait_recv()
        for s, peer in peers():
            _rcopy(mine, mine, ssem.at[s], rsem.at[s], peer).wait_send()

    return _Comm([small], _like([small]), {0: 0}, 7, start, finish)


def _exchange_chips(a4s, b2s, pieces=None):
    nw = len(a4s)
    pieces = _whole_halves(a4s) if pieces is None else pieces

    def copies(cins, couts):
        x, y, c = _mesh_pos()
        for j, (i, k, n) in enumerate(pieces):
            rows = a4s[i].shape[1] // n
            part = pl.ds(k * rows, rows)
            for kc, chip in enumerate(_other_chips(x, y)):
                yield j * 3 + kc, cins[i].at[2 * chip[0] + chip[1], part], couts[i].at[kc, part], (*chip, c)

    def start(cins, couts, ssem, rsem):
        for s, src, dst, dev in copies(cins, couts):
            _rcopy(src, dst, ssem.at[s], rsem.at[s], dev).start()

    def finish(cins, couts, ssem, rsem):
        for s, src, dst, dev in copies(cins, couts):
            _rcopy(src, dst, ssem.at[s], rsem.at[s], dev).wait_recv()
        for s, src, dst, dev in copies(cins, couts):
            _rcopy(src, dst, ssem.at[s], rsem.at[s], dev).wait_send()

    return _Comm(list(a4s) + list(b2s), _like(b2s), {nw + i: i for i in range(nw)}, 3 * len(pieces), start, finish)


def _landing(a4s):
    return [lax.empty((3,) + a.shape[1:], a.dtype) for a in a4s]


def _share_halves(rs):
    nw = len(rs)

    def halves(couts, i, hf):
        rows = rs[i].shape[0] // 2
        return couts[i].at[pl.ds(hf * rows, rows)]

    def start(cins, couts, ssem, rsem):
        x, y, c = _mesh_pos()
        for i in range(nw):
            _rcopy(halves(couts, i, c), halves(couts, i, c), ssem.at[i], rsem.at[i], (x, y, 1 - c)).start()

    def finish(cins, couts, ssem, rsem):
        x, y, c = _mesh_pos()
        for i in range(nw):
            _rcopy(halves(couts, i, 1 - c), halves(couts, i, 1 - c), ssem.at[i], rsem.at[i], (x, y, 1 - c)).wait_recv()
        for i in range(nw):
            _rcopy(halves(couts, i, c), halves(couts, i, c), ssem.at[i], rsem.at[i], (x, y, 1 - c)).wait_send()

    return _Comm(rs, _like(rs), {i: i for i in range(nw)}, nw, start, finish)


ADD_ROWS = 256


def _add_pair(g4, b1, qc_idx, name):
    _, half, cols = b1.shape
    rb = ADD_ROWS if half % ADD_ROWS == 0 else half
    nb = half // rb

    def body(qc_ref, g_ref, b_ref, o_ref, ob_ref):
        a = g_ref[...] + b_ref[...]
        o_ref[...] = a
        ob_ref[...] = a.astype(bf16)

    blk = (1, rb, cols)
    out = pl.BlockSpec(blk, lambda d, i, qc: (d, i, 0))
    return pl.pallas_call(
        body, name=name,
        grid_spec=pltpu.PrefetchScalarGridSpec(
            num_scalar_prefetch=1, grid=(N_CHIPS, nb),
            in_specs=[pl.BlockSpec(blk, lambda d, i, qc: (d, qc[1] * nb + i, 0)), out],
            out_specs=(out, out)),
        out_shape=(jax.ShapeDtypeStruct(b1.shape, f32), jax.ShapeDtypeStruct(b1.shape, bf16)),
        compiler_params=_params(("parallel", "parallel")),
    )(qc_idx, g4, b1)


def _add_chips(a4, b2, qc_idx, name):
    _, half, cols = a4.shape
    rb = ADD_ROWS if half % ADD_ROWS == 0 else half
    nb = half // rb

    def body(qc_ref, a_ref, b_ref, o_ref):
        o_ref[...] = ((a_ref[0] + b_ref[0].astype(f32)) + b_ref[1].astype(f32)) + b_ref[2].astype(f32)

    return pl.pallas_call(
        body, name=name,
        grid_spec=pltpu.PrefetchScalarGridSpec(
            num_scalar_prefetch=1, grid=(nb,),
            in_specs=[pl.BlockSpec((1, rb, cols), lambda i, qc: (qc[0], i, 0)), pl.BlockSpec((3, rb, cols), lambda i, qc: (0, i, 0))],
            out_specs=pl.BlockSpec((rb, cols), lambda i, qc: (qc[1] * nb + i, 0))),
        out_shape=jax.ShapeDtypeStruct((2 * half, cols), f32),
        compiler_params=_params(("parallel",)),
    )(qc_idx, a4, b2)


def _adamw_math(w, g, m, v):
    m = ADAM_B1 * m + (1.0 - ADAM_B1) * g
    v = ADAM_B2 * v + (1.0 - ADAM_B2) * (g * g)
    m_hat = m / (1.0 - ADAM_B1 ** ADAM_STEP)
    v_hat = v / (1.0 - ADAM_B2 ** ADAM_STEP)
    return -ADAM_LR * (m_hat / (jnp.sqrt(v_hat) + ADAM_EPS) + ADAM_WD * w), m, v


def _adamw(w, g, m, v, name, comm=None):
    rows = w.shape[0]
    if w.ndim == 3:
        rb = max(r for r in range(1, ADD_ROWS // 4 + 1) if rows % r == 0)
    else:
        rb = ADD_ROWS if rows % ADD_ROWS == 0 else rows

    def body(w_ref, g_ref, m_ref, v_ref, go_ref, d_ref, mo_ref, vo_ref):
        g = g_ref[...]
        go_ref[...] = g
        d_ref[...], mo_ref[...], vo_ref[...] = _adamw_math(w_ref[...], g, m_ref[...], v_ref[...])

    blk = pl.BlockSpec((rb,) + w.shape[1:], lambda i: (i,) + (0,) * (w.ndim - 1))
    return _hosted(body, comm, name=name, grid=(rows // rb,), in_specs=[blk] * 4, out_specs=(blk,) * 4,
                   out_shape=(jax.ShapeDtypeStruct(w.shape, f32),) * 4, args=(w, g, m, v))


def _small_sum_adamw(all_pkts, w, m, v):
    names = [n for n, _, _ in SMALL_LAYOUT if n in w]
    place = {n: (r0, size) for n, r0, size in SMALL_LAYOUT}
    rows_of = lambda size: -(-size // LANES)
    flat = lambda a: a.reshape(1, -1)
    k = len(names)

    def body(*refs):
        a_ref, ins = refs[0], refs[1:1 + 3 * k]
        g_ref, outs = refs[1 + 3 * k], refs[2 + 3 * k:2 + 7 * k]
        packs = refs[2 + 7 * k:]
        g = a_ref[0]
        for r in range(1, 8):
            g = g + a_ref[r]
        g_ref[...] = g
        for kind in range(3):
            packs[kind][...] = jnp.zeros_like(packs[kind])
            for j, n in enumerate(names):
                r0, size = place[n]
                for r in range(rows_of(size)):
                    width = min(LANES, size - r * LANES)
                    packs[kind][r0 + r:r0 + r + 1, 0:width] = ins[kind * k + j][:, r * LANES:r * LANES + width]
        results = (g,) + _adamw_math(packs[0][...], g, packs[1][...], packs[2][...])
        for kind, val in enumerate(results):
            for j, n in enumerate(names):
                r0, size = place[n]
                for r in range(rows_of(size)):
                    width = min(LANES, size - r * LANES)
                    outs[kind * k + j][:, r * LANES:r * LANES + width] = val[r0 + r:r0 + r + 1, 0:width]

    args = [all_pkts] + [flat(d[n]) for d in (w, m, v) for n in names]
    out_shape = [jax.ShapeDtypeStruct(all_pkts.shape[1:], f32)] + [jax.ShapeDtypeStruct((1, place[n][1]), f32) for _ in range(4) for n in names]
    res = pl.pallas_call(body, name="small_sum_adamw", out_shape=tuple(out_shape),
                         scratch_shapes=[pltpu.VMEM(all_pkts.shape[1:], f32)] * 3)(*args)
    by_kind = [{n: res[1 + kind * k + j].reshape(w[n].shape) for j, n in enumerate(names)} for kind in range(4)]
    return res[0], by_kind


SMALL_LAYOUT = (("ln_in_g", 0, 1024), ("ln_in_b", 8, 1024), ("ln1_g", 16, 1024), ("ln1_b", 24, 1024), ("b_ple_gate", 32, 1024),
                ("ln2_g", 40, 1024), ("ln2_b", 48, 1024), ("gdn_norm_g", 56, 128), ("fox_norm_g", 57, 64), ("a_log", 58, 4),
                ("dt_bias", 59, 4), ("b_f", 60, 8), ("loss", 61, 1))
SMALL_CONV_ROW = 64
SMALL_ROWS = 128


def _pack_small(vals, conv=None):
    rows = []
    nxt = 0
    for n, r0, size in SMALL_LAYOUT:
        assert r0 == nxt
        v = vals[n].reshape(-1).astype(f32) if n in vals else jnp.zeros((size,), f32)
        nrows = -(-size // LANES)
        rows.append(jnp.pad(v, (0, nrows * LANES - size)).reshape(nrows, LANES))
        nxt = r0 + nrows
    rows.append(jnp.zeros((SMALL_CONV_ROW - nxt, LANES), f32))
    conv_rows = CONV_W * GDN_QKV // LANES
    rows.append(jnp.zeros((conv_rows, LANES), f32) if conv is None else conv.reshape(conv_rows, LANES))
    rows.append(jnp.zeros((SMALL_ROWS - SMALL_CONV_ROW - conv_rows, LANES), f32))
    return jnp.concatenate(rows, axis=0)


def _unpack_small(pkt, shapes):
    out = {}
    for n, r0, size in SMALL_LAYOUT:
        if n in shapes:
            nrows = -(-size // LANES)
            out[n] = pkt[r0:r0 + nrows].reshape(-1)[:size].reshape(shapes[n])
    return out


WEIGHTS = ("ln_in_g", "ln_in_b", "w_in", "conv_w", "a_log", "dt_bias", "gdn_norm_g", "b_f", "fox_norm_g", "w_out", "ln1_g", "ln1_b",
           "w_up", "w_down", "w_ple", "w_ple_gate", "b_ple_gate", "ln2_g", "ln2_b")
SMALL_NAMES = tuple(n for n, _, _ in SMALL_LAYOUT if n != "loss")


def kernel(x, p, ln_in_g, ln_in_b, w_in, conv_w, a_log, dt_bias, gdn_norm_g, b_f, fox_norm_g, w_out, ln1_g, ln1_b, w_up, w_down, w_ple, w_ple_gate, b_ple_gate, ln2_g, ln2_b, loss_target, m_ln_in_g, m_ln_in_b, m_w_in, m_conv_w, m_a_log, m_dt_bias, m_gdn_norm_g, m_b_f, m_fox_norm_g, m_w_out, m_ln1_g, m_ln1_b, m_w_up, m_w_down, m_w_ple, m_w_ple_gate, m_b_ple_gate, m_ln2_g, m_ln2_b, v_ln_in_g, v_ln_in_b, v_w_in, v_conv_w, v_a_log, v_dt_bias, v_gdn_norm_g, v_b_f, v_fox_norm_g, v_w_out, v_ln1_g, v_ln1_b, v_w_up, v_w_down, v_w_ple, v_w_ple_gate, v_b_ple_gate, v_ln2_g, v_ln2_b):
    given = dict(locals())
    w = {n: given[n] for n in WEIGHTS}
    m = {n: given["m_" + n] for n in WEIGHTS}
    v = {n: given["v_" + n] for n in WEIGHTS}
    xi, yi, ci = _mesh_pos()
    q = 2 * xi + yi

    def slot_buffer(val, dtype, slots=N_CHIPS, slot=q, rows=None):
        rows = val.shape[0] if rows is None else rows
        return lax.dynamic_update_slice(lax.empty((slots, rows) + val.shape[1:], dtype), val.astype(dtype)[None], (slot, 0, 0))

    shard_cols = D_IN // N_CHIPS
    conv_rows = CONV_W * GDN_QKV // N_CHIPS // LANES
    conv_pkt = jnp.pad(w["conv_w"][0].reshape(-1, LANES), ((0, CONV_PKT_ROWS - conv_rows), (0, 0)))
    ln_in_out, (w_in4, conv_all) = _ln_in(x[0], _row(w["ln_in_g"]), _row(w["ln_in_b"]),
                                          _gather_now([slot_buffer(w["w_in"][0].T, bf16, rows=W_IN_ROWS)], [slot_buffer(conv_pkt, f32)]))
    conv_full = jnp.concatenate([conv_all[d, :conv_rows].reshape(CONV_W, GDN_QKV // N_CHIPS) for d in range(N_CHIPS)], axis=1)
    wi = jnp.concatenate([w_in4[d, :shard_cols] for d in range(N_CHIPS)], axis=0)
    w_cat = jnp.concatenate([wi[:OFF_BETA], wi[OFF_FOX:OFF_F], wi[OFF_BETA:OFF_FOX], wi[OFF_F:],
                             jnp.zeros((D_CAT - D_IN, D_MODEL), bf16)], axis=0)

    small = {n: w[n] for n in SMALL_NAMES}
    qc = jnp.stack([q, ci]).astype(jnp.int32)
    tail_state = {}

    def pairs_phase(gc):
        g_in = jnp.concatenate([gc[:OFF_BETA], gc[SEG_SMALL:SEG_SMALL + 8], gc[SEG_FOX:SEG_SMALL], gc[SEG_SMALL + 8:SEG_SMALL + 16]], axis=0)
        g_in4 = jnp.stack([jnp.pad(g_in[d * shard_cols:(d + 1) * shard_cols], ((0, W_IN_ROWS - shard_cols), (0, 0))) for d in range(N_CHIPS)])

        def took(moved):
            own, sent = _add_pair(g_in4, moved[0], qc, "add_pair_w_in")
            tail_state.update(own=own, sent=[sent], landing=_landing([sent]))
        return _exchange_pairs([g_in4]), took

    grad_x, _, g_conv, g_late, small_g = _device_grads(
        x[0], p[0, 0], loss_target[0], small, w_cat, conv_full, [slot_buffer(w[n][0], bf16) for n in LATE], qc, tail=(pairs_phase, None),
        ln_in_out=ln_in_out)
    packets = _gather_packets(slot_buffer(_pack_small(small_g, g_conv), f32, 8, 4 * xi + 2 * yi + ci))
    (b2,), (small_all,) = _comm_only([_exchange_chips(tail_state["sent"], tail_state["landing"]), packets], "exchange_chips_w_in")
    (g_late["w_in"],), = _comm_only([_share_halves([_add_chips(tail_state["own"], b2, qc, "add_chips_w_in")])], "share_w_in")

    grads, delta, new_m, new_v = {}, {}, {}, {}
    for n, g in g_late.items():
        if n == "w_in":
            as_stored = lambda a: jnp.transpose(a, (2, 0, 1))
            outs, _ = _adamw(as_stored(w[n]), g[:shard_cols].reshape(shard_cols, 1, D_MODEL), as_stored(m[n]), as_stored(v[n]), "adamw_" + n)
            grads[n], delta[n], new_m[n], new_v[n] = (jnp.transpose(a, (1, 2, 0)) for a in outs)
        else:
            outs, _ = _adamw(w[n][0], g, m[n][0], v[n][0], "adamw_" + n)
            grads[n], delta[n], new_m[n], new_v[n] = (a.reshape(w[n].shape) for a in outs)
    pick = lambda d: {n: d[n] for n in SMALL_NAMES}
    g_pkt, by_kind = _small_sum_adamw(small_all, pick(w), pick(m), pick(v))
    for dst, vals in zip((grads, delta, new_m, new_v), by_kind):
        dst.update(vals)
    conv_rows_all = CONV_W * GDN_QKV // LANES
    conv_g_full = g_pkt[SMALL_CONV_ROW:SMALL_CONV_ROW + conv_rows_all].reshape(CONV_W, GDN_QKV)
    conv_g = lax.dynamic_slice_in_dim(conv_g_full, q * (GDN_QKV // N_CHIPS), GDN_QKV // N_CHIPS, axis=1)
    outs, _ = _adamw(w["conv_w"][0], conv_g, m["conv_w"][0], v["conv_w"][0], "adamw_conv_w")
    grads["conv_w"], delta["conv_w"], new_m["conv_w"], new_v["conv_w"] = (a.reshape(w["conv_w"].shape) for a in outs)
    loss = g_pkt[61, 0]
    return (loss, grad_x[None], *[grads[n] for n in WEIGHTS], *[delta[n] for n in WEIGHTS],
            *[new_m[n] for n in WEIGHTS], *[new_v[n] for n in WEIGHTS])
```

```python
import functools

import jax
import jax.numpy as jnp
from jax import lax
from jax.experimental import pallas as pl
from jax.experimental.pallas import tpu as pltpu

f32 = jnp.float32
bf16 = jnp.bfloat16
HI = lax.Precision.HIGHEST
MESH = pl.DeviceIdType.MESH

D_MODEL = 1024
CHUNK = 64
GDN_HEADS = 4
GDN_DK = 128
FOX_HEADS = 8
FOX_DH = 64
CONV_W = 4
D_FF = 4096
D_PLE = 256
LN_EPS = 1e-5
NORM_EPS = 1e-6
ALPHA = 2.0 ** 0.25
GDN_QKV = 1536
OFF_Z = 1536
OFF_BETA = 2048
OFF_FOX = 2056
OFF_F = 3592
D_IN = 3600
ADAM_LR = 0.001
ADAM_B1 = 0.9
ADAM_B2 = 0.999
ADAM_EPS = 1e-08
ADAM_WD = 0.01
ADAM_STEP = 10

SEG_FOX = 2048
SEG_SMALL = 3584
D_CAT = 3840
LANES = 128
TOK_BLK = 256
FOX_BQ = 256
VMEM_LIMIT = 56 * 1024 * 1024
NEG = -1e30

N_CHIPS = 4
W_IN_ROWS = 928


def _params(sem=None, **kw):
    return pltpu.CompilerParams(dimension_semantics=sem, vmem_limit_bytes=VMEM_LIMIT, **kw)


def _sigmoid(x):
    return 1.0 / (1.0 + jnp.exp(-x))


def _softplus(x):
    return jnp.maximum(x, 0.0) + jnp.log(1.0 + jnp.exp(-jnp.abs(x)))


def _ln_fwd(x, g, b):
    mu = jnp.mean(x, -1, keepdims=True)
    xc = x - mu
    var = jnp.mean(xc * xc, -1, keepdims=True)
    rstd = lax.rsqrt(var + LN_EPS)
    xhat = xc * rstd
    return xhat * g + b, xhat, rstd


def _ln_bwd(dy, xhat, rstd, g):
    dxh = dy * g
    m1 = jnp.mean(dxh, -1, keepdims=True)
    m2 = jnp.mean(dxh * xhat, -1, keepdims=True)
    return rstd * (dxh - m1 - xhat * m2)


def _dot(a, b, prec=HI):
    return jnp.dot(a, b, precision=prec, preferred_element_type=f32)


def _dot_nt(a, b, prec=HI):
    return lax.dot_general(a, b, (((1,), (1,)), ((), ())), precision=prec, preferred_element_type=f32)


def _dot_tn(a, b, prec=HI):
    return lax.dot_general(a, b, (((0,), (0,)), ((), ())), precision=prec, preferred_element_type=f32)


def _bdot(a, b):
    return _dot(a.astype(bf16), b.astype(bf16), None)


def _bdot_nt(a, b):
    return _dot_nt(a.astype(bf16), b.astype(bf16), None)


def _bdot_tn(a, b):
    return _dot_tn(a.astype(bf16), b.astype(bf16), None)


def _lane(shape):
    return lax.broadcasted_iota(jnp.int32, shape, len(shape) - 1)


def _mm(a, b, mode, tm, tn, name, out_dtype=f32, epi=None, extra=None, shards=1, comm=None):
    if mode == "nn":
        (m, k), n = a.shape, b.shape[-1] * shards
    elif mode == "nt":
        (m, k), n = a.shape, b.shape[-2]
    else:
        (k, m), n = a.shape, b.shape[1]
    assert m % tm == 0 and n % tn == 0, (name, m, n, tm, tn)
    per = (n // shards) // tn
    assert mode == "nt" or per * tn * shards == n, (name, n, tn, shards)
    nc = 512 if tn % 512 == 0 else (256 if tn % 256 == 0 else 128)
    ks = k // shards

    def body(a_ref, b_ref, *rest):
        for n0 in range(0, tn, nc):
            if mode == "nn":
                acc = jnp.dot(a_ref[...], b_ref[:, n0:n0 + nc], preferred_element_type=f32)
            elif mode == "nt" and shards > 1:
                acc = jnp.zeros((tm, nc), f32)
                for d in range(shards):
                    acc = acc + lax.dot_general(a_ref[:, d * ks:(d + 1) * ks], b_ref[d, n0:n0 + nc, :], (((1,), (1,)), ((), ())),
                                                preferred_element_type=f32)
            elif mode == "nt":
                acc = lax.dot_general(a_ref[...], b_ref[n0:n0 + nc, :], (((1,), (1,)), ((), ())), preferred_element_type=f32)
            else:
                acc = lax.dot_general(a_ref[...], b_ref[:, n0:n0 + nc], (((0,), (0,)), ((), ())), preferred_element_type=f32)
            if epi == "relu2":
                relu_ref, act_ref = rest
                r = jnp.maximum(acc, 0.0)
                relu_ref[:, n0:n0 + nc] = r.astype(bf16)
                act_ref[:, n0:n0 + nc] = (r * r).astype(bf16)
            elif epi == "relu2_bwd":
                relu_ref, o_ref = rest
                o_ref[:, n0:n0 + nc] = (acc * (2.0 * relu_ref[:, n0:n0 + nc].astype(f32))).astype(bf16)
            else:
                (o_ref,) = rest
                o_ref[:, n0:n0 + nc] = acc.astype(out_dtype)

    if mode == "tn":
        a_spec = pl.BlockSpec((k, tm), lambda j, i: (0, i))
    else:
        a_spec = pl.BlockSpec((tm, k), lambda j, i: (i, 0))
    if mode == "nt" and shards > 1:
        b_spec = pl.BlockSpec((shards, tn, ks), lambda j, i: (0, j, 0))
    elif mode == "nt":
        b_spec = pl.BlockSpec((tn, k), lambda j, i: (j, 0))
    elif mode == "nn" and shards > 1:
        b_spec = pl.BlockSpec((None, k, tn), lambda j, i: (j // per, 0, j % per))
    else:
        b_spec = pl.BlockSpec((k, tn), lambda j, i: (0, j))
    o_spec = pl.BlockSpec((tm, tn), lambda j, i: (i, j))
    in_specs = [a_spec, b_spec]
    args = [a, b]
    if epi == "relu2":
        out_shape = (jax.ShapeDtypeStruct((m, n), bf16), jax.ShapeDtypeStruct((m, n), bf16))
        out_specs = (o_spec, o_spec)
    elif epi == "relu2_bwd":
        in_specs.append(o_spec)
        args.append(extra)
        out_shape = jax.ShapeDtypeStruct((m, n), bf16)
        out_specs = o_spec
    elif mode == "tn" and shards > 1:
        out_shape = jax.ShapeDtypeStruct((shards, m, n // shards), out_dtype)
        out_specs = pl.BlockSpec((None, tm, tn), lambda j, i: (j // per, i, j % per))
    else:
        out_shape = jax.ShapeDtypeStruct((m, n), out_dtype)
        out_specs = o_spec
    single = not isinstance(out_shape, tuple)
    res, moved = _hosted(body, comm, name=name, grid=(n // tn, m // tm), in_specs=in_specs,
                         out_specs=(out_specs,) if single else out_specs, out_shape=(out_shape,) if single else out_shape, args=args)
    res = res[0] if single else res
    return res if comm is None else (res, moved)


def _row_spec(width, col=0):
    return pl.BlockSpec((TOK_BLK, width), lambda i: (i, col))


def _vec_spec(rows, width):
    return pl.BlockSpec((rows, width), lambda i: (0, 0))


def _ln_in(x, g, b, comm=None):
    t, d = x.shape

    def body(x_ref, g_ref, b_ref, h_ref, hb_ref):
        h, _, _ = _ln_fwd(x_ref[...], g_ref[...], b_ref[...])
        h_ref[...] = h
        hb_ref[...] = h.astype(bf16)

    return _hosted(
        body, comm, name="ln_in", grid=(t // TOK_BLK,),
        in_specs=[_row_spec(d), _vec_spec(1, d), _vec_spec(1, d)],
        out_specs=(_row_spec(d), _row_spec(d)),
        out_shape=(jax.ShapeDtypeStruct((t, d), f32), jax.ShapeDtypeStruct((t, d), bf16)),
        args=(x, g, b))


def _attn_post(o_gdn, proj, o_fox, g_gdn, g_fox2, comm=None):
    t = o_gdn.shape[0]

    def body(og_ref, z_ref, of_ref, gg_ref, gf_ref, out_ref):
        for h in range(GDN_HEADS):
            sl = slice(h * LANES, (h + 1) * LANES)
            og = og_ref[:, sl]
            z = z_ref[:, sl]
            r = lax.rsqrt(jnp.mean(og * og, -1, keepdims=True) + NORM_EPS)
            out_ref[:, sl] = (og * r * gg_ref[...] * (z * _sigmoid(z))).astype(bf16)
        lo = _lane((TOK_BLK, LANES)) < FOX_DH
        for pr in range(FOX_HEADS // 2):
            sl = slice(pr * LANES, (pr + 1) * LANES)
            of = of_ref[:, sl]
            sq = of * of
            s0 = jnp.sum(jnp.where(lo, sq, 0.0), -1, keepdims=True)
            s1 = jnp.sum(jnp.where(lo, 0.0, sq), -1, keepdims=True)
            r = lax.rsqrt(jnp.where(lo, s0, s1) * (1.0 / FOX_DH) + NORM_EPS)
            out_ref[:, 512 + pr * LANES:512 + (pr + 1) * LANES] = (of * r * gf_ref[...]).astype(bf16)

    return _hosted(
        body, comm, name="attn_post", grid=(t // TOK_BLK,),
        in_specs=[_row_spec(512), _row_spec(512, OFF_Z // 512), _row_spec(512), _vec_spec(1, LANES), _vec_spec(1, LANES)],
        out_specs=(_row_spec(D_MODEL),),
        out_shape=(jax.ShapeDtypeStruct((t, D_MODEL), bf16),),
        args=(o_gdn, proj, o_fox, g_gdn, g_fox2))


def _attn_post_bwd(dattn, o_gdn, proj, o_fox, g_gdn, g_fox2):
    t = o_gdn.shape[0]

    def body(da_ref, og_ref, z_ref, of_ref, gg_ref, gf_ref, dog_ref, dz_ref, dof_ref, pg_ref):
        i = pl.program_id(0)

        @pl.when(i == 0)
        def _():
            pg_ref[...] = jnp.zeros_like(pg_ref)

        dgg = jnp.zeros((1, LANES), f32)
        for h in range(GDN_HEADS):
            sl = slice(h * LANES, (h + 1) * LANES)
            og = og_ref[:, sl]
            z = z_ref[:, sl]
            dout = da_ref[:, sl]
            g = gg_ref[...]
            r = lax.rsqrt(jnp.mean(og * og, -1, keepdims=True) + NORM_EPS)
            sg = _sigmoid(z)
            silu = z * sg
            ng = og * r * g
            dng = dout * silu
            dz_ref[:, sl] = (dout * ng * (sg * (1.0 + z * (1.0 - sg)))).astype(bf16)
            dgg = dgg + jnp.sum(dng * og * r, 0, keepdims=True)
            gd = dng * g
            dog_ref[:, sl] = r * gd - og * (r * r * r) * jnp.mean(og * gd, -1, keepdims=True)
        pg_ref[0:1, :] += dgg
        lo = _lane((TOK_BLK, LANES)) < FOX_DH
        dgf = jnp.zeros((1, LANES), f32)
        for pr in range(FOX_HEADS // 2):
            sl = slice(pr * LANES, (pr + 1) * LANES)
            of = of_ref[:, sl]
            dout = da_ref[:, 512 + pr * LANES:512 + (pr + 1) * LANES]
            g = gf_ref[...]
            sq = of * of
            s0 = jnp.sum(jnp.where(lo, sq, 0.0), -1, keepdims=True)
            s1 = jnp.sum(jnp.where(lo, 0.0, sq), -1, keepdims=True)
            r = lax.rsqrt(jnp.where(lo, s0, s1) * (1.0 / FOX_DH) + NORM_EPS)
            dgf = dgf + jnp.sum(dout * of * r, 0, keepdims=True)
            gd = dout * g
            xg = of * gd
            m0 = jnp.sum(jnp.where(lo, xg, 0.0), -1, keepdims=True)
            m1 = jnp.sum(jnp.where(lo, 0.0, xg), -1, keepdims=True)
            dof_ref[:, sl] = r * gd - of * (r * r * r) * (jnp.where(lo, m0, m1) * (1.0 / FOX_DH))
        pg_ref[1:2, :] += dgf

    return pl.pallas_call(
        body, name="attn_post_bwd", grid=(t // TOK_BLK,),
        in_specs=[_row_spec(D_MODEL), _row_spec(512), _row_spec(512, OFF_Z // 512), _row_spec(512), _vec_spec(1, LANES), _vec_spec(1, LANES)],
        out_specs=(_row_spec(512), _row_spec(512), _row_spec(512), _vec_spec(8, LANES)),
        out_shape=(jax.ShapeDtypeStruct((t, 512), f32), jax.ShapeDtypeStruct((t, 512), bf16),
                   jax.ShapeDtypeStruct((t, 512), f32), jax.ShapeDtypeStruct((8, LANES), f32)),
        compiler_params=_params(("arbitrary",)),
    )(dattn, o_gdn, proj, o_fox, g_gdn, g_fox2)


def _product_specs(lhs, rhs):
    return [_row_spec(lhs.shape[1]), pl.BlockSpec(rhs.shape, lambda i: (0, 0))]


def _ln1(h0, lhs, rhs, g, b, comm=None):
    t, d = h0.shape

    def body(h0_ref, lhs_ref, rhs_ref, g_ref, b_ref, h_ref, hb_ref, xh_ref, rs_ref):
        mix = jnp.dot(lhs_ref[...], rhs_ref[...], preferred_element_type=f32)
        h, xhat, rstd = _ln_fwd(ALPHA * h0_ref[...] + mix, g_ref[...], b_ref[...])
        h_ref[...] = h
        hb_ref[...] = h.astype(bf16)
        xh_ref[...] = xhat
        rs_ref[...] = jnp.broadcast_to(rstd, rs_ref.shape)

    return _hosted(
        body, comm, name="ln1", grid=(t // TOK_BLK,),
        in_specs=[_row_spec(d)] + _product_specs(lhs, rhs) + [_vec_spec(1, d), _vec_spec(1, d)],
        out_specs=(_row_spec(d), _row_spec(d), _row_spec(d), _row_spec(LANES)),
        out_shape=(jax.ShapeDtypeStruct((t, d), f32), jax.ShapeDtypeStruct((t, d), bf16),
                   jax.ShapeDtypeStruct((t, d), f32), jax.ShapeDtypeStruct((t, LANES), f32)),
        args=(h0, lhs, rhs, g, b))


def _ln2_loss(h1, lhs, rhs, pe, gp, b_gate, g, b, target):
    t, d = h1.shape

    def body(h1_ref, lhs_ref, rhs_ref, pe_ref, gp_ref, bg_ref, g_ref, b_ref, t_ref, dr_ref, drb_ref, dpe_ref, dgp_ref, pg_ref):
        i = pl.program_id(0)

        @pl.when(i == 0)
        def _():
            pg_ref[...] = jnp.zeros_like(pg_ref)

        ff = jnp.dot(lhs_ref[...], rhs_ref[...], preferred_element_type=f32)
        sig = _sigmoid(gp_ref[...] + bg_ref[...])
        pe = pe_ref[...]
        r2 = ALPHA * h1_ref[...] + ff + pe * sig
        y, xhat, rstd = _ln_fwd(r2, g_ref[...], b_ref[...])
        err = y - t_ref[...]
        dy = err * (1.0 / d)
        dr = _ln_bwd(dy, xhat, rstd, g_ref[...])
        dr_ref[...] = dr
        drb_ref[...] = dr.astype(bf16)
        dpe_ref[...] = (dr * sig).astype(bf16)
        dgp = dr * pe * sig * (1.0 - sig)
        dgp_ref[...] = dgp.astype(bf16)
        pg_ref[0:1, :] += jnp.sum(dy * xhat, 0, keepdims=True)
        pg_ref[1:2, :] += jnp.sum(dy, 0, keepdims=True)
        pg_ref[2:3, :] += jnp.sum(dgp, 0, keepdims=True)
        pg_ref[3:4, :] += 0.5 * jnp.sum(jnp.mean(err * err, -1, keepdims=True), 0, keepdims=True)

    return pl.pallas_call(
        body, name="ln2_loss", grid=(t // TOK_BLK,),
        in_specs=[_row_spec(d)] + _product_specs(lhs, rhs) + [_row_spec(d)] * 2 + [_vec_spec(1, d)] * 3 + [_row_spec(d)],
        out_specs=(_row_spec(d), _row_spec(d), _row_spec(d), _row_spec(d), _vec_spec(8, d)),
        out_shape=(jax.ShapeDtypeStruct((t, d), f32), jax.ShapeDtypeStruct((t, d), bf16), jax.ShapeDtypeStruct((t, d), bf16),
                   jax.ShapeDtypeStruct((t, d), bf16), jax.ShapeDtypeStruct((8, d), f32)),
        compiler_params=_params(("arbitrary",)),
    )(h1, lhs, rhs, pe, gp, b_gate, g, b, target)


def _ln1_bwd(dr2, da, db, xhat, rstd, g):
    t, d = dr2.shape

    def body(dr2_ref, da_ref, db_ref, xh_ref, rs_ref, g_ref, dr_ref, drb_ref, pg_ref):
        i = pl.program_id(0)

        @pl.when(i == 0)
        def _():
            pg_ref[...] = jnp.zeros_like(pg_ref)

        dh = ALPHA * dr2_ref[...] + da_ref[...] + db_ref[...]
        xhat = xh_ref[...]
        dr = _ln_bwd(dh, xhat, rs_ref[:, 0:1], g_ref[...])
        dr_ref[...] = dr
        drb_ref[...] = dr.astype(bf16)
        pg_ref[0:1, :] += jnp.sum(dh * xhat, 0, keepdims=True)
        pg_ref[1:2, :] += jnp.sum(dh, 0, keepdims=True)

    return pl.pallas_call(
        body, name="ln1_bwd", grid=(t // TOK_BLK,),
        in_specs=[_row_spec(d)] * 4 + [_row_spec(LANES), _vec_spec(1, d)],
        out_specs=(_row_spec(d), _row_spec(d), _vec_spec(8, d)),
        out_shape=(jax.ShapeDtypeStruct((t, d), f32), jax.ShapeDtypeStruct((t, d), bf16), jax.ShapeDtypeStruct((8, d), f32)),
        compiler_params=_params(("arbitrary",)),
    )(dr2, da, db, xhat, rstd, g)


def _ln_in_bwd(x, dr1, dmm, g, comm=None):
    t, d = x.shape

    def body(x_ref, dr1_ref, dmm_ref, g_ref, dx_ref, pg_ref):
        i = pl.program_id(0)

        @pl.when(i == 0)
        def _():
            pg_ref[...] = jnp.zeros_like(pg_ref)

        dh = ALPHA * dr1_ref[...] + dmm_ref[...]
        _, xhat, rstd = _ln_fwd(x_ref[...], g_ref[...], 0.0)
        dx_ref[...] = _ln_bwd(dh, xhat, rstd, g_ref[...])
        pg_ref[0:1, :] += jnp.sum(dh * xhat, 0, keepdims=True)
        pg_ref[1:2, :] += jnp.sum(dh, 0, keepdims=True)

    return _hosted(
        body, comm, name="ln_in_bwd", grid=(t // TOK_BLK,),
        in_specs=[_row_spec(d)] * 3 + [_vec_spec(1, d)],
        out_specs=(_row_spec(d), _vec_spec(8, d)),
        out_shape=(jax.ShapeDtypeStruct((t, d), f32), jax.ShapeDtypeStruct((8, d), f32)),
        args=(x, dr1, dmm, g))


def _tri(n, upper=False, strict=False):
    r = lax.broadcasted_iota(jnp.int32, (n, n), 0)
    c = lax.broadcasted_iota(jnp.int32, (n, n), 1)
    if upper:
        m = (c > r) if strict else (c >= r)
    else:
        m = (c < r) if strict else (c <= r)
    return jnp.where(m, 1.0, 0.0).astype(f32)


def _gate_values(x, bias, alog, lane):
    z = x + bias
    return jnp.where(lane < 4, _sigmoid(z), jnp.where(lane < 8, -jnp.exp(alog) * _softplus(z), jnp.where(lane < 16, -_softplus(-z), 0.0)))


def _gates(proj, bias_row, alog_row):
    t = proj.shape[0]
    nch = t // CHUNK

    def body(x_ref, bias_ref, alog_ref, gates_ref, gcum_ref, gcumt_ref):
        lane = _lane((t, LANES))
        gates = _gate_values(x_ref[...], bias_ref[...], alog_ref[...], lane)
        gates_ref[...] = gates
        g3 = gates.reshape(nch, CHUNK, LANES)
        tri = jnp.broadcast_to(_tri(CHUNK)[None], (nch, CHUNK, CHUNK))
        loc = jnp.einsum("bij,bjk->bik", tri, g3, precision=HI, preferred_element_type=f32)
        tot = jnp.sum(g3, axis=1)
        offs = _dot(_tri(nch, strict=True), tot)
        glob = loc + offs[:, None, :]
        lane3 = _lane((nch, CHUNK, LANES))
        gcum = jnp.where(lane3 < 4, g3, jnp.where(lane3 < 8, loc, glob)).reshape(t, LANES)
        gcum_ref[...] = gcum
        gcumt_ref[...] = gcum.T

    return pl.pallas_call(
        body, name="gates", grid=(1,),
        in_specs=[pl.BlockSpec((t, LANES), lambda i: (0, SEG_SMALL // LANES)), _vec_spec(1, LANES), _vec_spec(1, LANES)],
        out_specs=(pl.BlockSpec((t, LANES), lambda i: (0, 0)), pl.BlockSpec((t, LANES), lambda i: (0, 0)),
                   pl.BlockSpec((LANES, t), lambda i: (0, 0))),
        out_shape=(jax.ShapeDtypeStruct((t, LANES), f32), jax.ShapeDtypeStruct((t, LANES), f32), jax.ShapeDtypeStruct((LANES, t), f32)),
        compiler_params=_params(("arbitrary",)),
    )(proj, bias_row, alog_row)


def _gates_bwd(proj, bias_row, alog_row, gates, dgates, dccol, dct):
    t = proj.shape[0]
    nch = t // CHUNK

    def body(x_ref, bias_ref, alog_ref, gates_ref, dg_ref, dcc_ref, dct_ref, dx_ref, pg_ref):
        lane = _lane((t, LANES))
        d = dg_ref[...] + dcc_ref[...] + dct_ref[...].T
        d3 = d.reshape(nch, CHUNK, LANES)
        tri = jnp.broadcast_to(_tri(CHUNK, upper=True)[None], (nch, CHUNK, CHUNK))
        loc = jnp.einsum("bij,bjk->bik", tri, d3, precision=HI, preferred_element_type=f32)
        tot = jnp.sum(d3, axis=1)
        offs = _dot(_tri(nch, upper=True, strict=True), tot)
        glob = loc + offs[:, None, :]
        lane3 = _lane((nch, CHUNK, LANES))
        dpre = jnp.where(lane3 < 4, d3, jnp.where(lane3 < 8, loc, glob)).reshape(t, LANES)
        z = x_ref[...] + bias_ref[...]
        sg = _sigmoid(z)
        dx = jnp.where(lane < 4, dpre * sg * (1.0 - sg),
                       jnp.where(lane < 8, dpre * (-jnp.exp(alog_ref[...])) * sg, jnp.where(lane < 16, dpre * (1.0 - sg), 0.0)))
        dx_ref[...] = dx.astype(bf16)
        pg_ref[...] = jnp.zeros_like(pg_ref)
        pg_ref[0:1, :] = jnp.sum(dx, 0, keepdims=True)
        pg_ref[1:2, :] = jnp.sum(jnp.where((lane >= 4) & (lane < 8), dpre * gates_ref[...], 0.0), 0, keepdims=True)

    full = pl.BlockSpec((t, LANES), lambda i: (0, 0))
    return pl.pallas_call(
        body, name="gates_bwd", grid=(1,),
        in_specs=[pl.BlockSpec((t, LANES), lambda i: (0, SEG_SMALL // LANES)), _vec_spec(1, LANES), _vec_spec(1, LANES),
                  full, full, full, pl.BlockSpec((LANES, t), lambda i: (0, 0))],
        out_specs=(full, _vec_spec(8, LANES)),
        out_shape=(jax.ShapeDtypeStruct((t, LANES), bf16), jax.ShapeDtypeStruct((8, LANES), f32)),
        compiler_params=_params(("arbitrary",)),
    )(proj, bias_row, alog_row, gates, dgates, dccol, dct)


def _conv_act(u, cw, row, t):
    c = cw[3:4, :] * u
    for jj in range(CONV_W - 1):
        sh = CONV_W - 1 - jj
        c = c + cw[jj:jj + 1, :] * jnp.where(row >= sh, pltpu.roll(u, sh, axis=0), 0.0)
    return c


def _gdn_conv(proj, conv_w, comm=None):
    t = proj.shape[0]
    nblk = GDN_QKV // LANES

    def body(u_ref, cw_ref, c_ref, y_ref):
        j = pl.program_id(0)
        row = lax.broadcasted_iota(jnp.int32, (t, LANES), 0)
        c = _conv_act(u_ref[...], cw_ref[...], row, t)
        c_ref[...] = c
        s = c * _sigmoid(c)
        r = lax.rsqrt(jnp.sum(s * s, -1, keepdims=True) + NORM_EPS)
        scale = jnp.where(j < GDN_HEADS, GDN_DK ** -0.5, 1.0)
        y_ref[...] = jnp.where(j < 2 * GDN_HEADS, s * (r * scale), s)

    blk = pl.BlockSpec((t, LANES), lambda j: (0, j))
    return _hosted(
        body, comm, name="gdn_conv", grid=(nblk,),
        in_specs=[blk, pl.BlockSpec((CONV_W, LANES), lambda j: (0, j))],
        out_specs=(blk, blk),
        out_shape=(jax.ShapeDtypeStruct((t, GDN_QKV), f32), jax.ShapeDtypeStruct((t, GDN_QKV), f32)),
        args=(proj, conv_w))


def _gdn_conv_bwd(proj, conv_w, c, dy, comm=None):
    t = proj.shape[0]
    nblk = GDN_QKV // LANES

    def body(u_ref, cw_ref, c_ref, dy_ref, du_ref, dcw_ref):
        j = pl.program_id(0)
        row = lax.broadcasted_iota(jnp.int32, (t, LANES), 0)
        u = u_ref[...]
        cw = cw_ref[...]
        c = c_ref[...]
        dy = dy_ref[...]
        sg = _sigmoid(c)
        s = c * sg
        r = lax.rsqrt(jnp.sum(s * s, -1, keepdims=True) + NORM_EPS)
        n = s * r
        scale = jnp.where(j < GDN_HEADS, GDN_DK ** -0.5, 1.0)
        dn = dy * scale
        ds = jnp.where(j < 2 * GDN_HEADS, r * (dn - n * jnp.sum(dn * n, -1, keepdims=True)), dy)
        dc = ds * (sg * (1.0 + c * (1.0 - sg)))
        du = cw[3:4, :] * dc
        dcw_ref[...] = jnp.zeros_like(dcw_ref)
        dcw_ref[3:4, :] = jnp.sum(dc * u, 0, keepdims=True)
        for jj in range(CONV_W - 1):
            sh = CONV_W - 1 - jj
            du = du + cw[jj:jj + 1, :] * jnp.where(row < t - sh, pltpu.roll(dc, t - sh, axis=0), 0.0)
            dcw_ref[jj:jj + 1, :] = jnp.sum(dc * jnp.where(row >= sh, pltpu.roll(u, sh, axis=0), 0.0), 0, keepdims=True)
        du_ref[...] = du.astype(bf16)

    blk = pl.BlockSpec((t, LANES), lambda j: (0, j))
    return _hosted(
        body, comm, name="gdn_conv_bwd", grid=(nblk,),
        in_specs=[blk, pl.BlockSpec((CONV_W, LANES), lambda j: (0, j)), blk, blk],
        out_specs=(blk, pl.BlockSpec((8, LANES), lambda j: (0, j))),
        out_shape=(jax.ShapeDtypeStruct((t, GDN_QKV), bf16), jax.ShapeDtypeStruct((8, GDN_QKV), f32)),
        args=(proj, conv_w, c, dy))


def _chunk_masks():
    r = lax.broadcasted_iota(jnp.int32, (CHUNK, CHUNK), 0)
    c = lax.broadcasted_iota(jnp.int32, (CHUNK, CHUNK), 1)
    return r >= c, r > c, r == c


def _col_to_row(col, eye):
    return jnp.sum(jnp.where(eye, col, 0.0), axis=0, keepdims=True)


def _row_to_col(row, eye):
    return jnp.sum(jnp.where(eye, row, 0.0), axis=1, keepdims=True)


NN = (((1,), (0,)), ((), ()))
NT = (((1,), (1,)), ((), ()))
TN = (((0,), (0,)), ((), ()))
GDN_GROUP = 4


def _mx(a, b, dims=NN, passes=1):
    d = lambda p, q: lax.dot_general(p, q, dims, preferred_element_type=f32)
    ah, bh = a.astype(bf16), b.astype(bf16)
    if passes == 1:
        return d(ah, bh)
    al = (a - ah.astype(f32)).astype(bf16)
    bl = (b - bh.astype(f32)).astype(bf16)
    return d(ah, bh) + (d(ah, bl) + d(al, bh))


def _gdn_decay(gam, masks):
    causal, _, eye = masks
    return jnp.exp(jnp.where(causal, gam - _col_to_row(gam, eye), NEG))


def _gdn_local(y, gcum, comm=None):
    t = y.shape[0]
    nch = t // CHUNK
    rows_blk = GDN_GROUP * CHUNK

    def body(y_ref, g_ref, u_ref, w_ref, qk_ref, tinv_ref):
        masks = _chunk_masks()
        _, strict, eye = masks
        ids = [(j, h) for j in range(GDN_GROUP) for h in range(GDN_HEADS)]
        rs = lambda j: slice(j * CHUNK, (j + 1) * CHUNK)
        col = lambda base, h: slice(base + h * LANES, base + (h + 1) * LANES)
        kn = [y_ref[rs(j), col(512, h)] for j, h in ids]
        beta = [g_ref[rs(j), h:h + 1] for j, h in ids]
        gam = [g_ref[rs(j), 4 + h:5 + h] for j, h in ids]
        dec = [_gdn_decay(g, masks) for g in gam]
        x = [-jnp.where(strict, _mx(k, k, NT) * d * b, 0.0) for k, d, b in zip(kn, dec, beta)]
        tinv = [jnp.where(eye, 1.0, 0.0) + a for a in x]
        for _ in range(5):
            x = [_mx(a, a, NN, 3) for a in x]
            tinv = [t_ + _mx(t_, a, NN, 3) for t_, a in zip(tinv, x)]
        for (j, h), t_, k, d, b, g in zip(ids, tinv, kn, dec, beta, gam):
            u_ref[rs(j), col(0, h)] = _mx(t_, b * y_ref[rs(j), col(1024, h)])
            w_ref[rs(j), col(0, h)] = _mx(t_, (b * jnp.exp(g)) * k)
            qk_ref[j, h] = _mx(y_ref[rs(j), col(0, h)], k, NT) * d
            tinv_ref[j, h] = t_

    mat = pl.BlockSpec((GDN_GROUP, GDN_HEADS, CHUNK, CHUNK), lambda n: (n, 0, 0, 0))
    return _hosted(
        body, comm, name="gdn_local", grid=(nch // GDN_GROUP,),
        in_specs=[pl.BlockSpec((rows_blk, GDN_QKV), lambda n: (n, 0)), pl.BlockSpec((rows_blk, LANES), lambda n: (n, 0))],
        out_specs=(pl.BlockSpec((rows_blk, 512), lambda n: (n, 0)), pl.BlockSpec((rows_blk, 512), lambda n: (n, 0)), mat, mat),
        out_shape=(jax.ShapeDtypeStruct((t, 512), f32), jax.ShapeDtypeStruct((t, 512), f32),
                   jax.ShapeDtypeStruct((nch, GDN_HEADS, CHUNK, CHUNK), f32), jax.ShapeDtypeStruct((nch, GDN_HEADS, CHUNK, CHUNK), f32)),
        args=(y, gcum))


def _gdn_fwd(y, gcum, u, w, qk, comm=None):
    t = y.shape[0]
    nch = t // CHUNK

    def body(y_ref, g_ref, u_ref, w_ref, qk_ref, o_ref, sall_ref, s_ref):
        @pl.when(pl.program_id(0) == 0)
        def _():
            s_ref[...] = jnp.zeros_like(s_ref)

        heads = range(GDN_HEADS)
        sl = [slice(h * LANES, (h + 1) * LANES) for h in heads]
        gam = [g_ref[:, 4 + h:5 + h] for h in heads]
        gam_last = [g[CHUNK - 1:CHUNK, :] for g in gam]
        s = [s_ref[h] for h in heads]
        for h in heads:
            sall_ref[0, h] = s[h]
        ws = [_mx(w_ref[:, sl[h]], s[h]) for h in heads]
        qs = [_mx(y_ref[:, sl[h]] * jnp.exp(gam[h]), s[h]) for h in heads]
        vn = [u_ref[:, sl[h]] - ws[h] for h in heads]
        av = [_mx(qk_ref[0, h], vn[h]) for h in heads]
        kv = [_mx(y_ref[:, 512 + h * LANES:512 + (h + 1) * LANES] * jnp.exp(gam_last[h] - gam[h]), vn[h], TN) for h in heads]
        for h in heads:
            o_ref[:, sl[h]] = qs[h] + av[h]
            s_ref[h] = jnp.exp(gam_last[h]) * s[h] + kv[h]

    row = lambda width: pl.BlockSpec((CHUNK, width), lambda n: (n, 0))
    return _hosted(
        body, comm, name="gdn_fwd", grid=(nch,),
        in_specs=[row(GDN_QKV), row(LANES), row(512), row(512), pl.BlockSpec((1, GDN_HEADS, CHUNK, CHUNK), lambda n: (n, 0, 0, 0))],
        out_specs=(row(512), pl.BlockSpec((1, GDN_HEADS, LANES, LANES), lambda n: (n, 0, 0, 0))),
        out_shape=(jax.ShapeDtypeStruct((t, 512), f32), jax.ShapeDtypeStruct((nch, GDN_HEADS, LANES, LANES), f32)),
        scratch_shapes=[pltpu.VMEM((GDN_HEADS, LANES, LANES), f32)],
        args=(y, gcum, u, w, qk))


def _gdn_bwd(y, gcum, u_all, w_all, qk_all, tinv_all, sall, do, comm=None):
    t = y.shape[0]
    nch = t // CHUNK

    def body(y_ref, g_ref, u_ref, w_ref, qk_ref, tinv_ref, sall_ref, do_ref, dy_ref, dg_ref, ds_ref):
        @pl.when(pl.program_id(0) == 0)
        def _():
            ds_ref[...] = jnp.zeros_like(ds_ref)

        masks = _chunk_masks()
        causal, strict, eye = masks
        lane = _lane((CHUNK, LANES))
        row = lax.broadcasted_iota(jnp.int32, (CHUNK, 1), 0)
        heads = range(GDN_HEADS)
        each = lambda f, *ls: [f(*a) for a in zip(*ls)]
        rsum = lambda a: jnp.sum(a, axis=1, keepdims=True)
        sl = [slice(h * LANES, (h + 1) * LANES) for h in heads]
        qn = [y_ref[:, sl[h]] for h in heads]
        kn = [y_ref[:, 512 + h * LANES:512 + (h + 1) * LANES] for h in heads]
        v = [y_ref[:, 1024 + h * LANES:1024 + (h + 1) * LANES] for h in heads]
        beta = [g_ref[:, h:h + 1] for h in heads]
        gam = [g_ref[:, 4 + h:5 + h] for h in heads]
        gam_last = [g[CHUNK - 1:CHUNK, :] for g in gam]
        dec = [_gdn_decay(g, masks) for g in gam]
        e = [jnp.exp(g) for g in gam]
        f = each(lambda gl_, g: jnp.exp(gl_ - g), gam_last, gam)
        gl = [jnp.exp(g) for g in gam_last]
        u = [u_ref[:, sl[h]] for h in heads]
        w = [w_ref[:, sl[h]] for h in heads]
        qk = [qk_ref[0, h] for h in heads]
        tinv = [tinv_ref[0, h] for h in heads]
        s = [sall_ref[0, h] for h in heads]
        dsn = [ds_ref[h] for h in heads]
        d_o = [do_ref[:, sl[h]] for h in heads]
        qd = each(lambda a, b: a * b, qn, e)
        kd = each(lambda a, b: a * b, kn, f)
        ws = each(_mx, w, s)
        kds = each(_mx, kd, dsn)
        qkdo = each(lambda a, b: _mx(a, b, TN), qk, d_o)
        dqd = each(lambda a, b: _mx(a, b, NT), d_o, s)
        qddo = each(lambda a, b: _mx(a, b, TN), qd, d_o)
        kkd = each(lambda k, d: _mx(k, k, NT) * d, kn, dec)
        vn = each(lambda a, b: a - b, u, ws)
        dvn = each(lambda a, b: a + b, qkdo, kds)
        dqk = each(lambda a, b: jnp.where(causal, _mx(a, b, NT), 0.0), d_o, vn)
        dkd = each(lambda a, b: _mx(a, b, NT), vn, dsn)
        dw = each(lambda a, b: -_mx(a, b, NT), dvn, s)
        wdvn = each(lambda a, b: _mx(a, b, TN), w, dvn)
        dgl = each(lambda a, b: jnp.sum(rsum(a * b), axis=0, keepdims=True), dsn, s)
        for h in heads:
            ds_ref[h] = qddo[h] - wdvn[h] + gl[h] * dsn[h]
        dru = each(lambda a, b: _mx(a, b, TN), tinv, dvn)
        drw = each(lambda a, b: _mx(a, b, TN), tinv, dw)
        dqkr = each(lambda a, b: a * b, dqk, dec)
        dq1 = each(_mx, dqkr, kn)
        dk1 = each(lambda a, b: _mx(a, b, TN), dqkr, qn)
        dnu = each(lambda a, b: _mx(a, b, NT), dru, u)
        dnw = each(lambda a, b: _mx(a, b, NT), drw, w)
        dn = each(lambda a, b: jnp.where(strict, -(a + b), 0.0), dnu, dnw)
        dkk = each(lambda a, b, d: a * b * d, dn, beta, dec)
        dk2 = each(_mx, dkk, kn)
        dk3 = each(lambda a, b: _mx(a, b, TN), dkk, kn)
        dgates = jnp.zeros((CHUNK, LANES), f32)
        for h in heads:
            drw_k = rsum(drw[h] * kn[h])
            dbeta = rsum(dru[h] * v[h]) + e[h] * drw_k + rsum(dn[h] * kkd[h])
            m = dn[h] * (kkd[h] * beta[h]) + dqk[h] * qk[h]
            de = beta[h] * drw_k + rsum(dqd[h] * qn[h])
            df = rsum(dkd[h] * kn[h])
            dgam = rsum(m) - _row_to_col(jnp.sum(m, axis=0, keepdims=True), eye) + de * e[h] - df * f[h]
            dgam_last = jnp.sum(df * f[h], axis=0, keepdims=True) + dgl[h] * gl[h]
            dgam = dgam + jnp.where(row == CHUNK - 1, dgam_last, 0.0)
            dy_ref[:, sl[h]] = dq1[h] + dqd[h] * e[h]
            dy_ref[:, 512 + h * LANES:512 + (h + 1) * LANES] = (beta[h] * e[h]) * drw[h] + dk2[h] + dk3[h] + dk1[h] + dkd[h] * f[h]
            dy_ref[:, 1024 + h * LANES:1024 + (h + 1) * LANES] = beta[h] * dru[h]
            dgates = dgates + jnp.where(lane == h, dbeta, 0.0) + jnp.where(lane == 4 + h, dgam, 0.0)
        dg_ref[...] = dgates

    rev = lambda width: pl.BlockSpec((CHUNK, width), lambda n: (nch - 1 - n, 0))
    mat = lambda d: pl.BlockSpec((1, GDN_HEADS, d, d), lambda n: (nch - 1 - n, 0, 0, 0))
    return _hosted(
        body, comm, name="gdn_bwd", grid=(nch,),
        in_specs=[rev(GDN_QKV), rev(LANES), rev(512), rev(512), mat(CHUNK), mat(CHUNK), mat(LANES), rev(512)],
        out_specs=(rev(GDN_QKV), rev(LANES)),
        out_shape=(jax.ShapeDtypeStruct((t, GDN_QKV), f32), jax.ShapeDtypeStruct((t, LANES), f32)),
        scratch_shapes=[pltpu.VMEM((GDN_HEADS, LANES, LANES), f32)],
        args=(y, gcum, u_all, w_all, qk_all, tinv_all, sall, do))


FOX_CLASSES = 4


def _fox_groups(t):
    nq = t // FOX_BQ
    ncls = min(FOX_CLASSES, nq)
    per = nq // ncls
    return [(g * per, per, (g + 1) * per * FOX_BQ) for g in range(ncls)]


def _fox_scores(q_ref, k_ref, gcum_ref, gcumt_ref, h, i, keys):
    pr = h // 2
    lo = (h % 2) * FOX_DH
    lane = _lane((FOX_BQ, LANES))
    mask = (lane >= lo) & (lane < lo + FOX_DH)
    qm = jnp.where(mask, q_ref[:, pr * LANES:(pr + 1) * LANES], 0.0).astype(bf16)
    kp = k_ref[:, pr * LANES:(pr + 1) * LANES].astype(bf16)
    s = _dot_nt(qm, kp, None) * (FOX_DH ** -0.5)
    s = s + gcum_ref[:, 8 + h:9 + h] - gcumt_ref[8 + h:9 + h, :]
    rows = i * FOX_BQ + lax.broadcasted_iota(jnp.int32, (FOX_BQ, keys), 0)
    cols = lax.broadcasted_iota(jnp.int32, (FOX_BQ, keys), 1)
    return jnp.where(cols <= rows, s, NEG), mask, qm, kp


def _fox_fwd(proj, gcum, gcumt, ride=None):
    c0 = SEG_FOX // 512

    def group_call(q0, nq, keys, comm):
        def body(q_ref, k_ref, v_ref, gcum_ref, gcumt_ref, o_ref, lse_ref):
            i = q0 + pl.program_id(0)
            lane = _lane((FOX_BQ, LANES))
            lse_all = jnp.zeros((FOX_BQ, LANES), f32)
            for pr in range(FOX_HEADS // 2):
                vp = v_ref[:, pr * LANES:(pr + 1) * LANES].astype(bf16)
                o_pair = jnp.zeros((FOX_BQ, LANES), f32)
                for h in (2 * pr, 2 * pr + 1):
                    s, mask, _, _ = _fox_scores(q_ref, k_ref, gcum_ref, gcumt_ref, h, i, keys)
                    m = jnp.max(s, axis=1, keepdims=True)
                    p = jnp.exp(s - m)
                    l = jnp.sum(p, axis=1, keepdims=True)
                    o_h = _dot((p * (1.0 / l)).astype(bf16), vp, None)
                    o_pair = jnp.where(mask, o_h, o_pair)
                    lse_all = jnp.where(lane == h, m + jnp.log(l), lse_all)
                o_ref[:, pr * LANES:(pr + 1) * LANES] = o_pair
            lse_ref[...] = lse_all

        seen = lambda col: pl.BlockSpec((keys, 512), lambda i: (0, col))
        return _hosted(
            body, comm, name=f"fox_fwd_{keys}", grid=(nq,),
            in_specs=[pl.BlockSpec((FOX_BQ, 512), lambda i: (q0 + i, c0)), seen(c0 + 1), seen(c0 + 2),
                      pl.BlockSpec((FOX_BQ, LANES), lambda i: (q0 + i, 0)), pl.BlockSpec((LANES, keys), lambda i: (0, 0))],
            out_specs=(pl.BlockSpec((FOX_BQ, 512), lambda i: (i, 0)), pl.BlockSpec((FOX_BQ, LANES), lambda i: (i, 0))),
            out_shape=(jax.ShapeDtypeStruct((nq * FOX_BQ, 512), f32), jax.ShapeDtypeStruct((nq * FOX_BQ, LANES), f32)),
            args=(proj, proj, proj, gcum, gcumt))

    parts = []
    for n, g in enumerate(_fox_groups(proj.shape[0])):
        hook = ride(n) if ride else None
        part, moved = group_call(*g, hook[0] if hook else None)
        parts.append(part)
        if hook:
            hook[1](moved)
    return jnp.concatenate([o for o, _ in parts], axis=0), jnp.concatenate([l for _, l in parts], axis=0)


def _fox_bwd(proj, gcum, gcumt, o, lse, do, ride=None):
    t = proj.shape[0]
    c0 = SEG_FOX // 512

    def group_call(q0, nq, keys, acc, comm):
        first = acc is None

        def body(q_ref, k_ref, v_ref, gcum_ref, gcumt_ref, o_ref, lse_ref, do_ref, *rest):
            dq_ref, dk_ref, dv_ref, dcc_ref, dct_ref = rest[-5:]
            j = pl.program_id(0)
            i = q0 + j

            @pl.when(j == 0)
            def _():
                if first:
                    dk_ref[...] = jnp.zeros_like(dk_ref)
                    dv_ref[...] = jnp.zeros_like(dv_ref)
                    dct_ref[...] = jnp.zeros_like(dct_ref)
                else:
                    dk_ref[...], dv_ref[...], dct_ref[...] = rest[0][...], rest[1][...], rest[2][...]

            lane = _lane((FOX_BQ, LANES))
            dcc = jnp.zeros((FOX_BQ, LANES), f32)
            scale = FOX_DH ** -0.5
            for pr in range(FOX_HEADS // 2):
                sl = slice(pr * LANES, (pr + 1) * LANES)
                vp = v_ref[:, sl].astype(bf16)
                dq_pair = jnp.zeros((FOX_BQ, LANES), f32)
                for h in (2 * pr, 2 * pr + 1):
                    s, mask, qm, kp = _fox_scores(q_ref, k_ref, gcum_ref, gcumt_ref, h, i, keys)
                    p = jnp.exp(s - lse_ref[:, h:h + 1])
                    dom = jnp.where(mask, do_ref[:, sl], 0.0)
                    delta = jnp.sum(dom * o_ref[:, sl], axis=1, keepdims=True)
                    domb = dom.astype(bf16)
                    ds = p * (_dot_nt(domb, vp, None) - delta)
                    dsb = ds.astype(bf16)
                    dv_ref[:, sl] += _dot_tn(p.astype(bf16), domb, None)
                    dk_ref[:, sl] += _dot_tn(dsb, qm, None) * scale
                    dq_pair = jnp.where(mask, _dot(dsb, kp, None) * scale, dq_pair)
                    dcc = jnp.where(lane == 8 + h, jnp.sum(ds, axis=1, keepdims=True), dcc)
                    dct_ref[8 + h:9 + h, :] += -jnp.sum(ds, axis=0, keepdims=True)
                dq_ref[:, sl] = dq_pair.astype(bf16)
            dcc_ref[...] = dcc

        qblk = lambda col: pl.BlockSpec((FOX_BQ, 512), lambda i: (q0 + i, col))
        oblk = pl.BlockSpec((FOX_BQ, 512), lambda i: (i, 0))
        seen = lambda col: pl.BlockSpec((keys, 512), lambda i: (0, col))
        rblk = pl.BlockSpec((FOX_BQ, LANES), lambda i: (q0 + i, 0))
        seen_t = pl.BlockSpec((LANES, keys), lambda i: (0, 0))
        in_specs = [qblk(c0), seen(c0 + 1), seen(c0 + 2), rblk, seen_t, qblk(0), rblk, qblk(0)]
        args = [proj, proj, proj, gcum, gcumt, o, lse, do]
        aliases = {}
        if not first:
            in_specs += [seen(0), seen(0), seen_t]
            args += list(acc)
            aliases = {8: 1, 9: 2, 10: 4}
        return _hosted(
            body, comm, name=f"fox_bwd_{keys}", grid=(nq,), in_specs=in_specs,
            out_specs=(oblk, seen(0), seen(0), pl.BlockSpec((FOX_BQ, LANES), lambda i: (i, 0)), seen_t),
            out_shape=(jax.ShapeDtypeStruct((nq * FOX_BQ, 512), bf16), jax.ShapeDtypeStruct((t, 512), f32), jax.ShapeDtypeStruct((t, 512), f32),
                       jax.ShapeDtypeStruct((nq * FOX_BQ, LANES), f32), jax.ShapeDtypeStruct((LANES, t), f32)),
            aliases=aliases, args=args)

    acc, dqs, dccs = None, [], []
    for n, g in enumerate(reversed(_fox_groups(t))):
        hook = ride(n) if ride else None
        (dq, dk, dv, dcc, dct), moved = group_call(*g, acc, hook[0] if hook else None)
        if hook:
            hook[1](moved)
        acc = (dk, dv, dct)
        dqs.insert(0, dq)
        dccs.insert(0, dcc)
    return jnp.concatenate(dqs, axis=0), acc[0], acc[1], jnp.concatenate(dccs, axis=0), acc[2]


def _row(v, width=None):
    v = v.reshape(1, -1).astype(f32)
    if width is not None and v.shape[1] < width:
        v = jnp.pad(v, ((0, 0), (0, width - v.shape[1])))
    return v


LATE = ("w_out", "w_up", "w_ple_gate", "w_ple", "w_down")


def _device_grads(x, p, target, small, w_cat, conv_w, late, qc=None, tail=None, ln_in_out=None):
    z4 = jnp.zeros((4,), f32)
    bias_row = _row(jnp.concatenate([z4, small["dt_bias"].reshape(-1), small["b_f"].reshape(-1)]), LANES)
    alog_row = _row(jnp.concatenate([z4, small["a_log"].reshape(-1)]), LANES)
    g_gdn = _row(small["gdn_norm_g"])
    g_fox2 = _row(jnp.tile(small["fox_norm_g"].reshape(-1), 2))
    pb = p.astype(bf16)
    late = list(late)
    comm = qc is not None

    h0, h0b = ln_in_out if ln_in_out is not None else _ln_in(x, _row(small["ln_in_g"]), _row(small["ln_in_b"]))[0]
    proj = _mm(h0b, w_cat, "nt", 512, D_CAT, "mm_proj")
    gates, gcum, gcumt = _gates(proj, bias_row, alog_row)
    w_down_pieces = [(4, 0, 1)]

    def gather(phase, pieces):
        if not comm or not pieces:
            return None, lambda moved: None
        touched = sorted({i for i, _, _ in pieces})

        def took(moved):
            for i, buf in zip(touched, moved):
                late[i] = buf
        return phase([late[i] for i in touched], [(touched.index(i), k, n) for i, k, n in pieces]), took

    over, on = _gather_chips, _gather_pass_on
    cm, took = gather(over, [(0, 0, 1), (3, 0, 1)])
    (conv_c, qkv_n), moved = _gdn_conv(proj, conv_w, cm)
    took(moved)
    cm, took = gather(over, [(1, 0, 2)])
    (gu, gw, gqk, gtinv), moved = _gdn_local(qkv_n, gcum, cm)
    took(moved)
    cm, took = gather(over, [(1, 1, 2)])
    (o_gdn, sall), moved = _gdn_fwd(qkv_n, gcum, gu, gw, gqk, cm)
    took(moved)
    fox_plan = [(over, []), (on, [(0, 0, 1), (3, 0, 1), (1, 0, 2), (1, 1, 2)]), (over, [(2, 0, 1)]), (over, [(4, 0, 2)])]
    assert not comm or len(_fox_groups(x.shape[0])) == len(fox_plan)
    o_fox, lse = _fox_fwd(proj, gcum, gcumt, (lambda n: gather(*fox_plan[n])) if comm else None)
    cm, took = gather(on, [(2, 0, 1)])
    (attn,), moved = _attn_post(o_gdn, proj, o_fox, g_gdn, g_fox2, cm)
    took(moved)
    w_out = late[0].reshape(D_MODEL, D_MODEL)
    (h1, h1b, xhat1, rstd1), _ = _ln1(h0, attn, w_out, _row(small["ln1_g"]), _row(small["ln1_b"]))
    w_up, w_ple = late[1], late[3]
    cm, took = gather(over, [(4, 1, 2)])
    up_act = _mm(h1b, w_up, "nn", 512, 1024, "mm_up", epi="relu2", shards=N_CHIPS, comm=cm)
    if cm:
        up_act, moved = up_act
        took(moved)
    up, act = up_act
    w_gate = late[2].reshape(D_MODEL, D_MODEL)
    cm, took = gather(on, w_down_pieces)
    gp = _mm(h1b, w_gate, "nn", 512, D_MODEL, "mm_gate", comm=cm)
    if cm:
        gp, moved = gp
        took(moved)
    pe = _mm(pb, w_ple, "nn", 512, D_MODEL // N_CHIPS, "mm_ple", shards=N_CHIPS)
    w_down = late[4].reshape(D_FF, D_MODEL)
    dr2, dr2b, dpe, dgp, pg2 = _ln2_loss(h1, act, w_down, pe, gp, _row(small["b_ple_gate"]), _row(small["ln2_g"]), _row(small["ln2_b"]), target)

    dup = _mm(dr2b, w_down, "nt", 512, 2048, "mm_dact", epi="relu2_bwd", extra=up)
    g_down = _mm(act, dr2b, "tn", 1024, D_MODEL, "mm_gdown")
    dh1_a = _mm(dup, w_up, "nt", 512, D_MODEL, "mm_dh1a", shards=N_CHIPS)
    g_up = _mm(h1b, dup, "tn", 1024, 1024, "mm_gup", shards=N_CHIPS)
    dh1_b = _mm(dgp, w_gate, "nt", 512, D_MODEL, "mm_dh1b")
    g_gate = _mm(h1b, dgp, "tn", 1024, D_MODEL, "mm_ggate")
    g_ple = _mm(pb, dpe, "tn", D_PLE, D_MODEL // N_CHIPS, "mm_gple", shards=N_CHIPS)
    dr1, dr1b, pg1 = _ln1_bwd(dr2, dh1_a, dh1_b, xhat1, rstd1, _row(small["ln1_g"]))
    dattn = _mm(dr1b, w_out, "nt", 512, D_MODEL, "mm_dattn")
    g_out = _mm(attn, dr1b, "tn", 1024, D_MODEL, "mm_gout")
    do_gdn, dz, do_fox, pga = _attn_post_bwd(dattn, o_gdn, proj, o_fox, g_gdn, g_fox2)
    g_late = [g.reshape((N_CHIPS, -1, g.shape[-1])) for g in (g_out, g_up, g_gate, g_ple, g_down)]
    chip_plan = [[(4, 0, 2), (4, 1, 2), (0, 0, 1)], [(1, 0, 2)], [(1, 1, 2)], [(2, 0, 1), (3, 0, 1)]]
    state = {}

    def to_sibling():
        def took(moved):
            sums = [_add_pair(g, b1, qc, "add_pair_" + n) for g, b1, n in zip(g_late, moved, LATE)]
            state.update(own=[a for a, _ in sums], sent=[ab for _, ab in sums], landing=_landing([ab for _, ab in sums]))
        return _exchange_pairs(g_late), took

    def to_chips(pieces):
        if not comm:
            return None, lambda moved: None
        return _exchange_chips(state["sent"], state["landing"], pieces), lambda moved: state.update(landing=list(moved))

    def gdn_backward():
        cm, took = to_chips(chip_plan[0])
        state["gdn"], moved = _gdn_bwd(qkv_n, gcum, gu, gw, gqk, gtinv, sall, do_gdn, cm)
        took(moved)

    def fox_ride(n):
        if n == 0:
            return to_sibling()
        if n == 1:
            gdn_backward()
        return to_chips(chip_plan[n])

    assert not comm or len(_fox_groups(x.shape[0])) == len(chip_plan)
    dfq, dfk, dfv, dccol, dct = _fox_bwd(proj, gcum, gcumt, o_fox, lse, do_fox, fox_ride if comm else None)
    if not comm:
        gdn_backward()
    dqkv_n, dgates = state["gdn"]
    dsmall, pgg = _gates_bwd(proj, bias_row, alog_row, gates, dgates, dccol, dct)
    cm = None
    if comm:
        cm = _share_halves([_add_chips(a, b2, qc, "add_chips_" + n) for a, b2, n in zip(state["own"], state["landing"], LATE)])
    (du, g_conv8), reduced = _gdn_conv_bwd(proj, conv_w, conv_c, dqkv_n, cm)
    if comm:
        g_late = list(reduced)
    t = x.shape[0]
    dproj = jnp.concatenate([du, dz, dfq, dfk.astype(bf16), dfv.astype(bf16), dsmall, jnp.zeros((t, D_CAT - SEG_SMALL - LANES), bf16)], axis=1)
    g_cat = _mm(dproj, h0b, "tn", 1280, D_MODEL, "mm_gcat")
    cm, took = tail[0](g_cat) if tail else (None, None)
    dh0_mm = _mm(dproj, w_cat, "nn", 512, D_MODEL, "mm_dh0", comm=cm)
    if cm:
        dh0_mm, moved = dh0_mm
        took(moved)
    cm, took = tail[1]() if tail and tail[1] else (None, None)
    (grad_x, pg0), moved = _ln_in_bwd(x, dr1, dh0_mm, _row(small["ln_in_g"]), cm)
    if cm:
        took(moved)

    g_fox = pga[1, :FOX_DH] + pga[1, FOX_DH:]
    small_grads = dict(
        ln_in_g=pg0[0], ln_in_b=pg0[1], ln1_g=pg1[0], ln1_b=pg1[1], b_ple_gate=pg2[2], ln2_g=pg2[0], ln2_b=pg2[1],
        gdn_norm_g=pga[0], fox_norm_g=g_fox, a_log=pgg[1, 4:8], dt_bias=pgg[0, 4:8], b_f=pgg[0, 8:16], loss=pg2[3, 0:1])
    return grad_x, g_cat, g_conv8[:CONV_W], dict(zip(LATE, g_late)), small_grads


ANY = pl.BlockSpec(memory_space=pl.ANY)
CONV_PKT_ROWS = 16


def _mesh_pos():
    return lax.axis_index("x"), lax.axis_index("y"), lax.axis_index("c")


def _other_chips(x, y):
    return [(1 - x, y), (x, 1 - y), (1 - x, 1 - y)]


def _rcopy(src, dst, send_sem, recv_sem, dev):
    return pltpu.make_async_remote_copy(src_ref=src, dst_ref=dst, send_sem=send_sem, recv_sem=recv_sem,
                                        device_id=dev, device_id_type=MESH)


class _Comm:
    def __init__(self, ins, outs, aliases, n_sems, start, finish):
        self.ins, self.outs, self.aliases, self.n_sems, self.start, self.finish = list(ins), list(outs), dict(aliases), n_sems, start, finish


def _hosted(body, comm, *, name, grid, in_specs, out_specs, out_shape, args, scratch_shapes=(), aliases=None):
    n_in, n_out, n_sc = len(in_specs), len(out_specs), len(scratch_shapes)
    k, ko = (len(comm.ins), len(comm.outs)) if comm else (0, 0)

    def kernel_body(*refs):
        o0 = n_in + k
        s0 = o0 + n_out + ko
        if comm:
            cins, couts, (ssem, rsem) = refs[n_in:o0], refs[o0 + n_out:s0], refs[s0 + n_sc:]
            step = pl.program_id(0)
            for d in range(1, len(grid)):
                step = step * grid[d] + pl.program_id(d)

            @pl.when(step == 0)
            def _():
                comm.start(cins, couts, ssem, rsem)

        body(*refs[:n_in], *refs[o0:o0 + n_out], *refs[s0:s0 + n_sc])
        if comm:
            last = 1
            for n in grid:
                last *= n

            @pl.when(step == last - 1)
            def _():
                comm.finish(cins, couts, ssem, rsem)

    io_aliases = dict(aliases or {})
    scratch = list(scratch_shapes)
    if comm:
        io_aliases.update({n_in + i: n_out + j for i, j in comm.aliases.items()})
        scratch += [pltpu.SemaphoreType.DMA((comm.n_sems,)), pltpu.SemaphoreType.DMA((comm.n_sems,))]
    res = pl.pallas_call(
        kernel_body, name=name, grid=grid, in_specs=list(in_specs) + [ANY] * k, out_specs=tuple(out_specs) + (ANY,) * ko,
        out_shape=tuple(out_shape) + tuple(comm.outs if comm else ()), scratch_shapes=scratch, input_output_aliases=io_aliases,
        compiler_params=_params(("arbitrary",) * len(grid)),
    )(*args, *(comm.ins if comm else ()))
    return tuple(res[:n_out]), tuple(res[n_out:])


def _comm_only(phases, name):
    n_in = sum(len(p.ins) for p in phases)

    def body(*refs):
        n_out = sum(len(p.outs) for p in phases)
        sems = refs[n_in + n_out:]
        i0, o0 = 0, n_in
        for j, p in enumerate(phases):
            cins, couts = refs[i0:i0 + len(p.ins)], refs[o0:o0 + len(p.outs)]
            p.start(cins, couts, sems[2 * j], sems[2 * j + 1])
            p.finish(cins, couts, sems[2 * j], sems[2 * j + 1])
            i0 += len(p.ins)
            o0 += len(p.outs)

    aliases, i0, o0 = {}, 0, 0
    for p in phases:
        aliases.update({i0 + i: o0 + j for i, j in p.aliases.items()})
        i0 += len(p.ins)
        o0 += len(p.outs)
    outs = [o for p in phases for o in p.outs]
    res = pl.pallas_call(
        body, name=name, out_shape=tuple(outs), in_specs=[ANY] * n_in, out_specs=(ANY,) * len(outs), input_output_aliases=aliases,
        scratch_shapes=[pltpu.SemaphoreType.DMA((p.n_sems,)) for p in phases for _ in range(2)],
    )(*[a for p in phases for a in p.ins])
    split, o0 = [], 0
    for p in phases:
        split.append(tuple(res[o0:o0 + len(p.outs)]))
        o0 += len(p.outs)
    return split


def _like(arrays):
    return [jax.ShapeDtypeStruct(a.shape, a.dtype) for a in arrays]


def _half(ref, slot, hf, piece=(0, 1)):
    k, n = piece
    rows = ref.shape[1] // 2 // n
    return ref.at[slot, pl.ds((hf * n + k) * rows, rows)]


def _whole_halves(arrays):
    return [(i, 0, 1) for i in range(len(arrays))]


def _gather_chips(bufs, pieces=None, whole=False, base=0):
    nw = len(bufs)
    pieces = _whole_halves(bufs) if pieces is None else pieces
    part = (lambda ref, slot, c, piece: ref.at[slot]) if whole else _half

    def copies(couts):
        x, y, c = _mesh_pos()
        q = 2 * x + y
        for j, (i, k, n) in enumerate(pieces):
            for kc, chip in enumerate(_other_chips(x, y)):
                mine, theirs = part(couts[i], q, c, (k, n)), part(couts[i], 2 * chip[0] + chip[1], c, (k, n))
                yield base + j * 3 + kc, mine, theirs, (*chip, c)

    def start(cins, couts, ssem, rsem):
        for s, mine, _, dev in copies(couts):
            _rcopy(mine, mine, ssem.at[s], rsem.at[s], dev).start()

    def finish(cins, couts, ssem, rsem):
        for s, _, theirs, dev in copies(couts):
            _rcopy(theirs, theirs, ssem.at[s], rsem.at[s], dev).wait_recv()
        for s, mine, _, dev in copies(couts):
            _rcopy(mine, mine, ssem.at[s], rsem.at[s], dev).wait_send()

    return _Comm(bufs, _like(bufs), {i: i for i in range(nw)}, 3 * len(pieces), start, finish)


def _gather_pass_on(bufs, pieces=None, base=0):
    nw = len(bufs)
    pieces = _whole_halves(bufs) if pieces is None else pieces

    def copies(couts):
        x, y, c = _mesh_pos()
        for j, (i, k, n) in enumerate(pieces):
            for kc, chip in enumerate(_other_chips(x, y)):
                slot = 2 * chip[0] + chip[1]
                yield base + j * 3 + kc, _half(couts[i], slot, c, (k, n)), _half(couts[i], slot, 1 - c, (k, n)), (x, y, 1 - c)

    def start(cins, couts, ssem, rsem):
        for s, landed, _, sib in copies(couts):
            _rcopy(landed, landed, ssem.at[s], rsem.at[s], sib).start()

    def finish(cins, couts, ssem, rsem):
        for s, _, passed, sib in copies(couts):
            _rcopy(passed, passed, ssem.at[s], rsem.at[s], sib).wait_recv()
        for s, landed, _, sib in copies(couts):
            _rcopy(landed, landed, ssem.at[s], rsem.at[s], sib).wait_send()

    return _Comm(bufs, _like(bufs), {i: i for i in range(nw)}, 3 * len(pieces), start, finish)


def _gather_now(bufs, packets):
    nb = len(bufs)
    over, on, pk = _gather_chips(bufs), _gather_pass_on(bufs, base=3 * nb), _gather_chips(packets, whole=True, base=6 * nb)

    def start(cins, couts, ssem, rsem):
        over.start(cins[:nb], couts[:nb], ssem, rsem)
        pk.start(cins[nb:], couts[nb:], ssem, rsem)

    def finish(cins, couts, ssem, rsem):
        over.finish(cins[:nb], couts[:nb], ssem, rsem)
        on.start(cins[:nb], couts[:nb], ssem, rsem)
        on.finish(cins[:nb], couts[:nb], ssem, rsem)
        pk.finish(cins[nb:], couts[nb:], ssem, rsem)

    every = list(bufs) + list(packets)
    return _Comm(every, _like(every), {i: i for i in range(len(every))}, 6 * nb + 3 * len(packets), start, finish)


def _exchange_pairs(gs):
    nw = len(gs)

    def copies(cins, couts):
        x, y, c = _mesh_pos()
        for i in range(nw):
            for d in range(N_CHIPS):
                yield i * N_CHIPS + d, _half(cins[i], d, 1 - c), couts[i].at[d], (x, y, 1 - c)

    def start(cins, couts, ssem, rsem):
        for s, src, dst, sib in copies(cins, couts):
            _rcopy(src, dst, ssem.at[s], rsem.at[s], sib).start()

    def finish(cins, couts, ssem, rsem):
        for s, src, dst, sib in copies(cins, couts):
            _rcopy(src, dst, ssem.at[s], rsem.at[s], sib).wait_recv()
        for s, src, dst, sib in copies(cins, couts):
            _rcopy(src, dst, ssem.at[s], rsem.at[s], sib).wait_send()

    outs = [jax.ShapeDtypeStruct((N_CHIPS, g.shape[1] // 2, g.shape[2]), g.dtype) for g in gs]
    return _Comm(gs, outs, {}, N_CHIPS * nw, start, finish)


def _gather_packets(small):
    def peers():
        x, y, c = _mesh_pos()
        for r in range(1, 8):
            fx, fy, fc = (r >> 2) & 1, (r >> 1) & 1, r & 1
            yield r - 1, (1 - x if fx else x, 1 - y if fy else y, 1 - c if fc else c)

    def start(cins, couts, ssem, rsem):
        x, y, c = _mesh_pos()
        mine = couts[0].at[4 * x + 2 * y + c]
        for s, peer in peers():
            _rcopy(mine, mine, ssem.at[s], rsem.at[s], peer).start()

    def finish(cins, couts, ssem, rsem):
        x, y, c = _mesh_pos()
        mine = couts[0].at[4 * x + 2 * y + c]
        for s, peer in peers():
            theirs = couts[0].at[4 * peer[0] + 2 * peer[1] + peer[2]]
            _rcopy(theirs, theirs, ssem.at[s], rsem.at[s], peer).wait_recv()
        for s, peer in peers():
            _rcopy(mine, mine, ssem.at[s], rsem.at[s], peer).wait_send()

    return _Comm([small], _like([small]), {0: 0}, 7, start, finish)


def _exchange_chips(a4s, b2s, pieces=None):
    nw = len(a4s)
    pieces = _whole_halves(a4s) if pieces is None else pieces

    def copies(cins, couts):
        x, y, c = _mesh_pos()
        for j, (i, k, n) in enumerate(pieces):
            rows = a4s[i].shape[1] // n
            part = pl.ds(k * rows, rows)
            for kc, chip in enumerate(_other_chips(x, y)):
                yield j * 3 + kc, cins[i].at[2 * chip[0] + chip[1], part], couts[i].at[kc, part], (*chip, c)

    def start(cins, couts, ssem, rsem):
        for s, src, dst, dev in copies(cins, couts):
            _rcopy(src, dst, ssem.at[s], rsem.at[s], dev).start()

    def finish(cins, couts, ssem, rsem):
        for s, src, dst, dev in copies(cins, couts):
            _rcopy(src, dst, ssem.at[s], rsem.at[s], dev).wait_recv()
        for s, src, dst, dev in copies(cins, couts):
            _rcopy(src, dst, ssem.at[s], rsem.at[s], dev).wait_send()

    return _Comm(list(a4s) + list(b2s), _like(b2s), {nw + i: i for i in range(nw)}, 3 * len(pieces), start, finish)


def _landing(a4s):
    return [lax.empty((3,) + a.shape[1:], a.dtype) for a in a4s]


def _share_halves(rs):
    nw = len(rs)

    def halves(couts, i, hf):
        rows = rs[i].shape[0] // 2
        return couts[i].at[pl.ds(hf * rows, rows)]

    def start(cins, couts, ssem, rsem):
        x, y, c = _mesh_pos()
        for i in range(nw):
            _rcopy(halves(couts, i, c), halves(couts, i, c), ssem.at[i], rsem.at[i], (x, y, 1 - c)).start()

    def finish(cins, couts, ssem, rsem):
        x, y, c = _mesh_pos()
        for i in range(nw):
            _rcopy(halves(couts, i, 1 - c), halves(couts, i, 1 - c), ssem.at[i], rsem.at[i], (x, y, 1 - c)).wait_recv()
        for i in range(nw):
            _rcopy(halves(couts, i, c), halves(couts, i, c), ssem.at[i], rsem.at[i], (x, y, 1 - c)).wait_send()

    return _Comm(rs, _like(rs), {i: i for i in range(nw)}, nw, start, finish)


ADD_ROWS = 256


def _add_pair(g4, b1, qc_idx, name):
    _, half, cols = b1.shape
    rb = ADD_ROWS if half % ADD_ROWS == 0 else half
    nb = half // rb

    def body(qc_ref, g_ref, b_ref, o_ref, ob_ref):
        a = g_ref[...] + b_ref[...]
        o_ref[...] = a
        ob_ref[...] = a.astype(bf16)

    blk = (1, rb, cols)
    out = pl.BlockSpec(blk, lambda d, i, qc: (d, i, 0))
    return pl.pallas_call(
        body, name=name,
        grid_spec=pltpu.PrefetchScalarGridSpec(
            num_scalar_prefetch=1, grid=(N_CHIPS, nb),
            in_specs=[pl.BlockSpec(blk, lambda d, i, qc: (d, qc[1] * nb + i, 0)), out],
            out_specs=(out, out)),
        out_shape=(jax.ShapeDtypeStruct(b1.shape, f32), jax.ShapeDtypeStruct(b1.shape, bf16)),
        compiler_params=_params(("parallel", "parallel")),
    )(qc_idx, g4, b1)


def _add_chips(a4, b2, qc_idx, name):
    _, half, cols = a4.shape
    rb = ADD_ROWS if half % ADD_ROWS == 0 else half
    nb = half // rb

    def body(qc_ref, a_ref, b_ref, o_ref):
        o_ref[...] = ((a_ref[0] + b_ref[0].astype(f32)) + b_ref[1].astype(f32)) + b_ref[2].astype(f32)

    return pl.pallas_call(
        body, name=name,
        grid_spec=pltpu.PrefetchScalarGridSpec(
            num_scalar_prefetch=1, grid=(nb,),
            in_specs=[pl.BlockSpec((1, rb, cols), lambda i, qc: (qc[0], i, 0)), pl.BlockSpec((3, rb, cols), lambda i, qc: (0, i, 0))],
            out_specs=pl.BlockSpec((rb, cols), lambda i, qc: (qc[1] * nb + i, 0))),
        out_shape=jax.ShapeDtypeStruct((2 * half, cols), f32),
        compiler_params=_params(("parallel",)),
    )(qc_idx, a4, b2)


def _adamw_math(w, g, m, v):
    m = ADAM_B1 * m + (1.0 - ADAM_B1) * g
    v = ADAM_B2 * v + (1.0 - ADAM_B2) * (g * g)
    m_hat = m / (1.0 - ADAM_B1 ** ADAM_STEP)
    v_hat = v / (1.0 - ADAM_B2 ** ADAM_STEP)
    return -ADAM_LR * (m_hat / (jnp.sqrt(v_hat) + ADAM_EPS) + ADAM_WD * w), m, v


def _adamw(w, g, m, v, name, comm=None):
    rows = w.shape[0]
    if w.ndim == 3:
        rb = max(r for r in range(1, ADD_ROWS // 4 + 1) if rows % r == 0)
    else:
        rb = ADD_ROWS if rows % ADD_ROWS == 0 else rows

    def body(w_ref, g_ref, m_ref, v_ref, go_ref, d_ref, mo_ref, vo_ref):
        g = g_ref[...]
        go_ref[...] = g
        d_ref[...], mo_ref[...], vo_ref[...] = _adamw_math(w_ref[...], g, m_ref[...], v_ref[...])

    blk = pl.BlockSpec((rb,) + w.shape[1:], lambda i: (i,) + (0,) * (w.ndim - 1))
    return _hosted(body, comm, name=name, grid=(rows // rb,), in_specs=[blk] * 4, out_specs=(blk,) * 4,
                   out_shape=(jax.ShapeDtypeStruct(w.shape, f32),) * 4, args=(w, g, m, v))


def _small_sum_adamw(all_pkts, w, m, v):
    names = [n for n, _, _ in SMALL_LAYOUT if n in w]
    place = {n: (r0, size) for n, r0, size in SMALL_LAYOUT}
    rows_of = lambda size: -(-size // LANES)
    flat = lambda a: a.reshape(1, -1)
    k = len(names)

    def body(*refs):
        a_ref, ins = refs[0], refs[1:1 + 3 * k]
        g_ref, outs = refs[1 + 3 * k], refs[2 + 3 * k:2 + 7 * k]
        packs = refs[2 + 7 * k:]
        g = a_ref[0]
        for r in range(1, 8):
            g = g + a_ref[r]
        g_ref[...] = g
        for kind in range(3):
            packs[kind][...] = jnp.zeros_like(packs[kind])
            for j, n in enumerate(names):
                r0, size = place[n]
                for r in range(rows_of(size)):
                    width = min(LANES, size - r * LANES)
                    packs[kind][r0 + r:r0 + r + 1, 0:width] = ins[kind * k + j][:, r * LANES:r * LANES + width]
        results = (g,) + _adamw_math(packs[0][...], g, packs[1][...], packs[2][...])
        for kind, val in enumerate(results):
            for j, n in enumerate(names):
                r0, size = place[n]
                for r in range(rows_of(size)):
                    width = min(LANES, size - r * LANES)
                    outs[kind * k + j][:, r * LANES:r * LANES + width] = val[r0 + r:r0 + r + 1, 0:width]

    args = [all_pkts] + [flat(d[n]) for d in (w, m, v) for n in names]
    out_shape = [jax.ShapeDtypeStruct(all_pkts.shape[1:], f32)] + [jax.ShapeDtypeStruct((1, place[n][1]), f32) for _ in range(4) for n in names]
    res = pl.pallas_call(body, name="small_sum_adamw", out_shape=tuple(out_shape),
                         scratch_shapes=[pltpu.VMEM(all_pkts.shape[1:], f32)] * 3)(*args)
    by_kind = [{n: res[1 + kind * k + j].reshape(w[n].shape) for j, n in enumerate(names)} for kind in range(4)]
    return res[0], by_kind


SMALL_LAYOUT = (("ln_in_g", 0, 1024), ("ln_in_b", 8, 1024), ("ln1_g", 16, 1024), ("ln1_b", 24, 1024), ("b_ple_gate", 32, 1024),
                ("ln2_g", 40, 1024), ("ln2_b", 48, 1024), ("gdn_norm_g", 56, 128), ("fox_norm_g", 57, 64), ("a_log", 58, 4),
                ("dt_bias", 59, 4), ("b_f", 60, 8), ("loss", 61, 1))
SMALL_CONV_ROW = 64
SMALL_ROWS = 128


def _pack_small(vals, conv=None):
    rows = []
    nxt = 0
    for n, r0, size in SMALL_LAYOUT:
        assert r0 == nxt
        v = vals[n].reshape(-1).astype(f32) if n in vals else jnp.zeros((size,), f32)
        nrows = -(-size // LANES)
        rows.append(jnp.pad(v, (0, nrows * LANES - size)).reshape(nrows, LANES))
        nxt = r0 + nrows
    rows.append(jnp.zeros((SMALL_CONV_ROW - nxt, LANES), f32))
    conv_rows = CONV_W * GDN_QKV // LANES
    rows.append(jnp.zeros((conv_rows, LANES), f32) if conv is None else conv.reshape(conv_rows, LANES))
    rows.append(jnp.zeros((SMALL_ROWS - SMALL_CONV_ROW - conv_rows, LANES), f32))
    return jnp.concatenate(rows, axis=0)


def _unpack_small(pkt, shapes):
    out = {}
    for n, r0, size in SMALL_LAYOUT:
        if n in shapes:
            nrows = -(-size // LANES)
            out[n] = pkt[r0:r0 + nrows].reshape(-1)[:size].reshape(shapes[n])
    return out


WEIGHTS = ("ln_in_g", "ln_in_b", "w_in", "conv_w", "a_log", "dt_bias", "gdn_norm_g", "b_f", "fox_norm_g", "w_out", "ln1_g", "ln1_b",
           "w_up", "w_down", "w_ple", "w_ple_gate", "b_ple_gate", "ln2_g", "ln2_b")
SMALL_NAMES = tuple(n for n, _, _ in SMALL_LAYOUT if n != "loss")


def kernel(x, p, ln_in_g, ln_in_b, w_in, conv_w, a_log, dt_bias, gdn_norm_g, b_f, fox_norm_g, w_out, ln1_g, ln1_b, w_up, w_down, w_ple, w_ple_gate, b_ple_gate, ln2_g, ln2_b, loss_target, m_ln_in_g, m_ln_in_b, m_w_in, m_conv_w, m_a_log, m_dt_bias, m_gdn_norm_g, m_b_f, m_fox_norm_g, m_w_out, m_ln1_g, m_ln1_b, m_w_up, m_w_down, m_w_ple, m_w_ple_gate, m_b_ple_gate, m_ln2_g, m_ln2_b, v_ln_in_g, v_ln_in_b, v_w_in, v_conv_w, v_a_log, v_dt_bias, v_gdn_norm_g, v_b_f, v_fox_norm_g, v_w_out, v_ln1_g, v_ln1_b, v_w_up, v_w_down, v_w_ple, v_w_ple_gate, v_b_ple_gate, v_ln2_g, v_ln2_b):
    given = dict(locals())
    w = {n: given[n] for n in WEIGHTS}
    m = {n: given["m_" + n] for n in WEIGHTS}
    v = {n: given["v_" + n] for n in WEIGHTS}
    xi, yi, ci = _mesh_pos()
    q = 2 * xi + yi

    def slot_buffer(val, dtype, slots=N_CHIPS, slot=q, rows=None):
        rows = val.shape[0] if rows is None else rows
        return lax.dynamic_update_slice(lax.empty((slots, rows) + val.shape[1:], dtype), val.astype(dtype)[None], (slot, 0, 0))

    shard_cols = D_IN // N_CHIPS
    conv_rows = CONV_W * GDN_QKV // N_CHIPS // LANES
    conv_pkt = jnp.pad(w["conv_w"][0].reshape(-1, LANES), ((0, CONV_PKT_ROWS - conv_rows), (0, 0)))
    ln_in_out, (w_in4, conv_all) = _ln_in(x[0], _row(w["ln_in_g"]), _row(w["ln_in_b"]),
                                          _gather_now([slot_buffer(w["w_in"][0].T, bf16, rows=W_IN_ROWS)], [slot_buffer(conv_pkt, f32)]))
    conv_full = jnp.concatenate([conv_all[d, :conv_rows].reshape(CONV_W, GDN_QKV // N_CHIPS) for d in range(N_CHIPS)], axis=1)
    wi = jnp.concatenate([w_in4[d, :shard_cols] for d in range(N_CHIPS)], axis=0)
    w_cat = jnp.concatenate([wi[:OFF_BETA], wi[OFF_FOX:OFF_F], wi[OFF_BETA:OFF_FOX], wi[OFF_F:],
                             jnp.zeros((D_CAT - D_IN, D_MODEL), bf16)], axis=0)

    small = {n: w[n] for n in SMALL_NAMES}
    qc = jnp.stack([q, ci]).astype(jnp.int32)
    tail_state = {}

    def pairs_phase(gc):
        g_in = jnp.concatenate([gc[:OFF_BETA], gc[SEG_SMALL:SEG_SMALL + 8], gc[SEG_FOX:SEG_SMALL], gc[SEG_SMALL + 8:SEG_SMALL + 16]], axis=0)
        g_in4 = jnp.stack([jnp.pad(g_in[d * shard_cols:(d + 1) * shard_cols], ((0, W_IN_ROWS - shard_cols), (0, 0))) for d in range(N_CHIPS)])

        def took(moved):
            own, sent = _add_pair(g_in4, moved[0], qc, "add_pair_w_in")
            tail_state.update(own=own, sent=[sent], landing=_landing([sent]))
        return _exchange_pairs([g_in4]), took

    grad_x, _, g_conv, g_late, small_g = _device_grads(
        x[0], p[0, 0], loss_target[0], small, w_cat, conv_full, [slot_buffer(w[n][0], bf16) for n in LATE], qc, tail=(pairs_phase, None),
        ln_in_out=ln_in_out)
    packets = _gather_packets(slot_buffer(_pack_small(small_g, g_conv), f32, 8, 4 * xi + 2 * yi + ci))
    (b2,), (small_all,) = _comm_only([_exchange_chips(tail_state["sent"], tail_state["landing"]), packets], "exchange_chips_w_in")
    (g_late["w_in"],), = _comm_only([_share_halves([_add_chips(tail_state["own"], b2, qc, "add_chips_w_in")])], "share_w_in")

    grads, delta, new_m, new_v = {}, {}, {}, {}
    for n, g in g_late.items():
        if n == "w_in":
            as_stored = lambda a: jnp.transpose(a, (2, 0, 1))
            outs, _ = _adamw(as_stored(w[n]), g[:shard_cols].reshape(shard_cols, 1, D_MODEL), as_stored(m[n]), as_stored(v[n]), "adamw_" + n)
            grads[n], delta[n], new_m[n], new_v[n] = (jnp.transpose(a, (1, 2, 0)) for a in outs)
        else:
            outs, _ = _adamw(w[n][0], g, m[n][0], v[n][0], "adamw_" + n)
            grads[n], delta[n], new_m[n], new_v[n] = (a.reshape(w[n].shape) for a in outs)
    pick = lambda d: {n: d[n] for n in SMALL_NAMES}
    g_pkt, by_kind = _small_sum_adamw(small_all, pick(w), pick(m), pick(v))
    for dst, vals in zip((grads, delta, new_m, new_v), by_kind):
        dst.update(vals)
    conv_rows_all = CONV_W * GDN_QKV // LANES
    conv_g_full = g_pkt[SMALL_CONV_ROW:SMALL_CONV_ROW + conv_rows_all].reshape(CONV_W, GDN_QKV)
    conv_g = lax.dynamic_slice_in_dim(conv_g_full, q * (GDN_QKV // N_CHIPS), GDN_QKV // N_CHIPS, axis=1)
    outs, _ = _adamw(w["conv_w"][0], conv_g, m["conv_w"][0], v["conv_w"][0], "adamw_conv_w")
    grads["conv_w"], delta["conv_w"], new_m["conv_w"], new_v["conv_w"] = (a.reshape(w["conv_w"].shape) for a in outs)
    loss = g_pkt[61, 0]
    return (loss, grad_x[None], *[grads[n] for n in WEIGHTS], *[delta[n] for n in WEIGHTS],
            *[new_m[n] for n in WEIGHTS], *[new_v[n] for n in WEIGHTS])
```

```python
import functools

import jax
import jax.numpy as jnp
from jax import lax
from jax.experimental import pallas as pl
from jax.experimental.pallas import tpu as pltpu

f32 = jnp.float32
bf16 = jnp.bfloat16
HI = lax.Precision.HIGHEST
MESH = pl.DeviceIdType.MESH

D_MODEL = 1024
CHUNK = 64
GDN_HEADS = 4
GDN_DK = 128
FOX_HEADS = 8
FOX_DH = 64
CONV_W = 4
D_FF = 4096
D_PLE = 256
LN_EPS = 1e-5
NORM_EPS = 1e-6
ALPHA = 2.0 ** 0.25
GDN_QKV = 1536
OFF_Z = 1536
OFF_BETA = 2048
OFF_FOX = 2056
OFF_F = 3592
D_IN = 3600
ADAM_LR = 0.001
ADAM_B1 = 0.9
ADAM_B2 = 0.999
ADAM_EPS = 1e-08
ADAM_WD = 0.01
ADAM_STEP = 10

SEG_FOX = 2048
SEG_SMALL = 3584
D_CAT = 3840
LANES = 128
TOK_BLK = 256
FOX_BQ = 256
VMEM_LIMIT = 56 * 1024 * 1024
NEG = -1e30

N_CHIPS = 4
W_IN_ROWS = 928


def _params(sem=None, **kw):
    return pltpu.CompilerParams(dimension_semantics=sem, vmem_limit_bytes=VMEM_LIMIT, **kw)


def _sigmoid(x):
    return 1.0 / (1.0 + jnp.exp(-x))


def _softplus(x):
    return jnp.maximum(x, 0.0) + jnp.log(1.0 + jnp.exp(-jnp.abs(x)))


def _ln_fwd(x, g, b):
    mu = jnp.mean(x, -1, keepdims=True)
    xc = x - mu
    var = jnp.mean(xc * xc, -1, keepdims=True)
    rstd = lax.rsqrt(var + LN_EPS)
    xhat = xc * rstd
    return xhat * g + b, xhat, rstd


def _ln_bwd(dy, xhat, rstd, g):
    dxh = dy * g
    m1 = jnp.mean(dxh, -1, keepdims=True)
    m2 = jnp.mean(dxh * xhat, -1, keepdims=True)
    return rstd * (dxh - m1 - xhat * m2)


def _dot(a, b, prec=HI):
    return jnp.dot(a, b, precision=prec, preferred_element_type=f32)


def _dot_nt(a, b, prec=HI):
    return lax.dot_general(a, b, (((1,), (1,)), ((), ())), precision=prec, preferred_element_type=f32)


def _dot_tn(a, b, prec=HI):
    return lax.dot_general(a, b, (((0,), (0,)), ((), ())), precision=prec, preferred_element_type=f32)


def _bdot(a, b):
    return _dot(a.astype(bf16), b.astype(bf16), None)


def _bdot_nt(a, b):
    return _dot_nt(a.astype(bf16), b.astype(bf16), None)


def _bdot_tn(a, b):
    return _dot_tn(a.astype(bf16), b.astype(bf16), None)


def _lane(shape):
    return lax.broadcasted_iota(jnp.int32, shape, len(shape) - 1)


def _mm(a, b, mode, tm, tn, name, out_dtype=f32, epi=None, extra=None, shards=1, comm=None):
    if mode == "nn":
        (m, k), n = a.shape, b.shape[-1] * shards
    elif mode == "nt":
        (m, k), n = a.shape, b.shape[-2]
    else:
        (k, m), n = a.shape, b.shape[1]
    assert m % tm == 0 and n % tn == 0, (name, m, n, tm, tn)
    per = (n // shards) // tn
    assert mode == "nt" or per * tn * shards == n, (name, n, tn, shards)
    nc = 512 if tn % 512 == 0 else (256 if tn % 256 == 0 else 128)
    ks = k // shards

    def body(a_ref, b_ref, *rest):
        for n0 in range(0, tn, nc):
            if mode == "nn":
                acc = jnp.dot(a_ref[...], b_ref[:, n0:n0 + nc], preferred_element_type=f32)
            elif mode == "nt" and shards > 1:
                acc = jnp.zeros((tm, nc), f32)
                for d in range(shards):
                    acc = acc + lax.dot_general(a_ref[:, d * ks:(d + 1) * ks], b_ref[d, n0:n0 + nc, :], (((1,), (1,)), ((), ())),
                                                preferred_element_type=f32)
            elif mode == "nt":
                acc = lax.dot_general(a_ref[...], b_ref[n0:n0 + nc, :], (((1,), (1,)), ((), ())), preferred_element_type=f32)
            else:
                acc = lax.dot_general(a_ref[...], b_ref[:, n0:n0 + nc], (((0,), (0,)), ((), ())), preferred_element_type=f32)
            if epi == "relu2":
                relu_ref, act_ref = rest
                r = jnp.maximum(acc, 0.0)
                relu_ref[:, n0:n0 + nc] = r.astype(bf16)
                act_ref[:, n0:n0 + nc] = (r * r).astype(bf16)
            elif epi == "relu2_bwd":
                relu_ref, o_ref = rest
                o_ref[:, n0:n0 + nc] = (acc * (2.0 * relu_ref[:, n0:n0 + nc].astype(f32))).astype(bf16)
            else:
                (o_ref,) = rest
                o_ref[:, n0:n0 + nc] = acc.astype(out_dtype)

    if mode == "tn":
        a_spec = pl.BlockSpec((k, tm), lambda j, i: (0, i))
    else:
        a_spec = pl.BlockSpec((tm, k), lambda j, i: (i, 0))
    if mode == "nt" and shards > 1:
        b_spec = pl.BlockSpec((shards, tn, ks), lambda j, i: (0, j, 0))
    elif mode == "nt":
        b_spec = pl.BlockSpec((tn, k), lambda j, i: (j, 0))
    elif mode == "nn" and shards > 1:
        b_spec = pl.BlockSpec((None, k, tn), lambda j, i: (j // per, 0, j % per))
    else:
        b_spec = pl.BlockSpec((k, tn), lambda j, i: (0, j))
    o_spec = pl.BlockSpec((tm, tn), lambda j, i: (i, j))
    in_specs = [a_spec, b_spec]
    args = [a, b]
    if epi == "relu2":
        out_shape = (jax.ShapeDtypeStruct((m, n), bf16), jax.ShapeDtypeStruct((m, n), bf16))
        out_specs = (o_spec, o_spec)
    elif epi == "relu2_bwd":
        in_specs.append(o_spec)
        args.append(extra)
        out_shape = jax.ShapeDtypeStruct((m, n), bf16)
        out_specs = o_spec
    elif mode == "tn" and shards > 1:
        out_shape = jax.ShapeDtypeStruct((shards, m, n // shards), out_dtype)
        out_specs = pl.BlockSpec((None, tm, tn), lambda j, i: (j // per, i, j % per))
    else:
        out_shape = jax.ShapeDtypeStruct((m, n), out_dtype)
        out_specs = o_spec
    single = not isinstance(out_shape, tuple)
    res, moved = _hosted(body, comm, name=name, grid=(n // tn, m // tm), in_specs=in_specs,
                         out_specs=(out_specs,) if single else out_specs, out_shape=(out_shape,) if single else out_shape, args=args)
    res = res[0] if single else res
    return res if comm is None else (res, moved)


def _row_spec(width, col=0):
    return pl.BlockSpec((TOK_BLK, width), lambda i: (i, col))


def _vec_spec(rows, width):
    return pl.BlockSpec((rows, width), lambda i: (0, 0))


def _ln_in(x, g, b, comm=None):
    t, d = x.shape

    def body(x_ref, g_ref, b_ref, h_ref, hb_ref):
        h, _, _ = _ln_fwd(x_ref[...], g_ref[...], b_ref[...])
        h_ref[...] = h
        hb_ref[...] = h.astype(bf16)

    return _hosted(
        body, comm, name="ln_in", grid=(t // TOK_BLK,),
        in_specs=[_row_spec(d), _vec_spec(1, d), _vec_spec(1, d)],
        out_specs=(_row_spec(d), _row_spec(d)),
        out_shape=(jax.ShapeDtypeStruct((t, d), f32), jax.ShapeDtypeStruct((t, d), bf16)),
        args=(x, g, b))


def _attn_post(o_gdn, proj, o_fox, g_gdn, g_fox2, comm=None):
    t = o_gdn.shape[0]

    def body(og_ref, z_ref, of_ref, gg_ref, gf_ref, out_ref):
        for h in range(GDN_HEADS):
            sl = slice(h * LANES, (h + 1) * LANES)
            og = og_ref[:, sl]
            z = z_ref[:, sl]
            r = lax.rsqrt(jnp.mean(og * og, -1, keepdims=True) + NORM_EPS)
            out_ref[:, sl] = (og * r * gg_ref[...] * (z * _sigmoid(z))).astype(bf16)
        lo = _lane((TOK_BLK, LANES)) < FOX_DH
        for pr in range(FOX_HEADS // 2):
            sl = slice(pr * LANES, (pr + 1) * LANES)
            of = of_ref[:, sl]
            sq = of * of
            s0 = jnp.sum(jnp.where(lo, sq, 0.0), -1, keepdims=True)
            s1 = jnp.sum(jnp.where(lo, 0.0, sq), -1, keepdims=True)
            r = lax.rsqrt(jnp.where(lo, s0, s1) * (1.0 / FOX_DH) + NORM_EPS)
            out_ref[:, 512 + pr * LANES:512 + (pr + 1) * LANES] = (of * r * gf_ref[...]).astype(bf16)

    return _hosted(
        body, comm, name="attn_post", grid=(t // TOK_BLK,),
        in_specs=[_row_spec(512), _row_spec(512, OFF_Z // 512), _row_spec(512), _vec_spec(1, LANES), _vec_spec(1, LANES)],
        out_specs=(_row_spec(D_MODEL),),
        out_shape=(jax.ShapeDtypeStruct((t, D_MODEL), bf16),),
        args=(o_gdn, proj, o_fox, g_gdn, g_fox2))


def _attn_post_bwd(dr1b, w_out, o_gdn, proj, o_fox, g_gdn, g_fox2):
    t = o_gdn.shape[0]

    def body(dr_ref, wo_ref, og_ref, z_ref, of_ref, gg_ref, gf_ref, dog_ref, dz_ref, dof_ref, pg_ref):
        i = pl.program_id(0)

        @pl.when(i == 0)
        def _():
            pg_ref[...] = jnp.zeros_like(pg_ref)

        da = _dot_nt(dr_ref[...], wo_ref[...], None)
        dgg = jnp.zeros((1, LANES), f32)
        for h in range(GDN_HEADS):
            sl = slice(h * LANES, (h + 1) * LANES)
            og = og_ref[:, sl]
            z = z_ref[:, sl]
            dout = da[:, sl]
            g = gg_ref[...]
            r = lax.rsqrt(jnp.mean(og * og, -1, keepdims=True) + NORM_EPS)
            sg = _sigmoid(z)
            silu = z * sg
            ng = og * r * g
            dng = dout * silu
            dz_ref[:, sl] = (dout * ng * (sg * (1.0 + z * (1.0 - sg)))).astype(bf16)
            dgg = dgg + jnp.sum(dng * og * r, 0, keepdims=True)
            gd = dng * g
            dog_ref[:, sl] = r * gd - og * (r * r * r) * jnp.mean(og * gd, -1, keepdims=True)
        pg_ref[0:1, :] += dgg
        lo = _lane((TOK_BLK, LANES)) < FOX_DH
        dgf = jnp.zeros((1, LANES), f32)
        for pr in range(FOX_HEADS // 2):
            sl = slice(pr * LANES, (pr + 1) * LANES)
            of = of_ref[:, sl]
            dout = da[:, 512 + pr * LANES:512 + (pr + 1) * LANES]
            g = gf_ref[...]
            sq = of * of
            s0 = jnp.sum(jnp.where(lo, sq, 0.0), -1, keepdims=True)
            s1 = jnp.sum(jnp.where(lo, 0.0, sq), -1, keepdims=True)
            r = lax.rsqrt(jnp.where(lo, s0, s1) * (1.0 / FOX_DH) + NORM_EPS)
            dgf = dgf + jnp.sum(dout * of * r, 0, keepdims=True)
            gd = dout * g
            xg = of * gd
            m0 = jnp.sum(jnp.where(lo, xg, 0.0), -1, keepdims=True)
            m1 = jnp.sum(jnp.where(lo, 0.0, xg), -1, keepdims=True)
            dof_ref[:, sl] = r * gd - of * (r * r * r) * (jnp.where(lo, m0, m1) * (1.0 / FOX_DH))
        pg_ref[1:2, :] += dgf

    return pl.pallas_call(
        body, name="attn_post_bwd", grid=(t // TOK_BLK,),
        in_specs=_product_specs(dr1b, w_out) + [_row_spec(512), _row_spec(512, OFF_Z // 512), _row_spec(512), _vec_spec(1, LANES), _vec_spec(1, LANES)],
        out_specs=(_row_spec(512), _row_spec(512), _row_spec(512), _vec_spec(8, LANES)),
        out_shape=(jax.ShapeDtypeStruct((t, 512), f32), jax.ShapeDtypeStruct((t, 512), bf16),
                   jax.ShapeDtypeStruct((t, 512), f32), jax.ShapeDtypeStruct((8, LANES), f32)),
        compiler_params=_params(("arbitrary",)),
    )(dr1b, w_out, o_gdn, proj, o_fox, g_gdn, g_fox2)


def _product_specs(lhs, rhs):
    return [_row_spec(lhs.shape[1]), pl.BlockSpec(rhs.shape, lambda i: (0, 0))]


def _ln1(h0, lhs, rhs, g, b, comm=None):
    t, d = h0.shape

    def body(h0_ref, lhs_ref, rhs_ref, g_ref, b_ref, h_ref, hb_ref, xh_ref, rs_ref):
        mix = jnp.dot(lhs_ref[...], rhs_ref[...], preferred_element_type=f32)
        h, xhat, rstd = _ln_fwd(ALPHA * h0_ref[...] + mix, g_ref[...], b_ref[...])
        h_ref[...] = h
        hb_ref[...] = h.astype(bf16)
        xh_ref[...] = xhat
        rs_ref[...] = jnp.broadcast_to(rstd, rs_ref.shape)

    return _hosted(
        body, comm, name="ln1", grid=(t // TOK_BLK,),
        in_specs=[_row_spec(d)] + _product_specs(lhs, rhs) + [_vec_spec(1, d), _vec_spec(1, d)],
        out_specs=(_row_spec(d), _row_spec(d), _row_spec(d), _row_spec(LANES)),
        out_shape=(jax.ShapeDtypeStruct((t, d), f32), jax.ShapeDtypeStruct((t, d), bf16),
                   jax.ShapeDtypeStruct((t, d), f32), jax.ShapeDtypeStruct((t, LANES), f32)),
        args=(h0, lhs, rhs, g, b))


def _ln2_loss(h1, lhs, rhs, pb, w_ple, gp, b_gate, g, b, target):
    t, d = h1.shape

    def body(h1_ref, lhs_ref, rhs_ref, pb_ref, wp_ref, gp_ref, bg_ref, g_ref, b_ref, t_ref, dr_ref, drb_ref, dpe_ref, dgp_ref, pg_ref):
        i = pl.program_id(0)

        @pl.when(i == 0)
        def _():
            pg_ref[...] = jnp.zeros_like(pg_ref)

        ff = jnp.dot(lhs_ref[...], rhs_ref[...], preferred_element_type=f32)
        sig = _sigmoid(gp_ref[...] + bg_ref[...])
        pe = jnp.concatenate([jnp.dot(pb_ref[...], wp_ref[s], preferred_element_type=f32) for s in range(w_ple.shape[0])], axis=1)
        r2 = ALPHA * h1_ref[...] + ff + pe * sig
        y, xhat, rstd = _ln_fwd(r2, g_ref[...], b_ref[...])
        err = y - t_ref[...]
        dy = err * (1.0 / d)
        dr = _ln_bwd(dy, xhat, rstd, g_ref[...])
        dr_ref[...] = dr
        drb_ref[...] = dr.astype(bf16)
        dpe_ref[...] = (dr * sig).astype(bf16)
        dgp = dr * pe * sig * (1.0 - sig)
        dgp_ref[...] = dgp.astype(bf16)
        pg_ref[0:1, :] += jnp.sum(dy * xhat, 0, keepdims=True)
        pg_ref[1:2, :] += jnp.sum(dy, 0, keepdims=True)
        pg_ref[2:3, :] += jnp.sum(dgp, 0, keepdims=True)
        pg_ref[3:4, :] += 0.5 * jnp.sum(jnp.mean(err * err, -1, keepdims=True), 0, keepdims=True)

    return pl.pallas_call(
        body, name="ln2_loss", grid=(t // TOK_BLK,),
        in_specs=[_row_spec(d)] + _product_specs(lhs, rhs) + [_row_spec(pb.shape[1]), pl.BlockSpec(w_ple.shape, lambda i: (0, 0, 0)), _row_spec(d)]
        + [_vec_spec(1, d)] * 3 + [_row_spec(d)],
        out_specs=(_row_spec(d), _row_spec(d), _row_spec(d), _row_spec(d), _vec_spec(8, d)),
        out_shape=(jax.ShapeDtypeStruct((t, d), f32), jax.ShapeDtypeStruct((t, d), bf16), jax.ShapeDtypeStruct((t, d), bf16),
                   jax.ShapeDtypeStruct((t, d), bf16), jax.ShapeDtypeStruct((8, d), f32)),
        compiler_params=_params(("arbitrary",)),
    )(h1, lhs, rhs, pb, w_ple, gp, b_gate, g, b, target)


def _ln1_bwd(dr2, dup, w_up, dgp, w_gate, xhat, rstd, g):
    t, d = dr2.shape
    ks = w_up.shape[2]

    def body(dr2_ref, dup_ref, wup_ref, dgp_ref, wg_ref, xh_ref, rs_ref, g_ref, dr_ref, drb_ref, pg_ref):
        i = pl.program_id(0)

        @pl.when(i == 0)
        def _():
            pg_ref[...] = jnp.zeros_like(pg_ref)

        dh = ALPHA * dr2_ref[...] + _dot_nt(dgp_ref[...], wg_ref[...], None)
        for s in range(w_up.shape[0]):
            dh = dh + _dot_nt(dup_ref[:, s * ks:(s + 1) * ks], wup_ref[s], None)
        xhat = xh_ref[...]
        dr = _ln_bwd(dh, xhat, rs_ref[:, 0:1], g_ref[...])
        dr_ref[...] = dr
        drb_ref[...] = dr.astype(bf16)
        pg_ref[0:1, :] += jnp.sum(dh * xhat, 0, keepdims=True)
        pg_ref[1:2, :] += jnp.sum(dh, 0, keepdims=True)

    return pl.pallas_call(
        body, name="ln1_bwd", grid=(t // TOK_BLK,),
        in_specs=[_row_spec(d), _row_spec(dup.shape[1]), pl.BlockSpec(w_up.shape, lambda i: (0, 0, 0))] + _product_specs(dgp, w_gate)
        + [_row_spec(d), _row_spec(LANES), _vec_spec(1, d)],
        out_specs=(_row_spec(d), _row_spec(d), _vec_spec(8, d)),
        out_shape=(jax.ShapeDtypeStruct((t, d), f32), jax.ShapeDtypeStruct((t, d), bf16), jax.ShapeDtypeStruct((8, d), f32)),
        compiler_params=_params(("arbitrary",)),
    )(dr2, dup, w_up, dgp, w_gate, xhat, rstd, g)


def _ln_in_bwd(x, dr1, dmm, g, comm=None):
    t, d = x.shape

    def body(x_ref, dr1_ref, dmm_ref, g_ref, dx_ref, pg_ref):
        i = pl.program_id(0)

        @pl.when(i == 0)
        def _():
            pg_ref[...] = jnp.zeros_like(pg_ref)

        dh = ALPHA * dr1_ref[...] + dmm_ref[...]
        _, xhat, rstd = _ln_fwd(x_ref[...], g_ref[...], 0.0)
        dx_ref[...] = _ln_bwd(dh, xhat, rstd, g_ref[...])
        pg_ref[0:1, :] += jnp.sum(dh * xhat, 0, keepdims=True)
        pg_ref[1:2, :] += jnp.sum(dh, 0, keepdims=True)

    return _hosted(
        body, comm, name="ln_in_bwd", grid=(t // TOK_BLK,),
        in_specs=[_row_spec(d)] * 3 + [_vec_spec(1, d)],
        out_specs=(_row_spec(d), _vec_spec(8, d)),
        out_shape=(jax.ShapeDtypeStruct((t, d), f32), jax.ShapeDtypeStruct((8, d), f32)),
        args=(x, dr1, dmm, g))


def _tri(n, upper=False, strict=False):
    r = lax.broadcasted_iota(jnp.int32, (n, n), 0)
    c = lax.broadcasted_iota(jnp.int32, (n, n), 1)
    if upper:
        m = (c > r) if strict else (c >= r)
    else:
        m = (c < r) if strict else (c <= r)
    return jnp.where(m, 1.0, 0.0).astype(f32)


def _gate_values(x, bias, alog, lane):
    z = x + bias
    return jnp.where(lane < 4, _sigmoid(z), jnp.where(lane < 8, -jnp.exp(alog) * _softplus(z), jnp.where(lane < 16, -_softplus(-z), 0.0)))


def _gates(proj, bias_row, alog_row):
    t = proj.shape[0]
    nch = t // CHUNK

    def body(x_ref, bias_ref, alog_ref, gates_ref, gcum_ref, gcumt_ref):
        lane = _lane((t, LANES))
        gates = _gate_values(x_ref[...], bias_ref[...], alog_ref[...], lane)
        gates_ref[...] = gates
        g3 = gates.reshape(nch, CHUNK, LANES)
        tri = jnp.broadcast_to(_tri(CHUNK)[None], (nch, CHUNK, CHUNK))
        loc = jnp.einsum("bij,bjk->bik", tri, g3, precision=HI, preferred_element_type=f32)
        tot = jnp.sum(g3, axis=1)
        offs = _dot(_tri(nch, strict=True), tot)
        glob = loc + offs[:, None, :]
        lane3 = _lane((nch, CHUNK, LANES))
        gcum = jnp.where(lane3 < 4, g3, jnp.where(lane3 < 8, loc, glob)).reshape(t, LANES)
        gcum_ref[...] = gcum
        gcumt_ref[...] = gcum.T

    return pl.pallas_call(
        body, name="gates", grid=(1,),
        in_specs=[pl.BlockSpec((t, LANES), lambda i: (0, SEG_SMALL // LANES)), _vec_spec(1, LANES), _vec_spec(1, LANES)],
        out_specs=(pl.BlockSpec((t, LANES), lambda i: (0, 0)), pl.BlockSpec((t, LANES), lambda i: (0, 0)),
                   pl.BlockSpec((LANES, t), lambda i: (0, 0))),
        out_shape=(jax.ShapeDtypeStruct((t, LANES), f32), jax.ShapeDtypeStruct((t, LANES), f32), jax.ShapeDtypeStruct((LANES, t), f32)),
        compiler_params=_params(("arbitrary",)),
    )(proj, bias_row, alog_row)


def _gates_bwd(proj, bias_row, alog_row, gates, dgates, dccol, dct):
    t = proj.shape[0]
    nch = t // CHUNK

    def body(x_ref, bias_ref, alog_ref, gates_ref, dg_ref, dcc_ref, dct_ref, dx_ref, pg_ref):
        lane = _lane((t, LANES))
        d = dg_ref[...] + dcc_ref[...] + dct_ref[...].T
        d3 = d.reshape(nch, CHUNK, LANES)
        tri = jnp.broadcast_to(_tri(CHUNK, upper=True)[None], (nch, CHUNK, CHUNK))
        loc = jnp.einsum("bij,bjk->bik", tri, d3, precision=HI, preferred_element_type=f32)
        tot = jnp.sum(d3, axis=1)
        offs = _dot(_tri(nch, upper=True, strict=True), tot)
        glob = loc + offs[:, None, :]
        lane3 = _lane((nch, CHUNK, LANES))
        dpre = jnp.where(lane3 < 4, d3, jnp.where(lane3 < 8, loc, glob)).reshape(t, LANES)
        z = x_ref[...] + bias_ref[...]
        sg = _sigmoid(z)
        dx = jnp.where(lane < 4, dpre * sg * (1.0 - sg),
                       jnp.where(lane < 8, dpre * (-jnp.exp(alog_ref[...])) * sg, jnp.where(lane < 16, dpre * (1.0 - sg), 0.0)))
        dx_ref[...] = dx.astype(bf16)
        pg_ref[...] = jnp.zeros_like(pg_ref)
        pg_ref[0:1, :] = jnp.sum(dx, 0, keepdims=True)
        pg_ref[1:2, :] = jnp.sum(jnp.where((lane >= 4) & (lane < 8), dpre * gates_ref[...], 0.0), 0, keepdims=True)

    full = pl.BlockSpec((t, LANES), lambda i: (0, 0))
    return pl.pallas_call(
        body, name="gates_bwd", grid=(1,),
        in_specs=[pl.BlockSpec((t, LANES), lambda i: (0, SEG_SMALL // LANES)), _vec_spec(1, LANES), _vec_spec(1, LANES),
                  full, full, full, pl.BlockSpec((LANES, t), lambda i: (0, 0))],
        out_specs=(full, _vec_spec(8, LANES)),
        out_shape=(jax.ShapeDtypeStruct((t, LANES), bf16), jax.ShapeDtypeStruct((8, LANES), f32)),
        compiler_params=_params(("arbitrary",)),
    )(proj, bias_row, alog_row, gates, dgates, dccol, dct)


def _conv_act(u, cw, row, t):
    c = cw[3:4, :] * u
    for jj in range(CONV_W - 1):
        sh = CONV_W - 1 - jj
        c = c + cw[jj:jj + 1, :] * jnp.where(row >= sh, pltpu.roll(u, sh, axis=0), 0.0)
    return c


def _gdn_conv(proj, conv_w, comm=None):
    t = proj.shape[0]
    nblk = GDN_QKV // LANES

    def body(u_ref, cw_ref, c_ref, y_ref):
        j = pl.program_id(0)
        row = lax.broadcasted_iota(jnp.int32, (t, LANES), 0)
        c = _conv_act(u_ref[...], cw_ref[...], row, t)
        c_ref[...] = c
        s = c * _sigmoid(c)
        r = lax.rsqrt(jnp.sum(s * s, -1, keepdims=True) + NORM_EPS)
        scale = jnp.where(j < GDN_HEADS, GDN_DK ** -0.5, 1.0)
        y_ref[...] = jnp.where(j < 2 * GDN_HEADS, s * (r * scale), s)

    blk = pl.BlockSpec((t, LANES), lambda j: (0, j))
    return _hosted(
        body, comm, name="gdn_conv", grid=(nblk,),
        in_specs=[blk, pl.BlockSpec((CONV_W, LANES), lambda j: (0, j))],
        out_specs=(blk, blk),
        out_shape=(jax.ShapeDtypeStruct((t, GDN_QKV), f32), jax.ShapeDtypeStruct((t, GDN_QKV), f32)),
        args=(proj, conv_w))


def _gdn_conv_bwd(proj, conv_w, c, dy, comm=None):
    t = proj.shape[0]
    nblk = GDN_QKV // LANES

    def body(u_ref, cw_ref, c_ref, dy_ref, du_ref, dcw_ref):
        j = pl.program_id(0)
        row = lax.broadcasted_iota(jnp.int32, (t, LANES), 0)
        u = u_ref[...]
        cw = cw_ref[...]
        c = c_ref[...]
        dy = dy_ref[...]
        sg = _sigmoid(c)
        s = c * sg
        r = lax.rsqrt(jnp.sum(s * s, -1, keepdims=True) + NORM_EPS)
        n = s * r
        scale = jnp.where(j < GDN_HEADS, GDN_DK ** -0.5, 1.0)
        dn = dy * scale
        ds = jnp.where(j < 2 * GDN_HEADS, r * (dn - n * jnp.sum(dn * n, -1, keepdims=True)), dy)
        dc = ds * (sg * (1.0 + c * (1.0 - sg)))
        du = cw[3:4, :] * dc
        dcw_ref[...] = jnp.zeros_like(dcw_ref)
        dcw_ref[3:4, :] = jnp.sum(dc * u, 0, keepdims=True)
        for jj in range(CONV_W - 1):
            sh = CONV_W - 1 - jj
            du = du + cw[jj:jj + 1, :] * jnp.where(row < t - sh, pltpu.roll(dc, t - sh, axis=0), 0.0)
            dcw_ref[jj:jj + 1, :] = jnp.sum(dc * jnp.where(row >= sh, pltpu.roll(u, sh, axis=0), 0.0), 0, keepdims=True)
        du_ref[...] = du.astype(bf16)

    blk = pl.BlockSpec((t, LANES), lambda j: (0, j))
    return _hosted(
        body, comm, name="gdn_conv_bwd", grid=(nblk,),
        in_specs=[blk, pl.BlockSpec((CONV_W, LANES), lambda j: (0, j)), blk, blk],
        out_specs=(blk, pl.BlockSpec((8, LANES), lambda j: (0, j))),
        out_shape=(jax.ShapeDtypeStruct((t, GDN_QKV), bf16), jax.ShapeDtypeStruct((8, GDN_QKV), f32)),
        args=(proj, conv_w, c, dy))


def _chunk_masks():
    r = lax.broadcasted_iota(jnp.int32, (CHUNK, CHUNK), 0)
    c = lax.broadcasted_iota(jnp.int32, (CHUNK, CHUNK), 1)
    return r >= c, r > c, r == c


def _col_to_row(col, eye):
    return jnp.sum(jnp.where(eye, col, 0.0), axis=0, keepdims=True)


def _row_to_col(row, eye):
    return jnp.sum(jnp.where(eye, row, 0.0), axis=1, keepdims=True)


NN = (((1,), (0,)), ((), ()))
NT = (((1,), (1,)), ((), ()))
TN = (((0,), (0,)), ((), ()))
GDN_GROUP = 4


def _mx(a, b, dims=NN, passes=1):
    d = lambda p, q: lax.dot_general(p, q, dims, preferred_element_type=f32)
    ah, bh = a.astype(bf16), b.astype(bf16)
    if passes == 1:
        return d(ah, bh)
    al = (a - ah.astype(f32)).astype(bf16)
    bl = (b - bh.astype(f32)).astype(bf16)
    return d(ah, bh) + (d(ah, bl) + d(al, bh))


def _gdn_decay(gam, masks):
    causal, _, eye = masks
    return jnp.exp(jnp.where(causal, gam - _col_to_row(gam, eye), NEG))


def _gdn_local(y, gcum, comm=None):
    t = y.shape[0]
    nch = t // CHUNK
    rows_blk = GDN_GROUP * CHUNK

    def body(y_ref, g_ref, u_ref, w_ref, qk_ref, tinv_ref):
        masks = _chunk_masks()
        _, strict, eye = masks
        ids = [(j, h) for j in range(GDN_GROUP) for h in range(GDN_HEADS)]
        rs = lambda j: slice(j * CHUNK, (j + 1) * CHUNK)
        col = lambda base, h: slice(base + h * LANES, base + (h + 1) * LANES)
        kn = [y_ref[rs(j), col(512, h)] for j, h in ids]
        beta = [g_ref[rs(j), h:h + 1] for j, h in ids]
        gam = [g_ref[rs(j), 4 + h:5 + h] for j, h in ids]
        dec = [_gdn_decay(g, masks) for g in gam]
        x = [-jnp.where(strict, _mx(k, k, NT) * d * b, 0.0) for k, d, b in zip(kn, dec, beta)]
        tinv = [jnp.where(eye, 1.0, 0.0) + a for a in x]
        for _ in range(5):
            x = [_mx(a, a, NN, 3) for a in x]
            tinv = [t_ + _mx(t_, a, NN, 3) for t_, a in zip(tinv, x)]
        for (j, h), t_, k, d, b, g in zip(ids, tinv, kn, dec, beta, gam):
            u_ref[rs(j), col(0, h)] = _mx(t_, b * y_ref[rs(j), col(1024, h)])
            w_ref[rs(j), col(0, h)] = _mx(t_, (b * jnp.exp(g)) * k)
            qk_ref[j, h] = _mx(y_ref[rs(j), col(0, h)], k, NT) * d
            tinv_ref[j, h] = t_

    mat = pl.BlockSpec((GDN_GROUP, GDN_HEADS, CHUNK, CHUNK), lambda n: (n, 0, 0, 0))
    return _hosted(
        body, comm, name="gdn_local", grid=(nch // GDN_GROUP,),
        in_specs=[pl.BlockSpec((rows_blk, GDN_QKV), lambda n: (n, 0)), pl.BlockSpec((rows_blk, LANES), lambda n: (n, 0))],
        out_specs=(pl.BlockSpec((rows_blk, 512), lambda n: (n, 0)), pl.BlockSpec((rows_blk, 512), lambda n: (n, 0)), mat, mat),
        out_shape=(jax.ShapeDtypeStruct((t, 512), f32), jax.ShapeDtypeStruct((t, 512), f32),
                   jax.ShapeDtypeStruct((nch, GDN_HEADS, CHUNK, CHUNK), f32), jax.ShapeDtypeStruct((nch, GDN_HEADS, CHUNK, CHUNK), f32)),
        args=(y, gcum))


def _gdn_fwd(y, gcum, u, w, qk, comm=None):
    t = y.shape[0]
    nch = t // CHUNK

    def body(y_ref, g_ref, u_ref, w_ref, qk_ref, o_ref, sall_ref, s_ref):
        @pl.when(pl.program_id(0) == 0)
        def _():
            s_ref[...] = jnp.zeros_like(s_ref)

        heads = range(GDN_HEADS)
        sl = [slice(h * LANES, (h + 1) * LANES) for h in heads]
        gam = [g_ref[:, 4 + h:5 + h] for h in heads]
        gam_last = [g[CHUNK - 1:CHUNK, :] for g in gam]
        s = [s_ref[h] for h in heads]
        for h in heads:
            sall_ref[0, h] = s[h]
        ws = [_mx(w_ref[:, sl[h]], s[h]) for h in heads]
        qs = [_mx(y_ref[:, sl[h]] * jnp.exp(gam[h]), s[h]) for h in heads]
        vn = [u_ref[:, sl[h]] - ws[h] for h in heads]
        av = [_mx(qk_ref[0, h], vn[h]) for h in heads]
        kv = [_mx(y_ref[:, 512 + h * LANES:512 + (h + 1) * LANES] * jnp.exp(gam_last[h] - gam[h]), vn[h], TN) for h in heads]
        for h in heads:
            o_ref[:, sl[h]] = qs[h] + av[h]
            s_ref[h] = jnp.exp(gam_last[h]) * s[h] + kv[h]

    row = lambda width: pl.BlockSpec((CHUNK, width), lambda n: (n, 0))
    return _hosted(
        body, comm, name="gdn_fwd", grid=(nch,),
        in_specs=[row(GDN_QKV), row(LANES), row(512), row(512), pl.BlockSpec((1, GDN_HEADS, CHUNK, CHUNK), lambda n: (n, 0, 0, 0))],
        out_specs=(row(512), pl.BlockSpec((1, GDN_HEADS, LANES, LANES), lambda n: (n, 0, 0, 0))),
        out_shape=(jax.ShapeDtypeStruct((t, 512), f32), jax.ShapeDtypeStruct((nch, GDN_HEADS, LANES, LANES), f32)),
        scratch_shapes=[pltpu.VMEM((GDN_HEADS, LANES, LANES), f32)],
        args=(y, gcum, u, w, qk))


def _gdn_bwd(y, gcum, u_all, w_all, qk_all, tinv_all, sall, do, comm=None):
    t = y.shape[0]
    nch = t // CHUNK

    def body(y_ref, g_ref, u_ref, w_ref, qk_ref, tinv_ref, sall_ref, do_ref, dy_ref, dg_ref, ds_ref):
        @pl.when(pl.program_id(0) == 0)
        def _():
            ds_ref[...] = jnp.zeros_like(ds_ref)

        masks = _chunk_masks()
        causal, strict, eye = masks
        lane = _lane((CHUNK, LANES))
        row = lax.broadcasted_iota(jnp.int32, (CHUNK, 1), 0)
        heads = range(GDN_HEADS)
        each = lambda f, *ls: [f(*a) for a in zip(*ls)]
        rsum = lambda a: jnp.sum(a, axis=1, keepdims=True)
        sl = [slice(h * LANES, (h + 1) * LANES) for h in heads]
        qn = [y_ref[:, sl[h]] for h in heads]
        kn = [y_ref[:, 512 + h * LANES:512 + (h + 1) * LANES] for h in heads]
        v = [y_ref[:, 1024 + h * LANES:1024 + (h + 1) * LANES] for h in heads]
        beta = [g_ref[:, h:h + 1] for h in heads]
        gam = [g_ref[:, 4 + h:5 + h] for h in heads]
        gam_last = [g[CHUNK - 1:CHUNK, :] for g in gam]
        dec = [_gdn_decay(g, masks) for g in gam]
        e = [jnp.exp(g) for g in gam]
        f = each(lambda gl_, g: jnp.exp(gl_ - g), gam_last, gam)
        gl = [jnp.exp(g) for g in gam_last]
        u = [u_ref[:, sl[h]] for h in heads]
        w = [w_ref[:, sl[h]] for h in heads]
        qk = [qk_ref[0, h] for h in heads]
        tinv = [tinv_ref[0, h] for h in heads]
        s = [sall_ref[0, h] for h in heads]
        dsn = [ds_ref[h] for h in heads]
        d_o = [do_ref[:, sl[h]] for h in heads]
        qd = each(lambda a, b: a * b, qn, e)
        kd = each(lambda a, b: a * b, kn, f)
        ws = each(_mx, w, s)
        kds = each(_mx, kd, dsn)
        qkdo = each(lambda a, b: _mx(a, b, TN), qk, d_o)
        dqd = each(lambda a, b: _mx(a, b, NT), d_o, s)
        qddo = each(lambda a, b: _mx(a, b, TN), qd, d_o)
        kkd = each(lambda k, d: _mx(k, k, NT) * d, kn, dec)
        vn = each(lambda a, b: a - b, u, ws)
        dvn = each(lambda a, b: a + b, qkdo, kds)
        dqk = each(lambda a, b: jnp.where(causal, _mx(a, b, NT), 0.0), d_o, vn)
        dkd = each(lambda a, b: _mx(a, b, NT), vn, dsn)
        dw = each(lambda a, b: -_mx(a, b, NT), dvn, s)
        wdvn = each(lambda a, b: _mx(a, b, TN), w, dvn)
        dgl = each(lambda a, b: jnp.sum(rsum(a * b), axis=0, keepdims=True), dsn, s)
        for h in heads:
            ds_ref[h] = qddo[h] - wdvn[h] + gl[h] * dsn[h]
        dru = each(lambda a, b: _mx(a, b, TN), tinv, dvn)
        drw = each(lambda a, b: _mx(a, b, TN), tinv, dw)
        dqkr = each(lambda a, b: a * b, dqk, dec)
        dq1 = each(_mx, dqkr, kn)
        dk1 = each(lambda a, b: _mx(a, b, TN), dqkr, qn)
        dnu = each(lambda a, b: _mx(a, b, NT), dru, u)
        dnw = each(lambda a, b: _mx(a, b, NT), drw, w)
        dn = each(lambda a, b: jnp.where(strict, -(a + b), 0.0), dnu, dnw)
        dkk = each(lambda a, b, d: a * b * d, dn, beta, dec)
        dk2 = each(_mx, dkk, kn)
        dk3 = each(lambda a, b: _mx(a, b, TN), dkk, kn)
        dgates = jnp.zeros((CHUNK, LANES), f32)
        for h in heads:
            drw_k = rsum(drw[h] * kn[h])
            dbeta = rsum(dru[h] * v[h]) + e[h] * drw_k + rsum(dn[h] * kkd[h])
            m = dn[h] * (kkd[h] * beta[h]) + dqk[h] * qk[h]
            de = beta[h] * drw_k + rsum(dqd[h] * qn[h])
            df = rsum(dkd[h] * kn[h])
            dgam = rsum(m) - _row_to_col(jnp.sum(m, axis=0, keepdims=True), eye) + de * e[h] - df * f[h]
            dgam_last = jnp.sum(df * f[h], axis=0, keepdims=True) + dgl[h] * gl[h]
            dgam = dgam + jnp.where(row == CHUNK - 1, dgam_last, 0.0)
            dy_ref[:, sl[h]] = dq1[h] + dqd[h] * e[h]
            dy_ref[:, 512 + h * LANES:512 + (h + 1) * LANES] = (beta[h] * e[h]) * drw[h] + dk2[h] + dk3[h] + dk1[h] + dkd[h] * f[h]
            dy_ref[:, 1024 + h * LANES:1024 + (h + 1) * LANES] = beta[h] * dru[h]
            dgates = dgates + jnp.where(lane == h, dbeta, 0.0) + jnp.where(lane == 4 + h, dgam, 0.0)
        dg_ref[...] = dgates

    rev = lambda width: pl.BlockSpec((CHUNK, width), lambda n: (nch - 1 - n, 0))
    mat = lambda d: pl.BlockSpec((1, GDN_HEADS, d, d), lambda n: (nch - 1 - n, 0, 0, 0))
    return _hosted(
        body, comm, name="gdn_bwd", grid=(nch,),
        in_specs=[rev(GDN_QKV), rev(LANES), rev(512), rev(512), mat(CHUNK), mat(CHUNK), mat(LANES), rev(512)],
        out_specs=(rev(GDN_QKV), rev(LANES)),
        out_shape=(jax.ShapeDtypeStruct((t, GDN_QKV), f32), jax.ShapeDtypeStruct((t, LANES), f32)),
        scratch_shapes=[pltpu.VMEM((GDN_HEADS, LANES, LANES), f32)],
        args=(y, gcum, u_all, w_all, qk_all, tinv_all, sall, do))


FOX_CLASSES = 4


def _fox_groups(t):
    nq = t // FOX_BQ
    ncls = min(FOX_CLASSES, nq)
    per = nq // ncls
    return [(g * per, per, (g + 1) * per * FOX_BQ) for g in range(ncls)]


def _fox_scores(q_ref, k_ref, gcum_ref, gcumt_ref, h, i, keys):
    pr = h // 2
    lo = (h % 2) * FOX_DH
    lane = _lane((FOX_BQ, LANES))
    mask = (lane >= lo) & (lane < lo + FOX_DH)
    qm = jnp.where(mask, q_ref[:, pr * LANES:(pr + 1) * LANES], 0.0).astype(bf16)
    kp = k_ref[:, pr * LANES:(pr + 1) * LANES].astype(bf16)
    s = _dot_nt(qm, kp, None) * (FOX_DH ** -0.5)
    s = s + gcum_ref[:, 8 + h:9 + h] - gcumt_ref[8 + h:9 + h, :]
    rows = i * FOX_BQ + lax.broadcasted_iota(jnp.int32, (FOX_BQ, keys), 0)
    cols = lax.broadcasted_iota(jnp.int32, (FOX_BQ, keys), 1)
    return jnp.where(cols <= rows, s, NEG), mask, qm, kp


def _fox_fwd(proj, gcum, gcumt, ride=None):
    c0 = SEG_FOX // 512

    def group_call(q0, nq, keys, comm):
        def body(q_ref, k_ref, v_ref, gcum_ref, gcumt_ref, o_ref, lse_ref):
            i = q0 + pl.program_id(0)
            lane = _lane((FOX_BQ, LANES))
            lse_all = jnp.zeros((FOX_BQ, LANES), f32)
            for pr in range(FOX_HEADS // 2):
                vp = v_ref[:, pr * LANES:(pr + 1) * LANES].astype(bf16)
                o_pair = jnp.zeros((FOX_BQ, LANES), f32)
                for h in (2 * pr, 2 * pr + 1):
                    s, mask, _, _ = _fox_scores(q_ref, k_ref, gcum_ref, gcumt_ref, h, i, keys)
                    m = jnp.max(s, axis=1, keepdims=True)
                    p = jnp.exp(s - m)
                    l = jnp.sum(p, axis=1, keepdims=True)
                    o_h = _dot((p * (1.0 / l)).astype(bf16), vp, None)
                    o_pair = jnp.where(mask, o_h, o_pair)
                    lse_all = jnp.where(lane == h, m + jnp.log(l), lse_all)
                o_ref[:, pr * LANES:(pr + 1) * LANES] = o_pair
            lse_ref[...] = lse_all

        seen = lambda col: pl.BlockSpec((keys, 512), lambda i: (0, col))
        return _hosted(
            body, comm, name=f"fox_fwd_{keys}", grid=(nq,),
            in_specs=[pl.BlockSpec((FOX_BQ, 512), lambda i: (q0 + i, c0)), seen(c0 + 1), seen(c0 + 2),
                      pl.BlockSpec((FOX_BQ, LANES), lambda i: (q0 + i, 0)), pl.BlockSpec((LANES, keys), lambda i: (0, 0))],
            out_specs=(pl.BlockSpec((FOX_BQ, 512), lambda i: (i, 0)), pl.BlockSpec((FOX_BQ, LANES), lambda i: (i, 0))),
            out_shape=(jax.ShapeDtypeStruct((nq * FOX_BQ, 512), f32), jax.ShapeDtypeStruct((nq * FOX_BQ, LANES), f32)),
            args=(proj, proj, proj, gcum, gcumt))

    parts = []
    for n, g in enumerate(_fox_groups(proj.shape[0])):
        hook = ride(n) if ride else None
        part, moved = group_call(*g, hook[0] if hook else None)
        parts.append(part)
        if hook:
            hook[1](moved)
    return jnp.concatenate([o for o, _ in parts], axis=0), jnp.concatenate([l for _, l in parts], axis=0)


def _fox_bwd(proj, gcum, gcumt, o, lse, do, ride=None):
    t = proj.shape[0]
    c0 = SEG_FOX // 512

    def group_call(q0, nq, keys, acc, comm):
        first = acc is None

        def body(q_ref, k_ref, v_ref, gcum_ref, gcumt_ref, o_ref, lse_ref, do_ref, *rest):
            dq_ref, dk_ref, dv_ref, dcc_ref, dct_ref = rest[-5:]
            j = pl.program_id(0)
            i = q0 + j

            @pl.when(j == 0)
            def _():
                if first:
                    dk_ref[...] = jnp.zeros_like(dk_ref)
                    dv_ref[...] = jnp.zeros_like(dv_ref)
                    dct_ref[...] = jnp.zeros_like(dct_ref)
                else:
                    dk_ref[...], dv_ref[...], dct_ref[...] = rest[0][...], rest[1][...], rest[2][...]

            lane = _lane((FOX_BQ, LANES))
            dcc = jnp.zeros((FOX_BQ, LANES), f32)
            scale = FOX_DH ** -0.5
            for pr in range(FOX_HEADS // 2):
                sl = slice(pr * LANES, (pr + 1) * LANES)
                vp = v_ref[:, sl].astype(bf16)
                dq_pair = jnp.zeros((FOX_BQ, LANES), f32)
                for h in (2 * pr, 2 * pr + 1):
                    s, mask, qm, kp = _fox_scores(q_ref, k_ref, gcum_ref, gcumt_ref, h, i, keys)
                    p = jnp.exp(s - lse_ref[:, h:h + 1])
                    dom = jnp.where(mask, do_ref[:, sl], 0.0)
                    delta = jnp.sum(dom * o_ref[:, sl], axis=1, keepdims=True)
                    domb = dom.astype(bf16)
                    ds = p * (_dot_nt(domb, vp, None) - delta)
                    dsb = ds.astype(bf16)
                    dv_ref[:, sl] += _dot_tn(p.astype(bf16), domb, None)
                    dk_ref[:, sl] += _dot_tn(dsb, qm, None) * scale
                    dq_pair = jnp.where(mask, _dot(dsb, kp, None) * scale, dq_pair)
                    dcc = jnp.where(lane == 8 + h, jnp.sum(ds, axis=1, keepdims=True), dcc)
                    dct_ref[8 + h:9 + h, :] += -jnp.sum(ds, axis=0, keepdims=True)
                dq_ref[:, sl] = dq_pair.astype(bf16)
            dcc_ref[...] = dcc

        qblk = lambda col: pl.BlockSpec((FOX_BQ, 512), lambda i: (q0 + i, col))
        oblk = pl.BlockSpec((FOX_BQ, 512), lambda i: (i, 0))
        seen = lambda col: pl.BlockSpec((keys, 512), lambda i: (0, col))
        rblk = pl.BlockSpec((FOX_BQ, LANES), lambda i: (q0 + i, 0))
        seen_t = pl.BlockSpec((LANES, keys), lambda i: (0, 0))
        in_specs = [qblk(c0), seen(c0 + 1), seen(c0 + 2), rblk, seen_t, qblk(0), rblk, qblk(0)]
        args = [proj, proj, proj, gcum, gcumt, o, lse, do]
        aliases = {}
        if not first:
            in_specs += [seen(0), seen(0), seen_t]
            args += list(acc)
            aliases = {8: 1, 9: 2, 10: 4}
        return _hosted(
            body, comm, name=f"fox_bwd_{keys}", grid=(nq,), in_specs=in_specs,
            out_specs=(oblk, seen(0), seen(0), pl.BlockSpec((FOX_BQ, LANES), lambda i: (i, 0)), seen_t),
            out_shape=(jax.ShapeDtypeStruct((nq * FOX_BQ, 512), bf16), jax.ShapeDtypeStruct((t, 512), f32), jax.ShapeDtypeStruct((t, 512), f32),
                       jax.ShapeDtypeStruct((nq * FOX_BQ, LANES), f32), jax.ShapeDtypeStruct((LANES, t), f32)),
            aliases=aliases, args=args)

    acc, dqs, dccs = None, [], []
    for n, g in enumerate(reversed(_fox_groups(t))):
        hook = ride(n) if ride else None
        (dq, dk, dv, dcc, dct), moved = group_call(*g, acc, hook[0] if hook else None)
        if hook:
            hook[1](moved)
        acc = (dk, dv, dct)
        dqs.insert(0, dq)
        dccs.insert(0, dcc)
    return jnp.concatenate(dqs, axis=0), acc[0], acc[1], jnp.concatenate(dccs, axis=0), acc[2]


def _row(v, width=None):
    v = v.reshape(1, -1).astype(f32)
    if width is not None and v.shape[1] < width:
        v = jnp.pad(v, ((0, 0), (0, width - v.shape[1])))
    return v


LATE = ("w_out", "w_up", "w_ple_gate", "w_ple", "w_down")


def _device_grads(x, p, target, small, w_cat, conv_w, late, qc=None, tail=None, ln_in_out=None):
    z4 = jnp.zeros((4,), f32)
    bias_row = _row(jnp.concatenate([z4, small["dt_bias"].reshape(-1), small["b_f"].reshape(-1)]), LANES)
    alog_row = _row(jnp.concatenate([z4, small["a_log"].reshape(-1)]), LANES)
    g_gdn = _row(small["gdn_norm_g"])
    g_fox2 = _row(jnp.tile(small["fox_norm_g"].reshape(-1), 2))
    pb = p.astype(bf16)
    late = list(late)
    comm = qc is not None

    h0, h0b = ln_in_out if ln_in_out is not None else _ln_in(x, _row(small["ln_in_g"]), _row(small["ln_in_b"]))[0]
    proj = _mm(h0b, w_cat, "nt", 512, D_CAT, "mm_proj")
    gates, gcum, gcumt = _gates(proj, bias_row, alog_row)
    w_down_pieces = [(4, 0, 1)]

    def gather(phase, pieces):
        if not comm or not pieces:
            return None, lambda moved: None
        touched = sorted({i for i, _, _ in pieces})

        def took(moved):
            for i, buf in zip(touched, moved):
                late[i] = buf
        return phase([late[i] for i in touched], [(touched.index(i), k, n) for i, k, n in pieces]), took

    over, on = _gather_chips, _gather_pass_on
    cm, took = gather(over, [(0, 0, 1), (3, 0, 1)])
    (conv_c, qkv_n), moved = _gdn_conv(proj, conv_w, cm)
    took(moved)
    cm, took = gather(over, [(1, 0, 2)])
    (gu, gw, gqk, gtinv), moved = _gdn_local(qkv_n, gcum, cm)
    took(moved)
    cm, took = gather(over, [(1, 1, 2)])
    (o_gdn, sall), moved = _gdn_fwd(qkv_n, gcum, gu, gw, gqk, cm)
    took(moved)
    fox_plan = [(over, []), (on, [(0, 0, 1), (3, 0, 1), (1, 0, 2), (1, 1, 2)]), (over, [(2, 0, 1)]), (over, [(4, 0, 2)])]
    assert not comm or len(_fox_groups(x.shape[0])) == len(fox_plan)
    o_fox, lse = _fox_fwd(proj, gcum, gcumt, (lambda n: gather(*fox_plan[n])) if comm else None)
    cm, took = gather(on, [(2, 0, 1)])
    (attn,), moved = _attn_post(o_gdn, proj, o_fox, g_gdn, g_fox2, cm)
    took(moved)
    w_out = late[0].reshape(D_MODEL, D_MODEL)
    (h1, h1b, xhat1, rstd1), _ = _ln1(h0, attn, w_out, _row(small["ln1_g"]), _row(small["ln1_b"]))
    w_up, w_ple = late[1], late[3]
    cm, took = gather(over, [(4, 1, 2)])
    up_act = _mm(h1b, w_up, "nn", 512, 1024, "mm_up", epi="relu2", shards=N_CHIPS, comm=cm)
    if cm:
        up_act, moved = up_act
        took(moved)
    up, act = up_act
    w_gate = late[2].reshape(D_MODEL, D_MODEL)
    cm, took = gather(on, w_down_pieces)
    gp = _mm(h1b, w_gate, "nn", 512, D_MODEL, "mm_gate", comm=cm)
    if cm:
        gp, moved = gp
        took(moved)
    w_down = late[4].reshape(D_FF, D_MODEL)
    dr2, dr2b, dpe, dgp, pg2 = _ln2_loss(h1, act, w_down, pb, w_ple, gp, _row(small["b_ple_gate"]), _row(small["ln2_g"]),
                                         _row(small["ln2_b"]), target)

    dup = _mm(dr2b, w_down, "nt", 512, 2048, "mm_dact", epi="relu2_bwd", extra=up)
    g_down = _mm(act, dr2b, "tn", 1024, D_MODEL, "mm_gdown")
    g_up = _mm(h1b, dup, "tn", 1024, 1024, "mm_gup", shards=N_CHIPS)
    g_gate = _mm(h1b, dgp, "tn", 1024, D_MODEL, "mm_ggate")
    g_ple = _mm(pb, dpe, "tn", D_PLE, D_MODEL // N_CHIPS, "mm_gple", shards=N_CHIPS)
    dr1, dr1b, pg1 = _ln1_bwd(dr2, dup, w_up, dgp, w_gate, xhat1, rstd1, _row(small["ln1_g"]))
    g_out = _mm(attn, dr1b, "tn", 1024, D_MODEL, "mm_gout")
    do_gdn, dz, do_fox, pga = _attn_post_bwd(dr1b, w_out, o_gdn, proj, o_fox, g_gdn, g_fox2)
    g_late = [g.reshape((N_CHIPS, -1, g.shape[-1])) for g in (g_out, g_up, g_gate, g_ple, g_down)]
    chip_plan = [[(4, 0, 2), (4, 1, 2), (0, 0, 1)], [(1, 0, 2)], [(1, 1, 2)], [(2, 0, 1), (3, 0, 1)]]
    state = {}

    def to_sibling():
        def took(moved):
            sums = [_add_pair(g, b1, qc, "add_pair_" + n) for g, b1, n in zip(g_late, moved, LATE)]
            state.update(own=[a for a, _ in sums], sent=[ab for _, ab in sums], landing=_landing([ab for _, ab in sums]))
        return _exchange_pairs(g_late), took

    def to_chips(pieces):
        if not comm:
            return None, lambda moved: None
        return _exchange_chips(state["sent"], state["landing"], pieces), lambda moved: state.update(landing=list(moved))

    def gdn_backward():
        cm, took = to_chips(chip_plan[0])
        state["gdn"], moved = _gdn_bwd(qkv_n, gcum, gu, gw, gqk, gtinv, sall, do_gdn, cm)
        took(moved)

    def fox_ride(n):
        if n == 0:
            return to_sibling()
        if n == 1:
            gdn_backward()
        return to_chips(chip_plan[n])

    assert not comm or len(_fox_groups(x.shape[0])) == len(chip_plan)
    dfq, dfk, dfv, dccol, dct = _fox_bwd(proj, gcum, gcumt, o_fox, lse, do_fox, fox_ride if comm else None)
    if not comm:
        gdn_backward()
    dqkv_n, dgates = state["gdn"]
    dsmall, pgg = _gates_bwd(proj, bias_row, alog_row, gates, dgates, dccol, dct)
    cm = None
    if comm:
        cm = _share_halves([_add_chips(a, b2, qc, "add_chips_" + n) for a, b2, n in zip(state["own"], state["landing"], LATE)])
    (du, g_conv8), reduced = _gdn_conv_bwd(proj, conv_w, conv_c, dqkv_n, cm)
    if comm:
        g_late = list(reduced)
    t = x.shape[0]
    dproj = jnp.concatenate([du, dz, dfq, dfk.astype(bf16), dfv.astype(bf16), dsmall, jnp.zeros((t, D_CAT - SEG_SMALL - LANES), bf16)], axis=1)
    g_cat = _mm(dproj, h0b, "tn", 1280, D_MODEL, "mm_gcat")
    cm, took = tail[0](g_cat) if tail else (None, None)
    dh0_mm = _mm(dproj, w_cat, "nn", 512, D_MODEL, "mm_dh0", comm=cm)
    if cm:
        dh0_mm, moved = dh0_mm
        took(moved)
    cm, took = tail[1]() if tail and tail[1] else (None, None)
    (grad_x, pg0), moved = _ln_in_bwd(x, dr1, dh0_mm, _row(small["ln_in_g"]), cm)
    if cm:
        took(moved)

    g_fox = pga[1, :FOX_DH] + pga[1, FOX_DH:]
    small_grads = dict(
        ln_in_g=pg0[0], ln_in_b=pg0[1], ln1_g=pg1[0], ln1_b=pg1[1], b_ple_gate=pg2[2], ln2_g=pg2[0], ln2_b=pg2[1],
        gdn_norm_g=pga[0], fox_norm_g=g_fox, a_log=pgg[1, 4:8], dt_bias=pgg[0, 4:8], b_f=pgg[0, 8:16], loss=pg2[3, 0:1])
    return grad_x, g_cat, g_conv8[:CONV_W], dict(zip(LATE, g_late)), small_grads


ANY = pl.BlockSpec(memory_space=pl.ANY)
CONV_PKT_ROWS = 16


def _mesh_pos():
    return lax.axis_index("x"), lax.axis_index("y"), lax.axis_index("c")


def _other_chips(x, y):
    return [(1 - x, y), (x, 1 - y), (1 - x, 1 - y)]


def _rcopy(src, dst, send_sem, recv_sem, dev):
    return pltpu.make_async_remote_copy(src_ref=src, dst_ref=dst, send_sem=send_sem, recv_sem=recv_sem,
                                        device_id=dev, device_id_type=MESH)


class _Comm:
    def __init__(self, ins, outs, aliases, n_sems, start, finish):
        self.ins, self.outs, self.aliases, self.n_sems, self.start, self.finish = list(ins), list(outs), dict(aliases), n_sems, start, finish


def _hosted(body, comm, *, name, grid, in_specs, out_specs, out_shape, args, scratch_shapes=(), aliases=None):
    n_in, n_out, n_sc = len(in_specs), len(out_specs), len(scratch_shapes)
    k, ko = (len(comm.ins), len(comm.outs)) if comm else (0, 0)

    def kernel_body(*refs):
        o0 = n_in + k
        s0 = o0 + n_out + ko
        if comm:
            cins, couts, (ssem, rsem) = refs[n_in:o0], refs[o0 + n_out:s0], refs[s0 + n_sc:]
            step = pl.program_id(0)
            for d in range(1, len(grid)):
                step = step * grid[d] + pl.program_id(d)

            @pl.when(step == 0)
            def _():
                comm.start(cins, couts, ssem, rsem)

        body(*refs[:n_in], *refs[o0:o0 + n_out], *refs[s0:s0 + n_sc])
        if comm:
            last = 1
            for n in grid:
                last *= n

            @pl.when(step == last - 1)
            def _():
                comm.finish(cins, couts, ssem, rsem)

    io_aliases = dict(aliases or {})
    scratch = list(scratch_shapes)
    if comm:
        io_aliases.update({n_in + i: n_out + j for i, j in comm.aliases.items()})
        scratch += [pltpu.SemaphoreType.DMA((comm.n_sems,)), pltpu.SemaphoreType.DMA((comm.n_sems,))]
    res = pl.pallas_call(
        kernel_body, name=name, grid=grid, in_specs=list(in_specs) + [ANY] * k, out_specs=tuple(out_specs) + (ANY,) * ko,
        out_shape=tuple(out_shape) + tuple(comm.outs if comm else ()), scratch_shapes=scratch, input_output_aliases=io_aliases,
        compiler_params=_params(("arbitrary",) * len(grid)),
    )(*args, *(comm.ins if comm else ()))
    return tuple(res[:n_out]), tuple(res[n_out:])


def _comm_only(phases, name):
    n_in = sum(len(p.ins) for p in phases)

    def body(*refs):
        n_out = sum(len(p.outs) for p in phases)
        sems = refs[n_in + n_out:]
        i0, o0 = 0, n_in
        for j, p in enumerate(phases):
            cins, couts = refs[i0:i0 + len(p.ins)], refs[o0:o0 + len(p.outs)]
            p.start(cins, couts, sems[2 * j], sems[2 * j + 1])
            p.finish(cins, couts, sems[2 * j], sems[2 * j + 1])
            i0 += len(p.ins)
            o0 += len(p.outs)

    aliases, i0, o0 = {}, 0, 0
    for p in phases:
        aliases.update({i0 + i: o0 + j for i, j in p.aliases.items()})
        i0 += len(p.ins)
        o0 += len(p.outs)
    outs = [o for p in phases for o in p.outs]
    res = pl.pallas_call(
        body, name=name, out_shape=tuple(outs), in_specs=[ANY] * n_in, out_specs=(ANY,) * len(outs), input_output_aliases=aliases,
        scratch_shapes=[pltpu.SemaphoreType.DMA((p.n_sems,)) for p in phases for _ in range(2)],
    )(*[a for p in phases for a in p.ins])
    split, o0 = [], 0
    for p in phases:
        split.append(tuple(res[o0:o0 + len(p.outs)]))
        o0 += len(p.outs)
    return split


def _like(arrays):
    return [jax.ShapeDtypeStruct(a.shape, a.dtype) for a in arrays]


def _half(ref, slot, hf, piece=(0, 1)):
    k, n = piece
    rows = ref.shape[1] // 2 // n
    return ref.at[slot, pl.ds((hf * n + k) * rows, rows)]


def _whole_halves(arrays):
    return [(i, 0, 1) for i in range(len(arrays))]


def _gather_chips(bufs, pieces=None, whole=False, base=0):
    nw = len(bufs)
    pieces = _whole_halves(bufs) if pieces is None else pieces
    part = (lambda ref, slot, c, piece: ref.at[slot]) if whole else _half

    def copies(couts):
        x, y, c = _mesh_pos()
        q = 2 * x + y
        for j, (i, k, n) in enumerate(pieces):
            for kc, chip in enumerate(_other_chips(x, y)):
                mine, theirs = part(couts[i], q, c, (k, n)), part(couts[i], 2 * chip[0] + chip[1], c, (k, n))
                yield base + j * 3 + kc, mine, theirs, (*chip, c)

    def start(cins, couts, ssem, rsem):
        for s, mine, _, dev in copies(couts):
            _rcopy(mine, mine, ssem.at[s], rsem.at[s], dev).start()

    def finish(cins, couts, ssem, rsem):
        for s, _, theirs, dev in copies(couts):
            _rcopy(theirs, theirs, ssem.at[s], rsem.at[s], dev).wait_recv()
        for s, mine, _, dev in copies(couts):
            _rcopy(mine, mine, ssem.at[s], rsem.at[s], dev).wait_send()

    return _Comm(bufs, _like(bufs), {i: i for i in range(nw)}, 3 * len(pieces), start, finish)


def _gather_pass_on(bufs, pieces=None, base=0):
    nw = len(bufs)
    pieces = _whole_halves(bufs) if pieces is None else pieces

    def copies(couts):
        x, y, c = _mesh_pos()
        for j, (i, k, n) in enumerate(pieces):
            for kc, chip in enumerate(_other_chips(x, y)):
                slot = 2 * chip[0] + chip[1]
                yield base + j * 3 + kc, _half(couts[i], slot, c, (k, n)), _half(couts[i], slot, 1 - c, (k, n)), (x, y, 1 - c)

    def start(cins, couts, ssem, rsem):
        for s, landed, _, sib in copies(couts):
            _rcopy(landed, landed, ssem.at[s], rsem.at[s], sib).start()

    def finish(cins, couts, ssem, rsem):
        for s, _, passed, sib in copies(couts):
            _rcopy(passed, passed, ssem.at[s], rsem.at[s], sib).wait_recv()
        for s, landed, _, sib in copies(couts):
            _rcopy(landed, landed, ssem.at[s], rsem.at[s], sib).wait_send()

    return _Comm(bufs, _like(bufs), {i: i for i in range(nw)}, 3 * len(pieces), start, finish)


def _gather_now(bufs, packets):
    nb = len(bufs)
    over, on, pk = _gather_chips(bufs), _gather_pass_on(bufs, base=3 * nb), _gather_chips(packets, whole=True, base=6 * nb)

    def start(cins, couts, ssem, rsem):
        over.start(cins[:nb], couts[:nb], ssem, rsem)
        pk.start(cins[nb:], couts[nb:], ssem, rsem)

    def finish(cins, couts, ssem, rsem):
        over.finish(cins[:nb], couts[:nb], ssem, rsem)
        on.start(cins[:nb], couts[:nb], ssem, rsem)
        on.finish(cins[:nb], couts[:nb], ssem, rsem)
        pk.finish(cins[nb:], couts[nb:], ssem, rsem)

    every = list(bufs) + list(packets)
    return _Comm(every, _like(every), {i: i for i in range(len(every))}, 6 * nb + 3 * len(packets), start, finish)


def _exchange_pairs(gs):
    nw = len(gs)

    def copies(cins, couts):
        x, y, c = _mesh_pos()
        for i in range(nw):
            for d in range(N_CHIPS):
                yield i * N_CHIPS + d, _half(cins[i], d, 1 - c), couts[i].at[d], (x, y, 1 - c)

    def start(cins, couts, ssem, rsem):
        for s, src, dst, sib in copies(cins, couts):
            _rcopy(src, dst, ssem.at[s], rsem.at[s], sib).start()

    def finish(cins, couts, ssem, rsem):
        for s, src, dst, sib in copies(cins, couts):
            _rcopy(src, dst, ssem.at[s], rsem.at[s], sib).wait_recv()
        for s, src, dst, sib in copies(cins, couts):
            _rcopy(src, dst, ssem.at[s], rsem.at[s], sib).wait_send()

    outs = [jax.ShapeDtypeStruct((N_CHIPS, g.shape[1] // 2, g.shape[2]), g.dtype) for g in gs]
    return _Comm(gs, outs, {}, N_CHIPS * nw, start, finish)


def _gather_packets(small):
    def peers():
        x, y, c = _mesh_pos()
        for r in range(1, 8):
            fx, fy, fc = (r >> 2) & 1, (r >> 1) & 1, r & 1
            yield r - 1, (1 - x if fx else x, 1 - y if fy else y, 1 - c if fc else c)

    def start(cins, couts, ssem, rsem):
        x, y, c = _mesh_pos()
        mine = couts[0].at[4 * x + 2 * y + c]
        for s, peer in peers():
            _rcopy(mine, mine, ssem.at[s], rsem.at[s], peer).start()

    def finish(cins, couts, ssem, rsem):
        x, y, c = _mesh_pos()
        mine = couts[0].at[4 * x + 2 * y + c]
        for s, peer in peers():
            theirs = couts[0].at[4 * peer[0] + 2 * peer[1] + peer[2]]
            _rcopy(theirs, theirs, ssem.at[s], rsem.at[s], peer).wait_recv()
        for s, peer in peers():
            _rcopy(mine, mine, ssem.at[s], rsem.at[s], peer).wait_send()

    return _Comm([small], _like([small]), {0: 0}, 7, start, finish)


def _exchange_chips(a4s, b2s, pieces=None):
    nw = len(a4s)
    pieces = _whole_halves(a4s) if pieces is None else pieces

    def copies(cins, couts):
        x, y, c = _mesh_pos()
        for j, (i, k, n) in enumerate(pieces):
            rows = a4s[i].shape[1] // n
            part = pl.ds(k * rows, rows)
            for kc, chip in enumerate(_other_chips(x, y)):
                yield j * 3 + kc, cins[i].at[2 * chip[0] + chip[1], part], couts[i].at[kc, part], (*chip, c)

    def start(cins, couts, ssem, rsem):
        for s, src, dst, dev in copies(cins, couts):
            _rcopy(src, dst, ssem.at[s], rsem.at[s], dev).start()

    def finish(cins, couts, ssem, rsem):
        for s, src, dst, dev in copies(cins, couts):
            _rcopy(src, dst, ssem.at[s], rsem.at[s], dev).wait_recv()
        for s, src, dst, dev in copies(cins, couts):
            _rcopy(src, dst, ssem.at[s], rsem.at[s], dev).wait_send()

    return _Comm(list(a4s) + list(b2s), _like(b2s), {nw + i: i for i in range(nw)}, 3 * len(pieces), start, finish)


def _landing(a4s):
    return [lax.empty((3,) + a.shape[1:], a.dtype) for a in a4s]


def _share_halves(rs):
    nw = len(rs)

    def halves(couts, i, hf):
        rows = rs[i].shape[0] // 2
        return couts[i].at[pl.ds(hf * rows, rows)]

    def start(cins, couts, ssem, rsem):
        x, y, c = _mesh_pos()
        for i in range(nw):
            _rcopy(halves(couts, i, c), halves(couts, i, c), ssem.at[i], rsem.at[i], (x, y, 1 - c)).start()

    def finish(cins, couts, ssem, rsem):
        x, y, c = _mesh_pos()
        for i in range(nw):
            _rcopy(halves(couts, i, 1 - c), halves(couts, i, 1 - c), ssem.at[i], rsem.at[i], (x, y, 1 - c)).wait_recv()
        for i in range(nw):
            _rcopy(halves(couts, i, c), halves(couts, i, c), ssem.at[i], rsem.at[i], (x, y, 1 - c)).wait_send()

    return _Comm(rs, _like(rs), {i: i for i in range(nw)}, nw, start, finish)


ADD_ROWS = 256


def _add_pair(g4, b1, qc_idx, name):
    _, half, cols = b1.shape
    rb = ADD_ROWS if half % ADD_ROWS == 0 else half
    nb = half // rb

    def body(qc_ref, g_ref, b_ref, o_ref, ob_ref):
        a = g_ref[...] + b_ref[...]
        o_ref[...] = a
        ob_ref[...] = a.astype(bf16)

    blk = (1, rb, cols)
    out = pl.BlockSpec(blk, lambda d, i, qc: (d, i, 0))
    return pl.pallas_call(
        body, name=name,
        grid_spec=pltpu.PrefetchScalarGridSpec(
            num_scalar_prefetch=1, grid=(N_CHIPS, nb),
            in_specs=[pl.BlockSpec(blk, lambda d, i, qc: (d, qc[1] * nb + i, 0)), out],
            out_specs=(out, out)),
        out_shape=(jax.ShapeDtypeStruct(b1.shape, f32), jax.ShapeDtypeStruct(b1.shape, bf16)),
        compiler_params=_params(("parallel", "parallel")),
    )(qc_idx, g4, b1)


def _add_chips(a4, b2, qc_idx, name):
    _, half, cols = a4.shape
    rb = ADD_ROWS if half % ADD_ROWS == 0 else half
    nb = half // rb

    def body(qc_ref, a_ref, b_ref, o_ref):
        o_ref[...] = ((a_ref[0] + b_ref[0].astype(f32)) + b_ref[1].astype(f32)) + b_ref[2].astype(f32)

    return pl.pallas_call(
        body, name=name,
        grid_spec=pltpu.PrefetchScalarGridSpec(
            num_scalar_prefetch=1, grid=(nb,),
            in_specs=[pl.BlockSpec((1, rb, cols), lambda i, qc: (qc[0], i, 0)), pl.BlockSpec((3, rb, cols), lambda i, qc: (0, i, 0))],
            out_specs=pl.BlockSpec((rb, cols), lambda i, qc: (qc[1] * nb + i, 0))),
        out_shape=jax.ShapeDtypeStruct((2 * half, cols), f32),
        compiler_params=_params(("parallel",)),
    )(qc_idx, a4, b2)


def _adamw_math(w, g, m, v):
    m = ADAM_B1 * m + (1.0 - ADAM_B1) * g
    v = ADAM_B2 * v + (1.0 - ADAM_B2) * (g * g)
    m_hat = m / (1.0 - ADAM_B1 ** ADAM_STEP)
    v_hat = v / (1.0 - ADAM_B2 ** ADAM_STEP)
    return -ADAM_LR * (m_hat / (jnp.sqrt(v_hat) + ADAM_EPS) + ADAM_WD * w), m, v


def _adamw(w, g, m, v, name, comm=None):
    rows = w.shape[0]
    if w.ndim == 3:
        rb = max(r for r in range(1, ADD_ROWS // 4 + 1) if rows % r == 0)
    else:
        rb = ADD_ROWS if rows % ADD_ROWS == 0 else rows

    def body(w_ref, g_ref, m_ref, v_ref, go_ref, d_ref, mo_ref, vo_ref):
        g = g_ref[...]
        go_ref[...] = g
        d_ref[...], mo_ref[...], vo_ref[...] = _adamw_math(w_ref[...], g, m_ref[...], v_ref[...])

    blk = pl.BlockSpec((rb,) + w.shape[1:], lambda i: (i,) + (0,) * (w.ndim - 1))
    return _hosted(body, comm, name=name, grid=(rows // rb,), in_specs=[blk] * 4, out_specs=(blk,) * 4,
                   out_shape=(jax.ShapeDtypeStruct(w.shape, f32),) * 4, args=(w, g, m, v))


def _small_sum_adamw(all_pkts, w, m, v):
    names = [n for n, _, _ in SMALL_LAYOUT if n in w]
    place = {n: (r0, size) for n, r0, size in SMALL_LAYOUT}
    rows_of = lambda size: -(-size // LANES)
    flat = lambda a: a.reshape(1, -1)
    k = len(names)

    def body(*refs):
        a_ref, ins = refs[0], refs[1:1 + 3 * k]
        g_ref, outs = refs[1 + 3 * k], refs[2 + 3 * k:2 + 7 * k]
        packs = refs[2 + 7 * k:]
        g = a_ref[0]
        for r in range(1, 8):
            g = g + a_ref[r]
        g_ref[...] = g
        for kind in range(3):
            packs[kind][...] = jnp.zeros_like(packs[kind])
            for j, n in enumerate(names):
                r0, size = place[n]
                for r in range(rows_of(size)):
                    width = min(LANES, size - r * LANES)
                    packs[kind][r0 + r:r0 + r + 1, 0:width] = ins[kind * k + j][:, r * LANES:r * LANES + width]
        results = (g,) + _adamw_math(packs[0][...], g, packs[1][...], packs[2][...])
        for kind, val in enumerate(results):
            for j, n in enumerate(names):
                r0, size = place[n]
                for r in range(rows_of(size)):
                    width = min(LANES, size - r * LANES)
                    outs[kind * k + j][:, r * LANES:r * LANES + width] = val[r0 + r:r0 + r + 1, 0:width]

    args = [all_pkts] + [flat(d[n]) for d in (w, m, v) for n in names]
    out_shape = [jax.ShapeDtypeStruct(all_pkts.shape[1:], f32)] + [jax.ShapeDtypeStruct((1, place[n][1]), f32) for _ in range(4) for n in names]
    res = pl.pallas_call(body, name="small_sum_adamw", out_shape=tuple(out_shape),
                         scratch_shapes=[pltpu.VMEM(all_pkts.shape[1:], f32)] * 3)(*args)
    by_kind = [{n: res[1 + kind * k + j].reshape(w[n].shape) for j, n in enumerate(names)} for kind in range(4)]
    return res[0], by_kind


SMALL_LAYOUT = (("ln_in_g", 0, 1024), ("ln_in_b", 8, 1024), ("ln1_g", 16, 1024), ("ln1_b", 24, 1024), ("b_ple_gate", 32, 1024),
                ("ln2_g", 40, 1024), ("ln2_b", 48, 1024), ("gdn_norm_g", 56, 128), ("fox_norm_g", 57, 64), ("a_log", 58, 4),
                ("dt_bias", 59, 4), ("b_f", 60, 8), ("loss", 61, 1))
SMALL_CONV_ROW = 64
SMALL_ROWS = 128


def _pack_small(vals, conv=None):
    rows = []
    nxt = 0
    for n, r0, size in SMALL_LAYOUT:
        assert r0 == nxt
        v = vals[n].reshape(-1).astype(f32) if n in vals else jnp.zeros((size,), f32)
        nrows = -(-size // LANES)
        rows.append(jnp.pad(v, (0, nrows * LANES - size)).reshape(nrows, LANES))
        nxt = r0 + nrows
    rows.append(jnp.zeros((SMALL_CONV_ROW - nxt, LANES), f32))
    conv_rows = CONV_W * GDN_QKV // LANES
    rows.append(jnp.zeros((conv_rows, LANES), f32) if conv is None else conv.reshape(conv_rows, LANES))
    rows.append(jnp.zeros((SMALL_ROWS - SMALL_CONV_ROW - conv_rows, LANES), f32))
    return jnp.concatenate(rows, axis=0)


def _unpack_small(pkt, shapes):
    out = {}
    for n, r0, size in SMALL_LAYOUT:
        if n in shapes:
            nrows = -(-size // LANES)
            out[n] = pkt[r0:r0 + nrows].reshape(-1)[:size].reshape(shapes[n])
    return out


WEIGHTS = ("ln_in_g", "ln_in_b", "w_in", "conv_w", "a_log", "dt_bias", "gdn_norm_g", "b_f", "fox_norm_g", "w_out", "ln1_g", "ln1_b",
           "w_up", "w_down", "w_ple", "w_ple_gate", "b_ple_gate", "ln2_g", "ln2_b")
SMALL_NAMES = tuple(n for n, _, _ in SMALL_LAYOUT if n != "loss")


def kernel(x, p, ln_in_g, ln_in_b, w_in, conv_w, a_log, dt_bias, gdn_norm_g, b_f, fox_norm_g, w_out, ln1_g, ln1_b, w_up, w_down, w_ple, w_ple_gate, b_ple_gate, ln2_g, ln2_b, loss_target, m_ln_in_g, m_ln_in_b, m_w_in, m_conv_w, m_a_log, m_dt_bias, m_gdn_norm_g, m_b_f, m_fox_norm_g, m_w_out, m_ln1_g, m_ln1_b, m_w_up, m_w_down, m_w_ple, m_w_ple_gate, m_b_ple_gate, m_ln2_g, m_ln2_b, v_ln_in_g, v_ln_in_b, v_w_in, v_conv_w, v_a_log, v_dt_bias, v_gdn_norm_g, v_b_f, v_fox_norm_g, v_w_out, v_ln1_g, v_ln1_b, v_w_up, v_w_down, v_w_ple, v_w_ple_gate, v_b_ple_gate, v_ln2_g, v_ln2_b):
    given = dict(locals())
    w = {n: given[n] for n in WEIGHTS}
    m = {n: given["m_" + n] for n in WEIGHTS}
    v = {n: given["v_" + n] for n in WEIGHTS}
    xi, yi, ci = _mesh_pos()
    q = 2 * xi + yi

    def slot_buffer(val, dtype, slots=N_CHIPS, slot=q, rows=None):
        rows = val.shape[0] if rows is None else rows
        return lax.dynamic_update_slice(lax.empty((slots, rows) + val.shape[1:], dtype), val.astype(dtype)[None], (slot, 0, 0))

    shard_cols = D_IN // N_CHIPS
    conv_rows = CONV_W * GDN_QKV // N_CHIPS // LANES
    conv_pkt = jnp.pad(w["conv_w"][0].reshape(-1, LANES), ((0, CONV_PKT_ROWS - conv_rows), (0, 0)))
    ln_in_out, (w_in4, conv_all) = _ln_in(x[0], _row(w["ln_in_g"]), _row(w["ln_in_b"]),
                                          _gather_now([slot_buffer(w["w_in"][0].T, bf16, rows=W_IN_ROWS)], [slot_buffer(conv_pkt, f32)]))
    conv_full = jnp.concatenate([conv_all[d, :conv_rows].reshape(CONV_W, GDN_QKV // N_CHIPS) for d in range(N_CHIPS)], axis=1)
    wi = jnp.concatenate([w_in4[d, :shard_cols] for d in range(N_CHIPS)], axis=0)
    w_cat = jnp.concatenate([wi[:OFF_BETA], wi[OFF_FOX:OFF_F], wi[OFF_BETA:OFF_FOX], wi[OFF_F:],
                             jnp.zeros((D_CAT - D_IN, D_MODEL), bf16)], axis=0)

    small = {n: w[n] for n in SMALL_NAMES}
    qc = jnp.stack([q, ci]).astype(jnp.int32)
    tail_state = {}

    def pairs_phase(gc):
        g_in = jnp.concatenate([gc[:OFF_BETA], gc[SEG_SMALL:SEG_SMALL + 8], gc[SEG_FOX:SEG_SMALL], gc[SEG_SMALL + 8:SEG_SMALL + 16]], axis=0)
        g_in4 = jnp.stack([jnp.pad(g_in[d * shard_cols:(d + 1) * shard_cols], ((0, W_IN_ROWS - shard_cols), (0, 0))) for d in range(N_CHIPS)])

        def took(moved):
            own, sent = _add_pair(g_in4, moved[0], qc, "add_pair_w_in")
            tail_state.update(own=own, sent=[sent], landing=_landing([sent]))
        return _exchange_pairs([g_in4]), took

    grad_x, _, g_conv, g_late, small_g = _device_grads(
        x[0], p[0, 0], loss_target[0], small, w_cat, conv_full, [slot_buffer(w[n][0], bf16) for n in LATE], qc, tail=(pairs_phase, None),
        ln_in_out=ln_in_out)
    packets = _gather_packets(slot_buffer(_pack_small(small_g, g_conv), f32, 8, 4 * xi + 2 * yi + ci))
    (b2,), (small_all,) = _comm_only([_exchange_chips(tail_state["sent"], tail_state["landing"]), packets], "exchange_chips_w_in")
    (g_late["w_in"],), = _comm_only([_share_halves([_add_chips(tail_state["own"], b2, qc, "add_chips_w_in")])], "share_w_in")

    grads, delta, new_m, new_v = {}, {}, {}, {}
    for n, g in g_late.items():
        if n == "w_in":
            as_stored = lambda a: jnp.transpose(a, (2, 0, 1))
            outs, _ = _adamw(as_stored(w[n]), g[:shard_cols].reshape(shard_cols, 1, D_MODEL), as_stored(m[n]), as_stored(v[n]), "adamw_" + n)
            grads[n], delta[n], new_m[n], new_v[n] = (jnp.transpose(a, (1, 2, 0)) for a in outs)
        else:
            outs, _ = _adamw(w[n][0], g, m[n][0], v[n][0], "adamw_" + n)
            grads[n], delta[n], new_m[n], new_v[n] = (a.reshape(w[n].shape) for a in outs)
    pick = lambda d: {n: d[n] for n in SMALL_NAMES}
    g_pkt, by_kind = _small_sum_adamw(small_all, pick(w), pick(m), pick(v))
    for dst, vals in zip((grads, delta, new_m, new_v), by_kind):
        dst.update(vals)
    conv_rows_all = CONV_W * GDN_QKV // LANES
    conv_g_full = g_pkt[SMALL_CONV_ROW:SMALL_CONV_ROW + conv_rows_all].reshape(CONV_W, GDN_QKV)
    conv_g = lax.dynamic_slice_in_dim(conv_g_full, q * (GDN_QKV // N_CHIPS), GDN_QKV // N_CHIPS, axis=1)
    outs, _ = _adamw(w["conv_w"][0], conv_g, m["conv_w"][0], v["conv_w"][0], "adamw_conv_w")
    grads["conv_w"], delta["conv_w"], new_m["conv_w"], new_v["conv_w"] = (a.reshape(w["conv_w"].shape) for a in outs)
    loss = g_pkt[61, 0]
    return (loss, grad_x[None], *[grads[n] for n in WEIGHTS], *[delta[n] for n in WEIGHTS],
            *[new_m[n] for n in WEIGHTS], *[new_v[n] for n in WEIGHTS])
```

```python
import functools

import jax
import jax.numpy as jnp
from jax import lax
from jax.experimental import pallas as pl
from jax.experimental.pallas import tpu as pltpu

f32 = jnp.float32
bf16 = jnp.bfloat16
HI = lax.Precision.HIGHEST
MESH = pl.DeviceIdType.MESH

D_MODEL = 1024
CHUNK = 64
GDN_HEADS = 4
GDN_DK = 128
FOX_HEADS = 8
FOX_DH = 64
CONV_W = 4
D_FF = 4096
D_PLE = 256
LN_EPS = 1e-5
NORM_EPS = 1e-6
ALPHA = 2.0 ** 0.25
GDN_QKV = 1536
OFF_Z = 1536
OFF_BETA = 2048
OFF_FOX = 2056
OFF_F = 3592
D_IN = 3600
ADAM_LR = 0.001
ADAM_B1 = 0.9
ADAM_B2 = 0.999
ADAM_EPS = 1e-08
ADAM_WD = 0.01
ADAM_STEP = 10

SEG_FOX = 2048
SEG_SMALL = 3584
D_CAT = 3840
LANES = 128
TOK_BLK = 256
FOX_BQ = 256
VMEM_LIMIT = 56 * 1024 * 1024
NEG = -1e30

N_CHIPS = 4
W_IN_ROWS = 928


def _params(sem=None, **kw):
    return pltpu.CompilerParams(dimension_semantics=sem, vmem_limit_bytes=VMEM_LIMIT, **kw)


def _sigmoid(x):
    return 1.0 / (1.0 + jnp.exp(-x))


def _softplus(x):
    return jnp.maximum(x, 0.0) + jnp.log(1.0 + jnp.exp(-jnp.abs(x)))


def _ln_fwd(x, g, b):
    mu = jnp.mean(x, -1, keepdims=True)
    xc = x - mu
    var = jnp.mean(xc * xc, -1, keepdims=True)
    rstd = lax.rsqrt(var + LN_EPS)
    xhat = xc * rstd
    return xhat * g + b, xhat, rstd


def _ln_bwd(dy, xhat, rstd, g):
    dxh = dy * g
    m1 = jnp.mean(dxh, -1, keepdims=True)
    m2 = jnp.mean(dxh * xhat, -1, keepdims=True)
    return rstd * (dxh - m1 - xhat * m2)


def _dot(a, b, prec=HI):
    return jnp.dot(a, b, precision=prec, preferred_element_type=f32)


def _dot_nt(a, b, prec=HI):
    return lax.dot_general(a, b, (((1,), (1,)), ((), ())), precision=prec, preferred_element_type=f32)


def _dot_tn(a, b, prec=HI):
    return lax.dot_general(a, b, (((0,), (0,)), ((), ())), precision=prec, preferred_element_type=f32)


def _bdot(a, b):
    return _dot(a.astype(bf16), b.astype(bf16), None)


def _bdot_nt(a, b):
    return _dot_nt(a.astype(bf16), b.astype(bf16), None)


def _bdot_tn(a, b):
    return _dot_tn(a.astype(bf16), b.astype(bf16), None)


def _lane(shape):
    return lax.broadcasted_iota(jnp.int32, shape, len(shape) - 1)


def _mm(a, b, mode, tm, tn, name, out_dtype=f32, epi=None, extra=None, shards=1, comm=None):
    if mode == "nn":
        (m, k), n = a.shape, b.shape[-1] * shards
    elif mode == "nt":
        (m, k), n = a.shape, b.shape[-2]
    else:
        (k, m), n = a.shape, b.shape[1]
    assert m % tm == 0 and n % tn == 0, (name, m, n, tm, tn)
    per = (n // shards) // tn
    assert mode == "nt" or per * tn * shards == n, (name, n, tn, shards)
    nc = 512 if tn % 512 == 0 else (256 if tn % 256 == 0 else 128)
    ks = k // shards

    def body(a_ref, b_ref, *rest):
        for n0 in range(0, tn, nc):
            if mode == "nn":
                acc = jnp.dot(a_ref[...], b_ref[:, n0:n0 + nc], preferred_element_type=f32)
            elif mode == "nt" and shards > 1:
                acc = jnp.zeros((tm, nc), f32)
                for d in range(shards):
                    acc = acc + lax.dot_general(a_ref[:, d * ks:(d + 1) * ks], b_ref[d, n0:n0 + nc, :], (((1,), (1,)), ((), ())),
                                                preferred_element_type=f32)
            elif mode == "nt":
                acc = lax.dot_general(a_ref[...], b_ref[n0:n0 + nc, :], (((1,), (1,)), ((), ())), preferred_element_type=f32)
            else:
                acc = lax.dot_general(a_ref[...], b_ref[:, n0:n0 + nc], (((0,), (0,)), ((), ())), preferred_element_type=f32)
            if epi == "relu2":
                relu_ref, act_ref = rest
                r = jnp.maximum(acc, 0.0)
                relu_ref[:, n0:n0 + nc] = r.astype(bf16)
                act_ref[:, n0:n0 + nc] = (r * r).astype(bf16)
            elif epi == "relu2_bwd":
                relu_ref, o_ref = rest
                o_ref[:, n0:n0 + nc] = (acc * (2.0 * relu_ref[:, n0:n0 + nc].astype(f32))).astype(bf16)
            else:
                (o_ref,) = rest
                o_ref[:, n0:n0 + nc] = acc.astype(out_dtype)

    if mode == "tn":
        a_spec = pl.BlockSpec((k, tm), lambda j, i: (0, i))
    else:
        a_spec = pl.BlockSpec((tm, k), lambda j, i: (i, 0))
    if mode == "nt" and shards > 1:
        b_spec = pl.BlockSpec((shards, tn, ks), lambda j, i: (0, j, 0))
    elif mode == "nt":
        b_spec = pl.BlockSpec((tn, k), lambda j, i: (j, 0))
    elif mode == "nn" and shards > 1:
        b_spec = pl.BlockSpec((None, k, tn), lambda j, i: (j // per, 0, j % per))
    else:
        b_spec = pl.BlockSpec((k, tn), lambda j, i: (0, j))
    o_spec = pl.BlockSpec((tm, tn), lambda j, i: (i, j))
    in_specs = [a_spec, b_spec]
    args = [a, b]
    if epi == "relu2":
        out_shape = (jax.ShapeDtypeStruct((m, n), bf16), jax.ShapeDtypeStruct((m, n), bf16))
        out_specs = (o_spec, o_spec)
    elif epi == "relu2_bwd":
        in_specs.append(o_spec)
        args.append(extra)
        out_shape = jax.ShapeDtypeStruct((m, n), bf16)
        out_specs = o_spec
    elif mode == "tn" and shards > 1:
        out_shape = jax.ShapeDtypeStruct((shards, m, n // shards), out_dtype)
        out_specs = pl.BlockSpec((None, tm, tn), lambda j, i: (j // per, i, j % per))
    else:
        out_shape = jax.ShapeDtypeStruct((m, n), out_dtype)
        out_specs = o_spec
    single = not isinstance(out_shape, tuple)
    res, moved = _hosted(body, comm, name=name, grid=(n // tn, m // tm), in_specs=in_specs,
                         out_specs=(out_specs,) if single else out_specs, out_shape=(out_shape,) if single else out_shape, args=args)
    res = res[0] if single else res
    return res if comm is None else (res, moved)


def _row_spec(width, col=0):
    return pl.BlockSpec((TOK_BLK, width), lambda i: (i, col))


def _vec_spec(rows, width):
    return pl.BlockSpec((rows, width), lambda i: (0, 0))


def _ln_in(x, g, b, comm=None):
    t, d = x.shape

    def body(x_ref, g_ref, b_ref, h_ref, hb_ref):
        h, _, _ = _ln_fwd(x_ref[...], g_ref[...], b_ref[...])
        h_ref[...] = h
        hb_ref[...] = h.astype(bf16)

    return _hosted(
        body, comm, name="ln_in", grid=(t // TOK_BLK,),
        in_specs=[_row_spec(d), _vec_spec(1, d), _vec_spec(1, d)],
        out_specs=(_row_spec(d), _row_spec(d)),
        out_shape=(jax.ShapeDtypeStruct((t, d), f32), jax.ShapeDtypeStruct((t, d), bf16)),
        args=(x, g, b))


def _attn_post(o_gdn, proj, o_fox, g_gdn, g_fox2, comm=None):
    t = o_gdn.shape[0]

    def body(og_ref, z_ref, of_ref, gg_ref, gf_ref, out_ref):
        for h in range(GDN_HEADS):
            sl = slice(h * LANES, (h + 1) * LANES)
            og = og_ref[:, sl]
            z = z_ref[:, sl]
            r = lax.rsqrt(jnp.mean(og * og, -1, keepdims=True) + NORM_EPS)
            out_ref[:, sl] = (og * r * gg_ref[...] * (z * _sigmoid(z))).astype(bf16)
        lo = _lane((TOK_BLK, LANES)) < FOX_DH
        for pr in range(FOX_HEADS // 2):
            sl = slice(pr * LANES, (pr + 1) * LANES)
            of = of_ref[:, sl]
            sq = of * of
            s0 = jnp.sum(jnp.where(lo, sq, 0.0), -1, keepdims=True)
            s1 = jnp.sum(jnp.where(lo, 0.0, sq), -1, keepdims=True)
            r = lax.rsqrt(jnp.where(lo, s0, s1) * (1.0 / FOX_DH) + NORM_EPS)
            out_ref[:, 512 + pr * LANES:512 + (pr + 1) * LANES] = (of * r * gf_ref[...]).astype(bf16)

    return _hosted(
        body, comm, name="attn_post", grid=(t // TOK_BLK,),
        in_specs=[_row_spec(512), _row_spec(512, OFF_Z // 512), _row_spec(512), _vec_spec(1, LANES), _vec_spec(1, LANES)],
        out_specs=(_row_spec(D_MODEL),),
        out_shape=(jax.ShapeDtypeStruct((t, D_MODEL), bf16),),
        args=(o_gdn, proj, o_fox, g_gdn, g_fox2))


def _attn_post_bwd(dr1b, w_out, o_gdn, proj, o_fox, g_gdn, g_fox2):
    t = o_gdn.shape[0]

    def body(dr_ref, wo_ref, og_ref, z_ref, of_ref, gg_ref, gf_ref, dog_ref, dz_ref, dof_ref, pg_ref):
        i = pl.program_id(0)

        @pl.when(i == 0)
        def _():
            pg_ref[...] = jnp.zeros_like(pg_ref)

        da = _dot_nt(dr_ref[...], wo_ref[...], None)
        dgg = jnp.zeros((1, LANES), f32)
        for h in range(GDN_HEADS):
            sl = slice(h * LANES, (h + 1) * LANES)
            og = og_ref[:, sl]
            z = z_ref[:, sl]
            dout = da[:, sl]
            g = gg_ref[...]
            r = lax.rsqrt(jnp.mean(og * og, -1, keepdims=True) + NORM_EPS)
            sg = _sigmoid(z)
            silu = z * sg
            ng = og * r * g
            dng = dout * silu
            dz_ref[:, sl] = (dout * ng * (sg * (1.0 + z * (1.0 - sg)))).astype(bf16)
            dgg = dgg + jnp.sum(dng * og * r, 0, keepdims=True)
            gd = dng * g
            dog_ref[:, sl] = r * gd - og * (r * r * r) * jnp.mean(og * gd, -1, keepdims=True)
        pg_ref[0:1, :] += dgg
        lo = _lane((TOK_BLK, LANES)) < FOX_DH
        dgf = jnp.zeros((1, LANES), f32)
        for pr in range(FOX_HEADS // 2):
            sl = slice(pr * LANES, (pr + 1) * LANES)
            of = of_ref[:, sl]
            dout = da[:, 512 + pr * LANES:512 + (pr + 1) * LANES]
            g = gf_ref[...]
            sq = of * of
            s0 = jnp.sum(jnp.where(lo, sq, 0.0), -1, keepdims=True)
            s1 = jnp.sum(jnp.where(lo, 0.0, sq), -1, keepdims=True)
            r = lax.rsqrt(jnp.where(lo, s0, s1) * (1.0 / FOX_DH) + NORM_EPS)
            dgf = dgf + jnp.sum(dout * of * r, 0, keepdims=True)
            gd = dout * g
            xg = of * gd
            m0 = jnp.sum(jnp.where(lo, xg, 0.0), -1, keepdims=True)
            m1 = jnp.sum(jnp.where(lo, 0.0, xg), -1, keepdims=True)
            dof_ref[:, sl] = r * gd - of * (r * r * r) * (jnp.where(lo, m0, m1) * (1.0 / FOX_DH))
        pg_ref[1:2, :] += dgf

    return pl.pallas_call(
        body, name="attn_post_bwd", grid=(t // TOK_BLK,),
        in_specs=_product_specs(dr1b, w_out) + [_row_spec(512), _row_spec(512, OFF_Z // 512), _row_spec(512), _vec_spec(1, LANES), _vec_spec(1, LANES)],
        out_specs=(_row_spec(512), _row_spec(512), _row_spec(512), _vec_spec(8, LANES)),
        out_shape=(jax.ShapeDtypeStruct((t, 512), f32), jax.ShapeDtypeStruct((t, 512), bf16),
                   jax.ShapeDtypeStruct((t, 512), f32), jax.ShapeDtypeStruct((8, LANES), f32)),
        compiler_params=_params(("arbitrary",)),
    )(dr1b, w_out, o_gdn, proj, o_fox, g_gdn, g_fox2)


def _product_specs(lhs, rhs):
    return [_row_spec(lhs.shape[1]), pl.BlockSpec(rhs.shape, lambda i: (0, 0))]


def _ln1(h0, lhs, rhs, g, b, comm=None):
    t, d = h0.shape

    def body(h0_ref, lhs_ref, rhs_ref, g_ref, b_ref, h_ref, hb_ref, xh_ref, rs_ref):
        mix = jnp.dot(lhs_ref[...], rhs_ref[...], preferred_element_type=f32)
        h, xhat, rstd = _ln_fwd(ALPHA * h0_ref[...] + mix, g_ref[...], b_ref[...])
        h_ref[...] = h
        hb_ref[...] = h.astype(bf16)
        xh_ref[...] = xhat
        rs_ref[...] = jnp.broadcast_to(rstd, rs_ref.shape)

    return _hosted(
        body, comm, name="ln1", grid=(t // TOK_BLK,),
        in_specs=[_row_spec(d)] + _product_specs(lhs, rhs) + [_vec_spec(1, d), _vec_spec(1, d)],
        out_specs=(_row_spec(d), _row_spec(d), _row_spec(d), _row_spec(LANES)),
        out_shape=(jax.ShapeDtypeStruct((t, d), f32), jax.ShapeDtypeStruct((t, d), bf16),
                   jax.ShapeDtypeStruct((t, d), f32), jax.ShapeDtypeStruct((t, LANES), f32)),
        args=(h0, lhs, rhs, g, b))


def _ln2_loss(h1, lhs, rhs, pb, w_ple, gp, b_gate, g, b, target):
    t, d = h1.shape

    def body(h1_ref, lhs_ref, rhs_ref, pb_ref, wp_ref, gp_ref, bg_ref, g_ref, b_ref, t_ref, dr_ref, drb_ref, dpe_ref, dgp_ref, pg_ref):
        i = pl.program_id(0)

        @pl.when(i == 0)
        def _():
            pg_ref[...] = jnp.zeros_like(pg_ref)

        ff = jnp.dot(lhs_ref[...], rhs_ref[...], preferred_element_type=f32)
        sig = _sigmoid(gp_ref[...] + bg_ref[...])
        pe = jnp.concatenate([jnp.dot(pb_ref[...], wp_ref[s], preferred_element_type=f32) for s in range(w_ple.shape[0])], axis=1)
        r2 = ALPHA * h1_ref[...] + ff + pe * sig
        y, xhat, rstd = _ln_fwd(r2, g_ref[...], b_ref[...])
        err = y - t_ref[...]
        dy = err * (1.0 / d)
        dr = _ln_bwd(dy, xhat, rstd, g_ref[...])
        dr_ref[...] = dr
        drb_ref[...] = dr.astype(bf16)
        dpe_ref[...] = (dr * sig).astype(bf16)
        dgp = dr * pe * sig * (1.0 - sig)
        dgp_ref[...] = dgp.astype(bf16)
        pg_ref[0:1, :] += jnp.sum(dy * xhat, 0, keepdims=True)
        pg_ref[1:2, :] += jnp.sum(dy, 0, keepdims=True)
        pg_ref[2:3, :] += jnp.sum(dgp, 0, keepdims=True)
        pg_ref[3:4, :] += 0.5 * jnp.sum(jnp.mean(err * err, -1, keepdims=True), 0, keepdims=True)

    return pl.pallas_call(
        body, name="ln2_loss", grid=(t // TOK_BLK,),
        in_specs=[_row_spec(d)] + _product_specs(lhs, rhs) + [_row_spec(pb.shape[1]), pl.BlockSpec(w_ple.shape, lambda i: (0, 0, 0)), _row_spec(d)]
        + [_vec_spec(1, d)] * 3 + [_row_spec(d)],
        out_specs=(_row_spec(d), _row_spec(d), _row_spec(d), _row_spec(d), _vec_spec(8, d)),
        out_shape=(jax.ShapeDtypeStruct((t, d), f32), jax.ShapeDtypeStruct((t, d), bf16), jax.ShapeDtypeStruct((t, d), bf16),
                   jax.ShapeDtypeStruct((t, d), bf16), jax.ShapeDtypeStruct((8, d), f32)),
        compiler_params=_params(("arbitrary",)),
    )(h1, lhs, rhs, pb, w_ple, gp, b_gate, g, b, target)


def _ln1_bwd(dr2, dup, w_up, dgp, w_gate, xhat, rstd, g):
    t, d = dr2.shape
    ks = w_up.shape[2]

    def body(dr2_ref, dup_ref, wup_ref, dgp_ref, wg_ref, xh_ref, rs_ref, g_ref, dr_ref, drb_ref, pg_ref):
        i = pl.program_id(0)

        @pl.when(i == 0)
        def _():
            pg_ref[...] = jnp.zeros_like(pg_ref)

        dh = ALPHA * dr2_ref[...] + _dot_nt(dgp_ref[...], wg_ref[...], None)
        for s in range(w_up.shape[0]):
            dh = dh + _dot_nt(dup_ref[:, s * ks:(s + 1) * ks], wup_ref[s], None)
        xhat = xh_ref[...]
        dr = _ln_bwd(dh, xhat, rs_ref[:, 0:1], g_ref[...])
        dr_ref[...] = dr
        drb_ref[...] = dr.astype(bf16)
        pg_ref[0:1, :] += jnp.sum(dh * xhat, 0, keepdims=True)
        pg_ref[1:2, :] += jnp.sum(dh, 0, keepdims=True)

    return pl.pallas_call(
        body, name="ln1_bwd", grid=(t // TOK_BLK,),
        in_specs=[_row_spec(d), _row_spec(dup.shape[1]), pl.BlockSpec(w_up.shape, lambda i: (0, 0, 0))] + _product_specs(dgp, w_gate)
        + [_row_spec(d), _row_spec(LANES), _vec_spec(1, d)],
        out_specs=(_row_spec(d), _row_spec(d), _vec_spec(8, d)),
        out_shape=(jax.ShapeDtypeStruct((t, d), f32), jax.ShapeDtypeStruct((t, d), bf16), jax.ShapeDtypeStruct((8, d), f32)),
        compiler_params=_params(("arbitrary",)),
    )(dr2, dup, w_up, dgp, w_gate, xhat, rstd, g)


def _ln_in_bwd(x, dr1, dmm, g, comm=None):
    t, d = x.shape

    def body(x_ref, dr1_ref, dmm_ref, g_ref, dx_ref, pg_ref):
        i = pl.program_id(0)

        @pl.when(i == 0)
        def _():
            pg_ref[...] = jnp.zeros_like(pg_ref)

        dh = ALPHA * dr1_ref[...] + dmm_ref[...]
        _, xhat, rstd = _ln_fwd(x_ref[...], g_ref[...], 0.0)
        dx_ref[...] = _ln_bwd(dh, xhat, rstd, g_ref[...])
        pg_ref[0:1, :] += jnp.sum(dh * xhat, 0, keepdims=True)
        pg_ref[1:2, :] += jnp.sum(dh, 0, keepdims=True)

    return _hosted(
        body, comm, name="ln_in_bwd", grid=(t // TOK_BLK,),
        in_specs=[_row_spec(d)] * 3 + [_vec_spec(1, d)],
        out_specs=(_row_spec(d), _vec_spec(8, d)),
        out_shape=(jax.ShapeDtypeStruct((t, d), f32), jax.ShapeDtypeStruct((8, d), f32)),
        args=(x, dr1, dmm, g))


def _tri(n, upper=False, strict=False):
    r = lax.broadcasted_iota(jnp.int32, (n, n), 0)
    c = lax.broadcasted_iota(jnp.int32, (n, n), 1)
    if upper:
        m = (c > r) if strict else (c >= r)
    else:
        m = (c < r) if strict else (c <= r)
    return jnp.where(m, 1.0, 0.0).astype(f32)


def _gate_values(x, bias, alog, lane):
    z = x + bias
    return jnp.where(lane < 4, _sigmoid(z), jnp.where(lane < 8, -jnp.exp(alog) * _softplus(z), jnp.where(lane < 16, -_softplus(-z), 0.0)))


def _gates(proj, bias_row, alog_row):
    t = proj.shape[0]
    nch = t // CHUNK

    def body(x_ref, bias_ref, alog_ref, gates_ref, gcum_ref, gcumt_ref):
        lane = _lane((t, LANES))
        gates = _gate_values(x_ref[...], bias_ref[...], alog_ref[...], lane)
        gates_ref[...] = gates
        g3 = gates.reshape(nch, CHUNK, LANES)
        tri = jnp.broadcast_to(_tri(CHUNK)[None], (nch, CHUNK, CHUNK))
        loc = jnp.einsum("bij,bjk->bik", tri, g3, precision=HI, preferred_element_type=f32)
        tot = jnp.sum(g3, axis=1)
        offs = _dot(_tri(nch, strict=True), tot)
        glob = loc + offs[:, None, :]
        lane3 = _lane((nch, CHUNK, LANES))
        gcum = jnp.where(lane3 < 4, g3, jnp.where(lane3 < 8, loc, glob)).reshape(t, LANES)
        gcum_ref[...] = gcum
        gcumt_ref[...] = gcum.T

    return pl.pallas_call(
        body, name="gates", grid=(1,),
        in_specs=[pl.BlockSpec((t, LANES), lambda i: (0, SEG_SMALL // LANES)), _vec_spec(1, LANES), _vec_spec(1, LANES)],
        out_specs=(pl.BlockSpec((t, LANES), lambda i: (0, 0)), pl.BlockSpec((t, LANES), lambda i: (0, 0)),
                   pl.BlockSpec((LANES, t), lambda i: (0, 0))),
        out_shape=(jax.ShapeDtypeStruct((t, LANES), f32), jax.ShapeDtypeStruct((t, LANES), f32), jax.ShapeDtypeStruct((LANES, t), f32)),
        compiler_params=_params(("arbitrary",)),
    )(proj, bias_row, alog_row)


def _gates_bwd(proj, bias_row, alog_row, gates, dgates, dccol, dct):
    t = proj.shape[0]
    nch = t // CHUNK

    def body(x_ref, bias_ref, alog_ref, gates_ref, dg_ref, dcc_ref, dct_ref, dx_ref, pg_ref):
        lane = _lane((t, LANES))
        d = dg_ref[...] + dcc_ref[...] + dct_ref[...].T
        d3 = d.reshape(nch, CHUNK, LANES)
        tri = jnp.broadcast_to(_tri(CHUNK, upper=True)[None], (nch, CHUNK, CHUNK))
        loc = jnp.einsum("bij,bjk->bik", tri, d3, precision=HI, preferred_element_type=f32)
        tot = jnp.sum(d3, axis=1)
        offs = _dot(_tri(nch, upper=True, strict=True), tot)
        glob = loc + offs[:, None, :]
        lane3 = _lane((nch, CHUNK, LANES))
        dpre = jnp.where(lane3 < 4, d3, jnp.where(lane3 < 8, loc, glob)).reshape(t, LANES)
        z = x_ref[...] + bias_ref[...]
        sg = _sigmoid(z)
        dx = jnp.where(lane < 4, dpre * sg * (1.0 - sg),
                       jnp.where(lane < 8, dpre * (-jnp.exp(alog_ref[...])) * sg, jnp.where(lane < 16, dpre * (1.0 - sg), 0.0)))
        dx_ref[...] = dx.astype(bf16)
        pg_ref[...] = jnp.zeros_like(pg_ref)
        pg_ref[0:1, :] = jnp.sum(dx, 0, keepdims=True)
        pg_ref[1:2, :] = jnp.sum(jnp.where((lane >= 4) & (lane < 8), dpre * gates_ref[...], 0.0), 0, keepdims=True)

    full = pl.BlockSpec((t, LANES), lambda i: (0, 0))
    return pl.pallas_call(
        body, name="gates_bwd", grid=(1,),
        in_specs=[pl.BlockSpec((t, LANES), lambda i: (0, SEG_SMALL // LANES)), _vec_spec(1, LANES), _vec_spec(1, LANES),
                  full, full, full, pl.BlockSpec((LANES, t), lambda i: (0, 0))],
        out_specs=(full, _vec_spec(8, LANES)),
        out_shape=(jax.ShapeDtypeStruct((t, LANES), bf16), jax.ShapeDtypeStruct((8, LANES), f32)),
        compiler_params=_params(("arbitrary",)),
    )(proj, bias_row, alog_row, gates, dgates, dccol, dct)


def _conv_act(u, cw, row, t):
    c = cw[3:4, :] * u
    for jj in range(CONV_W - 1):
        sh = CONV_W - 1 - jj
        c = c + cw[jj:jj + 1, :] * jnp.where(row >= sh, pltpu.roll(u, sh, axis=0), 0.0)
    return c


def _gdn_conv(proj, conv_w, comm=None):
    t = proj.shape[0]
    nblk = GDN_QKV // LANES

    def body(u_ref, cw_ref, c_ref, y_ref):
        j = pl.program_id(0)
        row = lax.broadcasted_iota(jnp.int32, (t, LANES), 0)
        c = _conv_act(u_ref[...], cw_ref[...], row, t)
        c_ref[...] = c
        s = c * _sigmoid(c)
        r = lax.rsqrt(jnp.sum(s * s, -1, keepdims=True) + NORM_EPS)
        scale = jnp.where(j < GDN_HEADS, GDN_DK ** -0.5, 1.0)
        y_ref[...] = jnp.where(j < 2 * GDN_HEADS, s * (r * scale), s)

    blk = pl.BlockSpec((t, LANES), lambda j: (0, j))
    return _hosted(
        body, comm, name="gdn_conv", grid=(nblk,),
        in_specs=[blk, pl.BlockSpec((CONV_W, LANES), lambda j: (0, j))],
        out_specs=(blk, blk),
        out_shape=(jax.ShapeDtypeStruct((t, GDN_QKV), f32), jax.ShapeDtypeStruct((t, GDN_QKV), f32)),
        args=(proj, conv_w))


def _gdn_conv_bwd(proj, conv_w, c, dy, comm=None):
    t = proj.shape[0]
    nblk = GDN_QKV // LANES

    def body(u_ref, cw_ref, c_ref, dy_ref, du_ref, dcw_ref):
        j = pl.program_id(0)
        row = lax.broadcasted_iota(jnp.int32, (t, LANES), 0)
        u = u_ref[...]
        cw = cw_ref[...]
        c = c_ref[...]
        dy = dy_ref[...]
        sg = _sigmoid(c)
        s = c * sg
        r = lax.rsqrt(jnp.sum(s * s, -1, keepdims=True) + NORM_EPS)
        n = s * r
        scale = jnp.where(j < GDN_HEADS, GDN_DK ** -0.5, 1.0)
        dn = dy * scale
        ds = jnp.where(j < 2 * GDN_HEADS, r * (dn - n * jnp.sum(dn * n, -1, keepdims=True)), dy)
        dc = ds * (sg * (1.0 + c * (1.0 - sg)))
        du = cw[3:4, :] * dc
        dcw_ref[...] = jnp.zeros_like(dcw_ref)
        dcw_ref[3:4, :] = jnp.sum(dc * u, 0, keepdims=True)
        for jj in range(CONV_W - 1):
            sh = CONV_W - 1 - jj
            du = du + cw[jj:jj + 1, :] * jnp.where(row < t - sh, pltpu.roll(dc, t - sh, axis=0), 0.0)
            dcw_ref[jj:jj + 1, :] = jnp.sum(dc * jnp.where(row >= sh, pltpu.roll(u, sh, axis=0), 0.0), 0, keepdims=True)
        du_ref[...] = du.astype(bf16)

    blk = pl.BlockSpec((t, LANES), lambda j: (0, j))
    return _hosted(
        body, comm, name="gdn_conv_bwd", grid=(nblk,),
        in_specs=[blk, pl.BlockSpec((CONV_W, LANES), lambda j: (0, j)), blk, blk],
        out_specs=(blk, pl.BlockSpec((8, LANES), lambda j: (0, j))),
        out_shape=(jax.ShapeDtypeStruct((t, GDN_QKV), bf16), jax.ShapeDtypeStruct((8, GDN_QKV), f32)),
        args=(proj, conv_w, c, dy))


def _chunk_masks():
    r = lax.broadcasted_iota(jnp.int32, (CHUNK, CHUNK), 0)
    c = lax.broadcasted_iota(jnp.int32, (CHUNK, CHUNK), 1)
    return r >= c, r > c, r == c


def _col_to_row(col, eye):
    return jnp.sum(jnp.where(eye, col, 0.0), axis=0, keepdims=True)


def _row_to_col(row, eye):
    return jnp.sum(jnp.where(eye, row, 0.0), axis=1, keepdims=True)


NN = (((1,), (0,)), ((), ()))
NT = (((1,), (1,)), ((), ()))
TN = (((0,), (0,)), ((), ()))
GDN_GROUP = 4


def _mx(a, b, dims=NN, passes=1):
    d = lambda p, q: lax.dot_general(p, q, dims, preferred_element_type=f32)
    ah, bh = a.astype(bf16), b.astype(bf16)
    if passes == 1:
        return d(ah, bh)
    al = (a - ah.astype(f32)).astype(bf16)
    bl = (b - bh.astype(f32)).astype(bf16)
    return d(ah, bh) + (d(ah, bl) + d(al, bh))


def _gdn_decay(gam, masks):
    causal, _, eye = masks
    return jnp.exp(jnp.where(causal, gam - _col_to_row(gam, eye), NEG))


def _gdn_local(y, gcum, comm=None):
    t = y.shape[0]
    nch = t // CHUNK
    rows_blk = GDN_GROUP * CHUNK

    def body(y_ref, g_ref, u_ref, w_ref, qk_ref, tinv_ref):
        masks = _chunk_masks()
        _, strict, eye = masks
        ids = [(j, h) for j in range(GDN_GROUP) for h in range(GDN_HEADS)]
        rs = lambda j: slice(j * CHUNK, (j + 1) * CHUNK)
        col = lambda base, h: slice(base + h * LANES, base + (h + 1) * LANES)
        kn = [y_ref[rs(j), col(512, h)] for j, h in ids]
        beta = [g_ref[rs(j), h:h + 1] for j, h in ids]
        gam = [g_ref[rs(j), 4 + h:5 + h] for j, h in ids]
        dec = [_gdn_decay(g, masks) for g in gam]
        x = [-jnp.where(strict, _mx(k, k, NT) * d * b, 0.0) for k, d, b in zip(kn, dec, beta)]
        tinv = [jnp.where(eye, 1.0, 0.0) + a for a in x]
        for _ in range(5):
            x = [_mx(a, a, NN, 3) for a in x]
            tinv = [t_ + _mx(t_, a, NN, 3) for t_, a in zip(tinv, x)]
        for (j, h), t_, k, d, b, g in zip(ids, tinv, kn, dec, beta, gam):
            u_ref[rs(j), col(0, h)] = _mx(t_, b * y_ref[rs(j), col(1024, h)])
            w_ref[rs(j), col(0, h)] = _mx(t_, (b * jnp.exp(g)) * k)
            qk_ref[j, h] = _mx(y_ref[rs(j), col(0, h)], k, NT) * d
            tinv_ref[j, h] = t_

    mat = pl.BlockSpec((GDN_GROUP, GDN_HEADS, CHUNK, CHUNK), lambda n: (n, 0, 0, 0))
    return _hosted(
        body, comm, name="gdn_local", grid=(nch // GDN_GROUP,),
        in_specs=[pl.BlockSpec((rows_blk, GDN_QKV), lambda n: (n, 0)), pl.BlockSpec((rows_blk, LANES), lambda n: (n, 0))],
        out_specs=(pl.BlockSpec((rows_blk, 512), lambda n: (n, 0)), pl.BlockSpec((rows_blk, 512), lambda n: (n, 0)), mat, mat),
        out_shape=(jax.ShapeDtypeStruct((t, 512), f32), jax.ShapeDtypeStruct((t, 512), f32),
                   jax.ShapeDtypeStruct((nch, GDN_HEADS, CHUNK, CHUNK), f32), jax.ShapeDtypeStruct((nch, GDN_HEADS, CHUNK, CHUNK), f32)),
        args=(y, gcum))


def _gdn_fwd(y, gcum, u, w, qk, comm=None):
    t = y.shape[0]
    nch = t // CHUNK

    def body(y_ref, g_ref, u_ref, w_ref, qk_ref, o_ref, sall_ref, s_ref):
        @pl.when(pl.program_id(0) == 0)
        def _():
            s_ref[...] = jnp.zeros_like(s_ref)

        heads = range(GDN_HEADS)
        sl = [slice(h * LANES, (h + 1) * LANES) for h in heads]
        gam = [g_ref[:, 4 + h:5 + h] for h in heads]
        gam_last = [g[CHUNK - 1:CHUNK, :] for g in gam]
        s = [s_ref[h] for h in heads]
        for h in heads:
            sall_ref[0, h] = s[h]
        ws = [_mx(w_ref[:, sl[h]], s[h]) for h in heads]
        qs = [_mx(y_ref[:, sl[h]] * jnp.exp(gam[h]), s[h]) for h in heads]
        vn = [u_ref[:, sl[h]] - ws[h] for h in heads]
        av = [_mx(qk_ref[0, h], vn[h]) for h in heads]
        kv = [_mx(y_ref[:, 512 + h * LANES:512 + (h + 1) * LANES] * jnp.exp(gam_last[h] - gam[h]), vn[h], TN) for h in heads]
        for h in heads:
            o_ref[:, sl[h]] = qs[h] + av[h]
            s_ref[h] = jnp.exp(gam_last[h]) * s[h] + kv[h]

    row = lambda width: pl.BlockSpec((CHUNK, width), lambda n: (n, 0))
    return _hosted(
        body, comm, name="gdn_fwd", grid=(nch,),
        in_specs=[row(GDN_QKV), row(LANES), row(512), row(512), pl.BlockSpec((1, GDN_HEADS, CHUNK, CHUNK), lambda n: (n, 0, 0, 0))],
        out_specs=(row(512), pl.BlockSpec((1, GDN_HEADS, LANES, LANES), lambda n: (n, 0, 0, 0))),
        out_shape=(jax.ShapeDtypeStruct((t, 512), f32), jax.ShapeDtypeStruct((nch, GDN_HEADS, LANES, LANES), f32)),
        scratch_shapes=[pltpu.VMEM((GDN_HEADS, LANES, LANES), f32)],
        args=(y, gcum, u, w, qk))


def _gdn_bwd(y, gcum, u_all, w_all, qk_all, tinv_all, sall, do, comm=None):
    t = y.shape[0]
    nch = t // CHUNK

    def body(y_ref, g_ref, u_ref, w_ref, qk_ref, tinv_ref, sall_ref, do_ref, dy_ref, dg_ref, ds_ref):
        @pl.when(pl.program_id(0) == 0)
        def _():
            ds_ref[...] = jnp.zeros_like(ds_ref)

        masks = _chunk_masks()
        causal, strict, eye = masks
        lane = _lane((CHUNK, LANES))
        row = lax.broadcasted_iota(jnp.int32, (CHUNK, 1), 0)
        heads = range(GDN_HEADS)
        each = lambda f, *ls: [f(*a) for a in zip(*ls)]
        rsum = lambda a: jnp.sum(a, axis=1, keepdims=True)
        sl = [slice(h * LANES, (h + 1) * LANES) for h in heads]
        qn = [y_ref[:, sl[h]] for h in heads]
        kn = [y_ref[:, 512 + h * LANES:512 + (h + 1) * LANES] for h in heads]
        v = [y_ref[:, 1024 + h * LANES:1024 + (h + 1) * LANES] for h in heads]
        beta = [g_ref[:, h:h + 1] for h in heads]
        gam = [g_ref[:, 4 + h:5 + h] for h in heads]
        gam_last = [g[CHUNK - 1:CHUNK, :] for g in gam]
        dec = [_gdn_decay(g, masks) for g in gam]
        e = [jnp.exp(g) for g in gam]
        f = each(lambda gl_, g: jnp.exp(gl_ - g), gam_last, gam)
        gl = [jnp.exp(g) for g in gam_last]
        u = [u_ref[:, sl[h]] for h in heads]
        w = [w_ref[:, sl[h]] for h in heads]
        qk = [qk_ref[0, h] for h in heads]
        tinv = [tinv_ref[0, h] for h in heads]
        s = [sall_ref[0, h] for h in heads]
        dsn = [ds_ref[h] for h in heads]
        d_o = [do_ref[:, sl[h]] for h in heads]
        qd = each(lambda a, b: a * b, qn, e)
        kd = each(lambda a, b: a * b, kn, f)
        ws = each(_mx, w, s)
        kds = each(_mx, kd, dsn)
        qkdo = each(lambda a, b: _mx(a, b, TN), qk, d_o)
        dqd = each(lambda a, b: _mx(a, b, NT), d_o, s)
        qddo = each(lambda a, b: _mx(a, b, TN), qd, d_o)
        kkd = each(lambda k, d: _mx(k, k, NT) * d, kn, dec)
        vn = each(lambda a, b: a - b, u, ws)
        dvn = each(lambda a, b: a + b, qkdo, kds)
        dqk = each(lambda a, b: jnp.where(causal, _mx(a, b, NT), 0.0), d_o, vn)
        dkd = each(lambda a, b: _mx(a, b, NT), vn, dsn)
        dw = each(lambda a, b: -_mx(a, b, NT), dvn, s)
        wdvn = each(lambda a, b: _mx(a, b, TN), w, dvn)
        dgl = each(lambda a, b: jnp.sum(rsum(a * b), axis=0, keepdims=True), dsn, s)
        for h in heads:
            ds_ref[h] = qddo[h] - wdvn[h] + gl[h] * dsn[h]
        dru = each(lambda a, b: _mx(a, b, TN), tinv, dvn)
        drw = each(lambda a, b: _mx(a, b, TN), tinv, dw)
        dqkr = each(lambda a, b: a * b, dqk, dec)
        dq1 = each(_mx, dqkr, kn)
        dk1 = each(lambda a, b: _mx(a, b, TN), dqkr, qn)
        dnu = each(lambda a, b: _mx(a, b, NT), dru, u)
        dnw = each(lambda a, b: _mx(a, b, NT), drw, w)
        dn = each(lambda a, b: jnp.where(strict, -(a + b), 0.0), dnu, dnw)
        dkk = each(lambda a, b, d: a * b * d, dn, beta, dec)
        dk2 = each(_mx, dkk, kn)
        dk3 = each(lambda a, b: _mx(a, b, TN), dkk, kn)
        dgates = jnp.zeros((CHUNK, LANES), f32)
        for h in heads:
            drw_k = rsum(drw[h] * kn[h])
            dbeta = rsum(dru[h] * v[h]) + e[h] * drw_k + rsum(dn[h] * kkd[h])
            m = dn[h] * (kkd[h] * beta[h]) + dqk[h] * qk[h]
            de = beta[h] * drw_k + rsum(dqd[h] * qn[h])
            df = rsum(dkd[h] * kn[h])
            dgam = rsum(m) - _row_to_col(jnp.sum(m, axis=0, keepdims=True), eye) + de * e[h] - df * f[h]
            dgam_last = jnp.sum(df * f[h], axis=0, keepdims=True) + dgl[h] * gl[h]
            dgam = dgam + jnp.where(row == CHUNK - 1, dgam_last, 0.0)
            dy_ref[:, sl[h]] = dq1[h] + dqd[h] * e[h]
            dy_ref[:, 512 + h * LANES:512 + (h + 1) * LANES] = (beta[h] * e[h]) * drw[h] + dk2[h] + dk3[h] + dk1[h] + dkd[h] * f[h]
            dy_ref[:, 1024 + h * LANES:1024 + (h + 1) * LANES] = beta[h] * dru[h]
            dgates = dgates + jnp.where(lane == h, dbeta, 0.0) + jnp.where(lane == 4 + h, dgam, 0.0)
        dg_ref[...] = dgates

    rev = lambda width: pl.BlockSpec((CHUNK, width), lambda n: (nch - 1 - n, 0))
    mat = lambda d: pl.BlockSpec((1, GDN_HEADS, d, d), lambda n: (nch - 1 - n, 0, 0, 0))
    return _hosted(
        body, comm, name="gdn_bwd", grid=(nch,),
        in_specs=[rev(GDN_QKV), rev(LANES), rev(512), rev(512), mat(CHUNK), mat(CHUNK), mat(LANES), rev(512)],
        out_specs=(rev(GDN_QKV), rev(LANES)),
        out_shape=(jax.ShapeDtypeStruct((t, GDN_QKV), f32), jax.ShapeDtypeStruct((t, LANES), f32)),
        scratch_shapes=[pltpu.VMEM((GDN_HEADS, LANES, LANES), f32)],
        args=(y, gcum, u_all, w_all, qk_all, tinv_all, sall, do))


FOX_CLASSES = 4


def _fox_groups(t):
    nq = t // FOX_BQ
    ncls = min(FOX_CLASSES, nq)
    per = nq // ncls
    return [(g * per, per, (g + 1) * per * FOX_BQ) for g in range(ncls)]


def _fox_scores(q_ref, k_ref, gcum_ref, gcumt_ref, h, i, keys):
    pr = h // 2
    lo = (h % 2) * FOX_DH
    lane = _lane((FOX_BQ, LANES))
    mask = (lane >= lo) & (lane < lo + FOX_DH)
    qm = jnp.where(mask, q_ref[:, pr * LANES:(pr + 1) * LANES], 0.0).astype(bf16)
    kp = k_ref[:, pr * LANES:(pr + 1) * LANES].astype(bf16)
    s = _dot_nt(qm, kp, None) * (FOX_DH ** -0.5)
    s = s + gcum_ref[:, 8 + h:9 + h] - gcumt_ref[8 + h:9 + h, :]
    rows = i * FOX_BQ + lax.broadcasted_iota(jnp.int32, (FOX_BQ, keys), 0)
    cols = lax.broadcasted_iota(jnp.int32, (FOX_BQ, keys), 1)
    return jnp.where(cols <= rows, s, NEG), mask, qm, kp


def _fox_fwd(proj, gcum, gcumt, ride=None):
    c0 = SEG_FOX // 512

    def group_call(q0, nq, keys, comm):
        def body(q_ref, k_ref, v_ref, gcum_ref, gcumt_ref, o_ref, lse_ref):
            i = q0 + pl.program_id(0)
            lane = _lane((FOX_BQ, LANES))
            lse_all = jnp.zeros((FOX_BQ, LANES), f32)
            for pr in range(FOX_HEADS // 2):
                vp = v_ref[:, pr * LANES:(pr + 1) * LANES].astype(bf16)
                o_pair = jnp.zeros((FOX_BQ, LANES), f32)
                for h in (2 * pr, 2 * pr + 1):
                    s, mask, _, _ = _fox_scores(q_ref, k_ref, gcum_ref, gcumt_ref, h, i, keys)
                    m = jnp.max(s, axis=1, keepdims=True)
                    p = jnp.exp(s - m)
                    l = jnp.sum(p, axis=1, keepdims=True)
                    o_h = _dot((p * (1.0 / l)).astype(bf16), vp, None)
                    o_pair = jnp.where(mask, o_h, o_pair)
                    lse_all = jnp.where(lane == h, m + jnp.log(l), lse_all)
                o_ref[:, pr * LANES:(pr + 1) * LANES] = o_pair
            lse_ref[...] = lse_all

        seen = lambda col: pl.BlockSpec((keys, 512), lambda i: (0, col))
        return _hosted(
            body, comm, name=f"fox_fwd_{keys}", grid=(nq,),
            in_specs=[pl.BlockSpec((FOX_BQ, 512), lambda i: (q0 + i, c0)), seen(c0 + 1), seen(c0 + 2),
                      pl.BlockSpec((FOX_BQ, LANES), lambda i: (q0 + i, 0)), pl.BlockSpec((LANES, keys), lambda i: (0, 0))],
            out_specs=(pl.BlockSpec((FOX_BQ, 512), lambda i: (i, 0)), pl.BlockSpec((FOX_BQ, LANES), lambda i: (i, 0))),
            out_shape=(jax.ShapeDtypeStruct((nq * FOX_BQ, 512), f32), jax.ShapeDtypeStruct((nq * FOX_BQ, LANES), f32)),
            args=(proj, proj, proj, gcum, gcumt))

    parts = []
    for n, g in enumerate(_fox_groups(proj.shape[0])):
        hook = ride(n) if ride else None
        part, moved = group_call(*g, hook[0] if hook else None)
        parts.append(part)
        if hook:
            hook[1](moved)
    return jnp.concatenate([o for o, _ in parts], axis=0), jnp.concatenate([l for _, l in parts], axis=0)


def _fox_bwd(proj, gcum, gcumt, o, lse, do, ride=None):
    t = proj.shape[0]
    c0 = SEG_FOX // 512

    def group_call(q0, nq, keys, acc, comm):
        first = acc is None

        def body(q_ref, k_ref, v_ref, gcum_ref, gcumt_ref, o_ref, lse_ref, do_ref, *rest):
            dq_ref, dk_ref, dv_ref, dcc_ref, dct_ref = rest[-5:]
            j = pl.program_id(0)
            i = q0 + j

            @pl.when(j == 0)
            def _():
                if first:
                    dk_ref[...] = jnp.zeros_like(dk_ref)
                    dv_ref[...] = jnp.zeros_like(dv_ref)
                    dct_ref[...] = jnp.zeros_like(dct_ref)
                else:
                    dk_ref[...], dv_ref[...], dct_ref[...] = rest[0][...], rest[1][...], rest[2][...]

            lane = _lane((FOX_BQ, LANES))
            dcc = jnp.zeros((FOX_BQ, LANES), f32)
            scale = FOX_DH ** -0.5
            for pr in range(FOX_HEADS // 2):
                sl = slice(pr * LANES, (pr + 1) * LANES)
                vp = v_ref[:, sl].astype(bf16)
                dq_pair = jnp.zeros((FOX_BQ, LANES), f32)
                for h in (2 * pr, 2 * pr + 1):
                    s, mask, qm, kp = _fox_scores(q_ref, k_ref, gcum_ref, gcumt_ref, h, i, keys)
                    p = jnp.exp(s - lse_ref[:, h:h + 1])
                    dom = jnp.where(mask, do_ref[:, sl], 0.0)
                    delta = jnp.sum(dom * o_ref[:, sl], axis=1, keepdims=True)
                    domb = dom.astype(bf16)
                    ds = p * (_dot_nt(domb, vp, None) - delta)
                    dsb = ds.astype(bf16)
                    dv_ref[:, sl] += _dot_tn(p.astype(bf16), domb, None)
                    dk_ref[:, sl] += _dot_tn(dsb, qm, None) * scale
                    dq_pair = jnp.where(mask, _dot(dsb, kp, None) * scale, dq_pair)
                    dcc = jnp.where(lane == 8 + h, jnp.sum(ds, axis=1, keepdims=True), dcc)
                    dct_ref[8 + h:9 + h, :] += -jnp.sum(ds, axis=0, keepdims=True)
                dq_ref[:, sl] = dq_pair.astype(bf16)
            dcc_ref[...] = dcc

        qblk = lambda col: pl.BlockSpec((FOX_BQ, 512), lambda i: (q0 + i, col))
        oblk = pl.BlockSpec((FOX_BQ, 512), lambda i: (i, 0))
        seen = lambda col: pl.BlockSpec((keys, 512), lambda i: (0, col))
        rblk = pl.BlockSpec((FOX_BQ, LANES), lambda i: (q0 + i, 0))
        seen_t = pl.BlockSpec((LANES, keys), lambda i: (0, 0))
        in_specs = [qblk(c0), seen(c0 + 1), seen(c0 + 2), rblk, seen_t, qblk(0), rblk, qblk(0)]
        args = [proj, proj, proj, gcum, gcumt, o, lse, do]
        aliases = {}
        if not first:
            in_specs += [seen(0), seen(0), seen_t]
            args += list(acc)
            aliases = {8: 1, 9: 2, 10: 4}
        return _hosted(
            body, comm, name=f"fox_bwd_{keys}", grid=(nq,), in_specs=in_specs,
            out_specs=(oblk, seen(0), seen(0), pl.BlockSpec((FOX_BQ, LANES), lambda i: (i, 0)), seen_t),
            out_shape=(jax.ShapeDtypeStruct((nq * FOX_BQ, 512), bf16), jax.ShapeDtypeStruct((t, 512), f32), jax.ShapeDtypeStruct((t, 512), f32),
                       jax.ShapeDtypeStruct((nq * FOX_BQ, LANES), f32), jax.ShapeDtypeStruct((LANES, t), f32)),
            aliases=aliases, args=args)

    acc, dqs, dccs = None, [], []
    for n, g in enumerate(reversed(_fox_groups(t))):
        hook = ride(n) if ride else None
        (dq, dk, dv, dcc, dct), moved = group_call(*g, acc, hook[0] if hook else None)
        if hook:
            hook[1](moved)
        acc = (dk, dv, dct)
        dqs.insert(0, dq)
        dccs.insert(0, dcc)
    return jnp.concatenate(dqs, axis=0), acc[0], acc[1], jnp.concatenate(dccs, axis=0), acc[2]


def _row(v, width=None):
    v = v.reshape(1, -1).astype(f32)
    if width is not None and v.shape[1] < width:
        v = jnp.pad(v, ((0, 0), (0, width - v.shape[1])))
    return v


LATE = ("w_out", "w_up", "w_ple_gate", "w_ple", "w_down")


def _device_grads(x, p, target, small, w_cat, conv_w, late, qc=None, tail=None, ln_in_out=None):
    z4 = jnp.zeros((4,), f32)
    bias_row = _row(jnp.concatenate([z4, small["dt_bias"].reshape(-1), small["b_f"].reshape(-1)]), LANES)
    alog_row = _row(jnp.concatenate([z4, small["a_log"].reshape(-1)]), LANES)
    g_gdn = _row(small["gdn_norm_g"])
    g_fox2 = _row(jnp.tile(small["fox_norm_g"].reshape(-1), 2))
    pb = p.astype(bf16)
    late = list(late)
    comm = qc is not None

    h0, h0b = ln_in_out if ln_in_out is not None else _ln_in(x, _row(small["ln_in_g"]), _row(small["ln_in_b"]))[0]
    proj = _mm(h0b, w_cat, "nt", 512, D_CAT, "mm_proj")
    gates, gcum, gcumt = _gates(proj, bias_row, alog_row)
    w_down_pieces = [(4, 0, 1)]

    def gather(phase, pieces):
        if not comm or not pieces:
            return None, lambda moved: None
        touched = sorted({i for i, _, _ in pieces})

        def took(moved):
            for i, buf in zip(touched, moved):
                late[i] = buf
        return phase([late[i] for i in touched], [(touched.index(i), k, n) for i, k, n in pieces]), took

    over, on = _gather_chips, _gather_pass_on
    cm, took = gather(over, [(0, 0, 1), (3, 0, 1)])
    (conv_c, qkv_n), moved = _gdn_conv(proj, conv_w, cm)
    took(moved)
    cm, took = gather(over, [(1, 0, 2)])
    (gu, gw, gqk, gtinv), moved = _gdn_local(qkv_n, gcum, cm)
    took(moved)
    cm, took = gather(over, [(1, 1, 2)])
    (o_gdn, sall), moved = _gdn_fwd(qkv_n, gcum, gu, gw, gqk, cm)
    took(moved)
    fox_plan = [(over, []), (on, [(0, 0, 1), (3, 0, 1), (1, 0, 2), (1, 1, 2)]), (over, [(2, 0, 1)]), (over, [(4, 0, 2)])]
    assert not comm or len(_fox_groups(x.shape[0])) == len(fox_plan)
    o_fox, lse = _fox_fwd(proj, gcum, gcumt, (lambda n: gather(*fox_plan[n])) if comm else None)
    cm, took = gather(on, [(2, 0, 1)])
    (attn,), moved = _attn_post(o_gdn, proj, o_fox, g_gdn, g_fox2, cm)
    took(moved)
    w_out = late[0].reshape(D_MODEL, D_MODEL)
    (h1, h1b, xhat1, rstd1), _ = _ln1(h0, attn, w_out, _row(small["ln1_g"]), _row(small["ln1_b"]))
    w_up, w_ple = late[1], late[3]
    cm, took = gather(over, [(4, 1, 2)])
    up_act = _mm(h1b, w_up, "nn", 512, 1024, "mm_up", epi="relu2", shards=N_CHIPS, comm=cm)
    if cm:
        up_act, moved = up_act
        took(moved)
    up, act = up_act
    w_gate = late[2].reshape(D_MODEL, D_MODEL)
    cm, took = gather(on, w_down_pieces)
    gp = _mm(h1b, w_gate, "nn", 512, D_MODEL, "mm_gate", comm=cm)
    if cm:
        gp, moved = gp
        took(moved)
    w_down = late[4].reshape(D_FF, D_MODEL)
    dr2, dr2b, dpe, dgp, pg2 = _ln2_loss(h1, act, w_down, pb, w_ple, gp, _row(small["b_ple_gate"]), _row(small["ln2_g"]),
                                         _row(small["ln2_b"]), target)

    dup = _mm(dr2b, w_down, "nt", 512, 2048, "mm_dact", epi="relu2_bwd", extra=up)
    g_down = _mm(act, dr2b, "tn", 1024, D_MODEL, "mm_gdown")
    g_up = _mm(h1b, dup, "tn", 1024, 1024, "mm_gup", shards=N_CHIPS)
    g_gate = _mm(h1b, dgp, "tn", 1024, D_MODEL, "mm_ggate")
    g_ple = _mm(pb, dpe, "tn", D_PLE, D_MODEL // N_CHIPS, "mm_gple", shards=N_CHIPS)
    dr1, dr1b, pg1 = _ln1_bwd(dr2, dup, w_up, dgp, w_gate, xhat1, rstd1, _row(small["ln1_g"]))
    g_out = _mm(attn, dr1b, "tn", 1024, D_MODEL, "mm_gout")
    do_gdn, dz, do_fox, pga = _attn_post_bwd(dr1b, w_out, o_gdn, proj, o_fox, g_gdn, g_fox2)
    g_late = [g.reshape((N_CHIPS, -1, g.shape[-1])) for g in (g_out, g_up, g_gate, g_ple, g_down)]
    chip_plan = [[(4, 0, 2), (4, 1, 2), (0, 0, 1)], [(1, 0, 2)], [(1, 1, 2)], [(2, 0, 1), (3, 0, 1)]]
    state = {}

    def to_sibling():
        def took(moved):
            sent = [_add_pair(g, b1, qc, "add_pair_" + n) for g, b1, n in zip(g_late, moved, LATE)]
            state.update(from_sibling=list(moved), sent=sent, landing=_landing(sent))
        return _exchange_pairs(g_late), took

    def to_chips(pieces):
        if not comm:
            return None, lambda moved: None
        return _exchange_chips(state["sent"], state["landing"], pieces), lambda moved: state.update(landing=list(moved))

    def gdn_backward():
        cm, took = to_chips(chip_plan[0])
        state["gdn"], moved = _gdn_bwd(qkv_n, gcum, gu, gw, gqk, gtinv, sall, do_gdn, cm)
        took(moved)

    def fox_ride(n):
        if n == 0:
            return to_sibling()
        if n == 1:
            gdn_backward()
        return to_chips(chip_plan[n])

    assert not comm or len(_fox_groups(x.shape[0])) == len(chip_plan)
    dfq, dfk, dfv, dccol, dct = _fox_bwd(proj, gcum, gcumt, o_fox, lse, do_fox, fox_ride if comm else None)
    if not comm:
        gdn_backward()
    dqkv_n, dgates = state["gdn"]
    dsmall, pgg = _gates_bwd(proj, bias_row, alog_row, gates, dgates, dccol, dct)
    cm = None
    if comm:
        cm = _share_halves([_add_chips(g, b1, b2, qc, "add_chips_" + n)
                            for g, b1, b2, n in zip(g_late, state["from_sibling"], state["landing"], LATE)])
    (du, g_conv8), reduced = _gdn_conv_bwd(proj, conv_w, conv_c, dqkv_n, cm)
    if comm:
        g_late = list(reduced)
    t = x.shape[0]
    dproj = jnp.concatenate([du, dz, dfq, dfk.astype(bf16), dfv.astype(bf16), dsmall, jnp.zeros((t, D_CAT - SEG_SMALL - LANES), bf16)], axis=1)
    g_cat = _mm(dproj, h0b, "tn", 1280, D_MODEL, "mm_gcat")
    cm, took = tail[0](g_cat) if tail else (None, None)
    dh0_mm = _mm(dproj, w_cat, "nn", 512, D_MODEL, "mm_dh0", comm=cm)
    if cm:
        dh0_mm, moved = dh0_mm
        took(moved)
    cm, took = tail[1]() if tail and tail[1] else (None, None)
    (grad_x, pg0), moved = _ln_in_bwd(x, dr1, dh0_mm, _row(small["ln_in_g"]), cm)
    if cm:
        took(moved)

    g_fox = pga[1, :FOX_DH] + pga[1, FOX_DH:]
    small_grads = dict(
        ln_in_g=pg0[0], ln_in_b=pg0[1], ln1_g=pg1[0], ln1_b=pg1[1], b_ple_gate=pg2[2], ln2_g=pg2[0], ln2_b=pg2[1],
        gdn_norm_g=pga[0], fox_norm_g=g_fox, a_log=pgg[1, 4:8], dt_bias=pgg[0, 4:8], b_f=pgg[0, 8:16], loss=pg2[3, 0:1])
    return grad_x, g_cat, g_conv8[:CONV_W], dict(zip(LATE, g_late)), small_grads


ANY = pl.BlockSpec(memory_space=pl.ANY)
CONV_PKT_ROWS = 16


def _mesh_pos():
    return lax.axis_index("x"), lax.axis_index("y"), lax.axis_index("c")


def _other_chips(x, y):
    return [(1 - x, y), (x, 1 - y), (1 - x, 1 - y)]


def _rcopy(src, dst, send_sem, recv_sem, dev):
    return pltpu.make_async_remote_copy(src_ref=src, dst_ref=dst, send_sem=send_sem, recv_sem=recv_sem,
                                        device_id=dev, device_id_type=MESH)


class _Comm:
    def __init__(self, ins, outs, aliases, n_sems, start, finish):
        self.ins, self.outs, self.aliases, self.n_sems, self.start, self.finish = list(ins), list(outs), dict(aliases), n_sems, start, finish


def _hosted(body, comm, *, name, grid, in_specs, out_specs, out_shape, args, scratch_shapes=(), aliases=None):
    n_in, n_out, n_sc = len(in_specs), len(out_specs), len(scratch_shapes)
    k, ko = (len(comm.ins), len(comm.outs)) if comm else (0, 0)

    def kernel_body(*refs):
        o0 = n_in + k
        s0 = o0 + n_out + ko
        if comm:
            cins, couts, (ssem, rsem) = refs[n_in:o0], refs[o0 + n_out:s0], refs[s0 + n_sc:]
            step = pl.program_id(0)
            for d in range(1, len(grid)):
                step = step * grid[d] + pl.program_id(d)

            @pl.when(step == 0)
            def _():
                comm.start(cins, couts, ssem, rsem)

        body(*refs[:n_in], *refs[o0:o0 + n_out], *refs[s0:s0 + n_sc])
        if comm:
            last = 1
            for n in grid:
                last *= n

            @pl.when(step == last - 1)
            def _():
                comm.finish(cins, couts, ssem, rsem)

    io_aliases = dict(aliases or {})
    scratch = list(scratch_shapes)
    if comm:
        io_aliases.update({n_in + i: n_out + j for i, j in comm.aliases.items()})
        scratch += [pltpu.SemaphoreType.DMA((comm.n_sems,)), pltpu.SemaphoreType.DMA((comm.n_sems,))]
    res = pl.pallas_call(
        kernel_body, name=name, grid=grid, in_specs=list(in_specs) + [ANY] * k, out_specs=tuple(out_specs) + (ANY,) * ko,
        out_shape=tuple(out_shape) + tuple(comm.outs if comm else ()), scratch_shapes=scratch, input_output_aliases=io_aliases,
        compiler_params=_params(("arbitrary",) * len(grid)),
    )(*args, *(comm.ins if comm else ()))
    return tuple(res[:n_out]), tuple(res[n_out:])


def _comm_only(phases, name):
    n_in = sum(len(p.ins) for p in phases)

    def body(*refs):
        n_out = sum(len(p.outs) for p in phases)
        sems = refs[n_in + n_out:]
        i0, o0 = 0, n_in
        for j, p in enumerate(phases):
            cins, couts = refs[i0:i0 + len(p.ins)], refs[o0:o0 + len(p.outs)]
            p.start(cins, couts, sems[2 * j], sems[2 * j + 1])
            p.finish(cins, couts, sems[2 * j], sems[2 * j + 1])
            i0 += len(p.ins)
            o0 += len(p.outs)

    aliases, i0, o0 = {}, 0, 0
    for p in phases:
        aliases.update({i0 + i: o0 + j for i, j in p.aliases.items()})
        i0 += len(p.ins)
        o0 += len(p.outs)
    outs = [o for p in phases for o in p.outs]
    res = pl.pallas_call(
        body, name=name, out_shape=tuple(outs), in_specs=[ANY] * n_in, out_specs=(ANY,) * len(outs), input_output_aliases=aliases,
        scratch_shapes=[pltpu.SemaphoreType.DMA((p.n_sems,)) for p in phases for _ in range(2)],
    )(*[a for p in phases for a in p.ins])
    split, o0 = [], 0
    for p in phases:
        split.append(tuple(res[o0:o0 + len(p.outs)]))
        o0 += len(p.outs)
    return split


def _like(arrays):
    return [jax.ShapeDtypeStruct(a.shape, a.dtype) for a in arrays]


def _half(ref, slot, hf, piece=(0, 1)):
    k, n = piece
    rows = ref.shape[1] // 2 // n
    return ref.at[slot, pl.ds((hf * n + k) * rows, rows)]


def _whole_halves(arrays):
    return [(i, 0, 1) for i in range(len(arrays))]


def _gather_chips(bufs, pieces=None, whole=False, base=0):
    nw = len(bufs)
    pieces = _whole_halves(bufs) if pieces is None else pieces
    part = (lambda ref, slot, c, piece: ref.at[slot]) if whole else _half

    def copies(couts):
        x, y, c = _mesh_pos()
        q = 2 * x + y
        for j, (i, k, n) in enumerate(pieces):
            for kc, chip in enumerate(_other_chips(x, y)):
                mine, theirs = part(couts[i], q, c, (k, n)), part(couts[i], 2 * chip[0] + chip[1], c, (k, n))
                yield base + j * 3 + kc, mine, theirs, (*chip, c)

    def start(cins, couts, ssem, rsem):
        for s, mine, _, dev in copies(couts):
            _rcopy(mine, mine, ssem.at[s], rsem.at[s], dev).start()

    def finish(cins, couts, ssem, rsem):
        for s, _, theirs, dev in copies(couts):
            _rcopy(theirs, theirs, ssem.at[s], rsem.at[s], dev).wait_recv()
        for s, mine, _, dev in copies(couts):
            _rcopy(mine, mine, ssem.at[s], rsem.at[s], dev).wait_send()

    return _Comm(bufs, _like(bufs), {i: i for i in range(nw)}, 3 * len(pieces), start, finish)


def _gather_pass_on(bufs, pieces=None, base=0):
    nw = len(bufs)
    pieces = _whole_halves(bufs) if pieces is None else pieces

    def copies(couts):
        x, y, c = _mesh_pos()
        for j, (i, k, n) in enumerate(pieces):
            for kc, chip in enumerate(_other_chips(x, y)):
                slot = 2 * chip[0] + chip[1]
                yield base + j * 3 + kc, _half(couts[i], slot, c, (k, n)), _half(couts[i], slot, 1 - c, (k, n)), (x, y, 1 - c)

    def start(cins, couts, ssem, rsem):
        for s, landed, _, sib in copies(couts):
            _rcopy(landed, landed, ssem.at[s], rsem.at[s], sib).start()

    def finish(cins, couts, ssem, rsem):
        for s, _, passed, sib in copies(couts):
            _rcopy(passed, passed, ssem.at[s], rsem.at[s], sib).wait_recv()
        for s, landed, _, sib in copies(couts):
            _rcopy(landed, landed, ssem.at[s], rsem.at[s], sib).wait_send()

    return _Comm(bufs, _like(bufs), {i: i for i in range(nw)}, 3 * len(pieces), start, finish)


def _gather_now(bufs, packets):
    nb = len(bufs)
    over, on, pk = _gather_chips(bufs), _gather_pass_on(bufs, base=3 * nb), _gather_chips(packets, whole=True, base=6 * nb)

    def start(cins, couts, ssem, rsem):
        over.start(cins[:nb], couts[:nb], ssem, rsem)
        pk.start(cins[nb:], couts[nb:], ssem, rsem)

    def finish(cins, couts, ssem, rsem):
        over.finish(cins[:nb], couts[:nb], ssem, rsem)
        on.start(cins[:nb], couts[:nb], ssem, rsem)
        on.finish(cins[:nb], couts[:nb], ssem, rsem)
        pk.finish(cins[nb:], couts[nb:], ssem, rsem)

    every = list(bufs) + list(packets)
    return _Comm(every, _like(every), {i: i for i in range(len(every))}, 6 * nb + 3 * len(packets), start, finish)


def _exchange_pairs(gs):
    nw = len(gs)

    def copies(cins, couts):
        x, y, c = _mesh_pos()
        for i in range(nw):
            for d in range(N_CHIPS):
                yield i * N_CHIPS + d, _half(cins[i], d, 1 - c), couts[i].at[d], (x, y, 1 - c)

    def start(cins, couts, ssem, rsem):
        for s, src, dst, sib in copies(cins, couts):
            _rcopy(src, dst, ssem.at[s], rsem.at[s], sib).start()

    def finish(cins, couts, ssem, rsem):
        for s, src, dst, sib in copies(cins, couts):
            _rcopy(src, dst, ssem.at[s], rsem.at[s], sib).wait_recv()
        for s, src, dst, sib in copies(cins, couts):
            _rcopy(src, dst, ssem.at[s], rsem.at[s], sib).wait_send()

    outs = [jax.ShapeDtypeStruct((N_CHIPS, g.shape[1] // 2, g.shape[2]), g.dtype) for g in gs]
    return _Comm(gs, outs, {}, N_CHIPS * nw, start, finish)


def _gather_packets(small):
    def peers():
        x, y, c = _mesh_pos()
        for r in range(1, 8):
            fx, fy, fc = (r >> 2) & 1, (r >> 1) & 1, r & 1
            yield r - 1, (1 - x if fx else x, 1 - y if fy else y, 1 - c if fc else c)

    def start(cins, couts, ssem, rsem):
        x, y, c = _mesh_pos()
        mine = couts[0].at[4 * x + 2 * y + c]
        for s, peer in peers():
            _rcopy(mine, mine, ssem.at[s], rsem.at[s], peer).start()

    def finish(cins, couts, ssem, rsem):
        x, y, c = _mesh_pos()
        mine = couts[0].at[4 * x + 2 * y + c]
        for s, peer in peers():
            theirs = couts[0].at[4 * peer[0] + 2 * peer[1] + peer[2]]
            _rcopy(theirs, theirs, ssem.at[s], rsem.at[s], peer).wait_recv()
        for s, peer in peers():
            _rcopy(mine, mine, ssem.at[s], rsem.at[s], peer).wait_send()

    return _Comm([small], _like([small]), {0: 0}, 7, start, finish)


def _exchange_chips(a4s, b2s, pieces=None):
    nw = len(a4s)
    pieces = _whole_halves(a4s) if pieces is None else pieces

    def copies(cins, couts):
        x, y, c = _mesh_pos()
        for j, (i, k, n) in enumerate(pieces):
            rows = a4s[i].shape[1] // n
            part = pl.ds(k * rows, rows)
            for kc, chip in enumerate(_other_chips(x, y)):
                yield j * 3 + kc, cins[i].at[2 * chip[0] + chip[1], part], couts[i].at[kc, part], (*chip, c)

    def start(cins, couts, ssem, rsem):
        for s, src, dst, dev in copies(cins, couts):
            _rcopy(src, dst, ssem.at[s], rsem.at[s], dev).start()

    def finish(cins, couts, ssem, rsem):
        for s, src, dst, dev in copies(cins, couts):
            _rcopy(src, dst, ssem.at[s], rsem.at[s], dev).wait_recv()
        for s, src, dst, dev in copies(cins, couts):
            _rcopy(src, dst, ssem.at[s], rsem.at[s], dev).wait_send()

    return _Comm(list(a4s) + list(b2s), _like(b2s), {nw + i: i for i in range(nw)}, 3 * len(pieces), start, finish)


def _landing(a4s):
    return [lax.empty((3,) + a.shape[1:], a.dtype) for a in a4s]


def _share_halves(rs):
    nw = len(rs)

    def halves(couts, i, hf):
        rows = rs[i].shape[0] // 2
        return couts[i].at[pl.ds(hf * rows, rows)]

    def start(cins, couts, ssem, rsem):
        x, y, c = _mesh_pos()
        for i in range(nw):
            _rcopy(halves(couts, i, c), halves(couts, i, c), ssem.at[i], rsem.at[i], (x, y, 1 - c)).start()

    def finish(cins, couts, ssem, rsem):
        x, y, c = _mesh_pos()
        for i in range(nw):
            _rcopy(halves(couts, i, 1 - c), halves(couts, i, 1 - c), ssem.at[i], rsem.at[i], (x, y, 1 - c)).wait_recv()
        for i in range(nw):
            _rcopy(halves(couts, i, c), halves(couts, i, c), ssem.at[i], rsem.at[i], (x, y, 1 - c)).wait_send()

    return _Comm(rs, _like(rs), {i: i for i in range(nw)}, nw, start, finish)


ADD_ROWS = 256


def _add_pair(g4, b1, qc_idx, name):
    _, half, cols = b1.shape
    rb = ADD_ROWS if half % ADD_ROWS == 0 else half
    nb = half // rb

    def body(qc_ref, g_ref, b_ref, ob_ref):
        ob_ref[...] = (g_ref[...] + b_ref[...]).astype(bf16)

    blk = (1, rb, cols)
    out = pl.BlockSpec(blk, lambda d, i, qc: (d, i, 0))
    return pl.pallas_call(
        body, name=name,
        grid_spec=pltpu.PrefetchScalarGridSpec(
            num_scalar_prefetch=1, grid=(N_CHIPS, nb),
            in_specs=[pl.BlockSpec(blk, lambda d, i, qc: (d, qc[1] * nb + i, 0)), out],
            out_specs=out),
        out_shape=jax.ShapeDtypeStruct(b1.shape, bf16),
        compiler_params=_params(("parallel", "parallel")),
    )(qc_idx, g4, b1)


def _add_chips(g4, b1, b2, qc_idx, name):
    _, half, cols = b1.shape
    rb = ADD_ROWS if half % ADD_ROWS == 0 else half
    nb = half // rb

    def body(qc_ref, g_ref, s_ref, b_ref, o_ref):
        o_ref[...] = (((g_ref[0] + s_ref[0]) + b_ref[0].astype(f32)) + b_ref[1].astype(f32)) + b_ref[2].astype(f32)

    return pl.pallas_call(
        body, name=name,
        grid_spec=pltpu.PrefetchScalarGridSpec(
            num_scalar_prefetch=1, grid=(nb,),
            in_specs=[pl.BlockSpec((1, rb, cols), lambda i, qc: (qc[0], qc[1] * nb + i, 0)),
                      pl.BlockSpec((1, rb, cols), lambda i, qc: (qc[0], i, 0)), pl.BlockSpec((3, rb, cols), lambda i, qc: (0, i, 0))],
            out_specs=pl.BlockSpec((rb, cols), lambda i, qc: (qc[1] * nb + i, 0))),
        out_shape=jax.ShapeDtypeStruct((2 * half, cols), f32),
        compiler_params=_params(("parallel",)),
    )(qc_idx, g4, b1, b2)


def _adamw_math(w, g, m, v):
    m = ADAM_B1 * m + (1.0 - ADAM_B1) * g
    v = ADAM_B2 * v + (1.0 - ADAM_B2) * (g * g)
    m_hat = m / (1.0 - ADAM_B1 ** ADAM_STEP)
    v_hat = v / (1.0 - ADAM_B2 ** ADAM_STEP)
    return -ADAM_LR * (m_hat / (jnp.sqrt(v_hat) + ADAM_EPS) + ADAM_WD * w), m, v


def _adamw(w, g, m, v, name, comm=None):
    rows = w.shape[0]
    if w.ndim == 3:
        rb = max(r for r in range(1, ADD_ROWS // 4 + 1) if rows % r == 0)
    else:
        rb = ADD_ROWS if rows % ADD_ROWS == 0 else rows

    def body(w_ref, g_ref, m_ref, v_ref, go_ref, d_ref, mo_ref, vo_ref):
        g = g_ref[...]
        go_ref[...] = g
        d_ref[...], mo_ref[...], vo_ref[...] = _adamw_math(w_ref[...], g, m_ref[...], v_ref[...])

    blk = pl.BlockSpec((rb,) + w.shape[1:], lambda i: (i,) + (0,) * (w.ndim - 1))
    return _hosted(body, comm, name=name, grid=(rows // rb,), in_specs=[blk] * 4, out_specs=(blk,) * 4,
                   out_shape=(jax.ShapeDtypeStruct(w.shape, f32),) * 4, args=(w, g, m, v))


def _small_sum_adamw(all_pkts, w, m, v):
    names = [n for n, _, _ in SMALL_LAYOUT if n in w]
    place = {n: (r0, size) for n, r0, size in SMALL_LAYOUT}
    rows_of = lambda size: -(-size // LANES)
    flat = lambda a: a.reshape(1, -1)
    k = len(names)

    def body(*refs):
        a_ref, ins = refs[0], refs[1:1 + 3 * k]
        g_ref, outs = refs[1 + 3 * k], refs[2 + 3 * k:2 + 7 * k]
        packs = refs[2 + 7 * k:]
        g = a_ref[0]
        for r in range(1, 8):
            g = g + a_ref[r]
        g_ref[...] = g
        for kind in range(3):
            packs[kind][...] = jnp.zeros_like(packs[kind])
            for j, n in enumerate(names):
                r0, size = place[n]
                for r in range(rows_of(size)):
                    width = min(LANES, size - r * LANES)
                    packs[kind][r0 + r:r0 + r + 1, 0:width] = ins[kind * k + j][:, r * LANES:r * LANES + width]
        results = (g,) + _adamw_math(packs[0][...], g, packs[1][...], packs[2][...])
        for kind, val in enumerate(results):
            for j, n in enumerate(names):
                r0, size = place[n]
                for r in range(rows_of(size)):
                    width = min(LANES, size - r * LANES)
                    outs[kind * k + j][:, r * LANES:r * LANES + width] = val[r0 + r:r0 + r + 1, 0:width]

    args = [all_pkts] + [flat(d[n]) for d in (w, m, v) for n in names]
    out_shape = [jax.ShapeDtypeStruct(all_pkts.shape[1:], f32)] + [jax.ShapeDtypeStruct((1, place[n][1]), f32) for _ in range(4) for n in names]
    res = pl.pallas_call(body, name="small_sum_adamw", out_shape=tuple(out_shape),
                         scratch_shapes=[pltpu.VMEM(all_pkts.shape[1:], f32)] * 3)(*args)
    by_kind = [{n: res[1 + kind * k + j].reshape(w[n].shape) for j, n in enumerate(names)} for kind in range(4)]
    return res[0], by_kind


SMALL_LAYOUT = (("ln_in_g", 0, 1024), ("ln_in_b", 8, 1024), ("ln1_g", 16, 1024), ("ln1_b", 24, 1024), ("b_ple_gate", 32, 1024),
                ("ln2_g", 40, 1024), ("ln2_b", 48, 1024), ("gdn_norm_g", 56, 128), ("fox_norm_g", 57, 64), ("a_log", 58, 4),
                ("dt_bias", 59, 4), ("b_f", 60, 8), ("loss", 61, 1))
SMALL_CONV_ROW = 64
SMALL_ROWS = 128


def _pack_small(vals, conv=None):
    rows = []
    nxt = 0
    for n, r0, size in SMALL_LAYOUT:
        assert r0 == nxt
        v = vals[n].reshape(-1).astype(f32) if n in vals else jnp.zeros((size,), f32)
        nrows = -(-size // LANES)
        rows.append(jnp.pad(v, (0, nrows * LANES - size)).reshape(nrows, LANES))
        nxt = r0 + nrows
    rows.append(jnp.zeros((SMALL_CONV_ROW - nxt, LANES), f32))
    conv_rows = CONV_W * GDN_QKV // LANES
    rows.append(jnp.zeros((conv_rows, LANES), f32) if conv is None else conv.reshape(conv_rows, LANES))
    rows.append(jnp.zeros((SMALL_ROWS - SMALL_CONV_ROW - conv_rows, LANES), f32))
    return jnp.concatenate(rows, axis=0)


def _unpack_small(pkt, shapes):
    out = {}
    for n, r0, size in SMALL_LAYOUT:
        if n in shapes:
            nrows = -(-size // LANES)
            out[n] = pkt[r0:r0 + nrows].reshape(-1)[:size].reshape(shapes[n])
    return out


WEIGHTS = ("ln_in_g", "ln_in_b", "w_in", "conv_w", "a_log", "dt_bias", "gdn_norm_g", "b_f", "fox_norm_g", "w_out", "ln1_g", "ln1_b",
           "w_up", "w_down", "w_ple", "w_ple_gate", "b_ple_gate", "ln2_g", "ln2_b")
SMALL_NAMES = tuple(n for n, _, _ in SMALL_LAYOUT if n != "loss")


def kernel(x, p, ln_in_g, ln_in_b, w_in, conv_w, a_log, dt_bias, gdn_norm_g, b_f, fox_norm_g, w_out, ln1_g, ln1_b, w_up, w_down, w_ple, w_ple_gate, b_ple_gate, ln2_g, ln2_b, loss_target, m_ln_in_g, m_ln_in_b, m_w_in, m_conv_w, m_a_log, m_dt_bias, m_gdn_norm_g, m_b_f, m_fox_norm_g, m_w_out, m_ln1_g, m_ln1_b, m_w_up, m_w_down, m_w_ple, m_w_ple_gate, m_b_ple_gate, m_ln2_g, m_ln2_b, v_ln_in_g, v_ln_in_b, v_w_in, v_conv_w, v_a_log, v_dt_bias, v_gdn_norm_g, v_b_f, v_fox_norm_g, v_w_out, v_ln1_g, v_ln1_b, v_w_up, v_w_down, v_w_ple, v_w_ple_gate, v_b_ple_gate, v_ln2_g, v_ln2_b):
    given = dict(locals())
    w = {n: given[n] for n in WEIGHTS}
    m = {n: given["m_" + n] for n in WEIGHTS}
    v = {n: given["v_" + n] for n in WEIGHTS}
    xi, yi, ci = _mesh_pos()
    q = 2 * xi + yi

    def slot_buffer(val, dtype, slots=N_CHIPS, slot=q, rows=None):
        rows = val.shape[0] if rows is None else rows
        return lax.dynamic_update_slice(lax.empty((slots, rows) + val.shape[1:], dtype), val.astype(dtype)[None], (slot, 0, 0))

    shard_cols = D_IN // N_CHIPS
    conv_rows = CONV_W * GDN_QKV // N_CHIPS // LANES
    conv_pkt = jnp.pad(w["conv_w"][0].reshape(-1, LANES), ((0, CONV_PKT_ROWS - conv_rows), (0, 0)))
    ln_in_out, (w_in4, conv_all) = _ln_in(x[0], _row(w["ln_in_g"]), _row(w["ln_in_b"]),
                                          _gather_now([slot_buffer(w["w_in"][0].T, bf16, rows=W_IN_ROWS)], [slot_buffer(conv_pkt, f32)]))
    conv_full = jnp.concatenate([conv_all[d, :conv_rows].reshape(CONV_W, GDN_QKV // N_CHIPS) for d in range(N_CHIPS)], axis=1)
    wi = jnp.concatenate([w_in4[d, :shard_cols] for d in range(N_CHIPS)], axis=0)
    w_cat = jnp.concatenate([wi[:OFF_BETA], wi[OFF_FOX:OFF_F], wi[OFF_BETA:OFF_FOX], wi[OFF_F:],
                             jnp.zeros((D_CAT - D_IN, D_MODEL), bf16)], axis=0)

    small = {n: w[n] for n in SMALL_NAMES}
    qc = jnp.stack([q, ci]).astype(jnp.int32)
    tail_state = {}

    def pairs_phase(gc):
        g_in = jnp.concatenate([gc[:OFF_BETA], gc[SEG_SMALL:SEG_SMALL + 8], gc[SEG_FOX:SEG_SMALL], gc[SEG_SMALL + 8:SEG_SMALL + 16]], axis=0)
        g_in4 = jnp.stack([jnp.pad(g_in[d * shard_cols:(d + 1) * shard_cols], ((0, W_IN_ROWS - shard_cols), (0, 0))) for d in range(N_CHIPS)])

        def took(moved):
            sent = _add_pair(g_in4, moved[0], qc, "add_pair_w_in")
            tail_state.update(g=g_in4, from_sibling=moved[0], sent=[sent], landing=_landing([sent]))
        return _exchange_pairs([g_in4]), took

    grad_x, _, g_conv, g_late, small_g = _device_grads(
        x[0], p[0, 0], loss_target[0], small, w_cat, conv_full, [slot_buffer(w[n][0], bf16) for n in LATE], qc, tail=(pairs_phase, None),
        ln_in_out=ln_in_out)
    packets = _gather_packets(slot_buffer(_pack_small(small_g, g_conv), f32, 8, 4 * xi + 2 * yi + ci))
    (b2,), (small_all,) = _comm_only([_exchange_chips(tail_state["sent"], tail_state["landing"]), packets], "exchange_chips_w_in")
    (g_late["w_in"],), = _comm_only(
        [_share_halves([_add_chips(tail_state["g"], tail_state["from_sibling"], b2, qc, "add_chips_w_in")])], "share_w_in")

    grads, delta, new_m, new_v = {}, {}, {}, {}
    for n, g in g_late.items():
        if n == "w_in":
            as_stored = lambda a: jnp.transpose(a, (2, 0, 1))
            outs, _ = _adamw(as_stored(w[n]), g[:shard_cols].reshape(shard_cols, 1, D_MODEL), as_stored(m[n]), as_stored(v[n]), "adamw_" + n)
            grads[n], delta[n], new_m[n], new_v[n] = (jnp.transpose(a, (1, 2, 0)) for a in outs)
        else:
            outs, _ = _adamw(w[n][0], g, m[n][0], v[n][0], "adamw_" + n)
            grads[n], delta[n], new_m[n], new_v[n] = (a.reshape(w[n].shape) for a in outs)
    pick = lambda d: {n: d[n] for n in SMALL_NAMES}
    g_pkt, by_kind = _small_sum_adamw(small_all, pick(w), pick(m), pick(v))
    for dst, vals in zip((grads, delta, new_m, new_v), by_kind):
        dst.update(vals)
    conv_rows_all = CONV_W * GDN_QKV // LANES
    conv_g_full = g_pkt[SMALL_CONV_ROW:SMALL_CONV_ROW + conv_rows_all].reshape(CONV_W, GDN_QKV)
    conv_g = lax.dynamic_slice_in_dim(conv_g_full, q * (GDN_QKV // N_CHIPS), GDN_QKV // N_CHIPS, axis=1)
    outs, _ = _adamw(w["conv_w"][0], conv_g, m["conv_w"][0], v["conv_w"][0], "adamw_conv_w")
    grads["conv_w"], delta["conv_w"], new_m["conv_w"], new_v["conv_w"] = (a.reshape(w["conv_w"].shape) for a in outs)
    loss = g_pkt[61, 0]
    return (loss, grad_x[None], *[grads[n] for n in WEIGHTS], *[delta[n] for n in WEIGHTS],
            *[new_m[n] for n in WEIGHTS], *[new_v[n] for n in WEIGHTS])
```

```python
import functools

import jax
import jax.numpy as jnp
from jax import lax
from jax.experimental import pallas as pl
from jax.experimental.pallas import tpu as pltpu

f32 = jnp.float32
bf16 = jnp.bfloat16
HI = lax.Precision.HIGHEST
MESH = pl.DeviceIdType.MESH

D_MODEL = 1024
CHUNK = 64
GDN_HEADS = 4
GDN_DK = 128
FOX_HEADS = 8
FOX_DH = 64
CONV_W = 4
D_FF = 4096
D_PLE = 256
LN_EPS = 1e-5
NORM_EPS = 1e-6
ALPHA = 2.0 ** 0.25
GDN_QKV = 1536
OFF_Z = 1536
OFF_BETA = 2048
OFF_FOX = 2056
OFF_F = 3592
D_IN = 3600
ADAM_LR = 0.001
ADAM_B1 = 0.9
ADAM_B2 = 0.999
ADAM_EPS = 1e-08
ADAM_WD = 0.01
ADAM_STEP = 10

SEG_FOX = 2048
SEG_SMALL = 3584
D_CAT = 3840
LANES = 128
TOK_BLK = 256
FOX_BQ = 256
VMEM_LIMIT = 56 * 1024 * 1024
NEG = -1e30

N_CHIPS = 4
W_IN_ROWS = 928


def _params(sem=None, **kw):
    return pltpu.CompilerParams(dimension_semantics=sem, vmem_limit_bytes=VMEM_LIMIT, **kw)


def _sigmoid(x):
    return 1.0 / (1.0 + jnp.exp(-x))


def _softplus(x):
    return jnp.maximum(x, 0.0) + jnp.log(1.0 + jnp.exp(-jnp.abs(x)))


def _ln_fwd(x, g, b):
    mu = jnp.mean(x, -1, keepdims=True)
    xc = x - mu
    var = jnp.mean(xc * xc, -1, keepdims=True)
    rstd = lax.rsqrt(var + LN_EPS)
    xhat = xc * rstd
    return xhat * g + b, xhat, rstd


def _ln_bwd(dy, xhat, rstd, g):
    dxh = dy * g
    m1 = jnp.mean(dxh, -1, keepdims=True)
    m2 = jnp.mean(dxh * xhat, -1, keepdims=True)
    return rstd * (dxh - m1 - xhat * m2)


def _dot(a, b, prec=HI):
    return jnp.dot(a, b, precision=prec, preferred_element_type=f32)


def _dot_nt(a, b, prec=HI):
    return lax.dot_general(a, b, (((1,), (1,)), ((), ())), precision=prec, preferred_element_type=f32)


def _dot_tn(a, b, prec=HI):
    return lax.dot_general(a, b, (((0,), (0,)), ((), ())), precision=prec, preferred_element_type=f32)


def _bdot(a, b):
    return _dot(a.astype(bf16), b.astype(bf16), None)


def _bdot_nt(a, b):
    return _dot_nt(a.astype(bf16), b.astype(bf16), None)


def _bdot_tn(a, b):
    return _dot_tn(a.astype(bf16), b.astype(bf16), None)


def _lane(shape):
    return lax.broadcasted_iota(jnp.int32, shape, len(shape) - 1)


def _mm(a, b, mode, tm, tn, name, out_dtype=f32, epi=None, extra=None, shards=1, comm=None):
    if mode == "nn":
        (m, k), n = a.shape, b.shape[-1] * shards
    elif mode == "nt":
        (m, k), n = a.shape, b.shape[-2]
    else:
        (k, m), n = a.shape, b.shape[1]
    assert m % tm == 0 and n % tn == 0, (name, m, n, tm, tn)
    per = (n // shards) // tn
    assert mode == "nt" or per * tn * shards == n, (name, n, tn, shards)
    nc = 512 if tn % 512 == 0 else (256 if tn % 256 == 0 else 128)
    ks = k // shards

    def body(a_ref, b_ref, *rest):
        for n0 in range(0, tn, nc):
            if mode == "nn":
                acc = jnp.dot(a_ref[...], b_ref[:, n0:n0 + nc], preferred_element_type=f32)
            elif mode == "nt" and shards > 1:
                acc = jnp.zeros((tm, nc), f32)
                for d in range(shards):
                    acc = acc + lax.dot_general(a_ref[:, d * ks:(d + 1) * ks], b_ref[d, n0:n0 + nc, :], (((1,), (1,)), ((), ())),
                                                preferred_element_type=f32)
            elif mode == "nt":
                acc = lax.dot_general(a_ref[...], b_ref[n0:n0 + nc, :], (((1,), (1,)), ((), ())), preferred_element_type=f32)
            else:
                acc = lax.dot_general(a_ref[...], b_ref[:, n0:n0 + nc], (((0,), (0,)), ((), ())), preferred_element_type=f32)
            if epi == "relu2":
                relu_ref, act_ref = rest
                r = jnp.maximum(acc, 0.0)
                relu_ref[:, n0:n0 + nc] = r.astype(bf16)
                act_ref[:, n0:n0 + nc] = (r * r).astype(bf16)
            elif epi == "relu2_bwd":
                relu_ref, o_ref = rest
                o_ref[:, n0:n0 + nc] = (acc * (2.0 * relu_ref[:, n0:n0 + nc].astype(f32))).astype(bf16)
            else:
                (o_ref,) = rest
                o_ref[:, n0:n0 + nc] = acc.astype(out_dtype)

    if mode == "tn":
        a_spec = pl.BlockSpec((k, tm), lambda j, i: (0, i))
    else:
        a_spec = pl.BlockSpec((tm, k), lambda j, i: (i, 0))
    if mode == "nt" and shards > 1:
        b_spec = pl.BlockSpec((shards, tn, ks), lambda j, i: (0, j, 0))
    elif mode == "nt":
        b_spec = pl.BlockSpec((tn, k), lambda j, i: (j, 0))
    elif mode == "nn" and shards > 1:
        b_spec = pl.BlockSpec((None, k, tn), lambda j, i: (j // per, 0, j % per))
    else:
        b_spec = pl.BlockSpec((k, tn), lambda j, i: (0, j))
    o_spec = pl.BlockSpec((tm, tn), lambda j, i: (i, j))
    in_specs = [a_spec, b_spec]
    args = [a, b]
    if epi == "relu2":
        out_shape = (jax.ShapeDtypeStruct((m, n), bf16), jax.ShapeDtypeStruct((m, n), bf16))
        out_specs = (o_spec, o_spec)
    elif epi == "relu2_bwd":
        in_specs.append(o_spec)
        args.append(extra)
        out_shape = jax.ShapeDtypeStruct((m, n), bf16)
        out_specs = o_spec
    elif mode == "tn" and shards > 1:
        out_shape = jax.ShapeDtypeStruct((shards, m, n // shards), out_dtype)
        out_specs = pl.BlockSpec((None, tm, tn), lambda j, i: (j // per, i, j % per))
    else:
        out_shape = jax.ShapeDtypeStruct((m, n), out_dtype)
        out_specs = o_spec
    single = not isinstance(out_shape, tuple)
    res, moved = _hosted(body, comm, name=name, grid=(n // tn, m // tm), in_specs=in_specs,
                         out_specs=(out_specs,) if single else out_specs, out_shape=(out_shape,) if single else out_shape, args=args)
    res = res[0] if single else res
    return res if comm is None else (res, moved)


def _row_spec(width, col=0):
    return pl.BlockSpec((TOK_BLK, width), lambda i: (i, col))


def _vec_spec(rows, width):
    return pl.BlockSpec((rows, width), lambda i: (0, 0))


def _ln_in(x, g, b, comm=None):
    t, d = x.shape

    def body(x_ref, g_ref, b_ref, h_ref, hb_ref):
        h, _, _ = _ln_fwd(x_ref[...], g_ref[...], b_ref[...])
        h_ref[...] = h
        hb_ref[...] = h.astype(bf16)

    return _hosted(
        body, comm, name="ln_in", grid=(t // TOK_BLK,),
        in_specs=[_row_spec(d), _vec_spec(1, d), _vec_spec(1, d)],
        out_specs=(_row_spec(d), _row_spec(d)),
        out_shape=(jax.ShapeDtypeStruct((t, d), f32), jax.ShapeDtypeStruct((t, d), bf16)),
        args=(x, g, b))


def _attn_post(o_gdn, proj, o_fox, g_gdn, g_fox2, comm=None):
    t = o_gdn.shape[0]

    def body(og_ref, z_ref, of_ref, gg_ref, gf_ref, out_ref):
        for h in range(GDN_HEADS):
            sl = slice(h * LANES, (h + 1) * LANES)
            og = og_ref[:, sl]
            z = z_ref[:, sl]
            r = lax.rsqrt(jnp.mean(og * og, -1, keepdims=True) + NORM_EPS)
            out_ref[:, sl] = (og * r * gg_ref[...] * (z * _sigmoid(z))).astype(bf16)
        lo = _lane((TOK_BLK, LANES)) < FOX_DH
        for pr in range(FOX_HEADS // 2):
            sl = slice(pr * LANES, (pr + 1) * LANES)
            of = of_ref[:, sl]
            sq = of * of
            s0 = jnp.sum(jnp.where(lo, sq, 0.0), -1, keepdims=True)
            s1 = jnp.sum(jnp.where(lo, 0.0, sq), -1, keepdims=True)
            r = lax.rsqrt(jnp.where(lo, s0, s1) * (1.0 / FOX_DH) + NORM_EPS)
            out_ref[:, 512 + pr * LANES:512 + (pr + 1) * LANES] = (of * r * gf_ref[...]).astype(bf16)

    return _hosted(
        body, comm, name="attn_post", grid=(t // TOK_BLK,),
        in_specs=[_row_spec(512), _row_spec(512, OFF_Z // 512), _row_spec(512), _vec_spec(1, LANES), _vec_spec(1, LANES)],
        out_specs=(_row_spec(D_MODEL),),
        out_shape=(jax.ShapeDtypeStruct((t, D_MODEL), bf16),),
        args=(o_gdn, proj, o_fox, g_gdn, g_fox2))


def _attn_post_bwd(dr1b, w_out, o_gdn, proj, o_fox, g_gdn, g_fox2):
    t = o_gdn.shape[0]

    def body(dr_ref, wo_ref, og_ref, z_ref, of_ref, gg_ref, gf_ref, dog_ref, dz_ref, dof_ref, pg_ref):
        i = pl.program_id(0)

        @pl.when(i == 0)
        def _():
            pg_ref[...] = jnp.zeros_like(pg_ref)

        da = _dot_nt(dr_ref[...], wo_ref[...], None)
        dgg = jnp.zeros((1, LANES), f32)
        for h in range(GDN_HEADS):
            sl = slice(h * LANES, (h + 1) * LANES)
            og = og_ref[:, sl]
            z = z_ref[:, sl]
            dout = da[:, sl]
            g = gg_ref[...]
            r = lax.rsqrt(jnp.mean(og * og, -1, keepdims=True) + NORM_EPS)
            sg = _sigmoid(z)
            silu = z * sg
            ng = og * r * g
            dng = dout * silu
            dz_ref[:, sl] = (dout * ng * (sg * (1.0 + z * (1.0 - sg)))).astype(bf16)
            dgg = dgg + jnp.sum(dng * og * r, 0, keepdims=True)
            gd = dng * g
            dog_ref[:, sl] = r * gd - og * (r * r * r) * jnp.mean(og * gd, -1, keepdims=True)
        pg_ref[0:1, :] += dgg
        lo = _lane((TOK_BLK, LANES)) < FOX_DH
        dgf = jnp.zeros((1, LANES), f32)
        for pr in range(FOX_HEADS // 2):
            sl = slice(pr * LANES, (pr + 1) * LANES)
            of = of_ref[:, sl]
            dout = da[:, 512 + pr * LANES:512 + (pr + 1) * LANES]
            g = gf_ref[...]
            sq = of * of
            s0 = jnp.sum(jnp.where(lo, sq, 0.0), -1, keepdims=True)
            s1 = jnp.sum(jnp.where(lo, 0.0, sq), -1, keepdims=True)
            r = lax.rsqrt(jnp.where(lo, s0, s1) * (1.0 / FOX_DH) + NORM_EPS)
            dgf = dgf + jnp.sum(dout * of * r, 0, keepdims=True)
            gd = dout * g
            xg = of * gd
            m0 = jnp.sum(jnp.where(lo, xg, 0.0), -1, keepdims=True)
            m1 = jnp.sum(jnp.where(lo, 0.0, xg), -1, keepdims=True)
            dof_ref[:, sl] = r * gd - of * (r * r * r) * (jnp.where(lo, m0, m1) * (1.0 / FOX_DH))
        pg_ref[1:2, :] += dgf

    return pl.pallas_call(
        body, name="attn_post_bwd", grid=(t // TOK_BLK,),
        in_specs=_product_specs(dr1b, w_out) + [_row_spec(512), _row_spec(512, OFF_Z // 512), _row_spec(512), _vec_spec(1, LANES), _vec_spec(1, LANES)],
        out_specs=(_row_spec(512), _row_spec(512), _row_spec(512), _vec_spec(8, LANES)),
        out_shape=(jax.ShapeDtypeStruct((t, 512), f32), jax.ShapeDtypeStruct((t, 512), bf16),
                   jax.ShapeDtypeStruct((t, 512), f32), jax.ShapeDtypeStruct((8, LANES), f32)),
        compiler_params=_params(("arbitrary",)),
    )(dr1b, w_out, o_gdn, proj, o_fox, g_gdn, g_fox2)


def _product_specs(lhs, rhs):
    return [_row_spec(lhs.shape[1]), pl.BlockSpec(rhs.shape, lambda i: (0, 0))]


def _ln1(h0, lhs, rhs, g, b, comm=None):
    t, d = h0.shape

    def body(h0_ref, lhs_ref, rhs_ref, g_ref, b_ref, h_ref, hb_ref, xh_ref, rs_ref):
        mix = jnp.dot(lhs_ref[...], rhs_ref[...], preferred_element_type=f32)
        h, xhat, rstd = _ln_fwd(ALPHA * h0_ref[...] + mix, g_ref[...], b_ref[...])
        h_ref[...] = h
        hb_ref[...] = h.astype(bf16)
        xh_ref[...] = xhat
        rs_ref[...] = jnp.broadcast_to(rstd, rs_ref.shape)

    return _hosted(
        body, comm, name="ln1", grid=(t // TOK_BLK,),
        in_specs=[_row_spec(d)] + _product_specs(lhs, rhs) + [_vec_spec(1, d), _vec_spec(1, d)],
        out_specs=(_row_spec(d), _row_spec(d), _row_spec(d), _row_spec(LANES)),
        out_shape=(jax.ShapeDtypeStruct((t, d), f32), jax.ShapeDtypeStruct((t, d), bf16),
                   jax.ShapeDtypeStruct((t, d), f32), jax.ShapeDtypeStruct((t, LANES), f32)),
        args=(h0, lhs, rhs, g, b))


def _ln2_loss(h1, lhs, rhs, pb, w_ple, gp, b_gate, g, b, target):
    t, d = h1.shape

    def body(h1_ref, lhs_ref, rhs_ref, pb_ref, wp_ref, gp_ref, bg_ref, g_ref, b_ref, t_ref, dr_ref, drb_ref, dpe_ref, dgp_ref, pg_ref):
        i = pl.program_id(0)

        @pl.when(i == 0)
        def _():
            pg_ref[...] = jnp.zeros_like(pg_ref)

        ff = jnp.dot(lhs_ref[...], rhs_ref[...], preferred_element_type=f32)
        sig = _sigmoid(gp_ref[...] + bg_ref[...])
        pe = jnp.concatenate([jnp.dot(pb_ref[...], wp_ref[s], preferred_element_type=f32) for s in range(w_ple.shape[0])], axis=1)
        r2 = ALPHA * h1_ref[...] + ff + pe * sig
        y, xhat, rstd = _ln_fwd(r2, g_ref[...], b_ref[...])
        err = y - t_ref[...]
        dy = err * (1.0 / d)
        dr = _ln_bwd(dy, xhat, rstd, g_ref[...])
        dr_ref[...] = dr
        drb_ref[...] = dr.astype(bf16)
        dpe_ref[...] = (dr * sig).astype(bf16)
        dgp = dr * pe * sig * (1.0 - sig)
        dgp_ref[...] = dgp.astype(bf16)
        pg_ref[0:1, :] += jnp.sum(dy * xhat, 0, keepdims=True)
        pg_ref[1:2, :] += jnp.sum(dy, 0, keepdims=True)
        pg_ref[2:3, :] += jnp.sum(dgp, 0, keepdims=True)
        pg_ref[3:4, :] += 0.5 * jnp.sum(jnp.mean(err * err, -1, keepdims=True), 0, keepdims=True)

    return pl.pallas_call(
        body, name="ln2_loss", grid=(t // TOK_BLK,),
        in_specs=[_row_spec(d)] + _product_specs(lhs, rhs) + [_row_spec(pb.shape[1]), pl.BlockSpec(w_ple.shape, lambda i: (0, 0, 0)), _row_spec(d)]
        + [_vec_spec(1, d)] * 3 + [_row_spec(d)],
        out_specs=(_row_spec(d), _row_spec(d), _row_spec(d), _row_spec(d), _vec_spec(8, d)),
        out_shape=(jax.ShapeDtypeStruct((t, d), f32), jax.ShapeDtypeStruct((t, d), bf16), jax.ShapeDtypeStruct((t, d), bf16),
                   jax.ShapeDtypeStruct((t, d), bf16), jax.ShapeDtypeStruct((8, d), f32)),
        compiler_params=_params(("arbitrary",)),
    )(h1, lhs, rhs, pb, w_ple, gp, b_gate, g, b, target)


def _ln1_bwd(dr2, dup, w_up, dgp, w_gate, xhat, rstd, g):
    t, d = dr2.shape
    ks = w_up.shape[2]

    def body(dr2_ref, dup_ref, wup_ref, dgp_ref, wg_ref, xh_ref, rs_ref, g_ref, dr_ref, drb_ref, pg_ref):
        i = pl.program_id(0)

        @pl.when(i == 0)
        def _():
            pg_ref[...] = jnp.zeros_like(pg_ref)

        dh = ALPHA * dr2_ref[...] + _dot_nt(dgp_ref[...], wg_ref[...], None)
        for s in range(w_up.shape[0]):
            dh = dh + _dot_nt(dup_ref[:, s * ks:(s + 1) * ks], wup_ref[s], None)
        xhat = xh_ref[...]
        dr = _ln_bwd(dh, xhat, rs_ref[:, 0:1], g_ref[...])
        dr_ref[...] = dr
        drb_ref[...] = dr.astype(bf16)
        pg_ref[0:1, :] += jnp.sum(dh * xhat, 0, keepdims=True)
        pg_ref[1:2, :] += jnp.sum(dh, 0, keepdims=True)

    return pl.pallas_call(
        body, name="ln1_bwd", grid=(t // TOK_BLK,),
        in_specs=[_row_spec(d), _row_spec(dup.shape[1]), pl.BlockSpec(w_up.shape, lambda i: (0, 0, 0))] + _product_specs(dgp, w_gate)
        + [_row_spec(d), _row_spec(LANES), _vec_spec(1, d)],
        out_specs=(_row_spec(d), _row_spec(d), _vec_spec(8, d)),
        out_shape=(jax.ShapeDtypeStruct((t, d), f32), jax.ShapeDtypeStruct((t, d), bf16), jax.ShapeDtypeStruct((8, d), f32)),
        compiler_params=_params(("arbitrary",)),
    )(dr2, dup, w_up, dgp, w_gate, xhat, rstd, g)


def _ln_in_bwd(x, dr1, dmm, g, comm=None):
    t, d = x.shape

    def body(x_ref, dr1_ref, dmm_ref, g_ref, dx_ref, pg_ref):
        i = pl.program_id(0)

        @pl.when(i == 0)
        def _():
            pg_ref[...] = jnp.zeros_like(pg_ref)

        dh = ALPHA * dr1_ref[...] + dmm_ref[...]
        _, xhat, rstd = _ln_fwd(x_ref[...], g_ref[...], 0.0)
        dx_ref[...] = _ln_bwd(dh, xhat, rstd, g_ref[...])
        pg_ref[0:1, :] += jnp.sum(dh * xhat, 0, keepdims=True)
        pg_ref[1:2, :] += jnp.sum(dh, 0, keepdims=True)

    return _hosted(
        body, comm, name="ln_in_bwd", grid=(t // TOK_BLK,),
        in_specs=[_row_spec(d)] * 3 + [_vec_spec(1, d)],
        out_specs=(_row_spec(d), _vec_spec(8, d)),
        out_shape=(jax.ShapeDtypeStruct((t, d), f32), jax.ShapeDtypeStruct((8, d), f32)),
        args=(x, dr1, dmm, g))


def _tri(n, upper=False, strict=False):
    r = lax.broadcasted_iota(jnp.int32, (n, n), 0)
    c = lax.broadcasted_iota(jnp.int32, (n, n), 1)
    if upper:
        m = (c > r) if strict else (c >= r)
    else:
        m = (c < r) if strict else (c <= r)
    return jnp.where(m, 1.0, 0.0).astype(f32)


def _gate_values(x, bias, alog, lane):
    z = x + bias
    return jnp.where(lane < 4, _sigmoid(z), jnp.where(lane < 8, -jnp.exp(alog) * _softplus(z), jnp.where(lane < 16, -_softplus(-z), 0.0)))


def _gates(proj, bias_row, alog_row):
    t = proj.shape[0]
    nch = t // CHUNK

    def body(x_ref, bias_ref, alog_ref, gates_ref, gcum_ref, gcumt_ref):
        lane = _lane((t, LANES))
        gates = _gate_values(x_ref[...], bias_ref[...], alog_ref[...], lane)
        gates_ref[...] = gates
        g3 = gates.reshape(nch, CHUNK, LANES)
        tri = jnp.broadcast_to(_tri(CHUNK)[None], (nch, CHUNK, CHUNK))
        loc = jnp.einsum("bij,bjk->bik", tri, g3, precision=HI, preferred_element_type=f32)
        tot = jnp.sum(g3, axis=1)
        offs = _dot(_tri(nch, strict=True), tot)
        glob = loc + offs[:, None, :]
        lane3 = _lane((nch, CHUNK, LANES))
        gcum = jnp.where(lane3 < 4, g3, jnp.where(lane3 < 8, loc, glob)).reshape(t, LANES)
        gcum_ref[...] = gcum
        gcumt_ref[...] = gcum.T

    return pl.pallas_call(
        body, name="gates", grid=(1,),
        in_specs=[pl.BlockSpec((t, LANES), lambda i: (0, SEG_SMALL // LANES)), _vec_spec(1, LANES), _vec_spec(1, LANES)],
        out_specs=(pl.BlockSpec((t, LANES), lambda i: (0, 0)), pl.BlockSpec((t, LANES), lambda i: (0, 0)),
                   pl.BlockSpec((LANES, t), lambda i: (0, 0))),
        out_shape=(jax.ShapeDtypeStruct((t, LANES), f32), jax.ShapeDtypeStruct((t, LANES), f32), jax.ShapeDtypeStruct((LANES, t), f32)),
        compiler_params=_params(("arbitrary",)),
    )(proj, bias_row, alog_row)


def _gates_bwd(proj, bias_row, alog_row, gates, dgates, dccol, dct):
    t = proj.shape[0]
    nch = t // CHUNK

    def body(x_ref, bias_ref, alog_ref, gates_ref, dg_ref, dcc_ref, dct_ref, dx_ref, pg_ref):
        lane = _lane((t, LANES))
        d = dg_ref[...] + dcc_ref[...] + dct_ref[...].T
        d3 = d.reshape(nch, CHUNK, LANES)
        tri = jnp.broadcast_to(_tri(CHUNK, upper=True)[None], (nch, CHUNK, CHUNK))
        loc = jnp.einsum("bij,bjk->bik", tri, d3, precision=HI, preferred_element_type=f32)
        tot = jnp.sum(d3, axis=1)
        offs = _dot(_tri(nch, upper=True, strict=True), tot)
        glob = loc + offs[:, None, :]
        lane3 = _lane((nch, CHUNK, LANES))
        dpre = jnp.where(lane3 < 4, d3, jnp.where(lane3 < 8, loc, glob)).reshape(t, LANES)
        z = x_ref[...] + bias_ref[...]
        sg = _sigmoid(z)
        dx = jnp.where(lane < 4, dpre * sg * (1.0 - sg),
                       jnp.where(lane < 8, dpre * (-jnp.exp(alog_ref[...])) * sg, jnp.where(lane < 16, dpre * (1.0 - sg), 0.0)))
        dx_ref[...] = dx.astype(bf16)
        pg_ref[...] = jnp.zeros_like(pg_ref)
        pg_ref[0:1, :] = jnp.sum(dx, 0, keepdims=True)
        pg_ref[1:2, :] = jnp.sum(jnp.where((lane >= 4) & (lane < 8), dpre * gates_ref[...], 0.0), 0, keepdims=True)

    full = pl.BlockSpec((t, LANES), lambda i: (0, 0))
    return pl.pallas_call(
        body, name="gates_bwd", grid=(1,),
        in_specs=[pl.BlockSpec((t, LANES), lambda i: (0, SEG_SMALL // LANES)), _vec_spec(1, LANES), _vec_spec(1, LANES),
                  full, full, full, pl.BlockSpec((LANES, t), lambda i: (0, 0))],
        out_specs=(full, _vec_spec(8, LANES)),
        out_shape=(jax.ShapeDtypeStruct((t, LANES), bf16), jax.ShapeDtypeStruct((8, LANES), f32)),
        compiler_params=_params(("arbitrary",)),
    )(proj, bias_row, alog_row, gates, dgates, dccol, dct)


def _conv_act(u, cw, row, t):
    c = cw[3:4, :] * u
    for jj in range(CONV_W - 1):
        sh = CONV_W - 1 - jj
        c = c + cw[jj:jj + 1, :] * jnp.where(row >= sh, pltpu.roll(u, sh, axis=0), 0.0)
    return c


def _gdn_conv(proj, conv_w, comm=None):
    t = proj.shape[0]
    nblk = GDN_QKV // LANES

    def body(u_ref, cw_ref, c_ref, y_ref):
        j = pl.program_id(0)
        row = lax.broadcasted_iota(jnp.int32, (t, LANES), 0)
        c = _conv_act(u_ref[...], cw_ref[...], row, t)
        c_ref[...] = c
        s = c * _sigmoid(c)
        r = lax.rsqrt(jnp.sum(s * s, -1, keepdims=True) + NORM_EPS)
        scale = jnp.where(j < GDN_HEADS, GDN_DK ** -0.5, 1.0)
        y_ref[...] = jnp.where(j < 2 * GDN_HEADS, s * (r * scale), s)

    blk = pl.BlockSpec((t, LANES), lambda j: (0, j))
    return _hosted(
        body, comm, name="gdn_conv", grid=(nblk,),
        in_specs=[blk, pl.BlockSpec((CONV_W, LANES), lambda j: (0, j))],
        out_specs=(blk, blk),
        out_shape=(jax.ShapeDtypeStruct((t, GDN_QKV), f32), jax.ShapeDtypeStruct((t, GDN_QKV), f32)),
        args=(proj, conv_w))


def _gdn_conv_bwd(proj, conv_w, c, dy, comm=None):
    t = proj.shape[0]
    nblk = GDN_QKV // LANES

    def body(u_ref, cw_ref, c_ref, dy_ref, du_ref, dcw_ref):
        j = pl.program_id(0)
        row = lax.broadcasted_iota(jnp.int32, (t, LANES), 0)
        u = u_ref[...]
        cw = cw_ref[...]
        c = c_ref[...]
        dy = dy_ref[...]
        sg = _sigmoid(c)
        s = c * sg
        r = lax.rsqrt(jnp.sum(s * s, -1, keepdims=True) + NORM_EPS)
        n = s * r
        scale = jnp.where(j < GDN_HEADS, GDN_DK ** -0.5, 1.0)
        dn = dy * scale
        ds = jnp.where(j < 2 * GDN_HEADS, r * (dn - n * jnp.sum(dn * n, -1, keepdims=True)), dy)
        dc = ds * (sg * (1.0 + c * (1.0 - sg)))
        du = cw[3:4, :] * dc
        dcw_ref[...] = jnp.zeros_like(dcw_ref)
        dcw_ref[3:4, :] = jnp.sum(dc * u, 0, keepdims=True)
        for jj in range(CONV_W - 1):
            sh = CONV_W - 1 - jj
            du = du + cw[jj:jj + 1, :] * jnp.where(row < t - sh, pltpu.roll(dc, t - sh, axis=0), 0.0)
            dcw_ref[jj:jj + 1, :] = jnp.sum(dc * jnp.where(row >= sh, pltpu.roll(u, sh, axis=0), 0.0), 0, keepdims=True)
        du_ref[...] = du.astype(bf16)

    blk = pl.BlockSpec((t, LANES), lambda j: (0, j))
    return _hosted(
        body, comm, name="gdn_conv_bwd", grid=(nblk,),
        in_specs=[blk, pl.BlockSpec((CONV_W, LANES), lambda j: (0, j)), blk, blk],
        out_specs=(blk, pl.BlockSpec((8, LANES), lambda j: (0, j))),
        out_shape=(jax.ShapeDtypeStruct((t, GDN_QKV), bf16), jax.ShapeDtypeStruct((8, GDN_QKV), f32)),
        args=(proj, conv_w, c, dy))


def _chunk_masks():
    r = lax.broadcasted_iota(jnp.int32, (CHUNK, CHUNK), 0)
    c = lax.broadcasted_iota(jnp.int32, (CHUNK, CHUNK), 1)
    return r >= c, r > c, r == c


def _col_to_row(col, eye):
    return jnp.sum(jnp.where(eye, col, 0.0), axis=0, keepdims=True)


def _row_to_col(row, eye):
    return jnp.sum(jnp.where(eye, row, 0.0), axis=1, keepdims=True)


NN = (((1,), (0,)), ((), ()))
NT = (((1,), (1,)), ((), ()))
TN = (((0,), (0,)), ((), ()))
GDN_GROUP = 4


def _mx(a, b, dims=NN, passes=1):
    d = lambda p, q: lax.dot_general(p, q, dims, preferred_element_type=f32)
    ah, bh = a.astype(bf16), b.astype(bf16)
    if passes == 1:
        return d(ah, bh)
    al = (a - ah.astype(f32)).astype(bf16)
    bl = (b - bh.astype(f32)).astype(bf16)
    return d(ah, bh) + (d(ah, bl) + d(al, bh))


def _gdn_decay(gam, masks):
    causal, _, eye = masks
    return jnp.exp(jnp.where(causal, gam - _col_to_row(gam, eye), NEG))


def _gdn_local(y, gcum, comm=None):
    t = y.shape[0]
    nch = t // CHUNK
    rows_blk = GDN_GROUP * CHUNK

    def body(y_ref, g_ref, u_ref, w_ref, qk_ref, tinv_ref):
        masks = _chunk_masks()
        _, strict, eye = masks
        ids = [(j, h) for j in range(GDN_GROUP) for h in range(GDN_HEADS)]
        rs = lambda j: slice(j * CHUNK, (j + 1) * CHUNK)
        col = lambda base, h: slice(base + h * LANES, base + (h + 1) * LANES)
        kn = [y_ref[rs(j), col(512, h)] for j, h in ids]
        beta = [g_ref[rs(j), h:h + 1] for j, h in ids]
        gam = [g_ref[rs(j), 4 + h:5 + h] for j, h in ids]
        dec = [_gdn_decay(g, masks) for g in gam]
        x = [-jnp.where(strict, _mx(k, k, NT) * d * b, 0.0) for k, d, b in zip(kn, dec, beta)]
        tinv = [jnp.where(eye, 1.0, 0.0) + a for a in x]
        for _ in range(5):
            x = [_mx(a, a, NN, 3) for a in x]
            tinv = [t_ + _mx(t_, a, NN, 3) for t_, a in zip(tinv, x)]
        for (j, h), t_, k, d, b, g in zip(ids, tinv, kn, dec, beta, gam):
            u_ref[rs(j), col(0, h)] = _mx(t_, b * y_ref[rs(j), col(1024, h)])
            w_ref[rs(j), col(0, h)] = _mx(t_, (b * jnp.exp(g)) * k)
            qk_ref[j, h] = _mx(y_ref[rs(j), col(0, h)], k, NT) * d
            tinv_ref[j, h] = t_

    mat = pl.BlockSpec((GDN_GROUP, GDN_HEADS, CHUNK, CHUNK), lambda n: (n, 0, 0, 0))
    return _hosted(
        body, comm, name="gdn_local", grid=(nch // GDN_GROUP,),
        in_specs=[pl.BlockSpec((rows_blk, GDN_QKV), lambda n: (n, 0)), pl.BlockSpec((rows_blk, LANES), lambda n: (n, 0))],
        out_specs=(pl.BlockSpec((rows_blk, 512), lambda n: (n, 0)), pl.BlockSpec((rows_blk, 512), lambda n: (n, 0)), mat, mat),
        out_shape=(jax.ShapeDtypeStruct((t, 512), f32), jax.ShapeDtypeStruct((t, 512), f32),
                   jax.ShapeDtypeStruct((nch, GDN_HEADS, CHUNK, CHUNK), f32), jax.ShapeDtypeStruct((nch, GDN_HEADS, CHUNK, CHUNK), f32)),
        args=(y, gcum))


def _gdn_fwd(y, gcum, u, w, qk, comm=None):
    t = y.shape[0]
    nch = t // CHUNK

    def body(y_ref, g_ref, u_ref, w_ref, qk_ref, o_ref, sall_ref, s_ref):
        @pl.when(pl.program_id(0) == 0)
        def _():
            s_ref[...] = jnp.zeros_like(s_ref)

        heads = range(GDN_HEADS)
        sl = [slice(h * LANES, (h + 1) * LANES) for h in heads]
        gam = [g_ref[:, 4 + h:5 + h] for h in heads]
        gam_last = [g[CHUNK - 1:CHUNK, :] for g in gam]
        s = [s_ref[h] for h in heads]
        for h in heads:
            sall_ref[0, h] = s[h]
        ws = [_mx(w_ref[:, sl[h]], s[h]) for h in heads]
        qs = [_mx(y_ref[:, sl[h]] * jnp.exp(gam[h]), s[h]) for h in heads]
        vn = [u_ref[:, sl[h]] - ws[h] for h in heads]
        av = [_mx(qk_ref[0, h], vn[h]) for h in heads]
        kv = [_mx(y_ref[:, 512 + h * LANES:512 + (h + 1) * LANES] * jnp.exp(gam_last[h] - gam[h]), vn[h], TN) for h in heads]
        for h in heads:
            o_ref[:, sl[h]] = qs[h] + av[h]
            s_ref[h] = jnp.exp(gam_last[h]) * s[h] + kv[h]

    row = lambda width: pl.BlockSpec((CHUNK, width), lambda n: (n, 0))
    return _hosted(
        body, comm, name="gdn_fwd", grid=(nch,),
        in_specs=[row(GDN_QKV), row(LANES), row(512), row(512), pl.BlockSpec((1, GDN_HEADS, CHUNK, CHUNK), lambda n: (n, 0, 0, 0))],
        out_specs=(row(512), pl.BlockSpec((1, GDN_HEADS, LANES, LANES), lambda n: (n, 0, 0, 0))),
        out_shape=(jax.ShapeDtypeStruct((t, 512), f32), jax.ShapeDtypeStruct((nch, GDN_HEADS, LANES, LANES), f32)),
        scratch_shapes=[pltpu.VMEM((GDN_HEADS, LANES, LANES), f32)],
        args=(y, gcum, u, w, qk))


def _gdn_bwd(y, gcum, u_all, w_all, qk_all, tinv_all, sall, do, comm=None):
    t = y.shape[0]
    nch = t // CHUNK

    def body(y_ref, g_ref, u_ref, w_ref, qk_ref, tinv_ref, sall_ref, do_ref, dy_ref, dg_ref, ds_ref):
        @pl.when(pl.program_id(0) == 0)
        def _():
            ds_ref[...] = jnp.zeros_like(ds_ref)

        masks = _chunk_masks()
        causal, strict, eye = masks
        lane = _lane((CHUNK, LANES))
        row = lax.broadcasted_iota(jnp.int32, (CHUNK, 1), 0)
        heads = range(GDN_HEADS)
        each = lambda f, *ls: [f(*a) for a in zip(*ls)]
        rsum = lambda a: jnp.sum(a, axis=1, keepdims=True)
        sl = [slice(h * LANES, (h + 1) * LANES) for h in heads]
        qn = [y_ref[:, sl[h]] for h in heads]
        kn = [y_ref[:, 512 + h * LANES:512 + (h + 1) * LANES] for h in heads]
        v = [y_ref[:, 1024 + h * LANES:1024 + (h + 1) * LANES] for h in heads]
        beta = [g_ref[:, h:h + 1] for h in heads]
        gam = [g_ref[:, 4 + h:5 + h] for h in heads]
        gam_last = [g[CHUNK - 1:CHUNK, :] for g in gam]
        dec = [_gdn_decay(g, masks) for g in gam]
        e = [jnp.exp(g) for g in gam]
        f = each(lambda gl_, g: jnp.exp(gl_ - g), gam_last, gam)
        gl = [jnp.exp(g) for g in gam_last]
        u = [u_ref[:, sl[h]] for h in heads]
        w = [w_ref[:, sl[h]] for h in heads]
        qk = [qk_ref[0, h] for h in heads]
        tinv = [tinv_ref[0, h] for h in heads]
        s = [sall_ref[0, h] for h in heads]
        dsn = [ds_ref[h] for h in heads]
        d_o = [do_ref[:, sl[h]] for h in heads]
        qd = each(lambda a, b: a * b, qn, e)
        kd = each(lambda a, b: a * b, kn, f)
        ws = each(_mx, w, s)
        kds = each(_mx, kd, dsn)
        qkdo = each(lambda a, b: _mx(a, b, TN), qk, d_o)
        dqd = each(lambda a, b: _mx(a, b, NT), d_o, s)
        qddo = each(lambda a, b: _mx(a, b, TN), qd, d_o)
        kkd = each(lambda k, d: _mx(k, k, NT) * d, kn, dec)
        vn = each(lambda a, b: a - b, u, ws)
        dvn = each(lambda a, b: a + b, qkdo, kds)
        dqk = each(lambda a, b: jnp.where(causal, _mx(a, b, NT), 0.0), d_o, vn)
        dkd = each(lambda a, b: _mx(a, b, NT), vn, dsn)
        dw = each(lambda a, b: -_mx(a, b, NT), dvn, s)
        wdvn = each(lambda a, b: _mx(a, b, TN), w, dvn)
        dgl = each(lambda a, b: jnp.sum(rsum(a * b), axis=0, keepdims=True), dsn, s)
        for h in heads:
            ds_ref[h] = qddo[h] - wdvn[h] + gl[h] * dsn[h]
        dru = each(lambda a, b: _mx(a, b, TN), tinv, dvn)
        drw = each(lambda a, b: _mx(a, b, TN), tinv, dw)
        dqkr = each(lambda a, b: a * b, dqk, dec)
        dq1 = each(_mx, dqkr, kn)
        dk1 = each(lambda a, b: _mx(a, b, TN), dqkr, qn)
        dnu = each(lambda a, b: _mx(a, b, NT), dru, u)
        dnw = each(lambda a, b: _mx(a, b, NT), drw, w)
        dn = each(lambda a, b: jnp.where(strict, -(a + b), 0.0), dnu, dnw)
        dkk = each(lambda a, b, d: a * b * d, dn, beta, dec)
        dk2 = each(_mx, dkk, kn)
        dk3 = each(lambda a, b: _mx(a, b, TN), dkk, kn)
        dgates = jnp.zeros((CHUNK, LANES), f32)
        for h in heads:
            drw_k = rsum(drw[h] * kn[h])
            dbeta = rsum(dru[h] * v[h]) + e[h] * drw_k + rsum(dn[h] * kkd[h])
            m = dn[h] * (kkd[h] * beta[h]) + dqk[h] * qk[h]
            de = beta[h] * drw_k + rsum(dqd[h] * qn[h])
            df = rsum(dkd[h] * kn[h])
            dgam = rsum(m) - _row_to_col(jnp.sum(m, axis=0, keepdims=True), eye) + de * e[h] - df * f[h]
            dgam_last = jnp.sum(df * f[h], axis=0, keepdims=True) + dgl[h] * gl[h]
            dgam = dgam + jnp.where(row == CHUNK - 1, dgam_last, 0.0)
            dy_ref[:, sl[h]] = dq1[h] + dqd[h] * e[h]
            dy_ref[:, 512 + h * LANES:512 + (h + 1) * LANES] = (beta[h] * e[h]) * drw[h] + dk2[h] + dk3[h] + dk1[h] + dkd[h] * f[h]
            dy_ref[:, 1024 + h * LANES:1024 + (h + 1) * LANES] = beta[h] * dru[h]
            dgates = dgates + jnp.where(lane == h, dbeta, 0.0) + jnp.where(lane == 4 + h, dgam, 0.0)
        dg_ref[...] = dgates

    rev = lambda width: pl.BlockSpec((CHUNK, width), lambda n: (nch - 1 - n, 0))
    mat = lambda d: pl.BlockSpec((1, GDN_HEADS, d, d), lambda n: (nch - 1 - n, 0, 0, 0))
    return _hosted(
        body, comm, name="gdn_bwd", grid=(nch,),
        in_specs=[rev(GDN_QKV), rev(LANES), rev(512), rev(512), mat(CHUNK), mat(CHUNK), mat(LANES), rev(512)],
        out_specs=(rev(GDN_QKV), rev(LANES)),
        out_shape=(jax.ShapeDtypeStruct((t, GDN_QKV), f32), jax.ShapeDtypeStruct((t, LANES), f32)),
        scratch_shapes=[pltpu.VMEM((GDN_HEADS, LANES, LANES), f32)],
        args=(y, gcum, u_all, w_all, qk_all, tinv_all, sall, do))


FOX_CLASSES = 4


def _fox_groups(t):
    nq = t // FOX_BQ
    ncls = min(FOX_CLASSES, nq)
    per = nq // ncls
    return [(g * per, per, (g + 1) * per * FOX_BQ) for g in range(ncls)]


def _fox_causal(i, keys):
    rows = i * FOX_BQ + lax.broadcasted_iota(jnp.int32, (FOX_BQ, keys), 0)
    return lax.broadcasted_iota(jnp.int32, (FOX_BQ, keys), 1) <= rows


def _fox_scores(q_ref, k_ref, gcumt_ref, h, causal):
    pr = h // 2
    lo = (h % 2) * FOX_DH
    lane = _lane((FOX_BQ, LANES))
    mask = (lane >= lo) & (lane < lo + FOX_DH)
    qm = jnp.where(mask, q_ref[:, pr * LANES:(pr + 1) * LANES] * (FOX_DH ** -0.5), 0.0).astype(bf16)
    kp = k_ref[:, pr * LANES:(pr + 1) * LANES].astype(bf16)
    s = _dot_nt(qm, kp, None) - gcumt_ref[8 + h:9 + h, :]
    return jnp.where(causal, s, NEG), mask, qm, kp


def _fox_fwd(proj, gcum, gcumt, ride=None):
    c0 = SEG_FOX // 512

    def group_call(q0, nq, keys, comm):
        def body(q_ref, k_ref, v_ref, gcumt_ref, o_ref, lse_ref):
            causal = _fox_causal(q0 + pl.program_id(0), keys)
            lane = _lane((FOX_BQ, LANES))
            lse_all = jnp.zeros((FOX_BQ, LANES), f32)
            for pr in range(FOX_HEADS // 2):
                vp = v_ref[:, pr * LANES:(pr + 1) * LANES].astype(bf16)
                o_pair = jnp.zeros((FOX_BQ, LANES), f32)
                for h in (2 * pr, 2 * pr + 1):
                    s, mask, _, _ = _fox_scores(q_ref, k_ref, gcumt_ref, h, causal)
                    m = jnp.max(s, axis=1, keepdims=True)
                    p = jnp.exp(s - m)
                    l = jnp.sum(p, axis=1, keepdims=True)
                    o_h = _dot(p.astype(bf16), vp, None) * (1.0 / l)
                    o_pair = jnp.where(mask, o_h, o_pair)
                    lse_all = jnp.where(lane == h, m + jnp.log(l), lse_all)
                o_ref[:, pr * LANES:(pr + 1) * LANES] = o_pair
            lse_ref[...] = lse_all

        seen = lambda col: pl.BlockSpec((keys, 512), lambda i: (0, col))
        return _hosted(
            body, comm, name=f"fox_fwd_{keys}", grid=(nq,),
            in_specs=[pl.BlockSpec((FOX_BQ, 512), lambda i: (q0 + i, c0)), seen(c0 + 1), seen(c0 + 2),
                      pl.BlockSpec((LANES, keys), lambda i: (0, 0))],
            out_specs=(pl.BlockSpec((FOX_BQ, 512), lambda i: (i, 0)), pl.BlockSpec((FOX_BQ, LANES), lambda i: (i, 0))),
            out_shape=(jax.ShapeDtypeStruct((nq * FOX_BQ, 512), f32), jax.ShapeDtypeStruct((nq * FOX_BQ, LANES), f32)),
            args=(proj, proj, proj, gcumt))

    parts = []
    for n, g in enumerate(_fox_groups(proj.shape[0])):
        hook = ride(n) if ride else None
        part, moved = group_call(*g, hook[0] if hook else None)
        parts.append(part)
        if hook:
            hook[1](moved)
    return jnp.concatenate([o for o, _ in parts], axis=0), jnp.concatenate([l for _, l in parts], axis=0)


def _fox_bwd(proj, gcum, gcumt, o, lse, do, ride=None):
    t = proj.shape[0]
    c0 = SEG_FOX // 512

    def group_call(q0, nq, keys, acc, comm):
        first = acc is None

        def body(q_ref, k_ref, v_ref, gcumt_ref, o_ref, lse_ref, do_ref, *rest):
            dq_ref, dk_ref, dv_ref, dcc_ref, dct_ref = rest[-5:]
            j = pl.program_id(0)
            causal = _fox_causal(q0 + j, keys)

            @pl.when(j == 0)
            def _():
                if first:
                    dk_ref[...] = jnp.zeros_like(dk_ref)
                    dv_ref[...] = jnp.zeros_like(dv_ref)
                    dct_ref[...] = jnp.zeros_like(dct_ref)
                else:
                    dk_ref[...], dv_ref[...], dct_ref[...] = rest[0][...], rest[1][...], rest[2][...]

            lane = _lane((FOX_BQ, LANES))
            dcc = jnp.zeros((FOX_BQ, LANES), f32)
            scale = FOX_DH ** -0.5
            for pr in range(FOX_HEADS // 2):
                sl = slice(pr * LANES, (pr + 1) * LANES)
                vp = v_ref[:, sl].astype(bf16)
                dq_pair = jnp.zeros((FOX_BQ, LANES), f32)
                for h in (2 * pr, 2 * pr + 1):
                    s, mask, qm, kp = _fox_scores(q_ref, k_ref, gcumt_ref, h, causal)
                    p = jnp.exp(s - lse_ref[:, h:h + 1])
                    dom = jnp.where(mask, do_ref[:, sl], 0.0)
                    delta = jnp.sum(dom * o_ref[:, sl], axis=1, keepdims=True)
                    domb = dom.astype(bf16)
                    ds = p * (_dot_nt(domb, vp, None) - delta)
                    dsb = ds.astype(bf16)
                    dv_ref[:, sl] += _dot_tn(p.astype(bf16), domb, None)
                    dk_ref[:, sl] += _dot_tn(dsb, qm, None)
                    dq_pair = jnp.where(mask, _dot(dsb, kp, None) * scale, dq_pair)
                    dcc = jnp.where(lane == 8 + h, jnp.sum(ds, axis=1, keepdims=True), dcc)
                    dct_ref[8 + h:9 + h, :] += -jnp.sum(ds, axis=0, keepdims=True)
                dq_ref[:, sl] = dq_pair.astype(bf16)
            dcc_ref[...] = dcc

        qblk = lambda col: pl.BlockSpec((FOX_BQ, 512), lambda i: (q0 + i, col))
        oblk = pl.BlockSpec((FOX_BQ, 512), lambda i: (i, 0))
        seen = lambda col: pl.BlockSpec((keys, 512), lambda i: (0, col))
        rblk = pl.BlockSpec((FOX_BQ, LANES), lambda i: (q0 + i, 0))
        seen_t = pl.BlockSpec((LANES, keys), lambda i: (0, 0))
        in_specs = [qblk(c0), seen(c0 + 1), seen(c0 + 2), seen_t, qblk(0), rblk, qblk(0)]
        args = [proj, proj, proj, gcumt, o, lse, do]
        aliases = {}
        if not first:
            in_specs += [seen(0), seen(0), seen_t]
            args += list(acc)
            aliases = {7: 1, 8: 2, 9: 4}
        return _hosted(
            body, comm, name=f"fox_bwd_{keys}", grid=(nq,), in_specs=in_specs,
            out_specs=(oblk, seen(0), seen(0), pl.BlockSpec((FOX_BQ, LANES), lambda i: (i, 0)), seen_t),
            out_shape=(jax.ShapeDtypeStruct((nq * FOX_BQ, 512), bf16), jax.ShapeDtypeStruct((t, 512), f32), jax.ShapeDtypeStruct((t, 512), f32),
                       jax.ShapeDtypeStruct((nq * FOX_BQ, LANES), f32), jax.ShapeDtypeStruct((LANES, t), f32)),
            aliases=aliases, args=args)

    acc, dqs, dccs = None, [], []
    for n, g in enumerate(reversed(_fox_groups(t))):
        hook = ride(n) if ride else None
        (dq, dk, dv, dcc, dct), moved = group_call(*g, acc, hook[0] if hook else None)
        if hook:
            hook[1](moved)
        acc = (dk, dv, dct)
        dqs.insert(0, dq)
        dccs.insert(0, dcc)
    return jnp.concatenate(dqs, axis=0), acc[0], acc[1], jnp.concatenate(dccs, axis=0), acc[2]


def _row(v, width=None):
    v = v.reshape(1, -1).astype(f32)
    if width is not None and v.shape[1] < width:
        v = jnp.pad(v, ((0, 0), (0, width - v.shape[1])))
    return v


LATE = ("w_out", "w_up", "w_ple_gate", "w_ple", "w_down")


def _device_grads(x, p, target, small, w_cat, conv_w, late, qc=None, tail=None, ln_in_out=None):
    z4 = jnp.zeros((4,), f32)
    bias_row = _row(jnp.concatenate([z4, small["dt_bias"].reshape(-1), small["b_f"].reshape(-1)]), LANES)
    alog_row = _row(jnp.concatenate([z4, small["a_log"].reshape(-1)]), LANES)
    g_gdn = _row(small["gdn_norm_g"])
    g_fox2 = _row(jnp.tile(small["fox_norm_g"].reshape(-1), 2))
    pb = p.astype(bf16)
    late = list(late)
    comm = qc is not None

    h0, h0b = ln_in_out if ln_in_out is not None else _ln_in(x, _row(small["ln_in_g"]), _row(small["ln_in_b"]))[0]
    proj = _mm(h0b, w_cat, "nt", 512, D_CAT, "mm_proj")
    gates, gcum, gcumt = _gates(proj, bias_row, alog_row)
    w_down_pieces = [(4, 0, 1)]

    def gather(phase, pieces):
        if not comm or not pieces:
            return None, lambda moved: None
        touched = sorted({i for i, _, _ in pieces})

        def took(moved):
            for i, buf in zip(touched, moved):
                late[i] = buf
        return phase([late[i] for i in touched], [(touched.index(i), k, n) for i, k, n in pieces]), took

    over, on = _gather_chips, _gather_pass_on
    cm, took = gather(over, [(0, 0, 1), (3, 0, 1)])
    (conv_c, qkv_n), moved = _gdn_conv(proj, conv_w, cm)
    took(moved)
    cm, took = gather(over, [(1, 0, 2)])
    (gu, gw, gqk, gtinv), moved = _gdn_local(qkv_n, gcum, cm)
    took(moved)
    cm, took = gather(over, [(1, 1, 2)])
    (o_gdn, sall), moved = _gdn_fwd(qkv_n, gcum, gu, gw, gqk, cm)
    took(moved)
    fox_plan = [(over, []), (on, [(0, 0, 1), (3, 0, 1), (1, 0, 2), (1, 1, 2)]), (over, [(2, 0, 1)]), (over, [(4, 0, 2)])]
    assert not comm or len(_fox_groups(x.shape[0])) == len(fox_plan)
    o_fox, lse = _fox_fwd(proj, gcum, gcumt, (lambda n: gather(*fox_plan[n])) if comm else None)
    cm, took = gather(on, [(2, 0, 1)])
    (attn,), moved = _attn_post(o_gdn, proj, o_fox, g_gdn, g_fox2, cm)
    took(moved)
    w_out = late[0].reshape(D_MODEL, D_MODEL)
    (h1, h1b, xhat1, rstd1), _ = _ln1(h0, attn, w_out, _row(small["ln1_g"]), _row(small["ln1_b"]))
    w_up, w_ple = late[1], late[3]
    cm, took = gather(over, [(4, 1, 2)])
    up_act = _mm(h1b, w_up, "nn", 512, 1024, "mm_up", epi="relu2", shards=N_CHIPS, comm=cm)
    if cm:
        up_act, moved = up_act
        took(moved)
    up, act = up_act
    w_gate = late[2].reshape(D_MODEL, D_MODEL)
    cm, took = gather(on, w_down_pieces)
    gp = _mm(h1b, w_gate, "nn", 512, D_MODEL, "mm_gate", comm=cm)
    if cm:
        gp, moved = gp
        took(moved)
    w_down = late[4].reshape(D_FF, D_MODEL)
    dr2, dr2b, dpe, dgp, pg2 = _ln2_loss(h1, act, w_down, pb, w_ple, gp, _row(small["b_ple_gate"]), _row(small["ln2_g"]),
                                         _row(small["ln2_b"]), target)

    dup = _mm(dr2b, w_down, "nt", 512, 2048, "mm_dact", epi="relu2_bwd", extra=up)
    g_down = _mm(act, dr2b, "tn", 1024, D_MODEL, "mm_gdown")
    g_up = _mm(h1b, dup, "tn", 1024, 1024, "mm_gup", shards=N_CHIPS)
    g_gate = _mm(h1b, dgp, "tn", 1024, D_MODEL, "mm_ggate")
    g_ple = _mm(pb, dpe, "tn", D_PLE, D_MODEL // N_CHIPS, "mm_gple", shards=N_CHIPS)
    dr1, dr1b, pg1 = _ln1_bwd(dr2, dup, w_up, dgp, w_gate, xhat1, rstd1, _row(small["ln1_g"]))
    g_out = _mm(attn, dr1b, "tn", 1024, D_MODEL, "mm_gout")
    do_gdn, dz, do_fox, pga = _attn_post_bwd(dr1b, w_out, o_gdn, proj, o_fox, g_gdn, g_fox2)
    g_late = [g.reshape((N_CHIPS, -1, g.shape[-1])) for g in (g_out, g_up, g_gate, g_ple, g_down)]
    chip_plan = [[(4, 0, 2), (4, 1, 2), (0, 0, 1)], [(1, 0, 2)], [(1, 1, 2)], [(2, 0, 1), (3, 0, 1)]]
    state = {}

    def to_sibling():
        def took(moved):
            sent = [_add_pair(g, b1, qc, "add_pair_" + n) for g, b1, n in zip(g_late, moved, LATE)]
            state.update(from_sibling=list(moved), sent=sent, landing=_landing(sent))
        return _exchange_pairs(g_late), took

    def to_chips(pieces):
        if not comm:
            return None, lambda moved: None
        return _exchange_chips(state["sent"], state["landing"], pieces), lambda moved: state.update(landing=list(moved))

    def gdn_backward():
        cm, took = to_chips(chip_plan[0])
        state["gdn"], moved = _gdn_bwd(qkv_n, gcum, gu, gw, gqk, gtinv, sall, do_gdn, cm)
        took(moved)

    def fox_ride(n):
        if n == 0:
            return to_sibling()
        if n == 1:
            gdn_backward()
        return to_chips(chip_plan[n])

    assert not comm or len(_fox_groups(x.shape[0])) == len(chip_plan)
    dfq, dfk, dfv, dccol, dct = _fox_bwd(proj, gcum, gcumt, o_fox, lse, do_fox, fox_ride if comm else None)
    if not comm:
        gdn_backward()
    dqkv_n, dgates = state["gdn"]
    dsmall, pgg = _gates_bwd(proj, bias_row, alog_row, gates, dgates, dccol, dct)
    cm = None
    if comm:
        cm = _share_halves([_add_chips(g, b1, b2, qc, "add_chips_" + n)
                            for g, b1, b2, n in zip(g_late, state["from_sibling"], state["landing"], LATE)])
    (du, g_conv8), reduced = _gdn_conv_bwd(proj, conv_w, conv_c, dqkv_n, cm)
    if comm:
        g_late = list(reduced)
    t = x.shape[0]
    dproj = jnp.concatenate([du, dz, dfq, dfk.astype(bf16), dfv.astype(bf16), dsmall, jnp.zeros((t, D_CAT - SEG_SMALL - LANES), bf16)], axis=1)
    g_cat = _mm(dproj, h0b, "tn", 1280, D_MODEL, "mm_gcat")
    cm, took = tail[0](g_cat) if tail else (None, None)
    dh0_mm = _mm(dproj, w_cat, "nn", 512, D_MODEL, "mm_dh0", comm=cm)
    if cm:
        dh0_mm, moved = dh0_mm
        took(moved)
    cm, took = tail[1]() if tail and tail[1] else (None, None)
    (grad_x, pg0), moved = _ln_in_bwd(x, dr1, dh0_mm, _row(small["ln_in_g"]), cm)
    if cm:
        took(moved)

    g_fox = pga[1, :FOX_DH] + pga[1, FOX_DH:]
    small_grads = dict(
        ln_in_g=pg0[0], ln_in_b=pg0[1], ln1_g=pg1[0], ln1_b=pg1[1], b_ple_gate=pg2[2], ln2_g=pg2[0], ln2_b=pg2[1],
        gdn_norm_g=pga[0], fox_norm_g=g_fox, a_log=pgg[1, 4:8], dt_bias=pgg[0, 4:8], b_f=pgg[0, 8:16], loss=pg2[3, 0:1])
    return grad_x, g_cat, g_conv8[:CONV_W], dict(zip(LATE, g_late)), small_grads


ANY = pl.BlockSpec(memory_space=pl.ANY)
CONV_PKT_ROWS = 16


def _mesh_pos():
    return lax.axis_index("x"), lax.axis_index("y"), lax.axis_index("c")


def _other_chips(x, y):
    return [(1 - x, y), (x, 1 - y), (1 - x, 1 - y)]


def _rcopy(src, dst, send_sem, recv_sem, dev):
    return pltpu.make_async_remote_copy(src_ref=src, dst_ref=dst, send_sem=send_sem, recv_sem=recv_sem,
                                        device_id=dev, device_id_type=MESH)


class _Comm:
    def __init__(self, ins, outs, aliases, n_sems, start, finish):
        self.ins, self.outs, self.aliases, self.n_sems, self.start, self.finish = list(ins), list(outs), dict(aliases), n_sems, start, finish


def _hosted(body, comm, *, name, grid, in_specs, out_specs, out_shape, args, scratch_shapes=(), aliases=None):
    n_in, n_out, n_sc = len(in_specs), len(out_specs), len(scratch_shapes)
    k, ko = (len(comm.ins), len(comm.outs)) if comm else (0, 0)

    def kernel_body(*refs):
        o0 = n_in + k
        s0 = o0 + n_out + ko
        if comm:
            cins, couts, (ssem, rsem) = refs[n_in:o0], refs[o0 + n_out:s0], refs[s0 + n_sc:]
            step = pl.program_id(0)
            for d in range(1, len(grid)):
                step = step * grid[d] + pl.program_id(d)

            @pl.when(step == 0)
            def _():
                comm.start(cins, couts, ssem, rsem)

        body(*refs[:n_in], *refs[o0:o0 + n_out], *refs[s0:s0 + n_sc])
        if comm:
            last = 1
            for n in grid:
                last *= n

            @pl.when(step == last - 1)
            def _():
                comm.finish(cins, couts, ssem, rsem)

    io_aliases = dict(aliases or {})
    scratch = list(scratch_shapes)
    if comm:
        io_aliases.update({n_in + i: n_out + j for i, j in comm.aliases.items()})
        scratch += [pltpu.SemaphoreType.DMA((comm.n_sems,)), pltpu.SemaphoreType.DMA((comm.n_sems,))]
    res = pl.pallas_call(
        kernel_body, name=name, grid=grid, in_specs=list(in_specs) + [ANY] * k, out_specs=tuple(out_specs) + (ANY,) * ko,
        out_shape=tuple(out_shape) + tuple(comm.outs if comm else ()), scratch_shapes=scratch, input_output_aliases=io_aliases,
        compiler_params=_params(("arbitrary",) * len(grid)),
    )(*args, *(comm.ins if comm else ()))
    return tuple(res[:n_out]), tuple(res[n_out:])


def _comm_only(phases, name):
    n_in = sum(len(p.ins) for p in phases)

    def body(*refs):
        n_out = sum(len(p.outs) for p in phases)
        sems = refs[n_in + n_out:]
        i0, o0 = 0, n_in
        for j, p in enumerate(phases):
            cins, couts = refs[i0:i0 + len(p.ins)], refs[o0:o0 + len(p.outs)]
            p.start(cins, couts, sems[2 * j], sems[2 * j + 1])
            p.finish(cins, couts, sems[2 * j], sems[2 * j + 1])
            i0 += len(p.ins)
            o0 += len(p.outs)

    aliases, i0, o0 = {}, 0, 0
    for p in phases:
        aliases.update({i0 + i: o0 + j for i, j in p.aliases.items()})
        i0 += len(p.ins)
        o0 += len(p.outs)
    outs = [o for p in phases for o in p.outs]
    res = pl.pallas_call(
        body, name=name, out_shape=tuple(outs), in_specs=[ANY] * n_in, out_specs=(ANY,) * len(outs), input_output_aliases=aliases,
        scratch_shapes=[pltpu.SemaphoreType.DMA((p.n_sems,)) for p in phases for _ in range(2)],
    )(*[a for p in phases for a in p.ins])
    split, o0 = [], 0
    for p in phases:
        split.append(tuple(res[o0:o0 + len(p.outs)]))
        o0 += len(p.outs)
    return split


def _like(arrays):
    return [jax.ShapeDtypeStruct(a.shape, a.dtype) for a in arrays]


def _half(ref, slot, hf, piece=(0, 1)):
    k, n = piece
    rows = ref.shape[1] // 2 // n
    return ref.at[slot, pl.ds((hf * n + k) * rows, rows)]


def _whole_halves(arrays):
    return [(i, 0, 1) for i in range(len(arrays))]


def _gather_chips(bufs, pieces=None, whole=False, base=0):
    nw = len(bufs)
    pieces = _whole_halves(bufs) if pieces is None else pieces
    part = (lambda ref, slot, c, piece: ref.at[slot]) if whole else _half

    def copies(couts):
        x, y, c = _mesh_pos()
        q = 2 * x + y
        for j, (i, k, n) in enumerate(pieces):
            for kc, chip in enumerate(_other_chips(x, y)):
                mine, theirs = part(couts[i], q, c, (k, n)), part(couts[i], 2 * chip[0] + chip[1], c, (k, n))
                yield base + j * 3 + kc, mine, theirs, (*chip, c)

    def start(cins, couts, ssem, rsem):
        for s, mine, _, dev in copies(couts):
            _rcopy(mine, mine, ssem.at[s], rsem.at[s], dev).start()

    def finish(cins, couts, ssem, rsem):
        for s, _, theirs, dev in copies(couts):
            _rcopy(theirs, theirs, ssem.at[s], rsem.at[s], dev).wait_recv()
        for s, mine, _, dev in copies(couts):
            _rcopy(mine, mine, ssem.at[s], rsem.at[s], dev).wait_send()

    return _Comm(bufs, _like(bufs), {i: i for i in range(nw)}, 3 * len(pieces), start, finish)


def _gather_pass_on(bufs, pieces=None, base=0):
    nw = len(bufs)
    pieces = _whole_halves(bufs) if pieces is None else pieces

    def copies(couts):
        x, y, c = _mesh_pos()
        for j, (i, k, n) in enumerate(pieces):
            for kc, chip in enumerate(_other_chips(x, y)):
                slot = 2 * chip[0] + chip[1]
                yield base + j * 3 + kc, _half(couts[i], slot, c, (k, n)), _half(couts[i], slot, 1 - c, (k, n)), (x, y, 1 - c)

    def start(cins, couts, ssem, rsem):
        for s, landed, _, sib in copies(couts):
            _rcopy(landed, landed, ssem.at[s], rsem.at[s], sib).start()

    def finish(cins, couts, ssem, rsem):
        for s, _, passed, sib in copies(couts):
            _rcopy(passed, passed, ssem.at[s], rsem.at[s], sib).wait_recv()
        for s, landed, _, sib in copies(couts):
            _rcopy(landed, landed, ssem.at[s], rsem.at[s], sib).wait_send()

    return _Comm(bufs, _like(bufs), {i: i for i in range(nw)}, 3 * len(pieces), start, finish)


def _gather_now(bufs, packets):
    nb = len(bufs)
    over, on, pk = _gather_chips(bufs), _gather_pass_on(bufs, base=3 * nb), _gather_chips(packets, whole=True, base=6 * nb)

    def start(cins, couts, ssem, rsem):
        over.start(cins[:nb], couts[:nb], ssem, rsem)
        pk.start(cins[nb:], couts[nb:], ssem, rsem)

    def finish(cins, couts, ssem, rsem):
        over.finish(cins[:nb], couts[:nb], ssem, rsem)
        on.start(cins[:nb], couts[:nb], ssem, rsem)
        on.finish(cins[:nb], couts[:nb], ssem, rsem)
        pk.finish(cins[nb:], couts[nb:], ssem, rsem)

    every = list(bufs) + list(packets)
    return _Comm(every, _like(every), {i: i for i in range(len(every))}, 6 * nb + 3 * len(packets), start, finish)


def _exchange_pairs(gs):
    nw = len(gs)

    def copies(cins, couts):
        x, y, c = _mesh_pos()
        for i in range(nw):
            for d in range(N_CHIPS):
                yield i * N_CHIPS + d, _half(cins[i], d, 1 - c), couts[i].at[d], (x, y, 1 - c)

    def start(cins, couts, ssem, rsem):
        for s, src, dst, sib in copies(cins, couts):
            _rcopy(src, dst, ssem.at[s], rsem.at[s], sib).start()

    def finish(cins, couts, ssem, rsem):
        for s, src, dst, sib in copies(cins, couts):
            _rcopy(src, dst, ssem.at[s], rsem.at[s], sib).wait_recv()
        for s, src, dst, sib in copies(cins, couts):
            _rcopy(src, dst, ssem.at[s], rsem.at[s], sib).wait_send()

    outs = [jax.ShapeDtypeStruct((N_CHIPS, g.shape[1] // 2, g.shape[2]), g.dtype) for g in gs]
    return _Comm(gs, outs, {}, N_CHIPS * nw, start, finish)


def _gather_packets(small):
    def peers():
        x, y, c = _mesh_pos()
        for r in range(1, 8):
            fx, fy, fc = (r >> 2) & 1, (r >> 1) & 1, r & 1
            yield r - 1, (1 - x if fx else x, 1 - y if fy else y, 1 - c if fc else c)

    def start(cins, couts, ssem, rsem):
        x, y, c = _mesh_pos()
        mine = couts[0].at[4 * x + 2 * y + c]
        for s, peer in peers():
            _rcopy(mine, mine, ssem.at[s], rsem.at[s], peer).start()

    def finish(cins, couts, ssem, rsem):
        x, y, c = _mesh_pos()
        mine = couts[0].at[4 * x + 2 * y + c]
        for s, peer in peers():
            theirs = couts[0].at[4 * peer[0] + 2 * peer[1] + peer[2]]
            _rcopy(theirs, theirs, ssem.at[s], rsem.at[s], peer).wait_recv()
        for s, peer in peers():
            _rcopy(mine, mine, ssem.at[s], rsem.at[s], peer).wait_send()

    return _Comm([small], _like([small]), {0: 0}, 7, start, finish)


def _exchange_chips(a4s, b2s, pieces=None):
    nw = len(a4s)
    pieces = _whole_halves(a4s) if pieces is None else pieces

    def copies(cins, couts):
        x, y, c = _mesh_pos()
        for j, (i, k, n) in enumerate(pieces):
            rows = a4s[i].shape[1] // n
            part = pl.ds(k * rows, rows)
            for kc, chip in enumerate(_other_chips(x, y)):
                yield j * 3 + kc, cins[i].at[2 * chip[0] + chip[1], part], couts[i].at[kc, part], (*chip, c)

    def start(cins, couts, ssem, rsem):
        for s, src, dst, dev in copies(cins, couts):
            _rcopy(src, dst, ssem.at[s], rsem.at[s], dev).start()

    def finish(cins, couts, ssem, rsem):
        for s, src, dst, dev in copies(cins, couts):
            _rcopy(src, dst, ssem.at[s], rsem.at[s], dev).wait_recv()
        for s, src, dst, dev in copies(cins, couts):
            _rcopy(src, dst, ssem.at[s], rsem.at[s], dev).wait_send()

    return _Comm(list(a4s) + list(b2s), _like(b2s), {nw + i: i for i in range(nw)}, 3 * len(pieces), start, finish)


def _landing(a4s):
    return [lax.empty((3,) + a.shape[1:], a.dtype) for a in a4s]


def _share_halves(rs):
    nw = len(rs)

    def halves(couts, i, hf):
        rows = rs[i].shape[0] // 2
        return couts[i].at[pl.ds(hf * rows, rows)]

    def start(cins, couts, ssem, rsem):
        x, y, c = _mesh_pos()
        for i in range(nw):
            _rcopy(halves(couts, i, c), halves(couts, i, c), ssem.at[i], rsem.at[i], (x, y, 1 - c)).start()

    def finish(cins, couts, ssem, rsem):
        x, y, c = _mesh_pos()
        for i in range(nw):
            _rcopy(halves(couts, i, 1 - c), halves(couts, i, 1 - c), ssem.at[i], rsem.at[i], (x, y, 1 - c)).wait_recv()
        for i in range(nw):
            _rcopy(halves(couts, i, c), halves(couts, i, c), ssem.at[i], rsem.at[i], (x, y, 1 - c)).wait_send()

    return _Comm(rs, _like(rs), {i: i for i in range(nw)}, nw, start, finish)


ADD_ROWS = 256


def _add_pair(g4, b1, qc_idx, name):
    _, half, cols = b1.shape
    rb = ADD_ROWS if half % ADD_ROWS == 0 else half
    nb = half // rb

    def body(qc_ref, g_ref, b_ref, ob_ref):
        ob_ref[...] = (g_ref[...] + b_ref[...]).astype(bf16)

    blk = (1, rb, cols)
    out = pl.BlockSpec(blk, lambda d, i, qc: (d, i, 0))
    return pl.pallas_call(
        body, name=name,
        grid_spec=pltpu.PrefetchScalarGridSpec(
            num_scalar_prefetch=1, grid=(N_CHIPS, nb),
            in_specs=[pl.BlockSpec(blk, lambda d, i, qc: (d, qc[1] * nb + i, 0)), out],
            out_specs=out),
        out_shape=jax.ShapeDtypeStruct(b1.shape, bf16),
        compiler_params=_params(("parallel", "parallel")),
    )(qc_idx, g4, b1)


def _add_chips(g4, b1, b2, qc_idx, name):
    _, half, cols = b1.shape
    rb = ADD_ROWS if half % ADD_ROWS == 0 else half
    nb = half // rb

    def body(qc_ref, g_ref, s_ref, b_ref, o_ref):
        o_ref[...] = (((g_ref[0] + s_ref[0]) + b_ref[0].astype(f32)) + b_ref[1].astype(f32)) + b_ref[2].astype(f32)

    return pl.pallas_call(
        body, name=name,
        grid_spec=pltpu.PrefetchScalarGridSpec(
            num_scalar_prefetch=1, grid=(nb,),
            in_specs=[pl.BlockSpec((1, rb, cols), lambda i, qc: (qc[0], qc[1] * nb + i, 0)),
                      pl.BlockSpec((1, rb, cols), lambda i, qc: (qc[0], i, 0)), pl.BlockSpec((3, rb, cols), lambda i, qc: (0, i, 0))],
            out_specs=pl.BlockSpec((rb, cols), lambda i, qc: (qc[1] * nb + i, 0))),
        out_shape=jax.ShapeDtypeStruct((2 * half, cols), f32),
        compiler_params=_params(("parallel",)),
    )(qc_idx, g4, b1, b2)


def _adamw_math(w, g, m, v):
    m = ADAM_B1 * m + (1.0 - ADAM_B1) * g
    v = ADAM_B2 * v + (1.0 - ADAM_B2) * (g * g)
    m_hat = m / (1.0 - ADAM_B1 ** ADAM_STEP)
    v_hat = v / (1.0 - ADAM_B2 ** ADAM_STEP)
    return -ADAM_LR * (m_hat / (jnp.sqrt(v_hat) + ADAM_EPS) + ADAM_WD * w), m, v


def _adamw(w, g, m, v, name, comm=None):
    rows = w.shape[0]
    if w.ndim == 3:
        rb = max(r for r in range(1, ADD_ROWS // 4 + 1) if rows % r == 0)
    else:
        rb = ADD_ROWS if rows % ADD_ROWS == 0 else rows

    def body(w_ref, g_ref, m_ref, v_ref, go_ref, d_ref, mo_ref, vo_ref):
        g = g_ref[...]
        go_ref[...] = g
        d_ref[...], mo_ref[...], vo_ref[...] = _adamw_math(w_ref[...], g, m_ref[...], v_ref[...])

    blk = pl.BlockSpec((rb,) + w.shape[1:], lambda i: (i,) + (0,) * (w.ndim - 1))
    return _hosted(body, comm, name=name, grid=(rows // rb,), in_specs=[blk] * 4, out_specs=(blk,) * 4,
                   out_shape=(jax.ShapeDtypeStruct(w.shape, f32),) * 4, args=(w, g, m, v))


def _small_sum_adamw(all_pkts, w, m, v):
    names = [n for n, _, _ in SMALL_LAYOUT if n in w]
    place = {n: (r0, size) for n, r0, size in SMALL_LAYOUT}
    rows_of = lambda size: -(-size // LANES)
    flat = lambda a: a.reshape(1, -1)
    k = len(names)

    def body(*refs):
        a_ref, ins = refs[0], refs[1:1 + 3 * k]
        g_ref, outs = refs[1 + 3 * k], refs[2 + 3 * k:2 + 7 * k]
        packs = refs[2 + 7 * k:]
        g = a_ref[0]
        for r in range(1, 8):
            g = g + a_ref[r]
        g_ref[...] = g
        for kind in range(3):
            packs[kind][...] = jnp.zeros_like(packs[kind])
            for j, n in enumerate(names):
                r0, size = place[n]
                for r in range(rows_of(size)):
                    width = min(LANES, size - r * LANES)
                    packs[kind][r0 + r:r0 + r + 1, 0:width] = ins[kind * k + j][:, r * LANES:r * LANES + width]
        results = (g,) + _adamw_math(packs[0][...], g, packs[1][...], packs[2][...])
        for kind, val in enumerate(results):
            for j, n in enumerate(names):
                r0, size = place[n]
                for r in range(rows_of(size)):
                    width = min(LANES, size - r * LANES)
                    outs[kind * k + j][:, r * LANES:r * LANES + width] = val[r0 + r:r0 + r + 1, 0:width]

    args = [all_pkts] + [flat(d[n]) for d in (w, m, v) for n in names]
    out_shape = [jax.ShapeDtypeStruct(all_pkts.shape[1:], f32)] + [jax.ShapeDtypeStruct((1, place[n][1]), f32) for _ in range(4) for n in names]
    res = pl.pallas_call(body, name="small_sum_adamw", out_shape=tuple(out_shape),
                         scratch_shapes=[pltpu.VMEM(all_pkts.shape[1:], f32)] * 3)(*args)
    by_kind = [{n: res[1 + kind * k + j].reshape(w[n].shape) for j, n in enumerate(names)} for kind in range(4)]
    return res[0], by_kind


SMALL_LAYOUT = (("ln_in_g", 0, 1024), ("ln_in_b", 8, 1024), ("ln1_g", 16, 1024), ("ln1_b", 24, 1024), ("b_ple_gate", 32, 1024),
                ("ln2_g", 40, 1024), ("ln2_b", 48, 1024), ("gdn_norm_g", 56, 128), ("fox_norm_g", 57, 64), ("a_log", 58, 4),
                ("dt_bias", 59, 4), ("b_f", 60, 8), ("loss", 61, 1))
SMALL_CONV_ROW = 64
SMALL_ROWS = 128


def _pack_small(vals, conv=None):
    rows = []
    nxt = 0
    for n, r0, size in SMALL_LAYOUT:
        assert r0 == nxt
        v = vals[n].reshape(-1).astype(f32) if n in vals else jnp.zeros((size,), f32)
        nrows = -(-size // LANES)
        rows.append(jnp.pad(v, (0, nrows * LANES - size)).reshape(nrows, LANES))
        nxt = r0 + nrows
    rows.append(jnp.zeros((SMALL_CONV_ROW - nxt, LANES), f32))
    conv_rows = CONV_W * GDN_QKV // LANES
    rows.append(jnp.zeros((conv_rows, LANES), f32) if conv is None else conv.reshape(conv_rows, LANES))
    rows.append(jnp.zeros((SMALL_ROWS - SMALL_CONV_ROW - conv_rows, LANES), f32))
    return jnp.concatenate(rows, axis=0)


def _unpack_small(pkt, shapes):
    out = {}
    for n, r0, size in SMALL_LAYOUT:
        if n in shapes:
            nrows = -(-size // LANES)
            out[n] = pkt[r0:r0 + nrows].reshape(-1)[:size].reshape(shapes[n])
    return out


WEIGHTS = ("ln_in_g", "ln_in_b", "w_in", "conv_w", "a_log", "dt_bias", "gdn_norm_g", "b_f", "fox_norm_g", "w_out", "ln1_g", "ln1_b",
           "w_up", "w_down", "w_ple", "w_ple_gate", "b_ple_gate", "ln2_g", "ln2_b")
SMALL_NAMES = tuple(n for n, _, _ in SMALL_LAYOUT if n != "loss")


def kernel(x, p, ln_in_g, ln_in_b, w_in, conv_w, a_log, dt_bias, gdn_norm_g, b_f, fox_norm_g, w_out, ln1_g, ln1_b, w_up, w_down, w_ple, w_ple_gate, b_ple_gate, ln2_g, ln2_b, loss_target, m_ln_in_g, m_ln_in_b, m_w_in, m_conv_w, m_a_log, m_dt_bias, m_gdn_norm_g, m_b_f, m_fox_norm_g, m_w_out, m_ln1_g, m_ln1_b, m_w_up, m_w_down, m_w_ple, m_w_ple_gate, m_b_ple_gate, m_ln2_g, m_ln2_b, v_ln_in_g, v_ln_in_b, v_w_in, v_conv_w, v_a_log, v_dt_bias, v_gdn_norm_g, v_b_f, v_fox_norm_g, v_w_out, v_ln1_g, v_ln1_b, v_w_up, v_w_down, v_w_ple, v_w_ple_gate, v_b_ple_gate, v_ln2_g, v_ln2_b):
    given = dict(locals())
    w = {n: given[n] for n in WEIGHTS}
    m = {n: given["m_" + n] for n in WEIGHTS}
    v = {n: given["v_" + n] for n in WEIGHTS}
    xi, yi, ci = _mesh_pos()
    q = 2 * xi + yi

    def slot_buffer(val, dtype, slots=N_CHIPS, slot=q, rows=None):
        rows = val.shape[0] if rows is None else rows
        return lax.dynamic_update_slice(lax.empty((slots, rows) + val.shape[1:], dtype), val.astype(dtype)[None], (slot, 0, 0))

    shard_cols = D_IN // N_CHIPS
    conv_rows = CONV_W * GDN_QKV // N_CHIPS // LANES
    conv_pkt = jnp.pad(w["conv_w"][0].reshape(-1, LANES), ((0, CONV_PKT_ROWS - conv_rows), (0, 0)))
    ln_in_out, (w_in4, conv_all) = _ln_in(x[0], _row(w["ln_in_g"]), _row(w["ln_in_b"]),
                                          _gather_now([slot_buffer(w["w_in"][0].T, bf16, rows=W_IN_ROWS)], [slot_buffer(conv_pkt, f32)]))
    conv_full = jnp.concatenate([conv_all[d, :conv_rows].reshape(CONV_W, GDN_QKV // N_CHIPS) for d in range(N_CHIPS)], axis=1)
    wi = jnp.concatenate([w_in4[d, :shard_cols] for d in range(N_CHIPS)], axis=0)
    w_cat = jnp.concatenate([wi[:OFF_BETA], wi[OFF_FOX:OFF_F], wi[OFF_BETA:OFF_FOX], wi[OFF_F:],
                             jnp.zeros((D_CAT - D_IN, D_MODEL), bf16)], axis=0)

    small = {n: w[n] for n in SMALL_NAMES}
    qc = jnp.stack([q, ci]).astype(jnp.int32)
    tail_state = {}

    def pairs_phase(gc):
        g_in = jnp.concatenate([gc[:OFF_BETA], gc[SEG_SMALL:SEG_SMALL + 8], gc[SEG_FOX:SEG_SMALL], gc[SEG_SMALL + 8:SEG_SMALL + 16]], axis=0)
        g_in4 = jnp.stack([jnp.pad(g_in[d * shard_cols:(d + 1) * shard_cols], ((0, W_IN_ROWS - shard_cols), (0, 0))) for d in range(N_CHIPS)])

        def took(moved):
            sent = _add_pair(g_in4, moved[0], qc, "add_pair_w_in")
            tail_state.update(g=g_in4, from_sibling=moved[0], sent=[sent], landing=_landing([sent]))
        return _exchange_pairs([g_in4]), took

    grad_x, _, g_conv, g_late, small_g = _device_grads(
        x[0], p[0, 0], loss_target[0], small, w_cat, conv_full, [slot_buffer(w[n][0], bf16) for n in LATE], qc, tail=(pairs_phase, None),
        ln_in_out=ln_in_out)
    packets = _gather_packets(slot_buffer(_pack_small(small_g, g_conv), f32, 8, 4 * xi + 2 * yi + ci))
    (b2,), (small_all,) = _comm_only([_exchange_chips(tail_state["sent"], tail_state["landing"]), packets], "exchange_chips_w_in")
    (g_late["w_in"],), = _comm_only(
        [_share_halves([_add_chips(tail_state["g"], tail_state["from_sibling"], b2, qc, "add_chips_w_in")])], "share_w_in")

    grads, delta, new_m, new_v = {}, {}, {}, {}
    for n, g in g_late.items():
        if n == "w_in":
            as_stored = lambda a: jnp.transpose(a, (2, 0, 1))
            outs, _ = _adamw(as_stored(w[n]), g[:shard_cols].reshape(shard_cols, 1, D_MODEL), as_stored(m[n]), as_stored(v[n]), "adamw_" + n)
            grads[n], delta[n], new_m[n], new_v[n] = (jnp.transpose(a, (1, 2, 0)) for a in outs)
        else:
            outs, _ = _adamw(w[n][0], g, m[n][0], v[n][0], "adamw_" + n)
            grads[n], delta[n], new_m[n], new_v[n] = (a.reshape(w[n].shape) for a in outs)
    pick = lambda d: {n: d[n] for n in SMALL_NAMES}
    g_pkt, by_kind = _small_sum_adamw(small_all, pick(w), pick(m), pick(v))
    for dst, vals in zip((grads, delta, new_m, new_v), by_kind):
        dst.update(vals)
    conv_rows_all = CONV_W * GDN_QKV // LANES
    conv_g_full = g_pkt[SMALL_CONV_ROW:SMALL_CONV_ROW + conv_rows_all].reshape(CONV_W, GDN_QKV)
    conv_g = lax.dynamic_slice_in_dim(conv_g_full, q * (GDN_QKV // N_CHIPS), GDN_QKV // N_CHIPS, axis=1)
    outs, _ = _adamw(w["conv_w"][0], conv_g, m["conv_w"][0], v["conv_w"][0], "adamw_conv_w")
    grads["conv_w"], delta["conv_w"], new_m["conv_w"], new_v["conv_w"] = (a.reshape(w["conv_w"].shape) for a in outs)
    loss = g_pkt[61, 0]
    return (loss, grad_x[None], *[grads[n] for n in WEIGHTS], *[delta[n] for n in WEIGHTS],
            *[new_m[n] for n in WEIGHTS], *[new_v[n] for n in WEIGHTS])
```

```python
import functools

import jax
import jax.numpy as jnp
from jax import lax
from jax.experimental import pallas as pl
from jax.experimental.pallas import tpu as pltpu

f32 = jnp.float32
bf16 = jnp.bfloat16
HI = lax.Precision.HIGHEST
MESH = pl.DeviceIdType.MESH

D_MODEL = 1024
CHUNK = 64
GDN_HEADS = 4
GDN_DK = 128
FOX_HEADS = 8
FOX_DH = 64
CONV_W = 4
D_FF = 4096
D_PLE = 256
LN_EPS = 1e-5
NORM_EPS = 1e-6
ALPHA = 2.0 ** 0.25
GDN_QKV = 1536
OFF_Z = 1536
OFF_BETA = 2048
OFF_FOX = 2056
OFF_F = 3592
D_IN = 3600
ADAM_LR = 0.001
ADAM_B1 = 0.9
ADAM_B2 = 0.999
ADAM_EPS = 1e-08
ADAM_WD = 0.01
ADAM_STEP = 10

SEG_FOX = 2048
SEG_SMALL = 3584
D_CAT = 3840
LANES = 128
TOK_BLK = 256
FOX_BQ = 256
VMEM_LIMIT = 56 * 1024 * 1024
NEG = -1e30

N_CHIPS = 4
W_IN_ROWS = 928


def _params(sem=None, **kw):
    return pltpu.CompilerParams(dimension_semantics=sem, vmem_limit_bytes=VMEM_LIMIT, **kw)


def _sigmoid(x):
    return 1.0 / (1.0 + jnp.exp(-x))


def _softplus(x):
    return jnp.maximum(x, 0.0) + jnp.log(1.0 + jnp.exp(-jnp.abs(x)))


def _ln_fwd(x, g, b):
    mu = jnp.mean(x, -1, keepdims=True)
    xc = x - mu
    var = jnp.mean(xc * xc, -1, keepdims=True)
    rstd = lax.rsqrt(var + LN_EPS)
    xhat = xc * rstd
    return xhat * g + b, xhat, rstd


def _ln_bwd(dy, xhat, rstd, g):
    dxh = dy * g
    m1 = jnp.mean(dxh, -1, keepdims=True)
    m2 = jnp.mean(dxh * xhat, -1, keepdims=True)
    return rstd * (dxh - m1 - xhat * m2)


def _dot(a, b, prec=HI):
    return jnp.dot(a, b, precision=prec, preferred_element_type=f32)


def _dot_nt(a, b, prec=HI):
    return lax.dot_general(a, b, (((1,), (1,)), ((), ())), precision=prec, preferred_element_type=f32)


def _dot_tn(a, b, prec=HI):
    return lax.dot_general(a, b, (((0,), (0,)), ((), ())), precision=prec, preferred_element_type=f32)


def _bdot(a, b):
    return _dot(a.astype(bf16), b.astype(bf16), None)


def _bdot_nt(a, b):
    return _dot_nt(a.astype(bf16), b.astype(bf16), None)


def _bdot_tn(a, b):
    return _dot_tn(a.astype(bf16), b.astype(bf16), None)


def _lane(shape):
    return lax.broadcasted_iota(jnp.int32, shape, len(shape) - 1)


def _mm(a, b, mode, tm, tn, name, out_dtype=f32, epi=None, extra=None, shards=1, comm=None):
    if mode == "nn":
        (m, k), n = a.shape, b.shape[-1] * shards
    elif mode == "nt":
        (m, k), n = a.shape, b.shape[-2]
    else:
        (k, m), n = a.shape, b.shape[1]
    assert m % tm == 0 and n % tn == 0, (name, m, n, tm, tn)
    per = (n // shards) // tn
    assert mode == "nt" or per * tn * shards == n, (name, n, tn, shards)
    nc = 512 if tn % 512 == 0 else (256 if tn % 256 == 0 else 128)
    ks = k // shards

    def body(a_ref, b_ref, *rest):
        for n0 in range(0, tn, nc):
            if mode == "nn":
                acc = jnp.dot(a_ref[...], b_ref[:, n0:n0 + nc], preferred_element_type=f32)
            elif mode == "nt" and shards > 1:
                acc = jnp.zeros((tm, nc), f32)
                for d in range(shards):
                    acc = acc + lax.dot_general(a_ref[:, d * ks:(d + 1) * ks], b_ref[d, n0:n0 + nc, :], (((1,), (1,)), ((), ())),
                                                preferred_element_type=f32)
            elif mode == "nt":
                acc = lax.dot_general(a_ref[...], b_ref[n0:n0 + nc, :], (((1,), (1,)), ((), ())), preferred_element_type=f32)
            else:
                acc = lax.dot_general(a_ref[...], b_ref[:, n0:n0 + nc], (((0,), (0,)), ((), ())), preferred_element_type=f32)
            if epi == "relu2":
                relu_ref, act_ref = rest
                r = jnp.maximum(acc, 0.0)
                relu_ref[:, n0:n0 + nc] = r.astype(bf16)
                act_ref[:, n0:n0 + nc] = (r * r).astype(bf16)
            elif epi == "relu2_bwd":
                relu_ref, o_ref = rest
                o_ref[:, n0:n0 + nc] = (acc * (2.0 * relu_ref[:, n0:n0 + nc].astype(f32))).astype(bf16)
            else:
                (o_ref,) = rest
                o_ref[:, n0:n0 + nc] = acc.astype(out_dtype)

    if mode == "tn":
        a_spec = pl.BlockSpec((k, tm), lambda j, i: (0, i))
    else:
        a_spec = pl.BlockSpec((tm, k), lambda j, i: (i, 0))
    if mode == "nt" and shards > 1:
        b_spec = pl.BlockSpec((shards, tn, ks), lambda j, i: (0, j, 0))
    elif mode == "nt":
        b_spec = pl.BlockSpec((tn, k), lambda j, i: (j, 0))
    elif mode == "nn" and shards > 1:
        b_spec = pl.BlockSpec((None, k, tn), lambda j, i: (j // per, 0, j % per))
    else:
        b_spec = pl.BlockSpec((k, tn), lambda j, i: (0, j))
    o_spec = pl.BlockSpec((tm, tn), lambda j, i: (i, j))
    in_specs = [a_spec, b_spec]
    args = [a, b]
    if epi == "relu2":
        out_shape = (jax.ShapeDtypeStruct((m, n), bf16), jax.ShapeDtypeStruct((m, n), bf16))
        out_specs = (o_spec, o_spec)
    elif epi == "relu2_bwd":
        in_specs.append(o_spec)
        args.append(extra)
        out_shape = jax.ShapeDtypeStruct((m, n), bf16)
        out_specs = o_spec
    elif mode == "tn" and shards > 1:
        out_shape = jax.ShapeDtypeStruct((shards, m, n // shards), out_dtype)
        out_specs = pl.BlockSpec((None, tm, tn), lambda j, i: (j // per, i, j % per))
    else:
        out_shape = jax.ShapeDtypeStruct((m, n), out_dtype)
        out_specs = o_spec
    single = not isinstance(out_shape, tuple)
    res, moved = _hosted(body, comm, name=name, grid=(n // tn, m // tm), in_specs=in_specs,
                         out_specs=(out_specs,) if single else out_specs, out_shape=(out_shape,) if single else out_shape, args=args)
    res = res[0] if single else res
    return res if comm is None else (res, moved)


def _row_spec(width, col=0):
    return pl.BlockSpec((TOK_BLK, width), lambda i: (i, col))


def _vec_spec(rows, width):
    return pl.BlockSpec((rows, width), lambda i: (0, 0))


def _ln_in(x, g, b, comm=None):
    t, d = x.shape

    def body(x_ref, g_ref, b_ref, h_ref, hb_ref):
        h, _, _ = _ln_fwd(x_ref[...], g_ref[...], b_ref[...])
        h_ref[...] = h
        hb_ref[...] = h.astype(bf16)

    return _hosted(
        body, comm, name="ln_in", grid=(t // TOK_BLK,),
        in_specs=[_row_spec(d), _vec_spec(1, d), _vec_spec(1, d)],
        out_specs=(_row_spec(d), _row_spec(d)),
        out_shape=(jax.ShapeDtypeStruct((t, d), f32), jax.ShapeDtypeStruct((t, d), bf16)),
        args=(x, g, b))


def _attn_post(o_gdn, proj, o_fox, g_gdn, g_fox2, comm=None):
    t = o_gdn.shape[0]

    def body(og_ref, z_ref, of_ref, gg_ref, gf_ref, out_ref):
        for h in range(GDN_HEADS):
            sl = slice(h * LANES, (h + 1) * LANES)
            og = og_ref[:, sl]
            z = z_ref[:, sl]
            r = lax.rsqrt(jnp.mean(og * og, -1, keepdims=True) + NORM_EPS)
            out_ref[:, sl] = (og * r * gg_ref[...] * (z * _sigmoid(z))).astype(bf16)
        lo = _lane((TOK_BLK, LANES)) < FOX_DH
        for pr in range(FOX_HEADS // 2):
            sl = slice(pr * LANES, (pr + 1) * LANES)
            of = of_ref[:, sl]
            sq = of * of
            s0 = jnp.sum(jnp.where(lo, sq, 0.0), -1, keepdims=True)
            s1 = jnp.sum(jnp.where(lo, 0.0, sq), -1, keepdims=True)
            r = lax.rsqrt(jnp.where(lo, s0, s1) * (1.0 / FOX_DH) + NORM_EPS)
            out_ref[:, 512 + pr * LANES:512 + (pr + 1) * LANES] = (of * r * gf_ref[...]).astype(bf16)

    return _hosted(
        body, comm, name="attn_post", grid=(t // TOK_BLK,),
        in_specs=[_row_spec(512), _row_spec(512, OFF_Z // 512), _row_spec(512), _vec_spec(1, LANES), _vec_spec(1, LANES)],
        out_specs=(_row_spec(D_MODEL),),
        out_shape=(jax.ShapeDtypeStruct((t, D_MODEL), bf16),),
        args=(o_gdn, proj, o_fox, g_gdn, g_fox2))


def _attn_post_bwd(dr1b, w_out, o_gdn, proj, o_fox, g_gdn, g_fox2):
    t = o_gdn.shape[0]

    def body(dr_ref, wo_ref, og_ref, z_ref, of_ref, gg_ref, gf_ref, dog_ref, dz_ref, dof_ref, pg_ref):
        i = pl.program_id(0)

        @pl.when(i == 0)
        def _():
            pg_ref[...] = jnp.zeros_like(pg_ref)

        da = _dot_nt(dr_ref[...], wo_ref[...], None)
        dgg = jnp.zeros((1, LANES), f32)
        for h in range(GDN_HEADS):
            sl = slice(h * LANES, (h + 1) * LANES)
            og = og_ref[:, sl]
            z = z_ref[:, sl]
            dout = da[:, sl]
            g = gg_ref[...]
            r = lax.rsqrt(jnp.mean(og * og, -1, keepdims=True) + NORM_EPS)
            sg = _sigmoid(z)
            silu = z * sg
            ng = og * r * g
            dng = dout * silu
            dz_ref[:, sl] = (dout * ng * (sg * (1.0 + z * (1.0 - sg)))).astype(bf16)
            dgg = dgg + jnp.sum(dng * og * r, 0, keepdims=True)
            gd = dng * g
            dog_ref[:, sl] = r * gd - og * (r * r * r) * jnp.mean(og * gd, -1, keepdims=True)
        pg_ref[0:1, :] += dgg
        lo = _lane((TOK_BLK, LANES)) < FOX_DH
        dgf = jnp.zeros((1, LANES), f32)
        for pr in range(FOX_HEADS // 2):
            sl = slice(pr * LANES, (pr + 1) * LANES)
            of = of_ref[:, sl]
            dout = da[:, 512 + pr * LANES:512 + (pr + 1) * LANES]
            g = gf_ref[...]
            sq = of * of
            s0 = jnp.sum(jnp.where(lo, sq, 0.0), -1, keepdims=True)
            s1 = jnp.sum(jnp.where(lo, 0.0, sq), -1, keepdims=True)
            r = lax.rsqrt(jnp.where(lo, s0, s1) * (1.0 / FOX_DH) + NORM_EPS)
            dgf = dgf + jnp.sum(dout * of * r, 0, keepdims=True)
            gd = dout * g
            xg = of * gd
            m0 = jnp.sum(jnp.where(lo, xg, 0.0), -1, keepdims=True)
            m1 = jnp.sum(jnp.where(lo, 0.0, xg), -1, keepdims=True)
            dof_ref[:, sl] = r * gd - of * (r * r * r) * (jnp.where(lo, m0, m1) * (1.0 / FOX_DH))
        pg_ref[1:2, :] += dgf

    return pl.pallas_call(
        body, name="attn_post_bwd", grid=(t // TOK_BLK,),
        in_specs=_product_specs(dr1b, w_out) + [_row_spec(512), _row_spec(512, OFF_Z // 512), _row_spec(512), _vec_spec(1, LANES), _vec_spec(1, LANES)],
        out_specs=(_row_spec(512), _row_spec(512), _row_spec(512), _vec_spec(8, LANES)),
        out_shape=(jax.ShapeDtypeStruct((t, 512), f32), jax.ShapeDtypeStruct((t, 512), bf16),
                   jax.ShapeDtypeStruct((t, 512), f32), jax.ShapeDtypeStruct((8, LANES), f32)),
        compiler_params=_params(("arbitrary",)),
    )(dr1b, w_out, o_gdn, proj, o_fox, g_gdn, g_fox2)


def _product_specs(lhs, rhs):
    return [_row_spec(lhs.shape[1]), pl.BlockSpec(rhs.shape, lambda i: (0, 0))]


def _ln1(h0, lhs, rhs, g, b, comm=None):
    t, d = h0.shape

    def body(h0_ref, lhs_ref, rhs_ref, g_ref, b_ref, h_ref, hb_ref, xh_ref, rs_ref):
        mix = jnp.dot(lhs_ref[...], rhs_ref[...], preferred_element_type=f32)
        h, xhat, rstd = _ln_fwd(ALPHA * h0_ref[...] + mix, g_ref[...], b_ref[...])
        h_ref[...] = h
        hb_ref[...] = h.astype(bf16)
        xh_ref[...] = xhat
        rs_ref[...] = jnp.broadcast_to(rstd, rs_ref.shape)

    return _hosted(
        body, comm, name="ln1", grid=(t // TOK_BLK,),
        in_specs=[_row_spec(d)] + _product_specs(lhs, rhs) + [_vec_spec(1, d), _vec_spec(1, d)],
        out_specs=(_row_spec(d), _row_spec(d), _row_spec(d), _row_spec(LANES)),
        out_shape=(jax.ShapeDtypeStruct((t, d), f32), jax.ShapeDtypeStruct((t, d), bf16),
                   jax.ShapeDtypeStruct((t, d), f32), jax.ShapeDtypeStruct((t, LANES), f32)),
        args=(h0, lhs, rhs, g, b))


def _ln2_loss(h1, lhs, rhs, pb, w_ple, gp, b_gate, g, b, target):
    t, d = h1.shape

    def body(h1_ref, lhs_ref, rhs_ref, pb_ref, wp_ref, gp_ref, bg_ref, g_ref, b_ref, t_ref, dr_ref, drb_ref, dpe_ref, dgp_ref, pg_ref):
        i = pl.program_id(0)

        @pl.when(i == 0)
        def _():
            pg_ref[...] = jnp.zeros_like(pg_ref)

        ff = jnp.dot(lhs_ref[...], rhs_ref[...], preferred_element_type=f32)
        sig = _sigmoid(gp_ref[...] + bg_ref[...])
        pe = jnp.concatenate([jnp.dot(pb_ref[...], wp_ref[s], preferred_element_type=f32) for s in range(w_ple.shape[0])], axis=1)
        r2 = ALPHA * h1_ref[...] + ff + pe * sig
        y, xhat, rstd = _ln_fwd(r2, g_ref[...], b_ref[...])
        err = y - t_ref[...]
        dy = err * (1.0 / d)
        dr = _ln_bwd(dy, xhat, rstd, g_ref[...])
        dr_ref[...] = dr
        drb_ref[...] = dr.astype(bf16)
        dpe_ref[...] = (dr * sig).astype(bf16)
        dgp = dr * pe * sig * (1.0 - sig)
        dgp_ref[...] = dgp.astype(bf16)
        pg_ref[0:1, :] += jnp.sum(dy * xhat, 0, keepdims=True)
        pg_ref[1:2, :] += jnp.sum(dy, 0, keepdims=True)
        pg_ref[2:3, :] += jnp.sum(dgp, 0, keepdims=True)
        pg_ref[3:4, :] += 0.5 * jnp.sum(jnp.mean(err * err, -1, keepdims=True), 0, keepdims=True)

    return pl.pallas_call(
        body, name="ln2_loss", grid=(t // TOK_BLK,),
        in_specs=[_row_spec(d)] + _product_specs(lhs, rhs) + [_row_spec(pb.shape[1]), pl.BlockSpec(w_ple.shape, lambda i: (0, 0, 0)), _row_spec(d)]
        + [_vec_spec(1, d)] * 3 + [_row_spec(d)],
        out_specs=(_row_spec(d), _row_spec(d), _row_spec(d), _row_spec(d), _vec_spec(8, d)),
        out_shape=(jax.ShapeDtypeStruct((t, d), f32), jax.ShapeDtypeStruct((t, d), bf16), jax.ShapeDtypeStruct((t, d), bf16),
                   jax.ShapeDtypeStruct((t, d), bf16), jax.ShapeDtypeStruct((8, d), f32)),
        compiler_params=_params(("arbitrary",)),
    )(h1, lhs, rhs, pb, w_ple, gp, b_gate, g, b, target)


def _ln1_bwd(dr2, dup, w_up, dgp, w_gate, xhat, rstd, g, comm=None):
    t, d = dr2.shape
    ks = w_up.shape[2]

    def body(dr2_ref, dup_ref, wup_ref, dgp_ref, wg_ref, xh_ref, rs_ref, g_ref, dr_ref, drb_ref, pg_ref):
        i = pl.program_id(0)

        @pl.when(i == 0)
        def _():
            pg_ref[...] = jnp.zeros_like(pg_ref)

        dh = ALPHA * dr2_ref[...] + _dot_nt(dgp_ref[...], wg_ref[...], None)
        for s in range(w_up.shape[0]):
            dh = dh + _dot_nt(dup_ref[:, s * ks:(s + 1) * ks], wup_ref[s], None)
        xhat = xh_ref[...]
        dr = _ln_bwd(dh, xhat, rs_ref[:, 0:1], g_ref[...])
        dr_ref[...] = dr
        drb_ref[...] = dr.astype(bf16)
        pg_ref[0:1, :] += jnp.sum(dh * xhat, 0, keepdims=True)
        pg_ref[1:2, :] += jnp.sum(dh, 0, keepdims=True)

    return _hosted(
        body, comm, name="ln1_bwd", grid=(t // TOK_BLK,),
        in_specs=[_row_spec(d), _row_spec(dup.shape[1]), pl.BlockSpec(w_up.shape, lambda i: (0, 0, 0))] + _product_specs(dgp, w_gate)
        + [_row_spec(d), _row_spec(LANES), _vec_spec(1, d)],
        out_specs=(_row_spec(d), _row_spec(d), _vec_spec(8, d)),
        out_shape=(jax.ShapeDtypeStruct((t, d), f32), jax.ShapeDtypeStruct((t, d), bf16), jax.ShapeDtypeStruct((8, d), f32)),
        args=(dr2, dup, w_up, dgp, w_gate, xhat, rstd, g))


def _ln_in_bwd(x, dr1, dmm, g, comm=None):
    t, d = x.shape

    def body(x_ref, dr1_ref, dmm_ref, g_ref, dx_ref, pg_ref):
        i = pl.program_id(0)

        @pl.when(i == 0)
        def _():
            pg_ref[...] = jnp.zeros_like(pg_ref)

        dh = ALPHA * dr1_ref[...] + dmm_ref[...]
        _, xhat, rstd = _ln_fwd(x_ref[...], g_ref[...], 0.0)
        dx_ref[...] = _ln_bwd(dh, xhat, rstd, g_ref[...])
        pg_ref[0:1, :] += jnp.sum(dh * xhat, 0, keepdims=True)
        pg_ref[1:2, :] += jnp.sum(dh, 0, keepdims=True)

    return _hosted(
        body, comm, name="ln_in_bwd", grid=(t // TOK_BLK,),
        in_specs=[_row_spec(d)] * 3 + [_vec_spec(1, d)],
        out_specs=(_row_spec(d), _vec_spec(8, d)),
        out_shape=(jax.ShapeDtypeStruct((t, d), f32), jax.ShapeDtypeStruct((8, d), f32)),
        args=(x, dr1, dmm, g))


def _tri(n, upper=False, strict=False):
    r = lax.broadcasted_iota(jnp.int32, (n, n), 0)
    c = lax.broadcasted_iota(jnp.int32, (n, n), 1)
    if upper:
        m = (c > r) if strict else (c >= r)
    else:
        m = (c < r) if strict else (c <= r)
    return jnp.where(m, 1.0, 0.0).astype(f32)


def _gate_values(x, bias, alog, lane):
    z = x + bias
    return jnp.where(lane < 4, _sigmoid(z), jnp.where(lane < 8, -jnp.exp(alog) * _softplus(z), jnp.where(lane < 16, -_softplus(-z), 0.0)))


def _gates(proj, bias_row, alog_row):
    t = proj.shape[0]
    nch = t // CHUNK

    def body(x_ref, bias_ref, alog_ref, gates_ref, gcum_ref, gcumt_ref):
        lane = _lane((t, LANES))
        gates = _gate_values(x_ref[...], bias_ref[...], alog_ref[...], lane)
        gates_ref[...] = gates
        g3 = gates.reshape(nch, CHUNK, LANES)
        tri = jnp.broadcast_to(_tri(CHUNK)[None], (nch, CHUNK, CHUNK))
        loc = jnp.einsum("bij,bjk->bik", tri, g3, precision=HI, preferred_element_type=f32)
        tot = jnp.sum(g3, axis=1)
        offs = _dot(_tri(nch, strict=True), tot)
        glob = loc + offs[:, None, :]
        lane3 = _lane((nch, CHUNK, LANES))
        gcum = jnp.where(lane3 < 4, g3, jnp.where(lane3 < 8, loc, glob)).reshape(t, LANES)
        gcum_ref[...] = gcum
        gcumt_ref[...] = gcum.T

    return pl.pallas_call(
        body, name="gates", grid=(1,),
        in_specs=[pl.BlockSpec((t, LANES), lambda i: (0, SEG_SMALL // LANES)), _vec_spec(1, LANES), _vec_spec(1, LANES)],
        out_specs=(pl.BlockSpec((t, LANES), lambda i: (0, 0)), pl.BlockSpec((t, LANES), lambda i: (0, 0)),
                   pl.BlockSpec((LANES, t), lambda i: (0, 0))),
        out_shape=(jax.ShapeDtypeStruct((t, LANES), f32), jax.ShapeDtypeStruct((t, LANES), f32), jax.ShapeDtypeStruct((LANES, t), f32)),
        compiler_params=_params(("arbitrary",)),
    )(proj, bias_row, alog_row)


def _gates_bwd(proj, bias_row, alog_row, gates, dgates, dccol, dct):
    t = proj.shape[0]
    nch = t // CHUNK

    def body(x_ref, bias_ref, alog_ref, gates_ref, dg_ref, dcc_ref, dct_ref, dx_ref, pg_ref):
        lane = _lane((t, LANES))
        d = dg_ref[...] + dcc_ref[...] + dct_ref[...].T
        d3 = d.reshape(nch, CHUNK, LANES)
        tri = jnp.broadcast_to(_tri(CHUNK, upper=True)[None], (nch, CHUNK, CHUNK))
        loc = jnp.einsum("bij,bjk->bik", tri, d3, precision=HI, preferred_element_type=f32)
        tot = jnp.sum(d3, axis=1)
        offs = _dot(_tri(nch, upper=True, strict=True), tot)
        glob = loc + offs[:, None, :]
        lane3 = _lane((nch, CHUNK, LANES))
        dpre = jnp.where(lane3 < 4, d3, jnp.where(lane3 < 8, loc, glob)).reshape(t, LANES)
        z = x_ref[...] + bias_ref[...]
        sg = _sigmoid(z)
        dx = jnp.where(lane < 4, dpre * sg * (1.0 - sg),
                       jnp.where(lane < 8, dpre * (-jnp.exp(alog_ref[...])) * sg, jnp.where(lane < 16, dpre * (1.0 - sg), 0.0)))
        dx_ref[...] = dx.astype(bf16)
        pg_ref[...] = jnp.zeros_like(pg_ref)
        pg_ref[0:1, :] = jnp.sum(dx, 0, keepdims=True)
        pg_ref[1:2, :] = jnp.sum(jnp.where((lane >= 4) & (lane < 8), dpre * gates_ref[...], 0.0), 0, keepdims=True)

    full = pl.BlockSpec((t, LANES), lambda i: (0, 0))
    return pl.pallas_call(
        body, name="gates_bwd", grid=(1,),
        in_specs=[pl.BlockSpec((t, LANES), lambda i: (0, SEG_SMALL // LANES)), _vec_spec(1, LANES), _vec_spec(1, LANES),
                  full, full, full, pl.BlockSpec((LANES, t), lambda i: (0, 0))],
        out_specs=(full, _vec_spec(8, LANES)),
        out_shape=(jax.ShapeDtypeStruct((t, LANES), bf16), jax.ShapeDtypeStruct((8, LANES), f32)),
        compiler_params=_params(("arbitrary",)),
    )(proj, bias_row, alog_row, gates, dgates, dccol, dct)


def _conv_act(u, cw, row, t):
    c = cw[3:4, :] * u
    for jj in range(CONV_W - 1):
        sh = CONV_W - 1 - jj
        c = c + cw[jj:jj + 1, :] * jnp.where(row >= sh, pltpu.roll(u, sh, axis=0), 0.0)
    return c


def _gdn_conv(proj, conv_w, comm=None):
    t = proj.shape[0]
    nblk = GDN_QKV // LANES

    def body(u_ref, cw_ref, c_ref, y_ref):
        j = pl.program_id(0)
        row = lax.broadcasted_iota(jnp.int32, (t, LANES), 0)
        c = _conv_act(u_ref[...], cw_ref[...], row, t)
        c_ref[...] = c
        s = c * _sigmoid(c)
        r = lax.rsqrt(jnp.sum(s * s, -1, keepdims=True) + NORM_EPS)
        scale = jnp.where(j < GDN_HEADS, GDN_DK ** -0.5, 1.0)
        y_ref[...] = jnp.where(j < 2 * GDN_HEADS, s * (r * scale), s)

    blk = pl.BlockSpec((t, LANES), lambda j: (0, j))
    return _hosted(
        body, comm, name="gdn_conv", grid=(nblk,),
        in_specs=[blk, pl.BlockSpec((CONV_W, LANES), lambda j: (0, j))],
        out_specs=(blk, blk),
        out_shape=(jax.ShapeDtypeStruct((t, GDN_QKV), f32), jax.ShapeDtypeStruct((t, GDN_QKV), f32)),
        args=(proj, conv_w))


def _gdn_conv_bwd(proj, conv_w, c, dy, comm=None):
    t = proj.shape[0]
    nblk = GDN_QKV // LANES

    def body(u_ref, cw_ref, c_ref, dy_ref, du_ref, dcw_ref):
        j = pl.program_id(0)
        row = lax.broadcasted_iota(jnp.int32, (t, LANES), 0)
        u = u_ref[...]
        cw = cw_ref[...]
        c = c_ref[...]
        dy = dy_ref[...]
        sg = _sigmoid(c)
        s = c * sg
        r = lax.rsqrt(jnp.sum(s * s, -1, keepdims=True) + NORM_EPS)
        n = s * r
        scale = jnp.where(j < GDN_HEADS, GDN_DK ** -0.5, 1.0)
        dn = dy * scale
        ds = jnp.where(j < 2 * GDN_HEADS, r * (dn - n * jnp.sum(dn * n, -1, keepdims=True)), dy)
        dc = ds * (sg * (1.0 + c * (1.0 - sg)))
        du = cw[3:4, :] * dc
        dcw_ref[...] = jnp.zeros_like(dcw_ref)
        dcw_ref[3:4, :] = jnp.sum(dc * u, 0, keepdims=True)
        for jj in range(CONV_W - 1):
            sh = CONV_W - 1 - jj
            du = du + cw[jj:jj + 1, :] * jnp.where(row < t - sh, pltpu.roll(dc, t - sh, axis=0), 0.0)
            dcw_ref[jj:jj + 1, :] = jnp.sum(dc * jnp.where(row >= sh, pltpu.roll(u, sh, axis=0), 0.0), 0, keepdims=True)
        du_ref[...] = du.astype(bf16)

    blk = pl.BlockSpec((t, LANES), lambda j: (0, j))
    return _hosted(
        body, comm, name="gdn_conv_bwd", grid=(nblk,),
        in_specs=[blk, pl.BlockSpec((CONV_W, LANES), lambda j: (0, j)), blk, blk],
        out_specs=(blk, pl.BlockSpec((8, LANES), lambda j: (0, j))),
        out_shape=(jax.ShapeDtypeStruct((t, GDN_QKV), bf16), jax.ShapeDtypeStruct((8, GDN_QKV), f32)),
        args=(proj, conv_w, c, dy))


def _chunk_masks():
    r = lax.broadcasted_iota(jnp.int32, (CHUNK, CHUNK), 0)
    c = lax.broadcasted_iota(jnp.int32, (CHUNK, CHUNK), 1)
    return r >= c, r > c, r == c


def _col_to_row(col, eye):
    return jnp.sum(jnp.where(eye, col, 0.0), axis=0, keepdims=True)


def _row_to_col(row, eye):
    return jnp.sum(jnp.where(eye, row, 0.0), axis=1, keepdims=True)


NN = (((1,), (0,)), ((), ()))
NT = (((1,), (1,)), ((), ()))
TN = (((0,), (0,)), ((), ()))
GDN_GROUP = 4


def _mx(a, b, dims=NN, passes=1):
    d = lambda p, q: lax.dot_general(p, q, dims, preferred_element_type=f32)
    ah, bh = a.astype(bf16), b.astype(bf16)
    if passes == 1:
        return d(ah, bh)
    al = (a - ah.astype(f32)).astype(bf16)
    bl = (b - bh.astype(f32)).astype(bf16)
    return d(ah, bh) + (d(ah, bl) + d(al, bh))


def _gdn_decay(gam, masks):
    causal, _, eye = masks
    return jnp.exp(jnp.where(causal, gam - _col_to_row(gam, eye), NEG))


def _gdn_local(y, gcum, comm=None):
    t = y.shape[0]
    nch = t // CHUNK
    rows_blk = GDN_GROUP * CHUNK

    def body(y_ref, g_ref, u_ref, w_ref, qk_ref, tinv_ref):
        masks = _chunk_masks()
        _, strict, eye = masks
        ids = [(j, h) for j in range(GDN_GROUP) for h in range(GDN_HEADS)]
        rs = lambda j: slice(j * CHUNK, (j + 1) * CHUNK)
        col = lambda base, h: slice(base + h * LANES, base + (h + 1) * LANES)
        kn = [y_ref[rs(j), col(512, h)] for j, h in ids]
        beta = [g_ref[rs(j), h:h + 1] for j, h in ids]
        gam = [g_ref[rs(j), 4 + h:5 + h] for j, h in ids]
        dec = [_gdn_decay(g, masks) for g in gam]
        x = [-jnp.where(strict, _mx(k, k, NT) * d * b, 0.0) for k, d, b in zip(kn, dec, beta)]
        tinv = [jnp.where(eye, 1.0, 0.0) + a for a in x]
        for _ in range(5):
            x = [_mx(a, a, NN, 3) for a in x]
            tinv = [t_ + _mx(t_, a, NN, 3) for t_, a in zip(tinv, x)]
        for (j, h), t_, k, d, b, g in zip(ids, tinv, kn, dec, beta, gam):
            u_ref[rs(j), col(0, h)] = _mx(t_, b * y_ref[rs(j), col(1024, h)])
            w_ref[rs(j), col(0, h)] = _mx(t_, (b * jnp.exp(g)) * k)
            qk_ref[j, h] = _mx(y_ref[rs(j), col(0, h)], k, NT) * d
            tinv_ref[j, h] = t_

    mat = pl.BlockSpec((GDN_GROUP, GDN_HEADS, CHUNK, CHUNK), lambda n: (n, 0, 0, 0))
    return _hosted(
        body, comm, name="gdn_local", grid=(nch // GDN_GROUP,),
        in_specs=[pl.BlockSpec((rows_blk, GDN_QKV), lambda n: (n, 0)), pl.BlockSpec((rows_blk, LANES), lambda n: (n, 0))],
        out_specs=(pl.BlockSpec((rows_blk, 512), lambda n: (n, 0)), pl.BlockSpec((rows_blk, 512), lambda n: (n, 0)), mat, mat),
        out_shape=(jax.ShapeDtypeStruct((t, 512), f32), jax.ShapeDtypeStruct((t, 512), f32),
                   jax.ShapeDtypeStruct((nch, GDN_HEADS, CHUNK, CHUNK), f32), jax.ShapeDtypeStruct((nch, GDN_HEADS, CHUNK, CHUNK), f32)),
        args=(y, gcum))


def _gdn_fwd(y, gcum, u, w, qk, comm=None):
    t = y.shape[0]
    nch = t // CHUNK

    def body(y_ref, g_ref, u_ref, w_ref, qk_ref, o_ref, sall_ref, s_ref):
        @pl.when(pl.program_id(0) == 0)
        def _():
            s_ref[...] = jnp.zeros_like(s_ref)

        heads = range(GDN_HEADS)
        sl = [slice(h * LANES, (h + 1) * LANES) for h in heads]
        gam = [g_ref[:, 4 + h:5 + h] for h in heads]
        gam_last = [g[CHUNK - 1:CHUNK, :] for g in gam]
        s = [s_ref[h] for h in heads]
        for h in heads:
            sall_ref[0, h] = s[h]
        ws = [_mx(w_ref[:, sl[h]], s[h]) for h in heads]
        qs = [_mx(y_ref[:, sl[h]] * jnp.exp(gam[h]), s[h]) for h in heads]
        vn = [u_ref[:, sl[h]] - ws[h] for h in heads]
        av = [_mx(qk_ref[0, h], vn[h]) for h in heads]
        kv = [_mx(y_ref[:, 512 + h * LANES:512 + (h + 1) * LANES] * jnp.exp(gam_last[h] - gam[h]), vn[h], TN) for h in heads]
        for h in heads:
            o_ref[:, sl[h]] = qs[h] + av[h]
            s_ref[h] = jnp.exp(gam_last[h]) * s[h] + kv[h]

    row = lambda width: pl.BlockSpec((CHUNK, width), lambda n: (n, 0))
    return _hosted(
        body, comm, name="gdn_fwd", grid=(nch,),
        in_specs=[row(GDN_QKV), row(LANES), row(512), row(512), pl.BlockSpec((1, GDN_HEADS, CHUNK, CHUNK), lambda n: (n, 0, 0, 0))],
        out_specs=(row(512), pl.BlockSpec((1, GDN_HEADS, LANES, LANES), lambda n: (n, 0, 0, 0))),
        out_shape=(jax.ShapeDtypeStruct((t, 512), f32), jax.ShapeDtypeStruct((nch, GDN_HEADS, LANES, LANES), f32)),
        scratch_shapes=[pltpu.VMEM((GDN_HEADS, LANES, LANES), f32)],
        args=(y, gcum, u, w, qk))


def _gdn_bwd(y, gcum, u_all, w_all, qk_all, tinv_all, sall, do, comm=None):
    t = y.shape[0]
    nch = t // CHUNK

    def body(y_ref, g_ref, u_ref, w_ref, qk_ref, tinv_ref, sall_ref, do_ref, dy_ref, dg_ref, ds_ref):
        @pl.when(pl.program_id(0) == 0)
        def _():
            ds_ref[...] = jnp.zeros_like(ds_ref)

        masks = _chunk_masks()
        causal, strict, eye = masks
        lane = _lane((CHUNK, LANES))
        row = lax.broadcasted_iota(jnp.int32, (CHUNK, 1), 0)
        heads = range(GDN_HEADS)
        each = lambda f, *ls: [f(*a) for a in zip(*ls)]
        rsum = lambda a: jnp.sum(a, axis=1, keepdims=True)
        sl = [slice(h * LANES, (h + 1) * LANES) for h in heads]
        qn = [y_ref[:, sl[h]] for h in heads]
        kn = [y_ref[:, 512 + h * LANES:512 + (h + 1) * LANES] for h in heads]
        v = [y_ref[:, 1024 + h * LANES:1024 + (h + 1) * LANES] for h in heads]
        beta = [g_ref[:, h:h + 1] for h in heads]
        gam = [g_ref[:, 4 + h:5 + h] for h in heads]
        gam_last = [g[CHUNK - 1:CHUNK, :] for g in gam]
        dec = [_gdn_decay(g, masks) for g in gam]
        e = [jnp.exp(g) for g in gam]
        f = each(lambda gl_, g: jnp.exp(gl_ - g), gam_last, gam)
        gl = [jnp.exp(g) for g in gam_last]
        u = [u_ref[:, sl[h]] for h in heads]
        w = [w_ref[:, sl[h]] for h in heads]
        qk = [qk_ref[0, h] for h in heads]
        tinv = [tinv_ref[0, h] for h in heads]
        s = [sall_ref[0, h] for h in heads]
        dsn = [ds_ref[h] for h in heads]
        d_o = [do_ref[:, sl[h]] for h in heads]
        qd = each(lambda a, b: a * b, qn, e)
        kd = each(lambda a, b: a * b, kn, f)
        ws = each(_mx, w, s)
        kds = each(_mx, kd, dsn)
        qkdo = each(lambda a, b: _mx(a, b, TN), qk, d_o)
        dqd = each(lambda a, b: _mx(a, b, NT), d_o, s)
        qddo = each(lambda a, b: _mx(a, b, TN), qd, d_o)
        kkd = each(lambda k, d: _mx(k, k, NT) * d, kn, dec)
        vn = each(lambda a, b: a - b, u, ws)
        dvn = each(lambda a, b: a + b, qkdo, kds)
        dqk = each(lambda a, b: jnp.where(causal, _mx(a, b, NT), 0.0), d_o, vn)
        dkd = each(lambda a, b: _mx(a, b, NT), vn, dsn)
        dw = each(lambda a, b: -_mx(a, b, NT), dvn, s)
        wdvn = each(lambda a, b: _mx(a, b, TN), w, dvn)
        dgl = each(lambda a, b: jnp.sum(rsum(a * b), axis=0, keepdims=True), dsn, s)
        for h in heads:
            ds_ref[h] = qddo[h] - wdvn[h] + gl[h] * dsn[h]
        dru = each(lambda a, b: _mx(a, b, TN), tinv, dvn)
        drw = each(lambda a, b: _mx(a, b, TN), tinv, dw)
        dqkr = each(lambda a, b: a * b, dqk, dec)
        dq1 = each(_mx, dqkr, kn)
        dk1 = each(lambda a, b: _mx(a, b, TN), dqkr, qn)
        dnu = each(lambda a, b: _mx(a, b, NT), dru, u)
        dnw = each(lambda a, b: _mx(a, b, NT), drw, w)
        dn = each(lambda a, b: jnp.where(strict, -(a + b), 0.0), dnu, dnw)
        dkk = each(lambda a, b, d: a * b * d, dn, beta, dec)
        dk2 = each(_mx, dkk, kn)
        dk3 = each(lambda a, b: _mx(a, b, TN), dkk, kn)
        dgates = jnp.zeros((CHUNK, LANES), f32)
        for h in heads:
            drw_k = rsum(drw[h] * kn[h])
            dbeta = rsum(dru[h] * v[h]) + e[h] * drw_k + rsum(dn[h] * kkd[h])
            m = dn[h] * (kkd[h] * beta[h]) + dqk[h] * qk[h]
            de = beta[h] * drw_k + rsum(dqd[h] * qn[h])
            df = rsum(dkd[h] * kn[h])
            dgam = rsum(m) - _row_to_col(jnp.sum(m, axis=0, keepdims=True), eye) + de * e[h] - df * f[h]
            dgam_last = jnp.sum(df * f[h], axis=0, keepdims=True) + dgl[h] * gl[h]
            dgam = dgam + jnp.where(row == CHUNK - 1, dgam_last, 0.0)
            dy_ref[:, sl[h]] = dq1[h] + dqd[h] * e[h]
            dy_ref[:, 512 + h * LANES:512 + (h + 1) * LANES] = (beta[h] * e[h]) * drw[h] + dk2[h] + dk3[h] + dk1[h] + dkd[h] * f[h]
            dy_ref[:, 1024 + h * LANES:1024 + (h + 1) * LANES] = beta[h] * dru[h]
            dgates = dgates + jnp.where(lane == h, dbeta, 0.0) + jnp.where(lane == 4 + h, dgam, 0.0)
        dg_ref[...] = dgates

    rev = lambda width: pl.BlockSpec((CHUNK, width), lambda n: (nch - 1 - n, 0))
    mat = lambda d: pl.BlockSpec((1, GDN_HEADS, d, d), lambda n: (nch - 1 - n, 0, 0, 0))
    return _hosted(
        body, comm, name="gdn_bwd", grid=(nch,),
        in_specs=[rev(GDN_QKV), rev(LANES), rev(512), rev(512), mat(CHUNK), mat(CHUNK), mat(LANES), rev(512)],
        out_specs=(rev(GDN_QKV), rev(LANES)),
        out_shape=(jax.ShapeDtypeStruct((t, GDN_QKV), f32), jax.ShapeDtypeStruct((t, LANES), f32)),
        scratch_shapes=[pltpu.VMEM((GDN_HEADS, LANES, LANES), f32)],
        args=(y, gcum, u_all, w_all, qk_all, tinv_all, sall, do))


FOX_CLASSES = 4


def _fox_groups(t):
    nq = t // FOX_BQ
    ncls = min(FOX_CLASSES, nq)
    per = nq // ncls
    return [(g * per, per, (g + 1) * per * FOX_BQ) for g in range(ncls)]


def _fox_causal(i, keys):
    rows = i * FOX_BQ + lax.broadcasted_iota(jnp.int32, (FOX_BQ, keys), 0)
    return lax.broadcasted_iota(jnp.int32, (FOX_BQ, keys), 1) <= rows


def _fox_scores(q_ref, k_ref, gcumt_ref, h, causal):
    pr = h // 2
    lo = (h % 2) * FOX_DH
    lane = _lane((FOX_BQ, LANES))
    mask = (lane >= lo) & (lane < lo + FOX_DH)
    qm = jnp.where(mask, q_ref[:, pr * LANES:(pr + 1) * LANES] * (FOX_DH ** -0.5), 0.0).astype(bf16)
    kp = k_ref[:, pr * LANES:(pr + 1) * LANES].astype(bf16)
    s = _dot_nt(qm, kp, None) - gcumt_ref[8 + h:9 + h, :]
    return jnp.where(causal, s, NEG), mask, qm, kp


def _fox_fwd(proj, gcum, gcumt, ride=None):
    c0 = SEG_FOX // 512

    def group_call(q0, nq, keys, comm):
        def body(q_ref, k_ref, v_ref, gcumt_ref, o_ref, lse_ref):
            causal = _fox_causal(q0 + pl.program_id(0), keys)
            lane = _lane((FOX_BQ, LANES))
            lse_all = jnp.zeros((FOX_BQ, LANES), f32)
            for pr in range(FOX_HEADS // 2):
                vp = v_ref[:, pr * LANES:(pr + 1) * LANES].astype(bf16)
                o_pair = jnp.zeros((FOX_BQ, LANES), f32)
                for h in (2 * pr, 2 * pr + 1):
                    s, mask, _, _ = _fox_scores(q_ref, k_ref, gcumt_ref, h, causal)
                    m = jnp.max(s, axis=1, keepdims=True)
                    p = jnp.exp(s - m)
                    l = jnp.sum(p, axis=1, keepdims=True)
                    o_h = _dot(p.astype(bf16), vp, None) * (1.0 / l)
                    o_pair = jnp.where(mask, o_h, o_pair)
                    lse_all = jnp.where(lane == h, m + jnp.log(l), lse_all)
                o_ref[:, pr * LANES:(pr + 1) * LANES] = o_pair
            lse_ref[...] = lse_all

        seen = lambda col: pl.BlockSpec((keys, 512), lambda i: (0, col))
        return _hosted(
            body, comm, name=f"fox_fwd_{keys}", grid=(nq,),
            in_specs=[pl.BlockSpec((FOX_BQ, 512), lambda i: (q0 + i, c0)), seen(c0 + 1), seen(c0 + 2),
                      pl.BlockSpec((LANES, keys), lambda i: (0, 0))],
            out_specs=(pl.BlockSpec((FOX_BQ, 512), lambda i: (i, 0)), pl.BlockSpec((FOX_BQ, LANES), lambda i: (i, 0))),
            out_shape=(jax.ShapeDtypeStruct((nq * FOX_BQ, 512), f32), jax.ShapeDtypeStruct((nq * FOX_BQ, LANES), f32)),
            args=(proj, proj, proj, gcumt))

    parts = []
    for n, g in enumerate(_fox_groups(proj.shape[0])):
        hook = ride(n) if ride else None
        part, moved = group_call(*g, hook[0] if hook else None)
        parts.append(part)
        if hook:
            hook[1](moved)
    return jnp.concatenate([o for o, _ in parts], axis=0), jnp.concatenate([l for _, l in parts], axis=0)


def _fox_bwd(proj, gcum, gcumt, o, lse, do, ride=None):
    t = proj.shape[0]
    c0 = SEG_FOX // 512

    def group_call(q0, nq, keys, acc, comm):
        first = acc is None

        def body(q_ref, k_ref, v_ref, gcumt_ref, o_ref, lse_ref, do_ref, *rest):
            dq_ref, dk_ref, dv_ref, dcc_ref, dct_ref = rest[-5:]
            j = pl.program_id(0)
            causal = _fox_causal(q0 + j, keys)

            @pl.when(j == 0)
            def _():
                if first:
                    dk_ref[...] = jnp.zeros_like(dk_ref)
                    dv_ref[...] = jnp.zeros_like(dv_ref)
                    dct_ref[...] = jnp.zeros_like(dct_ref)
                else:
                    dk_ref[...], dv_ref[...], dct_ref[...] = rest[0][...], rest[1][...], rest[2][...]

            lane = _lane((FOX_BQ, LANES))
            dcc = jnp.zeros((FOX_BQ, LANES), f32)
            scale = FOX_DH ** -0.5
            for pr in range(FOX_HEADS // 2):
                sl = slice(pr * LANES, (pr + 1) * LANES)
                vp = v_ref[:, sl].astype(bf16)
                dq_pair = jnp.zeros((FOX_BQ, LANES), f32)
                for h in (2 * pr, 2 * pr + 1):
                    s, mask, qm, kp = _fox_scores(q_ref, k_ref, gcumt_ref, h, causal)
                    p = jnp.exp(s - lse_ref[:, h:h + 1])
                    dom = jnp.where(mask, do_ref[:, sl], 0.0)
                    delta = jnp.sum(dom * o_ref[:, sl], axis=1, keepdims=True)
                    domb = dom.astype(bf16)
                    ds = p * (_dot_nt(domb, vp, None) - delta)
                    dsb = ds.astype(bf16)
                    dv_ref[:, sl] += _dot_tn(p.astype(bf16), domb, None)
                    dk_ref[:, sl] += _dot_tn(dsb, qm, None)
                    dq_pair = jnp.where(mask, _dot(dsb, kp, None) * scale, dq_pair)
                    dcc = jnp.where(lane == 8 + h, jnp.sum(ds, axis=1, keepdims=True), dcc)
                    dct_ref[8 + h:9 + h, :] += -jnp.sum(ds, axis=0, keepdims=True)
                dq_ref[:, sl] = dq_pair.astype(bf16)
            dcc_ref[...] = dcc

        qblk = lambda col: pl.BlockSpec((FOX_BQ, 512), lambda i: (q0 + i, col))
        oblk = pl.BlockSpec((FOX_BQ, 512), lambda i: (i, 0))
        seen = lambda col: pl.BlockSpec((keys, 512), lambda i: (0, col))
        rblk = pl.BlockSpec((FOX_BQ, LANES), lambda i: (q0 + i, 0))
        seen_t = pl.BlockSpec((LANES, keys), lambda i: (0, 0))
        in_specs = [qblk(c0), seen(c0 + 1), seen(c0 + 2), seen_t, qblk(0), rblk, qblk(0)]
        args = [proj, proj, proj, gcumt, o, lse, do]
        aliases = {}
        if not first:
            in_specs += [seen(0), seen(0), seen_t]
            args += list(acc)
            aliases = {7: 1, 8: 2, 9: 4}
        return _hosted(
            body, comm, name=f"fox_bwd_{keys}", grid=(nq,), in_specs=in_specs,
            out_specs=(oblk, seen(0), seen(0), pl.BlockSpec((FOX_BQ, LANES), lambda i: (i, 0)), seen_t),
            out_shape=(jax.ShapeDtypeStruct((nq * FOX_BQ, 512), bf16), jax.ShapeDtypeStruct((t, 512), f32), jax.ShapeDtypeStruct((t, 512), f32),
                       jax.ShapeDtypeStruct((nq * FOX_BQ, LANES), f32), jax.ShapeDtypeStruct((LANES, t), f32)),
            aliases=aliases, args=args)

    acc, dqs, dccs = None, [], []
    for n, g in enumerate(reversed(_fox_groups(t))):
        hook = ride(n) if ride else None
        (dq, dk, dv, dcc, dct), moved = group_call(*g, acc, hook[0] if hook else None)
        if hook:
            hook[1](moved)
        acc = (dk, dv, dct)
        dqs.insert(0, dq)
        dccs.insert(0, dcc)
    return jnp.concatenate(dqs, axis=0), acc[0], acc[1], jnp.concatenate(dccs, axis=0), acc[2]


def _row(v, width=None):
    v = v.reshape(1, -1).astype(f32)
    if width is not None and v.shape[1] < width:
        v = jnp.pad(v, ((0, 0), (0, width - v.shape[1])))
    return v


LATE = ("w_out", "w_up", "w_ple_gate", "w_ple", "w_down")


def _device_grads(x, p, target, small, w_cat, conv_w, late, qc=None, tail=None, ln_in_out=None):
    z4 = jnp.zeros((4,), f32)
    bias_row = _row(jnp.concatenate([z4, small["dt_bias"].reshape(-1), small["b_f"].reshape(-1)]), LANES)
    alog_row = _row(jnp.concatenate([z4, small["a_log"].reshape(-1)]), LANES)
    g_gdn = _row(small["gdn_norm_g"])
    g_fox2 = _row(jnp.tile(small["fox_norm_g"].reshape(-1), 2))
    pb = p.astype(bf16)
    late = list(late)
    comm = qc is not None

    h0, h0b = ln_in_out if ln_in_out is not None else _ln_in(x, _row(small["ln_in_g"]), _row(small["ln_in_b"]))[0]
    proj = _mm(h0b, w_cat, "nt", 512, D_CAT, "mm_proj")
    gates, gcum, gcumt = _gates(proj, bias_row, alog_row)
    w_down_pieces = [(4, 0, 1)]

    def gather(phase, pieces):
        if not comm or not pieces:
            return None, lambda moved: None
        touched = sorted({i for i, _, _ in pieces})

        def took(moved):
            for i, buf in zip(touched, moved):
                late[i] = buf
        return phase([late[i] for i in touched], [(touched.index(i), k, n) for i, k, n in pieces]), took

    over, on = _gather_chips, _gather_pass_on
    cm, took = gather(over, [(0, 0, 1), (3, 0, 1)])
    (conv_c, qkv_n), moved = _gdn_conv(proj, conv_w, cm)
    took(moved)
    cm, took = gather(over, [(1, 0, 2)])
    (gu, gw, gqk, gtinv), moved = _gdn_local(qkv_n, gcum, cm)
    took(moved)
    cm, took = gather(over, [(1, 1, 2)])
    (o_gdn, sall), moved = _gdn_fwd(qkv_n, gcum, gu, gw, gqk, cm)
    took(moved)
    fox_plan = [(over, []), (on, [(0, 0, 1), (3, 0, 1), (1, 0, 2), (1, 1, 2)]), (over, [(2, 0, 1)]), (over, [(4, 0, 4)])]
    assert not comm or len(_fox_groups(x.shape[0])) == len(fox_plan)
    o_fox, lse = _fox_fwd(proj, gcum, gcumt, (lambda n: gather(*fox_plan[n])) if comm else None)
    cm, took = gather(on, [(2, 0, 1)])
    (attn,), moved = _attn_post(o_gdn, proj, o_fox, g_gdn, g_fox2, cm)
    took(moved)
    w_out = late[0].reshape(D_MODEL, D_MODEL)
    cm, took = gather(over, [(4, 1, 4)])
    (h1, h1b, xhat1, rstd1), moved = _ln1(h0, attn, w_out, _row(small["ln1_g"]), _row(small["ln1_b"]), cm)
    took(moved)
    w_up, w_ple = late[1], late[3]
    cm, took = gather(over, [(4, 1, 2)])
    up_act = _mm(h1b, w_up, "nn", 512, 1024, "mm_up", epi="relu2", shards=N_CHIPS, comm=cm)
    if cm:
        up_act, moved = up_act
        took(moved)
    up, act = up_act
    w_gate = late[2].reshape(D_MODEL, D_MODEL)
    cm, took = gather(on, w_down_pieces)
    gp = _mm(h1b, w_gate, "nn", 512, D_MODEL, "mm_gate", comm=cm)
    if cm:
        gp, moved = gp
        took(moved)
    w_down = late[4].reshape(D_FF, D_MODEL)
    dr2, dr2b, dpe, dgp, pg2 = _ln2_loss(h1, act, w_down, pb, w_ple, gp, _row(small["b_ple_gate"]), _row(small["ln2_g"]),
                                         _row(small["ln2_b"]), target)

    by_dest = lambda g: g.reshape((N_CHIPS, -1, g.shape[-1]))
    g_late = [None] * len(LATE)
    state = dict(from_sibling=[None] * len(LATE), sent=[None] * len(LATE), landing=[None] * len(LATE))
    nothing = (None, lambda moved: None)

    def to_sibling(idx):
        if not comm:
            return nothing

        def took(moved):
            for i, b1 in zip(idx, moved):
                state["from_sibling"][i] = b1
                state["sent"][i] = _add_pair(g_late[i], b1, qc, "add_pair_" + LATE[i])
                state["landing"][i] = _landing([state["sent"][i]])[0]
        return _exchange_pairs([g_late[i] for i in idx]), took

    def to_chips(pieces):
        if not comm:
            return nothing
        touched = sorted({i for i, _, _ in pieces})

        def took(moved):
            for i, b2 in zip(touched, moved):
                state["landing"][i] = b2
        return _exchange_chips([state["sent"][i] for i in touched], [state["landing"][i] for i in touched],
                               [(touched.index(i), k, n) for i, k, n in pieces]), took

    def ride(result, cm, took):
        if cm:
            result, moved = result
            took(moved)
        return result

    dup = _mm(dr2b, w_down, "nt", 512, 2048, "mm_dact", epi="relu2_bwd", extra=up)
    g_late[4] = by_dest(_mm(act, dr2b, "tn", 1024, D_MODEL, "mm_gdown"))
    cm, took = to_sibling([4])
    g_late[1] = ride(_mm(h1b, dup, "tn", 1024, 1024, "mm_gup", shards=N_CHIPS, comm=cm), cm, took)
    g_late[2] = by_dest(_mm(h1b, dgp, "tn", 1024, D_MODEL, "mm_ggate"))
    g_late[3] = _mm(pb, dpe, "tn", D_PLE, D_MODEL // N_CHIPS, "mm_gple", shards=N_CHIPS)
    cm, took = to_chips([(4, 0, 2)])
    (dr1, dr1b, pg1), moved = _ln1_bwd(dr2, dup, w_up, dgp, w_gate, xhat1, rstd1, _row(small["ln1_g"]), cm)
    took(moved)
    g_late[0] = by_dest(_mm(attn, dr1b, "tn", 1024, D_MODEL, "mm_gout"))
    do_gdn, dz, do_fox, pga = _attn_post_bwd(dr1b, w_out, o_gdn, proj, o_fox, g_gdn, g_fox2)
    chip_plan = [[(4, 1, 2), (1, 0, 2)], [(1, 1, 2)], [(0, 0, 1), (2, 0, 1)], [(3, 0, 1)]]

    def gdn_backward():
        cm, took = to_chips(chip_plan[0])
        state["gdn"], moved = _gdn_bwd(qkv_n, gcum, gu, gw, gqk, gtinv, sall, do_gdn, cm)
        took(moved)

    def fox_ride(n):
        if n == 0:
            return to_sibling([1, 0, 2, 3])
        if n == 1:
            gdn_backward()
        return to_chips(chip_plan[n])

    assert not comm or len(_fox_groups(x.shape[0])) == len(chip_plan)
    dfq, dfk, dfv, dccol, dct = _fox_bwd(proj, gcum, gcumt, o_fox, lse, do_fox, fox_ride if comm else None)
    if not comm:
        gdn_backward()
    dqkv_n, dgates = state["gdn"]
    dsmall, pgg = _gates_bwd(proj, bias_row, alog_row, gates, dgates, dccol, dct)
    cm = None
    if comm:
        cm = _share_halves([_add_chips(g, b1, b2, qc, "add_chips_" + n)
                            for g, b1, b2, n in zip(g_late, state["from_sibling"], state["landing"], LATE)])
    (du, g_conv8), reduced = _gdn_conv_bwd(proj, conv_w, conv_c, dqkv_n, cm)
    if comm:
        g_late = list(reduced)
    t = x.shape[0]
    dproj = jnp.concatenate([du, dz, dfq, dfk.astype(bf16), dfv.astype(bf16), dsmall, jnp.zeros((t, D_CAT - SEG_SMALL - LANES), bf16)], axis=1)
    g_cat = _mm(dproj, h0b, "tn", 1280, D_MODEL, "mm_gcat")
    cm, took = tail[0](g_cat) if tail else (None, None)
    dh0_mm = _mm(dproj, w_cat, "nn", 512, D_MODEL, "mm_dh0", comm=cm)
    if cm:
        dh0_mm, moved = dh0_mm
        took(moved)
    cm, took = tail[1]() if tail and tail[1] else (None, None)
    (grad_x, pg0), moved = _ln_in_bwd(x, dr1, dh0_mm, _row(small["ln_in_g"]), cm)
    if cm:
        took(moved)

    g_fox = pga[1, :FOX_DH] + pga[1, FOX_DH:]
    small_grads = dict(
        ln_in_g=pg0[0], ln_in_b=pg0[1], ln1_g=pg1[0], ln1_b=pg1[1], b_ple_gate=pg2[2], ln2_g=pg2[0], ln2_b=pg2[1],
        gdn_norm_g=pga[0], fox_norm_g=g_fox, a_log=pgg[1, 4:8], dt_bias=pgg[0, 4:8], b_f=pgg[0, 8:16], loss=pg2[3, 0:1])
    return grad_x, g_cat, g_conv8[:CONV_W], dict(zip(LATE, g_late)), small_grads


ANY = pl.BlockSpec(memory_space=pl.ANY)
CONV_PKT_ROWS = 16


def _mesh_pos():
    return lax.axis_index("x"), lax.axis_index("y"), lax.axis_index("c")


def _other_chips(x, y):
    return [(1 - x, y), (x, 1 - y), (1 - x, 1 - y)]


def _rcopy(src, dst, send_sem, recv_sem, dev):
    return pltpu.make_async_remote_copy(src_ref=src, dst_ref=dst, send_sem=send_sem, recv_sem=recv_sem,
                                        device_id=dev, device_id_type=MESH)


class _Comm:
    def __init__(self, ins, outs, aliases, n_sems, start, finish):
        self.ins, self.outs, self.aliases, self.n_sems, self.start, self.finish = list(ins), list(outs), dict(aliases), n_sems, start, finish


def _hosted(body, comm, *, name, grid, in_specs, out_specs, out_shape, args, scratch_shapes=(), aliases=None):
    n_in, n_out, n_sc = len(in_specs), len(out_specs), len(scratch_shapes)
    k, ko = (len(comm.ins), len(comm.outs)) if comm else (0, 0)

    def kernel_body(*refs):
        o0 = n_in + k
        s0 = o0 + n_out + ko
        if comm:
            cins, couts, (ssem, rsem) = refs[n_in:o0], refs[o0 + n_out:s0], refs[s0 + n_sc:]
            step = pl.program_id(0)
            for d in range(1, len(grid)):
                step = step * grid[d] + pl.program_id(d)

            @pl.when(step == 0)
            def _():
                comm.start(cins, couts, ssem, rsem)

        body(*refs[:n_in], *refs[o0:o0 + n_out], *refs[s0:s0 + n_sc])
        if comm:
            last = 1
            for n in grid:
                last *= n

            @pl.when(step == last - 1)
            def _():
                comm.finish(cins, couts, ssem, rsem)

    io_aliases = dict(aliases or {})
    scratch = list(scratch_shapes)
    if comm:
        io_aliases.update({n_in + i: n_out + j for i, j in comm.aliases.items()})
        scratch += [pltpu.SemaphoreType.DMA((comm.n_sems,)), pltpu.SemaphoreType.DMA((comm.n_sems,))]
    res = pl.pallas_call(
        kernel_body, name=name, grid=grid, in_specs=list(in_specs) + [ANY] * k, out_specs=tuple(out_specs) + (ANY,) * ko,
        out_shape=tuple(out_shape) + tuple(comm.outs if comm else ()), scratch_shapes=scratch, input_output_aliases=io_aliases,
        compiler_params=_params(("arbitrary",) * len(grid)),
    )(*args, *(comm.ins if comm else ()))
    return tuple(res[:n_out]), tuple(res[n_out:])


def _comm_only(phases, name):
    n_in = sum(len(p.ins) for p in phases)

    def body(*refs):
        n_out = sum(len(p.outs) for p in phases)
        sems = refs[n_in + n_out:]
        i0, o0 = 0, n_in
        for j, p in enumerate(phases):
            cins, couts = refs[i0:i0 + len(p.ins)], refs[o0:o0 + len(p.outs)]
            p.start(cins, couts, sems[2 * j], sems[2 * j + 1])
            p.finish(cins, couts, sems[2 * j], sems[2 * j + 1])
            i0 += len(p.ins)
            o0 += len(p.outs)

    aliases, i0, o0 = {}, 0, 0
    for p in phases:
        aliases.update({i0 + i: o0 + j for i, j in p.aliases.items()})
        i0 += len(p.ins)
        o0 += len(p.outs)
    outs = [o for p in phases for o in p.outs]
    res = pl.pallas_call(
        body, name=name, out_shape=tuple(outs), in_specs=[ANY] * n_in, out_specs=(ANY,) * len(outs), input_output_aliases=aliases,
        scratch_shapes=[pltpu.SemaphoreType.DMA((p.n_sems,)) for p in phases for _ in range(2)],
    )(*[a for p in phases for a in p.ins])
    split, o0 = [], 0
    for p in phases:
        split.append(tuple(res[o0:o0 + len(p.outs)]))
        o0 += len(p.outs)
    return split


def _like(arrays):
    return [jax.ShapeDtypeStruct(a.shape, a.dtype) for a in arrays]


def _half(ref, slot, hf, piece=(0, 1)):
    k, n = piece
    rows = ref.shape[1] // 2 // n
    return ref.at[slot, pl.ds((hf * n + k) * rows, rows)]


def _whole_halves(arrays):
    return [(i, 0, 1) for i in range(len(arrays))]


def _gather_chips(bufs, pieces=None, whole=False, base=0):
    nw = len(bufs)
    pieces = _whole_halves(bufs) if pieces is None else pieces
    part = (lambda ref, slot, c, piece: ref.at[slot]) if whole else _half

    def copies(couts):
        x, y, c = _mesh_pos()
        q = 2 * x + y
        for j, (i, k, n) in enumerate(pieces):
            for kc, chip in enumerate(_other_chips(x, y)):
                mine, theirs = part(couts[i], q, c, (k, n)), part(couts[i], 2 * chip[0] + chip[1], c, (k, n))
                yield base + j * 3 + kc, mine, theirs, (*chip, c)

    def start(cins, couts, ssem, rsem):
        for s, mine, _, dev in copies(couts):
            _rcopy(mine, mine, ssem.at[s], rsem.at[s], dev).start()

    def finish(cins, couts, ssem, rsem):
        for s, _, theirs, dev in copies(couts):
            _rcopy(theirs, theirs, ssem.at[s], rsem.at[s], dev).wait_recv()
        for s, mine, _, dev in copies(couts):
            _rcopy(mine, mine, ssem.at[s], rsem.at[s], dev).wait_send()

    return _Comm(bufs, _like(bufs), {i: i for i in range(nw)}, 3 * len(pieces), start, finish)


def _gather_pass_on(bufs, pieces=None, base=0):
    nw = len(bufs)
    pieces = _whole_halves(bufs) if pieces is None else pieces

    def copies(couts):
        x, y, c = _mesh_pos()
        for j, (i, k, n) in enumerate(pieces):
            for kc, chip in enumerate(_other_chips(x, y)):
                slot = 2 * chip[0] + chip[1]
                yield base + j * 3 + kc, _half(couts[i], slot, c, (k, n)), _half(couts[i], slot, 1 - c, (k, n)), (x, y, 1 - c)

    def start(cins, couts, ssem, rsem):
        for s, landed, _, sib in copies(couts):
            _rcopy(landed, landed, ssem.at[s], rsem.at[s], sib).start()

    def finish(cins, couts, ssem, rsem):
        for s, _, passed, sib in copies(couts):
            _rcopy(passed, passed, ssem.at[s], rsem.at[s], sib).wait_recv()
        for s, landed, _, sib in copies(couts):
            _rcopy(landed, landed, ssem.at[s], rsem.at[s], sib).wait_send()

    return _Comm(bufs, _like(bufs), {i: i for i in range(nw)}, 3 * len(pieces), start, finish)


def _gather_now(bufs, packets):
    nb = len(bufs)
    over, on, pk = _gather_chips(bufs), _gather_pass_on(bufs, base=3 * nb), _gather_chips(packets, whole=True, base=6 * nb)

    def start(cins, couts, ssem, rsem):
        over.start(cins[:nb], couts[:nb], ssem, rsem)
        pk.start(cins[nb:], couts[nb:], ssem, rsem)

    def finish(cins, couts, ssem, rsem):
        over.finish(cins[:nb], couts[:nb], ssem, rsem)
        on.start(cins[:nb], couts[:nb], ssem, rsem)
        on.finish(cins[:nb], couts[:nb], ssem, rsem)
        pk.finish(cins[nb:], couts[nb:], ssem, rsem)

    every = list(bufs) + list(packets)
    return _Comm(every, _like(every), {i: i for i in range(len(every))}, 6 * nb + 3 * len(packets), start, finish)


def _exchange_pairs(gs):
    nw = len(gs)

    def copies(cins, couts):
        x, y, c = _mesh_pos()
        for i in range(nw):
            for d in range(N_CHIPS):
                yield i * N_CHIPS + d, _half(cins[i], d, 1 - c), couts[i].at[d], (x, y, 1 - c)

    def start(cins, couts, ssem, rsem):
        for s, src, dst, sib in copies(cins, couts):
            _rcopy(src, dst, ssem.at[s], rsem.at[s], sib).start()

    def finish(cins, couts, ssem, rsem):
        for s, src, dst, sib in copies(cins, couts):
            _rcopy(src, dst, ssem.at[s], rsem.at[s], sib).wait_recv()
        for s, src, dst, sib in copies(cins, couts):
            _rcopy(src, dst, ssem.at[s], rsem.at[s], sib).wait_send()

    outs = [jax.ShapeDtypeStruct((N_CHIPS, g.shape[1] // 2, g.shape[2]), g.dtype) for g in gs]
    return _Comm(gs, outs, {}, N_CHIPS * nw, start, finish)


def _gather_packets(small):
    def peers():
        x, y, c = _mesh_pos()
        for r in range(1, 8):
            fx, fy, fc = (r >> 2) & 1, (r >> 1) & 1, r & 1
            yield r - 1, (1 - x if fx else x, 1 - y if fy else y, 1 - c if fc else c)

    def start(cins, couts, ssem, rsem):
        x, y, c = _mesh_pos()
        mine = couts[0].at[4 * x + 2 * y + c]
        for s, peer in peers():
            _rcopy(mine, mine, ssem.at[s], rsem.at[s], peer).start()

    def finish(cins, couts, ssem, rsem):
        x, y, c = _mesh_pos()
        mine = couts[0].at[4 * x + 2 * y + c]
        for s, peer in peers():
            theirs = couts[0].at[4 * peer[0] + 2 * peer[1] + peer[2]]
            _rcopy(theirs, theirs, ssem.at[s], rsem.at[s], peer).wait_recv()
        for s, peer in peers():
            _rcopy(mine, mine, ssem.at[s], rsem.at[s], peer).wait_send()

    return _Comm([small], _like([small]), {0: 0}, 7, start, finish)


def _exchange_chips(a4s, b2s, pieces=None):
    nw = len(a4s)
    pieces = _whole_halves(a4s) if pieces is None else pieces

    def copies(cins, couts):
        x, y, c = _mesh_pos()
        for j, (i, k, n) in enumerate(pieces):
            rows = a4s[i].shape[1] // n
            part = pl.ds(k * rows, rows)
            for kc, chip in enumerate(_other_chips(x, y)):
                yield j * 3 + kc, cins[i].at[2 * chip[0] + chip[1], part], couts[i].at[kc, part], (*chip, c)

    def start(cins, couts, ssem, rsem):
        for s, src, dst, dev in copies(cins, couts):
            _rcopy(src, dst, ssem.at[s], rsem.at[s], dev).start()

    def finish(cins, couts, ssem, rsem):
        for s, src, dst, dev in copies(cins, couts):
            _rcopy(src, dst, ssem.at[s], rsem.at[s], dev).wait_recv()
        for s, src, dst, dev in copies(cins, couts):
            _rcopy(src, dst, ssem.at[s], rsem.at[s], dev).wait_send()

    return _Comm(list(a4s) + list(b2s), _like(b2s), {nw + i: i for i in range(nw)}, 3 * len(pieces), start, finish)


def _landing(a4s):
    return [lax.empty((3,) + a.shape[1:], a.dtype) for a in a4s]


def _share_halves(rs):
    nw = len(rs)

    def halves(couts, i, hf):
        rows = rs[i].shape[0] // 2
        return couts[i].at[pl.ds(hf * rows, rows)]

    def start(cins, couts, ssem, rsem):
        x, y, c = _mesh_pos()
        for i in range(nw):
            _rcopy(halves(couts, i, c), halves(couts, i, c), ssem.at[i], rsem.at[i], (x, y, 1 - c)).start()

    def finish(cins, couts, ssem, rsem):
        x, y, c = _mesh_pos()
        for i in range(nw):
            _rcopy(halves(couts, i, 1 - c), halves(couts, i, 1 - c), ssem.at[i], rsem.at[i], (x, y, 1 - c)).wait_recv()
        for i in range(nw):
            _rcopy(halves(couts, i, c), halves(couts, i, c), ssem.at[i], rsem.at[i], (x, y, 1 - c)).wait_send()

    return _Comm(rs, _like(rs), {i: i for i in range(nw)}, nw, start, finish)


ADD_ROWS = 256


def _add_pair(g4, b1, qc_idx, name):
    _, half, cols = b1.shape
    rb = ADD_ROWS if half % ADD_ROWS == 0 else half
    nb = half // rb

    def body(qc_ref, g_ref, b_ref, ob_ref):
        ob_ref[...] = (g_ref[...] + b_ref[...]).astype(bf16)

    blk = (1, rb, cols)
    out = pl.BlockSpec(blk, lambda d, i, qc: (d, i, 0))
    return pl.pallas_call(
        body, name=name,
        grid_spec=pltpu.PrefetchScalarGridSpec(
            num_scalar_prefetch=1, grid=(N_CHIPS, nb),
            in_specs=[pl.BlockSpec(blk, lambda d, i, qc: (d, qc[1] * nb + i, 0)), out],
            out_specs=out),
        out_shape=jax.ShapeDtypeStruct(b1.shape, bf16),
        compiler_params=_params(("parallel", "parallel")),
    )(qc_idx, g4, b1)


def _add_chips(g4, b1, b2, qc_idx, name):
    _, half, cols = b1.shape
    rb = ADD_ROWS if half % ADD_ROWS == 0 else half
    nb = half // rb

    def body(qc_ref, g_ref, s_ref, b_ref, o_ref):
        o_ref[...] = (((g_ref[0] + s_ref[0]) + b_ref[0].astype(f32)) + b_ref[1].astype(f32)) + b_ref[2].astype(f32)

    return pl.pallas_call(
        body, name=name,
        grid_spec=pltpu.PrefetchScalarGridSpec(
            num_scalar_prefetch=1, grid=(nb,),
            in_specs=[pl.BlockSpec((1, rb, cols), lambda i, qc: (qc[0], qc[1] * nb + i, 0)),
                      pl.BlockSpec((1, rb, cols), lambda i, qc: (qc[0], i, 0)), pl.BlockSpec((3, rb, cols), lambda i, qc: (0, i, 0))],
            out_specs=pl.BlockSpec((rb, cols), lambda i, qc: (qc[1] * nb + i, 0))),
        out_shape=jax.ShapeDtypeStruct((2 * half, cols), f32),
        compiler_params=_params(("parallel",)),
    )(qc_idx, g4, b1, b2)


def _adamw_math(w, g, m, v):
    m = ADAM_B1 * m + (1.0 - ADAM_B1) * g
    v = ADAM_B2 * v + (1.0 - ADAM_B2) * (g * g)
    m_hat = m / (1.0 - ADAM_B1 ** ADAM_STEP)
    v_hat = v / (1.0 - ADAM_B2 ** ADAM_STEP)
    return -ADAM_LR * (m_hat / (jnp.sqrt(v_hat) + ADAM_EPS) + ADAM_WD * w), m, v


def _adamw(w, g, m, v, name, comm=None):
    rows = w.shape[0]
    if w.ndim == 3:
        rb = max(r for r in range(1, ADD_ROWS // 4 + 1) if rows % r == 0)
    else:
        rb = ADD_ROWS if rows % ADD_ROWS == 0 else rows

    def body(w_ref, g_ref, m_ref, v_ref, go_ref, d_ref, mo_ref, vo_ref):
        g = g_ref[...]
        go_ref[...] = g
        d_ref[...], mo_ref[...], vo_ref[...] = _adamw_math(w_ref[...], g, m_ref[...], v_ref[...])

    blk = pl.BlockSpec((rb,) + w.shape[1:], lambda i: (i,) + (0,) * (w.ndim - 1))
    return _hosted(body, comm, name=name, grid=(rows // rb,), in_specs=[blk] * 4, out_specs=(blk,) * 4,
                   out_shape=(jax.ShapeDtypeStruct(w.shape, f32),) * 4, args=(w, g, m, v))


def _small_sum_adamw(all_pkts, w, m, v):
    names = [n for n, _, _ in SMALL_LAYOUT if n in w]
    place = {n: (r0, size) for n, r0, size in SMALL_LAYOUT}
    rows_of = lambda size: -(-size // LANES)
    flat = lambda a: a.reshape(1, -1)
    k = len(names)

    def body(*refs):
        a_ref, ins = refs[0], refs[1:1 + 3 * k]
        g_ref, outs = refs[1 + 3 * k], refs[2 + 3 * k:2 + 7 * k]
        packs = refs[2 + 7 * k:]
        g = a_ref[0]
        for r in range(1, 8):
            g = g + a_ref[r]
        g_ref[...] = g
        for kind in range(3):
            packs[kind][...] = jnp.zeros_like(packs[kind])
            for j, n in enumerate(names):
                r0, size = place[n]
                for r in range(rows_of(size)):
                    width = min(LANES, size - r * LANES)
                    packs[kind][r0 + r:r0 + r + 1, 0:width] = ins[kind * k + j][:, r * LANES:r * LANES + width]
        results = (g,) + _adamw_math(packs[0][...], g, packs[1][...], packs[2][...])
        for kind, val in enumerate(results):
            for j, n in enumerate(names):
                r0, size = place[n]
                for r in range(rows_of(size)):
                    width = min(LANES, size - r * LANES)
                    outs[kind * k + j][:, r * LANES:r * LANES + width] = val[r0 + r:r0 + r + 1, 0:width]

    args = [all_pkts] + [flat(d[n]) for d in (w, m, v) for n in names]
    out_shape = [jax.ShapeDtypeStruct(all_pkts.shape[1:], f32)] + [jax.ShapeDtypeStruct((1, place[n][1]), f32) for _ in range(4) for n in names]
    res = pl.pallas_call(body, name="small_sum_adamw", out_shape=tuple(out_shape),
                         scratch_shapes=[pltpu.VMEM(all_pkts.shape[1:], f32)] * 3)(*args)
    by_kind = [{n: res[1 + kind * k + j].reshape(w[n].shape) for j, n in enumerate(names)} for kind in range(4)]
    return res[0], by_kind


SMALL_LAYOUT = (("ln_in_g", 0, 1024), ("ln_in_b", 8, 1024), ("ln1_g", 16, 1024), ("ln1_b", 24, 1024), ("b_ple_gate", 32, 1024),
                ("ln2_g", 40, 1024), ("ln2_b", 48, 1024), ("gdn_norm_g", 56, 128), ("fox_norm_g", 57, 64), ("a_log", 58, 4),
                ("dt_bias", 59, 4), ("b_f", 60, 8), ("loss", 61, 1))
SMALL_CONV_ROW = 64
SMALL_ROWS = 128


def _pack_small(vals, conv=None):
    rows = []
    nxt = 0
    for n, r0, size in SMALL_LAYOUT:
        assert r0 == nxt
        v = vals[n].reshape(-1).astype(f32) if n in vals else jnp.zeros((size,), f32)
        nrows = -(-size // LANES)
        rows.append(jnp.pad(v, (0, nrows * LANES - size)).reshape(nrows, LANES))
        nxt = r0 + nrows
    rows.append(jnp.zeros((SMALL_CONV_ROW - nxt, LANES), f32))
    conv_rows = CONV_W * GDN_QKV // LANES
    rows.append(jnp.zeros((conv_rows, LANES), f32) if conv is None else conv.reshape(conv_rows, LANES))
    rows.append(jnp.zeros((SMALL_ROWS - SMALL_CONV_ROW - conv_rows, LANES), f32))
    return jnp.concatenate(rows, axis=0)


def _unpack_small(pkt, shapes):
    out = {}
    for n, r0, size in SMALL_LAYOUT:
        if n in shapes:
            nrows = -(-size // LANES)
            out[n] = pkt[r0:r0 + nrows].reshape(-1)[:size].reshape(shapes[n])
    return out


WEIGHTS = ("ln_in_g", "ln_in_b", "w_in", "conv_w", "a_log", "dt_bias", "gdn_norm_g", "b_f", "fox_norm_g", "w_out", "ln1_g", "ln1_b",
           "w_up", "w_down", "w_ple", "w_ple_gate", "b_ple_gate", "ln2_g", "ln2_b")
SMALL_NAMES = tuple(n for n, _, _ in SMALL_LAYOUT if n != "loss")


def kernel(x, p, ln_in_g, ln_in_b, w_in, conv_w, a_log, dt_bias, gdn_norm_g, b_f, fox_norm_g, w_out, ln1_g, ln1_b, w_up, w_down, w_ple, w_ple_gate, b_ple_gate, ln2_g, ln2_b, loss_target, m_ln_in_g, m_ln_in_b, m_w_in, m_conv_w, m_a_log, m_dt_bias, m_gdn_norm_g, m_b_f, m_fox_norm_g, m_w_out, m_ln1_g, m_ln1_b, m_w_up, m_w_down, m_w_ple, m_w_ple_gate, m_b_ple_gate, m_ln2_g, m_ln2_b, v_ln_in_g, v_ln_in_b, v_w_in, v_conv_w, v_a_log, v_dt_bias, v_gdn_norm_g, v_b_f, v_fox_norm_g, v_w_out, v_ln1_g, v_ln1_b, v_w_up, v_w_down, v_w_ple, v_w_ple_gate, v_b_ple_gate, v_ln2_g, v_ln2_b):
    given = dict(locals())
    w = {n: given[n] for n in WEIGHTS}
    m = {n: given["m_" + n] for n in WEIGHTS}
    v = {n: given["v_" + n] for n in WEIGHTS}
    xi, yi, ci = _mesh_pos()
    q = 2 * xi + yi

    def slot_buffer(val, dtype, slots=N_CHIPS, slot=q, rows=None):
        rows = val.shape[0] if rows is None else rows
        return lax.dynamic_update_slice(lax.empty((slots, rows) + val.shape[1:], dtype), val.astype(dtype)[None], (slot, 0, 0))

    shard_cols = D_IN // N_CHIPS
    conv_rows = CONV_W * GDN_QKV // N_CHIPS // LANES
    conv_pkt = jnp.pad(w["conv_w"][0].reshape(-1, LANES), ((0, CONV_PKT_ROWS - conv_rows), (0, 0)))
    ln_in_out, (w_in4, conv_all) = _ln_in(x[0], _row(w["ln_in_g"]), _row(w["ln_in_b"]),
                                          _gather_now([slot_buffer(w["w_in"][0].T, bf16, rows=W_IN_ROWS)], [slot_buffer(conv_pkt, f32)]))
    conv_full = jnp.concatenate([conv_all[d, :conv_rows].reshape(CONV_W, GDN_QKV // N_CHIPS) for d in range(N_CHIPS)], axis=1)
    wi = jnp.concatenate([w_in4[d, :shard_cols] for d in range(N_CHIPS)], axis=0)
    w_cat = jnp.concatenate([wi[:OFF_BETA], wi[OFF_FOX:OFF_F], wi[OFF_BETA:OFF_FOX], wi[OFF_F:],
                             jnp.zeros((D_CAT - D_IN, D_MODEL), bf16)], axis=0)

    small = {n: w[n] for n in SMALL_NAMES}
    qc = jnp.stack([q, ci]).astype(jnp.int32)
    tail_state = {}

    def pairs_phase(gc):
        g_in = jnp.concatenate([gc[:OFF_BETA], gc[SEG_SMALL:SEG_SMALL + 8], gc[SEG_FOX:SEG_SMALL], gc[SEG_SMALL + 8:SEG_SMALL + 16]], axis=0)
        g_in4 = jnp.stack([jnp.pad(g_in[d * shard_cols:(d + 1) * shard_cols], ((0, W_IN_ROWS - shard_cols), (0, 0))) for d in range(N_CHIPS)])

        def took(moved):
            sent = _add_pair(g_in4, moved[0], qc, "add_pair_w_in")
            tail_state.update(g=g_in4, from_sibling=moved[0], sent=[sent], landing=_landing([sent]))
        return _exchange_pairs([g_in4]), took

    grad_x, _, g_conv, g_late, small_g = _device_grads(
        x[0], p[0, 0], loss_target[0], small, w_cat, conv_full, [slot_buffer(w[n][0], bf16) for n in LATE], qc, tail=(pairs_phase, None),
        ln_in_out=ln_in_out)
    packets = _gather_packets(slot_buffer(_pack_small(small_g, g_conv), f32, 8, 4 * xi + 2 * yi + ci))
    (b2,), (small_all,) = _comm_only([_exchange_chips(tail_state["sent"], tail_state["landing"]), packets], "exchange_chips_w_in")
    (g_late["w_in"],), = _comm_only(
        [_share_halves([_add_chips(tail_state["g"], tail_state["from_sibling"], b2, qc, "add_chips_w_in")])], "share_w_in")

    grads, delta, new_m, new_v = {}, {}, {}, {}
    for n, g in g_late.items():
        if n == "w_in":
            as_stored = lambda a: jnp.transpose(a, (2, 0, 1))
            outs, _ = _adamw(as_stored(w[n]), g[:shard_cols].reshape(shard_cols, 1, D_MODEL), as_stored(m[n]), as_stored(v[n]), "adamw_" + n)
            grads[n], delta[n], new_m[n], new_v[n] = (jnp.transpose(a, (1, 2, 0)) for a in outs)
        else:
            outs, _ = _adamw(w[n][0], g, m[n][0], v[n][0], "adamw_" + n)
            grads[n], delta[n], new_m[n], new_v[n] = (a.reshape(w[n].shape) for a in outs)
    pick = lambda d: {n: d[n] for n in SMALL_NAMES}
    g_pkt, by_kind = _small_sum_adamw(small_all, pick(w), pick(m), pick(v))
    for dst, vals in zip((grads, delta, new_m, new_v), by_kind):
        dst.update(vals)
    conv_rows_all = CONV_W * GDN_QKV // LANES
    conv_g_full = g_pkt[SMALL_CONV_ROW:SMALL_CONV_ROW + conv_rows_all].reshape(CONV_W, GDN_QKV)
    conv_g = lax.dynamic_slice_in_dim(conv_g_full, q * (GDN_QKV // N_CHIPS), GDN_QKV // N_CHIPS, axis=1)
    outs, _ = _adamw(w["conv_w"][0], conv_g, m["conv_w"][0], v["conv_w"][0], "adamw_conv_w")
    grads["conv_w"], delta["conv_w"], new_m["conv_w"], new_v["conv_w"] = (a.reshape(w["conv_w"].shape) for a in outs)
    loss = g_pkt[61, 0]
    return (loss, grad_x[None], *[grads[n] for n in WEIGHTS], *[delta[n] for n in WEIGHTS],
            *[new_m[n] for n in WEIGHTS], *[new_v[n] for n in WEIGHTS])
```

```python
import functools

import jax
import jax.numpy as jnp
from jax import lax
from jax.experimental import pallas as pl
from jax.experimental.pallas import tpu as pltpu

f32 = jnp.float32
bf16 = jnp.bfloat16
HI = lax.Precision.HIGHEST
MESH = pl.DeviceIdType.MESH

D_MODEL = 1024
CHUNK = 64
GDN_HEADS = 4
GDN_DK = 128
FOX_HEADS = 8
FOX_DH = 64
CONV_W = 4
D_FF = 4096
D_PLE = 256
LN_EPS = 1e-5
NORM_EPS = 1e-6
ALPHA = 2.0 ** 0.25
GDN_QKV = 1536
OFF_Z = 1536
OFF_BETA = 2048
OFF_FOX = 2056
OFF_F = 3592
D_IN = 3600
ADAM_LR = 0.001
ADAM_B1 = 0.9
ADAM_B2 = 0.999
ADAM_EPS = 1e-08
ADAM_WD = 0.01
ADAM_STEP = 10

SEG_FOX = 2048
SEG_SMALL = 3584
D_CAT = 3840
LANES = 128
TOK_BLK = 256
FOX_BQ = 256
VMEM_LIMIT = 56 * 1024 * 1024
NEG = -1e30

N_CHIPS = 4
W_IN_ROWS = 928


def _params(sem=None, **kw):
    return pltpu.CompilerParams(dimension_semantics=sem, vmem_limit_bytes=VMEM_LIMIT, **kw)


def _sigmoid(x):
    return 1.0 / (1.0 + jnp.exp(-x))


def _softplus(x):
    return jnp.maximum(x, 0.0) + jnp.log(1.0 + jnp.exp(-jnp.abs(x)))


def _ln_fwd(x, g, b):
    mu = jnp.mean(x, -1, keepdims=True)
    xc = x - mu
    var = jnp.mean(xc * xc, -1, keepdims=True)
    rstd = lax.rsqrt(var + LN_EPS)
    xhat = xc * rstd
    return xhat * g + b, xhat, rstd


def _ln_bwd(dy, xhat, rstd, g):
    dxh = dy * g
    m1 = jnp.mean(dxh, -1, keepdims=True)
    m2 = jnp.mean(dxh * xhat, -1, keepdims=True)
    return rstd * (dxh - m1 - xhat * m2)


def _dot(a, b, prec=HI):
    return jnp.dot(a, b, precision=prec, preferred_element_type=f32)


def _dot_nt(a, b, prec=HI):
    return lax.dot_general(a, b, (((1,), (1,)), ((), ())), precision=prec, preferred_element_type=f32)


def _dot_tn(a, b, prec=HI):
    return lax.dot_general(a, b, (((0,), (0,)), ((), ())), precision=prec, preferred_element_type=f32)


def _bdot(a, b):
    return _dot(a.astype(bf16), b.astype(bf16), None)


def _bdot_nt(a, b):
    return _dot_nt(a.astype(bf16), b.astype(bf16), None)


def _bdot_tn(a, b):
    return _dot_tn(a.astype(bf16), b.astype(bf16), None)


def _lane(shape):
    return lax.broadcasted_iota(jnp.int32, shape, len(shape) - 1)


def _mm(a, b, mode, tm, tn, name, out_dtype=f32, epi=None, extra=None, shards=1, comm=None):
    if mode == "nn":
        (m, k), n = a.shape, b.shape[-1] * shards
    elif mode == "nt":
        (m, k), n = a.shape, b.shape[-2]
    else:
        (k, m), n = a.shape, b.shape[1]
    assert m % tm == 0 and n % tn == 0, (name, m, n, tm, tn)
    per = (n // shards) // tn
    assert mode == "nt" or per * tn * shards == n, (name, n, tn, shards)
    nc = 512 if tn % 512 == 0 else (256 if tn % 256 == 0 else 128)
    ks = k // shards

    def body(a_ref, b_ref, *rest):
        for n0 in range(0, tn, nc):
            if mode == "nn":
                acc = jnp.dot(a_ref[...], b_ref[:, n0:n0 + nc], preferred_element_type=f32)
            elif mode == "nt" and shards > 1:
                acc = jnp.zeros((tm, nc), f32)
                for d in range(shards):
                    acc = acc + lax.dot_general(a_ref[:, d * ks:(d + 1) * ks], b_ref[d, n0:n0 + nc, :], (((1,), (1,)), ((), ())),
                                                preferred_element_type=f32)
            elif mode == "nt":
                acc = lax.dot_general(a_ref[...], b_ref[n0:n0 + nc, :], (((1,), (1,)), ((), ())), preferred_element_type=f32)
            else:
                acc = lax.dot_general(a_ref[...], b_ref[:, n0:n0 + nc], (((0,), (0,)), ((), ())), preferred_element_type=f32)
            if epi == "relu2":
                relu_ref, act_ref = rest
                r = jnp.maximum(acc, 0.0)
                relu_ref[:, n0:n0 + nc] = r.astype(bf16)
                act_ref[:, n0:n0 + nc] = (r * r).astype(bf16)
            elif epi == "relu2_bwd":
                relu_ref, o_ref = rest
                o_ref[:, n0:n0 + nc] = (acc * (2.0 * relu_ref[:, n0:n0 + nc].astype(f32))).astype(bf16)
            else:
                (o_ref,) = rest
                o_ref[:, n0:n0 + nc] = acc.astype(out_dtype)

    if mode == "tn":
        a_spec = pl.BlockSpec((k, tm), lambda j, i: (0, i))
    else:
        a_spec = pl.BlockSpec((tm, k), lambda j, i: (i, 0))
    if mode == "nt" and shards > 1:
        b_spec = pl.BlockSpec((shards, tn, ks), lambda j, i: (0, j, 0))
    elif mode == "nt":
        b_spec = pl.BlockSpec((tn, k), lambda j, i: (j, 0))
    elif mode == "nn" and shards > 1:
        b_spec = pl.BlockSpec((None, k, tn), lambda j, i: (j // per, 0, j % per))
    else:
        b_spec = pl.BlockSpec((k, tn), lambda j, i: (0, j))
    o_spec = pl.BlockSpec((tm, tn), lambda j, i: (i, j))
    in_specs = [a_spec, b_spec]
    args = [a, b]
    if epi == "relu2":
        out_shape = (jax.ShapeDtypeStruct((m, n), bf16), jax.ShapeDtypeStruct((m, n), bf16))
        out_specs = (o_spec, o_spec)
    elif epi == "relu2_bwd":
        in_specs.append(o_spec)
        args.append(extra)
        out_shape = jax.ShapeDtypeStruct((m, n), bf16)
        out_specs = o_spec
    elif mode == "tn" and shards > 1:
        out_shape = jax.ShapeDtypeStruct((shards, m, n // shards), out_dtype)
        out_specs = pl.BlockSpec((None, tm, tn), lambda j, i: (j // per, i, j % per))
    else:
        out_shape = jax.ShapeDtypeStruct((m, n), out_dtype)
        out_specs = o_spec
    single = not isinstance(out_shape, tuple)
    res, moved = _hosted(body, comm, name=name, grid=(n // tn, m // tm), in_specs=in_specs,
                         out_specs=(out_specs,) if single else out_specs, out_shape=(out_shape,) if single else out_shape, args=args)
    res = res[0] if single else res
    return res if comm is None else (res, moved)


def _row_spec(width, col=0):
    return pl.BlockSpec((TOK_BLK, width), lambda i: (i, col))


def _vec_spec(rows, width):
    return pl.BlockSpec((rows, width), lambda i: (0, 0))


def _ln_in(x, g, b, comm=None):
    t, d = x.shape

    def body(x_ref, g_ref, b_ref, h_ref, hb_ref):
        h, _, _ = _ln_fwd(x_ref[...], g_ref[...], b_ref[...])
        h_ref[...] = h
        hb_ref[...] = h.astype(bf16)

    return _hosted(
        body, comm, name="ln_in", grid=(t // TOK_BLK,),
        in_specs=[_row_spec(d), _vec_spec(1, d), _vec_spec(1, d)],
        out_specs=(_row_spec(d), _row_spec(d)),
        out_shape=(jax.ShapeDtypeStruct((t, d), f32), jax.ShapeDtypeStruct((t, d), bf16)),
        args=(x, g, b))


def _attn_post(o_gdn, proj, o_fox, g_gdn, g_fox2, comm=None):
    t = o_gdn.shape[0]

    def body(og_ref, z_ref, of_ref, gg_ref, gf_ref, out_ref):
        for h in range(GDN_HEADS):
            sl = slice(h * LANES, (h + 1) * LANES)
            og = og_ref[:, sl]
            z = z_ref[:, sl]
            r = lax.rsqrt(jnp.mean(og * og, -1, keepdims=True) + NORM_EPS)
            out_ref[:, sl] = (og * r * gg_ref[...] * (z * _sigmoid(z))).astype(bf16)
        lo = _lane((TOK_BLK, LANES)) < FOX_DH
        for pr in range(FOX_HEADS // 2):
            sl = slice(pr * LANES, (pr + 1) * LANES)
            of = of_ref[:, sl]
            sq = of * of
            s0 = jnp.sum(jnp.where(lo, sq, 0.0), -1, keepdims=True)
            s1 = jnp.sum(jnp.where(lo, 0.0, sq), -1, keepdims=True)
            r = lax.rsqrt(jnp.where(lo, s0, s1) * (1.0 / FOX_DH) + NORM_EPS)
            out_ref[:, 512 + pr * LANES:512 + (pr + 1) * LANES] = (of * r * gf_ref[...]).astype(bf16)

    return _hosted(
        body, comm, name="attn_post", grid=(t // TOK_BLK,),
        in_specs=[_row_spec(512), _row_spec(512, OFF_Z // 512), _row_spec(512), _vec_spec(1, LANES), _vec_spec(1, LANES)],
        out_specs=(_row_spec(D_MODEL),),
        out_shape=(jax.ShapeDtypeStruct((t, D_MODEL), bf16),),
        args=(o_gdn, proj, o_fox, g_gdn, g_fox2))


def _attn_post_bwd(dr1b, w_out, o_gdn, proj, o_fox, g_gdn, g_fox2, comm=None):
    t = o_gdn.shape[0]

    def body(dr_ref, wo_ref, og_ref, z_ref, of_ref, gg_ref, gf_ref, dog_ref, dz_ref, dof_ref, pg_ref):
        i = pl.program_id(0)

        @pl.when(i == 0)
        def _():
            pg_ref[...] = jnp.zeros_like(pg_ref)

        da = _dot_nt(dr_ref[...], wo_ref[...], None)
        dgg = jnp.zeros((1, LANES), f32)
        for h in range(GDN_HEADS):
            sl = slice(h * LANES, (h + 1) * LANES)
            og = og_ref[:, sl]
            z = z_ref[:, sl]
            dout = da[:, sl]
            g = gg_ref[...]
            r = lax.rsqrt(jnp.mean(og * og, -1, keepdims=True) + NORM_EPS)
            sg = _sigmoid(z)
            silu = z * sg
            ng = og * r * g
            dng = dout * silu
            dz_ref[:, sl] = (dout * ng * (sg * (1.0 + z * (1.0 - sg)))).astype(bf16)
            dgg = dgg + jnp.sum(dng * og * r, 0, keepdims=True)
            gd = dng * g
            dog_ref[:, sl] = r * gd - og * (r * r * r) * jnp.mean(og * gd, -1, keepdims=True)
        pg_ref[0:1, :] += dgg
        lo = _lane((TOK_BLK, LANES)) < FOX_DH
        dgf = jnp.zeros((1, LANES), f32)
        for pr in range(FOX_HEADS // 2):
            sl = slice(pr * LANES, (pr + 1) * LANES)
            of = of_ref[:, sl]
            dout = da[:, 512 + pr * LANES:512 + (pr + 1) * LANES]
            g = gf_ref[...]
            sq = of * of
            s0 = jnp.sum(jnp.where(lo, sq, 0.0), -1, keepdims=True)
            s1 = jnp.sum(jnp.where(lo, 0.0, sq), -1, keepdims=True)
            r = lax.rsqrt(jnp.where(lo, s0, s1) * (1.0 / FOX_DH) + NORM_EPS)
            dgf = dgf + jnp.sum(dout * of * r, 0, keepdims=True)
            gd = dout * g
            xg = of * gd
            m0 = jnp.sum(jnp.where(lo, xg, 0.0), -1, keepdims=True)
            m1 = jnp.sum(jnp.where(lo, 0.0, xg), -1, keepdims=True)
            dof_ref[:, sl] = r * gd - of * (r * r * r) * (jnp.where(lo, m0, m1) * (1.0 / FOX_DH))
        pg_ref[1:2, :] += dgf

    return _hosted(
        body, comm, name="attn_post_bwd", grid=(t // TOK_BLK,),
        in_specs=_product_specs(dr1b, w_out) + [_row_spec(512), _row_spec(512, OFF_Z // 512), _row_spec(512), _vec_spec(1, LANES), _vec_spec(1, LANES)],
        out_specs=(_row_spec(512), _row_spec(512), _row_spec(512), _vec_spec(8, LANES)),
        out_shape=(jax.ShapeDtypeStruct((t, 512), f32), jax.ShapeDtypeStruct((t, 512), bf16),
                   jax.ShapeDtypeStruct((t, 512), f32), jax.ShapeDtypeStruct((8, LANES), f32)),
        args=(dr1b, w_out, o_gdn, proj, o_fox, g_gdn, g_fox2))


def _product_specs(lhs, rhs):
    return [_row_spec(lhs.shape[1]), pl.BlockSpec(rhs.shape, lambda i: (0, 0))]


def _ln1(h0, lhs, rhs, g, b, comm=None):
    t, d = h0.shape

    def body(h0_ref, lhs_ref, rhs_ref, g_ref, b_ref, h_ref, hb_ref, xh_ref, rs_ref):
        mix = jnp.dot(lhs_ref[...], rhs_ref[...], preferred_element_type=f32)
        h, xhat, rstd = _ln_fwd(ALPHA * h0_ref[...] + mix, g_ref[...], b_ref[...])
        h_ref[...] = h
        hb_ref[...] = h.astype(bf16)
        xh_ref[...] = xhat
        rs_ref[...] = jnp.broadcast_to(rstd, rs_ref.shape)

    return _hosted(
        body, comm, name="ln1", grid=(t // TOK_BLK,),
        in_specs=[_row_spec(d)] + _product_specs(lhs, rhs) + [_vec_spec(1, d), _vec_spec(1, d)],
        out_specs=(_row_spec(d), _row_spec(d), _row_spec(d), _row_spec(LANES)),
        out_shape=(jax.ShapeDtypeStruct((t, d), f32), jax.ShapeDtypeStruct((t, d), bf16),
                   jax.ShapeDtypeStruct((t, d), f32), jax.ShapeDtypeStruct((t, LANES), f32)),
        args=(h0, lhs, rhs, g, b))


def _ln2_loss(h1, lhs, rhs, pb, w_ple, gp, b_gate, g, b, target):
    t, d = h1.shape

    def body(h1_ref, lhs_ref, rhs_ref, pb_ref, wp_ref, gp_ref, bg_ref, g_ref, b_ref, t_ref, dr_ref, drb_ref, dpe_ref, dgp_ref, pg_ref):
        i = pl.program_id(0)

        @pl.when(i == 0)
        def _():
            pg_ref[...] = jnp.zeros_like(pg_ref)

        ff = jnp.dot(lhs_ref[...], rhs_ref[...], preferred_element_type=f32)
        sig = _sigmoid(gp_ref[...] + bg_ref[...])
        pe = jnp.concatenate([jnp.dot(pb_ref[...], wp_ref[s], preferred_element_type=f32) for s in range(w_ple.shape[0])], axis=1)
        r2 = ALPHA * h1_ref[...] + ff + pe * sig
        y, xhat, rstd = _ln_fwd(r2, g_ref[...], b_ref[...])
        err = y - t_ref[...]
        dy = err * (1.0 / d)
        dr = _ln_bwd(dy, xhat, rstd, g_ref[...])
        dr_ref[...] = dr
        drb_ref[...] = dr.astype(bf16)
        dpe_ref[...] = (dr * sig).astype(bf16)
        dgp = dr * pe * sig * (1.0 - sig)
        dgp_ref[...] = dgp.astype(bf16)
        pg_ref[0:1, :] += jnp.sum(dy * xhat, 0, keepdims=True)
        pg_ref[1:2, :] += jnp.sum(dy, 0, keepdims=True)
        pg_ref[2:3, :] += jnp.sum(dgp, 0, keepdims=True)
        pg_ref[3:4, :] += 0.5 * jnp.sum(jnp.mean(err * err, -1, keepdims=True), 0, keepdims=True)

    return pl.pallas_call(
        body, name="ln2_loss", grid=(t // TOK_BLK,),
        in_specs=[_row_spec(d)] + _product_specs(lhs, rhs) + [_row_spec(pb.shape[1]), pl.BlockSpec(w_ple.shape, lambda i: (0, 0, 0)), _row_spec(d)]
        + [_vec_spec(1, d)] * 3 + [_row_spec(d)],
        out_specs=(_row_spec(d), _row_spec(d), _row_spec(d), _row_spec(d), _vec_spec(8, d)),
        out_shape=(jax.ShapeDtypeStruct((t, d), f32), jax.ShapeDtypeStruct((t, d), bf16), jax.ShapeDtypeStruct((t, d), bf16),
                   jax.ShapeDtypeStruct((t, d), bf16), jax.ShapeDtypeStruct((8, d), f32)),
        compiler_params=_params(("arbitrary",)),
    )(h1, lhs, rhs, pb, w_ple, gp, b_gate, g, b, target)


def _ln1_bwd(dr2, dup, w_up, dgp, w_gate, xhat, rstd, g, comm=None):
    t, d = dr2.shape
    ks = w_up.shape[2]

    def body(dr2_ref, dup_ref, wup_ref, dgp_ref, wg_ref, xh_ref, rs_ref, g_ref, dr_ref, drb_ref, pg_ref):
        i = pl.program_id(0)

        @pl.when(i == 0)
        def _():
            pg_ref[...] = jnp.zeros_like(pg_ref)

        dh = ALPHA * dr2_ref[...] + _dot_nt(dgp_ref[...], wg_ref[...], None)
        for s in range(w_up.shape[0]):
            dh = dh + _dot_nt(dup_ref[:, s * ks:(s + 1) * ks], wup_ref[s], None)
        xhat = xh_ref[...]
        dr = _ln_bwd(dh, xhat, rs_ref[:, 0:1], g_ref[...])
        dr_ref[...] = dr
        drb_ref[...] = dr.astype(bf16)
        pg_ref[0:1, :] += jnp.sum(dh * xhat, 0, keepdims=True)
        pg_ref[1:2, :] += jnp.sum(dh, 0, keepdims=True)

    return _hosted(
        body, comm, name="ln1_bwd", grid=(t // TOK_BLK,),
        in_specs=[_row_spec(d), _row_spec(dup.shape[1]), pl.BlockSpec(w_up.shape, lambda i: (0, 0, 0))] + _product_specs(dgp, w_gate)
        + [_row_spec(d), _row_spec(LANES), _vec_spec(1, d)],
        out_specs=(_row_spec(d), _row_spec(d), _vec_spec(8, d)),
        out_shape=(jax.ShapeDtypeStruct((t, d), f32), jax.ShapeDtypeStruct((t, d), bf16), jax.ShapeDtypeStruct((8, d), f32)),
        args=(dr2, dup, w_up, dgp, w_gate, xhat, rstd, g))


def _ln_in_bwd(x, dr1, dmm, g, comm=None):
    t, d = x.shape

    def body(x_ref, dr1_ref, dmm_ref, g_ref, dx_ref, pg_ref):
        i = pl.program_id(0)

        @pl.when(i == 0)
        def _():
            pg_ref[...] = jnp.zeros_like(pg_ref)

        dh = ALPHA * dr1_ref[...] + dmm_ref[...]
        _, xhat, rstd = _ln_fwd(x_ref[...], g_ref[...], 0.0)
        dx_ref[...] = _ln_bwd(dh, xhat, rstd, g_ref[...])
        pg_ref[0:1, :] += jnp.sum(dh * xhat, 0, keepdims=True)
        pg_ref[1:2, :] += jnp.sum(dh, 0, keepdims=True)

    return _hosted(
        body, comm, name="ln_in_bwd", grid=(t // TOK_BLK,),
        in_specs=[_row_spec(d)] * 3 + [_vec_spec(1, d)],
        out_specs=(_row_spec(d), _vec_spec(8, d)),
        out_shape=(jax.ShapeDtypeStruct((t, d), f32), jax.ShapeDtypeStruct((8, d), f32)),
        args=(x, dr1, dmm, g))


def _tri(n, upper=False, strict=False):
    r = lax.broadcasted_iota(jnp.int32, (n, n), 0)
    c = lax.broadcasted_iota(jnp.int32, (n, n), 1)
    if upper:
        m = (c > r) if strict else (c >= r)
    else:
        m = (c < r) if strict else (c <= r)
    return jnp.where(m, 1.0, 0.0).astype(f32)


def _gate_values(x, bias, alog, lane):
    z = x + bias
    return jnp.where(lane < 4, _sigmoid(z), jnp.where(lane < 8, -jnp.exp(alog) * _softplus(z), jnp.where(lane < 16, -_softplus(-z), 0.0)))


def _gates(proj, bias_row, alog_row):
    t = proj.shape[0]
    nch = t // CHUNK

    def body(x_ref, bias_ref, alog_ref, gates_ref, gcum_ref, gcumt_ref):
        lane = _lane((t, LANES))
        gates = _gate_values(x_ref[...], bias_ref[...], alog_ref[...], lane)
        gates_ref[...] = gates
        g3 = gates.reshape(nch, CHUNK, LANES)
        tri = jnp.broadcast_to(_tri(CHUNK)[None], (nch, CHUNK, CHUNK))
        loc = jnp.einsum("bij,bjk->bik", tri, g3, precision=HI, preferred_element_type=f32)
        tot = jnp.sum(g3, axis=1)
        offs = _dot(_tri(nch, strict=True), tot)
        glob = loc + offs[:, None, :]
        lane3 = _lane((nch, CHUNK, LANES))
        gcum = jnp.where(lane3 < 4, g3, jnp.where(lane3 < 8, loc, glob)).reshape(t, LANES)
        gcum_ref[...] = gcum
        gcumt_ref[...] = gcum.T

    return pl.pallas_call(
        body, name="gates", grid=(1,),
        in_specs=[pl.BlockSpec((t, LANES), lambda i: (0, SEG_SMALL // LANES)), _vec_spec(1, LANES), _vec_spec(1, LANES)],
        out_specs=(pl.BlockSpec((t, LANES), lambda i: (0, 0)), pl.BlockSpec((t, LANES), lambda i: (0, 0)),
                   pl.BlockSpec((LANES, t), lambda i: (0, 0))),
        out_shape=(jax.ShapeDtypeStruct((t, LANES), f32), jax.ShapeDtypeStruct((t, LANES), f32), jax.ShapeDtypeStruct((LANES, t), f32)),
        compiler_params=_params(("arbitrary",)),
    )(proj, bias_row, alog_row)


def _gates_bwd(proj, bias_row, alog_row, gates, dgates, dccol, dct):
    t = proj.shape[0]
    nch = t // CHUNK

    def body(x_ref, bias_ref, alog_ref, gates_ref, dg_ref, dcc_ref, dct_ref, dx_ref, pg_ref):
        lane = _lane((t, LANES))
        d = dg_ref[...] + dcc_ref[...] + dct_ref[...].T
        d3 = d.reshape(nch, CHUNK, LANES)
        tri = jnp.broadcast_to(_tri(CHUNK, upper=True)[None], (nch, CHUNK, CHUNK))
        loc = jnp.einsum("bij,bjk->bik", tri, d3, precision=HI, preferred_element_type=f32)
        tot = jnp.sum(d3, axis=1)
        offs = _dot(_tri(nch, upper=True, strict=True), tot)
        glob = loc + offs[:, None, :]
        lane3 = _lane((nch, CHUNK, LANES))
        dpre = jnp.where(lane3 < 4, d3, jnp.where(lane3 < 8, loc, glob)).reshape(t, LANES)
        z = x_ref[...] + bias_ref[...]
        sg = _sigmoid(z)
        dx = jnp.where(lane < 4, dpre * sg * (1.0 - sg),
                       jnp.where(lane < 8, dpre * (-jnp.exp(alog_ref[...])) * sg, jnp.where(lane < 16, dpre * (1.0 - sg), 0.0)))
        dx_ref[...] = dx.astype(bf16)
        pg_ref[...] = jnp.zeros_like(pg_ref)
        pg_ref[0:1, :] = jnp.sum(dx, 0, keepdims=True)
        pg_ref[1:2, :] = jnp.sum(jnp.where((lane >= 4) & (lane < 8), dpre * gates_ref[...], 0.0), 0, keepdims=True)

    full = pl.BlockSpec((t, LANES), lambda i: (0, 0))
    return pl.pallas_call(
        body, name="gates_bwd", grid=(1,),
        in_specs=[pl.BlockSpec((t, LANES), lambda i: (0, SEG_SMALL // LANES)), _vec_spec(1, LANES), _vec_spec(1, LANES),
                  full, full, full, pl.BlockSpec((LANES, t), lambda i: (0, 0))],
        out_specs=(full, _vec_spec(8, LANES)),
        out_shape=(jax.ShapeDtypeStruct((t, LANES), bf16), jax.ShapeDtypeStruct((8, LANES), f32)),
        compiler_params=_params(("arbitrary",)),
    )(proj, bias_row, alog_row, gates, dgates, dccol, dct)


def _conv_act(u, cw, row, t):
    c = cw[3:4, :] * u
    for jj in range(CONV_W - 1):
        sh = CONV_W - 1 - jj
        c = c + cw[jj:jj + 1, :] * jnp.where(row >= sh, pltpu.roll(u, sh, axis=0), 0.0)
    return c


def _gdn_conv(proj, conv_w, comm=None):
    t = proj.shape[0]
    nblk = GDN_QKV // LANES

    def body(u_ref, cw_ref, c_ref, y_ref):
        j = pl.program_id(0)
        row = lax.broadcasted_iota(jnp.int32, (t, LANES), 0)
        c = _conv_act(u_ref[...], cw_ref[...], row, t)
        c_ref[...] = c
        s = c * _sigmoid(c)
        r = lax.rsqrt(jnp.sum(s * s, -1, keepdims=True) + NORM_EPS)
        scale = jnp.where(j < GDN_HEADS, GDN_DK ** -0.5, 1.0)
        y_ref[...] = jnp.where(j < 2 * GDN_HEADS, s * (r * scale), s)

    blk = pl.BlockSpec((t, LANES), lambda j: (0, j))
    return _hosted(
        body, comm, name="gdn_conv", grid=(nblk,),
        in_specs=[blk, pl.BlockSpec((CONV_W, LANES), lambda j: (0, j))],
        out_specs=(blk, blk),
        out_shape=(jax.ShapeDtypeStruct((t, GDN_QKV), f32), jax.ShapeDtypeStruct((t, GDN_QKV), f32)),
        args=(proj, conv_w))


def _gdn_conv_bwd(proj, conv_w, c, dy, comm=None):
    t = proj.shape[0]
    nblk = GDN_QKV // LANES

    def body(u_ref, cw_ref, c_ref, dy_ref, du_ref, dcw_ref):
        j = pl.program_id(0)
        row = lax.broadcasted_iota(jnp.int32, (t, LANES), 0)
        u = u_ref[...]
        cw = cw_ref[...]
        c = c_ref[...]
        dy = dy_ref[...]
        sg = _sigmoid(c)
        s = c * sg
        r = lax.rsqrt(jnp.sum(s * s, -1, keepdims=True) + NORM_EPS)
        n = s * r
        scale = jnp.where(j < GDN_HEADS, GDN_DK ** -0.5, 1.0)
        dn = dy * scale
        ds = jnp.where(j < 2 * GDN_HEADS, r * (dn - n * jnp.sum(dn * n, -1, keepdims=True)), dy)
        dc = ds * (sg * (1.0 + c * (1.0 - sg)))
        du = cw[3:4, :] * dc
        dcw_ref[...] = jnp.zeros_like(dcw_ref)
        dcw_ref[3:4, :] = jnp.sum(dc * u, 0, keepdims=True)
        for jj in range(CONV_W - 1):
            sh = CONV_W - 1 - jj
            du = du + cw[jj:jj + 1, :] * jnp.where(row < t - sh, pltpu.roll(dc, t - sh, axis=0), 0.0)
            dcw_ref[jj:jj + 1, :] = jnp.sum(dc * jnp.where(row >= sh, pltpu.roll(u, sh, axis=0), 0.0), 0, keepdims=True)
        du_ref[...] = du.astype(bf16)

    blk = pl.BlockSpec((t, LANES), lambda j: (0, j))
    return _hosted(
        body, comm, name="gdn_conv_bwd", grid=(nblk,),
        in_specs=[blk, pl.BlockSpec((CONV_W, LANES), lambda j: (0, j)), blk, blk],
        out_specs=(blk, pl.BlockSpec((8, LANES), lambda j: (0, j))),
        out_shape=(jax.ShapeDtypeStruct((t, GDN_QKV), bf16), jax.ShapeDtypeStruct((8, GDN_QKV), f32)),
        args=(proj, conv_w, c, dy))


def _chunk_masks():
    r = lax.broadcasted_iota(jnp.int32, (CHUNK, CHUNK), 0)
    c = lax.broadcasted_iota(jnp.int32, (CHUNK, CHUNK), 1)
    return r >= c, r > c, r == c


def _col_to_row(col, eye):
    return jnp.sum(jnp.where(eye, col, 0.0), axis=0, keepdims=True)


def _row_to_col(row, eye):
    return jnp.sum(jnp.where(eye, row, 0.0), axis=1, keepdims=True)


NN = (((1,), (0,)), ((), ()))
NT = (((1,), (1,)), ((), ()))
TN = (((0,), (0,)), ((), ()))
GDN_GROUP = 4


def _mx(a, b, dims=NN, passes=1):
    d = lambda p, q: lax.dot_general(p, q, dims, preferred_element_type=f32)
    ah, bh = a.astype(bf16), b.astype(bf16)
    if passes == 1:
        return d(ah, bh)
    al = (a - ah.astype(f32)).astype(bf16)
    bl = (b - bh.astype(f32)).astype(bf16)
    return d(ah, bh) + (d(ah, bl) + d(al, bh))


def _gdn_decay(gam, masks):
    causal, _, eye = masks
    return jnp.exp(jnp.where(causal, gam - _col_to_row(gam, eye), NEG))


def _gdn_local(y, gcum, comm=None):
    t = y.shape[0]
    nch = t // CHUNK
    rows_blk = GDN_GROUP * CHUNK

    def body(y_ref, g_ref, u_ref, w_ref, qk_ref, tinv_ref):
        masks = _chunk_masks()
        _, strict, eye = masks
        ids = [(j, h) for j in range(GDN_GROUP) for h in range(GDN_HEADS)]
        rs = lambda j: slice(j * CHUNK, (j + 1) * CHUNK)
        col = lambda base, h: slice(base + h * LANES, base + (h + 1) * LANES)
        kn = [y_ref[rs(j), col(512, h)] for j, h in ids]
        beta = [g_ref[rs(j), h:h + 1] for j, h in ids]
        gam = [g_ref[rs(j), 4 + h:5 + h] for j, h in ids]
        dec = [_gdn_decay(g, masks) for g in gam]
        x = [-jnp.where(strict, _mx(k, k, NT) * d * b, 0.0) for k, d, b in zip(kn, dec, beta)]
        tinv = [jnp.where(eye, 1.0, 0.0) + a for a in x]
        for _ in range(5):
            x = [_mx(a, a, NN, 3) for a in x]
            tinv = [t_ + _mx(t_, a, NN, 3) for t_, a in zip(tinv, x)]
        for (j, h), t_, k, d, b, g in zip(ids, tinv, kn, dec, beta, gam):
            u_ref[rs(j), col(0, h)] = _mx(t_, b * y_ref[rs(j), col(1024, h)])
            w_ref[rs(j), col(0, h)] = _mx(t_, (b * jnp.exp(g)) * k)
            qk_ref[j, h] = _mx(y_ref[rs(j), col(0, h)], k, NT) * d
            tinv_ref[j, h] = t_

    mat = pl.BlockSpec((GDN_GROUP, GDN_HEADS, CHUNK, CHUNK), lambda n: (n, 0, 0, 0))
    return _hosted(
        body, comm, name="gdn_local", grid=(nch // GDN_GROUP,),
        in_specs=[pl.BlockSpec((rows_blk, GDN_QKV), lambda n: (n, 0)), pl.BlockSpec((rows_blk, LANES), lambda n: (n, 0))],
        out_specs=(pl.BlockSpec((rows_blk, 512), lambda n: (n, 0)), pl.BlockSpec((rows_blk, 512), lambda n: (n, 0)), mat, mat),
        out_shape=(jax.ShapeDtypeStruct((t, 512), f32), jax.ShapeDtypeStruct((t, 512), f32),
                   jax.ShapeDtypeStruct((nch, GDN_HEADS, CHUNK, CHUNK), f32), jax.ShapeDtypeStruct((nch, GDN_HEADS, CHUNK, CHUNK), f32)),
        args=(y, gcum))


def _gdn_fwd(y, gcum, u, w, qk, comm=None):
    t = y.shape[0]
    nch = t // CHUNK

    def body(y_ref, g_ref, u_ref, w_ref, qk_ref, o_ref, sall_ref, s_ref):
        @pl.when(pl.program_id(0) == 0)
        def _():
            s_ref[...] = jnp.zeros_like(s_ref)

        heads = range(GDN_HEADS)
        sl = [slice(h * LANES, (h + 1) * LANES) for h in heads]
        gam = [g_ref[:, 4 + h:5 + h] for h in heads]
        gam_last = [g[CHUNK - 1:CHUNK, :] for g in gam]
        s = [s_ref[h] for h in heads]
        for h in heads:
            sall_ref[0, h] = s[h]
        ws = [_mx(w_ref[:, sl[h]], s[h]) for h in heads]
        qs = [_mx(y_ref[:, sl[h]] * jnp.exp(gam[h]), s[h]) for h in heads]
        vn = [u_ref[:, sl[h]] - ws[h] for h in heads]
        av = [_mx(qk_ref[0, h], vn[h]) for h in heads]
        kv = [_mx(y_ref[:, 512 + h * LANES:512 + (h + 1) * LANES] * jnp.exp(gam_last[h] - gam[h]), vn[h], TN) for h in heads]
        for h in heads:
            o_ref[:, sl[h]] = qs[h] + av[h]
            s_ref[h] = jnp.exp(gam_last[h]) * s[h] + kv[h]

    row = lambda width: pl.BlockSpec((CHUNK, width), lambda n: (n, 0))
    return _hosted(
        body, comm, name="gdn_fwd", grid=(nch,),
        in_specs=[row(GDN_QKV), row(LANES), row(512), row(512), pl.BlockSpec((1, GDN_HEADS, CHUNK, CHUNK), lambda n: (n, 0, 0, 0))],
        out_specs=(row(512), pl.BlockSpec((1, GDN_HEADS, LANES, LANES), lambda n: (n, 0, 0, 0))),
        out_shape=(jax.ShapeDtypeStruct((t, 512), f32), jax.ShapeDtypeStruct((nch, GDN_HEADS, LANES, LANES), f32)),
        scratch_shapes=[pltpu.VMEM((GDN_HEADS, LANES, LANES), f32)],
        args=(y, gcum, u, w, qk))


def _gdn_bwd(y, gcum, u_all, w_all, qk_all, tinv_all, sall, do, comm=None):
    t = y.shape[0]
    nch = t // CHUNK

    def body(y_ref, g_ref, u_ref, w_ref, qk_ref, tinv_ref, sall_ref, do_ref, dy_ref, dg_ref, ds_ref):
        @pl.when(pl.program_id(0) == 0)
        def _():
            ds_ref[...] = jnp.zeros_like(ds_ref)

        masks = _chunk_masks()
        causal, strict, eye = masks
        lane = _lane((CHUNK, LANES))
        row = lax.broadcasted_iota(jnp.int32, (CHUNK, 1), 0)
        heads = range(GDN_HEADS)
        each = lambda f, *ls: [f(*a) for a in zip(*ls)]
        rsum = lambda a: jnp.sum(a, axis=1, keepdims=True)
        sl = [slice(h * LANES, (h + 1) * LANES) for h in heads]
        qn = [y_ref[:, sl[h]] for h in heads]
        kn = [y_ref[:, 512 + h * LANES:512 + (h + 1) * LANES] for h in heads]
        v = [y_ref[:, 1024 + h * LANES:1024 + (h + 1) * LANES] for h in heads]
        beta = [g_ref[:, h:h + 1] for h in heads]
        gam = [g_ref[:, 4 + h:5 + h] for h in heads]
        gam_last = [g[CHUNK - 1:CHUNK, :] for g in gam]
        dec = [_gdn_decay(g, masks) for g in gam]
        e = [jnp.exp(g) for g in gam]
        f = each(lambda gl_, g: jnp.exp(gl_ - g), gam_last, gam)
        gl = [jnp.exp(g) for g in gam_last]
        u = [u_ref[:, sl[h]] for h in heads]
        w = [w_ref[:, sl[h]] for h in heads]
        qk = [qk_ref[0, h] for h in heads]
        tinv = [tinv_ref[0, h] for h in heads]
        s = [sall_ref[0, h] for h in heads]
        dsn = [ds_ref[h] for h in heads]
        d_o = [do_ref[:, sl[h]] for h in heads]
        qd = each(lambda a, b: a * b, qn, e)
        kd = each(lambda a, b: a * b, kn, f)
        ws = each(_mx, w, s)
        kds = each(_mx, kd, dsn)
        qkdo = each(lambda a, b: _mx(a, b, TN), qk, d_o)
        dqd = each(lambda a, b: _mx(a, b, NT), d_o, s)
        qddo = each(lambda a, b: _mx(a, b, TN), qd, d_o)
        kkd = each(lambda k, d: _mx(k, k, NT) * d, kn, dec)
        vn = each(lambda a, b: a - b, u, ws)
        dvn = each(lambda a, b: a + b, qkdo, kds)
        dqk = each(lambda a, b: jnp.where(causal, _mx(a, b, NT), 0.0), d_o, vn)
        dkd = each(lambda a, b: _mx(a, b, NT), vn, dsn)
        dw = each(lambda a, b: -_mx(a, b, NT), dvn, s)
        wdvn = each(lambda a, b: _mx(a, b, TN), w, dvn)
        dgl = each(lambda a, b: jnp.sum(rsum(a * b), axis=0, keepdims=True), dsn, s)
        for h in heads:
            ds_ref[h] = qddo[h] - wdvn[h] + gl[h] * dsn[h]
        dru = each(lambda a, b: _mx(a, b, TN), tinv, dvn)
        drw = each(lambda a, b: _mx(a, b, TN), tinv, dw)
        dqkr = each(lambda a, b: a * b, dqk, dec)
        dq1 = each(_mx, dqkr, kn)
        dk1 = each(lambda a, b: _mx(a, b, TN), dqkr, qn)
        dnu = each(lambda a, b: _mx(a, b, NT), dru, u)
        dnw = each(lambda a, b: _mx(a, b, NT), drw, w)
        dn = each(lambda a, b: jnp.where(strict, -(a + b), 0.0), dnu, dnw)
        dkk = each(lambda a, b, d: a * b * d, dn, beta, dec)
        dk2 = each(_mx, dkk, kn)
        dk3 = each(lambda a, b: _mx(a, b, TN), dkk, kn)
        dgates = jnp.zeros((CHUNK, LANES), f32)
        for h in heads:
            drw_k = rsum(drw[h] * kn[h])
            dbeta = rsum(dru[h] * v[h]) + e[h] * drw_k + rsum(dn[h] * kkd[h])
            m = dn[h] * (kkd[h] * beta[h]) + dqk[h] * qk[h]
            de = beta[h] * drw_k + rsum(dqd[h] * qn[h])
            df = rsum(dkd[h] * kn[h])
            dgam = rsum(m) - _row_to_col(jnp.sum(m, axis=0, keepdims=True), eye) + de * e[h] - df * f[h]
            dgam_last = jnp.sum(df * f[h], axis=0, keepdims=True) + dgl[h] * gl[h]
            dgam = dgam + jnp.where(row == CHUNK - 1, dgam_last, 0.0)
            dy_ref[:, sl[h]] = dq1[h] + dqd[h] * e[h]
            dy_ref[:, 512 + h * LANES:512 + (h + 1) * LANES] = (beta[h] * e[h]) * drw[h] + dk2[h] + dk3[h] + dk1[h] + dkd[h] * f[h]
            dy_ref[:, 1024 + h * LANES:1024 + (h + 1) * LANES] = beta[h] * dru[h]
            dgates = dgates + jnp.where(lane == h, dbeta, 0.0) + jnp.where(lane == 4 + h, dgam, 0.0)
        dg_ref[...] = dgates

    rev = lambda width: pl.BlockSpec((CHUNK, width), lambda n: (nch - 1 - n, 0))
    mat = lambda d: pl.BlockSpec((1, GDN_HEADS, d, d), lambda n: (nch - 1 - n, 0, 0, 0))
    return _hosted(
        body, comm, name="gdn_bwd", grid=(nch,),
        in_specs=[rev(GDN_QKV), rev(LANES), rev(512), rev(512), mat(CHUNK), mat(CHUNK), mat(LANES), rev(512)],
        out_specs=(rev(GDN_QKV), rev(LANES)),
        out_shape=(jax.ShapeDtypeStruct((t, GDN_QKV), f32), jax.ShapeDtypeStruct((t, LANES), f32)),
        scratch_shapes=[pltpu.VMEM((GDN_HEADS, LANES, LANES), f32)],
        args=(y, gcum, u_all, w_all, qk_all, tinv_all, sall, do))


FOX_CLASSES = 4


def _fox_groups(t):
    nq = t // FOX_BQ
    ncls = min(FOX_CLASSES, nq)
    per = nq // ncls
    return [(g * per, per, (g + 1) * per * FOX_BQ) for g in range(ncls)]


def _fox_causal(i, keys):
    rows = i * FOX_BQ + lax.broadcasted_iota(jnp.int32, (FOX_BQ, keys), 0)
    return lax.broadcasted_iota(jnp.int32, (FOX_BQ, keys), 1) <= rows


def _fox_scores(q_ref, k_ref, gcumt_ref, h, causal):
    pr = h // 2
    lo = (h % 2) * FOX_DH
    lane = _lane((FOX_BQ, LANES))
    mask = (lane >= lo) & (lane < lo + FOX_DH)
    qm = jnp.where(mask, q_ref[:, pr * LANES:(pr + 1) * LANES] * (FOX_DH ** -0.5), 0.0).astype(bf16)
    kp = k_ref[:, pr * LANES:(pr + 1) * LANES].astype(bf16)
    s = _dot_nt(qm, kp, None) - gcumt_ref[8 + h:9 + h, :]
    return jnp.where(causal, s, NEG), mask, qm, kp


def _fox_fwd(proj, gcum, gcumt, ride=None):
    c0 = SEG_FOX // 512

    def group_call(q0, nq, keys, comm):
        def body(q_ref, k_ref, v_ref, gcumt_ref, o_ref, lse_ref):
            causal = _fox_causal(q0 + pl.program_id(0), keys)
            lane = _lane((FOX_BQ, LANES))
            lse_all = jnp.zeros((FOX_BQ, LANES), f32)
            for pr in range(FOX_HEADS // 2):
                vp = v_ref[:, pr * LANES:(pr + 1) * LANES].astype(bf16)
                o_pair = jnp.zeros((FOX_BQ, LANES), f32)
                for h in (2 * pr, 2 * pr + 1):
                    s, mask, _, _ = _fox_scores(q_ref, k_ref, gcumt_ref, h, causal)
                    m = jnp.max(s, axis=1, keepdims=True)
                    p = jnp.exp(s - m)
                    l = jnp.sum(p, axis=1, keepdims=True)
                    o_h = _dot(p.astype(bf16), vp, None) * (1.0 / l)
                    o_pair = jnp.where(mask, o_h, o_pair)
                    lse_all = jnp.where(lane == h, m + jnp.log(l), lse_all)
                o_ref[:, pr * LANES:(pr + 1) * LANES] = o_pair
            lse_ref[...] = lse_all

        seen = lambda col: pl.BlockSpec((keys, 512), lambda i: (0, col))
        return _hosted(
            body, comm, name=f"fox_fwd_{keys}", grid=(nq,),
            in_specs=[pl.BlockSpec((FOX_BQ, 512), lambda i: (q0 + i, c0)), seen(c0 + 1), seen(c0 + 2),
                      pl.BlockSpec((LANES, keys), lambda i: (0, 0))],
            out_specs=(pl.BlockSpec((FOX_BQ, 512), lambda i: (i, 0)), pl.BlockSpec((FOX_BQ, LANES), lambda i: (i, 0))),
            out_shape=(jax.ShapeDtypeStruct((nq * FOX_BQ, 512), f32), jax.ShapeDtypeStruct((nq * FOX_BQ, LANES), f32)),
            args=(proj, proj, proj, gcumt))

    parts = []
    for n, g in enumerate(_fox_groups(proj.shape[0])):
        hook = ride(n) if ride else None
        part, moved = group_call(*g, hook[0] if hook else None)
        parts.append(part)
        if hook:
            hook[1](moved)
    return jnp.concatenate([o for o, _ in parts], axis=0), jnp.concatenate([l for _, l in parts], axis=0)


def _fox_bwd(proj, gcum, gcumt, o, lse, do, ride=None):
    t = proj.shape[0]
    c0 = SEG_FOX // 512

    def group_call(q0, nq, keys, acc, comm):
        first = acc is None

        def body(q_ref, k_ref, v_ref, gcumt_ref, o_ref, lse_ref, do_ref, *rest):
            dq_ref, dk_ref, dv_ref, dcc_ref, dct_ref = rest[-5:]
            j = pl.program_id(0)
            causal = _fox_causal(q0 + j, keys)

            @pl.when(j == 0)
            def _():
                if first:
                    dk_ref[...] = jnp.zeros_like(dk_ref)
                    dv_ref[...] = jnp.zeros_like(dv_ref)
                    dct_ref[...] = jnp.zeros_like(dct_ref)
                else:
                    dk_ref[...], dv_ref[...], dct_ref[...] = rest[0][...], rest[1][...], rest[2][...]

            lane = _lane((FOX_BQ, LANES))
            dcc = jnp.zeros((FOX_BQ, LANES), f32)
            scale = FOX_DH ** -0.5
            for pr in range(FOX_HEADS // 2):
                sl = slice(pr * LANES, (pr + 1) * LANES)
                vp = v_ref[:, sl].astype(bf16)
                dq_pair = jnp.zeros((FOX_BQ, LANES), f32)
                for h in (2 * pr, 2 * pr + 1):
                    s, mask, qm, kp = _fox_scores(q_ref, k_ref, gcumt_ref, h, causal)
                    p = jnp.exp(s - lse_ref[:, h:h + 1])
                    dom = jnp.where(mask, do_ref[:, sl], 0.0)
                    delta = jnp.sum(dom * o_ref[:, sl], axis=1, keepdims=True)
                    domb = dom.astype(bf16)
                    ds = p * (_dot_nt(domb, vp, None) - delta)
                    dsb = ds.astype(bf16)
                    dv_ref[:, sl] += _dot_tn(p.astype(bf16), domb, None)
                    dk_ref[:, sl] += _dot_tn(dsb, qm, None)
                    dq_pair = jnp.where(mask, _dot(dsb, kp, None) * scale, dq_pair)
                    dcc = jnp.where(lane == 8 + h, jnp.sum(ds, axis=1, keepdims=True), dcc)
                    dct_ref[8 + h:9 + h, :] += -jnp.sum(ds, axis=0, keepdims=True)
                dq_ref[:, sl] = dq_pair.astype(bf16)
            dcc_ref[...] = dcc

        qblk = lambda col: pl.BlockSpec((FOX_BQ, 512), lambda i: (q0 + i, col))
        oblk = pl.BlockSpec((FOX_BQ, 512), lambda i: (i, 0))
        seen = lambda col: pl.BlockSpec((keys, 512), lambda i: (0, col))
        rblk = pl.BlockSpec((FOX_BQ, LANES), lambda i: (q0 + i, 0))
        seen_t = pl.BlockSpec((LANES, keys), lambda i: (0, 0))
        in_specs = [qblk(c0), seen(c0 + 1), seen(c0 + 2), seen_t, qblk(0), rblk, qblk(0)]
        args = [proj, proj, proj, gcumt, o, lse, do]
        aliases = {}
        if not first:
            in_specs += [seen(0), seen(0), seen_t]
            args += list(acc)
            aliases = {7: 1, 8: 2, 9: 4}
        return _hosted(
            body, comm, name=f"fox_bwd_{keys}", grid=(nq,), in_specs=in_specs,
            out_specs=(oblk, seen(0), seen(0), pl.BlockSpec((FOX_BQ, LANES), lambda i: (i, 0)), seen_t),
            out_shape=(jax.ShapeDtypeStruct((nq * FOX_BQ, 512), bf16), jax.ShapeDtypeStruct((t, 512), f32), jax.ShapeDtypeStruct((t, 512), f32),
                       jax.ShapeDtypeStruct((nq * FOX_BQ, LANES), f32), jax.ShapeDtypeStruct((LANES, t), f32)),
            aliases=aliases, args=args)

    acc, dqs, dccs = None, [], []
    for n, g in enumerate(reversed(_fox_groups(t))):
        hook = ride(n) if ride else None
        (dq, dk, dv, dcc, dct), moved = group_call(*g, acc, hook[0] if hook else None)
        if hook:
            hook[1](moved)
        acc = (dk, dv, dct)
        dqs.insert(0, dq)
        dccs.insert(0, dcc)
    return jnp.concatenate(dqs, axis=0), acc[0], acc[1], jnp.concatenate(dccs, axis=0), acc[2]


def _row(v, width=None):
    v = v.reshape(1, -1).astype(f32)
    if width is not None and v.shape[1] < width:
        v = jnp.pad(v, ((0, 0), (0, width - v.shape[1])))
    return v


LATE = ("w_out", "w_up", "w_ple_gate", "w_ple", "w_down")


def _device_grads(x, p, target, small, w_cat, conv_w, late, qc=None, tail=None, ln_in_out=None):
    z4 = jnp.zeros((4,), f32)
    bias_row = _row(jnp.concatenate([z4, small["dt_bias"].reshape(-1), small["b_f"].reshape(-1)]), LANES)
    alog_row = _row(jnp.concatenate([z4, small["a_log"].reshape(-1)]), LANES)
    g_gdn = _row(small["gdn_norm_g"])
    g_fox2 = _row(jnp.tile(small["fox_norm_g"].reshape(-1), 2))
    pb = p.astype(bf16)
    late = list(late)
    comm = qc is not None

    h0, h0b = ln_in_out if ln_in_out is not None else _ln_in(x, _row(small["ln_in_g"]), _row(small["ln_in_b"]))[0]
    proj = _mm(h0b, w_cat, "nt", 512, D_CAT, "mm_proj")
    gates, gcum, gcumt = _gates(proj, bias_row, alog_row)
    w_down_pieces = [(4, 0, 1)]

    def gather(phase, pieces):
        if not comm or not pieces:
            return None, lambda moved: None
        touched = sorted({i for i, _, _ in pieces})

        def took(moved):
            for i, buf in zip(touched, moved):
                late[i] = buf
        return phase([late[i] for i in touched], [(touched.index(i), k, n) for i, k, n in pieces]), took

    over, on = _gather_chips, _gather_pass_on
    cm, took = gather(over, [(0, 0, 1), (3, 0, 1)])
    (conv_c, qkv_n), moved = _gdn_conv(proj, conv_w, cm)
    took(moved)
    cm, took = gather(over, [(1, 0, 2)])
    (gu, gw, gqk, gtinv), moved = _gdn_local(qkv_n, gcum, cm)
    took(moved)
    cm, took = gather(over, [(1, 1, 2)])
    (o_gdn, sall), moved = _gdn_fwd(qkv_n, gcum, gu, gw, gqk, cm)
    took(moved)
    fox_plan = [(over, []), (on, [(0, 0, 1), (3, 0, 1), (1, 0, 2), (1, 1, 2)]), (over, [(2, 0, 1)]), (over, [(4, 0, 4)])]
    assert not comm or len(_fox_groups(x.shape[0])) == len(fox_plan)
    o_fox, lse = _fox_fwd(proj, gcum, gcumt, (lambda n: gather(*fox_plan[n])) if comm else None)
    cm, took = gather(on, [(2, 0, 1)])
    (attn,), moved = _attn_post(o_gdn, proj, o_fox, g_gdn, g_fox2, cm)
    took(moved)
    w_out = late[0].reshape(D_MODEL, D_MODEL)
    cm, took = gather(over, [(4, 1, 4)])
    (h1, h1b, xhat1, rstd1), moved = _ln1(h0, attn, w_out, _row(small["ln1_g"]), _row(small["ln1_b"]), cm)
    took(moved)
    w_up, w_ple = late[1], late[3]
    cm, took = gather(over, [(4, 1, 2)])
    up_act = _mm(h1b, w_up, "nn", 512, 1024, "mm_up", epi="relu2", shards=N_CHIPS, comm=cm)
    if cm:
        up_act, moved = up_act
        took(moved)
    up, act = up_act
    w_gate = late[2].reshape(D_MODEL, D_MODEL)
    cm, took = gather(on, w_down_pieces)
    gp = _mm(h1b, w_gate, "nn", 512, D_MODEL, "mm_gate", comm=cm)
    if cm:
        gp, moved = gp
        took(moved)
    w_down = late[4].reshape(D_FF, D_MODEL)
    dr2, dr2b, dpe, dgp, pg2 = _ln2_loss(h1, act, w_down, pb, w_ple, gp, _row(small["b_ple_gate"]), _row(small["ln2_g"]),
                                         _row(small["ln2_b"]), target)

    by_dest = lambda g: g.reshape((N_CHIPS, -1, g.shape[-1]))
    g_late = [None] * len(LATE)
    state = dict(from_sibling=[None] * len(LATE), sent=[None] * len(LATE), landing=[None] * len(LATE))
    nothing = (None, lambda moved: None)

    def to_sibling(idx):
        if not comm:
            return nothing

        def took(moved):
            for i, b1 in zip(idx, moved):
                state["from_sibling"][i] = b1
                state["sent"][i] = _add_pair(g_late[i], b1, qc, "add_pair_" + LATE[i])
                state["landing"][i] = _landing([state["sent"][i]])[0]
        return _exchange_pairs([g_late[i] for i in idx]), took

    def to_chips(pieces):
        if not comm or not pieces:
            return nothing
        touched = sorted({i for i, _, _ in pieces})

        def took(moved):
            for i, b2 in zip(touched, moved):
                state["landing"][i] = b2
        return _exchange_chips([state["sent"][i] for i in touched], [state["landing"][i] for i in touched],
                               [(touched.index(i), k, n) for i, k, n in pieces]), took

    def ride(result, cm, took):
        if cm:
            result, moved = result
            took(moved)
        return result

    dup = _mm(dr2b, w_down, "nt", 512, 2048, "mm_dact", epi="relu2_bwd", extra=up)
    g_late[4] = by_dest(_mm(act, dr2b, "tn", 1024, D_MODEL, "mm_gdown"))
    cm, took = to_sibling([4])
    g_late[1] = ride(_mm(h1b, dup, "tn", 1024, 1024, "mm_gup", shards=N_CHIPS, comm=cm), cm, took)
    g_late[2] = by_dest(_mm(h1b, dgp, "tn", 1024, D_MODEL, "mm_ggate"))
    g_late[3] = _mm(pb, dpe, "tn", D_PLE, D_MODEL // N_CHIPS, "mm_gple", shards=N_CHIPS)
    cm, took = to_chips([(4, 0, 2)])
    (dr1, dr1b, pg1), moved = _ln1_bwd(dr2, dup, w_up, dgp, w_gate, xhat1, rstd1, _row(small["ln1_g"]), cm)
    took(moved)
    cm, took = to_sibling([2, 3])
    g_late[0] = by_dest(ride(_mm(attn, dr1b, "tn", 1024, D_MODEL, "mm_gout", comm=cm), cm, took))
    cm, took = to_chips([(2, 0, 1), (3, 0, 1)])
    (do_gdn, dz, do_fox, pga), moved = _attn_post_bwd(dr1b, w_out, o_gdn, proj, o_fox, g_gdn, g_fox2, cm)
    took(moved)
    chip_plan = [[(4, 1, 2), (1, 0, 2)], [(1, 1, 2)], [(0, 0, 1)], []]

    def gdn_backward():
        cm, took = to_chips(chip_plan[0])
        state["gdn"], moved = _gdn_bwd(qkv_n, gcum, gu, gw, gqk, gtinv, sall, do_gdn, cm)
        took(moved)

    def fox_ride(n):
        if n == 0:
            return to_sibling([1, 0])
        if n == 1:
            gdn_backward()
        return to_chips(chip_plan[n])

    assert not comm or len(_fox_groups(x.shape[0])) == len(chip_plan)
    dfq, dfk, dfv, dccol, dct = _fox_bwd(proj, gcum, gcumt, o_fox, lse, do_fox, fox_ride if comm else None)
    if not comm:
        gdn_backward()
    dqkv_n, dgates = state["gdn"]
    dsmall, pgg = _gates_bwd(proj, bias_row, alog_row, gates, dgates, dccol, dct)
    cm = None
    if comm:
        cm = _share_halves([_add_chips(g, b1, b2, qc, "add_chips_" + n)
                            for g, b1, b2, n in zip(g_late, state["from_sibling"], state["landing"], LATE)])
    (du, g_conv8), reduced = _gdn_conv_bwd(proj, conv_w, conv_c, dqkv_n, cm)
    if comm:
        g_late = list(reduced)
    t = x.shape[0]
    dproj = jnp.concatenate([du, dz, dfq, dfk.astype(bf16), dfv.astype(bf16), dsmall, jnp.zeros((t, D_CAT - SEG_SMALL - LANES), bf16)], axis=1)
    g_cat = _mm(dproj, h0b, "tn", 1280, D_MODEL, "mm_gcat")
    cm, took = tail[0](g_cat) if tail else (None, None)
    dh0_mm = _mm(dproj, w_cat, "nn", 512, D_MODEL, "mm_dh0", comm=cm)
    if cm:
        dh0_mm, moved = dh0_mm
        took(moved)
    cm, took = tail[1]() if tail and tail[1] else (None, None)
    (grad_x, pg0), moved = _ln_in_bwd(x, dr1, dh0_mm, _row(small["ln_in_g"]), cm)
    if cm:
        took(moved)

    g_fox = pga[1, :FOX_DH] + pga[1, FOX_DH:]
    small_grads = dict(
        ln_in_g=pg0[0], ln_in_b=pg0[1], ln1_g=pg1[0], ln1_b=pg1[1], b_ple_gate=pg2[2], ln2_g=pg2[0], ln2_b=pg2[1],
        gdn_norm_g=pga[0], fox_norm_g=g_fox, a_log=pgg[1, 4:8], dt_bias=pgg[0, 4:8], b_f=pgg[0, 8:16], loss=pg2[3, 0:1])
    return grad_x, g_cat, g_conv8[:CONV_W], dict(zip(LATE, g_late)), small_grads


ANY = pl.BlockSpec(memory_space=pl.ANY)
CONV_PKT_ROWS = 16


def _mesh_pos():
    return lax.axis_index("x"), lax.axis_index("y"), lax.axis_index("c")


def _other_chips(x, y):
    return [(1 - x, y), (x, 1 - y), (1 - x, 1 - y)]


def _rcopy(src, dst, send_sem, recv_sem, dev):
    return pltpu.make_async_remote_copy(src_ref=src, dst_ref=dst, send_sem=send_sem, recv_sem=recv_sem,
                                        device_id=dev, device_id_type=MESH)


class _Comm:
    def __init__(self, ins, outs, aliases, n_sems, start, finish):
        self.ins, self.outs, self.aliases, self.n_sems, self.start, self.finish = list(ins), list(outs), dict(aliases), n_sems, start, finish


def _hosted(body, comm, *, name, grid, in_specs, out_specs, out_shape, args, scratch_shapes=(), aliases=None):
    n_in, n_out, n_sc = len(in_specs), len(out_specs), len(scratch_shapes)
    k, ko = (len(comm.ins), len(comm.outs)) if comm else (0, 0)

    def kernel_body(*refs):
        o0 = n_in + k
        s0 = o0 + n_out + ko
        if comm:
            cins, couts, (ssem, rsem) = refs[n_in:o0], refs[o0 + n_out:s0], refs[s0 + n_sc:]
            step = pl.program_id(0)
            for d in range(1, len(grid)):
                step = step * grid[d] + pl.program_id(d)

            @pl.when(step == 0)
            def _():
                comm.start(cins, couts, ssem, rsem)

        body(*refs[:n_in], *refs[o0:o0 + n_out], *refs[s0:s0 + n_sc])
        if comm:
            last = 1
            for n in grid:
                last *= n

            @pl.when(step == last - 1)
            def _():
                comm.finish(cins, couts, ssem, rsem)

    io_aliases = dict(aliases or {})
    scratch = list(scratch_shapes)
    if comm:
        io_aliases.update({n_in + i: n_out + j for i, j in comm.aliases.items()})
        scratch += [pltpu.SemaphoreType.DMA((comm.n_sems,)), pltpu.SemaphoreType.DMA((comm.n_sems,))]
    res = pl.pallas_call(
        kernel_body, name=name, grid=grid, in_specs=list(in_specs) + [ANY] * k, out_specs=tuple(out_specs) + (ANY,) * ko,
        out_shape=tuple(out_shape) + tuple(comm.outs if comm else ()), scratch_shapes=scratch, input_output_aliases=io_aliases,
        compiler_params=_params(("arbitrary",) * len(grid)),
    )(*args, *(comm.ins if comm else ()))
    return tuple(res[:n_out]), tuple(res[n_out:])


def _comm_only(phases, name):
    n_in = sum(len(p.ins) for p in phases)

    def body(*refs):
        n_out = sum(len(p.outs) for p in phases)
        sems = refs[n_in + n_out:]
        i0, o0 = 0, n_in
        for j, p in enumerate(phases):
            cins, couts = refs[i0:i0 + len(p.ins)], refs[o0:o0 + len(p.outs)]
            p.start(cins, couts, sems[2 * j], sems[2 * j + 1])
            p.finish(cins, couts, sems[2 * j], sems[2 * j + 1])
            i0 += len(p.ins)
            o0 += len(p.outs)

    aliases, i0, o0 = {}, 0, 0
    for p in phases:
        aliases.update({i0 + i: o0 + j for i, j in p.aliases.items()})
        i0 += len(p.ins)
        o0 += len(p.outs)
    outs = [o for p in phases for o in p.outs]
    res = pl.pallas_call(
        body, name=name, out_shape=tuple(outs), in_specs=[ANY] * n_in, out_specs=(ANY,) * len(outs), input_output_aliases=aliases,
        scratch_shapes=[pltpu.SemaphoreType.DMA((p.n_sems,)) for p in phases for _ in range(2)],
    )(*[a for p in phases for a in p.ins])
    split, o0 = [], 0
    for p in phases:
        split.append(tuple(res[o0:o0 + len(p.outs)]))
        o0 += len(p.outs)
    return split


def _like(arrays):
    return [jax.ShapeDtypeStruct(a.shape, a.dtype) for a in arrays]


def _half(ref, slot, hf, piece=(0, 1)):
    k, n = piece
    rows = ref.shape[1] // 2 // n
    return ref.at[slot, pl.ds((hf * n + k) * rows, rows)]


def _whole_halves(arrays):
    return [(i, 0, 1) for i in range(len(arrays))]


def _gather_chips(bufs, pieces=None, whole=False, base=0):
    nw = len(bufs)
    pieces = _whole_halves(bufs) if pieces is None else pieces
    part = (lambda ref, slot, c, piece: ref.at[slot]) if whole else _half

    def copies(couts):
        x, y, c = _mesh_pos()
        q = 2 * x + y
        for j, (i, k, n) in enumerate(pieces):
            for kc, chip in enumerate(_other_chips(x, y)):
                mine, theirs = part(couts[i], q, c, (k, n)), part(couts[i], 2 * chip[0] + chip[1], c, (k, n))
                yield base + j * 3 + kc, mine, theirs, (*chip, c)

    def start(cins, couts, ssem, rsem):
        for s, mine, _, dev in copies(couts):
            _rcopy(mine, mine, ssem.at[s], rsem.at[s], dev).start()

    def finish(cins, couts, ssem, rsem):
        for s, _, theirs, dev in copies(couts):
            _rcopy(theirs, theirs, ssem.at[s], rsem.at[s], dev).wait_recv()
        for s, mine, _, dev in copies(couts):
            _rcopy(mine, mine, ssem.at[s], rsem.at[s], dev).wait_send()

    return _Comm(bufs, _like(bufs), {i: i for i in range(nw)}, 3 * len(pieces), start, finish)


def _gather_pass_on(bufs, pieces=None, base=0):
    nw = len(bufs)
    pieces = _whole_halves(bufs) if pieces is None else pieces

    def copies(couts):
        x, y, c = _mesh_pos()
        for j, (i, k, n) in enumerate(pieces):
            for kc, chip in enumerate(_other_chips(x, y)):
                slot = 2 * chip[0] + chip[1]
                yield base + j * 3 + kc, _half(couts[i], slot, c, (k, n)), _half(couts[i], slot, 1 - c, (k, n)), (x, y, 1 - c)

    def start(cins, couts, ssem, rsem):
        for s, landed, _, sib in copies(couts):
            _rcopy(landed, landed, ssem.at[s], rsem.at[s], sib).start()

    def finish(cins, couts, ssem, rsem):
        for s, _, passed, sib in copies(couts):
            _rcopy(passed, passed, ssem.at[s], rsem.at[s], sib).wait_recv()
        for s, landed, _, sib in copies(couts):
            _rcopy(landed, landed, ssem.at[s], rsem.at[s], sib).wait_send()

    return _Comm(bufs, _like(bufs), {i: i for i in range(nw)}, 3 * len(pieces), start, finish)


def _gather_now(bufs, packets):
    nb = len(bufs)
    over, on, pk = _gather_chips(bufs), _gather_pass_on(bufs, base=3 * nb), _gather_chips(packets, whole=True, base=6 * nb)

    def start(cins, couts, ssem, rsem):
        over.start(cins[:nb], couts[:nb], ssem, rsem)
        pk.start(cins[nb:], couts[nb:], ssem, rsem)

    def finish(cins, couts, ssem, rsem):
        over.finish(cins[:nb], couts[:nb], ssem, rsem)
        on.start(cins[:nb], couts[:nb], ssem, rsem)
        on.finish(cins[:nb], couts[:nb], ssem, rsem)
        pk.finish(cins[nb:], couts[nb:], ssem, rsem)

    every = list(bufs) + list(packets)
    return _Comm(every, _like(every), {i: i for i in range(len(every))}, 6 * nb + 3 * len(packets), start, finish)


def _exchange_pairs(gs):
    nw = len(gs)

    def copies(cins, couts):
        x, y, c = _mesh_pos()
        for i in range(nw):
            for d in range(N_CHIPS):
                yield i * N_CHIPS + d, _half(cins[i], d, 1 - c), couts[i].at[d], (x, y, 1 - c)

    def start(cins, couts, ssem, rsem):
        for s, src, dst, sib in copies(cins, couts):
            _rcopy(src, dst, ssem.at[s], rsem.at[s], sib).start()

    def finish(cins, couts, ssem, rsem):
        for s, src, dst, sib in copies(cins, couts):
            _rcopy(src, dst, ssem.at[s], rsem.at[s], sib).wait_recv()
        for s, src, dst, sib in copies(cins, couts):
            _rcopy(src, dst, ssem.at[s], rsem.at[s], sib).wait_send()

    outs = [jax.ShapeDtypeStruct((N_CHIPS, g.shape[1] // 2, g.shape[2]), g.dtype) for g in gs]
    return _Comm(gs, outs, {}, N_CHIPS * nw, start, finish)


def _gather_packets(small):
    def peers():
        x, y, c = _mesh_pos()
        for r in range(1, 8):
            fx, fy, fc = (r >> 2) & 1, (r >> 1) & 1, r & 1
            yield r - 1, (1 - x if fx else x, 1 - y if fy else y, 1 - c if fc else c)

    def start(cins, couts, ssem, rsem):
        x, y, c = _mesh_pos()
        mine = couts[0].at[4 * x + 2 * y + c]
        for s, peer in peers():
            _rcopy(mine, mine, ssem.at[s], rsem.at[s], peer).start()

    def finish(cins, couts, ssem, rsem):
        x, y, c = _mesh_pos()
        mine = couts[0].at[4 * x + 2 * y + c]
        for s, peer in peers():
            theirs = couts[0].at[4 * peer[0] + 2 * peer[1] + peer[2]]
            _rcopy(theirs, theirs, ssem.at[s], rsem.at[s], peer).wait_recv()
        for s, peer in peers():
            _rcopy(mine, mine, ssem.at[s], rsem.at[s], peer).wait_send()

    return _Comm([small], _like([small]), {0: 0}, 7, start, finish)


def _exchange_chips(a4s, b2s, pieces=None):
    nw = len(a4s)
    pieces = _whole_halves(a4s) if pieces is None else pieces

    def copies(cins, couts):
        x, y, c = _mesh_pos()
        for j, (i, k, n) in enumerate(pieces):
            rows = a4s[i].shape[1] // n
            part = pl.ds(k * rows, rows)
            for kc, chip in enumerate(_other_chips(x, y)):
                yield j * 3 + kc, cins[i].at[2 * chip[0] + chip[1], part], couts[i].at[kc, part], (*chip, c)

    def start(cins, couts, ssem, rsem):
        for s, src, dst, dev in copies(cins, couts):
            _rcopy(src, dst, ssem.at[s], rsem.at[s], dev).start()

    def finish(cins, couts, ssem, rsem):
        for s, src, dst, dev in copies(cins, couts):
            _rcopy(src, dst, ssem.at[s], rsem.at[s], dev).wait_recv()
        for s, src, dst, dev in copies(cins, couts):
            _rcopy(src, dst, ssem.at[s], rsem.at[s], dev).wait_send()

    return _Comm(list(a4s) + list(b2s), _like(b2s), {nw + i: i for i in range(nw)}, 3 * len(pieces), start, finish)


def _landing(a4s):
    return [lax.empty((3,) + a.shape[1:], a.dtype) for a in a4s]


def _share_halves(rs):
    nw = len(rs)

    def halves(couts, i, hf):
        rows = rs[i].shape[0] // 2
        return couts[i].at[pl.ds(hf * rows, rows)]

    def start(cins, couts, ssem, rsem):
        x, y, c = _mesh_pos()
        for i in range(nw):
            _rcopy(halves(couts, i, c), halves(couts, i, c), ssem.at[i], rsem.at[i], (x, y, 1 - c)).start()

    def finish(cins, couts, ssem, rsem):
        x, y, c = _mesh_pos()
        for i in range(nw):
            _rcopy(halves(couts, i, 1 - c), halves(couts, i, 1 - c), ssem.at[i], rsem.at[i], (x, y, 1 - c)).wait_recv()
        for i in range(nw):
            _rcopy(halves(couts, i, c), halves(couts, i, c), ssem.at[i], rsem.at[i], (x, y, 1 - c)).wait_send()

    return _Comm(rs, _like(rs), {i: i for i in range(nw)}, nw, start, finish)


ADD_ROWS = 256


def _add_pair(g4, b1, qc_idx, name):
    _, half, cols = b1.shape
    rb = ADD_ROWS if half % ADD_ROWS == 0 else half
    nb = half // rb

    def body(qc_ref, g_ref, b_ref, ob_ref):
        ob_ref[...] = (g_ref[...] + b_ref[...]).astype(bf16)

    blk = (1, rb, cols)
    out = pl.BlockSpec(blk, lambda d, i, qc: (d, i, 0))
    return pl.pallas_call(
        body, name=name,
        grid_spec=pltpu.PrefetchScalarGridSpec(
            num_scalar_prefetch=1, grid=(N_CHIPS, nb),
            in_specs=[pl.BlockSpec(blk, lambda d, i, qc: (d, qc[1] * nb + i, 0)), out],
            out_specs=out),
        out_shape=jax.ShapeDtypeStruct(b1.shape, bf16),
        compiler_params=_params(("parallel", "parallel")),
    )(qc_idx, g4, b1)


def _add_chips(g4, b1, b2, qc_idx, name):
    _, half, cols = b1.shape
    rb = ADD_ROWS if half % ADD_ROWS == 0 else half
    nb = half // rb

    def body(qc_ref, g_ref, s_ref, b_ref, o_ref):
        o_ref[...] = (((g_ref[0] + s_ref[0]) + b_ref[0].astype(f32)) + b_ref[1].astype(f32)) + b_ref[2].astype(f32)

    return pl.pallas_call(
        body, name=name,
        grid_spec=pltpu.PrefetchScalarGridSpec(
            num_scalar_prefetch=1, grid=(nb,),
            in_specs=[pl.BlockSpec((1, rb, cols), lambda i, qc: (qc[0], qc[1] * nb + i, 0)),
                      pl.BlockSpec((1, rb, cols), lambda i, qc: (qc[0], i, 0)), pl.BlockSpec((3, rb, cols), lambda i, qc: (0, i, 0))],
            out_specs=pl.BlockSpec((rb, cols), lambda i, qc: (qc[1] * nb + i, 0))),
        out_shape=jax.ShapeDtypeStruct((2 * half, cols), f32),
        compiler_params=_params(("parallel",)),
    )(qc_idx, g4, b1, b2)


def _adamw_math(w, g, m, v):
    m = ADAM_B1 * m + (1.0 - ADAM_B1) * g
    v = ADAM_B2 * v + (1.0 - ADAM_B2) * (g * g)
    m_hat = m / (1.0 - ADAM_B1 ** ADAM_STEP)
    v_hat = v / (1.0 - ADAM_B2 ** ADAM_STEP)
    return -ADAM_LR * (m_hat / (jnp.sqrt(v_hat) + ADAM_EPS) + ADAM_WD * w), m, v


def _adamw(w, g, m, v, name, comm=None):
    rows = w.shape[0]
    if w.ndim == 3:
        rb = max(r for r in range(1, ADD_ROWS // 4 + 1) if rows % r == 0)
    else:
        rb = ADD_ROWS if rows % ADD_ROWS == 0 else rows

    def body(w_ref, g_ref, m_ref, v_ref, go_ref, d_ref, mo_ref, vo_ref):
        g = g_ref[...]
        go_ref[...] = g
        d_ref[...], mo_ref[...], vo_ref[...] = _adamw_math(w_ref[...], g, m_ref[...], v_ref[...])

    blk = pl.BlockSpec((rb,) + w.shape[1:], lambda i: (i,) + (0,) * (w.ndim - 1))
    return _hosted(body, comm, name=name, grid=(rows // rb,), in_specs=[blk] * 4, out_specs=(blk,) * 4,
                   out_shape=(jax.ShapeDtypeStruct(w.shape, f32),) * 4, args=(w, g, m, v))


def _small_sum_adamw(all_pkts, w, m, v):
    names = [n for n, _, _ in SMALL_LAYOUT if n in w]
    place = {n: (r0, size) for n, r0, size in SMALL_LAYOUT}
    rows_of = lambda size: -(-size // LANES)
    flat = lambda a: a.reshape(1, -1)
    k = len(names)

    def body(*refs):
        a_ref, ins = refs[0], refs[1:1 + 3 * k]
        g_ref, outs = refs[1 + 3 * k], refs[2 + 3 * k:2 + 7 * k]
        packs = refs[2 + 7 * k:]
        g = a_ref[0]
        for r in range(1, 8):
            g = g + a_ref[r]
        g_ref[...] = g
        for kind in range(3):
            packs[kind][...] = jnp.zeros_like(packs[kind])
            for j, n in enumerate(names):
                r0, size = place[n]
                for r in range(rows_of(size)):
                    width = min(LANES, size - r * LANES)
                    packs[kind][r0 + r:r0 + r + 1, 0:width] = ins[kind * k + j][:, r * LANES:r * LANES + width]
        results = (g,) + _adamw_math(packs[0][...], g, packs[1][...], packs[2][...])
        for kind, val in enumerate(results):
            for j, n in enumerate(names):
                r0, size = place[n]
                for r in range(rows_of(size)):
                    width = min(LANES, size - r * LANES)
                    outs[kind * k + j][:, r * LANES:r * LANES + width] = val[r0 + r:r0 + r + 1, 0:width]

    args = [all_pkts] + [flat(d[n]) for d in (w, m, v) for n in names]
    out_shape = [jax.ShapeDtypeStruct(all_pkts.shape[1:], f32)] + [jax.ShapeDtypeStruct((1, place[n][1]), f32) for _ in range(4) for n in names]
    res = pl.pallas_call(body, name="small_sum_adamw", out_shape=tuple(out_shape),
                         scratch_shapes=[pltpu.VMEM(all_pkts.shape[1:], f32)] * 3)(*args)
    by_kind = [{n: res[1 + kind * k + j].reshape(w[n].shape) for j, n in enumerate(names)} for kind in range(4)]
    return res[0], by_kind


SMALL_LAYOUT = (("ln_in_g", 0, 1024), ("ln_in_b", 8, 1024), ("ln1_g", 16, 1024), ("ln1_b", 24, 1024), ("b_ple_gate", 32, 1024),
                ("ln2_g", 40, 1024), ("ln2_b", 48, 1024), ("gdn_norm_g", 56, 128), ("fox_norm_g", 57, 64), ("a_log", 58, 4),
                ("dt_bias", 59, 4), ("b_f", 60, 8), ("loss", 61, 1))
SMALL_CONV_ROW = 64
SMALL_ROWS = 128


def _pack_small(vals, conv=None):
    rows = []
    nxt = 0
    for n, r0, size in SMALL_LAYOUT:
        assert r0 == nxt
        v = vals[n].reshape(-1).astype(f32) if n in vals else jnp.zeros((size,), f32)
        nrows = -(-size // LANES)
        rows.append(jnp.pad(v, (0, nrows * LANES - size)).reshape(nrows, LANES))
        nxt = r0 + nrows
    rows.append(jnp.zeros((SMALL_CONV_ROW - nxt, LANES), f32))
    conv_rows = CONV_W * GDN_QKV // LANES
    rows.append(jnp.zeros((conv_rows, LANES), f32) if conv is None else conv.reshape(conv_rows, LANES))
    rows.append(jnp.zeros((SMALL_ROWS - SMALL_CONV_ROW - conv_rows, LANES), f32))
    return jnp.concatenate(rows, axis=0)


def _unpack_small(pkt, shapes):
    out = {}
    for n, r0, size in SMALL_LAYOUT:
        if n in shapes:
            nrows = -(-size // LANES)
            out[n] = pkt[r0:r0 + nrows].reshape(-1)[:size].reshape(shapes[n])
    return out


WEIGHTS = ("ln_in_g", "ln_in_b", "w_in", "conv_w", "a_log", "dt_bias", "gdn_norm_g", "b_f", "fox_norm_g", "w_out", "ln1_g", "ln1_b",
           "w_up", "w_down", "w_ple", "w_ple_gate", "b_ple_gate", "ln2_g", "ln2_b")
SMALL_NAMES = tuple(n for n, _, _ in SMALL_LAYOUT if n != "loss")


def kernel(x, p, ln_in_g, ln_in_b, w_in, conv_w, a_log, dt_bias, gdn_norm_g, b_f, fox_norm_g, w_out, ln1_g, ln1_b, w_up, w_down, w_ple, w_ple_gate, b_ple_gate, ln2_g, ln2_b, loss_target, m_ln_in_g, m_ln_in_b, m_w_in, m_conv_w, m_a_log, m_dt_bias, m_gdn_norm_g, m_b_f, m_fox_norm_g, m_w_out, m_ln1_g, m_ln1_b, m_w_up, m_w_down, m_w_ple, m_w_ple_gate, m_b_ple_gate, m_ln2_g, m_ln2_b, v_ln_in_g, v_ln_in_b, v_w_in, v_conv_w, v_a_log, v_dt_bias, v_gdn_norm_g, v_b_f, v_fox_norm_g, v_w_out, v_ln1_g, v_ln1_b, v_w_up, v_w_down, v_w_ple, v_w_ple_gate, v_b_ple_gate, v_ln2_g, v_ln2_b):
    given = dict(locals())
    w = {n: given[n] for n in WEIGHTS}
    m = {n: given["m_" + n] for n in WEIGHTS}
    v = {n: given["v_" + n] for n in WEIGHTS}
    xi, yi, ci = _mesh_pos()
    q = 2 * xi + yi

    def slot_buffer(val, dtype, slots=N_CHIPS, slot=q, rows=None):
        rows = val.shape[0] if rows is None else rows
        return lax.dynamic_update_slice(lax.empty((slots, rows) + val.shape[1:], dtype), val.astype(dtype)[None], (slot, 0, 0))

    shard_cols = D_IN // N_CHIPS
    conv_rows = CONV_W * GDN_QKV // N_CHIPS // LANES
    conv_pkt = jnp.pad(w["conv_w"][0].reshape(-1, LANES), ((0, CONV_PKT_ROWS - conv_rows), (0, 0)))
    ln_in_out, (w_in4, conv_all) = _ln_in(x[0], _row(w["ln_in_g"]), _row(w["ln_in_b"]),
                                          _gather_now([slot_buffer(w["w_in"][0].T, bf16, rows=W_IN_ROWS)], [slot_buffer(conv_pkt, f32)]))
    conv_full = jnp.concatenate([conv_all[d, :conv_rows].reshape(CONV_W, GDN_QKV // N_CHIPS) for d in range(N_CHIPS)], axis=1)
    wi = jnp.concatenate([w_in4[d, :shard_cols] for d in range(N_CHIPS)], axis=0)
    w_cat = jnp.concatenate([wi[:OFF_BETA], wi[OFF_FOX:OFF_F], wi[OFF_BETA:OFF_FOX], wi[OFF_F:],
                             jnp.zeros((D_CAT - D_IN, D_MODEL), bf16)], axis=0)

    small = {n: w[n] for n in SMALL_NAMES}
    qc = jnp.stack([q, ci]).astype(jnp.int32)
    tail_state = {}

    def pairs_phase(gc):
        g_in = jnp.concatenate([gc[:OFF_BETA], gc[SEG_SMALL:SEG_SMALL + 8], gc[SEG_FOX:SEG_SMALL], gc[SEG_SMALL + 8:SEG_SMALL + 16]], axis=0)
        g_in4 = jnp.stack([jnp.pad(g_in[d * shard_cols:(d + 1) * shard_cols], ((0, W_IN_ROWS - shard_cols), (0, 0))) for d in range(N_CHIPS)])

        def took(moved):
            sent = _add_pair(g_in4, moved[0], qc, "add_pair_w_in")
            tail_state.update(g=g_in4, from_sibling=moved[0], sent=[sent], landing=_landing([sent]))
        return _exchange_pairs([g_in4]), took

    grad_x, _, g_conv, g_late, small_g = _device_grads(
        x[0], p[0, 0], loss_target[0], small, w_cat, conv_full, [slot_buffer(w[n][0], bf16) for n in LATE], qc, tail=(pairs_phase, None),
        ln_in_out=ln_in_out)
    packets = _gather_packets(slot_buffer(_pack_small(small_g, g_conv), f32, 8, 4 * xi + 2 * yi + ci))
    (b2,), (small_all,) = _comm_only([_exchange_chips(tail_state["sent"], tail_state["landing"]), packets], "exchange_chips_w_in")
    (g_late["w_in"],), = _comm_only(
        [_share_halves([_add_chips(tail_state["g"], tail_state["from_sibling"], b2, qc, "add_chips_w_in")])], "share_w_in")

    grads, delta, new_m, new_v = {}, {}, {}, {}
    for n, g in g_late.items():
        if n == "w_in":
            as_stored = lambda a: jnp.transpose(a, (2, 0, 1))
            outs, _ = _adamw(as_stored(w[n]), g[:shard_cols].reshape(shard_cols, 1, D_MODEL), as_stored(m[n]), as_stored(v[n]), "adamw_" + n)
            grads[n], delta[n], new_m[n], new_v[n] = (jnp.transpose(a, (1, 2, 0)) for a in outs)
        else:
            outs, _ = _adamw(w[n][0], g, m[n][0], v[n][0], "adamw_" + n)
            grads[n], delta[n], new_m[n], new_v[n] = (a.reshape(w[n].shape) for a in outs)
    pick = lambda d: {n: d[n] for n in SMALL_NAMES}
    g_pkt, by_kind = _small_sum_adamw(small_all, pick(w), pick(m), pick(v))
    for dst, vals in zip((grads, delta, new_m, new_v), by_kind):
        dst.update(vals)
    conv_rows_all = CONV_W * GDN_QKV // LANES
    conv_g_full = g_pkt[SMALL_CONV_ROW:SMALL_CONV_ROW + conv_rows_all].reshape(CONV_W, GDN_QKV)
    conv_g = lax.dynamic_slice_in_dim(conv_g_full, q * (GDN_QKV // N_CHIPS), GDN_QKV // N_CHIPS, axis=1)
    outs, _ = _adamw(w["conv_w"][0], conv_g, m["conv_w"][0], v["conv_w"][0], "adamw_conv_w")
    grads["conv_w"], delta["conv_w"], new_m["conv_w"], new_v["conv_w"] = (a.reshape(w["conv_w"].shape) for a in outs)
    loss = g_pkt[61, 0]
    return (loss, grad_x[None], *[grads[n] for n in WEIGHTS], *[delta[n] for n in WEIGHTS],
            *[new_m[n] for n in WEIGHTS], *[new_v[n] for n in WEIGHTS])
```

```python
import functools

import jax
import jax.numpy as jnp
from jax import lax
from jax.experimental import pallas as pl
from jax.experimental.pallas import tpu as pltpu

f32 = jnp.float32
bf16 = jnp.bfloat16
HI = lax.Precision.HIGHEST
MESH = pl.DeviceIdType.MESH

D_MODEL = 1024
CHUNK = 64
GDN_HEADS = 4
GDN_DK = 128
FOX_HEADS = 8
FOX_DH = 64
CONV_W = 4
D_FF = 4096
D_PLE = 256
LN_EPS = 1e-5
NORM_EPS = 1e-6
ALPHA = 2.0 ** 0.25
GDN_QKV = 1536
OFF_Z = 1536
OFF_BETA = 2048
OFF_FOX = 2056
OFF_F = 3592
D_IN = 3600
ADAM_LR = 0.001
ADAM_B1 = 0.9
ADAM_B2 = 0.999
ADAM_EPS = 1e-08
ADAM_WD = 0.01
ADAM_STEP = 10

SEG_FOX = 2048
SEG_SMALL = 3584
D_CAT = 3840
LANES = 128
TOK_BLK = 256
FOX_BQ = 256
VMEM_LIMIT = 56 * 1024 * 1024
NEG = -1e30

N_CHIPS = 4
W_IN_ROWS = 928


def _params(sem=None, **kw):
    return pltpu.CompilerParams(dimension_semantics=sem, vmem_limit_bytes=VMEM_LIMIT, **kw)


def _sigmoid(x):
    return 1.0 / (1.0 + jnp.exp(-x))


def _softplus(x):
    return jnp.maximum(x, 0.0) + jnp.log(1.0 + jnp.exp(-jnp.abs(x)))


def _ln_fwd(x, g, b):
    mu = jnp.mean(x, -1, keepdims=True)
    xc = x - mu
    var = jnp.mean(xc * xc, -1, keepdims=True)
    rstd = lax.rsqrt(var + LN_EPS)
    xhat = xc * rstd
    return xhat * g + b, xhat, rstd


def _ln_bwd(dy, xhat, rstd, g):
    dxh = dy * g
    m1 = jnp.mean(dxh, -1, keepdims=True)
    m2 = jnp.mean(dxh * xhat, -1, keepdims=True)
    return rstd * (dxh - m1 - xhat * m2)


def _dot(a, b, prec=HI):
    return jnp.dot(a, b, precision=prec, preferred_element_type=f32)


def _dot_nt(a, b, prec=HI):
    return lax.dot_general(a, b, (((1,), (1,)), ((), ())), precision=prec, preferred_element_type=f32)


def _dot_tn(a, b, prec=HI):
    return lax.dot_general(a, b, (((0,), (0,)), ((), ())), precision=prec, preferred_element_type=f32)


def _lane(shape):
    return lax.broadcasted_iota(jnp.int32, shape, len(shape) - 1)


def _mm(a, b, mode, tm, tn, name, out_dtype=f32, epi=None, extra=None, shards=1, comm=None):
    if mode == "nn":
        (m, k), n = a.shape, b.shape[-1] * shards
    elif mode == "nt":
        (m, k), n = a.shape, b.shape[-2]
    else:
        (k, m), n = a.shape, b.shape[1]
    assert m % tm == 0 and n % tn == 0, (name, m, n, tm, tn)
    per = (n // shards) // tn
    assert mode == "nt" or per * tn * shards == n, (name, n, tn, shards)
    nc = 512 if tn % 512 == 0 else (256 if tn % 256 == 0 else 128)
    ks = k // shards

    def body(a_ref, b_ref, *rest):
        for n0 in range(0, tn, nc):
            if mode == "nn":
                acc = jnp.dot(a_ref[...], b_ref[:, n0:n0 + nc], preferred_element_type=f32)
            elif mode == "nt" and shards > 1:
                acc = jnp.zeros((tm, nc), f32)
                for d in range(shards):
                    acc = acc + lax.dot_general(a_ref[:, d * ks:(d + 1) * ks], b_ref[d, n0:n0 + nc, :], (((1,), (1,)), ((), ())),
                                                preferred_element_type=f32)
            elif mode == "nt":
                acc = lax.dot_general(a_ref[...], b_ref[n0:n0 + nc, :], (((1,), (1,)), ((), ())), preferred_element_type=f32)
            else:
                acc = lax.dot_general(a_ref[...], b_ref[:, n0:n0 + nc], (((0,), (0,)), ((), ())), preferred_element_type=f32)
            if epi == "relu2":
                relu_ref, act_ref = rest
                r = jnp.maximum(acc, 0.0)
                relu_ref[:, n0:n0 + nc] = r.astype(bf16)
                act_ref[:, n0:n0 + nc] = (r * r).astype(bf16)
            elif epi == "relu2_bwd":
                relu_ref, o_ref = rest
                o_ref[:, n0:n0 + nc] = (acc * (2.0 * relu_ref[:, n0:n0 + nc].astype(f32))).astype(bf16)
            else:
                (o_ref,) = rest
                o_ref[:, n0:n0 + nc] = acc.astype(out_dtype)

    if mode == "tn":
        a_spec = pl.BlockSpec((k, tm), lambda j, i: (0, i))
    else:
        a_spec = pl.BlockSpec((tm, k), lambda j, i: (i, 0))
    if mode == "nt" and shards > 1:
        b_spec = pl.BlockSpec((shards, tn, ks), lambda j, i: (0, j, 0))
    elif mode == "nt":
        b_spec = pl.BlockSpec((tn, k), lambda j, i: (j, 0))
    elif mode == "nn" and shards > 1:
        b_spec = pl.BlockSpec((None, k, tn), lambda j, i: (j // per, 0, j % per))
    else:
        b_spec = pl.BlockSpec((k, tn), lambda j, i: (0, j))
    o_spec = pl.BlockSpec((tm, tn), lambda j, i: (i, j))
    in_specs = [a_spec, b_spec]
    args = [a, b]
    if epi == "relu2":
        out_shape = (jax.ShapeDtypeStruct((m, n), bf16), jax.ShapeDtypeStruct((m, n), bf16))
        out_specs = (o_spec, o_spec)
    elif epi == "relu2_bwd":
        in_specs.append(o_spec)
        args.append(extra)
        out_shape = jax.ShapeDtypeStruct((m, n), bf16)
        out_specs = o_spec
    elif mode == "tn" and shards > 1:
        out_shape = jax.ShapeDtypeStruct((shards, m, n // shards), out_dtype)
        out_specs = pl.BlockSpec((None, tm, tn), lambda j, i: (j // per, i, j % per))
    else:
        out_shape = jax.ShapeDtypeStruct((m, n), out_dtype)
        out_specs = o_spec
    single = not isinstance(out_shape, tuple)
    res, moved = _hosted(body, comm, name=name, grid=(n // tn, m // tm), in_specs=in_specs,
                         out_specs=(out_specs,) if single else out_specs, out_shape=(out_shape,) if single else out_shape, args=args)
    res = res[0] if single else res
    return res if comm is None else (res, moved)


def _row_spec(width, col=0):
    return pl.BlockSpec((TOK_BLK, width), lambda i: (i, col))


def _vec_spec(rows, width):
    return pl.BlockSpec((rows, width), lambda i: (0, 0))


def _ln_in(x, g, b, comm=None):
    t, d = x.shape

    def body(x_ref, g_ref, b_ref, h_ref, hb_ref):
        h, _, _ = _ln_fwd(x_ref[...], g_ref[...], b_ref[...])
        h_ref[...] = h
        hb_ref[...] = h.astype(bf16)

    return _hosted(
        body, comm, name="ln_in", grid=(t // TOK_BLK,),
        in_specs=[_row_spec(d), _vec_spec(1, d), _vec_spec(1, d)],
        out_specs=(_row_spec(d), _row_spec(d)),
        out_shape=(jax.ShapeDtypeStruct((t, d), f32), jax.ShapeDtypeStruct((t, d), bf16)),
        args=(x, g, b))


def _attn_post(o_gdn, proj, o_fox, g_gdn, g_fox2, comm=None):
    t = o_gdn.shape[0]

    def body(og_ref, z_ref, of_ref, gg_ref, gf_ref, out_ref):
        for h in range(GDN_HEADS):
            sl = slice(h * LANES, (h + 1) * LANES)
            og = og_ref[:, sl]
            z = z_ref[:, sl]
            r = lax.rsqrt(jnp.mean(og * og, -1, keepdims=True) + NORM_EPS)
            out_ref[:, sl] = (og * r * gg_ref[...] * (z * _sigmoid(z))).astype(bf16)
        lo = _lane((TOK_BLK, LANES)) < FOX_DH
        for pr in range(FOX_HEADS // 2):
            sl = slice(pr * LANES, (pr + 1) * LANES)
            of = of_ref[:, sl]
            sq = of * of
            s0 = jnp.sum(jnp.where(lo, sq, 0.0), -1, keepdims=True)
            s1 = jnp.sum(jnp.where(lo, 0.0, sq), -1, keepdims=True)
            r = lax.rsqrt(jnp.where(lo, s0, s1) * (1.0 / FOX_DH) + NORM_EPS)
            out_ref[:, 512 + pr * LANES:512 + (pr + 1) * LANES] = (of * r * gf_ref[...]).astype(bf16)

    return _hosted(
        body, comm, name="attn_post", grid=(t // TOK_BLK,),
        in_specs=[_row_spec(512), _row_spec(512, OFF_Z // 512), _row_spec(512), _vec_spec(1, LANES), _vec_spec(1, LANES)],
        out_specs=(_row_spec(D_MODEL),),
        out_shape=(jax.ShapeDtypeStruct((t, D_MODEL), bf16),),
        args=(o_gdn, proj, o_fox, g_gdn, g_fox2))


def _attn_post_bwd(dr1b, w_out, o_gdn, proj, o_fox, g_gdn, g_fox2):
    t = o_gdn.shape[0]

    def body(dr_ref, wo_ref, og_ref, z_ref, of_ref, gg_ref, gf_ref, dog_ref, dz_ref, dof_ref, pg_ref):
        i = pl.program_id(0)

        @pl.when(i == 0)
        def _():
            pg_ref[...] = jnp.zeros_like(pg_ref)

        da = _dot_nt(dr_ref[...], wo_ref[...], None)
        dgg = jnp.zeros((1, LANES), f32)
        for h in range(GDN_HEADS):
            sl = slice(h * LANES, (h + 1) * LANES)
            og = og_ref[:, sl]
            z = z_ref[:, sl]
            dout = da[:, sl]
            g = gg_ref[...]
            r = lax.rsqrt(jnp.mean(og * og, -1, keepdims=True) + NORM_EPS)
            sg = _sigmoid(z)
            silu = z * sg
            ng = og * r * g
            dng = dout * silu
            dz_ref[:, sl] = (dout * ng * (sg * (1.0 + z * (1.0 - sg)))).astype(bf16)
            dgg = dgg + jnp.sum(dng * og * r, 0, keepdims=True)
            gd = dng * g
            dog_ref[:, sl] = r * gd - og * (r * r * r) * jnp.mean(og * gd, -1, keepdims=True)
        pg_ref[0:1, :] += dgg
        lo = _lane((TOK_BLK, LANES)) < FOX_DH
        dgf = jnp.zeros((1, LANES), f32)
        for pr in range(FOX_HEADS // 2):
            sl = slice(pr * LANES, (pr + 1) * LANES)
            of = of_ref[:, sl]
            dout = da[:, 512 + pr * LANES:512 + (pr + 1) * LANES]
            g = gf_ref[...]
            sq = of * of
            s0 = jnp.sum(jnp.where(lo, sq, 0.0), -1, keepdims=True)
            s1 = jnp.sum(jnp.where(lo, 0.0, sq), -1, keepdims=True)
            r = lax.rsqrt(jnp.where(lo, s0, s1) * (1.0 / FOX_DH) + NORM_EPS)
            dgf = dgf + jnp.sum(dout * of * r, 0, keepdims=True)
            gd = dout * g
            xg = of * gd
            m0 = jnp.sum(jnp.where(lo, xg, 0.0), -1, keepdims=True)
            m1 = jnp.sum(jnp.where(lo, 0.0, xg), -1, keepdims=True)
            dof_ref[:, sl] = r * gd - of * (r * r * r) * (jnp.where(lo, m0, m1) * (1.0 / FOX_DH))
        pg_ref[1:2, :] += dgf

    return pl.pallas_call(
        body, name="attn_post_bwd", grid=(t // TOK_BLK,),
        in_specs=_product_specs(dr1b, w_out) + [_row_spec(512), _row_spec(512, OFF_Z // 512), _row_spec(512), _vec_spec(1, LANES), _vec_spec(1, LANES)],
        out_specs=(_row_spec(512), _row_spec(512), _row_spec(512), _vec_spec(8, LANES)),
        out_shape=(jax.ShapeDtypeStruct((t, 512), f32), jax.ShapeDtypeStruct((t, 512), bf16),
                   jax.ShapeDtypeStruct((t, 512), f32), jax.ShapeDtypeStruct((8, LANES), f32)),
        compiler_params=_params(("arbitrary",)),
    )(dr1b, w_out, o_gdn, proj, o_fox, g_gdn, g_fox2)


def _product_specs(lhs, rhs):
    return [_row_spec(lhs.shape[1]), pl.BlockSpec(rhs.shape, lambda i: (0, 0))]


def _ln1(h0, lhs, rhs, g, b, comm=None):
    t, d = h0.shape

    def body(h0_ref, lhs_ref, rhs_ref, g_ref, b_ref, h_ref, hb_ref, xh_ref, rs_ref):
        mix = jnp.dot(lhs_ref[...], rhs_ref[...], preferred_element_type=f32)
        h, xhat, rstd = _ln_fwd(ALPHA * h0_ref[...] + mix, g_ref[...], b_ref[...])
        h_ref[...] = h
        hb_ref[...] = h.astype(bf16)
        xh_ref[...] = xhat
        rs_ref[...] = jnp.broadcast_to(rstd, rs_ref.shape)

    return _hosted(
        body, comm, name="ln1", grid=(t // TOK_BLK,),
        in_specs=[_row_spec(d)] + _product_specs(lhs, rhs) + [_vec_spec(1, d), _vec_spec(1, d)],
        out_specs=(_row_spec(d), _row_spec(d), _row_spec(d), _row_spec(LANES)),
        out_shape=(jax.ShapeDtypeStruct((t, d), f32), jax.ShapeDtypeStruct((t, d), bf16),
                   jax.ShapeDtypeStruct((t, d), f32), jax.ShapeDtypeStruct((t, LANES), f32)),
        args=(h0, lhs, rhs, g, b))


def _ln2_loss(h1, lhs, rhs, pb, w_ple, gp, b_gate, g, b, target):
    t, d = h1.shape

    def body(h1_ref, lhs_ref, rhs_ref, pb_ref, wp_ref, gp_ref, bg_ref, g_ref, b_ref, t_ref, dr_ref, drb_ref, dpe_ref, dgp_ref, pg_ref):
        i = pl.program_id(0)

        @pl.when(i == 0)
        def _():
            pg_ref[...] = jnp.zeros_like(pg_ref)

        ff = jnp.dot(lhs_ref[...], rhs_ref[...], preferred_element_type=f32)
        sig = _sigmoid(gp_ref[...] + bg_ref[...])
        pe = jnp.concatenate([jnp.dot(pb_ref[...], wp_ref[s], preferred_element_type=f32) for s in range(w_ple.shape[0])], axis=1)
        r2 = ALPHA * h1_ref[...] + ff + pe * sig
        y, xhat, rstd = _ln_fwd(r2, g_ref[...], b_ref[...])
        err = y - t_ref[...]
        dy = err * (1.0 / d)
        dr = _ln_bwd(dy, xhat, rstd, g_ref[...])
        dr_ref[...] = dr
        drb_ref[...] = dr.astype(bf16)
        dpe_ref[...] = (dr * sig).astype(bf16)
        dgp = dr * pe * sig * (1.0 - sig)
        dgp_ref[...] = dgp.astype(bf16)
        pg_ref[0:1, :] += jnp.sum(dy * xhat, 0, keepdims=True)
        pg_ref[1:2, :] += jnp.sum(dy, 0, keepdims=True)
        pg_ref[2:3, :] += jnp.sum(dgp, 0, keepdims=True)
        pg_ref[3:4, :] += 0.5 * jnp.sum(jnp.mean(err * err, -1, keepdims=True), 0, keepdims=True)

    return pl.pallas_call(
        body, name="ln2_loss", grid=(t // TOK_BLK,),
        in_specs=[_row_spec(d)] + _product_specs(lhs, rhs) + [_row_spec(pb.shape[1]), pl.BlockSpec(w_ple.shape, lambda i: (0, 0, 0)), _row_spec(d)]
        + [_vec_spec(1, d)] * 3 + [_row_spec(d)],
        out_specs=(_row_spec(d), _row_spec(d), _row_spec(d), _row_spec(d), _vec_spec(8, d)),
        out_shape=(jax.ShapeDtypeStruct((t, d), f32), jax.ShapeDtypeStruct((t, d), bf16), jax.ShapeDtypeStruct((t, d), bf16),
                   jax.ShapeDtypeStruct((t, d), bf16), jax.ShapeDtypeStruct((8, d), f32)),
        compiler_params=_params(("arbitrary",)),
    )(h1, lhs, rhs, pb, w_ple, gp, b_gate, g, b, target)


def _ln1_bwd(dr2, dup, w_up, dgp, w_gate, xhat, rstd, g, comm=None):
    t, d = dr2.shape
    ks = w_up.shape[2]

    def body(dr2_ref, dup_ref, wup_ref, dgp_ref, wg_ref, xh_ref, rs_ref, g_ref, dr_ref, drb_ref, pg_ref):
        i = pl.program_id(0)

        @pl.when(i == 0)
        def _():
            pg_ref[...] = jnp.zeros_like(pg_ref)

        dh = ALPHA * dr2_ref[...] + _dot_nt(dgp_ref[...], wg_ref[...], None)
        for s in range(w_up.shape[0]):
            dh = dh + _dot_nt(dup_ref[:, s * ks:(s + 1) * ks], wup_ref[s], None)
        xhat = xh_ref[...]
        dr = _ln_bwd(dh, xhat, rs_ref[:, 0:1], g_ref[...])
        dr_ref[...] = dr
        drb_ref[...] = dr.astype(bf16)
        pg_ref[0:1, :] += jnp.sum(dh * xhat, 0, keepdims=True)
        pg_ref[1:2, :] += jnp.sum(dh, 0, keepdims=True)

    return _hosted(
        body, comm, name="ln1_bwd", grid=(t // TOK_BLK,),
        in_specs=[_row_spec(d), _row_spec(dup.shape[1]), pl.BlockSpec(w_up.shape, lambda i: (0, 0, 0))] + _product_specs(dgp, w_gate)
        + [_row_spec(d), _row_spec(LANES), _vec_spec(1, d)],
        out_specs=(_row_spec(d), _row_spec(d), _vec_spec(8, d)),
        out_shape=(jax.ShapeDtypeStruct((t, d), f32), jax.ShapeDtypeStruct((t, d), bf16), jax.ShapeDtypeStruct((8, d), f32)),
        args=(dr2, dup, w_up, dgp, w_gate, xhat, rstd, g))


def _ln_in_bwd(x, dr1, dmm, g, comm=None):
    t, d = x.shape

    def body(x_ref, dr1_ref, dmm_ref, g_ref, dx_ref, pg_ref):
        i = pl.program_id(0)

        @pl.when(i == 0)
        def _():
            pg_ref[...] = jnp.zeros_like(pg_ref)

        dh = ALPHA * dr1_ref[...] + dmm_ref[...]
        _, xhat, rstd = _ln_fwd(x_ref[...], g_ref[...], 0.0)
        dx_ref[...] = _ln_bwd(dh, xhat, rstd, g_ref[...])
        pg_ref[0:1, :] += jnp.sum(dh * xhat, 0, keepdims=True)
        pg_ref[1:2, :] += jnp.sum(dh, 0, keepdims=True)

    return _hosted(
        body, comm, name="ln_in_bwd", grid=(t // TOK_BLK,),
        in_specs=[_row_spec(d)] * 3 + [_vec_spec(1, d)],
        out_specs=(_row_spec(d), _vec_spec(8, d)),
        out_shape=(jax.ShapeDtypeStruct((t, d), f32), jax.ShapeDtypeStruct((8, d), f32)),
        args=(x, dr1, dmm, g))


def _tri(n, upper=False, strict=False):
    r = lax.broadcasted_iota(jnp.int32, (n, n), 0)
    c = lax.broadcasted_iota(jnp.int32, (n, n), 1)
    if upper:
        m = (c > r) if strict else (c >= r)
    else:
        m = (c < r) if strict else (c <= r)
    return jnp.where(m, 1.0, 0.0).astype(f32)


def _gate_values(x, bias, alog, lane):
    z = x + bias
    return jnp.where(lane < 4, _sigmoid(z), jnp.where(lane < 8, -jnp.exp(alog) * _softplus(z), jnp.where(lane < 16, -_softplus(-z), 0.0)))


def _gates(proj, bias_row, alog_row):
    t = proj.shape[0]
    nch = t // CHUNK

    def body(x_ref, bias_ref, alog_ref, gates_ref, gcum_ref, gcumt_ref):
        lane = _lane((t, LANES))
        gates = _gate_values(x_ref[...], bias_ref[...], alog_ref[...], lane)
        gates_ref[...] = gates
        g3 = gates.reshape(nch, CHUNK, LANES)
        tri = jnp.broadcast_to(_tri(CHUNK)[None], (nch, CHUNK, CHUNK))
        loc = jnp.einsum("bij,bjk->bik", tri, g3, precision=HI, preferred_element_type=f32)
        tot = jnp.sum(g3, axis=1)
        offs = _dot(_tri(nch, strict=True), tot)
        glob = loc + offs[:, None, :]
        lane3 = _lane((nch, CHUNK, LANES))
        gcum = jnp.where(lane3 < 4, g3, jnp.where(lane3 < 8, loc, glob)).reshape(t, LANES)
        gcum_ref[...] = gcum
        gcumt_ref[...] = gcum.T

    return pl.pallas_call(
        body, name="gates", grid=(1,),
        in_specs=[pl.BlockSpec((t, LANES), lambda i: (0, SEG_SMALL // LANES)), _vec_spec(1, LANES), _vec_spec(1, LANES)],
        out_specs=(pl.BlockSpec((t, LANES), lambda i: (0, 0)), pl.BlockSpec((t, LANES), lambda i: (0, 0)),
                   pl.BlockSpec((LANES, t), lambda i: (0, 0))),
        out_shape=(jax.ShapeDtypeStruct((t, LANES), f32), jax.ShapeDtypeStruct((t, LANES), f32), jax.ShapeDtypeStruct((LANES, t), f32)),
        compiler_params=_params(("arbitrary",)),
    )(proj, bias_row, alog_row)


def _gates_bwd(proj, bias_row, alog_row, gates, dgates, dccol, dct):
    t = proj.shape[0]
    nch = t // CHUNK

    def body(x_ref, bias_ref, alog_ref, gates_ref, dg_ref, dcc_ref, dct_ref, dx_ref, pg_ref):
        lane = _lane((t, LANES))
        d = dg_ref[...] + dcc_ref[...] + dct_ref[...].T
        d3 = d.reshape(nch, CHUNK, LANES)
        tri = jnp.broadcast_to(_tri(CHUNK, upper=True)[None], (nch, CHUNK, CHUNK))
        loc = jnp.einsum("bij,bjk->bik", tri, d3, precision=HI, preferred_element_type=f32)
        tot = jnp.sum(d3, axis=1)
        offs = _dot(_tri(nch, upper=True, strict=True), tot)
        glob = loc + offs[:, None, :]
        lane3 = _lane((nch, CHUNK, LANES))
        dpre = jnp.where(lane3 < 4, d3, jnp.where(lane3 < 8, loc, glob)).reshape(t, LANES)
        z = x_ref[...] + bias_ref[...]
        sg = _sigmoid(z)
        dx = jnp.where(lane < 4, dpre * sg * (1.0 - sg),
                       jnp.where(lane < 8, dpre * (-jnp.exp(alog_ref[...])) * sg, jnp.where(lane < 16, dpre * (1.0 - sg), 0.0)))
        dx_ref[...] = dx.astype(bf16)
        pg_ref[...] = jnp.zeros_like(pg_ref)
        pg_ref[0:1, :] = jnp.sum(dx, 0, keepdims=True)
        pg_ref[1:2, :] = jnp.sum(jnp.where((lane >= 4) & (lane < 8), dpre * gates_ref[...], 0.0), 0, keepdims=True)

    full = pl.BlockSpec((t, LANES), lambda i: (0, 0))
    return pl.pallas_call(
        body, name="gates_bwd", grid=(1,),
        in_specs=[pl.BlockSpec((t, LANES), lambda i: (0, SEG_SMALL // LANES)), _vec_spec(1, LANES), _vec_spec(1, LANES),
                  full, full, full, pl.BlockSpec((LANES, t), lambda i: (0, 0))],
        out_specs=(full, _vec_spec(8, LANES)),
        out_shape=(jax.ShapeDtypeStruct((t, LANES), bf16), jax.ShapeDtypeStruct((8, LANES), f32)),
        compiler_params=_params(("arbitrary",)),
    )(proj, bias_row, alog_row, gates, dgates, dccol, dct)


def _conv_act(u, cw, row, t):
    c = cw[3:4, :] * u
    for jj in range(CONV_W - 1):
        sh = CONV_W - 1 - jj
        c = c + cw[jj:jj + 1, :] * jnp.where(row >= sh, pltpu.roll(u, sh, axis=0), 0.0)
    return c


def _gdn_conv(proj, conv_w, comm=None):
    t = proj.shape[0]
    nblk = GDN_QKV // LANES

    def body(u_ref, cw_ref, c_ref, y_ref):
        j = pl.program_id(0)
        row = lax.broadcasted_iota(jnp.int32, (t, LANES), 0)
        c = _conv_act(u_ref[...], cw_ref[...], row, t)
        c_ref[...] = c
        s = c * _sigmoid(c)
        r = lax.rsqrt(jnp.sum(s * s, -1, keepdims=True) + NORM_EPS)
        scale = jnp.where(j < GDN_HEADS, GDN_DK ** -0.5, 1.0)
        y_ref[...] = jnp.where(j < 2 * GDN_HEADS, s * (r * scale), s)

    blk = pl.BlockSpec((t, LANES), lambda j: (0, j))
    return _hosted(
        body, comm, name="gdn_conv", grid=(nblk,),
        in_specs=[blk, pl.BlockSpec((CONV_W, LANES), lambda j: (0, j))],
        out_specs=(blk, blk),
        out_shape=(jax.ShapeDtypeStruct((t, GDN_QKV), f32), jax.ShapeDtypeStruct((t, GDN_QKV), f32)),
        args=(proj, conv_w))


def _gdn_conv_bwd(proj, conv_w, c, dy, comm=None):
    t = proj.shape[0]
    nblk = GDN_QKV // LANES

    def body(u_ref, cw_ref, c_ref, dy_ref, du_ref, dcw_ref):
        j = pl.program_id(0)
        row = lax.broadcasted_iota(jnp.int32, (t, LANES), 0)
        u = u_ref[...]
        cw = cw_ref[...]
        c = c_ref[...]
        dy = dy_ref[...]
        sg = _sigmoid(c)
        s = c * sg
        r = lax.rsqrt(jnp.sum(s * s, -1, keepdims=True) + NORM_EPS)
        n = s * r
        scale = jnp.where(j < GDN_HEADS, GDN_DK ** -0.5, 1.0)
        dn = dy * scale
        ds = jnp.where(j < 2 * GDN_HEADS, r * (dn - n * jnp.sum(dn * n, -1, keepdims=True)), dy)
        dc = ds * (sg * (1.0 + c * (1.0 - sg)))
        du = cw[3:4, :] * dc
        dcw_ref[...] = jnp.zeros_like(dcw_ref)
        dcw_ref[3:4, :] = jnp.sum(dc * u, 0, keepdims=True)
        for jj in range(CONV_W - 1):
            sh = CONV_W - 1 - jj
            du = du + cw[jj:jj + 1, :] * jnp.where(row < t - sh, pltpu.roll(dc, t - sh, axis=0), 0.0)
            dcw_ref[jj:jj + 1, :] = jnp.sum(dc * jnp.where(row >= sh, pltpu.roll(u, sh, axis=0), 0.0), 0, keepdims=True)
        du_ref[...] = du.astype(bf16)

    blk = pl.BlockSpec((t, LANES), lambda j: (0, j))
    return _hosted(
        body, comm, name="gdn_conv_bwd", grid=(nblk,),
        in_specs=[blk, pl.BlockSpec((CONV_W, LANES), lambda j: (0, j)), blk, blk],
        out_specs=(blk, pl.BlockSpec((8, LANES), lambda j: (0, j))),
        out_shape=(jax.ShapeDtypeStruct((t, GDN_QKV), bf16), jax.ShapeDtypeStruct((8, GDN_QKV), f32)),
        args=(proj, conv_w, c, dy))


def _chunk_masks():
    r = lax.broadcasted_iota(jnp.int32, (CHUNK, CHUNK), 0)
    c = lax.broadcasted_iota(jnp.int32, (CHUNK, CHUNK), 1)
    return r >= c, r > c, r == c


def _col_to_row(col, eye):
    return jnp.sum(jnp.where(eye, col, 0.0), axis=0, keepdims=True)


def _row_to_col(row, eye):
    return jnp.sum(jnp.where(eye, row, 0.0), axis=1, keepdims=True)


NN = (((1,), (0,)), ((), ()))
NT = (((1,), (1,)), ((), ()))
TN = (((0,), (0,)), ((), ()))
GDN_GROUP = 4


def _mx(a, b, dims=NN, passes=1):
    d = lambda p, q: lax.dot_general(p, q, dims, preferred_element_type=f32)
    ah, bh = a.astype(bf16), b.astype(bf16)
    if passes == 1:
        return d(ah, bh)
    al = (a - ah.astype(f32)).astype(bf16)
    bl = (b - bh.astype(f32)).astype(bf16)
    return d(ah, bh) + (d(ah, bl) + d(al, bh))


def _gdn_decay(gam, masks):
    causal, _, eye = masks
    return jnp.exp(jnp.where(causal, gam - _col_to_row(gam, eye), NEG))


def _gdn_local(y, gcum, comm=None):
    t = y.shape[0]
    nch = t // CHUNK
    rows_blk = GDN_GROUP * CHUNK

    def body(y_ref, g_ref, u_ref, w_ref, qk_ref, tinv_ref):
        masks = _chunk_masks()
        _, strict, eye = masks
        ids = [(j, h) for j in range(GDN_GROUP) for h in range(GDN_HEADS)]
        rs = lambda j: slice(j * CHUNK, (j + 1) * CHUNK)
        col = lambda base, h: slice(base + h * LANES, base + (h + 1) * LANES)
        kn = [y_ref[rs(j), col(512, h)] for j, h in ids]
        beta = [g_ref[rs(j), h:h + 1] for j, h in ids]
        gam = [g_ref[rs(j), 4 + h:5 + h] for j, h in ids]
        dec = [_gdn_decay(g, masks) for g in gam]
        x = [-jnp.where(strict, _mx(k, k, NT) * d * b, 0.0) for k, d, b in zip(kn, dec, beta)]
        tinv = [jnp.where(eye, 1.0, 0.0) + a for a in x]
        for _ in range(5):
            x = [_mx(a, a, NN, 3) for a in x]
            tinv = [t_ + _mx(t_, a, NN, 3) for t_, a in zip(tinv, x)]
        for (j, h), t_, k, d, b, g in zip(ids, tinv, kn, dec, beta, gam):
            u_ref[rs(j), col(0, h)] = _mx(t_, b * y_ref[rs(j), col(1024, h)])
            w_ref[rs(j), col(0, h)] = _mx(t_, (b * jnp.exp(g)) * k)
            qk_ref[j, h] = _mx(y_ref[rs(j), col(0, h)], k, NT) * d
            tinv_ref[j, h] = t_

    mat = pl.BlockSpec((GDN_GROUP, GDN_HEADS, CHUNK, CHUNK), lambda n: (n, 0, 0, 0))
    return _hosted(
        body, comm, name="gdn_local", grid=(nch // GDN_GROUP,),
        in_specs=[pl.BlockSpec((rows_blk, GDN_QKV), lambda n: (n, 0)), pl.BlockSpec((rows_blk, LANES), lambda n: (n, 0))],
        out_specs=(pl.BlockSpec((rows_blk, 512), lambda n: (n, 0)), pl.BlockSpec((rows_blk, 512), lambda n: (n, 0)), mat, mat),
        out_shape=(jax.ShapeDtypeStruct((t, 512), f32), jax.ShapeDtypeStruct((t, 512), f32),
                   jax.ShapeDtypeStruct((nch, GDN_HEADS, CHUNK, CHUNK), f32), jax.ShapeDtypeStruct((nch, GDN_HEADS, CHUNK, CHUNK), f32)),
        args=(y, gcum))


def _gdn_fwd(y, gcum, u, w, qk, comm=None):
    t = y.shape[0]
    nch = t // CHUNK

    def body(y_ref, g_ref, u_ref, w_ref, qk_ref, o_ref, sall_ref, s_ref):
        @pl.when(pl.program_id(0) == 0)
        def _():
            s_ref[...] = jnp.zeros_like(s_ref)

        heads = range(GDN_HEADS)
        sl = [slice(h * LANES, (h + 1) * LANES) for h in heads]
        gam = [g_ref[:, 4 + h:5 + h] for h in heads]
        gam_last = [g[CHUNK - 1:CHUNK, :] for g in gam]
        s = [s_ref[h] for h in heads]
        for h in heads:
            sall_ref[0, h] = s[h]
        ws = [_mx(w_ref[:, sl[h]], s[h]) for h in heads]
        qs = [_mx(y_ref[:, sl[h]] * jnp.exp(gam[h]), s[h]) for h in heads]
        vn = [u_ref[:, sl[h]] - ws[h] for h in heads]
        av = [_mx(qk_ref[0, h], vn[h]) for h in heads]
        kv = [_mx(y_ref[:, 512 + h * LANES:512 + (h + 1) * LANES] * jnp.exp(gam_last[h] - gam[h]), vn[h], TN) for h in heads]
        for h in heads:
            o_ref[:, sl[h]] = qs[h] + av[h]
            s_ref[h] = jnp.exp(gam_last[h]) * s[h] + kv[h]

    row = lambda width: pl.BlockSpec((CHUNK, width), lambda n: (n, 0))
    return _hosted(
        body, comm, name="gdn_fwd", grid=(nch,),
        in_specs=[row(GDN_QKV), row(LANES), row(512), row(512), pl.BlockSpec((1, GDN_HEADS, CHUNK, CHUNK), lambda n: (n, 0, 0, 0))],
        out_specs=(row(512), pl.BlockSpec((1, GDN_HEADS, LANES, LANES), lambda n: (n, 0, 0, 0))),
        out_shape=(jax.ShapeDtypeStruct((t, 512), f32), jax.ShapeDtypeStruct((nch, GDN_HEADS, LANES, LANES), f32)),
        scratch_shapes=[pltpu.VMEM((GDN_HEADS, LANES, LANES), f32)],
        args=(y, gcum, u, w, qk))


def _gdn_bwd(y, gcum, u_all, w_all, qk_all, tinv_all, sall, do, comm=None):
    t = y.shape[0]
    nch = t // CHUNK

    def body(y_ref, g_ref, u_ref, w_ref, qk_ref, tinv_ref, sall_ref, do_ref, dy_ref, dg_ref, ds_ref):
        @pl.when(pl.program_id(0) == 0)
        def _():
            ds_ref[...] = jnp.zeros_like(ds_ref)

        masks = _chunk_masks()
        causal, strict, eye = masks
        lane = _lane((CHUNK, LANES))
        row = lax.broadcasted_iota(jnp.int32, (CHUNK, 1), 0)
        heads = range(GDN_HEADS)
        each = lambda f, *ls: [f(*a) for a in zip(*ls)]
        rsum = lambda a: jnp.sum(a, axis=1, keepdims=True)
        sl = [slice(h * LANES, (h + 1) * LANES) for h in heads]
        qn = [y_ref[:, sl[h]] for h in heads]
        kn = [y_ref[:, 512 + h * LANES:512 + (h + 1) * LANES] for h in heads]
        v = [y_ref[:, 1024 + h * LANES:1024 + (h + 1) * LANES] for h in heads]
        beta = [g_ref[:, h:h + 1] for h in heads]
        gam = [g_ref[:, 4 + h:5 + h] for h in heads]
        gam_last = [g[CHUNK - 1:CHUNK, :] for g in gam]
        dec = [_gdn_decay(g, masks) for g in gam]
        e = [jnp.exp(g) for g in gam]
        f = each(lambda gl_, g: jnp.exp(gl_ - g), gam_last, gam)
        gl = [jnp.exp(g) for g in gam_last]
        u = [u_ref[:, sl[h]] for h in heads]
        w = [w_ref[:, sl[h]] for h in heads]
        qk = [qk_ref[0, h] for h in heads]
        tinv = [tinv_ref[0, h] for h in heads]
        s = [sall_ref[0, h] for h in heads]
        dsn = [ds_ref[h] for h in heads]
        d_o = [do_ref[:, sl[h]] for h in heads]
        qd = each(lambda a, b: a * b, qn, e)
        kd = each(lambda a, b: a * b, kn, f)
        ws = each(_mx, w, s)
        kds = each(_mx, kd, dsn)
        qkdo = each(lambda a, b: _mx(a, b, TN), qk, d_o)
        dqd = each(lambda a, b: _mx(a, b, NT), d_o, s)
        qddo = each(lambda a, b: _mx(a, b, TN), qd, d_o)
        kkd = each(lambda k, d: _mx(k, k, NT) * d, kn, dec)
        vn = each(lambda a, b: a - b, u, ws)
        dvn = each(lambda a, b: a + b, qkdo, kds)
        dqk = each(lambda a, b: jnp.where(causal, _mx(a, b, NT), 0.0), d_o, vn)
        dkd = each(lambda a, b: _mx(a, b, NT), vn, dsn)
        dw = each(lambda a, b: -_mx(a, b, NT), dvn, s)
        wdvn = each(lambda a, b: _mx(a, b, TN), w, dvn)
        dgl = each(lambda a, b: jnp.sum(rsum(a * b), axis=0, keepdims=True), dsn, s)
        for h in heads:
            ds_ref[h] = qddo[h] - wdvn[h] + gl[h] * dsn[h]
        dru = each(lambda a, b: _mx(a, b, TN), tinv, dvn)
        drw = each(lambda a, b: _mx(a, b, TN), tinv, dw)
        dqkr = each(lambda a, b: a * b, dqk, dec)
        dq1 = each(_mx, dqkr, kn)
        dk1 = each(lambda a, b: _mx(a, b, TN), dqkr, qn)
        dnu = each(lambda a, b: _mx(a, b, NT), dru, u)
        dnw = each(lambda a, b: _mx(a, b, NT), drw, w)
        dn = each(lambda a, b: jnp.where(strict, -(a + b), 0.0), dnu, dnw)
        dkk = each(lambda a, b, d: a * b * d, dn, beta, dec)
        dk2 = each(_mx, dkk, kn)
        dk3 = each(lambda a, b: _mx(a, b, TN), dkk, kn)
        dgates = jnp.zeros((CHUNK, LANES), f32)
        for h in heads:
            drw_k = rsum(drw[h] * kn[h])
            dbeta = rsum(dru[h] * v[h]) + e[h] * drw_k + rsum(dn[h] * kkd[h])
            m = dn[h] * (kkd[h] * beta[h]) + dqk[h] * qk[h]
            de = beta[h] * drw_k + rsum(dqd[h] * qn[h])
            df = rsum(dkd[h] * kn[h])
            dgam = rsum(m) - _row_to_col(jnp.sum(m, axis=0, keepdims=True), eye) + de * e[h] - df * f[h]
            dgam_last = jnp.sum(df * f[h], axis=0, keepdims=True) + dgl[h] * gl[h]
            dgam = dgam + jnp.where(row == CHUNK - 1, dgam_last, 0.0)
            dy_ref[:, sl[h]] = dq1[h] + dqd[h] * e[h]
            dy_ref[:, 512 + h * LANES:512 + (h + 1) * LANES] = (beta[h] * e[h]) * drw[h] + dk2[h] + dk3[h] + dk1[h] + dkd[h] * f[h]
            dy_ref[:, 1024 + h * LANES:1024 + (h + 1) * LANES] = beta[h] * dru[h]
            dgates = dgates + jnp.where(lane == h, dbeta, 0.0) + jnp.where(lane == 4 + h, dgam, 0.0)
        dg_ref[...] = dgates

    rev = lambda width: pl.BlockSpec((CHUNK, width), lambda n: (nch - 1 - n, 0))
    mat = lambda d: pl.BlockSpec((1, GDN_HEADS, d, d), lambda n: (nch - 1 - n, 0, 0, 0))
    return _hosted(
        body, comm, name="gdn_bwd", grid=(nch,),
        in_specs=[rev(GDN_QKV), rev(LANES), rev(512), rev(512), mat(CHUNK), mat(CHUNK), mat(LANES), rev(512)],
        out_specs=(rev(GDN_QKV), rev(LANES)),
        out_shape=(jax.ShapeDtypeStruct((t, GDN_QKV), f32), jax.ShapeDtypeStruct((t, LANES), f32)),
        scratch_shapes=[pltpu.VMEM((GDN_HEADS, LANES, LANES), f32)],
        args=(y, gcum, u_all, w_all, qk_all, tinv_all, sall, do))


FOX_CLASSES = 4


def _fox_groups(t):
    nq = t // FOX_BQ
    ncls = min(FOX_CLASSES, nq)
    per = nq // ncls
    return [(g * per, per, (g + 1) * per * FOX_BQ) for g in range(ncls)]


def _fox_causal(i, keys):
    rows = i * FOX_BQ + lax.broadcasted_iota(jnp.int32, (FOX_BQ, keys), 0)
    return lax.broadcasted_iota(jnp.int32, (FOX_BQ, keys), 1) <= rows


def _fox_scores(q_ref, k_ref, gcumt_ref, h, causal):
    pr = h // 2
    lo = (h % 2) * FOX_DH
    lane = _lane((FOX_BQ, LANES))
    mask = (lane >= lo) & (lane < lo + FOX_DH)
    qm = jnp.where(mask, q_ref[:, pr * LANES:(pr + 1) * LANES] * (FOX_DH ** -0.5), 0.0).astype(bf16)
    kp = k_ref[:, pr * LANES:(pr + 1) * LANES].astype(bf16)
    s = _dot_nt(qm, kp, None) - gcumt_ref[8 + h:9 + h, :]
    return jnp.where(causal, s, NEG), mask, qm, kp


def _fox_fwd(proj, gcumt, ride=None):
    c0 = SEG_FOX // 512

    def group_call(q0, nq, keys, comm):
        def body(q_ref, k_ref, v_ref, gcumt_ref, o_ref, lse_ref):
            causal = _fox_causal(q0 + pl.program_id(0), keys)
            lane = _lane((FOX_BQ, LANES))
            lse_all = jnp.zeros((FOX_BQ, LANES), f32)
            for pr in range(FOX_HEADS // 2):
                vp = v_ref[:, pr * LANES:(pr + 1) * LANES].astype(bf16)
                o_pair = jnp.zeros((FOX_BQ, LANES), f32)
                for h in (2 * pr, 2 * pr + 1):
                    s, mask, _, _ = _fox_scores(q_ref, k_ref, gcumt_ref, h, causal)
                    m = jnp.max(s, axis=1, keepdims=True)
                    p = jnp.exp(s - m)
                    l = jnp.sum(p, axis=1, keepdims=True)
                    o_h = _dot(p.astype(bf16), vp, None) * (1.0 / l)
                    o_pair = jnp.where(mask, o_h, o_pair)
                    lse_all = jnp.where(lane == h, m + jnp.log(l), lse_all)
                o_ref[:, pr * LANES:(pr + 1) * LANES] = o_pair
            lse_ref[...] = lse_all

        seen = lambda col: pl.BlockSpec((keys, 512), lambda i: (0, col))
        return _hosted(
            body, comm, name=f"fox_fwd_{keys}", grid=(nq,),
            in_specs=[pl.BlockSpec((FOX_BQ, 512), lambda i: (q0 + i, c0)), seen(c0 + 1), seen(c0 + 2),
                      pl.BlockSpec((LANES, keys), lambda i: (0, 0))],
            out_specs=(pl.BlockSpec((FOX_BQ, 512), lambda i: (i, 0)), pl.BlockSpec((FOX_BQ, LANES), lambda i: (i, 0))),
            out_shape=(jax.ShapeDtypeStruct((nq * FOX_BQ, 512), f32), jax.ShapeDtypeStruct((nq * FOX_BQ, LANES), f32)),
            args=(proj, proj, proj, gcumt))

    parts = []
    for n, g in enumerate(_fox_groups(proj.shape[0])):
        hook = ride(n) if ride else None
        part, moved = group_call(*g, hook[0] if hook else None)
        parts.append(part)
        if hook:
            hook[1](moved)
    return jnp.concatenate([o for o, _ in parts], axis=0), jnp.concatenate([l for _, l in parts], axis=0)


def _fox_bwd(proj, gcumt, o, lse, do, ride=None):
    t = proj.shape[0]
    c0 = SEG_FOX // 512

    def group_call(q0, nq, keys, acc, comm):
        first = acc is None

        def body(q_ref, k_ref, v_ref, gcumt_ref, o_ref, lse_ref, do_ref, *rest):
            dq_ref, dk_ref, dv_ref, dcc_ref, dct_ref = rest[-5:]
            j = pl.program_id(0)
            causal = _fox_causal(q0 + j, keys)

            @pl.when(j == 0)
            def _():
                if first:
                    dk_ref[...] = jnp.zeros_like(dk_ref)
                    dv_ref[...] = jnp.zeros_like(dv_ref)
                    dct_ref[...] = jnp.zeros_like(dct_ref)
                else:
                    dk_ref[...], dv_ref[...], dct_ref[...] = rest[0][...], rest[1][...], rest[2][...]

            lane = _lane((FOX_BQ, LANES))
            dcc = jnp.zeros((FOX_BQ, LANES), f32)
            scale = FOX_DH ** -0.5
            for pr in range(FOX_HEADS // 2):
                sl = slice(pr * LANES, (pr + 1) * LANES)
                vp = v_ref[:, sl].astype(bf16)
                dq_pair = jnp.zeros((FOX_BQ, LANES), f32)
                for h in (2 * pr, 2 * pr + 1):
                    s, mask, qm, kp = _fox_scores(q_ref, k_ref, gcumt_ref, h, causal)
                    p = jnp.exp(s - lse_ref[:, h:h + 1])
                    dom = jnp.where(mask, do_ref[:, sl], 0.0)
                    delta = jnp.sum(dom * o_ref[:, sl], axis=1, keepdims=True)
                    domb = dom.astype(bf16)
                    ds = p * (_dot_nt(domb, vp, None) - delta)
                    dsb = ds.astype(bf16)
                    dv_ref[:, sl] += _dot_tn(p.astype(bf16), domb, None)
                    dk_ref[:, sl] += _dot_tn(dsb, qm, None)
                    dq_pair = jnp.where(mask, _dot(dsb, kp, None) * scale, dq_pair)
                    dcc = jnp.where(lane == 8 + h, jnp.sum(ds, axis=1, keepdims=True), dcc)
                    dct_ref[8 + h:9 + h, :] += -jnp.sum(ds, axis=0, keepdims=True)
                dq_ref[:, sl] = dq_pair.astype(bf16)
            dcc_ref[...] = dcc

        qblk = lambda col: pl.BlockSpec((FOX_BQ, 512), lambda i: (q0 + i, col))
        oblk = pl.BlockSpec((FOX_BQ, 512), lambda i: (i, 0))
        seen = lambda col: pl.BlockSpec((keys, 512), lambda i: (0, col))
        rblk = pl.BlockSpec((FOX_BQ, LANES), lambda i: (q0 + i, 0))
        seen_t = pl.BlockSpec((LANES, keys), lambda i: (0, 0))
        in_specs = [qblk(c0), seen(c0 + 1), seen(c0 + 2), seen_t, qblk(0), rblk, qblk(0)]
        args = [proj, proj, proj, gcumt, o, lse, do]
        aliases = {}
        if not first:
            in_specs += [seen(0), seen(0), seen_t]
            args += list(acc)
            aliases = {7: 1, 8: 2, 9: 4}
        return _hosted(
            body, comm, name=f"fox_bwd_{keys}", grid=(nq,), in_specs=in_specs,
            out_specs=(oblk, seen(0), seen(0), pl.BlockSpec((FOX_BQ, LANES), lambda i: (i, 0)), seen_t),
            out_shape=(jax.ShapeDtypeStruct((nq * FOX_BQ, 512), bf16), jax.ShapeDtypeStruct((t, 512), f32), jax.ShapeDtypeStruct((t, 512), f32),
                       jax.ShapeDtypeStruct((nq * FOX_BQ, LANES), f32), jax.ShapeDtypeStruct((LANES, t), f32)),
            aliases=aliases, args=args)

    acc, dqs, dccs = None, [], []
    for n, g in enumerate(reversed(_fox_groups(t))):
        hook = ride(n) if ride else None
        (dq, dk, dv, dcc, dct), moved = group_call(*g, acc, hook[0] if hook else None)
        if hook:
            hook[1](moved)
        acc = (dk, dv, dct)
        dqs.insert(0, dq)
        dccs.insert(0, dcc)
    return jnp.concatenate(dqs, axis=0), acc[0], acc[1], jnp.concatenate(dccs, axis=0), acc[2]


def _row(v, width=None):
    v = v.reshape(1, -1).astype(f32)
    if width is not None and v.shape[1] < width:
        v = jnp.pad(v, ((0, 0), (0, width - v.shape[1])))
    return v


LATE = ("w_out", "w_up", "w_ple_gate", "w_ple", "w_down")


def _device_grads(x, p, target, small, w_cat, conv_w, late, qc=None, tail=None, ln_in_out=None):
    z4 = jnp.zeros((4,), f32)
    bias_row = _row(jnp.concatenate([z4, small["dt_bias"].reshape(-1), small["b_f"].reshape(-1)]), LANES)
    alog_row = _row(jnp.concatenate([z4, small["a_log"].reshape(-1)]), LANES)
    g_gdn = _row(small["gdn_norm_g"])
    g_fox2 = _row(jnp.tile(small["fox_norm_g"].reshape(-1), 2))
    pb = p.astype(bf16)
    late = list(late)
    comm = qc is not None

    h0, h0b = ln_in_out if ln_in_out is not None else _ln_in(x, _row(small["ln_in_g"]), _row(small["ln_in_b"]))[0]
    proj = _mm(h0b, w_cat, "nt", 512, D_CAT, "mm_proj")
    gates, gcum, gcumt = _gates(proj, bias_row, alog_row)
    w_down_pieces = [(4, 0, 1)]

    def gather(phase, pieces):
        if not comm or not pieces:
            return None, lambda moved: None
        touched = sorted({i for i, _, _ in pieces})

        def took(moved):
            for i, buf in zip(touched, moved):
                late[i] = buf
        return phase([late[i] for i in touched], [(touched.index(i), k, n) for i, k, n in pieces]), took

    over, on = _gather_chips, _gather_pass_on
    cm, took = gather(over, [(0, 0, 1), (3, 0, 1)])
    (conv_c, qkv_n), moved = _gdn_conv(proj, conv_w, cm)
    took(moved)
    cm, took = gather(over, [(1, 0, 2)])
    (gu, gw, gqk, gtinv), moved = _gdn_local(qkv_n, gcum, cm)
    took(moved)
    cm, took = gather(over, [(1, 1, 2)])
    (o_gdn, sall), moved = _gdn_fwd(qkv_n, gcum, gu, gw, gqk, cm)
    took(moved)
    fox_plan = [(over, []), (on, [(0, 0, 1), (3, 0, 1), (1, 0, 2), (1, 1, 2)]), (over, [(2, 0, 1)]), (over, [(4, 0, 4)])]
    assert not comm or len(_fox_groups(x.shape[0])) == len(fox_plan)
    o_fox, lse = _fox_fwd(proj, gcumt, (lambda n: gather(*fox_plan[n])) if comm else None)
    cm, took = gather(on, [(2, 0, 1)])
    (attn,), moved = _attn_post(o_gdn, proj, o_fox, g_gdn, g_fox2, cm)
    took(moved)
    w_out = late[0].reshape(D_MODEL, D_MODEL)
    cm, took = gather(over, [(4, 1, 4)])
    (h1, h1b, xhat1, rstd1), moved = _ln1(h0, attn, w_out, _row(small["ln1_g"]), _row(small["ln1_b"]), cm)
    took(moved)
    w_up, w_ple = late[1], late[3]
    cm, took = gather(over, [(4, 1, 2)])
    up_act = _mm(h1b, w_up, "nn", 512, 1024, "mm_up", epi="relu2", shards=N_CHIPS, comm=cm)
    if cm:
        up_act, moved = up_act
        took(moved)
    up, act = up_act
    w_gate = late[2].reshape(D_MODEL, D_MODEL)
    cm, took = gather(on, w_down_pieces)
    gp = _mm(h1b, w_gate, "nn", 512, D_MODEL, "mm_gate", comm=cm)
    if cm:
        gp, moved = gp
        took(moved)
    w_down = late[4].reshape(D_FF, D_MODEL)
    dr2, dr2b, dpe, dgp, pg2 = _ln2_loss(h1, act, w_down, pb, w_ple, gp, _row(small["b_ple_gate"]), _row(small["ln2_g"]),
                                         _row(small["ln2_b"]), target)

    by_dest = lambda g: g.reshape((N_CHIPS, -1, g.shape[-1]))
    g_late = [None] * len(LATE)
    state = dict(from_sibling=[None] * len(LATE), sent=[None] * len(LATE), landing=[None] * len(LATE))
    nothing = (None, lambda moved: None)

    def to_sibling(idx):
        if not comm:
            return nothing

        def took(moved):
            for i, b1 in zip(idx, moved):
                state["from_sibling"][i] = b1
                state["sent"][i] = _add_pair(g_late[i], b1, qc, "add_pair_" + LATE[i])
                state["landing"][i] = _landing([state["sent"][i]])[0]
        return _exchange_pairs([g_late[i] for i in idx]), took

    def to_chips(pieces):
        if not comm:
            return nothing
        touched = sorted({i for i, _, _ in pieces})

        def took(moved):
            for i, b2 in zip(touched, moved):
                state["landing"][i] = b2
        return _exchange_chips([state["sent"][i] for i in touched], [state["landing"][i] for i in touched],
                               [(touched.index(i), k, n) for i, k, n in pieces]), took

    def ride(result, cm, took):
        if cm:
            result, moved = result
            took(moved)
        return result

    dup = _mm(dr2b, w_down, "nt", 512, 2048, "mm_dact", epi="relu2_bwd", extra=up)
    g_late[4] = by_dest(_mm(act, dr2b, "tn", 1024, D_MODEL, "mm_gdown"))
    cm, took = to_sibling([4])
    g_late[1] = ride(_mm(h1b, dup, "tn", 1024, 1024, "mm_gup", shards=N_CHIPS, comm=cm), cm, took)
    g_late[2] = by_dest(_mm(h1b, dgp, "tn", 1024, D_MODEL, "mm_ggate"))
    g_late[3] = _mm(pb, dpe, "tn", D_PLE, D_MODEL // N_CHIPS, "mm_gple", shards=N_CHIPS)
    cm, took = to_chips([(4, 0, 2)])
    (dr1, dr1b, pg1), moved = _ln1_bwd(dr2, dup, w_up, dgp, w_gate, xhat1, rstd1, _row(small["ln1_g"]), cm)
    took(moved)
    g_late[0] = by_dest(_mm(attn, dr1b, "tn", 1024, D_MODEL, "mm_gout"))
    do_gdn, dz, do_fox, pga = _attn_post_bwd(dr1b, w_out, o_gdn, proj, o_fox, g_gdn, g_fox2)
    chip_plan = [[(4, 1, 2), (1, 0, 2)], [(1, 1, 2)], [(0, 0, 1)], [(2, 0, 1), (3, 0, 1)]]

    def gdn_backward():
        cm, took = to_chips(chip_plan[0])
        state["gdn"], moved = _gdn_bwd(qkv_n, gcum, gu, gw, gqk, gtinv, sall, do_gdn, cm)
        took(moved)

    def fox_ride(n):
        if n == 0:
            return to_sibling([1, 0, 2, 3])
        if n == 1:
            gdn_backward()
        return to_chips(chip_plan[n])

    assert not comm or len(_fox_groups(x.shape[0])) == len(chip_plan)
    dfq, dfk, dfv, dccol, dct = _fox_bwd(proj, gcumt, o_fox, lse, do_fox, fox_ride if comm else None)
    if not comm:
        gdn_backward()
    dqkv_n, dgates = state["gdn"]
    dsmall, pgg = _gates_bwd(proj, bias_row, alog_row, gates, dgates, dccol, dct)
    cm = None
    if comm:
        cm = _share_halves([_add_chips(g, b1, b2, qc, "add_chips_" + n)
                            for g, b1, b2, n in zip(g_late, state["from_sibling"], state["landing"], LATE)])
    (du, g_conv8), reduced = _gdn_conv_bwd(proj, conv_w, conv_c, dqkv_n, cm)
    if comm:
        g_late = list(reduced)
    t = x.shape[0]
    dproj = jnp.concatenate([du, dz, dfq, dfk.astype(bf16), dfv.astype(bf16), dsmall, jnp.zeros((t, D_CAT - SEG_SMALL - LANES), bf16)], axis=1)
    g_cat = _mm(dproj, h0b, "tn", 1280, D_MODEL, "mm_gcat")
    cm, took = tail[0](g_cat) if tail else (None, None)
    dh0_mm = _mm(dproj, w_cat, "nn", 512, D_MODEL, "mm_dh0", comm=cm)
    if cm:
        dh0_mm, moved = dh0_mm
        took(moved)
    cm, took = tail[1]() if tail and tail[1] else (None, None)
    (grad_x, pg0), moved = _ln_in_bwd(x, dr1, dh0_mm, _row(small["ln_in_g"]), cm)
    if cm:
        took(moved)

    g_fox = pga[1, :FOX_DH] + pga[1, FOX_DH:]
    small_grads = dict(
        ln_in_g=pg0[0], ln_in_b=pg0[1], ln1_g=pg1[0], ln1_b=pg1[1], b_ple_gate=pg2[2], ln2_g=pg2[0], ln2_b=pg2[1],
        gdn_norm_g=pga[0], fox_norm_g=g_fox, a_log=pgg[1, 4:8], dt_bias=pgg[0, 4:8], b_f=pgg[0, 8:16], loss=pg2[3, 0:1])
    return grad_x, g_cat, g_conv8[:CONV_W], dict(zip(LATE, g_late)), small_grads


ANY = pl.BlockSpec(memory_space=pl.ANY)
CONV_PKT_ROWS = 16


def _mesh_pos():
    return lax.axis_index("x"), lax.axis_index("y"), lax.axis_index("c")


def _other_chips(x, y):
    return [(1 - x, y), (x, 1 - y), (1 - x, 1 - y)]


def _rcopy(src, dst, send_sem, recv_sem, dev):
    return pltpu.make_async_remote_copy(src_ref=src, dst_ref=dst, send_sem=send_sem, recv_sem=recv_sem,
                                        device_id=dev, device_id_type=MESH)


class _Comm:
    def __init__(self, ins, outs, aliases, n_sems, start, finish):
        self.ins, self.outs, self.aliases, self.n_sems, self.start, self.finish = list(ins), list(outs), dict(aliases), n_sems, start, finish


def _hosted(body, comm, *, name, grid, in_specs, out_specs, out_shape, args, scratch_shapes=(), aliases=None):
    n_in, n_out, n_sc = len(in_specs), len(out_specs), len(scratch_shapes)
    k, ko = (len(comm.ins), len(comm.outs)) if comm else (0, 0)

    def kernel_body(*refs):
        o0 = n_in + k
        s0 = o0 + n_out + ko
        if comm:
            cins, couts, (ssem, rsem) = refs[n_in:o0], refs[o0 + n_out:s0], refs[s0 + n_sc:]
            step = pl.program_id(0)
            for d in range(1, len(grid)):
                step = step * grid[d] + pl.program_id(d)

            @pl.when(step == 0)
            def _():
                comm.start(cins, couts, ssem, rsem)

        body(*refs[:n_in], *refs[o0:o0 + n_out], *refs[s0:s0 + n_sc])
        if comm:
            last = 1
            for n in grid:
                last *= n

            @pl.when(step == last - 1)
            def _():
                comm.finish(cins, couts, ssem, rsem)

    io_aliases = dict(aliases or {})
    scratch = list(scratch_shapes)
    if comm:
        io_aliases.update({n_in + i: n_out + j for i, j in comm.aliases.items()})
        scratch += [pltpu.SemaphoreType.DMA((comm.n_sems,)), pltpu.SemaphoreType.DMA((comm.n_sems,))]
    res = pl.pallas_call(
        kernel_body, name=name, grid=grid, in_specs=list(in_specs) + [ANY] * k, out_specs=tuple(out_specs) + (ANY,) * ko,
        out_shape=tuple(out_shape) + tuple(comm.outs if comm else ()), scratch_shapes=scratch, input_output_aliases=io_aliases,
        compiler_params=_params(("arbitrary",) * len(grid)),
    )(*args, *(comm.ins if comm else ()))
    return tuple(res[:n_out]), tuple(res[n_out:])


def _comm_only(phases, name):
    n_in = sum(len(p.ins) for p in phases)

    def body(*refs):
        n_out = sum(len(p.outs) for p in phases)
        sems = refs[n_in + n_out:]
        i0, o0 = 0, n_in
        for j, p in enumerate(phases):
            cins, couts = refs[i0:i0 + len(p.ins)], refs[o0:o0 + len(p.outs)]
            p.start(cins, couts, sems[2 * j], sems[2 * j + 1])
            p.finish(cins, couts, sems[2 * j], sems[2 * j + 1])
            i0 += len(p.ins)
            o0 += len(p.outs)

    aliases, i0, o0 = {}, 0, 0
    for p in phases:
        aliases.update({i0 + i: o0 + j for i, j in p.aliases.items()})
        i0 += len(p.ins)
        o0 += len(p.outs)
    outs = [o for p in phases for o in p.outs]
    res = pl.pallas_call(
        body, name=name, out_shape=tuple(outs), in_specs=[ANY] * n_in, out_specs=(ANY,) * len(outs), input_output_aliases=aliases,
        scratch_shapes=[pltpu.SemaphoreType.DMA((p.n_sems,)) for p in phases for _ in range(2)],
    )(*[a for p in phases for a in p.ins])
    split, o0 = [], 0
    for p in phases:
        split.append(tuple(res[o0:o0 + len(p.outs)]))
        o0 += len(p.outs)
    return split


def _like(arrays):
    return [jax.ShapeDtypeStruct(a.shape, a.dtype) for a in arrays]


def _half(ref, slot, hf, piece=(0, 1)):
    k, n = piece
    rows = ref.shape[1] // 2 // n
    return ref.at[slot, pl.ds((hf * n + k) * rows, rows)]


def _whole_halves(arrays):
    return [(i, 0, 1) for i in range(len(arrays))]


def _gather_chips(bufs, pieces=None, whole=False, base=0):
    nw = len(bufs)
    pieces = _whole_halves(bufs) if pieces is None else pieces
    part = (lambda ref, slot, c, piece: ref.at[slot]) if whole else _half

    def copies(couts):
        x, y, c = _mesh_pos()
        q = 2 * x + y
        for j, (i, k, n) in enumerate(pieces):
            for kc, chip in enumerate(_other_chips(x, y)):
                mine, theirs = part(couts[i], q, c, (k, n)), part(couts[i], 2 * chip[0] + chip[1], c, (k, n))
                yield base + j * 3 + kc, mine, theirs, (*chip, c)

    def start(cins, couts, ssem, rsem):
        for s, mine, _, dev in copies(couts):
            _rcopy(mine, mine, ssem.at[s], rsem.at[s], dev).start()

    def finish(cins, couts, ssem, rsem):
        for s, _, theirs, dev in copies(couts):
            _rcopy(theirs, theirs, ssem.at[s], rsem.at[s], dev).wait_recv()
        for s, mine, _, dev in copies(couts):
            _rcopy(mine, mine, ssem.at[s], rsem.at[s], dev).wait_send()

    return _Comm(bufs, _like(bufs), {i: i for i in range(nw)}, 3 * len(pieces), start, finish)


def _gather_pass_on(bufs, pieces=None, base=0):
    nw = len(bufs)
    pieces = _whole_halves(bufs) if pieces is None else pieces

    def copies(couts):
        x, y, c = _mesh_pos()
        for j, (i, k, n) in enumerate(pieces):
            for kc, chip in enumerate(_other_chips(x, y)):
                slot = 2 * chip[0] + chip[1]
                yield base + j * 3 + kc, _half(couts[i], slot, c, (k, n)), _half(couts[i], slot, 1 - c, (k, n)), (x, y, 1 - c)

    def start(cins, couts, ssem, rsem):
        for s, landed, _, sib in copies(couts):
            _rcopy(landed, landed, ssem.at[s], rsem.at[s], sib).start()

    def finish(cins, couts, ssem, rsem):
        for s, _, passed, sib in copies(couts):
            _rcopy(passed, passed, ssem.at[s], rsem.at[s], sib).wait_recv()
        for s, landed, _, sib in copies(couts):
            _rcopy(landed, landed, ssem.at[s], rsem.at[s], sib).wait_send()

    return _Comm(bufs, _like(bufs), {i: i for i in range(nw)}, 3 * len(pieces), start, finish)


def _gather_now(bufs, packets):
    nb = len(bufs)
    over, on, pk = _gather_chips(bufs), _gather_pass_on(bufs, base=3 * nb), _gather_chips(packets, whole=True, base=6 * nb)

    def start(cins, couts, ssem, rsem):
        over.start(cins[:nb], couts[:nb], ssem, rsem)
        pk.start(cins[nb:], couts[nb:], ssem, rsem)

    def finish(cins, couts, ssem, rsem):
        over.finish(cins[:nb], couts[:nb], ssem, rsem)
        on.start(cins[:nb], couts[:nb], ssem, rsem)
        on.finish(cins[:nb], couts[:nb], ssem, rsem)
        pk.finish(cins[nb:], couts[nb:], ssem, rsem)

    every = list(bufs) + list(packets)
    return _Comm(every, _like(every), {i: i for i in range(len(every))}, 6 * nb + 3 * len(packets), start, finish)


def _exchange_pairs(gs):
    nw = len(gs)

    def copies(cins, couts):
        x, y, c = _mesh_pos()
        for i in range(nw):
            for d in range(N_CHIPS):
                yield i * N_CHIPS + d, _half(cins[i], d, 1 - c), couts[i].at[d], (x, y, 1 - c)

    def start(cins, couts, ssem, rsem):
        for s, src, dst, sib in copies(cins, couts):
            _rcopy(src, dst, ssem.at[s], rsem.at[s], sib).start()

    def finish(cins, couts, ssem, rsem):
        for s, src, dst, sib in copies(cins, couts):
            _rcopy(src, dst, ssem.at[s], rsem.at[s], sib).wait_recv()
        for s, src, dst, sib in copies(cins, couts):
            _rcopy(src, dst, ssem.at[s], rsem.at[s], sib).wait_send()

    outs = [jax.ShapeDtypeStruct((N_CHIPS, g.shape[1] // 2, g.shape[2]), g.dtype) for g in gs]
    return _Comm(gs, outs, {}, N_CHIPS * nw, start, finish)


def _gather_packets(small):
    def peers():
        x, y, c = _mesh_pos()
        for r in range(1, 8):
            fx, fy, fc = (r >> 2) & 1, (r >> 1) & 1, r & 1
            yield r - 1, (1 - x if fx else x, 1 - y if fy else y, 1 - c if fc else c)

    def start(cins, couts, ssem, rsem):
        x, y, c = _mesh_pos()
        mine = couts[0].at[4 * x + 2 * y + c]
        for s, peer in peers():
            _rcopy(mine, mine, ssem.at[s], rsem.at[s], peer).start()

    def finish(cins, couts, ssem, rsem):
        x, y, c = _mesh_pos()
        mine = couts[0].at[4 * x + 2 * y + c]
        for s, peer in peers():
            theirs = couts[0].at[4 * peer[0] + 2 * peer[1] + peer[2]]
            _rcopy(theirs, theirs, ssem.at[s], rsem.at[s], peer).wait_recv()
        for s, peer in peers():
            _rcopy(mine, mine, ssem.at[s], rsem.at[s], peer).wait_send()

    return _Comm([small], _like([small]), {0: 0}, 7, start, finish)


def _exchange_chips(a4s, b2s, pieces=None):
    nw = len(a4s)
    pieces = _whole_halves(a4s) if pieces is None else pieces

    def copies(cins, couts):
        x, y, c = _mesh_pos()
        for j, (i, k, n) in enumerate(pieces):
            rows = a4s[i].shape[1] // n
            part = pl.ds(k * rows, rows)
            for kc, chip in enumerate(_other_chips(x, y)):
                yield j * 3 + kc, cins[i].at[2 * chip[0] + chip[1], part], couts[i].at[kc, part], (*chip, c)

    def start(cins, couts, ssem, rsem):
        for s, src, dst, dev in copies(cins, couts):
            _rcopy(src, dst, ssem.at[s], rsem.at[s], dev).start()

    def finish(cins, couts, ssem, rsem):
        for s, src, dst, dev in copies(cins, couts):
            _rcopy(src, dst, ssem.at[s], rsem.at[s], dev).wait_recv()
        for s, src, dst, dev in copies(cins, couts):
            _rcopy(src, dst, ssem.at[s], rsem.at[s], dev).wait_send()

    return _Comm(list(a4s) + list(b2s), _like(b2s), {nw + i: i for i in range(nw)}, 3 * len(pieces), start, finish)


def _landing(a4s):
    return [lax.empty((3,) + a.shape[1:], a.dtype) for a in a4s]


def _share_halves(rs):
    nw = len(rs)

    def halves(couts, i, hf):
        rows = rs[i].shape[0] // 2
        return couts[i].at[pl.ds(hf * rows, rows)]

    def start(cins, couts, ssem, rsem):
        x, y, c = _mesh_pos()
        for i in range(nw):
            _rcopy(halves(couts, i, c), halves(couts, i, c), ssem.at[i], rsem.at[i], (x, y, 1 - c)).start()

    def finish(cins, couts, ssem, rsem):
        x, y, c = _mesh_pos()
        for i in range(nw):
            _rcopy(halves(couts, i, 1 - c), halves(couts, i, 1 - c), ssem.at[i], rsem.at[i], (x, y, 1 - c)).wait_recv()
        for i in range(nw):
            _rcopy(halves(couts, i, c), halves(couts, i, c), ssem.at[i], rsem.at[i], (x, y, 1 - c)).wait_send()

    return _Comm(rs, _like(rs), {i: i for i in range(nw)}, nw, start, finish)


ADD_ROWS = 256


def _add_pair(g4, b1, qc_idx, name):
    _, half, cols = b1.shape
    rb = ADD_ROWS if half % ADD_ROWS == 0 else half
    nb = half // rb

    def body(qc_ref, g_ref, b_ref, ob_ref):
        ob_ref[...] = (g_ref[...] + b_ref[...]).astype(bf16)

    blk = (1, rb, cols)
    out = pl.BlockSpec(blk, lambda d, i, qc: (d, i, 0))
    return pl.pallas_call(
        body, name=name,
        grid_spec=pltpu.PrefetchScalarGridSpec(
            num_scalar_prefetch=1, grid=(N_CHIPS, nb),
            in_specs=[pl.BlockSpec(blk, lambda d, i, qc: (d, qc[1] * nb + i, 0)), out],
            out_specs=out),
        out_shape=jax.ShapeDtypeStruct(b1.shape, bf16),
        compiler_params=_params(("parallel", "parallel")),
    )(qc_idx, g4, b1)


def _add_chips(g4, b1, b2, qc_idx, name):
    _, half, cols = b1.shape
    rb = ADD_ROWS if half % ADD_ROWS == 0 else half
    nb = half // rb

    def body(qc_ref, g_ref, s_ref, b_ref, o_ref):
        o_ref[...] = (((g_ref[0] + s_ref[0]) + b_ref[0].astype(f32)) + b_ref[1].astype(f32)) + b_ref[2].astype(f32)

    return pl.pallas_call(
        body, name=name,
        grid_spec=pltpu.PrefetchScalarGridSpec(
            num_scalar_prefetch=1, grid=(nb,),
            in_specs=[pl.BlockSpec((1, rb, cols), lambda i, qc: (qc[0], qc[1] * nb + i, 0)),
                      pl.BlockSpec((1, rb, cols), lambda i, qc: (qc[0], i, 0)), pl.BlockSpec((3, rb, cols), lambda i, qc: (0, i, 0))],
            out_specs=pl.BlockSpec((rb, cols), lambda i, qc: (qc[1] * nb + i, 0))),
        out_shape=jax.ShapeDtypeStruct((2 * half, cols), f32),
        compiler_params=_params(("parallel",)),
    )(qc_idx, g4, b1, b2)


def _adamw_math(w, g, m, v):
    m = ADAM_B1 * m + (1.0 - ADAM_B1) * g
    v = ADAM_B2 * v + (1.0 - ADAM_B2) * (g * g)
    m_hat = m / (1.0 - ADAM_B1 ** ADAM_STEP)
    v_hat = v / (1.0 - ADAM_B2 ** ADAM_STEP)
    return -ADAM_LR * (m_hat / (jnp.sqrt(v_hat) + ADAM_EPS) + ADAM_WD * w), m, v


def _adamw(w, g, m, v, name, comm=None):
    rows = w.shape[0]
    if w.ndim == 3:
        rb = max(r for r in range(1, ADD_ROWS // 4 + 1) if rows % r == 0)
    else:
        rb = ADD_ROWS if rows % ADD_ROWS == 0 else rows

    def body(w_ref, g_ref, m_ref, v_ref, go_ref, d_ref, mo_ref, vo_ref):
        g = g_ref[...]
        go_ref[...] = g
        d_ref[...], mo_ref[...], vo_ref[...] = _adamw_math(w_ref[...], g, m_ref[...], v_ref[...])

    blk = pl.BlockSpec((rb,) + w.shape[1:], lambda i: (i,) + (0,) * (w.ndim - 1))
    return _hosted(body, comm, name=name, grid=(rows // rb,), in_specs=[blk] * 4, out_specs=(blk,) * 4,
                   out_shape=(jax.ShapeDtypeStruct(w.shape, f32),) * 4, args=(w, g, m, v))


def _small_sum_adamw(all_pkts, w, m, v):
    names = [n for n, _, _ in SMALL_LAYOUT if n in w]
    place = {n: (r0, size) for n, r0, size in SMALL_LAYOUT}
    rows_of = lambda size: -(-size // LANES)
    flat = lambda a: a.reshape(1, -1)
    k = len(names)

    def body(*refs):
        a_ref, ins = refs[0], refs[1:1 + 3 * k]
        g_ref, outs = refs[1 + 3 * k], refs[2 + 3 * k:2 + 7 * k]
        packs = refs[2 + 7 * k:]
        g = a_ref[0]
        for r in range(1, 8):
            g = g + a_ref[r]
        g_ref[...] = g
        for kind in range(3):
            packs[kind][...] = jnp.zeros_like(packs[kind])
            for j, n in enumerate(names):
                r0, size = place[n]
                for r in range(rows_of(size)):
                    width = min(LANES, size - r * LANES)
                    packs[kind][r0 + r:r0 + r + 1, 0:width] = ins[kind * k + j][:, r * LANES:r * LANES + width]
        results = (g,) + _adamw_math(packs[0][...], g, packs[1][...], packs[2][...])
        for kind, val in enumerate(results):
            for j, n in enumerate(names):
                r0, size = place[n]
                for r in range(rows_of(size)):
                    width = min(LANES, size - r * LANES)
                    outs[kind * k + j][:, r * LANES:r * LANES + width] = val[r0 + r:r0 + r + 1, 0:width]

    args = [all_pkts] + [flat(d[n]) for d in (w, m, v) for n in names]
    out_shape = [jax.ShapeDtypeStruct(all_pkts.shape[1:], f32)] + [jax.ShapeDtypeStruct((1, place[n][1]), f32) for _ in range(4) for n in names]
    res = pl.pallas_call(body, name="small_sum_adamw", out_shape=tuple(out_shape),
                         scratch_shapes=[pltpu.VMEM(all_pkts.shape[1:], f32)] * 3)(*args)
    by_kind = [{n: res[1 + kind * k + j].reshape(w[n].shape) for j, n in enumerate(names)} for kind in range(4)]
    return res[0], by_kind


SMALL_LAYOUT = (("ln_in_g", 0, 1024), ("ln_in_b", 8, 1024), ("ln1_g", 16, 1024), ("ln1_b", 24, 1024), ("b_ple_gate", 32, 1024),
                ("ln2_g", 40, 1024), ("ln2_b", 48, 1024), ("gdn_norm_g", 56, 128), ("fox_norm_g", 57, 64), ("a_log", 58, 4),
                ("dt_bias", 59, 4), ("b_f", 60, 8), ("loss", 61, 1))
SMALL_CONV_ROW = 64
SMALL_ROWS = 128


def _pack_small(vals, conv=None):
    rows = []
    nxt = 0
    for n, r0, size in SMALL_LAYOUT:
        assert r0 == nxt
        v = vals[n].reshape(-1).astype(f32) if n in vals else jnp.zeros((size,), f32)
        nrows = -(-size // LANES)
        rows.append(jnp.pad(v, (0, nrows * LANES - size)).reshape(nrows, LANES))
        nxt = r0 + nrows
    rows.append(jnp.zeros((SMALL_CONV_ROW - nxt, LANES), f32))
    conv_rows = CONV_W * GDN_QKV // LANES
    rows.append(jnp.zeros((conv_rows, LANES), f32) if conv is None else conv.reshape(conv_rows, LANES))
    rows.append(jnp.zeros((SMALL_ROWS - SMALL_CONV_ROW - conv_rows, LANES), f32))
    return jnp.concatenate(rows, axis=0)


WEIGHTS = ("ln_in_g", "ln_in_b", "w_in", "conv_w", "a_log", "dt_bias", "gdn_norm_g", "b_f", "fox_norm_g", "w_out", "ln1_g", "ln1_b",
           "w_up", "w_down", "w_ple", "w_ple_gate", "b_ple_gate", "ln2_g", "ln2_b")
SMALL_NAMES = tuple(n for n, _, _ in SMALL_LAYOUT if n != "loss")


def kernel(x, p, ln_in_g, ln_in_b, w_in, conv_w, a_log, dt_bias, gdn_norm_g, b_f, fox_norm_g, w_out, ln1_g, ln1_b, w_up, w_down, w_ple, w_ple_gate, b_ple_gate, ln2_g, ln2_b, loss_target, m_ln_in_g, m_ln_in_b, m_w_in, m_conv_w, m_a_log, m_dt_bias, m_gdn_norm_g, m_b_f, m_fox_norm_g, m_w_out, m_ln1_g, m_ln1_b, m_w_up, m_w_down, m_w_ple, m_w_ple_gate, m_b_ple_gate, m_ln2_g, m_ln2_b, v_ln_in_g, v_ln_in_b, v_w_in, v_conv_w, v_a_log, v_dt_bias, v_gdn_norm_g, v_b_f, v_fox_norm_g, v_w_out, v_ln1_g, v_ln1_b, v_w_up, v_w_down, v_w_ple, v_w_ple_gate, v_b_ple_gate, v_ln2_g, v_ln2_b):
    given = dict(locals())
    w = {n: given[n] for n in WEIGHTS}
    m = {n: given["m_" + n] for n in WEIGHTS}
    v = {n: given["v_" + n] for n in WEIGHTS}
    xi, yi, ci = _mesh_pos()
    q = 2 * xi + yi

    def slot_buffer(val, dtype, slots=N_CHIPS, slot=q, rows=None):
        rows = val.shape[0] if rows is None else rows
        return lax.dynamic_update_slice(lax.empty((slots, rows) + val.shape[1:], dtype), val.astype(dtype)[None], (slot, 0, 0))

    shard_cols = D_IN // N_CHIPS
    conv_rows = CONV_W * GDN_QKV // N_CHIPS // LANES
    conv_pkt = jnp.pad(w["conv_w"][0].reshape(-1, LANES), ((0, CONV_PKT_ROWS - conv_rows), (0, 0)))
    ln_in_out, (w_in4, conv_all) = _ln_in(x[0], _row(w["ln_in_g"]), _row(w["ln_in_b"]),
                                          _gather_now([slot_buffer(w["w_in"][0].T, bf16, rows=W_IN_ROWS)], [slot_buffer(conv_pkt, f32)]))
    conv_full = jnp.concatenate([conv_all[d, :conv_rows].reshape(CONV_W, GDN_QKV // N_CHIPS) for d in range(N_CHIPS)], axis=1)
    wi = jnp.concatenate([w_in4[d, :shard_cols] for d in range(N_CHIPS)], axis=0)
    w_cat = jnp.concatenate([wi[:OFF_BETA], wi[OFF_FOX:OFF_F], wi[OFF_BETA:OFF_FOX], wi[OFF_F:],
                             jnp.zeros((D_CAT - D_IN, D_MODEL), bf16)], axis=0)

    small = {n: w[n] for n in SMALL_NAMES}
    qc = jnp.stack([q, ci]).astype(jnp.int32)
    tail_state = {}

    def pairs_phase(gc):
        g_in = jnp.concatenate([gc[:OFF_BETA], gc[SEG_SMALL:SEG_SMALL + 8], gc[SEG_FOX:SEG_SMALL], gc[SEG_SMALL + 8:SEG_SMALL + 16]], axis=0)
        g_in4 = jnp.stack([jnp.pad(g_in[d * shard_cols:(d + 1) * shard_cols], ((0, W_IN_ROWS - shard_cols), (0, 0))) for d in range(N_CHIPS)])

        def took(moved):
            sent = _add_pair(g_in4, moved[0], qc, "add_pair_w_in")
            tail_state.update(g=g_in4, from_sibling=moved[0], sent=[sent], landing=_landing([sent]))
        return _exchange_pairs([g_in4]), took

    grad_x, _, g_conv, g_late, small_g = _device_grads(
        x[0], p[0, 0], loss_target[0], small, w_cat, conv_full, [slot_buffer(w[n][0], bf16) for n in LATE], qc, tail=(pairs_phase, None),
        ln_in_out=ln_in_out)
    packets = _gather_packets(slot_buffer(_pack_small(small_g, g_conv), f32, 8, 4 * xi + 2 * yi + ci))
    (b2,), (small_all,) = _comm_only([_exchange_chips(tail_state["sent"], tail_state["landing"]), packets], "exchange_chips_w_in")
    (g_late["w_in"],), = _comm_only(
        [_share_halves([_add_chips(tail_state["g"], tail_state["from_sibling"], b2, qc, "add_chips_w_in")])], "share_w_in")

    grads, delta, new_m, new_v = {}, {}, {}, {}
    for n, g in g_late.items():
        if n == "w_in":
            as_stored = lambda a: jnp.transpose(a, (2, 0, 1))
            outs, _ = _adamw(as_stored(w[n]), g[:shard_cols].reshape(shard_cols, 1, D_MODEL), as_stored(m[n]), as_stored(v[n]), "adamw_" + n)
            grads[n], delta[n], new_m[n], new_v[n] = (jnp.transpose(a, (1, 2, 0)) for a in outs)
        else:
            outs, _ = _adamw(w[n][0], g, m[n][0], v[n][0], "adamw_" + n)
            grads[n], delta[n], new_m[n], new_v[n] = (a.reshape(w[n].shape) for a in outs)
    pick = lambda d: {n: d[n] for n in SMALL_NAMES}
    g_pkt, by_kind = _small_sum_adamw(small_all, pick(w), pick(m), pick(v))
    for dst, vals in zip((grads, delta, new_m, new_v), by_kind):
        dst.update(vals)
    conv_rows_all = CONV_W * GDN_QKV // LANES
    conv_g_full = g_pkt[SMALL_CONV_ROW:SMALL_CONV_ROW + conv_rows_all].reshape(CONV_W, GDN_QKV)
    conv_g = lax.dynamic_slice_in_dim(conv_g_full, q * (GDN_QKV // N_CHIPS), GDN_QKV // N_CHIPS, axis=1)
    outs, _ = _adamw(w["conv_w"][0], conv_g, m["conv_w"][0], v["conv_w"][0], "adamw_conv_w")
    grads["conv_w"], delta["conv_w"], new_m["conv_w"], new_v["conv_w"] = (a.reshape(w["conv_w"].shape) for a in outs)
    loss = g_pkt[61, 0]
    return (loss, grad_x[None], *[grads[n] for n in WEIGHTS], *[delta[n] for n in WEIGHTS],
            *[new_m[n] for n in WEIGHTS], *[new_v[n] for n in WEIGHTS])
```

```python
import functools

import jax
import jax.numpy as jnp
from jax import lax
from jax.experimental import pallas as pl
from jax.experimental.pallas import tpu as pltpu

f32 = jnp.float32
bf16 = jnp.bfloat16
HI = lax.Precision.HIGHEST
MESH = pl.DeviceIdType.MESH

D_MODEL = 1024
CHUNK = 64
GDN_HEADS = 4
GDN_DK = 128
FOX_HEADS = 8
FOX_DH = 64
CONV_W = 4
D_FF = 4096
D_PLE = 256
LN_EPS = 1e-5
NORM_EPS = 1e-6
ALPHA = 2.0 ** 0.25
GDN_QKV = 1536
OFF_Z = 1536
OFF_BETA = 2048
OFF_FOX = 2056
OFF_F = 3592
D_IN = 3600
ADAM_LR = 0.001
ADAM_B1 = 0.9
ADAM_B2 = 0.999
ADAM_EPS = 1e-08
ADAM_WD = 0.01
ADAM_STEP = 10

SEG_FOX = 2048
SEG_SMALL = 3584
D_CAT = 3840
LANES = 128
TOK_BLK = 256
FOX_BQ = 256
VMEM_LIMIT = 56 * 1024 * 1024
NEG = -1e30

N_CHIPS = 4
W_IN_ROWS = 928


def _params(sem=None, **kw):
    return pltpu.CompilerParams(dimension_semantics=sem, vmem_limit_bytes=VMEM_LIMIT, **kw)


def _sigmoid(x):
    return 1.0 / (1.0 + jnp.exp(-x))


def _softplus(x):
    return jnp.maximum(x, 0.0) + jnp.log(1.0 + jnp.exp(-jnp.abs(x)))


def _ln_fwd(x, g, b):
    mu = jnp.mean(x, -1, keepdims=True)
    xc = x - mu
    var = jnp.mean(xc * xc, -1, keepdims=True)
    rstd = lax.rsqrt(var + LN_EPS)
    xhat = xc * rstd
    return xhat * g + b, xhat, rstd


def _ln_bwd(dy, xhat, rstd, g):
    dxh = dy * g
    m1 = jnp.mean(dxh, -1, keepdims=True)
    m2 = jnp.mean(dxh * xhat, -1, keepdims=True)
    return rstd * (dxh - m1 - xhat * m2)


def _dot(a, b, prec=HI):
    return jnp.dot(a, b, precision=prec, preferred_element_type=f32)


def _dot_nt(a, b, prec=HI):
    return lax.dot_general(a, b, (((1,), (1,)), ((), ())), precision=prec, preferred_element_type=f32)


def _dot_tn(a, b, prec=HI):
    return lax.dot_general(a, b, (((0,), (0,)), ((), ())), precision=prec, preferred_element_type=f32)


def _lane(shape):
    return lax.broadcasted_iota(jnp.int32, shape, len(shape) - 1)


def _mm(a, b, mode, tm, tn, name, out_dtype=f32, epi=None, extra=None, shards=1, comm=None):
    if mode == "nn":
        (m, k), n = a.shape, b.shape[-1] * shards
    elif mode == "nt":
        (m, k), n = a.shape, b.shape[-2]
    else:
        (k, m), n = a.shape, b.shape[1]
    assert m % tm == 0 and n % tn == 0, (name, m, n, tm, tn)
    per = (n // shards) // tn
    assert mode == "nt" or per * tn * shards == n, (name, n, tn, shards)
    nc = 512 if tn % 512 == 0 else (256 if tn % 256 == 0 else 128)
    ks = k // shards

    def body(a_ref, b_ref, *rest):
        for n0 in range(0, tn, nc):
            if mode == "nn":
                acc = jnp.dot(a_ref[...], b_ref[:, n0:n0 + nc], preferred_element_type=f32)
            elif mode == "nt" and shards > 1:
                acc = jnp.zeros((tm, nc), f32)
                for d in range(shards):
                    acc = acc + lax.dot_general(a_ref[:, d * ks:(d + 1) * ks], b_ref[d, n0:n0 + nc, :], (((1,), (1,)), ((), ())),
                                                preferred_element_type=f32)
            elif mode == "nt":
                acc = lax.dot_general(a_ref[...], b_ref[n0:n0 + nc, :], (((1,), (1,)), ((), ())), preferred_element_type=f32)
            else:
                acc = lax.dot_general(a_ref[...], b_ref[:, n0:n0 + nc], (((0,), (0,)), ((), ())), preferred_element_type=f32)
            if epi == "relu2":
                relu_ref, act_ref = rest
                r = jnp.maximum(acc, 0.0)
                relu_ref[:, n0:n0 + nc] = r.astype(bf16)
                act_ref[:, n0:n0 + nc] = (r * r).astype(bf16)
            elif epi == "relu2_bwd":
                relu_ref, o_ref = rest
                o_ref[:, n0:n0 + nc] = (acc * (2.0 * relu_ref[:, n0:n0 + nc].astype(f32))).astype(bf16)
            else:
                (o_ref,) = rest
                o_ref[:, n0:n0 + nc] = acc.astype(out_dtype)

    if mode == "tn":
        a_spec = pl.BlockSpec((k, tm), lambda j, i: (0, i))
    else:
        a_spec = pl.BlockSpec((tm, k), lambda j, i: (i, 0))
    if mode == "nt" and shards > 1:
        b_spec = pl.BlockSpec((shards, tn, ks), lambda j, i: (0, j, 0))
    elif mode == "nt":
        b_spec = pl.BlockSpec((tn, k), lambda j, i: (j, 0))
    elif mode == "nn" and shards > 1:
        b_spec = pl.BlockSpec((None, k, tn), lambda j, i: (j // per, 0, j % per))
    else:
        b_spec = pl.BlockSpec((k, tn), lambda j, i: (0, j))
    o_spec = pl.BlockSpec((tm, tn), lambda j, i: (i, j))
    in_specs = [a_spec, b_spec]
    args = [a, b]
    if epi == "relu2":
        out_shape = (jax.ShapeDtypeStruct((m, n), bf16), jax.ShapeDtypeStruct((m, n), bf16))
        out_specs = (o_spec, o_spec)
    elif epi == "relu2_bwd":
        in_specs.append(o_spec)
        args.append(extra)
        out_shape = jax.ShapeDtypeStruct((m, n), bf16)
        out_specs = o_spec
    elif mode == "tn" and shards > 1:
        out_shape = jax.ShapeDtypeStruct((shards, m, n // shards), out_dtype)
        out_specs = pl.BlockSpec((None, tm, tn), lambda j, i: (j // per, i, j % per))
    else:
        out_shape = jax.ShapeDtypeStruct((m, n), out_dtype)
        out_specs = o_spec
    single = not isinstance(out_shape, tuple)
    res, moved = _hosted(body, comm, name=name, grid=(n // tn, m // tm), in_specs=in_specs,
                         out_specs=(out_specs,) if single else out_specs, out_shape=(out_shape,) if single else out_shape, args=args)
    res = res[0] if single else res
    return res if comm is None else (res, moved)


def _row_spec(width, col=0):
    return pl.BlockSpec((TOK_BLK, width), lambda i: (i, col))


def _vec_spec(rows, width):
    return pl.BlockSpec((rows, width), lambda i: (0, 0))


def _ln_in(x, g, b, comm=None):
    t, d = x.shape

    def body(x_ref, g_ref, b_ref, h_ref, hb_ref):
        h, _, _ = _ln_fwd(x_ref[...], g_ref[...], b_ref[...])
        h_ref[...] = h
        hb_ref[...] = h.astype(bf16)

    return _hosted(
        body, comm, name="ln_in", grid=(t // TOK_BLK,),
        in_specs=[_row_spec(d), _vec_spec(1, d), _vec_spec(1, d)],
        out_specs=(_row_spec(d), _row_spec(d)),
        out_shape=(jax.ShapeDtypeStruct((t, d), f32), jax.ShapeDtypeStruct((t, d), bf16)),
        args=(x, g, b))


def _attn_post(o_gdn, proj, o_fox, g_gdn, g_fox2, comm=None):
    t = o_gdn.shape[0]

    def body(og_ref, z_ref, of_ref, gg_ref, gf_ref, out_ref):
        for h in range(GDN_HEADS):
            sl = slice(h * LANES, (h + 1) * LANES)
            og = og_ref[:, sl]
            z = z_ref[:, sl]
            r = lax.rsqrt(jnp.mean(og * og, -1, keepdims=True) + NORM_EPS)
            out_ref[:, sl] = (og * r * gg_ref[...] * (z * _sigmoid(z))).astype(bf16)
        lo = _lane((TOK_BLK, LANES)) < FOX_DH
        for pr in range(FOX_HEADS // 2):
            sl = slice(pr * LANES, (pr + 1) * LANES)
            of = of_ref[:, sl]
            sq = of * of
            s0 = jnp.sum(jnp.where(lo, sq, 0.0), -1, keepdims=True)
            s1 = jnp.sum(jnp.where(lo, 0.0, sq), -1, keepdims=True)
            r = lax.rsqrt(jnp.where(lo, s0, s1) * (1.0 / FOX_DH) + NORM_EPS)
            out_ref[:, 512 + pr * LANES:512 + (pr + 1) * LANES] = (of * r * gf_ref[...]).astype(bf16)

    return _hosted(
        body, comm, name="attn_post", grid=(t // TOK_BLK,),
        in_specs=[_row_spec(512), _row_spec(512, OFF_Z // 512), _row_spec(512), _vec_spec(1, LANES), _vec_spec(1, LANES)],
        out_specs=(_row_spec(D_MODEL),),
        out_shape=(jax.ShapeDtypeStruct((t, D_MODEL), bf16),),
        args=(o_gdn, proj, o_fox, g_gdn, g_fox2))


def _attn_post_bwd(dr1b, w_out, o_gdn, proj, o_fox, g_gdn, g_fox2):
    t = o_gdn.shape[0]

    def body(dr_ref, wo_ref, og_ref, z_ref, of_ref, gg_ref, gf_ref, dog_ref, dz_ref, dof_ref, pg_ref):
        i = pl.program_id(0)

        @pl.when(i == 0)
        def _():
            pg_ref[...] = jnp.zeros_like(pg_ref)

        da = _dot_nt(dr_ref[...], wo_ref[...], None)
        dgg = jnp.zeros((1, LANES), f32)
        for h in range(GDN_HEADS):
            sl = slice(h * LANES, (h + 1) * LANES)
            og = og_ref[:, sl]
            z = z_ref[:, sl]
            dout = da[:, sl]
            g = gg_ref[...]
            r = lax.rsqrt(jnp.mean(og * og, -1, keepdims=True) + NORM_EPS)
            sg = _sigmoid(z)
            silu = z * sg
            ng = og * r * g
            dng = dout * silu
            dz_ref[:, sl] = (dout * ng * (sg * (1.0 + z * (1.0 - sg)))).astype(bf16)
            dgg = dgg + jnp.sum(dng * og * r, 0, keepdims=True)
            gd = dng * g
            dog_ref[:, sl] = r * gd - og * (r * r * r) * jnp.mean(og * gd, -1, keepdims=True)
        pg_ref[0:1, :] += dgg
        lo = _lane((TOK_BLK, LANES)) < FOX_DH
        dgf = jnp.zeros((1, LANES), f32)
        for pr in range(FOX_HEADS // 2):
            sl = slice(pr * LANES, (pr + 1) * LANES)
            of = of_ref[:, sl]
            dout = da[:, 512 + pr * LANES:512 + (pr + 1) * LANES]
            g = gf_ref[...]
            sq = of * of
            s0 = jnp.sum(jnp.where(lo, sq, 0.0), -1, keepdims=True)
            s1 = jnp.sum(jnp.where(lo, 0.0, sq), -1, keepdims=True)
            r = lax.rsqrt(jnp.where(lo, s0, s1) * (1.0 / FOX_DH) + NORM_EPS)
            dgf = dgf + jnp.sum(dout * of * r, 0, keepdims=True)
            gd = dout * g
            xg = of * gd
            m0 = jnp.sum(jnp.where(lo, xg, 0.0), -1, keepdims=True)
            m1 = jnp.sum(jnp.where(lo, 0.0, xg), -1, keepdims=True)
            dof_ref[:, sl] = r * gd - of * (r * r * r) * (jnp.where(lo, m0, m1) * (1.0 / FOX_DH))
        pg_ref[1:2, :] += dgf

    return pl.pallas_call(
        body, name="attn_post_bwd", grid=(t // TOK_BLK,),
        in_specs=_product_specs(dr1b, w_out) + [_row_spec(512), _row_spec(512, OFF_Z // 512), _row_spec(512), _vec_spec(1, LANES), _vec_spec(1, LANES)],
        out_specs=(_row_spec(512), _row_spec(512), _row_spec(512), _vec_spec(8, LANES)),
        out_shape=(jax.ShapeDtypeStruct((t, 512), f32), jax.ShapeDtypeStruct((t, 512), bf16),
                   jax.ShapeDtypeStruct((t, 512), f32), jax.ShapeDtypeStruct((8, LANES), f32)),
        compiler_params=_params(("arbitrary",)),
    )(dr1b, w_out, o_gdn, proj, o_fox, g_gdn, g_fox2)


def _product_specs(lhs, rhs):
    return [_row_spec(lhs.shape[1]), pl.BlockSpec(rhs.shape, lambda i: (0, 0))]


def _ln1(h0, lhs, rhs, g, b, comm=None):
    t, d = h0.shape

    def body(h0_ref, lhs_ref, rhs_ref, g_ref, b_ref, h_ref, hb_ref, xh_ref, rs_ref):
        mix = jnp.dot(lhs_ref[...], rhs_ref[...], preferred_element_type=f32)
        h, xhat, rstd = _ln_fwd(ALPHA * h0_ref[...] + mix, g_ref[...], b_ref[...])
        h_ref[...] = h
        hb_ref[...] = h.astype(bf16)
        xh_ref[...] = xhat
        rs_ref[...] = jnp.broadcast_to(rstd, rs_ref.shape)

    return _hosted(
        body, comm, name="ln1", grid=(t // TOK_BLK,),
        in_specs=[_row_spec(d)] + _product_specs(lhs, rhs) + [_vec_spec(1, d), _vec_spec(1, d)],
        out_specs=(_row_spec(d), _row_spec(d), _row_spec(d), _row_spec(LANES)),
        out_shape=(jax.ShapeDtypeStruct((t, d), f32), jax.ShapeDtypeStruct((t, d), bf16),
                   jax.ShapeDtypeStruct((t, d), f32), jax.ShapeDtypeStruct((t, LANES), f32)),
        args=(h0, lhs, rhs, g, b))


def _ln2_loss(h1, lhs, rhs, pb, w_ple, gp, b_gate, g, b, target):
    t, d = h1.shape

    def body(h1_ref, lhs_ref, rhs_ref, pb_ref, wp_ref, gp_ref, bg_ref, g_ref, b_ref, t_ref, dr_ref, drb_ref, dpe_ref, dgp_ref, pg_ref):
        i = pl.program_id(0)

        @pl.when(i == 0)
        def _():
            pg_ref[...] = jnp.zeros_like(pg_ref)

        ff = jnp.dot(lhs_ref[...], rhs_ref[...], preferred_element_type=f32)
        sig = _sigmoid(gp_ref[...] + bg_ref[...])
        pe = jnp.concatenate([jnp.dot(pb_ref[...], wp_ref[s], preferred_element_type=f32) for s in range(w_ple.shape[0])], axis=1)
        r2 = ALPHA * h1_ref[...] + ff + pe * sig
        y, xhat, rstd = _ln_fwd(r2, g_ref[...], b_ref[...])
        err = y - t_ref[...]
        dy = err * (1.0 / d)
        dr = _ln_bwd(dy, xhat, rstd, g_ref[...])
        dr_ref[...] = dr
        drb_ref[...] = dr.astype(bf16)
        dpe_ref[...] = (dr * sig).astype(bf16)
        dgp = dr * pe * sig * (1.0 - sig)
        dgp_ref[...] = dgp.astype(bf16)
        pg_ref[0:1, :] += jnp.sum(dy * xhat, 0, keepdims=True)
        pg_ref[1:2, :] += jnp.sum(dy, 0, keepdims=True)
        pg_ref[2:3, :] += jnp.sum(dgp, 0, keepdims=True)
        pg_ref[3:4, :] += 0.5 * jnp.sum(jnp.mean(err * err, -1, keepdims=True), 0, keepdims=True)

    return pl.pallas_call(
        body, name="ln2_loss", grid=(t // TOK_BLK,),
        in_specs=[_row_spec(d)] + _product_specs(lhs, rhs) + [_row_spec(pb.shape[1]), pl.BlockSpec(w_ple.shape, lambda i: (0, 0, 0)), _row_spec(d)]
        + [_vec_spec(1, d)] * 3 + [_row_spec(d)],
        out_specs=(_row_spec(d), _row_spec(d), _row_spec(d), _row_spec(d), _vec_spec(8, d)),
        out_shape=(jax.ShapeDtypeStruct((t, d), f32), jax.ShapeDtypeStruct((t, d), bf16), jax.ShapeDtypeStruct((t, d), bf16),
                   jax.ShapeDtypeStruct((t, d), bf16), jax.ShapeDtypeStruct((8, d), f32)),
        compiler_params=_params(("arbitrary",)),
    )(h1, lhs, rhs, pb, w_ple, gp, b_gate, g, b, target)


def _ln1_bwd(dr2, dup, w_up, dgp, w_gate, xhat, rstd, g, comm=None):
    t, d = dr2.shape
    ks = w_up.shape[2]

    def body(dr2_ref, dup_ref, wup_ref, dgp_ref, wg_ref, xh_ref, rs_ref, g_ref, dr_ref, drb_ref, pg_ref):
        i = pl.program_id(0)

        @pl.when(i == 0)
        def _():
            pg_ref[...] = jnp.zeros_like(pg_ref)

        dh = ALPHA * dr2_ref[...] + _dot_nt(dgp_ref[...], wg_ref[...], None)
        for s in range(w_up.shape[0]):
            dh = dh + _dot_nt(dup_ref[:, s * ks:(s + 1) * ks], wup_ref[s], None)
        xhat = xh_ref[...]
        dr = _ln_bwd(dh, xhat, rs_ref[:, 0:1], g_ref[...])
        dr_ref[...] = dr
        drb_ref[...] = dr.astype(bf16)
        pg_ref[0:1, :] += jnp.sum(dh * xhat, 0, keepdims=True)
        pg_ref[1:2, :] += jnp.sum(dh, 0, keepdims=True)

    return _hosted(
        body, comm, name="ln1_bwd", grid=(t // TOK_BLK,),
        in_specs=[_row_spec(d), _row_spec(dup.shape[1]), pl.BlockSpec(w_up.shape, lambda i: (0, 0, 0))] + _product_specs(dgp, w_gate)
        + [_row_spec(d), _row_spec(LANES), _vec_spec(1, d)],
        out_specs=(_row_spec(d), _row_spec(d), _vec_spec(8, d)),
        out_shape=(jax.ShapeDtypeStruct((t, d), f32), jax.ShapeDtypeStruct((t, d), bf16), jax.ShapeDtypeStruct((8, d), f32)),
        args=(dr2, dup, w_up, dgp, w_gate, xhat, rstd, g))


def _ln_in_bwd(x, dr1, dmm, g, comm=None):
    t, d = x.shape

    def body(x_ref, dr1_ref, dmm_ref, g_ref, dx_ref, pg_ref):
        i = pl.program_id(0)

        @pl.when(i == 0)
        def _():
            pg_ref[...] = jnp.zeros_like(pg_ref)

        dh = ALPHA * dr1_ref[...] + dmm_ref[...]
        _, xhat, rstd = _ln_fwd(x_ref[...], g_ref[...], 0.0)
        dx_ref[...] = _ln_bwd(dh, xhat, rstd, g_ref[...])
        pg_ref[0:1, :] += jnp.sum(dh * xhat, 0, keepdims=True)
        pg_ref[1:2, :] += jnp.sum(dh, 0, keepdims=True)

    return _hosted(
        body, comm, name="ln_in_bwd", grid=(t // TOK_BLK,),
        in_specs=[_row_spec(d)] * 3 + [_vec_spec(1, d)],
        out_specs=(_row_spec(d), _vec_spec(8, d)),
        out_shape=(jax.ShapeDtypeStruct((t, d), f32), jax.ShapeDtypeStruct((8, d), f32)),
        args=(x, dr1, dmm, g))


def _tri(n, upper=False, strict=False):
    r = lax.broadcasted_iota(jnp.int32, (n, n), 0)
    c = lax.broadcasted_iota(jnp.int32, (n, n), 1)
    if upper:
        m = (c > r) if strict else (c >= r)
    else:
        m = (c < r) if strict else (c <= r)
    return jnp.where(m, 1.0, 0.0).astype(f32)


def _gate_values(x, bias, alog, lane):
    z = x + bias
    return jnp.where(lane < 4, _sigmoid(z), jnp.where(lane < 8, -jnp.exp(alog) * _softplus(z), jnp.where(lane < 16, -_softplus(-z), 0.0)))


def _gates(proj, bias_row, alog_row):
    t = proj.shape[0]
    nch = t // CHUNK

    def body(x_ref, bias_ref, alog_ref, gates_ref, gcum_ref, gcumt_ref):
        lane = _lane((t, LANES))
        gates = _gate_values(x_ref[...], bias_ref[...], alog_ref[...], lane)
        gates_ref[...] = gates
        g3 = gates.reshape(nch, CHUNK, LANES)
        tri = jnp.broadcast_to(_tri(CHUNK)[None], (nch, CHUNK, CHUNK))
        loc = jnp.einsum("bij,bjk->bik", tri, g3, precision=HI, preferred_element_type=f32)
        tot = jnp.sum(g3, axis=1)
        offs = _dot(_tri(nch, strict=True), tot)
        glob = loc + offs[:, None, :]
        lane3 = _lane((nch, CHUNK, LANES))
        gcum = jnp.where(lane3 < 4, g3, jnp.where(lane3 < 8, loc, glob)).reshape(t, LANES)
        gcum_ref[...] = gcum
        gcumt_ref[...] = gcum.T

    return pl.pallas_call(
        body, name="gates", grid=(1,),
        in_specs=[pl.BlockSpec((t, LANES), lambda i: (0, SEG_SMALL // LANES)), _vec_spec(1, LANES), _vec_spec(1, LANES)],
        out_specs=(pl.BlockSpec((t, LANES), lambda i: (0, 0)), pl.BlockSpec((t, LANES), lambda i: (0, 0)),
                   pl.BlockSpec((LANES, t), lambda i: (0, 0))),
        out_shape=(jax.ShapeDtypeStruct((t, LANES), f32), jax.ShapeDtypeStruct((t, LANES), f32), jax.ShapeDtypeStruct((LANES, t), f32)),
        compiler_params=_params(("arbitrary",)),
    )(proj, bias_row, alog_row)


def _gates_bwd(proj, bias_row, alog_row, gates, dgates, dccol, dct):
    t = proj.shape[0]
    nch = t // CHUNK

    def body(x_ref, bias_ref, alog_ref, gates_ref, dg_ref, dcc_ref, dct_ref, dx_ref, pg_ref):
        lane = _lane((t, LANES))
        d = dg_ref[...] + dcc_ref[...] + dct_ref[...].T
        d3 = d.reshape(nch, CHUNK, LANES)
        tri = jnp.broadcast_to(_tri(CHUNK, upper=True)[None], (nch, CHUNK, CHUNK))
        loc = jnp.einsum("bij,bjk->bik", tri, d3, precision=HI, preferred_element_type=f32)
        tot = jnp.sum(d3, axis=1)
        offs = _dot(_tri(nch, upper=True, strict=True), tot)
        glob = loc + offs[:, None, :]
        lane3 = _lane((nch, CHUNK, LANES))
        dpre = jnp.where(lane3 < 4, d3, jnp.where(lane3 < 8, loc, glob)).reshape(t, LANES)
        z = x_ref[...] + bias_ref[...]
        sg = _sigmoid(z)
        dx = jnp.where(lane < 4, dpre * sg * (1.0 - sg),
                       jnp.where(lane < 8, dpre * (-jnp.exp(alog_ref[...])) * sg, jnp.where(lane < 16, dpre * (1.0 - sg), 0.0)))
        dx_ref[...] = dx.astype(bf16)
        pg_ref[...] = jnp.zeros_like(pg_ref)
        pg_ref[0:1, :] = jnp.sum(dx, 0, keepdims=True)
        pg_ref[1:2, :] = jnp.sum(jnp.where((lane >= 4) & (lane < 8), dpre * gates_ref[...], 0.0), 0, keepdims=True)

    full = pl.BlockSpec((t, LANES), lambda i: (0, 0))
    return pl.pallas_call(
        body, name="gates_bwd", grid=(1,),
        in_specs=[pl.BlockSpec((t, LANES), lambda i: (0, SEG_SMALL // LANES)), _vec_spec(1, LANES), _vec_spec(1, LANES),
                  full, full, full, pl.BlockSpec((LANES, t), lambda i: (0, 0))],
        out_specs=(full, _vec_spec(8, LANES)),
        out_shape=(jax.ShapeDtypeStruct((t, LANES), bf16), jax.ShapeDtypeStruct((8, LANES), f32)),
        compiler_params=_params(("arbitrary",)),
    )(proj, bias_row, alog_row, gates, dgates, dccol, dct)


def _conv_act(u, cw, row, t):
    c = cw[3:4, :] * u
    for jj in range(CONV_W - 1):
        sh = CONV_W - 1 - jj
        c = c + cw[jj:jj + 1, :] * jnp.where(row >= sh, pltpu.roll(u, sh, axis=0), 0.0)
    return c


def _gdn_conv(proj, conv_w, comm=None):
    t = proj.shape[0]
    nblk = GDN_QKV // LANES

    def body(u_ref, cw_ref, c_ref, y_ref):
        j = pl.program_id(0)
        row = lax.broadcasted_iota(jnp.int32, (t, LANES), 0)
        c = _conv_act(u_ref[...], cw_ref[...], row, t)
        c_ref[...] = c
        s = c * _sigmoid(c)
        r = lax.rsqrt(jnp.sum(s * s, -1, keepdims=True) + NORM_EPS)
        scale = jnp.where(j < GDN_HEADS, GDN_DK ** -0.5, 1.0)
        y_ref[...] = jnp.where(j < 2 * GDN_HEADS, s * (r * scale), s)

    blk = pl.BlockSpec((t, LANES), lambda j: (0, j))
    return _hosted(
        body, comm, name="gdn_conv", grid=(nblk,),
        in_specs=[blk, pl.BlockSpec((CONV_W, LANES), lambda j: (0, j))],
        out_specs=(blk, blk),
        out_shape=(jax.ShapeDtypeStruct((t, GDN_QKV), f32), jax.ShapeDtypeStruct((t, GDN_QKV), f32)),
        args=(proj, conv_w))


def _gdn_conv_bwd(proj, conv_w, c, dy, comm=None):
    t = proj.shape[0]
    nblk = GDN_QKV // LANES

    def body(u_ref, cw_ref, c_ref, dy_ref, du_ref, dcw_ref):
        j = pl.program_id(0)
        row = lax.broadcasted_iota(jnp.int32, (t, LANES), 0)
        u = u_ref[...]
        cw = cw_ref[...]
        c = c_ref[...]
        dy = dy_ref[...]
        sg = _sigmoid(c)
        s = c * sg
        r = lax.rsqrt(jnp.sum(s * s, -1, keepdims=True) + NORM_EPS)
        n = s * r
        scale = jnp.where(j < GDN_HEADS, GDN_DK ** -0.5, 1.0)
        dn = dy * scale
        ds = jnp.where(j < 2 * GDN_HEADS, r * (dn - n * jnp.sum(dn * n, -1, keepdims=True)), dy)
        dc = ds * (sg * (1.0 + c * (1.0 - sg)))
        du = cw[3:4, :] * dc
        dcw_ref[...] = jnp.zeros_like(dcw_ref)
        dcw_ref[3:4, :] = jnp.sum(dc * u, 0, keepdims=True)
        for jj in range(CONV_W - 1):
            sh = CONV_W - 1 - jj
            du = du + cw[jj:jj + 1, :] * jnp.where(row < t - sh, pltpu.roll(dc, t - sh, axis=0), 0.0)
            dcw_ref[jj:jj + 1, :] = jnp.sum(dc * jnp.where(row >= sh, pltpu.roll(u, sh, axis=0), 0.0), 0, keepdims=True)
        du_ref[...] = du.astype(bf16)

    blk = pl.BlockSpec((t, LANES), lambda j: (0, j))
    return _hosted(
        body, comm, name="gdn_conv_bwd", grid=(nblk,),
        in_specs=[blk, pl.BlockSpec((CONV_W, LANES), lambda j: (0, j)), blk, blk],
        out_specs=(blk, pl.BlockSpec((8, LANES), lambda j: (0, j))),
        out_shape=(jax.ShapeDtypeStruct((t, GDN_QKV), bf16), jax.ShapeDtypeStruct((8, GDN_QKV), f32)),
        args=(proj, conv_w, c, dy))


def _chunk_masks():
    r = lax.broadcasted_iota(jnp.int32, (CHUNK, CHUNK), 0)
    c = lax.broadcasted_iota(jnp.int32, (CHUNK, CHUNK), 1)
    return r >= c, r > c, r == c


def _col_to_row(col, eye):
    return jnp.sum(jnp.where(eye, col, 0.0), axis=0, keepdims=True)


def _row_to_col(row, eye):
    return jnp.sum(jnp.where(eye, row, 0.0), axis=1, keepdims=True)


NN = (((1,), (0,)), ((), ()))
NT = (((1,), (1,)), ((), ()))
TN = (((0,), (0,)), ((), ()))
GDN_GROUP = 4


def _mx(a, b, dims=NN, passes=1):
    d = lambda p, q: lax.dot_general(p, q, dims, preferred_element_type=f32)
    ah, bh = a.astype(bf16), b.astype(bf16)
    if passes == 1:
        return d(ah, bh)
    al = (a - ah.astype(f32)).astype(bf16)
    bl = (b - bh.astype(f32)).astype(bf16)
    return d(ah, bh) + (d(ah, bl) + d(al, bh))


def _gdn_decay(gam, masks):
    causal, _, eye = masks
    return jnp.exp(jnp.where(causal, gam - _col_to_row(gam, eye), NEG))


def _gdn_local(y, gcum, comm=None):
    t = y.shape[0]
    nch = t // CHUNK
    rows_blk = GDN_GROUP * CHUNK

    def body(y_ref, g_ref, u_ref, w_ref, qk_ref, tinv_ref):
        masks = _chunk_masks()
        _, strict, eye = masks
        ids = [(j, h) for j in range(GDN_GROUP) for h in range(GDN_HEADS)]
        rs = lambda j: slice(j * CHUNK, (j + 1) * CHUNK)
        col = lambda base, h: slice(base + h * LANES, base + (h + 1) * LANES)
        kn = [y_ref[rs(j), col(512, h)] for j, h in ids]
        beta = [g_ref[rs(j), h:h + 1] for j, h in ids]
        gam = [g_ref[rs(j), 4 + h:5 + h] for j, h in ids]
        dec = [_gdn_decay(g, masks) for g in gam]
        x = [-jnp.where(strict, _mx(k, k, NT) * d * b, 0.0) for k, d, b in zip(kn, dec, beta)]
        tinv = [jnp.where(eye, 1.0, 0.0) + a for a in x]
        for _ in range(5):
            x = [_mx(a, a, NN, 3) for a in x]
            tinv = [t_ + _mx(t_, a, NN, 3) for t_, a in zip(tinv, x)]
        for (j, h), t_, k, d, b, g in zip(ids, tinv, kn, dec, beta, gam):
            u_ref[rs(j), col(0, h)] = _mx(t_, b * y_ref[rs(j), col(1024, h)])
            w_ref[rs(j), col(0, h)] = _mx(t_, (b * jnp.exp(g)) * k)
            qk_ref[j, h] = _mx(y_ref[rs(j), col(0, h)], k, NT) * d
            tinv_ref[j, h] = t_

    mat = pl.BlockSpec((GDN_GROUP, GDN_HEADS, CHUNK, CHUNK), lambda n: (n, 0, 0, 0))
    return _hosted(
        body, comm, name="gdn_local", grid=(nch // GDN_GROUP,),
        in_specs=[pl.BlockSpec((rows_blk, GDN_QKV), lambda n: (n, 0)), pl.BlockSpec((rows_blk, LANES), lambda n: (n, 0))],
        out_specs=(pl.BlockSpec((rows_blk, 512), lambda n: (n, 0)), pl.BlockSpec((rows_blk, 512), lambda n: (n, 0)), mat, mat),
        out_shape=(jax.ShapeDtypeStruct((t, 512), f32), jax.ShapeDtypeStruct((t, 512), f32),
                   jax.ShapeDtypeStruct((nch, GDN_HEADS, CHUNK, CHUNK), f32), jax.ShapeDtypeStruct((nch, GDN_HEADS, CHUNK, CHUNK), f32)),
        args=(y, gcum))


def _gdn_fwd(y, gcum, u, w, qk, comm=None):
    t = y.shape[0]
    nch = t // CHUNK

    def body(y_ref, g_ref, u_ref, w_ref, qk_ref, o_ref, sall_ref, s_ref):
        @pl.when(pl.program_id(0) == 0)
        def _():
            s_ref[...] = jnp.zeros_like(s_ref)

        heads = range(GDN_HEADS)
        sl = [slice(h * LANES, (h + 1) * LANES) for h in heads]
        gam = [g_ref[:, 4 + h:5 + h] for h in heads]
        gam_last = [g[CHUNK - 1:CHUNK, :] for g in gam]
        s = [s_ref[h] for h in heads]
        for h in heads:
            sall_ref[0, h] = s[h]
        ws = [_mx(w_ref[:, sl[h]], s[h]) for h in heads]
        qs = [_mx(y_ref[:, sl[h]] * jnp.exp(gam[h]), s[h]) for h in heads]
        vn = [u_ref[:, sl[h]] - ws[h] for h in heads]
        av = [_mx(qk_ref[0, h], vn[h]) for h in heads]
        kv = [_mx(y_ref[:, 512 + h * LANES:512 + (h + 1) * LANES] * jnp.exp(gam_last[h] - gam[h]), vn[h], TN) for h in heads]
        for h in heads:
            o_ref[:, sl[h]] = qs[h] + av[h]
            s_ref[h] = jnp.exp(gam_last[h]) * s[h] + kv[h]

    row = lambda width: pl.BlockSpec((CHUNK, width), lambda n: (n, 0))
    return _hosted(
        body, comm, name="gdn_fwd", grid=(nch,),
        in_specs=[row(GDN_QKV), row(LANES), row(512), row(512), pl.BlockSpec((1, GDN_HEADS, CHUNK, CHUNK), lambda n: (n, 0, 0, 0))],
        out_specs=(row(512), pl.BlockSpec((1, GDN_HEADS, LANES, LANES), lambda n: (n, 0, 0, 0))),
        out_shape=(jax.ShapeDtypeStruct((t, 512), f32), jax.ShapeDtypeStruct((nch, GDN_HEADS, LANES, LANES), f32)),
        scratch_shapes=[pltpu.VMEM((GDN_HEADS, LANES, LANES), f32)],
        args=(y, gcum, u, w, qk))


def _gdn_bwd(y, gcum, u_all, w_all, qk_all, tinv_all, sall, do, comm=None):
    t = y.shape[0]
    nch = t // CHUNK

    def body(y_ref, g_ref, u_ref, w_ref, qk_ref, tinv_ref, sall_ref, do_ref, dy_ref, dg_ref, ds_ref):
        @pl.when(pl.program_id(0) == 0)
        def _():
            ds_ref[...] = jnp.zeros_like(ds_ref)

        masks = _chunk_masks()
        causal, strict, eye = masks
        lane = _lane((CHUNK, LANES))
        row = lax.broadcasted_iota(jnp.int32, (CHUNK, 1), 0)
        heads = range(GDN_HEADS)
        each = lambda f, *ls: [f(*a) for a in zip(*ls)]
        rsum = lambda a: jnp.sum(a, axis=1, keepdims=True)
        sl = [slice(h * LANES, (h + 1) * LANES) for h in heads]
        qn = [y_ref[:, sl[h]] for h in heads]
        kn = [y_ref[:, 512 + h * LANES:512 + (h + 1) * LANES] for h in heads]
        v = [y_ref[:, 1024 + h * LANES:1024 + (h + 1) * LANES] for h in heads]
        beta = [g_ref[:, h:h + 1] for h in heads]
        gam = [g_ref[:, 4 + h:5 + h] for h in heads]
        gam_last = [g[CHUNK - 1:CHUNK, :] for g in gam]
        dec = [_gdn_decay(g, masks) for g in gam]
        e = [jnp.exp(g) for g in gam]
        f = each(lambda gl_, g: jnp.exp(gl_ - g), gam_last, gam)
        gl = [jnp.exp(g) for g in gam_last]
        u = [u_ref[:, sl[h]] for h in heads]
        w = [w_ref[:, sl[h]] for h in heads]
        qk = [qk_ref[0, h] for h in heads]
        tinv = [tinv_ref[0, h] for h in heads]
        s = [sall_ref[0, h] for h in heads]
        dsn = [ds_ref[h] for h in heads]
        d_o = [do_ref[:, sl[h]] for h in heads]
        qd = each(lambda a, b: a * b, qn, e)
        kd = each(lambda a, b: a * b, kn, f)
        ws = each(_mx, w, s)
        kds = each(_mx, kd, dsn)
        qkdo = each(lambda a, b: _mx(a, b, TN), qk, d_o)
        dqd = each(lambda a, b: _mx(a, b, NT), d_o, s)
        qddo = each(lambda a, b: _mx(a, b, TN), qd, d_o)
        kkd = each(lambda k, d: _mx(k, k, NT) * d, kn, dec)
        vn = each(lambda a, b: a - b, u, ws)
        dvn = each(lambda a, b: a + b, qkdo, kds)
        dqk = each(lambda a, b: jnp.where(causal, _mx(a, b, NT), 0.0), d_o, vn)
        dkd = each(lambda a, b: _mx(a, b, NT), vn, dsn)
        dw = each(lambda a, b: -_mx(a, b, NT), dvn, s)
        wdvn = each(lambda a, b: _mx(a, b, TN), w, dvn)
        dgl = each(lambda a, b: jnp.sum(rsum(a * b), axis=0, keepdims=True), dsn, s)
        for h in heads:
            ds_ref[h] = qddo[h] - wdvn[h] + gl[h] * dsn[h]
        dru = each(lambda a, b: _mx(a, b, TN), tinv, dvn)
        drw = each(lambda a, b: _mx(a, b, TN), tinv, dw)
        dqkr = each(lambda a, b: a * b, dqk, dec)
        dq1 = each(_mx, dqkr, kn)
        dk1 = each(lambda a, b: _mx(a, b, TN), dqkr, qn)
        dnu = each(lambda a, b: _mx(a, b, NT), dru, u)
        dnw = each(lambda a, b: _mx(a, b, NT), drw, w)
        dn = each(lambda a, b: jnp.where(strict, -(a + b), 0.0), dnu, dnw)
        dkk = each(lambda a, b, d: a * b * d, dn, beta, dec)
        dk2 = each(_mx, dkk, kn)
        dk3 = each(lambda a, b: _mx(a, b, TN), dkk, kn)
        dgates = jnp.zeros((CHUNK, LANES), f32)
        for h in heads:
            drw_k = rsum(drw[h] * kn[h])
            dbeta = rsum(dru[h] * v[h]) + e[h] * drw_k + rsum(dn[h] * kkd[h])
            m = dn[h] * (kkd[h] * beta[h]) + dqk[h] * qk[h]
            de = beta[h] * drw_k + rsum(dqd[h] * qn[h])
            df = rsum(dkd[h] * kn[h])
            dgam = rsum(m) - _row_to_col(jnp.sum(m, axis=0, keepdims=True), eye) + de * e[h] - df * f[h]
            dgam_last = jnp.sum(df * f[h], axis=0, keepdims=True) + dgl[h] * gl[h]
            dgam = dgam + jnp.where(row == CHUNK - 1, dgam_last, 0.0)
            dy_ref[:, sl[h]] = dq1[h] + dqd[h] * e[h]
            dy_ref[:, 512 + h * LANES:512 + (h + 1) * LANES] = (beta[h] * e[h]) * drw[h] + dk2[h] + dk3[h] + dk1[h] + dkd[h] * f[h]
            dy_ref[:, 1024 + h * LANES:1024 + (h + 1) * LANES] = beta[h] * dru[h]
            dgates = dgates + jnp.where(lane == h, dbeta, 0.0) + jnp.where(lane == 4 + h, dgam, 0.0)
        dg_ref[...] = dgates

    rev = lambda width: pl.BlockSpec((CHUNK, width), lambda n: (nch - 1 - n, 0))
    mat = lambda d: pl.BlockSpec((1, GDN_HEADS, d, d), lambda n: (nch - 1 - n, 0, 0, 0))
    return _hosted(
        body, comm, name="gdn_bwd", grid=(nch,),
        in_specs=[rev(GDN_QKV), rev(LANES), rev(512), rev(512), mat(CHUNK), mat(CHUNK), mat(LANES), rev(512)],
        out_specs=(rev(GDN_QKV), rev(LANES)),
        out_shape=(jax.ShapeDtypeStruct((t, GDN_QKV), f32), jax.ShapeDtypeStruct((t, LANES), f32)),
        scratch_shapes=[pltpu.VMEM((GDN_HEADS, LANES, LANES), f32)],
        args=(y, gcum, u_all, w_all, qk_all, tinv_all, sall, do))


FOX_CLASSES = 4


def _fox_groups(t):
    nq = t // FOX_BQ
    ncls = min(FOX_CLASSES, nq)
    per = nq // ncls
    return [(g * per, per, (g + 1) * per * FOX_BQ) for g in range(ncls)]


def _fox_causal(i, keys):
    rows = i * FOX_BQ + lax.broadcasted_iota(jnp.int32, (FOX_BQ, keys), 0)
    return lax.broadcasted_iota(jnp.int32, (FOX_BQ, keys), 1) <= rows


def _fox_scores(q_ref, k_ref, gcumt_ref, h, causal):
    pr = h // 2
    lo = (h % 2) * FOX_DH
    lane = _lane((FOX_BQ, LANES))
    mask = (lane >= lo) & (lane < lo + FOX_DH)
    qm = jnp.where(mask, q_ref[:, pr * LANES:(pr + 1) * LANES] * (FOX_DH ** -0.5), 0.0).astype(bf16)
    kp = k_ref[:, pr * LANES:(pr + 1) * LANES].astype(bf16)
    s = _dot_nt(qm, kp, None) - gcumt_ref[8 + h:9 + h, :]
    return jnp.where(causal, s, NEG), mask, qm, kp


def _fox_fwd(proj, gcumt, ride=None):
    c0 = SEG_FOX // 512

    def group_call(q0, nq, keys, comm):
        def body(q_ref, k_ref, v_ref, gcumt_ref, o_ref, lse_ref):
            causal = _fox_causal(q0 + pl.program_id(0), keys)
            lane = _lane((FOX_BQ, LANES))
            lse_all = jnp.zeros((FOX_BQ, LANES), f32)
            for pr in range(FOX_HEADS // 2):
                vp = v_ref[:, pr * LANES:(pr + 1) * LANES].astype(bf16)
                o_pair = jnp.zeros((FOX_BQ, LANES), f32)
                for h in (2 * pr, 2 * pr + 1):
                    s, mask, _, _ = _fox_scores(q_ref, k_ref, gcumt_ref, h, causal)
                    m = jnp.max(s, axis=1, keepdims=True)
                    p = jnp.exp(s - m)
                    l = jnp.sum(p, axis=1, keepdims=True)
                    o_h = _dot(p.astype(bf16), vp, None) * (1.0 / l)
                    o_pair = jnp.where(mask, o_h, o_pair)
                    lse_all = jnp.where(lane == h, m + jnp.log(l), lse_all)
                o_ref[:, pr * LANES:(pr + 1) * LANES] = o_pair
            lse_ref[...] = lse_all

        seen = lambda col: pl.BlockSpec((keys, 512), lambda i: (0, col))
        return _hosted(
            body, comm, name=f"fox_fwd_{keys}", grid=(nq,),
            in_specs=[pl.BlockSpec((FOX_BQ, 512), lambda i: (q0 + i, c0)), seen(c0 + 1), seen(c0 + 2),
                      pl.BlockSpec((LANES, keys), lambda i: (0, 0))],
            out_specs=(pl.BlockSpec((FOX_BQ, 512), lambda i: (i, 0)), pl.BlockSpec((FOX_BQ, LANES), lambda i: (i, 0))),
            out_shape=(jax.ShapeDtypeStruct((nq * FOX_BQ, 512), f32), jax.ShapeDtypeStruct((nq * FOX_BQ, LANES), f32)),
            args=(proj, proj, proj, gcumt))

    parts = []
    for n, g in enumerate(_fox_groups(proj.shape[0])):
        hook = ride(n) if ride else None
        part, moved = group_call(*g, hook[0] if hook else None)
        parts.append(part)
        if hook:
            hook[1](moved)
    return jnp.concatenate([o for o, _ in parts], axis=0), jnp.concatenate([l for _, l in parts], axis=0)


def _fox_bwd(proj, gcumt, o, lse, do, ride=None):
    t = proj.shape[0]
    c0 = SEG_FOX // 512

    def group_call(q0, nq, keys, acc, comm):
        first = acc is None

        def body(q_ref, k_ref, v_ref, gcumt_ref, o_ref, lse_ref, do_ref, *rest):
            dq_ref, dk_ref, dv_ref, dcc_ref, dct_ref = rest[-5:]
            j = pl.program_id(0)
            causal = _fox_causal(q0 + j, keys)

            @pl.when(j == 0)
            def _():
                if first:
                    dk_ref[...] = jnp.zeros_like(dk_ref)
                    dv_ref[...] = jnp.zeros_like(dv_ref)
                    dct_ref[...] = jnp.zeros_like(dct_ref)
                else:
                    dk_ref[...], dv_ref[...], dct_ref[...] = rest[0][...], rest[1][...], rest[2][...]

            lane = _lane((FOX_BQ, LANES))
            dcc = jnp.zeros((FOX_BQ, LANES), f32)
            scale = FOX_DH ** -0.5
            for pr in range(FOX_HEADS // 2):
                sl = slice(pr * LANES, (pr + 1) * LANES)
                vp = v_ref[:, sl].astype(bf16)
                dq_pair = jnp.zeros((FOX_BQ, LANES), f32)
                for h in (2 * pr, 2 * pr + 1):
                    s, mask, qm, kp = _fox_scores(q_ref, k_ref, gcumt_ref, h, causal)
                    p = jnp.exp(s - lse_ref[:, h:h + 1])
                    dom = jnp.where(mask, do_ref[:, sl], 0.0)
                    delta = jnp.sum(dom * o_ref[:, sl], axis=1, keepdims=True)
                    domb = dom.astype(bf16)
                    ds = p * (_dot_nt(domb, vp, None) - delta)
                    dsb = ds.astype(bf16)
                    dv_ref[:, sl] += _dot_tn(p.astype(bf16), domb, None)
                    dk_ref[:, sl] += _dot_tn(dsb, qm, None)
                    dq_pair = jnp.where(mask, _dot(dsb, kp, None) * scale, dq_pair)
                    dcc = jnp.where(lane == 8 + h, jnp.sum(ds, axis=1, keepdims=True), dcc)
                    dct_ref[8 + h:9 + h, :] += -jnp.sum(ds, axis=0, keepdims=True)
                dq_ref[:, sl] = dq_pair.astype(bf16)
            dcc_ref[...] = dcc

        qblk = lambda col: pl.BlockSpec((FOX_BQ, 512), lambda i: (q0 + i, col))
        oblk = pl.BlockSpec((FOX_BQ, 512), lambda i: (i, 0))
        seen = lambda col: pl.BlockSpec((keys, 512), lambda i: (0, col))
        rblk = pl.BlockSpec((FOX_BQ, LANES), lambda i: (q0 + i, 0))
        seen_t = pl.BlockSpec((LANES, keys), lambda i: (0, 0))
        in_specs = [qblk(c0), seen(c0 + 1), seen(c0 + 2), seen_t, qblk(0), rblk, qblk(0)]
        args = [proj, proj, proj, gcumt, o, lse, do]
        aliases = {}
        if not first:
            in_specs += [seen(0), seen(0), seen_t]
            args += list(acc)
            aliases = {7: 1, 8: 2, 9: 4}
        return _hosted(
            body, comm, name=f"fox_bwd_{keys}", grid=(nq,), in_specs=in_specs,
            out_specs=(oblk, seen(0), seen(0), pl.BlockSpec((FOX_BQ, LANES), lambda i: (i, 0)), seen_t),
            out_shape=(jax.ShapeDtypeStruct((nq * FOX_BQ, 512), bf16), jax.ShapeDtypeStruct((t, 512), f32), jax.ShapeDtypeStruct((t, 512), f32),
                       jax.ShapeDtypeStruct((nq * FOX_BQ, LANES), f32), jax.ShapeDtypeStruct((LANES, t), f32)),
            aliases=aliases, args=args)

    acc, dqs, dccs = None, [], []
    for n, g in enumerate(reversed(_fox_groups(t))):
        hook = ride(n) if ride else None
        (dq, dk, dv, dcc, dct), moved = group_call(*g, acc, hook[0] if hook else None)
        if hook:
            hook[1](moved)
        acc = (dk, dv, dct)
        dqs.insert(0, dq)
        dccs.insert(0, dcc)
    return jnp.concatenate(dqs, axis=0), acc[0], acc[1], jnp.concatenate(dccs, axis=0), acc[2]


def _row(v, width=None):
    v = v.reshape(1, -1).astype(f32)
    if width is not None and v.shape[1] < width:
        v = jnp.pad(v, ((0, 0), (0, width - v.shape[1])))
    return v


LATE = ("w_out", "w_up", "w_ple_gate", "w_ple", "w_down")


def _device_grads(x, p, target, small, w_cat, conv_w, late, qc=None, tail=None, ln_in_out=None):
    z4 = jnp.zeros((4,), f32)
    bias_row = _row(jnp.concatenate([z4, small["dt_bias"].reshape(-1), small["b_f"].reshape(-1)]), LANES)
    alog_row = _row(jnp.concatenate([z4, small["a_log"].reshape(-1)]), LANES)
    g_gdn = _row(small["gdn_norm_g"])
    g_fox2 = _row(jnp.tile(small["fox_norm_g"].reshape(-1), 2))
    pb = p.astype(bf16)
    late = list(late)
    comm = qc is not None

    h0, h0b = ln_in_out if ln_in_out is not None else _ln_in(x, _row(small["ln_in_g"]), _row(small["ln_in_b"]))[0]
    proj = _mm(h0b, w_cat, "nt", 512, D_CAT, "mm_proj")
    gates, gcum, gcumt = _gates(proj, bias_row, alog_row)
    w_down_pieces = [(4, 0, 1)]

    def gather(phase, pieces):
        if not comm or not pieces:
            return None, lambda moved: None
        touched = sorted({i for i, _, _ in pieces})

        def took(moved):
            for i, buf in zip(touched, moved):
                late[i] = buf
        return phase([late[i] for i in touched], [(touched.index(i), k, n) for i, k, n in pieces]), took

    over, on = _gather_chips, _gather_pass_on
    cm, took = gather(over, [(0, 0, 1), (3, 0, 1)])
    (conv_c, qkv_n), moved = _gdn_conv(proj, conv_w, cm)
    took(moved)
    cm, took = gather(over, [(1, 0, 2)])
    (gu, gw, gqk, gtinv), moved = _gdn_local(qkv_n, gcum, cm)
    took(moved)
    cm, took = gather(over, [(1, 1, 2)])
    (o_gdn, sall), moved = _gdn_fwd(qkv_n, gcum, gu, gw, gqk, cm)
    took(moved)
    fox_plan = [(over, []), (on, [(0, 0, 1), (3, 0, 1), (1, 0, 2), (1, 1, 2)]), (over, [(2, 0, 1)]), (over, [(4, 0, 4)])]
    assert not comm or len(_fox_groups(x.shape[0])) == len(fox_plan)
    o_fox, lse = _fox_fwd(proj, gcumt, (lambda n: gather(*fox_plan[n])) if comm else None)
    cm, took = gather(on, [(2, 0, 1)])
    (attn,), moved = _attn_post(o_gdn, proj, o_fox, g_gdn, g_fox2, cm)
    took(moved)
    w_out = late[0].reshape(D_MODEL, D_MODEL)
    cm, took = gather(over, [(4, 1, 4)])
    (h1, h1b, xhat1, rstd1), moved = _ln1(h0, attn, w_out, _row(small["ln1_g"]), _row(small["ln1_b"]), cm)
    took(moved)
    w_up, w_ple = late[1], late[3]
    cm, took = gather(over, [(4, 1, 2)])
    up_act = _mm(h1b, w_up, "nn", 512, 1024, "mm_up", epi="relu2", shards=N_CHIPS, comm=cm)
    if cm:
        up_act, moved = up_act
        took(moved)
    up, act = up_act
    w_gate = late[2].reshape(D_MODEL, D_MODEL)
    cm, took = gather(on, w_down_pieces)
    gp = _mm(h1b, w_gate, "nn", 512, D_MODEL, "mm_gate", comm=cm)
    if cm:
        gp, moved = gp
        took(moved)
    w_down = late[4].reshape(D_FF, D_MODEL)
    dr2, dr2b, dpe, dgp, pg2 = _ln2_loss(h1, act, w_down, pb, w_ple, gp, _row(small["b_ple_gate"]), _row(small["ln2_g"]),
                                         _row(small["ln2_b"]), target)

    by_dest = lambda g: g.reshape((N_CHIPS, -1, g.shape[-1]))
    g_late = [None] * len(LATE)
    state = dict(from_sibling=[None] * len(LATE), sent=[None] * len(LATE), landing=[None] * len(LATE))
    nothing = (None, lambda moved: None)

    def to_sibling(idx):
        if not comm:
            return nothing

        def took(moved):
            for i, b1 in zip(idx, moved):
                state["from_sibling"][i] = b1
                state["sent"][i] = _add_pair(g_late[i], b1, qc, "add_pair_" + LATE[i])
                state["landing"][i] = _landing([state["sent"][i]])[0]
        return _exchange_pairs([g_late[i] for i in idx]), took

    def to_chips(pieces):
        if not comm:
            return nothing
        touched = sorted({i for i, _, _ in pieces})

        def took(moved):
            for i, b2 in zip(touched, moved):
                state["landing"][i] = b2
        return _exchange_chips([state["sent"][i] for i in touched], [state["landing"][i] for i in touched],
                               [(touched.index(i), k, n) for i, k, n in pieces]), took

    def ride(result, cm, took):
        if cm:
            result, moved = result
            took(moved)
        return result

    dup = _mm(dr2b, w_down, "nt", 512, 2048, "mm_dact", epi="relu2_bwd", extra=up)
    g_late[4] = by_dest(_mm(act, dr2b, "tn", 1024, D_MODEL, "mm_gdown"))
    cm, took = to_sibling([4])
    g_late[1] = ride(_mm(h1b, dup, "tn", 1024, 1024, "mm_gup", shards=N_CHIPS, comm=cm), cm, took)
    g_late[2] = by_dest(_mm(h1b, dgp, "tn", 1024, D_MODEL, "mm_ggate"))
    g_late[3] = _mm(pb, dpe, "tn", D_PLE, D_MODEL // N_CHIPS, "mm_gple", shards=N_CHIPS)
    cm, took = to_chips([(4, 0, 2)])
    (dr1, dr1b, pg1), moved = _ln1_bwd(dr2, dup, w_up, dgp, w_gate, xhat1, rstd1, _row(small["ln1_g"]), cm)
    took(moved)
    g_late[0] = by_dest(_mm(attn, dr1b, "tn", 1024, D_MODEL, "mm_gout"))
    do_gdn, dz, do_fox, pga = _attn_post_bwd(dr1b, w_out, o_gdn, proj, o_fox, g_gdn, g_fox2)
    chip_plan = [[(4, 1, 2), (1, 0, 2)], [(1, 1, 2)], [(0, 0, 1)], [(2, 0, 1), (3, 0, 1)]]

    def gdn_backward():
        cm, took = to_chips(chip_plan[0])
        state["gdn"], moved = _gdn_bwd(qkv_n, gcum, gu, gw, gqk, gtinv, sall, do_gdn, cm)
        took(moved)

    def fox_ride(n):
        if n == 0:
            return to_sibling([1, 0, 2, 3])
        if n == 1:
            gdn_backward()
        return to_chips(chip_plan[n])

    assert not comm or len(_fox_groups(x.shape[0])) == len(chip_plan)
    dfq, dfk, dfv, dccol, dct = _fox_bwd(proj, gcumt, o_fox, lse, do_fox, fox_ride if comm else None)
    if not comm:
        gdn_backward()
    dqkv_n, dgates = state["gdn"]
    dsmall, pgg = _gates_bwd(proj, bias_row, alog_row, gates, dgates, dccol, dct)
    cm = None
    if comm:
        cm = _share_halves([_add_chips(g, b1, b2, qc, "add_chips_" + n)
                            for g, b1, b2, n in zip(g_late, state["from_sibling"], state["landing"], LATE)])
    (du, g_conv8), reduced = _gdn_conv_bwd(proj, conv_w, conv_c, dqkv_n, cm)
    if comm:
        g_late = list(reduced)
    t = x.shape[0]
    dproj = jnp.concatenate([du, dz, dfq, dfk.astype(bf16), dfv.astype(bf16), dsmall, jnp.zeros((t, D_CAT - SEG_SMALL - LANES), bf16)], axis=1)
    g_cat = _mm(dproj, h0b, "tn", 1280, D_MODEL, "mm_gcat")
    cm, took = tail[0](g_cat) if tail else (None, None)
    dh0_mm = _mm(dproj, w_cat, "nn", 512, D_MODEL, "mm_dh0", comm=cm)
    if cm:
        dh0_mm, moved = dh0_mm
        took(moved)
    cm, took = tail[1]() if tail and tail[1] else (None, None)
    (grad_x, pg0), moved = _ln_in_bwd(x, dr1, dh0_mm, _row(small["ln_in_g"]), cm)
    if cm:
        took(moved)

    g_fox = pga[1, :FOX_DH] + pga[1, FOX_DH:]
    small_grads = dict(
        ln_in_g=pg0[0], ln_in_b=pg0[1], ln1_g=pg1[0], ln1_b=pg1[1], b_ple_gate=pg2[2], ln2_g=pg2[0], ln2_b=pg2[1],
        gdn_norm_g=pga[0], fox_norm_g=g_fox, a_log=pgg[1, 4:8], dt_bias=pgg[0, 4:8], b_f=pgg[0, 8:16], loss=pg2[3, 0:1])
    return grad_x, g_cat, g_conv8[:CONV_W], dict(zip(LATE, g_late)), small_grads


ANY = pl.BlockSpec(memory_space=pl.ANY)
CONV_PKT_ROWS = 16


def _mesh_pos():
    return lax.axis_index("x"), lax.axis_index("y"), lax.axis_index("c")


def _other_chips(x, y):
    return [(1 - x, y), (x, 1 - y), (1 - x, 1 - y)]


def _rcopy(src, dst, send_sem, recv_sem, dev):
    return pltpu.make_async_remote_copy(src_ref=src, dst_ref=dst, send_sem=send_sem, recv_sem=recv_sem,
                                        device_id=dev, device_id_type=MESH)


class _Comm:
    def __init__(self, ins, outs, aliases, n_sems, start, finish):
        self.ins, self.outs, self.aliases, self.n_sems, self.start, self.finish = list(ins), list(outs), dict(aliases), n_sems, start, finish


def _hosted(body, comm, *, name, grid, in_specs, out_specs, out_shape, args, scratch_shapes=(), aliases=None):
    n_in, n_out, n_sc = len(in_specs), len(out_specs), len(scratch_shapes)
    k, ko = (len(comm.ins), len(comm.outs)) if comm else (0, 0)

    def kernel_body(*refs):
        o0 = n_in + k
        s0 = o0 + n_out + ko
        if comm:
            cins, couts, (ssem, rsem) = refs[n_in:o0], refs[o0 + n_out:s0], refs[s0 + n_sc:]
            step = pl.program_id(0)
            for d in range(1, len(grid)):
                step = step * grid[d] + pl.program_id(d)

            @pl.when(step == 0)
            def _():
                comm.start(cins, couts, ssem, rsem)

        body(*refs[:n_in], *refs[o0:o0 + n_out], *refs[s0:s0 + n_sc])
        if comm:
            last = 1
            for n in grid:
                last *= n

            @pl.when(step == last - 1)
            def _():
                comm.finish(cins, couts, ssem, rsem)

    io_aliases = dict(aliases or {})
    scratch = list(scratch_shapes)
    if comm:
        io_aliases.update({n_in + i: n_out + j for i, j in comm.aliases.items()})
        scratch += [pltpu.SemaphoreType.DMA((comm.n_sems,)), pltpu.SemaphoreType.DMA((comm.n_sems,))]
    res = pl.pallas_call(
        kernel_body, name=name, grid=grid, in_specs=list(in_specs) + [ANY] * k, out_specs=tuple(out_specs) + (ANY,) * ko,
        out_shape=tuple(out_shape) + tuple(comm.outs if comm else ()), scratch_shapes=scratch, input_output_aliases=io_aliases,
        compiler_params=_params(("arbitrary",) * len(grid)),
    )(*args, *(comm.ins if comm else ()))
    return tuple(res[:n_out]), tuple(res[n_out:])


def _comm_only(phases, name):
    n_in = sum(len(p.ins) for p in phases)

    def body(*refs):
        n_out = sum(len(p.outs) for p in phases)
        sems = refs[n_in + n_out:]
        i0, o0 = 0, n_in
        for j, p in enumerate(phases):
            cins, couts = refs[i0:i0 + len(p.ins)], refs[o0:o0 + len(p.outs)]
            p.start(cins, couts, sems[2 * j], sems[2 * j + 1])
            p.finish(cins, couts, sems[2 * j], sems[2 * j + 1])
            i0 += len(p.ins)
            o0 += len(p.outs)

    aliases, i0, o0 = {}, 0, 0
    for p in phases:
        aliases.update({i0 + i: o0 + j for i, j in p.aliases.items()})
        i0 += len(p.ins)
        o0 += len(p.outs)
    outs = [o for p in phases for o in p.outs]
    res = pl.pallas_call(
        body, name=name, out_shape=tuple(outs), in_specs=[ANY] * n_in, out_specs=(ANY,) * len(outs), input_output_aliases=aliases,
        scratch_shapes=[pltpu.SemaphoreType.DMA((p.n_sems,)) for p in phases for _ in range(2)],
    )(*[a for p in phases for a in p.ins])
    split, o0 = [], 0
    for p in phases:
        split.append(tuple(res[o0:o0 + len(p.outs)]))
        o0 += len(p.outs)
    return split


def _like(arrays):
    return [jax.ShapeDtypeStruct(a.shape, a.dtype) for a in arrays]


def _half(ref, slot, hf, piece=(0, 1)):
    k, n = piece
    rows = ref.shape[1] // 2 // n
    return ref.at[slot, pl.ds((hf * n + k) * rows, rows)]


def _whole_halves(arrays):
    return [(i, 0, 1) for i in range(len(arrays))]


def _gather_chips(bufs, pieces=None, whole=False, base=0):
    nw = len(bufs)
    pieces = _whole_halves(bufs) if pieces is None else pieces
    part = (lambda ref, slot, c, piece: ref.at[slot]) if whole else _half

    def copies(couts):
        x, y, c = _mesh_pos()
        q = 2 * x + y
        for j, (i, k, n) in enumerate(pieces):
            for kc, chip in enumerate(_other_chips(x, y)):
                mine, theirs = part(couts[i], q, c, (k, n)), part(couts[i], 2 * chip[0] + chip[1], c, (k, n))
                yield base + j * 3 + kc, mine, theirs, (*chip, c)

    def start(cins, couts, ssem, rsem):
        for s, mine, _, dev in copies(couts):
            _rcopy(mine, mine, ssem.at[s], rsem.at[s], dev).start()

    def finish(cins, couts, ssem, rsem):
        for s, _, theirs, dev in copies(couts):
            _rcopy(theirs, theirs, ssem.at[s], rsem.at[s], dev).wait_recv()
        for s, mine, _, dev in copies(couts):
            _rcopy(mine, mine, ssem.at[s], rsem.at[s], dev).wait_send()

    return _Comm(bufs, _like(bufs), {i: i for i in range(nw)}, 3 * len(pieces), start, finish)


def _gather_pass_on(bufs, pieces=None, base=0):
    nw = len(bufs)
    pieces = _whole_halves(bufs) if pieces is None else pieces

    def copies(couts):
        x, y, c = _mesh_pos()
        for j, (i, k, n) in enumerate(pieces):
            for kc, chip in enumerate(_other_chips(x, y)):
                slot = 2 * chip[0] + chip[1]
                yield base + j * 3 + kc, _half(couts[i], slot, c, (k, n)), _half(couts[i], slot, 1 - c, (k, n)), (x, y, 1 - c)

    def start(cins, couts, ssem, rsem):
        for s, landed, _, sib in copies(couts):
            _rcopy(landed, landed, ssem.at[s], rsem.at[s], sib).start()

    def finish(cins, couts, ssem, rsem):
        for s, _, passed, sib in copies(couts):
            _rcopy(passed, passed, ssem.at[s], rsem.at[s], sib).wait_recv()
        for s, landed, _, sib in copies(couts):
            _rcopy(landed, landed, ssem.at[s], rsem.at[s], sib).wait_send()

    return _Comm(bufs, _like(bufs), {i: i for i in range(nw)}, 3 * len(pieces), start, finish)


def _gather_now(bufs, packets):
    nb = len(bufs)
    over, on, pk = _gather_chips(bufs), _gather_pass_on(bufs, base=3 * nb), _gather_chips(packets, whole=True, base=6 * nb)

    def start(cins, couts, ssem, rsem):
        over.start(cins[:nb], couts[:nb], ssem, rsem)
        pk.start(cins[nb:], couts[nb:], ssem, rsem)

    def finish(cins, couts, ssem, rsem):
        over.finish(cins[:nb], couts[:nb], ssem, rsem)
        on.start(cins[:nb], couts[:nb], ssem, rsem)
        on.finish(cins[:nb], couts[:nb], ssem, rsem)
        pk.finish(cins[nb:], couts[nb:], ssem, rsem)

    every = list(bufs) + list(packets)
    return _Comm(every, _like(every), {i: i for i in range(len(every))}, 6 * nb + 3 * len(packets), start, finish)


def _exchange_pairs(gs):
    nw = len(gs)

    def copies(cins, couts):
        x, y, c = _mesh_pos()
        for i in range(nw):
            for d in range(N_CHIPS):
                yield i * N_CHIPS + d, _half(cins[i], d, 1 - c), couts[i].at[d], (x, y, 1 - c)

    def start(cins, couts, ssem, rsem):
        for s, src, dst, sib in copies(cins, couts):
            _rcopy(src, dst, ssem.at[s], rsem.at[s], sib).start()

    def finish(cins, couts, ssem, rsem):
        for s, src, dst, sib in copies(cins, couts):
            _rcopy(src, dst, ssem.at[s], rsem.at[s], sib).wait_recv()
        for s, src, dst, sib in copies(cins, couts):
            _rcopy(src, dst, ssem.at[s], rsem.at[s], sib).wait_send()

    outs = [jax.ShapeDtypeStruct((N_CHIPS, g.shape[1] // 2, g.shape[2]), g.dtype) for g in gs]
    return _Comm(gs, outs, {}, N_CHIPS * nw, start, finish)


def _gather_packets(small):
    def peers():
        x, y, c = _mesh_pos()
        for r in range(1, 8):
            fx, fy, fc = (r >> 2) & 1, (r >> 1) & 1, r & 1
            yield r - 1, (1 - x if fx else x, 1 - y if fy else y, 1 - c if fc else c)

    def start(cins, couts, ssem, rsem):
        x, y, c = _mesh_pos()
        mine = couts[0].at[4 * x + 2 * y + c]
        for s, peer in peers():
            _rcopy(mine, mine, ssem.at[s], rsem.at[s], peer).start()

    def finish(cins, couts, ssem, rsem):
        x, y, c = _mesh_pos()
        mine = couts[0].at[4 * x + 2 * y + c]
        for s, peer in peers():
            theirs = couts[0].at[4 * peer[0] + 2 * peer[1] + peer[2]]
            _rcopy(theirs, theirs, ssem.at[s], rsem.at[s], peer).wait_recv()
        for s, peer in peers():
            _rcopy(mine, mine, ssem.at[s], rsem.at[s], peer).wait_send()

    return _Comm([small], _like([small]), {0: 0}, 7, start, finish)


def _exchange_chips(a4s, b2s, pieces=None):
    nw = len(a4s)
    pieces = _whole_halves(a4s) if pieces is None else pieces

    def copies(cins, couts):
        x, y, c = _mesh_pos()
        for j, (i, k, n) in enumerate(pieces):
            rows = a4s[i].shape[1] // n
            part = pl.ds(k * rows, rows)
            for kc, chip in enumerate(_other_chips(x, y)):
                yield j * 3 + kc, cins[i].at[2 * chip[0] + chip[1], part], couts[i].at[kc, part], (*chip, c)

    def start(cins, couts, ssem, rsem):
        for s, src, dst, dev in copies(cins, couts):
            _rcopy(src, dst, ssem.at[s], rsem.at[s], dev).start()

    def finish(cins, couts, ssem, rsem):
        for s, src, dst, dev in copies(cins, couts):
            _rcopy(src, dst, ssem.at[s], rsem.at[s], dev).wait_recv()
        for s, src, dst, dev in copies(cins, couts):
            _rcopy(src, dst, ssem.at[s], rsem.at[s], dev).wait_send()

    return _Comm(list(a4s) + list(b2s), _like(b2s), {nw + i: i for i in range(nw)}, 3 * len(pieces), start, finish)


def _landing(a4s):
    return [lax.empty((3,) + a.shape[1:], a.dtype) for a in a4s]


def _share_halves(rs):
    nw = len(rs)

    def halves(couts, i, hf):
        rows = rs[i].shape[0] // 2
        return couts[i].at[pl.ds(hf * rows, rows)]

    def start(cins, couts, ssem, rsem):
        x, y, c = _mesh_pos()
        for i in range(nw):
            _rcopy(halves(couts, i, c), halves(couts, i, c), ssem.at[i], rsem.at[i], (x, y, 1 - c)).start()

    def finish(cins, couts, ssem, rsem):
        x, y, c = _mesh_pos()
        for i in range(nw):
            _rcopy(halves(couts, i, 1 - c), halves(couts, i, 1 - c), ssem.at[i], rsem.at[i], (x, y, 1 - c)).wait_recv()
        for i in range(nw):
            _rcopy(halves(couts, i, c), halves(couts, i, c), ssem.at[i], rsem.at[i], (x, y, 1 - c)).wait_send()

    return _Comm(rs, _like(rs), {i: i for i in range(nw)}, nw, start, finish)


ADD_ROWS = 256


def _add_pair(g4, b1, qc_idx, name):
    _, half, cols = b1.shape
    rb = ADD_ROWS if half % ADD_ROWS == 0 else half
    nb = half // rb

    def body(qc_ref, g_ref, b_ref, ob_ref):
        ob_ref[...] = (g_ref[...] + b_ref[...]).astype(bf16)

    blk = (1, rb, cols)
    out = pl.BlockSpec(blk, lambda d, i, qc: (d, i, 0))
    return pl.pallas_call(
        body, name=name,
        grid_spec=pltpu.PrefetchScalarGridSpec(
            num_scalar_prefetch=1, grid=(N_CHIPS, nb),
            in_specs=[pl.BlockSpec(blk, lambda d, i, qc: (d, qc[1] * nb + i, 0)), out],
            out_specs=out),
        out_shape=jax.ShapeDtypeStruct(b1.shape, bf16),
        compiler_params=_params(("parallel", "parallel")),
    )(qc_idx, g4, b1)


def _add_chips(g4, b1, b2, qc_idx, name):
    _, half, cols = b1.shape
    rb = ADD_ROWS if half % ADD_ROWS == 0 else half
    nb = half // rb

    def body(qc_ref, g_ref, s_ref, b_ref, o_ref):
        o_ref[...] = (((g_ref[0] + s_ref[0]) + b_ref[0].astype(f32)) + b_ref[1].astype(f32)) + b_ref[2].astype(f32)

    return pl.pallas_call(
        body, name=name,
        grid_spec=pltpu.PrefetchScalarGridSpec(
            num_scalar_prefetch=1, grid=(nb,),
            in_specs=[pl.BlockSpec((1, rb, cols), lambda i, qc: (qc[0], qc[1] * nb + i, 0)),
                      pl.BlockSpec((1, rb, cols), lambda i, qc: (qc[0], i, 0)), pl.BlockSpec((3, rb, cols), lambda i, qc: (0, i, 0))],
            out_specs=pl.BlockSpec((rb, cols), lambda i, qc: (qc[1] * nb + i, 0))),
        out_shape=jax.ShapeDtypeStruct((2 * half, cols), f32),
        compiler_params=_params(("parallel",)),
    )(qc_idx, g4, b1, b2)


def _adamw_math(w, g, m, v):
    m = ADAM_B1 * m + (1.0 - ADAM_B1) * g
    v = ADAM_B2 * v + (1.0 - ADAM_B2) * (g * g)
    m_hat = m / (1.0 - ADAM_B1 ** ADAM_STEP)
    v_hat = v / (1.0 - ADAM_B2 ** ADAM_STEP)
    return -ADAM_LR * (m_hat / (jnp.sqrt(v_hat) + ADAM_EPS) + ADAM_WD * w), m, v


def _adamw(w, g, m, v, name, comm=None):
    rows = w.shape[0]
    if w.ndim == 3:
        rb = max(r for r in range(1, ADD_ROWS // 4 + 1) if rows % r == 0)
    else:
        rb = ADD_ROWS if rows % ADD_ROWS == 0 else rows

    def body(w_ref, g_ref, m_ref, v_ref, go_ref, d_ref, mo_ref, vo_ref):
        g = g_ref[...]
        go_ref[...] = g
        d_ref[...], mo_ref[...], vo_ref[...] = _adamw_math(w_ref[...], g, m_ref[...], v_ref[...])

    blk = pl.BlockSpec((rb,) + w.shape[1:], lambda i: (i,) + (0,) * (w.ndim - 1))
    return _hosted(body, comm, name=name, grid=(rows // rb,), in_specs=[blk] * 4, out_specs=(blk,) * 4,
                   out_shape=(jax.ShapeDtypeStruct(w.shape, f32),) * 4, args=(w, g, m, v))


def _small_sum_adamw(all_pkts, w, m, v):
    names = [n for n, _, _ in SMALL_LAYOUT if n in w]
    place = {n: (r0, size) for n, r0, size in SMALL_LAYOUT}
    rows_of = lambda size: -(-size // LANES)
    flat = lambda a: a.reshape(1, -1)
    k = len(names)

    def body(*refs):
        a_ref, ins = refs[0], refs[1:1 + 3 * k]
        g_ref, outs = refs[1 + 3 * k], refs[2 + 3 * k:2 + 7 * k]
        packs = refs[2 + 7 * k:]
        g = a_ref[0]
        for r in range(1, 8):
            g = g + a_ref[r]
        g_ref[...] = g
        for kind in range(3):
            packs[kind][...] = jnp.zeros_like(packs[kind])
            for j, n in enumerate(names):
                r0, size = place[n]
                for r in range(rows_of(size)):
                    width = min(LANES, size - r * LANES)
                    packs[kind][r0 + r:r0 + r + 1, 0:width] = ins[kind * k + j][:, r * LANES:r * LANES + width]
        results = (g,) + _adamw_math(packs[0][...], g, packs[1][...], packs[2][...])
        for kind, val in enumerate(results):
            for j, n in enumerate(names):
                r0, size = place[n]
                for r in range(rows_of(size)):
                    width = min(LANES, size - r * LANES)
                    outs[kind * k + j][:, r * LANES:r * LANES + width] = val[r0 + r:r0 + r + 1, 0:width]

    args = [all_pkts] + [flat(d[n]) for d in (w, m, v) for n in names]
    out_shape = [jax.ShapeDtypeStruct(all_pkts.shape[1:], f32)] + [jax.ShapeDtypeStruct((1, place[n][1]), f32) for _ in range(4) for n in names]
    res = pl.pallas_call(body, name="small_sum_adamw", out_shape=tuple(out_shape),
                         scratch_shapes=[pltpu.VMEM(all_pkts.shape[1:], f32)] * 3)(*args)
    by_kind = [{n: res[1 + kind * k + j].reshape(w[n].shape) for j, n in enumerate(names)} for kind in range(4)]
    return res[0], by_kind


SMALL_LAYOUT = (("ln_in_g", 0, 1024), ("ln_in_b", 8, 1024), ("ln1_g", 16, 1024), ("ln1_b", 24, 1024), ("b_ple_gate", 32, 1024),
                ("ln2_g", 40, 1024), ("ln2_b", 48, 1024), ("gdn_norm_g", 56, 128), ("fox_norm_g", 57, 64), ("a_log", 58, 4),
                ("dt_bias", 59, 4), ("b_f", 60, 8), ("loss", 61, 1))
SMALL_CONV_ROW = 64
SMALL_ROWS = 128


def _pack_small(vals, conv=None):
    rows = []
    nxt = 0
    for n, r0, size in SMALL_LAYOUT:
        assert r0 == nxt
        v = vals[n].reshape(-1).astype(f32) if n in vals else jnp.zeros((size,), f32)
        nrows = -(-size // LANES)
        rows.append(jnp.pad(v, (0, nrows * LANES - size)).reshape(nrows, LANES))
        nxt = r0 + nrows
    rows.append(jnp.zeros((SMALL_CONV_ROW - nxt, LANES), f32))
    conv_rows = CONV_W * GDN_QKV // LANES
    rows.append(jnp.zeros((conv_rows, LANES), f32) if conv is None else conv.reshape(conv_rows, LANES))
    rows.append(jnp.zeros((SMALL_ROWS - SMALL_CONV_ROW - conv_rows, LANES), f32))
    return jnp.concatenate(rows, axis=0)


WEIGHTS = ("ln_in_g", "ln_in_b", "w_in", "conv_w", "a_log", "dt_bias", "gdn_norm_g", "b_f", "fox_norm_g", "w_out", "ln1_g", "ln1_b",
           "w_up", "w_down", "w_ple", "w_ple_gate", "b_ple_gate", "ln2_g", "ln2_b")
SMALL_NAMES = tuple(n for n, _, _ in SMALL_LAYOUT if n != "loss")


def kernel(x, p, ln_in_g, ln_in_b, w_in, conv_w, a_log, dt_bias, gdn_norm_g, b_f, fox_norm_g, w_out, ln1_g, ln1_b, w_up, w_down, w_ple, w_ple_gate, b_ple_gate, ln2_g, ln2_b, loss_target, m_ln_in_g, m_ln_in_b, m_w_in, m_conv_w, m_a_log, m_dt_bias, m_gdn_norm_g, m_b_f, m_fox_norm_g, m_w_out, m_ln1_g, m_ln1_b, m_w_up, m_w_down, m_w_ple, m_w_ple_gate, m_b_ple_gate, m_ln2_g, m_ln2_b, v_ln_in_g, v_ln_in_b, v_w_in, v_conv_w, v_a_log, v_dt_bias, v_gdn_norm_g, v_b_f, v_fox_norm_g, v_w_out, v_ln1_g, v_ln1_b, v_w_up, v_w_down, v_w_ple, v_w_ple_gate, v_b_ple_gate, v_ln2_g, v_ln2_b):
    given = dict(locals())
    w = {n: given[n] for n in WEIGHTS}
    m = {n: given["m_" + n] for n in WEIGHTS}
    v = {n: given["v_" + n] for n in WEIGHTS}
    xi, yi, ci = _mesh_pos()
    q = 2 * xi + yi

    def slot_buffer(val, dtype, slots=N_CHIPS, slot=q, rows=None):
        rows = val.shape[0] if rows is None else rows
        return lax.dynamic_update_slice(lax.empty((slots, rows) + val.shape[1:], dtype), val.astype(dtype)[None], (slot, 0, 0))

    shard_cols = D_IN // N_CHIPS
    conv_rows = CONV_W * GDN_QKV // N_CHIPS // LANES
    conv_pkt = jnp.pad(w["conv_w"][0].reshape(-1, LANES), ((0, CONV_PKT_ROWS - conv_rows), (0, 0)))
    ln_in_out, (w_in4, conv_all) = _ln_in(x[0], _row(w["ln_in_g"]), _row(w["ln_in_b"]),
                                          _gather_now([slot_buffer(w["w_in"][0].T, bf16, rows=W_IN_ROWS)], [slot_buffer(conv_pkt, f32)]))
    conv_full = jnp.concatenate([conv_all[d, :conv_rows].reshape(CONV_W, GDN_QKV // N_CHIPS) for d in range(N_CHIPS)], axis=1)
    wi = jnp.concatenate([w_in4[d, :shard_cols] for d in range(N_CHIPS)], axis=0)
    w_cat = jnp.concatenate([wi[:OFF_BETA], wi[OFF_FOX:OFF_F], wi[OFF_BETA:OFF_FOX], wi[OFF_F:],
                             jnp.zeros((D_CAT - D_IN, D_MODEL), bf16)], axis=0)

    small = {n: w[n] for n in SMALL_NAMES}
    qc = jnp.stack([q, ci]).astype(jnp.int32)
    tail_state = {}

    def chips_phase(gc):
        g_in = jnp.concatenate([gc[:OFF_BETA], gc[SEG_SMALL:SEG_SMALL + 8], gc[SEG_FOX:SEG_SMALL], gc[SEG_SMALL + 8:SEG_SMALL + 16]], axis=0)
        g_in4 = jnp.stack([jnp.pad(g_in[d * shard_cols:(d + 1) * shard_cols], ((0, W_IN_ROWS - shard_cols), (0, 0))) for d in range(N_CHIPS)])
        (from_sibling,), = _comm_only([_exchange_pairs([g_in4])], "exchange_pairs_w_in")
        sent = _add_pair(g_in4, from_sibling, qc, "add_pair_w_in")
        tail_state.update(g=g_in4, from_sibling=from_sibling)
        return _exchange_chips([sent], _landing([sent])), lambda moved: tail_state.update(from_chips=moved[0])

    grad_x, _, g_conv, g_late, small_g = _device_grads(
        x[0], p[0, 0], loss_target[0], small, w_cat, conv_full, [slot_buffer(w[n][0], bf16) for n in LATE], qc, tail=(chips_phase, None),
        ln_in_out=ln_in_out)
    packets = _gather_packets(slot_buffer(_pack_small(small_g, g_conv), f32, 8, 4 * xi + 2 * yi + ci))
    halves = _share_halves([_add_chips(tail_state["g"], tail_state["from_sibling"], tail_state["from_chips"], qc, "add_chips_w_in")])
    (g_late["w_in"],), (small_all,) = _comm_only([halves, packets], "share_w_in")

    grads, delta, new_m, new_v = {}, {}, {}, {}
    for n, g in g_late.items():
        if n == "w_in":
            as_stored = lambda a: jnp.transpose(a, (2, 0, 1))
            outs, _ = _adamw(as_stored(w[n]), g[:shard_cols].reshape(shard_cols, 1, D_MODEL), as_stored(m[n]), as_stored(v[n]), "adamw_" + n)
            grads[n], delta[n], new_m[n], new_v[n] = (jnp.transpose(a, (1, 2, 0)) for a in outs)
        else:
            outs, _ = _adamw(w[n][0], g, m[n][0], v[n][0], "adamw_" + n)
            grads[n], delta[n], new_m[n], new_v[n] = (a.reshape(w[n].shape) for a in outs)
    pick = lambda d: {n: d[n] for n in SMALL_NAMES}
    g_pkt, by_kind = _small_sum_adamw(small_all, pick(w), pick(m), pick(v))
    for dst, vals in zip((grads, delta, new_m, new_v), by_kind):
        dst.update(vals)
    conv_rows_all = CONV_W * GDN_QKV // LANES
    conv_g_full = g_pkt[SMALL_CONV_ROW:SMALL_CONV_ROW + conv_rows_all].reshape(CONV_W, GDN_QKV)
    conv_g = lax.dynamic_slice_in_dim(conv_g_full, q * (GDN_QKV // N_CHIPS), GDN_QKV // N_CHIPS, axis=1)
    outs, _ = _adamw(w["conv_w"][0], conv_g, m["conv_w"][0], v["conv_w"][0], "adamw_conv_w")
    grads["conv_w"], delta["conv_w"], new_m["conv_w"], new_v["conv_w"] = (a.reshape(w["conv_w"].shape) for a in outs)
    loss = g_pkt[61, 0]
    return (loss, grad_x[None], *[grads[n] for n in WEIGHTS], *[delta[n] for n in WEIGHTS],
            *[new_m[n] for n in WEIGHTS], *[new_v[n] for n in WEIGHTS])
```

```python
import functools

import jax
import jax.numpy as jnp
from jax import lax
from jax.experimental import pallas as pl
from jax.experimental.pallas import tpu as pltpu

f32 = jnp.float32
bf16 = jnp.bfloat16
HI = lax.Precision.HIGHEST
MESH = pl.DeviceIdType.MESH

D_MODEL = 1024
CHUNK = 64
GDN_HEADS = 4
GDN_DK = 128
FOX_HEADS = 8
FOX_DH = 64
CONV_W = 4
D_FF = 4096
D_PLE = 256
LN_EPS = 1e-5
NORM_EPS = 1e-6
ALPHA = 2.0 ** 0.25
GDN_QKV = 1536
OFF_Z = 1536
OFF_BETA = 2048
OFF_FOX = 2056
OFF_F = 3592
D_IN = 3600
ADAM_LR = 0.001
ADAM_B1 = 0.9
ADAM_B2 = 0.999
ADAM_EPS = 1e-08
ADAM_WD = 0.01
ADAM_STEP = 10

SEG_FOX = 2048
SEG_SMALL = 3584
D_CAT = 3840
LANES = 128
TOK_BLK = 256
FOX_BQ = 256
VMEM_LIMIT = 56 * 1024 * 1024
NEG = -1e30

N_CHIPS = 4
W_IN_ROWS = 928


def _params(sem=None, **kw):
    return pltpu.CompilerParams(dimension_semantics=sem, vmem_limit_bytes=VMEM_LIMIT, **kw)


def _sigmoid(x):
    return 1.0 / (1.0 + jnp.exp(-x))


def _softplus(x):
    return jnp.maximum(x, 0.0) + jnp.log(1.0 + jnp.exp(-jnp.abs(x)))


def _ln_fwd(x, g, b):
    mu = jnp.mean(x, -1, keepdims=True)
    xc = x - mu
    var = jnp.mean(xc * xc, -1, keepdims=True)
    rstd = lax.rsqrt(var + LN_EPS)
    xhat = xc * rstd
    return xhat * g + b, xhat, rstd


def _ln_bwd(dy, xhat, rstd, g):
    dxh = dy * g
    m1 = jnp.mean(dxh, -1, keepdims=True)
    m2 = jnp.mean(dxh * xhat, -1, keepdims=True)
    return rstd * (dxh - m1 - xhat * m2)


def _dot(a, b, prec=HI):
    return jnp.dot(a, b, precision=prec, preferred_element_type=f32)


def _dot_nt(a, b, prec=HI):
    return lax.dot_general(a, b, (((1,), (1,)), ((), ())), precision=prec, preferred_element_type=f32)


def _dot_tn(a, b, prec=HI):
    return lax.dot_general(a, b, (((0,), (0,)), ((), ())), precision=prec, preferred_element_type=f32)


def _lane(shape):
    return lax.broadcasted_iota(jnp.int32, shape, len(shape) - 1)


def _mm(a, b, mode, tm, tn, name, out_dtype=f32, epi=None, extra=None, shards=1, comm=None):
    if mode == "nn":
        (m, k), n = a.shape, b.shape[-1] * shards
    elif mode == "nt":
        (m, k), n = a.shape, b.shape[-2]
    else:
        (k, m), n = a.shape, b.shape[1]
    assert m % tm == 0 and n % tn == 0, (name, m, n, tm, tn)
    per = (n // shards) // tn
    assert mode == "nt" or per * tn * shards == n, (name, n, tn, shards)
    nc = 512 if tn % 512 == 0 else (256 if tn % 256 == 0 else 128)
    ks = k // shards

    def body(a_ref, b_ref, *rest):
        for n0 in range(0, tn, nc):
            if mode == "nn":
                acc = jnp.dot(a_ref[...], b_ref[:, n0:n0 + nc], preferred_element_type=f32)
            elif mode == "nt" and shards > 1:
                acc = jnp.zeros((tm, nc), f32)
                for d in range(shards):
                    acc = acc + lax.dot_general(a_ref[:, d * ks:(d + 1) * ks], b_ref[d, n0:n0 + nc, :], (((1,), (1,)), ((), ())),
                                                preferred_element_type=f32)
            elif mode == "nt":
                acc = lax.dot_general(a_ref[...], b_ref[n0:n0 + nc, :], (((1,), (1,)), ((), ())), preferred_element_type=f32)
            else:
                acc = lax.dot_general(a_ref[...], b_ref[:, n0:n0 + nc], (((0,), (0,)), ((), ())), preferred_element_type=f32)
            if epi == "relu2":
                relu_ref, act_ref = rest
                r = jnp.maximum(acc, 0.0)
                relu_ref[:, n0:n0 + nc] = r.astype(bf16)
                act_ref[:, n0:n0 + nc] = (r * r).astype(bf16)
            elif epi == "relu2_bwd":
                relu_ref, o_ref = rest
                o_ref[:, n0:n0 + nc] = (acc * (2.0 * relu_ref[:, n0:n0 + nc].astype(f32))).astype(bf16)
            else:
                (o_ref,) = rest
                o_ref[:, n0:n0 + nc] = acc.astype(out_dtype)

    if mode == "tn":
        a_spec = pl.BlockSpec((k, tm), lambda j, i: (0, i))
    else:
        a_spec = pl.BlockSpec((tm, k), lambda j, i: (i, 0))
    if mode == "nt" and shards > 1:
        b_spec = pl.BlockSpec((shards, tn, ks), lambda j, i: (0, j, 0))
    elif mode == "nt":
        b_spec = pl.BlockSpec((tn, k), lambda j, i: (j, 0))
    elif mode == "nn" and shards > 1:
        b_spec = pl.BlockSpec((None, k, tn), lambda j, i: (j // per, 0, j % per))
    else:
        b_spec = pl.BlockSpec((k, tn), lambda j, i: (0, j))
    o_spec = pl.BlockSpec((tm, tn), lambda j, i: (i, j))
    in_specs = [a_spec, b_spec]
    args = [a, b]
    if epi == "relu2":
        out_shape = (jax.ShapeDtypeStruct((m, n), bf16), jax.ShapeDtypeStruct((m, n), bf16))
        out_specs = (o_spec, o_spec)
    elif epi == "relu2_bwd":
        in_specs.append(o_spec)
        args.append(extra)
        out_shape = jax.ShapeDtypeStruct((m, n), bf16)
        out_specs = o_spec
    elif mode == "tn" and shards > 1:
        out_shape = jax.ShapeDtypeStruct((shards, m, n // shards), out_dtype)
        out_specs = pl.BlockSpec((None, tm, tn), lambda j, i: (j // per, i, j % per))
    else:
        out_shape = jax.ShapeDtypeStruct((m, n), out_dtype)
        out_specs = o_spec
    single = not isinstance(out_shape, tuple)
    res, moved = _hosted(body, comm, name=name, grid=(n // tn, m // tm), in_specs=in_specs,
                         out_specs=(out_specs,) if single else out_specs, out_shape=(out_shape,) if single else out_shape, args=args)
    res = res[0] if single else res
    return res if comm is None else (res, moved)


def _row_spec(width, col=0):
    return pl.BlockSpec((TOK_BLK, width), lambda i: (i, col))


def _vec_spec(rows, width):
    return pl.BlockSpec((rows, width), lambda i: (0, 0))


def _ln_in(x, g, b, comm=None):
    t, d = x.shape

    def body(x_ref, g_ref, b_ref, h_ref, hb_ref):
        h, _, _ = _ln_fwd(x_ref[...], g_ref[...], b_ref[...])
        h_ref[...] = h
        hb_ref[...] = h.astype(bf16)

    return _hosted(
        body, comm, name="ln_in", grid=(t // TOK_BLK,),
        in_specs=[_row_spec(d), _vec_spec(1, d), _vec_spec(1, d)],
        out_specs=(_row_spec(d), _row_spec(d)),
        out_shape=(jax.ShapeDtypeStruct((t, d), f32), jax.ShapeDtypeStruct((t, d), bf16)),
        args=(x, g, b))


def _attn_post(o_gdn, proj, o_fox, g_gdn, g_fox2, comm=None):
    t = o_gdn.shape[0]

    def body(og_ref, z_ref, of_ref, gg_ref, gf_ref, out_ref):
        for h in range(GDN_HEADS):
            sl = slice(h * LANES, (h + 1) * LANES)
            og = og_ref[:, sl]
            z = z_ref[:, sl]
            r = lax.rsqrt(jnp.mean(og * og, -1, keepdims=True) + NORM_EPS)
            out_ref[:, sl] = (og * r * gg_ref[...] * (z * _sigmoid(z))).astype(bf16)
        lo = _lane((TOK_BLK, LANES)) < FOX_DH
        for pr in range(FOX_HEADS // 2):
            sl = slice(pr * LANES, (pr + 1) * LANES)
            of = of_ref[:, sl]
            sq = of * of
            s0 = jnp.sum(jnp.where(lo, sq, 0.0), -1, keepdims=True)
            s1 = jnp.sum(jnp.where(lo, 0.0, sq), -1, keepdims=True)
            r = lax.rsqrt(jnp.where(lo, s0, s1) * (1.0 / FOX_DH) + NORM_EPS)
            out_ref[:, 512 + pr * LANES:512 + (pr + 1) * LANES] = (of * r * gf_ref[...]).astype(bf16)

    return _hosted(
        body, comm, name="attn_post", grid=(t // TOK_BLK,),
        in_specs=[_row_spec(512), _row_spec(512, OFF_Z // 512), _row_spec(512), _vec_spec(1, LANES), _vec_spec(1, LANES)],
        out_specs=(_row_spec(D_MODEL),),
        out_shape=(jax.ShapeDtypeStruct((t, D_MODEL), bf16),),
        args=(o_gdn, proj, o_fox, g_gdn, g_fox2))


def _attn_post_bwd(dr1b, w_out, o_gdn, proj, o_fox, g_gdn, g_fox2):
    t = o_gdn.shape[0]

    def body(dr_ref, wo_ref, og_ref, z_ref, of_ref, gg_ref, gf_ref, dog_ref, dz_ref, dof_ref, pg_ref):
        i = pl.program_id(0)

        @pl.when(i == 0)
        def _():
            pg_ref[...] = jnp.zeros_like(pg_ref)

        da = _dot_nt(dr_ref[...], wo_ref[...], None)
        dgg = jnp.zeros((1, LANES), f32)
        for h in range(GDN_HEADS):
            sl = slice(h * LANES, (h + 1) * LANES)
            og = og_ref[:, sl]
            z = z_ref[:, sl]
            dout = da[:, sl]
            g = gg_ref[...]
            r = lax.rsqrt(jnp.mean(og * og, -1, keepdims=True) + NORM_EPS)
            sg = _sigmoid(z)
            silu = z * sg
            ng = og * r * g
            dng = dout * silu
            dz_ref[:, sl] = (dout * ng * (sg * (1.0 + z * (1.0 - sg)))).astype(bf16)
            dgg = dgg + jnp.sum(dng * og * r, 0, keepdims=True)
            gd = dng * g
            dog_ref[:, sl] = r * gd - og * (r * r * r) * jnp.mean(og * gd, -1, keepdims=True)
        pg_ref[0:1, :] += dgg
        lo = _lane((TOK_BLK, LANES)) < FOX_DH
        dgf = jnp.zeros((1, LANES), f32)
        for pr in range(FOX_HEADS // 2):
            sl = slice(pr * LANES, (pr + 1) * LANES)
            of = of_ref[:, sl]
            dout = da[:, 512 + pr * LANES:512 + (pr + 1) * LANES]
            g = gf_ref[...]
            sq = of * of
            s0 = jnp.sum(jnp.where(lo, sq, 0.0), -1, keepdims=True)
            s1 = jnp.sum(jnp.where(lo, 0.0, sq), -1, keepdims=True)
            r = lax.rsqrt(jnp.where(lo, s0, s1) * (1.0 / FOX_DH) + NORM_EPS)
            dgf = dgf + jnp.sum(dout * of * r, 0, keepdims=True)
            gd = dout * g
            xg = of * gd
            m0 = jnp.sum(jnp.where(lo, xg, 0.0), -1, keepdims=True)
            m1 = jnp.sum(jnp.where(lo, 0.0, xg), -1, keepdims=True)
            dof_ref[:, sl] = r * gd - of * (r * r * r) * (jnp.where(lo, m0, m1) * (1.0 / FOX_DH))
        pg_ref[1:2, :] += dgf

    return pl.pallas_call(
        body, name="attn_post_bwd", grid=(t // TOK_BLK,),
        in_specs=_product_specs(dr1b, w_out) + [_row_spec(512), _row_spec(512, OFF_Z // 512), _row_spec(512), _vec_spec(1, LANES), _vec_spec(1, LANES)],
        out_specs=(_row_spec(512), _row_spec(512), _row_spec(512), _vec_spec(8, LANES)),
        out_shape=(jax.ShapeDtypeStruct((t, 512), f32), jax.ShapeDtypeStruct((t, 512), bf16),
                   jax.ShapeDtypeStruct((t, 512), f32), jax.ShapeDtypeStruct((8, LANES), f32)),
        compiler_params=_params(("arbitrary",)),
    )(dr1b, w_out, o_gdn, proj, o_fox, g_gdn, g_fox2)


def _product_specs(lhs, rhs):
    return [_row_spec(lhs.shape[1]), pl.BlockSpec(rhs.shape, lambda i: (0, 0))]


def _ln1(h0, lhs, rhs, g, b, comm=None):
    t, d = h0.shape

    def body(h0_ref, lhs_ref, rhs_ref, g_ref, b_ref, h_ref, hb_ref, xh_ref, rs_ref):
        mix = jnp.dot(lhs_ref[...], rhs_ref[...], preferred_element_type=f32)
        h, xhat, rstd = _ln_fwd(ALPHA * h0_ref[...] + mix, g_ref[...], b_ref[...])
        h_ref[...] = h
        hb_ref[...] = h.astype(bf16)
        xh_ref[...] = xhat
        rs_ref[...] = jnp.broadcast_to(rstd, rs_ref.shape)

    return _hosted(
        body, comm, name="ln1", grid=(t // TOK_BLK,),
        in_specs=[_row_spec(d)] + _product_specs(lhs, rhs) + [_vec_spec(1, d), _vec_spec(1, d)],
        out_specs=(_row_spec(d), _row_spec(d), _row_spec(d), _row_spec(LANES)),
        out_shape=(jax.ShapeDtypeStruct((t, d), f32), jax.ShapeDtypeStruct((t, d), bf16),
                   jax.ShapeDtypeStruct((t, d), f32), jax.ShapeDtypeStruct((t, LANES), f32)),
        args=(h0, lhs, rhs, g, b))


def _ln2_loss(h1, lhs, rhs, pb, w_ple, gp, b_gate, g, b, target):
    t, d = h1.shape

    def body(h1_ref, lhs_ref, rhs_ref, pb_ref, wp_ref, gp_ref, bg_ref, g_ref, b_ref, t_ref, dr_ref, drb_ref, dpe_ref, dgp_ref, pg_ref):
        i = pl.program_id(0)

        @pl.when(i == 0)
        def _():
            pg_ref[...] = jnp.zeros_like(pg_ref)

        ff = jnp.dot(lhs_ref[...], rhs_ref[...], preferred_element_type=f32)
        sig = _sigmoid(gp_ref[...] + bg_ref[...])
        pe = jnp.concatenate([jnp.dot(pb_ref[...], wp_ref[s], preferred_element_type=f32) for s in range(w_ple.shape[0])], axis=1)
        r2 = ALPHA * h1_ref[...] + ff + pe * sig
        y, xhat, rstd = _ln_fwd(r2, g_ref[...], b_ref[...])
        err = y - t_ref[...]
        dy = err * (1.0 / d)
        dr = _ln_bwd(dy, xhat, rstd, g_ref[...])
        dr_ref[...] = dr
        drb_ref[...] = dr.astype(bf16)
        dpe_ref[...] = (dr * sig).astype(bf16)
        dgp = dr * pe * sig * (1.0 - sig)
        dgp_ref[...] = dgp.astype(bf16)
        pg_ref[0:1, :] += jnp.sum(dy * xhat, 0, keepdims=True)
        pg_ref[1:2, :] += jnp.sum(dy, 0, keepdims=True)
        pg_ref[2:3, :] += jnp.sum(dgp, 0, keepdims=True)
        pg_ref[3:4, :] += 0.5 * jnp.sum(jnp.mean(err * err, -1, keepdims=True), 0, keepdims=True)

    return pl.pallas_call(
        body, name="ln2_loss", grid=(t // TOK_BLK,),
        in_specs=[_row_spec(d)] + _product_specs(lhs, rhs) + [_row_spec(pb.shape[1]), pl.BlockSpec(w_ple.shape, lambda i: (0, 0, 0)), _row_spec(d)]
        + [_vec_spec(1, d)] * 3 + [_row_spec(d)],
        out_specs=(_row_spec(d), _row_spec(d), _row_spec(d), _row_spec(d), _vec_spec(8, d)),
        out_shape=(jax.ShapeDtypeStruct((t, d), f32), jax.ShapeDtypeStruct((t, d), bf16), jax.ShapeDtypeStruct((t, d), bf16),
                   jax.ShapeDtypeStruct((t, d), bf16), jax.ShapeDtypeStruct((8, d), f32)),
        compiler_params=_params(("arbitrary",)),
    )(h1, lhs, rhs, pb, w_ple, gp, b_gate, g, b, target)


def _ln1_bwd(dr2, dup, w_up, dgp, w_gate, xhat, rstd, g, comm=None):
    t, d = dr2.shape
    ks = w_up.shape[2]

    def body(dr2_ref, dup_ref, wup_ref, dgp_ref, wg_ref, xh_ref, rs_ref, g_ref, dr_ref, drb_ref, pg_ref):
        i = pl.program_id(0)

        @pl.when(i == 0)
        def _():
            pg_ref[...] = jnp.zeros_like(pg_ref)

        dh = ALPHA * dr2_ref[...] + _dot_nt(dgp_ref[...], wg_ref[...], None)
        for s in range(w_up.shape[0]):
            dh = dh + _dot_nt(dup_ref[:, s * ks:(s + 1) * ks], wup_ref[s], None)
        xhat = xh_ref[...]
        dr = _ln_bwd(dh, xhat, rs_ref[:, 0:1], g_ref[...])
        dr_ref[...] = dr
        drb_ref[...] = dr.astype(bf16)
        pg_ref[0:1, :] += jnp.sum(dh * xhat, 0, keepdims=True)
        pg_ref[1:2, :] += jnp.sum(dh, 0, keepdims=True)

    return _hosted(
        body, comm, name="ln1_bwd", grid=(t // TOK_BLK,),
        in_specs=[_row_spec(d), _row_spec(dup.shape[1]), pl.BlockSpec(w_up.shape, lambda i: (0, 0, 0))] + _product_specs(dgp, w_gate)
        + [_row_spec(d), _row_spec(LANES), _vec_spec(1, d)],
        out_specs=(_row_spec(d), _row_spec(d), _vec_spec(8, d)),
        out_shape=(jax.ShapeDtypeStruct((t, d), f32), jax.ShapeDtypeStruct((t, d), bf16), jax.ShapeDtypeStruct((8, d), f32)),
        args=(dr2, dup, w_up, dgp, w_gate, xhat, rstd, g))


def _ln_in_bwd(x, dr1, dmm, g, comm=None):
    t, d = x.shape

    def body(x_ref, dr1_ref, dmm_ref, g_ref, dx_ref, pg_ref):
        i = pl.program_id(0)

        @pl.when(i == 0)
        def _():
            pg_ref[...] = jnp.zeros_like(pg_ref)

        dh = ALPHA * dr1_ref[...] + dmm_ref[...]
        _, xhat, rstd = _ln_fwd(x_ref[...], g_ref[...], 0.0)
        dx_ref[...] = _ln_bwd(dh, xhat, rstd, g_ref[...])
        pg_ref[0:1, :] += jnp.sum(dh * xhat, 0, keepdims=True)
        pg_ref[1:2, :] += jnp.sum(dh, 0, keepdims=True)

    return _hosted(
        body, comm, name="ln_in_bwd", grid=(t // TOK_BLK,),
        in_specs=[_row_spec(d)] * 3 + [_vec_spec(1, d)],
        out_specs=(_row_spec(d), _vec_spec(8, d)),
        out_shape=(jax.ShapeDtypeStruct((t, d), f32), jax.ShapeDtypeStruct((8, d), f32)),
        args=(x, dr1, dmm, g))


def _tri(n, upper=False, strict=False):
    r = lax.broadcasted_iota(jnp.int32, (n, n), 0)
    c = lax.broadcasted_iota(jnp.int32, (n, n), 1)
    if upper:
        m = (c > r) if strict else (c >= r)
    else:
        m = (c < r) if strict else (c <= r)
    return jnp.where(m, 1.0, 0.0).astype(f32)


def _gate_values(x, bias, alog, lane):
    z = x + bias
    return jnp.where(lane < 4, _sigmoid(z), jnp.where(lane < 8, -jnp.exp(alog) * _softplus(z), jnp.where(lane < 16, -_softplus(-z), 0.0)))


def _gates(proj, bias_row, alog_row):
    t = proj.shape[0]
    nch = t // CHUNK

    def body(x_ref, bias_ref, alog_ref, gates_ref, gcum_ref, gcumt_ref):
        lane = _lane((t, LANES))
        gates = _gate_values(x_ref[...], bias_ref[...], alog_ref[...], lane)
        gates_ref[...] = gates
        g3 = gates.reshape(nch, CHUNK, LANES)
        tri = jnp.broadcast_to(_tri(CHUNK)[None], (nch, CHUNK, CHUNK))
        loc = jnp.einsum("bij,bjk->bik", tri, g3, precision=HI, preferred_element_type=f32)
        tot = jnp.sum(g3, axis=1)
        offs = _dot(_tri(nch, strict=True), tot)
        glob = loc + offs[:, None, :]
        lane3 = _lane((nch, CHUNK, LANES))
        gcum = jnp.where(lane3 < 4, g3, jnp.where(lane3 < 8, loc, glob)).reshape(t, LANES)
        gcum_ref[...] = gcum
        gcumt_ref[...] = gcum.T

    return pl.pallas_call(
        body, name="gates", grid=(1,),
        in_specs=[pl.BlockSpec((t, LANES), lambda i: (0, SEG_SMALL // LANES)), _vec_spec(1, LANES), _vec_spec(1, LANES)],
        out_specs=(pl.BlockSpec((t, LANES), lambda i: (0, 0)), pl.BlockSpec((t, LANES), lambda i: (0, 0)),
                   pl.BlockSpec((LANES, t), lambda i: (0, 0))),
        out_shape=(jax.ShapeDtypeStruct((t, LANES), f32), jax.ShapeDtypeStruct((t, LANES), f32), jax.ShapeDtypeStruct((LANES, t), f32)),
        compiler_params=_params(("arbitrary",)),
    )(proj, bias_row, alog_row)


def _gates_bwd(proj, bias_row, alog_row, gates, dgates, dccol, dct):
    t = proj.shape[0]
    nch = t // CHUNK

    def body(x_ref, bias_ref, alog_ref, gates_ref, dg_ref, dcc_ref, dct_ref, dx_ref, pg_ref):
        lane = _lane((t, LANES))
        d = dg_ref[...] + dcc_ref[...] + dct_ref[...].T
        d3 = d.reshape(nch, CHUNK, LANES)
        tri = jnp.broadcast_to(_tri(CHUNK, upper=True)[None], (nch, CHUNK, CHUNK))
        loc = jnp.einsum("bij,bjk->bik", tri, d3, precision=HI, preferred_element_type=f32)
        tot = jnp.sum(d3, axis=1)
        offs = _dot(_tri(nch, upper=True, strict=True), tot)
        glob = loc + offs[:, None, :]
        lane3 = _lane((nch, CHUNK, LANES))
        dpre = jnp.where(lane3 < 4, d3, jnp.where(lane3 < 8, loc, glob)).reshape(t, LANES)
        z = x_ref[...] + bias_ref[...]
        sg = _sigmoid(z)
        dx = jnp.where(lane < 4, dpre * sg * (1.0 - sg),
                       jnp.where(lane < 8, dpre * (-jnp.exp(alog_ref[...])) * sg, jnp.where(lane < 16, dpre * (1.0 - sg), 0.0)))
        dx_ref[...] = dx.astype(bf16)
        pg_ref[...] = jnp.zeros_like(pg_ref)
        pg_ref[0:1, :] = jnp.sum(dx, 0, keepdims=True)
        pg_ref[1:2, :] = jnp.sum(jnp.where((lane >= 4) & (lane < 8), dpre * gates_ref[...], 0.0), 0, keepdims=True)

    full = pl.BlockSpec((t, LANES), lambda i: (0, 0))
    return pl.pallas_call(
        body, name="gates_bwd", grid=(1,),
        in_specs=[pl.BlockSpec((t, LANES), lambda i: (0, SEG_SMALL // LANES)), _vec_spec(1, LANES), _vec_spec(1, LANES),
                  full, full, full, pl.BlockSpec((LANES, t), lambda i: (0, 0))],
        out_specs=(full, _vec_spec(8, LANES)),
        out_shape=(jax.ShapeDtypeStruct((t, LANES), bf16), jax.ShapeDtypeStruct((8, LANES), f32)),
        compiler_params=_params(("arbitrary",)),
    )(proj, bias_row, alog_row, gates, dgates, dccol, dct)


def _conv_act(u, cw, row, t):
    c = cw[3:4, :] * u
    for jj in range(CONV_W - 1):
        sh = CONV_W - 1 - jj
        c = c + cw[jj:jj + 1, :] * jnp.where(row >= sh, pltpu.roll(u, sh, axis=0), 0.0)
    return c


def _gdn_conv(proj, conv_w, comm=None):
    t = proj.shape[0]
    nblk = GDN_QKV // LANES

    def body(u_ref, cw_ref, c_ref, y_ref):
        j = pl.program_id(0)
        row = lax.broadcasted_iota(jnp.int32, (t, LANES), 0)
        c = _conv_act(u_ref[...], cw_ref[...], row, t)
        c_ref[...] = c
        s = c * _sigmoid(c)
        r = lax.rsqrt(jnp.sum(s * s, -1, keepdims=True) + NORM_EPS)
        scale = jnp.where(j < GDN_HEADS, GDN_DK ** -0.5, 1.0)
        y_ref[...] = jnp.where(j < 2 * GDN_HEADS, s * (r * scale), s)

    blk = pl.BlockSpec((t, LANES), lambda j: (0, j))
    return _hosted(
        body, comm, name="gdn_conv", grid=(nblk,),
        in_specs=[blk, pl.BlockSpec((CONV_W, LANES), lambda j: (0, j))],
        out_specs=(blk, blk),
        out_shape=(jax.ShapeDtypeStruct((t, GDN_QKV), f32), jax.ShapeDtypeStruct((t, GDN_QKV), f32)),
        args=(proj, conv_w))


def _gdn_conv_bwd(proj, conv_w, c, dy, comm=None):
    t = proj.shape[0]
    nblk = GDN_QKV // LANES

    def body(u_ref, cw_ref, c_ref, dy_ref, du_ref, dcw_ref):
        j = pl.program_id(0)
        row = lax.broadcasted_iota(jnp.int32, (t, LANES), 0)
        u = u_ref[...]
        cw = cw_ref[...]
        c = c_ref[...]
        dy = dy_ref[...]
        sg = _sigmoid(c)
        s = c * sg
        r = lax.rsqrt(jnp.sum(s * s, -1, keepdims=True) + NORM_EPS)
        n = s * r
        scale = jnp.where(j < GDN_HEADS, GDN_DK ** -0.5, 1.0)
        dn = dy * scale
        ds = jnp.where(j < 2 * GDN_HEADS, r * (dn - n * jnp.sum(dn * n, -1, keepdims=True)), dy)
        dc = ds * (sg * (1.0 + c * (1.0 - sg)))
        du = cw[3:4, :] * dc
        dcw_ref[...] = jnp.zeros_like(dcw_ref)
        dcw_ref[3:4, :] = jnp.sum(dc * u, 0, keepdims=True)
        for jj in range(CONV_W - 1):
            sh = CONV_W - 1 - jj
            du = du + cw[jj:jj + 1, :] * jnp.where(row < t - sh, pltpu.roll(dc, t - sh, axis=0), 0.0)
            dcw_ref[jj:jj + 1, :] = jnp.sum(dc * jnp.where(row >= sh, pltpu.roll(u, sh, axis=0), 0.0), 0, keepdims=True)
        du_ref[...] = du.astype(bf16)

    blk = pl.BlockSpec((t, LANES), lambda j: (0, j))
    return _hosted(
        body, comm, name="gdn_conv_bwd", grid=(nblk,),
        in_specs=[blk, pl.BlockSpec((CONV_W, LANES), lambda j: (0, j)), blk, blk],
        out_specs=(blk, pl.BlockSpec((8, LANES), lambda j: (0, j))),
        out_shape=(jax.ShapeDtypeStruct((t, GDN_QKV), bf16), jax.ShapeDtypeStruct((8, GDN_QKV), f32)),
        args=(proj, conv_w, c, dy))


def _chunk_masks():
    r = lax.broadcasted_iota(jnp.int32, (CHUNK, CHUNK), 0)
    c = lax.broadcasted_iota(jnp.int32, (CHUNK, CHUNK), 1)
    return r >= c, r > c, r == c


def _col_to_row(col, eye):
    return jnp.sum(jnp.where(eye, col, 0.0), axis=0, keepdims=True)


def _row_to_col(row, eye):
    return jnp.sum(jnp.where(eye, row, 0.0), axis=1, keepdims=True)


NN = (((1,), (0,)), ((), ()))
NT = (((1,), (1,)), ((), ()))
TN = (((0,), (0,)), ((), ()))
GDN_GROUP = 4


def _mx(a, b, dims=NN, passes=1):
    d = lambda p, q: lax.dot_general(p, q, dims, preferred_element_type=f32)
    ah, bh = a.astype(bf16), b.astype(bf16)
    if passes == 1:
        return d(ah, bh)
    al = (a - ah.astype(f32)).astype(bf16)
    bl = (b - bh.astype(f32)).astype(bf16)
    return d(ah, bh) + (d(ah, bl) + d(al, bh))


def _gdn_decay(gam, masks):
    causal, _, eye = masks
    return jnp.exp(jnp.where(causal, gam - _col_to_row(gam, eye), NEG))


def _gdn_local(y, gcum, comm=None):
    t = y.shape[0]
    nch = t // CHUNK
    rows_blk = GDN_GROUP * CHUNK

    def body(y_ref, g_ref, u_ref, w_ref, qk_ref, tinv_ref):
        masks = _chunk_masks()
        _, strict, eye = masks
        ids = [(j, h) for j in range(GDN_GROUP) for h in range(GDN_HEADS)]
        rs = lambda j: slice(j * CHUNK, (j + 1) * CHUNK)
        col = lambda base, h: slice(base + h * LANES, base + (h + 1) * LANES)
        kn = [y_ref[rs(j), col(512, h)] for j, h in ids]
        beta = [g_ref[rs(j), h:h + 1] for j, h in ids]
        gam = [g_ref[rs(j), 4 + h:5 + h] for j, h in ids]
        dec = [_gdn_decay(g, masks) for g in gam]
        x = [-jnp.where(strict, _mx(k, k, NT) * d * b, 0.0) for k, d, b in zip(kn, dec, beta)]
        tinv = [jnp.where(eye, 1.0, 0.0) + a for a in x]
        for _ in range(5):
            x = [_mx(a, a, NN, 3) for a in x]
            tinv = [t_ + _mx(t_, a, NN, 3) for t_, a in zip(tinv, x)]
        for (j, h), t_, k, d, b, g in zip(ids, tinv, kn, dec, beta, gam):
            u_ref[rs(j), col(0, h)] = _mx(t_, b * y_ref[rs(j), col(1024, h)])
            w_ref[rs(j), col(0, h)] = _mx(t_, (b * jnp.exp(g)) * k)
            qk_ref[j, h] = _mx(y_ref[rs(j), col(0, h)], k, NT) * d
            tinv_ref[j, h] = t_

    mat = pl.BlockSpec((GDN_GROUP, GDN_HEADS, CHUNK, CHUNK), lambda n: (n, 0, 0, 0))
    return _hosted(
        body, comm, name="gdn_local", grid=(nch // GDN_GROUP,),
        in_specs=[pl.BlockSpec((rows_blk, GDN_QKV), lambda n: (n, 0)), pl.BlockSpec((rows_blk, LANES), lambda n: (n, 0))],
        out_specs=(pl.BlockSpec((rows_blk, 512), lambda n: (n, 0)), pl.BlockSpec((rows_blk, 512), lambda n: (n, 0)), mat, mat),
        out_shape=(jax.ShapeDtypeStruct((t, 512), f32), jax.ShapeDtypeStruct((t, 512), f32),
                   jax.ShapeDtypeStruct((nch, GDN_HEADS, CHUNK, CHUNK), f32), jax.ShapeDtypeStruct((nch, GDN_HEADS, CHUNK, CHUNK), f32)),
        args=(y, gcum))


def _gdn_fwd(y, gcum, u, w, qk, comm=None):
    t = y.shape[0]
    nch = t // CHUNK

    def body(y_ref, g_ref, u_ref, w_ref, qk_ref, o_ref, sall_ref, s_ref):
        @pl.when(pl.program_id(0) == 0)
        def _():
            s_ref[...] = jnp.zeros_like(s_ref)

        heads = range(GDN_HEADS)
        sl = [slice(h * LANES, (h + 1) * LANES) for h in heads]
        gam = [g_ref[:, 4 + h:5 + h] for h in heads]
        gam_last = [g[CHUNK - 1:CHUNK, :] for g in gam]
        s = [s_ref[h] for h in heads]
        for h in heads:
            sall_ref[0, h] = s[h]
        ws = [_mx(w_ref[:, sl[h]], s[h]) for h in heads]
        qs = [_mx(y_ref[:, sl[h]] * jnp.exp(gam[h]), s[h]) for h in heads]
        vn = [u_ref[:, sl[h]] - ws[h] for h in heads]
        av = [_mx(qk_ref[0, h], vn[h]) for h in heads]
        kv = [_mx(y_ref[:, 512 + h * LANES:512 + (h + 1) * LANES] * jnp.exp(gam_last[h] - gam[h]), vn[h], TN) for h in heads]
        for h in heads:
            o_ref[:, sl[h]] = qs[h] + av[h]
            s_ref[h] = jnp.exp(gam_last[h]) * s[h] + kv[h]

    row = lambda width: pl.BlockSpec((CHUNK, width), lambda n: (n, 0))
    return _hosted(
        body, comm, name="gdn_fwd", grid=(nch,),
        in_specs=[row(GDN_QKV), row(LANES), row(512), row(512), pl.BlockSpec((1, GDN_HEADS, CHUNK, CHUNK), lambda n: (n, 0, 0, 0))],
        out_specs=(row(512), pl.BlockSpec((1, GDN_HEADS, LANES, LANES), lambda n: (n, 0, 0, 0))),
        out_shape=(jax.ShapeDtypeStruct((t, 512), f32), jax.ShapeDtypeStruct((nch, GDN_HEADS, LANES, LANES), f32)),
        scratch_shapes=[pltpu.VMEM((GDN_HEADS, LANES, LANES), f32)],
        args=(y, gcum, u, w, qk))


def _gdn_bwd(y, gcum, u_all, w_all, qk_all, tinv_all, sall, do, comm=None):
    t = y.shape[0]
    nch = t // CHUNK

    def body(y_ref, g_ref, u_ref, w_ref, qk_ref, tinv_ref, sall_ref, do_ref, dy_ref, dg_ref, ds_ref):
        @pl.when(pl.program_id(0) == 0)
        def _():
            ds_ref[...] = jnp.zeros_like(ds_ref)

        masks = _chunk_masks()
        causal, strict, eye = masks
        lane = _lane((CHUNK, LANES))
        row = lax.broadcasted_iota(jnp.int32, (CHUNK, 1), 0)
        heads = range(GDN_HEADS)
        each = lambda f, *ls: [f(*a) for a in zip(*ls)]
        rsum = lambda a: jnp.sum(a, axis=1, keepdims=True)
        sl = [slice(h * LANES, (h + 1) * LANES) for h in heads]
        qn = [y_ref[:, sl[h]] for h in heads]
        kn = [y_ref[:, 512 + h * LANES:512 + (h + 1) * LANES] for h in heads]
        v = [y_ref[:, 1024 + h * LANES:1024 + (h + 1) * LANES] for h in heads]
        beta = [g_ref[:, h:h + 1] for h in heads]
        gam = [g_ref[:, 4 + h:5 + h] for h in heads]
        gam_last = [g[CHUNK - 1:CHUNK, :] for g in gam]
        dec = [_gdn_decay(g, masks) for g in gam]
        e = [jnp.exp(g) for g in gam]
        f = each(lambda gl_, g: jnp.exp(gl_ - g), gam_last, gam)
        gl = [jnp.exp(g) for g in gam_last]
        u = [u_ref[:, sl[h]] for h in heads]
        w = [w_ref[:, sl[h]] for h in heads]
        qk = [qk_ref[0, h] for h in heads]
        tinv = [tinv_ref[0, h] for h in heads]
        s = [sall_ref[0, h] for h in heads]
        dsn = [ds_ref[h] for h in heads]
        d_o = [do_ref[:, sl[h]] for h in heads]
        qd = each(lambda a, b: a * b, qn, e)
        kd = each(lambda a, b: a * b, kn, f)
        ws = each(_mx, w, s)
        kds = each(_mx, kd, dsn)
        qkdo = each(lambda a, b: _mx(a, b, TN), qk, d_o)
        dqd = each(lambda a, b: _mx(a, b, NT), d_o, s)
        qddo = each(lambda a, b: _mx(a, b, TN), qd, d_o)
        kkd = each(lambda k, d: _mx(k, k, NT) * d, kn, dec)
        vn = each(lambda a, b: a - b, u, ws)
        dvn = each(lambda a, b: a + b, qkdo, kds)
        dqk = each(lambda a, b: jnp.where(causal, _mx(a, b, NT), 0.0), d_o, vn)
        dkd = each(lambda a, b: _mx(a, b, NT), vn, dsn)
        dw = each(lambda a, b: -_mx(a, b, NT), dvn, s)
        wdvn = each(lambda a, b: _mx(a, b, TN), w, dvn)
        dgl = each(lambda a, b: jnp.sum(rsum(a * b), axis=0, keepdims=True), dsn, s)
        for h in heads:
            ds_ref[h] = qddo[h] - wdvn[h] + gl[h] * dsn[h]
        dru = each(lambda a, b: _mx(a, b, TN), tinv, dvn)
        drw = each(lambda a, b: _mx(a, b, TN), tinv, dw)
        dqkr = each(lambda a, b: a * b, dqk, dec)
        dq1 = each(_mx, dqkr, kn)
        dk1 = each(lambda a, b: _mx(a, b, TN), dqkr, qn)
        dnu = each(lambda a, b: _mx(a, b, NT), dru, u)
        dnw = each(lambda a, b: _mx(a, b, NT), drw, w)
        dn = each(lambda a, b: jnp.where(strict, -(a + b), 0.0), dnu, dnw)
        dkk = each(lambda a, b, d: a * b * d, dn, beta, dec)
        dk2 = each(_mx, dkk, kn)
        dk3 = each(lambda a, b: _mx(a, b, TN), dkk, kn)
        dgates = jnp.zeros((CHUNK, LANES), f32)
        for h in heads:
            drw_k = rsum(drw[h] * kn[h])
            dbeta = rsum(dru[h] * v[h]) + e[h] * drw_k + rsum(dn[h] * kkd[h])
            m = dn[h] * (kkd[h] * beta[h]) + dqk[h] * qk[h]
            de = beta[h] * drw_k + rsum(dqd[h] * qn[h])
            df = rsum(dkd[h] * kn[h])
            dgam = rsum(m) - _row_to_col(jnp.sum(m, axis=0, keepdims=True), eye) + de * e[h] - df * f[h]
            dgam_last = jnp.sum(df * f[h], axis=0, keepdims=True) + dgl[h] * gl[h]
            dgam = dgam + jnp.where(row == CHUNK - 1, dgam_last, 0.0)
            dy_ref[:, sl[h]] = dq1[h] + dqd[h] * e[h]
            dy_ref[:, 512 + h * LANES:512 + (h + 1) * LANES] = (beta[h] * e[h]) * drw[h] + dk2[h] + dk3[h] + dk1[h] + dkd[h] * f[h]
            dy_ref[:, 1024 + h * LANES:1024 + (h + 1) * LANES] = beta[h] * dru[h]
            dgates = dgates + jnp.where(lane == h, dbeta, 0.0) + jnp.where(lane == 4 + h, dgam, 0.0)
        dg_ref[...] = dgates

    rev = lambda width: pl.BlockSpec((CHUNK, width), lambda n: (nch - 1 - n, 0))
    mat = lambda d: pl.BlockSpec((1, GDN_HEADS, d, d), lambda n: (nch - 1 - n, 0, 0, 0))
    return _hosted(
        body, comm, name="gdn_bwd", grid=(nch,),
        in_specs=[rev(GDN_QKV), rev(LANES), rev(512), rev(512), mat(CHUNK), mat(CHUNK), mat(LANES), rev(512)],
        out_specs=(rev(GDN_QKV), rev(LANES)),
        out_shape=(jax.ShapeDtypeStruct((t, GDN_QKV), f32), jax.ShapeDtypeStruct((t, LANES), f32)),
        scratch_shapes=[pltpu.VMEM((GDN_HEADS, LANES, LANES), f32)],
        args=(y, gcum, u_all, w_all, qk_all, tinv_all, sall, do))


FOX_CLASSES = 4


def _fox_groups(t):
    nq = t // FOX_BQ
    ncls = min(FOX_CLASSES, nq)
    per = nq // ncls
    return [(g * per, per, (g + 1) * per * FOX_BQ) for g in range(ncls)]


def _fox_causal(i, keys):
    rows = i * FOX_BQ + lax.broadcasted_iota(jnp.int32, (FOX_BQ, keys), 0)
    return lax.broadcasted_iota(jnp.int32, (FOX_BQ, keys), 1) <= rows


def _fox_scores(q_ref, k_ref, gcumt_ref, h, causal):
    pr = h // 2
    lo = (h % 2) * FOX_DH
    lane = _lane((FOX_BQ, LANES))
    mask = (lane >= lo) & (lane < lo + FOX_DH)
    qm = jnp.where(mask, q_ref[:, pr * LANES:(pr + 1) * LANES] * (FOX_DH ** -0.5), 0.0).astype(bf16)
    kp = k_ref[:, pr * LANES:(pr + 1) * LANES].astype(bf16)
    s = _dot_nt(qm, kp, None) - gcumt_ref[8 + h:9 + h, :]
    return jnp.where(causal, s, NEG), mask, qm, kp


def _fox_fwd(proj, gcumt, ride=None):
    c0 = SEG_FOX // 512

    def group_call(q0, nq, keys, comm):
        def body(q_ref, k_ref, v_ref, gcumt_ref, o_ref, lse_ref):
            causal = _fox_causal(q0 + pl.program_id(0), keys)
            lane = _lane((FOX_BQ, LANES))
            lse_all = jnp.zeros((FOX_BQ, LANES), f32)
            for pr in range(FOX_HEADS // 2):
                vp = v_ref[:, pr * LANES:(pr + 1) * LANES].astype(bf16)
                o_pair = jnp.zeros((FOX_BQ, LANES), f32)
                for h in (2 * pr, 2 * pr + 1):
                    s, mask, _, _ = _fox_scores(q_ref, k_ref, gcumt_ref, h, causal)
                    m = jnp.max(s, axis=1, keepdims=True)
                    p = jnp.exp(s - m)
                    l = jnp.sum(p, axis=1, keepdims=True)
                    o_h = _dot(p.astype(bf16), vp, None) * (1.0 / l)
                    o_pair = jnp.where(mask, o_h, o_pair)
                    lse_all = jnp.where(lane == h, m + jnp.log(l), lse_all)
                o_ref[:, pr * LANES:(pr + 1) * LANES] = o_pair
            lse_ref[...] = lse_all

        seen = lambda col: pl.BlockSpec((keys, 512), lambda i: (0, col))
        return _hosted(
            body, comm, name=f"fox_fwd_{keys}", grid=(nq,),
            in_specs=[pl.BlockSpec((FOX_BQ, 512), lambda i: (q0 + i, c0)), seen(c0 + 1), seen(c0 + 2),
                      pl.BlockSpec((LANES, keys), lambda i: (0, 0))],
            out_specs=(pl.BlockSpec((FOX_BQ, 512), lambda i: (i, 0)), pl.BlockSpec((FOX_BQ, LANES), lambda i: (i, 0))),
            out_shape=(jax.ShapeDtypeStruct((nq * FOX_BQ, 512), f32), jax.ShapeDtypeStruct((nq * FOX_BQ, LANES), f32)),
            args=(proj, proj, proj, gcumt))

    parts = []
    for n, g in enumerate(_fox_groups(proj.shape[0])):
        hook = ride(n) if ride else None
        part, moved = group_call(*g, hook[0] if hook else None)
        parts.append(part)
        if hook:
            hook[1](moved)
    return jnp.concatenate([o for o, _ in parts], axis=0), jnp.concatenate([l for _, l in parts], axis=0)


def _fox_bwd(proj, gcumt, o, lse, do, ride=None):
    t = proj.shape[0]
    c0 = SEG_FOX // 512

    def group_call(q0, nq, keys, acc, comm):
        first = acc is None

        def body(q_ref, k_ref, v_ref, gcumt_ref, o_ref, lse_ref, do_ref, *rest):
            dq_ref, dk_ref, dv_ref, dcc_ref, dct_ref = rest[-5:]
            j = pl.program_id(0)
            causal = _fox_causal(q0 + j, keys)

            @pl.when(j == 0)
            def _():
                if first:
                    dk_ref[...] = jnp.zeros_like(dk_ref)
                    dv_ref[...] = jnp.zeros_like(dv_ref)
                    dct_ref[...] = jnp.zeros_like(dct_ref)
                else:
                    dk_ref[...], dv_ref[...], dct_ref[...] = rest[0][...], rest[1][...], rest[2][...]

            lane = _lane((FOX_BQ, LANES))
            dcc = jnp.zeros((FOX_BQ, LANES), f32)
            scale = FOX_DH ** -0.5
            for pr in range(FOX_HEADS // 2):
                sl = slice(pr * LANES, (pr + 1) * LANES)
                vp = v_ref[:, sl].astype(bf16)
                dq_pair = jnp.zeros((FOX_BQ, LANES), f32)
                for h in (2 * pr, 2 * pr + 1):
                    s, mask, qm, kp = _fox_scores(q_ref, k_ref, gcumt_ref, h, causal)
                    p = jnp.exp(s - lse_ref[:, h:h + 1])
                    dom = jnp.where(mask, do_ref[:, sl], 0.0)
                    delta = jnp.sum(dom * o_ref[:, sl], axis=1, keepdims=True)
                    domb = dom.astype(bf16)
                    ds = p * (_dot_nt(domb, vp, None) - delta)
                    dsb = ds.astype(bf16)
                    dv_ref[:, sl] += _dot_tn(p.astype(bf16), domb, None)
                    dk_ref[:, sl] += _dot_tn(dsb, qm, None)
                    dq_pair = jnp.where(mask, _dot(dsb, kp, None) * scale, dq_pair)
                    dcc = jnp.where(lane == 8 + h, jnp.sum(ds, axis=1, keepdims=True), dcc)
                    dct_ref[8 + h:9 + h, :] += -jnp.sum(ds, axis=0, keepdims=True)
                dq_ref[:, sl] = dq_pair.astype(bf16)
            dcc_ref[...] = dcc

        qblk = lambda col: pl.BlockSpec((FOX_BQ, 512), lambda i: (q0 + i, col))
        oblk = pl.BlockSpec((FOX_BQ, 512), lambda i: (i, 0))
        seen = lambda col: pl.BlockSpec((keys, 512), lambda i: (0, col))
        rblk = pl.BlockSpec((FOX_BQ, LANES), lambda i: (q0 + i, 0))
        seen_t = pl.BlockSpec((LANES, keys), lambda i: (0, 0))
        in_specs = [qblk(c0), seen(c0 + 1), seen(c0 + 2), seen_t, qblk(0), rblk, qblk(0)]
        args = [proj, proj, proj, gcumt, o, lse, do]
        aliases = {}
        if not first:
            in_specs += [seen(0), seen(0), seen_t]
            args += list(acc)
            aliases = {7: 1, 8: 2, 9: 4}
        return _hosted(
            body, comm, name=f"fox_bwd_{keys}", grid=(nq,), in_specs=in_specs,
            out_specs=(oblk, seen(0), seen(0), pl.BlockSpec((FOX_BQ, LANES), lambda i: (i, 0)), seen_t),
            out_shape=(jax.ShapeDtypeStruct((nq * FOX_BQ, 512), bf16), jax.ShapeDtypeStruct((t, 512), f32), jax.ShapeDtypeStruct((t, 512), f32),
                       jax.ShapeDtypeStruct((nq * FOX_BQ, LANES), f32), jax.ShapeDtypeStruct((LANES, t), f32)),
            aliases=aliases, args=args)

    acc, dqs, dccs = None, [], []
    for n, g in enumerate(reversed(_fox_groups(t))):
        hook = ride(n) if ride else None
        (dq, dk, dv, dcc, dct), moved = group_call(*g, acc, hook[0] if hook else None)
        if hook:
            hook[1](moved)
        acc = (dk, dv, dct)
        dqs.insert(0, dq)
        dccs.insert(0, dcc)
    return jnp.concatenate(dqs, axis=0), acc[0], acc[1], jnp.concatenate(dccs, axis=0), acc[2]


def _row(v, width=None):
    v = v.reshape(1, -1).astype(f32)
    if width is not None and v.shape[1] < width:
        v = jnp.pad(v, ((0, 0), (0, width - v.shape[1])))
    return v


LATE = ("w_out", "w_up", "w_ple_gate", "w_ple", "w_down")


def _device_grads(x, p, target, small, w_cat, conv_w, late, qc=None, tail=None, ln_in_out=None):
    z4 = jnp.zeros((4,), f32)
    bias_row = _row(jnp.concatenate([z4, small["dt_bias"].reshape(-1), small["b_f"].reshape(-1)]), LANES)
    alog_row = _row(jnp.concatenate([z4, small["a_log"].reshape(-1)]), LANES)
    g_gdn = _row(small["gdn_norm_g"])
    g_fox2 = _row(jnp.tile(small["fox_norm_g"].reshape(-1), 2))
    pb = p.astype(bf16)
    late = list(late)
    comm = qc is not None

    h0, h0b = ln_in_out if ln_in_out is not None else _ln_in(x, _row(small["ln_in_g"]), _row(small["ln_in_b"]))[0]
    proj = _mm(h0b, w_cat, "nt", 512, D_CAT, "mm_proj")
    gates, gcum, gcumt = _gates(proj, bias_row, alog_row)
    w_down_pieces = [(4, 0, 1)]

    def gather(phase, pieces):
        if not comm or not pieces:
            return None, lambda moved: None
        touched = sorted({i for i, _, _ in pieces})

        def took(moved):
            for i, buf in zip(touched, moved):
                late[i] = buf
        return phase([late[i] for i in touched], [(touched.index(i), k, n) for i, k, n in pieces]), took

    over, on = _gather_chips, _gather_pass_on
    cm, took = gather(over, [(0, 0, 1), (3, 0, 1)])
    (conv_c, qkv_n), moved = _gdn_conv(proj, conv_w, cm)
    took(moved)
    cm, took = gather(over, [(1, 0, 2)])
    (gu, gw, gqk, gtinv), moved = _gdn_local(qkv_n, gcum, cm)
    took(moved)
    cm, took = gather(over, [(1, 1, 2)])
    (o_gdn, sall), moved = _gdn_fwd(qkv_n, gcum, gu, gw, gqk, cm)
    took(moved)
    fox_plan = [(over, []), (on, [(0, 0, 1), (3, 0, 1), (1, 0, 2), (1, 1, 2)]), (over, [(2, 0, 1)]), (over, [(4, 0, 4)])]
    assert not comm or len(_fox_groups(x.shape[0])) == len(fox_plan)
    o_fox, lse = _fox_fwd(proj, gcumt, (lambda n: gather(*fox_plan[n])) if comm else None)
    cm, took = gather(on, [(2, 0, 1)])
    (attn,), moved = _attn_post(o_gdn, proj, o_fox, g_gdn, g_fox2, cm)
    took(moved)
    w_out = late[0].reshape(D_MODEL, D_MODEL)
    cm, took = gather(over, [(4, 1, 4)])
    (h1, h1b, xhat1, rstd1), moved = _ln1(h0, attn, w_out, _row(small["ln1_g"]), _row(small["ln1_b"]), cm)
    took(moved)
    w_up, w_ple = late[1], late[3]
    cm, took = gather(over, [(4, 1, 2)])
    up_act = _mm(h1b, w_up, "nn", 512, 1024, "mm_up", epi="relu2", shards=N_CHIPS, comm=cm)
    if cm:
        up_act, moved = up_act
        took(moved)
    up, act = up_act
    w_gate = late[2].reshape(D_MODEL, D_MODEL)
    cm, took = gather(on, w_down_pieces)
    gp = _mm(h1b, w_gate, "nn", 512, D_MODEL, "mm_gate", comm=cm)
    if cm:
        gp, moved = gp
        took(moved)
    w_down = late[4].reshape(D_FF, D_MODEL)
    dr2, dr2b, dpe, dgp, pg2 = _ln2_loss(h1, act, w_down, pb, w_ple, gp, _row(small["b_ple_gate"]), _row(small["ln2_g"]),
                                         _row(small["ln2_b"]), target)

    by_dest = lambda g: g.reshape((N_CHIPS, -1, g.shape[-1]))
    g_late = [None] * len(LATE)
    state = dict(from_sibling=[None] * len(LATE), sent=[None] * len(LATE), landing=[None] * len(LATE))
    nothing = (None, lambda moved: None)

    def to_sibling(idx):
        if not comm:
            return nothing

        def took(moved):
            for i, b1 in zip(idx, moved):
                state["from_sibling"][i] = b1
                state["sent"][i] = _add_pair(g_late[i], b1, qc, "add_pair_" + LATE[i])
                state["landing"][i] = _landing([state["sent"][i]])[0]
        return _exchange_pairs([g_late[i] for i in idx]), took

    def to_chips(pieces):
        if not comm:
            return nothing
        touched = sorted({i for i, _, _ in pieces})

        def took(moved):
            for i, b2 in zip(touched, moved):
                state["landing"][i] = b2
        return _exchange_chips([state["sent"][i] for i in touched], [state["landing"][i] for i in touched],
                               [(touched.index(i), k, n) for i, k, n in pieces]), took

    def ride(result, cm, took):
        if cm:
            result, moved = result
            took(moved)
        return result

    dup = _mm(dr2b, w_down, "nt", 512, 2048, "mm_dact", epi="relu2_bwd", extra=up)
    g_late[4] = by_dest(_mm(act, dr2b, "tn", 1024, D_MODEL, "mm_gdown"))
    cm, took = to_sibling([4])
    g_late[1] = ride(_mm(h1b, dup, "tn", 1024, 1024, "mm_gup", shards=N_CHIPS, comm=cm), cm, took)
    g_late[2] = by_dest(_mm(h1b, dgp, "tn", 1024, D_MODEL, "mm_ggate"))
    g_late[3] = _mm(pb, dpe, "tn", D_PLE, D_MODEL // N_CHIPS, "mm_gple", shards=N_CHIPS)
    cm, took = to_chips([(4, 0, 2)])
    (dr1, dr1b, pg1), moved = _ln1_bwd(dr2, dup, w_up, dgp, w_gate, xhat1, rstd1, _row(small["ln1_g"]), cm)
    took(moved)
    g_late[0] = by_dest(_mm(attn, dr1b, "tn", 1024, D_MODEL, "mm_gout"))
    do_gdn, dz, do_fox, pga = _attn_post_bwd(dr1b, w_out, o_gdn, proj, o_fox, g_gdn, g_fox2)
    chip_plan = [[(4, 1, 2), (1, 0, 2)], [(1, 1, 2)], [(0, 0, 1)], [(2, 0, 1), (3, 0, 1)]]

    def gdn_backward():
        cm, took = to_chips(chip_plan[0])
        state["gdn"], moved = _gdn_bwd(qkv_n, gcum, gu, gw, gqk, gtinv, sall, do_gdn, cm)
        took(moved)

    def fox_ride(n):
        if n == 0:
            return to_sibling([1, 0, 2, 3])
        if n == 1:
            gdn_backward()
        return to_chips(chip_plan[n])

    assert not comm or len(_fox_groups(x.shape[0])) == len(chip_plan)
    dfq, dfk, dfv, dccol, dct = _fox_bwd(proj, gcumt, o_fox, lse, do_fox, fox_ride if comm else None)
    if not comm:
        gdn_backward()
    dqkv_n, dgates = state["gdn"]
    dsmall, pgg = _gates_bwd(proj, bias_row, alog_row, gates, dgates, dccol, dct)
    cm = None
    if comm:
        cm = _share_halves([_add_chips(g, b1, b2, qc, "add_chips_" + n)
                            for g, b1, b2, n in zip(g_late, state["from_sibling"], state["landing"], LATE)])
    (du, g_conv8), reduced = _gdn_conv_bwd(proj, conv_w, conv_c, dqkv_n, cm)
    if comm:
        g_late = list(reduced)
    t = x.shape[0]
    dproj = jnp.concatenate([du, dz, dfq, dfk.astype(bf16), dfv.astype(bf16), dsmall, jnp.zeros((t, D_CAT - SEG_SMALL - LANES), bf16)], axis=1)
    g_cat = _mm(dproj, h0b, "tn", 1280, D_MODEL, "mm_gcat")
    cm, took = tail[0](g_cat) if tail else (None, None)
    dh0_mm = _mm(dproj, w_cat, "nn", 512, D_MODEL, "mm_dh0", comm=cm)
    if cm:
        dh0_mm, moved = dh0_mm
        took(moved)
    cm, took = tail[1]() if tail and tail[1] else (None, None)
    (grad_x, pg0), moved = _ln_in_bwd(x, dr1, dh0_mm, _row(small["ln_in_g"]), cm)
    if cm:
        took(moved)

    g_fox = pga[1, :FOX_DH] + pga[1, FOX_DH:]
    small_grads = dict(
        ln_in_g=pg0[0], ln_in_b=pg0[1], ln1_g=pg1[0], ln1_b=pg1[1], b_ple_gate=pg2[2], ln2_g=pg2[0], ln2_b=pg2[1],
        gdn_norm_g=pga[0], fox_norm_g=g_fox, a_log=pgg[1, 4:8], dt_bias=pgg[0, 4:8], b_f=pgg[0, 8:16], loss=pg2[3, 0:1])
    return grad_x, g_cat, g_conv8[:CONV_W], dict(zip(LATE, g_late)), small_grads


ANY = pl.BlockSpec(memory_space=pl.ANY)
CONV_PKT_ROWS = 16


def _mesh_pos():
    return lax.axis_index("x"), lax.axis_index("y"), lax.axis_index("c")


def _other_chips(x, y):
    return [(1 - x, y), (x, 1 - y), (1 - x, 1 - y)]


def _rcopy(src, dst, send_sem, recv_sem, dev):
    return pltpu.make_async_remote_copy(src_ref=src, dst_ref=dst, send_sem=send_sem, recv_sem=recv_sem,
                                        device_id=dev, device_id_type=MESH)


class _Comm:
    def __init__(self, ins, outs, aliases, n_sems, start, finish):
        self.ins, self.outs, self.aliases, self.n_sems, self.start, self.finish = list(ins), list(outs), dict(aliases), n_sems, start, finish


def _hosted(body, comm, *, name, grid, in_specs, out_specs, out_shape, args, scratch_shapes=(), aliases=None):
    n_in, n_out, n_sc = len(in_specs), len(out_specs), len(scratch_shapes)
    k, ko = (len(comm.ins), len(comm.outs)) if comm else (0, 0)

    def kernel_body(*refs):
        o0 = n_in + k
        s0 = o0 + n_out + ko
        if comm:
            cins, couts, (ssem, rsem) = refs[n_in:o0], refs[o0 + n_out:s0], refs[s0 + n_sc:]
            step = pl.program_id(0)
            for d in range(1, len(grid)):
                step = step * grid[d] + pl.program_id(d)

            @pl.when(step == 0)
            def _():
                comm.start(cins, couts, ssem, rsem)

        body(*refs[:n_in], *refs[o0:o0 + n_out], *refs[s0:s0 + n_sc])
        if comm:
            last = 1
            for n in grid:
                last *= n

            @pl.when(step == last - 1)
            def _():
                comm.finish(cins, couts, ssem, rsem)

    io_aliases = dict(aliases or {})
    scratch = list(scratch_shapes)
    if comm:
        io_aliases.update({n_in + i: n_out + j for i, j in comm.aliases.items()})
        scratch += [pltpu.SemaphoreType.DMA((comm.n_sems,)), pltpu.SemaphoreType.DMA((comm.n_sems,))]
    res = pl.pallas_call(
        kernel_body, name=name, grid=grid, in_specs=list(in_specs) + [ANY] * k, out_specs=tuple(out_specs) + (ANY,) * ko,
        out_shape=tuple(out_shape) + tuple(comm.outs if comm else ()), scratch_shapes=scratch, input_output_aliases=io_aliases,
        compiler_params=_params(("arbitrary",) * len(grid)),
    )(*args, *(comm.ins if comm else ()))
    return tuple(res[:n_out]), tuple(res[n_out:])


def _comm_only(phases, name):
    n_in = sum(len(p.ins) for p in phases)

    def body(*refs):
        n_out = sum(len(p.outs) for p in phases)
        sems = refs[n_in + n_out:]
        i0, o0 = 0, n_in
        for j, p in enumerate(phases):
            cins, couts = refs[i0:i0 + len(p.ins)], refs[o0:o0 + len(p.outs)]
            p.start(cins, couts, sems[2 * j], sems[2 * j + 1])
            p.finish(cins, couts, sems[2 * j], sems[2 * j + 1])
            i0 += len(p.ins)
            o0 += len(p.outs)

    aliases, i0, o0 = {}, 0, 0
    for p in phases:
        aliases.update({i0 + i: o0 + j for i, j in p.aliases.items()})
        i0 += len(p.ins)
        o0 += len(p.outs)
    outs = [o for p in phases for o in p.outs]
    res = pl.pallas_call(
        body, name=name, out_shape=tuple(outs), in_specs=[ANY] * n_in, out_specs=(ANY,) * len(outs), input_output_aliases=aliases,
        scratch_shapes=[pltpu.SemaphoreType.DMA((p.n_sems,)) for p in phases for _ in range(2)],
    )(*[a for p in phases for a in p.ins])
    split, o0 = [], 0
    for p in phases:
        split.append(tuple(res[o0:o0 + len(p.outs)]))
        o0 += len(p.outs)
    return split


def _like(arrays):
    return [jax.ShapeDtypeStruct(a.shape, a.dtype) for a in arrays]


def _half(ref, slot, hf, piece=(0, 1)):
    k, n = piece
    rows = ref.shape[1] // 2 // n
    return ref.at[slot, pl.ds((hf * n + k) * rows, rows)]


def _whole_halves(arrays):
    return [(i, 0, 1) for i in range(len(arrays))]


def _gather_chips(bufs, pieces=None, whole=False, base=0):
    nw = len(bufs)
    pieces = _whole_halves(bufs) if pieces is None else pieces
    part = (lambda ref, slot, c, piece: ref.at[slot]) if whole else _half

    def copies(couts):
        x, y, c = _mesh_pos()
        q = 2 * x + y
        for j, (i, k, n) in enumerate(pieces):
            for kc, chip in enumerate(_other_chips(x, y)):
                mine, theirs = part(couts[i], q, c, (k, n)), part(couts[i], 2 * chip[0] + chip[1], c, (k, n))
                yield base + j * 3 + kc, mine, theirs, (*chip, c)

    def start(cins, couts, ssem, rsem):
        for s, mine, _, dev in copies(couts):
            _rcopy(mine, mine, ssem.at[s], rsem.at[s], dev).start()

    def finish(cins, couts, ssem, rsem):
        for s, _, theirs, dev in copies(couts):
            _rcopy(theirs, theirs, ssem.at[s], rsem.at[s], dev).wait_recv()
        for s, mine, _, dev in copies(couts):
            _rcopy(mine, mine, ssem.at[s], rsem.at[s], dev).wait_send()

    return _Comm(bufs, _like(bufs), {i: i for i in range(nw)}, 3 * len(pieces), start, finish)


def _gather_pass_on(bufs, pieces=None, base=0):
    nw = len(bufs)
    pieces = _whole_halves(bufs) if pieces is None else pieces

    def copies(couts):
        x, y, c = _mesh_pos()
        for j, (i, k, n) in enumerate(pieces):
            for kc, chip in enumerate(_other_chips(x, y)):
                slot = 2 * chip[0] + chip[1]
                yield base + j * 3 + kc, _half(couts[i], slot, c, (k, n)), _half(couts[i], slot, 1 - c, (k, n)), (x, y, 1 - c)

    def start(cins, couts, ssem, rsem):
        for s, landed, _, sib in copies(couts):
            _rcopy(landed, landed, ssem.at[s], rsem.at[s], sib).start()

    def finish(cins, couts, ssem, rsem):
        for s, _, passed, sib in copies(couts):
            _rcopy(passed, passed, ssem.at[s], rsem.at[s], sib).wait_recv()
        for s, landed, _, sib in copies(couts):
            _rcopy(landed, landed, ssem.at[s], rsem.at[s], sib).wait_send()

    return _Comm(bufs, _like(bufs), {i: i for i in range(nw)}, 3 * len(pieces), start, finish)


def _gather_now(bufs, packets):
    nb = len(bufs)
    over, on, pk = _gather_chips(bufs), _gather_pass_on(bufs, base=3 * nb), _gather_chips(packets, whole=True, base=6 * nb)

    def start(cins, couts, ssem, rsem):
        over.start(cins[:nb], couts[:nb], ssem, rsem)
        pk.start(cins[nb:], couts[nb:], ssem, rsem)

    def finish(cins, couts, ssem, rsem):
        over.finish(cins[:nb], couts[:nb], ssem, rsem)
        on.start(cins[:nb], couts[:nb], ssem, rsem)
        on.finish(cins[:nb], couts[:nb], ssem, rsem)
        pk.finish(cins[nb:], couts[nb:], ssem, rsem)

    every = list(bufs) + list(packets)
    return _Comm(every, _like(every), {i: i for i in range(len(every))}, 6 * nb + 3 * len(packets), start, finish)


def _exchange_pairs(gs):
    nw = len(gs)

    def copies(cins, couts):
        x, y, c = _mesh_pos()
        for i in range(nw):
            for d in range(N_CHIPS):
                yield i * N_CHIPS + d, _half(cins[i], d, 1 - c), couts[i].at[d], (x, y, 1 - c)

    def start(cins, couts, ssem, rsem):
        for s, src, dst, sib in copies(cins, couts):
            _rcopy(src, dst, ssem.at[s], rsem.at[s], sib).start()

    def finish(cins, couts, ssem, rsem):
        for s, src, dst, sib in copies(cins, couts):
            _rcopy(src, dst, ssem.at[s], rsem.at[s], sib).wait_recv()
        for s, src, dst, sib in copies(cins, couts):
            _rcopy(src, dst, ssem.at[s], rsem.at[s], sib).wait_send()

    outs = [jax.ShapeDtypeStruct((N_CHIPS, g.shape[1] // 2, g.shape[2]), g.dtype) for g in gs]
    return _Comm(gs, outs, {}, N_CHIPS * nw, start, finish)


def _gather_packets(small):
    def peers():
        x, y, c = _mesh_pos()
        for r in range(1, 8):
            fx, fy, fc = (r >> 2) & 1, (r >> 1) & 1, r & 1
            yield r - 1, (1 - x if fx else x, 1 - y if fy else y, 1 - c if fc else c)

    def start(cins, couts, ssem, rsem):
        x, y, c = _mesh_pos()
        mine = couts[0].at[4 * x + 2 * y + c]
        for s, peer in peers():
            _rcopy(mine, mine, ssem.at[s], rsem.at[s], peer).start()

    def finish(cins, couts, ssem, rsem):
        x, y, c = _mesh_pos()
        mine = couts[0].at[4 * x + 2 * y + c]
        for s, peer in peers():
            theirs = couts[0].at[4 * peer[0] + 2 * peer[1] + peer[2]]
            _rcopy(theirs, theirs, ssem.at[s], rsem.at[s], peer).wait_recv()
        for s, peer in peers():
            _rcopy(mine, mine, ssem.at[s], rsem.at[s], peer).wait_send()

    return _Comm([small], _like([small]), {0: 0}, 7, start, finish)


def _exchange_chips(a4s, b2s, pieces=None):
    nw = len(a4s)
    pieces = _whole_halves(a4s) if pieces is None else pieces

    def copies(cins, couts):
        x, y, c = _mesh_pos()
        for j, (i, k, n) in enumerate(pieces):
            rows = a4s[i].shape[1] // n
            part = pl.ds(k * rows, rows)
            for kc, chip in enumerate(_other_chips(x, y)):
                yield j * 3 + kc, cins[i].at[2 * chip[0] + chip[1], part], couts[i].at[kc, part], (*chip, c)

    def start(cins, couts, ssem, rsem):
        for s, src, dst, dev in copies(cins, couts):
            _rcopy(src, dst, ssem.at[s], rsem.at[s], dev).start()

    def finish(cins, couts, ssem, rsem):
        for s, src, dst, dev in copies(cins, couts):
            _rcopy(src, dst, ssem.at[s], rsem.at[s], dev).wait_recv()
        for s, src, dst, dev in copies(cins, couts):
            _rcopy(src, dst, ssem.at[s], rsem.at[s], dev).wait_send()

    return _Comm(list(a4s) + list(b2s), _like(b2s), {nw + i: i for i in range(nw)}, 3 * len(pieces), start, finish)


def _landing(a4s):
    return [lax.empty((3,) + a.shape[1:], a.dtype) for a in a4s]


def _share_halves(rs):
    nw = len(rs)

    def halves(couts, i, hf):
        rows = rs[i].shape[0] // 2
        return couts[i].at[pl.ds(hf * rows, rows)]

    def start(cins, couts, ssem, rsem):
        x, y, c = _mesh_pos()
        for i in range(nw):
            _rcopy(halves(couts, i, c), halves(couts, i, c), ssem.at[i], rsem.at[i], (x, y, 1 - c)).start()

    def finish(cins, couts, ssem, rsem):
        x, y, c = _mesh_pos()
        for i in range(nw):
            _rcopy(halves(couts, i, 1 - c), halves(couts, i, 1 - c), ssem.at[i], rsem.at[i], (x, y, 1 - c)).wait_recv()
        for i in range(nw):
            _rcopy(halves(couts, i, c), halves(couts, i, c), ssem.at[i], rsem.at[i], (x, y, 1 - c)).wait_send()

    return _Comm(rs, _like(rs), {i: i for i in range(nw)}, nw, start, finish)


HBM_SPEC = pl.BlockSpec(memory_space=pltpu.HBM)
SEM_SPEC = pl.BlockSpec(memory_space=pltpu.SEMAPHORE)
DATAFLOW = pltpu.SideEffectType.DATAFLOW_SIDE_EFFECTING


def _chips_start(sent, landing):
    def body(sent_ref, land_ref, *rest):
        sems, token = rest[:6], rest[8]
        x, y, c = _mesh_pos()
        for k, chip in enumerate(_other_chips(x, y)):
            _rcopy(sent_ref.at[2 * chip[0] + chip[1]], land_ref.at[k], sems[k], sems[3 + k], (*chip, c)).start()
        token[...] = jnp.zeros_like(token)

    return pl.pallas_call(
        body, name="chips_start_w_in",
        out_shape=(pltpu.SemaphoreType.DMA(()),) * 6 + (pltpu.HBM(sent.shape, sent.dtype), pltpu.HBM(landing.shape, landing.dtype),
                                                       jax.ShapeDtypeStruct((8, LANES), f32)),
        in_specs=(HBM_SPEC, HBM_SPEC), out_specs=(SEM_SPEC,) * 6 + (HBM_SPEC, HBM_SPEC, pl.BlockSpec(memory_space=pltpu.VMEM)),
        input_output_aliases={0: 6, 1: 7}, compiler_params=pltpu.CompilerParams(has_side_effects=DATAFLOW),
    )(pltpu.with_memory_space_constraint(sent, pltpu.HBM), pltpu.with_memory_space_constraint(landing, pltpu.HBM))


def _chips_wait(sems, sent_thru, land_thru, after):
    n_after = len(after)

    def body(sent_ref, land_ref, *rest):
        sems = rest[:6]
        x, y, c = _mesh_pos()
        for k, chip in enumerate(_other_chips(x, y)):
            cp = _rcopy(sent_ref.at[2 * chip[0] + chip[1]], land_ref.at[k], sems[k], sems[3 + k], (*chip, c))
            cp.wait_send()
            cp.wait_recv()

    return pl.pallas_call(
        body, name="chips_wait_w_in", out_shape=(pltpu.HBM(sent_thru.shape, sent_thru.dtype), pltpu.HBM(land_thru.shape, land_thru.dtype)),
        in_specs=(HBM_SPEC, HBM_SPEC) + (SEM_SPEC,) * 6 + (ANY,) * n_after, out_specs=(HBM_SPEC, HBM_SPEC),
        input_output_aliases={0: 0, 1: 1}, compiler_params=pltpu.CompilerParams(has_side_effects=DATAFLOW),
    )(sent_thru, land_thru, *sems, *after)[1]


ADD_ROWS = 256


def _add_pair(g4, b1, qc_idx, name):
    _, half, cols = b1.shape
    rb = ADD_ROWS if half % ADD_ROWS == 0 else half
    nb = half // rb

    def body(qc_ref, g_ref, b_ref, ob_ref):
        ob_ref[...] = (g_ref[...] + b_ref[...]).astype(bf16)

    blk = (1, rb, cols)
    out = pl.BlockSpec(blk, lambda d, i, qc: (d, i, 0))
    return pl.pallas_call(
        body, name=name,
        grid_spec=pltpu.PrefetchScalarGridSpec(
            num_scalar_prefetch=1, grid=(N_CHIPS, nb),
            in_specs=[pl.BlockSpec(blk, lambda d, i, qc: (d, qc[1] * nb + i, 0)), out],
            out_specs=out),
        out_shape=jax.ShapeDtypeStruct(b1.shape, bf16),
        compiler_params=_params(("parallel", "parallel")),
    )(qc_idx, g4, b1)


def _add_chips(g4, b1, b2, qc_idx, name):
    _, half, cols = b1.shape
    rb = ADD_ROWS if half % ADD_ROWS == 0 else half
    nb = half // rb

    def body(qc_ref, g_ref, s_ref, b_ref, o_ref):
        o_ref[...] = (((g_ref[0] + s_ref[0]) + b_ref[0].astype(f32)) + b_ref[1].astype(f32)) + b_ref[2].astype(f32)

    return pl.pallas_call(
        body, name=name,
        grid_spec=pltpu.PrefetchScalarGridSpec(
            num_scalar_prefetch=1, grid=(nb,),
            in_specs=[pl.BlockSpec((1, rb, cols), lambda i, qc: (qc[0], qc[1] * nb + i, 0)),
                      pl.BlockSpec((1, rb, cols), lambda i, qc: (qc[0], i, 0)), pl.BlockSpec((3, rb, cols), lambda i, qc: (0, i, 0))],
            out_specs=pl.BlockSpec((rb, cols), lambda i, qc: (qc[1] * nb + i, 0))),
        out_shape=jax.ShapeDtypeStruct((2 * half, cols), f32),
        compiler_params=_params(("parallel",)),
    )(qc_idx, g4, b1, b2)


def _adamw_math(w, g, m, v):
    m = ADAM_B1 * m + (1.0 - ADAM_B1) * g
    v = ADAM_B2 * v + (1.0 - ADAM_B2) * (g * g)
    m_hat = m / (1.0 - ADAM_B1 ** ADAM_STEP)
    v_hat = v / (1.0 - ADAM_B2 ** ADAM_STEP)
    return -ADAM_LR * (m_hat / (jnp.sqrt(v_hat) + ADAM_EPS) + ADAM_WD * w), m, v


def _adamw(w, g, m, v, name, comm=None, token=None):
    rows = w.shape[0]
    if w.ndim == 3:
        rb = max(r for r in range(1, ADD_ROWS // 4 + 1) if rows % r == 0)
    else:
        rb = ADD_ROWS if rows % ADD_ROWS == 0 else rows
    extra = [] if token is None else [token]

    def body(w_ref, g_ref, m_ref, v_ref, *rest):
        go_ref, d_ref, mo_ref, vo_ref = rest[len(extra):]
        g = g_ref[...]
        go_ref[...] = g
        d_ref[...], mo_ref[...], vo_ref[...] = _adamw_math(w_ref[...], g, m_ref[...], v_ref[...])

    blk = pl.BlockSpec((rb,) + w.shape[1:], lambda i: (i,) + (0,) * (w.ndim - 1))
    return _hosted(body, comm, name=name, grid=(rows // rb,), in_specs=[blk] * 4 + [pl.BlockSpec((8, LANES), lambda i: (0, 0))] * len(extra),
                   out_specs=(blk,) * 4, out_shape=(jax.ShapeDtypeStruct(w.shape, f32),) * 4, args=(w, g, m, v, *extra))


def _small_sum_adamw(all_pkts, w, m, v):
    names = [n for n, _, _ in SMALL_LAYOUT if n in w]
    place = {n: (r0, size) for n, r0, size in SMALL_LAYOUT}
    rows_of = lambda size: -(-size // LANES)
    flat = lambda a: a.reshape(1, -1)
    k = len(names)

    def body(*refs):
        a_ref, ins = refs[0], refs[1:1 + 3 * k]
        g_ref, outs = refs[1 + 3 * k], refs[2 + 3 * k:2 + 7 * k]
        packs = refs[2 + 7 * k:]
        g = a_ref[0]
        for r in range(1, 8):
            g = g + a_ref[r]
        g_ref[...] = g
        for kind in range(3):
            packs[kind][...] = jnp.zeros_like(packs[kind])
            for j, n in enumerate(names):
                r0, size = place[n]
                for r in range(rows_of(size)):
                    width = min(LANES, size - r * LANES)
                    packs[kind][r0 + r:r0 + r + 1, 0:width] = ins[kind * k + j][:, r * LANES:r * LANES + width]
        results = (g,) + _adamw_math(packs[0][...], g, packs[1][...], packs[2][...])
        for kind, val in enumerate(results):
            for j, n in enumerate(names):
                r0, size = place[n]
                for r in range(rows_of(size)):
                    width = min(LANES, size - r * LANES)
                    outs[kind * k + j][:, r * LANES:r * LANES + width] = val[r0 + r:r0 + r + 1, 0:width]

    args = [all_pkts] + [flat(d[n]) for d in (w, m, v) for n in names]
    out_shape = [jax.ShapeDtypeStruct(all_pkts.shape[1:], f32)] + [jax.ShapeDtypeStruct((1, place[n][1]), f32) for _ in range(4) for n in names]
    res = pl.pallas_call(body, name="small_sum_adamw", out_shape=tuple(out_shape),
                         scratch_shapes=[pltpu.VMEM(all_pkts.shape[1:], f32)] * 3)(*args)
    by_kind = [{n: res[1 + kind * k + j].reshape(w[n].shape) for j, n in enumerate(names)} for kind in range(4)]
    return res[0], by_kind


SMALL_LAYOUT = (("ln_in_g", 0, 1024), ("ln_in_b", 8, 1024), ("ln1_g", 16, 1024), ("ln1_b", 24, 1024), ("b_ple_gate", 32, 1024),
                ("ln2_g", 40, 1024), ("ln2_b", 48, 1024), ("gdn_norm_g", 56, 128), ("fox_norm_g", 57, 64), ("a_log", 58, 4),
                ("dt_bias", 59, 4), ("b_f", 60, 8), ("loss", 61, 1))
SMALL_CONV_ROW = 64
SMALL_ROWS = 128


def _pack_small(vals, conv=None):
    rows = []
    nxt = 0
    for n, r0, size in SMALL_LAYOUT:
        assert r0 == nxt
        v = vals[n].reshape(-1).astype(f32) if n in vals else jnp.zeros((size,), f32)
        nrows = -(-size // LANES)
        rows.append(jnp.pad(v, (0, nrows * LANES - size)).reshape(nrows, LANES))
        nxt = r0 + nrows
    rows.append(jnp.zeros((SMALL_CONV_ROW - nxt, LANES), f32))
    conv_rows = CONV_W * GDN_QKV // LANES
    rows.append(jnp.zeros((conv_rows, LANES), f32) if conv is None else conv.reshape(conv_rows, LANES))
    rows.append(jnp.zeros((SMALL_ROWS - SMALL_CONV_ROW - conv_rows, LANES), f32))
    return jnp.concatenate(rows, axis=0)


WEIGHTS = ("ln_in_g", "ln_in_b", "w_in", "conv_w", "a_log", "dt_bias", "gdn_norm_g", "b_f", "fox_norm_g", "w_out", "ln1_g", "ln1_b",
           "w_up", "w_down", "w_ple", "w_ple_gate", "b_ple_gate", "ln2_g", "ln2_b")
SMALL_NAMES = tuple(n for n, _, _ in SMALL_LAYOUT if n != "loss")


def kernel(x, p, ln_in_g, ln_in_b, w_in, conv_w, a_log, dt_bias, gdn_norm_g, b_f, fox_norm_g, w_out, ln1_g, ln1_b, w_up, w_down, w_ple, w_ple_gate, b_ple_gate, ln2_g, ln2_b, loss_target, m_ln_in_g, m_ln_in_b, m_w_in, m_conv_w, m_a_log, m_dt_bias, m_gdn_norm_g, m_b_f, m_fox_norm_g, m_w_out, m_ln1_g, m_ln1_b, m_w_up, m_w_down, m_w_ple, m_w_ple_gate, m_b_ple_gate, m_ln2_g, m_ln2_b, v_ln_in_g, v_ln_in_b, v_w_in, v_conv_w, v_a_log, v_dt_bias, v_gdn_norm_g, v_b_f, v_fox_norm_g, v_w_out, v_ln1_g, v_ln1_b, v_w_up, v_w_down, v_w_ple, v_w_ple_gate, v_b_ple_gate, v_ln2_g, v_ln2_b):
    given = dict(locals())
    w = {n: given[n] for n in WEIGHTS}
    m = {n: given["m_" + n] for n in WEIGHTS}
    v = {n: given["v_" + n] for n in WEIGHTS}
    xi, yi, ci = _mesh_pos()
    q = 2 * xi + yi

    def slot_buffer(val, dtype, slots=N_CHIPS, slot=q, rows=None):
        rows = val.shape[0] if rows is None else rows
        return lax.dynamic_update_slice(lax.empty((slots, rows) + val.shape[1:], dtype), val.astype(dtype)[None], (slot, 0, 0))

    shard_cols = D_IN // N_CHIPS
    conv_rows = CONV_W * GDN_QKV // N_CHIPS // LANES
    conv_pkt = jnp.pad(w["conv_w"][0].reshape(-1, LANES), ((0, CONV_PKT_ROWS - conv_rows), (0, 0)))
    ln_in_out, (w_in4, conv_all) = _ln_in(x[0], _row(w["ln_in_g"]), _row(w["ln_in_b"]),
                                          _gather_now([slot_buffer(w["w_in"][0].T, bf16, rows=W_IN_ROWS)], [slot_buffer(conv_pkt, f32)]))
    conv_full = jnp.concatenate([conv_all[d, :conv_rows].reshape(CONV_W, GDN_QKV // N_CHIPS) for d in range(N_CHIPS)], axis=1)
    wi = jnp.concatenate([w_in4[d, :shard_cols] for d in range(N_CHIPS)], axis=0)
    w_cat = jnp.concatenate([wi[:OFF_BETA], wi[OFF_FOX:OFF_F], wi[OFF_BETA:OFF_FOX], wi[OFF_F:],
                             jnp.zeros((D_CAT - D_IN, D_MODEL), bf16)], axis=0)

    small = {n: w[n] for n in SMALL_NAMES}
    qc = jnp.stack([q, ci]).astype(jnp.int32)
    tail_state = {}

    def chips_phase(gc):
        g_in = jnp.concatenate([gc[:OFF_BETA], gc[SEG_SMALL:SEG_SMALL + 8], gc[SEG_FOX:SEG_SMALL], gc[SEG_SMALL + 8:SEG_SMALL + 16]], axis=0)
        g_in4 = jnp.stack([jnp.pad(g_in[d * shard_cols:(d + 1) * shard_cols], ((0, W_IN_ROWS - shard_cols), (0, 0))) for d in range(N_CHIPS)])
        (from_sibling,), = _comm_only([_exchange_pairs([g_in4])], "exchange_pairs_w_in")
        sent = _add_pair(g_in4, from_sibling, qc, "add_pair_w_in")
        *sems, sent_thru, land_thru, token = _chips_start(sent, _landing([sent])[0])
        tail_state.update(g=g_in4, from_sibling=from_sibling, sems=sems, sent=sent_thru, landing=land_thru, token=token)
        return None, None

    grad_x, _, g_conv, g_late, small_g = _device_grads(
        x[0], p[0, 0], loss_target[0], small, w_cat, conv_full, [slot_buffer(w[n][0], bf16) for n in LATE], qc, tail=(chips_phase, None),
        ln_in_out=ln_in_out)

    grads, delta, new_m, new_v = {}, {}, {}, {}
    for n in LATE:
        outs, _ = _adamw(w[n][0], g_late[n], m[n][0], v[n][0], "adamw_" + n, token=tail_state["token"])
        grads[n], delta[n], new_m[n], new_v[n] = (a.reshape(w[n].shape) for a in outs)
    from_chips = _chips_wait(tail_state["sems"], tail_state["sent"], tail_state["landing"], [delta[n] for n in LATE])
    packets = _gather_packets(slot_buffer(_pack_small(small_g, g_conv), f32, 8, 4 * xi + 2 * yi + ci))
    halves = _share_halves([_add_chips(tail_state["g"], tail_state["from_sibling"], from_chips, qc, "add_chips_w_in")])
    (g_in_red,), (small_all,) = _comm_only([halves, packets], "share_w_in")
    as_stored = lambda a: jnp.transpose(a, (2, 0, 1))
    outs, _ = _adamw(as_stored(w["w_in"]), g_in_red[:shard_cols].reshape(shard_cols, 1, D_MODEL), as_stored(m["w_in"]), as_stored(v["w_in"]),
                     "adamw_w_in")
    grads["w_in"], delta["w_in"], new_m["w_in"], new_v["w_in"] = (jnp.transpose(a, (1, 2, 0)) for a in outs)
    pick = lambda d: {n: d[n] for n in SMALL_NAMES}
    g_pkt, by_kind = _small_sum_adamw(small_all, pick(w), pick(m), pick(v))
    for dst, vals in zip((grads, delta, new_m, new_v), by_kind):
        dst.update(vals)
    conv_rows_all = CONV_W * GDN_QKV // LANES
    conv_g_full = g_pkt[SMALL_CONV_ROW:SMALL_CONV_ROW + conv_rows_all].reshape(CONV_W, GDN_QKV)
    conv_g = lax.dynamic_slice_in_dim(conv_g_full, q * (GDN_QKV // N_CHIPS), GDN_QKV // N_CHIPS, axis=1)
    outs, _ = _adamw(w["conv_w"][0], conv_g, m["conv_w"][0], v["conv_w"][0], "adamw_conv_w")
    grads["conv_w"], delta["conv_w"], new_m["conv_w"], new_v["conv_w"] = (a.reshape(w["conv_w"].shape) for a in outs)
    loss = g_pkt[61, 0]
    return (loss, grad_x[None], *[grads[n] for n in WEIGHTS], *[delta[n] for n in WEIGHTS],
            *[new_m[n] for n in WEIGHTS], *[new_v[n] for n in WEIGHTS])
```

```python
import functools

import jax
import jax.numpy as jnp
from jax import lax
from jax.experimental import pallas as pl
from jax.experimental.pallas import tpu as pltpu

f32 = jnp.float32
bf16 = jnp.bfloat16
HI = lax.Precision.HIGHEST
MESH = pl.DeviceIdType.MESH

D_MODEL = 1024
CHUNK = 64
GDN_HEADS = 4
GDN_DK = 128
FOX_HEADS = 8
FOX_DH = 64
CONV_W = 4
D_FF = 4096
D_PLE = 256
LN_EPS = 1e-5
NORM_EPS = 1e-6
ALPHA = 2.0 ** 0.25
GDN_QKV = 1536
OFF_Z = 1536
OFF_BETA = 2048
OFF_FOX = 2056
OFF_F = 3592
D_IN = 3600
ADAM_LR = 0.001
ADAM_B1 = 0.9
ADAM_B2 = 0.999
ADAM_EPS = 1e-08
ADAM_WD = 0.01
ADAM_STEP = 10

SEG_FOX = 2048
SEG_SMALL = 3584
D_CAT = 3840
LANES = 128
TOK_BLK = 256
FOX_BQ = 256
VMEM_LIMIT = 56 * 1024 * 1024
NEG = -1e30

N_CHIPS = 4
W_IN_ROWS = 928


def _params(sem=None, **kw):
    return pltpu.CompilerParams(dimension_semantics=sem, vmem_limit_bytes=VMEM_LIMIT, **kw)


def _sigmoid(x):
    return 1.0 / (1.0 + jnp.exp(-x))


def _softplus(x):
    return jnp.maximum(x, 0.0) + jnp.log(1.0 + jnp.exp(-jnp.abs(x)))


def _ln_fwd(x, g, b):
    mu = jnp.mean(x, -1, keepdims=True)
    xc = x - mu
    var = jnp.mean(xc * xc, -1, keepdims=True)
    rstd = lax.rsqrt(var + LN_EPS)
    xhat = xc * rstd
    return xhat * g + b, xhat, rstd


def _ln_bwd(dy, xhat, rstd, g):
    dxh = dy * g
    m1 = jnp.mean(dxh, -1, keepdims=True)
    m2 = jnp.mean(dxh * xhat, -1, keepdims=True)
    return rstd * (dxh - m1 - xhat * m2)


def _dot(a, b, prec=HI):
    return jnp.dot(a, b, precision=prec, preferred_element_type=f32)


def _dot_nt(a, b, prec=HI):
    return lax.dot_general(a, b, (((1,), (1,)), ((), ())), precision=prec, preferred_element_type=f32)


def _dot_tn(a, b, prec=HI):
    return lax.dot_general(a, b, (((0,), (0,)), ((), ())), precision=prec, preferred_element_type=f32)


def _lane(shape):
    return lax.broadcasted_iota(jnp.int32, shape, len(shape) - 1)


def _mm(a, b, mode, tm, tn, name, out_dtype=f32, epi=None, extra=None, shards=1, comm=None):
    if mode == "nn":
        (m, k), n = a.shape, b.shape[-1] * shards
    elif mode == "nt":
        (m, k), n = a.shape, b.shape[-2]
    else:
        (k, m), n = a.shape, b.shape[1]
    assert m % tm == 0 and n % tn == 0, (name, m, n, tm, tn)
    per = (n // shards) // tn
    assert mode == "nt" or per * tn * shards == n, (name, n, tn, shards)
    nc = 512 if tn % 512 == 0 else (256 if tn % 256 == 0 else 128)
    ks = k // shards

    def body(a_ref, b_ref, *rest):
        for n0 in range(0, tn, nc):
            if mode == "nn":
                acc = jnp.dot(a_ref[...], b_ref[:, n0:n0 + nc], preferred_element_type=f32)
            elif mode == "nt" and shards > 1:
                acc = jnp.zeros((tm, nc), f32)
                for d in range(shards):
                    acc = acc + lax.dot_general(a_ref[:, d * ks:(d + 1) * ks], b_ref[d, n0:n0 + nc, :], (((1,), (1,)), ((), ())),
                                                preferred_element_type=f32)
            elif mode == "nt":
                acc = lax.dot_general(a_ref[...], b_ref[n0:n0 + nc, :], (((1,), (1,)), ((), ())), preferred_element_type=f32)
            else:
                acc = lax.dot_general(a_ref[...], b_ref[:, n0:n0 + nc], (((0,), (0,)), ((), ())), preferred_element_type=f32)
            if epi == "relu2":
                relu_ref, act_ref = rest
                r = jnp.maximum(acc, 0.0)
                relu_ref[:, n0:n0 + nc] = r.astype(bf16)
                act_ref[:, n0:n0 + nc] = (r * r).astype(bf16)
            elif epi == "relu2_bwd":
                relu_ref, o_ref = rest
                o_ref[:, n0:n0 + nc] = (acc * (2.0 * relu_ref[:, n0:n0 + nc].astype(f32))).astype(bf16)
            else:
                (o_ref,) = rest
                o_ref[:, n0:n0 + nc] = acc.astype(out_dtype)

    if mode == "tn":
        a_spec = pl.BlockSpec((k, tm), lambda j, i: (0, i))
    else:
        a_spec = pl.BlockSpec((tm, k), lambda j, i: (i, 0))
    if mode == "nt" and shards > 1:
        b_spec = pl.BlockSpec((shards, tn, ks), lambda j, i: (0, j, 0))
    elif mode == "nt":
        b_spec = pl.BlockSpec((tn, k), lambda j, i: (j, 0))
    elif mode == "nn" and shards > 1:
        b_spec = pl.BlockSpec((None, k, tn), lambda j, i: (j // per, 0, j % per))
    else:
        b_spec = pl.BlockSpec((k, tn), lambda j, i: (0, j))
    o_spec = pl.BlockSpec((tm, tn), lambda j, i: (i, j))
    in_specs = [a_spec, b_spec]
    args = [a, b]
    if epi == "relu2":
        out_shape = (jax.ShapeDtypeStruct((m, n), bf16), jax.ShapeDtypeStruct((m, n), bf16))
        out_specs = (o_spec, o_spec)
    elif epi == "relu2_bwd":
        in_specs.append(o_spec)
        args.append(extra)
        out_shape = jax.ShapeDtypeStruct((m, n), bf16)
        out_specs = o_spec
    elif mode == "tn" and shards > 1:
        out_shape = jax.ShapeDtypeStruct((shards, m, n // shards), out_dtype)
        out_specs = pl.BlockSpec((None, tm, tn), lambda j, i: (j // per, i, j % per))
    else:
        out_shape = jax.ShapeDtypeStruct((m, n), out_dtype)
        out_specs = o_spec
    single = not isinstance(out_shape, tuple)
    res, moved = _hosted(body, comm, name=name, grid=(n // tn, m // tm), in_specs=in_specs,
                         out_specs=(out_specs,) if single else out_specs, out_shape=(out_shape,) if single else out_shape, args=args)
    res = res[0] if single else res
    return res if comm is None else (res, moved)


def _row_spec(width, col=0):
    return pl.BlockSpec((TOK_BLK, width), lambda i: (i, col))


def _vec_spec(rows, width):
    return pl.BlockSpec((rows, width), lambda i: (0, 0))


def _ln_in(x, g, b, comm=None):
    t, d = x.shape

    def body(x_ref, g_ref, b_ref, h_ref, hb_ref):
        h, _, _ = _ln_fwd(x_ref[...], g_ref[...], b_ref[...])
        h_ref[...] = h
        hb_ref[...] = h.astype(bf16)

    return _hosted(
        body, comm, name="ln_in", grid=(t // TOK_BLK,),
        in_specs=[_row_spec(d), _vec_spec(1, d), _vec_spec(1, d)],
        out_specs=(_row_spec(d), _row_spec(d)),
        out_shape=(jax.ShapeDtypeStruct((t, d), f32), jax.ShapeDtypeStruct((t, d), bf16)),
        args=(x, g, b))


def _attn_post(o_gdn, proj, o_fox, g_gdn, g_fox2, comm=None):
    t = o_gdn.shape[0]

    def body(og_ref, z_ref, of_ref, gg_ref, gf_ref, out_ref):
        for h in range(GDN_HEADS):
            sl = slice(h * LANES, (h + 1) * LANES)
            og = og_ref[:, sl]
            z = z_ref[:, sl]
            r = lax.rsqrt(jnp.mean(og * og, -1, keepdims=True) + NORM_EPS)
            out_ref[:, sl] = (og * r * gg_ref[...] * (z * _sigmoid(z))).astype(bf16)
        lo = _lane((TOK_BLK, LANES)) < FOX_DH
        for pr in range(FOX_HEADS // 2):
            sl = slice(pr * LANES, (pr + 1) * LANES)
            of = of_ref[:, sl]
            sq = of * of
            s0 = jnp.sum(jnp.where(lo, sq, 0.0), -1, keepdims=True)
            s1 = jnp.sum(jnp.where(lo, 0.0, sq), -1, keepdims=True)
            r = lax.rsqrt(jnp.where(lo, s0, s1) * (1.0 / FOX_DH) + NORM_EPS)
            out_ref[:, 512 + pr * LANES:512 + (pr + 1) * LANES] = (of * r * gf_ref[...]).astype(bf16)

    return _hosted(
        body, comm, name="attn_post", grid=(t // TOK_BLK,),
        in_specs=[_row_spec(512), _row_spec(512, OFF_Z // 512), _row_spec(512), _vec_spec(1, LANES), _vec_spec(1, LANES)],
        out_specs=(_row_spec(D_MODEL),),
        out_shape=(jax.ShapeDtypeStruct((t, D_MODEL), bf16),),
        args=(o_gdn, proj, o_fox, g_gdn, g_fox2))


def _attn_post_bwd(dr1b, w_out, o_gdn, proj, o_fox, g_gdn, g_fox2):
    t = o_gdn.shape[0]

    def body(dr_ref, wo_ref, og_ref, z_ref, of_ref, gg_ref, gf_ref, dog_ref, dz_ref, dof_ref, pg_ref):
        i = pl.program_id(0)

        @pl.when(i == 0)
        def _():
            pg_ref[...] = jnp.zeros_like(pg_ref)

        da = _dot_nt(dr_ref[...], wo_ref[...], None)
        dgg = jnp.zeros((1, LANES), f32)
        for h in range(GDN_HEADS):
            sl = slice(h * LANES, (h + 1) * LANES)
            og = og_ref[:, sl]
            z = z_ref[:, sl]
            dout = da[:, sl]
            g = gg_ref[...]
            r = lax.rsqrt(jnp.mean(og * og, -1, keepdims=True) + NORM_EPS)
            sg = _sigmoid(z)
            silu = z * sg
            ng = og * r * g
            dng = dout * silu
            dz_ref[:, sl] = (dout * ng * (sg * (1.0 + z * (1.0 - sg)))).astype(bf16)
            dgg = dgg + jnp.sum(dng * og * r, 0, keepdims=True)
            gd = dng * g
            dog_ref[:, sl] = r * gd - og * (r * r * r) * jnp.mean(og * gd, -1, keepdims=True)
        pg_ref[0:1, :] += dgg
        lo = _lane((TOK_BLK, LANES)) < FOX_DH
        dgf = jnp.zeros((1, LANES), f32)
        for pr in range(FOX_HEADS // 2):
            sl = slice(pr * LANES, (pr + 1) * LANES)
            of = of_ref[:, sl]
            dout = da[:, 512 + pr * LANES:512 + (pr + 1) * LANES]
            g = gf_ref[...]
            sq = of * of
            s0 = jnp.sum(jnp.where(lo, sq, 0.0), -1, keepdims=True)
            s1 = jnp.sum(jnp.where(lo, 0.0, sq), -1, keepdims=True)
            r = lax.rsqrt(jnp.where(lo, s0, s1) * (1.0 / FOX_DH) + NORM_EPS)
            dgf = dgf + jnp.sum(dout * of * r, 0, keepdims=True)
            gd = dout * g
            xg = of * gd
            m0 = jnp.sum(jnp.where(lo, xg, 0.0), -1, keepdims=True)
            m1 = jnp.sum(jnp.where(lo, 0.0, xg), -1, keepdims=True)
            dof_ref[:, sl] = r * gd - of * (r * r * r) * (jnp.where(lo, m0, m1) * (1.0 / FOX_DH))
        pg_ref[1:2, :] += dgf

    return pl.pallas_call(
        body, name="attn_post_bwd", grid=(t // TOK_BLK,),
        in_specs=_product_specs(dr1b, w_out) + [_row_spec(512), _row_spec(512, OFF_Z // 512), _row_spec(512), _vec_spec(1, LANES), _vec_spec(1, LANES)],
        out_specs=(_row_spec(512), _row_spec(512), _row_spec(512), _vec_spec(8, LANES)),
        out_shape=(jax.ShapeDtypeStruct((t, 512), f32), jax.ShapeDtypeStruct((t, 512), bf16),
                   jax.ShapeDtypeStruct((t, 512), f32), jax.ShapeDtypeStruct((8, LANES), f32)),
        compiler_params=_params(("arbitrary",)),
    )(dr1b, w_out, o_gdn, proj, o_fox, g_gdn, g_fox2)


def _product_specs(lhs, rhs):
    return [_row_spec(lhs.shape[1]), pl.BlockSpec(rhs.shape, lambda i: (0, 0))]


def _ln1(h0, lhs, rhs, g, b, comm=None):
    t, d = h0.shape

    def body(h0_ref, lhs_ref, rhs_ref, g_ref, b_ref, h_ref, hb_ref, xh_ref, rs_ref):
        mix = jnp.dot(lhs_ref[...], rhs_ref[...], preferred_element_type=f32)
        h, xhat, rstd = _ln_fwd(ALPHA * h0_ref[...] + mix, g_ref[...], b_ref[...])
        h_ref[...] = h
        hb_ref[...] = h.astype(bf16)
        xh_ref[...] = xhat
        rs_ref[...] = jnp.broadcast_to(rstd, rs_ref.shape)

    return _hosted(
        body, comm, name="ln1", grid=(t // TOK_BLK,),
        in_specs=[_row_spec(d)] + _product_specs(lhs, rhs) + [_vec_spec(1, d), _vec_spec(1, d)],
        out_specs=(_row_spec(d), _row_spec(d), _row_spec(d), _row_spec(LANES)),
        out_shape=(jax.ShapeDtypeStruct((t, d), f32), jax.ShapeDtypeStruct((t, d), bf16),
                   jax.ShapeDtypeStruct((t, d), f32), jax.ShapeDtypeStruct((t, LANES), f32)),
        args=(h0, lhs, rhs, g, b))


def _ln2_loss(h1, lhs, rhs, pb, w_ple, gp, b_gate, g, b, target):
    t, d = h1.shape

    def body(h1_ref, lhs_ref, rhs_ref, pb_ref, wp_ref, gp_ref, bg_ref, g_ref, b_ref, t_ref, dr_ref, drb_ref, dpe_ref, dgp_ref, pg_ref):
        i = pl.program_id(0)

        @pl.when(i == 0)
        def _():
            pg_ref[...] = jnp.zeros_like(pg_ref)

        ff = jnp.dot(lhs_ref[...], rhs_ref[...], preferred_element_type=f32)
        sig = _sigmoid(gp_ref[...] + bg_ref[...])
        pe = jnp.concatenate([jnp.dot(pb_ref[...], wp_ref[s], preferred_element_type=f32) for s in range(w_ple.shape[0])], axis=1)
        r2 = ALPHA * h1_ref[...] + ff + pe * sig
        y, xhat, rstd = _ln_fwd(r2, g_ref[...], b_ref[...])
        err = y - t_ref[...]
        dy = err * (1.0 / d)
        dr = _ln_bwd(dy, xhat, rstd, g_ref[...])
        dr_ref[...] = dr
        drb_ref[...] = dr.astype(bf16)
        dpe_ref[...] = (dr * sig).astype(bf16)
        dgp = dr * pe * sig * (1.0 - sig)
        dgp_ref[...] = dgp.astype(bf16)
        pg_ref[0:1, :] += jnp.sum(dy * xhat, 0, keepdims=True)
        pg_ref[1:2, :] += jnp.sum(dy, 0, keepdims=True)
        pg_ref[2:3, :] += jnp.sum(dgp, 0, keepdims=True)
        pg_ref[3:4, :] += 0.5 * jnp.sum(jnp.mean(err * err, -1, keepdims=True), 0, keepdims=True)

    return pl.pallas_call(
        body, name="ln2_loss", grid=(t // TOK_BLK,),
        in_specs=[_row_spec(d)] + _product_specs(lhs, rhs) + [_row_spec(pb.shape[1]), pl.BlockSpec(w_ple.shape, lambda i: (0, 0, 0)), _row_spec(d)]
        + [_vec_spec(1, d)] * 3 + [_row_spec(d)],
        out_specs=(_row_spec(d), _row_spec(d), _row_spec(d), _row_spec(d), _vec_spec(8, d)),
        out_shape=(jax.ShapeDtypeStruct((t, d), f32), jax.ShapeDtypeStruct((t, d), bf16), jax.ShapeDtypeStruct((t, d), bf16),
                   jax.ShapeDtypeStruct((t, d), bf16), jax.ShapeDtypeStruct((8, d), f32)),
        compiler_params=_params(("arbitrary",)),
    )(h1, lhs, rhs, pb, w_ple, gp, b_gate, g, b, target)


def _ln1_bwd(dr2, dup, w_up, dgp, w_gate, xhat, rstd, g, comm=None):
    t, d = dr2.shape
    ks = w_up.shape[2]

    def body(dr2_ref, dup_ref, wup_ref, dgp_ref, wg_ref, xh_ref, rs_ref, g_ref, dr_ref, drb_ref, pg_ref):
        i = pl.program_id(0)

        @pl.when(i == 0)
        def _():
            pg_ref[...] = jnp.zeros_like(pg_ref)

        dh = ALPHA * dr2_ref[...] + _dot_nt(dgp_ref[...], wg_ref[...], None)
        for s in range(w_up.shape[0]):
            dh = dh + _dot_nt(dup_ref[:, s * ks:(s + 1) * ks], wup_ref[s], None)
        xhat = xh_ref[...]
        dr = _ln_bwd(dh, xhat, rs_ref[:, 0:1], g_ref[...])
        dr_ref[...] = dr
        drb_ref[...] = dr.astype(bf16)
        pg_ref[0:1, :] += jnp.sum(dh * xhat, 0, keepdims=True)
        pg_ref[1:2, :] += jnp.sum(dh, 0, keepdims=True)

    return _hosted(
        body, comm, name="ln1_bwd", grid=(t // TOK_BLK,),
        in_specs=[_row_spec(d), _row_spec(dup.shape[1]), pl.BlockSpec(w_up.shape, lambda i: (0, 0, 0))] + _product_specs(dgp, w_gate)
        + [_row_spec(d), _row_spec(LANES), _vec_spec(1, d)],
        out_specs=(_row_spec(d), _row_spec(d), _vec_spec(8, d)),
        out_shape=(jax.ShapeDtypeStruct((t, d), f32), jax.ShapeDtypeStruct((t, d), bf16), jax.ShapeDtypeStruct((8, d), f32)),
        args=(dr2, dup, w_up, dgp, w_gate, xhat, rstd, g))


def _ln_in_bwd(x, dr1, dmm, g, comm=None):
    t, d = x.shape

    def body(x_ref, dr1_ref, dmm_ref, g_ref, dx_ref, pg_ref):
        i = pl.program_id(0)

        @pl.when(i == 0)
        def _():
            pg_ref[...] = jnp.zeros_like(pg_ref)

        dh = ALPHA * dr1_ref[...] + dmm_ref[...]
        _, xhat, rstd = _ln_fwd(x_ref[...], g_ref[...], 0.0)
        dx_ref[...] = _ln_bwd(dh, xhat, rstd, g_ref[...])
        pg_ref[0:1, :] += jnp.sum(dh * xhat, 0, keepdims=True)
        pg_ref[1:2, :] += jnp.sum(dh, 0, keepdims=True)

    return _hosted(
        body, comm, name="ln_in_bwd", grid=(t // TOK_BLK,),
        in_specs=[_row_spec(d)] * 3 + [_vec_spec(1, d)],
        out_specs=(_row_spec(d), _vec_spec(8, d)),
        out_shape=(jax.ShapeDtypeStruct((t, d), f32), jax.ShapeDtypeStruct((8, d), f32)),
        args=(x, dr1, dmm, g))


def _tri(n, upper=False, strict=False):
    r = lax.broadcasted_iota(jnp.int32, (n, n), 0)
    c = lax.broadcasted_iota(jnp.int32, (n, n), 1)
    if upper:
        m = (c > r) if strict else (c >= r)
    else:
        m = (c < r) if strict else (c <= r)
    return jnp.where(m, 1.0, 0.0).astype(f32)


def _gate_values(x, bias, alog, lane):
    z = x + bias
    return jnp.where(lane < 4, _sigmoid(z), jnp.where(lane < 8, -jnp.exp(alog) * _softplus(z), jnp.where(lane < 16, -_softplus(-z), 0.0)))


def _gates(proj, bias_row, alog_row):
    t = proj.shape[0]
    nch = t // CHUNK

    def body(x_ref, bias_ref, alog_ref, gates_ref, gcum_ref, gcumt_ref):
        lane = _lane((t, LANES))
        gates = _gate_values(x_ref[...], bias_ref[...], alog_ref[...], lane)
        gates_ref[...] = gates
        g3 = gates.reshape(nch, CHUNK, LANES)
        tri = jnp.broadcast_to(_tri(CHUNK)[None], (nch, CHUNK, CHUNK))
        loc = jnp.einsum("bij,bjk->bik", tri, g3, precision=HI, preferred_element_type=f32)
        tot = jnp.sum(g3, axis=1)
        offs = _dot(_tri(nch, strict=True), tot)
        glob = loc + offs[:, None, :]
        lane3 = _lane((nch, CHUNK, LANES))
        gcum = jnp.where(lane3 < 4, g3, jnp.where(lane3 < 8, loc, glob)).reshape(t, LANES)
        gcum_ref[...] = gcum
        gcumt_ref[...] = gcum.T

    return pl.pallas_call(
        body, name="gates", grid=(1,),
        in_specs=[pl.BlockSpec((t, LANES), lambda i: (0, SEG_SMALL // LANES)), _vec_spec(1, LANES), _vec_spec(1, LANES)],
        out_specs=(pl.BlockSpec((t, LANES), lambda i: (0, 0)), pl.BlockSpec((t, LANES), lambda i: (0, 0)),
                   pl.BlockSpec((LANES, t), lambda i: (0, 0))),
        out_shape=(jax.ShapeDtypeStruct((t, LANES), f32), jax.ShapeDtypeStruct((t, LANES), f32), jax.ShapeDtypeStruct((LANES, t), f32)),
        compiler_params=_params(("arbitrary",)),
    )(proj, bias_row, alog_row)


def _gates_bwd(proj, bias_row, alog_row, gates, dgates, dccol, dct):
    t = proj.shape[0]
    nch = t // CHUNK

    def body(x_ref, bias_ref, alog_ref, gates_ref, dg_ref, dcc_ref, dct_ref, dx_ref, pg_ref):
        lane = _lane((t, LANES))
        d = dg_ref[...] + dcc_ref[...] + dct_ref[...].T
        d3 = d.reshape(nch, CHUNK, LANES)
        tri = jnp.broadcast_to(_tri(CHUNK, upper=True)[None], (nch, CHUNK, CHUNK))
        loc = jnp.einsum("bij,bjk->bik", tri, d3, precision=HI, preferred_element_type=f32)
        tot = jnp.sum(d3, axis=1)
        offs = _dot(_tri(nch, upper=True, strict=True), tot)
        glob = loc + offs[:, None, :]
        lane3 = _lane((nch, CHUNK, LANES))
        dpre = jnp.where(lane3 < 4, d3, jnp.where(lane3 < 8, loc, glob)).reshape(t, LANES)
        z = x_ref[...] + bias_ref[...]
        sg = _sigmoid(z)
        dx = jnp.where(lane < 4, dpre * sg * (1.0 - sg),
                       jnp.where(lane < 8, dpre * (-jnp.exp(alog_ref[...])) * sg, jnp.where(lane < 16, dpre * (1.0 - sg), 0.0)))
        dx_ref[...] = dx.astype(bf16)
        pg_ref[...] = jnp.zeros_like(pg_ref)
        pg_ref[0:1, :] = jnp.sum(dx, 0, keepdims=True)
        pg_ref[1:2, :] = jnp.sum(jnp.where((lane >= 4) & (lane < 8), dpre * gates_ref[...], 0.0), 0, keepdims=True)

    full = pl.BlockSpec((t, LANES), lambda i: (0, 0))
    return pl.pallas_call(
        body, name="gates_bwd", grid=(1,),
        in_specs=[pl.BlockSpec((t, LANES), lambda i: (0, SEG_SMALL // LANES)), _vec_spec(1, LANES), _vec_spec(1, LANES),
                  full, full, full, pl.BlockSpec((LANES, t), lambda i: (0, 0))],
        out_specs=(full, _vec_spec(8, LANES)),
        out_shape=(jax.ShapeDtypeStruct((t, LANES), bf16), jax.ShapeDtypeStruct((8, LANES), f32)),
        compiler_params=_params(("arbitrary",)),
    )(proj, bias_row, alog_row, gates, dgates, dccol, dct)


def _conv_act(u, cw, row, t):
    c = cw[3:4, :] * u
    for jj in range(CONV_W - 1):
        sh = CONV_W - 1 - jj
        c = c + cw[jj:jj + 1, :] * jnp.where(row >= sh, pltpu.roll(u, sh, axis=0), 0.0)
    return c


def _gdn_conv(proj, conv_w, comm=None):
    t = proj.shape[0]
    nblk = GDN_QKV // LANES

    def body(u_ref, cw_ref, c_ref, y_ref):
        j = pl.program_id(0)
        row = lax.broadcasted_iota(jnp.int32, (t, LANES), 0)
        c = _conv_act(u_ref[...], cw_ref[...], row, t)
        c_ref[...] = c
        s = c * _sigmoid(c)
        r = lax.rsqrt(jnp.sum(s * s, -1, keepdims=True) + NORM_EPS)
        scale = jnp.where(j < GDN_HEADS, GDN_DK ** -0.5, 1.0)
        y_ref[...] = jnp.where(j < 2 * GDN_HEADS, s * (r * scale), s)

    blk = pl.BlockSpec((t, LANES), lambda j: (0, j))
    return _hosted(
        body, comm, name="gdn_conv", grid=(nblk,),
        in_specs=[blk, pl.BlockSpec((CONV_W, LANES), lambda j: (0, j))],
        out_specs=(blk, blk),
        out_shape=(jax.ShapeDtypeStruct((t, GDN_QKV), f32), jax.ShapeDtypeStruct((t, GDN_QKV), f32)),
        args=(proj, conv_w))


def _gdn_conv_bwd(proj, conv_w, c, dy, comm=None):
    t = proj.shape[0]
    nblk = GDN_QKV // LANES

    def body(u_ref, cw_ref, c_ref, dy_ref, du_ref, dcw_ref):
        j = pl.program_id(0)
        row = lax.broadcasted_iota(jnp.int32, (t, LANES), 0)
        u = u_ref[...]
        cw = cw_ref[...]
        c = c_ref[...]
        dy = dy_ref[...]
        sg = _sigmoid(c)
        s = c * sg
        r = lax.rsqrt(jnp.sum(s * s, -1, keepdims=True) + NORM_EPS)
        n = s * r
        scale = jnp.where(j < GDN_HEADS, GDN_DK ** -0.5, 1.0)
        dn = dy * scale
        ds = jnp.where(j < 2 * GDN_HEADS, r * (dn - n * jnp.sum(dn * n, -1, keepdims=True)), dy)
        dc = ds * (sg * (1.0 + c * (1.0 - sg)))
        du = cw[3:4, :] * dc
        dcw_ref[...] = jnp.zeros_like(dcw_ref)
        dcw_ref[3:4, :] = jnp.sum(dc * u, 0, keepdims=True)
        for jj in range(CONV_W - 1):
            sh = CONV_W - 1 - jj
            du = du + cw[jj:jj + 1, :] * jnp.where(row < t - sh, pltpu.roll(dc, t - sh, axis=0), 0.0)
            dcw_ref[jj:jj + 1, :] = jnp.sum(dc * jnp.where(row >= sh, pltpu.roll(u, sh, axis=0), 0.0), 0, keepdims=True)
        du_ref[...] = du.astype(bf16)

    blk = pl.BlockSpec((t, LANES), lambda j: (0, j))
    return _hosted(
        body, comm, name="gdn_conv_bwd", grid=(nblk,),
        in_specs=[blk, pl.BlockSpec((CONV_W, LANES), lambda j: (0, j)), blk, blk],
        out_specs=(blk, pl.BlockSpec((8, LANES), lambda j: (0, j))),
        out_shape=(jax.ShapeDtypeStruct((t, GDN_QKV), bf16), jax.ShapeDtypeStruct((8, GDN_QKV), f32)),
        args=(proj, conv_w, c, dy))


def _chunk_masks():
    r = lax.broadcasted_iota(jnp.int32, (CHUNK, CHUNK), 0)
    c = lax.broadcasted_iota(jnp.int32, (CHUNK, CHUNK), 1)
    return r >= c, r > c, r == c


def _col_to_row(col, eye):
    return jnp.sum(jnp.where(eye, col, 0.0), axis=0, keepdims=True)


def _row_to_col(row, eye):
    return jnp.sum(jnp.where(eye, row, 0.0), axis=1, keepdims=True)


NN = (((1,), (0,)), ((), ()))
NT = (((1,), (1,)), ((), ()))
TN = (((0,), (0,)), ((), ()))
GDN_GROUP = 4


def _mx(a, b, dims=NN, passes=1):
    d = lambda p, q: lax.dot_general(p, q, dims, preferred_element_type=f32)
    ah, bh = a.astype(bf16), b.astype(bf16)
    if passes == 1:
        return d(ah, bh)
    al = (a - ah.astype(f32)).astype(bf16)
    bl = (b - bh.astype(f32)).astype(bf16)
    return d(ah, bh) + (d(ah, bl) + d(al, bh))


def _gdn_decay(gam, masks):
    causal, _, eye = masks
    return jnp.exp(jnp.where(causal, gam - _col_to_row(gam, eye), NEG))


def _gdn_local(y, gcum, comm=None):
    t = y.shape[0]
    nch = t // CHUNK
    rows_blk = GDN_GROUP * CHUNK

    def body(y_ref, g_ref, u_ref, w_ref, qk_ref, tinv_ref):
        masks = _chunk_masks()
        _, strict, eye = masks
        ids = [(j, h) for j in range(GDN_GROUP) for h in range(GDN_HEADS)]
        rs = lambda j: slice(j * CHUNK, (j + 1) * CHUNK)
        col = lambda base, h: slice(base + h * LANES, base + (h + 1) * LANES)
        kn = [y_ref[rs(j), col(512, h)] for j, h in ids]
        beta = [g_ref[rs(j), h:h + 1] for j, h in ids]
        gam = [g_ref[rs(j), 4 + h:5 + h] for j, h in ids]
        dec = [_gdn_decay(g, masks) for g in gam]
        x = [-jnp.where(strict, _mx(k, k, NT) * d * b, 0.0) for k, d, b in zip(kn, dec, beta)]
        tinv = [jnp.where(eye, 1.0, 0.0) + a for a in x]
        for _ in range(5):
            x = [_mx(a, a, NN, 3) for a in x]
            tinv = [t_ + _mx(t_, a, NN, 3) for t_, a in zip(tinv, x)]
        for (j, h), t_, k, d, b, g in zip(ids, tinv, kn, dec, beta, gam):
            u_ref[rs(j), col(0, h)] = _mx(t_, b * y_ref[rs(j), col(1024, h)])
            w_ref[rs(j), col(0, h)] = _mx(t_, (b * jnp.exp(g)) * k)
            qk_ref[j, h] = _mx(y_ref[rs(j), col(0, h)], k, NT) * d
            tinv_ref[j, h] = t_

    mat = pl.BlockSpec((GDN_GROUP, GDN_HEADS, CHUNK, CHUNK), lambda n: (n, 0, 0, 0))
    return _hosted(
        body, comm, name="gdn_local", grid=(nch // GDN_GROUP,),
        in_specs=[pl.BlockSpec((rows_blk, GDN_QKV), lambda n: (n, 0)), pl.BlockSpec((rows_blk, LANES), lambda n: (n, 0))],
        out_specs=(pl.BlockSpec((rows_blk, 512), lambda n: (n, 0)), pl.BlockSpec((rows_blk, 512), lambda n: (n, 0)), mat, mat),
        out_shape=(jax.ShapeDtypeStruct((t, 512), f32), jax.ShapeDtypeStruct((t, 512), f32),
                   jax.ShapeDtypeStruct((nch, GDN_HEADS, CHUNK, CHUNK), f32), jax.ShapeDtypeStruct((nch, GDN_HEADS, CHUNK, CHUNK), f32)),
        args=(y, gcum))


def _gdn_fwd(y, gcum, u, w, qk, comm=None):
    t = y.shape[0]
    nch = t // CHUNK

    def body(y_ref, g_ref, u_ref, w_ref, qk_ref, o_ref, sall_ref, s_ref):
        @pl.when(pl.program_id(0) == 0)
        def _():
            s_ref[...] = jnp.zeros_like(s_ref)

        heads = range(GDN_HEADS)
        sl = [slice(h * LANES, (h + 1) * LANES) for h in heads]
        gam = [g_ref[:, 4 + h:5 + h] for h in heads]
        gam_last = [g[CHUNK - 1:CHUNK, :] for g in gam]
        s = [s_ref[h] for h in heads]
        for h in heads:
            sall_ref[0, h] = s[h]
        ws = [_mx(w_ref[:, sl[h]], s[h]) for h in heads]
        qs = [_mx(y_ref[:, sl[h]] * jnp.exp(gam[h]), s[h]) for h in heads]
        vn = [u_ref[:, sl[h]] - ws[h] for h in heads]
        av = [_mx(qk_ref[0, h], vn[h]) for h in heads]
        kv = [_mx(y_ref[:, 512 + h * LANES:512 + (h + 1) * LANES] * jnp.exp(gam_last[h] - gam[h]), vn[h], TN) for h in heads]
        for h in heads:
            o_ref[:, sl[h]] = qs[h] + av[h]
            s_ref[h] = jnp.exp(gam_last[h]) * s[h] + kv[h]

    row = lambda width: pl.BlockSpec((CHUNK, width), lambda n: (n, 0))
    return _hosted(
        body, comm, name="gdn_fwd", grid=(nch,),
        in_specs=[row(GDN_QKV), row(LANES), row(512), row(512), pl.BlockSpec((1, GDN_HEADS, CHUNK, CHUNK), lambda n: (n, 0, 0, 0))],
        out_specs=(row(512), pl.BlockSpec((1, GDN_HEADS, LANES, LANES), lambda n: (n, 0, 0, 0))),
        out_shape=(jax.ShapeDtypeStruct((t, 512), f32), jax.ShapeDtypeStruct((nch, GDN_HEADS, LANES, LANES), f32)),
        scratch_shapes=[pltpu.VMEM((GDN_HEADS, LANES, LANES), f32)],
        args=(y, gcum, u, w, qk))


def _gdn_bwd(y, gcum, u_all, w_all, qk_all, tinv_all, sall, do, comm=None):
    t = y.shape[0]
    nch = t // CHUNK

    def body(y_ref, g_ref, u_ref, w_ref, qk_ref, tinv_ref, sall_ref, do_ref, dy_ref, dg_ref, ds_ref):
        @pl.when(pl.program_id(0) == 0)
        def _():
            ds_ref[...] = jnp.zeros_like(ds_ref)

        masks = _chunk_masks()
        causal, strict, eye = masks
        lane = _lane((CHUNK, LANES))
        row = lax.broadcasted_iota(jnp.int32, (CHUNK, 1), 0)
        heads = range(GDN_HEADS)
        each = lambda f, *ls: [f(*a) for a in zip(*ls)]
        rsum = lambda a: jnp.sum(a, axis=1, keepdims=True)
        sl = [slice(h * LANES, (h + 1) * LANES) for h in heads]
        qn = [y_ref[:, sl[h]] for h in heads]
        kn = [y_ref[:, 512 + h * LANES:512 + (h + 1) * LANES] for h in heads]
        v = [y_ref[:, 1024 + h * LANES:1024 + (h + 1) * LANES] for h in heads]
        beta = [g_ref[:, h:h + 1] for h in heads]
        gam = [g_ref[:, 4 + h:5 + h] for h in heads]
        gam_last = [g[CHUNK - 1:CHUNK, :] for g in gam]
        dec = [_gdn_decay(g, masks) for g in gam]
        e = [jnp.exp(g) for g in gam]
        f = each(lambda gl_, g: jnp.exp(gl_ - g), gam_last, gam)
        gl = [jnp.exp(g) for g in gam_last]
        u = [u_ref[:, sl[h]] for h in heads]
        w = [w_ref[:, sl[h]] for h in heads]
        qk = [qk_ref[0, h] for h in heads]
        tinv = [tinv_ref[0, h] for h in heads]
        s = [sall_ref[0, h] for h in heads]
        dsn = [ds_ref[h] for h in heads]
        d_o = [do_ref[:, sl[h]] for h in heads]
        qd = each(lambda a, b: a * b, qn, e)
        kd = each(lambda a, b: a * b, kn, f)
        ws = each(_mx, w, s)
        kds = each(_mx, kd, dsn)
        qkdo = each(lambda a, b: _mx(a, b, TN), qk, d_o)
        dqd = each(lambda a, b: _mx(a, b, NT), d_o, s)
        qddo = each(lambda a, b: _mx(a, b, TN), qd, d_o)
        kkd = each(lambda k, d: _mx(k, k, NT) * d, kn, dec)
        vn = each(lambda a, b: a - b, u, ws)
        dvn = each(lambda a, b: a + b, qkdo, kds)
        dqk = each(lambda a, b: jnp.where(causal, _mx(a, b, NT), 0.0), d_o, vn)
        dkd = each(lambda a, b: _mx(a, b, NT), vn, dsn)
        dw = each(lambda a, b: -_mx(a, b, NT), dvn, s)
        wdvn = each(lambda a, b: _mx(a, b, TN), w, dvn)
        dgl = each(lambda a, b: jnp.sum(rsum(a * b), axis=0, keepdims=True), dsn, s)
        for h in heads:
            ds_ref[h] = qddo[h] - wdvn[h] + gl[h] * dsn[h]
        dru = each(lambda a, b: _mx(a, b, TN), tinv, dvn)
        drw = each(lambda a, b: _mx(a, b, TN), tinv, dw)
        dqkr = each(lambda a, b: a * b, dqk, dec)
        dq1 = each(_mx, dqkr, kn)
        dk1 = each(lambda a, b: _mx(a, b, TN), dqkr, qn)
        dnu = each(lambda a, b: _mx(a, b, NT), dru, u)
        dnw = each(lambda a, b: _mx(a, b, NT), drw, w)
        dn = each(lambda a, b: jnp.where(strict, -(a + b), 0.0), dnu, dnw)
        dkk = each(lambda a, b, d: a * b * d, dn, beta, dec)
        dk2 = each(_mx, dkk, kn)
        dk3 = each(lambda a, b: _mx(a, b, TN), dkk, kn)
        dgates = jnp.zeros((CHUNK, LANES), f32)
        for h in heads:
            drw_k = rsum(drw[h] * kn[h])
            dbeta = rsum(dru[h] * v[h]) + e[h] * drw_k + rsum(dn[h] * kkd[h])
            m = dn[h] * (kkd[h] * beta[h]) + dqk[h] * qk[h]
            de = beta[h] * drw_k + rsum(dqd[h] * qn[h])
            df = rsum(dkd[h] * kn[h])
            dgam = rsum(m) - _row_to_col(jnp.sum(m, axis=0, keepdims=True), eye) + de * e[h] - df * f[h]
            dgam_last = jnp.sum(df * f[h], axis=0, keepdims=True) + dgl[h] * gl[h]
            dgam = dgam + jnp.where(row == CHUNK - 1, dgam_last, 0.0)
            dy_ref[:, sl[h]] = dq1[h] + dqd[h] * e[h]
            dy_ref[:, 512 + h * LANES:512 + (h + 1) * LANES] = (beta[h] * e[h]) * drw[h] + dk2[h] + dk3[h] + dk1[h] + dkd[h] * f[h]
            dy_ref[:, 1024 + h * LANES:1024 + (h + 1) * LANES] = beta[h] * dru[h]
            dgates = dgates + jnp.where(lane == h, dbeta, 0.0) + jnp.where(lane == 4 + h, dgam, 0.0)
        dg_ref[...] = dgates

    rev = lambda width: pl.BlockSpec((CHUNK, width), lambda n: (nch - 1 - n, 0))
    mat = lambda d: pl.BlockSpec((1, GDN_HEADS, d, d), lambda n: (nch - 1 - n, 0, 0, 0))
    return _hosted(
        body, comm, name="gdn_bwd", grid=(nch,),
        in_specs=[rev(GDN_QKV), rev(LANES), rev(512), rev(512), mat(CHUNK), mat(CHUNK), mat(LANES), rev(512)],
        out_specs=(rev(GDN_QKV), rev(LANES)),
        out_shape=(jax.ShapeDtypeStruct((t, GDN_QKV), f32), jax.ShapeDtypeStruct((t, LANES), f32)),
        scratch_shapes=[pltpu.VMEM((GDN_HEADS, LANES, LANES), f32)],
        args=(y, gcum, u_all, w_all, qk_all, tinv_all, sall, do))


FOX_CLASSES = 4


def _fox_groups(t):
    nq = t // FOX_BQ
    ncls = min(FOX_CLASSES, nq)
    per = nq // ncls
    return [(g * per, per, (g + 1) * per * FOX_BQ) for g in range(ncls)]


def _fox_causal(i, keys):
    rows = i * FOX_BQ + lax.broadcasted_iota(jnp.int32, (FOX_BQ, keys), 0)
    return lax.broadcasted_iota(jnp.int32, (FOX_BQ, keys), 1) <= rows


def _fox_scores(q_ref, k_ref, gcumt_ref, h, causal):
    pr = h // 2
    lo = (h % 2) * FOX_DH
    lane = _lane((FOX_BQ, LANES))
    mask = (lane >= lo) & (lane < lo + FOX_DH)
    qm = jnp.where(mask, q_ref[:, pr * LANES:(pr + 1) * LANES] * (FOX_DH ** -0.5), 0.0).astype(bf16)
    kp = k_ref[:, pr * LANES:(pr + 1) * LANES].astype(bf16)
    s = _dot_nt(qm, kp, None) - gcumt_ref[8 + h:9 + h, :]
    return jnp.where(causal, s, NEG), mask, qm, kp


def _fox_fwd(proj, gcumt, ride=None):
    c0 = SEG_FOX // 512

    def group_call(q0, nq, keys, comm):
        def body(q_ref, k_ref, v_ref, gcumt_ref, o_ref, lse_ref):
            causal = _fox_causal(q0 + pl.program_id(0), keys)
            lane = _lane((FOX_BQ, LANES))
            lse_all = jnp.zeros((FOX_BQ, LANES), f32)
            for pr in range(FOX_HEADS // 2):
                vp = v_ref[:, pr * LANES:(pr + 1) * LANES].astype(bf16)
                o_pair = jnp.zeros((FOX_BQ, LANES), f32)
                for h in (2 * pr, 2 * pr + 1):
                    s, mask, _, _ = _fox_scores(q_ref, k_ref, gcumt_ref, h, causal)
                    m = jnp.max(s, axis=1, keepdims=True)
                    p = jnp.exp(s - m)
                    l = jnp.sum(p, axis=1, keepdims=True)
                    o_h = _dot(p.astype(bf16), vp, None) * (1.0 / l)
                    o_pair = jnp.where(mask, o_h, o_pair)
                    lse_all = jnp.where(lane == h, m + jnp.log(l), lse_all)
                o_ref[:, pr * LANES:(pr + 1) * LANES] = o_pair
            lse_ref[...] = lse_all

        seen = lambda col: pl.BlockSpec((keys, 512), lambda i: (0, col))
        return _hosted(
            body, comm, name=f"fox_fwd_{keys}", grid=(nq,),
            in_specs=[pl.BlockSpec((FOX_BQ, 512), lambda i: (q0 + i, c0)), seen(c0 + 1), seen(c0 + 2),
                      pl.BlockSpec((LANES, keys), lambda i: (0, 0))],
            out_specs=(pl.BlockSpec((FOX_BQ, 512), lambda i: (i, 0)), pl.BlockSpec((FOX_BQ, LANES), lambda i: (i, 0))),
            out_shape=(jax.ShapeDtypeStruct((nq * FOX_BQ, 512), f32), jax.ShapeDtypeStruct((nq * FOX_BQ, LANES), f32)),
            args=(proj, proj, proj, gcumt))

    parts = []
    for n, g in enumerate(_fox_groups(proj.shape[0])):
        hook = ride(n) if ride else None
        part, moved = group_call(*g, hook[0] if hook else None)
        parts.append(part)
        if hook:
            hook[1](moved)
    return jnp.concatenate([o for o, _ in parts], axis=0), jnp.concatenate([l for _, l in parts], axis=0)


def _fox_bwd(proj, gcumt, o, lse, do, ride=None):
    t = proj.shape[0]
    c0 = SEG_FOX // 512

    def group_call(q0, nq, keys, acc, comm):
        first = acc is None

        def body(q_ref, k_ref, v_ref, gcumt_ref, o_ref, lse_ref, do_ref, *rest):
            dq_ref, dk_ref, dv_ref, dcc_ref, dct_ref = rest[-5:]
            j = pl.program_id(0)
            causal = _fox_causal(q0 + j, keys)

            @pl.when(j == 0)
            def _():
                if first:
                    dk_ref[...] = jnp.zeros_like(dk_ref)
                    dv_ref[...] = jnp.zeros_like(dv_ref)
                    dct_ref[...] = jnp.zeros_like(dct_ref)
                else:
                    dk_ref[...], dv_ref[...], dct_ref[...] = rest[0][...], rest[1][...], rest[2][...]

            lane = _lane((FOX_BQ, LANES))
            dcc = jnp.zeros((FOX_BQ, LANES), f32)
            scale = FOX_DH ** -0.5
            for pr in range(FOX_HEADS // 2):
                sl = slice(pr * LANES, (pr + 1) * LANES)
                vp = v_ref[:, sl].astype(bf16)
                dq_pair = jnp.zeros((FOX_BQ, LANES), f32)
                for h in (2 * pr, 2 * pr + 1):
                    s, mask, qm, kp = _fox_scores(q_ref, k_ref, gcumt_ref, h, causal)
                    p = jnp.exp(s - lse_ref[:, h:h + 1])
                    dom = jnp.where(mask, do_ref[:, sl], 0.0)
                    delta = jnp.sum(dom * o_ref[:, sl], axis=1, keepdims=True)
                    domb = dom.astype(bf16)
                    ds = p * (_dot_nt(domb, vp, None) - delta)
                    dsb = ds.astype(bf16)
                    dv_ref[:, sl] += _dot_tn(p.astype(bf16), domb, None)
                    dk_ref[:, sl] += _dot_tn(dsb, qm, None)
                    dq_pair = jnp.where(mask, _dot(dsb, kp, None) * scale, dq_pair)
                    dcc = jnp.where(lane == 8 + h, jnp.sum(ds, axis=1, keepdims=True), dcc)
                    dct_ref[8 + h:9 + h, :] += -jnp.sum(ds, axis=0, keepdims=True)
                dq_ref[:, sl] = dq_pair.astype(bf16)
            dcc_ref[...] = dcc

        qblk = lambda col: pl.BlockSpec((FOX_BQ, 512), lambda i: (q0 + i, col))
        oblk = pl.BlockSpec((FOX_BQ, 512), lambda i: (i, 0))
        seen = lambda col: pl.BlockSpec((keys, 512), lambda i: (0, col))
        rblk = pl.BlockSpec((FOX_BQ, LANES), lambda i: (q0 + i, 0))
        seen_t = pl.BlockSpec((LANES, keys), lambda i: (0, 0))
        in_specs = [qblk(c0), seen(c0 + 1), seen(c0 + 2), seen_t, qblk(0), rblk, qblk(0)]
        args = [proj, proj, proj, gcumt, o, lse, do]
        aliases = {}
        if not first:
            in_specs += [seen(0), seen(0), seen_t]
            args += list(acc)
            aliases = {7: 1, 8: 2, 9: 4}
        return _hosted(
            body, comm, name=f"fox_bwd_{keys}", grid=(nq,), in_specs=in_specs,
            out_specs=(oblk, seen(0), seen(0), pl.BlockSpec((FOX_BQ, LANES), lambda i: (i, 0)), seen_t),
            out_shape=(jax.ShapeDtypeStruct((nq * FOX_BQ, 512), bf16), jax.ShapeDtypeStruct((t, 512), f32), jax.ShapeDtypeStruct((t, 512), f32),
                       jax.ShapeDtypeStruct((nq * FOX_BQ, LANES), f32), jax.ShapeDtypeStruct((LANES, t), f32)),
            aliases=aliases, args=args)

    acc, dqs, dccs = None, [], []
    for n, g in enumerate(reversed(_fox_groups(t))):
        hook = ride(n) if ride else None
        (dq, dk, dv, dcc, dct), moved = group_call(*g, acc, hook[0] if hook else None)
        if hook:
            hook[1](moved)
        acc = (dk, dv, dct)
        dqs.insert(0, dq)
        dccs.insert(0, dcc)
    return jnp.concatenate(dqs, axis=0), acc[0], acc[1], jnp.concatenate(dccs, axis=0), acc[2]


def _row(v, width=None):
    v = v.reshape(1, -1).astype(f32)
    if width is not None and v.shape[1] < width:
        v = jnp.pad(v, ((0, 0), (0, width - v.shape[1])))
    return v


LATE = ("w_out", "w_up", "w_ple_gate", "w_ple", "w_down")


def _device_grads(x, p, target, small, w_cat, conv_w, late, qc=None, tail=None, ln_in_out=None):
    z4 = jnp.zeros((4,), f32)
    bias_row = _row(jnp.concatenate([z4, small["dt_bias"].reshape(-1), small["b_f"].reshape(-1)]), LANES)
    alog_row = _row(jnp.concatenate([z4, small["a_log"].reshape(-1)]), LANES)
    g_gdn = _row(small["gdn_norm_g"])
    g_fox2 = _row(jnp.tile(small["fox_norm_g"].reshape(-1), 2))
    pb = p.astype(bf16)
    late = list(late)
    comm = qc is not None

    h0, h0b = ln_in_out if ln_in_out is not None else _ln_in(x, _row(small["ln_in_g"]), _row(small["ln_in_b"]))[0]
    proj = _mm(h0b, w_cat, "nt", 512, D_CAT, "mm_proj")
    gates, gcum, gcumt = _gates(proj, bias_row, alog_row)
    w_down_pieces = [(4, 0, 1)]

    def gather(phase, pieces):
        if not comm or not pieces:
            return None, lambda moved: None
        touched = sorted({i for i, _, _ in pieces})

        def took(moved):
            for i, buf in zip(touched, moved):
                late[i] = buf
        return phase([late[i] for i in touched], [(touched.index(i), k, n) for i, k, n in pieces]), took

    over, on = _gather_chips, _gather_pass_on
    cm, took = gather(over, [(0, 0, 1), (3, 0, 1)])
    (conv_c, qkv_n), moved = _gdn_conv(proj, conv_w, cm)
    took(moved)
    cm, took = gather(over, [(1, 0, 2)])
    (gu, gw, gqk, gtinv), moved = _gdn_local(qkv_n, gcum, cm)
    took(moved)
    cm, took = gather(over, [(1, 1, 2)])
    (o_gdn, sall), moved = _gdn_fwd(qkv_n, gcum, gu, gw, gqk, cm)
    took(moved)
    fox_plan = [(over, []), (on, [(0, 0, 1), (3, 0, 1), (1, 0, 2), (1, 1, 2)]), (over, [(2, 0, 1)]), (over, [(4, 0, 4)])]
    assert not comm or len(_fox_groups(x.shape[0])) == len(fox_plan)
    o_fox, lse = _fox_fwd(proj, gcumt, (lambda n: gather(*fox_plan[n])) if comm else None)
    cm, took = gather(on, [(2, 0, 1)])
    (attn,), moved = _attn_post(o_gdn, proj, o_fox, g_gdn, g_fox2, cm)
    took(moved)
    w_out = late[0].reshape(D_MODEL, D_MODEL)
    cm, took = gather(over, [(4, 1, 4)])
    (h1, h1b, xhat1, rstd1), moved = _ln1(h0, attn, w_out, _row(small["ln1_g"]), _row(small["ln1_b"]), cm)
    took(moved)
    w_up, w_ple = late[1], late[3]
    cm, took = gather(over, [(4, 1, 2)])
    up_act = _mm(h1b, w_up, "nn", 512, 1024, "mm_up", epi="relu2", shards=N_CHIPS, comm=cm)
    if cm:
        up_act, moved = up_act
        took(moved)
    up, act = up_act
    w_gate = late[2].reshape(D_MODEL, D_MODEL)
    cm, took = gather(on, w_down_pieces)
    gp = _mm(h1b, w_gate, "nn", 512, D_MODEL, "mm_gate", comm=cm)
    if cm:
        gp, moved = gp
        took(moved)
    w_down = late[4].reshape(D_FF, D_MODEL)
    dr2, dr2b, dpe, dgp, pg2 = _ln2_loss(h1, act, w_down, pb, w_ple, gp, _row(small["b_ple_gate"]), _row(small["ln2_g"]),
                                         _row(small["ln2_b"]), target)

    by_dest = lambda g: g.reshape((N_CHIPS, -1, g.shape[-1]))
    g_late = [None] * len(LATE)
    state = dict(from_sibling=[None] * len(LATE), sent=[None] * len(LATE), landing=[None] * len(LATE))
    nothing = (None, lambda moved: None)

    def to_sibling(idx):
        if not comm:
            return nothing

        def took(moved):
            for i, b1 in zip(idx, moved):
                state["from_sibling"][i] = b1
                state["sent"][i] = _add_pair(g_late[i], b1, qc, "add_pair_" + LATE[i])
                state["landing"][i] = _landing([state["sent"][i]])[0]
        return _exchange_pairs([g_late[i] for i in idx]), took

    def to_chips(pieces):
        if not comm:
            return nothing
        touched = sorted({i for i, _, _ in pieces})

        def took(moved):
            for i, b2 in zip(touched, moved):
                state["landing"][i] = b2
        return _exchange_chips([state["sent"][i] for i in touched], [state["landing"][i] for i in touched],
                               [(touched.index(i), k, n) for i, k, n in pieces]), took

    def ride(result, cm, took):
        if cm:
            result, moved = result
            took(moved)
        return result

    dup = _mm(dr2b, w_down, "nt", 512, 2048, "mm_dact", epi="relu2_bwd", extra=up)
    g_late[4] = by_dest(_mm(act, dr2b, "tn", 1024, D_MODEL, "mm_gdown"))
    cm, took = to_sibling([4])
    g_late[1] = ride(_mm(h1b, dup, "tn", 1024, 1024, "mm_gup", shards=N_CHIPS, comm=cm), cm, took)
    g_late[2] = by_dest(_mm(h1b, dgp, "tn", 1024, D_MODEL, "mm_ggate"))
    g_late[3] = _mm(pb, dpe, "tn", D_PLE, D_MODEL // N_CHIPS, "mm_gple", shards=N_CHIPS)
    cm, took = to_chips([(4, 0, 2)])
    (dr1, dr1b, pg1), moved = _ln1_bwd(dr2, dup, w_up, dgp, w_gate, xhat1, rstd1, _row(small["ln1_g"]), cm)
    took(moved)
    g_late[0] = by_dest(_mm(attn, dr1b, "tn", 1024, D_MODEL, "mm_gout"))
    do_gdn, dz, do_fox, pga = _attn_post_bwd(dr1b, w_out, o_gdn, proj, o_fox, g_gdn, g_fox2)
    chip_plan = [[(4, 1, 2), (1, 0, 2)], [(1, 1, 2)], [(0, 0, 1)], [(2, 0, 1), (3, 0, 1)]]

    def gdn_backward():
        cm, took = to_chips(chip_plan[0])
        state["gdn"], moved = _gdn_bwd(qkv_n, gcum, gu, gw, gqk, gtinv, sall, do_gdn, cm)
        took(moved)

    def fox_ride(n):
        if n == 0:
            return to_sibling([1, 0, 2, 3])
        if n == 1:
            gdn_backward()
        return to_chips(chip_plan[n])

    assert not comm or len(_fox_groups(x.shape[0])) == len(chip_plan)
    dfq, dfk, dfv, dccol, dct = _fox_bwd(proj, gcumt, o_fox, lse, do_fox, fox_ride if comm else None)
    if not comm:
        gdn_backward()
    dqkv_n, dgates = state["gdn"]
    dsmall, pgg = _gates_bwd(proj, bias_row, alog_row, gates, dgates, dccol, dct)
    cm = None
    if comm:
        cm = _share_halves([_add_chips(g, b1, b2, qc, "add_chips_" + n)
                            for g, b1, b2, n in zip(g_late, state["from_sibling"], state["landing"], LATE)])
    (du, g_conv8), reduced = _gdn_conv_bwd(proj, conv_w, conv_c, dqkv_n, cm)
    if comm:
        g_late = list(reduced)
    t = x.shape[0]
    dproj = jnp.concatenate([du, dz, dfq, dfk.astype(bf16), dfv.astype(bf16), dsmall, jnp.zeros((t, D_CAT - SEG_SMALL - LANES), bf16)], axis=1)
    g_cat = _mm(dproj, h0b, "tn", 1280, D_MODEL, "mm_gcat")
    cm, took = tail[0](g_cat) if tail else (None, None)
    dh0_mm = _mm(dproj, w_cat, "nn", 512, D_MODEL, "mm_dh0", comm=cm)
    if cm:
        dh0_mm, moved = dh0_mm
        took(moved)
    cm, took = tail[1](dh0_mm) if tail and tail[1] else (None, None)
    (grad_x, pg0), moved = _ln_in_bwd(x, dr1, dh0_mm, _row(small["ln_in_g"]), cm)
    if cm:
        took(moved)

    g_fox = pga[1, :FOX_DH] + pga[1, FOX_DH:]
    small_grads = dict(
        ln_in_g=pg0[0], ln_in_b=pg0[1], ln1_g=pg1[0], ln1_b=pg1[1], b_ple_gate=pg2[2], ln2_g=pg2[0], ln2_b=pg2[1],
        gdn_norm_g=pga[0], fox_norm_g=g_fox, a_log=pgg[1, 4:8], dt_bias=pgg[0, 4:8], b_f=pgg[0, 8:16], loss=pg2[3, 0:1])
    return grad_x, g_cat, g_conv8[:CONV_W], dict(zip(LATE, g_late)), small_grads


ANY = pl.BlockSpec(memory_space=pl.ANY)
CONV_PKT_ROWS = 16


def _mesh_pos():
    return lax.axis_index("x"), lax.axis_index("y"), lax.axis_index("c")


def _other_chips(x, y):
    return [(1 - x, y), (x, 1 - y), (1 - x, 1 - y)]


def _rcopy(src, dst, send_sem, recv_sem, dev):
    return pltpu.make_async_remote_copy(src_ref=src, dst_ref=dst, send_sem=send_sem, recv_sem=recv_sem,
                                        device_id=dev, device_id_type=MESH)


class _Comm:
    def __init__(self, ins, outs, aliases, n_sems, start, finish):
        self.ins, self.outs, self.aliases, self.n_sems, self.start, self.finish = list(ins), list(outs), dict(aliases), n_sems, start, finish


def _hosted(body, comm, *, name, grid, in_specs, out_specs, out_shape, args, scratch_shapes=(), aliases=None):
    n_in, n_out, n_sc = len(in_specs), len(out_specs), len(scratch_shapes)
    k, ko = (len(comm.ins), len(comm.outs)) if comm else (0, 0)

    def kernel_body(*refs):
        o0 = n_in + k
        s0 = o0 + n_out + ko
        if comm:
            cins, couts, (ssem, rsem) = refs[n_in:o0], refs[o0 + n_out:s0], refs[s0 + n_sc:]
            step = pl.program_id(0)
            for d in range(1, len(grid)):
                step = step * grid[d] + pl.program_id(d)

            @pl.when(step == 0)
            def _():
                comm.start(cins, couts, ssem, rsem)

        body(*refs[:n_in], *refs[o0:o0 + n_out], *refs[s0:s0 + n_sc])
        if comm:
            last = 1
            for n in grid:
                last *= n

            @pl.when(step == last - 1)
            def _():
                comm.finish(cins, couts, ssem, rsem)

    io_aliases = dict(aliases or {})
    scratch = list(scratch_shapes)
    if comm:
        io_aliases.update({n_in + i: n_out + j for i, j in comm.aliases.items()})
        scratch += [pltpu.SemaphoreType.DMA((comm.n_sems,)), pltpu.SemaphoreType.DMA((comm.n_sems,))]
    res = pl.pallas_call(
        kernel_body, name=name, grid=grid, in_specs=list(in_specs) + [ANY] * k, out_specs=tuple(out_specs) + (ANY,) * ko,
        out_shape=tuple(out_shape) + tuple(comm.outs if comm else ()), scratch_shapes=scratch, input_output_aliases=io_aliases,
        compiler_params=_params(("arbitrary",) * len(grid)),
    )(*args, *(comm.ins if comm else ()))
    return tuple(res[:n_out]), tuple(res[n_out:])


def _comm_only(phases, name):
    n_in = sum(len(p.ins) for p in phases)

    def body(*refs):
        n_out = sum(len(p.outs) for p in phases)
        sems = refs[n_in + n_out:]
        i0, o0 = 0, n_in
        for j, p in enumerate(phases):
            cins, couts = refs[i0:i0 + len(p.ins)], refs[o0:o0 + len(p.outs)]
            p.start(cins, couts, sems[2 * j], sems[2 * j + 1])
            p.finish(cins, couts, sems[2 * j], sems[2 * j + 1])
            i0 += len(p.ins)
            o0 += len(p.outs)

    aliases, i0, o0 = {}, 0, 0
    for p in phases:
        aliases.update({i0 + i: o0 + j for i, j in p.aliases.items()})
        i0 += len(p.ins)
        o0 += len(p.outs)
    outs = [o for p in phases for o in p.outs]
    res = pl.pallas_call(
        body, name=name, out_shape=tuple(outs), in_specs=[ANY] * n_in, out_specs=(ANY,) * len(outs), input_output_aliases=aliases,
        scratch_shapes=[pltpu.SemaphoreType.DMA((p.n_sems,)) for p in phases for _ in range(2)],
    )(*[a for p in phases for a in p.ins])
    split, o0 = [], 0
    for p in phases:
        split.append(tuple(res[o0:o0 + len(p.outs)]))
        o0 += len(p.outs)
    return split


def _like(arrays):
    return [jax.ShapeDtypeStruct(a.shape, a.dtype) for a in arrays]


def _half(ref, slot, hf, piece=(0, 1)):
    k, n = piece
    rows = ref.shape[1] // 2 // n
    return ref.at[slot, pl.ds((hf * n + k) * rows, rows)]


def _whole_halves(arrays):
    return [(i, 0, 1) for i in range(len(arrays))]


def _gather_chips(bufs, pieces=None, whole=False, base=0):
    nw = len(bufs)
    pieces = _whole_halves(bufs) if pieces is None else pieces
    part = (lambda ref, slot, c, piece: ref.at[slot]) if whole else _half

    def copies(couts):
        x, y, c = _mesh_pos()
        q = 2 * x + y
        for j, (i, k, n) in enumerate(pieces):
            for kc, chip in enumerate(_other_chips(x, y)):
                mine, theirs = part(couts[i], q, c, (k, n)), part(couts[i], 2 * chip[0] + chip[1], c, (k, n))
                yield base + j * 3 + kc, mine, theirs, (*chip, c)

    def start(cins, couts, ssem, rsem):
        for s, mine, _, dev in copies(couts):
            _rcopy(mine, mine, ssem.at[s], rsem.at[s], dev).start()

    def finish(cins, couts, ssem, rsem):
        for s, _, theirs, dev in copies(couts):
            _rcopy(theirs, theirs, ssem.at[s], rsem.at[s], dev).wait_recv()
        for s, mine, _, dev in copies(couts):
            _rcopy(mine, mine, ssem.at[s], rsem.at[s], dev).wait_send()

    return _Comm(bufs, _like(bufs), {i: i for i in range(nw)}, 3 * len(pieces), start, finish)


def _gather_pass_on(bufs, pieces=None, base=0):
    nw = len(bufs)
    pieces = _whole_halves(bufs) if pieces is None else pieces

    def copies(couts):
        x, y, c = _mesh_pos()
        for j, (i, k, n) in enumerate(pieces):
            for kc, chip in enumerate(_other_chips(x, y)):
                slot = 2 * chip[0] + chip[1]
                yield base + j * 3 + kc, _half(couts[i], slot, c, (k, n)), _half(couts[i], slot, 1 - c, (k, n)), (x, y, 1 - c)

    def start(cins, couts, ssem, rsem):
        for s, landed, _, sib in copies(couts):
            _rcopy(landed, landed, ssem.at[s], rsem.at[s], sib).start()

    def finish(cins, couts, ssem, rsem):
        for s, _, passed, sib in copies(couts):
            _rcopy(passed, passed, ssem.at[s], rsem.at[s], sib).wait_recv()
        for s, landed, _, sib in copies(couts):
            _rcopy(landed, landed, ssem.at[s], rsem.at[s], sib).wait_send()

    return _Comm(bufs, _like(bufs), {i: i for i in range(nw)}, 3 * len(pieces), start, finish)


def _gather_now(bufs, packets):
    nb = len(bufs)
    over, on, pk = _gather_chips(bufs), _gather_pass_on(bufs, base=3 * nb), _gather_chips(packets, whole=True, base=6 * nb)

    def start(cins, couts, ssem, rsem):
        over.start(cins[:nb], couts[:nb], ssem, rsem)
        pk.start(cins[nb:], couts[nb:], ssem, rsem)

    def finish(cins, couts, ssem, rsem):
        over.finish(cins[:nb], couts[:nb], ssem, rsem)
        on.start(cins[:nb], couts[:nb], ssem, rsem)
        on.finish(cins[:nb], couts[:nb], ssem, rsem)
        pk.finish(cins[nb:], couts[nb:], ssem, rsem)

    every = list(bufs) + list(packets)
    return _Comm(every, _like(every), {i: i for i in range(len(every))}, 6 * nb + 3 * len(packets), start, finish)


def _exchange_pairs(gs):
    nw = len(gs)

    def copies(cins, couts):
        x, y, c = _mesh_pos()
        for i in range(nw):
            for d in range(N_CHIPS):
                yield i * N_CHIPS + d, _half(cins[i], d, 1 - c), couts[i].at[d], (x, y, 1 - c)

    def start(cins, couts, ssem, rsem):
        for s, src, dst, sib in copies(cins, couts):
            _rcopy(src, dst, ssem.at[s], rsem.at[s], sib).start()

    def finish(cins, couts, ssem, rsem):
        for s, src, dst, sib in copies(cins, couts):
            _rcopy(src, dst, ssem.at[s], rsem.at[s], sib).wait_recv()
        for s, src, dst, sib in copies(cins, couts):
            _rcopy(src, dst, ssem.at[s], rsem.at[s], sib).wait_send()

    outs = [jax.ShapeDtypeStruct((N_CHIPS, g.shape[1] // 2, g.shape[2]), g.dtype) for g in gs]
    return _Comm(gs, outs, {}, N_CHIPS * nw, start, finish)


def _gather_packets(small):
    def peers():
        x, y, c = _mesh_pos()
        for r in range(1, 8):
            fx, fy, fc = (r >> 2) & 1, (r >> 1) & 1, r & 1
            yield r - 1, (1 - x if fx else x, 1 - y if fy else y, 1 - c if fc else c)

    def start(cins, couts, ssem, rsem):
        x, y, c = _mesh_pos()
        mine = couts[0].at[4 * x + 2 * y + c]
        for s, peer in peers():
            _rcopy(mine, mine, ssem.at[s], rsem.at[s], peer).start()

    def finish(cins, couts, ssem, rsem):
        x, y, c = _mesh_pos()
        mine = couts[0].at[4 * x + 2 * y + c]
        for s, peer in peers():
            theirs = couts[0].at[4 * peer[0] + 2 * peer[1] + peer[2]]
            _rcopy(theirs, theirs, ssem.at[s], rsem.at[s], peer).wait_recv()
        for s, peer in peers():
            _rcopy(mine, mine, ssem.at[s], rsem.at[s], peer).wait_send()

    return _Comm([small], _like([small]), {0: 0}, 7, start, finish)


def _exchange_chips(a4s, b2s, pieces=None):
    nw = len(a4s)
    pieces = _whole_halves(a4s) if pieces is None else pieces

    def copies(cins, couts):
        x, y, c = _mesh_pos()
        for j, (i, k, n) in enumerate(pieces):
            rows = a4s[i].shape[1] // n
            part = pl.ds(k * rows, rows)
            for kc, chip in enumerate(_other_chips(x, y)):
                yield j * 3 + kc, cins[i].at[2 * chip[0] + chip[1], part], couts[i].at[kc, part], (*chip, c)

    def start(cins, couts, ssem, rsem):
        for s, src, dst, dev in copies(cins, couts):
            _rcopy(src, dst, ssem.at[s], rsem.at[s], dev).start()

    def finish(cins, couts, ssem, rsem):
        for s, src, dst, dev in copies(cins, couts):
            _rcopy(src, dst, ssem.at[s], rsem.at[s], dev).wait_recv()
        for s, src, dst, dev in copies(cins, couts):
            _rcopy(src, dst, ssem.at[s], rsem.at[s], dev).wait_send()

    return _Comm(list(a4s) + list(b2s), _like(b2s), {nw + i: i for i in range(nw)}, 3 * len(pieces), start, finish)


def _landing(a4s):
    return [lax.empty((3,) + a.shape[1:], a.dtype) for a in a4s]


def _share_halves(rs):
    nw = len(rs)

    def halves(couts, i, hf):
        rows = rs[i].shape[0] // 2
        return couts[i].at[pl.ds(hf * rows, rows)]

    def start(cins, couts, ssem, rsem):
        x, y, c = _mesh_pos()
        for i in range(nw):
            _rcopy(halves(couts, i, c), halves(couts, i, c), ssem.at[i], rsem.at[i], (x, y, 1 - c)).start()

    def finish(cins, couts, ssem, rsem):
        x, y, c = _mesh_pos()
        for i in range(nw):
            _rcopy(halves(couts, i, 1 - c), halves(couts, i, 1 - c), ssem.at[i], rsem.at[i], (x, y, 1 - c)).wait_recv()
        for i in range(nw):
            _rcopy(halves(couts, i, c), halves(couts, i, c), ssem.at[i], rsem.at[i], (x, y, 1 - c)).wait_send()

    return _Comm(rs, _like(rs), {i: i for i in range(nw)}, nw, start, finish)


HBM_SPEC = pl.BlockSpec(memory_space=pltpu.HBM)
SEM_SPEC = pl.BlockSpec(memory_space=pltpu.SEMAPHORE)
DATAFLOW = pltpu.SideEffectType.DATAFLOW_SIDE_EFFECTING


SPLIT_COPIES = dict(chips=3, pairs=N_CHIPS)


def _split_copies(kind, src_ref, dst_ref):
    x, y, c = _mesh_pos()
    if kind == "chips":
        return [(src_ref.at[2 * chip[0] + chip[1]], dst_ref.at[k], (*chip, c)) for k, chip in enumerate(_other_chips(x, y))]
    return [(_half(src_ref, d, 1 - c), dst_ref.at[d], (x, y, 1 - c)) for d in range(N_CHIPS)]


def _chips_start(sent, landing, kind="chips"):
    n = SPLIT_COPIES[kind]

    def body(sent_ref, land_ref, *rest):
        sems, token = rest[:2 * n], rest[2 * n + 2]
        for k, (src, dst, dev) in enumerate(_split_copies(kind, sent_ref, land_ref)):
            _rcopy(src, dst, sems[k], sems[n + k], dev).start()
        token[...] = jnp.zeros_like(token)

    return pl.pallas_call(
        body, name=kind + "_start_w_in",
        out_shape=(pltpu.SemaphoreType.DMA(()),) * (2 * n) + (pltpu.HBM(sent.shape, sent.dtype), pltpu.HBM(landing.shape, landing.dtype),
                                                             jax.ShapeDtypeStruct((8, LANES), f32)),
        in_specs=(HBM_SPEC, HBM_SPEC), out_specs=(SEM_SPEC,) * (2 * n) + (HBM_SPEC, HBM_SPEC, pl.BlockSpec(memory_space=pltpu.VMEM)),
        input_output_aliases={0: 2 * n, 1: 2 * n + 1}, compiler_params=pltpu.CompilerParams(has_side_effects=DATAFLOW),
    )(pltpu.with_memory_space_constraint(sent, pltpu.HBM), pltpu.with_memory_space_constraint(landing, pltpu.HBM))


def _chips_wait(sems, sent_thru, land_thru, after, kind="chips"):
    n = SPLIT_COPIES[kind]

    def body(sent_ref, land_ref, *rest):
        sems = rest[:2 * n]
        for k, (src, dst, dev) in enumerate(_split_copies(kind, sent_ref, land_ref)):
            cp = _rcopy(src, dst, sems[k], sems[n + k], dev)
            cp.wait_send()
            cp.wait_recv()

    return pl.pallas_call(
        body, name=kind + "_wait_w_in", out_shape=(pltpu.HBM(sent_thru.shape, sent_thru.dtype), pltpu.HBM(land_thru.shape, land_thru.dtype)),
        in_specs=(HBM_SPEC, HBM_SPEC) + (SEM_SPEC,) * (2 * n) + (ANY,) * len(after), out_specs=(HBM_SPEC, HBM_SPEC),
        input_output_aliases={0: 0, 1: 1}, compiler_params=pltpu.CompilerParams(has_side_effects=DATAFLOW),
    )(sent_thru, land_thru, *sems, *after)[1]


ADD_ROWS = 256


def _add_pair(g4, b1, qc_idx, name):
    _, half, cols = b1.shape
    rb = ADD_ROWS if half % ADD_ROWS == 0 else half
    nb = half // rb

    def body(qc_ref, g_ref, b_ref, ob_ref):
        ob_ref[...] = (g_ref[...] + b_ref[...]).astype(bf16)

    blk = (1, rb, cols)
    out = pl.BlockSpec(blk, lambda d, i, qc: (d, i, 0))
    return pl.pallas_call(
        body, name=name,
        grid_spec=pltpu.PrefetchScalarGridSpec(
            num_scalar_prefetch=1, grid=(N_CHIPS, nb),
            in_specs=[pl.BlockSpec(blk, lambda d, i, qc: (d, qc[1] * nb + i, 0)), out],
            out_specs=out),
        out_shape=jax.ShapeDtypeStruct(b1.shape, bf16),
        compiler_params=_params(("parallel", "parallel")),
    )(qc_idx, g4, b1)


def _add_chips(g4, b1, b2, qc_idx, name):
    _, half, cols = b1.shape
    rb = ADD_ROWS if half % ADD_ROWS == 0 else half
    nb = half // rb

    def body(qc_ref, g_ref, s_ref, b_ref, o_ref):
        o_ref[...] = (((g_ref[0] + s_ref[0]) + b_ref[0].astype(f32)) + b_ref[1].astype(f32)) + b_ref[2].astype(f32)

    return pl.pallas_call(
        body, name=name,
        grid_spec=pltpu.PrefetchScalarGridSpec(
            num_scalar_prefetch=1, grid=(nb,),
            in_specs=[pl.BlockSpec((1, rb, cols), lambda i, qc: (qc[0], qc[1] * nb + i, 0)),
                      pl.BlockSpec((1, rb, cols), lambda i, qc: (qc[0], i, 0)), pl.BlockSpec((3, rb, cols), lambda i, qc: (0, i, 0))],
            out_specs=pl.BlockSpec((rb, cols), lambda i, qc: (qc[1] * nb + i, 0))),
        out_shape=jax.ShapeDtypeStruct((2 * half, cols), f32),
        compiler_params=_params(("parallel",)),
    )(qc_idx, g4, b1, b2)


def _adamw_math(w, g, m, v):
    m = ADAM_B1 * m + (1.0 - ADAM_B1) * g
    v = ADAM_B2 * v + (1.0 - ADAM_B2) * (g * g)
    m_hat = m / (1.0 - ADAM_B1 ** ADAM_STEP)
    v_hat = v / (1.0 - ADAM_B2 ** ADAM_STEP)
    return -ADAM_LR * (m_hat / (jnp.sqrt(v_hat) + ADAM_EPS) + ADAM_WD * w), m, v


def _adamw(w, g, m, v, name, comm=None, token=None):
    rows = w.shape[0]
    if w.ndim == 3:
        rb = max(r for r in range(1, ADD_ROWS // 4 + 1) if rows % r == 0)
    else:
        rb = ADD_ROWS if rows % ADD_ROWS == 0 else rows
    extra = [] if token is None else [token]

    def body(w_ref, g_ref, m_ref, v_ref, *rest):
        go_ref, d_ref, mo_ref, vo_ref = rest[len(extra):]
        g = g_ref[...]
        go_ref[...] = g
        d_ref[...], mo_ref[...], vo_ref[...] = _adamw_math(w_ref[...], g, m_ref[...], v_ref[...])

    blk = pl.BlockSpec((rb,) + w.shape[1:], lambda i: (i,) + (0,) * (w.ndim - 1))
    return _hosted(body, comm, name=name, grid=(rows // rb,), in_specs=[blk] * 4 + [pl.BlockSpec((8, LANES), lambda i: (0, 0))] * len(extra),
                   out_specs=(blk,) * 4, out_shape=(jax.ShapeDtypeStruct(w.shape, f32),) * 4, args=(w, g, m, v, *extra))


def _small_sum_adamw(all_pkts, w, m, v):
    names = [n for n, _, _ in SMALL_LAYOUT if n in w]
    place = {n: (r0, size) for n, r0, size in SMALL_LAYOUT}
    rows_of = lambda size: -(-size // LANES)
    flat = lambda a: a.reshape(1, -1)
    k = len(names)

    def body(*refs):
        a_ref, ins = refs[0], refs[1:1 + 3 * k]
        g_ref, outs = refs[1 + 3 * k], refs[2 + 3 * k:2 + 7 * k]
        packs = refs[2 + 7 * k:]
        g = a_ref[0]
        for r in range(1, 8):
            g = g + a_ref[r]
        g_ref[...] = g
        for kind in range(3):
            packs[kind][...] = jnp.zeros_like(packs[kind])
            for j, n in enumerate(names):
                r0, size = place[n]
                for r in range(rows_of(size)):
                    width = min(LANES, size - r * LANES)
                    packs[kind][r0 + r:r0 + r + 1, 0:width] = ins[kind * k + j][:, r * LANES:r * LANES + width]
        results = (g,) + _adamw_math(packs[0][...], g, packs[1][...], packs[2][...])
        for kind, val in enumerate(results):
            for j, n in enumerate(names):
                r0, size = place[n]
                for r in range(rows_of(size)):
                    width = min(LANES, size - r * LANES)
                    outs[kind * k + j][:, r * LANES:r * LANES + width] = val[r0 + r:r0 + r + 1, 0:width]

    args = [all_pkts] + [flat(d[n]) for d in (w, m, v) for n in names]
    out_shape = [jax.ShapeDtypeStruct(all_pkts.shape[1:], f32)] + [jax.ShapeDtypeStruct((1, place[n][1]), f32) for _ in range(4) for n in names]
    res = pl.pallas_call(body, name="small_sum_adamw", out_shape=tuple(out_shape),
                         scratch_shapes=[pltpu.VMEM(all_pkts.shape[1:], f32)] * 3)(*args)
    by_kind = [{n: res[1 + kind * k + j].reshape(w[n].shape) for j, n in enumerate(names)} for kind in range(4)]
    return res[0], by_kind


SMALL_LAYOUT = (("ln_in_g", 0, 1024), ("ln_in_b", 8, 1024), ("ln1_g", 16, 1024), ("ln1_b", 24, 1024), ("b_ple_gate", 32, 1024),
                ("ln2_g", 40, 1024), ("ln2_b", 48, 1024), ("gdn_norm_g", 56, 128), ("fox_norm_g", 57, 64), ("a_log", 58, 4),
                ("dt_bias", 59, 4), ("b_f", 60, 8), ("loss", 61, 1))
SMALL_CONV_ROW = 64
SMALL_ROWS = 128


def _pack_small(vals, conv=None):
    rows = []
    nxt = 0
    for n, r0, size in SMALL_LAYOUT:
        assert r0 == nxt
        v = vals[n].reshape(-1).astype(f32) if n in vals else jnp.zeros((size,), f32)
        nrows = -(-size // LANES)
        rows.append(jnp.pad(v, (0, nrows * LANES - size)).reshape(nrows, LANES))
        nxt = r0 + nrows
    rows.append(jnp.zeros((SMALL_CONV_ROW - nxt, LANES), f32))
    conv_rows = CONV_W * GDN_QKV // LANES
    rows.append(jnp.zeros((conv_rows, LANES), f32) if conv is None else conv.reshape(conv_rows, LANES))
    rows.append(jnp.zeros((SMALL_ROWS - SMALL_CONV_ROW - conv_rows, LANES), f32))
    return jnp.concatenate(rows, axis=0)


WEIGHTS = ("ln_in_g", "ln_in_b", "w_in", "conv_w", "a_log", "dt_bias", "gdn_norm_g", "b_f", "fox_norm_g", "w_out", "ln1_g", "ln1_b",
           "w_up", "w_down", "w_ple", "w_ple_gate", "b_ple_gate", "ln2_g", "ln2_b")
SMALL_NAMES = tuple(n for n, _, _ in SMALL_LAYOUT if n != "loss")


def kernel(x, p, ln_in_g, ln_in_b, w_in, conv_w, a_log, dt_bias, gdn_norm_g, b_f, fox_norm_g, w_out, ln1_g, ln1_b, w_up, w_down, w_ple, w_ple_gate, b_ple_gate, ln2_g, ln2_b, loss_target, m_ln_in_g, m_ln_in_b, m_w_in, m_conv_w, m_a_log, m_dt_bias, m_gdn_norm_g, m_b_f, m_fox_norm_g, m_w_out, m_ln1_g, m_ln1_b, m_w_up, m_w_down, m_w_ple, m_w_ple_gate, m_b_ple_gate, m_ln2_g, m_ln2_b, v_ln_in_g, v_ln_in_b, v_w_in, v_conv_w, v_a_log, v_dt_bias, v_gdn_norm_g, v_b_f, v_fox_norm_g, v_w_out, v_ln1_g, v_ln1_b, v_w_up, v_w_down, v_w_ple, v_w_ple_gate, v_b_ple_gate, v_ln2_g, v_ln2_b):
    given = dict(locals())
    w = {n: given[n] for n in WEIGHTS}
    m = {n: given["m_" + n] for n in WEIGHTS}
    v = {n: given["v_" + n] for n in WEIGHTS}
    xi, yi, ci = _mesh_pos()
    q = 2 * xi + yi

    def slot_buffer(val, dtype, slots=N_CHIPS, slot=q, rows=None):
        rows = val.shape[0] if rows is None else rows
        return lax.dynamic_update_slice(lax.empty((slots, rows) + val.shape[1:], dtype), val.astype(dtype)[None], (slot, 0, 0))

    shard_cols = D_IN // N_CHIPS
    conv_rows = CONV_W * GDN_QKV // N_CHIPS // LANES
    conv_pkt = jnp.pad(w["conv_w"][0].reshape(-1, LANES), ((0, CONV_PKT_ROWS - conv_rows), (0, 0)))
    ln_in_out, (w_in4, conv_all) = _ln_in(x[0], _row(w["ln_in_g"]), _row(w["ln_in_b"]),
                                          _gather_now([slot_buffer(w["w_in"][0].T, bf16, rows=W_IN_ROWS)], [slot_buffer(conv_pkt, f32)]))
    conv_full = jnp.concatenate([conv_all[d, :conv_rows].reshape(CONV_W, GDN_QKV // N_CHIPS) for d in range(N_CHIPS)], axis=1)
    wi = jnp.concatenate([w_in4[d, :shard_cols] for d in range(N_CHIPS)], axis=0)
    w_cat = jnp.concatenate([wi[:OFF_BETA], wi[OFF_FOX:OFF_F], wi[OFF_BETA:OFF_FOX], wi[OFF_F:],
                             jnp.zeros((D_CAT - D_IN, D_MODEL), bf16)], axis=0)

    small = {n: w[n] for n in SMALL_NAMES}
    qc = jnp.stack([q, ci]).astype(jnp.int32)
    tail_state = {}

    def chips_phase(gc):
        g_in = jnp.concatenate([gc[:OFF_BETA], gc[SEG_SMALL:SEG_SMALL + 8], gc[SEG_FOX:SEG_SMALL], gc[SEG_SMALL + 8:SEG_SMALL + 16]], axis=0)
        g_in4 = jnp.stack([jnp.pad(g_in[d * shard_cols:(d + 1) * shard_cols], ((0, W_IN_ROWS - shard_cols), (0, 0))) for d in range(N_CHIPS)])
        *sems, g_thru, b1_thru, _ = _chips_start(g_in4, lax.empty((N_CHIPS, W_IN_ROWS // 2, D_MODEL), f32), "pairs")
        tail_state.update(pair_sems=sems, g=g_thru, b1=b1_thru)
        return None, None

    def after_dh0(dh0):
        from_sibling = _chips_wait(tail_state["pair_sems"], tail_state["g"], tail_state["b1"], [dh0], "pairs")
        sent = _add_pair(tail_state["g"], from_sibling, qc, "add_pair_w_in")
        *sems, sent_thru, land_thru, token = _chips_start(sent, _landing([sent])[0])
        tail_state.update(from_sibling=from_sibling, sems=sems, sent=sent_thru, landing=land_thru, token=token)
        return None, None

    grad_x, _, g_conv, g_late, small_g = _device_grads(
        x[0], p[0, 0], loss_target[0], small, w_cat, conv_full, [slot_buffer(w[n][0], bf16) for n in LATE], qc, tail=(chips_phase, after_dh0),
        ln_in_out=ln_in_out)

    grads, delta, new_m, new_v = {}, {}, {}, {}
    for n in LATE:
        outs, _ = _adamw(w[n][0], g_late[n], m[n][0], v[n][0], "adamw_" + n, token=tail_state["token"])
        grads[n], delta[n], new_m[n], new_v[n] = (a.reshape(w[n].shape) for a in outs)
    from_chips = _chips_wait(tail_state["sems"], tail_state["sent"], tail_state["landing"], [delta[n] for n in LATE])
    packets = _gather_packets(slot_buffer(_pack_small(small_g, g_conv), f32, 8, 4 * xi + 2 * yi + ci))
    halves = _share_halves([_add_chips(tail_state["g"], tail_state["from_sibling"], from_chips, qc, "add_chips_w_in")])
    (g_in_red,), (small_all,) = _comm_only([halves, packets], "share_w_in")
    as_stored = lambda a: jnp.transpose(a, (2, 0, 1))
    outs, _ = _adamw(as_stored(w["w_in"]), g_in_red[:shard_cols].reshape(shard_cols, 1, D_MODEL), as_stored(m["w_in"]), as_stored(v["w_in"]),
                     "adamw_w_in")
    grads["w_in"], delta["w_in"], new_m["w_in"], new_v["w_in"] = (jnp.transpose(a, (1, 2, 0)) for a in outs)
    pick = lambda d: {n: d[n] for n in SMALL_NAMES}
    g_pkt, by_kind = _small_sum_adamw(small_all, pick(w), pick(m), pick(v))
    for dst, vals in zip((grads, delta, new_m, new_v), by_kind):
        dst.update(vals)
    conv_rows_all = CONV_W * GDN_QKV // LANES
    conv_g_full = g_pkt[SMALL_CONV_ROW:SMALL_CONV_ROW + conv_rows_all].reshape(CONV_W, GDN_QKV)
    conv_g = lax.dynamic_slice_in_dim(conv_g_full, q * (GDN_QKV // N_CHIPS), GDN_QKV // N_CHIPS, axis=1)
    outs, _ = _adamw(w["conv_w"][0], conv_g, m["conv_w"][0], v["conv_w"][0], "adamw_conv_w")
    grads["conv_w"], delta["conv_w"], new_m["conv_w"], new_v["conv_w"] = (a.reshape(w["conv_w"].shape) for a in outs)
    loss = g_pkt[61, 0]
    return (loss, grad_x[None], *[grads[n] for n in WEIGHTS], *[delta[n] for n in WEIGHTS],
            *[new_m[n] for n in WEIGHTS], *[new_v[n] for n in WEIGHTS])
```
